```python
import math
import jax, jax.numpy as jnp
from jax import lax
import numpy as np

D_MODEL = 1024
BATCH = 8
SEQ = 2048
DEPTH = 1

MEM_LEN = 256
DN_HEADS = 8
DN_DK = 128
DN_DV = 128
DN_CHUNK = 64
CONV_K = 4
SB_HEADS = 8
SB_DH = 128
SB_BLOCK = 128
MEM_HEADS = 4
MEM_DH = 64
N_BRANCH = 3
NORM_EPS = 1e-6

DN_QK = DN_HEADS * DN_DK
DN_VW = DN_HEADS * DN_DV
DN_QKV_W = 2 * DN_QK + DN_VW
SB_W = SB_HEADS * SB_DH
MEM_W = MEM_HEADS * MEM_DH
IN_SIZES = (DN_QKV_W, DN_VW, DN_HEADS, DN_HEADS, 3 * SB_W, SB_W, MEM_W, MEM_W, N_BRANCH * D_MODEL)
IN_WIDTH = sum(IN_SIZES)

kernel_name = "hybrid_deltanet_stickbreak_memory_block"


def rmsnorm(x, g):
    xf = x.astype(jnp.float32)
    y = xf * lax.rsqrt(jnp.mean(xf * xf, axis=-1, keepdims=True) + NORM_EPS)
    return (y * g.astype(jnp.float32)).astype(x.dtype)


def l2norm(x):
    return x * lax.rsqrt(jnp.sum(x * x, axis=-1, keepdims=True) + NORM_EPS)


def to_heads(t, n_heads):
    b, s, _ = t.shape
    return t.reshape(b, s, n_heads, -1).transpose(0, 2, 1, 3)


def merge_heads(t):
    b, h, s, d = t.shape
    return t.transpose(0, 2, 1, 3).reshape(b, s, h * d)


def causal_dwconv(x, w):
    k = w.shape[0]
    t = x.shape[1]
    xp = jnp.pad(x, ((0, 0), (k - 1, 0), (0, 0)))
    return sum(xp[:, j:j + t] * w[j] for j in range(k))


def gated_delta_rule(q, k, v, beta, g):
    b, h, t, dk = q.shape
    dv = v.shape[-1]
    c = DN_CHUNK
    n = t // c
    q = q.reshape(b, h, n, c, dk)
    k = k.reshape(b, h, n, c, dk)
    v = v.reshape(b, h, n, c, dv)
    beta = beta.reshape(b, h, n, c)
    G = jnp.cumsum(g.reshape(b, h, n, c), axis=-1)
    idx = jnp.arange(c)
    incl = idx[:, None] >= idx[None, :]
    strict = idx[:, None] > idx[None, :]
    diff = G[..., :, None] - G[..., None, :]
    gam_incl = jnp.exp(jnp.where(incl, diff, -jnp.inf))
    gam_strict = jnp.where(strict, gam_incl, 0.0)
    kk = jnp.einsum('bhncd,bhnsd->bhncs', k, k)
    m = beta[..., :, None] * kk * gam_strict
    eye = jnp.eye(c, dtype=jnp.float32)
    t_inv = lax.linalg.triangular_solve(eye + m, jnp.broadcast_to(eye, m.shape),
                                        left_side=True, lower=True, unit_diagonal=True)
    u = jnp.einsum('bhncs,bhnsd->bhncd', t_inv, v * beta[..., None])
    w = jnp.einsum('bhncs,bhnsd->bhncd', t_inv, k * (beta * jnp.exp(G))[..., None])
    a_intra = jnp.einsum('bhncd,bhnsd->bhncs', q, k) * gam_incl
    q_dec = q * jnp.exp(G)[..., None]
    last = G[..., -1]
    k_dec = k * jnp.exp(last[..., None] - G)[..., None]

    def step(s, xs):
        q_n, w_n, u_n, k_n, a_n, last_n = xs
        v_new = u_n - jnp.einsum('bhcd,bhde->bhce', w_n, s)
        o = jnp.einsum('bhcd,bhde->bhce', q_n, s) + jnp.einsum('bhcs,bhse->bhce', a_n, v_new)
        s = s * jnp.exp(last_n)[..., None, None] + jnp.einsum('bhcd,bhce->bhde', k_n, v_new)
        return s, o

    xs = (jnp.moveaxis(q_dec, 2, 0), jnp.moveaxis(w, 2, 0), jnp.moveaxis(u, 2, 0),
          jnp.moveaxis(k_dec, 2, 0), jnp.moveaxis(a_intra, 2, 0), jnp.moveaxis(last, 2, 0))
    s0 = jnp.zeros((b, h, dk, dv), jnp.float32)
    _, o = lax.scan(step, s0, xs)
    return jnp.moveaxis(o, 0, 2).reshape(b, h, t, dv)


def stick_breaking_attention(q, k, v):
    _, _, t, d = q.shape
    scale = 1.0 / math.sqrt(d)
    outs = []
    for i in range(t // SB_BLOCK):
        t0 = i * SB_BLOCK
        kl = t0 + SB_BLOCK
        z = jnp.einsum('bhtd,bhsd->bhts', q[:, :, t0:kl], k[:, :, :kl]).astype(jnp.float32) * scale
        t_pos = t0 + jnp.arange(SB_BLOCK)
        s_pos = jnp.arange(kl)
        causal = s_pos[None, :] < t_pos[:, None]
        log_beta = jax.nn.log_sigmoid(z)
        log_fail = jnp.where(causal, jax.nn.log_sigmoid(-z), 0.0)
        surv = lax.cumsum(log_fail, axis=3, reverse=True) - log_fail
        att = jnp.where(causal, jnp.exp(log_beta + surv), 0.0)
        outs.append(jnp.einsum('bhts,bhsd->bhtd', att.astype(v.dtype), v[:, :, :kl]))
    return jnp.concatenate(outs, axis=2)


def memory_attention(q, mk, mv):
    s = jnp.einsum('bhtd,bhmd->bhtm', q, mk).astype(jnp.float32) * (1.0 / math.sqrt(q.shape[-1]))
    p = jax.nn.softmax(s, axis=-1)
    return jnp.einsum('bhtm,bhmd->bhtd', p.astype(mv.dtype), mv)


def hybrid_layer(x, mem, norm_g, mem_norm_g, w_in, conv_w, a_log, dt_bias, dn_norm_g,
                 w_mem_kv, w_br_dn, w_br_sb, w_br_mem, w_out):
    h = rmsnorm(x, norm_g)
    proj = h @ w_in
    splits = [int(s) for s in np.cumsum(IN_SIZES)[:-1]]
    dn_qkv, dn_z, dn_b, dn_a, sb_qkv, sb_z, m_q, m_z, gates = jnp.split(proj, splits, axis=-1)

    dn_qkv = jax.nn.silu(causal_dwconv(dn_qkv, conv_w))
    dq, dk, dv = jnp.split(dn_qkv, [DN_QK, 2 * DN_QK], axis=-1)
    dq = l2norm(to_heads(dq, DN_HEADS).astype(jnp.float32)) * (DN_DK ** -0.5)
    dk = l2norm(to_heads(dk, DN_HEADS).astype(jnp.float32))
    dv = to_heads(dv, DN_HEADS).astype(jnp.float32)
    beta = jax.nn.sigmoid(dn_b.astype(jnp.float32)).transpose(0, 2, 1)
    g = -(jnp.exp(a_log.astype(jnp.float32))
          * jax.nn.softplus(dn_a.astype(jnp.float32) + dt_bias.astype(jnp.float32))).transpose(0, 2, 1)
    o_dn = gated_delta_rule(dq, dk, dv, beta, g)
    o_dn = merge_heads(rmsnorm(o_dn, dn_norm_g)).astype(x.dtype) * jax.nn.silu(dn_z)

    sq, sk, sv = jnp.split(sb_qkv, 3, axis=-1)
    o_sb = stick_breaking_attention(to_heads(sq, SB_HEADS), to_heads(sk, SB_HEADS), to_heads(sv, SB_HEADS))
    o_sb = merge_heads(o_sb) * jax.nn.silu(sb_z)

    mkv = rmsnorm(mem, mem_norm_g) @ w_mem_kv
    mk, mv = jnp.split(mkv, 2, axis=-1)
    o_m = memory_attention(to_heads(m_q, MEM_HEADS), to_heads(mk, MEM_HEADS), to_heads(mv, MEM_HEADS))
    o_m = merge_heads(o_m) * jax.nn.silu(m_z)

    g_dn, g_sb, g_m = jnp.split(jax.nn.sigmoid(gates), N_BRANCH, axis=-1)
    merged = g_dn * (o_dn @ w_br_dn) + g_sb * (o_sb @ w_br_sb) + g_m * (o_m @ w_br_mem)
    return x + merged @ w_out


def _fwd_setup_inputs(seed: int = 0) -> dict:
    key = jax.random.key(seed)
    ks = jax.random.split(key, 16)
    f = jnp.float32

    def dense(k, shape, fan_in):
        return jax.random.normal(k, shape, f) * (fan_in ** -0.5)

    def gain(k, shape):
        return 1.0 + 0.02 * jax.random.normal(k, shape, f)

    x = jax.random.normal(ks[0], (BATCH, SEQ, D_MODEL), f)
    mem = jax.random.normal(ks[1], (BATCH, MEM_LEN, D_MODEL), f)
    norm_g = gain(ks[2], (DEPTH, D_MODEL))
    mem_norm_g = gain(ks[3], (DEPTH, D_MODEL))
    w_in = dense(ks[4], (DEPTH, D_MODEL, IN_WIDTH), D_MODEL)
    conv_w = dense(ks[5], (DEPTH, CONV_K, DN_QKV_W), CONV_K)
    a_log = jnp.log(jax.random.uniform(ks[6], (DEPTH, DN_HEADS), f, 1.0, 16.0))
    dt = jnp.exp(jax.random.uniform(ks[7], (DEPTH, DN_HEADS), f, math.log(1e-3), math.log(1e-1)))
    dt_bias = dt + jnp.log(-jnp.expm1(-dt))
    dn_norm_g = gain(ks[8], (DEPTH, DN_DV))
    w_mem_kv = dense(ks[9], (DEPTH, D_MODEL, 2 * MEM_W), D_MODEL)
    w_br_dn = dense(ks[10], (DEPTH, DN_VW, D_MODEL), DN_VW)
    w_br_sb = dense(ks[11], (DEPTH, SB_W, D_MODEL), SB_W)
    w_br_mem = dense(ks[12], (DEPTH, MEM_W, D_MODEL), MEM_W)
    w_out = dense(ks[13], (DEPTH, D_MODEL, D_MODEL), D_MODEL)
    final_g = gain(ks[14], (D_MODEL,))
    return {"x": x, "mem": mem, "norm_g": norm_g, "mem_norm_g": mem_norm_g, "w_in": w_in,
            "conv_w": conv_w, "a_log": a_log, "dt_bias": dt_bias, "dn_norm_g": dn_norm_g,
            "w_mem_kv": w_mem_kv, "w_br_dn": w_br_dn, "w_br_sb": w_br_sb, "w_br_mem": w_br_mem,
            "w_out": w_out, "final_g": final_g}


def _fwd_reference(x, mem, norm_g, mem_norm_g, w_in, conv_w, a_log, dt_bias, dn_norm_g,
              w_mem_kv, w_br_dn, w_br_sb, w_br_mem, w_out, final_g):
    for l in range(DEPTH):
        x = hybrid_layer(x, mem, norm_g[l], mem_norm_g[l], w_in[l], conv_w[l], a_log[l], dt_bias[l],
                         dn_norm_g[l], w_mem_kv[l], w_br_dn[l], w_br_sb[l], w_br_mem[l], w_out[l])
    return rmsnorm(x, final_g)


import jax as _jax
import jax.numpy as _jnp

TWIN_FORMAT = 'train_step'
FWD_PARAMS = ['x', 'mem', 'norm_g', 'mem_norm_g', 'w_in', 'conv_w', 'a_log', 'dt_bias', 'dn_norm_g', 'w_mem_kv', 'w_br_dn', 'w_br_sb', 'w_br_mem', 'w_out', 'final_g']
TWIN_WEIGHTS = ['norm_g', 'mem_norm_g', 'w_in', 'conv_w', 'a_log', 'dt_bias', 'dn_norm_g', 'w_mem_kv', 'w_br_dn', 'w_br_sb', 'w_br_mem', 'w_out', 'final_g']
TWIN_DIFF_INPUT = 'x'
TWIN_INPUTS = ['x', 'mem', 'norm_g', 'mem_norm_g', 'w_in', 'conv_w', 'a_log', 'dt_bias', 'dn_norm_g', 'w_mem_kv', 'w_br_dn', 'w_br_sb', 'w_br_mem', 'w_out', 'final_g', 'loss_target', 'm_norm_g', 'm_mem_norm_g', 'm_w_in', 'm_conv_w', 'm_a_log', 'm_dt_bias', 'm_dn_norm_g', 'm_w_mem_kv', 'm_w_br_dn', 'm_w_br_sb', 'm_w_br_mem', 'm_w_out', 'm_final_g', 'v_norm_g', 'v_mem_norm_g', 'v_w_in', 'v_conv_w', 'v_a_log', 'v_dt_bias', 'v_dn_norm_g', 'v_w_mem_kv', 'v_w_br_dn', 'v_w_br_sb', 'v_w_br_mem', 'v_w_out', 'v_final_g']
TWIN_OUTPUTS = ['loss', 'grad_x', 'grad_norm_g', 'grad_mem_norm_g', 'grad_w_in', 'grad_conv_w', 'grad_a_log', 'grad_dt_bias', 'grad_dn_norm_g', 'grad_w_mem_kv', 'grad_w_br_dn', 'grad_w_br_sb', 'grad_w_br_mem', 'grad_w_out', 'grad_final_g', 'delta_norm_g', 'delta_mem_norm_g', 'delta_w_in', 'delta_conv_w', 'delta_a_log', 'delta_dt_bias', 'delta_dn_norm_g', 'delta_w_mem_kv', 'delta_w_br_dn', 'delta_w_br_sb', 'delta_w_br_mem', 'delta_w_out', 'delta_final_g', 'new_m_norm_g', 'new_m_mem_norm_g', 'new_m_w_in', 'new_m_conv_w', 'new_m_a_log', 'new_m_dt_bias', 'new_m_dn_norm_g', 'new_m_w_mem_kv', 'new_m_w_br_dn', 'new_m_w_br_sb', 'new_m_w_br_mem', 'new_m_w_out', 'new_m_final_g', 'new_v_norm_g', 'new_v_mem_norm_g', 'new_v_w_in', 'new_v_conv_w', 'new_v_a_log', 'new_v_dt_bias', 'new_v_dn_norm_g', 'new_v_w_mem_kv', 'new_v_w_br_dn', 'new_v_w_br_sb', 'new_v_w_br_mem', 'new_v_w_out', 'new_v_final_g']
TWIN_LEAF_KINDS = {'loss': 'loss', 'grad_x': 'grad_x', 'grad_norm_g': 'grad_w', 'grad_mem_norm_g': 'grad_w', 'grad_w_in': 'grad_w', 'grad_conv_w': 'grad_w', 'grad_a_log': 'grad_w', 'grad_dt_bias': 'grad_w', 'grad_dn_norm_g': 'grad_w', 'grad_w_mem_kv': 'grad_w', 'grad_w_br_dn': 'grad_w', 'grad_w_br_sb': 'grad_w', 'grad_w_br_mem': 'grad_w', 'grad_w_out': 'grad_w', 'grad_final_g': 'grad_w', 'delta_norm_g': 'delta_w', 'delta_mem_norm_g': 'delta_w', 'delta_w_in': 'delta_w', 'delta_conv_w': 'delta_w', 'delta_a_log': 'delta_w', 'delta_dt_bias': 'delta_w', 'delta_dn_norm_g': 'delta_w', 'delta_w_mem_kv': 'delta_w', 'delta_w_br_dn': 'delta_w', 'delta_w_br_sb': 'delta_w', 'delta_w_br_mem': 'delta_w', 'delta_w_out': 'delta_w', 'delta_final_g': 'delta_w', 'new_m_norm_g': 'new_m', 'new_m_mem_norm_g': 'new_m', 'new_m_w_in': 'new_m', 'new_m_conv_w': 'new_m', 'new_m_a_log': 'new_m', 'new_m_dt_bias': 'new_m', 'new_m_dn_norm_g': 'new_m', 'new_m_w_mem_kv': 'new_m', 'new_m_w_br_dn': 'new_m', 'new_m_w_br_sb': 'new_m', 'new_m_w_br_mem': 'new_m', 'new_m_w_out': 'new_m', 'new_m_final_g': 'new_m', 'new_v_norm_g': 'new_v', 'new_v_mem_norm_g': 'new_v', 'new_v_w_in': 'new_v', 'new_v_conv_w': 'new_v', 'new_v_a_log': 'new_v', 'new_v_dt_bias': 'new_v', 'new_v_dn_norm_g': 'new_v', 'new_v_w_mem_kv': 'new_v', 'new_v_w_br_dn': 'new_v', 'new_v_w_br_sb': 'new_v', 'new_v_w_br_mem': 'new_v', 'new_v_w_out': 'new_v', 'new_v_final_g': 'new_v'}


def _forward(args):
    return _fwd_reference(*[args[k] for k in FWD_PARAMS])


def _output_shape():
    out = _jax.eval_shape(lambda: _forward(_fwd_setup_inputs(0)))
    return out.shape, out.dtype

N_MICROBATCH = 1
ADAM_LR = 0.001
ADAM_B1 = 0.9
ADAM_B2 = 0.999
ADAM_EPS = 1e-08
ADAM_WD = 0.01
ADAM_STEP = 10
PER_EXAMPLE_BATCH_AXIS = {'x': 0, 'mem': 0, 'loss_target': 0}
SHARED_INPUTS = []
_WEIGHT_DTYPES = {'norm_g': _jnp.float32, 'mem_norm_g': _jnp.float32, 'w_in': _jnp.float32, 'conv_w': _jnp.float32, 'a_log': _jnp.float32, 'dt_bias': _jnp.float32, 'dn_norm_g': _jnp.float32, 'w_mem_kv': _jnp.float32, 'w_br_dn': _jnp.float32, 'w_br_sb': _jnp.float32, 'w_br_mem': _jnp.float32, 'w_out': _jnp.float32, 'final_g': _jnp.float32}
MOMENT_SCALE = {'norm_g': 7.419583e-02, 'mem_norm_g': 6.159065e-03, 'w_in': 2.221072e-02, 'conv_w': 2.954516e-02, 'a_log': 1.291108e-01, 'dt_bias': 1.284108e-01, 'dn_norm_g': 1.126050e-01, 'w_mem_kv': 8.128106e-03, 'w_br_dn': 3.762739e-02, 'w_br_sb': 2.434285e-02, 'w_br_mem': 4.066493e-03, 'w_out': 4.491345e-02, 'final_g': 1.599961e+01}


def _to_microbatches(a, axis):
    t = _jnp.moveaxis(a, axis, 0)
    t = t.reshape((N_MICROBATCH, t.shape[0] // N_MICROBATCH) + t.shape[1:])
    return _jnp.moveaxis(t, 1, axis + 1)


def setup_inputs(seed: int = 0) -> dict:
    inp = _fwd_setup_inputs(seed)
    key = _jax.random.fold_in(_jax.random.key(seed), 7919)
    shape, _ = _output_shape()
    out = dict(inp)
    out["loss_target"] = _jax.random.normal(_jax.random.fold_in(key, 0), shape, _jnp.float32)
    for i, name in enumerate(TWIN_WEIGHTS):
        w = inp[name].astype(_jnp.float32)
        if MOMENT_SCALE is None:
            s = _jnp.sqrt(_jnp.mean(_jnp.square(w)) + 1e-30)
        else:
            s = MOMENT_SCALE[name]
        km, kv = _jax.random.split(_jax.random.fold_in(key, i + 1))
        out[name] = w
        out["m_" + name] = s * _jax.random.normal(km, w.shape, _jnp.float32)
        out["v_" + name] = (s * s) * _jax.random.uniform(kv, w.shape, _jnp.float32, 0.5, 1.5)
    if N_MICROBATCH > 1:
        for name, axis in PER_EXAMPLE_BATCH_AXIS.items():
            out[name] = _to_microbatches(out[name], axis)
    return {'x': out['x'], 'mem': out['mem'], 'norm_g': out['norm_g'], 'mem_norm_g': out['mem_norm_g'], 'w_in': out['w_in'], 'conv_w': out['conv_w'], 'a_log': out['a_log'], 'dt_bias': out['dt_bias'], 'dn_norm_g': out['dn_norm_g'], 'w_mem_kv': out['w_mem_kv'], 'w_br_dn': out['w_br_dn'], 'w_br_sb': out['w_br_sb'], 'w_br_mem': out['w_br_mem'], 'w_out': out['w_out'], 'final_g': out['final_g'], 'loss_target': out['loss_target'], 'm_norm_g': out['m_norm_g'], 'm_mem_norm_g': out['m_mem_norm_g'], 'm_w_in': out['m_w_in'], 'm_conv_w': out['m_conv_w'], 'm_a_log': out['m_a_log'], 'm_dt_bias': out['m_dt_bias'], 'm_dn_norm_g': out['m_dn_norm_g'], 'm_w_mem_kv': out['m_w_mem_kv'], 'm_w_br_dn': out['m_w_br_dn'], 'm_w_br_sb': out['m_w_br_sb'], 'm_w_br_mem': out['m_w_br_mem'], 'm_w_out': out['m_w_out'], 'm_final_g': out['m_final_g'], 'v_norm_g': out['v_norm_g'], 'v_mem_norm_g': out['v_mem_norm_g'], 'v_w_in': out['v_w_in'], 'v_conv_w': out['v_conv_w'], 'v_a_log': out['v_a_log'], 'v_dt_bias': out['v_dt_bias'], 'v_dn_norm_g': out['v_dn_norm_g'], 'v_w_mem_kv': out['v_w_mem_kv'], 'v_w_br_dn': out['v_w_br_dn'], 'v_w_br_sb': out['v_w_br_sb'], 'v_w_br_mem': out['v_w_br_mem'], 'v_w_out': out['v_w_out'], 'v_final_g': out['v_final_g']}


def _loss(weights, diff, rest, loss_target):
    with _jax.named_scope("forward"):
        args = {**rest, TWIN_DIFF_INPUT: diff, **{k: w.astype(_WEIGHT_DTYPES[k]) for k, w in weights.items()}}
        y = _forward(args)
    with _jax.named_scope("loss_head"):
        err = _jnp.square(y.astype(_jnp.float32) - loss_target)
        return 0.5 * _jnp.sum(_jnp.mean(err, axis=-1)) if err.ndim else 0.5 * err


def _adamw(w, g, m, v):
    m = ADAM_B1 * m + (1.0 - ADAM_B1) * g
    v = ADAM_B2 * v + (1.0 - ADAM_B2) * _jnp.square(g)
    m_hat = m / (1.0 - ADAM_B1 ** ADAM_STEP)
    v_hat = v / (1.0 - ADAM_B2 ** ADAM_STEP)
    delta = -ADAM_LR * (m_hat / (_jnp.sqrt(v_hat) + ADAM_EPS) + ADAM_WD * w)
    return delta, m, v


def reference(x, mem, norm_g, mem_norm_g, w_in, conv_w, a_log, dt_bias, dn_norm_g, w_mem_kv, w_br_dn, w_br_sb, w_br_mem, w_out, final_g, loss_target, m_norm_g, m_mem_norm_g, m_w_in, m_conv_w, m_a_log, m_dt_bias, m_dn_norm_g, m_w_mem_kv, m_w_br_dn, m_w_br_sb, m_w_br_mem, m_w_out, m_final_g, v_norm_g, v_mem_norm_g, v_w_in, v_conv_w, v_a_log, v_dt_bias, v_dn_norm_g, v_w_mem_kv, v_w_br_dn, v_w_br_sb, v_w_br_mem, v_w_out, v_final_g):
    given = dict(x=x, mem=mem, norm_g=norm_g, mem_norm_g=mem_norm_g, w_in=w_in, conv_w=conv_w, a_log=a_log, dt_bias=dt_bias, dn_norm_g=dn_norm_g, w_mem_kv=w_mem_kv, w_br_dn=w_br_dn, w_br_sb=w_br_sb, w_br_mem=w_br_mem, w_out=w_out, final_g=final_g, loss_target=loss_target, m_norm_g=m_norm_g, m_mem_norm_g=m_mem_norm_g, m_w_in=m_w_in, m_conv_w=m_conv_w, m_a_log=m_a_log, m_dt_bias=m_dt_bias, m_dn_norm_g=m_dn_norm_g, m_w_mem_kv=m_w_mem_kv, m_w_br_dn=m_w_br_dn, m_w_br_sb=m_w_br_sb, m_w_br_mem=m_w_br_mem, m_w_out=m_w_out, m_final_g=m_final_g, v_norm_g=v_norm_g, v_mem_norm_g=v_mem_norm_g, v_w_in=v_w_in, v_conv_w=v_conv_w, v_a_log=v_a_log, v_dt_bias=v_dt_bias, v_dn_norm_g=v_dn_norm_g, v_w_mem_kv=v_w_mem_kv, v_w_br_dn=v_w_br_dn, v_w_br_sb=v_w_br_sb, v_w_br_mem=v_w_br_mem, v_w_out=v_w_out, v_final_g=v_final_g)
    weights = {n: given[n] for n in TWIN_WEIGHTS}
    shared = {n: given[n] for n in SHARED_INPUTS}
    per_example = {n: given[n] for n in ['x', 'mem']}
    grad_fn = _jax.value_and_grad(_loss, argnums=(0, 1))

    def one_microbatch(ex, loss_target):
        ex = dict(ex)
        diff = ex.pop(TWIN_DIFF_INPUT)
        return grad_fn(weights, diff, {**shared, **ex}, loss_target)

    if N_MICROBATCH == 1:
        loss, (grad_w, grad_x) = one_microbatch(per_example, given["loss_target"])
    else:
        def body(carry, xs):
            loss_sum, grad_sum = carry
            l_k, (gw_k, gx_k) = one_microbatch(xs[0], xs[1])
            with _jax.named_scope("update"):
                return (loss_sum + l_k, _jax.tree.map(_jnp.add, grad_sum, gw_k)), gx_k

        init = (_jnp.zeros((), _jnp.float32), _jax.tree.map(_jnp.zeros_like, weights))
        (loss, grad_w), grad_x = _jax.lax.scan(body, init, (per_example, given["loss_target"]))
    with _jax.named_scope("update"):
        delta_w, new_m, new_v = {}, {}, {}
        for n in TWIN_WEIGHTS:
            delta_w[n], new_m[n], new_v[n] = _adamw(weights[n], grad_w[n], given["m_" + n], given["v_" + n])
    return (loss, grad_x, *[grad_w[n] for n in TWIN_WEIGHTS], *[delta_w[n] for n in TWIN_WEIGHTS],
            *[new_m[n] for n in TWIN_WEIGHTS], *[new_v[n] for n in TWIN_WEIGHTS])
```

```python
import functools
import math

import jax
import jax.numpy as jnp
from jax import lax
from jax.experimental import pallas as pl
from jax.experimental.pallas import tpu as pltpu

F32 = jnp.float32
BF16 = jnp.bfloat16
MESH = pl.DeviceIdType.MESH
HIGHEST = lax.Precision.HIGHEST

D_MODEL = 1024
N_HEADS = 8
D_HEAD = 128
DN_CHUNK = 64
SB_BLOCK = 128
MEM_HEADS = 4
MEM_DH = 64
MEM_W = MEM_HEADS * MEM_DH
NORM_EPS = 1e-6
IN_WIDTH = 11792
N_SHARD = 4
N_DEV = 8

C_DNZ = 3072
C_SBQ = 4096
C_SBZ = 7168
C_MQ = 8192
C_MZ = 8448
C_GATES = 8704
C_BA = 11776
W_R = 11904

ADAM_LR = 0.001
ADAM_B1 = 0.9
ADAM_B2 = 0.999
ADAM_EPS = 1e-08
ADAM_WD = 0.01
ADAM_STEP = 10

VMEM_LIMIT = 56 * 1024 * 1024

B_ROWS = 992
B_MEMKV, B_BRDN, B_BRSB, B_BRMEM, B_OUT, B_CONV = 0, 128, 384, 640, 704, 960
S_NORM, S_MEMNORM, S_FINAL, S_DNNORM, S_ALOG, S_DTB, S_LOSS = 0, 8, 16, 24, 25, 26, 27


def _cp(**kw):
    return pltpu.CompilerParams(vmem_limit_bytes=VMEM_LIMIT, **kw)


def _dot(a, b, dims, precision=None):
    cn = {"nn": ((1,), (0,)), "nt": ((1,), (1,)), "tn": ((0,), (0,))}[dims]
    return lax.dot_general(a, b, (cn, ((), ())), precision=precision, preferred_element_type=F32)


def _bdot(a, b, dims):
    return _dot(a.astype(BF16), b.astype(BF16), dims)


def _hdot(a, b, dims):
    return _dot(a, b, dims, precision=HIGHEST)


def _sigmoid(x):
    return 1.0 / (1.0 + jnp.exp(-x))


def _log1p_small(u):
    return jnp.where(u < 1e-2, u * (1.0 - u * (0.5 - u * (1.0 / 3.0))), jnp.log(1.0 + u))


def _log_sigmoid(z):
    return jnp.minimum(z, 0.0) - _log1p_small(jnp.exp(-jnp.abs(z)))


def _pick(dim, cands):
    for c in cands:
        if dim % c == 0:
            return c
    return dim


def _mm(a, b, dims, name, out_dtype=F32):
    ta, tb = dims[0] == "t", dims[1] == "t"
    m, k = (a.shape[1], a.shape[0]) if ta else a.shape
    n = b.shape[0] if tb else b.shape[1]
    tm = _pick(m, (1024, 512, 256))
    tn = _pick(n, (512, 384, 256, 128))
    tk = _pick(k, (1024, 512, 384, 256))
    nk = k // tk

    def body(a_ref, b_ref, o_ref, acc_ref):
        kk = pl.program_id(2)

        @pl.when(kk == 0)
        def _():
            acc_ref[...] = jnp.zeros_like(acc_ref)

        acc_ref[...] += _bdot(a_ref[...], b_ref[...], dims)

        @pl.when(kk == nk - 1)
        def _():
            o_ref[...] = acc_ref[...].astype(out_dtype)

    a_spec = pl.BlockSpec((tk, tm), lambda i, j, q: (q, i)) if ta else pl.BlockSpec((tm, tk), lambda i, j, q: (i, q))
    b_spec = pl.BlockSpec((tn, tk), lambda i, j, q: (j, q)) if tb else pl.BlockSpec((tk, tn), lambda i, j, q: (q, j))
    return pl.pallas_call(
        body, name=name, grid=(m // tm, n // tn, nk),
        in_specs=[a_spec, b_spec], out_specs=pl.BlockSpec((tm, tn), lambda i, j, q: (i, j)),
        out_shape=jax.ShapeDtypeStruct((m, n), out_dtype),
        scratch_shapes=[pltpu.VMEM((tm, tn), F32)],
        compiler_params=_cp(dimension_semantics=("parallel", "parallel", "arbitrary")),
    )(a, b)


def _rmsnorm_fwd(x, g, name):
    t, d = x.shape
    tb = _pick(t, (512, 256))

    def body(x_ref, g_ref, h_ref):
        xv = x_ref[...]
        r = lax.rsqrt(jnp.mean(xv * xv, axis=-1, keepdims=True) + NORM_EPS)
        h_ref[...] = ((xv * r) * g_ref[...]).astype(BF16)

    return pl.pallas_call(
        body, name=name, grid=(t // tb,),
        in_specs=[pl.BlockSpec((tb, d), lambda i: (i, 0)), pl.BlockSpec((1, d), lambda i: (0, 0))],
        out_specs=pl.BlockSpec((tb, d), lambda i: (i, 0)),
        out_shape=jax.ShapeDtypeStruct((t, d), BF16), compiler_params=_cp(),
    )(x, g)


def _rmsnorm_bwd(x, g, dh, resid, name):
    t, d = x.shape
    tb = _pick(t, (256,))

    def body(x_ref, g_ref, dh_ref, r_ref, dx_ref, dg_ref):
        @pl.when(pl.program_id(0) == 0)
        def _():
            dg_ref[...] = jnp.zeros_like(dg_ref)

        xv = x_ref[...]
        r = lax.rsqrt(jnp.mean(xv * xv, axis=-1, keepdims=True) + NORM_EPS)
        xhat = xv * r
        dhv = dh_ref[...]
        dg_ref[...] += jnp.sum(dhv * xhat, axis=0, keepdims=True)
        dxh = dhv * g_ref[...]
        dx_ref[...] = r_ref[...] + r * (dxh - xhat * jnp.mean(dxh * xhat, axis=-1, keepdims=True))

    row = pl.BlockSpec((tb, d), lambda i: (i, 0))
    vec = pl.BlockSpec((1, d), lambda i: (0, 0))
    return pl.pallas_call(
        body, name=name, grid=(t // tb,), in_specs=[row, vec, row, row], out_specs=[row, vec],
        out_shape=[jax.ShapeDtypeStruct((t, d), F32), jax.ShapeDtypeStruct((1, d), F32)], compiler_params=_cp(),
    )(x, g, dh, resid)


def _conv_silu(xv, w, row):
    y = xv * w[3:4, :]
    for s in (1, 2, 3):
        xs = jnp.where(row >= s, pltpu.roll(xv, s, 0), 0.0)
        y = y + xs * w[3 - s:4 - s, :]
    return y, y * _sigmoid(y)


def _dn_prep_fwd(proj, conv_w):
    t = proj.shape[0]

    def body(p_ref, w_ref, o_ref):
        j = pl.program_id(0)
        xv = p_ref[...]
        row = lax.broadcasted_iota(jnp.int32, xv.shape, 0)
        _, a = _conv_silu(xv, w_ref[...], row)
        inv = lax.rsqrt(jnp.sum(a * a, axis=-1, keepdims=True) + NORM_EPS)
        scale = jnp.where(j < N_HEADS, D_HEAD ** -0.5, 1.0)
        normed = jnp.where(j < 2 * N_HEADS, 1.0, 0.0)
        o_ref[...] = a * (normed * (inv * scale) + (1.0 - normed))

    return pl.pallas_call(
        body, name="dn_prep_fwd", grid=(3 * N_HEADS,),
        in_specs=[pl.BlockSpec((t, D_HEAD), lambda j: (0, j)), pl.BlockSpec((4, D_HEAD), lambda j: (0, j))],
        out_specs=pl.BlockSpec((t, D_HEAD), lambda j: (0, j)),
        out_shape=jax.ShapeDtypeStruct((t, 3 * D_MODEL), F32), compiler_params=_cp(),
    )(proj, conv_w)


def _dn_prep_bwd(proj, conv_w, dqkv):
    t = proj.shape[0]

    def body(p_ref, w_ref, d_ref, dp_ref, dw_ref):
        j = pl.program_id(0)
        xv = p_ref[...]
        w = w_ref[...]
        row = lax.broadcasted_iota(jnp.int32, xv.shape, 0)
        y, a = _conv_silu(xv, w, row)
        dn = d_ref[...]
        inv = lax.rsqrt(jnp.sum(a * a, axis=-1, keepdims=True) + NORM_EPS)
        scale = jnp.where(j < N_HEADS, D_HEAD ** -0.5, 1.0)
        ds = dn * scale
        da_norm = inv * ds - a * (inv * inv * inv) * jnp.sum(ds * a, axis=-1, keepdims=True)
        normed = jnp.where(j < 2 * N_HEADS, 1.0, 0.0)
        da = normed * da_norm + (1.0 - normed) * dn
        s = _sigmoid(y)
        dy = da * (s * (1.0 + y * (1.0 - s)))
        dx = dy * w[3:4, :]
        dw_ref[3:4, :] = jnp.sum(dy * xv, axis=0, keepdims=True)
        for sft in (1, 2, 3):
            xs = jnp.where(row >= sft, pltpu.roll(xv, sft, 0), 0.0)
            dw_ref[3 - sft:4 - sft, :] = jnp.sum(dy * xs, axis=0, keepdims=True)
            dys = jnp.where(row < t - sft, pltpu.roll(dy, t - sft, 0), 0.0)
            dx = dx + dys * w[3 - sft:4 - sft, :]
        dp_ref[...] = dx.astype(BF16)

    blk = pl.BlockSpec((t, D_HEAD), lambda j: (0, j))
    wblk = pl.BlockSpec((4, D_HEAD), lambda j: (0, j))
    return pl.pallas_call(
        body, name="dn_prep_bwd", grid=(3 * N_HEADS,), in_specs=[blk, wblk, blk], out_specs=[blk, wblk],
        out_shape=[jax.ShapeDtypeStruct((t, 3 * D_MODEL), BF16), jax.ShapeDtypeStruct((4, 3 * D_MODEL), F32)],
        compiler_params=_cp(),
    )(proj, conv_w, dqkv)


def _softplus_parts(xv):
    e = jnp.exp(-jnp.abs(xv))
    return jnp.maximum(xv, 0.0) + _log1p_small(e)


def _chunk_scan(v, row, reverse):
    t = v.shape[0]
    pos = row & (DN_CHUNK - 1)
    s = 1
    while s < DN_CHUNK:
        if reverse:
            v = v + jnp.where(pos < DN_CHUNK - s, pltpu.roll(v, t - s, 0), 0.0)
        else:
            v = v + jnp.where(pos >= s, pltpu.roll(v, s, 0), 0.0)
        s *= 2
    return v


def _dn_gate_fwd(proj, alog_row, dtb_row):
    t = proj.shape[0]

    def body(p_ref, al_ref, dt_ref, b_ref, g_ref):
        p = p_ref[...]
        row = lax.broadcasted_iota(jnp.int32, p.shape, 0)
        b_ref[...] = _sigmoid(p)
        g = -jnp.exp(al_ref[...]) * _softplus_parts(p + dt_ref[...])
        g_ref[...] = _chunk_scan(g, row, reverse=False)

    blk = pl.BlockSpec((t, 128), lambda i: (0, C_BA // 128))
    vec = pl.BlockSpec((1, 128), lambda i: (0, 0))
    out = pl.BlockSpec((t, 128), lambda i: (0, 0))
    return pl.pallas_call(
        body, name="dn_gate_fwd", grid=(1,), in_specs=[blk, vec, vec], out_specs=[out, out],
        out_shape=[jax.ShapeDtypeStruct((t, 128), F32)] * 2, compiler_params=_cp(),
    )(proj, alog_row, dtb_row)


def _dn_gate_bwd(proj, alog_row, dtb_row, dbeta, dgc):
    t = proj.shape[0]

    def body(p_ref, al_ref, dt_ref, db_ref, dg_ref, dp_ref, dal_ref, ddt_ref):
        p = p_ref[...]
        row = lax.broadcasted_iota(jnp.int32, p.shape, 0)
        lane = lax.broadcasted_iota(jnp.int32, p.shape, 1)
        s = _sigmoid(p)
        d_b = db_ref[...] * s * (1.0 - s)
        dg = _chunk_scan(dg_ref[...], row, reverse=True)
        xa = p + dt_ref[...]
        ea = jnp.exp(al_ref[...])
        g = -ea * _softplus_parts(xa)
        d_a = dg * (-ea) * _sigmoid(xa)
        dp_ref[...] = jnp.where(lane < N_HEADS, d_b, jnp.where(lane < 2 * N_HEADS, d_a, 0.0)).astype(BF16)
        dal_ref[...] = jnp.sum(dg * g, axis=0, keepdims=True)
        ddt_ref[...] = jnp.sum(d_a, axis=0, keepdims=True)

    blk = pl.BlockSpec((t, 128), lambda i: (0, C_BA // 128))
    vec = pl.BlockSpec((1, 128), lambda i: (0, 0))
    full = pl.BlockSpec((t, 128), lambda i: (0, 0))
    return pl.pallas_call(
        body, name="dn_gate_bwd", grid=(1,), in_specs=[blk, vec, vec, full, full], out_specs=[full, vec, vec],
        out_shape=[jax.ShapeDtypeStruct((t, 128), BF16), jax.ShapeDtypeStruct((1, 128), F32),
                   jax.ShapeDtypeStruct((1, 128), F32)], compiler_params=_cp(),
    )(proj, alog_row, dtb_row, dbeta, dgc)


def _col_to_row(col, eye):
    return jnp.sum(jnp.where(eye, col, 0.0), axis=0, keepdims=True)


def _row_to_col(rowv, eye):
    return jnp.sum(jnp.where(eye, rowv, 0.0), axis=1, keepdims=True)


def _tri_inverse(m, ri, ci):
    eye = (ri == ci).astype(F32)
    b16 = (ri >> 4) == (ci >> 4)
    b32 = (ri >> 5) == (ci >> 5)
    m1 = jnp.where(b16, m, 0.0)
    x = eye - m1
    p = _hdot(m1, m1, "nn")
    x = x + _hdot(x, p, "nn")
    p = _hdot(p, p, "nn")
    x = x + _hdot(x, p, "nn")
    p = _hdot(p, p, "nn")
    x = x + _hdot(x, p, "nn")
    c1 = jnp.where(jnp.logical_and(b32, jnp.logical_not(b16)), m, 0.0)
    x = x - _hdot(_hdot(x, c1, "nn"), x, "nn")
    c2 = jnp.where(b32, 0.0, m)
    x = x - _hdot(_hdot(x, c2, "nn"), x, "nn")
    return x


def _dn_chunk_common(q, k, b, gc, ri, ci):
    eye = ri == ci
    g_row = _col_to_row(gc, eye)
    diff = jnp.minimum(gc - g_row, 0.0)
    gam = jnp.where(ri >= ci, jnp.exp(diff), 0.0)
    kk = _hdot(k, k, "nt")
    qk = _hdot(q, k, "nt")
    rcol = lax.broadcasted_iota(jnp.int32, gc.shape, 0)
    last = jnp.sum(jnp.where(rcol == DN_CHUNK - 1, gc, 0.0), axis=0, keepdims=True)
    e_g = jnp.exp(gc)
    dec = jnp.exp(last - gc)
    return eye, gam, kk, qk, last, e_g, dec, rcol


def _dn_fwd(qkv, beta8, g8):
    t = qkv.shape[0]
    n_chunks = t // DN_CHUNK

    def body(q_ref, k_ref, v_ref, b_ref, g_ref, o_ref, s_ref, ti_ref, s_scr):
        s_scr[...] = jnp.zeros_like(s_scr)
        ri = lax.broadcasted_iota(jnp.int32, (DN_CHUNK, DN_CHUNK), 0)
        ci = lax.broadcasted_iota(jnp.int32, (DN_CHUNK, DN_CHUNK), 1)

        def chunk(n, carry):
            rows = pl.ds(pl.multiple_of(n * DN_CHUNK, DN_CHUNK), DN_CHUNK)
            q, k, v = q_ref[rows, :], k_ref[rows, :], v_ref[rows, :]
            b, gc = b_ref[0, rows, :], g_ref[0, rows, :]
            _, gam, kk, qk, last, e_g, dec, _ = _dn_chunk_common(q, k, b, gc, ri, ci)
            m = jnp.where(ri > ci, b * kk * gam, 0.0)
            tinv = _tri_inverse(m, ri, ci)
            u = _hdot(tinv, v * b, "nn")
            w = _hdot(tinv, k * (b * e_g), "nn")
            s = s_scr[...]
            s_ref[0, n] = s
            ti_ref[0, rows, :] = tinv
            v_new = u - _hdot(w, s, "nn")
            o_ref[rows, :] = _hdot(q * e_g, s, "nn") + _hdot(qk * gam, v_new, "nn")
            s_scr[...] = s * jnp.exp(last) + _hdot(k * dec, v_new, "tn")
            return carry

        lax.fori_loop(0, n_chunks, chunk, 0)

    def head(off):
        return pl.BlockSpec((t, D_HEAD), lambda h: (0, off + h))

    col = pl.BlockSpec((1, t, 1), lambda h: (h, 0, 0))
    return pl.pallas_call(
        body, name="dn_fwd", grid=(N_HEADS,),
        in_specs=[head(0), head(N_HEADS), head(2 * N_HEADS), col, col],
        out_specs=[head(0), pl.BlockSpec((1, n_chunks, D_HEAD, D_HEAD), lambda h: (h, 0, 0, 0)),
                   pl.BlockSpec((1, t, DN_CHUNK), lambda h: (h, 0, 0))],
        out_shape=[jax.ShapeDtypeStruct((t, D_MODEL), F32),
                   jax.ShapeDtypeStruct((N_HEADS, n_chunks, D_HEAD, D_HEAD), F32),
                   jax.ShapeDtypeStruct((N_HEADS, t, DN_CHUNK), F32)],
        scratch_shapes=[pltpu.VMEM((D_HEAD, D_HEAD), F32)], compiler_params=_cp(),
    )(qkv, qkv, qkv, beta8, g8)


def _dn_bwd(qkv, beta8, g8, s_all, tinv_all, do):
    t = qkv.shape[0]
    n_chunks = t // DN_CHUNK

    def body(q_ref, k_ref, v_ref, b_ref, g_ref, s_ref, ti_ref, do_ref, dq_ref, dk_ref, dv_ref, db_ref, dg_ref, ds_scr):
        ds_scr[...] = jnp.zeros_like(ds_scr)
        ri = lax.broadcasted_iota(jnp.int32, (DN_CHUNK, DN_CHUNK), 0)
        ci = lax.broadcasted_iota(jnp.int32, (DN_CHUNK, DN_CHUNK), 1)

        def chunk(i, carry):
            n = n_chunks - 1 - i
            rows = pl.ds(pl.multiple_of(n * DN_CHUNK, DN_CHUNK), DN_CHUNK)
            q, k, v = q_ref[rows, :], k_ref[rows, :], v_ref[rows, :]
            b, gc = b_ref[0, rows, :], g_ref[0, rows, :]
            d_o = do_ref[rows, :]
            s = s_ref[0, n]
            tinv = ti_ref[0, rows, :]
            d_s = ds_scr[...]
            eye, gam, kk, qk, last, e_g, dec, rcol = _dn_chunk_common(q, k, b, gc, ri, ci)
            e_last = jnp.exp(last)
            bv = v * b
            bk = k * (b * e_g)
            u = _hdot(tinv, bv, "nn")
            w = _hdot(tinv, bk, "nn")
            a = qk * gam
            q_dec = q * e_g
            k_dec = k * dec
            v_new = u - _hdot(w, s, "nn")

            dq_dec = _hdot(d_o, s, "nt")
            d_a = jnp.where(ri >= ci, _hdot(d_o, v_new, "nt"), 0.0)
            dv_new = _hdot(a, d_o, "tn") + _hdot(k_dec, d_s, "nn")
            ds_scr[...] = d_s * e_last + _hdot(q_dec, d_o, "tn") - _hdot(w, dv_new, "tn")
            dlast = jnp.sum(jnp.sum(d_s * s, axis=1, keepdims=True), axis=0, keepdims=True) * e_last
            dk_dec = _hdot(v_new, d_s, "nt")
            d_w = -_hdot(dv_new, s, "nt")
            dbv = _hdot(tinv, dv_new, "tn")
            dbk = _hdot(tinv, d_w, "tn")
            d_tinv = _hdot(dv_new, bv, "nt") + _hdot(d_w, bk, "nt")
            d_m = -jnp.where(ri > ci, _hdot(_hdot(tinv, d_tinv, "tn"), tinv, "nt"), 0.0)

            d_b = jnp.sum(d_m * kk * gam, axis=1, keepdims=True)
            d_kk = d_m * b * gam
            d_gam = d_m * b * kk + d_a * qk
            d_qk = d_a * gam
            dq_ref[rows, :] = _hdot(d_qk, k, "nn") + dq_dec * e_g
            dk_ref[rows, :] = (_hdot(d_qk, q, "tn") + _hdot(d_kk, k, "nn") + _hdot(d_kk, k, "tn")
                               + dk_dec * dec + dbk * (b * e_g))
            dv_ref[rows, :] = dbv * b
            d_b = d_b + jnp.sum(dbv * v, axis=1, keepdims=True) + jnp.sum(dbk * k, axis=1, keepdims=True) * e_g
            db_ref[0, rows, :] = d_b

            xg = d_gam * gam
            kdk = jnp.sum(dk_dec * k_dec, axis=1, keepdims=True)
            d_gc = (jnp.sum(xg, axis=1, keepdims=True) - _row_to_col(jnp.sum(xg, axis=0, keepdims=True), eye)
                    + jnp.sum(dq_dec * q_dec, axis=1, keepdims=True) - kdk
                    + jnp.sum(dbk * bk, axis=1, keepdims=True))
            d_last_total = dlast + jnp.sum(kdk, axis=0, keepdims=True)
            dg_ref[0, rows, :] = d_gc + jnp.where(rcol == DN_CHUNK - 1, d_last_total, 0.0)
            return carry

        lax.fori_loop(0, n_chunks, chunk, 0)

    def head(off):
        return pl.BlockSpec((t, D_HEAD), lambda h: (0, off + h))

    col = pl.BlockSpec((1, t, 1), lambda h: (h, 0, 0))
    res = pl.pallas_call(
        body, name="dn_bwd", grid=(N_HEADS,),
        in_specs=[head(0), head(N_HEADS), head(2 * N_HEADS), col, col,
                  pl.BlockSpec((1, n_chunks, D_HEAD, D_HEAD), lambda h: (h, 0, 0, 0)),
                  pl.BlockSpec((1, t, DN_CHUNK), lambda h: (h, 0, 0)), head(0)],
        out_specs=[head(0), head(0), head(0), col, col],
        out_shape=[jax.ShapeDtypeStruct((t, D_MODEL), F32)] * 3 + [jax.ShapeDtypeStruct((N_HEADS, t, 1), F32)] * 2,
        scratch_shapes=[pltpu.VMEM((D_HEAD, D_HEAD), F32)], compiler_params=_cp(),
    )(qkv, qkv, qkv, beta8, g8, s_all, tinv_all, do)
    return res


def _dn_post_fwd(o, proj, gn):
    t = o.shape[0]

    def body(o_ref, z_ref, g_ref, out_ref):
        ov, z = o_ref[...], z_ref[...]
        r = lax.rsqrt(jnp.mean(ov * ov, axis=-1, keepdims=True) + NORM_EPS)
        out_ref[...] = (((ov * r) * g_ref[...]) * (z * _sigmoid(z))).astype(BF16)

    blk = pl.BlockSpec((t, D_HEAD), lambda h: (0, h))
    return pl.pallas_call(
        body, name="dn_post_fwd", grid=(N_HEADS,),
        in_specs=[blk, pl.BlockSpec((t, D_HEAD), lambda h: (0, C_DNZ // D_HEAD + h)),
                  pl.BlockSpec((1, D_HEAD), lambda h: (0, 0))],
        out_specs=blk, out_shape=jax.ShapeDtypeStruct((t, D_MODEL), BF16), compiler_params=_cp(),
    )(o, proj, gn)


def _dn_post_bwd(o, proj, gn, dout):
    t = o.shape[0]

    def body(o_ref, z_ref, g_ref, d_ref, do_ref, dz_ref, dg_ref):
        @pl.when(pl.program_id(0) == 0)
        def _():
            dg_ref[...] = jnp.zeros_like(dg_ref)

        ov, z, d = o_ref[...], z_ref[...], d_ref[...]
        r = lax.rsqrt(jnp.mean(ov * ov, axis=-1, keepdims=True) + NORM_EPS)
        ohat = ov * r
        s = _sigmoid(z)
        d_on = d * (z * s)
        dz_ref[...] = (d * (ohat * g_ref[...]) * (s * (1.0 + z * (1.0 - s)))).astype(BF16)
        dg_ref[...] += jnp.sum(d_on * ohat, axis=0, keepdims=True)
        dxh = d_on * g_ref[...]
        do_ref[...] = r * (dxh - ohat * jnp.mean(dxh * ohat, axis=-1, keepdims=True))

    blk = pl.BlockSpec((t, D_HEAD), lambda h: (0, h))
    vec = pl.BlockSpec((1, D_HEAD), lambda h: (0, 0))
    return pl.pallas_call(
        body, name="dn_post_bwd", grid=(N_HEADS,),
        in_specs=[blk, pl.BlockSpec((t, D_HEAD), lambda h: (0, C_DNZ // D_HEAD + h)), vec, blk],
        out_specs=[blk, blk, vec],
        out_shape=[jax.ShapeDtypeStruct((t, D_MODEL), F32), jax.ShapeDtypeStruct((t, D_MODEL), BF16),
                   jax.ShapeDtypeStruct((1, D_HEAD), F32)], compiler_params=_cp(),
    )(o, proj, gn, dout)


def _sb_fwd(proj):
    t = proj.shape[0]
    nb = t // SB_BLOCK
    scale = 1.0 / math.sqrt(D_HEAD)

    def body(q_ref, k_ref, v_ref, z_ref, o_ref, og_ref, l_ref):
        ri = lax.broadcasted_iota(jnp.int32, (SB_BLOCK, SB_BLOCK), 0)
        ci = lax.broadcasted_iota(jnp.int32, (SB_BLOCK, SB_BLOCK), 1)
        upper = (ri > ci).astype(F32)

        def qblock(i, carry):
            rows = pl.ds(pl.multiple_of(i * SB_BLOCK, SB_BLOCK), SB_BLOCK)
            qi = q_ref[rows, :].astype(BF16)

            def kblock(jj, st):
                acc, c = st
                j = i - jj
                cols = pl.ds(pl.multiple_of(j * SB_BLOCK, SB_BLOCK), SB_BLOCK)
                z = _dot(qi, k_ref[cols, :].astype(BF16), "nt") * scale
                mask = (j * SB_BLOCK + ci) < (i * SB_BLOCK + ri)
                lb = _log_sigmoid(z)
                lf = jnp.where(mask, lb - z, 0.0)
                surv = _hdot(lf, upper, "nn") + c
                att = jnp.where(mask, jnp.exp(lb + surv), 0.0)
                acc = acc + _bdot(att, v_ref[cols, :], "nn")
                return acc, c + jnp.sum(lf, axis=1, keepdims=True)

            acc, c = lax.fori_loop(0, i + 1, kblock, (jnp.zeros((SB_BLOCK, D_HEAD), F32), jnp.zeros((SB_BLOCK, 1), F32)))
            zg = z_ref[rows, :]
            o_ref[rows, :] = acc
            og_ref[rows, :] = (acc * (zg * _sigmoid(zg))).astype(BF16)
            l_ref[0, rows, :] = c
            return carry

        lax.fori_loop(0, nb, qblock, 0)

    def head(off):
        return pl.BlockSpec((t, D_HEAD), lambda h: (0, off // D_HEAD + h))

    out = pl.BlockSpec((t, D_HEAD), lambda h: (0, h))
    return pl.pallas_call(
        body, name="sb_fwd", grid=(N_HEADS,),
        in_specs=[head(C_SBQ), head(C_SBQ + D_MODEL), head(C_SBQ + 2 * D_MODEL), head(C_SBZ)],
        out_specs=[out, out, pl.BlockSpec((1, t, 1), lambda h: (h, 0, 0))],
        out_shape=[jax.ShapeDtypeStruct((t, D_MODEL), F32), jax.ShapeDtypeStruct((t, D_MODEL), BF16),
                   jax.ShapeDtypeStruct((N_HEADS, t, 1), F32)], compiler_params=_cp(),
    )(proj, proj, proj, proj)


def _sb_bwd(proj, o, ltot, dog):
    t = proj.shape[0]
    nb = t // SB_BLOCK
    scale = 1.0 / math.sqrt(D_HEAD)

    def body(q_ref, k_ref, v_ref, z_ref, o_ref, l_ref, d_ref, dq_ref, dk_ref, dv_ref, dz_ref, dk_scr, dv_scr):
        dk_scr[...] = jnp.zeros_like(dk_scr)
        dv_scr[...] = jnp.zeros_like(dv_scr)
        ri = lax.broadcasted_iota(jnp.int32, (SB_BLOCK, SB_BLOCK), 0)
        ci = lax.broadcasted_iota(jnp.int32, (SB_BLOCK, SB_BLOCK), 1)
        incl = (ri <= ci).astype(F32)
        below = (ri < ci).astype(F32)

        def qblock(i, carry):
            rows = pl.ds(pl.multiple_of(i * SB_BLOCK, SB_BLOCK), SB_BLOCK)
            qi = q_ref[rows, :].astype(BF16)
            zg = z_ref[rows, :]
            sg = _sigmoid(zg)
            dgo = d_ref[rows, :]
            d_o = (dgo * (zg * sg)).astype(BF16)
            dz_ref[rows, :] = (dgo * o_ref[rows, :] * (sg * (1.0 + zg * (1.0 - sg)))).astype(BF16)
            lt = l_ref[0, rows, :]

            def kblock(j, st):
                dq, cpre, ce = st
                cols = pl.ds(pl.multiple_of(j * SB_BLOCK, SB_BLOCK), SB_BLOCK)
                kj = k_ref[cols, :].astype(BF16)
                vj = v_ref[cols, :].astype(BF16)
                z = _dot(qi, kj, "nt") * scale
                mask = (j * SB_BLOCK + ci) < (i * SB_BLOCK + ri)
                lb = _log_sigmoid(z)
                lfu = lb - z
                lf = jnp.where(mask, lfu, 0.0)
                surv = lt - (cpre + _hdot(lf, incl, "nn"))
                att = jnp.where(mask, jnp.exp(lb + surv), 0.0)
                e = _dot(d_o, vj, "nt") * att
                dlf = ce + _hdot(e, below, "nn")
                dzz = jnp.where(mask, e * jnp.exp(lfu) - dlf * jnp.exp(lb), 0.0).astype(BF16)
                dq = dq + _dot(dzz, kj, "nn")
                dk_scr[cols, :] += _dot(dzz, qi, "tn")
                dv_scr[cols, :] += _dot(att.astype(BF16), d_o, "tn")
                return dq, cpre + jnp.sum(lf, axis=1, keepdims=True), ce + jnp.sum(e, axis=1, keepdims=True)

            zero_col = jnp.zeros((SB_BLOCK, 1), F32)
            dq, _, _ = lax.fori_loop(0, i + 1, kblock, (jnp.zeros((SB_BLOCK, D_HEAD), F32), zero_col, zero_col))
            dq_ref[rows, :] = (dq * scale).astype(BF16)
            return carry

        lax.fori_loop(0, nb, qblock, 0)
        dk_ref[...] = (dk_scr[...] * scale).astype(BF16)
        dv_ref[...] = dv_scr[...].astype(BF16)

    def head(off):
        return pl.BlockSpec((t, D_HEAD), lambda h: (0, off // D_HEAD + h))

    return pl.pallas_call(
        body, name="sb_bwd", grid=(N_HEADS,),
        in_specs=[head(C_SBQ), head(C_SBQ + D_MODEL), head(C_SBQ + 2 * D_MODEL), head(C_SBZ), head(0),
                  pl.BlockSpec((1, t, 1), lambda h: (h, 0, 0)), head(0)],
        out_specs=[head(0)] * 4, out_shape=[jax.ShapeDtypeStruct((t, D_MODEL), BF16)] * 4,
        scratch_shapes=[pltpu.VMEM((t, D_HEAD), F32), pltpu.VMEM((t, D_HEAD), F32)], compiler_params=_cp(),
    )(proj, proj, proj, proj, o, ltot, dog)


def _mem_fwd(proj, mkv):
    t = proj.shape[0]
    tq = _pick(t, (512, 256))
    m_len = mkv.shape[0]
    scale = 1.0 / math.sqrt(MEM_DH)

    def body(q_ref, z_ref, kv_ref, o_ref, og_ref):
        q = q_ref[...]
        mk = kv_ref[:, :MEM_W].astype(BF16)
        mv = kv_ref[:, MEM_W:].astype(BF16)
        lane = lax.broadcasted_iota(jnp.int32, q.shape, 1) >> 6
        o = jnp.zeros(q.shape, F32)
        for h in range(MEM_HEADS):
            s = _bdot(jnp.where(lane == h, q, 0.0), mk, "nt") * scale
            p = jnp.exp(s - jnp.max(s, axis=-1, keepdims=True))
            p = p / jnp.sum(p, axis=-1, keepdims=True)
            o = o + jnp.where(lane == h, _bdot(p, mv, "nn"), 0.0)
        z = z_ref[...]
        o_ref[...] = o
        og_ref[...] = (o * (z * _sigmoid(z))).astype(BF16)

    out = pl.BlockSpec((tq, MEM_W), lambda i: (i, 0))
    return pl.pallas_call(
        body, name="mem_fwd", grid=(t // tq,),
        in_specs=[pl.BlockSpec((tq, MEM_W), lambda i: (i, C_MQ // MEM_W)),
                  pl.BlockSpec((tq, MEM_W), lambda i: (i, C_MZ // MEM_W)),
                  pl.BlockSpec((m_len, 2 * MEM_W), lambda i: (0, 0))],
        out_specs=[out, out],
        out_shape=[jax.ShapeDtypeStruct((t, MEM_W), F32), jax.ShapeDtypeStruct((t, MEM_W), BF16)],
        compiler_params=_cp(),
    )(proj, proj, mkv)


def _mem_bwd(proj, mkv, o, dog):
    t = proj.shape[0]
    tq = _pick(t, (512, 256))
    m_len = mkv.shape[0]
    scale = 1.0 / math.sqrt(MEM_DH)

    def body(q_ref, z_ref, kv_ref, o_ref, d_ref, dq_ref, dz_ref, dkv_ref):
        @pl.when(pl.program_id(0) == 0)
        def _():
            dkv_ref[...] = jnp.zeros_like(dkv_ref)

        q = q_ref[...]
        z = z_ref[...]
        sg = _sigmoid(z)
        dgo = d_ref[...]
        d_o = dgo * (z * sg)
        dz_ref[...] = (dgo * o_ref[...] * (sg * (1.0 + z * (1.0 - sg)))).astype(BF16)
        mk = kv_ref[:, :MEM_W].astype(BF16)
        mv = kv_ref[:, MEM_W:].astype(BF16)
        lane = lax.broadcasted_iota(jnp.int32, q.shape, 1) >> 6
        klane = lax.broadcasted_iota(jnp.int32, (m_len, MEM_W), 1) >> 6
        dq = jnp.zeros(q.shape, F32)
        dmk = jnp.zeros((m_len, MEM_W), F32)
        dmv = jnp.zeros((m_len, MEM_W), F32)
        for h in range(MEM_HEADS):
            qh = jnp.where(lane == h, q, 0.0)
            doh = jnp.where(lane == h, d_o, 0.0)
            s = _bdot(qh, mk, "nt") * scale
            p = jnp.exp(s - jnp.max(s, axis=-1, keepdims=True))
            p = p / jnp.sum(p, axis=-1, keepdims=True)
            dp = _bdot(doh, mv, "nt")
            ds = p * (dp - jnp.sum(dp * p, axis=-1, keepdims=True)) * scale
            dq = dq + jnp.where(lane == h, _bdot(ds, mk, "nn"), 0.0)
            dmk = dmk + jnp.where(klane == h, _bdot(ds, qh, "tn"), 0.0)
            dmv = dmv + jnp.where(klane == h, _bdot(p, doh, "tn"), 0.0)
        dq_ref[...] = dq.astype(BF16)
        dkv_ref[:, :MEM_W] += dmk
        dkv_ref[:, MEM_W:] += dmv

    blk = pl.BlockSpec((tq, MEM_W), lambda i: (i, 0))
    kv = pl.BlockSpec((m_len, 2 * MEM_W), lambda i: (0, 0))
    return pl.pallas_call(
        body, name="mem_bwd", grid=(t // tq,),
        in_specs=[pl.BlockSpec((tq, MEM_W), lambda i: (i, C_MQ // MEM_W)),
                  pl.BlockSpec((tq, MEM_W), lambda i: (i, C_MZ // MEM_W)), kv, blk, blk],
        out_specs=[blk, blk, kv],
        out_shape=[jax.ShapeDtypeStruct((t, MEM_W), BF16), jax.ShapeDtypeStruct((t, MEM_W), BF16),
                   jax.ShapeDtypeStruct((m_len, 2 * MEM_W), F32)], compiler_params=_cp(),
    )(proj, proj, mkv, o, dog)


_GW = 512


def _merge_fwd(proj, y_dn, y_sb, y_m):
    t = proj.shape[0]
    tb = _pick(t, (256,))
    nc = D_MODEL // _GW

    def body(g1, g2, g3, y1, y2, y3, out_ref):
        out_ref[...] = (_sigmoid(g1[...]) * y1[...] + _sigmoid(g2[...]) * y2[...] + _sigmoid(g3[...]) * y3[...]).astype(BF16)

    def gate(kb):
        return pl.BlockSpec((tb, _GW), lambda i, c: (i, C_GATES // _GW + kb * nc + c))

    blk = pl.BlockSpec((tb, _GW), lambda i, c: (i, c))
    return pl.pallas_call(
        body, name="merge_fwd", grid=(t // tb, nc), in_specs=[gate(0), gate(1), gate(2), blk, blk, blk],
        out_specs=blk, out_shape=jax.ShapeDtypeStruct((t, D_MODEL), BF16), compiler_params=_cp(),
    )(proj, proj, proj, y_dn, y_sb, y_m)


def _merge_bwd(proj, y_dn, y_sb, y_m, dm):
    t = proj.shape[0]
    tb = _pick(t, (256,))
    nc = D_MODEL // _GW

    def body(g1, g2, g3, y1, y2, y3, dm_ref, d1, d2, d3, dg1, dg2, dg3):
        d = dm_ref[...]
        for g, y, dy, dg in ((g1, y1, d1, dg1), (g2, y2, d2, dg2), (g3, y3, d3, dg3)):
            s = _sigmoid(g[...])
            dy[...] = (d * s).astype(BF16)
            dg[...] = (d * y[...] * (s * (1.0 - s))).astype(BF16)

    def gate(kb):
        return pl.BlockSpec((tb, _GW), lambda i, c: (i, C_GATES // _GW + kb * nc + c))

    blk = pl.BlockSpec((tb, _GW), lambda i, c: (i, c))
    act = jax.ShapeDtypeStruct((t, D_MODEL), BF16)
    return pl.pallas_call(
        body, name="merge_bwd", grid=(t // tb, nc), in_specs=[gate(0), gate(1), gate(2), blk, blk, blk, blk],
        out_specs=[blk] * 6, out_shape=[act] * 6, compiler_params=_cp(),
    )(proj, proj, proj, y_dn, y_sb, y_m, dm)


def _final_loss(x, mo, g, tgt):
    t, d = x.shape
    tb = _pick(t, (256,))

    def body(x_ref, mo_ref, g_ref, t_ref, do_ref, dob_ref, loss_ref, dg_ref):
        @pl.when(pl.program_id(0) == 0)
        def _():
            loss_ref[...] = jnp.zeros_like(loss_ref)
            dg_ref[...] = jnp.zeros_like(dg_ref)

        out = x_ref[...] + mo_ref[...]
        r = lax.rsqrt(jnp.mean(out * out, axis=-1, keepdims=True) + NORM_EPS)
        xhat = out * r
        gv = g_ref[...]
        err = xhat * gv - t_ref[...]
        per_tok = jnp.mean(err * err, axis=-1, keepdims=True)
        loss_ref[...] += 0.5 * jnp.sum(per_tok, axis=0, keepdims=True)
        dy = err * (1.0 / d)
        dg_ref[...] += jnp.sum(dy * xhat, axis=0, keepdims=True)
        dxh = dy * gv
        dout = r * (dxh - xhat * jnp.mean(dxh * xhat, axis=-1, keepdims=True))
        do_ref[...] = dout
        dob_ref[...] = dout.astype(BF16)

    row = pl.BlockSpec((tb, d), lambda i: (i, 0))
    vec = pl.BlockSpec((1, d), lambda i: (0, 0))
    return pl.pallas_call(
        body, name="final_loss", grid=(t // tb,), in_specs=[row, row, vec, row],
        out_specs=[row, row, pl.BlockSpec((1, 128), lambda i: (0, 0)), vec],
        out_shape=[jax.ShapeDtypeStruct((t, d), F32), jax.ShapeDtypeStruct((t, d), BF16),
                   jax.ShapeDtypeStruct((1, 128), F32), jax.ShapeDtypeStruct((1, d), F32)],
        compiler_params=_cp(),
    )(x, mo, g, tgt)


def _rows_call(body, name, ins, n_out, out_dtypes):
    r, c = ins[0].shape[-2:]
    tb = _pick(r, (128, 496))
    specs = []
    for a in ins:
        if a.ndim == 3:
            specs.append(pl.BlockSpec((a.shape[0], tb, c), lambda i: (0, i, 0)))
        else:
            specs.append(pl.BlockSpec((tb, c), lambda i: (i, 0)))
    out_blk = pl.BlockSpec((tb, c), lambda i: (i, 0))
    return pl.pallas_call(
        body, name=name, grid=(r // tb,), in_specs=specs, out_specs=[out_blk] * n_out,
        out_shape=[jax.ShapeDtypeStruct((r, c), dt) for dt in out_dtypes], compiler_params=_cp(),
    )(*ins)


def _cast_bf16(a, name):
    def body(a_ref, o_ref):
        o_ref[...] = a_ref[...].astype(BF16)

    return _rows_call(body, name, [a], 1, [BF16])[0]


def _add_pair(a, b, name):
    n, r, c = a.shape
    tb = _pick(r, (64, 248))

    def body(a_ref, b_ref, o_ref):
        o_ref[...] = a_ref[...] + b_ref[...]

    blk = pl.BlockSpec((n, tb, c), lambda i: (0, i, 0))
    return pl.pallas_call(body, name=name, grid=(r // tb,), in_specs=[blk, blk], out_specs=blk,
                          out_shape=jax.ShapeDtypeStruct(a.shape, F32), compiler_params=_cp())(a, b)


def _sum_lead(a, name):
    def body(a_ref, o_ref):
        acc = a_ref[0]
        for i in range(1, a.shape[0]):
            acc = acc + a_ref[i]
        o_ref[...] = acc

    return _rows_call(body, name, [a], 1, [F32])[0]


def _adamw_math(w, g, m, v):
    m = ADAM_B1 * m + (1.0 - ADAM_B1) * g
    v = ADAM_B2 * v + (1.0 - ADAM_B2) * (g * g)
    m_hat = m / (1.0 - ADAM_B1 ** ADAM_STEP)
    v_hat = v / (1.0 - ADAM_B2 ** ADAM_STEP)
    delta = -ADAM_LR * (m_hat / (jnp.sqrt(v_hat) + ADAM_EPS) + ADAM_WD * w)
    return delta, m, v


def _adamw(w, g, m, v, name):
    def body(w_ref, g_ref, m_ref, v_ref, d_ref, mo_ref, vo_ref):
        d, mn, vn = _adamw_math(w_ref[...], g_ref[...], m_ref[...], v_ref[...])
        d_ref[...] = d
        mo_ref[...] = mn
        vo_ref[...] = vn

    return _rows_call(body, name, [w, g, m, v], 3, [F32, F32, F32])


def _small_update(gathered, w, m, v):
    def body(p_ref, w_ref, m_ref, v_ref, g_ref, d_ref, mo_ref, vo_ref):
        g = p_ref[0]
        for i in range(1, N_DEV):
            g = g + p_ref[i]
        d, mn, vn = _adamw_math(w_ref[...], g, m_ref[...], v_ref[...])
        g_ref[...] = g
        d_ref[...] = d
        mo_ref[...] = mn
        vo_ref[...] = vn

    full = pl.BlockSpec((32, 128), lambda i: (0, 0))
    return pl.pallas_call(
        body, name="small_update", grid=(1,),
        in_specs=[pl.BlockSpec((N_DEV, 32, 128), lambda i: (0, 0, 0)), full, full, full], out_specs=[full] * 4,
        out_shape=[jax.ShapeDtypeStruct((32, 128), F32)] * 4, compiler_params=_cp(),
    )(gathered, w, m, v)


_ANY = pl.BlockSpec(memory_space=pl.ANY)


def _place():
    x, y, c = lax.axis_index("x"), lax.axis_index("y"), lax.axis_index("c")
    chips = [(1 - x, y), (x, 1 - y), (1 - x, 1 - y)]
    return x, y, c, chips


def _gather_shards(arrs):
    n = len(arrs)

    def body(*refs):
        ins, outs = refs[:n], refs[n:2 * n]
        send_sems, recv_sems, local_sems = refs[2 * n:]
        x, y, c, chips = _place()
        me = 2 * x + y
        sibling = (x, y, 1 - c)
        locals_, sends = [], []
        for a in range(n):
            half = ins[a].shape[0] // 2
            mine = pl.ds(pl.multiple_of(c * half, 16), half)
            lc = pltpu.make_async_copy(ins[a], outs[a].at[me], local_sems.at[a])
            lc.start()
            locals_.append(lc)
            for j, (qx, qy) in enumerate(chips):
                cp = pltpu.make_async_remote_copy(
                    src_ref=ins[a].at[mine], dst_ref=outs[a].at[me, mine],
                    send_sem=send_sems.at[6 * a + j], recv_sem=recv_sems.at[6 * a + j],
                    device_id=(qx, qy, c), device_id_type=MESH)
                cp.start()
                sends.append(cp)
        for a in range(n):
            half = ins[a].shape[0] // 2
            mine = pl.ds(pl.multiple_of(c * half, 16), half)
            for j, (qx, qy) in enumerate(chips):
                q = 2 * qx + qy
                landed = outs[a].at[q, mine]
                pltpu.make_async_remote_copy(
                    src_ref=landed, dst_ref=landed, send_sem=send_sems.at[6 * a + j], recv_sem=recv_sems.at[6 * a + j],
                    device_id=(qx, qy, c), device_id_type=MESH).wait_recv()
                fw = pltpu.make_async_remote_copy(
                    src_ref=landed, dst_ref=landed, send_sem=send_sems.at[6 * a + 3 + j],
                    recv_sem=recv_sems.at[6 * a + 3 + j], device_id=sibling, device_id_type=MESH)
                fw.start()
                sends.append(fw)
        for a in range(n):
            half = ins[a].shape[0] // 2
            theirs = pl.ds(pl.multiple_of((1 - c) * half, 16), half)
            for j, (qx, qy) in enumerate(chips):
                q = 2 * qx + qy
                dst = outs[a].at[q, theirs]
                pltpu.make_async_remote_copy(
                    src_ref=dst, dst_ref=dst, send_sem=send_sems.at[6 * a + 3 + j], recv_sem=recv_sems.at[6 * a + 3 + j],
                    device_id=sibling, device_id_type=MESH).wait_recv()
        for cp in sends:
            cp.wait_send()
        for lc in locals_:
            lc.wait()

    return pl.pallas_call(
        body, name="gather_shards", in_specs=[_ANY] * n, out_specs=[_ANY] * n,
        out_shape=[jax.ShapeDtypeStruct((N_SHARD,) + a.shape, a.dtype) for a in arrs],
        scratch_shapes=[pltpu.SemaphoreType.DMA((6 * n,)), pltpu.SemaphoreType.DMA((6 * n,)),
                        pltpu.SemaphoreType.DMA((n,))],
        compiler_params=pltpu.CompilerParams(has_side_effects=True),
    )(*arrs)


def _pair_reduce_send(grads):
    n = len(grads)

    def body(*refs):
        ins, outs = refs[:n], refs[n:2 * n]
        send_sems, recv_sems = refs[2 * n:]
        x, y, c, _ = _place()
        sibling = (x, y, 1 - c)
        cps = []
        for a in range(n):
            half = ins[a].shape[1] // 2
            theirs = pl.ds(pl.multiple_of((1 - c) * half, 8), half)
            cp = pltpu.make_async_remote_copy(
                src_ref=ins[a].at[:, theirs], dst_ref=outs[a], send_sem=send_sems.at[a], recv_sem=recv_sems.at[a],
                device_id=sibling, device_id_type=MESH)
            cp.start()
            cps.append(cp)
        for cp in cps:
            cp.wait()

    return pl.pallas_call(
        body, name="pair_reduce_send", in_specs=[_ANY] * n, out_specs=[_ANY] * n,
        out_shape=[jax.ShapeDtypeStruct((g.shape[0], g.shape[1] // 2, g.shape[2]), g.dtype) for g in grads],
        scratch_shapes=[pltpu.SemaphoreType.DMA((n,)), pltpu.SemaphoreType.DMA((n,))],
        compiler_params=pltpu.CompilerParams(has_side_effects=True),
    )(*grads)


def _chip_exchange(parts):
    n = len(parts)

    def body(*refs):
        ins, outs = refs[:n], refs[n:2 * n]
        send_sems, recv_sems, local_sems = refs[2 * n:]
        x, y, c, chips = _place()
        me = 2 * x + y
        locals_, cps = [], []
        for a in range(n):
            lc = pltpu.make_async_copy(ins[a].at[me], outs[a].at[me], local_sems.at[a])
            lc.start()
            locals_.append(lc)
            for j, (qx, qy) in enumerate(chips):
                q = 2 * qx + qy
                cp = pltpu.make_async_remote_copy(
                    src_ref=ins[a].at[q], dst_ref=outs[a].at[me], send_sem=send_sems.at[3 * a + j],
                    recv_sem=recv_sems.at[3 * a + j], device_id=(qx, qy, c), device_id_type=MESH)
                cp.start()
                cps.append(cp)
        for a in range(n):
            for j, (qx, qy) in enumerate(chips):
                q = 2 * qx + qy
                dst = outs[a].at[q]
                pltpu.make_async_remote_copy(
                    src_ref=dst, dst_ref=dst, send_sem=send_sems.at[3 * a + j], recv_sem=recv_sems.at[3 * a + j],
                    device_id=(qx, qy, c), device_id_type=MESH).wait_recv()
        for cp in cps:
            cp.wait_send()
        for lc in locals_:
            lc.wait()

    return pl.pallas_call(
        body, name="chip_exchange", in_specs=[_ANY] * n, out_specs=[_ANY] * n,
        out_shape=[jax.ShapeDtypeStruct(p.shape, p.dtype) for p in parts],
        scratch_shapes=[pltpu.SemaphoreType.DMA((3 * n,)), pltpu.SemaphoreType.DMA((3 * n,)),
                        pltpu.SemaphoreType.DMA((n,))],
        compiler_params=pltpu.CompilerParams(has_side_effects=True),
    )(*parts)


def _pair_allgather(halves):
    n = len(halves)

    def body(*refs):
        ins, outs = refs[:n], refs[n:2 * n]
        send_sems, recv_sems, local_sems = refs[2 * n:]
        x, y, c, _ = _place()
        sibling = (x, y, 1 - c)
        cps, locals_ = [], []
        for a in range(n):
            half = ins[a].shape[0]
            mine = pl.ds(pl.multiple_of(c * half, 8), half)
            lc = pltpu.make_async_copy(ins[a], outs[a].at[mine], local_sems.at[a])
            lc.start()
            locals_.append(lc)
            cp = pltpu.make_async_remote_copy(
                src_ref=ins[a], dst_ref=outs[a].at[mine], send_sem=send_sems.at[a], recv_sem=recv_sems.at[a],
                device_id=sibling, device_id_type=MESH)
            cp.start()
            cps.append(cp)
        for a in range(n):
            half = ins[a].shape[0]
            theirs = outs[a].at[pl.ds(pl.multiple_of((1 - c) * half, 8), half)]
            pltpu.make_async_remote_copy(
                src_ref=theirs, dst_ref=theirs, send_sem=send_sems.at[a], recv_sem=recv_sems.at[a],
                device_id=sibling, device_id_type=MESH).wait_recv()
        for cp in cps:
            cp.wait_send()
        for lc in locals_:
            lc.wait()

    return pl.pallas_call(
        body, name="pair_allgather", in_specs=[_ANY] * n, out_specs=[_ANY] * n,
        out_shape=[jax.ShapeDtypeStruct((2 * h.shape[0], h.shape[1]), h.dtype) for h in halves],
        scratch_shapes=[pltpu.SemaphoreType.DMA((n,)), pltpu.SemaphoreType.DMA((n,)), pltpu.SemaphoreType.DMA((n,))],
        compiler_params=pltpu.CompilerParams(has_side_effects=True),
    )(*halves)


def _allgather_small(slab):
    def body(s_ref, out_ref, send_sems, recv_sems):
        x, y, c, _ = _place()
        me = 4 * x + 2 * y + c
        out_ref[me] = s_ref[...]
        cps = []
        for mask in range(1, N_DEV):
            peer = (x ^ (mask >> 2), y ^ ((mask >> 1) & 1), c ^ (mask & 1))
            cp = pltpu.make_async_remote_copy(
                src_ref=s_ref, dst_ref=out_ref.at[me], send_sem=send_sems.at[mask - 1], recv_sem=recv_sems.at[mask - 1],
                device_id=peer, device_id_type=MESH)
            cp.start()
            cps.append(cp)
        for mask in range(1, N_DEV):
            peer = (x ^ (mask >> 2), y ^ ((mask >> 1) & 1), c ^ (mask & 1))
            dst = out_ref.at[4 * peer[0] + 2 * peer[1] + peer[2]]
            pltpu.make_async_remote_copy(
                src_ref=dst, dst_ref=dst, send_sem=send_sems.at[mask - 1], recv_sem=recv_sems.at[mask - 1],
                device_id=peer, device_id_type=MESH).wait_recv()
        for cp in cps:
            cp.wait_send()

    vm = pl.BlockSpec(memory_space=pltpu.VMEM)
    return pl.pallas_call(
        body, name="allgather_small", in_specs=[vm], out_specs=vm,
        out_shape=jax.ShapeDtypeStruct((N_DEV,) + slab.shape, slab.dtype),
        scratch_shapes=[pltpu.SemaphoreType.DMA((N_DEV - 1,)), pltpu.SemaphoreType.DMA((N_DEV - 1,))],
        compiler_params=pltpu.CompilerParams(has_side_effects=True),
    )(slab)


def _pack_b(w_mem_kv, w_br_dn, w_br_sb, w_br_mem, w_out, conv_w):
    parts = [w_mem_kv.reshape(128, D_MODEL), w_br_dn, w_br_sb, w_br_mem.reshape(64, D_MODEL), w_out,
             conv_w.reshape(3, D_MODEL)]
    rows = sum(p.shape[0] for p in parts)
    return jnp.concatenate(parts + [jnp.zeros((B_ROWS - rows, D_MODEL), parts[0].dtype)], axis=0)


def _unpack_b(slab):
    return (slab[B_MEMKV:B_BRDN].reshape(1, 256, 512), slab[B_BRDN:B_BRSB].reshape(1, 256, D_MODEL),
            slab[B_BRSB:B_BRMEM].reshape(1, 256, D_MODEL), slab[B_BRMEM:B_OUT].reshape(1, 256, 256),
            slab[B_OUT:B_CONV].reshape(1, 256, D_MODEL), slab[B_CONV:B_CONV + 3].reshape(1, 4, 768))


def _pack_small(norm_g, mem_norm_g, final_g, dn_norm_g, a_log, dt_bias, loss=None):
    slab = jnp.zeros((32, 128), F32)
    slab = slab.at[S_NORM:S_NORM + 8].set(norm_g.reshape(8, 128))
    slab = slab.at[S_MEMNORM:S_MEMNORM + 8].set(mem_norm_g.reshape(8, 128))
    slab = slab.at[S_FINAL:S_FINAL + 8].set(final_g.reshape(8, 128))
    slab = slab.at[S_DNNORM].set(dn_norm_g.reshape(128))
    slab = slab.at[S_ALOG, :N_HEADS].set(a_log.reshape(N_HEADS))
    slab = slab.at[S_DTB, :N_HEADS].set(dt_bias.reshape(N_HEADS))
    if loss is not None:
        slab = slab.at[S_LOSS, 0].set(loss)
    return slab


def _unpack_small(slab):
    return (slab[S_NORM:S_NORM + 8].reshape(1, D_MODEL), slab[S_MEMNORM:S_MEMNORM + 8].reshape(1, D_MODEL),
            slab[S_FINAL:S_FINAL + 8].reshape(D_MODEL), slab[S_DNNORM].reshape(1, 128),
            slab[S_ALOG, :N_HEADS].reshape(1, N_HEADS), slab[S_DTB, :N_HEADS].reshape(1, N_HEADS))


def _reorder_w_in(w_full):
    pad = jnp.zeros((w_full.shape[0], W_R - IN_WIDTH), w_full.dtype)
    return jnp.concatenate([w_full[:, :4096], w_full[:, 4112:], w_full[:, 4096:4112], pad], axis=1)


def _restore_w_in(g_r):
    return jnp.concatenate([g_r[:, :4096], g_r[:, C_BA:C_BA + 16], g_r[:, 4096:C_BA]], axis=1)


def _local_step(x, mem, tgt, norm_g, mem_norm_g, w_r, conv_w, a_log, dt_bias, dn_norm_g, w_mem_kv, w_br_dn, w_br_sb,
                w_br_mem, w_out, final_g):
    t = x.shape[0]
    final_row = final_g.reshape(1, D_MODEL)
    alog_row = jnp.zeros((1, 128), F32).at[0, N_HEADS:2 * N_HEADS].set(a_log.reshape(N_HEADS))
    dtb_row = jnp.zeros((1, 128), F32).at[0, N_HEADS:2 * N_HEADS].set(dt_bias.reshape(N_HEADS))

    h = _rmsnorm_fwd(x, norm_g, "norm_fwd")
    proj = _mm(h, w_r, "nn", "in_proj")
    qkv = _dn_prep_fwd(proj, conv_w)
    beta_t, g_t = _dn_gate_fwd(proj, alog_row, dtb_row)
    beta8 = beta_t[:, :N_HEADS].T.reshape(N_HEADS, t, 1)
    g8 = g_t[:, N_HEADS:2 * N_HEADS].T.reshape(N_HEADS, t, 1)
    o_dn, s_all, tinv_all = _dn_fwd(qkv, beta8, g8)
    o_dn_g = _dn_post_fwd(o_dn, proj, dn_norm_g)
    o_sb, o_sb_g, sb_l = _sb_fwd(proj)
    mem_n = _rmsnorm_fwd(mem, mem_norm_g, "mem_norm_fwd")
    mkv = _mm(mem_n, w_mem_kv, "nn", "mem_kv")
    o_m, o_m_g = _mem_fwd(proj, mkv)
    y_dn = _mm(o_dn_g, w_br_dn, "nn", "br_dn")
    y_sb = _mm(o_sb_g, w_br_sb, "nn", "br_sb")
    y_m = _mm(o_m_g, w_br_mem, "nn", "br_mem")
    merged = _merge_fwd(proj, y_dn, y_sb, y_m)
    mo = _mm(merged, w_out, "nn", "out_proj")
    d_out, d_out_b, loss_row, g_final = _final_loss(x, mo, final_row, tgt)

    g_w_out = _mm(merged, d_out_b, "tn", "g_w_out")
    d_merged = _mm(d_out_b, w_out, "nt", "d_merged")
    dy_dn, dy_sb, dy_m, dg1, dg2, dg3 = _merge_bwd(proj, y_dn, y_sb, y_m, d_merged)
    g_w_br_dn = _mm(o_dn_g, dy_dn, "tn", "g_w_br_dn")
    g_w_br_sb = _mm(o_sb_g, dy_sb, "tn", "g_w_br_sb")
    g_w_br_mem = _mm(o_m_g, dy_m, "tn", "g_w_br_mem")
    d_o_dn_g = _mm(dy_dn, w_br_dn, "nt", "d_o_dn")
    d_o_sb_g = _mm(dy_sb, w_br_sb, "nt", "d_o_sb")
    d_o_m_g = _mm(dy_m, w_br_mem, "nt", "d_o_mem")

    d_mq, d_mz, d_mkv = _mem_bwd(proj, mkv, o_m, d_o_m_g)
    d_mkv_b = _cast_bf16(d_mkv, "cast_dmkv")
    g_w_mem_kv = _mm(mem_n, d_mkv_b, "tn", "g_w_mem_kv")
    d_mem_n = _mm(d_mkv_b, w_mem_kv, "nt", "d_mem_n")
    _, g_mem_norm = _rmsnorm_bwd(mem, mem_norm_g, d_mem_n, jnp.zeros_like(mem), "mem_norm_bwd")

    d_sq, d_sk, d_sv, d_sz = _sb_bwd(proj, o_sb, sb_l, d_o_sb_g)

    d_o_dn, d_dnz, g_dn_norm = _dn_post_bwd(o_dn, proj, dn_norm_g, d_o_dn_g)
    d_qn, d_kn, d_vn, d_beta8, d_g8 = _dn_bwd(qkv, beta8, g8, s_all, tinv_all, d_o_dn)
    d_qkv_n = jnp.concatenate([d_qn, d_kn, d_vn], axis=1)
    d_conv_in, g_conv = _dn_prep_bwd(proj, conv_w, d_qkv_n)
    pad = jnp.zeros((t, 128 - N_HEADS), F32)
    dbeta_t = jnp.concatenate([d_beta8.reshape(N_HEADS, t).T, pad], axis=1)
    dg_t = jnp.concatenate([pad[:, :N_HEADS], d_g8.reshape(N_HEADS, t).T, pad[:, :128 - 2 * N_HEADS]], axis=1)
    d_ba, g_alog_row, g_dtb_row = _dn_gate_bwd(proj, alog_row, dtb_row, dbeta_t, dg_t)

    dproj = jnp.concatenate([d_conv_in, d_dnz, d_sq, d_sk, d_sv, d_sz, d_mq, d_mz, dg1, dg2, dg3, d_ba], axis=1)
    g_w_r = _mm(h, dproj, "tn", "g_w_in")
    dh = _mm(dproj, w_r, "nt", "d_h")
    grad_x, g_norm = _rmsnorm_bwd(x, norm_g, dh, d_out, "norm_bwd")

    small = dict(norm_g=g_norm, mem_norm_g=g_mem_norm, final_g=g_final, dn_norm_g=g_dn_norm,
                 a_log=g_alog_row[:, N_HEADS:2 * N_HEADS], dt_bias=g_dtb_row[:, N_HEADS:2 * N_HEADS])
    big = dict(w_r=g_w_r, conv_w=g_conv, w_mem_kv=g_w_mem_kv, w_br_dn=g_w_br_dn, w_br_sb=g_w_br_sb,
               w_br_mem=g_w_br_mem, w_out=g_w_out)
    return loss_row[0, 0], grad_x, small, big


def _reduce_scatter(grads):
    x, y, c = lax.axis_index("x"), lax.axis_index("y"), lax.axis_index("c")
    recv = _pair_reduce_send(grads)
    parts = []
    for g, r in zip(grads, recv):
        half = g.shape[1] // 2
        mine = lax.dynamic_slice_in_dim(g, c * half, half, axis=1)
        parts.append(_add_pair(mine, r, "pair_add"))
    by_chip = _chip_exchange(parts)
    halves = [_sum_lead(b, "chip_sum") for b in by_chip]
    return _pair_allgather(halves)


def kernel(x, mem, norm_g, mem_norm_g, w_in, conv_w, a_log, dt_bias, dn_norm_g, w_mem_kv, w_br_dn, w_br_sb, w_br_mem, w_out, final_g, loss_target, m_norm_g, m_mem_norm_g, m_w_in, m_conv_w, m_a_log, m_dt_bias, m_dn_norm_g, m_w_mem_kv, m_w_br_dn, m_w_br_sb, m_w_br_mem, m_w_out, m_final_g, v_norm_g, v_mem_norm_g, v_w_in, v_conv_w, v_a_log, v_dt_bias, v_dn_norm_g, v_w_mem_kv, v_w_br_dn, v_w_br_sb, v_w_br_mem, v_w_out, v_final_g):
    w_a = w_in[0]
    w_b = _pack_b(w_mem_kv[0], w_br_dn[0], w_br_sb[0], w_br_mem[0], w_out[0], conv_w[0])
    m_b = _pack_b(m_w_mem_kv[0], m_w_br_dn[0], m_w_br_sb[0], m_w_br_mem[0], m_w_out[0], m_conv_w[0])
    v_b = _pack_b(v_w_mem_kv[0], v_w_br_dn[0], v_w_br_sb[0], v_w_br_mem[0], v_w_out[0], v_conv_w[0])

    ga, gb = _gather_shards([_cast_bf16(w_a, "cast_w_in"), _cast_bf16(w_b, "cast_w_b")])
    w_full = ga.transpose(1, 0, 2).reshape(D_MODEL, IN_WIDTH)
    w_r = _reorder_w_in(w_full)
    f_mem_kv = gb[:, B_MEMKV:B_BRDN].reshape(N_SHARD * 256, 512)
    f_br_dn = gb[:, B_BRDN:B_BRSB].reshape(N_SHARD * 256, D_MODEL)
    f_br_sb = gb[:, B_BRSB:B_BRMEM].reshape(N_SHARD * 256, D_MODEL)
    f_br_mem = gb[:, B_BRMEM:B_OUT].reshape(N_SHARD, 256, 256).transpose(1, 0, 2).reshape(256, D_MODEL)
    f_out = gb[:, B_OUT:B_CONV].reshape(N_SHARD * 256, D_MODEL)
    f_conv = gb[:, B_CONV:B_CONV + 3].reshape(N_SHARD, 4, 768).transpose(1, 0, 2).reshape(4, 3 * D_MODEL).astype(F32)

    loss, grad_x, small, big = _local_step(
        x[0], mem[0], loss_target[0], norm_g, mem_norm_g, w_r, f_conv, a_log, dt_bias, dn_norm_g,
        f_mem_kv, f_br_dn, f_br_sb, f_br_mem, f_out, final_g)

    g_in = _restore_w_in(big["w_r"]).reshape(D_MODEL, N_SHARD, IN_WIDTH // N_SHARD).transpose(1, 0, 2)
    g_b = jnp.stack([
        _pack_b(big["w_mem_kv"][256 * s:256 * (s + 1)], big["w_br_dn"][256 * s:256 * (s + 1)],
                big["w_br_sb"][256 * s:256 * (s + 1)], big["w_br_mem"][:, 256 * s:256 * (s + 1)],
                big["w_out"][256 * s:256 * (s + 1)], big["conv_w"][:, 768 * s:768 * (s + 1)])
        for s in range(N_SHARD)])
    gs_in, gs_b = _reduce_scatter([g_in, g_b])

    d_in, nm_in, nv_in = _adamw(w_a, gs_in, m_w_in[0], v_w_in[0], "adamw_w_in")
    d_b, nm_b, nv_b = _adamw(w_b, gs_b, m_b, v_b, "adamw_b")

    part = _pack_small(small["norm_g"], small["mem_norm_g"], small["final_g"], small["dn_norm_g"],
                       small["a_log"], small["dt_bias"], loss)
    w_s = _pack_small(norm_g, mem_norm_g, final_g, dn_norm_g, a_log, dt_bias)
    m_s = _pack_small(m_norm_g, m_mem_norm_g, m_final_g, m_dn_norm_g, m_a_log, m_dt_bias)
    v_s = _pack_small(v_norm_g, v_mem_norm_g, v_final_g, v_dn_norm_g, v_a_log, v_dt_bias)
    g_s, d_s, nm_s, nv_s = _small_update(_allgather_small(part), w_s, m_s, v_s)

    def assemble(slab_small, a_in, slab_b):
        s_norm, s_memnorm, s_final, s_dnnorm, s_alog, s_dtb = _unpack_small(slab_small)
        b_memkv, b_brdn, b_brsb, b_brmem, b_out, b_conv = _unpack_b(slab_b)
        return [s_norm, s_memnorm, a_in.reshape(1, D_MODEL, IN_WIDTH // N_SHARD), b_conv, s_alog, s_dtb, s_dnnorm,
                b_memkv, b_brdn, b_brsb, b_brmem, b_out, s_final]

    outs = [g_s[S_LOSS, 0], grad_x.reshape(1, -1, D_MODEL)]
    outs += assemble(g_s, gs_in, gs_b)
    outs += assemble(d_s, d_in, d_b)
    outs += assemble(nm_s, nm_in, nm_b)
    outs += assemble(nv_s, nv_in, nv_b)
    return tuple(outs)
```

```python
import functools
import math

import jax
import jax.numpy as jnp
from jax import lax
from jax.experimental import pallas as pl
from jax.experimental.pallas import tpu as pltpu

F32 = jnp.float32
BF16 = jnp.bfloat16
MESH = pl.DeviceIdType.MESH
HIGHEST = lax.Precision.HIGHEST

D_MODEL = 1024
N_HEADS = 8
D_HEAD = 128
DN_CHUNK = 64
DN_GROUP = 4
DN_HEADS_PER_STEP = 2
SB_BLOCK = 128
SB_HEADS_PER_STEP = 2
SB_QBLOCK = 256
MEM_HEADS = 4
MEM_DH = 64
MEM_W = MEM_HEADS * MEM_DH
NORM_EPS = 1e-6
IN_WIDTH = 11792
N_SHARD = 4
N_DEV = 8

C_DNZ = 3072
C_SBQ = 4096
C_SBZ = 7168
C_MQ = 8192
C_MZ = 8448
C_GATES = 8704
C_BA = 11776
W_R = 11904

ADAM_LR = 0.001
ADAM_B1 = 0.9
ADAM_B2 = 0.999
ADAM_EPS = 1e-08
ADAM_WD = 0.01
ADAM_STEP = 10

VMEM_LIMIT = 56 * 1024 * 1024

B_ROWS = 992
B_MEMKV, B_BRDN, B_BRSB, B_BRMEM, B_OUT, B_CONV = 0, 128, 384, 640, 704, 960
S_NORM, S_MEMNORM, S_FINAL, S_DNNORM, S_ALOG, S_DTB, S_LOSS = 0, 8, 16, 24, 25, 26, 27


def _cp(**kw):
    return pltpu.CompilerParams(vmem_limit_bytes=VMEM_LIMIT, **kw)


def _dot(a, b, dims):
    lead = a.ndim - 2
    ca, cb = {"nn": (1, 0), "nt": (1, 1), "tn": (0, 0)}[dims]
    batch = tuple(range(lead))
    return lax.dot_general(a, b, (((ca + lead,), (cb + lead,)), (batch, batch)), preferred_element_type=F32)


def _chunks(x):
    return x.reshape(x.shape[0] // DN_CHUNK, DN_CHUNK, x.shape[1])


def _unchunk(x):
    return x.reshape(x.shape[0] * x.shape[1], x.shape[2])


def _bdot(a, b, dims):
    return _dot(a.astype(BF16), b.astype(BF16), dims)


def _split(a):
    hi = a.astype(BF16)
    return hi, (a - hi.astype(F32)).astype(BF16)


def _dot3(a, b, dims):
    a1, a2 = _split(a)
    b1, b2 = _split(b)
    return _dot(a1, b1, dims) + (_dot(a1, b2, dims) + _dot(a2, b1, dims))


def _split_dot(a, ones_bf16):
    hi, lo = _split(a.reshape(-1, a.shape[-1]))
    out = _dot(hi, ones_bf16, "nn") + _dot(lo, ones_bf16, "nn")
    return out.reshape(a.shape[:-1] + (ones_bf16.shape[1],))


def _sigmoid(x):
    return 1.0 / (1.0 + jnp.exp(-x))


def _log1p_small(u):
    return jnp.where(u < 1e-2, u * (1.0 - u * (0.5 - u * (1.0 / 3.0))), jnp.log(1.0 + u))


def _log_sigmoid(z):
    return jnp.minimum(z, 0.0) - _log1p_small(jnp.exp(-jnp.abs(z)))


def _pick(dim, cands):
    for c in cands:
        if dim % c == 0:
            return c
    return dim


def _mm(a, b, dims, name, out_dtype=F32):
    ta, tb = dims[0] == "t", dims[1] == "t"
    m, k = (a.shape[1], a.shape[0]) if ta else a.shape
    n = b.shape[0] if tb else b.shape[1]
    tm = _pick(m, (1024, 512, 256))
    tn = _pick(n, (512, 384, 256, 128))
    tk = _pick(k, (1024, 512, 384, 256))
    nk = k // tk

    def body(a_ref, b_ref, o_ref, acc_ref):
        kk = pl.program_id(2)

        @pl.when(kk == 0)
        def _():
            acc_ref[...] = jnp.zeros_like(acc_ref)

        acc_ref[...] += _bdot(a_ref[...], b_ref[...], dims)

        @pl.when(kk == nk - 1)
        def _():
            o_ref[...] = acc_ref[...].astype(out_dtype)

    a_spec = pl.BlockSpec((tk, tm), lambda i, j, q: (q, i)) if ta else pl.BlockSpec((tm, tk), lambda i, j, q: (i, q))
    b_spec = pl.BlockSpec((tn, tk), lambda i, j, q: (j, q)) if tb else pl.BlockSpec((tk, tn), lambda i, j, q: (q, j))
    return pl.pallas_call(
        body, name=name, grid=(m // tm, n // tn, nk),
        in_specs=[a_spec, b_spec], out_specs=pl.BlockSpec((tm, tn), lambda i, j, q: (i, j)),
        out_shape=jax.ShapeDtypeStruct((m, n), out_dtype),
        scratch_shapes=[pltpu.VMEM((tm, tn), F32)],
        compiler_params=_cp(dimension_semantics=("parallel", "parallel", "arbitrary")),
    )(a, b)


def _rmsnorm_fwd(x, g, name):
    t, d = x.shape
    tb = _pick(t, (512, 256))

    def body(x_ref, g_ref, h_ref):
        xv = x_ref[...]
        r = lax.rsqrt(jnp.mean(xv * xv, axis=-1, keepdims=True) + NORM_EPS)
        h_ref[...] = ((xv * r) * g_ref[...]).astype(BF16)

    return pl.pallas_call(
        body, name=name, grid=(t // tb,),
        in_specs=[pl.BlockSpec((tb, d), lambda i: (i, 0)), pl.BlockSpec((1, d), lambda i: (0, 0))],
        out_specs=pl.BlockSpec((tb, d), lambda i: (i, 0)),
        out_shape=jax.ShapeDtypeStruct((t, d), BF16), compiler_params=_cp(),
    )(x, g)


def _rmsnorm_bwd(x, g, dh, resid, name):
    t, d = x.shape
    tb = _pick(t, (256,))

    def body(x_ref, g_ref, dh_ref, r_ref, dx_ref, dg_ref):
        @pl.when(pl.program_id(0) == 0)
        def _():
            dg_ref[...] = jnp.zeros_like(dg_ref)

        xv = x_ref[...]
        r = lax.rsqrt(jnp.mean(xv * xv, axis=-1, keepdims=True) + NORM_EPS)
        xhat = xv * r
        dhv = dh_ref[...]
        dg_ref[...] += jnp.sum(dhv * xhat, axis=0, keepdims=True)
        dxh = dhv * g_ref[...]
        dx_ref[...] = r_ref[...] + r * (dxh - xhat * jnp.mean(dxh * xhat, axis=-1, keepdims=True))

    row = pl.BlockSpec((tb, d), lambda i: (i, 0))
    vec = pl.BlockSpec((1, d), lambda i: (0, 0))
    return pl.pallas_call(
        body, name=name, grid=(t // tb,), in_specs=[row, vec, row, row], out_specs=[row, vec],
        out_shape=[jax.ShapeDtypeStruct((t, d), F32), jax.ShapeDtypeStruct((1, d), F32)], compiler_params=_cp(),
    )(x, g, dh, resid)


def _conv_silu(xv, w, row):
    y = xv * w[3:4, :]
    for s in (1, 2, 3):
        xs = jnp.where(row >= s, pltpu.roll(xv, s, 0), 0.0)
        y = y + xs * w[3 - s:4 - s, :]
    return y, y * _sigmoid(y)


def _dn_prep_fwd(proj, conv_w):
    t = proj.shape[0]

    def body(p_ref, w_ref, o_ref):
        j = pl.program_id(0)
        xv = p_ref[...]
        row = lax.broadcasted_iota(jnp.int32, xv.shape, 0)
        _, a = _conv_silu(xv, w_ref[...], row)
        inv = lax.rsqrt(jnp.sum(a * a, axis=-1, keepdims=True) + NORM_EPS)
        scale = jnp.where(j < N_HEADS, D_HEAD ** -0.5, 1.0)
        normed = jnp.where(j < 2 * N_HEADS, 1.0, 0.0)
        o_ref[...] = a * (normed * (inv * scale) + (1.0 - normed))

    return pl.pallas_call(
        body, name="dn_prep_fwd", grid=(3 * N_HEADS,),
        in_specs=[pl.BlockSpec((t, D_HEAD), lambda j: (0, j)), pl.BlockSpec((4, D_HEAD), lambda j: (0, j))],
        out_specs=pl.BlockSpec((t, D_HEAD), lambda j: (0, j)),
        out_shape=jax.ShapeDtypeStruct((t, 3 * D_MODEL), F32), compiler_params=_cp(),
    )(proj, conv_w)


def _dn_prep_bwd(proj, conv_w, dqkv):
    t = proj.shape[0]

    def body(p_ref, w_ref, d_ref, dp_ref, dw_ref):
        j = pl.program_id(0)
        xv = p_ref[...]
        w = w_ref[...]
        row = lax.broadcasted_iota(jnp.int32, xv.shape, 0)
        y, a = _conv_silu(xv, w, row)
        dn = d_ref[...]
        inv = lax.rsqrt(jnp.sum(a * a, axis=-1, keepdims=True) + NORM_EPS)
        scale = jnp.where(j < N_HEADS, D_HEAD ** -0.5, 1.0)
        ds = dn * scale
        da_norm = inv * ds - a * (inv * inv * inv) * jnp.sum(ds * a, axis=-1, keepdims=True)
        normed = jnp.where(j < 2 * N_HEADS, 1.0, 0.0)
        da = normed * da_norm + (1.0 - normed) * dn
        s = _sigmoid(y)
        dy = da * (s * (1.0 + y * (1.0 - s)))
        dx = dy * w[3:4, :]
        dw_ref[3:4, :] = jnp.sum(dy * xv, axis=0, keepdims=True)
        for sft in (1, 2, 3):
            xs = jnp.where(row >= sft, pltpu.roll(xv, sft, 0), 0.0)
            dw_ref[3 - sft:4 - sft, :] = jnp.sum(dy * xs, axis=0, keepdims=True)
            dys = jnp.where(row < t - sft, pltpu.roll(dy, t - sft, 0), 0.0)
            dx = dx + dys * w[3 - sft:4 - sft, :]
        dp_ref[...] = dx.astype(BF16)

    blk = pl.BlockSpec((t, D_HEAD), lambda j: (0, j))
    wblk = pl.BlockSpec((4, D_HEAD), lambda j: (0, j))
    return pl.pallas_call(
        body, name="dn_prep_bwd", grid=(3 * N_HEADS,), in_specs=[blk, wblk, blk], out_specs=[blk, wblk],
        out_shape=[jax.ShapeDtypeStruct((t, 3 * D_MODEL), BF16), jax.ShapeDtypeStruct((4, 3 * D_MODEL), F32)],
        compiler_params=_cp(),
    )(proj, conv_w, dqkv)


def _softplus_parts(xv):
    e = jnp.exp(-jnp.abs(xv))
    return jnp.maximum(xv, 0.0) + _log1p_small(e)


def _chunk_scan(v, row, reverse):
    t = v.shape[0]
    pos = row & (DN_CHUNK - 1)
    s = 1
    while s < DN_CHUNK:
        if reverse:
            v = v + jnp.where(pos < DN_CHUNK - s, pltpu.roll(v, t - s, 0), 0.0)
        else:
            v = v + jnp.where(pos >= s, pltpu.roll(v, s, 0), 0.0)
        s *= 2
    return v


def _dn_gate_fwd(proj, alog_row, dtb_row):
    t = proj.shape[0]

    def body(p_ref, al_ref, dt_ref, b_ref, g_ref):
        p = p_ref[...]
        row = lax.broadcasted_iota(jnp.int32, p.shape, 0)
        b_ref[...] = _sigmoid(p)
        g = -jnp.exp(al_ref[...]) * _softplus_parts(p + dt_ref[...])
        g_ref[...] = _chunk_scan(g, row, reverse=False)

    blk = pl.BlockSpec((t, 128), lambda i: (0, C_BA // 128))
    vec = pl.BlockSpec((1, 128), lambda i: (0, 0))
    out = pl.BlockSpec((t, 128), lambda i: (0, 0))
    return pl.pallas_call(
        body, name="dn_gate_fwd", grid=(1,), in_specs=[blk, vec, vec], out_specs=[out, out],
        out_shape=[jax.ShapeDtypeStruct((t, 128), F32)] * 2, compiler_params=_cp(),
    )(proj, alog_row, dtb_row)


def _dn_gate_bwd(proj, alog_row, dtb_row, dbeta, dgc):
    t = proj.shape[0]

    def body(p_ref, al_ref, dt_ref, db_ref, dg_ref, dp_ref, dal_ref, ddt_ref):
        p = p_ref[...]
        row = lax.broadcasted_iota(jnp.int32, p.shape, 0)
        lane = lax.broadcasted_iota(jnp.int32, p.shape, 1)
        s = _sigmoid(p)
        d_b = db_ref[...] * s * (1.0 - s)
        dg = _chunk_scan(dg_ref[...], row, reverse=True)
        xa = p + dt_ref[...]
        ea = jnp.exp(al_ref[...])
        g = -ea * _softplus_parts(xa)
        d_a = dg * (-ea) * _sigmoid(xa)
        dp_ref[...] = jnp.where(lane < N_HEADS, d_b, jnp.where(lane < 2 * N_HEADS, d_a, 0.0)).astype(BF16)
        dal_ref[...] = jnp.sum(dg * g, axis=0, keepdims=True)
        ddt_ref[...] = jnp.sum(d_a, axis=0, keepdims=True)

    blk = pl.BlockSpec((t, 128), lambda i: (0, C_BA // 128))
    vec = pl.BlockSpec((1, 128), lambda i: (0, 0))
    full = pl.BlockSpec((t, 128), lambda i: (0, 0))
    return pl.pallas_call(
        body, name="dn_gate_bwd", grid=(1,), in_specs=[blk, vec, vec, full, full], out_specs=[full, vec, vec],
        out_shape=[jax.ShapeDtypeStruct((t, 128), BF16), jax.ShapeDtypeStruct((1, 128), F32),
                   jax.ShapeDtypeStruct((1, 128), F32)], compiler_params=_cp(),
    )(proj, alog_row, dtb_row, dbeta, dgc)


def _col_to_row(col, eye):
    return jnp.sum(jnp.where(eye, col, 0.0), axis=-2, keepdims=True)


def _row_to_col(rowv, eye):
    return jnp.sum(jnp.where(eye, rowv, 0.0), axis=-1, keepdims=True)


def _tri_inverse(m, ri, ci):
    eye = (ri == ci).astype(F32)
    b16 = (ri >> 4) == (ci >> 4)
    b32 = (ri >> 5) == (ci >> 5)
    m1 = jnp.where(b16, m, 0.0)
    x = eye - m1
    p = _dot3(m1, m1, "nn")
    x = x + _dot3(x, p, "nn")
    p = _dot3(p, p, "nn")
    x = x + _dot3(x, p, "nn")
    p = _dot3(p, p, "nn")
    x = x + _dot3(x, p, "nn")
    c1 = jnp.where(jnp.logical_and(b32, jnp.logical_not(b16)), m, 0.0)
    x = x - _dot3(_dot3(x, c1, "nn"), x, "nn")
    c2 = jnp.where(b32, 0.0, m)
    x = x - _dot3(_dot3(x, c2, "nn"), x, "nn")
    return x


def _dn_chunk_common(q, k, gc, ri, ci):
    eye = ri == ci
    g_row = _col_to_row(gc, eye)
    diff = jnp.minimum(gc - g_row, 0.0)
    gam = jnp.where(ri >= ci, jnp.exp(diff), 0.0)
    kk = _bdot(k, k, "nt")
    qk = _bdot(q, k, "nt")
    rcol = lax.broadcasted_iota(jnp.int32, gc.shape, gc.ndim - 2)
    last = jnp.sum(jnp.where(rcol == DN_CHUNK - 1, gc, 0.0), axis=-2, keepdims=True)
    e_g = jnp.exp(gc)
    dec = jnp.exp(last - gc)
    return eye, gam, kk, qk, last, e_g, dec, rcol


def _dn_specs(t, rows_blk):
    def head(off):
        return pl.BlockSpec((rows_blk, D_HEAD), lambda h, g: (g, off + h))

    col = pl.BlockSpec((1, rows_blk, 1), lambda h, g: (h, g, 0))
    sq = pl.BlockSpec((1, rows_blk, DN_CHUNK), lambda h, g: (h, g, 0))
    tile = pl.BlockSpec((1, rows_blk // DN_CHUNK, 8, 128), lambda h, g: (h, g, 0, 0))
    return head, col, sq, tile


def _dn_intra_fwd(qkv, beta8, g8):
    t = qkv.shape[0]
    n_chunks = t // DN_CHUNK
    rows_blk = DN_GROUP * DN_CHUNK

    def body(q_ref, k_ref, v_ref, b_ref, g_ref, u_ref, w_ref, qd_ref, kd_ref, a_ref, ti_ref, el_ref):
        ri = lax.broadcasted_iota(jnp.int32, (DN_CHUNK, DN_CHUNK), 0)
        ci = lax.broadcasted_iota(jnp.int32, (DN_CHUNK, DN_CHUNK), 1)
        q, k, v = (_chunks(r[...]) for r in (q_ref, k_ref, v_ref))
        b, gc = _chunks(b_ref[0]), _chunks(g_ref[0])
        _, gam, kk, qk, last, e_g, dec, _ = _dn_chunk_common(q, k, gc, ri, ci)
        tinv = _tri_inverse(jnp.where(ri > ci, b * kk * gam, 0.0), ri, ci)
        u_ref[...] = _unchunk(_bdot(tinv, v * b, "nn"))
        w_ref[...] = _unchunk(_bdot(tinv, k * (b * e_g), "nn"))
        qd_ref[...] = _unchunk(q * e_g)
        kd_ref[...] = _unchunk(k * dec)
        a_ref[0] = _unchunk(qk * gam)
        ti_ref[0] = _unchunk(tinv)
        el_ref[0] = jnp.broadcast_to(jnp.exp(last), (DN_GROUP, 8, 128))

    head, col, sq, tile = _dn_specs(t, rows_blk)
    act = jax.ShapeDtypeStruct((t, D_MODEL), F32)
    sqs = jax.ShapeDtypeStruct((N_HEADS, t, DN_CHUNK), F32)
    return pl.pallas_call(
        body, name="dn_intra_fwd", grid=(N_HEADS, t // rows_blk),
        in_specs=[head(0), head(N_HEADS), head(2 * N_HEADS), col, col],
        out_specs=[head(0)] * 4 + [sq, sq, tile],
        out_shape=[act] * 4 + [sqs, sqs, jax.ShapeDtypeStruct((N_HEADS, n_chunks, 8, 128), F32)],
        compiler_params=_cp(),
    )(qkv, qkv, qkv, beta8, g8)


def _dn_scan_specs(t, n_chunks):
    hp = DN_HEADS_PER_STEP
    act = pl.BlockSpec((t, hp * D_HEAD), lambda h: (0, h))
    sq = pl.BlockSpec((hp, t, DN_CHUNK), lambda h: (h, 0, 0))
    state = pl.BlockSpec((hp, n_chunks, D_HEAD, D_HEAD), lambda h: (h, 0, 0, 0))
    tile = pl.BlockSpec((hp, n_chunks, 8, 128), lambda h: (h, 0, 0, 0))
    return act, sq, state, tile


def _dn_scan_fwd(u, w, qd, kd, a, el):
    t = u.shape[0]
    n_chunks = t // DN_CHUNK
    hp = DN_HEADS_PER_STEP

    def body(u_ref, w_ref, qd_ref, kd_ref, a_ref, el_ref, o_ref, vn_ref, s_ref, s_scr):
        s_scr[...] = jnp.zeros_like(s_scr)

        def chunk(n, carry):
            rows = pl.ds(pl.multiple_of(n * DN_CHUNK, DN_CHUNK), DN_CHUNK)
            for hh in range(hp):
                hs = slice(hh * D_HEAD, (hh + 1) * D_HEAD)
                s = s_scr[hh]
                s_ref[hh, n] = s
                v_new = u_ref[rows, hs] - _bdot(w_ref[rows, hs], s, "nn")
                vn_ref[rows, hs] = v_new
                o_ref[rows, hs] = _bdot(qd_ref[rows, hs], s, "nn") + _bdot(a_ref[hh, rows, :], v_new, "nn")
                s_scr[hh] = s * el_ref[hh, n][0:1, :] + _bdot(kd_ref[rows, hs], v_new, "tn")
            return carry

        lax.fori_loop(0, n_chunks, chunk, 0)

    act, sq, state, tile = _dn_scan_specs(t, n_chunks)
    shp = jax.ShapeDtypeStruct((t, D_MODEL), F32)
    return pl.pallas_call(
        body, name="dn_scan_fwd", grid=(N_HEADS // hp,),
        in_specs=[act, act, act, act, sq, tile], out_specs=[act, act, state],
        out_shape=[shp, shp, jax.ShapeDtypeStruct((N_HEADS, n_chunks, D_HEAD, D_HEAD), F32)],
        scratch_shapes=[pltpu.VMEM((hp, D_HEAD, D_HEAD), F32)], compiler_params=_cp(),
    )(u, w, qd, kd, a, el)


def _dn_scan_bwd(w, qd, kd, a, el, vn, s_all, do):
    t = w.shape[0]
    n_chunks = t // DN_CHUNK
    hp = DN_HEADS_PER_STEP

    def body(w_ref, qd_ref, kd_ref, a_ref, el_ref, vn_ref, s_ref, do_ref, dvn_ref, dkd_ref, dqd_ref, dw_ref, dl_ref, ds_scr):
        ds_scr[...] = jnp.zeros_like(ds_scr)

        def chunk(i, carry):
            n = n_chunks - 1 - i
            rows = pl.ds(pl.multiple_of(n * DN_CHUNK, DN_CHUNK), DN_CHUNK)
            for hh in range(hp):
                hs = slice(hh * D_HEAD, (hh + 1) * D_HEAD)
                s = s_ref[hh, n]
                d_s = ds_scr[hh]
                e_last = el_ref[hh, n][0:1, :]
                d_o = do_ref[rows, hs]
                dv_new = _bdot(a_ref[hh, rows, :], d_o, "tn") + _bdot(kd_ref[rows, hs], d_s, "nn")
                ds_scr[hh] = d_s * e_last + _bdot(qd_ref[rows, hs], d_o, "tn") - _bdot(w_ref[rows, hs], dv_new, "tn")
                dvn_ref[rows, hs] = dv_new
                dkd_ref[rows, hs] = _bdot(vn_ref[rows, hs], d_s, "nt")
                dqd_ref[rows, hs] = _bdot(d_o, s, "nt")
                dw_ref[rows, hs] = -_bdot(dv_new, s, "nt")
                dlast = jnp.sum(jnp.sum(d_s * s, axis=1, keepdims=True), axis=0, keepdims=True)
                dl_ref[hh, n] = jnp.broadcast_to(dlast * e_last, (8, 128))
            return carry

        lax.fori_loop(0, n_chunks, chunk, 0)

    act, sq, state, tile = _dn_scan_specs(t, n_chunks)
    shp = jax.ShapeDtypeStruct((t, D_MODEL), F32)
    return pl.pallas_call(
        body, name="dn_scan_bwd", grid=(N_HEADS // hp,),
        in_specs=[act, act, act, sq, tile, act, state, act], out_specs=[act] * 4 + [tile],
        out_shape=[shp] * 4 + [jax.ShapeDtypeStruct((N_HEADS, n_chunks, 8, 128), F32)],
        scratch_shapes=[pltpu.VMEM((hp, D_HEAD, D_HEAD), F32)], compiler_params=_cp(),
    )(w, qd, kd, a, el, vn, s_all, do)


def _dn_intra_bwd(qkv, beta8, g8, tinv_all, vn, do, dvn, dkd, dqd, dw, dl):
    t = qkv.shape[0]
    rows_blk = DN_GROUP * DN_CHUNK

    def body(q_ref, k_ref, v_ref, b_ref, g_ref, ti_ref, vn_ref, do_ref, dvn_ref, dkd_ref, dqd_ref, dw_ref, dl_ref,
             dq_ref, dk_ref, dv_ref, db_ref, dg_ref):
        ri = lax.broadcasted_iota(jnp.int32, (DN_CHUNK, DN_CHUNK), 0)
        ci = lax.broadcasted_iota(jnp.int32, (DN_CHUNK, DN_CHUNK), 1)
        q, k, v = (_chunks(r[...]) for r in (q_ref, k_ref, v_ref))
        b, gc = _chunks(b_ref[0]), _chunks(g_ref[0])
        tinv = _chunks(ti_ref[0])
        dv_new, dk_dec, dq_dec, d_w = (_chunks(r[...]) for r in (dvn_ref, dkd_ref, dqd_ref, dw_ref))
        eye, gam, kk, qk, _, e_g, dec, rcol = _dn_chunk_common(q, k, gc, ri, ci)
        bv = v * b
        bk = k * (b * e_g)

        d_a = jnp.where(ri >= ci, _bdot(_chunks(do_ref[...]), _chunks(vn_ref[...]), "nt"), 0.0)
        dbv = _bdot(tinv, dv_new, "tn")
        dbk = _bdot(tinv, d_w, "tn")
        d_tinv = _bdot(dv_new, bv, "nt") + _bdot(d_w, bk, "nt")
        d_m = -jnp.where(ri > ci, _dot3(_dot3(tinv, d_tinv, "tn"), tinv, "nt"), 0.0)

        d_kk = d_m * b * gam
        d_gam = d_m * b * kk + d_a * qk
        d_qk = d_a * gam
        dq_ref[...] = _unchunk(_bdot(d_qk, k, "nn") + dq_dec * e_g)
        dk_ref[...] = _unchunk(_bdot(d_qk, q, "tn") + _bdot(d_kk, k, "nn") + _bdot(d_kk, k, "tn")
                               + dk_dec * dec + dbk * (b * e_g))
        dv_ref[...] = _unchunk(dbv * b)
        db_ref[0] = _unchunk(jnp.sum(d_m * kk * gam, axis=-1, keepdims=True) + jnp.sum(dbv * v, axis=-1, keepdims=True)
                             + jnp.sum(dbk * k, axis=-1, keepdims=True) * e_g)

        xg = d_gam * gam
        kdk = jnp.sum(dk_dec * (k * dec), axis=-1, keepdims=True)
        d_gc = (jnp.sum(xg, axis=-1, keepdims=True) - _row_to_col(jnp.sum(xg, axis=-2, keepdims=True), eye)
                + jnp.sum(dq_dec * (q * e_g), axis=-1, keepdims=True) - kdk
                + jnp.sum(dbk * bk, axis=-1, keepdims=True))
        d_last_total = dl_ref[0][:, 0:1, 0:1] + jnp.sum(kdk, axis=-2, keepdims=True)
        dg_ref[0] = _unchunk(d_gc + jnp.where(rcol == DN_CHUNK - 1, d_last_total, 0.0))

    head, col, sq, tile = _dn_specs(t, rows_blk)
    return pl.pallas_call(
        body, name="dn_intra_bwd", grid=(N_HEADS, t // rows_blk),
        in_specs=[head(0), head(N_HEADS), head(2 * N_HEADS), col, col, sq] + [head(0)] * 6 + [tile],
        out_specs=[head(0), head(0), head(0), col, col],
        out_shape=[jax.ShapeDtypeStruct((t, D_MODEL), F32)] * 3 + [jax.ShapeDtypeStruct((N_HEADS, t, 1), F32)] * 2,
        compiler_params=_cp(),
    )(qkv, qkv, qkv, beta8, g8, tinv_all, vn, do, dvn, dkd, dqd, dw, dl)


def _dn_post_fwd(o, proj, gn):
    t = o.shape[0]

    def body(o_ref, z_ref, g_ref, out_ref):
        ov, z = o_ref[...], z_ref[...]
        r = lax.rsqrt(jnp.mean(ov * ov, axis=-1, keepdims=True) + NORM_EPS)
        out_ref[...] = (((ov * r) * g_ref[...]) * (z * _sigmoid(z))).astype(BF16)

    blk = pl.BlockSpec((t, D_HEAD), lambda h: (0, h))
    return pl.pallas_call(
        body, name="dn_post_fwd", grid=(N_HEADS,),
        in_specs=[blk, pl.BlockSpec((t, D_HEAD), lambda h: (0, C_DNZ // D_HEAD + h)),
                  pl.BlockSpec((1, D_HEAD), lambda h: (0, 0))],
        out_specs=blk, out_shape=jax.ShapeDtypeStruct((t, D_MODEL), BF16), compiler_params=_cp(),
    )(o, proj, gn)


def _dn_post_bwd(o, proj, gn, dout):
    t = o.shape[0]

    def body(o_ref, z_ref, g_ref, d_ref, do_ref, dz_ref, dg_ref):
        @pl.when(pl.program_id(0) == 0)
        def _():
            dg_ref[...] = jnp.zeros_like(dg_ref)

        ov, z, d = o_ref[...], z_ref[...], d_ref[...]
        r = lax.rsqrt(jnp.mean(ov * ov, axis=-1, keepdims=True) + NORM_EPS)
        ohat = ov * r
        s = _sigmoid(z)
        d_on = d * (z * s)
        dz_ref[...] = (d * (ohat * g_ref[...]) * (s * (1.0 + z * (1.0 - s)))).astype(BF16)
        dg_ref[...] += jnp.sum(d_on * ohat, axis=0, keepdims=True)
        dxh = d_on * g_ref[...]
        do_ref[...] = r * (dxh - ohat * jnp.mean(dxh * ohat, axis=-1, keepdims=True))

    blk = pl.BlockSpec((t, D_HEAD), lambda h: (0, h))
    vec = pl.BlockSpec((1, D_HEAD), lambda h: (0, 0))
    return pl.pallas_call(
        body, name="dn_post_bwd", grid=(N_HEADS,),
        in_specs=[blk, pl.BlockSpec((t, D_HEAD), lambda h: (0, C_DNZ // D_HEAD + h)), vec, blk],
        out_specs=[blk, blk, vec],
        out_shape=[jax.ShapeDtypeStruct((t, D_MODEL), F32), jax.ShapeDtypeStruct((t, D_MODEL), BF16),
                   jax.ShapeDtypeStruct((1, D_HEAD), F32)], compiler_params=_cp(),
    )(o, proj, gn, dout)


def _sb_fwd(proj):
    t = proj.shape[0]
    qblk = min(SB_QBLOCK, t)
    scale = 1.0 / math.sqrt(D_HEAD)

    hp = SB_HEADS_PER_STEP
    wid = hp * D_HEAD

    def body(q_ref, k_ref, v_ref, z_ref, o_ref, og_ref, l_ref, qb, kb, vb):
        for hh in range(hp):
            hs = slice(hh * D_HEAD, (hh + 1) * D_HEAD)
            qb[hh] = q_ref[:, hs].astype(BF16)
            kb[hh] = k_ref[:, hs].astype(BF16)
            vb[hh] = v_ref[:, hs].astype(BF16)
        ri = lax.broadcasted_iota(jnp.int32, (qblk, SB_BLOCK), 0)
        ci = lax.broadcasted_iota(jnp.int32, (qblk, SB_BLOCK), 1)
        r2 = lax.broadcasted_iota(jnp.int32, (SB_BLOCK, SB_BLOCK), 0)
        c2 = lax.broadcasted_iota(jnp.int32, (SB_BLOCK, SB_BLOCK), 1)
        upper = (r2 > c2).astype(BF16)
        nkb = qblk // SB_BLOCK

        def qblock(i, carry):
            rows = pl.ds(pl.multiple_of(i * qblk, qblk), qblk)
            qi = qb[:, rows, :]

            def kblock(jj, st):
                acc, c = st
                j = (i + 1) * nkb - 1 - jj
                cols = pl.ds(pl.multiple_of(j * SB_BLOCK, SB_BLOCK), SB_BLOCK)
                mask = (j * SB_BLOCK + ci) < (i * qblk + ri)
                z = _dot(qi, kb[:, cols, :], "nt") * scale
                lb = jnp.minimum(z, 0.0) - jnp.log(1.0 + jnp.exp(-jnp.abs(z)))
                lf = jnp.where(mask, lb - z, 0.0)
                surv = _split_dot(lf, upper) + c
                att = jnp.where(mask, jnp.exp(lb + surv), 0.0)
                acc = acc + _dot(att.astype(BF16), vb[:, cols, :], "nn")
                return acc, c + jnp.sum(lf, axis=-1, keepdims=True)

            init = (jnp.zeros((hp, qblk, D_HEAD), F32), jnp.zeros((hp, qblk, 1), F32))
            acc, c = lax.fori_loop(0, (i + 1) * nkb, kblock, init)
            l_ref[:, rows, :] = c
            for hh in range(hp):
                hs = slice(hh * D_HEAD, (hh + 1) * D_HEAD)
                zg = z_ref[rows, hs]
                o_ref[rows, hs] = acc[hh]
                og_ref[rows, hs] = (acc[hh] * (zg * _sigmoid(zg))).astype(BF16)
            return carry

        lax.fori_loop(0, t // qblk, qblock, 0)

    def head(off):
        return pl.BlockSpec((t, wid), lambda h: (0, off // wid + h))

    out = pl.BlockSpec((t, wid), lambda h: (0, h))
    return pl.pallas_call(
        body, name="sb_fwd", grid=(N_HEADS // hp,),
        in_specs=[head(C_SBQ), head(C_SBQ + D_MODEL), head(C_SBQ + 2 * D_MODEL), head(C_SBZ)],
        out_specs=[out, out, pl.BlockSpec((hp, t, 1), lambda h: (h, 0, 0))],
        out_shape=[jax.ShapeDtypeStruct((t, D_MODEL), F32), jax.ShapeDtypeStruct((t, D_MODEL), BF16),
                   jax.ShapeDtypeStruct((N_HEADS, t, 1), F32)],
        scratch_shapes=[pltpu.VMEM((hp, t, D_HEAD), BF16)] * 3, compiler_params=_cp(),
    )(proj, proj, proj, proj)


def _sb_bwd(proj, o, ltot, dog):
    t = proj.shape[0]
    qblk = min(SB_QBLOCK, t)
    scale = 1.0 / math.sqrt(D_HEAD)

    hp = SB_HEADS_PER_STEP
    wid = hp * D_HEAD

    def body(q_ref, k_ref, v_ref, z_ref, o_ref, l_ref, d_ref, dq_ref, dk_ref, dv_ref, dz_ref,
             qb, kb, vb, dob, dk_scr, dv_scr):
        for hh in range(hp):
            hs = slice(hh * D_HEAD, (hh + 1) * D_HEAD)
            qb[hh] = q_ref[:, hs].astype(BF16)
            kb[hh] = k_ref[:, hs].astype(BF16)
            vb[hh] = v_ref[:, hs].astype(BF16)
            zg = z_ref[:, hs]
            sg = _sigmoid(zg)
            dgo = d_ref[:, hs]
            dob[hh] = (dgo * (zg * sg)).astype(BF16)
            dz_ref[:, hs] = (dgo * o_ref[:, hs] * (sg * (1.0 + zg * (1.0 - sg)))).astype(BF16)
        dk_scr[...] = jnp.zeros_like(dk_scr)
        dv_scr[...] = jnp.zeros_like(dv_scr)
        ri = lax.broadcasted_iota(jnp.int32, (qblk, SB_BLOCK), 0)
        ci = lax.broadcasted_iota(jnp.int32, (qblk, SB_BLOCK), 1)
        r2 = lax.broadcasted_iota(jnp.int32, (SB_BLOCK, SB_BLOCK), 0)
        c2 = lax.broadcasted_iota(jnp.int32, (SB_BLOCK, SB_BLOCK), 1)
        incl = (r2 <= c2).astype(BF16)
        below = (r2 < c2).astype(BF16)

        def qblock(i, carry):
            rows = pl.ds(pl.multiple_of(i * qblk, qblk), qblk)
            qi = qb[:, rows, :]
            d_o = dob[:, rows, :]
            ltot = l_ref[:, rows, :]

            def kblock(j, st):
                dq, cpre, ce = st
                cols = pl.ds(pl.multiple_of(j * SB_BLOCK, SB_BLOCK), SB_BLOCK)
                mask = (j * SB_BLOCK + ci) < (i * qblk + ri)
                kj, vj = kb[:, cols, :], vb[:, cols, :]
                z = _dot(qi, kj, "nt") * scale
                e1 = jnp.exp(-jnp.abs(z))
                den = 1.0 + e1
                lb = jnp.minimum(z, 0.0) - jnp.log(den)
                r = 1.0 / den
                sig = jnp.where(z >= 0.0, r, e1 * r)
                nsig = jnp.where(z >= 0.0, e1 * r, r)
                lf = jnp.where(mask, lb - z, 0.0)
                surv = ltot - (cpre + _split_dot(lf, incl))
                att = jnp.where(mask, jnp.exp(lb + surv), 0.0)
                e = _dot(d_o, vj, "nt") * att
                dlf = ce + _split_dot(e, below)
                dzz = jnp.where(mask, e * nsig - dlf * sig, 0.0).astype(BF16)
                dq = dq + _dot(dzz, kj, "nn")
                dk_scr[:, cols, :] += _dot(dzz, qi, "tn")
                dv_scr[:, cols, :] += _dot(att.astype(BF16), d_o, "tn")
                return dq, cpre + jnp.sum(lf, axis=-1, keepdims=True), ce + jnp.sum(e, axis=-1, keepdims=True)

            zero_col = jnp.zeros((hp, qblk, 1), F32)
            init = (jnp.zeros((hp, qblk, D_HEAD), F32), zero_col, zero_col)
            dq, _, _ = lax.fori_loop(0, (i + 1) * (qblk // SB_BLOCK), kblock, init)
            for hh in range(hp):
                dq_ref[rows, hh * D_HEAD:(hh + 1) * D_HEAD] = (dq[hh] * scale).astype(BF16)
            return carry

        lax.fori_loop(0, t // qblk, qblock, 0)
        for hh in range(hp):
            hs = slice(hh * D_HEAD, (hh + 1) * D_HEAD)
            dk_ref[:, hs] = (dk_scr[hh] * scale).astype(BF16)
            dv_ref[:, hs] = dv_scr[hh].astype(BF16)

    def head(off):
        return pl.BlockSpec((t, wid), lambda h: (0, off // wid + h))

    return pl.pallas_call(
        body, name="sb_bwd", grid=(N_HEADS // hp,),
        in_specs=[head(C_SBQ), head(C_SBQ + D_MODEL), head(C_SBQ + 2 * D_MODEL), head(C_SBZ), head(0),
                  pl.BlockSpec((hp, t, 1), lambda h: (h, 0, 0)), head(0)],
        out_specs=[head(0)] * 4, out_shape=[jax.ShapeDtypeStruct((t, D_MODEL), BF16)] * 4,
        scratch_shapes=[pltpu.VMEM((hp, t, D_HEAD), BF16)] * 4 + [pltpu.VMEM((hp, t, D_HEAD), F32)] * 2,
        compiler_params=_cp(),
    )(proj, proj, proj, proj, o, ltot, dog)


def _mem_fwd(proj, mkv):
    t = proj.shape[0]
    tq = _pick(t, (512, 256))
    m_len = mkv.shape[0]
    scale = 1.0 / math.sqrt(MEM_DH)

    def body(q_ref, z_ref, kv_ref, o_ref, og_ref):
        q = q_ref[...]
        mk = kv_ref[:, :MEM_W].astype(BF16)
        mv = kv_ref[:, MEM_W:].astype(BF16)
        lane = lax.broadcasted_iota(jnp.int32, q.shape, 1) >> 6
        o = jnp.zeros(q.shape, F32)
        for h in range(MEM_HEADS):
            s = _bdot(jnp.where(lane == h, q, 0.0), mk, "nt") * scale
            p = jnp.exp(s - jnp.max(s, axis=-1, keepdims=True))
            p = p / jnp.sum(p, axis=-1, keepdims=True)
            o = o + jnp.where(lane == h, _bdot(p, mv, "nn"), 0.0)
        z = z_ref[...]
        o_ref[...] = o
        og_ref[...] = (o * (z * _sigmoid(z))).astype(BF16)

    out = pl.BlockSpec((tq, MEM_W), lambda i: (i, 0))
    return pl.pallas_call(
        body, name="mem_fwd", grid=(t // tq,),
        in_specs=[pl.BlockSpec((tq, MEM_W), lambda i: (i, C_MQ // MEM_W)),
                  pl.BlockSpec((tq, MEM_W), lambda i: (i, C_MZ // MEM_W)),
                  pl.BlockSpec((m_len, 2 * MEM_W), lambda i: (0, 0))],
        out_specs=[out, out],
        out_shape=[jax.ShapeDtypeStruct((t, MEM_W), F32), jax.ShapeDtypeStruct((t, MEM_W), BF16)],
        compiler_params=_cp(),
    )(proj, proj, mkv)


def _mem_bwd(proj, mkv, o, dog):
    t = proj.shape[0]
    tq = _pick(t, (512, 256))
    m_len = mkv.shape[0]
    scale = 1.0 / math.sqrt(MEM_DH)

    def body(q_ref, z_ref, kv_ref, o_ref, d_ref, dq_ref, dz_ref, dkv_ref):
        @pl.when(pl.program_id(0) == 0)
        def _():
            dkv_ref[...] = jnp.zeros_like(dkv_ref)

        q = q_ref[...]
        z = z_ref[...]
        sg = _sigmoid(z)
        dgo = d_ref[...]
        d_o = dgo * (z * sg)
        dz_ref[...] = (dgo * o_ref[...] * (sg * (1.0 + z * (1.0 - sg)))).astype(BF16)
        mk = kv_ref[:, :MEM_W].astype(BF16)
        mv = kv_ref[:, MEM_W:].astype(BF16)
        lane = lax.broadcasted_iota(jnp.int32, q.shape, 1) >> 6
        klane = lax.broadcasted_iota(jnp.int32, (m_len, MEM_W), 1) >> 6
        dq = jnp.zeros(q.shape, F32)
        dmk = jnp.zeros((m_len, MEM_W), F32)
        dmv = jnp.zeros((m_len, MEM_W), F32)
        for h in range(MEM_HEADS):
            qh = jnp.where(lane == h, q, 0.0)
            doh = jnp.where(lane == h, d_o, 0.0)
            s = _bdot(qh, mk, "nt") * scale
            p = jnp.exp(s - jnp.max(s, axis=-1, keepdims=True))
            p = p / jnp.sum(p, axis=-1, keepdims=True)
            dp = _bdot(doh, mv, "nt")
            ds = p * (dp - jnp.sum(dp * p, axis=-1, keepdims=True)) * scale
            dq = dq + jnp.where(lane == h, _bdot(ds, mk, "nn"), 0.0)
            dmk = dmk + jnp.where(klane == h, _bdot(ds, qh, "tn"), 0.0)
            dmv = dmv + jnp.where(klane == h, _bdot(p, doh, "tn"), 0.0)
        dq_ref[...] = dq.astype(BF16)
        dkv_ref[:, :MEM_W] += dmk
        dkv_ref[:, MEM_W:] += dmv

    blk = pl.BlockSpec((tq, MEM_W), lambda i: (i, 0))
    kv = pl.BlockSpec((m_len, 2 * MEM_W), lambda i: (0, 0))
    return pl.pallas_call(
        body, name="mem_bwd", grid=(t // tq,),
        in_specs=[pl.BlockSpec((tq, MEM_W), lambda i: (i, C_MQ // MEM_W)),
                  pl.BlockSpec((tq, MEM_W), lambda i: (i, C_MZ // MEM_W)), kv, blk, blk],
        out_specs=[blk, blk, kv],
        out_shape=[jax.ShapeDtypeStruct((t, MEM_W), BF16), jax.ShapeDtypeStruct((t, MEM_W), BF16),
                   jax.ShapeDtypeStruct((m_len, 2 * MEM_W), F32)], compiler_params=_cp(),
    )(proj, proj, mkv, o, dog)


_GW = 512


def _merge_fwd(proj, y_dn, y_sb, y_m):
    t = proj.shape[0]
    tb = _pick(t, (256,))
    nc = D_MODEL // _GW

    def body(g1, g2, g3, y1, y2, y3, out_ref):
        out_ref[...] = (_sigmoid(g1[...]) * y1[...] + _sigmoid(g2[...]) * y2[...] + _sigmoid(g3[...]) * y3[...]).astype(BF16)

    def gate(kb):
        return pl.BlockSpec((tb, _GW), lambda i, c: (i, C_GATES // _GW + kb * nc + c))

    blk = pl.BlockSpec((tb, _GW), lambda i, c: (i, c))
    return pl.pallas_call(
        body, name="merge_fwd", grid=(t // tb, nc), in_specs=[gate(0), gate(1), gate(2), blk, blk, blk],
        out_specs=blk, out_shape=jax.ShapeDtypeStruct((t, D_MODEL), BF16), compiler_params=_cp(),
    )(proj, proj, proj, y_dn, y_sb, y_m)


def _merge_bwd(proj, y_dn, y_sb, y_m, dm):
    t = proj.shape[0]
    tb = _pick(t, (256,))
    nc = D_MODEL // _GW

    def body(g1, g2, g3, y1, y2, y3, dm_ref, d1, d2, d3, dg1, dg2, dg3):
        d = dm_ref[...]
        for g, y, dy, dg in ((g1, y1, d1, dg1), (g2, y2, d2, dg2), (g3, y3, d3, dg3)):
            s = _sigmoid(g[...])
            dy[...] = (d * s).astype(BF16)
            dg[...] = (d * y[...] * (s * (1.0 - s))).astype(BF16)

    def gate(kb):
        return pl.BlockSpec((tb, _GW), lambda i, c: (i, C_GATES // _GW + kb * nc + c))

    blk = pl.BlockSpec((tb, _GW), lambda i, c: (i, c))
    act = jax.ShapeDtypeStruct((t, D_MODEL), BF16)
    return pl.pallas_call(
        body, name="merge_bwd", grid=(t // tb, nc), in_specs=[gate(0), gate(1), gate(2), blk, blk, blk, blk],
        out_specs=[blk] * 6, out_shape=[act] * 6, compiler_params=_cp(),
    )(proj, proj, proj, y_dn, y_sb, y_m, dm)


def _final_loss(x, mo, g, tgt):
    t, d = x.shape
    tb = _pick(t, (256,))

    def body(x_ref, mo_ref, g_ref, t_ref, do_ref, dob_ref, loss_ref, dg_ref):
        @pl.when(pl.program_id(0) == 0)
        def _():
            loss_ref[...] = jnp.zeros_like(loss_ref)
            dg_ref[...] = jnp.zeros_like(dg_ref)

        out = x_ref[...] + mo_ref[...]
        r = lax.rsqrt(jnp.mean(out * out, axis=-1, keepdims=True) + NORM_EPS)
        xhat = out * r
        gv = g_ref[...]
        err = xhat * gv - t_ref[...]
        per_tok = jnp.mean(err * err, axis=-1, keepdims=True)
        loss_ref[...] += 0.5 * jnp.sum(per_tok, axis=0, keepdims=True)
        dy = err * (1.0 / d)
        dg_ref[...] += jnp.sum(dy * xhat, axis=0, keepdims=True)
        dxh = dy * gv
        dout = r * (dxh - xhat * jnp.mean(dxh * xhat, axis=-1, keepdims=True))
        do_ref[...] = dout
        dob_ref[...] = dout.astype(BF16)

    row = pl.BlockSpec((tb, d), lambda i: (i, 0))
    vec = pl.BlockSpec((1, d), lambda i: (0, 0))
    return pl.pallas_call(
        body, name="final_loss", grid=(t // tb,), in_specs=[row, row, vec, row],
        out_specs=[row, row, pl.BlockSpec((1, 128), lambda i: (0, 0)), vec],
        out_shape=[jax.ShapeDtypeStruct((t, d), F32), jax.ShapeDtypeStruct((t, d), BF16),
                   jax.ShapeDtypeStruct((1, 128), F32), jax.ShapeDtypeStruct((1, d), F32)],
        compiler_params=_cp(),
    )(x, mo, g, tgt)


def _rows_call(body, name, ins, n_out, out_dtypes):
    r, c = ins[0].shape[-2:]
    tb = _pick(r, (128, 496))
    specs = []
    for a in ins:
        if a.ndim == 3:
            specs.append(pl.BlockSpec((a.shape[0], tb, c), lambda i: (0, i, 0)))
        else:
            specs.append(pl.BlockSpec((tb, c), lambda i: (i, 0)))
    out_blk = pl.BlockSpec((tb, c), lambda i: (i, 0))
    return pl.pallas_call(
        body, name=name, grid=(r // tb,), in_specs=specs, out_specs=[out_blk] * n_out,
        out_shape=[jax.ShapeDtypeStruct((r, c), dt) for dt in out_dtypes], compiler_params=_cp(),
    )(*ins)


def _cast_bf16(a, name):
    def body(a_ref, o_ref):
        o_ref[...] = a_ref[...].astype(BF16)

    return _rows_call(body, name, [a], 1, [BF16])[0]


def _add_pair(a, b, name):
    n, r, c = a.shape
    tb = _pick(r, (64, 248))

    def body(a_ref, b_ref, o_ref):
        o_ref[...] = a_ref[...] + b_ref[...]

    blk = pl.BlockSpec((n, tb, c), lambda i: (0, i, 0))
    return pl.pallas_call(body, name=name, grid=(r // tb,), in_specs=[blk, blk], out_specs=blk,
                          out_shape=jax.ShapeDtypeStruct(a.shape, F32), compiler_params=_cp())(a, b)


def _sum_lead(a, name):
    def body(a_ref, o_ref):
        acc = a_ref[0]
        for i in range(1, a.shape[0]):
            acc = acc + a_ref[i]
        o_ref[...] = acc

    return _rows_call(body, name, [a], 1, [F32])[0]


def _adamw_math(w, g, m, v):
    m = ADAM_B1 * m + (1.0 - ADAM_B1) * g
    v = ADAM_B2 * v + (1.0 - ADAM_B2) * (g * g)
    m_hat = m / (1.0 - ADAM_B1 ** ADAM_STEP)
    v_hat = v / (1.0 - ADAM_B2 ** ADAM_STEP)
    delta = -ADAM_LR * (m_hat / (jnp.sqrt(v_hat) + ADAM_EPS) + ADAM_WD * w)
    return delta, m, v


def _adamw(w, g, m, v, name):
    def body(w_ref, g_ref, m_ref, v_ref, d_ref, mo_ref, vo_ref):
        d, mn, vn = _adamw_math(w_ref[...], g_ref[...], m_ref[...], v_ref[...])
        d_ref[...] = d
        mo_ref[...] = mn
        vo_ref[...] = vn

    return _rows_call(body, name, [w, g, m, v], 3, [F32, F32, F32])


def _small_update(gathered, w, m, v):
    def body(p_ref, w_ref, m_ref, v_ref, g_ref, d_ref, mo_ref, vo_ref):
        g = p_ref[0]
        for i in range(1, N_DEV):
            g = g + p_ref[i]
        d, mn, vn = _adamw_math(w_ref[...], g, m_ref[...], v_ref[...])
        g_ref[...] = g
        d_ref[...] = d
        mo_ref[...] = mn
        vo_ref[...] = vn

    full = pl.BlockSpec((32, 128), lambda i: (0, 0))
    return pl.pallas_call(
        body, name="small_update", grid=(1,),
        in_specs=[pl.BlockSpec((N_DEV, 32, 128), lambda i: (0, 0, 0)), full, full, full], out_specs=[full] * 4,
        out_shape=[jax.ShapeDtypeStruct((32, 128), F32)] * 4, compiler_params=_cp(),
    )(gathered, w, m, v)


_ANY = pl.BlockSpec(memory_space=pl.ANY)


def _place():
    x, y, c = lax.axis_index("x"), lax.axis_index("y"), lax.axis_index("c")
    chips = [(1 - x, y), (x, 1 - y), (1 - x, 1 - y)]
    return x, y, c, chips


def _gather_shards(arrs):
    n = len(arrs)

    def body(*refs):
        ins, outs = refs[:n], refs[n:2 * n]
        send_sems, recv_sems, local_sems = refs[2 * n:]
        x, y, c, chips = _place()
        me = 2 * x + y
        sibling = (x, y, 1 - c)
        locals_, sends = [], []
        for a in range(n):
            half = ins[a].shape[0] // 2
            mine = pl.ds(pl.multiple_of(c * half, 16), half)
            lc = pltpu.make_async_copy(ins[a], outs[a].at[me], local_sems.at[a])
            lc.start()
            locals_.append(lc)
            for j, (qx, qy) in enumerate(chips):
                cp = pltpu.make_async_remote_copy(
                    src_ref=ins[a].at[mine], dst_ref=outs[a].at[me, mine],
                    send_sem=send_sems.at[6 * a + j], recv_sem=recv_sems.at[6 * a + j],
                    device_id=(qx, qy, c), device_id_type=MESH)
                cp.start()
                sends.append(cp)
        for a in range(n):
            half = ins[a].shape[0] // 2
            mine = pl.ds(pl.multiple_of(c * half, 16), half)
            for j, (qx, qy) in enumerate(chips):
                q = 2 * qx + qy
                landed = outs[a].at[q, mine]
                pltpu.make_async_remote_copy(
                    src_ref=landed, dst_ref=landed, send_sem=send_sems.at[6 * a + j], recv_sem=recv_sems.at[6 * a + j],
                    device_id=(qx, qy, c), device_id_type=MESH).wait_recv()
                fw = pltpu.make_async_remote_copy(
                    src_ref=landed, dst_ref=landed, send_sem=send_sems.at[6 * a + 3 + j],
                    recv_sem=recv_sems.at[6 * a + 3 + j], device_id=sibling, device_id_type=MESH)
                fw.start()
                sends.append(fw)
        for a in range(n):
            half = ins[a].shape[0] // 2
            theirs = pl.ds(pl.multiple_of((1 - c) * half, 16), half)
            for j, (qx, qy) in enumerate(chips):
                q = 2 * qx + qy
                dst = outs[a].at[q, theirs]
                pltpu.make_async_remote_copy(
                    src_ref=dst, dst_ref=dst, send_sem=send_sems.at[6 * a + 3 + j], recv_sem=recv_sems.at[6 * a + 3 + j],
                    device_id=sibling, device_id_type=MESH).wait_recv()
        for cp in sends:
            cp.wait_send()
        for lc in locals_:
            lc.wait()

    return pl.pallas_call(
        body, name="gather_shards", in_specs=[_ANY] * n, out_specs=[_ANY] * n,
        out_shape=[jax.ShapeDtypeStruct((N_SHARD,) + a.shape, a.dtype) for a in arrs],
        scratch_shapes=[pltpu.SemaphoreType.DMA((6 * n,)), pltpu.SemaphoreType.DMA((6 * n,)),
                        pltpu.SemaphoreType.DMA((n,))],
        compiler_params=pltpu.CompilerParams(has_side_effects=True),
    )(*arrs)


def _pair_reduce_send(grads):
    n = len(grads)

    def body(*refs):
        ins, outs = refs[:n], refs[n:2 * n]
        send_sems, recv_sems = refs[2 * n:]
        x, y, c, _ = _place()
        sibling = (x, y, 1 - c)
        cps = []
        for a in range(n):
            half = ins[a].shape[1] // 2
            theirs = pl.ds(pl.multiple_of((1 - c) * half, 8), half)
            cp = pltpu.make_async_remote_copy(
                src_ref=ins[a].at[:, theirs], dst_ref=outs[a], send_sem=send_sems.at[a], recv_sem=recv_sems.at[a],
                device_id=sibling, device_id_type=MESH)
            cp.start()
            cps.append(cp)
        for cp in cps:
            cp.wait()

    return pl.pallas_call(
        body, name="pair_reduce_send", in_specs=[_ANY] * n, out_specs=[_ANY] * n,
        out_shape=[jax.ShapeDtypeStruct((g.shape[0], g.shape[1] // 2, g.shape[2]), g.dtype) for g in grads],
        scratch_shapes=[pltpu.SemaphoreType.DMA((n,)), pltpu.SemaphoreType.DMA((n,))],
        compiler_params=pltpu.CompilerParams(has_side_effects=True),
    )(*grads)


def _chip_exchange(parts):
    n = len(parts)

    def body(*refs):
        ins, outs = refs[:n], refs[n:2 * n]
        send_sems, recv_sems, local_sems = refs[2 * n:]
        x, y, c, chips = _place()
        me = 2 * x + y
        locals_, cps = [], []
        for a in range(n):
            lc = pltpu.make_async_copy(ins[a].at[me], outs[a].at[me], local_sems.at[a])
            lc.start()
            locals_.append(lc)
            for j, (qx, qy) in enumerate(chips):
                q = 2 * qx + qy
                cp = pltpu.make_async_remote_copy(
                    src_ref=ins[a].at[q], dst_ref=outs[a].at[me], send_sem=send_sems.at[3 * a + j],
                    recv_sem=recv_sems.at[3 * a + j], device_id=(qx, qy, c), device_id_type=MESH)
                cp.start()
                cps.append(cp)
        for a in range(n):
            for j, (qx, qy) in enumerate(chips):
                q = 2 * qx + qy
                dst = outs[a].at[q]
                pltpu.make_async_remote_copy(
                    src_ref=dst, dst_ref=dst, send_sem=send_sems.at[3 * a + j], recv_sem=recv_sems.at[3 * a + j],
                    device_id=(qx, qy, c), device_id_type=MESH).wait_recv()
        for cp in cps:
            cp.wait_send()
        for lc in locals_:
            lc.wait()

    return pl.pallas_call(
        body, name="chip_exchange", in_specs=[_ANY] * n, out_specs=[_ANY] * n,
        out_shape=[jax.ShapeDtypeStruct(p.shape, p.dtype) for p in parts],
        scratch_shapes=[pltpu.SemaphoreType.DMA((3 * n,)), pltpu.SemaphoreType.DMA((3 * n,)),
                        pltpu.SemaphoreType.DMA((n,))],
        compiler_params=pltpu.CompilerParams(has_side_effects=True),
    )(*parts)


def _pair_allgather(halves):
    n = len(halves)

    def body(*refs):
        ins, outs = refs[:n], refs[n:2 * n]
        send_sems, recv_sems, local_sems = refs[2 * n:]
        x, y, c, _ = _place()
        sibling = (x, y, 1 - c)
        cps, locals_ = [], []
        for a in range(n):
            half = ins[a].shape[0]
            mine = pl.ds(pl.multiple_of(c * half, 8), half)
            lc = pltpu.make_async_copy(ins[a], outs[a].at[mine], local_sems.at[a])
            lc.start()
            locals_.append(lc)
            cp = pltpu.make_async_remote_copy(
                src_ref=ins[a], dst_ref=outs[a].at[mine], send_sem=send_sems.at[a], recv_sem=recv_sems.at[a],
                device_id=sibling, device_id_type=MESH)
            cp.start()
            cps.append(cp)
        for a in range(n):
            half = ins[a].shape[0]
            theirs = outs[a].at[pl.ds(pl.multiple_of((1 - c) * half, 8), half)]
            pltpu.make_async_remote_copy(
                src_ref=theirs, dst_ref=theirs, send_sem=send_sems.at[a], recv_sem=recv_sems.at[a],
                device_id=sibling, device_id_type=MESH).wait_recv()
        for cp in cps:
            cp.wait_send()
        for lc in locals_:
            lc.wait()

    return pl.pallas_call(
        body, name="pair_allgather", in_specs=[_ANY] * n, out_specs=[_ANY] * n,
        out_shape=[jax.ShapeDtypeStruct((2 * h.shape[0], h.shape[1]), h.dtype) for h in halves],
        scratch_shapes=[pltpu.SemaphoreType.DMA((n,)), pltpu.SemaphoreType.DMA((n,)), pltpu.SemaphoreType.DMA((n,))],
        compiler_params=pltpu.CompilerParams(has_side_effects=True),
    )(*halves)


def _allgather_small(slab):
    def body(s_ref, out_ref, send_sems, recv_sems):
        x, y, c, _ = _place()
        me = 4 * x + 2 * y + c
        out_ref[me] = s_ref[...]
        cps = []
        for mask in range(1, N_DEV):
            peer = (x ^ (mask >> 2), y ^ ((mask >> 1) & 1), c ^ (mask & 1))
            cp = pltpu.make_async_remote_copy(
                src_ref=s_ref, dst_ref=out_ref.at[me], send_sem=send_sems.at[mask - 1], recv_sem=recv_sems.at[mask - 1],
                device_id=peer, device_id_type=MESH)
            cp.start()
            cps.append(cp)
        for mask in range(1, N_DEV):
            peer = (x ^ (mask >> 2), y ^ ((mask >> 1) & 1), c ^ (mask & 1))
            dst = out_ref.at[4 * peer[0] + 2 * peer[1] + peer[2]]
            pltpu.make_async_remote_copy(
                src_ref=dst, dst_ref=dst, send_sem=send_sems.at[mask - 1], recv_sem=recv_sems.at[mask - 1],
                device_id=peer, device_id_type=MESH).wait_recv()
        for cp in cps:
            cp.wait_send()

    vm = pl.BlockSpec(memory_space=pltpu.VMEM)
    return pl.pallas_call(
        body, name="allgather_small", in_specs=[vm], out_specs=vm,
        out_shape=jax.ShapeDtypeStruct((N_DEV,) + slab.shape, slab.dtype),
        scratch_shapes=[pltpu.SemaphoreType.DMA((N_DEV - 1,)), pltpu.SemaphoreType.DMA((N_DEV - 1,))],
        compiler_params=pltpu.CompilerParams(has_side_effects=True),
    )(slab)


def _pack_b(w_mem_kv, w_br_dn, w_br_sb, w_br_mem, w_out, conv_w):
    parts = [w_mem_kv.reshape(128, D_MODEL), w_br_dn, w_br_sb, w_br_mem.reshape(64, D_MODEL), w_out,
             conv_w.reshape(3, D_MODEL)]
    rows = sum(p.shape[0] for p in parts)
    return jnp.concatenate(parts + [jnp.zeros((B_ROWS - rows, D_MODEL), parts[0].dtype)], axis=0)


def _unpack_b(slab):
    return (slab[B_MEMKV:B_BRDN].reshape(1, 256, 512), slab[B_BRDN:B_BRSB].reshape(1, 256, D_MODEL),
            slab[B_BRSB:B_BRMEM].reshape(1, 256, D_MODEL), slab[B_BRMEM:B_OUT].reshape(1, 256, 256),
            slab[B_OUT:B_CONV].reshape(1, 256, D_MODEL), slab[B_CONV:B_CONV + 3].reshape(1, 4, 768))


def _pack_small(norm_g, mem_norm_g, final_g, dn_norm_g, a_log, dt_bias, loss=None):
    slab = jnp.zeros((32, 128), F32)
    slab = slab.at[S_NORM:S_NORM + 8].set(norm_g.reshape(8, 128))
    slab = slab.at[S_MEMNORM:S_MEMNORM + 8].set(mem_norm_g.reshape(8, 128))
    slab = slab.at[S_FINAL:S_FINAL + 8].set(final_g.reshape(8, 128))
    slab = slab.at[S_DNNORM].set(dn_norm_g.reshape(128))
    slab = slab.at[S_ALOG, :N_HEADS].set(a_log.reshape(N_HEADS))
    slab = slab.at[S_DTB, :N_HEADS].set(dt_bias.reshape(N_HEADS))
    if loss is not None:
        slab = slab.at[S_LOSS, 0].set(loss)
    return slab


def _unpack_small(slab):
    return (slab[S_NORM:S_NORM + 8].reshape(1, D_MODEL), slab[S_MEMNORM:S_MEMNORM + 8].reshape(1, D_MODEL),
            slab[S_FINAL:S_FINAL + 8].reshape(D_MODEL), slab[S_DNNORM].reshape(1, 128),
            slab[S_ALOG, :N_HEADS].reshape(1, N_HEADS), slab[S_DTB, :N_HEADS].reshape(1, N_HEADS))


def _reorder_w_in(w_full):
    pad = jnp.zeros((w_full.shape[0], W_R - IN_WIDTH), w_full.dtype)
    return jnp.concatenate([w_full[:, :4096], w_full[:, 4112:], w_full[:, 4096:4112], pad], axis=1)


def _restore_w_in(g_r):
    return jnp.concatenate([g_r[:, :4096], g_r[:, C_BA:C_BA + 16], g_r[:, 4096:C_BA]], axis=1)


def _local_step(x, mem, tgt, norm_g, mem_norm_g, w_r, conv_w, a_log, dt_bias, dn_norm_g, w_mem_kv, w_br_dn, w_br_sb,
                w_br_mem, w_out, final_g):
    t = x.shape[0]
    final_row = final_g.reshape(1, D_MODEL)
    alog_row = jnp.zeros((1, 128), F32).at[0, N_HEADS:2 * N_HEADS].set(a_log.reshape(N_HEADS))
    dtb_row = jnp.zeros((1, 128), F32).at[0, N_HEADS:2 * N_HEADS].set(dt_bias.reshape(N_HEADS))

    h = _rmsnorm_fwd(x, norm_g, "norm_fwd")
    proj = _mm(h, w_r, "nn", "in_proj")
    qkv = _dn_prep_fwd(proj, conv_w)
    beta_t, g_t = _dn_gate_fwd(proj, alog_row, dtb_row)
    beta8 = beta_t[:, :N_HEADS].T.reshape(N_HEADS, t, 1)
    g8 = g_t[:, N_HEADS:2 * N_HEADS].T.reshape(N_HEADS, t, 1)
    dn_u, dn_w, dn_qd, dn_kd, dn_a, tinv_all, dn_el = _dn_intra_fwd(qkv, beta8, g8)
    o_dn, dn_vn, s_all = _dn_scan_fwd(dn_u, dn_w, dn_qd, dn_kd, dn_a, dn_el)
    o_dn_g = _dn_post_fwd(o_dn, proj, dn_norm_g)
    o_sb, o_sb_g, sb_l = _sb_fwd(proj)
    mem_n = _rmsnorm_fwd(mem, mem_norm_g, "mem_norm_fwd")
    mkv = _mm(mem_n, w_mem_kv, "nn", "mem_kv")
    o_m, o_m_g = _mem_fwd(proj, mkv)
    y_dn = _mm(o_dn_g, w_br_dn, "nn", "br_dn")
    y_sb = _mm(o_sb_g, w_br_sb, "nn", "br_sb")
    y_m = _mm(o_m_g, w_br_mem, "nn", "br_mem")
    merged = _merge_fwd(proj, y_dn, y_sb, y_m)
    mo = _mm(merged, w_out, "nn", "out_proj")
    d_out, d_out_b, loss_row, g_final = _final_loss(x, mo, final_row, tgt)

    g_w_out = _mm(merged, d_out_b, "tn", "g_w_out")
    d_merged = _mm(d_out_b, w_out, "nt", "d_merged")
    dy_dn, dy_sb, dy_m, dg1, dg2, dg3 = _merge_bwd(proj, y_dn, y_sb, y_m, d_merged)
    g_w_br_dn = _mm(o_dn_g, dy_dn, "tn", "g_w_br_dn")
    g_w_br_sb = _mm(o_sb_g, dy_sb, "tn", "g_w_br_sb")
    g_w_br_mem = _mm(o_m_g, dy_m, "tn", "g_w_br_mem")
    d_o_dn_g = _mm(dy_dn, w_br_dn, "nt", "d_o_dn")
    d_o_sb_g = _mm(dy_sb, w_br_sb, "nt", "d_o_sb")
    d_o_m_g = _mm(dy_m, w_br_mem, "nt", "d_o_mem")

    d_mq, d_mz, d_mkv = _mem_bwd(proj, mkv, o_m, d_o_m_g)
    d_mkv_b = _cast_bf16(d_mkv, "cast_dmkv")
    g_w_mem_kv = _mm(mem_n, d_mkv_b, "tn", "g_w_mem_kv")
    d_mem_n = _mm(d_mkv_b, w_mem_kv, "nt", "d_mem_n")
    _, g_mem_norm = _rmsnorm_bwd(mem, mem_norm_g, d_mem_n, jnp.zeros_like(mem), "mem_norm_bwd")

    d_sq, d_sk, d_sv, d_sz = _sb_bwd(proj, o_sb, sb_l, d_o_sb_g)

    d_o_dn, d_dnz, g_dn_norm = _dn_post_bwd(o_dn, proj, dn_norm_g, d_o_dn_g)
    d_vnew, d_kd, d_qd, d_w, d_el = _dn_scan_bwd(dn_w, dn_qd, dn_kd, dn_a, dn_el, dn_vn, s_all, d_o_dn)
    d_qn, d_kn, d_vn, d_beta8, d_g8 = _dn_intra_bwd(qkv, beta8, g8, tinv_all, dn_vn, d_o_dn, d_vnew, d_kd, d_qd, d_w, d_el)
    d_qkv_n = jnp.concatenate([d_qn, d_kn, d_vn], axis=1)
    d_conv_in, g_conv = _dn_prep_bwd(proj, conv_w, d_qkv_n)
    pad = jnp.zeros((t, 128 - N_HEADS), F32)
    dbeta_t = jnp.concatenate([d_beta8.reshape(N_HEADS, t).T, pad], axis=1)
    dg_t = jnp.concatenate([pad[:, :N_HEADS], d_g8.reshape(N_HEADS, t).T, pad[:, :128 - 2 * N_HEADS]], axis=1)
    d_ba, g_alog_row, g_dtb_row = _dn_gate_bwd(proj, alog_row, dtb_row, dbeta_t, dg_t)

    dproj = jnp.concatenate([d_conv_in, d_dnz, d_sq, d_sk, d_sv, d_sz, d_mq, d_mz, dg1, dg2, dg3, d_ba], axis=1)
    g_w_r = _mm(h, dproj, "tn", "g_w_in")
    dh = _mm(dproj, w_r, "nt", "d_h")
    grad_x, g_norm = _rmsnorm_bwd(x, norm_g, dh, d_out, "norm_bwd")

    small = dict(norm_g=g_norm, mem_norm_g=g_mem_norm, final_g=g_final, dn_norm_g=g_dn_norm,
                 a_log=g_alog_row[:, N_HEADS:2 * N_HEADS], dt_bias=g_dtb_row[:, N_HEADS:2 * N_HEADS])
    big = dict(w_r=g_w_r, conv_w=g_conv, w_mem_kv=g_w_mem_kv, w_br_dn=g_w_br_dn, w_br_sb=g_w_br_sb,
               w_br_mem=g_w_br_mem, w_out=g_w_out)
    return loss_row[0, 0], grad_x, small, big


def _reduce_scatter(grads):
    x, y, c = lax.axis_index("x"), lax.axis_index("y"), lax.axis_index("c")
    recv = _pair_reduce_send(grads)
    parts = []
    for g, r in zip(grads, recv):
        half = g.shape[1] // 2
        mine = lax.dynamic_slice_in_dim(g, c * half, half, axis=1)
        parts.append(_add_pair(mine, r, "pair_add"))
    by_chip = _chip_exchange(parts)
    halves = [_sum_lead(b, "chip_sum") for b in by_chip]
    return _pair_allgather(halves)


def kernel(x, mem, norm_g, mem_norm_g, w_in, conv_w, a_log, dt_bias, dn_norm_g, w_mem_kv, w_br_dn, w_br_sb, w_br_mem, w_out, final_g, loss_target, m_norm_g, m_mem_norm_g, m_w_in, m_conv_w, m_a_log, m_dt_bias, m_dn_norm_g, m_w_mem_kv, m_w_br_dn, m_w_br_sb, m_w_br_mem, m_w_out, m_final_g, v_norm_g, v_mem_norm_g, v_w_in, v_conv_w, v_a_log, v_dt_bias, v_dn_norm_g, v_w_mem_kv, v_w_br_dn, v_w_br_sb, v_w_br_mem, v_w_out, v_final_g):
    w_a = w_in[0]
    w_b = _pack_b(w_mem_kv[0], w_br_dn[0], w_br_sb[0], w_br_mem[0], w_out[0], conv_w[0])
    m_b = _pack_b(m_w_mem_kv[0], m_w_br_dn[0], m_w_br_sb[0], m_w_br_mem[0], m_w_out[0], m_conv_w[0])
    v_b = _pack_b(v_w_mem_kv[0], v_w_br_dn[0], v_w_br_sb[0], v_w_br_mem[0], v_w_out[0], v_conv_w[0])

    ga, gb = _gather_shards([_cast_bf16(w_a, "cast_w_in"), _cast_bf16(w_b, "cast_w_b")])
    w_full = ga.transpose(1, 0, 2).reshape(D_MODEL, IN_WIDTH)
    w_r = _reorder_w_in(w_full)
    f_mem_kv = gb[:, B_MEMKV:B_BRDN].reshape(N_SHARD * 256, 512)
    f_br_dn = gb[:, B_BRDN:B_BRSB].reshape(N_SHARD * 256, D_MODEL)
    f_br_sb = gb[:, B_BRSB:B_BRMEM].reshape(N_SHARD * 256, D_MODEL)
    f_br_mem = gb[:, B_BRMEM:B_OUT].reshape(N_SHARD, 256, 256).transpose(1, 0, 2).reshape(256, D_MODEL)
    f_out = gb[:, B_OUT:B_CONV].reshape(N_SHARD * 256, D_MODEL)
    f_conv = gb[:, B_CONV:B_CONV + 3].reshape(N_SHARD, 4, 768).transpose(1, 0, 2).reshape(4, 3 * D_MODEL).astype(F32)

    loss, grad_x, small, big = _local_step(
        x[0], mem[0], loss_target[0], norm_g, mem_norm_g, w_r, f_conv, a_log, dt_bias, dn_norm_g,
        f_mem_kv, f_br_dn, f_br_sb, f_br_mem, f_out, final_g)

    g_in = _restore_w_in(big["w_r"]).reshape(D_MODEL, N_SHARD, IN_WIDTH // N_SHARD).transpose(1, 0, 2)
    g_b = jnp.stack([
        _pack_b(big["w_mem_kv"][256 * s:256 * (s + 1)], big["w_br_dn"][256 * s:256 * (s + 1)],
                big["w_br_sb"][256 * s:256 * (s + 1)], big["w_br_mem"][:, 256 * s:256 * (s + 1)],
                big["w_out"][256 * s:256 * (s + 1)], big["conv_w"][:, 768 * s:768 * (s + 1)])
        for s in range(N_SHARD)])
    gs_in, gs_b = _reduce_scatter([g_in, g_b])

    d_in, nm_in, nv_in = _adamw(w_a, gs_in, m_w_in[0], v_w_in[0], "adamw_w_in")
    d_b, nm_b, nv_b = _adamw(w_b, gs_b, m_b, v_b, "adamw_b")

    part = _pack_small(small["norm_g"], small["mem_norm_g"], small["final_g"], small["dn_norm_g"],
                       small["a_log"], small["dt_bias"], loss)
    w_s = _pack_small(norm_g, mem_norm_g, final_g, dn_norm_g, a_log, dt_bias)
    m_s = _pack_small(m_norm_g, m_mem_norm_g, m_final_g, m_dn_norm_g, m_a_log, m_dt_bias)
    v_s = _pack_small(v_norm_g, v_mem_norm_g, v_final_g, v_dn_norm_g, v_a_log, v_dt_bias)
    g_s, d_s, nm_s, nv_s = _small_update(_allgather_small(part), w_s, m_s, v_s)

    def assemble(slab_small, a_in, slab_b):
        s_norm, s_memnorm, s_final, s_dnnorm, s_alog, s_dtb = _unpack_small(slab_small)
        b_memkv, b_brdn, b_brsb, b_brmem, b_out, b_conv = _unpack_b(slab_b)
        return [s_norm, s_memnorm, a_in.reshape(1, D_MODEL, IN_WIDTH // N_SHARD), b_conv, s_alog, s_dtb, s_dnnorm,
                b_memkv, b_brdn, b_brsb, b_brmem, b_out, s_final]

    outs = [g_s[S_LOSS, 0], grad_x.reshape(1, -1, D_MODEL)]
    outs += assemble(g_s, gs_in, gs_b)
    outs += assemble(d_s, d_in, d_b)
    outs += assemble(nm_s, nm_in, nm_b)
    outs += assemble(nv_s, nv_in, nv_b)
    return tuple(outs)
```

```python
import functools
import math

import jax
import jax.numpy as jnp
from jax import lax
from jax.experimental import pallas as pl
from jax.experimental.pallas import tpu as pltpu

F32 = jnp.float32
BF16 = jnp.bfloat16
MESH = pl.DeviceIdType.MESH
HIGHEST = lax.Precision.HIGHEST

D_MODEL = 1024
N_HEADS = 8
D_HEAD = 128
DN_CHUNK = 64
DN_GROUP = 4
DN_HEADS_PER_STEP = 2
SB_BLOCK = 128
SB_HEADS_PER_STEP = 2
SB_QBLOCK = 256
MEM_HEADS = 4
MEM_DH = 64
MEM_W = MEM_HEADS * MEM_DH
NORM_EPS = 1e-6
IN_WIDTH = 11792
N_SHARD = 4
SHARD_W = IN_WIDTH // N_SHARD
SHARD_PAD = 3072
N_DEV = 8

C_DNZ = 3072
C_SBQ = 4096
C_SBZ = 7168
C_MQ = 8192
C_MZ = 8448
C_GATES = 8704
C_BA = 11776
W_R = 11904

ADAM_LR = 0.001
ADAM_B1 = 0.9
ADAM_B2 = 0.999
ADAM_EPS = 1e-08
ADAM_WD = 0.01
ADAM_STEP = 10

VMEM_LIMIT = 56 * 1024 * 1024

B_ROWS = 992
B_MEMKV, B_BRDN, B_BRSB, B_BRMEM, B_OUT, B_CONV = 0, 128, 384, 640, 704, 960
S_NORM, S_MEMNORM, S_FINAL, S_DNNORM, S_ALOG, S_DTB, S_LOSS = 0, 8, 16, 24, 25, 26, 27


def _cp(**kw):
    return pltpu.CompilerParams(vmem_limit_bytes=VMEM_LIMIT, **kw)


def _dot(a, b, dims):
    lead = a.ndim - 2
    ca, cb = {"nn": (1, 0), "nt": (1, 1), "tn": (0, 0)}[dims]
    batch = tuple(range(lead))
    return lax.dot_general(a, b, (((ca + lead,), (cb + lead,)), (batch, batch)), preferred_element_type=F32)


def _chunks(x):
    return x.reshape(x.shape[0] // DN_CHUNK, DN_CHUNK, x.shape[1])


def _unchunk(x):
    return x.reshape(x.shape[0] * x.shape[1], x.shape[2])


def _bdot(a, b, dims):
    return _dot(a.astype(BF16), b.astype(BF16), dims)


def _split(a):
    hi = a.astype(BF16)
    return hi, (a - hi.astype(F32)).astype(BF16)


def _dot3(a, b, dims):
    a1, a2 = _split(a)
    b1, b2 = _split(b)
    return _dot(a1, b1, dims) + (_dot(a1, b2, dims) + _dot(a2, b1, dims))


def _split_dot(a, ones_bf16):
    hi, lo = _split(a.reshape(-1, a.shape[-1]))
    out = _dot(hi, ones_bf16, "nn") + _dot(lo, ones_bf16, "nn")
    return out.reshape(a.shape[:-1] + (ones_bf16.shape[1],))


def _sigmoid(x):
    return 1.0 / (1.0 + jnp.exp(-x))


def _log1p_small(u):
    return jnp.where(u < 1e-2, u * (1.0 - u * (0.5 - u * (1.0 / 3.0))), jnp.log(1.0 + u))


def _log_sigmoid(z):
    return jnp.minimum(z, 0.0) - _log1p_small(jnp.exp(-jnp.abs(z)))


def _pick(dim, cands):
    for c in cands:
        if dim % c == 0:
            return c
    return dim


def _mm(a, b, dims, name, out_dtype=F32, out_shards=1):
    ta, tb = dims[0] == "t", dims[1] == "t"
    m, k = (a.shape[1], a.shape[0]) if ta else a.shape
    b_shards = b.shape[0] if b.ndim == 3 else 1
    n = b.shape[-2] if tb else b.shape[-1]
    tm = _pick(m, (1024, 512, 256))
    tn = _pick(n // out_shards, (512, 384, 256, 128))
    tk = _pick(k // b_shards, (1024, 512, 384, 256))
    nk = k // tk

    def body(a_ref, b_ref, o_ref, acc_ref):
        kk = pl.program_id(2)

        @pl.when(kk == 0)
        def _():
            acc_ref[...] = jnp.zeros_like(acc_ref)

        acc_ref[...] += _bdot(a_ref[...], b_ref[...], dims)

        @pl.when(kk == nk - 1)
        def _():
            o_ref[...] = acc_ref[...].astype(out_dtype)

    a_spec = pl.BlockSpec((tk, tm), lambda i, j, q: (q, i)) if ta else pl.BlockSpec((tm, tk), lambda i, j, q: (i, q))
    if b_shards > 1:
        per_k = k // b_shards // tk
        b_spec = pl.BlockSpec((None, tn, tk), lambda i, j, q: (q // per_k, j, q % per_k))
    else:
        b_spec = pl.BlockSpec((tn, tk), lambda i, j, q: (j, q)) if tb else pl.BlockSpec((tk, tn), lambda i, j, q: (q, j))
    if out_shards > 1:
        per_n = n // out_shards // tn
        out_spec = pl.BlockSpec((None, tm, tn), lambda i, j, q: (j // per_n, i, j % per_n))
        out_shape = jax.ShapeDtypeStruct((out_shards, m, n // out_shards), out_dtype)
    else:
        out_spec = pl.BlockSpec((tm, tn), lambda i, j, q: (i, j))
        out_shape = jax.ShapeDtypeStruct((m, n), out_dtype)
    return pl.pallas_call(
        body, name=name, grid=(m // tm, n // tn, nk),
        in_specs=[a_spec, b_spec], out_specs=out_spec, out_shape=out_shape,
        scratch_shapes=[pltpu.VMEM((tm, tn), F32)],
        compiler_params=_cp(dimension_semantics=("parallel", "parallel", "arbitrary")),
    )(a, b)


def _rmsnorm_fwd(x, g, name):
    t, d = x.shape
    tb = _pick(t, (512, 256))

    def body(x_ref, g_ref, h_ref):
        xv = x_ref[...]
        r = lax.rsqrt(jnp.mean(xv * xv, axis=-1, keepdims=True) + NORM_EPS)
        h_ref[...] = ((xv * r) * g_ref[...]).astype(BF16)

    return pl.pallas_call(
        body, name=name, grid=(t // tb,),
        in_specs=[pl.BlockSpec((tb, d), lambda i: (i, 0)), pl.BlockSpec((1, d), lambda i: (0, 0))],
        out_specs=pl.BlockSpec((tb, d), lambda i: (i, 0)),
        out_shape=jax.ShapeDtypeStruct((t, d), BF16), compiler_params=_cp(),
    )(x, g)


def _rmsnorm_bwd(x, g, dh, resid, name):
    t, d = x.shape
    tb = _pick(t, (256,))

    def body(x_ref, g_ref, dh_ref, r_ref, dx_ref, dg_ref):
        @pl.when(pl.program_id(0) == 0)
        def _():
            dg_ref[...] = jnp.zeros_like(dg_ref)

        xv = x_ref[...]
        r = lax.rsqrt(jnp.mean(xv * xv, axis=-1, keepdims=True) + NORM_EPS)
        xhat = xv * r
        dhv = dh_ref[...]
        dg_ref[...] += jnp.sum(dhv * xhat, axis=0, keepdims=True)
        dxh = dhv * g_ref[...]
        dx_ref[...] = r_ref[...] + r * (dxh - xhat * jnp.mean(dxh * xhat, axis=-1, keepdims=True))

    row = pl.BlockSpec((tb, d), lambda i: (i, 0))
    vec = pl.BlockSpec((1, d), lambda i: (0, 0))
    return pl.pallas_call(
        body, name=name, grid=(t // tb,), in_specs=[row, vec, row, row], out_specs=[row, vec],
        out_shape=[jax.ShapeDtypeStruct((t, d), F32), jax.ShapeDtypeStruct((1, d), F32)], compiler_params=_cp(),
    )(x, g, dh, resid)


def _conv_silu(xv, w, row):
    y = xv * w[3:4, :]
    for s in (1, 2, 3):
        xs = jnp.where(row >= s, pltpu.roll(xv, s, 0), 0.0)
        y = y + xs * w[3 - s:4 - s, :]
    return y, y * _sigmoid(y)


def _dn_prep_fwd(proj, conv_w):
    t = proj.shape[0]

    def body(p_ref, w_ref, o_ref):
        j = pl.program_id(0)
        xv = p_ref[...]
        row = lax.broadcasted_iota(jnp.int32, xv.shape, 0)
        _, a = _conv_silu(xv, w_ref[...], row)
        inv = lax.rsqrt(jnp.sum(a * a, axis=-1, keepdims=True) + NORM_EPS)
        scale = jnp.where(j < N_HEADS, D_HEAD ** -0.5, 1.0)
        normed = jnp.where(j < 2 * N_HEADS, 1.0, 0.0)
        o_ref[...] = a * (normed * (inv * scale) + (1.0 - normed))

    return pl.pallas_call(
        body, name="dn_prep_fwd", grid=(3 * N_HEADS,),
        in_specs=[pl.BlockSpec((t, D_HEAD), lambda j: (0, j)), pl.BlockSpec((4, D_HEAD), lambda j: (0, j))],
        out_specs=pl.BlockSpec((t, D_HEAD), lambda j: (0, j)),
        out_shape=jax.ShapeDtypeStruct((t, 3 * D_MODEL), F32), compiler_params=_cp(),
    )(proj, conv_w)


def _dn_prep_bwd(proj, conv_w, dq, dk, dv):
    t = proj.shape[0]

    def body(p_ref, w_ref, dq_ref, dk_ref, dv_ref, dp_ref, dw_ref):
        j = pl.program_id(0)
        xv = p_ref[...]
        w = w_ref[...]
        row = lax.broadcasted_iota(jnp.int32, xv.shape, 0)
        y, a = _conv_silu(xv, w, row)
        part = jnp.zeros(xv.shape, jnp.int32) + j // N_HEADS
        dn = jnp.where(part == 0, dq_ref[...], jnp.where(part == 1, dk_ref[...], dv_ref[...]))
        inv = lax.rsqrt(jnp.sum(a * a, axis=-1, keepdims=True) + NORM_EPS)
        scale = jnp.where(j < N_HEADS, D_HEAD ** -0.5, 1.0)
        ds = dn * scale
        da_norm = inv * ds - a * (inv * inv * inv) * jnp.sum(ds * a, axis=-1, keepdims=True)
        normed = jnp.where(j < 2 * N_HEADS, 1.0, 0.0)
        da = normed * da_norm + (1.0 - normed) * dn
        s = _sigmoid(y)
        dy = da * (s * (1.0 + y * (1.0 - s)))
        dx = dy * w[3:4, :]
        dw_ref[3:4, :] = jnp.sum(dy * xv, axis=0, keepdims=True)
        for sft in (1, 2, 3):
            xs = jnp.where(row >= sft, pltpu.roll(xv, sft, 0), 0.0)
            dw_ref[3 - sft:4 - sft, :] = jnp.sum(dy * xs, axis=0, keepdims=True)
            dys = jnp.where(row < t - sft, pltpu.roll(dy, t - sft, 0), 0.0)
            dx = dx + dys * w[3 - sft:4 - sft, :]
        dp_ref[...] = dx.astype(BF16)

    blk = pl.BlockSpec((t, D_HEAD), lambda j: (0, j))
    wblk = pl.BlockSpec((4, D_HEAD), lambda j: (0, j))

    def grad(part):
        return pl.BlockSpec((t, D_HEAD), lambda j: (0, jnp.clip(j - part * N_HEADS, 0, N_HEADS - 1)))

    return pl.pallas_call(
        body, name="dn_prep_bwd", grid=(3 * N_HEADS,), in_specs=[blk, wblk, grad(0), grad(1), grad(2)],
        out_specs=[blk, wblk],
        out_shape=[jax.ShapeDtypeStruct((t, 3 * D_MODEL), BF16), jax.ShapeDtypeStruct((4, 3 * D_MODEL), F32)],
        compiler_params=_cp(),
    )(proj, conv_w, dq, dk, dv)


def _softplus_parts(xv):
    e = jnp.exp(-jnp.abs(xv))
    return jnp.maximum(xv, 0.0) + _log1p_small(e)


def _chunk_scan(v, row, reverse):
    t = v.shape[0]
    pos = row & (DN_CHUNK - 1)
    s = 1
    while s < DN_CHUNK:
        if reverse:
            v = v + jnp.where(pos < DN_CHUNK - s, pltpu.roll(v, t - s, 0), 0.0)
        else:
            v = v + jnp.where(pos >= s, pltpu.roll(v, s, 0), 0.0)
        s *= 2
    return v


def _dn_gate_fwd(proj, alog_row, dtb_row):
    t = proj.shape[0]

    def body(p_ref, al_ref, dt_ref, b_ref, g_ref):
        p = p_ref[...]
        row = lax.broadcasted_iota(jnp.int32, p.shape, 0)
        b_ref[...] = _sigmoid(p)
        g = -jnp.exp(al_ref[...]) * _softplus_parts(p + dt_ref[...])
        g_ref[...] = _chunk_scan(g, row, reverse=False)

    blk = pl.BlockSpec((t, 128), lambda i: (0, C_BA // 128))
    vec = pl.BlockSpec((1, 128), lambda i: (0, 0))
    out = pl.BlockSpec((t, 128), lambda i: (0, 0))
    return pl.pallas_call(
        body, name="dn_gate_fwd", grid=(1,), in_specs=[blk, vec, vec], out_specs=[out, out],
        out_shape=[jax.ShapeDtypeStruct((t, 128), F32)] * 2, compiler_params=_cp(),
    )(proj, alog_row, dtb_row)


def _dn_gate_bwd(proj, alog_row, dtb_row, dbeta, dgc):
    t = proj.shape[0]

    def body(p_ref, al_ref, dt_ref, db_ref, dg_ref, dp_ref, dal_ref, ddt_ref):
        p = p_ref[...]
        row = lax.broadcasted_iota(jnp.int32, p.shape, 0)
        lane = lax.broadcasted_iota(jnp.int32, p.shape, 1)
        s = _sigmoid(p)
        d_b = db_ref[...] * s * (1.0 - s)
        dg = _chunk_scan(dg_ref[...], row, reverse=True)
        xa = p + dt_ref[...]
        ea = jnp.exp(al_ref[...])
        g = -ea * _softplus_parts(xa)
        d_a = dg * (-ea) * _sigmoid(xa)
        dp_ref[...] = jnp.where(lane < N_HEADS, d_b, jnp.where(lane < 2 * N_HEADS, d_a, 0.0)).astype(BF16)
        dal_ref[...] = jnp.sum(dg * g, axis=0, keepdims=True)
        ddt_ref[...] = jnp.sum(d_a, axis=0, keepdims=True)

    blk = pl.BlockSpec((t, 128), lambda i: (0, C_BA // 128))
    vec = pl.BlockSpec((1, 128), lambda i: (0, 0))
    full = pl.BlockSpec((t, 128), lambda i: (0, 0))
    return pl.pallas_call(
        body, name="dn_gate_bwd", grid=(1,), in_specs=[blk, vec, vec, full, full], out_specs=[full, vec, vec],
        out_shape=[jax.ShapeDtypeStruct((t, 128), BF16), jax.ShapeDtypeStruct((1, 128), F32),
                   jax.ShapeDtypeStruct((1, 128), F32)], compiler_params=_cp(),
    )(proj, alog_row, dtb_row, dbeta, dgc)


def _col_to_row(col, eye):
    return jnp.sum(jnp.where(eye, col, 0.0), axis=-2, keepdims=True)


def _row_to_col(rowv, eye):
    return jnp.sum(jnp.where(eye, rowv, 0.0), axis=-1, keepdims=True)


def _tri_inverse(m, ri, ci):
    eye = (ri == ci).astype(F32)
    b16 = (ri >> 4) == (ci >> 4)
    b32 = (ri >> 5) == (ci >> 5)
    m1 = jnp.where(b16, m, 0.0)
    x = eye - m1
    p = _dot3(m1, m1, "nn")
    x = x + _dot3(x, p, "nn")
    p = _dot3(p, p, "nn")
    x = x + _dot3(x, p, "nn")
    p = _dot3(p, p, "nn")
    x = x + _dot3(x, p, "nn")
    c1 = jnp.where(jnp.logical_and(b32, jnp.logical_not(b16)), m, 0.0)
    x = x - _dot3(_dot3(x, c1, "nn"), x, "nn")
    c2 = jnp.where(b32, 0.0, m)
    x = x - _dot3(_dot3(x, c2, "nn"), x, "nn")
    return x


def _dn_chunk_common(q, k, gc, ri, ci):
    eye = ri == ci
    g_row = _col_to_row(gc, eye)
    diff = jnp.minimum(gc - g_row, 0.0)
    gam = jnp.where(ri >= ci, jnp.exp(diff), 0.0)
    kk = _bdot(k, k, "nt")
    qk = _bdot(q, k, "nt")
    rcol = lax.broadcasted_iota(jnp.int32, gc.shape, gc.ndim - 2)
    last = jnp.sum(jnp.where(rcol == DN_CHUNK - 1, gc, 0.0), axis=-2, keepdims=True)
    e_g = jnp.exp(gc)
    dec = jnp.exp(last - gc)
    return eye, gam, kk, qk, last, e_g, dec, rcol


def _dn_specs(t, rows_blk):
    def head(off):
        return pl.BlockSpec((rows_blk, D_HEAD), lambda h, g: (g, off + h))

    col = pl.BlockSpec((1, rows_blk, 1), lambda h, g: (h, g, 0))
    sq = pl.BlockSpec((1, rows_blk, DN_CHUNK), lambda h, g: (h, g, 0))
    tile = pl.BlockSpec((1, rows_blk // DN_CHUNK, 8, 128), lambda h, g: (h, g, 0, 0))
    return head, col, sq, tile


def _dn_intra_fwd(qkv, beta8, g8):
    t = qkv.shape[0]
    n_chunks = t // DN_CHUNK
    rows_blk = DN_GROUP * DN_CHUNK

    def body(q_ref, k_ref, v_ref, b_ref, g_ref, u_ref, w_ref, qd_ref, kd_ref, a_ref, ti_ref, el_ref):
        ri = lax.broadcasted_iota(jnp.int32, (DN_CHUNK, DN_CHUNK), 0)
        ci = lax.broadcasted_iota(jnp.int32, (DN_CHUNK, DN_CHUNK), 1)
        q, k, v = (_chunks(r[...]) for r in (q_ref, k_ref, v_ref))
        b, gc = _chunks(b_ref[0]), _chunks(g_ref[0])
        _, gam, kk, qk, last, e_g, dec, _ = _dn_chunk_common(q, k, gc, ri, ci)
        tinv = _tri_inverse(jnp.where(ri > ci, b * kk * gam, 0.0), ri, ci)
        u_ref[...] = _unchunk(_bdot(tinv, v * b, "nn"))
        w_ref[...] = _unchunk(_bdot(tinv, k * (b * e_g), "nn"))
        qd_ref[...] = _unchunk(q * e_g)
        kd_ref[...] = _unchunk(k * dec)
        a_ref[0] = _unchunk(qk * gam)
        ti_ref[0] = _unchunk(tinv)
        el_ref[0] = jnp.broadcast_to(jnp.exp(last), (DN_GROUP, 8, 128))

    head, col, sq, tile = _dn_specs(t, rows_blk)
    act = jax.ShapeDtypeStruct((t, D_MODEL), F32)
    sqs = jax.ShapeDtypeStruct((N_HEADS, t, DN_CHUNK), F32)
    return pl.pallas_call(
        body, name="dn_intra_fwd", grid=(N_HEADS, t // rows_blk),
        in_specs=[head(0), head(N_HEADS), head(2 * N_HEADS), col, col],
        out_specs=[head(0)] * 4 + [sq, sq, tile],
        out_shape=[act] * 4 + [sqs, sqs, jax.ShapeDtypeStruct((N_HEADS, n_chunks, 8, 128), F32)],
        compiler_params=_cp(),
    )(qkv, qkv, qkv, beta8, g8)


def _dn_scan_specs(t, n_chunks):
    hp = DN_HEADS_PER_STEP
    act = pl.BlockSpec((t, hp * D_HEAD), lambda h: (0, h))
    sq = pl.BlockSpec((hp, t, DN_CHUNK), lambda h: (h, 0, 0))
    state = pl.BlockSpec((hp, n_chunks, D_HEAD, D_HEAD), lambda h: (h, 0, 0, 0))
    tile = pl.BlockSpec((hp, n_chunks, 8, 128), lambda h: (h, 0, 0, 0))
    return act, sq, state, tile


def _dn_scan_fwd(u, w, qd, kd, a, el):
    t = u.shape[0]
    n_chunks = t // DN_CHUNK
    hp = DN_HEADS_PER_STEP

    def body(u_ref, w_ref, qd_ref, kd_ref, a_ref, el_ref, o_ref, vn_ref, s_ref, s_scr):
        s_scr[...] = jnp.zeros_like(s_scr)

        def chunk(n, carry):
            rows = pl.ds(pl.multiple_of(n * DN_CHUNK, DN_CHUNK), DN_CHUNK)
            for hh in range(hp):
                hs = slice(hh * D_HEAD, (hh + 1) * D_HEAD)
                s = s_scr[hh]
                s_ref[hh, n] = s
                v_new = u_ref[rows, hs] - _bdot(w_ref[rows, hs], s, "nn")
                vn_ref[rows, hs] = v_new
                o_ref[rows, hs] = _bdot(qd_ref[rows, hs], s, "nn") + _bdot(a_ref[hh, rows, :], v_new, "nn")
                s_scr[hh] = s * el_ref[hh, n][0:1, :] + _bdot(kd_ref[rows, hs], v_new, "tn")
            return carry

        lax.fori_loop(0, n_chunks, chunk, 0)

    act, sq, state, tile = _dn_scan_specs(t, n_chunks)
    shp = jax.ShapeDtypeStruct((t, D_MODEL), F32)
    return pl.pallas_call(
        body, name="dn_scan_fwd", grid=(N_HEADS // hp,),
        in_specs=[act, act, act, act, sq, tile], out_specs=[act, act, state],
        out_shape=[shp, shp, jax.ShapeDtypeStruct((N_HEADS, n_chunks, D_HEAD, D_HEAD), F32)],
        scratch_shapes=[pltpu.VMEM((hp, D_HEAD, D_HEAD), F32)], compiler_params=_cp(),
    )(u, w, qd, kd, a, el)


def _dn_scan_bwd(w, qd, kd, a, el, vn, s_all, do):
    t = w.shape[0]
    n_chunks = t // DN_CHUNK
    hp = DN_HEADS_PER_STEP

    def body(w_ref, qd_ref, kd_ref, a_ref, el_ref, vn_ref, s_ref, do_ref, dvn_ref, dkd_ref, dqd_ref, dw_ref, dl_ref, ds_scr):
        ds_scr[...] = jnp.zeros_like(ds_scr)

        def chunk(i, carry):
            n = n_chunks - 1 - i
            rows = pl.ds(pl.multiple_of(n * DN_CHUNK, DN_CHUNK), DN_CHUNK)
            for hh in range(hp):
                hs = slice(hh * D_HEAD, (hh + 1) * D_HEAD)
                s = s_ref[hh, n]
                d_s = ds_scr[hh]
                e_last = el_ref[hh, n][0:1, :]
                d_o = do_ref[rows, hs]
                dv_new = _bdot(a_ref[hh, rows, :], d_o, "tn") + _bdot(kd_ref[rows, hs], d_s, "nn")
                ds_scr[hh] = d_s * e_last + _bdot(qd_ref[rows, hs], d_o, "tn") - _bdot(w_ref[rows, hs], dv_new, "tn")
                dvn_ref[rows, hs] = dv_new
                dkd_ref[rows, hs] = _bdot(vn_ref[rows, hs], d_s, "nt")
                dqd_ref[rows, hs] = _bdot(d_o, s, "nt")
                dw_ref[rows, hs] = -_bdot(dv_new, s, "nt")
                dlast = jnp.sum(jnp.sum(d_s * s, axis=1, keepdims=True), axis=0, keepdims=True)
                dl_ref[hh, n] = jnp.broadcast_to(dlast * e_last, (8, 128))
            return carry

        lax.fori_loop(0, n_chunks, chunk, 0)

    act, sq, state, tile = _dn_scan_specs(t, n_chunks)
    shp = jax.ShapeDtypeStruct((t, D_MODEL), F32)
    return pl.pallas_call(
        body, name="dn_scan_bwd", grid=(N_HEADS // hp,),
        in_specs=[act, act, act, sq, tile, act, state, act], out_specs=[act] * 4 + [tile],
        out_shape=[shp] * 4 + [jax.ShapeDtypeStruct((N_HEADS, n_chunks, 8, 128), F32)],
        scratch_shapes=[pltpu.VMEM((hp, D_HEAD, D_HEAD), F32)], compiler_params=_cp(),
    )(w, qd, kd, a, el, vn, s_all, do)


def _dn_intra_bwd(qkv, beta8, g8, tinv_all, vn, do, dvn, dkd, dqd, dw, dl):
    t = qkv.shape[0]
    rows_blk = DN_GROUP * DN_CHUNK

    def body(q_ref, k_ref, v_ref, b_ref, g_ref, ti_ref, vn_ref, do_ref, dvn_ref, dkd_ref, dqd_ref, dw_ref, dl_ref,
             dq_ref, dk_ref, dv_ref, db_ref, dg_ref):
        ri = lax.broadcasted_iota(jnp.int32, (DN_CHUNK, DN_CHUNK), 0)
        ci = lax.broadcasted_iota(jnp.int32, (DN_CHUNK, DN_CHUNK), 1)
        q, k, v = (_chunks(r[...]) for r in (q_ref, k_ref, v_ref))
        b, gc = _chunks(b_ref[0]), _chunks(g_ref[0])
        tinv = _chunks(ti_ref[0])
        dv_new, dk_dec, dq_dec, d_w = (_chunks(r[...]) for r in (dvn_ref, dkd_ref, dqd_ref, dw_ref))
        eye, gam, kk, qk, _, e_g, dec, rcol = _dn_chunk_common(q, k, gc, ri, ci)
        bv = v * b
        bk = k * (b * e_g)

        d_a = jnp.where(ri >= ci, _bdot(_chunks(do_ref[...]), _chunks(vn_ref[...]), "nt"), 0.0)
        dbv = _bdot(tinv, dv_new, "tn")
        dbk = _bdot(tinv, d_w, "tn")
        d_tinv = _bdot(dv_new, bv, "nt") + _bdot(d_w, bk, "nt")
        d_m = -jnp.where(ri > ci, _dot3(_dot3(tinv, d_tinv, "tn"), tinv, "nt"), 0.0)

        d_kk = d_m * b * gam
        d_gam = d_m * b * kk + d_a * qk
        d_qk = d_a * gam
        dq_ref[...] = _unchunk(_bdot(d_qk, k, "nn") + dq_dec * e_g)
        dk_ref[...] = _unchunk(_bdot(d_qk, q, "tn") + _bdot(d_kk, k, "nn") + _bdot(d_kk, k, "tn")
                               + dk_dec * dec + dbk * (b * e_g))
        dv_ref[...] = _unchunk(dbv * b)
        db_ref[0] = _unchunk(jnp.sum(d_m * kk * gam, axis=-1, keepdims=True) + jnp.sum(dbv * v, axis=-1, keepdims=True)
                             + jnp.sum(dbk * k, axis=-1, keepdims=True) * e_g)

        xg = d_gam * gam
        kdk = jnp.sum(dk_dec * (k * dec), axis=-1, keepdims=True)
        d_gc = (jnp.sum(xg, axis=-1, keepdims=True) - _row_to_col(jnp.sum(xg, axis=-2, keepdims=True), eye)
                + jnp.sum(dq_dec * (q * e_g), axis=-1, keepdims=True) - kdk
                + jnp.sum(dbk * bk, axis=-1, keepdims=True))
        d_last_total = dl_ref[0][:, 0:1, 0:1] + jnp.sum(kdk, axis=-2, keepdims=True)
        dg_ref[0] = _unchunk(d_gc + jnp.where(rcol == DN_CHUNK - 1, d_last_total, 0.0))

    head, col, sq, tile = _dn_specs(t, rows_blk)
    return pl.pallas_call(
        body, name="dn_intra_bwd", grid=(N_HEADS, t // rows_blk),
        in_specs=[head(0), head(N_HEADS), head(2 * N_HEADS), col, col, sq] + [head(0)] * 6 + [tile],
        out_specs=[head(0), head(0), head(0), col, col],
        out_shape=[jax.ShapeDtypeStruct((t, D_MODEL), F32)] * 3 + [jax.ShapeDtypeStruct((N_HEADS, t, 1), F32)] * 2,
        compiler_params=_cp(),
    )(qkv, qkv, qkv, beta8, g8, tinv_all, vn, do, dvn, dkd, dqd, dw, dl)


def _dn_post_fwd(o, proj, gn):
    t = o.shape[0]

    def body(o_ref, z_ref, g_ref, out_ref):
        ov, z = o_ref[...], z_ref[...]
        r = lax.rsqrt(jnp.mean(ov * ov, axis=-1, keepdims=True) + NORM_EPS)
        out_ref[...] = (((ov * r) * g_ref[...]) * (z * _sigmoid(z))).astype(BF16)

    blk = pl.BlockSpec((t, D_HEAD), lambda h: (0, h))
    return pl.pallas_call(
        body, name="dn_post_fwd", grid=(N_HEADS,),
        in_specs=[blk, pl.BlockSpec((t, D_HEAD), lambda h: (0, C_DNZ // D_HEAD + h)),
                  pl.BlockSpec((1, D_HEAD), lambda h: (0, 0))],
        out_specs=blk, out_shape=jax.ShapeDtypeStruct((t, D_MODEL), BF16), compiler_params=_cp(),
    )(o, proj, gn)


def _dn_post_bwd(o, proj, gn, dout):
    t = o.shape[0]

    def body(o_ref, z_ref, g_ref, d_ref, do_ref, dz_ref, dg_ref):
        @pl.when(pl.program_id(0) == 0)
        def _():
            dg_ref[...] = jnp.zeros_like(dg_ref)

        ov, z, d = o_ref[...], z_ref[...], d_ref[...]
        r = lax.rsqrt(jnp.mean(ov * ov, axis=-1, keepdims=True) + NORM_EPS)
        ohat = ov * r
        s = _sigmoid(z)
        d_on = d * (z * s)
        dz_ref[...] = (d * (ohat * g_ref[...]) * (s * (1.0 + z * (1.0 - s)))).astype(BF16)
        dg_ref[...] += jnp.sum(d_on * ohat, axis=0, keepdims=True)
        dxh = d_on * g_ref[...]
        do_ref[...] = r * (dxh - ohat * jnp.mean(dxh * ohat, axis=-1, keepdims=True))

    blk = pl.BlockSpec((t, D_HEAD), lambda h: (0, h))
    vec = pl.BlockSpec((1, D_HEAD), lambda h: (0, 0))
    return pl.pallas_call(
        body, name="dn_post_bwd", grid=(N_HEADS,),
        in_specs=[blk, pl.BlockSpec((t, D_HEAD), lambda h: (0, C_DNZ // D_HEAD + h)), vec, blk],
        out_specs=[blk, blk, vec],
        out_shape=[jax.ShapeDtypeStruct((t, D_MODEL), F32), jax.ShapeDtypeStruct((t, D_MODEL), BF16),
                   jax.ShapeDtypeStruct((1, D_HEAD), F32)], compiler_params=_cp(),
    )(o, proj, gn, dout)


def _sb_fwd(proj):
    t = proj.shape[0]
    qblk = min(SB_QBLOCK, t)
    scale = 1.0 / math.sqrt(D_HEAD)

    hp = SB_HEADS_PER_STEP
    wid = hp * D_HEAD

    def body(q_ref, k_ref, v_ref, z_ref, o_ref, og_ref, l_ref, qb, kb, vb):
        for hh in range(hp):
            hs = slice(hh * D_HEAD, (hh + 1) * D_HEAD)
            qb[hh] = q_ref[:, hs].astype(BF16)
            kb[hh] = k_ref[:, hs].astype(BF16)
            vb[hh] = v_ref[:, hs].astype(BF16)
        ri = lax.broadcasted_iota(jnp.int32, (qblk, SB_BLOCK), 0)
        ci = lax.broadcasted_iota(jnp.int32, (qblk, SB_BLOCK), 1)
        r2 = lax.broadcasted_iota(jnp.int32, (SB_BLOCK, SB_BLOCK), 0)
        c2 = lax.broadcasted_iota(jnp.int32, (SB_BLOCK, SB_BLOCK), 1)
        upper = (r2 > c2).astype(BF16)
        nkb = qblk // SB_BLOCK

        def qblock(i, carry):
            rows = pl.ds(pl.multiple_of(i * qblk, qblk), qblk)
            qi = qb[:, rows, :]

            def kblock(jj, st):
                acc, c = st
                j = (i + 1) * nkb - 1 - jj
                cols = pl.ds(pl.multiple_of(j * SB_BLOCK, SB_BLOCK), SB_BLOCK)
                mask = (j * SB_BLOCK + ci) < (i * qblk + ri)
                z = _dot(qi, kb[:, cols, :], "nt") * scale
                lb = jnp.minimum(z, 0.0) - jnp.log(1.0 + jnp.exp(-jnp.abs(z)))
                lf = jnp.where(mask, lb - z, 0.0)
                surv = _split_dot(lf, upper) + c
                att = jnp.where(mask, jnp.exp(lb + surv), 0.0)
                acc = acc + _dot(att.astype(BF16), vb[:, cols, :], "nn")
                return acc, c + jnp.sum(lf, axis=-1, keepdims=True)

            init = (jnp.zeros((hp, qblk, D_HEAD), F32), jnp.zeros((hp, qblk, 1), F32))
            acc, c = lax.fori_loop(0, (i + 1) * nkb, kblock, init)
            l_ref[:, rows, :] = c
            for hh in range(hp):
                hs = slice(hh * D_HEAD, (hh + 1) * D_HEAD)
                zg = z_ref[rows, hs]
                o_ref[rows, hs] = acc[hh]
                og_ref[rows, hs] = (acc[hh] * (zg * _sigmoid(zg))).astype(BF16)
            return carry

        lax.fori_loop(0, t // qblk, qblock, 0)

    def head(off):
        return pl.BlockSpec((t, wid), lambda h: (0, off // wid + h))

    out = pl.BlockSpec((t, wid), lambda h: (0, h))
    return pl.pallas_call(
        body, name="sb_fwd", grid=(N_HEADS // hp,),
        in_specs=[head(C_SBQ), head(C_SBQ + D_MODEL), head(C_SBQ + 2 * D_MODEL), head(C_SBZ)],
        out_specs=[out, out, pl.BlockSpec((hp, t, 1), lambda h: (h, 0, 0))],
        out_shape=[jax.ShapeDtypeStruct((t, D_MODEL), F32), jax.ShapeDtypeStruct((t, D_MODEL), BF16),
                   jax.ShapeDtypeStruct((N_HEADS, t, 1), F32)],
        scratch_shapes=[pltpu.VMEM((hp, t, D_HEAD), BF16)] * 3, compiler_params=_cp(),
    )(proj, proj, proj, proj)


def _sb_bwd(proj, o, ltot, dog):
    t = proj.shape[0]
    qblk = min(SB_QBLOCK, t)
    scale = 1.0 / math.sqrt(D_HEAD)

    hp = SB_HEADS_PER_STEP
    wid = hp * D_HEAD

    def body(q_ref, k_ref, v_ref, z_ref, o_ref, l_ref, d_ref, dq_ref, dk_ref, dv_ref, dz_ref,
             qb, kb, vb, dob, dk_scr, dv_scr):
        for hh in range(hp):
            hs = slice(hh * D_HEAD, (hh + 1) * D_HEAD)
            qb[hh] = q_ref[:, hs].astype(BF16)
            kb[hh] = k_ref[:, hs].astype(BF16)
            vb[hh] = v_ref[:, hs].astype(BF16)
            zg = z_ref[:, hs]
            sg = _sigmoid(zg)
            dgo = d_ref[:, hs]
            dob[hh] = (dgo * (zg * sg)).astype(BF16)
            dz_ref[:, hs] = (dgo * o_ref[:, hs] * (sg * (1.0 + zg * (1.0 - sg)))).astype(BF16)
        dk_scr[...] = jnp.zeros_like(dk_scr)
        dv_scr[...] = jnp.zeros_like(dv_scr)
        ri = lax.broadcasted_iota(jnp.int32, (qblk, SB_BLOCK), 0)
        ci = lax.broadcasted_iota(jnp.int32, (qblk, SB_BLOCK), 1)
        r2 = lax.broadcasted_iota(jnp.int32, (SB_BLOCK, SB_BLOCK), 0)
        c2 = lax.broadcasted_iota(jnp.int32, (SB_BLOCK, SB_BLOCK), 1)
        incl = (r2 <= c2).astype(BF16)
        below = (r2 < c2).astype(BF16)

        def qblock(i, carry):
            rows = pl.ds(pl.multiple_of(i * qblk, qblk), qblk)
            qi = qb[:, rows, :]
            d_o = dob[:, rows, :]
            ltot = l_ref[:, rows, :]

            def kblock(j, st):
                dq, cpre, ce = st
                cols = pl.ds(pl.multiple_of(j * SB_BLOCK, SB_BLOCK), SB_BLOCK)
                mask = (j * SB_BLOCK + ci) < (i * qblk + ri)
                kj, vj = kb[:, cols, :], vb[:, cols, :]
                z = _dot(qi, kj, "nt") * scale
                e1 = jnp.exp(-jnp.abs(z))
                den = 1.0 + e1
                lb = jnp.minimum(z, 0.0) - jnp.log(den)
                r = 1.0 / den
                sig = jnp.where(z >= 0.0, r, e1 * r)
                nsig = jnp.where(z >= 0.0, e1 * r, r)
                lf = jnp.where(mask, lb - z, 0.0)
                surv = ltot - (cpre + _split_dot(lf, incl))
                att = jnp.where(mask, jnp.exp(lb + surv), 0.0)
                e = _dot(d_o, vj, "nt") * att
                dlf = ce + _split_dot(e, below)
                dzz = jnp.where(mask, e * nsig - dlf * sig, 0.0).astype(BF16)
                dq = dq + _dot(dzz, kj, "nn")
                dk_scr[:, cols, :] += _dot(dzz, qi, "tn")
                dv_scr[:, cols, :] += _dot(att.astype(BF16), d_o, "tn")
                return dq, cpre + jnp.sum(lf, axis=-1, keepdims=True), ce + jnp.sum(e, axis=-1, keepdims=True)

            zero_col = jnp.zeros((hp, qblk, 1), F32)
            init = (jnp.zeros((hp, qblk, D_HEAD), F32), zero_col, zero_col)
            dq, _, _ = lax.fori_loop(0, (i + 1) * (qblk // SB_BLOCK), kblock, init)
            for hh in range(hp):
                dq_ref[rows, hh * D_HEAD:(hh + 1) * D_HEAD] = (dq[hh] * scale).astype(BF16)
            return carry

        lax.fori_loop(0, t // qblk, qblock, 0)
        for hh in range(hp):
            hs = slice(hh * D_HEAD, (hh + 1) * D_HEAD)
            dk_ref[:, hs] = (dk_scr[hh] * scale).astype(BF16)
            dv_ref[:, hs] = dv_scr[hh].astype(BF16)

    def head(off):
        return pl.BlockSpec((t, wid), lambda h: (0, off // wid + h))

    return pl.pallas_call(
        body, name="sb_bwd", grid=(N_HEADS // hp,),
        in_specs=[head(C_SBQ), head(C_SBQ + D_MODEL), head(C_SBQ + 2 * D_MODEL), head(C_SBZ), head(0),
                  pl.BlockSpec((hp, t, 1), lambda h: (h, 0, 0)), head(0)],
        out_specs=[head(0)] * 4, out_shape=[jax.ShapeDtypeStruct((t, D_MODEL), BF16)] * 4,
        scratch_shapes=[pltpu.VMEM((hp, t, D_HEAD), BF16)] * 4 + [pltpu.VMEM((hp, t, D_HEAD), F32)] * 2,
        compiler_params=_cp(),
    )(proj, proj, proj, proj, o, ltot, dog)


def _mem_fwd(proj, mkv):
    t = proj.shape[0]
    tq = _pick(t, (512, 256))
    m_len = mkv.shape[0]
    scale = 1.0 / math.sqrt(MEM_DH)

    def body(q_ref, z_ref, kv_ref, o_ref, og_ref):
        q = q_ref[...]
        mk = kv_ref[:, :MEM_W].astype(BF16)
        mv = kv_ref[:, MEM_W:].astype(BF16)
        lane = lax.broadcasted_iota(jnp.int32, q.shape, 1) >> 6
        o = jnp.zeros(q.shape, F32)
        for h in range(MEM_HEADS):
            s = _bdot(jnp.where(lane == h, q, 0.0), mk, "nt") * scale
            p = jnp.exp(s - jnp.max(s, axis=-1, keepdims=True))
            p = p / jnp.sum(p, axis=-1, keepdims=True)
            o = o + jnp.where(lane == h, _bdot(p, mv, "nn"), 0.0)
        z = z_ref[...]
        o_ref[...] = o
        og_ref[...] = (o * (z * _sigmoid(z))).astype(BF16)

    out = pl.BlockSpec((tq, MEM_W), lambda i: (i, 0))
    return pl.pallas_call(
        body, name="mem_fwd", grid=(t // tq,),
        in_specs=[pl.BlockSpec((tq, MEM_W), lambda i: (i, C_MQ // MEM_W)),
                  pl.BlockSpec((tq, MEM_W), lambda i: (i, C_MZ // MEM_W)),
                  pl.BlockSpec((m_len, 2 * MEM_W), lambda i: (0, 0))],
        out_specs=[out, out],
        out_shape=[jax.ShapeDtypeStruct((t, MEM_W), F32), jax.ShapeDtypeStruct((t, MEM_W), BF16)],
        compiler_params=_cp(),
    )(proj, proj, mkv)


def _mem_bwd(proj, mkv, o, dog):
    t = proj.shape[0]
    tq = _pick(t, (512, 256))
    m_len = mkv.shape[0]
    scale = 1.0 / math.sqrt(MEM_DH)

    def body(q_ref, z_ref, kv_ref, o_ref, d_ref, dq_ref, dz_ref, dkv_ref):
        @pl.when(pl.program_id(0) == 0)
        def _():
            dkv_ref[...] = jnp.zeros_like(dkv_ref)

        q = q_ref[...]
        z = z_ref[...]
        sg = _sigmoid(z)
        dgo = d_ref[...]
        d_o = dgo * (z * sg)
        dz_ref[...] = (dgo * o_ref[...] * (sg * (1.0 + z * (1.0 - sg)))).astype(BF16)
        mk = kv_ref[:, :MEM_W].astype(BF16)
        mv = kv_ref[:, MEM_W:].astype(BF16)
        lane = lax.broadcasted_iota(jnp.int32, q.shape, 1) >> 6
        klane = lax.broadcasted_iota(jnp.int32, (m_len, MEM_W), 1) >> 6
        dq = jnp.zeros(q.shape, F32)
        dmk = jnp.zeros((m_len, MEM_W), F32)
        dmv = jnp.zeros((m_len, MEM_W), F32)
        for h in range(MEM_HEADS):
            qh = jnp.where(lane == h, q, 0.0)
            doh = jnp.where(lane == h, d_o, 0.0)
            s = _bdot(qh, mk, "nt") * scale
            p = jnp.exp(s - jnp.max(s, axis=-1, keepdims=True))
            p = p / jnp.sum(p, axis=-1, keepdims=True)
            dp = _bdot(doh, mv, "nt")
            ds = p * (dp - jnp.sum(dp * p, axis=-1, keepdims=True)) * scale
            dq = dq + jnp.where(lane == h, _bdot(ds, mk, "nn"), 0.0)
            dmk = dmk + jnp.where(klane == h, _bdot(ds, qh, "tn"), 0.0)
            dmv = dmv + jnp.where(klane == h, _bdot(p, doh, "tn"), 0.0)
        dq_ref[...] = dq.astype(BF16)
        dkv_ref[:, :MEM_W] += dmk
        dkv_ref[:, MEM_W:] += dmv

    blk = pl.BlockSpec((tq, MEM_W), lambda i: (i, 0))
    kv = pl.BlockSpec((m_len, 2 * MEM_W), lambda i: (0, 0))
    return pl.pallas_call(
        body, name="mem_bwd", grid=(t // tq,),
        in_specs=[pl.BlockSpec((tq, MEM_W), lambda i: (i, C_MQ // MEM_W)),
                  pl.BlockSpec((tq, MEM_W), lambda i: (i, C_MZ // MEM_W)), kv, blk, blk],
        out_specs=[blk, blk, kv],
        out_shape=[jax.ShapeDtypeStruct((t, MEM_W), BF16), jax.ShapeDtypeStruct((t, MEM_W), BF16),
                   jax.ShapeDtypeStruct((m_len, 2 * MEM_W), F32)], compiler_params=_cp(),
    )(proj, proj, mkv, o, dog)


_GW = 512


def _merge_fwd(proj, y_dn, y_sb, y_m):
    t = proj.shape[0]
    tb = _pick(t, (256,))
    nc = D_MODEL // _GW

    def body(g1, g2, g3, y1, y2, y3, out_ref):
        out_ref[...] = (_sigmoid(g1[...]) * y1[...] + _sigmoid(g2[...]) * y2[...] + _sigmoid(g3[...]) * y3[...]).astype(BF16)

    def gate(kb):
        return pl.BlockSpec((tb, _GW), lambda i, c: (i, C_GATES // _GW + kb * nc + c))

    blk = pl.BlockSpec((tb, _GW), lambda i, c: (i, c))
    return pl.pallas_call(
        body, name="merge_fwd", grid=(t // tb, nc), in_specs=[gate(0), gate(1), gate(2), blk, blk, blk],
        out_specs=blk, out_shape=jax.ShapeDtypeStruct((t, D_MODEL), BF16), compiler_params=_cp(),
    )(proj, proj, proj, y_dn, y_sb, y_m)


def _merge_bwd(proj, y_dn, y_sb, y_m, dm):
    t = proj.shape[0]
    tb = _pick(t, (256,))
    nc = D_MODEL // _GW

    def body(g1, g2, g3, y1, y2, y3, dm_ref, d1, d2, d3, dg1, dg2, dg3):
        d = dm_ref[...]
        for g, y, dy, dg in ((g1, y1, d1, dg1), (g2, y2, d2, dg2), (g3, y3, d3, dg3)):
            s = _sigmoid(g[...])
            dy[...] = (d * s).astype(BF16)
            dg[...] = (d * y[...] * (s * (1.0 - s))).astype(BF16)

    def gate(kb):
        return pl.BlockSpec((tb, _GW), lambda i, c: (i, C_GATES // _GW + kb * nc + c))

    blk = pl.BlockSpec((tb, _GW), lambda i, c: (i, c))
    act = jax.ShapeDtypeStruct((t, D_MODEL), BF16)
    return pl.pallas_call(
        body, name="merge_bwd", grid=(t // tb, nc), in_specs=[gate(0), gate(1), gate(2), blk, blk, blk, blk],
        out_specs=[blk] * 6, out_shape=[act] * 6, compiler_params=_cp(),
    )(proj, proj, proj, y_dn, y_sb, y_m, dm)


def _final_loss(x, mo, g, tgt):
    t, d = x.shape
    tb = _pick(t, (256,))

    def body(x_ref, mo_ref, g_ref, t_ref, do_ref, dob_ref, loss_ref, dg_ref):
        @pl.when(pl.program_id(0) == 0)
        def _():
            loss_ref[...] = jnp.zeros_like(loss_ref)
            dg_ref[...] = jnp.zeros_like(dg_ref)

        out = x_ref[...] + mo_ref[...]
        r = lax.rsqrt(jnp.mean(out * out, axis=-1, keepdims=True) + NORM_EPS)
        xhat = out * r
        gv = g_ref[...]
        err = xhat * gv - t_ref[...]
        per_tok = jnp.mean(err * err, axis=-1, keepdims=True)
        loss_ref[...] += 0.5 * jnp.sum(per_tok, axis=0, keepdims=True)
        dy = err * (1.0 / d)
        dg_ref[...] += jnp.sum(dy * xhat, axis=0, keepdims=True)
        dxh = dy * gv
        dout = r * (dxh - xhat * jnp.mean(dxh * xhat, axis=-1, keepdims=True))
        do_ref[...] = dout
        dob_ref[...] = dout.astype(BF16)

    row = pl.BlockSpec((tb, d), lambda i: (i, 0))
    vec = pl.BlockSpec((1, d), lambda i: (0, 0))
    return pl.pallas_call(
        body, name="final_loss", grid=(t // tb,), in_specs=[row, row, vec, row],
        out_specs=[row, row, pl.BlockSpec((1, 128), lambda i: (0, 0)), vec],
        out_shape=[jax.ShapeDtypeStruct((t, d), F32), jax.ShapeDtypeStruct((t, d), BF16),
                   jax.ShapeDtypeStruct((1, 128), F32), jax.ShapeDtypeStruct((1, d), F32)],
        compiler_params=_cp(),
    )(x, mo, g, tgt)


def _cast_bf16(a, name, cols_out=None):
    r, c = a.shape
    cols_out = c if cols_out is None else cols_out
    tb = _pick(r, (128, 496))

    def body(a_ref, o_ref):
        if cols_out != c:
            o_ref[...] = jnp.zeros_like(o_ref)
        o_ref[:, :c] = a_ref[...].astype(BF16)

    return pl.pallas_call(
        body, name=name, grid=(r // tb,), in_specs=[pl.BlockSpec((tb, c), lambda i: (i, 0))],
        out_specs=pl.BlockSpec((tb, cols_out), lambda i: (i, 0)),
        out_shape=jax.ShapeDtypeStruct((r, cols_out), BF16), compiler_params=_cp(),
    )(a)


def _pair_add(g, recv, c_idx, name):
    n, r, c = g.shape
    half = r // 2
    tb = _pick(half, (128, 248))
    nb = half // tb

    def body(c_ref, g_ref, r_ref, o_ref):
        o_ref[...] = (g_ref[...].astype(F32) + r_ref[...].astype(F32)).astype(BF16)

    blk = pl.BlockSpec((n, tb, c), lambda i, c_ref: (0, i, 0))
    return pl.pallas_call(
        body, name=name,
        grid_spec=pltpu.PrefetchScalarGridSpec(
            num_scalar_prefetch=1, grid=(nb,),
            in_specs=[pl.BlockSpec((n, tb, c), lambda i, c_ref: (0, c_ref[0] * nb + i, 0)), blk], out_specs=blk),
        out_shape=jax.ShapeDtypeStruct((n, half, c), BF16), compiler_params=_cp(),
    )(c_idx, g, recv)


def _chip_sum(parts, by_chip, place, name):
    n, h, c = parts.shape
    tb = _pick(h, (128, 248))
    nb = h // tb

    def body(p_ref, mine_ref, *rest):
        others, o_ref = rest[:n], rest[n]
        me = jnp.zeros((tb, c), jnp.int32) + p_ref[0]
        acc = None
        for q in range(n):
            term = jnp.where(me == q, mine_ref[...], others[q][...]).astype(F32)
            acc = term if acc is None else acc + term
        o_ref[...] = acc

    def other(q):
        return pl.BlockSpec((None, tb, c), lambda i, p: (jnp.where(p[0] == q, (q + 1) % n, q), i, 0))

    return pl.pallas_call(
        body, name=name,
        grid_spec=pltpu.PrefetchScalarGridSpec(
            num_scalar_prefetch=1, grid=(nb,),
            in_specs=[pl.BlockSpec((None, tb, c), lambda i, p: (p[0], i, 0))] + [other(q) for q in range(n)],
            out_specs=pl.BlockSpec((tb, c), lambda i, p: (p[1] * nb + i, 0))),
        out_shape=jax.ShapeDtypeStruct((2 * h, c), F32), compiler_params=_cp(),
    )(place, parts, *([by_chip] * n))


def _adamw_math(w, g, m, v):
    m = ADAM_B1 * m + (1.0 - ADAM_B1) * g
    v = ADAM_B2 * v + (1.0 - ADAM_B2) * (g * g)
    m_hat = m / (1.0 - ADAM_B1 ** ADAM_STEP)
    v_hat = v / (1.0 - ADAM_B2 ** ADAM_STEP)
    delta = -ADAM_LR * (m_hat / (jnp.sqrt(v_hat) + ADAM_EPS) + ADAM_WD * w)
    return delta, m, v


def _adamw(w, g, m, v, name):
    r, c = w.shape
    cg = g.shape[1]
    tb = _pick(r, (128, 496))

    def body(w_ref, g_ref, m_ref, v_ref, go_ref, d_ref, mo_ref, vo_ref):
        gv = g_ref[:, :c]
        d, mn, vn = _adamw_math(w_ref[...], gv, m_ref[...], v_ref[...])
        go_ref[...] = gv
        d_ref[...] = d
        mo_ref[...] = mn
        vo_ref[...] = vn

    blk = pl.BlockSpec((tb, c), lambda i: (i, 0))
    return pl.pallas_call(
        body, name=name, grid=(r // tb,), in_specs=[blk, pl.BlockSpec((tb, cg), lambda i: (i, 0)), blk, blk],
        out_specs=[blk] * 4, out_shape=[jax.ShapeDtypeStruct((r, c), F32)] * 4, compiler_params=_cp(),
    )(w, g, m, v)


def _small_update(gathered, w, m, v):
    def body(p_ref, w_ref, m_ref, v_ref, g_ref, d_ref, mo_ref, vo_ref):
        g = p_ref[0]
        for i in range(1, N_DEV):
            g = g + p_ref[i]
        d, mn, vn = _adamw_math(w_ref[...], g, m_ref[...], v_ref[...])
        g_ref[...] = g
        d_ref[...] = d
        mo_ref[...] = mn
        vo_ref[...] = vn

    full = pl.BlockSpec((32, 128), lambda i: (0, 0))
    return pl.pallas_call(
        body, name="small_update", grid=(1,),
        in_specs=[pl.BlockSpec((N_DEV, 32, 128), lambda i: (0, 0, 0)), full, full, full], out_specs=[full] * 4,
        out_shape=[jax.ShapeDtypeStruct((32, 128), F32)] * 4, compiler_params=_cp(),
    )(gathered, w, m, v)


_ANY = pl.BlockSpec(memory_space=pl.ANY)


def _place():
    x, y, c = lax.axis_index("x"), lax.axis_index("y"), lax.axis_index("c")
    chips = [(1 - x, y), (x, 1 - y), (1 - x, 1 - y)]
    return x, y, c, chips


def _gather_shards(arrs):
    n = len(arrs)

    def body(*refs):
        ins, outs = refs[:n], refs[n:2 * n]
        send_sems, recv_sems, local_sems = refs[2 * n:2 * n + 3]
        bufs = refs[2 * n + 3:]
        x, y, c, chips = _place()
        me = 2 * x + y
        sibling = (x, y, 1 - c)
        sends = []
        for a in range(n):
            half = ins[a].shape[0] // 2
            mine = pl.ds(pl.multiple_of(c * half, 16), half)
            for j, (qx, qy) in enumerate(chips):
                cp = pltpu.make_async_remote_copy(
                    src_ref=ins[a].at[mine], dst_ref=outs[a].at[me, mine],
                    send_sem=send_sems.at[6 * a + j], recv_sem=recv_sems.at[6 * a + j],
                    device_id=(qx, qy, c), device_id_type=MESH)
                cp.start()
                sends.append(cp)
        for a in range(n):
            step = bufs[a].shape[0]
            for r0 in range(0, ins[a].shape[0], step):
                rows = pl.ds(r0, step)
                load = pltpu.make_async_copy(ins[a].at[rows], bufs[a], local_sems.at[2 * a])
                load.start()
                load.wait()
                store = pltpu.make_async_copy(bufs[a], outs[a].at[me, rows], local_sems.at[2 * a + 1])
                store.start()
                store.wait()
        for a in range(n):
            half = ins[a].shape[0] // 2
            mine = pl.ds(pl.multiple_of(c * half, 16), half)
            for j, (qx, qy) in enumerate(chips):
                q = 2 * qx + qy
                landed = outs[a].at[q, mine]
                pltpu.make_async_remote_copy(
                    src_ref=landed, dst_ref=landed, send_sem=send_sems.at[6 * a + j], recv_sem=recv_sems.at[6 * a + j],
                    device_id=(qx, qy, c), device_id_type=MESH).wait_recv()
                fw = pltpu.make_async_remote_copy(
                    src_ref=landed, dst_ref=landed, send_sem=send_sems.at[6 * a + 3 + j],
                    recv_sem=recv_sems.at[6 * a + 3 + j], device_id=sibling, device_id_type=MESH)
                fw.start()
                sends.append(fw)
        for a in range(n):
            half = ins[a].shape[0] // 2
            theirs = pl.ds(pl.multiple_of((1 - c) * half, 16), half)
            for j, (qx, qy) in enumerate(chips):
                q = 2 * qx + qy
                dst = outs[a].at[q, theirs]
                pltpu.make_async_remote_copy(
                    src_ref=dst, dst_ref=dst, send_sem=send_sems.at[6 * a + 3 + j], recv_sem=recv_sems.at[6 * a + 3 + j],
                    device_id=sibling, device_id_type=MESH).wait_recv()
        for cp in sends:
            cp.wait_send()

    return pl.pallas_call(
        body, name="gather_shards", in_specs=[_ANY] * n, out_specs=[_ANY] * n,
        out_shape=[jax.ShapeDtypeStruct((N_SHARD,) + a.shape, a.dtype) for a in arrs],
        scratch_shapes=[pltpu.SemaphoreType.DMA((6 * n,)), pltpu.SemaphoreType.DMA((6 * n,)),
                        pltpu.SemaphoreType.DMA((2 * n,))]
        + [pltpu.VMEM((_pick(a.shape[0], (256, 496)), a.shape[1]), a.dtype) for a in arrs],
        compiler_params=pltpu.CompilerParams(has_side_effects=True, vmem_limit_bytes=VMEM_LIMIT),
    )(*arrs)


def _pair_reduce_send(grads):
    n = len(grads)

    def body(*refs):
        ins, outs = refs[:n], refs[n:2 * n]
        send_sems, recv_sems = refs[2 * n:]
        x, y, c, _ = _place()
        sibling = (x, y, 1 - c)
        cps = []
        for a in range(n):
            half = ins[a].shape[1] // 2
            theirs = pl.ds(pl.multiple_of((1 - c) * half, 8), half)
            cp = pltpu.make_async_remote_copy(
                src_ref=ins[a].at[:, theirs], dst_ref=outs[a], send_sem=send_sems.at[a], recv_sem=recv_sems.at[a],
                device_id=sibling, device_id_type=MESH)
            cp.start()
            cps.append(cp)
        for cp in cps:
            cp.wait()

    return pl.pallas_call(
        body, name="pair_reduce_send", in_specs=[_ANY] * n, out_specs=[_ANY] * n,
        out_shape=[jax.ShapeDtypeStruct((g.shape[0], g.shape[1] // 2, g.shape[2]), g.dtype) for g in grads],
        scratch_shapes=[pltpu.SemaphoreType.DMA((n,)), pltpu.SemaphoreType.DMA((n,))],
        compiler_params=pltpu.CompilerParams(has_side_effects=True),
    )(*grads)


def _chip_exchange(parts):
    n = len(parts)

    def body(*refs):
        ins, outs = refs[:n], refs[n:2 * n]
        send_sems, recv_sems = refs[2 * n:]
        x, y, c, chips = _place()
        me = 2 * x + y
        cps = []
        for a in range(n):
            for j, (qx, qy) in enumerate(chips):
                q = 2 * qx + qy
                cp = pltpu.make_async_remote_copy(
                    src_ref=ins[a].at[q], dst_ref=outs[a].at[me], send_sem=send_sems.at[3 * a + j],
                    recv_sem=recv_sems.at[3 * a + j], device_id=(qx, qy, c), device_id_type=MESH)
                cp.start()
                cps.append(cp)
        for a in range(n):
            for j, (qx, qy) in enumerate(chips):
                q = 2 * qx + qy
                dst = outs[a].at[q]
                pltpu.make_async_remote_copy(
                    src_ref=dst, dst_ref=dst, send_sem=send_sems.at[3 * a + j], recv_sem=recv_sems.at[3 * a + j],
                    device_id=(qx, qy, c), device_id_type=MESH).wait_recv()
        for cp in cps:
            cp.wait_send()

    return pl.pallas_call(
        body, name="chip_exchange", in_specs=[_ANY] * n, out_specs=[_ANY] * n,
        out_shape=[jax.ShapeDtypeStruct(p.shape, p.dtype) for p in parts],
        scratch_shapes=[pltpu.SemaphoreType.DMA((3 * n,)), pltpu.SemaphoreType.DMA((3 * n,))],
        compiler_params=pltpu.CompilerParams(has_side_effects=True),
    )(*parts)


def _pair_allgather(fulls):
    n = len(fulls)

    def body(*refs):
        outs = refs[n:2 * n]
        send_sems, recv_sems = refs[2 * n:]
        x, y, c, _ = _place()
        sibling = (x, y, 1 - c)
        cps = []
        for a in range(n):
            half = outs[a].shape[0] // 2
            mine = outs[a].at[pl.ds(pl.multiple_of(c * half, 8), half)]
            cp = pltpu.make_async_remote_copy(
                src_ref=mine, dst_ref=mine, send_sem=send_sems.at[a], recv_sem=recv_sems.at[a],
                device_id=sibling, device_id_type=MESH)
            cp.start()
            cps.append(cp)
        for a in range(n):
            half = outs[a].shape[0] // 2
            theirs = outs[a].at[pl.ds(pl.multiple_of((1 - c) * half, 8), half)]
            pltpu.make_async_remote_copy(
                src_ref=theirs, dst_ref=theirs, send_sem=send_sems.at[a], recv_sem=recv_sems.at[a],
                device_id=sibling, device_id_type=MESH).wait_recv()
        for cp in cps:
            cp.wait_send()

    return pl.pallas_call(
        body, name="pair_allgather", in_specs=[_ANY] * n, out_specs=[_ANY] * n,
        out_shape=[jax.ShapeDtypeStruct(f.shape, f.dtype) for f in fulls],
        input_output_aliases={a: a for a in range(n)},
        scratch_shapes=[pltpu.SemaphoreType.DMA((n,)), pltpu.SemaphoreType.DMA((n,))],
        compiler_params=pltpu.CompilerParams(has_side_effects=True),
    )(*fulls)


def _allgather_small(slab):
    def body(s_ref, out_ref, send_sems, recv_sems):
        x, y, c, _ = _place()
        me = 4 * x + 2 * y + c
        out_ref[me] = s_ref[...]
        cps = []
        for mask in range(1, N_DEV):
            peer = (x ^ (mask >> 2), y ^ ((mask >> 1) & 1), c ^ (mask & 1))
            cp = pltpu.make_async_remote_copy(
                src_ref=s_ref, dst_ref=out_ref.at[me], send_sem=send_sems.at[mask - 1], recv_sem=recv_sems.at[mask - 1],
                device_id=peer, device_id_type=MESH)
            cp.start()
            cps.append(cp)
        for mask in range(1, N_DEV):
            peer = (x ^ (mask >> 2), y ^ ((mask >> 1) & 1), c ^ (mask & 1))
            dst = out_ref.at[4 * peer[0] + 2 * peer[1] + peer[2]]
            pltpu.make_async_remote_copy(
                src_ref=dst, dst_ref=dst, send_sem=send_sems.at[mask - 1], recv_sem=recv_sems.at[mask - 1],
                device_id=peer, device_id_type=MESH).wait_recv()
        for cp in cps:
            cp.wait_send()

    vm = pl.BlockSpec(memory_space=pltpu.VMEM)
    return pl.pallas_call(
        body, name="allgather_small", in_specs=[vm], out_specs=vm,
        out_shape=jax.ShapeDtypeStruct((N_DEV,) + slab.shape, slab.dtype),
        scratch_shapes=[pltpu.SemaphoreType.DMA((N_DEV - 1,)), pltpu.SemaphoreType.DMA((N_DEV - 1,))],
        compiler_params=pltpu.CompilerParams(has_side_effects=True),
    )(slab)


def _pack_b(w_mem_kv, w_br_dn, w_br_sb, w_br_mem, w_out, conv_w):
    parts = [w_mem_kv.reshape(128, D_MODEL), w_br_dn, w_br_sb, w_br_mem.reshape(64, D_MODEL), w_out,
             conv_w.reshape(3, D_MODEL)]
    rows = sum(p.shape[0] for p in parts)
    return jnp.concatenate(parts + [jnp.zeros((B_ROWS - rows, D_MODEL), parts[0].dtype)], axis=0)


def _unpack_b(slab):
    return (slab[B_MEMKV:B_BRDN].reshape(1, 256, 512), slab[B_BRDN:B_BRSB].reshape(1, 256, D_MODEL),
            slab[B_BRSB:B_BRMEM].reshape(1, 256, D_MODEL), slab[B_BRMEM:B_OUT].reshape(1, 256, 256),
            slab[B_OUT:B_CONV].reshape(1, 256, D_MODEL), slab[B_CONV:B_CONV + 3].reshape(1, 4, 768))


def _pack_small(norm_g, mem_norm_g, final_g, dn_norm_g, a_log, dt_bias, loss=None):
    slab = jnp.zeros((32, 128), F32)
    slab = slab.at[S_NORM:S_NORM + 8].set(norm_g.reshape(8, 128))
    slab = slab.at[S_MEMNORM:S_MEMNORM + 8].set(mem_norm_g.reshape(8, 128))
    slab = slab.at[S_FINAL:S_FINAL + 8].set(final_g.reshape(8, 128))
    slab = slab.at[S_DNNORM].set(dn_norm_g.reshape(128))
    slab = slab.at[S_ALOG, :N_HEADS].set(a_log.reshape(N_HEADS))
    slab = slab.at[S_DTB, :N_HEADS].set(dt_bias.reshape(N_HEADS))
    if loss is not None:
        slab = slab.at[S_LOSS, 0].set(loss)
    return slab


def _unpack_small(slab):
    return (slab[S_NORM:S_NORM + 8].reshape(1, D_MODEL), slab[S_MEMNORM:S_MEMNORM + 8].reshape(1, D_MODEL),
            slab[S_FINAL:S_FINAL + 8].reshape(D_MODEL), slab[S_DNNORM].reshape(1, 128),
            slab[S_ALOG, :N_HEADS].reshape(1, N_HEADS), slab[S_DTB, :N_HEADS].reshape(1, N_HEADS))


def _reorder_w_in(w_full):
    pad = jnp.zeros((w_full.shape[0], W_R - IN_WIDTH), w_full.dtype)
    return jnp.concatenate([w_full[:, :4096], w_full[:, 4112:], w_full[:, 4096:4112], pad], axis=1)


def _restore_w_in(g_r):
    return jnp.concatenate([g_r[:, :4096], g_r[:, C_BA:C_BA + 16], g_r[:, 4096:C_BA]], axis=1)


def _local_step(x, mem, tgt, norm_g, mem_norm_g, w_r, w_sh, conv_w, a_log, dt_bias, dn_norm_g, w_mem_kv, w_br_dn,
                w_br_sb, w_br_mem, w_out, final_g):
    t = x.shape[0]
    final_row = final_g.reshape(1, D_MODEL)
    alog_row = jnp.zeros((1, 128), F32).at[0, N_HEADS:2 * N_HEADS].set(a_log.reshape(N_HEADS))
    dtb_row = jnp.zeros((1, 128), F32).at[0, N_HEADS:2 * N_HEADS].set(dt_bias.reshape(N_HEADS))

    h = _rmsnorm_fwd(x, norm_g, "norm_fwd")
    proj = _mm(h, w_r, "nn", "in_proj")
    qkv = _dn_prep_fwd(proj, conv_w)
    beta_t, g_t = _dn_gate_fwd(proj, alog_row, dtb_row)
    beta8 = beta_t[:, :N_HEADS].T.reshape(N_HEADS, t, 1)
    g8 = g_t[:, N_HEADS:2 * N_HEADS].T.reshape(N_HEADS, t, 1)
    dn_u, dn_w, dn_qd, dn_kd, dn_a, tinv_all, dn_el = _dn_intra_fwd(qkv, beta8, g8)
    o_dn, dn_vn, s_all = _dn_scan_fwd(dn_u, dn_w, dn_qd, dn_kd, dn_a, dn_el)
    o_dn_g = _dn_post_fwd(o_dn, proj, dn_norm_g)
    o_sb, o_sb_g, sb_l = _sb_fwd(proj)
    mem_n = _rmsnorm_fwd(mem, mem_norm_g, "mem_norm_fwd")
    mkv = _mm(mem_n, w_mem_kv, "nn", "mem_kv")
    o_m, o_m_g = _mem_fwd(proj, mkv)
    y_dn = _mm(o_dn_g, w_br_dn, "nn", "br_dn")
    y_sb = _mm(o_sb_g, w_br_sb, "nn", "br_sb")
    y_m = _mm(o_m_g, w_br_mem, "nn", "br_mem")
    merged = _merge_fwd(proj, y_dn, y_sb, y_m)
    mo = _mm(merged, w_out, "nn", "out_proj")
    d_out, d_out_b, loss_row, g_final = _final_loss(x, mo, final_row, tgt)

    g_w_out = _mm(merged, d_out_b, "tn", "g_w_out")
    d_merged = _mm(d_out_b, w_out, "nt", "d_merged")
    dy_dn, dy_sb, dy_m, dg1, dg2, dg3 = _merge_bwd(proj, y_dn, y_sb, y_m, d_merged)
    g_w_br_dn = _mm(o_dn_g, dy_dn, "tn", "g_w_br_dn")
    g_w_br_sb = _mm(o_sb_g, dy_sb, "tn", "g_w_br_sb")
    g_w_br_mem = _mm(o_m_g, dy_m, "tn", "g_w_br_mem")
    d_o_dn_g = _mm(dy_dn, w_br_dn, "nt", "d_o_dn")
    d_o_sb_g = _mm(dy_sb, w_br_sb, "nt", "d_o_sb")
    d_o_m_g = _mm(dy_m, w_br_mem, "nt", "d_o_mem")

    d_mq, d_mz, d_mkv = _mem_bwd(proj, mkv, o_m, d_o_m_g)
    d_mkv_b = _cast_bf16(d_mkv, "cast_dmkv")
    g_w_mem_kv = _mm(mem_n, d_mkv_b, "tn", "g_w_mem_kv")
    d_mem_n = _mm(d_mkv_b, w_mem_kv, "nt", "d_mem_n")
    _, g_mem_norm = _rmsnorm_bwd(mem, mem_norm_g, d_mem_n, jnp.zeros_like(mem), "mem_norm_bwd")

    d_sq, d_sk, d_sv, d_sz = _sb_bwd(proj, o_sb, sb_l, d_o_sb_g)

    d_o_dn, d_dnz, g_dn_norm = _dn_post_bwd(o_dn, proj, dn_norm_g, d_o_dn_g)
    d_vnew, d_kd, d_qd, d_w, d_el = _dn_scan_bwd(dn_w, dn_qd, dn_kd, dn_a, dn_el, dn_vn, s_all, d_o_dn)
    d_qn, d_kn, d_vn, d_beta8, d_g8 = _dn_intra_bwd(qkv, beta8, g8, tinv_all, dn_vn, d_o_dn, d_vnew, d_kd, d_qd, d_w, d_el)
    d_conv_in, g_conv = _dn_prep_bwd(proj, conv_w, d_qn, d_kn, d_vn)
    pad = jnp.zeros((t, 128 - N_HEADS), F32)
    dbeta_t = jnp.concatenate([d_beta8.reshape(N_HEADS, t).T, pad], axis=1)
    dg_t = jnp.concatenate([pad[:, :N_HEADS], d_g8.reshape(N_HEADS, t).T, pad[:, :128 - 2 * N_HEADS]], axis=1)
    d_ba, g_alog_row, g_dtb_row = _dn_gate_bwd(proj, alog_row, dtb_row, dbeta_t, dg_t)

    dproj = jnp.concatenate([d_conv_in, d_dnz, d_ba[:, :2 * N_HEADS], d_sq, d_sk, d_sv, d_sz, d_mq, d_mz, dg1, dg2, dg3],
                            axis=1)
    pad = jnp.zeros((t, SHARD_PAD - SHARD_W), dproj.dtype)
    dproj_sh = jnp.concatenate(
        [piece for s in range(N_SHARD) for piece in (dproj[:, SHARD_W * s:SHARD_W * (s + 1)], pad)], axis=1)
    g_w_sh = _mm(h, dproj_sh, "tn", "g_w_in", out_dtype=BF16, out_shards=N_SHARD)
    dh = _mm(dproj_sh, w_sh, "nt", "d_h")
    grad_x, g_norm = _rmsnorm_bwd(x, norm_g, dh, d_out, "norm_bwd")

    small = dict(norm_g=g_norm, mem_norm_g=g_mem_norm, final_g=g_final, dn_norm_g=g_dn_norm,
                 a_log=g_alog_row[:, N_HEADS:2 * N_HEADS], dt_bias=g_dtb_row[:, N_HEADS:2 * N_HEADS])
    big = dict(w_sh=g_w_sh, conv_w=g_conv, w_mem_kv=g_w_mem_kv, w_br_dn=g_w_br_dn, w_br_sb=g_w_br_sb,
               w_br_mem=g_w_br_mem, w_out=g_w_out)
    return loss_row[0, 0], grad_x, small, big


def _reduce_scatter(grads):
    x, y, c = lax.axis_index("x"), lax.axis_index("y"), lax.axis_index("c")
    core = jnp.reshape(c, (1,)).astype(jnp.int32)
    place = jnp.stack([2 * x + y, c]).astype(jnp.int32)
    recv = _pair_reduce_send(grads)
    parts = [_pair_add(g, r, core, "pair_add") for g, r in zip(grads, recv)]
    by_chip = _chip_exchange(parts)
    fulls = [_chip_sum(p, b, place, "chip_sum") for p, b in zip(parts, by_chip)]
    return _pair_allgather(fulls)


def kernel(x, mem, norm_g, mem_norm_g, w_in, conv_w, a_log, dt_bias, dn_norm_g, w_mem_kv, w_br_dn, w_br_sb, w_br_mem, w_out, final_g, loss_target, m_norm_g, m_mem_norm_g, m_w_in, m_conv_w, m_a_log, m_dt_bias, m_dn_norm_g, m_w_mem_kv, m_w_br_dn, m_w_br_sb, m_w_br_mem, m_w_out, m_final_g, v_norm_g, v_mem_norm_g, v_w_in, v_conv_w, v_a_log, v_dt_bias, v_dn_norm_g, v_w_mem_kv, v_w_br_dn, v_w_br_sb, v_w_br_mem, v_w_out, v_final_g):
    w_a = w_in[0]
    w_b = _pack_b(w_mem_kv[0], w_br_dn[0], w_br_sb[0], w_br_mem[0], w_out[0], conv_w[0])
    m_b = _pack_b(m_w_mem_kv[0], m_w_br_dn[0], m_w_br_sb[0], m_w_br_mem[0], m_w_out[0], m_conv_w[0])
    v_b = _pack_b(v_w_mem_kv[0], v_w_br_dn[0], v_w_br_sb[0], v_w_br_mem[0], v_w_out[0], v_conv_w[0])

    ga, gb = _gather_shards([_cast_bf16(w_a, "cast_w_in", SHARD_PAD), _cast_bf16(w_b, "cast_w_b")])
    w_r = _reorder_w_in(jnp.concatenate([ga[s, :, :SHARD_W] for s in range(N_SHARD)], axis=1))
    f_mem_kv = gb[:, B_MEMKV:B_BRDN].reshape(N_SHARD * 256, 512)
    f_br_dn = gb[:, B_BRDN:B_BRSB].reshape(N_SHARD * 256, D_MODEL)
    f_br_sb = gb[:, B_BRSB:B_BRMEM].reshape(N_SHARD * 256, D_MODEL)
    f_br_mem = gb[:, B_BRMEM:B_OUT].reshape(N_SHARD, 256, 256).transpose(1, 0, 2).reshape(256, D_MODEL)
    f_out = gb[:, B_OUT:B_CONV].reshape(N_SHARD * 256, D_MODEL)
    f_conv = gb[:, B_CONV:B_CONV + 3].reshape(N_SHARD, 4, 768).transpose(1, 0, 2).reshape(4, 3 * D_MODEL).astype(F32)

    loss, grad_x, small, big = _local_step(
        x[0], mem[0], loss_target[0], norm_g, mem_norm_g, w_r, ga, f_conv, a_log, dt_bias, dn_norm_g,
        f_mem_kv, f_br_dn, f_br_sb, f_br_mem, f_out, final_g)

    g_b = jnp.stack([
        _pack_b(big["w_mem_kv"][256 * s:256 * (s + 1)], big["w_br_dn"][256 * s:256 * (s + 1)],
                big["w_br_sb"][256 * s:256 * (s + 1)], big["w_br_mem"][:, 256 * s:256 * (s + 1)],
                big["w_out"][256 * s:256 * (s + 1)], big["conv_w"][:, 768 * s:768 * (s + 1)])
        for s in range(N_SHARD)]).astype(BF16)
    gs_in, gs_b = _reduce_scatter([big["w_sh"], g_b])

    gr_in, d_in, nm_in, nv_in = _adamw(w_a, gs_in, m_w_in[0], v_w_in[0], "adamw_w_in")
    gr_b, d_b, nm_b, nv_b = _adamw(w_b, gs_b, m_b, v_b, "adamw_b")

    part = _pack_small(small["norm_g"], small["mem_norm_g"], small["final_g"], small["dn_norm_g"],
                       small["a_log"], small["dt_bias"], loss)
    w_s = _pack_small(norm_g, mem_norm_g, final_g, dn_norm_g, a_log, dt_bias)
    m_s = _pack_small(m_norm_g, m_mem_norm_g, m_final_g, m_dn_norm_g, m_a_log, m_dt_bias)
    v_s = _pack_small(v_norm_g, v_mem_norm_g, v_final_g, v_dn_norm_g, v_a_log, v_dt_bias)
    g_s, d_s, nm_s, nv_s = _small_update(_allgather_small(part), w_s, m_s, v_s)

    def assemble(slab_small, a_in, slab_b):
        s_norm, s_memnorm, s_final, s_dnnorm, s_alog, s_dtb = _unpack_small(slab_small)
        b_memkv, b_brdn, b_brsb, b_brmem, b_out, b_conv = _unpack_b(slab_b)
        return [s_norm, s_memnorm, a_in.reshape(1, D_MODEL, IN_WIDTH // N_SHARD), b_conv, s_alog, s_dtb, s_dnnorm,
                b_memkv, b_brdn, b_brsb, b_brmem, b_out, s_final]

    outs = [g_s[S_LOSS, 0], grad_x.reshape(1, -1, D_MODEL)]
    outs += assemble(g_s, gr_in, gr_b)
    outs += assemble(d_s, d_in, d_b)
    outs += assemble(nm_s, nm_in, nm_b)
    outs += assemble(nv_s, nv_in, nv_b)
    return tuple(outs)
```

```python
import functools
import math

import jax
import jax.numpy as jnp
from jax import lax
from jax.experimental import pallas as pl
from jax.experimental.pallas import tpu as pltpu

F32 = jnp.float32
BF16 = jnp.bfloat16
MESH = pl.DeviceIdType.MESH
HIGHEST = lax.Precision.HIGHEST

D_MODEL = 1024
N_HEADS = 8
D_HEAD = 128
DN_CHUNK = 64
DN_GROUP = 4
DN_HEADS_PER_STEP = 2
SB_BLOCK = 128
SB_HEADS_PER_STEP = 2
SB_QBLOCK = 256
MEM_HEADS = 4
MEM_DH = 64
MEM_W = MEM_HEADS * MEM_DH
NORM_EPS = 1e-6
IN_WIDTH = 11792
N_SHARD = 4
SHARD_W = IN_WIDTH // N_SHARD
SHARD_PAD = 3072
N_DEV = 8

C_DNZ = 3072
C_SBQ = 4096
C_SBZ = 7168
C_MQ = 8192
C_MZ = 8448
C_GATES = 8704
C_BA = 11776
W_R = 11904

ADAM_LR = 0.001
ADAM_B1 = 0.9
ADAM_B2 = 0.999
ADAM_EPS = 1e-08
ADAM_WD = 0.01
ADAM_STEP = 10

VMEM_LIMIT = 56 * 1024 * 1024

B_ROWS = 992
B_MEMKV, B_BRDN, B_BRSB, B_BRMEM, B_OUT, B_CONV = 0, 128, 384, 640, 704, 960
S_NORM, S_MEMNORM, S_FINAL, S_DNNORM, S_ALOG, S_DTB, S_LOSS = 0, 8, 16, 24, 25, 26, 27


def _cp(**kw):
    return pltpu.CompilerParams(vmem_limit_bytes=VMEM_LIMIT, **kw)


def _dot(a, b, dims):
    lead = a.ndim - 2
    ca, cb = {"nn": (1, 0), "nt": (1, 1), "tn": (0, 0)}[dims]
    batch = tuple(range(lead))
    return lax.dot_general(a, b, (((ca + lead,), (cb + lead,)), (batch, batch)), preferred_element_type=F32)


def _chunks(x):
    return x.reshape(x.shape[0] // DN_CHUNK, DN_CHUNK, x.shape[1])


def _unchunk(x):
    return x.reshape(x.shape[0] * x.shape[1], x.shape[2])


def _bdot(a, b, dims):
    return _dot(a.astype(BF16), b.astype(BF16), dims)


def _split(a):
    hi = a.astype(BF16)
    return hi, (a - hi.astype(F32)).astype(BF16)


def _dot3(a, b, dims):
    a1, a2 = _split(a)
    b1, b2 = _split(b)
    return _dot(a1, b1, dims) + (_dot(a1, b2, dims) + _dot(a2, b1, dims))


def _split_dot(a, ones_bf16):
    hi, lo = _split(a.reshape(-1, a.shape[-1]))
    out = _dot(hi, ones_bf16, "nn") + _dot(lo, ones_bf16, "nn")
    return out.reshape(a.shape[:-1] + (ones_bf16.shape[1],))


def _sigmoid(x):
    return 1.0 / (1.0 + jnp.exp(-x))


def _log1p_small(u):
    return jnp.where(u < 1e-2, u * (1.0 - u * (0.5 - u * (1.0 / 3.0))), jnp.log(1.0 + u))


def _log_sigmoid(z):
    return jnp.minimum(z, 0.0) - _log1p_small(jnp.exp(-jnp.abs(z)))


def _pick(dim, cands):
    for c in cands:
        if dim % c == 0:
            return c
    return dim


def _mm(a, b, dims, name, out_dtype=F32, out_shards=1):
    ta, tb = dims[0] == "t", dims[1] == "t"
    m, k = (a.shape[1], a.shape[0]) if ta else a.shape
    b_shards = b.shape[0] if b.ndim == 3 else 1
    n = b.shape[-2] if tb else b.shape[-1]
    tm = _pick(m, (1024, 512, 256))
    tn = _pick(n // out_shards, (512, 384, 256, 128))
    tk = _pick(k // b_shards, (1024, 512, 384, 256))
    nk = k // tk

    def body(a_ref, b_ref, o_ref, acc_ref):
        kk = pl.program_id(2)

        @pl.when(kk == 0)
        def _():
            acc_ref[...] = jnp.zeros_like(acc_ref)

        acc_ref[...] += _bdot(a_ref[...], b_ref[...], dims)

        @pl.when(kk == nk - 1)
        def _():
            o_ref[...] = acc_ref[...].astype(out_dtype)

    a_spec = pl.BlockSpec((tk, tm), lambda i, j, q: (q, i)) if ta else pl.BlockSpec((tm, tk), lambda i, j, q: (i, q))
    if b_shards > 1:
        per_k = k // b_shards // tk
        b_spec = pl.BlockSpec((None, tn, tk), lambda i, j, q: (q // per_k, j, q % per_k))
    else:
        b_spec = pl.BlockSpec((tn, tk), lambda i, j, q: (j, q)) if tb else pl.BlockSpec((tk, tn), lambda i, j, q: (q, j))
    if out_shards > 1:
        per_n = n // out_shards // tn
        out_spec = pl.BlockSpec((None, tm, tn), lambda i, j, q: (j // per_n, i, j % per_n))
        out_shape = jax.ShapeDtypeStruct((out_shards, m, n // out_shards), out_dtype)
    else:
        out_spec = pl.BlockSpec((tm, tn), lambda i, j, q: (i, j))
        out_shape = jax.ShapeDtypeStruct((m, n), out_dtype)
    return pl.pallas_call(
        body, name=name, grid=(m // tm, n // tn, nk),
        in_specs=[a_spec, b_spec], out_specs=out_spec, out_shape=out_shape,
        scratch_shapes=[pltpu.VMEM((tm, tn), F32)],
        compiler_params=_cp(dimension_semantics=("parallel", "parallel", "arbitrary")),
    )(a, b)


def _rmsnorm_fwd(x, g, name):
    t, d = x.shape
    tb = _pick(t, (512, 256))

    def body(x_ref, g_ref, h_ref):
        xv = x_ref[...]
        r = lax.rsqrt(jnp.mean(xv * xv, axis=-1, keepdims=True) + NORM_EPS)
        h_ref[...] = ((xv * r) * g_ref[...]).astype(BF16)

    return pl.pallas_call(
        body, name=name, grid=(t // tb,),
        in_specs=[pl.BlockSpec((tb, d), lambda i: (i, 0)), pl.BlockSpec((1, d), lambda i: (0, 0))],
        out_specs=pl.BlockSpec((tb, d), lambda i: (i, 0)),
        out_shape=jax.ShapeDtypeStruct((t, d), BF16), compiler_params=_cp(),
    )(x, g)


def _rmsnorm_bwd(x, g, dh, resid, name):
    t, d = x.shape
    tb = _pick(t, (256,))

    def body(x_ref, g_ref, dh_ref, r_ref, dx_ref, dg_ref):
        @pl.when(pl.program_id(0) == 0)
        def _():
            dg_ref[...] = jnp.zeros_like(dg_ref)

        xv = x_ref[...]
        r = lax.rsqrt(jnp.mean(xv * xv, axis=-1, keepdims=True) + NORM_EPS)
        xhat = xv * r
        dhv = dh_ref[...]
        dg_ref[...] += jnp.sum(dhv * xhat, axis=0, keepdims=True)
        dxh = dhv * g_ref[...]
        dx_ref[...] = r_ref[...] + r * (dxh - xhat * jnp.mean(dxh * xhat, axis=-1, keepdims=True))

    row = pl.BlockSpec((tb, d), lambda i: (i, 0))
    vec = pl.BlockSpec((1, d), lambda i: (0, 0))
    return pl.pallas_call(
        body, name=name, grid=(t // tb,), in_specs=[row, vec, row, row], out_specs=[row, vec],
        out_shape=[jax.ShapeDtypeStruct((t, d), F32), jax.ShapeDtypeStruct((1, d), F32)], compiler_params=_cp(),
    )(x, g, dh, resid)


def _conv_silu(xv, w, row):
    y = xv * w[3:4, :]
    for s in (1, 2, 3):
        xs = jnp.where(row >= s, pltpu.roll(xv, s, 0), 0.0)
        y = y + xs * w[3 - s:4 - s, :]
    return y, y * _sigmoid(y)


def _dn_prep_fwd(proj, conv_w):
    t = proj.shape[0]

    def body(p_ref, w_ref, o_ref):
        j = pl.program_id(0)
        xv = p_ref[...]
        row = lax.broadcasted_iota(jnp.int32, xv.shape, 0)
        _, a = _conv_silu(xv, w_ref[...], row)
        inv = lax.rsqrt(jnp.sum(a * a, axis=-1, keepdims=True) + NORM_EPS)
        scale = jnp.where(j < N_HEADS, D_HEAD ** -0.5, 1.0)
        normed = jnp.where(j < 2 * N_HEADS, 1.0, 0.0)
        o_ref[...] = a * (normed * (inv * scale) + (1.0 - normed))

    return pl.pallas_call(
        body, name="dn_prep_fwd", grid=(3 * N_HEADS,),
        in_specs=[pl.BlockSpec((t, D_HEAD), lambda j: (0, j)), pl.BlockSpec((4, D_HEAD), lambda j: (0, j))],
        out_specs=pl.BlockSpec((t, D_HEAD), lambda j: (0, j)),
        out_shape=jax.ShapeDtypeStruct((t, 3 * D_MODEL), F32), compiler_params=_cp(),
    )(proj, conv_w)


def _dn_prep_bwd(proj, conv_w, dq, dk, dv):
    t = proj.shape[0]

    def body(p_ref, w_ref, dq_ref, dk_ref, dv_ref, dp_ref, dw_ref):
        j = pl.program_id(0)
        xv = p_ref[...]
        w = w_ref[...]
        row = lax.broadcasted_iota(jnp.int32, xv.shape, 0)
        y, a = _conv_silu(xv, w, row)
        part = jnp.zeros(xv.shape, jnp.int32) + j // N_HEADS
        dn = jnp.where(part == 0, dq_ref[...], jnp.where(part == 1, dk_ref[...], dv_ref[...]))
        inv = lax.rsqrt(jnp.sum(a * a, axis=-1, keepdims=True) + NORM_EPS)
        scale = jnp.where(j < N_HEADS, D_HEAD ** -0.5, 1.0)
        ds = dn * scale
        da_norm = inv * ds - a * (inv * inv * inv) * jnp.sum(ds * a, axis=-1, keepdims=True)
        normed = jnp.where(j < 2 * N_HEADS, 1.0, 0.0)
        da = normed * da_norm + (1.0 - normed) * dn
        s = _sigmoid(y)
        dy = da * (s * (1.0 + y * (1.0 - s)))
        dx = dy * w[3:4, :]
        dw_ref[3:4, :] = jnp.sum(dy * xv, axis=0, keepdims=True)
        for sft in (1, 2, 3):
            xs = jnp.where(row >= sft, pltpu.roll(xv, sft, 0), 0.0)
            dw_ref[3 - sft:4 - sft, :] = jnp.sum(dy * xs, axis=0, keepdims=True)
            dys = jnp.where(row < t - sft, pltpu.roll(dy, t - sft, 0), 0.0)
            dx = dx + dys * w[3 - sft:4 - sft, :]
        dp_ref[...] = dx.astype(BF16)

    blk = pl.BlockSpec((t, D_HEAD), lambda j: (0, j))
    wblk = pl.BlockSpec((4, D_HEAD), lambda j: (0, j))

    def grad(part):
        return pl.BlockSpec((t, D_HEAD), lambda j: (0, jnp.clip(j - part * N_HEADS, 0, N_HEADS - 1)))

    return pl.pallas_call(
        body, name="dn_prep_bwd", grid=(3 * N_HEADS,), in_specs=[blk, wblk, grad(0), grad(1), grad(2)],
        out_specs=[blk, wblk],
        out_shape=[jax.ShapeDtypeStruct((t, 3 * D_MODEL), BF16), jax.ShapeDtypeStruct((4, 3 * D_MODEL), F32)],
        compiler_params=_cp(),
    )(proj, conv_w, dq, dk, dv)


def _softplus_parts(xv):
    e = jnp.exp(-jnp.abs(xv))
    return jnp.maximum(xv, 0.0) + _log1p_small(e)


def _chunk_scan(v, row, reverse):
    t = v.shape[0]
    pos = row & (DN_CHUNK - 1)
    s = 1
    while s < DN_CHUNK:
        if reverse:
            v = v + jnp.where(pos < DN_CHUNK - s, pltpu.roll(v, t - s, 0), 0.0)
        else:
            v = v + jnp.where(pos >= s, pltpu.roll(v, s, 0), 0.0)
        s *= 2
    return v


def _dn_gate_fwd(proj, alog_row, dtb_row):
    t = proj.shape[0]

    def body(p_ref, al_ref, dt_ref, b_ref, g_ref):
        p = p_ref[...]
        row = lax.broadcasted_iota(jnp.int32, p.shape, 0)
        b_ref[...] = _sigmoid(p)
        g = -jnp.exp(al_ref[...]) * _softplus_parts(p + dt_ref[...])
        g_ref[...] = _chunk_scan(g, row, reverse=False)

    blk = pl.BlockSpec((t, 128), lambda i: (0, C_BA // 128))
    vec = pl.BlockSpec((1, 128), lambda i: (0, 0))
    out = pl.BlockSpec((t, 128), lambda i: (0, 0))
    return pl.pallas_call(
        body, name="dn_gate_fwd", grid=(1,), in_specs=[blk, vec, vec], out_specs=[out, out],
        out_shape=[jax.ShapeDtypeStruct((t, 128), F32)] * 2, compiler_params=_cp(),
    )(proj, alog_row, dtb_row)


def _dn_gate_bwd(proj, alog_row, dtb_row, dbeta, dgc):
    t = proj.shape[0]

    def body(p_ref, al_ref, dt_ref, db_ref, dg_ref, dp_ref, dal_ref, ddt_ref):
        p = p_ref[...]
        row = lax.broadcasted_iota(jnp.int32, p.shape, 0)
        lane = lax.broadcasted_iota(jnp.int32, p.shape, 1)
        s = _sigmoid(p)
        d_b = db_ref[...] * s * (1.0 - s)
        dg = _chunk_scan(dg_ref[...], row, reverse=True)
        xa = p + dt_ref[...]
        ea = jnp.exp(al_ref[...])
        g = -ea * _softplus_parts(xa)
        d_a = dg * (-ea) * _sigmoid(xa)
        dp_ref[...] = jnp.where(lane < N_HEADS, d_b, jnp.where(lane < 2 * N_HEADS, d_a, 0.0)).astype(BF16)
        dal_ref[...] = jnp.sum(dg * g, axis=0, keepdims=True)
        ddt_ref[...] = jnp.sum(d_a, axis=0, keepdims=True)

    blk = pl.BlockSpec((t, 128), lambda i: (0, C_BA // 128))
    vec = pl.BlockSpec((1, 128), lambda i: (0, 0))
    full = pl.BlockSpec((t, 128), lambda i: (0, 0))
    return pl.pallas_call(
        body, name="dn_gate_bwd", grid=(1,), in_specs=[blk, vec, vec, full, full], out_specs=[full, vec, vec],
        out_shape=[jax.ShapeDtypeStruct((t, 128), BF16), jax.ShapeDtypeStruct((1, 128), F32),
                   jax.ShapeDtypeStruct((1, 128), F32)], compiler_params=_cp(),
    )(proj, alog_row, dtb_row, dbeta, dgc)


def _col_to_row(col, eye):
    return jnp.sum(jnp.where(eye, col, 0.0), axis=-2, keepdims=True)


def _row_to_col(rowv, eye):
    return jnp.sum(jnp.where(eye, rowv, 0.0), axis=-1, keepdims=True)


def _tri_inverse(m, ri, ci):
    eye = (ri == ci).astype(F32)
    b16 = (ri >> 4) == (ci >> 4)
    b32 = (ri >> 5) == (ci >> 5)
    m1 = jnp.where(b16, m, 0.0)
    x = eye - m1
    p = _dot3(m1, m1, "nn")
    x = x + _dot3(x, p, "nn")
    p = _dot3(p, p, "nn")
    x = x + _dot3(x, p, "nn")
    p = _dot3(p, p, "nn")
    x = x + _dot3(x, p, "nn")
    c1 = jnp.where(jnp.logical_and(b32, jnp.logical_not(b16)), m, 0.0)
    x = x - _dot3(_dot3(x, c1, "nn"), x, "nn")
    c2 = jnp.where(b32, 0.0, m)
    x = x - _dot3(_dot3(x, c2, "nn"), x, "nn")
    return x


def _dn_chunk_common(q, k, gc, ri, ci):
    eye = ri == ci
    g_row = _col_to_row(gc, eye)
    diff = jnp.minimum(gc - g_row, 0.0)
    gam = jnp.where(ri >= ci, jnp.exp(diff), 0.0)
    kk = _bdot(k, k, "nt")
    qk = _bdot(q, k, "nt")
    rcol = lax.broadcasted_iota(jnp.int32, gc.shape, gc.ndim - 2)
    last = jnp.sum(jnp.where(rcol == DN_CHUNK - 1, gc, 0.0), axis=-2, keepdims=True)
    e_g = jnp.exp(gc)
    dec = jnp.exp(last - gc)
    return eye, gam, kk, qk, last, e_g, dec, rcol


def _dn_specs(t, rows_blk):
    def head(off):
        return pl.BlockSpec((rows_blk, D_HEAD), lambda g, h: (g, off + h))

    lanes = pl.BlockSpec((rows_blk, 128), lambda g, h: (g, 0))
    sq = pl.BlockSpec((1, rows_blk, DN_CHUNK), lambda g, h: (h, g, 0))
    tile = pl.BlockSpec((1, rows_blk // DN_CHUNK, 8, 128), lambda g, h: (h, g, 0, 0))
    return head, lanes, sq, tile


def _head_column(slab, lane_idx):
    lane = lax.broadcasted_iota(jnp.int32, slab.shape, 1)
    return _chunks(jnp.sum(jnp.where(lane == lane_idx, slab, 0.0), axis=1, keepdims=True))


def _dn_intra_fwd(qkv, beta_t, g_t):
    t = qkv.shape[0]
    n_chunks = t // DN_CHUNK
    rows_blk = DN_GROUP * DN_CHUNK

    def body(q_ref, k_ref, v_ref, b_ref, g_ref, u_ref, w_ref, qd_ref, kd_ref, a_ref, ti_ref, el_ref):
        ri = lax.broadcasted_iota(jnp.int32, (DN_CHUNK, DN_CHUNK), 0)
        ci = lax.broadcasted_iota(jnp.int32, (DN_CHUNK, DN_CHUNK), 1)
        h = pl.program_id(1)
        q, k, v = (_chunks(r[...]) for r in (q_ref, k_ref, v_ref))
        b, gc = _head_column(b_ref[...], h), _head_column(g_ref[...], h + N_HEADS)
        _, gam, kk, qk, last, e_g, dec, _ = _dn_chunk_common(q, k, gc, ri, ci)
        tinv = _tri_inverse(jnp.where(ri > ci, b * kk * gam, 0.0), ri, ci)
        u_ref[...] = _unchunk(_bdot(tinv, v * b, "nn"))
        w_ref[...] = _unchunk(_bdot(tinv, k * (b * e_g), "nn"))
        qd_ref[...] = _unchunk(q * e_g)
        kd_ref[...] = _unchunk(k * dec)
        a_ref[0] = _unchunk(qk * gam)
        ti_ref[0] = _unchunk(tinv)
        el_ref[0] = jnp.broadcast_to(jnp.exp(last), (DN_GROUP, 8, 128))

    head, lanes, sq, tile = _dn_specs(t, rows_blk)
    act = jax.ShapeDtypeStruct((t, D_MODEL), F32)
    sqs = jax.ShapeDtypeStruct((N_HEADS, t, DN_CHUNK), F32)
    return pl.pallas_call(
        body, name="dn_intra_fwd", grid=(t // rows_blk, N_HEADS),
        in_specs=[head(0), head(N_HEADS), head(2 * N_HEADS), lanes, lanes],
        out_specs=[head(0)] * 4 + [sq, sq, tile],
        out_shape=[act] * 4 + [sqs, sqs, jax.ShapeDtypeStruct((N_HEADS, n_chunks, 8, 128), F32)],
        compiler_params=_cp(),
    )(qkv, qkv, qkv, beta_t, g_t)


def _dn_scan_specs(t, n_chunks):
    hp = DN_HEADS_PER_STEP
    act = pl.BlockSpec((t, hp * D_HEAD), lambda h: (0, h))
    sq = pl.BlockSpec((hp, t, DN_CHUNK), lambda h: (h, 0, 0))
    state = pl.BlockSpec((hp, n_chunks, D_HEAD, D_HEAD), lambda h: (h, 0, 0, 0))
    tile = pl.BlockSpec((hp, n_chunks, 8, 128), lambda h: (h, 0, 0, 0))
    return act, sq, state, tile


def _dn_scan_fwd(u, w, qd, kd, a, el):
    t = u.shape[0]
    n_chunks = t // DN_CHUNK
    hp = DN_HEADS_PER_STEP

    def body(u_ref, w_ref, qd_ref, kd_ref, a_ref, el_ref, o_ref, vn_ref, s_ref, s_scr):
        s_scr[...] = jnp.zeros_like(s_scr)

        def chunk(n, carry):
            rows = pl.ds(pl.multiple_of(n * DN_CHUNK, DN_CHUNK), DN_CHUNK)
            for hh in range(hp):
                hs = slice(hh * D_HEAD, (hh + 1) * D_HEAD)
                s = s_scr[hh]
                s_ref[hh, n] = s
                v_new = u_ref[rows, hs] - _bdot(w_ref[rows, hs], s, "nn")
                vn_ref[rows, hs] = v_new
                o_ref[rows, hs] = _bdot(qd_ref[rows, hs], s, "nn") + _bdot(a_ref[hh, rows, :], v_new, "nn")
                s_scr[hh] = s * el_ref[hh, n][0:1, :] + _bdot(kd_ref[rows, hs], v_new, "tn")
            return carry

        lax.fori_loop(0, n_chunks, chunk, 0)

    act, sq, state, tile = _dn_scan_specs(t, n_chunks)
    shp = jax.ShapeDtypeStruct((t, D_MODEL), F32)
    return pl.pallas_call(
        body, name="dn_scan_fwd", grid=(N_HEADS // hp,),
        in_specs=[act, act, act, act, sq, tile], out_specs=[act, act, state],
        out_shape=[shp, shp, jax.ShapeDtypeStruct((N_HEADS, n_chunks, D_HEAD, D_HEAD), F32)],
        scratch_shapes=[pltpu.VMEM((hp, D_HEAD, D_HEAD), F32)], compiler_params=_cp(),
    )(u, w, qd, kd, a, el)


def _dn_scan_bwd(w, qd, kd, a, el, vn, s_all, do):
    t = w.shape[0]
    n_chunks = t // DN_CHUNK
    hp = DN_HEADS_PER_STEP

    def body(w_ref, qd_ref, kd_ref, a_ref, el_ref, vn_ref, s_ref, do_ref, dvn_ref, dkd_ref, dqd_ref, dw_ref, dl_ref, ds_scr):
        ds_scr[...] = jnp.zeros_like(ds_scr)

        def chunk(i, carry):
            n = n_chunks - 1 - i
            rows = pl.ds(pl.multiple_of(n * DN_CHUNK, DN_CHUNK), DN_CHUNK)
            for hh in range(hp):
                hs = slice(hh * D_HEAD, (hh + 1) * D_HEAD)
                s = s_ref[hh, n]
                d_s = ds_scr[hh]
                e_last = el_ref[hh, n][0:1, :]
                d_o = do_ref[rows, hs]
                dv_new = _bdot(a_ref[hh, rows, :], d_o, "tn") + _bdot(kd_ref[rows, hs], d_s, "nn")
                ds_scr[hh] = d_s * e_last + _bdot(qd_ref[rows, hs], d_o, "tn") - _bdot(w_ref[rows, hs], dv_new, "tn")
                dvn_ref[rows, hs] = dv_new
                dkd_ref[rows, hs] = _bdot(vn_ref[rows, hs], d_s, "nt")
                dqd_ref[rows, hs] = _bdot(d_o, s, "nt")
                dw_ref[rows, hs] = -_bdot(dv_new, s, "nt")
                dlast = jnp.sum(jnp.sum(d_s * s, axis=1, keepdims=True), axis=0, keepdims=True)
                dl_ref[hh, n] = jnp.broadcast_to(dlast * e_last, (8, 128))
            return carry

        lax.fori_loop(0, n_chunks, chunk, 0)

    act, sq, state, tile = _dn_scan_specs(t, n_chunks)
    shp = jax.ShapeDtypeStruct((t, D_MODEL), F32)
    return pl.pallas_call(
        body, name="dn_scan_bwd", grid=(N_HEADS // hp,),
        in_specs=[act, act, act, sq, tile, act, state, act], out_specs=[act] * 4 + [tile],
        out_shape=[shp] * 4 + [jax.ShapeDtypeStruct((N_HEADS, n_chunks, 8, 128), F32)],
        scratch_shapes=[pltpu.VMEM((hp, D_HEAD, D_HEAD), F32)], compiler_params=_cp(),
    )(w, qd, kd, a, el, vn, s_all, do)


def _dn_intra_bwd(qkv, beta_t, g_t, tinv_all, vn, do, dvn, dkd, dqd, dw, dl):
    t = qkv.shape[0]
    rows_blk = DN_GROUP * DN_CHUNK

    def body(q_ref, k_ref, v_ref, b_ref, g_ref, ti_ref, vn_ref, do_ref, dvn_ref, dkd_ref, dqd_ref, dw_ref, dl_ref,
             dq_ref, dk_ref, dv_ref, db_ref, dg_ref):
        ri = lax.broadcasted_iota(jnp.int32, (DN_CHUNK, DN_CHUNK), 0)
        ci = lax.broadcasted_iota(jnp.int32, (DN_CHUNK, DN_CHUNK), 1)
        h = pl.program_id(1)
        q, k, v = (_chunks(r[...]) for r in (q_ref, k_ref, v_ref))
        b, gc = _head_column(b_ref[...], h), _head_column(g_ref[...], h + N_HEADS)
        tinv = _chunks(ti_ref[0])
        dv_new, dk_dec, dq_dec, d_w = (_chunks(r[...]) for r in (dvn_ref, dkd_ref, dqd_ref, dw_ref))
        eye, gam, kk, qk, _, e_g, dec, rcol = _dn_chunk_common(q, k, gc, ri, ci)
        bv = v * b
        bk = k * (b * e_g)

        d_a = jnp.where(ri >= ci, _bdot(_chunks(do_ref[...]), _chunks(vn_ref[...]), "nt"), 0.0)
        dbv = _bdot(tinv, dv_new, "tn")
        dbk = _bdot(tinv, d_w, "tn")
        d_tinv = _bdot(dv_new, bv, "nt") + _bdot(d_w, bk, "nt")
        d_m = -jnp.where(ri > ci, _dot3(_dot3(tinv, d_tinv, "tn"), tinv, "nt"), 0.0)

        d_kk = d_m * b * gam
        d_gam = d_m * b * kk + d_a * qk
        d_qk = d_a * gam
        dq_ref[...] = _unchunk(_bdot(d_qk, k, "nn") + dq_dec * e_g)
        dk_ref[...] = _unchunk(_bdot(d_qk, q, "tn") + _bdot(d_kk, k, "nn") + _bdot(d_kk, k, "tn")
                               + dk_dec * dec + dbk * (b * e_g))
        dv_ref[...] = _unchunk(dbv * b)
        d_b = _unchunk(jnp.sum(d_m * kk * gam, axis=-1, keepdims=True) + jnp.sum(dbv * v, axis=-1, keepdims=True)
                       + jnp.sum(dbk * k, axis=-1, keepdims=True) * e_g)

        xg = d_gam * gam
        kdk = jnp.sum(dk_dec * (k * dec), axis=-1, keepdims=True)
        d_gc = (jnp.sum(xg, axis=-1, keepdims=True) - _row_to_col(jnp.sum(xg, axis=-2, keepdims=True), eye)
                + jnp.sum(dq_dec * (q * e_g), axis=-1, keepdims=True) - kdk
                + jnp.sum(dbk * bk, axis=-1, keepdims=True))
        d_last_total = dl_ref[0][:, 0:1, 0:1] + jnp.sum(kdk, axis=-2, keepdims=True)
        d_g = _unchunk(d_gc + jnp.where(rcol == DN_CHUNK - 1, d_last_total, 0.0))

        @pl.when(h == 0)
        def _():
            db_ref[...] = jnp.zeros_like(db_ref)
            dg_ref[...] = jnp.zeros_like(dg_ref)

        lane = lax.broadcasted_iota(jnp.int32, db_ref.shape, 1)
        db_ref[...] += jnp.where(lane == h, d_b, 0.0)
        dg_ref[...] += jnp.where(lane == h + N_HEADS, d_g, 0.0)

    head, lanes, sq, tile = _dn_specs(t, rows_blk)
    return pl.pallas_call(
        body, name="dn_intra_bwd", grid=(t // rows_blk, N_HEADS),
        in_specs=[head(0), head(N_HEADS), head(2 * N_HEADS), lanes, lanes, sq] + [head(0)] * 6 + [tile],
        out_specs=[head(0), head(0), head(0), lanes, lanes],
        out_shape=[jax.ShapeDtypeStruct((t, D_MODEL), F32)] * 3 + [jax.ShapeDtypeStruct((t, 128), F32)] * 2,
        compiler_params=_cp(),
    )(qkv, qkv, qkv, beta_t, g_t, tinv_all, vn, do, dvn, dkd, dqd, dw, dl)


def _dn_post_fwd(o, proj, gn):
    t = o.shape[0]

    def body(o_ref, z_ref, g_ref, out_ref):
        ov, z = o_ref[...], z_ref[...]
        r = lax.rsqrt(jnp.mean(ov * ov, axis=-1, keepdims=True) + NORM_EPS)
        out_ref[...] = (((ov * r) * g_ref[...]) * (z * _sigmoid(z))).astype(BF16)

    blk = pl.BlockSpec((t, D_HEAD), lambda h: (0, h))
    return pl.pallas_call(
        body, name="dn_post_fwd", grid=(N_HEADS,),
        in_specs=[blk, pl.BlockSpec((t, D_HEAD), lambda h: (0, C_DNZ // D_HEAD + h)),
                  pl.BlockSpec((1, D_HEAD), lambda h: (0, 0))],
        out_specs=blk, out_shape=jax.ShapeDtypeStruct((t, D_MODEL), BF16), compiler_params=_cp(),
    )(o, proj, gn)


def _dn_post_bwd(o, proj, gn, dout):
    t = o.shape[0]

    def body(o_ref, z_ref, g_ref, d_ref, do_ref, dz_ref, dg_ref):
        @pl.when(pl.program_id(0) == 0)
        def _():
            dg_ref[...] = jnp.zeros_like(dg_ref)

        ov, z, d = o_ref[...], z_ref[...], d_ref[...]
        r = lax.rsqrt(jnp.mean(ov * ov, axis=-1, keepdims=True) + NORM_EPS)
        ohat = ov * r
        s = _sigmoid(z)
        d_on = d * (z * s)
        dz_ref[...] = (d * (ohat * g_ref[...]) * (s * (1.0 + z * (1.0 - s)))).astype(BF16)
        dg_ref[...] += jnp.sum(d_on * ohat, axis=0, keepdims=True)
        dxh = d_on * g_ref[...]
        do_ref[...] = r * (dxh - ohat * jnp.mean(dxh * ohat, axis=-1, keepdims=True))

    blk = pl.BlockSpec((t, D_HEAD), lambda h: (0, h))
    vec = pl.BlockSpec((1, D_HEAD), lambda h: (0, 0))
    return pl.pallas_call(
        body, name="dn_post_bwd", grid=(N_HEADS,),
        in_specs=[blk, pl.BlockSpec((t, D_HEAD), lambda h: (0, C_DNZ // D_HEAD + h)), vec, blk],
        out_specs=[blk, blk, vec],
        out_shape=[jax.ShapeDtypeStruct((t, D_MODEL), F32), jax.ShapeDtypeStruct((t, D_MODEL), BF16),
                   jax.ShapeDtypeStruct((1, D_HEAD), F32)], compiler_params=_cp(),
    )(o, proj, gn, dout)


def _sb_fwd(proj):
    t = proj.shape[0]
    qblk = min(SB_QBLOCK, t)
    scale = 1.0 / math.sqrt(D_HEAD)

    hp = SB_HEADS_PER_STEP
    wid = hp * D_HEAD

    def body(q_ref, k_ref, v_ref, z_ref, o_ref, og_ref, l_ref, qb, kb, vb):
        for hh in range(hp):
            hs = slice(hh * D_HEAD, (hh + 1) * D_HEAD)
            qb[hh] = q_ref[:, hs].astype(BF16)
            kb[hh] = k_ref[:, hs].astype(BF16)
            vb[hh] = v_ref[:, hs].astype(BF16)
        ri = lax.broadcasted_iota(jnp.int32, (qblk, SB_BLOCK), 0)
        ci = lax.broadcasted_iota(jnp.int32, (qblk, SB_BLOCK), 1)
        r2 = lax.broadcasted_iota(jnp.int32, (SB_BLOCK, SB_BLOCK), 0)
        c2 = lax.broadcasted_iota(jnp.int32, (SB_BLOCK, SB_BLOCK), 1)
        upper = (r2 > c2).astype(BF16)
        nkb = qblk // SB_BLOCK

        def qblock(i, carry):
            rows = pl.ds(pl.multiple_of(i * qblk, qblk), qblk)
            qi = qb[:, rows, :]

            def kblock(jj, st):
                acc, c = st
                j = (i + 1) * nkb - 1 - jj
                cols = pl.ds(pl.multiple_of(j * SB_BLOCK, SB_BLOCK), SB_BLOCK)
                mask = (j * SB_BLOCK + ci) < (i * qblk + ri)
                z = _dot(qi, kb[:, cols, :], "nt") * scale
                lb = jnp.minimum(z, 0.0) - jnp.log(1.0 + jnp.exp(-jnp.abs(z)))
                lf = jnp.where(mask, lb - z, 0.0)
                surv = _split_dot(lf, upper) + c
                att = jnp.where(mask, jnp.exp(lb + surv), 0.0)
                acc = acc + _dot(att.astype(BF16), vb[:, cols, :], "nn")
                return acc, c + jnp.sum(lf, axis=-1, keepdims=True)

            init = (jnp.zeros((hp, qblk, D_HEAD), F32), jnp.zeros((hp, qblk, 1), F32))
            acc, c = lax.fori_loop(0, (i + 1) * nkb, kblock, init)
            l_ref[:, rows, :] = c
            for hh in range(hp):
                hs = slice(hh * D_HEAD, (hh + 1) * D_HEAD)
                zg = z_ref[rows, hs]
                o_ref[rows, hs] = acc[hh]
                og_ref[rows, hs] = (acc[hh] * (zg * _sigmoid(zg))).astype(BF16)
            return carry

        lax.fori_loop(0, t // qblk, qblock, 0)

    def head(off):
        return pl.BlockSpec((t, wid), lambda h: (0, off // wid + h))

    out = pl.BlockSpec((t, wid), lambda h: (0, h))
    return pl.pallas_call(
        body, name="sb_fwd", grid=(N_HEADS // hp,),
        in_specs=[head(C_SBQ), head(C_SBQ + D_MODEL), head(C_SBQ + 2 * D_MODEL), head(C_SBZ)],
        out_specs=[out, out, pl.BlockSpec((hp, t, 1), lambda h: (h, 0, 0))],
        out_shape=[jax.ShapeDtypeStruct((t, D_MODEL), F32), jax.ShapeDtypeStruct((t, D_MODEL), BF16),
                   jax.ShapeDtypeStruct((N_HEADS, t, 1), F32)],
        scratch_shapes=[pltpu.VMEM((hp, t, D_HEAD), BF16)] * 3, compiler_params=_cp(),
    )(proj, proj, proj, proj)


def _sb_bwd(proj, o, ltot, dog):
    t = proj.shape[0]
    qblk = min(SB_QBLOCK, t)
    scale = 1.0 / math.sqrt(D_HEAD)

    hp = SB_HEADS_PER_STEP
    wid = hp * D_HEAD

    def body(q_ref, k_ref, v_ref, z_ref, o_ref, l_ref, d_ref, dq_ref, dk_ref, dv_ref, dz_ref,
             qb, kb, vb, dob, dk_scr, dv_scr):
        for hh in range(hp):
            hs = slice(hh * D_HEAD, (hh + 1) * D_HEAD)
            qb[hh] = q_ref[:, hs].astype(BF16)
            kb[hh] = k_ref[:, hs].astype(BF16)
            vb[hh] = v_ref[:, hs].astype(BF16)
            zg = z_ref[:, hs]
            sg = _sigmoid(zg)
            dgo = d_ref[:, hs]
            dob[hh] = (dgo * (zg * sg)).astype(BF16)
            dz_ref[:, hs] = (dgo * o_ref[:, hs] * (sg * (1.0 + zg * (1.0 - sg)))).astype(BF16)
        dk_scr[...] = jnp.zeros_like(dk_scr)
        dv_scr[...] = jnp.zeros_like(dv_scr)
        ri = lax.broadcasted_iota(jnp.int32, (qblk, SB_BLOCK), 0)
        ci = lax.broadcasted_iota(jnp.int32, (qblk, SB_BLOCK), 1)
        r2 = lax.broadcasted_iota(jnp.int32, (SB_BLOCK, SB_BLOCK), 0)
        c2 = lax.broadcasted_iota(jnp.int32, (SB_BLOCK, SB_BLOCK), 1)
        incl = (r2 <= c2).astype(BF16)
        below = (r2 < c2).astype(BF16)

        def qblock(i, carry):
            rows = pl.ds(pl.multiple_of(i * qblk, qblk), qblk)
            qi = qb[:, rows, :]
            d_o = dob[:, rows, :]
            ltot = l_ref[:, rows, :]

            def kblock(j, st):
                dq, cpre, ce = st
                cols = pl.ds(pl.multiple_of(j * SB_BLOCK, SB_BLOCK), SB_BLOCK)
                mask = (j * SB_BLOCK + ci) < (i * qblk + ri)
                kj, vj = kb[:, cols, :], vb[:, cols, :]
                z = _dot(qi, kj, "nt") * scale
                e1 = jnp.exp(-jnp.abs(z))
                den = 1.0 + e1
                lb = jnp.minimum(z, 0.0) - jnp.log(den)
                r = 1.0 / den
                sig = jnp.where(z >= 0.0, r, e1 * r)
                nsig = jnp.where(z >= 0.0, e1 * r, r)
                lf = jnp.where(mask, lb - z, 0.0)
                surv = ltot - (cpre + _split_dot(lf, incl))
                att = jnp.where(mask, jnp.exp(lb + surv), 0.0)
                e = _dot(d_o, vj, "nt") * att
                dlf = ce + _split_dot(e, below)
                dzz = jnp.where(mask, e * nsig - dlf * sig, 0.0).astype(BF16)
                dq = dq + _dot(dzz, kj, "nn")
                dk_scr[:, cols, :] += _dot(dzz, qi, "tn")
                dv_scr[:, cols, :] += _dot(att.astype(BF16), d_o, "tn")
                return dq, cpre + jnp.sum(lf, axis=-1, keepdims=True), ce + jnp.sum(e, axis=-1, keepdims=True)

            zero_col = jnp.zeros((hp, qblk, 1), F32)
            init = (jnp.zeros((hp, qblk, D_HEAD), F32), zero_col, zero_col)
            dq, _, _ = lax.fori_loop(0, (i + 1) * (qblk // SB_BLOCK), kblock, init)
            for hh in range(hp):
                dq_ref[rows, hh * D_HEAD:(hh + 1) * D_HEAD] = (dq[hh] * scale).astype(BF16)
            return carry

        lax.fori_loop(0, t // qblk, qblock, 0)
        for hh in range(hp):
            hs = slice(hh * D_HEAD, (hh + 1) * D_HEAD)
            dk_ref[:, hs] = (dk_scr[hh] * scale).astype(BF16)
            dv_ref[:, hs] = dv_scr[hh].astype(BF16)

    def head(off):
        return pl.BlockSpec((t, wid), lambda h: (0, off // wid + h))

    return pl.pallas_call(
        body, name="sb_bwd", grid=(N_HEADS // hp,),
        in_specs=[head(C_SBQ), head(C_SBQ + D_MODEL), head(C_SBQ + 2 * D_MODEL), head(C_SBZ), head(0),
                  pl.BlockSpec((hp, t, 1), lambda h: (h, 0, 0)), head(0)],
        out_specs=[head(0)] * 4, out_shape=[jax.ShapeDtypeStruct((t, D_MODEL), BF16)] * 4,
        scratch_shapes=[pltpu.VMEM((hp, t, D_HEAD), BF16)] * 4 + [pltpu.VMEM((hp, t, D_HEAD), F32)] * 2,
        compiler_params=_cp(),
    )(proj, proj, proj, proj, o, ltot, dog)


def _mem_fwd(proj, mkv):
    t = proj.shape[0]
    tq = _pick(t, (512, 256))
    m_len = mkv.shape[0]
    scale = 1.0 / math.sqrt(MEM_DH)

    def body(q_ref, z_ref, kv_ref, o_ref, og_ref):
        q = q_ref[...]
        mk = kv_ref[:, :MEM_W].astype(BF16)
        mv = kv_ref[:, MEM_W:].astype(BF16)
        lane = lax.broadcasted_iota(jnp.int32, q.shape, 1) >> 6
        o = jnp.zeros(q.shape, F32)
        for h in range(MEM_HEADS):
            s = _bdot(jnp.where(lane == h, q, 0.0), mk, "nt") * scale
            p = jnp.exp(s - jnp.max(s, axis=-1, keepdims=True))
            p = p / jnp.sum(p, axis=-1, keepdims=True)
            o = o + jnp.where(lane == h, _bdot(p, mv, "nn"), 0.0)
        z = z_ref[...]
        o_ref[...] = o
        og_ref[...] = (o * (z * _sigmoid(z))).astype(BF16)

    out = pl.BlockSpec((tq, MEM_W), lambda i: (i, 0))
    return pl.pallas_call(
        body, name="mem_fwd", grid=(t // tq,),
        in_specs=[pl.BlockSpec((tq, MEM_W), lambda i: (i, C_MQ // MEM_W)),
                  pl.BlockSpec((tq, MEM_W), lambda i: (i, C_MZ // MEM_W)),
                  pl.BlockSpec((m_len, 2 * MEM_W), lambda i: (0, 0))],
        out_specs=[out, out],
        out_shape=[jax.ShapeDtypeStruct((t, MEM_W), F32), jax.ShapeDtypeStruct((t, MEM_W), BF16)],
        compiler_params=_cp(),
    )(proj, proj, mkv)


def _mem_bwd(proj, mkv, o, dog):
    t = proj.shape[0]
    tq = _pick(t, (512, 256))
    m_len = mkv.shape[0]
    scale = 1.0 / math.sqrt(MEM_DH)

    def body(q_ref, z_ref, kv_ref, o_ref, d_ref, dq_ref, dz_ref, dkv_ref):
        @pl.when(pl.program_id(0) == 0)
        def _():
            dkv_ref[...] = jnp.zeros_like(dkv_ref)

        q = q_ref[...]
        z = z_ref[...]
        sg = _sigmoid(z)
        dgo = d_ref[...]
        d_o = dgo * (z * sg)
        dz_ref[...] = (dgo * o_ref[...] * (sg * (1.0 + z * (1.0 - sg)))).astype(BF16)
        mk = kv_ref[:, :MEM_W].astype(BF16)
        mv = kv_ref[:, MEM_W:].astype(BF16)
        lane = lax.broadcasted_iota(jnp.int32, q.shape, 1) >> 6
        klane = lax.broadcasted_iota(jnp.int32, (m_len, MEM_W), 1) >> 6
        dq = jnp.zeros(q.shape, F32)
        dmk = jnp.zeros((m_len, MEM_W), F32)
        dmv = jnp.zeros((m_len, MEM_W), F32)
        for h in range(MEM_HEADS):
            qh = jnp.where(lane == h, q, 0.0)
            doh = jnp.where(lane == h, d_o, 0.0)
            s = _bdot(qh, mk, "nt") * scale
            p = jnp.exp(s - jnp.max(s, axis=-1, keepdims=True))
            p = p / jnp.sum(p, axis=-1, keepdims=True)
            dp = _bdot(doh, mv, "nt")
            ds = p * (dp - jnp.sum(dp * p, axis=-1, keepdims=True)) * scale
            dq = dq + jnp.where(lane == h, _bdot(ds, mk, "nn"), 0.0)
            dmk = dmk + jnp.where(klane == h, _bdot(ds, qh, "tn"), 0.0)
            dmv = dmv + jnp.where(klane == h, _bdot(p, doh, "tn"), 0.0)
        dq_ref[...] = dq.astype(BF16)
        dkv_ref[:, :MEM_W] += dmk
        dkv_ref[:, MEM_W:] += dmv

    blk = pl.BlockSpec((tq, MEM_W), lambda i: (i, 0))
    kv = pl.BlockSpec((m_len, 2 * MEM_W), lambda i: (0, 0))
    return pl.pallas_call(
        body, name="mem_bwd", grid=(t // tq,),
        in_specs=[pl.BlockSpec((tq, MEM_W), lambda i: (i, C_MQ // MEM_W)),
                  pl.BlockSpec((tq, MEM_W), lambda i: (i, C_MZ // MEM_W)), kv, blk, blk],
        out_specs=[blk, blk, kv],
        out_shape=[jax.ShapeDtypeStruct((t, MEM_W), BF16), jax.ShapeDtypeStruct((t, MEM_W), BF16),
                   jax.ShapeDtypeStruct((m_len, 2 * MEM_W), F32)], compiler_params=_cp(),
    )(proj, proj, mkv, o, dog)


_GW = 512


def _merge_fwd(proj, y_dn, y_sb, y_m):
    t = proj.shape[0]
    tb = _pick(t, (256,))
    nc = D_MODEL // _GW

    def body(g1, g2, g3, y1, y2, y3, out_ref):
        out_ref[...] = (_sigmoid(g1[...]) * y1[...] + _sigmoid(g2[...]) * y2[...] + _sigmoid(g3[...]) * y3[...]).astype(BF16)

    def gate(kb):
        return pl.BlockSpec((tb, _GW), lambda i, c: (i, C_GATES // _GW + kb * nc + c))

    blk = pl.BlockSpec((tb, _GW), lambda i, c: (i, c))
    return pl.pallas_call(
        body, name="merge_fwd", grid=(t // tb, nc), in_specs=[gate(0), gate(1), gate(2), blk, blk, blk],
        out_specs=blk, out_shape=jax.ShapeDtypeStruct((t, D_MODEL), BF16), compiler_params=_cp(),
    )(proj, proj, proj, y_dn, y_sb, y_m)


def _merge_bwd(proj, y_dn, y_sb, y_m, dm):
    t = proj.shape[0]
    tb = _pick(t, (256,))
    nc = D_MODEL // _GW

    def body(g1, g2, g3, y1, y2, y3, dm_ref, d1, d2, d3, dg1, dg2, dg3):
        d = dm_ref[...]
        for g, y, dy, dg in ((g1, y1, d1, dg1), (g2, y2, d2, dg2), (g3, y3, d3, dg3)):
            s = _sigmoid(g[...])
            dy[...] = (d * s).astype(BF16)
            dg[...] = (d * y[...] * (s * (1.0 - s))).astype(BF16)

    def gate(kb):
        return pl.BlockSpec((tb, _GW), lambda i, c: (i, C_GATES // _GW + kb * nc + c))

    blk = pl.BlockSpec((tb, _GW), lambda i, c: (i, c))
    act = jax.ShapeDtypeStruct((t, D_MODEL), BF16)
    return pl.pallas_call(
        body, name="merge_bwd", grid=(t // tb, nc), in_specs=[gate(0), gate(1), gate(2), blk, blk, blk, blk],
        out_specs=[blk] * 6, out_shape=[act] * 6, compiler_params=_cp(),
    )(proj, proj, proj, y_dn, y_sb, y_m, dm)


def _final_loss(x, mo, g, tgt):
    t, d = x.shape
    tb = _pick(t, (256,))

    def body(x_ref, mo_ref, g_ref, t_ref, do_ref, dob_ref, loss_ref, dg_ref):
        @pl.when(pl.program_id(0) == 0)
        def _():
            loss_ref[...] = jnp.zeros_like(loss_ref)
            dg_ref[...] = jnp.zeros_like(dg_ref)

        out = x_ref[...] + mo_ref[...]
        r = lax.rsqrt(jnp.mean(out * out, axis=-1, keepdims=True) + NORM_EPS)
        xhat = out * r
        gv = g_ref[...]
        err = xhat * gv - t_ref[...]
        per_tok = jnp.mean(err * err, axis=-1, keepdims=True)
        loss_ref[...] += 0.5 * jnp.sum(per_tok, axis=0, keepdims=True)
        dy = err * (1.0 / d)
        dg_ref[...] += jnp.sum(dy * xhat, axis=0, keepdims=True)
        dxh = dy * gv
        dout = r * (dxh - xhat * jnp.mean(dxh * xhat, axis=-1, keepdims=True))
        do_ref[...] = dout
        dob_ref[...] = dout.astype(BF16)

    row = pl.BlockSpec((tb, d), lambda i: (i, 0))
    vec = pl.BlockSpec((1, d), lambda i: (0, 0))
    return pl.pallas_call(
        body, name="final_loss", grid=(t // tb,), in_specs=[row, row, vec, row],
        out_specs=[row, row, pl.BlockSpec((1, 128), lambda i: (0, 0)), vec],
        out_shape=[jax.ShapeDtypeStruct((t, d), F32), jax.ShapeDtypeStruct((t, d), BF16),
                   jax.ShapeDtypeStruct((1, 128), F32), jax.ShapeDtypeStruct((1, d), F32)],
        compiler_params=_cp(),
    )(x, mo, g, tgt)


def _cast_bf16(a, name):
    r, c = a.shape
    tb = _pick(r, (128, 496))

    def body(a_ref, o_ref):
        o_ref[...] = a_ref[...].astype(BF16)

    blk = pl.BlockSpec((tb, c), lambda i: (i, 0))
    return pl.pallas_call(body, name=name, grid=(r // tb,), in_specs=[blk], out_specs=blk,
                          out_shape=jax.ShapeDtypeStruct((r, c), BF16), compiler_params=_cp())(a)


WIN_START = (0, 23, 45, 68)
_S1_LO, _S1_HI = 1148, 1164
_S1_BA_POS = SHARD_PAD - 128


def _to_window(x, s):
    if s == 0:
        return x
    if s in (2, 3):
        return pltpu.roll(x, 120 if s == 2 else 124, 1)
    pos = lax.broadcasted_iota(jnp.int32, x.shape, 1)
    head = pltpu.roll(x, 4, 1)
    tail = pltpu.roll(x, SHARD_PAD - 12, 1)
    ba = jnp.where(pos < _S1_BA_POS + (_S1_HI - _S1_LO), pltpu.roll(x, _S1_BA_POS - _S1_LO, 1), 0.0)
    return jnp.where(pos < _S1_LO + 4, head, jnp.where(pos < _S1_BA_POS, tail, ba))


def _from_window(g, s):
    if s == 0:
        return g
    if s in (2, 3):
        return pltpu.roll(g, SHARD_PAD - (120 if s == 2 else 124), 1)
    col = lax.broadcasted_iota(jnp.int32, g.shape, 1)
    head = pltpu.roll(g, SHARD_PAD - 4, 1)
    tail = pltpu.roll(g, 12, 1)
    ba = pltpu.roll(g, SHARD_PAD - (_S1_BA_POS - _S1_LO), 1)
    return jnp.where(col < _S1_LO, head, jnp.where(col < _S1_HI, ba, tail))


def _cast_to_window(w, shard, name):
    r, c = w.shape
    tb = _pick(r, (128,))

    def body(s_ref, w_ref, o_ref, pad_scr):
        pad_scr[...] = jnp.zeros_like(pad_scr)
        pad_scr[:, :c] = w_ref[...]
        x = pad_scr[...]
        for s in range(N_SHARD):
            @pl.when(s_ref[0] == s)
            def _():
                o_ref[...] = _to_window(x, s).astype(BF16)

    return pl.pallas_call(
        body, name=name,
        grid_spec=pltpu.PrefetchScalarGridSpec(
            num_scalar_prefetch=1, grid=(r // tb,),
            in_specs=[pl.BlockSpec((tb, c), lambda i, s: (i, 0))],
            out_specs=pl.BlockSpec((tb, SHARD_PAD), lambda i, s: (i, 0)),
            scratch_shapes=[pltpu.VMEM((tb, SHARD_PAD), F32)]),
        out_shape=jax.ShapeDtypeStruct((r, SHARD_PAD), BF16), compiler_params=_cp(),
    )(shard, w)


def _pair_add(g, recv, c_idx, name):
    n, r, c = g.shape
    half = r // 2
    tb = _pick(half, (128, 248))
    nb = half // tb

    def body(c_ref, g_ref, r_ref, o_ref):
        o_ref[...] = (g_ref[...].astype(F32) + r_ref[...].astype(F32)).astype(BF16)

    blk = pl.BlockSpec((n, tb, c), lambda i, c_ref: (0, i, 0))
    return pl.pallas_call(
        body, name=name,
        grid_spec=pltpu.PrefetchScalarGridSpec(
            num_scalar_prefetch=1, grid=(nb,),
            in_specs=[pl.BlockSpec((n, tb, c), lambda i, c_ref: (0, c_ref[0] * nb + i, 0)), blk], out_specs=blk),
        out_shape=jax.ShapeDtypeStruct((n, half, c), BF16), compiler_params=_cp(),
    )(c_idx, g, recv)


def _chip_sum(parts, by_chip, place, name):
    n, h, c = parts.shape
    tb = _pick(h, (128, 248))
    nb = h // tb

    def body(p_ref, mine_ref, *rest):
        others, o_ref = rest[:n], rest[n]
        me = jnp.zeros((tb, c), jnp.int32) + p_ref[0]
        acc = None
        for q in range(n):
            term = jnp.where(me == q, mine_ref[...], others[q][...]).astype(F32)
            acc = term if acc is None else acc + term
        o_ref[...] = acc

    def other(q):
        return pl.BlockSpec((None, tb, c), lambda i, p: (jnp.where(p[0] == q, (q + 1) % n, q), i, 0))

    return pl.pallas_call(
        body, name=name,
        grid_spec=pltpu.PrefetchScalarGridSpec(
            num_scalar_prefetch=1, grid=(nb,),
            in_specs=[pl.BlockSpec((None, tb, c), lambda i, p: (p[0], i, 0))] + [other(q) for q in range(n)],
            out_specs=pl.BlockSpec((tb, c), lambda i, p: (p[1] * nb + i, 0))),
        out_shape=jax.ShapeDtypeStruct((2 * h, c), F32), compiler_params=_cp(),
    )(place, parts, *([by_chip] * n))


def _adamw_math(w, g, m, v):
    m = ADAM_B1 * m + (1.0 - ADAM_B1) * g
    v = ADAM_B2 * v + (1.0 - ADAM_B2) * (g * g)
    m_hat = m / (1.0 - ADAM_B1 ** ADAM_STEP)
    v_hat = v / (1.0 - ADAM_B2 ** ADAM_STEP)
    delta = -ADAM_LR * (m_hat / (jnp.sqrt(v_hat) + ADAM_EPS) + ADAM_WD * w)
    return delta, m, v


def _adamw(w, g, m, v, name):
    r, c = w.shape
    tb = _pick(r, (128, 496))

    def body(w_ref, g_ref, m_ref, v_ref, go_ref, d_ref, mo_ref, vo_ref):
        gv = g_ref[...]
        d, mn, vn = _adamw_math(w_ref[...], gv, m_ref[...], v_ref[...])
        go_ref[...] = gv
        d_ref[...] = d
        mo_ref[...] = mn
        vo_ref[...] = vn

    blk = pl.BlockSpec((tb, c), lambda i: (i, 0))
    return pl.pallas_call(
        body, name=name, grid=(r // tb,), in_specs=[blk] * 4, out_specs=[blk] * 4,
        out_shape=[jax.ShapeDtypeStruct((r, c), F32)] * 4, compiler_params=_cp(),
    )(w, g, m, v)


def _adamw_window(w, g_win, m, v, shard, name):
    r, c = w.shape
    tb = _pick(r, (128,))

    def body(s_ref, w_ref, g_ref, m_ref, v_ref, go_ref, d_ref, mo_ref, vo_ref, g_scr):
        gw = g_ref[...]
        for s in range(N_SHARD):
            @pl.when(s_ref[0] == s)
            def _():
                g_scr[...] = _from_window(gw, s)

        gv = g_scr[:, :c]
        d, mn, vn = _adamw_math(w_ref[...], gv, m_ref[...], v_ref[...])
        go_ref[...] = gv
        d_ref[...] = d
        mo_ref[...] = mn
        vo_ref[...] = vn

    blk = pl.BlockSpec((tb, c), lambda i, s: (i, 0))
    return pl.pallas_call(
        body, name=name,
        grid_spec=pltpu.PrefetchScalarGridSpec(
            num_scalar_prefetch=1, grid=(r // tb,),
            in_specs=[blk, pl.BlockSpec((tb, SHARD_PAD), lambda i, s: (i, 0)), blk, blk], out_specs=[blk] * 4,
            scratch_shapes=[pltpu.VMEM((tb, SHARD_PAD), F32)]),
        out_shape=[jax.ShapeDtypeStruct((r, c), F32)] * 4, compiler_params=_cp(),
    )(shard, w, g_win, m, v)


def _small_update(gathered, w, m, v):
    def body(p_ref, w_ref, m_ref, v_ref, g_ref, d_ref, mo_ref, vo_ref):
        g = p_ref[0]
        for i in range(1, N_DEV):
            g = g + p_ref[i]
        d, mn, vn = _adamw_math(w_ref[...], g, m_ref[...], v_ref[...])
        g_ref[...] = g
        d_ref[...] = d
        mo_ref[...] = mn
        vo_ref[...] = vn

    full = pl.BlockSpec((32, 128), lambda i: (0, 0))
    return pl.pallas_call(
        body, name="small_update", grid=(1,),
        in_specs=[pl.BlockSpec((N_DEV, 32, 128), lambda i: (0, 0, 0)), full, full, full], out_specs=[full] * 4,
        out_shape=[jax.ShapeDtypeStruct((32, 128), F32)] * 4, compiler_params=_cp(),
    )(gathered, w, m, v)


_ANY = pl.BlockSpec(memory_space=pl.ANY)


def _place():
    x, y, c = lax.axis_index("x"), lax.axis_index("y"), lax.axis_index("c")
    chips = [(1 - x, y), (x, 1 - y), (1 - x, 1 - y)]
    return x, y, c, chips


def _gather_shards(arrs):
    n = len(arrs)

    def body(*refs):
        ins, outs = refs[:n], refs[n:2 * n]
        send_sems, recv_sems, local_sems = refs[2 * n:2 * n + 3]
        bufs = refs[2 * n + 3:]
        x, y, c, chips = _place()
        me = 2 * x + y
        sibling = (x, y, 1 - c)
        sends = []
        for a in range(n):
            half = ins[a].shape[0] // 2
            mine = pl.ds(pl.multiple_of(c * half, 16), half)
            for j, (qx, qy) in enumerate(chips):
                cp = pltpu.make_async_remote_copy(
                    src_ref=ins[a].at[mine], dst_ref=outs[a].at[me, mine],
                    send_sem=send_sems.at[6 * a + j], recv_sem=recv_sems.at[6 * a + j],
                    device_id=(qx, qy, c), device_id_type=MESH)
                cp.start()
                sends.append(cp)
        for a in range(n):
            step = bufs[a].shape[0]
            for r0 in range(0, ins[a].shape[0], step):
                rows = pl.ds(r0, step)
                load = pltpu.make_async_copy(ins[a].at[rows], bufs[a], local_sems.at[2 * a])
                load.start()
                load.wait()
                store = pltpu.make_async_copy(bufs[a], outs[a].at[me, rows], local_sems.at[2 * a + 1])
                store.start()
                store.wait()
        for a in range(n):
            half = ins[a].shape[0] // 2
            mine = pl.ds(pl.multiple_of(c * half, 16), half)
            for j, (qx, qy) in enumerate(chips):
                q = 2 * qx + qy
                landed = outs[a].at[q, mine]
                pltpu.make_async_remote_copy(
                    src_ref=landed, dst_ref=landed, send_sem=send_sems.at[6 * a + j], recv_sem=recv_sems.at[6 * a + j],
                    device_id=(qx, qy, c), device_id_type=MESH).wait_recv()
                fw = pltpu.make_async_remote_copy(
                    src_ref=landed, dst_ref=landed, send_sem=send_sems.at[6 * a + 3 + j],
                    recv_sem=recv_sems.at[6 * a + 3 + j], device_id=sibling, device_id_type=MESH)
                fw.start()
                sends.append(fw)
        for a in range(n):
            half = ins[a].shape[0] // 2
            theirs = pl.ds(pl.multiple_of((1 - c) * half, 16), half)
            for j, (qx, qy) in enumerate(chips):
                q = 2 * qx + qy
                dst = outs[a].at[q, theirs]
                pltpu.make_async_remote_copy(
                    src_ref=dst, dst_ref=dst, send_sem=send_sems.at[6 * a + 3 + j], recv_sem=recv_sems.at[6 * a + 3 + j],
                    device_id=sibling, device_id_type=MESH).wait_recv()
        for cp in sends:
            cp.wait_send()

    return pl.pallas_call(
        body, name="gather_shards", in_specs=[_ANY] * n, out_specs=[_ANY] * n,
        out_shape=[jax.ShapeDtypeStruct((N_SHARD,) + a.shape, a.dtype) for a in arrs],
        scratch_shapes=[pltpu.SemaphoreType.DMA((6 * n,)), pltpu.SemaphoreType.DMA((6 * n,)),
                        pltpu.SemaphoreType.DMA((2 * n,))]
        + [pltpu.VMEM((_pick(a.shape[0], (256, 496)), a.shape[1]), a.dtype) for a in arrs],
        compiler_params=pltpu.CompilerParams(has_side_effects=True, vmem_limit_bytes=VMEM_LIMIT),
    )(*arrs)


def _pair_reduce_send(grads):
    n = len(grads)

    def body(*refs):
        ins, outs = refs[:n], refs[n:2 * n]
        send_sems, recv_sems = refs[2 * n:]
        x, y, c, _ = _place()
        sibling = (x, y, 1 - c)
        cps = []
        for a in range(n):
            half = ins[a].shape[1] // 2
            theirs = pl.ds(pl.multiple_of((1 - c) * half, 8), half)
            cp = pltpu.make_async_remote_copy(
                src_ref=ins[a].at[:, theirs], dst_ref=outs[a], send_sem=send_sems.at[a], recv_sem=recv_sems.at[a],
                device_id=sibling, device_id_type=MESH)
            cp.start()
            cps.append(cp)
        for cp in cps:
            cp.wait()

    return pl.pallas_call(
        body, name="pair_reduce_send", in_specs=[_ANY] * n, out_specs=[_ANY] * n,
        out_shape=[jax.ShapeDtypeStruct((g.shape[0], g.shape[1] // 2, g.shape[2]), g.dtype) for g in grads],
        scratch_shapes=[pltpu.SemaphoreType.DMA((n,)), pltpu.SemaphoreType.DMA((n,))],
        compiler_params=pltpu.CompilerParams(has_side_effects=True),
    )(*grads)


def _chip_exchange(parts):
    n = len(parts)

    def body(*refs):
        ins, outs = refs[:n], refs[n:2 * n]
        send_sems, recv_sems = refs[2 * n:]
        x, y, c, chips = _place()
        me = 2 * x + y
        cps = []
        for a in range(n):
            for j, (qx, qy) in enumerate(chips):
                q = 2 * qx + qy
                cp = pltpu.make_async_remote_copy(
                    src_ref=ins[a].at[q], dst_ref=outs[a].at[me], send_sem=send_sems.at[3 * a + j],
                    recv_sem=recv_sems.at[3 * a + j], device_id=(qx, qy, c), device_id_type=MESH)
                cp.start()
                cps.append(cp)
        for a in range(n):
            for j, (qx, qy) in enumerate(chips):
                q = 2 * qx + qy
                dst = outs[a].at[q]
                pltpu.make_async_remote_copy(
                    src_ref=dst, dst_ref=dst, send_sem=send_sems.at[3 * a + j], recv_sem=recv_sems.at[3 * a + j],
                    device_id=(qx, qy, c), device_id_type=MESH).wait_recv()
        for cp in cps:
            cp.wait_send()

    return pl.pallas_call(
        body, name="chip_exchange", in_specs=[_ANY] * n, out_specs=[_ANY] * n,
        out_shape=[jax.ShapeDtypeStruct(p.shape, p.dtype) for p in parts],
        scratch_shapes=[pltpu.SemaphoreType.DMA((3 * n,)), pltpu.SemaphoreType.DMA((3 * n,))],
        compiler_params=pltpu.CompilerParams(has_side_effects=True),
    )(*parts)


def _pair_allgather(fulls):
    n = len(fulls)

    def body(*refs):
        outs = refs[n:2 * n]
        send_sems, recv_sems = refs[2 * n:]
        x, y, c, _ = _place()
        sibling = (x, y, 1 - c)
        cps = []
        for a in range(n):
            half = outs[a].shape[0] // 2
            mine = outs[a].at[pl.ds(pl.multiple_of(c * half, 8), half)]
            cp = pltpu.make_async_remote_copy(
                src_ref=mine, dst_ref=mine, send_sem=send_sems.at[a], recv_sem=recv_sems.at[a],
                device_id=sibling, device_id_type=MESH)
            cp.start()
            cps.append(cp)
        for a in range(n):
            half = outs[a].shape[0] // 2
            theirs = outs[a].at[pl.ds(pl.multiple_of((1 - c) * half, 8), half)]
            pltpu.make_async_remote_copy(
                src_ref=theirs, dst_ref=theirs, send_sem=send_sems.at[a], recv_sem=recv_sems.at[a],
                device_id=sibling, device_id_type=MESH).wait_recv()
        for cp in cps:
            cp.wait_send()

    return pl.pallas_call(
        body, name="pair_allgather", in_specs=[_ANY] * n, out_specs=[_ANY] * n,
        out_shape=[jax.ShapeDtypeStruct(f.shape, f.dtype) for f in fulls],
        input_output_aliases={a: a for a in range(n)},
        scratch_shapes=[pltpu.SemaphoreType.DMA((n,)), pltpu.SemaphoreType.DMA((n,))],
        compiler_params=pltpu.CompilerParams(has_side_effects=True),
    )(*fulls)


def _allgather_small(slab):
    def body(s_ref, out_ref, send_sems, recv_sems):
        x, y, c, _ = _place()
        me = 4 * x + 2 * y + c
        out_ref[me] = s_ref[...]
        cps = []
        for mask in range(1, N_DEV):
            peer = (x ^ (mask >> 2), y ^ ((mask >> 1) & 1), c ^ (mask & 1))
            cp = pltpu.make_async_remote_copy(
                src_ref=s_ref, dst_ref=out_ref.at[me], send_sem=send_sems.at[mask - 1], recv_sem=recv_sems.at[mask - 1],
                device_id=peer, device_id_type=MESH)
            cp.start()
            cps.append(cp)
        for mask in range(1, N_DEV):
            peer = (x ^ (mask >> 2), y ^ ((mask >> 1) & 1), c ^ (mask & 1))
            dst = out_ref.at[4 * peer[0] + 2 * peer[1] + peer[2]]
            pltpu.make_async_remote_copy(
                src_ref=dst, dst_ref=dst, send_sem=send_sems.at[mask - 1], recv_sem=recv_sems.at[mask - 1],
                device_id=peer, device_id_type=MESH).wait_recv()
        for cp in cps:
            cp.wait_send()

    vm = pl.BlockSpec(memory_space=pltpu.VMEM)
    return pl.pallas_call(
        body, name="allgather_small", in_specs=[vm], out_specs=vm,
        out_shape=jax.ShapeDtypeStruct((N_DEV,) + slab.shape, slab.dtype),
        scratch_shapes=[pltpu.SemaphoreType.DMA((N_DEV - 1,)), pltpu.SemaphoreType.DMA((N_DEV - 1,))],
        compiler_params=pltpu.CompilerParams(has_side_effects=True),
    )(slab)


def _pack_b(w_mem_kv, w_br_dn, w_br_sb, w_br_mem, w_out, conv_w):
    parts = [w_mem_kv.reshape(128, D_MODEL), w_br_dn, w_br_sb, w_br_mem.reshape(64, D_MODEL), w_out,
             conv_w.reshape(3, D_MODEL)]
    rows = sum(p.shape[0] for p in parts)
    return jnp.concatenate(parts + [jnp.zeros((B_ROWS - rows, D_MODEL), parts[0].dtype)], axis=0)


def _unpack_b(slab):
    return (slab[B_MEMKV:B_BRDN].reshape(1, 256, 512), slab[B_BRDN:B_BRSB].reshape(1, 256, D_MODEL),
            slab[B_BRSB:B_BRMEM].reshape(1, 256, D_MODEL), slab[B_BRMEM:B_OUT].reshape(1, 256, 256),
            slab[B_OUT:B_CONV].reshape(1, 256, D_MODEL), slab[B_CONV:B_CONV + 3].reshape(1, 4, 768))


def _pack_small(norm_g, mem_norm_g, final_g, dn_norm_g, a_log, dt_bias, loss=None):
    slab = jnp.zeros((32, 128), F32)
    slab = slab.at[S_NORM:S_NORM + 8].set(norm_g.reshape(8, 128))
    slab = slab.at[S_MEMNORM:S_MEMNORM + 8].set(mem_norm_g.reshape(8, 128))
    slab = slab.at[S_FINAL:S_FINAL + 8].set(final_g.reshape(8, 128))
    slab = slab.at[S_DNNORM].set(dn_norm_g.reshape(128))
    slab = slab.at[S_ALOG, :N_HEADS].set(a_log.reshape(N_HEADS))
    slab = slab.at[S_DTB, :N_HEADS].set(dt_bias.reshape(N_HEADS))
    if loss is not None:
        slab = slab.at[S_LOSS, 0].set(loss)
    return slab


def _unpack_small(slab):
    return (slab[S_NORM:S_NORM + 8].reshape(1, D_MODEL), slab[S_MEMNORM:S_MEMNORM + 8].reshape(1, D_MODEL),
            slab[S_FINAL:S_FINAL + 8].reshape(D_MODEL), slab[S_DNNORM].reshape(1, 128),
            slab[S_ALOG, :N_HEADS].reshape(1, N_HEADS), slab[S_DTB, :N_HEADS].reshape(1, N_HEADS))


def _reorder_w_in(w_full):
    pad = jnp.zeros((w_full.shape[0], W_R - IN_WIDTH), w_full.dtype)
    return jnp.concatenate([w_full[:, :4096], w_full[:, 4112:], w_full[:, 4096:4112], pad], axis=1)


def _windows_to_w_r(win):
    b = 128
    s0, s1, s2, s3 = win[0], win[1], win[2], win[3]
    e1, e2, e3 = WIN_START[1] * b, WIN_START[2] * b, WIN_START[3] * b
    n1, n2 = e2 - e1, e3 - e2
    return jnp.concatenate([
        s0[:, :e1], s0[:, e1:e1 + b] + s1[:, :b],
        s1[:, b:n1], s1[:, n1:n1 + b] + s2[:, :b],
        s2[:, b:n2], s2[:, n2:n2 + b] + s3[:, :b],
        s3[:, b:], s1[:, _S1_BA_POS:]], axis=1)


def _dproj_windows(dproj_r):
    b = 128
    pieces = []
    for s in range(N_SHARD):
        lo = WIN_START[s] * b
        if s == 1:
            pieces += [dproj_r[:, lo:lo + _S1_BA_POS], dproj_r[:, C_BA:C_BA + b]]
        else:
            pieces.append(dproj_r[:, lo:lo + SHARD_PAD])
    return jnp.concatenate(pieces, axis=1)


def _local_step(x, mem, tgt, norm_g, mem_norm_g, w_r, w_sh, conv_w, a_log, dt_bias, dn_norm_g, w_mem_kv, w_br_dn,
                w_br_sb, w_br_mem, w_out, final_g):
    t = x.shape[0]
    final_row = final_g.reshape(1, D_MODEL)
    alog_row = jnp.zeros((1, 128), F32).at[0, N_HEADS:2 * N_HEADS].set(a_log.reshape(N_HEADS))
    dtb_row = jnp.zeros((1, 128), F32).at[0, N_HEADS:2 * N_HEADS].set(dt_bias.reshape(N_HEADS))

    h = _rmsnorm_fwd(x, norm_g, "norm_fwd")
    proj = _mm(h, w_r, "nn", "in_proj")
    qkv = _dn_prep_fwd(proj, conv_w)
    beta_t, g_t = _dn_gate_fwd(proj, alog_row, dtb_row)
    dn_u, dn_w, dn_qd, dn_kd, dn_a, tinv_all, dn_el = _dn_intra_fwd(qkv, beta_t, g_t)
    o_dn, dn_vn, s_all = _dn_scan_fwd(dn_u, dn_w, dn_qd, dn_kd, dn_a, dn_el)
    o_dn_g = _dn_post_fwd(o_dn, proj, dn_norm_g)
    o_sb, o_sb_g, sb_l = _sb_fwd(proj)
    mem_n = _rmsnorm_fwd(mem, mem_norm_g, "mem_norm_fwd")
    mkv = _mm(mem_n, w_mem_kv, "nn", "mem_kv")
    o_m, o_m_g = _mem_fwd(proj, mkv)
    y_dn = _mm(o_dn_g, w_br_dn, "nn", "br_dn")
    y_sb = _mm(o_sb_g, w_br_sb, "nn", "br_sb")
    y_m = _mm(o_m_g, w_br_mem, "nn", "br_mem")
    merged = _merge_fwd(proj, y_dn, y_sb, y_m)
    mo = _mm(merged, w_out, "nn", "out_proj")
    d_out, d_out_b, loss_row, g_final = _final_loss(x, mo, final_row, tgt)

    g_w_out = _mm(merged, d_out_b, "tn", "g_w_out")
    d_merged = _mm(d_out_b, w_out, "nt", "d_merged")
    dy_dn, dy_sb, dy_m, dg1, dg2, dg3 = _merge_bwd(proj, y_dn, y_sb, y_m, d_merged)
    g_w_br_dn = _mm(o_dn_g, dy_dn, "tn", "g_w_br_dn")
    g_w_br_sb = _mm(o_sb_g, dy_sb, "tn", "g_w_br_sb")
    g_w_br_mem = _mm(o_m_g, dy_m, "tn", "g_w_br_mem")
    d_o_dn_g = _mm(dy_dn, w_br_dn, "nt", "d_o_dn")
    d_o_sb_g = _mm(dy_sb, w_br_sb, "nt", "d_o_sb")
    d_o_m_g = _mm(dy_m, w_br_mem, "nt", "d_o_mem")

    d_mq, d_mz, d_mkv = _mem_bwd(proj, mkv, o_m, d_o_m_g)
    d_mkv_b = _cast_bf16(d_mkv, "cast_dmkv")
    g_w_mem_kv = _mm(mem_n, d_mkv_b, "tn", "g_w_mem_kv")
    d_mem_n = _mm(d_mkv_b, w_mem_kv, "nt", "d_mem_n")
    _, g_mem_norm = _rmsnorm_bwd(mem, mem_norm_g, d_mem_n, jnp.zeros_like(mem), "mem_norm_bwd")

    d_sq, d_sk, d_sv, d_sz = _sb_bwd(proj, o_sb, sb_l, d_o_sb_g)

    d_o_dn, d_dnz, g_dn_norm = _dn_post_bwd(o_dn, proj, dn_norm_g, d_o_dn_g)
    d_vnew, d_kd, d_qd, d_w, d_el = _dn_scan_bwd(dn_w, dn_qd, dn_kd, dn_a, dn_el, dn_vn, s_all, d_o_dn)
    d_qn, d_kn, d_vn, dbeta_t, dg_t = _dn_intra_bwd(qkv, beta_t, g_t, tinv_all, dn_vn, d_o_dn, d_vnew, d_kd, d_qd, d_w, d_el)
    d_conv_in, g_conv = _dn_prep_bwd(proj, conv_w, d_qn, d_kn, d_vn)
    d_ba, g_alog_row, g_dtb_row = _dn_gate_bwd(proj, alog_row, dtb_row, dbeta_t, dg_t)

    dproj_sh = _dproj_windows(
        jnp.concatenate([d_conv_in, d_dnz, d_sq, d_sk, d_sv, d_sz, d_mq, d_mz, dg1, dg2, dg3, d_ba], axis=1))
    g_w_sh = _mm(h, dproj_sh, "tn", "g_w_in", out_dtype=BF16, out_shards=N_SHARD)
    dh = _mm(dproj_sh, w_sh, "nt", "d_h")
    grad_x, g_norm = _rmsnorm_bwd(x, norm_g, dh, d_out, "norm_bwd")

    small = dict(norm_g=g_norm, mem_norm_g=g_mem_norm, final_g=g_final, dn_norm_g=g_dn_norm,
                 a_log=g_alog_row[:, N_HEADS:2 * N_HEADS], dt_bias=g_dtb_row[:, N_HEADS:2 * N_HEADS])
    big = dict(w_sh=g_w_sh, conv_w=g_conv, w_mem_kv=g_w_mem_kv, w_br_dn=g_w_br_dn, w_br_sb=g_w_br_sb,
               w_br_mem=g_w_br_mem, w_out=g_w_out)
    return loss_row[0, 0], grad_x, small, big


def _reduce_scatter(grads):
    x, y, c = lax.axis_index("x"), lax.axis_index("y"), lax.axis_index("c")
    core = jnp.reshape(c, (1,)).astype(jnp.int32)
    place = jnp.stack([2 * x + y, c]).astype(jnp.int32)
    recv = _pair_reduce_send(grads)
    parts = [_pair_add(g, r, core, "pair_add") for g, r in zip(grads, recv)]
    by_chip = _chip_exchange(parts)
    fulls = [_chip_sum(p, b, place, "chip_sum") for p, b in zip(parts, by_chip)]
    return _pair_allgather(fulls)


def kernel(x, mem, norm_g, mem_norm_g, w_in, conv_w, a_log, dt_bias, dn_norm_g, w_mem_kv, w_br_dn, w_br_sb, w_br_mem, w_out, final_g, loss_target, m_norm_g, m_mem_norm_g, m_w_in, m_conv_w, m_a_log, m_dt_bias, m_dn_norm_g, m_w_mem_kv, m_w_br_dn, m_w_br_sb, m_w_br_mem, m_w_out, m_final_g, v_norm_g, v_mem_norm_g, v_w_in, v_conv_w, v_a_log, v_dt_bias, v_dn_norm_g, v_w_mem_kv, v_w_br_dn, v_w_br_sb, v_w_br_mem, v_w_out, v_final_g):
    w_a = w_in[0]
    w_b = _pack_b(w_mem_kv[0], w_br_dn[0], w_br_sb[0], w_br_mem[0], w_out[0], conv_w[0])
    m_b = _pack_b(m_w_mem_kv[0], m_w_br_dn[0], m_w_br_sb[0], m_w_br_mem[0], m_w_out[0], m_conv_w[0])
    v_b = _pack_b(v_w_mem_kv[0], v_w_br_dn[0], v_w_br_sb[0], v_w_br_mem[0], v_w_out[0], v_conv_w[0])

    shard = jnp.reshape(2 * lax.axis_index("x") + lax.axis_index("y"), (1,)).astype(jnp.int32)
    ga, gb = _gather_shards([_cast_to_window(w_a, shard, "cast_w_in"), _cast_bf16(w_b, "cast_w_b")])
    w_r = _windows_to_w_r(ga)
    f_mem_kv = gb[:, B_MEMKV:B_BRDN].reshape(N_SHARD * 256, 512)
    f_br_dn = gb[:, B_BRDN:B_BRSB].reshape(N_SHARD * 256, D_MODEL)
    f_br_sb = gb[:, B_BRSB:B_BRMEM].reshape(N_SHARD * 256, D_MODEL)
    f_br_mem = gb[:, B_BRMEM:B_OUT].reshape(N_SHARD, 256, 256).transpose(1, 0, 2).reshape(256, D_MODEL)
    f_out = gb[:, B_OUT:B_CONV].reshape(N_SHARD * 256, D_MODEL)
    f_conv = gb[:, B_CONV:B_CONV + 3].reshape(N_SHARD, 4, 768).transpose(1, 0, 2).reshape(4, 3 * D_MODEL).astype(F32)

    loss, grad_x, small, big = _local_step(
        x[0], mem[0], loss_target[0], norm_g, mem_norm_g, w_r, ga, f_conv, a_log, dt_bias, dn_norm_g,
        f_mem_kv, f_br_dn, f_br_sb, f_br_mem, f_out, final_g)

    g_b = jnp.stack([
        _pack_b(big["w_mem_kv"][256 * s:256 * (s + 1)], big["w_br_dn"][256 * s:256 * (s + 1)],
                big["w_br_sb"][256 * s:256 * (s + 1)], big["w_br_mem"][:, 256 * s:256 * (s + 1)],
                big["w_out"][256 * s:256 * (s + 1)], big["conv_w"][:, 768 * s:768 * (s + 1)])
        for s in range(N_SHARD)]).astype(BF16)
    gs_in, gs_b = _reduce_scatter([big["w_sh"], g_b])

    gr_in, d_in, nm_in, nv_in = _adamw_window(w_a, gs_in, m_w_in[0], v_w_in[0], shard, "adamw_w_in")
    gr_b, d_b, nm_b, nv_b = _adamw(w_b, gs_b, m_b, v_b, "adamw_b")

    part = _pack_small(small["norm_g"], small["mem_norm_g"], small["final_g"], small["dn_norm_g"],
                       small["a_log"], small["dt_bias"], loss)
    w_s = _pack_small(norm_g, mem_norm_g, final_g, dn_norm_g, a_log, dt_bias)
    m_s = _pack_small(m_norm_g, m_mem_norm_g, m_final_g, m_dn_norm_g, m_a_log, m_dt_bias)
    v_s = _pack_small(v_norm_g, v_mem_norm_g, v_final_g, v_dn_norm_g, v_a_log, v_dt_bias)
    g_s, d_s, nm_s, nv_s = _small_update(_allgather_small(part), w_s, m_s, v_s)

    def assemble(slab_small, a_in, slab_b):
        s_norm, s_memnorm, s_final, s_dnnorm, s_alog, s_dtb = _unpack_small(slab_small)
        b_memkv, b_brdn, b_brsb, b_brmem, b_out, b_conv = _unpack_b(slab_b)
        return [s_norm, s_memnorm, a_in.reshape(1, D_MODEL, IN_WIDTH // N_SHARD), b_conv, s_alog, s_dtb, s_dnnorm,
                b_memkv, b_brdn, b_brsb, b_brmem, b_out, s_final]

    outs = [g_s[S_LOSS, 0], grad_x.reshape(1, -1, D_MODEL)]
    outs += assemble(g_s, gr_in, gr_b)
    outs += assemble(d_s, d_in, d_b)
    outs += assemble(nm_s, nm_in, nm_b)
    outs += assemble(nv_s, nv_in, nv_b)
    return tuple(outs)
```

```python
import functools
import math

import jax
import jax.numpy as jnp
from jax import lax
from jax.experimental import pallas as pl
from jax.experimental.pallas import tpu as pltpu

F32 = jnp.float32
BF16 = jnp.bfloat16
MESH = pl.DeviceIdType.MESH
HIGHEST = lax.Precision.HIGHEST

D_MODEL = 1024
N_HEADS = 8
D_HEAD = 128
DN_CHUNK = 64
DN_GROUP = 8
DN_SCAN_GROUP = 4
SB_BLOCK = 128
SB_HEADS_PER_STEP = 2
SB_QBLOCK = 256
MEM_HEADS = 4
MEM_DH = 64
MEM_W = MEM_HEADS * MEM_DH
NORM_EPS = 1e-6
IN_WIDTH = 11792
N_SHARD = 4
SHARD_W = IN_WIDTH // N_SHARD
SHARD_PAD = 3072
N_DEV = 8

C_DNZ = 3072
C_SBQ = 4096
C_SBZ = 7168
C_MQ = 8192
C_MZ = 8448
C_GATES = 8704
C_BA = 11776
W_R = 11904

ADAM_LR = 0.001
ADAM_B1 = 0.9
ADAM_B2 = 0.999
ADAM_EPS = 1e-08
ADAM_WD = 0.01
ADAM_STEP = 10

VMEM_LIMIT = 56 * 1024 * 1024

B_ROWS = 992
B_MEMKV, B_BRDN, B_BRSB, B_BRMEM, B_OUT, B_CONV = 0, 128, 384, 640, 704, 960
S_NORM, S_MEMNORM, S_FINAL, S_DNNORM, S_ALOG, S_DTB, S_LOSS = 0, 8, 16, 24, 25, 26, 27


def _cp(**kw):
    return pltpu.CompilerParams(vmem_limit_bytes=VMEM_LIMIT, **kw)


def _dot(a, b, dims):
    lead = a.ndim - 2
    ca, cb = {"nn": (1, 0), "nt": (1, 1), "tn": (0, 0)}[dims]
    batch = tuple(range(lead))
    return lax.dot_general(a, b, (((ca + lead,), (cb + lead,)), (batch, batch)), preferred_element_type=F32)


def _chunks(x):
    return x.reshape(x.shape[0] // DN_CHUNK, DN_CHUNK, x.shape[1])


def _unchunk(x):
    return x.reshape(x.shape[0] * x.shape[1], x.shape[2])


def _bdot(a, b, dims):
    return _dot(a.astype(BF16), b.astype(BF16), dims)


def _split(a):
    hi = a.astype(BF16)
    return hi, (a - hi.astype(F32)).astype(BF16)


def _dot3(a, b, dims):
    a1, a2 = _split(a)
    b1, b2 = _split(b)
    return _dot(a1, b1, dims) + (_dot(a1, b2, dims) + _dot(a2, b1, dims))


def _split_dot(a, ones_bf16):
    hi, lo = _split(a.reshape(-1, a.shape[-1]))
    out = _dot(hi, ones_bf16, "nn") + _dot(lo, ones_bf16, "nn")
    return out.reshape(a.shape[:-1] + (ones_bf16.shape[1],))


def _sigmoid(x):
    return 1.0 / (1.0 + jnp.exp(-x))


def _log1p_small(u):
    return jnp.where(u < 1e-2, u * (1.0 - u * (0.5 - u * (1.0 / 3.0))), jnp.log(1.0 + u))


def _log_sigmoid(z):
    return jnp.minimum(z, 0.0) - _log1p_small(jnp.exp(-jnp.abs(z)))


def _pick(dim, cands):
    for c in cands:
        if dim % c == 0:
            return c
    return dim


def _mm(a, b, dims, name, out_dtype=F32, out_shards=1):
    ta, tb = dims[0] == "t", dims[1] == "t"
    m, k = (a.shape[1], a.shape[0]) if ta else a.shape
    b_shards = b.shape[0] if b.ndim == 3 else 1
    n = b.shape[-2] if tb else b.shape[-1]
    tm = _pick(m, (1024, 512, 256))
    tn = _pick(n // out_shards, (512, 384, 256, 128))
    tk = _pick(k // b_shards, (1024, 512, 384, 256))
    nk = k // tk

    def body(a_ref, b_ref, o_ref, acc_ref):
        kk = pl.program_id(2)

        @pl.when(kk == 0)
        def _():
            acc_ref[...] = jnp.zeros_like(acc_ref)

        acc_ref[...] += _bdot(a_ref[...], b_ref[...], dims)

        @pl.when(kk == nk - 1)
        def _():
            o_ref[...] = acc_ref[...].astype(out_dtype)

    a_spec = pl.BlockSpec((tk, tm), lambda i, j, q: (q, i)) if ta else pl.BlockSpec((tm, tk), lambda i, j, q: (i, q))
    if b_shards > 1:
        per_k = k // b_shards // tk
        b_spec = pl.BlockSpec((None, tn, tk), lambda i, j, q: (q // per_k, j, q % per_k))
    else:
        b_spec = pl.BlockSpec((tn, tk), lambda i, j, q: (j, q)) if tb else pl.BlockSpec((tk, tn), lambda i, j, q: (q, j))
    if out_shards > 1:
        per_n = n // out_shards // tn
        out_spec = pl.BlockSpec((None, tm, tn), lambda i, j, q: (j // per_n, i, j % per_n))
        out_shape = jax.ShapeDtypeStruct((out_shards, m, n // out_shards), out_dtype)
    else:
        out_spec = pl.BlockSpec((tm, tn), lambda i, j, q: (i, j))
        out_shape = jax.ShapeDtypeStruct((m, n), out_dtype)
    return pl.pallas_call(
        body, name=name, grid=(m // tm, n // tn, nk),
        in_specs=[a_spec, b_spec], out_specs=out_spec, out_shape=out_shape,
        scratch_shapes=[pltpu.VMEM((tm, tn), F32)],
        compiler_params=_cp(dimension_semantics=("parallel", "parallel", "arbitrary")),
    )(a, b)


def _rmsnorm_fwd(x, g, name):
    t, d = x.shape
    tb = _pick(t, (512, 256))

    def body(x_ref, g_ref, h_ref):
        xv = x_ref[...]
        r = lax.rsqrt(jnp.mean(xv * xv, axis=-1, keepdims=True) + NORM_EPS)
        h_ref[...] = ((xv * r) * g_ref[...]).astype(BF16)

    return pl.pallas_call(
        body, name=name, grid=(t // tb,),
        in_specs=[pl.BlockSpec((tb, d), lambda i: (i, 0)), pl.BlockSpec((1, d), lambda i: (0, 0))],
        out_specs=pl.BlockSpec((tb, d), lambda i: (i, 0)),
        out_shape=jax.ShapeDtypeStruct((t, d), BF16), compiler_params=_cp(),
    )(x, g)


def _rmsnorm_bwd(x, g, dh, resid, name):
    t, d = x.shape
    tb = _pick(t, (256,))

    def body(x_ref, g_ref, dh_ref, r_ref, dx_ref, dg_ref):
        @pl.when(pl.program_id(0) == 0)
        def _():
            dg_ref[...] = jnp.zeros_like(dg_ref)

        xv = x_ref[...]
        r = lax.rsqrt(jnp.mean(xv * xv, axis=-1, keepdims=True) + NORM_EPS)
        xhat = xv * r
        dhv = dh_ref[...]
        dg_ref[...] += jnp.sum(dhv * xhat, axis=0, keepdims=True)
        dxh = dhv * g_ref[...]
        dx_ref[...] = r_ref[...] + r * (dxh - xhat * jnp.mean(dxh * xhat, axis=-1, keepdims=True))

    row = pl.BlockSpec((tb, d), lambda i: (i, 0))
    vec = pl.BlockSpec((1, d), lambda i: (0, 0))
    return pl.pallas_call(
        body, name=name, grid=(t // tb,), in_specs=[row, vec, row, row], out_specs=[row, vec],
        out_shape=[jax.ShapeDtypeStruct((t, d), F32), jax.ShapeDtypeStruct((1, d), F32)], compiler_params=_cp(),
    )(x, g, dh, resid)


def _conv_silu(xv, w, row):
    y = xv * w[3:4, :]
    for s in (1, 2, 3):
        xs = jnp.where(row >= s, pltpu.roll(xv, s, 0), 0.0)
        y = y + xs * w[3 - s:4 - s, :]
    return y, y * _sigmoid(y)


def _dn_prep_fwd(proj, conv_w):
    t = proj.shape[0]

    def body(p_ref, w_ref, o_ref):
        j = pl.program_id(0)
        xv = p_ref[...]
        row = lax.broadcasted_iota(jnp.int32, xv.shape, 0)
        _, a = _conv_silu(xv, w_ref[...], row)
        inv = lax.rsqrt(jnp.sum(a * a, axis=-1, keepdims=True) + NORM_EPS)
        scale = jnp.where(j < N_HEADS, D_HEAD ** -0.5, 1.0)
        normed = jnp.where(j < 2 * N_HEADS, 1.0, 0.0)
        o_ref[...] = a * (normed * (inv * scale) + (1.0 - normed))

    return pl.pallas_call(
        body, name="dn_prep_fwd", grid=(3 * N_HEADS,),
        in_specs=[pl.BlockSpec((t, D_HEAD), lambda j: (0, j)), pl.BlockSpec((4, D_HEAD), lambda j: (0, j))],
        out_specs=pl.BlockSpec((t, D_HEAD), lambda j: (0, j)),
        out_shape=jax.ShapeDtypeStruct((t, 3 * D_MODEL), F32), compiler_params=_cp(),
    )(proj, conv_w)


def _dn_prep_bwd(proj, conv_w, dq, dk, dv):
    t = proj.shape[0]

    def body(p_ref, w_ref, dq_ref, dk_ref, dv_ref, dp_ref, dw_ref):
        j = pl.program_id(0)
        xv = p_ref[...]
        w = w_ref[...]
        row = lax.broadcasted_iota(jnp.int32, xv.shape, 0)
        y, a = _conv_silu(xv, w, row)
        part = jnp.zeros(xv.shape, jnp.int32) + j // N_HEADS
        dn = jnp.where(part == 0, dq_ref[...], jnp.where(part == 1, dk_ref[...], dv_ref[...]))
        inv = lax.rsqrt(jnp.sum(a * a, axis=-1, keepdims=True) + NORM_EPS)
        scale = jnp.where(j < N_HEADS, D_HEAD ** -0.5, 1.0)
        ds = dn * scale
        da_norm = inv * ds - a * (inv * inv * inv) * jnp.sum(ds * a, axis=-1, keepdims=True)
        normed = jnp.where(j < 2 * N_HEADS, 1.0, 0.0)
        da = normed * da_norm + (1.0 - normed) * dn
        s = _sigmoid(y)
        dy = da * (s * (1.0 + y * (1.0 - s)))
        dx = dy * w[3:4, :]
        dw_ref[3:4, :] = jnp.sum(dy * xv, axis=0, keepdims=True)
        for sft in (1, 2, 3):
            xs = jnp.where(row >= sft, pltpu.roll(xv, sft, 0), 0.0)
            dw_ref[3 - sft:4 - sft, :] = jnp.sum(dy * xs, axis=0, keepdims=True)
            dys = jnp.where(row < t - sft, pltpu.roll(dy, t - sft, 0), 0.0)
            dx = dx + dys * w[3 - sft:4 - sft, :]
        dp_ref[...] = dx.astype(BF16)

    blk = pl.BlockSpec((t, D_HEAD), lambda j: (0, j))
    wblk = pl.BlockSpec((4, D_HEAD), lambda j: (0, j))

    def grad(part):
        return pl.BlockSpec((t, D_HEAD), lambda j: (0, jnp.clip(j - part * N_HEADS, 0, N_HEADS - 1)))

    return pl.pallas_call(
        body, name="dn_prep_bwd", grid=(3 * N_HEADS,), in_specs=[blk, wblk, grad(0), grad(1), grad(2)],
        out_specs=[blk, wblk],
        out_shape=[jax.ShapeDtypeStruct((t, 3 * D_MODEL), BF16), jax.ShapeDtypeStruct((4, 3 * D_MODEL), F32)],
        compiler_params=_cp(),
    )(proj, conv_w, dq, dk, dv)


def _softplus_parts(xv):
    e = jnp.exp(-jnp.abs(xv))
    return jnp.maximum(xv, 0.0) + _log1p_small(e)


def _chunk_scan(v, row, reverse):
    t = v.shape[0]
    pos = row & (DN_CHUNK - 1)
    s = 1
    while s < DN_CHUNK:
        if reverse:
            v = v + jnp.where(pos < DN_CHUNK - s, pltpu.roll(v, t - s, 0), 0.0)
        else:
            v = v + jnp.where(pos >= s, pltpu.roll(v, s, 0), 0.0)
        s *= 2
    return v


def _dn_gate_fwd(proj, alog_row, dtb_row):
    t = proj.shape[0]

    def body(p_ref, al_ref, dt_ref, b_ref, g_ref):
        p = p_ref[...]
        row = lax.broadcasted_iota(jnp.int32, p.shape, 0)
        b_ref[...] = _sigmoid(p)
        g = -jnp.exp(al_ref[...]) * _softplus_parts(p + dt_ref[...])
        g_ref[...] = _chunk_scan(g, row, reverse=False)

    blk = pl.BlockSpec((t, 128), lambda i: (0, C_BA // 128))
    vec = pl.BlockSpec((1, 128), lambda i: (0, 0))
    out = pl.BlockSpec((t, 128), lambda i: (0, 0))
    return pl.pallas_call(
        body, name="dn_gate_fwd", grid=(1,), in_specs=[blk, vec, vec], out_specs=[out, out],
        out_shape=[jax.ShapeDtypeStruct((t, 128), F32)] * 2, compiler_params=_cp(),
    )(proj, alog_row, dtb_row)


def _dn_gate_bwd(proj, alog_row, dtb_row, dbeta, dgc):
    t = proj.shape[0]

    def body(p_ref, al_ref, dt_ref, db_ref, dg_ref, dp_ref, dal_ref, ddt_ref):
        p = p_ref[...]
        row = lax.broadcasted_iota(jnp.int32, p.shape, 0)
        lane = lax.broadcasted_iota(jnp.int32, p.shape, 1)
        s = _sigmoid(p)
        d_b = db_ref[...] * s * (1.0 - s)
        dg = _chunk_scan(dg_ref[...], row, reverse=True)
        xa = p + dt_ref[...]
        ea = jnp.exp(al_ref[...])
        g = -ea * _softplus_parts(xa)
        d_a = dg * (-ea) * _sigmoid(xa)
        dp_ref[...] = jnp.where(lane < N_HEADS, d_b, jnp.where(lane < 2 * N_HEADS, d_a, 0.0)).astype(BF16)
        dal_ref[...] = jnp.sum(dg * g, axis=0, keepdims=True)
        ddt_ref[...] = jnp.sum(d_a, axis=0, keepdims=True)

    blk = pl.BlockSpec((t, 128), lambda i: (0, C_BA // 128))
    vec = pl.BlockSpec((1, 128), lambda i: (0, 0))
    full = pl.BlockSpec((t, 128), lambda i: (0, 0))
    return pl.pallas_call(
        body, name="dn_gate_bwd", grid=(1,), in_specs=[blk, vec, vec, full, full], out_specs=[full, vec, vec],
        out_shape=[jax.ShapeDtypeStruct((t, 128), BF16), jax.ShapeDtypeStruct((1, 128), F32),
                   jax.ShapeDtypeStruct((1, 128), F32)], compiler_params=_cp(),
    )(proj, alog_row, dtb_row, dbeta, dgc)


def _col_to_row(col, eye):
    return jnp.sum(jnp.where(eye, col, 0.0), axis=-2, keepdims=True)


def _row_to_col(rowv, eye):
    return jnp.sum(jnp.where(eye, rowv, 0.0), axis=-1, keepdims=True)


def _tri_inverse(m, ri, ci):
    eye = (ri == ci).astype(F32)
    b16 = (ri >> 4) == (ci >> 4)
    b32 = (ri >> 5) == (ci >> 5)
    m1 = jnp.where(b16, m, 0.0)
    x = eye - m1
    p = _dot3(m1, m1, "nn")
    x = x + _dot3(x, p, "nn")
    p = _dot3(p, p, "nn")
    x = x + _dot3(x, p, "nn")
    p = _dot3(p, p, "nn")
    x = x + _dot3(x, p, "nn")
    c1 = jnp.where(jnp.logical_and(b32, jnp.logical_not(b16)), m, 0.0)
    x = x - _dot3(_dot3(x, c1, "nn"), x, "nn")
    c2 = jnp.where(b32, 0.0, m)
    x = x - _dot3(_dot3(x, c2, "nn"), x, "nn")
    return x


def _dn_chunk_common(q, k, gc, ri, ci):
    eye = ri == ci
    g_row = _col_to_row(gc, eye)
    diff = jnp.minimum(gc - g_row, 0.0)
    gam = jnp.where(ri >= ci, jnp.exp(diff), 0.0)
    kk = _bdot(k, k, "nt")
    qk = _bdot(q, k, "nt")
    rcol = lax.broadcasted_iota(jnp.int32, gc.shape, gc.ndim - 2)
    last = jnp.sum(jnp.where(rcol == DN_CHUNK - 1, gc, 0.0), axis=-2, keepdims=True)
    e_g = jnp.exp(gc)
    dec = jnp.exp(last - gc)
    return eye, gam, kk, qk, last, e_g, dec, rcol


def _dn_specs(t, rows_blk):
    def head(off):
        return pl.BlockSpec((rows_blk, D_HEAD), lambda g, h: (g, off + h))

    lanes = pl.BlockSpec((rows_blk, 128), lambda g, h: (g, 0))
    hm = pl.BlockSpec((None, rows_blk, D_HEAD), lambda g, h: (h, g, 0))
    sq = pl.BlockSpec((1, rows_blk, DN_CHUNK), lambda g, h: (h, g, 0))
    tile = pl.BlockSpec((1, rows_blk // DN_CHUNK, 8, 128), lambda g, h: (h, g, 0, 0))
    return head, lanes, hm, sq, tile


def _head_column(slab, lane_idx):
    lane = lax.broadcasted_iota(jnp.int32, slab.shape, 1)
    return _chunks(jnp.sum(jnp.where(lane == lane_idx, slab, 0.0), axis=1, keepdims=True))


def _dn_intra_fwd(qkv, beta_t, g_t):
    t = qkv.shape[0]
    n_chunks = t // DN_CHUNK
    rows_blk = min(DN_GROUP * DN_CHUNK, t)

    def body(q_ref, k_ref, v_ref, b_ref, g_ref, u_ref, w_ref, qd_ref, kd_ref, a_ref, ti_ref, el_ref):
        ri = lax.broadcasted_iota(jnp.int32, (DN_CHUNK, DN_CHUNK), 0)
        ci = lax.broadcasted_iota(jnp.int32, (DN_CHUNK, DN_CHUNK), 1)
        h = pl.program_id(1)
        q, k, v = (_chunks(r[...]) for r in (q_ref, k_ref, v_ref))
        b, gc = _head_column(b_ref[...], h), _head_column(g_ref[...], h + N_HEADS)
        _, gam, kk, qk, last, e_g, dec, _ = _dn_chunk_common(q, k, gc, ri, ci)
        tinv = _tri_inverse(jnp.where(ri > ci, b * kk * gam, 0.0), ri, ci)
        u_ref[...] = _unchunk(_bdot(tinv, v * b, "nn"))
        w_ref[...] = _unchunk(_bdot(tinv, k * (b * e_g), "nn"))
        qd_ref[...] = _unchunk(q * e_g)
        kd_ref[...] = _unchunk(k * dec)
        a_ref[0] = _unchunk(qk * gam)
        ti_ref[0] = _unchunk(tinv)
        el_ref[0] = jnp.broadcast_to(jnp.exp(last), (rows_blk // DN_CHUNK, 8, 128))

    head, lanes, hm, sq, tile = _dn_specs(t, rows_blk)
    act = jax.ShapeDtypeStruct((N_HEADS, t, D_HEAD), F32)
    sqs = jax.ShapeDtypeStruct((N_HEADS, t, DN_CHUNK), F32)
    return pl.pallas_call(
        body, name="dn_intra_fwd", grid=(t // rows_blk, N_HEADS),
        in_specs=[head(0), head(N_HEADS), head(2 * N_HEADS), lanes, lanes],
        out_specs=[hm] * 4 + [sq, sq, tile],
        out_shape=[act] * 4 + [sqs, sqs, jax.ShapeDtypeStruct((N_HEADS, n_chunks, 8, 128), F32)],
        compiler_params=_cp(),
    )(qkv, qkv, qkv, beta_t, g_t)


def _dn_scan_specs(t, rows_blk, reverse):
    n_groups = t // rows_blk

    def at(g):
        return n_groups - 1 - g if reverse else g

    per = rows_blk // DN_CHUNK
    act = pl.BlockSpec((N_HEADS, rows_blk, D_HEAD), lambda g: (0, at(g), 0))
    sq = pl.BlockSpec((N_HEADS, rows_blk, DN_CHUNK), lambda g: (0, at(g), 0))
    state = pl.BlockSpec((N_HEADS, per, D_HEAD, D_HEAD), lambda g: (0, at(g), 0, 0))
    tile = pl.BlockSpec((N_HEADS, per, 8, 128), lambda g: (0, at(g), 0, 0))
    return act, sq, state, tile


def _dn_scan_fwd(u, w, qd, kd, a, el):
    t = u.shape[1]
    n_chunks = t // DN_CHUNK
    rows_blk = DN_SCAN_GROUP * DN_CHUNK

    def body(u_ref, w_ref, qd_ref, kd_ref, a_ref, el_ref, o_ref, vn_ref, s_ref, s_scr):
        @pl.when(pl.program_id(0) == 0)
        def _():
            s_scr[...] = jnp.zeros_like(s_scr)

        for cc in range(DN_SCAN_GROUP):
            rows = slice(cc * DN_CHUNK, (cc + 1) * DN_CHUNK)
            s = s_scr[...]
            s_ref[:, cc] = s
            v_new = u_ref[:, rows, :] - _bdot(w_ref[:, rows, :], s, "nn")
            vn_ref[:, rows, :] = v_new
            o_ref[:, rows, :] = _bdot(qd_ref[:, rows, :], s, "nn") + _bdot(a_ref[:, rows, :], v_new, "nn")
            s_scr[...] = s * el_ref[:, cc][:, 0:1, :] + _bdot(kd_ref[:, rows, :], v_new, "tn")

    act, sq, state, tile = _dn_scan_specs(t, rows_blk, reverse=False)
    shp = jax.ShapeDtypeStruct((N_HEADS, t, D_HEAD), F32)
    return pl.pallas_call(
        body, name="dn_scan_fwd", grid=(t // rows_blk,),
        in_specs=[act, act, act, act, sq, tile], out_specs=[act, act, state],
        out_shape=[shp, shp, jax.ShapeDtypeStruct((N_HEADS, n_chunks, D_HEAD, D_HEAD), F32)],
        scratch_shapes=[pltpu.VMEM((N_HEADS, D_HEAD, D_HEAD), F32)],
        compiler_params=_cp(dimension_semantics=("arbitrary",)),
    )(u, w, qd, kd, a, el)


def _dn_scan_bwd(w, qd, kd, a, el, vn, s_all, do):
    t = w.shape[1]
    n_chunks = t // DN_CHUNK
    rows_blk = DN_SCAN_GROUP * DN_CHUNK

    def body(w_ref, qd_ref, kd_ref, a_ref, el_ref, vn_ref, s_ref, do_ref, dvn_ref, dkd_ref, dqd_ref, dw_ref, dl_ref, ds_scr):
        @pl.when(pl.program_id(0) == 0)
        def _():
            ds_scr[...] = jnp.zeros_like(ds_scr)

        for cc in reversed(range(DN_SCAN_GROUP)):
            rows = slice(cc * DN_CHUNK, (cc + 1) * DN_CHUNK)
            s = s_ref[:, cc]
            d_s = ds_scr[...]
            e_last = el_ref[:, cc][:, 0:1, :]
            d_o = do_ref[:, rows, :]
            dv_new = _bdot(a_ref[:, rows, :], d_o, "tn") + _bdot(kd_ref[:, rows, :], d_s, "nn")
            ds_scr[...] = d_s * e_last + _bdot(qd_ref[:, rows, :], d_o, "tn") - _bdot(w_ref[:, rows, :], dv_new, "tn")
            dvn_ref[:, rows, :] = dv_new
            dkd_ref[:, rows, :] = _bdot(vn_ref[:, rows, :], d_s, "nt")
            dqd_ref[:, rows, :] = _bdot(d_o, s, "nt")
            dw_ref[:, rows, :] = -_bdot(dv_new, s, "nt")
            dlast = jnp.sum(jnp.sum(d_s * s, axis=2, keepdims=True), axis=1, keepdims=True)
            dl_ref[:, cc] = jnp.broadcast_to(dlast * e_last, (N_HEADS, 8, 128))

    act, sq, state, tile = _dn_scan_specs(t, rows_blk, reverse=True)
    shp = jax.ShapeDtypeStruct((N_HEADS, t, D_HEAD), F32)
    return pl.pallas_call(
        body, name="dn_scan_bwd", grid=(t // rows_blk,),
        in_specs=[act, act, act, sq, tile, act, state, act], out_specs=[act] * 4 + [tile],
        out_shape=[shp] * 4 + [jax.ShapeDtypeStruct((N_HEADS, n_chunks, 8, 128), F32)],
        scratch_shapes=[pltpu.VMEM((N_HEADS, D_HEAD, D_HEAD), F32)],
        compiler_params=_cp(dimension_semantics=("arbitrary",)),
    )(w, qd, kd, a, el, vn, s_all, do)


def _dn_intra_bwd(qkv, beta_t, g_t, tinv_all, vn, do, dvn, dkd, dqd, dw, dl):
    t = qkv.shape[0]
    rows_blk = min(DN_GROUP * DN_CHUNK, t)

    def body(q_ref, k_ref, v_ref, b_ref, g_ref, ti_ref, vn_ref, do_ref, dvn_ref, dkd_ref, dqd_ref, dw_ref, dl_ref,
             dq_ref, dk_ref, dv_ref, db_ref, dg_ref):
        ri = lax.broadcasted_iota(jnp.int32, (DN_CHUNK, DN_CHUNK), 0)
        ci = lax.broadcasted_iota(jnp.int32, (DN_CHUNK, DN_CHUNK), 1)
        h = pl.program_id(1)
        q, k, v = (_chunks(r[...]) for r in (q_ref, k_ref, v_ref))
        b, gc = _head_column(b_ref[...], h), _head_column(g_ref[...], h + N_HEADS)
        tinv = _chunks(ti_ref[0])
        dv_new, dk_dec, dq_dec, d_w = (_chunks(r[...]) for r in (dvn_ref, dkd_ref, dqd_ref, dw_ref))
        eye, gam, kk, qk, _, e_g, dec, rcol = _dn_chunk_common(q, k, gc, ri, ci)
        bv = v * b
        bk = k * (b * e_g)

        d_a = jnp.where(ri >= ci, _bdot(_chunks(do_ref[...]), _chunks(vn_ref[...]), "nt"), 0.0)
        dbv = _bdot(tinv, dv_new, "tn")
        dbk = _bdot(tinv, d_w, "tn")
        d_tinv = _bdot(dv_new, bv, "nt") + _bdot(d_w, bk, "nt")
        d_m = -jnp.where(ri > ci, _dot3(_dot3(tinv, d_tinv, "tn"), tinv, "nt"), 0.0)

        d_kk = d_m * b * gam
        d_gam = d_m * b * kk + d_a * qk
        d_qk = d_a * gam
        dq_ref[...] = _unchunk(_bdot(d_qk, k, "nn") + dq_dec * e_g)
        dk_ref[...] = _unchunk(_bdot(d_qk, q, "tn") + _bdot(d_kk, k, "nn") + _bdot(d_kk, k, "tn")
                               + dk_dec * dec + dbk * (b * e_g))
        dv_ref[...] = _unchunk(dbv * b)
        d_b = _unchunk(jnp.sum(d_m * kk * gam, axis=-1, keepdims=True) + jnp.sum(dbv * v, axis=-1, keepdims=True)
                       + jnp.sum(dbk * k, axis=-1, keepdims=True) * e_g)

        xg = d_gam * gam
        kdk = jnp.sum(dk_dec * (k * dec), axis=-1, keepdims=True)
        d_gc = (jnp.sum(xg, axis=-1, keepdims=True) - _row_to_col(jnp.sum(xg, axis=-2, keepdims=True), eye)
                + jnp.sum(dq_dec * (q * e_g), axis=-1, keepdims=True) - kdk
                + jnp.sum(dbk * bk, axis=-1, keepdims=True))
        d_last_total = dl_ref[0][:, 0:1, 0:1] + jnp.sum(kdk, axis=-2, keepdims=True)
        d_g = _unchunk(d_gc + jnp.where(rcol == DN_CHUNK - 1, d_last_total, 0.0))

        @pl.when(h == 0)
        def _():
            db_ref[...] = jnp.zeros_like(db_ref)
            dg_ref[...] = jnp.zeros_like(dg_ref)

        lane = lax.broadcasted_iota(jnp.int32, db_ref.shape, 1)
        db_ref[...] += jnp.where(lane == h, d_b, 0.0)
        dg_ref[...] += jnp.where(lane == h + N_HEADS, d_g, 0.0)

    head, lanes, hm, sq, tile = _dn_specs(t, rows_blk)
    return pl.pallas_call(
        body, name="dn_intra_bwd", grid=(t // rows_blk, N_HEADS),
        in_specs=[head(0), head(N_HEADS), head(2 * N_HEADS), lanes, lanes, sq] + [hm] * 6 + [tile],
        out_specs=[head(0), head(0), head(0), lanes, lanes],
        out_shape=[jax.ShapeDtypeStruct((t, D_MODEL), F32)] * 3 + [jax.ShapeDtypeStruct((t, 128), F32)] * 2,
        compiler_params=_cp(),
    )(qkv, qkv, qkv, beta_t, g_t, tinv_all, vn, do, dvn, dkd, dqd, dw, dl)


def _dn_post_fwd(o, proj, gn):
    t = o.shape[1]

    def body(o_ref, z_ref, g_ref, out_ref):
        ov, z = o_ref[...], z_ref[...]
        r = lax.rsqrt(jnp.mean(ov * ov, axis=-1, keepdims=True) + NORM_EPS)
        out_ref[...] = (((ov * r) * g_ref[...]) * (z * _sigmoid(z))).astype(BF16)

    blk = pl.BlockSpec((t, D_HEAD), lambda h: (0, h))
    return pl.pallas_call(
        body, name="dn_post_fwd", grid=(N_HEADS,),
        in_specs=[pl.BlockSpec((None, t, D_HEAD), lambda h: (h, 0, 0)),
                  pl.BlockSpec((t, D_HEAD), lambda h: (0, C_DNZ // D_HEAD + h)),
                  pl.BlockSpec((1, D_HEAD), lambda h: (0, 0))],
        out_specs=blk, out_shape=jax.ShapeDtypeStruct((t, D_MODEL), BF16), compiler_params=_cp(),
    )(o, proj, gn)


def _dn_post_bwd(o, proj, gn, dout):
    t = o.shape[1]

    def body(o_ref, z_ref, g_ref, d_ref, do_ref, dz_ref, dg_ref):
        @pl.when(pl.program_id(0) == 0)
        def _():
            dg_ref[...] = jnp.zeros_like(dg_ref)

        ov, z, d = o_ref[...], z_ref[...], d_ref[...]
        r = lax.rsqrt(jnp.mean(ov * ov, axis=-1, keepdims=True) + NORM_EPS)
        ohat = ov * r
        s = _sigmoid(z)
        d_on = d * (z * s)
        dz_ref[...] = (d * (ohat * g_ref[...]) * (s * (1.0 + z * (1.0 - s)))).astype(BF16)
        dg_ref[...] += jnp.sum(d_on * ohat, axis=0, keepdims=True)
        dxh = d_on * g_ref[...]
        do_ref[...] = r * (dxh - ohat * jnp.mean(dxh * ohat, axis=-1, keepdims=True))

    blk = pl.BlockSpec((t, D_HEAD), lambda h: (0, h))
    hm = pl.BlockSpec((None, t, D_HEAD), lambda h: (h, 0, 0))
    vec = pl.BlockSpec((1, D_HEAD), lambda h: (0, 0))
    return pl.pallas_call(
        body, name="dn_post_bwd", grid=(N_HEADS,),
        in_specs=[hm, pl.BlockSpec((t, D_HEAD), lambda h: (0, C_DNZ // D_HEAD + h)), vec, blk],
        out_specs=[hm, blk, vec],
        out_shape=[jax.ShapeDtypeStruct((N_HEADS, t, D_HEAD), F32), jax.ShapeDtypeStruct((t, D_MODEL), BF16),
                   jax.ShapeDtypeStruct((1, D_HEAD), F32)], compiler_params=_cp(),
    )(o, proj, gn, dout)


def _sb_fwd(proj):
    t = proj.shape[0]
    qblk = min(SB_QBLOCK, t)
    scale = 1.0 / math.sqrt(D_HEAD)

    hp = SB_HEADS_PER_STEP
    wid = hp * D_HEAD

    def body(q_ref, k_ref, v_ref, z_ref, o_ref, og_ref, l_ref, qb, kb, vb):
        for hh in range(hp):
            hs = slice(hh * D_HEAD, (hh + 1) * D_HEAD)
            qb[hh] = q_ref[:, hs].astype(BF16)
            kb[hh] = k_ref[:, hs].astype(BF16)
            vb[hh] = v_ref[:, hs].astype(BF16)
        ri = lax.broadcasted_iota(jnp.int32, (qblk, SB_BLOCK), 0)
        ci = lax.broadcasted_iota(jnp.int32, (qblk, SB_BLOCK), 1)
        r2 = lax.broadcasted_iota(jnp.int32, (SB_BLOCK, SB_BLOCK), 0)
        c2 = lax.broadcasted_iota(jnp.int32, (SB_BLOCK, SB_BLOCK), 1)
        upper = (r2 > c2).astype(BF16)
        nkb = qblk // SB_BLOCK

        def qblock(i, carry):
            rows = pl.ds(pl.multiple_of(i * qblk, qblk), qblk)
            qi = qb[:, rows, :]

            def kblock(jj, st):
                acc, c = st
                j = (i + 1) * nkb - 1 - jj
                cols = pl.ds(pl.multiple_of(j * SB_BLOCK, SB_BLOCK), SB_BLOCK)
                mask = (j * SB_BLOCK + ci) < (i * qblk + ri)
                z = _dot(qi, kb[:, cols, :], "nt") * scale
                lb = jnp.minimum(z, 0.0) - jnp.log(1.0 + jnp.exp(-jnp.abs(z)))
                lf = jnp.where(mask, lb - z, 0.0)
                surv = _split_dot(lf, upper) + c
                att = jnp.where(mask, jnp.exp(lb + surv), 0.0)
                acc = acc + _dot(att.astype(BF16), vb[:, cols, :], "nn")
                return acc, c + jnp.sum(lf, axis=-1, keepdims=True)

            init = (jnp.zeros((hp, qblk, D_HEAD), F32), jnp.zeros((hp, qblk, 1), F32))
            acc, c = lax.fori_loop(0, (i + 1) * nkb, kblock, init)
            l_ref[:, rows, :] = c
            for hh in range(hp):
                hs = slice(hh * D_HEAD, (hh + 1) * D_HEAD)
                zg = z_ref[rows, hs]
                o_ref[rows, hs] = acc[hh]
                og_ref[rows, hs] = (acc[hh] * (zg * _sigmoid(zg))).astype(BF16)
            return carry

        lax.fori_loop(0, t // qblk, qblock, 0)

    def head(off):
        return pl.BlockSpec((t, wid), lambda h: (0, off // wid + h))

    out = pl.BlockSpec((t, wid), lambda h: (0, h))
    return pl.pallas_call(
        body, name="sb_fwd", grid=(N_HEADS // hp,),
        in_specs=[head(C_SBQ), head(C_SBQ + D_MODEL), head(C_SBQ + 2 * D_MODEL), head(C_SBZ)],
        out_specs=[out, out, pl.BlockSpec((hp, t, 1), lambda h: (h, 0, 0))],
        out_shape=[jax.ShapeDtypeStruct((t, D_MODEL), F32), jax.ShapeDtypeStruct((t, D_MODEL), BF16),
                   jax.ShapeDtypeStruct((N_HEADS, t, 1), F32)],
        scratch_shapes=[pltpu.VMEM((hp, t, D_HEAD), BF16)] * 3, compiler_params=_cp(),
    )(proj, proj, proj, proj)


def _sb_bwd(proj, o, ltot, dog):
    t = proj.shape[0]
    qblk = min(SB_QBLOCK, t)
    scale = 1.0 / math.sqrt(D_HEAD)

    hp = SB_HEADS_PER_STEP
    wid = hp * D_HEAD

    def body(q_ref, k_ref, v_ref, z_ref, o_ref, l_ref, d_ref, dq_ref, dk_ref, dv_ref, dz_ref,
             qb, kb, vb, dob, dk_scr, dv_scr):
        for hh in range(hp):
            hs = slice(hh * D_HEAD, (hh + 1) * D_HEAD)
            qb[hh] = q_ref[:, hs].astype(BF16)
            kb[hh] = k_ref[:, hs].astype(BF16)
            vb[hh] = v_ref[:, hs].astype(BF16)
            zg = z_ref[:, hs]
            sg = _sigmoid(zg)
            dgo = d_ref[:, hs]
            dob[hh] = (dgo * (zg * sg)).astype(BF16)
            dz_ref[:, hs] = (dgo * o_ref[:, hs] * (sg * (1.0 + zg * (1.0 - sg)))).astype(BF16)
        dk_scr[...] = jnp.zeros_like(dk_scr)
        dv_scr[...] = jnp.zeros_like(dv_scr)
        ri = lax.broadcasted_iota(jnp.int32, (qblk, SB_BLOCK), 0)
        ci = lax.broadcasted_iota(jnp.int32, (qblk, SB_BLOCK), 1)
        r2 = lax.broadcasted_iota(jnp.int32, (SB_BLOCK, SB_BLOCK), 0)
        c2 = lax.broadcasted_iota(jnp.int32, (SB_BLOCK, SB_BLOCK), 1)
        incl = (r2 <= c2).astype(BF16)
        below = (r2 < c2).astype(BF16)

        def qblock(i, carry):
            rows = pl.ds(pl.multiple_of(i * qblk, qblk), qblk)
            qi = qb[:, rows, :]
            d_o = dob[:, rows, :]
            ltot = l_ref[:, rows, :]

            def kblock(j, st):
                dq, cpre, ce = st
                cols = pl.ds(pl.multiple_of(j * SB_BLOCK, SB_BLOCK), SB_BLOCK)
                mask = (j * SB_BLOCK + ci) < (i * qblk + ri)
                kj, vj = kb[:, cols, :], vb[:, cols, :]
                z = _dot(qi, kj, "nt") * scale
                e1 = jnp.exp(-jnp.abs(z))
                den = 1.0 + e1
                lb = jnp.minimum(z, 0.0) - jnp.log(den)
                r = 1.0 / den
                sig = jnp.where(z >= 0.0, r, e1 * r)
                nsig = jnp.where(z >= 0.0, e1 * r, r)
                lf = jnp.where(mask, lb - z, 0.0)
                surv = ltot - (cpre + _split_dot(lf, incl))
                att = jnp.where(mask, jnp.exp(lb + surv), 0.0)
                e = _dot(d_o, vj, "nt") * att
                dlf = ce + _split_dot(e, below)
                dzz = jnp.where(mask, e * nsig - dlf * sig, 0.0).astype(BF16)
                dq = dq + _dot(dzz, kj, "nn")
                dk_scr[:, cols, :] += _dot(dzz, qi, "tn")
                dv_scr[:, cols, :] += _dot(att.astype(BF16), d_o, "tn")
                return dq, cpre + jnp.sum(lf, axis=-1, keepdims=True), ce + jnp.sum(e, axis=-1, keepdims=True)

            zero_col = jnp.zeros((hp, qblk, 1), F32)
            init = (jnp.zeros((hp, qblk, D_HEAD), F32), zero_col, zero_col)
            dq, _, _ = lax.fori_loop(0, (i + 1) * (qblk // SB_BLOCK), kblock, init)
            for hh in range(hp):
                dq_ref[rows, hh * D_HEAD:(hh + 1) * D_HEAD] = (dq[hh] * scale).astype(BF16)
            return carry

        lax.fori_loop(0, t // qblk, qblock, 0)
        for hh in range(hp):
            hs = slice(hh * D_HEAD, (hh + 1) * D_HEAD)
            dk_ref[:, hs] = (dk_scr[hh] * scale).astype(BF16)
            dv_ref[:, hs] = dv_scr[hh].astype(BF16)

    def head(off):
        return pl.BlockSpec((t, wid), lambda h: (0, off // wid + h))

    return pl.pallas_call(
        body, name="sb_bwd", grid=(N_HEADS // hp,),
        in_specs=[head(C_SBQ), head(C_SBQ + D_MODEL), head(C_SBQ + 2 * D_MODEL), head(C_SBZ), head(0),
                  pl.BlockSpec((hp, t, 1), lambda h: (h, 0, 0)), head(0)],
        out_specs=[head(0)] * 4, out_shape=[jax.ShapeDtypeStruct((t, D_MODEL), BF16)] * 4,
        scratch_shapes=[pltpu.VMEM((hp, t, D_HEAD), BF16)] * 4 + [pltpu.VMEM((hp, t, D_HEAD), F32)] * 2,
        compiler_params=_cp(),
    )(proj, proj, proj, proj, o, ltot, dog)


def _mem_fwd(proj, mkv):
    t = proj.shape[0]
    tq = _pick(t, (512, 256))
    m_len = mkv.shape[0]
    scale = 1.0 / math.sqrt(MEM_DH)

    def body(q_ref, z_ref, kv_ref, o_ref, og_ref):
        q = q_ref[...]
        mk = kv_ref[:, :MEM_W].astype(BF16)
        mv = kv_ref[:, MEM_W:].astype(BF16)
        lane = lax.broadcasted_iota(jnp.int32, q.shape, 1) >> 6
        o = jnp.zeros(q.shape, F32)
        for h in range(MEM_HEADS):
            s = _bdot(jnp.where(lane == h, q, 0.0), mk, "nt") * scale
            p = jnp.exp(s - jnp.max(s, axis=-1, keepdims=True))
            p = p / jnp.sum(p, axis=-1, keepdims=True)
            o = o + jnp.where(lane == h, _bdot(p, mv, "nn"), 0.0)
        z = z_ref[...]
        o_ref[...] = o
        og_ref[...] = (o * (z * _sigmoid(z))).astype(BF16)

    out = pl.BlockSpec((tq, MEM_W), lambda i: (i, 0))
    return pl.pallas_call(
        body, name="mem_fwd", grid=(t // tq,),
        in_specs=[pl.BlockSpec((tq, MEM_W), lambda i: (i, C_MQ // MEM_W)),
                  pl.BlockSpec((tq, MEM_W), lambda i: (i, C_MZ // MEM_W)),
                  pl.BlockSpec((m_len, 2 * MEM_W), lambda i: (0, 0))],
        out_specs=[out, out],
        out_shape=[jax.ShapeDtypeStruct((t, MEM_W), F32), jax.ShapeDtypeStruct((t, MEM_W), BF16)],
        compiler_params=_cp(),
    )(proj, proj, mkv)


def _mem_bwd(proj, mkv, o, dog):
    t = proj.shape[0]
    tq = _pick(t, (512, 256))
    m_len = mkv.shape[0]
    scale = 1.0 / math.sqrt(MEM_DH)

    def body(q_ref, z_ref, kv_ref, o_ref, d_ref, dq_ref, dz_ref, dkv_ref):
        @pl.when(pl.program_id(0) == 0)
        def _():
            dkv_ref[...] = jnp.zeros_like(dkv_ref)

        q = q_ref[...]
        z = z_ref[...]
        sg = _sigmoid(z)
        dgo = d_ref[...]
        d_o = dgo * (z * sg)
        dz_ref[...] = (dgo * o_ref[...] * (sg * (1.0 + z * (1.0 - sg)))).astype(BF16)
        mk = kv_ref[:, :MEM_W].astype(BF16)
        mv = kv_ref[:, MEM_W:].astype(BF16)
        lane = lax.broadcasted_iota(jnp.int32, q.shape, 1) >> 6
        klane = lax.broadcasted_iota(jnp.int32, (m_len, MEM_W), 1) >> 6
        dq = jnp.zeros(q.shape, F32)
        dmk = jnp.zeros((m_len, MEM_W), F32)
        dmv = jnp.zeros((m_len, MEM_W), F32)
        for h in range(MEM_HEADS):
            qh = jnp.where(lane == h, q, 0.0)
            doh = jnp.where(lane == h, d_o, 0.0)
            s = _bdot(qh, mk, "nt") * scale
            p = jnp.exp(s - jnp.max(s, axis=-1, keepdims=True))
            p = p / jnp.sum(p, axis=-1, keepdims=True)
            dp = _bdot(doh, mv, "nt")
            ds = p * (dp - jnp.sum(dp * p, axis=-1, keepdims=True)) * scale
            dq = dq + jnp.where(lane == h, _bdot(ds, mk, "nn"), 0.0)
            dmk = dmk + jnp.where(klane == h, _bdot(ds, qh, "tn"), 0.0)
            dmv = dmv + jnp.where(klane == h, _bdot(p, doh, "tn"), 0.0)
        dq_ref[...] = dq.astype(BF16)
        dkv_ref[:, :MEM_W] += dmk
        dkv_ref[:, MEM_W:] += dmv

    blk = pl.BlockSpec((tq, MEM_W), lambda i: (i, 0))
    kv = pl.BlockSpec((m_len, 2 * MEM_W), lambda i: (0, 0))
    return pl.pallas_call(
        body, name="mem_bwd", grid=(t // tq,),
        in_specs=[pl.BlockSpec((tq, MEM_W), lambda i: (i, C_MQ // MEM_W)),
                  pl.BlockSpec((tq, MEM_W), lambda i: (i, C_MZ // MEM_W)), kv, blk, blk],
        out_specs=[blk, blk, kv],
        out_shape=[jax.ShapeDtypeStruct((t, MEM_W), BF16), jax.ShapeDtypeStruct((t, MEM_W), BF16),
                   jax.ShapeDtypeStruct((m_len, 2 * MEM_W), F32)], compiler_params=_cp(),
    )(proj, proj, mkv, o, dog)


_GW = 512


def _merge_fwd(proj, y_dn, y_sb, y_m):
    t = proj.shape[0]
    tb = _pick(t, (256,))
    nc = D_MODEL // _GW

    def body(g1, g2, g3, y1, y2, y3, out_ref):
        out_ref[...] = (_sigmoid(g1[...]) * y1[...] + _sigmoid(g2[...]) * y2[...] + _sigmoid(g3[...]) * y3[...]).astype(BF16)

    def gate(kb):
        return pl.BlockSpec((tb, _GW), lambda i, c: (i, C_GATES // _GW + kb * nc + c))

    blk = pl.BlockSpec((tb, _GW), lambda i, c: (i, c))
    return pl.pallas_call(
        body, name="merge_fwd", grid=(t // tb, nc), in_specs=[gate(0), gate(1), gate(2), blk, blk, blk],
        out_specs=blk, out_shape=jax.ShapeDtypeStruct((t, D_MODEL), BF16), compiler_params=_cp(),
    )(proj, proj, proj, y_dn, y_sb, y_m)


def _merge_bwd(proj, y_dn, y_sb, y_m, dm):
    t = proj.shape[0]
    tb = _pick(t, (256,))
    nc = D_MODEL // _GW

    def body(g1, g2, g3, y1, y2, y3, dm_ref, d1, d2, d3, dg1, dg2, dg3):
        d = dm_ref[...]
        for g, y, dy, dg in ((g1, y1, d1, dg1), (g2, y2, d2, dg2), (g3, y3, d3, dg3)):
            s = _sigmoid(g[...])
            dy[...] = (d * s).astype(BF16)
            dg[...] = (d * y[...] * (s * (1.0 - s))).astype(BF16)

    def gate(kb):
        return pl.BlockSpec((tb, _GW), lambda i, c: (i, C_GATES // _GW + kb * nc + c))

    blk = pl.BlockSpec((tb, _GW), lambda i, c: (i, c))
    act = jax.ShapeDtypeStruct((t, D_MODEL), BF16)
    return pl.pallas_call(
        body, name="merge_bwd", grid=(t // tb, nc), in_specs=[gate(0), gate(1), gate(2), blk, blk, blk, blk],
        out_specs=[blk] * 6, out_shape=[act] * 6, compiler_params=_cp(),
    )(proj, proj, proj, y_dn, y_sb, y_m, dm)


def _final_loss(x, mo, g, tgt):
    t, d = x.shape
    tb = _pick(t, (256,))

    def body(x_ref, mo_ref, g_ref, t_ref, do_ref, dob_ref, loss_ref, dg_ref):
        @pl.when(pl.program_id(0) == 0)
        def _():
            loss_ref[...] = jnp.zeros_like(loss_ref)
            dg_ref[...] = jnp.zeros_like(dg_ref)

        out = x_ref[...] + mo_ref[...]
        r = lax.rsqrt(jnp.mean(out * out, axis=-1, keepdims=True) + NORM_EPS)
        xhat = out * r
        gv = g_ref[...]
        err = xhat * gv - t_ref[...]
        per_tok = jnp.mean(err * err, axis=-1, keepdims=True)
        loss_ref[...] += 0.5 * jnp.sum(per_tok, axis=0, keepdims=True)
        dy = err * (1.0 / d)
        dg_ref[...] += jnp.sum(dy * xhat, axis=0, keepdims=True)
        dxh = dy * gv
        dout = r * (dxh - xhat * jnp.mean(dxh * xhat, axis=-1, keepdims=True))
        do_ref[...] = dout
        dob_ref[...] = dout.astype(BF16)

    row = pl.BlockSpec((tb, d), lambda i: (i, 0))
    vec = pl.BlockSpec((1, d), lambda i: (0, 0))
    return pl.pallas_call(
        body, name="final_loss", grid=(t // tb,), in_specs=[row, row, vec, row],
        out_specs=[row, row, pl.BlockSpec((1, 128), lambda i: (0, 0)), vec],
        out_shape=[jax.ShapeDtypeStruct((t, d), F32), jax.ShapeDtypeStruct((t, d), BF16),
                   jax.ShapeDtypeStruct((1, 128), F32), jax.ShapeDtypeStruct((1, d), F32)],
        compiler_params=_cp(),
    )(x, mo, g, tgt)


def _cast_bf16(a, name):
    r, c = a.shape
    tb = _pick(r, (128, 496))

    def body(a_ref, o_ref):
        o_ref[...] = a_ref[...].astype(BF16)

    blk = pl.BlockSpec((tb, c), lambda i: (i, 0))
    return pl.pallas_call(body, name=name, grid=(r // tb,), in_specs=[blk], out_specs=blk,
                          out_shape=jax.ShapeDtypeStruct((r, c), BF16), compiler_params=_cp())(a)


WIN_START = (0, 23, 45, 68)
_S1_LO, _S1_HI = 1148, 1164
_S1_BA_POS = SHARD_PAD - 128


def _to_window(x, s):
    if s == 0:
        return x
    if s in (2, 3):
        return pltpu.roll(x, 120 if s == 2 else 124, 1)
    pos = lax.broadcasted_iota(jnp.int32, x.shape, 1)
    head = pltpu.roll(x, 4, 1)
    tail = pltpu.roll(x, SHARD_PAD - 12, 1)
    ba = jnp.where(pos < _S1_BA_POS + (_S1_HI - _S1_LO), pltpu.roll(x, _S1_BA_POS - _S1_LO, 1), 0.0)
    return jnp.where(pos < _S1_LO + 4, head, jnp.where(pos < _S1_BA_POS, tail, ba))


def _from_window(g, s):
    if s == 0:
        return g
    if s in (2, 3):
        return pltpu.roll(g, SHARD_PAD - (120 if s == 2 else 124), 1)
    col = lax.broadcasted_iota(jnp.int32, g.shape, 1)
    head = pltpu.roll(g, SHARD_PAD - 4, 1)
    tail = pltpu.roll(g, 12, 1)
    ba = pltpu.roll(g, SHARD_PAD - (_S1_BA_POS - _S1_LO), 1)
    return jnp.where(col < _S1_LO, head, jnp.where(col < _S1_HI, ba, tail))


def _cast_to_window(w, shard, name):
    r, c = w.shape
    tb = _pick(r, (128,))

    def body(s_ref, w_ref, o_ref, pad_scr):
        pad_scr[...] = jnp.zeros_like(pad_scr)
        pad_scr[:, :c] = w_ref[...]
        x = pad_scr[...]
        for s in range(N_SHARD):
            @pl.when(s_ref[0] == s)
            def _():
                o_ref[...] = _to_window(x, s).astype(BF16)

    return pl.pallas_call(
        body, name=name,
        grid_spec=pltpu.PrefetchScalarGridSpec(
            num_scalar_prefetch=1, grid=(r // tb,),
            in_specs=[pl.BlockSpec((tb, c), lambda i, s: (i, 0))],
            out_specs=pl.BlockSpec((tb, SHARD_PAD), lambda i, s: (i, 0)),
            scratch_shapes=[pltpu.VMEM((tb, SHARD_PAD), F32)]),
        out_shape=jax.ShapeDtypeStruct((r, SHARD_PAD), BF16), compiler_params=_cp(),
    )(shard, w)


def _pair_add(g, recv, c_idx, name):
    n, r, c = g.shape
    half = r // 2
    tb = _pick(half, (128, 248))
    nb = half // tb

    def body(c_ref, g_ref, r_ref, o_ref):
        o_ref[...] = (g_ref[...].astype(F32) + r_ref[...].astype(F32)).astype(BF16)

    blk = pl.BlockSpec((n, tb, c), lambda i, c_ref: (0, i, 0))
    return pl.pallas_call(
        body, name=name,
        grid_spec=pltpu.PrefetchScalarGridSpec(
            num_scalar_prefetch=1, grid=(nb,),
            in_specs=[pl.BlockSpec((n, tb, c), lambda i, c_ref: (0, c_ref[0] * nb + i, 0)), blk], out_specs=blk),
        out_shape=jax.ShapeDtypeStruct((n, half, c), BF16), compiler_params=_cp(),
    )(c_idx, g, recv)


def _chip_sum(parts, by_chip, place, name):
    n, h, c = parts.shape
    tb = _pick(h, (128, 248))
    nb = h // tb

    def body(p_ref, mine_ref, *rest):
        others, o_ref = rest[:n], rest[n]
        me = jnp.zeros((tb, c), jnp.int32) + p_ref[0]
        acc = None
        for q in range(n):
            term = jnp.where(me == q, mine_ref[...], others[q][...]).astype(F32)
            acc = term if acc is None else acc + term
        o_ref[...] = acc

    def other(q):
        return pl.BlockSpec((None, tb, c), lambda i, p: (jnp.where(p[0] == q, (q + 1) % n, q), i, 0))

    return pl.pallas_call(
        body, name=name,
        grid_spec=pltpu.PrefetchScalarGridSpec(
            num_scalar_prefetch=1, grid=(nb,),
            in_specs=[pl.BlockSpec((None, tb, c), lambda i, p: (p[0], i, 0))] + [other(q) for q in range(n)],
            out_specs=pl.BlockSpec((tb, c), lambda i, p: (p[1] * nb + i, 0))),
        out_shape=jax.ShapeDtypeStruct((2 * h, c), F32), compiler_params=_cp(),
    )(place, parts, *([by_chip] * n))


def _adamw_math(w, g, m, v):
    m = ADAM_B1 * m + (1.0 - ADAM_B1) * g
    v = ADAM_B2 * v + (1.0 - ADAM_B2) * (g * g)
    m_hat = m / (1.0 - ADAM_B1 ** ADAM_STEP)
    v_hat = v / (1.0 - ADAM_B2 ** ADAM_STEP)
    delta = -ADAM_LR * (m_hat / (jnp.sqrt(v_hat) + ADAM_EPS) + ADAM_WD * w)
    return delta, m, v


def _adamw(w, g, m, v, name):
    r, c = w.shape
    tb = _pick(r, (128, 496))

    def body(w_ref, g_ref, m_ref, v_ref, go_ref, d_ref, mo_ref, vo_ref):
        gv = g_ref[...]
        d, mn, vn = _adamw_math(w_ref[...], gv, m_ref[...], v_ref[...])
        go_ref[...] = gv
        d_ref[...] = d
        mo_ref[...] = mn
        vo_ref[...] = vn

    blk = pl.BlockSpec((tb, c), lambda i: (i, 0))
    return pl.pallas_call(
        body, name=name, grid=(r // tb,), in_specs=[blk] * 4, out_specs=[blk] * 4,
        out_shape=[jax.ShapeDtypeStruct((r, c), F32)] * 4, compiler_params=_cp(),
    )(w, g, m, v)


def _adamw_window(w, g_win, m, v, shard, name):
    r, c = w.shape
    tb = _pick(r, (128,))

    def body(s_ref, w_ref, g_ref, m_ref, v_ref, go_ref, d_ref, mo_ref, vo_ref, g_scr):
        gw = g_ref[...]
        for s in range(N_SHARD):
            @pl.when(s_ref[0] == s)
            def _():
                g_scr[...] = _from_window(gw, s)

        gv = g_scr[:, :c]
        d, mn, vn = _adamw_math(w_ref[...], gv, m_ref[...], v_ref[...])
        go_ref[...] = gv
        d_ref[...] = d
        mo_ref[...] = mn
        vo_ref[...] = vn

    blk = pl.BlockSpec((tb, c), lambda i, s: (i, 0))
    return pl.pallas_call(
        body, name=name,
        grid_spec=pltpu.PrefetchScalarGridSpec(
            num_scalar_prefetch=1, grid=(r // tb,),
            in_specs=[blk, pl.BlockSpec((tb, SHARD_PAD), lambda i, s: (i, 0)), blk, blk], out_specs=[blk] * 4,
            scratch_shapes=[pltpu.VMEM((tb, SHARD_PAD), F32)]),
        out_shape=[jax.ShapeDtypeStruct((r, c), F32)] * 4, compiler_params=_cp(),
    )(shard, w, g_win, m, v)


def _small_update(gathered, w, m, v):
    def body(p_ref, w_ref, m_ref, v_ref, g_ref, d_ref, mo_ref, vo_ref):
        g = p_ref[0]
        for i in range(1, N_DEV):
            g = g + p_ref[i]
        d, mn, vn = _adamw_math(w_ref[...], g, m_ref[...], v_ref[...])
        g_ref[...] = g
        d_ref[...] = d
        mo_ref[...] = mn
        vo_ref[...] = vn

    full = pl.BlockSpec((32, 128), lambda i: (0, 0))
    return pl.pallas_call(
        body, name="small_update", grid=(1,),
        in_specs=[pl.BlockSpec((N_DEV, 32, 128), lambda i: (0, 0, 0)), full, full, full], out_specs=[full] * 4,
        out_shape=[jax.ShapeDtypeStruct((32, 128), F32)] * 4, compiler_params=_cp(),
    )(gathered, w, m, v)


_ANY = pl.BlockSpec(memory_space=pl.ANY)


def _place():
    x, y, c = lax.axis_index("x"), lax.axis_index("y"), lax.axis_index("c")
    chips = [(1 - x, y), (x, 1 - y), (1 - x, 1 - y)]
    return x, y, c, chips


def _gather_shards(arrs):
    n = len(arrs)

    def body(*refs):
        ins, outs = refs[:n], refs[n:2 * n]
        send_sems, recv_sems, local_sems = refs[2 * n:2 * n + 3]
        bufs = refs[2 * n + 3:]
        x, y, c, chips = _place()
        me = 2 * x + y
        sibling = (x, y, 1 - c)
        sends = []
        for a in range(n):
            half = ins[a].shape[0] // 2
            mine = pl.ds(pl.multiple_of(c * half, 16), half)
            for j, (qx, qy) in enumerate(chips):
                cp = pltpu.make_async_remote_copy(
                    src_ref=ins[a].at[mine], dst_ref=outs[a].at[me, mine],
                    send_sem=send_sems.at[6 * a + j], recv_sem=recv_sems.at[6 * a + j],
                    device_id=(qx, qy, c), device_id_type=MESH)
                cp.start()
                sends.append(cp)
        for a in range(n):
            step = bufs[a].shape[0]
            for r0 in range(0, ins[a].shape[0], step):
                rows = pl.ds(r0, step)
                load = pltpu.make_async_copy(ins[a].at[rows], bufs[a], local_sems.at[2 * a])
                load.start()
                load.wait()
                store = pltpu.make_async_copy(bufs[a], outs[a].at[me, rows], local_sems.at[2 * a + 1])
                store.start()
                store.wait()
        for a in range(n):
            half = ins[a].shape[0] // 2
            mine = pl.ds(pl.multiple_of(c * half, 16), half)
            for j, (qx, qy) in enumerate(chips):
                q = 2 * qx + qy
                landed = outs[a].at[q, mine]
                pltpu.make_async_remote_copy(
                    src_ref=landed, dst_ref=landed, send_sem=send_sems.at[6 * a + j], recv_sem=recv_sems.at[6 * a + j],
                    device_id=(qx, qy, c), device_id_type=MESH).wait_recv()
                fw = pltpu.make_async_remote_copy(
                    src_ref=landed, dst_ref=landed, send_sem=send_sems.at[6 * a + 3 + j],
                    recv_sem=recv_sems.at[6 * a + 3 + j], device_id=sibling, device_id_type=MESH)
                fw.start()
                sends.append(fw)
        for a in range(n):
            half = ins[a].shape[0] // 2
            theirs = pl.ds(pl.multiple_of((1 - c) * half, 16), half)
            for j, (qx, qy) in enumerate(chips):
                q = 2 * qx + qy
                dst = outs[a].at[q, theirs]
                pltpu.make_async_remote_copy(
                    src_ref=dst, dst_ref=dst, send_sem=send_sems.at[6 * a + 3 + j], recv_sem=recv_sems.at[6 * a + 3 + j],
                    device_id=sibling, device_id_type=MESH).wait_recv()
        for cp in sends:
            cp.wait_send()

    return pl.pallas_call(
        body, name="gather_shards", in_specs=[_ANY] * n, out_specs=[_ANY] * n,
        out_shape=[jax.ShapeDtypeStruct((N_SHARD,) + a.shape, a.dtype) for a in arrs],
        scratch_shapes=[pltpu.SemaphoreType.DMA((6 * n,)), pltpu.SemaphoreType.DMA((6 * n,)),
                        pltpu.SemaphoreType.DMA((2 * n,))]
        + [pltpu.VMEM((_pick(a.shape[0], (256, 496)), a.shape[1]), a.dtype) for a in arrs],
        compiler_params=pltpu.CompilerParams(has_side_effects=True, vmem_limit_bytes=VMEM_LIMIT),
    )(*arrs)


def _pair_reduce_send(grads):
    n = len(grads)

    def body(*refs):
        ins, outs = refs[:n], refs[n:2 * n]
        send_sems, recv_sems = refs[2 * n:]
        x, y, c, _ = _place()
        sibling = (x, y, 1 - c)
        cps = []
        for a in range(n):
            half = ins[a].shape[1] // 2
            theirs = pl.ds(pl.multiple_of((1 - c) * half, 8), half)
            cp = pltpu.make_async_remote_copy(
                src_ref=ins[a].at[:, theirs], dst_ref=outs[a], send_sem=send_sems.at[a], recv_sem=recv_sems.at[a],
                device_id=sibling, device_id_type=MESH)
            cp.start()
            cps.append(cp)
        for cp in cps:
            cp.wait()

    return pl.pallas_call(
        body, name="pair_reduce_send", in_specs=[_ANY] * n, out_specs=[_ANY] * n,
        out_shape=[jax.ShapeDtypeStruct((g.shape[0], g.shape[1] // 2, g.shape[2]), g.dtype) for g in grads],
        scratch_shapes=[pltpu.SemaphoreType.DMA((n,)), pltpu.SemaphoreType.DMA((n,))],
        compiler_params=pltpu.CompilerParams(has_side_effects=True),
    )(*grads)


def _chip_exchange(parts):
    n = len(parts)

    def body(*refs):
        ins, outs = refs[:n], refs[n:2 * n]
        send_sems, recv_sems = refs[2 * n:]
        x, y, c, chips = _place()
        me = 2 * x + y
        cps = []
        for a in range(n):
            for j, (qx, qy) in enumerate(chips):
                q = 2 * qx + qy
                cp = pltpu.make_async_remote_copy(
                    src_ref=ins[a].at[q], dst_ref=outs[a].at[me], send_sem=send_sems.at[3 * a + j],
                    recv_sem=recv_sems.at[3 * a + j], device_id=(qx, qy, c), device_id_type=MESH)
                cp.start()
                cps.append(cp)
        for a in range(n):
            for j, (qx, qy) in enumerate(chips):
                q = 2 * qx + qy
                dst = outs[a].at[q]
                pltpu.make_async_remote_copy(
                    src_ref=dst, dst_ref=dst, send_sem=send_sems.at[3 * a + j], recv_sem=recv_sems.at[3 * a + j],
                    device_id=(qx, qy, c), device_id_type=MESH).wait_recv()
        for cp in cps:
            cp.wait_send()

    return pl.pallas_call(
        body, name="chip_exchange", in_specs=[_ANY] * n, out_specs=[_ANY] * n,
        out_shape=[jax.ShapeDtypeStruct(p.shape, p.dtype) for p in parts],
        scratch_shapes=[pltpu.SemaphoreType.DMA((3 * n,)), pltpu.SemaphoreType.DMA((3 * n,))],
        compiler_params=pltpu.CompilerParams(has_side_effects=True),
    )(*parts)


def _pair_allgather(fulls):
    n = len(fulls)

    def body(*refs):
        outs = refs[n:2 * n]
        send_sems, recv_sems = refs[2 * n:]
        x, y, c, _ = _place()
        sibling = (x, y, 1 - c)
        cps = []
        for a in range(n):
            half = outs[a].shape[0] // 2
            mine = outs[a].at[pl.ds(pl.multiple_of(c * half, 8), half)]
            cp = pltpu.make_async_remote_copy(
                src_ref=mine, dst_ref=mine, send_sem=send_sems.at[a], recv_sem=recv_sems.at[a],
                device_id=sibling, device_id_type=MESH)
            cp.start()
            cps.append(cp)
        for a in range(n):
            half = outs[a].shape[0] // 2
            theirs = outs[a].at[pl.ds(pl.multiple_of((1 - c) * half, 8), half)]
            pltpu.make_async_remote_copy(
                src_ref=theirs, dst_ref=theirs, send_sem=send_sems.at[a], recv_sem=recv_sems.at[a],
                device_id=sibling, device_id_type=MESH).wait_recv()
        for cp in cps:
            cp.wait_send()

    return pl.pallas_call(
        body, name="pair_allgather", in_specs=[_ANY] * n, out_specs=[_ANY] * n,
        out_shape=[jax.ShapeDtypeStruct(f.shape, f.dtype) for f in fulls],
        input_output_aliases={a: a for a in range(n)},
        scratch_shapes=[pltpu.SemaphoreType.DMA((n,)), pltpu.SemaphoreType.DMA((n,))],
        compiler_params=pltpu.CompilerParams(has_side_effects=True),
    )(*fulls)


def _allgather_small(slab):
    def body(s_ref, out_ref, send_sems, recv_sems):
        x, y, c, _ = _place()
        me = 4 * x + 2 * y + c
        out_ref[me] = s_ref[...]
        cps = []
        for mask in range(1, N_DEV):
            peer = (x ^ (mask >> 2), y ^ ((mask >> 1) & 1), c ^ (mask & 1))
            cp = pltpu.make_async_remote_copy(
                src_ref=s_ref, dst_ref=out_ref.at[me], send_sem=send_sems.at[mask - 1], recv_sem=recv_sems.at[mask - 1],
                device_id=peer, device_id_type=MESH)
            cp.start()
            cps.append(cp)
        for mask in range(1, N_DEV):
            peer = (x ^ (mask >> 2), y ^ ((mask >> 1) & 1), c ^ (mask & 1))
            dst = out_ref.at[4 * peer[0] + 2 * peer[1] + peer[2]]
            pltpu.make_async_remote_copy(
                src_ref=dst, dst_ref=dst, send_sem=send_sems.at[mask - 1], recv_sem=recv_sems.at[mask - 1],
                device_id=peer, device_id_type=MESH).wait_recv()
        for cp in cps:
            cp.wait_send()

    vm = pl.BlockSpec(memory_space=pltpu.VMEM)
    return pl.pallas_call(
        body, name="allgather_small", in_specs=[vm], out_specs=vm,
        out_shape=jax.ShapeDtypeStruct((N_DEV,) + slab.shape, slab.dtype),
        scratch_shapes=[pltpu.SemaphoreType.DMA((N_DEV - 1,)), pltpu.SemaphoreType.DMA((N_DEV - 1,))],
        compiler_params=pltpu.CompilerParams(has_side_effects=True),
    )(slab)


def _pack_b(w_mem_kv, w_br_dn, w_br_sb, w_br_mem, w_out, conv_w):
    conv = jnp.pad(conv_w.reshape(3, D_MODEL), ((0, B_ROWS - B_CONV - 3), (0, 0)))
    return jnp.concatenate([w_mem_kv.reshape(128, D_MODEL), w_br_dn, w_br_sb, w_br_mem.reshape(64, D_MODEL), w_out,
                            conv], axis=0)


def _unpack_b(slab):
    return (slab[B_MEMKV:B_BRDN].reshape(1, 256, 512), slab[B_BRDN:B_BRSB].reshape(1, 256, D_MODEL),
            slab[B_BRSB:B_BRMEM].reshape(1, 256, D_MODEL), slab[B_BRMEM:B_OUT].reshape(1, 256, 256),
            slab[B_OUT:B_CONV].reshape(1, 256, D_MODEL), slab[B_CONV:B_CONV + 3].reshape(1, 4, 768))


def _pack_small(norm_g, mem_norm_g, final_g, dn_norm_g, a_log, dt_bias, loss=None):
    slab = jnp.zeros((32, 128), F32)
    slab = slab.at[S_NORM:S_NORM + 8].set(norm_g.reshape(8, 128))
    slab = slab.at[S_MEMNORM:S_MEMNORM + 8].set(mem_norm_g.reshape(8, 128))
    slab = slab.at[S_FINAL:S_FINAL + 8].set(final_g.reshape(8, 128))
    slab = slab.at[S_DNNORM].set(dn_norm_g.reshape(128))
    slab = slab.at[S_ALOG, :N_HEADS].set(a_log.reshape(N_HEADS))
    slab = slab.at[S_DTB, :N_HEADS].set(dt_bias.reshape(N_HEADS))
    if loss is not None:
        slab = slab.at[S_LOSS, 0].set(loss)
    return slab


def _unpack_small(slab):
    return (slab[S_NORM:S_NORM + 8].reshape(1, D_MODEL), slab[S_MEMNORM:S_MEMNORM + 8].reshape(1, D_MODEL),
            slab[S_FINAL:S_FINAL + 8].reshape(D_MODEL), slab[S_DNNORM].reshape(1, 128),
            slab[S_ALOG, :N_HEADS].reshape(1, N_HEADS), slab[S_DTB, :N_HEADS].reshape(1, N_HEADS))


def _reorder_w_in(w_full):
    pad = jnp.zeros((w_full.shape[0], W_R - IN_WIDTH), w_full.dtype)
    return jnp.concatenate([w_full[:, :4096], w_full[:, 4112:], w_full[:, 4096:4112], pad], axis=1)


def _windows_to_w_r(win):
    b = 128
    s0, s1, s2, s3 = win[0], win[1], win[2], win[3]
    e1, e2, e3 = WIN_START[1] * b, WIN_START[2] * b, WIN_START[3] * b
    n1, n2 = e2 - e1, e3 - e2
    return jnp.concatenate([
        s0[:, :e1], s0[:, e1:e1 + b] + s1[:, :b],
        s1[:, b:n1], s1[:, n1:n1 + b] + s2[:, :b],
        s2[:, b:n2], s2[:, n2:n2 + b] + s3[:, :b],
        s3[:, b:], s1[:, _S1_BA_POS:]], axis=1)


def _dproj_windows(dproj_r):
    b = 128
    pieces = []
    for s in range(N_SHARD):
        lo = WIN_START[s] * b
        if s == 1:
            pieces += [dproj_r[:, lo:lo + _S1_BA_POS], dproj_r[:, C_BA:C_BA + b]]
        else:
            pieces.append(dproj_r[:, lo:lo + SHARD_PAD])
    return jnp.concatenate(pieces, axis=1)


def _local_step(x, mem, tgt, norm_g, mem_norm_g, w_r, w_sh, conv_w, a_log, dt_bias, dn_norm_g, w_mem_kv, w_br_dn,
                w_br_sb, w_br_mem, w_out, final_g):
    t = x.shape[0]
    final_row = final_g.reshape(1, D_MODEL)
    alog_row = jnp.zeros((1, 128), F32).at[0, N_HEADS:2 * N_HEADS].set(a_log.reshape(N_HEADS))
    dtb_row = jnp.zeros((1, 128), F32).at[0, N_HEADS:2 * N_HEADS].set(dt_bias.reshape(N_HEADS))

    h = _rmsnorm_fwd(x, norm_g, "norm_fwd")
    proj = _mm(h, w_r, "nn", "in_proj")
    qkv = _dn_prep_fwd(proj, conv_w)
    beta_t, g_t = _dn_gate_fwd(proj, alog_row, dtb_row)
    dn_u, dn_w, dn_qd, dn_kd, dn_a, tinv_all, dn_el = _dn_intra_fwd(qkv, beta_t, g_t)
    o_dn, dn_vn, s_all = _dn_scan_fwd(dn_u, dn_w, dn_qd, dn_kd, dn_a, dn_el)
    o_dn_g = _dn_post_fwd(o_dn, proj, dn_norm_g)
    o_sb, o_sb_g, sb_l = _sb_fwd(proj)
    mem_n = _rmsnorm_fwd(mem, mem_norm_g, "mem_norm_fwd")
    mkv = _mm(mem_n, w_mem_kv, "nn", "mem_kv")
    o_m, o_m_g = _mem_fwd(proj, mkv)
    y_dn = _mm(o_dn_g, w_br_dn, "nn", "br_dn")
    y_sb = _mm(o_sb_g, w_br_sb, "nn", "br_sb")
    y_m = _mm(o_m_g, w_br_mem, "nn", "br_mem")
    merged = _merge_fwd(proj, y_dn, y_sb, y_m)
    mo = _mm(merged, w_out, "nn", "out_proj")
    d_out, d_out_b, loss_row, g_final = _final_loss(x, mo, final_row, tgt)

    g_w_out = _mm(merged, d_out_b, "tn", "g_w_out")
    d_merged = _mm(d_out_b, w_out, "nt", "d_merged")
    dy_dn, dy_sb, dy_m, dg1, dg2, dg3 = _merge_bwd(proj, y_dn, y_sb, y_m, d_merged)
    g_w_br_dn = _mm(o_dn_g, dy_dn, "tn", "g_w_br_dn")
    g_w_br_sb = _mm(o_sb_g, dy_sb, "tn", "g_w_br_sb")
    g_w_br_mem = _mm(o_m_g, dy_m, "tn", "g_w_br_mem")
    d_o_dn_g = _mm(dy_dn, w_br_dn, "nt", "d_o_dn")
    d_o_sb_g = _mm(dy_sb, w_br_sb, "nt", "d_o_sb")
    d_o_m_g = _mm(dy_m, w_br_mem, "nt", "d_o_mem")

    d_mq, d_mz, d_mkv = _mem_bwd(proj, mkv, o_m, d_o_m_g)
    d_mkv_b = _cast_bf16(d_mkv, "cast_dmkv")
    g_w_mem_kv = _mm(mem_n, d_mkv_b, "tn", "g_w_mem_kv")
    d_mem_n = _mm(d_mkv_b, w_mem_kv, "nt", "d_mem_n")
    _, g_mem_norm = _rmsnorm_bwd(mem, mem_norm_g, d_mem_n, jnp.zeros_like(mem), "mem_norm_bwd")

    d_sq, d_sk, d_sv, d_sz = _sb_bwd(proj, o_sb, sb_l, d_o_sb_g)

    d_o_dn, d_dnz, g_dn_norm = _dn_post_bwd(o_dn, proj, dn_norm_g, d_o_dn_g)
    d_vnew, d_kd, d_qd, d_w, d_el = _dn_scan_bwd(dn_w, dn_qd, dn_kd, dn_a, dn_el, dn_vn, s_all, d_o_dn)
    d_qn, d_kn, d_vn, dbeta_t, dg_t = _dn_intra_bwd(qkv, beta_t, g_t, tinv_all, dn_vn, d_o_dn, d_vnew, d_kd, d_qd, d_w, d_el)
    d_conv_in, g_conv = _dn_prep_bwd(proj, conv_w, d_qn, d_kn, d_vn)
    d_ba, g_alog_row, g_dtb_row = _dn_gate_bwd(proj, alog_row, dtb_row, dbeta_t, dg_t)

    dproj_sh = _dproj_windows(
        jnp.concatenate([d_conv_in, d_dnz, d_sq, d_sk, d_sv, d_sz, d_mq, d_mz, dg1, dg2, dg3, d_ba], axis=1))
    g_w_sh = _mm(h, dproj_sh, "tn", "g_w_in", out_dtype=BF16, out_shards=N_SHARD)
    dh = _mm(dproj_sh, w_sh, "nt", "d_h")
    grad_x, g_norm = _rmsnorm_bwd(x, norm_g, dh, d_out, "norm_bwd")

    small = dict(norm_g=g_norm, mem_norm_g=g_mem_norm, final_g=g_final, dn_norm_g=g_dn_norm,
                 a_log=g_alog_row[:, N_HEADS:2 * N_HEADS], dt_bias=g_dtb_row[:, N_HEADS:2 * N_HEADS])
    big = dict(w_sh=g_w_sh, conv_w=g_conv, w_mem_kv=g_w_mem_kv, w_br_dn=g_w_br_dn, w_br_sb=g_w_br_sb,
               w_br_mem=g_w_br_mem, w_out=g_w_out)
    return loss_row[0, 0], grad_x, small, big


def _reduce_scatter(grads):
    x, y, c = lax.axis_index("x"), lax.axis_index("y"), lax.axis_index("c")
    core = jnp.reshape(c, (1,)).astype(jnp.int32)
    place = jnp.stack([2 * x + y, c]).astype(jnp.int32)
    recv = _pair_reduce_send(grads)
    parts = [_pair_add(g, r, core, "pair_add") for g, r in zip(grads, recv)]
    by_chip = _chip_exchange(parts)
    fulls = [_chip_sum(p, b, place, "chip_sum") for p, b in zip(parts, by_chip)]
    return _pair_allgather(fulls)


def kernel(x, mem, norm_g, mem_norm_g, w_in, conv_w, a_log, dt_bias, dn_norm_g, w_mem_kv, w_br_dn, w_br_sb, w_br_mem, w_out, final_g, loss_target, m_norm_g, m_mem_norm_g, m_w_in, m_conv_w, m_a_log, m_dt_bias, m_dn_norm_g, m_w_mem_kv, m_w_br_dn, m_w_br_sb, m_w_br_mem, m_w_out, m_final_g, v_norm_g, v_mem_norm_g, v_w_in, v_conv_w, v_a_log, v_dt_bias, v_dn_norm_g, v_w_mem_kv, v_w_br_dn, v_w_br_sb, v_w_br_mem, v_w_out, v_final_g):
    w_a = w_in[0]
    w_b = _pack_b(w_mem_kv[0], w_br_dn[0], w_br_sb[0], w_br_mem[0], w_out[0], conv_w[0])
    m_b = _pack_b(m_w_mem_kv[0], m_w_br_dn[0], m_w_br_sb[0], m_w_br_mem[0], m_w_out[0], m_conv_w[0])
    v_b = _pack_b(v_w_mem_kv[0], v_w_br_dn[0], v_w_br_sb[0], v_w_br_mem[0], v_w_out[0], v_conv_w[0])

    shard = jnp.reshape(2 * lax.axis_index("x") + lax.axis_index("y"), (1,)).astype(jnp.int32)
    ga, gb = _gather_shards([_cast_to_window(w_a, shard, "cast_w_in"), _cast_bf16(w_b, "cast_w_b")])
    w_r = _windows_to_w_r(ga)
    f_mem_kv = gb[:, B_MEMKV:B_BRDN].reshape(N_SHARD * 256, 512)
    f_br_dn = gb[:, B_BRDN:B_BRSB].reshape(N_SHARD * 256, D_MODEL)
    f_br_sb = gb[:, B_BRSB:B_BRMEM].reshape(N_SHARD * 256, D_MODEL)
    f_br_mem = gb[:, B_BRMEM:B_OUT].reshape(N_SHARD, 256, 256).transpose(1, 0, 2).reshape(256, D_MODEL)
    f_out = gb[:, B_OUT:B_CONV].reshape(N_SHARD * 256, D_MODEL)
    f_conv = gb[:, B_CONV:B_CONV + 3].reshape(N_SHARD, 4, 768).transpose(1, 0, 2).reshape(4, 3 * D_MODEL).astype(F32)

    loss, grad_x, small, big = _local_step(
        x[0], mem[0], loss_target[0], norm_g, mem_norm_g, w_r, ga, f_conv, a_log, dt_bias, dn_norm_g,
        f_mem_kv, f_br_dn, f_br_sb, f_br_mem, f_out, final_g)

    g_b = jnp.stack([
        _pack_b(big["w_mem_kv"][256 * s:256 * (s + 1)], big["w_br_dn"][256 * s:256 * (s + 1)],
                big["w_br_sb"][256 * s:256 * (s + 1)], big["w_br_mem"][:, 256 * s:256 * (s + 1)],
                big["w_out"][256 * s:256 * (s + 1)], big["conv_w"][:, 768 * s:768 * (s + 1)])
        for s in range(N_SHARD)]).astype(BF16)
    gs_in, gs_b = _reduce_scatter([big["w_sh"], g_b])

    gr_in, d_in, nm_in, nv_in = _adamw_window(w_a, gs_in, m_w_in[0], v_w_in[0], shard, "adamw_w_in")
    gr_b, d_b, nm_b, nv_b = _adamw(w_b, gs_b, m_b, v_b, "adamw_b")

    part = _pack_small(small["norm_g"], small["mem_norm_g"], small["final_g"], small["dn_norm_g"],
                       small["a_log"], small["dt_bias"], loss)
    w_s = _pack_small(norm_g, mem_norm_g, final_g, dn_norm_g, a_log, dt_bias)
    m_s = _pack_small(m_norm_g, m_mem_norm_g, m_final_g, m_dn_norm_g, m_a_log, m_dt_bias)
    v_s = _pack_small(v_norm_g, v_mem_norm_g, v_final_g, v_dn_norm_g, v_a_log, v_dt_bias)
    g_s, d_s, nm_s, nv_s = _small_update(_allgather_small(part), w_s, m_s, v_s)

    def assemble(slab_small, a_in, slab_b):
        s_norm, s_memnorm, s_final, s_dnnorm, s_alog, s_dtb = _unpack_small(slab_small)
        b_memkv, b_brdn, b_brsb, b_brmem, b_out, b_conv = _unpack_b(slab_b)
        return [s_norm, s_memnorm, a_in.reshape(1, D_MODEL, IN_WIDTH // N_SHARD), b_conv, s_alog, s_dtb, s_dnnorm,
                b_memkv, b_brdn, b_brsb, b_brmem, b_out, s_final]

    outs = [g_s[S_LOSS, 0], grad_x.reshape(1, -1, D_MODEL)]
    outs += assemble(g_s, gr_in, gr_b)
    outs += assemble(d_s, d_in, d_b)
    outs += assemble(nm_s, nm_in, nm_b)
    outs += assemble(nv_s, nv_in, nv_b)
    return tuple(outs)
```

```python
import functools
import math

import jax
import jax.numpy as jnp
from jax import lax
from jax.experimental import pallas as pl
from jax.experimental.pallas import tpu as pltpu

F32 = jnp.float32
BF16 = jnp.bfloat16
MESH = pl.DeviceIdType.MESH
HIGHEST = lax.Precision.HIGHEST

D_MODEL = 1024
N_HEADS = 8
D_HEAD = 128
DN_CHUNK = 64
DN_GROUP = 8
DN_SCAN_GROUP = 4
SB_BLOCK = 256
SB_HEADS_PER_STEP = 2
SB_QBLOCK = 256
MEM_HEADS = 4
MEM_DH = 64
MEM_W = MEM_HEADS * MEM_DH
NORM_EPS = 1e-6
IN_WIDTH = 11792
N_SHARD = 4
SHARD_W = IN_WIDTH // N_SHARD
SHARD_PAD = 3072
N_DEV = 8

C_DNZ = 3072
C_SBQ = 4096
C_SBZ = 7168
C_MQ = 8192
C_MZ = 8448
C_GATES = 8704
C_BA = 11776
W_R = 11904

ADAM_LR = 0.001
ADAM_B1 = 0.9
ADAM_B2 = 0.999
ADAM_EPS = 1e-08
ADAM_WD = 0.01
ADAM_STEP = 10

VMEM_LIMIT = 56 * 1024 * 1024

B_ROWS = 992
B_MEMKV, B_BRDN, B_BRSB, B_BRMEM, B_OUT, B_CONV = 0, 128, 384, 640, 704, 960
S_NORM, S_MEMNORM, S_FINAL, S_DNNORM, S_ALOG, S_DTB, S_LOSS = 0, 8, 16, 24, 25, 26, 27


def _cp(**kw):
    return pltpu.CompilerParams(vmem_limit_bytes=VMEM_LIMIT, **kw)


def _dot(a, b, dims):
    lead = a.ndim - 2
    ca, cb = {"nn": (1, 0), "nt": (1, 1), "tn": (0, 0)}[dims]
    batch = tuple(range(lead))
    return lax.dot_general(a, b, (((ca + lead,), (cb + lead,)), (batch, batch)), preferred_element_type=F32)


def _chunks(x):
    return x.reshape(x.shape[0] // DN_CHUNK, DN_CHUNK, x.shape[1])


def _unchunk(x):
    return x.reshape(x.shape[0] * x.shape[1], x.shape[2])


def _bdot(a, b, dims):
    return _dot(a.astype(BF16), b.astype(BF16), dims)


def _split(a):
    hi = a.astype(BF16)
    return hi, (a - hi.astype(F32)).astype(BF16)


def _dot3(a, b, dims):
    a1, a2 = _split(a)
    b1, b2 = _split(b)
    return _dot(a1, b1, dims) + (_dot(a1, b2, dims) + _dot(a2, b1, dims))


def _split_dot(a, ones_bf16):
    hi, lo = _split(a.reshape(-1, a.shape[-1]))
    out = _dot(hi, ones_bf16, "nn") + _dot(lo, ones_bf16, "nn")
    return out.reshape(a.shape[:-1] + (ones_bf16.shape[1],))


def _sigmoid(x):
    return 1.0 / (1.0 + jnp.exp(-x))


def _log1p_small(u):
    return jnp.where(u < 1e-2, u * (1.0 - u * (0.5 - u * (1.0 / 3.0))), jnp.log(1.0 + u))


def _log_sigmoid(z):
    return jnp.minimum(z, 0.0) - _log1p_small(jnp.exp(-jnp.abs(z)))


def _pick(dim, cands):
    for c in cands:
        if dim % c == 0:
            return c
    return dim


def _mm(a, b, dims, name, out_dtype=F32, out_shards=1):
    ta, tb = dims[0] == "t", dims[1] == "t"
    m, k = (a.shape[1], a.shape[0]) if ta else a.shape
    b_shards = b.shape[0] if b.ndim == 3 else 1
    n = b.shape[-2] if tb else b.shape[-1]
    tm = _pick(m, (1024, 512, 256))
    tn = _pick(n // out_shards, (512, 384, 256, 128))
    tk = _pick(k // b_shards, (1024, 512, 384, 256))
    nk = k // tk

    def body(a_ref, b_ref, o_ref, acc_ref):
        kk = pl.program_id(2)

        @pl.when(kk == 0)
        def _():
            acc_ref[...] = jnp.zeros_like(acc_ref)

        acc_ref[...] += _bdot(a_ref[...], b_ref[...], dims)

        @pl.when(kk == nk - 1)
        def _():
            o_ref[...] = acc_ref[...].astype(out_dtype)

    a_spec = pl.BlockSpec((tk, tm), lambda i, j, q: (q, i)) if ta else pl.BlockSpec((tm, tk), lambda i, j, q: (i, q))
    if b_shards > 1:
        per_k = k // b_shards // tk
        b_spec = pl.BlockSpec((None, tn, tk), lambda i, j, q: (q // per_k, j, q % per_k))
    else:
        b_spec = pl.BlockSpec((tn, tk), lambda i, j, q: (j, q)) if tb else pl.BlockSpec((tk, tn), lambda i, j, q: (q, j))
    if out_shards > 1:
        per_n = n // out_shards // tn
        out_spec = pl.BlockSpec((None, tm, tn), lambda i, j, q: (j // per_n, i, j % per_n))
        out_shape = jax.ShapeDtypeStruct((out_shards, m, n // out_shards), out_dtype)
    else:
        out_spec = pl.BlockSpec((tm, tn), lambda i, j, q: (i, j))
        out_shape = jax.ShapeDtypeStruct((m, n), out_dtype)
    return pl.pallas_call(
        body, name=name, grid=(m // tm, n // tn, nk),
        in_specs=[a_spec, b_spec], out_specs=out_spec, out_shape=out_shape,
        scratch_shapes=[pltpu.VMEM((tm, tn), F32)],
        compiler_params=_cp(dimension_semantics=("parallel", "parallel", "arbitrary")),
    )(a, b)


def _rmsnorm_fwd(x, g, name):
    t, d = x.shape
    tb = _pick(t, (512, 256))

    def body(x_ref, g_ref, h_ref):
        xv = x_ref[...]
        r = lax.rsqrt(jnp.mean(xv * xv, axis=-1, keepdims=True) + NORM_EPS)
        h_ref[...] = ((xv * r) * g_ref[...]).astype(BF16)

    return pl.pallas_call(
        body, name=name, grid=(t // tb,),
        in_specs=[pl.BlockSpec((tb, d), lambda i: (i, 0)), pl.BlockSpec((1, d), lambda i: (0, 0))],
        out_specs=pl.BlockSpec((tb, d), lambda i: (i, 0)),
        out_shape=jax.ShapeDtypeStruct((t, d), BF16), compiler_params=_cp(),
    )(x, g)


def _rmsnorm_bwd(x, g, dh, resid, name):
    t, d = x.shape
    tb = _pick(t, (256,))

    def body(x_ref, g_ref, dh_ref, r_ref, dx_ref, dg_ref):
        @pl.when(pl.program_id(0) == 0)
        def _():
            dg_ref[...] = jnp.zeros_like(dg_ref)

        xv = x_ref[...]
        r = lax.rsqrt(jnp.mean(xv * xv, axis=-1, keepdims=True) + NORM_EPS)
        xhat = xv * r
        dhv = dh_ref[...]
        dg_ref[...] += jnp.sum(dhv * xhat, axis=0, keepdims=True)
        dxh = dhv * g_ref[...]
        dx_ref[...] = r_ref[...] + r * (dxh - xhat * jnp.mean(dxh * xhat, axis=-1, keepdims=True))

    row = pl.BlockSpec((tb, d), lambda i: (i, 0))
    vec = pl.BlockSpec((1, d), lambda i: (0, 0))
    return pl.pallas_call(
        body, name=name, grid=(t // tb,), in_specs=[row, vec, row, row], out_specs=[row, vec],
        out_shape=[jax.ShapeDtypeStruct((t, d), F32), jax.ShapeDtypeStruct((1, d), F32)], compiler_params=_cp(),
    )(x, g, dh, resid)


def _conv_silu(xv, w, row):
    y = xv * w[3:4, :]
    for s in (1, 2, 3):
        xs = jnp.where(row >= s, pltpu.roll(xv, s, 0), 0.0)
        y = y + xs * w[3 - s:4 - s, :]
    return y, y * _sigmoid(y)


def _dn_prep_fwd(proj, conv_w):
    t = proj.shape[0]

    def body(p_ref, w_ref, o_ref):
        j = pl.program_id(0)
        xv = p_ref[...]
        row = lax.broadcasted_iota(jnp.int32, xv.shape, 0)
        _, a = _conv_silu(xv, w_ref[...], row)
        inv = lax.rsqrt(jnp.sum(a * a, axis=-1, keepdims=True) + NORM_EPS)
        scale = jnp.where(j < N_HEADS, D_HEAD ** -0.5, 1.0)
        normed = jnp.where(j < 2 * N_HEADS, 1.0, 0.0)
        o_ref[...] = a * (normed * (inv * scale) + (1.0 - normed))

    return pl.pallas_call(
        body, name="dn_prep_fwd", grid=(3 * N_HEADS,),
        in_specs=[pl.BlockSpec((t, D_HEAD), lambda j: (0, j)), pl.BlockSpec((4, D_HEAD), lambda j: (0, j))],
        out_specs=pl.BlockSpec((t, D_HEAD), lambda j: (0, j)),
        out_shape=jax.ShapeDtypeStruct((t, 3 * D_MODEL), F32), compiler_params=_cp(),
    )(proj, conv_w)


def _dn_prep_bwd(proj, conv_w, dq, dk, dv):
    t = proj.shape[0]

    def body(p_ref, w_ref, dq_ref, dk_ref, dv_ref, dp_ref, dw_ref):
        j = pl.program_id(0)
        xv = p_ref[...]
        w = w_ref[...]
        row = lax.broadcasted_iota(jnp.int32, xv.shape, 0)
        y, a = _conv_silu(xv, w, row)
        part = jnp.zeros(xv.shape, jnp.int32) + j // N_HEADS
        dn = jnp.where(part == 0, dq_ref[...], jnp.where(part == 1, dk_ref[...], dv_ref[...]))
        inv = lax.rsqrt(jnp.sum(a * a, axis=-1, keepdims=True) + NORM_EPS)
        scale = jnp.where(j < N_HEADS, D_HEAD ** -0.5, 1.0)
        ds = dn * scale
        da_norm = inv * ds - a * (inv * inv * inv) * jnp.sum(ds * a, axis=-1, keepdims=True)
        normed = jnp.where(j < 2 * N_HEADS, 1.0, 0.0)
        da = normed * da_norm + (1.0 - normed) * dn
        s = _sigmoid(y)
        dy = da * (s * (1.0 + y * (1.0 - s)))
        dx = dy * w[3:4, :]
        dw_ref[3:4, :] = jnp.sum(dy * xv, axis=0, keepdims=True)
        for sft in (1, 2, 3):
            xs = jnp.where(row >= sft, pltpu.roll(xv, sft, 0), 0.0)
            dw_ref[3 - sft:4 - sft, :] = jnp.sum(dy * xs, axis=0, keepdims=True)
            dys = jnp.where(row < t - sft, pltpu.roll(dy, t - sft, 0), 0.0)
            dx = dx + dys * w[3 - sft:4 - sft, :]
        dp_ref[...] = dx.astype(BF16)

    blk = pl.BlockSpec((t, D_HEAD), lambda j: (0, j))
    wblk = pl.BlockSpec((4, D_HEAD), lambda j: (0, j))

    def grad(part):
        return pl.BlockSpec((t, D_HEAD), lambda j: (0, jnp.clip(j - part * N_HEADS, 0, N_HEADS - 1)))

    return pl.pallas_call(
        body, name="dn_prep_bwd", grid=(3 * N_HEADS,), in_specs=[blk, wblk, grad(0), grad(1), grad(2)],
        out_specs=[blk, wblk],
        out_shape=[jax.ShapeDtypeStruct((t, 3 * D_MODEL), BF16), jax.ShapeDtypeStruct((4, 3 * D_MODEL), F32)],
        compiler_params=_cp(),
    )(proj, conv_w, dq, dk, dv)


def _softplus_parts(xv):
    e = jnp.exp(-jnp.abs(xv))
    return jnp.maximum(xv, 0.0) + _log1p_small(e)


def _chunk_scan(v, row, reverse):
    t = v.shape[0]
    pos = row & (DN_CHUNK - 1)
    s = 1
    while s < DN_CHUNK:
        if reverse:
            v = v + jnp.where(pos < DN_CHUNK - s, pltpu.roll(v, t - s, 0), 0.0)
        else:
            v = v + jnp.where(pos >= s, pltpu.roll(v, s, 0), 0.0)
        s *= 2
    return v


def _dn_gate_fwd(proj, alog_row, dtb_row):
    t = proj.shape[0]

    def body(p_ref, al_ref, dt_ref, b_ref, g_ref):
        p = p_ref[...]
        row = lax.broadcasted_iota(jnp.int32, p.shape, 0)
        b_ref[...] = _sigmoid(p)
        g = -jnp.exp(al_ref[...]) * _softplus_parts(p + dt_ref[...])
        g_ref[...] = _chunk_scan(g, row, reverse=False)

    blk = pl.BlockSpec((t, 128), lambda i: (0, C_BA // 128))
    vec = pl.BlockSpec((1, 128), lambda i: (0, 0))
    out = pl.BlockSpec((t, 128), lambda i: (0, 0))
    return pl.pallas_call(
        body, name="dn_gate_fwd", grid=(1,), in_specs=[blk, vec, vec], out_specs=[out, out],
        out_shape=[jax.ShapeDtypeStruct((t, 128), F32)] * 2, compiler_params=_cp(),
    )(proj, alog_row, dtb_row)


def _dn_gate_bwd(proj, alog_row, dtb_row, dbeta, dgc):
    t = proj.shape[0]

    def body(p_ref, al_ref, dt_ref, db_ref, dg_ref, dp_ref, dal_ref, ddt_ref):
        p = p_ref[...]
        row = lax.broadcasted_iota(jnp.int32, p.shape, 0)
        lane = lax.broadcasted_iota(jnp.int32, p.shape, 1)
        s = _sigmoid(p)
        d_b = db_ref[...] * s * (1.0 - s)
        dg = _chunk_scan(dg_ref[...], row, reverse=True)
        xa = p + dt_ref[...]
        ea = jnp.exp(al_ref[...])
        g = -ea * _softplus_parts(xa)
        d_a = dg * (-ea) * _sigmoid(xa)
        dp_ref[...] = jnp.where(lane < N_HEADS, d_b, jnp.where(lane < 2 * N_HEADS, d_a, 0.0)).astype(BF16)
        dal_ref[...] = jnp.sum(dg * g, axis=0, keepdims=True)
        ddt_ref[...] = jnp.sum(d_a, axis=0, keepdims=True)

    blk = pl.BlockSpec((t, 128), lambda i: (0, C_BA // 128))
    vec = pl.BlockSpec((1, 128), lambda i: (0, 0))
    full = pl.BlockSpec((t, 128), lambda i: (0, 0))
    return pl.pallas_call(
        body, name="dn_gate_bwd", grid=(1,), in_specs=[blk, vec, vec, full, full], out_specs=[full, vec, vec],
        out_shape=[jax.ShapeDtypeStruct((t, 128), BF16), jax.ShapeDtypeStruct((1, 128), F32),
                   jax.ShapeDtypeStruct((1, 128), F32)], compiler_params=_cp(),
    )(proj, alog_row, dtb_row, dbeta, dgc)


def _col_to_row(col, eye):
    return jnp.sum(jnp.where(eye, col, 0.0), axis=-2, keepdims=True)


def _row_to_col(rowv, eye):
    return jnp.sum(jnp.where(eye, rowv, 0.0), axis=-1, keepdims=True)


def _tri_inverse(m, ri, ci):
    eye = (ri == ci).astype(F32)
    b16 = (ri >> 4) == (ci >> 4)
    b32 = (ri >> 5) == (ci >> 5)
    m1 = jnp.where(b16, m, 0.0)
    x = eye - m1
    p = _dot3(m1, m1, "nn")
    x = x + _dot3(x, p, "nn")
    p = _dot3(p, p, "nn")
    x = x + _dot3(x, p, "nn")
    p = _dot3(p, p, "nn")
    x = x + _dot3(x, p, "nn")
    c1 = jnp.where(jnp.logical_and(b32, jnp.logical_not(b16)), m, 0.0)
    x = x - _dot3(_dot3(x, c1, "nn"), x, "nn")
    c2 = jnp.where(b32, 0.0, m)
    x = x - _dot3(_dot3(x, c2, "nn"), x, "nn")
    return x


def _dn_chunk_common(q, k, gc, ri, ci):
    eye = ri == ci
    g_row = _col_to_row(gc, eye)
    diff = jnp.minimum(gc - g_row, 0.0)
    gam = jnp.where(ri >= ci, jnp.exp(diff), 0.0)
    kk = _bdot(k, k, "nt")
    qk = _bdot(q, k, "nt")
    rcol = lax.broadcasted_iota(jnp.int32, gc.shape, gc.ndim - 2)
    last = jnp.sum(jnp.where(rcol == DN_CHUNK - 1, gc, 0.0), axis=-2, keepdims=True)
    e_g = jnp.exp(gc)
    dec = jnp.exp(last - gc)
    return eye, gam, kk, qk, last, e_g, dec, rcol


def _dn_specs(t, rows_blk):
    def head(off):
        return pl.BlockSpec((rows_blk, D_HEAD), lambda g, h: (g, off + h))

    lanes = pl.BlockSpec((rows_blk, 128), lambda g, h: (g, 0))
    hm = pl.BlockSpec((None, rows_blk, D_HEAD), lambda g, h: (h, g, 0))
    sq = pl.BlockSpec((1, rows_blk, DN_CHUNK), lambda g, h: (h, g, 0))
    tile = pl.BlockSpec((1, rows_blk // DN_CHUNK, 8, 128), lambda g, h: (h, g, 0, 0))
    return head, lanes, hm, sq, tile


def _head_column(slab, lane_idx):
    lane = lax.broadcasted_iota(jnp.int32, slab.shape, 1)
    return _chunks(jnp.sum(jnp.where(lane == lane_idx, slab, 0.0), axis=1, keepdims=True))


def _dn_intra_fwd(qkv, beta_t, g_t):
    t = qkv.shape[0]
    n_chunks = t // DN_CHUNK
    rows_blk = min(DN_GROUP * DN_CHUNK, t)

    def body(q_ref, k_ref, v_ref, b_ref, g_ref, u_ref, w_ref, qd_ref, kd_ref, a_ref, ti_ref, el_ref):
        ri = lax.broadcasted_iota(jnp.int32, (DN_CHUNK, DN_CHUNK), 0)
        ci = lax.broadcasted_iota(jnp.int32, (DN_CHUNK, DN_CHUNK), 1)
        h = pl.program_id(1)
        q, k, v = (_chunks(r[...]) for r in (q_ref, k_ref, v_ref))
        b, gc = _head_column(b_ref[...], h), _head_column(g_ref[...], h + N_HEADS)
        _, gam, kk, qk, last, e_g, dec, _ = _dn_chunk_common(q, k, gc, ri, ci)
        tinv = _tri_inverse(jnp.where(ri > ci, b * kk * gam, 0.0), ri, ci)
        u_ref[...] = _unchunk(_bdot(tinv, v * b, "nn"))
        w_ref[...] = _unchunk(_bdot(tinv, k * (b * e_g), "nn"))
        qd_ref[...] = _unchunk(q * e_g)
        kd_ref[...] = _unchunk(k * dec)
        a_ref[0] = _unchunk(qk * gam)
        ti_ref[0] = _unchunk(tinv)
        el_ref[0] = jnp.broadcast_to(jnp.exp(last), (rows_blk // DN_CHUNK, 8, 128))

    head, lanes, hm, sq, tile = _dn_specs(t, rows_blk)
    act = jax.ShapeDtypeStruct((N_HEADS, t, D_HEAD), F32)
    sqs = jax.ShapeDtypeStruct((N_HEADS, t, DN_CHUNK), F32)
    return pl.pallas_call(
        body, name="dn_intra_fwd", grid=(t // rows_blk, N_HEADS),
        in_specs=[head(0), head(N_HEADS), head(2 * N_HEADS), lanes, lanes],
        out_specs=[hm] * 4 + [sq, sq, tile],
        out_shape=[act] * 4 + [sqs, sqs, jax.ShapeDtypeStruct((N_HEADS, n_chunks, 8, 128), F32)],
        compiler_params=_cp(),
    )(qkv, qkv, qkv, beta_t, g_t)


def _dn_scan_specs(t, rows_blk, reverse):
    n_groups = t // rows_blk

    def at(g):
        return n_groups - 1 - g if reverse else g

    per = rows_blk // DN_CHUNK
    act = pl.BlockSpec((N_HEADS, rows_blk, D_HEAD), lambda g: (0, at(g), 0))
    sq = pl.BlockSpec((N_HEADS, rows_blk, DN_CHUNK), lambda g: (0, at(g), 0))
    state = pl.BlockSpec((N_HEADS, per, D_HEAD, D_HEAD), lambda g: (0, at(g), 0, 0))
    tile = pl.BlockSpec((N_HEADS, per, 8, 128), lambda g: (0, at(g), 0, 0))
    return act, sq, state, tile


def _dn_scan_fwd(u, w, qd, kd, a, el):
    t = u.shape[1]
    n_chunks = t // DN_CHUNK
    rows_blk = DN_SCAN_GROUP * DN_CHUNK

    def body(u_ref, w_ref, qd_ref, kd_ref, a_ref, el_ref, o_ref, vn_ref, s_ref, s_scr):
        @pl.when(pl.program_id(0) == 0)
        def _():
            s_scr[...] = jnp.zeros_like(s_scr)

        for cc in range(DN_SCAN_GROUP):
            rows = slice(cc * DN_CHUNK, (cc + 1) * DN_CHUNK)
            s = s_scr[...]
            s_ref[:, cc] = s
            v_new = u_ref[:, rows, :] - _bdot(w_ref[:, rows, :], s, "nn")
            vn_ref[:, rows, :] = v_new
            o_ref[:, rows, :] = _bdot(qd_ref[:, rows, :], s, "nn") + _bdot(a_ref[:, rows, :], v_new, "nn")
            s_scr[...] = s * el_ref[:, cc][:, 0:1, :] + _bdot(kd_ref[:, rows, :], v_new, "tn")

    act, sq, state, tile = _dn_scan_specs(t, rows_blk, reverse=False)
    shp = jax.ShapeDtypeStruct((N_HEADS, t, D_HEAD), F32)
    return pl.pallas_call(
        body, name="dn_scan_fwd", grid=(t // rows_blk,),
        in_specs=[act, act, act, act, sq, tile], out_specs=[act, act, state],
        out_shape=[shp, shp, jax.ShapeDtypeStruct((N_HEADS, n_chunks, D_HEAD, D_HEAD), F32)],
        scratch_shapes=[pltpu.VMEM((N_HEADS, D_HEAD, D_HEAD), F32)],
        compiler_params=_cp(dimension_semantics=("arbitrary",)),
    )(u, w, qd, kd, a, el)


def _dn_scan_bwd(w, qd, kd, a, el, vn, s_all, do):
    t = w.shape[1]
    n_chunks = t // DN_CHUNK
    rows_blk = DN_SCAN_GROUP * DN_CHUNK

    def body(w_ref, qd_ref, kd_ref, a_ref, el_ref, vn_ref, s_ref, do_ref, dvn_ref, dkd_ref, dqd_ref, dw_ref, dl_ref, ds_scr):
        @pl.when(pl.program_id(0) == 0)
        def _():
            ds_scr[...] = jnp.zeros_like(ds_scr)

        for cc in reversed(range(DN_SCAN_GROUP)):
            rows = slice(cc * DN_CHUNK, (cc + 1) * DN_CHUNK)
            s = s_ref[:, cc]
            d_s = ds_scr[...]
            e_last = el_ref[:, cc][:, 0:1, :]
            d_o = do_ref[:, rows, :]
            dv_new = _bdot(a_ref[:, rows, :], d_o, "tn") + _bdot(kd_ref[:, rows, :], d_s, "nn")
            ds_scr[...] = d_s * e_last + _bdot(qd_ref[:, rows, :], d_o, "tn") - _bdot(w_ref[:, rows, :], dv_new, "tn")
            dvn_ref[:, rows, :] = dv_new
            dkd_ref[:, rows, :] = _bdot(vn_ref[:, rows, :], d_s, "nt")
            dqd_ref[:, rows, :] = _bdot(d_o, s, "nt")
            dw_ref[:, rows, :] = -_bdot(dv_new, s, "nt")
            dlast = jnp.sum(jnp.sum(d_s * s, axis=2, keepdims=True), axis=1, keepdims=True)
            dl_ref[:, cc] = jnp.broadcast_to(dlast * e_last, (N_HEADS, 8, 128))

    act, sq, state, tile = _dn_scan_specs(t, rows_blk, reverse=True)
    shp = jax.ShapeDtypeStruct((N_HEADS, t, D_HEAD), F32)
    return pl.pallas_call(
        body, name="dn_scan_bwd", grid=(t // rows_blk,),
        in_specs=[act, act, act, sq, tile, act, state, act], out_specs=[act] * 4 + [tile],
        out_shape=[shp] * 4 + [jax.ShapeDtypeStruct((N_HEADS, n_chunks, 8, 128), F32)],
        scratch_shapes=[pltpu.VMEM((N_HEADS, D_HEAD, D_HEAD), F32)],
        compiler_params=_cp(dimension_semantics=("arbitrary",)),
    )(w, qd, kd, a, el, vn, s_all, do)


def _dn_intra_bwd(qkv, beta_t, g_t, tinv_all, vn, do, dvn, dkd, dqd, dw, dl):
    t = qkv.shape[0]
    rows_blk = min(DN_GROUP * DN_CHUNK, t)

    def body(q_ref, k_ref, v_ref, b_ref, g_ref, ti_ref, vn_ref, do_ref, dvn_ref, dkd_ref, dqd_ref, dw_ref, dl_ref,
             dq_ref, dk_ref, dv_ref, db_ref, dg_ref):
        ri = lax.broadcasted_iota(jnp.int32, (DN_CHUNK, DN_CHUNK), 0)
        ci = lax.broadcasted_iota(jnp.int32, (DN_CHUNK, DN_CHUNK), 1)
        h = pl.program_id(1)
        q, k, v = (_chunks(r[...]) for r in (q_ref, k_ref, v_ref))
        b, gc = _head_column(b_ref[...], h), _head_column(g_ref[...], h + N_HEADS)
        tinv = _chunks(ti_ref[0])
        dv_new, dk_dec, dq_dec, d_w = (_chunks(r[...]) for r in (dvn_ref, dkd_ref, dqd_ref, dw_ref))
        eye, gam, kk, qk, _, e_g, dec, rcol = _dn_chunk_common(q, k, gc, ri, ci)
        bv = v * b
        bk = k * (b * e_g)

        d_a = jnp.where(ri >= ci, _bdot(_chunks(do_ref[...]), _chunks(vn_ref[...]), "nt"), 0.0)
        dbv = _bdot(tinv, dv_new, "tn")
        dbk = _bdot(tinv, d_w, "tn")
        d_tinv = _bdot(dv_new, bv, "nt") + _bdot(d_w, bk, "nt")
        d_m = -jnp.where(ri > ci, _dot3(_dot3(tinv, d_tinv, "tn"), tinv, "nt"), 0.0)

        d_kk = d_m * b * gam
        d_gam = d_m * b * kk + d_a * qk
        d_qk = d_a * gam
        dq_ref[...] = _unchunk(_bdot(d_qk, k, "nn") + dq_dec * e_g)
        dk_ref[...] = _unchunk(_bdot(d_qk, q, "tn") + _bdot(d_kk, k, "nn") + _bdot(d_kk, k, "tn")
                               + dk_dec * dec + dbk * (b * e_g))
        dv_ref[...] = _unchunk(dbv * b)
        d_b = _unchunk(jnp.sum(d_m * kk * gam, axis=-1, keepdims=True) + jnp.sum(dbv * v, axis=-1, keepdims=True)
                       + jnp.sum(dbk * k, axis=-1, keepdims=True) * e_g)

        xg = d_gam * gam
        kdk = jnp.sum(dk_dec * (k * dec), axis=-1, keepdims=True)
        d_gc = (jnp.sum(xg, axis=-1, keepdims=True) - _row_to_col(jnp.sum(xg, axis=-2, keepdims=True), eye)
                + jnp.sum(dq_dec * (q * e_g), axis=-1, keepdims=True) - kdk
                + jnp.sum(dbk * bk, axis=-1, keepdims=True))
        d_last_total = dl_ref[0][:, 0:1, 0:1] + jnp.sum(kdk, axis=-2, keepdims=True)
        d_g = _unchunk(d_gc + jnp.where(rcol == DN_CHUNK - 1, d_last_total, 0.0))

        @pl.when(h == 0)
        def _():
            db_ref[...] = jnp.zeros_like(db_ref)
            dg_ref[...] = jnp.zeros_like(dg_ref)

        lane = lax.broadcasted_iota(jnp.int32, db_ref.shape, 1)
        db_ref[...] += jnp.where(lane == h, d_b, 0.0)
        dg_ref[...] += jnp.where(lane == h + N_HEADS, d_g, 0.0)

    head, lanes, hm, sq, tile = _dn_specs(t, rows_blk)
    return pl.pallas_call(
        body, name="dn_intra_bwd", grid=(t // rows_blk, N_HEADS),
        in_specs=[head(0), head(N_HEADS), head(2 * N_HEADS), lanes, lanes, sq] + [hm] * 6 + [tile],
        out_specs=[head(0), head(0), head(0), lanes, lanes],
        out_shape=[jax.ShapeDtypeStruct((t, D_MODEL), F32)] * 3 + [jax.ShapeDtypeStruct((t, 128), F32)] * 2,
        compiler_params=_cp(),
    )(qkv, qkv, qkv, beta_t, g_t, tinv_all, vn, do, dvn, dkd, dqd, dw, dl)


def _dn_post_fwd(o, proj, gn):
    t = o.shape[1]

    def body(o_ref, z_ref, g_ref, out_ref):
        ov, z = o_ref[...], z_ref[...]
        r = lax.rsqrt(jnp.mean(ov * ov, axis=-1, keepdims=True) + NORM_EPS)
        out_ref[...] = (((ov * r) * g_ref[...]) * (z * _sigmoid(z))).astype(BF16)

    blk = pl.BlockSpec((t, D_HEAD), lambda h: (0, h))
    return pl.pallas_call(
        body, name="dn_post_fwd", grid=(N_HEADS,),
        in_specs=[pl.BlockSpec((None, t, D_HEAD), lambda h: (h, 0, 0)),
                  pl.BlockSpec((t, D_HEAD), lambda h: (0, C_DNZ // D_HEAD + h)),
                  pl.BlockSpec((1, D_HEAD), lambda h: (0, 0))],
        out_specs=blk, out_shape=jax.ShapeDtypeStruct((t, D_MODEL), BF16), compiler_params=_cp(),
    )(o, proj, gn)


def _dn_post_bwd(o, proj, gn, dout):
    t = o.shape[1]

    def body(o_ref, z_ref, g_ref, d_ref, do_ref, dz_ref, dg_ref):
        @pl.when(pl.program_id(0) == 0)
        def _():
            dg_ref[...] = jnp.zeros_like(dg_ref)

        ov, z, d = o_ref[...], z_ref[...], d_ref[...]
        r = lax.rsqrt(jnp.mean(ov * ov, axis=-1, keepdims=True) + NORM_EPS)
        ohat = ov * r
        s = _sigmoid(z)
        d_on = d * (z * s)
        dz_ref[...] = (d * (ohat * g_ref[...]) * (s * (1.0 + z * (1.0 - s)))).astype(BF16)
        dg_ref[...] += jnp.sum(d_on * ohat, axis=0, keepdims=True)
        dxh = d_on * g_ref[...]
        do_ref[...] = r * (dxh - ohat * jnp.mean(dxh * ohat, axis=-1, keepdims=True))

    blk = pl.BlockSpec((t, D_HEAD), lambda h: (0, h))
    hm = pl.BlockSpec((None, t, D_HEAD), lambda h: (h, 0, 0))
    vec = pl.BlockSpec((1, D_HEAD), lambda h: (0, 0))
    return pl.pallas_call(
        body, name="dn_post_bwd", grid=(N_HEADS,),
        in_specs=[hm, pl.BlockSpec((t, D_HEAD), lambda h: (0, C_DNZ // D_HEAD + h)), vec, blk],
        out_specs=[hm, blk, vec],
        out_shape=[jax.ShapeDtypeStruct((N_HEADS, t, D_HEAD), F32), jax.ShapeDtypeStruct((t, D_MODEL), BF16),
                   jax.ShapeDtypeStruct((1, D_HEAD), F32)], compiler_params=_cp(),
    )(o, proj, gn, dout)


def _sb_fwd(proj):
    t = proj.shape[0]
    qblk = min(SB_QBLOCK, t)
    scale = 1.0 / math.sqrt(D_HEAD)

    hp = SB_HEADS_PER_STEP
    wid = hp * D_HEAD

    def body(q_ref, k_ref, v_ref, z_ref, o_ref, og_ref, l_ref, qb, kb, vb):
        for hh in range(hp):
            hs = slice(hh * D_HEAD, (hh + 1) * D_HEAD)
            qb[hh] = q_ref[:, hs].astype(BF16)
            kb[hh] = k_ref[:, hs].astype(BF16)
            vb[hh] = v_ref[:, hs].astype(BF16)
        ri = lax.broadcasted_iota(jnp.int32, (qblk, SB_BLOCK), 0)
        ci = lax.broadcasted_iota(jnp.int32, (qblk, SB_BLOCK), 1)
        r2 = lax.broadcasted_iota(jnp.int32, (SB_BLOCK, SB_BLOCK), 0)
        c2 = lax.broadcasted_iota(jnp.int32, (SB_BLOCK, SB_BLOCK), 1)
        upper = (r2 > c2).astype(BF16)
        nkb = qblk // SB_BLOCK

        def qblock(i, carry):
            rows = pl.ds(pl.multiple_of(i * qblk, qblk), qblk)
            qi = qb[:, rows, :]

            def kblock(jj, st):
                acc, c = st
                j = (i + 1) * nkb - 1 - jj
                cols = pl.ds(pl.multiple_of(j * SB_BLOCK, SB_BLOCK), SB_BLOCK)
                mask = (j * SB_BLOCK + ci) < (i * qblk + ri)
                z = _dot(qi, kb[:, cols, :], "nt") * scale
                lb = jnp.minimum(z, 0.0) - jnp.log(1.0 + jnp.exp(-jnp.abs(z)))
                lf = jnp.where(mask, lb - z, 0.0)
                surv = _split_dot(lf, upper) + c
                att = jnp.where(mask, jnp.exp(lb + surv), 0.0)
                acc = acc + _dot(att.astype(BF16), vb[:, cols, :], "nn")
                return acc, c + jnp.sum(lf, axis=-1, keepdims=True)

            init = (jnp.zeros((hp, qblk, D_HEAD), F32), jnp.zeros((hp, qblk, 1), F32))
            acc, c = lax.fori_loop(0, (i + 1) * nkb, kblock, init)
            l_ref[:, rows, :] = c
            for hh in range(hp):
                hs = slice(hh * D_HEAD, (hh + 1) * D_HEAD)
                zg = z_ref[rows, hs]
                o_ref[rows, hs] = acc[hh]
                og_ref[rows, hs] = (acc[hh] * (zg * _sigmoid(zg))).astype(BF16)
            return carry

        lax.fori_loop(0, t // qblk, qblock, 0)

    def head(off):
        return pl.BlockSpec((t, wid), lambda h: (0, off // wid + h))

    out = pl.BlockSpec((t, wid), lambda h: (0, h))
    return pl.pallas_call(
        body, name="sb_fwd", grid=(N_HEADS // hp,),
        in_specs=[head(C_SBQ), head(C_SBQ + D_MODEL), head(C_SBQ + 2 * D_MODEL), head(C_SBZ)],
        out_specs=[out, out, pl.BlockSpec((hp, t, 1), lambda h: (h, 0, 0))],
        out_shape=[jax.ShapeDtypeStruct((t, D_MODEL), F32), jax.ShapeDtypeStruct((t, D_MODEL), BF16),
                   jax.ShapeDtypeStruct((N_HEADS, t, 1), F32)],
        scratch_shapes=[pltpu.VMEM((hp, t, D_HEAD), BF16)] * 3, compiler_params=_cp(),
    )(proj, proj, proj, proj)


def _sb_bwd(proj, o, ltot, dog):
    t = proj.shape[0]
    qblk = min(SB_QBLOCK, t)
    scale = 1.0 / math.sqrt(D_HEAD)

    hp = SB_HEADS_PER_STEP
    wid = hp * D_HEAD

    def body(q_ref, k_ref, v_ref, z_ref, o_ref, l_ref, d_ref, dq_ref, dk_ref, dv_ref, dz_ref,
             qb, kb, vb, dob, dk_scr, dv_scr):
        for hh in range(hp):
            hs = slice(hh * D_HEAD, (hh + 1) * D_HEAD)
            qb[hh] = q_ref[:, hs].astype(BF16)
            kb[hh] = k_ref[:, hs].astype(BF16)
            vb[hh] = v_ref[:, hs].astype(BF16)
            zg = z_ref[:, hs]
            sg = _sigmoid(zg)
            dgo = d_ref[:, hs]
            dob[hh] = (dgo * (zg * sg)).astype(BF16)
            dz_ref[:, hs] = (dgo * o_ref[:, hs] * (sg * (1.0 + zg * (1.0 - sg)))).astype(BF16)
        dk_scr[...] = jnp.zeros_like(dk_scr)
        dv_scr[...] = jnp.zeros_like(dv_scr)
        ri = lax.broadcasted_iota(jnp.int32, (qblk, SB_BLOCK), 0)
        ci = lax.broadcasted_iota(jnp.int32, (qblk, SB_BLOCK), 1)
        r2 = lax.broadcasted_iota(jnp.int32, (SB_BLOCK, SB_BLOCK), 0)
        c2 = lax.broadcasted_iota(jnp.int32, (SB_BLOCK, SB_BLOCK), 1)
        incl = (r2 <= c2).astype(BF16)
        below = (r2 < c2).astype(BF16)

        def qblock(i, carry):
            rows = pl.ds(pl.multiple_of(i * qblk, qblk), qblk)
            qi = qb[:, rows, :]
            d_o = dob[:, rows, :]
            ltot = l_ref[:, rows, :]

            def kblock(j, st):
                dq, cpre, ce = st
                cols = pl.ds(pl.multiple_of(j * SB_BLOCK, SB_BLOCK), SB_BLOCK)
                mask = (j * SB_BLOCK + ci) < (i * qblk + ri)
                kj, vj = kb[:, cols, :], vb[:, cols, :]
                z = _dot(qi, kj, "nt") * scale
                lb = jnp.minimum(z, 0.0) - jnp.log(1.0 + jnp.exp(-jnp.abs(z)))
                lf = jnp.where(mask, lb - z, 0.0)
                surv = ltot - (cpre + _split_dot(lf, incl))
                att = jnp.where(mask, jnp.exp(lb + surv), 0.0)
                e = _dot(d_o, vj, "nt") * att
                dlf = ce + _split_dot(e, below)
                dzz = jnp.where(mask, e - (e + dlf) * jnp.exp(lb), 0.0).astype(BF16)
                dq = dq + _dot(dzz, kj, "nn")
                dk_scr[:, cols, :] += _dot(dzz, qi, "tn")
                dv_scr[:, cols, :] += _dot(att.astype(BF16), d_o, "tn")
                return dq, cpre + jnp.sum(lf, axis=-1, keepdims=True), ce + jnp.sum(e, axis=-1, keepdims=True)

            zero_col = jnp.zeros((hp, qblk, 1), F32)
            init = (jnp.zeros((hp, qblk, D_HEAD), F32), zero_col, zero_col)
            dq, _, _ = lax.fori_loop(0, (i + 1) * (qblk // SB_BLOCK), kblock, init)
            for hh in range(hp):
                dq_ref[rows, hh * D_HEAD:(hh + 1) * D_HEAD] = (dq[hh] * scale).astype(BF16)
            return carry

        lax.fori_loop(0, t // qblk, qblock, 0)
        for hh in range(hp):
            hs = slice(hh * D_HEAD, (hh + 1) * D_HEAD)
            dk_ref[:, hs] = (dk_scr[hh] * scale).astype(BF16)
            dv_ref[:, hs] = dv_scr[hh].astype(BF16)

    def head(off):
        return pl.BlockSpec((t, wid), lambda h: (0, off // wid + h))

    return pl.pallas_call(
        body, name="sb_bwd", grid=(N_HEADS // hp,),
        in_specs=[head(C_SBQ), head(C_SBQ + D_MODEL), head(C_SBQ + 2 * D_MODEL), head(C_SBZ), head(0),
                  pl.BlockSpec((hp, t, 1), lambda h: (h, 0, 0)), head(0)],
        out_specs=[head(0)] * 4, out_shape=[jax.ShapeDtypeStruct((t, D_MODEL), BF16)] * 4,
        scratch_shapes=[pltpu.VMEM((hp, t, D_HEAD), BF16)] * 4 + [pltpu.VMEM((hp, t, D_HEAD), F32)] * 2,
        compiler_params=_cp(),
    )(proj, proj, proj, proj, o, ltot, dog)


def _mem_fwd(proj, mkv):
    t = proj.shape[0]
    tq = _pick(t, (512, 256))
    m_len = mkv.shape[0]
    scale = 1.0 / math.sqrt(MEM_DH)

    def body(q_ref, z_ref, kv_ref, o_ref, og_ref):
        q = q_ref[...]
        mk = kv_ref[:, :MEM_W].astype(BF16)
        mv = kv_ref[:, MEM_W:].astype(BF16)
        lane = lax.broadcasted_iota(jnp.int32, q.shape, 1) >> 6
        o = jnp.zeros(q.shape, F32)
        for h in range(MEM_HEADS):
            s = _bdot(jnp.where(lane == h, q, 0.0), mk, "nt") * scale
            p = jnp.exp(s - jnp.max(s, axis=-1, keepdims=True))
            p = p / jnp.sum(p, axis=-1, keepdims=True)
            o = o + jnp.where(lane == h, _bdot(p, mv, "nn"), 0.0)
        z = z_ref[...]
        o_ref[...] = o
        og_ref[...] = (o * (z * _sigmoid(z))).astype(BF16)

    out = pl.BlockSpec((tq, MEM_W), lambda i: (i, 0))
    return pl.pallas_call(
        body, name="mem_fwd", grid=(t // tq,),
        in_specs=[pl.BlockSpec((tq, MEM_W), lambda i: (i, C_MQ // MEM_W)),
                  pl.BlockSpec((tq, MEM_W), lambda i: (i, C_MZ // MEM_W)),
                  pl.BlockSpec((m_len, 2 * MEM_W), lambda i: (0, 0))],
        out_specs=[out, out],
        out_shape=[jax.ShapeDtypeStruct((t, MEM_W), F32), jax.ShapeDtypeStruct((t, MEM_W), BF16)],
        compiler_params=_cp(),
    )(proj, proj, mkv)


def _mem_bwd(proj, mkv, o, dog):
    t = proj.shape[0]
    tq = _pick(t, (512, 256))
    m_len = mkv.shape[0]
    scale = 1.0 / math.sqrt(MEM_DH)

    def body(q_ref, z_ref, kv_ref, o_ref, d_ref, dq_ref, dz_ref, dkv_ref):
        @pl.when(pl.program_id(0) == 0)
        def _():
            dkv_ref[...] = jnp.zeros_like(dkv_ref)

        q = q_ref[...]
        z = z_ref[...]
        sg = _sigmoid(z)
        dgo = d_ref[...]
        d_o = dgo * (z * sg)
        dz_ref[...] = (dgo * o_ref[...] * (sg * (1.0 + z * (1.0 - sg)))).astype(BF16)
        mk = kv_ref[:, :MEM_W].astype(BF16)
        mv = kv_ref[:, MEM_W:].astype(BF16)
        lane = lax.broadcasted_iota(jnp.int32, q.shape, 1) >> 6
        klane = lax.broadcasted_iota(jnp.int32, (m_len, MEM_W), 1) >> 6
        dq = jnp.zeros(q.shape, F32)
        dmk = jnp.zeros((m_len, MEM_W), F32)
        dmv = jnp.zeros((m_len, MEM_W), F32)
        for h in range(MEM_HEADS):
            qh = jnp.where(lane == h, q, 0.0)
            doh = jnp.where(lane == h, d_o, 0.0)
            s = _bdot(qh, mk, "nt") * scale
            p = jnp.exp(s - jnp.max(s, axis=-1, keepdims=True))
            p = p / jnp.sum(p, axis=-1, keepdims=True)
            dp = _bdot(doh, mv, "nt")
            ds = p * (dp - jnp.sum(dp * p, axis=-1, keepdims=True)) * scale
            dq = dq + jnp.where(lane == h, _bdot(ds, mk, "nn"), 0.0)
            dmk = dmk + jnp.where(klane == h, _bdot(ds, qh, "tn"), 0.0)
            dmv = dmv + jnp.where(klane == h, _bdot(p, doh, "tn"), 0.0)
        dq_ref[...] = dq.astype(BF16)
        dkv_ref[:, :MEM_W] += dmk
        dkv_ref[:, MEM_W:] += dmv

    blk = pl.BlockSpec((tq, MEM_W), lambda i: (i, 0))
    kv = pl.BlockSpec((m_len, 2 * MEM_W), lambda i: (0, 0))
    return pl.pallas_call(
        body, name="mem_bwd", grid=(t // tq,),
        in_specs=[pl.BlockSpec((tq, MEM_W), lambda i: (i, C_MQ // MEM_W)),
                  pl.BlockSpec((tq, MEM_W), lambda i: (i, C_MZ // MEM_W)), kv, blk, blk],
        out_specs=[blk, blk, kv],
        out_shape=[jax.ShapeDtypeStruct((t, MEM_W), BF16), jax.ShapeDtypeStruct((t, MEM_W), BF16),
                   jax.ShapeDtypeStruct((m_len, 2 * MEM_W), F32)], compiler_params=_cp(),
    )(proj, proj, mkv, o, dog)


_GW = 512


def _merge_fwd(proj, y_dn, y_sb, y_m):
    t = proj.shape[0]
    tb = _pick(t, (256,))
    nc = D_MODEL // _GW

    def body(g1, g2, g3, y1, y2, y3, out_ref):
        out_ref[...] = (_sigmoid(g1[...]) * y1[...] + _sigmoid(g2[...]) * y2[...] + _sigmoid(g3[...]) * y3[...]).astype(BF16)

    def gate(kb):
        return pl.BlockSpec((tb, _GW), lambda i, c: (i, C_GATES // _GW + kb * nc + c))

    blk = pl.BlockSpec((tb, _GW), lambda i, c: (i, c))
    return pl.pallas_call(
        body, name="merge_fwd", grid=(t // tb, nc), in_specs=[gate(0), gate(1), gate(2), blk, blk, blk],
        out_specs=blk, out_shape=jax.ShapeDtypeStruct((t, D_MODEL), BF16), compiler_params=_cp(),
    )(proj, proj, proj, y_dn, y_sb, y_m)


def _merge_bwd(proj, y_dn, y_sb, y_m, dm):
    t = proj.shape[0]
    tb = _pick(t, (256,))
    nc = D_MODEL // _GW

    def body(g1, g2, g3, y1, y2, y3, dm_ref, d1, d2, d3, dg1, dg2, dg3):
        d = dm_ref[...]
        for g, y, dy, dg in ((g1, y1, d1, dg1), (g2, y2, d2, dg2), (g3, y3, d3, dg3)):
            s = _sigmoid(g[...])
            dy[...] = (d * s).astype(BF16)
            dg[...] = (d * y[...] * (s * (1.0 - s))).astype(BF16)

    def gate(kb):
        return pl.BlockSpec((tb, _GW), lambda i, c: (i, C_GATES // _GW + kb * nc + c))

    blk = pl.BlockSpec((tb, _GW), lambda i, c: (i, c))
    act = jax.ShapeDtypeStruct((t, D_MODEL), BF16)
    return pl.pallas_call(
        body, name="merge_bwd", grid=(t // tb, nc), in_specs=[gate(0), gate(1), gate(2), blk, blk, blk, blk],
        out_specs=[blk] * 6, out_shape=[act] * 6, compiler_params=_cp(),
    )(proj, proj, proj, y_dn, y_sb, y_m, dm)


def _final_loss(x, mo, g, tgt):
    t, d = x.shape
    tb = _pick(t, (256,))

    def body(x_ref, mo_ref, g_ref, t_ref, do_ref, dob_ref, loss_ref, dg_ref):
        @pl.when(pl.program_id(0) == 0)
        def _():
            loss_ref[...] = jnp.zeros_like(loss_ref)
            dg_ref[...] = jnp.zeros_like(dg_ref)

        out = x_ref[...] + mo_ref[...]
        r = lax.rsqrt(jnp.mean(out * out, axis=-1, keepdims=True) + NORM_EPS)
        xhat = out * r
        gv = g_ref[...]
        err = xhat * gv - t_ref[...]
        per_tok = jnp.mean(err * err, axis=-1, keepdims=True)
        loss_ref[...] += 0.5 * jnp.sum(per_tok, axis=0, keepdims=True)
        dy = err * (1.0 / d)
        dg_ref[...] += jnp.sum(dy * xhat, axis=0, keepdims=True)
        dxh = dy * gv
        dout = r * (dxh - xhat * jnp.mean(dxh * xhat, axis=-1, keepdims=True))
        do_ref[...] = dout
        dob_ref[...] = dout.astype(BF16)

    row = pl.BlockSpec((tb, d), lambda i: (i, 0))
    vec = pl.BlockSpec((1, d), lambda i: (0, 0))
    return pl.pallas_call(
        body, name="final_loss", grid=(t // tb,), in_specs=[row, row, vec, row],
        out_specs=[row, row, pl.BlockSpec((1, 128), lambda i: (0, 0)), vec],
        out_shape=[jax.ShapeDtypeStruct((t, d), F32), jax.ShapeDtypeStruct((t, d), BF16),
                   jax.ShapeDtypeStruct((1, 128), F32), jax.ShapeDtypeStruct((1, d), F32)],
        compiler_params=_cp(),
    )(x, mo, g, tgt)


def _cast_bf16(a, name):
    r, c = a.shape
    tb = _pick(r, (128, 496))

    def body(a_ref, o_ref):
        o_ref[...] = a_ref[...].astype(BF16)

    blk = pl.BlockSpec((tb, c), lambda i: (i, 0))
    return pl.pallas_call(body, name=name, grid=(r // tb,), in_specs=[blk], out_specs=blk,
                          out_shape=jax.ShapeDtypeStruct((r, c), BF16), compiler_params=_cp())(a)


WIN_START = (0, 23, 45, 68)
_S1_LO, _S1_HI = 1148, 1164
_S1_BA_POS = SHARD_PAD - 128


def _to_window(x, s):
    if s == 0:
        return x
    if s in (2, 3):
        return pltpu.roll(x, 120 if s == 2 else 124, 1)
    pos = lax.broadcasted_iota(jnp.int32, x.shape, 1)
    head = pltpu.roll(x, 4, 1)
    tail = pltpu.roll(x, SHARD_PAD - 12, 1)
    ba = jnp.where(pos < _S1_BA_POS + (_S1_HI - _S1_LO), pltpu.roll(x, _S1_BA_POS - _S1_LO, 1), 0.0)
    return jnp.where(pos < _S1_LO + 4, head, jnp.where(pos < _S1_BA_POS, tail, ba))


def _from_window(g, s):
    if s == 0:
        return g
    if s in (2, 3):
        return pltpu.roll(g, SHARD_PAD - (120 if s == 2 else 124), 1)
    col = lax.broadcasted_iota(jnp.int32, g.shape, 1)
    head = pltpu.roll(g, SHARD_PAD - 4, 1)
    tail = pltpu.roll(g, 12, 1)
    ba = pltpu.roll(g, SHARD_PAD - (_S1_BA_POS - _S1_LO), 1)
    return jnp.where(col < _S1_LO, head, jnp.where(col < _S1_HI, ba, tail))


def _cast_to_window(w, shard, name):
    r, c = w.shape
    tb = _pick(r, (128,))

    def body(s_ref, w_ref, o_ref, pad_scr):
        pad_scr[...] = jnp.zeros_like(pad_scr)
        pad_scr[:, :c] = w_ref[...]
        x = pad_scr[...]
        for s in range(N_SHARD):
            @pl.when(s_ref[0] == s)
            def _():
                o_ref[...] = _to_window(x, s).astype(BF16)

    return pl.pallas_call(
        body, name=name,
        grid_spec=pltpu.PrefetchScalarGridSpec(
            num_scalar_prefetch=1, grid=(r // tb,),
            in_specs=[pl.BlockSpec((tb, c), lambda i, s: (i, 0))],
            out_specs=pl.BlockSpec((tb, SHARD_PAD), lambda i, s: (i, 0)),
            scratch_shapes=[pltpu.VMEM((tb, SHARD_PAD), F32)]),
        out_shape=jax.ShapeDtypeStruct((r, SHARD_PAD), BF16), compiler_params=_cp(),
    )(shard, w)


def _pair_add(g, recv, c_idx, name):
    n, r, c = g.shape
    half = r // 2
    tb = _pick(half, (128, 248))
    nb = half // tb

    def body(c_ref, g_ref, r_ref, o_ref):
        o_ref[...] = (g_ref[...].astype(F32) + r_ref[...].astype(F32)).astype(BF16)

    blk = pl.BlockSpec((n, tb, c), lambda i, c_ref: (0, i, 0))
    return pl.pallas_call(
        body, name=name,
        grid_spec=pltpu.PrefetchScalarGridSpec(
            num_scalar_prefetch=1, grid=(nb,),
            in_specs=[pl.BlockSpec((n, tb, c), lambda i, c_ref: (0, c_ref[0] * nb + i, 0)), blk], out_specs=blk),
        out_shape=jax.ShapeDtypeStruct((n, half, c), BF16), compiler_params=_cp(),
    )(c_idx, g, recv)


def _chip_sum(parts, by_chip, place, name):
    n, h, c = parts.shape
    tb = _pick(h, (128, 248))
    nb = h // tb

    def body(p_ref, mine_ref, *rest):
        others, o_ref = rest[:n], rest[n]
        me = jnp.zeros((tb, c), jnp.int32) + p_ref[0]
        acc = None
        for q in range(n):
            term = jnp.where(me == q, mine_ref[...], others[q][...]).astype(F32)
            acc = term if acc is None else acc + term
        o_ref[...] = acc

    def other(q):
        return pl.BlockSpec((None, tb, c), lambda i, p: (jnp.where(p[0] == q, (q + 1) % n, q), i, 0))

    return pl.pallas_call(
        body, name=name,
        grid_spec=pltpu.PrefetchScalarGridSpec(
            num_scalar_prefetch=1, grid=(nb,),
            in_specs=[pl.BlockSpec((None, tb, c), lambda i, p: (p[0], i, 0))] + [other(q) for q in range(n)],
            out_specs=pl.BlockSpec((tb, c), lambda i, p: (p[1] * nb + i, 0))),
        out_shape=jax.ShapeDtypeStruct((2 * h, c), F32), compiler_params=_cp(),
    )(place, parts, *([by_chip] * n))


def _adamw_math(w, g, m, v):
    m = ADAM_B1 * m + (1.0 - ADAM_B1) * g
    v = ADAM_B2 * v + (1.0 - ADAM_B2) * (g * g)
    m_hat = m / (1.0 - ADAM_B1 ** ADAM_STEP)
    v_hat = v / (1.0 - ADAM_B2 ** ADAM_STEP)
    delta = -ADAM_LR * (m_hat / (jnp.sqrt(v_hat) + ADAM_EPS) + ADAM_WD * w)
    return delta, m, v


def _adamw(w, g, m, v, name):
    r, c = w.shape
    tb = _pick(r, (128, 496))

    def body(w_ref, g_ref, m_ref, v_ref, go_ref, d_ref, mo_ref, vo_ref):
        gv = g_ref[...]
        d, mn, vn = _adamw_math(w_ref[...], gv, m_ref[...], v_ref[...])
        go_ref[...] = gv
        d_ref[...] = d
        mo_ref[...] = mn
        vo_ref[...] = vn

    blk = pl.BlockSpec((tb, c), lambda i: (i, 0))
    return pl.pallas_call(
        body, name=name, grid=(r // tb,), in_specs=[blk] * 4, out_specs=[blk] * 4,
        out_shape=[jax.ShapeDtypeStruct((r, c), F32)] * 4, compiler_params=_cp(),
    )(w, g, m, v)


def _adamw_window(w, g_win, m, v, shard, name):
    r, c = w.shape
    tb = _pick(r, (128,))

    def body(s_ref, w_ref, g_ref, m_ref, v_ref, go_ref, d_ref, mo_ref, vo_ref, g_scr):
        gw = g_ref[...]
        for s in range(N_SHARD):
            @pl.when(s_ref[0] == s)
            def _():
                g_scr[...] = _from_window(gw, s)

        gv = g_scr[:, :c]
        d, mn, vn = _adamw_math(w_ref[...], gv, m_ref[...], v_ref[...])
        go_ref[...] = gv
        d_ref[...] = d
        mo_ref[...] = mn
        vo_ref[...] = vn

    blk = pl.BlockSpec((tb, c), lambda i, s: (i, 0))
    return pl.pallas_call(
        body, name=name,
        grid_spec=pltpu.PrefetchScalarGridSpec(
            num_scalar_prefetch=1, grid=(r // tb,),
            in_specs=[blk, pl.BlockSpec((tb, SHARD_PAD), lambda i, s: (i, 0)), blk, blk], out_specs=[blk] * 4,
            scratch_shapes=[pltpu.VMEM((tb, SHARD_PAD), F32)]),
        out_shape=[jax.ShapeDtypeStruct((r, c), F32)] * 4, compiler_params=_cp(),
    )(shard, w, g_win, m, v)


def _small_update(gathered, w, m, v):
    def body(p_ref, w_ref, m_ref, v_ref, g_ref, d_ref, mo_ref, vo_ref):
        g = p_ref[0]
        for i in range(1, N_DEV):
            g = g + p_ref[i]
        d, mn, vn = _adamw_math(w_ref[...], g, m_ref[...], v_ref[...])
        g_ref[...] = g
        d_ref[...] = d
        mo_ref[...] = mn
        vo_ref[...] = vn

    full = pl.BlockSpec((32, 128), lambda i: (0, 0))
    return pl.pallas_call(
        body, name="small_update", grid=(1,),
        in_specs=[pl.BlockSpec((N_DEV, 32, 128), lambda i: (0, 0, 0)), full, full, full], out_specs=[full] * 4,
        out_shape=[jax.ShapeDtypeStruct((32, 128), F32)] * 4, compiler_params=_cp(),
    )(gathered, w, m, v)


_ANY = pl.BlockSpec(memory_space=pl.ANY)


def _place():
    x, y, c = lax.axis_index("x"), lax.axis_index("y"), lax.axis_index("c")
    chips = [(1 - x, y), (x, 1 - y), (1 - x, 1 - y)]
    return x, y, c, chips


def _gather_shards(arrs):
    n = len(arrs)

    def body(*refs):
        ins, outs = refs[:n], refs[n:2 * n]
        send_sems, recv_sems, local_sems = refs[2 * n:2 * n + 3]
        bufs = refs[2 * n + 3:]
        x, y, c, chips = _place()
        me = 2 * x + y
        sibling = (x, y, 1 - c)
        sends = []
        for a in range(n):
            half = ins[a].shape[0] // 2
            mine = pl.ds(pl.multiple_of(c * half, 16), half)
            for j, (qx, qy) in enumerate(chips):
                cp = pltpu.make_async_remote_copy(
                    src_ref=ins[a].at[mine], dst_ref=outs[a].at[me, mine],
                    send_sem=send_sems.at[6 * a + j], recv_sem=recv_sems.at[6 * a + j],
                    device_id=(qx, qy, c), device_id_type=MESH)
                cp.start()
                sends.append(cp)
        for a in range(n):
            step = bufs[a].shape[0]
            for r0 in range(0, ins[a].shape[0], step):
                rows = pl.ds(r0, step)
                load = pltpu.make_async_copy(ins[a].at[rows], bufs[a], local_sems.at[2 * a])
                load.start()
                load.wait()
                store = pltpu.make_async_copy(bufs[a], outs[a].at[me, rows], local_sems.at[2 * a + 1])
                store.start()
                store.wait()
        for a in range(n):
            half = ins[a].shape[0] // 2
            mine = pl.ds(pl.multiple_of(c * half, 16), half)
            for j, (qx, qy) in enumerate(chips):
                q = 2 * qx + qy
                landed = outs[a].at[q, mine]
                pltpu.make_async_remote_copy(
                    src_ref=landed, dst_ref=landed, send_sem=send_sems.at[6 * a + j], recv_sem=recv_sems.at[6 * a + j],
                    device_id=(qx, qy, c), device_id_type=MESH).wait_recv()
                fw = pltpu.make_async_remote_copy(
                    src_ref=landed, dst_ref=landed, send_sem=send_sems.at[6 * a + 3 + j],
                    recv_sem=recv_sems.at[6 * a + 3 + j], device_id=sibling, device_id_type=MESH)
                fw.start()
                sends.append(fw)
        for a in range(n):
            half = ins[a].shape[0] // 2
            theirs = pl.ds(pl.multiple_of((1 - c) * half, 16), half)
            for j, (qx, qy) in enumerate(chips):
                q = 2 * qx + qy
                dst = outs[a].at[q, theirs]
                pltpu.make_async_remote_copy(
                    src_ref=dst, dst_ref=dst, send_sem=send_sems.at[6 * a + 3 + j], recv_sem=recv_sems.at[6 * a + 3 + j],
                    device_id=sibling, device_id_type=MESH).wait_recv()
        for cp in sends:
            cp.wait_send()

    return pl.pallas_call(
        body, name="gather_shards", in_specs=[_ANY] * n, out_specs=[_ANY] * n,
        out_shape=[jax.ShapeDtypeStruct((N_SHARD,) + a.shape, a.dtype) for a in arrs],
        scratch_shapes=[pltpu.SemaphoreType.DMA((6 * n,)), pltpu.SemaphoreType.DMA((6 * n,)),
                        pltpu.SemaphoreType.DMA((2 * n,))]
        + [pltpu.VMEM((_pick(a.shape[0], (256, 496)), a.shape[1]), a.dtype) for a in arrs],
        compiler_params=pltpu.CompilerParams(has_side_effects=True, vmem_limit_bytes=VMEM_LIMIT),
    )(*arrs)


def _pair_reduce_send(grads):
    n = len(grads)

    def body(*refs):
        ins, outs = refs[:n], refs[n:2 * n]
        send_sems, recv_sems = refs[2 * n:]
        x, y, c, _ = _place()
        sibling = (x, y, 1 - c)
        cps = []
        for a in range(n):
            half = ins[a].shape[1] // 2
            theirs = pl.ds(pl.multiple_of((1 - c) * half, 8), half)
            cp = pltpu.make_async_remote_copy(
                src_ref=ins[a].at[:, theirs], dst_ref=outs[a], send_sem=send_sems.at[a], recv_sem=recv_sems.at[a],
                device_id=sibling, device_id_type=MESH)
            cp.start()
            cps.append(cp)
        for cp in cps:
            cp.wait()

    return pl.pallas_call(
        body, name="pair_reduce_send", in_specs=[_ANY] * n, out_specs=[_ANY] * n,
        out_shape=[jax.ShapeDtypeStruct((g.shape[0], g.shape[1] // 2, g.shape[2]), g.dtype) for g in grads],
        scratch_shapes=[pltpu.SemaphoreType.DMA((n,)), pltpu.SemaphoreType.DMA((n,))],
        compiler_params=pltpu.CompilerParams(has_side_effects=True),
    )(*grads)


def _chip_exchange(parts):
    n = len(parts)

    def body(*refs):
        ins, outs = refs[:n], refs[n:2 * n]
        send_sems, recv_sems = refs[2 * n:]
        x, y, c, chips = _place()
        me = 2 * x + y
        cps = []
        for a in range(n):
            for j, (qx, qy) in enumerate(chips):
                q = 2 * qx + qy
                cp = pltpu.make_async_remote_copy(
                    src_ref=ins[a].at[q], dst_ref=outs[a].at[me], send_sem=send_sems.at[3 * a + j],
                    recv_sem=recv_sems.at[3 * a + j], device_id=(qx, qy, c), device_id_type=MESH)
                cp.start()
                cps.append(cp)
        for a in range(n):
            for j, (qx, qy) in enumerate(chips):
                q = 2 * qx + qy
                dst = outs[a].at[q]
                pltpu.make_async_remote_copy(
                    src_ref=dst, dst_ref=dst, send_sem=send_sems.at[3 * a + j], recv_sem=recv_sems.at[3 * a + j],
                    device_id=(qx, qy, c), device_id_type=MESH).wait_recv()
        for cp in cps:
            cp.wait_send()

    return pl.pallas_call(
        body, name="chip_exchange", in_specs=[_ANY] * n, out_specs=[_ANY] * n,
        out_shape=[jax.ShapeDtypeStruct(p.shape, p.dtype) for p in parts],
        scratch_shapes=[pltpu.SemaphoreType.DMA((3 * n,)), pltpu.SemaphoreType.DMA((3 * n,))],
        compiler_params=pltpu.CompilerParams(has_side_effects=True),
    )(*parts)


def _pair_allgather(fulls):
    n = len(fulls)

    def body(*refs):
        outs = refs[n:2 * n]
        send_sems, recv_sems = refs[2 * n:]
        x, y, c, _ = _place()
        sibling = (x, y, 1 - c)
        cps = []
        for a in range(n):
            half = outs[a].shape[0] // 2
            mine = outs[a].at[pl.ds(pl.multiple_of(c * half, 8), half)]
            cp = pltpu.make_async_remote_copy(
                src_ref=mine, dst_ref=mine, send_sem=send_sems.at[a], recv_sem=recv_sems.at[a],
                device_id=sibling, device_id_type=MESH)
            cp.start()
            cps.append(cp)
        for a in range(n):
            half = outs[a].shape[0] // 2
            theirs = outs[a].at[pl.ds(pl.multiple_of((1 - c) * half, 8), half)]
            pltpu.make_async_remote_copy(
                src_ref=theirs, dst_ref=theirs, send_sem=send_sems.at[a], recv_sem=recv_sems.at[a],
                device_id=sibling, device_id_type=MESH).wait_recv()
        for cp in cps:
            cp.wait_send()

    return pl.pallas_call(
        body, name="pair_allgather", in_specs=[_ANY] * n, out_specs=[_ANY] * n,
        out_shape=[jax.ShapeDtypeStruct(f.shape, f.dtype) for f in fulls],
        input_output_aliases={a: a for a in range(n)},
        scratch_shapes=[pltpu.SemaphoreType.DMA((n,)), pltpu.SemaphoreType.DMA((n,))],
        compiler_params=pltpu.CompilerParams(has_side_effects=True),
    )(*fulls)


def _allgather_small(slab):
    def body(s_ref, out_ref, send_sems, recv_sems):
        x, y, c, _ = _place()
        me = 4 * x + 2 * y + c
        out_ref[me] = s_ref[...]
        cps = []
        for mask in range(1, N_DEV):
            peer = (x ^ (mask >> 2), y ^ ((mask >> 1) & 1), c ^ (mask & 1))
            cp = pltpu.make_async_remote_copy(
                src_ref=s_ref, dst_ref=out_ref.at[me], send_sem=send_sems.at[mask - 1], recv_sem=recv_sems.at[mask - 1],
                device_id=peer, device_id_type=MESH)
            cp.start()
            cps.append(cp)
        for mask in range(1, N_DEV):
            peer = (x ^ (mask >> 2), y ^ ((mask >> 1) & 1), c ^ (mask & 1))
            dst = out_ref.at[4 * peer[0] + 2 * peer[1] + peer[2]]
            pltpu.make_async_remote_copy(
                src_ref=dst, dst_ref=dst, send_sem=send_sems.at[mask - 1], recv_sem=recv_sems.at[mask - 1],
                device_id=peer, device_id_type=MESH).wait_recv()
        for cp in cps:
            cp.wait_send()

    vm = pl.BlockSpec(memory_space=pltpu.VMEM)
    return pl.pallas_call(
        body, name="allgather_small", in_specs=[vm], out_specs=vm,
        out_shape=jax.ShapeDtypeStruct((N_DEV,) + slab.shape, slab.dtype),
        scratch_shapes=[pltpu.SemaphoreType.DMA((N_DEV - 1,)), pltpu.SemaphoreType.DMA((N_DEV - 1,))],
        compiler_params=pltpu.CompilerParams(has_side_effects=True),
    )(slab)


def _pack_b(w_mem_kv, w_br_dn, w_br_sb, w_br_mem, w_out, conv_w):
    conv = jnp.pad(conv_w.reshape(3, D_MODEL), ((0, B_ROWS - B_CONV - 3), (0, 0)))
    return jnp.concatenate([w_mem_kv.reshape(128, D_MODEL), w_br_dn, w_br_sb, w_br_mem.reshape(64, D_MODEL), w_out,
                            conv], axis=0)


def _unpack_b(slab):
    return (slab[B_MEMKV:B_BRDN].reshape(1, 256, 512), slab[B_BRDN:B_BRSB].reshape(1, 256, D_MODEL),
            slab[B_BRSB:B_BRMEM].reshape(1, 256, D_MODEL), slab[B_BRMEM:B_OUT].reshape(1, 256, 256),
            slab[B_OUT:B_CONV].reshape(1, 256, D_MODEL), slab[B_CONV:B_CONV + 3].reshape(1, 4, 768))


def _pack_small(norm_g, mem_norm_g, final_g, dn_norm_g, a_log, dt_bias, loss=None):
    slab = jnp.zeros((32, 128), F32)
    slab = slab.at[S_NORM:S_NORM + 8].set(norm_g.reshape(8, 128))
    slab = slab.at[S_MEMNORM:S_MEMNORM + 8].set(mem_norm_g.reshape(8, 128))
    slab = slab.at[S_FINAL:S_FINAL + 8].set(final_g.reshape(8, 128))
    slab = slab.at[S_DNNORM].set(dn_norm_g.reshape(128))
    slab = slab.at[S_ALOG, :N_HEADS].set(a_log.reshape(N_HEADS))
    slab = slab.at[S_DTB, :N_HEADS].set(dt_bias.reshape(N_HEADS))
    if loss is not None:
        slab = slab.at[S_LOSS, 0].set(loss)
    return slab


def _unpack_small(slab):
    return (slab[S_NORM:S_NORM + 8].reshape(1, D_MODEL), slab[S_MEMNORM:S_MEMNORM + 8].reshape(1, D_MODEL),
            slab[S_FINAL:S_FINAL + 8].reshape(D_MODEL), slab[S_DNNORM].reshape(1, 128),
            slab[S_ALOG, :N_HEADS].reshape(1, N_HEADS), slab[S_DTB, :N_HEADS].reshape(1, N_HEADS))


def _reorder_w_in(w_full):
    pad = jnp.zeros((w_full.shape[0], W_R - IN_WIDTH), w_full.dtype)
    return jnp.concatenate([w_full[:, :4096], w_full[:, 4112:], w_full[:, 4096:4112], pad], axis=1)


def _windows_to_w_r(win):
    b = 128
    s0, s1, s2, s3 = win[0], win[1], win[2], win[3]
    e1, e2, e3 = WIN_START[1] * b, WIN_START[2] * b, WIN_START[3] * b
    n1, n2 = e2 - e1, e3 - e2
    return jnp.concatenate([
        s0[:, :e1], s0[:, e1:e1 + b] + s1[:, :b],
        s1[:, b:n1], s1[:, n1:n1 + b] + s2[:, :b],
        s2[:, b:n2], s2[:, n2:n2 + b] + s3[:, :b],
        s3[:, b:], s1[:, _S1_BA_POS:]], axis=1)


def _dproj_windows(dproj_r):
    b = 128
    pieces = []
    for s in range(N_SHARD):
        lo = WIN_START[s] * b
        if s == 1:
            pieces += [dproj_r[:, lo:lo + _S1_BA_POS], dproj_r[:, C_BA:C_BA + b]]
        else:
            pieces.append(dproj_r[:, lo:lo + SHARD_PAD])
    return jnp.concatenate(pieces, axis=1)


def _local_step(x, mem, tgt, norm_g, mem_norm_g, w_r, w_sh, conv_w, a_log, dt_bias, dn_norm_g, w_mem_kv, w_br_dn,
                w_br_sb, w_br_mem, w_out, final_g):
    t = x.shape[0]
    final_row = final_g.reshape(1, D_MODEL)
    alog_row = jnp.zeros((1, 128), F32).at[0, N_HEADS:2 * N_HEADS].set(a_log.reshape(N_HEADS))
    dtb_row = jnp.zeros((1, 128), F32).at[0, N_HEADS:2 * N_HEADS].set(dt_bias.reshape(N_HEADS))

    h = _rmsnorm_fwd(x, norm_g, "norm_fwd")
    proj = _mm(h, w_r, "nn", "in_proj")
    qkv = _dn_prep_fwd(proj, conv_w)
    beta_t, g_t = _dn_gate_fwd(proj, alog_row, dtb_row)
    dn_u, dn_w, dn_qd, dn_kd, dn_a, tinv_all, dn_el = _dn_intra_fwd(qkv, beta_t, g_t)
    o_dn, dn_vn, s_all = _dn_scan_fwd(dn_u, dn_w, dn_qd, dn_kd, dn_a, dn_el)
    o_dn_g = _dn_post_fwd(o_dn, proj, dn_norm_g)
    o_sb, o_sb_g, sb_l = _sb_fwd(proj)
    mem_n = _rmsnorm_fwd(mem, mem_norm_g, "mem_norm_fwd")
    mkv = _mm(mem_n, w_mem_kv, "nn", "mem_kv")
    o_m, o_m_g = _mem_fwd(proj, mkv)
    y_dn = _mm(o_dn_g, w_br_dn, "nn", "br_dn")
    y_sb = _mm(o_sb_g, w_br_sb, "nn", "br_sb")
    y_m = _mm(o_m_g, w_br_mem, "nn", "br_mem")
    merged = _merge_fwd(proj, y_dn, y_sb, y_m)
    mo = _mm(merged, w_out, "nn", "out_proj")
    d_out, d_out_b, loss_row, g_final = _final_loss(x, mo, final_row, tgt)

    g_w_out = _mm(merged, d_out_b, "tn", "g_w_out")
    d_merged = _mm(d_out_b, w_out, "nt", "d_merged")
    dy_dn, dy_sb, dy_m, dg1, dg2, dg3 = _merge_bwd(proj, y_dn, y_sb, y_m, d_merged)
    g_w_br_dn = _mm(o_dn_g, dy_dn, "tn", "g_w_br_dn")
    g_w_br_sb = _mm(o_sb_g, dy_sb, "tn", "g_w_br_sb")
    g_w_br_mem = _mm(o_m_g, dy_m, "tn", "g_w_br_mem")
    d_o_dn_g = _mm(dy_dn, w_br_dn, "nt", "d_o_dn")
    d_o_sb_g = _mm(dy_sb, w_br_sb, "nt", "d_o_sb")
    d_o_m_g = _mm(dy_m, w_br_mem, "nt", "d_o_mem")

    d_mq, d_mz, d_mkv = _mem_bwd(proj, mkv, o_m, d_o_m_g)
    d_mkv_b = _cast_bf16(d_mkv, "cast_dmkv")
    g_w_mem_kv = _mm(mem_n, d_mkv_b, "tn", "g_w_mem_kv")
    d_mem_n = _mm(d_mkv_b, w_mem_kv, "nt", "d_mem_n")
    _, g_mem_norm = _rmsnorm_bwd(mem, mem_norm_g, d_mem_n, jnp.zeros_like(mem), "mem_norm_bwd")

    d_sq, d_sk, d_sv, d_sz = _sb_bwd(proj, o_sb, sb_l, d_o_sb_g)

    d_o_dn, d_dnz, g_dn_norm = _dn_post_bwd(o_dn, proj, dn_norm_g, d_o_dn_g)
    d_vnew, d_kd, d_qd, d_w, d_el = _dn_scan_bwd(dn_w, dn_qd, dn_kd, dn_a, dn_el, dn_vn, s_all, d_o_dn)
    d_qn, d_kn, d_vn, dbeta_t, dg_t = _dn_intra_bwd(qkv, beta_t, g_t, tinv_all, dn_vn, d_o_dn, d_vnew, d_kd, d_qd, d_w, d_el)
    d_conv_in, g_conv = _dn_prep_bwd(proj, conv_w, d_qn, d_kn, d_vn)
    d_ba, g_alog_row, g_dtb_row = _dn_gate_bwd(proj, alog_row, dtb_row, dbeta_t, dg_t)

    dproj_sh = _dproj_windows(
        jnp.concatenate([d_conv_in, d_dnz, d_sq, d_sk, d_sv, d_sz, d_mq, d_mz, dg1, dg2, dg3, d_ba], axis=1))
    g_w_sh = _mm(h, dproj_sh, "tn", "g_w_in", out_dtype=BF16, out_shards=N_SHARD)
    dh = _mm(dproj_sh, w_sh, "nt", "d_h")
    grad_x, g_norm = _rmsnorm_bwd(x, norm_g, dh, d_out, "norm_bwd")

    small = dict(norm_g=g_norm, mem_norm_g=g_mem_norm, final_g=g_final, dn_norm_g=g_dn_norm,
                 a_log=g_alog_row[:, N_HEADS:2 * N_HEADS], dt_bias=g_dtb_row[:, N_HEADS:2 * N_HEADS])
    big = dict(w_sh=g_w_sh, conv_w=g_conv, w_mem_kv=g_w_mem_kv, w_br_dn=g_w_br_dn, w_br_sb=g_w_br_sb,
               w_br_mem=g_w_br_mem, w_out=g_w_out)
    return loss_row[0, 0], grad_x, small, big


def _reduce_scatter(grads):
    x, y, c = lax.axis_index("x"), lax.axis_index("y"), lax.axis_index("c")
    core = jnp.reshape(c, (1,)).astype(jnp.int32)
    place = jnp.stack([2 * x + y, c]).astype(jnp.int32)
    recv = _pair_reduce_send(grads)
    parts = [_pair_add(g, r, core, "pair_add") for g, r in zip(grads, recv)]
    by_chip = _chip_exchange(parts)
    fulls = [_chip_sum(p, b, place, "chip_sum") for p, b in zip(parts, by_chip)]
    return _pair_allgather(fulls)


def kernel(x, mem, norm_g, mem_norm_g, w_in, conv_w, a_log, dt_bias, dn_norm_g, w_mem_kv, w_br_dn, w_br_sb, w_br_mem, w_out, final_g, loss_target, m_norm_g, m_mem_norm_g, m_w_in, m_conv_w, m_a_log, m_dt_bias, m_dn_norm_g, m_w_mem_kv, m_w_br_dn, m_w_br_sb, m_w_br_mem, m_w_out, m_final_g, v_norm_g, v_mem_norm_g, v_w_in, v_conv_w, v_a_log, v_dt_bias, v_dn_norm_g, v_w_mem_kv, v_w_br_dn, v_w_br_sb, v_w_br_mem, v_w_out, v_final_g):
    w_a = w_in[0]
    w_b = _pack_b(w_mem_kv[0], w_br_dn[0], w_br_sb[0], w_br_mem[0], w_out[0], conv_w[0])
    m_b = _pack_b(m_w_mem_kv[0], m_w_br_dn[0], m_w_br_sb[0], m_w_br_mem[0], m_w_out[0], m_conv_w[0])
    v_b = _pack_b(v_w_mem_kv[0], v_w_br_dn[0], v_w_br_sb[0], v_w_br_mem[0], v_w_out[0], v_conv_w[0])

    shard = jnp.reshape(2 * lax.axis_index("x") + lax.axis_index("y"), (1,)).astype(jnp.int32)
    ga, gb = _gather_shards([_cast_to_window(w_a, shard, "cast_w_in"), _cast_bf16(w_b, "cast_w_b")])
    w_r = _windows_to_w_r(ga)
    f_mem_kv = gb[:, B_MEMKV:B_BRDN].reshape(N_SHARD * 256, 512)
    f_br_dn = gb[:, B_BRDN:B_BRSB].reshape(N_SHARD * 256, D_MODEL)
    f_br_sb = gb[:, B_BRSB:B_BRMEM].reshape(N_SHARD * 256, D_MODEL)
    f_br_mem = gb[:, B_BRMEM:B_OUT].reshape(N_SHARD, 256, 256).transpose(1, 0, 2).reshape(256, D_MODEL)
    f_out = gb[:, B_OUT:B_CONV].reshape(N_SHARD * 256, D_MODEL)
    f_conv = gb[:, B_CONV:B_CONV + 3].reshape(N_SHARD, 4, 768).transpose(1, 0, 2).reshape(4, 3 * D_MODEL).astype(F32)

    loss, grad_x, small, big = _local_step(
        x[0], mem[0], loss_target[0], norm_g, mem_norm_g, w_r, ga, f_conv, a_log, dt_bias, dn_norm_g,
        f_mem_kv, f_br_dn, f_br_sb, f_br_mem, f_out, final_g)

    g_b = jnp.stack([
        _pack_b(big["w_mem_kv"][256 * s:256 * (s + 1)], big["w_br_dn"][256 * s:256 * (s + 1)],
                big["w_br_sb"][256 * s:256 * (s + 1)], big["w_br_mem"][:, 256 * s:256 * (s + 1)],
                big["w_out"][256 * s:256 * (s + 1)], big["conv_w"][:, 768 * s:768 * (s + 1)])
        for s in range(N_SHARD)]).astype(BF16)
    gs_in, gs_b = _reduce_scatter([big["w_sh"], g_b])

    gr_in, d_in, nm_in, nv_in = _adamw_window(w_a, gs_in, m_w_in[0], v_w_in[0], shard, "adamw_w_in")
    gr_b, d_b, nm_b, nv_b = _adamw(w_b, gs_b, m_b, v_b, "adamw_b")

    part = _pack_small(small["norm_g"], small["mem_norm_g"], small["final_g"], small["dn_norm_g"],
                       small["a_log"], small["dt_bias"], loss)
    w_s = _pack_small(norm_g, mem_norm_g, final_g, dn_norm_g, a_log, dt_bias)
    m_s = _pack_small(m_norm_g, m_mem_norm_g, m_final_g, m_dn_norm_g, m_a_log, m_dt_bias)
    v_s = _pack_small(v_norm_g, v_mem_norm_g, v_final_g, v_dn_norm_g, v_a_log, v_dt_bias)
    g_s, d_s, nm_s, nv_s = _small_update(_allgather_small(part), w_s, m_s, v_s)

    def assemble(slab_small, a_in, slab_b):
        s_norm, s_memnorm, s_final, s_dnnorm, s_alog, s_dtb = _unpack_small(slab_small)
        b_memkv, b_brdn, b_brsb, b_brmem, b_out, b_conv = _unpack_b(slab_b)
        return [s_norm, s_memnorm, a_in.reshape(1, D_MODEL, IN_WIDTH // N_SHARD), b_conv, s_alog, s_dtb, s_dnnorm,
                b_memkv, b_brdn, b_brsb, b_brmem, b_out, s_final]

    outs = [g_s[S_LOSS, 0], grad_x.reshape(1, -1, D_MODEL)]
    outs += assemble(g_s, gr_in, gr_b)
    outs += assemble(d_s, d_in, d_b)
    outs += assemble(nm_s, nm_in, nm_b)
    outs += assemble(nv_s, nv_in, nv_b)
    return tuple(outs)
```

```python
import functools
import math

import jax
import jax.numpy as jnp
from jax import lax
from jax.experimental import pallas as pl
from jax.experimental.pallas import tpu as pltpu

F32 = jnp.float32
BF16 = jnp.bfloat16
MESH = pl.DeviceIdType.MESH
HIGHEST = lax.Precision.HIGHEST

D_MODEL = 1024
N_HEADS = 8
D_HEAD = 128
DN_CHUNK = 64
DN_GROUP = 8
DN_SCAN_GROUP = 4
SB_BLOCK = 256
SB_HEADS_PER_STEP = 2
SB_QBLOCK = 256
MEM_HEADS = 4
MEM_DH = 64
MEM_W = MEM_HEADS * MEM_DH
NORM_EPS = 1e-6
IN_WIDTH = 11792
N_SHARD = 4
SHARD_W = IN_WIDTH // N_SHARD
SHARD_PAD = 3072
N_DEV = 8

C_DNZ = 3072
C_SBQ = 4096
C_SBZ = 7168
C_MQ = 8192
C_MZ = 8448
C_GATES = 8704
C_BA = 11776
W_R = 11904

ADAM_LR = 0.001
ADAM_B1 = 0.9
ADAM_B2 = 0.999
ADAM_EPS = 1e-08
ADAM_WD = 0.01
ADAM_STEP = 10

VMEM_LIMIT = 56 * 1024 * 1024

B_ROWS = 992
B_MEMKV, B_BRDN, B_BRSB, B_BRMEM, B_OUT, B_CONV = 0, 128, 384, 640, 704, 960
S_NORM, S_MEMNORM, S_FINAL, S_DNNORM, S_ALOG, S_DTB, S_LOSS = 0, 8, 16, 24, 25, 26, 27


def _cp(**kw):
    return pltpu.CompilerParams(vmem_limit_bytes=VMEM_LIMIT, **kw)


def _dot(a, b, dims):
    lead = a.ndim - 2
    ca, cb = {"nn": (1, 0), "nt": (1, 1), "tn": (0, 0)}[dims]
    batch = tuple(range(lead))
    return lax.dot_general(a, b, (((ca + lead,), (cb + lead,)), (batch, batch)), preferred_element_type=F32)


def _chunks(x):
    return x.reshape(x.shape[0] // DN_CHUNK, DN_CHUNK, x.shape[1])


def _unchunk(x):
    return x.reshape(x.shape[0] * x.shape[1], x.shape[2])


def _bdot(a, b, dims):
    return _dot(a.astype(BF16), b.astype(BF16), dims)


def _split(a):
    hi = a.astype(BF16)
    return hi, (a - hi.astype(F32)).astype(BF16)


def _dot3(a, b, dims):
    a1, a2 = _split(a)
    b1, b2 = _split(b)
    return _dot(a1, b1, dims) + (_dot(a1, b2, dims) + _dot(a2, b1, dims))


def _split_dot(a, ones_bf16):
    hi, lo = _split(a.reshape(-1, a.shape[-1]))
    out = _dot(hi, ones_bf16, "nn") + _dot(lo, ones_bf16, "nn")
    return out.reshape(a.shape[:-1] + (ones_bf16.shape[1],))


def _sigmoid(x):
    return 1.0 / (1.0 + jnp.exp(-x))


def _log1p_small(u):
    return jnp.where(u < 1e-2, u * (1.0 - u * (0.5 - u * (1.0 / 3.0))), jnp.log(1.0 + u))


def _log_sigmoid(z):
    return jnp.minimum(z, 0.0) - _log1p_small(jnp.exp(-jnp.abs(z)))


def _pick(dim, cands):
    for c in cands:
        if dim % c == 0:
            return c
    return dim


def _mm(a, b, dims, name, out_dtype=F32, out_shards=1, after=None):
    ta, tb = dims[0] == "t", dims[1] == "t"
    m, k = (a.shape[1], a.shape[0]) if ta else a.shape
    b_shards = b.shape[0] if b.ndim == 3 else 1
    n = b.shape[-2] if tb else b.shape[-1]
    tm = _pick(m, (1024, 512, 256))
    tn = _pick(n // out_shards, (512, 384, 256, 128))
    tk = _pick(k // b_shards, (1024, 512, 384, 256))
    nk = k // tk

    def body(a_ref, b_ref, *rest):
        o_ref, acc_ref = rest[-2:]
        kk = pl.program_id(2)

        @pl.when(kk == 0)
        def _():
            acc_ref[...] = jnp.zeros_like(acc_ref)

        acc_ref[...] += _bdot(a_ref[...], b_ref[...], dims)

        @pl.when(kk == nk - 1)
        def _():
            o_ref[...] = acc_ref[...].astype(out_dtype)

    a_spec = pl.BlockSpec((tk, tm), lambda i, j, q: (q, i)) if ta else pl.BlockSpec((tm, tk), lambda i, j, q: (i, q))
    if b_shards > 1:
        per_k = k // b_shards // tk
        b_spec = pl.BlockSpec((None, tn, tk), lambda i, j, q: (q // per_k, j, q % per_k))
    else:
        b_spec = pl.BlockSpec((tn, tk), lambda i, j, q: (j, q)) if tb else pl.BlockSpec((tk, tn), lambda i, j, q: (q, j))
    if out_shards > 1:
        per_n = n // out_shards // tn
        out_spec = pl.BlockSpec((None, tm, tn), lambda i, j, q: (j // per_n, i, j % per_n))
        out_shape = jax.ShapeDtypeStruct((out_shards, m, n // out_shards), out_dtype)
    else:
        out_spec = pl.BlockSpec((tm, tn), lambda i, j, q: (i, j))
        out_shape = jax.ShapeDtypeStruct((m, n), out_dtype)
    extra_specs, extra = [], []
    if after is not None:
        extra_specs, extra = [pl.BlockSpec(after.shape, lambda i, j, q: (0, 0))], [after]
    return pl.pallas_call(
        body, name=name, grid=(m // tm, n // tn, nk),
        in_specs=[a_spec, b_spec] + extra_specs, out_specs=out_spec, out_shape=out_shape,
        scratch_shapes=[pltpu.VMEM((tm, tn), F32)],
        compiler_params=_cp(dimension_semantics=("parallel", "parallel", "arbitrary")),
    )(a, b, *extra)


def _rmsnorm_fwd(x, g, name):
    t, d = x.shape
    tb = _pick(t, (512, 256))

    def body(x_ref, g_ref, h_ref):
        xv = x_ref[...]
        r = lax.rsqrt(jnp.mean(xv * xv, axis=-1, keepdims=True) + NORM_EPS)
        h_ref[...] = ((xv * r) * g_ref[...]).astype(BF16)

    return pl.pallas_call(
        body, name=name, grid=(t // tb,),
        in_specs=[pl.BlockSpec((tb, d), lambda i: (i, 0)), pl.BlockSpec((1, d), lambda i: (0, 0))],
        out_specs=pl.BlockSpec((tb, d), lambda i: (i, 0)),
        out_shape=jax.ShapeDtypeStruct((t, d), BF16), compiler_params=_cp(),
    )(x, g)


def _rmsnorm_bwd(x, g, dh, resid, name):
    t, d = x.shape
    tb = _pick(t, (256,))

    def body(x_ref, g_ref, dh_ref, r_ref, dx_ref, dg_ref):
        @pl.when(pl.program_id(0) == 0)
        def _():
            dg_ref[...] = jnp.zeros_like(dg_ref)

        xv = x_ref[...]
        r = lax.rsqrt(jnp.mean(xv * xv, axis=-1, keepdims=True) + NORM_EPS)
        xhat = xv * r
        dhv = dh_ref[...]
        dg_ref[...] += jnp.sum(dhv * xhat, axis=0, keepdims=True)
        dxh = dhv * g_ref[...]
        dx_ref[...] = r_ref[...] + r * (dxh - xhat * jnp.mean(dxh * xhat, axis=-1, keepdims=True))

    row = pl.BlockSpec((tb, d), lambda i: (i, 0))
    vec = pl.BlockSpec((1, d), lambda i: (0, 0))
    return pl.pallas_call(
        body, name=name, grid=(t // tb,), in_specs=[row, vec, row, row], out_specs=[row, vec],
        out_shape=[jax.ShapeDtypeStruct((t, d), F32), jax.ShapeDtypeStruct((1, d), F32)], compiler_params=_cp(),
    )(x, g, dh, resid)


def _conv_silu(xv, w, row):
    y = xv * w[3:4, :]
    for s in (1, 2, 3):
        xs = jnp.where(row >= s, pltpu.roll(xv, s, 0), 0.0)
        y = y + xs * w[3 - s:4 - s, :]
    return y, y * _sigmoid(y)


def _dn_prep_fwd(proj, conv_w):
    t = proj.shape[0]

    def body(p_ref, w_ref, o_ref):
        j = pl.program_id(0)
        xv = p_ref[...]
        row = lax.broadcasted_iota(jnp.int32, xv.shape, 0)
        _, a = _conv_silu(xv, w_ref[...], row)
        inv = lax.rsqrt(jnp.sum(a * a, axis=-1, keepdims=True) + NORM_EPS)
        scale = jnp.where(j < N_HEADS, D_HEAD ** -0.5, 1.0)
        normed = jnp.where(j < 2 * N_HEADS, 1.0, 0.0)
        o_ref[...] = a * (normed * (inv * scale) + (1.0 - normed))

    return pl.pallas_call(
        body, name="dn_prep_fwd", grid=(3 * N_HEADS,),
        in_specs=[pl.BlockSpec((t, D_HEAD), lambda j: (0, j)), pl.BlockSpec((4, D_HEAD), lambda j: (0, j))],
        out_specs=pl.BlockSpec((t, D_HEAD), lambda j: (0, j)),
        out_shape=jax.ShapeDtypeStruct((t, 3 * D_MODEL), F32), compiler_params=_cp(),
    )(proj, conv_w)


def _dn_prep_bwd(proj, conv_w, dq, dk, dv):
    t = proj.shape[0]

    def body(p_ref, w_ref, dq_ref, dk_ref, dv_ref, dp_ref, dw_ref):
        j = pl.program_id(0)
        xv = p_ref[...]
        w = w_ref[...]
        row = lax.broadcasted_iota(jnp.int32, xv.shape, 0)
        y, a = _conv_silu(xv, w, row)
        part = jnp.zeros(xv.shape, jnp.int32) + j // N_HEADS
        dn = jnp.where(part == 0, dq_ref[...], jnp.where(part == 1, dk_ref[...], dv_ref[...]))
        inv = lax.rsqrt(jnp.sum(a * a, axis=-1, keepdims=True) + NORM_EPS)
        scale = jnp.where(j < N_HEADS, D_HEAD ** -0.5, 1.0)
        ds = dn * scale
        da_norm = inv * ds - a * (inv * inv * inv) * jnp.sum(ds * a, axis=-1, keepdims=True)
        normed = jnp.where(j < 2 * N_HEADS, 1.0, 0.0)
        da = normed * da_norm + (1.0 - normed) * dn
        s = _sigmoid(y)
        dy = da * (s * (1.0 + y * (1.0 - s)))
        dx = dy * w[3:4, :]
        dw_ref[3:4, :] = jnp.sum(dy * xv, axis=0, keepdims=True)
        for sft in (1, 2, 3):
            xs = jnp.where(row >= sft, pltpu.roll(xv, sft, 0), 0.0)
            dw_ref[3 - sft:4 - sft, :] = jnp.sum(dy * xs, axis=0, keepdims=True)
            dys = jnp.where(row < t - sft, pltpu.roll(dy, t - sft, 0), 0.0)
            dx = dx + dys * w[3 - sft:4 - sft, :]
        dp_ref[...] = dx.astype(BF16)

    blk = pl.BlockSpec((t, D_HEAD), lambda j: (0, j))
    wblk = pl.BlockSpec((4, D_HEAD), lambda j: (0, j))

    def grad(part):
        return pl.BlockSpec((t, D_HEAD), lambda j: (0, jnp.clip(j - part * N_HEADS, 0, N_HEADS - 1)))

    return pl.pallas_call(
        body, name="dn_prep_bwd", grid=(3 * N_HEADS,), in_specs=[blk, wblk, grad(0), grad(1), grad(2)],
        out_specs=[blk, wblk],
        out_shape=[jax.ShapeDtypeStruct((t, 3 * D_MODEL), BF16), jax.ShapeDtypeStruct((4, 3 * D_MODEL), F32)],
        compiler_params=_cp(),
    )(proj, conv_w, dq, dk, dv)


def _softplus_parts(xv):
    e = jnp.exp(-jnp.abs(xv))
    return jnp.maximum(xv, 0.0) + _log1p_small(e)


def _chunk_scan(v, row, reverse):
    t = v.shape[0]
    pos = row & (DN_CHUNK - 1)
    s = 1
    while s < DN_CHUNK:
        if reverse:
            v = v + jnp.where(pos < DN_CHUNK - s, pltpu.roll(v, t - s, 0), 0.0)
        else:
            v = v + jnp.where(pos >= s, pltpu.roll(v, s, 0), 0.0)
        s *= 2
    return v


def _dn_gate_fwd(proj, alog_row, dtb_row):
    t = proj.shape[0]

    def body(p_ref, al_ref, dt_ref, b_ref, g_ref):
        p = p_ref[...]
        row = lax.broadcasted_iota(jnp.int32, p.shape, 0)
        b_ref[...] = _sigmoid(p)
        g = -jnp.exp(al_ref[...]) * _softplus_parts(p + dt_ref[...])
        g_ref[...] = _chunk_scan(g, row, reverse=False)

    blk = pl.BlockSpec((t, 128), lambda i: (0, C_BA // 128))
    vec = pl.BlockSpec((1, 128), lambda i: (0, 0))
    out = pl.BlockSpec((t, 128), lambda i: (0, 0))
    return pl.pallas_call(
        body, name="dn_gate_fwd", grid=(1,), in_specs=[blk, vec, vec], out_specs=[out, out],
        out_shape=[jax.ShapeDtypeStruct((t, 128), F32)] * 2, compiler_params=_cp(),
    )(proj, alog_row, dtb_row)


def _dn_gate_bwd(proj, alog_row, dtb_row, dbeta, dgc):
    t = proj.shape[0]

    def body(p_ref, al_ref, dt_ref, db_ref, dg_ref, dp_ref, dal_ref, ddt_ref):
        p = p_ref[...]
        row = lax.broadcasted_iota(jnp.int32, p.shape, 0)
        lane = lax.broadcasted_iota(jnp.int32, p.shape, 1)
        s = _sigmoid(p)
        d_b = db_ref[...] * s * (1.0 - s)
        dg = _chunk_scan(dg_ref[...], row, reverse=True)
        xa = p + dt_ref[...]
        ea = jnp.exp(al_ref[...])
        g = -ea * _softplus_parts(xa)
        d_a = dg * (-ea) * _sigmoid(xa)
        dp_ref[...] = jnp.where(lane < N_HEADS, d_b, jnp.where(lane < 2 * N_HEADS, d_a, 0.0)).astype(BF16)
        dal_ref[...] = jnp.sum(dg * g, axis=0, keepdims=True)
        ddt_ref[...] = jnp.sum(d_a, axis=0, keepdims=True)

    blk = pl.BlockSpec((t, 128), lambda i: (0, C_BA // 128))
    vec = pl.BlockSpec((1, 128), lambda i: (0, 0))
    full = pl.BlockSpec((t, 128), lambda i: (0, 0))
    return pl.pallas_call(
        body, name="dn_gate_bwd", grid=(1,), in_specs=[blk, vec, vec, full, full], out_specs=[full, vec, vec],
        out_shape=[jax.ShapeDtypeStruct((t, 128), BF16), jax.ShapeDtypeStruct((1, 128), F32),
                   jax.ShapeDtypeStruct((1, 128), F32)], compiler_params=_cp(),
    )(proj, alog_row, dtb_row, dbeta, dgc)


def _col_to_row(col, eye):
    return jnp.sum(jnp.where(eye, col, 0.0), axis=-2, keepdims=True)


def _row_to_col(rowv, eye):
    return jnp.sum(jnp.where(eye, rowv, 0.0), axis=-1, keepdims=True)


def _tri_inverse(m, ri, ci):
    eye = (ri == ci).astype(F32)
    b16 = (ri >> 4) == (ci >> 4)
    b32 = (ri >> 5) == (ci >> 5)
    m1 = jnp.where(b16, m, 0.0)
    x = eye - m1
    p = _dot3(m1, m1, "nn")
    x = x + _dot3(x, p, "nn")
    p = _dot3(p, p, "nn")
    x = x + _dot3(x, p, "nn")
    p = _dot3(p, p, "nn")
    x = x + _dot3(x, p, "nn")
    c1 = jnp.where(jnp.logical_and(b32, jnp.logical_not(b16)), m, 0.0)
    x = x - _dot3(_dot3(x, c1, "nn"), x, "nn")
    c2 = jnp.where(b32, 0.0, m)
    x = x - _dot3(_dot3(x, c2, "nn"), x, "nn")
    return x


def _dn_chunk_common(q, k, gc, ri, ci):
    eye = ri == ci
    g_row = _col_to_row(gc, eye)
    diff = jnp.minimum(gc - g_row, 0.0)
    gam = jnp.where(ri >= ci, jnp.exp(diff), 0.0)
    kk = _bdot(k, k, "nt")
    qk = _bdot(q, k, "nt")
    rcol = lax.broadcasted_iota(jnp.int32, gc.shape, gc.ndim - 2)
    last = jnp.sum(jnp.where(rcol == DN_CHUNK - 1, gc, 0.0), axis=-2, keepdims=True)
    e_g = jnp.exp(gc)
    dec = jnp.exp(last - gc)
    return eye, gam, kk, qk, last, e_g, dec, rcol


def _dn_specs(t, rows_blk):
    def head(off):
        return pl.BlockSpec((rows_blk, D_HEAD), lambda g, h: (g, off + h))

    lanes = pl.BlockSpec((rows_blk, 128), lambda g, h: (g, 0))
    hm = pl.BlockSpec((None, rows_blk, D_HEAD), lambda g, h: (h, g, 0))
    sq = pl.BlockSpec((1, rows_blk, DN_CHUNK), lambda g, h: (h, g, 0))
    tile = pl.BlockSpec((1, rows_blk // DN_CHUNK, 8, 128), lambda g, h: (h, g, 0, 0))
    return head, lanes, hm, sq, tile


def _head_column(slab, lane_idx):
    lane = lax.broadcasted_iota(jnp.int32, slab.shape, 1)
    return _chunks(jnp.sum(jnp.where(lane == lane_idx, slab, 0.0), axis=1, keepdims=True))


def _dn_intra_fwd(qkv, beta_t, g_t):
    t = qkv.shape[0]
    n_chunks = t // DN_CHUNK
    rows_blk = min(DN_GROUP * DN_CHUNK, t)

    def body(q_ref, k_ref, v_ref, b_ref, g_ref, u_ref, w_ref, qd_ref, kd_ref, a_ref, ti_ref, el_ref):
        ri = lax.broadcasted_iota(jnp.int32, (DN_CHUNK, DN_CHUNK), 0)
        ci = lax.broadcasted_iota(jnp.int32, (DN_CHUNK, DN_CHUNK), 1)
        h = pl.program_id(1)
        q, k, v = (_chunks(r[...]) for r in (q_ref, k_ref, v_ref))
        b, gc = _head_column(b_ref[...], h), _head_column(g_ref[...], h + N_HEADS)
        _, gam, kk, qk, last, e_g, dec, _ = _dn_chunk_common(q, k, gc, ri, ci)
        tinv = _tri_inverse(jnp.where(ri > ci, b * kk * gam, 0.0), ri, ci)
        u_ref[...] = _unchunk(_bdot(tinv, v * b, "nn"))
        w_ref[...] = _unchunk(_bdot(tinv, k * (b * e_g), "nn"))
        qd_ref[...] = _unchunk(q * e_g)
        kd_ref[...] = _unchunk(k * dec)
        a_ref[0] = _unchunk(qk * gam)
        ti_ref[0] = _unchunk(tinv)
        el_ref[0] = jnp.broadcast_to(jnp.exp(last), (rows_blk // DN_CHUNK, 8, 128))

    head, lanes, hm, sq, tile = _dn_specs(t, rows_blk)
    act = jax.ShapeDtypeStruct((N_HEADS, t, D_HEAD), F32)
    sqs = jax.ShapeDtypeStruct((N_HEADS, t, DN_CHUNK), F32)
    return pl.pallas_call(
        body, name="dn_intra_fwd", grid=(t // rows_blk, N_HEADS),
        in_specs=[head(0), head(N_HEADS), head(2 * N_HEADS), lanes, lanes],
        out_specs=[hm] * 4 + [sq, sq, tile],
        out_shape=[act] * 4 + [sqs, sqs, jax.ShapeDtypeStruct((N_HEADS, n_chunks, 8, 128), F32)],
        compiler_params=_cp(),
    )(qkv, qkv, qkv, beta_t, g_t)


def _dn_scan_specs(t, rows_blk, reverse):
    n_groups = t // rows_blk

    def at(g):
        return n_groups - 1 - g if reverse else g

    per = rows_blk // DN_CHUNK
    act = pl.BlockSpec((N_HEADS, rows_blk, D_HEAD), lambda g: (0, at(g), 0))
    sq = pl.BlockSpec((N_HEADS, rows_blk, DN_CHUNK), lambda g: (0, at(g), 0))
    state = pl.BlockSpec((N_HEADS, per, D_HEAD, D_HEAD), lambda g: (0, at(g), 0, 0))
    tile = pl.BlockSpec((N_HEADS, per, 8, 128), lambda g: (0, at(g), 0, 0))
    return act, sq, state, tile


def _dn_scan_fwd(u, w, qd, kd, a, el):
    t = u.shape[1]
    n_chunks = t // DN_CHUNK
    rows_blk = DN_SCAN_GROUP * DN_CHUNK

    def body(u_ref, w_ref, qd_ref, kd_ref, a_ref, el_ref, o_ref, vn_ref, s_ref, s_scr):
        @pl.when(pl.program_id(0) == 0)
        def _():
            s_scr[...] = jnp.zeros_like(s_scr)

        for cc in range(DN_SCAN_GROUP):
            rows = slice(cc * DN_CHUNK, (cc + 1) * DN_CHUNK)
            s = s_scr[...]
            s_ref[:, cc] = s
            v_new = u_ref[:, rows, :] - _bdot(w_ref[:, rows, :], s, "nn")
            vn_ref[:, rows, :] = v_new
            o_ref[:, rows, :] = _bdot(qd_ref[:, rows, :], s, "nn") + _bdot(a_ref[:, rows, :], v_new, "nn")
            s_scr[...] = s * el_ref[:, cc][:, 0:1, :] + _bdot(kd_ref[:, rows, :], v_new, "tn")

    act, sq, state, tile = _dn_scan_specs(t, rows_blk, reverse=False)
    shp = jax.ShapeDtypeStruct((N_HEADS, t, D_HEAD), F32)
    return pl.pallas_call(
        body, name="dn_scan_fwd", grid=(t // rows_blk,),
        in_specs=[act, act, act, act, sq, tile], out_specs=[act, act, state],
        out_shape=[shp, shp, jax.ShapeDtypeStruct((N_HEADS, n_chunks, D_HEAD, D_HEAD), F32)],
        scratch_shapes=[pltpu.VMEM((N_HEADS, D_HEAD, D_HEAD), F32)],
        compiler_params=_cp(dimension_semantics=("arbitrary",)),
    )(u, w, qd, kd, a, el)


def _dn_scan_bwd(w, qd, kd, a, el, vn, s_all, do):
    t = w.shape[1]
    n_chunks = t // DN_CHUNK
    rows_blk = DN_SCAN_GROUP * DN_CHUNK

    def body(w_ref, qd_ref, kd_ref, a_ref, el_ref, vn_ref, s_ref, do_ref, dvn_ref, dkd_ref, dqd_ref, dw_ref, dl_ref, ds_scr):
        @pl.when(pl.program_id(0) == 0)
        def _():
            ds_scr[...] = jnp.zeros_like(ds_scr)

        for cc in reversed(range(DN_SCAN_GROUP)):
            rows = slice(cc * DN_CHUNK, (cc + 1) * DN_CHUNK)
            s = s_ref[:, cc]
            d_s = ds_scr[...]
            e_last = el_ref[:, cc][:, 0:1, :]
            d_o = do_ref[:, rows, :]
            dv_new = _bdot(a_ref[:, rows, :], d_o, "tn") + _bdot(kd_ref[:, rows, :], d_s, "nn")
            ds_scr[...] = d_s * e_last + _bdot(qd_ref[:, rows, :], d_o, "tn") - _bdot(w_ref[:, rows, :], dv_new, "tn")
            dvn_ref[:, rows, :] = dv_new
            dkd_ref[:, rows, :] = _bdot(vn_ref[:, rows, :], d_s, "nt")
            dqd_ref[:, rows, :] = _bdot(d_o, s, "nt")
            dw_ref[:, rows, :] = -_bdot(dv_new, s, "nt")
            dlast = jnp.sum(jnp.sum(d_s * s, axis=2, keepdims=True), axis=1, keepdims=True)
            dl_ref[:, cc] = jnp.broadcast_to(dlast * e_last, (N_HEADS, 8, 128))

    act, sq, state, tile = _dn_scan_specs(t, rows_blk, reverse=True)
    shp = jax.ShapeDtypeStruct((N_HEADS, t, D_HEAD), F32)
    return pl.pallas_call(
        body, name="dn_scan_bwd", grid=(t // rows_blk,),
        in_specs=[act, act, act, sq, tile, act, state, act], out_specs=[act] * 4 + [tile],
        out_shape=[shp] * 4 + [jax.ShapeDtypeStruct((N_HEADS, n_chunks, 8, 128), F32)],
        scratch_shapes=[pltpu.VMEM((N_HEADS, D_HEAD, D_HEAD), F32)],
        compiler_params=_cp(dimension_semantics=("arbitrary",)),
    )(w, qd, kd, a, el, vn, s_all, do)


def _dn_intra_bwd(qkv, beta_t, g_t, tinv_all, vn, do, dvn, dkd, dqd, dw, dl):
    t = qkv.shape[0]
    rows_blk = min(DN_GROUP * DN_CHUNK, t)

    def body(q_ref, k_ref, v_ref, b_ref, g_ref, ti_ref, vn_ref, do_ref, dvn_ref, dkd_ref, dqd_ref, dw_ref, dl_ref,
             dq_ref, dk_ref, dv_ref, db_ref, dg_ref):
        ri = lax.broadcasted_iota(jnp.int32, (DN_CHUNK, DN_CHUNK), 0)
        ci = lax.broadcasted_iota(jnp.int32, (DN_CHUNK, DN_CHUNK), 1)
        h = pl.program_id(1)
        q, k, v = (_chunks(r[...]) for r in (q_ref, k_ref, v_ref))
        b, gc = _head_column(b_ref[...], h), _head_column(g_ref[...], h + N_HEADS)
        tinv = _chunks(ti_ref[0])
        dv_new, dk_dec, dq_dec, d_w = (_chunks(r[...]) for r in (dvn_ref, dkd_ref, dqd_ref, dw_ref))
        eye, gam, kk, qk, _, e_g, dec, rcol = _dn_chunk_common(q, k, gc, ri, ci)
        bv = v * b
        bk = k * (b * e_g)

        d_a = jnp.where(ri >= ci, _bdot(_chunks(do_ref[...]), _chunks(vn_ref[...]), "nt"), 0.0)
        dbv = _bdot(tinv, dv_new, "tn")
        dbk = _bdot(tinv, d_w, "tn")
        d_tinv = _bdot(dv_new, bv, "nt") + _bdot(d_w, bk, "nt")
        d_m = -jnp.where(ri > ci, _dot3(_dot3(tinv, d_tinv, "tn"), tinv, "nt"), 0.0)

        d_kk = d_m * b * gam
        d_gam = d_m * b * kk + d_a * qk
        d_qk = d_a * gam
        dq_ref[...] = _unchunk(_bdot(d_qk, k, "nn") + dq_dec * e_g)
        dk_ref[...] = _unchunk(_bdot(d_qk, q, "tn") + _bdot(d_kk, k, "nn") + _bdot(d_kk, k, "tn")
                               + dk_dec * dec + dbk * (b * e_g))
        dv_ref[...] = _unchunk(dbv * b)
        d_b = _unchunk(jnp.sum(d_m * kk * gam, axis=-1, keepdims=True) + jnp.sum(dbv * v, axis=-1, keepdims=True)
                       + jnp.sum(dbk * k, axis=-1, keepdims=True) * e_g)

        xg = d_gam * gam
        kdk = jnp.sum(dk_dec * (k * dec), axis=-1, keepdims=True)
        d_gc = (jnp.sum(xg, axis=-1, keepdims=True) - _row_to_col(jnp.sum(xg, axis=-2, keepdims=True), eye)
                + jnp.sum(dq_dec * (q * e_g), axis=-1, keepdims=True) - kdk
                + jnp.sum(dbk * bk, axis=-1, keepdims=True))
        d_last_total = dl_ref[0][:, 0:1, 0:1] + jnp.sum(kdk, axis=-2, keepdims=True)
        d_g = _unchunk(d_gc + jnp.where(rcol == DN_CHUNK - 1, d_last_total, 0.0))

        @pl.when(h == 0)
        def _():
            db_ref[...] = jnp.zeros_like(db_ref)
            dg_ref[...] = jnp.zeros_like(dg_ref)

        lane = lax.broadcasted_iota(jnp.int32, db_ref.shape, 1)
        db_ref[...] += jnp.where(lane == h, d_b, 0.0)
        dg_ref[...] += jnp.where(lane == h + N_HEADS, d_g, 0.0)

    head, lanes, hm, sq, tile = _dn_specs(t, rows_blk)
    return pl.pallas_call(
        body, name="dn_intra_bwd", grid=(t // rows_blk, N_HEADS),
        in_specs=[head(0), head(N_HEADS), head(2 * N_HEADS), lanes, lanes, sq] + [hm] * 6 + [tile],
        out_specs=[head(0), head(0), head(0), lanes, lanes],
        out_shape=[jax.ShapeDtypeStruct((t, D_MODEL), F32)] * 3 + [jax.ShapeDtypeStruct((t, 128), F32)] * 2,
        compiler_params=_cp(),
    )(qkv, qkv, qkv, beta_t, g_t, tinv_all, vn, do, dvn, dkd, dqd, dw, dl)


def _dn_post_fwd(o, proj, gn):
    t = o.shape[1]

    def body(o_ref, z_ref, g_ref, out_ref):
        ov, z = o_ref[...], z_ref[...]
        r = lax.rsqrt(jnp.mean(ov * ov, axis=-1, keepdims=True) + NORM_EPS)
        out_ref[...] = (((ov * r) * g_ref[...]) * (z * _sigmoid(z))).astype(BF16)

    blk = pl.BlockSpec((t, D_HEAD), lambda h: (0, h))
    return pl.pallas_call(
        body, name="dn_post_fwd", grid=(N_HEADS,),
        in_specs=[pl.BlockSpec((None, t, D_HEAD), lambda h: (h, 0, 0)),
                  pl.BlockSpec((t, D_HEAD), lambda h: (0, C_DNZ // D_HEAD + h)),
                  pl.BlockSpec((1, D_HEAD), lambda h: (0, 0))],
        out_specs=blk, out_shape=jax.ShapeDtypeStruct((t, D_MODEL), BF16), compiler_params=_cp(),
    )(o, proj, gn)


def _dn_post_bwd(o, proj, gn, dout):
    t = o.shape[1]

    def body(o_ref, z_ref, g_ref, d_ref, do_ref, dz_ref, dg_ref):
        @pl.when(pl.program_id(0) == 0)
        def _():
            dg_ref[...] = jnp.zeros_like(dg_ref)

        ov, z, d = o_ref[...], z_ref[...], d_ref[...]
        r = lax.rsqrt(jnp.mean(ov * ov, axis=-1, keepdims=True) + NORM_EPS)
        ohat = ov * r
        s = _sigmoid(z)
        d_on = d * (z * s)
        dz_ref[...] = (d * (ohat * g_ref[...]) * (s * (1.0 + z * (1.0 - s)))).astype(BF16)
        dg_ref[...] += jnp.sum(d_on * ohat, axis=0, keepdims=True)
        dxh = d_on * g_ref[...]
        do_ref[...] = r * (dxh - ohat * jnp.mean(dxh * ohat, axis=-1, keepdims=True))

    blk = pl.BlockSpec((t, D_HEAD), lambda h: (0, h))
    hm = pl.BlockSpec((None, t, D_HEAD), lambda h: (h, 0, 0))
    vec = pl.BlockSpec((1, D_HEAD), lambda h: (0, 0))
    return pl.pallas_call(
        body, name="dn_post_bwd", grid=(N_HEADS,),
        in_specs=[hm, pl.BlockSpec((t, D_HEAD), lambda h: (0, C_DNZ // D_HEAD + h)), vec, blk],
        out_specs=[hm, blk, vec],
        out_shape=[jax.ShapeDtypeStruct((N_HEADS, t, D_HEAD), F32), jax.ShapeDtypeStruct((t, D_MODEL), BF16),
                   jax.ShapeDtypeStruct((1, D_HEAD), F32)], compiler_params=_cp(),
    )(o, proj, gn, dout)


def _sb_fwd(proj):
    t = proj.shape[0]
    qblk = min(SB_QBLOCK, t)
    scale = 1.0 / math.sqrt(D_HEAD)

    hp = SB_HEADS_PER_STEP
    wid = hp * D_HEAD

    def body(q_ref, k_ref, v_ref, z_ref, o_ref, og_ref, l_ref, qb, kb, vb):
        for hh in range(hp):
            hs = slice(hh * D_HEAD, (hh + 1) * D_HEAD)
            qb[hh] = q_ref[:, hs].astype(BF16)
            kb[hh] = k_ref[:, hs].astype(BF16)
            vb[hh] = v_ref[:, hs].astype(BF16)
        ri = lax.broadcasted_iota(jnp.int32, (qblk, SB_BLOCK), 0)
        ci = lax.broadcasted_iota(jnp.int32, (qblk, SB_BLOCK), 1)
        r2 = lax.broadcasted_iota(jnp.int32, (SB_BLOCK, SB_BLOCK), 0)
        c2 = lax.broadcasted_iota(jnp.int32, (SB_BLOCK, SB_BLOCK), 1)
        upper = (r2 > c2).astype(BF16)
        nkb = qblk // SB_BLOCK

        def qblock(i, carry):
            rows = pl.ds(pl.multiple_of(i * qblk, qblk), qblk)
            qi = qb[:, rows, :]

            def kblock(jj, st):
                acc, c = st
                j = (i + 1) * nkb - 1 - jj
                cols = pl.ds(pl.multiple_of(j * SB_BLOCK, SB_BLOCK), SB_BLOCK)
                mask = (j * SB_BLOCK + ci) < (i * qblk + ri)
                z = _dot(qi, kb[:, cols, :], "nt") * scale
                lb = jnp.minimum(z, 0.0) - jnp.log(1.0 + jnp.exp(-jnp.abs(z)))
                lf = jnp.where(mask, lb - z, 0.0)
                surv = _split_dot(lf, upper) + c
                att = jnp.where(mask, jnp.exp(lb + surv), 0.0)
                acc = acc + _dot(att.astype(BF16), vb[:, cols, :], "nn")
                return acc, c + jnp.sum(lf, axis=-1, keepdims=True)

            init = (jnp.zeros((hp, qblk, D_HEAD), F32), jnp.zeros((hp, qblk, 1), F32))
            acc, c = lax.fori_loop(0, (i + 1) * nkb, kblock, init)
            l_ref[:, rows, :] = c
            for hh in range(hp):
                hs = slice(hh * D_HEAD, (hh + 1) * D_HEAD)
                zg = z_ref[rows, hs]
                o_ref[rows, hs] = acc[hh]
                og_ref[rows, hs] = (acc[hh] * (zg * _sigmoid(zg))).astype(BF16)
            return carry

        lax.fori_loop(0, t // qblk, qblock, 0)

    def head(off):
        return pl.BlockSpec((t, wid), lambda h: (0, off // wid + h))

    out = pl.BlockSpec((t, wid), lambda h: (0, h))
    return pl.pallas_call(
        body, name="sb_fwd", grid=(N_HEADS // hp,),
        in_specs=[head(C_SBQ), head(C_SBQ + D_MODEL), head(C_SBQ + 2 * D_MODEL), head(C_SBZ)],
        out_specs=[out, out, pl.BlockSpec((hp, t, 1), lambda h: (h, 0, 0))],
        out_shape=[jax.ShapeDtypeStruct((t, D_MODEL), F32), jax.ShapeDtypeStruct((t, D_MODEL), BF16),
                   jax.ShapeDtypeStruct((N_HEADS, t, 1), F32)],
        scratch_shapes=[pltpu.VMEM((hp, t, D_HEAD), BF16)] * 3, compiler_params=_cp(),
    )(proj, proj, proj, proj)


def _sb_bwd(proj, o, ltot, dog):
    t = proj.shape[0]
    qblk = min(SB_QBLOCK, t)
    scale = 1.0 / math.sqrt(D_HEAD)

    hp = SB_HEADS_PER_STEP
    wid = hp * D_HEAD

    def body(q_ref, k_ref, v_ref, z_ref, o_ref, l_ref, d_ref, dq_ref, dk_ref, dv_ref, dz_ref,
             qb, kb, vb, dob, dk_scr, dv_scr):
        for hh in range(hp):
            hs = slice(hh * D_HEAD, (hh + 1) * D_HEAD)
            qb[hh] = q_ref[:, hs].astype(BF16)
            kb[hh] = k_ref[:, hs].astype(BF16)
            vb[hh] = v_ref[:, hs].astype(BF16)
            zg = z_ref[:, hs]
            sg = _sigmoid(zg)
            dgo = d_ref[:, hs]
            dob[hh] = (dgo * (zg * sg)).astype(BF16)
            dz_ref[:, hs] = (dgo * o_ref[:, hs] * (sg * (1.0 + zg * (1.0 - sg)))).astype(BF16)
        dk_scr[...] = jnp.zeros_like(dk_scr)
        dv_scr[...] = jnp.zeros_like(dv_scr)
        ri = lax.broadcasted_iota(jnp.int32, (qblk, SB_BLOCK), 0)
        ci = lax.broadcasted_iota(jnp.int32, (qblk, SB_BLOCK), 1)
        r2 = lax.broadcasted_iota(jnp.int32, (SB_BLOCK, SB_BLOCK), 0)
        c2 = lax.broadcasted_iota(jnp.int32, (SB_BLOCK, SB_BLOCK), 1)
        incl = (r2 <= c2).astype(BF16)
        below = (r2 < c2).astype(BF16)

        def qblock(i, carry):
            rows = pl.ds(pl.multiple_of(i * qblk, qblk), qblk)
            qi = qb[:, rows, :]
            d_o = dob[:, rows, :]
            ltot = l_ref[:, rows, :]

            def kblock(j, st):
                dq, cpre, ce = st
                cols = pl.ds(pl.multiple_of(j * SB_BLOCK, SB_BLOCK), SB_BLOCK)
                mask = (j * SB_BLOCK + ci) < (i * qblk + ri)
                kj, vj = kb[:, cols, :], vb[:, cols, :]
                z = _dot(qi, kj, "nt") * scale
                lb = jnp.minimum(z, 0.0) - jnp.log(1.0 + jnp.exp(-jnp.abs(z)))
                lf = jnp.where(mask, lb - z, 0.0)
                surv = ltot - (cpre + _split_dot(lf, incl))
                att = jnp.where(mask, jnp.exp(lb + surv), 0.0)
                e = _dot(d_o, vj, "nt") * att
                dlf = ce + _split_dot(e, below)
                dzz = jnp.where(mask, e - (e + dlf) * jnp.exp(lb), 0.0).astype(BF16)
                dq = dq + _dot(dzz, kj, "nn")
                dk_scr[:, cols, :] += _dot(dzz, qi, "tn")
                dv_scr[:, cols, :] += _dot(att.astype(BF16), d_o, "tn")
                return dq, cpre + jnp.sum(lf, axis=-1, keepdims=True), ce + jnp.sum(e, axis=-1, keepdims=True)

            zero_col = jnp.zeros((hp, qblk, 1), F32)
            init = (jnp.zeros((hp, qblk, D_HEAD), F32), zero_col, zero_col)
            dq, _, _ = lax.fori_loop(0, (i + 1) * (qblk // SB_BLOCK), kblock, init)
            for hh in range(hp):
                dq_ref[rows, hh * D_HEAD:(hh + 1) * D_HEAD] = (dq[hh] * scale).astype(BF16)
            return carry

        lax.fori_loop(0, t // qblk, qblock, 0)
        for hh in range(hp):
            hs = slice(hh * D_HEAD, (hh + 1) * D_HEAD)
            dk_ref[:, hs] = (dk_scr[hh] * scale).astype(BF16)
            dv_ref[:, hs] = dv_scr[hh].astype(BF16)

    def head(off):
        return pl.BlockSpec((t, wid), lambda h: (0, off // wid + h))

    return pl.pallas_call(
        body, name="sb_bwd", grid=(N_HEADS // hp,),
        in_specs=[head(C_SBQ), head(C_SBQ + D_MODEL), head(C_SBQ + 2 * D_MODEL), head(C_SBZ), head(0),
                  pl.BlockSpec((hp, t, 1), lambda h: (h, 0, 0)), head(0)],
        out_specs=[head(0)] * 4, out_shape=[jax.ShapeDtypeStruct((t, D_MODEL), BF16)] * 4,
        scratch_shapes=[pltpu.VMEM((hp, t, D_HEAD), BF16)] * 4 + [pltpu.VMEM((hp, t, D_HEAD), F32)] * 2,
        compiler_params=_cp(),
    )(proj, proj, proj, proj, o, ltot, dog)


def _mem_fwd(proj, mkv):
    t = proj.shape[0]
    tq = _pick(t, (512, 256))
    m_len = mkv.shape[0]
    scale = 1.0 / math.sqrt(MEM_DH)

    def body(q_ref, z_ref, kv_ref, o_ref, og_ref):
        q = q_ref[...]
        mk = kv_ref[:, :MEM_W].astype(BF16)
        mv = kv_ref[:, MEM_W:].astype(BF16)
        lane = lax.broadcasted_iota(jnp.int32, q.shape, 1) >> 6
        o = jnp.zeros(q.shape, F32)
        for h in range(MEM_HEADS):
            s = _bdot(jnp.where(lane == h, q, 0.0), mk, "nt") * scale
            p = jnp.exp(s - jnp.max(s, axis=-1, keepdims=True))
            p = p / jnp.sum(p, axis=-1, keepdims=True)
            o = o + jnp.where(lane == h, _bdot(p, mv, "nn"), 0.0)
        z = z_ref[...]
        o_ref[...] = o
        og_ref[...] = (o * (z * _sigmoid(z))).astype(BF16)

    out = pl.BlockSpec((tq, MEM_W), lambda i: (i, 0))
    return pl.pallas_call(
        body, name="mem_fwd", grid=(t // tq,),
        in_specs=[pl.BlockSpec((tq, MEM_W), lambda i: (i, C_MQ // MEM_W)),
                  pl.BlockSpec((tq, MEM_W), lambda i: (i, C_MZ // MEM_W)),
                  pl.BlockSpec((m_len, 2 * MEM_W), lambda i: (0, 0))],
        out_specs=[out, out],
        out_shape=[jax.ShapeDtypeStruct((t, MEM_W), F32), jax.ShapeDtypeStruct((t, MEM_W), BF16)],
        compiler_params=_cp(),
    )(proj, proj, mkv)


def _mem_bwd(proj, mkv, o, dog):
    t = proj.shape[0]
    tq = _pick(t, (512, 256))
    m_len = mkv.shape[0]
    scale = 1.0 / math.sqrt(MEM_DH)

    def body(q_ref, z_ref, kv_ref, o_ref, d_ref, dq_ref, dz_ref, dkv_ref):
        @pl.when(pl.program_id(0) == 0)
        def _():
            dkv_ref[...] = jnp.zeros_like(dkv_ref)

        q = q_ref[...]
        z = z_ref[...]
        sg = _sigmoid(z)
        dgo = d_ref[...]
        d_o = dgo * (z * sg)
        dz_ref[...] = (dgo * o_ref[...] * (sg * (1.0 + z * (1.0 - sg)))).astype(BF16)
        mk = kv_ref[:, :MEM_W].astype(BF16)
        mv = kv_ref[:, MEM_W:].astype(BF16)
        lane = lax.broadcasted_iota(jnp.int32, q.shape, 1) >> 6
        klane = lax.broadcasted_iota(jnp.int32, (m_len, MEM_W), 1) >> 6
        dq = jnp.zeros(q.shape, F32)
        dmk = jnp.zeros((m_len, MEM_W), F32)
        dmv = jnp.zeros((m_len, MEM_W), F32)
        for h in range(MEM_HEADS):
            qh = jnp.where(lane == h, q, 0.0)
            doh = jnp.where(lane == h, d_o, 0.0)
            s = _bdot(qh, mk, "nt") * scale
            p = jnp.exp(s - jnp.max(s, axis=-1, keepdims=True))
            p = p / jnp.sum(p, axis=-1, keepdims=True)
            dp = _bdot(doh, mv, "nt")
            ds = p * (dp - jnp.sum(dp * p, axis=-1, keepdims=True)) * scale
            dq = dq + jnp.where(lane == h, _bdot(ds, mk, "nn"), 0.0)
            dmk = dmk + jnp.where(klane == h, _bdot(ds, qh, "tn"), 0.0)
            dmv = dmv + jnp.where(klane == h, _bdot(p, doh, "tn"), 0.0)
        dq_ref[...] = dq.astype(BF16)
        dkv_ref[:, :MEM_W] += dmk
        dkv_ref[:, MEM_W:] += dmv

    blk = pl.BlockSpec((tq, MEM_W), lambda i: (i, 0))
    kv = pl.BlockSpec((m_len, 2 * MEM_W), lambda i: (0, 0))
    return pl.pallas_call(
        body, name="mem_bwd", grid=(t // tq,),
        in_specs=[pl.BlockSpec((tq, MEM_W), lambda i: (i, C_MQ // MEM_W)),
                  pl.BlockSpec((tq, MEM_W), lambda i: (i, C_MZ // MEM_W)), kv, blk, blk],
        out_specs=[blk, blk, kv],
        out_shape=[jax.ShapeDtypeStruct((t, MEM_W), BF16), jax.ShapeDtypeStruct((t, MEM_W), BF16),
                   jax.ShapeDtypeStruct((m_len, 2 * MEM_W), F32)], compiler_params=_cp(),
    )(proj, proj, mkv, o, dog)


_GW = 512


def _merge_fwd(proj, y_dn, y_sb, y_m):
    t = proj.shape[0]
    tb = _pick(t, (256,))
    nc = D_MODEL // _GW

    def body(g1, g2, g3, y1, y2, y3, out_ref):
        out_ref[...] = (_sigmoid(g1[...]) * y1[...] + _sigmoid(g2[...]) * y2[...] + _sigmoid(g3[...]) * y3[...]).astype(BF16)

    def gate(kb):
        return pl.BlockSpec((tb, _GW), lambda i, c: (i, C_GATES // _GW + kb * nc + c))

    blk = pl.BlockSpec((tb, _GW), lambda i, c: (i, c))
    return pl.pallas_call(
        body, name="merge_fwd", grid=(t // tb, nc), in_specs=[gate(0), gate(1), gate(2), blk, blk, blk],
        out_specs=blk, out_shape=jax.ShapeDtypeStruct((t, D_MODEL), BF16), compiler_params=_cp(),
    )(proj, proj, proj, y_dn, y_sb, y_m)


def _merge_bwd(proj, y_dn, y_sb, y_m, dm):
    t = proj.shape[0]
    tb = _pick(t, (256,))
    nc = D_MODEL // _GW

    def body(g1, g2, g3, y1, y2, y3, dm_ref, d1, d2, d3, dg1, dg2, dg3):
        d = dm_ref[...]
        for g, y, dy, dg in ((g1, y1, d1, dg1), (g2, y2, d2, dg2), (g3, y3, d3, dg3)):
            s = _sigmoid(g[...])
            dy[...] = (d * s).astype(BF16)
            dg[...] = (d * y[...] * (s * (1.0 - s))).astype(BF16)

    def gate(kb):
        return pl.BlockSpec((tb, _GW), lambda i, c: (i, C_GATES // _GW + kb * nc + c))

    blk = pl.BlockSpec((tb, _GW), lambda i, c: (i, c))
    act = jax.ShapeDtypeStruct((t, D_MODEL), BF16)
    return pl.pallas_call(
        body, name="merge_bwd", grid=(t // tb, nc), in_specs=[gate(0), gate(1), gate(2), blk, blk, blk, blk],
        out_specs=[blk] * 6, out_shape=[act] * 6, compiler_params=_cp(),
    )(proj, proj, proj, y_dn, y_sb, y_m, dm)


def _final_loss(x, mo, g, tgt):
    t, d = x.shape
    tb = _pick(t, (256,))

    def body(x_ref, mo_ref, g_ref, t_ref, do_ref, dob_ref, loss_ref, dg_ref):
        @pl.when(pl.program_id(0) == 0)
        def _():
            loss_ref[...] = jnp.zeros_like(loss_ref)
            dg_ref[...] = jnp.zeros_like(dg_ref)

        out = x_ref[...] + mo_ref[...]
        r = lax.rsqrt(jnp.mean(out * out, axis=-1, keepdims=True) + NORM_EPS)
        xhat = out * r
        gv = g_ref[...]
        err = xhat * gv - t_ref[...]
        per_tok = jnp.mean(err * err, axis=-1, keepdims=True)
        loss_ref[...] += 0.5 * jnp.sum(per_tok, axis=0, keepdims=True)
        dy = err * (1.0 / d)
        dg_ref[...] += jnp.sum(dy * xhat, axis=0, keepdims=True)
        dxh = dy * gv
        dout = r * (dxh - xhat * jnp.mean(dxh * xhat, axis=-1, keepdims=True))
        do_ref[...] = dout
        dob_ref[...] = dout.astype(BF16)

    row = pl.BlockSpec((tb, d), lambda i: (i, 0))
    vec = pl.BlockSpec((1, d), lambda i: (0, 0))
    return pl.pallas_call(
        body, name="final_loss", grid=(t // tb,), in_specs=[row, row, vec, row],
        out_specs=[row, row, pl.BlockSpec((1, 128), lambda i: (0, 0)), vec],
        out_shape=[jax.ShapeDtypeStruct((t, d), F32), jax.ShapeDtypeStruct((t, d), BF16),
                   jax.ShapeDtypeStruct((1, 128), F32), jax.ShapeDtypeStruct((1, d), F32)],
        compiler_params=_cp(),
    )(x, mo, g, tgt)


def _cast_bf16(a, name):
    r, c = a.shape
    tb = _pick(r, (128, 496))

    def body(a_ref, o_ref):
        o_ref[...] = a_ref[...].astype(BF16)

    blk = pl.BlockSpec((tb, c), lambda i: (i, 0))
    return pl.pallas_call(body, name=name, grid=(r // tb,), in_specs=[blk], out_specs=blk,
                          out_shape=jax.ShapeDtypeStruct((r, c), BF16), compiler_params=_cp())(a)


WIN_START = (0, 23, 45, 68)
_S1_LO, _S1_HI = 1148, 1164
_S1_BA_POS = SHARD_PAD - 128


def _to_window(x, s):
    if s == 0:
        return x
    if s in (2, 3):
        return pltpu.roll(x, 120 if s == 2 else 124, 1)
    pos = lax.broadcasted_iota(jnp.int32, x.shape, 1)
    head = pltpu.roll(x, 4, 1)
    tail = pltpu.roll(x, SHARD_PAD - 12, 1)
    ba = jnp.where(pos < _S1_BA_POS + (_S1_HI - _S1_LO), pltpu.roll(x, _S1_BA_POS - _S1_LO, 1), 0.0)
    return jnp.where(pos < _S1_LO + 4, head, jnp.where(pos < _S1_BA_POS, tail, ba))


def _from_window(g, s):
    if s == 0:
        return g
    if s in (2, 3):
        return pltpu.roll(g, SHARD_PAD - (120 if s == 2 else 124), 1)
    col = lax.broadcasted_iota(jnp.int32, g.shape, 1)
    head = pltpu.roll(g, SHARD_PAD - 4, 1)
    tail = pltpu.roll(g, 12, 1)
    ba = pltpu.roll(g, SHARD_PAD - (_S1_BA_POS - _S1_LO), 1)
    return jnp.where(col < _S1_LO, head, jnp.where(col < _S1_HI, ba, tail))


def _cast_to_window(w, shard, name):
    r, c = w.shape
    tb = _pick(r, (128,))

    def body(s_ref, w_ref, o_ref, pad_scr):
        pad_scr[...] = jnp.zeros_like(pad_scr)
        pad_scr[:, :c] = w_ref[...]
        x = pad_scr[...]
        for s in range(N_SHARD):
            @pl.when(s_ref[0] == s)
            def _():
                o_ref[...] = _to_window(x, s).astype(BF16)

    return pl.pallas_call(
        body, name=name,
        grid_spec=pltpu.PrefetchScalarGridSpec(
            num_scalar_prefetch=1, grid=(r // tb,),
            in_specs=[pl.BlockSpec((tb, c), lambda i, s: (i, 0))],
            out_specs=pl.BlockSpec((tb, SHARD_PAD), lambda i, s: (i, 0)),
            scratch_shapes=[pltpu.VMEM((tb, SHARD_PAD), F32)]),
        out_shape=jax.ShapeDtypeStruct((r, SHARD_PAD), BF16), compiler_params=_cp(),
    )(shard, w)


def _pair_add(g, recv, c_idx, name):
    n, r, c = g.shape
    half = r // 2
    tb = _pick(half, (128, 248))
    nb = half // tb

    def body(c_ref, g_ref, r_ref, o_ref):
        o_ref[...] = (g_ref[...].astype(F32) + r_ref[...].astype(F32)).astype(BF16)

    blk = pl.BlockSpec((n, tb, c), lambda i, c_ref: (0, i, 0))
    return pl.pallas_call(
        body, name=name,
        grid_spec=pltpu.PrefetchScalarGridSpec(
            num_scalar_prefetch=1, grid=(nb,),
            in_specs=[pl.BlockSpec((n, tb, c), lambda i, c_ref: (0, c_ref[0] * nb + i, 0)), blk], out_specs=blk),
        out_shape=jax.ShapeDtypeStruct((n, half, c), BF16), compiler_params=_cp(),
    )(c_idx, g, recv)


def _chip_sum(parts, by_chip, place, name):
    n, h, c = parts.shape
    tb = _pick(h, (128, 248))
    nb = h // tb

    def body(p_ref, mine_ref, *rest):
        others, o_ref = rest[:n], rest[n]
        me = jnp.zeros((tb, c), jnp.int32) + p_ref[0]
        acc = None
        for q in range(n):
            term = jnp.where(me == q, mine_ref[...], others[q][...]).astype(F32)
            acc = term if acc is None else acc + term
        o_ref[...] = acc

    def other(q):
        return pl.BlockSpec((None, tb, c), lambda i, p: (jnp.where(p[0] == q, (q + 1) % n, q), i, 0))

    return pl.pallas_call(
        body, name=name,
        grid_spec=pltpu.PrefetchScalarGridSpec(
            num_scalar_prefetch=1, grid=(nb,),
            in_specs=[pl.BlockSpec((None, tb, c), lambda i, p: (p[0], i, 0))] + [other(q) for q in range(n)],
            out_specs=pl.BlockSpec((tb, c), lambda i, p: (p[1] * nb + i, 0))),
        out_shape=jax.ShapeDtypeStruct((2 * h, c), F32), compiler_params=_cp(),
    )(place, parts, *([by_chip] * n))


def _adamw_math(w, g, m, v):
    m = ADAM_B1 * m + (1.0 - ADAM_B1) * g
    v = ADAM_B2 * v + (1.0 - ADAM_B2) * (g * g)
    m_hat = m / (1.0 - ADAM_B1 ** ADAM_STEP)
    v_hat = v / (1.0 - ADAM_B2 ** ADAM_STEP)
    delta = -ADAM_LR * (m_hat / (jnp.sqrt(v_hat) + ADAM_EPS) + ADAM_WD * w)
    return delta, m, v


def _adamw(w, g, m, v, name):
    r, c = w.shape
    tb = _pick(r, (128, 496))

    def body(w_ref, g_ref, m_ref, v_ref, go_ref, d_ref, mo_ref, vo_ref):
        gv = g_ref[...]
        d, mn, vn = _adamw_math(w_ref[...], gv, m_ref[...], v_ref[...])
        go_ref[...] = gv
        d_ref[...] = d
        mo_ref[...] = mn
        vo_ref[...] = vn

    blk = pl.BlockSpec((tb, c), lambda i: (i, 0))
    return pl.pallas_call(
        body, name=name, grid=(r // tb,), in_specs=[blk] * 4, out_specs=[blk] * 4,
        out_shape=[jax.ShapeDtypeStruct((r, c), F32)] * 4, compiler_params=_cp(),
    )(w, g, m, v)


def _adamw_window(w, g_win, m, v, shard, name):
    r, c = w.shape
    tb = _pick(r, (128,))

    def body(s_ref, w_ref, g_ref, m_ref, v_ref, go_ref, d_ref, mo_ref, vo_ref, g_scr):
        gw = g_ref[...]
        for s in range(N_SHARD):
            @pl.when(s_ref[0] == s)
            def _():
                g_scr[...] = _from_window(gw, s)

        gv = g_scr[:, :c]
        d, mn, vn = _adamw_math(w_ref[...], gv, m_ref[...], v_ref[...])
        go_ref[...] = gv
        d_ref[...] = d
        mo_ref[...] = mn
        vo_ref[...] = vn

    blk = pl.BlockSpec((tb, c), lambda i, s: (i, 0))
    return pl.pallas_call(
        body, name=name,
        grid_spec=pltpu.PrefetchScalarGridSpec(
            num_scalar_prefetch=1, grid=(r // tb,),
            in_specs=[blk, pl.BlockSpec((tb, SHARD_PAD), lambda i, s: (i, 0)), blk, blk], out_specs=[blk] * 4,
            scratch_shapes=[pltpu.VMEM((tb, SHARD_PAD), F32)]),
        out_shape=[jax.ShapeDtypeStruct((r, c), F32)] * 4, compiler_params=_cp(),
    )(shard, w, g_win, m, v)


def _small_update(gathered, w, m, v):
    def body(p_ref, w_ref, m_ref, v_ref, g_ref, d_ref, mo_ref, vo_ref):
        g = p_ref[0]
        for i in range(1, N_DEV):
            g = g + p_ref[i]
        d, mn, vn = _adamw_math(w_ref[...], g, m_ref[...], v_ref[...])
        g_ref[...] = g
        d_ref[...] = d
        mo_ref[...] = mn
        vo_ref[...] = vn

    full = pl.BlockSpec((32, 128), lambda i: (0, 0))
    return pl.pallas_call(
        body, name="small_update", grid=(1,),
        in_specs=[pl.BlockSpec((N_DEV, 32, 128), lambda i: (0, 0, 0)), full, full, full], out_specs=[full] * 4,
        out_shape=[jax.ShapeDtypeStruct((32, 128), F32)] * 4, compiler_params=_cp(),
    )(gathered, w, m, v)


_ANY = pl.BlockSpec(memory_space=pl.ANY)


def _place():
    x, y, c = lax.axis_index("x"), lax.axis_index("y"), lax.axis_index("c")
    chips = [(1 - x, y), (x, 1 - y), (1 - x, 1 - y)]
    return x, y, c, chips


def _gather_shards(arrs):
    n = len(arrs)

    def body(*refs):
        ins, outs = refs[:n], refs[n:2 * n]
        send_sems, recv_sems, local_sems = refs[2 * n:2 * n + 3]
        bufs = refs[2 * n + 3:]
        x, y, c, chips = _place()
        me = 2 * x + y
        sibling = (x, y, 1 - c)
        sends = []
        for a in range(n):
            half = ins[a].shape[0] // 2
            mine = pl.ds(pl.multiple_of(c * half, 16), half)
            for j, (qx, qy) in enumerate(chips):
                cp = pltpu.make_async_remote_copy(
                    src_ref=ins[a].at[mine], dst_ref=outs[a].at[me, mine],
                    send_sem=send_sems.at[6 * a + j], recv_sem=recv_sems.at[6 * a + j],
                    device_id=(qx, qy, c), device_id_type=MESH)
                cp.start()
                sends.append(cp)
        for a in range(n):
            step = bufs[a].shape[0]
            for r0 in range(0, ins[a].shape[0], step):
                rows = pl.ds(r0, step)
                load = pltpu.make_async_copy(ins[a].at[rows], bufs[a], local_sems.at[2 * a])
                load.start()
                load.wait()
                store = pltpu.make_async_copy(bufs[a], outs[a].at[me, rows], local_sems.at[2 * a + 1])
                store.start()
                store.wait()
        for a in range(n):
            half = ins[a].shape[0] // 2
            mine = pl.ds(pl.multiple_of(c * half, 16), half)
            for j, (qx, qy) in enumerate(chips):
                q = 2 * qx + qy
                landed = outs[a].at[q, mine]
                pltpu.make_async_remote_copy(
                    src_ref=landed, dst_ref=landed, send_sem=send_sems.at[6 * a + j], recv_sem=recv_sems.at[6 * a + j],
                    device_id=(qx, qy, c), device_id_type=MESH).wait_recv()
                fw = pltpu.make_async_remote_copy(
                    src_ref=landed, dst_ref=landed, send_sem=send_sems.at[6 * a + 3 + j],
                    recv_sem=recv_sems.at[6 * a + 3 + j], device_id=sibling, device_id_type=MESH)
                fw.start()
                sends.append(fw)
        for a in range(n):
            half = ins[a].shape[0] // 2
            theirs = pl.ds(pl.multiple_of((1 - c) * half, 16), half)
            for j, (qx, qy) in enumerate(chips):
                q = 2 * qx + qy
                dst = outs[a].at[q, theirs]
                pltpu.make_async_remote_copy(
                    src_ref=dst, dst_ref=dst, send_sem=send_sems.at[6 * a + 3 + j], recv_sem=recv_sems.at[6 * a + 3 + j],
                    device_id=sibling, device_id_type=MESH).wait_recv()
        for cp in sends:
            cp.wait_send()

    return pl.pallas_call(
        body, name="gather_shards", in_specs=[_ANY] * n, out_specs=[_ANY] * n,
        out_shape=[jax.ShapeDtypeStruct((N_SHARD,) + a.shape, a.dtype) for a in arrs],
        scratch_shapes=[pltpu.SemaphoreType.DMA((6 * n,)), pltpu.SemaphoreType.DMA((6 * n,)),
                        pltpu.SemaphoreType.DMA((2 * n,))]
        + [pltpu.VMEM((_pick(a.shape[0], (256, 496)), a.shape[1]), a.dtype) for a in arrs],
        compiler_params=pltpu.CompilerParams(has_side_effects=True, vmem_limit_bytes=VMEM_LIMIT),
    )(*arrs)


def _pair_reduce_send(grads):
    n = len(grads)

    def body(*refs):
        ins, outs = refs[:n], refs[n:2 * n]
        send_sems, recv_sems = refs[2 * n:]
        x, y, c, _ = _place()
        sibling = (x, y, 1 - c)
        cps = []
        for a in range(n):
            half = ins[a].shape[1] // 2
            theirs = pl.ds(pl.multiple_of((1 - c) * half, 8), half)
            cp = pltpu.make_async_remote_copy(
                src_ref=ins[a].at[:, theirs], dst_ref=outs[a], send_sem=send_sems.at[a], recv_sem=recv_sems.at[a],
                device_id=sibling, device_id_type=MESH)
            cp.start()
            cps.append(cp)
        for cp in cps:
            cp.wait()

    return pl.pallas_call(
        body, name="pair_reduce_send", in_specs=[_ANY] * n, out_specs=[_ANY] * n,
        out_shape=[jax.ShapeDtypeStruct((g.shape[0], g.shape[1] // 2, g.shape[2]), g.dtype) for g in grads],
        scratch_shapes=[pltpu.SemaphoreType.DMA((n,)), pltpu.SemaphoreType.DMA((n,))],
        compiler_params=pltpu.CompilerParams(has_side_effects=True),
    )(*grads)


_HBM = pl.BlockSpec(memory_space=pltpu.HBM)
_SEM = pl.BlockSpec(memory_space=pltpu.SEMAPHORE)
_DATAFLOW = pltpu.SideEffectType.DATAFLOW_SIDE_EFFECTING


def _chip_exchange_copies(ins, lands, send_sems, recv_sems):
    x, y, c, chips = _place()
    me = 2 * x + y
    cps = []
    for a in range(len(ins)):
        for j, (qx, qy) in enumerate(chips):
            cps.append(pltpu.make_async_remote_copy(
                src_ref=ins[a].at[2 * qx + qy], dst_ref=lands[a].at[me], send_sem=send_sems.at[3 * a + j],
                recv_sem=recv_sems.at[3 * a + j], device_id=(qx, qy, c), device_id_type=MESH))
    return cps


def _chip_exchange_start(parts):
    n = len(parts)

    def body(*refs):
        ins, lands = refs[:n], refs[n:2 * n]
        send_sems, recv_sems = refs[2 * n:2 * n + 2]
        token = refs[4 * n + 2]
        for cp in _chip_exchange_copies(ins, lands, send_sems, recv_sems):
            cp.start()
        token[...] = jnp.zeros_like(token)

    hbm = [pltpu.HBM(p.shape, p.dtype) for p in parts]
    lands = [pltpu.with_memory_space_constraint(lax.empty(p.shape, p.dtype), pltpu.HBM) for p in parts]
    res = pl.pallas_call(
        body, name="chip_exchange_start",
        out_shape=(pltpu.SemaphoreType.DMA((3 * n,)), pltpu.SemaphoreType.DMA((3 * n,)), *hbm, *hbm,
                   jax.ShapeDtypeStruct((8, 128), F32)),
        in_specs=[_HBM] * (2 * n), out_specs=(_SEM, _SEM, *([_HBM] * (2 * n)), pl.BlockSpec(memory_space=pltpu.VMEM)),
        input_output_aliases={a: 2 + a for a in range(2 * n)},
        compiler_params=pltpu.CompilerParams(has_side_effects=_DATAFLOW),
    )(*[pltpu.with_memory_space_constraint(p, pltpu.HBM) for p in parts], *lands)
    return res[0], res[1], res[2:2 + n], res[2 + n:2 + 2 * n], res[2 + 2 * n]


def _chip_exchange_wait(send_sems, recv_sems, parts, lands, after):
    n = len(parts)

    def body(*refs):
        ins, land_refs = refs[:n], refs[n:2 * n]
        s_sems, r_sems = refs[2 * n:2 * n + 2]
        for cp in _chip_exchange_copies(ins, land_refs, s_sems, r_sems):
            cp.wait_send()
            cp.wait_recv()

    hbm = [pltpu.HBM(p.shape, p.dtype) for p in parts]
    res = pl.pallas_call(
        body, name="chip_exchange_wait", out_shape=(*hbm, *hbm),
        in_specs=[_HBM] * (2 * n) + [_SEM, _SEM, _ANY], out_specs=tuple([_HBM] * (2 * n)),
        input_output_aliases={a: a for a in range(2 * n)},
        compiler_params=pltpu.CompilerParams(has_side_effects=_DATAFLOW),
    )(*parts, *lands, send_sems, recv_sems, after)
    return res[:n], res[n:]


def _pair_allgather(fulls):
    n = len(fulls)

    def body(*refs):
        outs = refs[n:2 * n]
        send_sems, recv_sems = refs[2 * n:]
        x, y, c, _ = _place()
        sibling = (x, y, 1 - c)
        cps = []
        for a in range(n):
            half = outs[a].shape[0] // 2
            mine = outs[a].at[pl.ds(pl.multiple_of(c * half, 8), half)]
            cp = pltpu.make_async_remote_copy(
                src_ref=mine, dst_ref=mine, send_sem=send_sems.at[a], recv_sem=recv_sems.at[a],
                device_id=sibling, device_id_type=MESH)
            cp.start()
            cps.append(cp)
        for a in range(n):
            half = outs[a].shape[0] // 2
            theirs = outs[a].at[pl.ds(pl.multiple_of((1 - c) * half, 8), half)]
            pltpu.make_async_remote_copy(
                src_ref=theirs, dst_ref=theirs, send_sem=send_sems.at[a], recv_sem=recv_sems.at[a],
                device_id=sibling, device_id_type=MESH).wait_recv()
        for cp in cps:
            cp.wait_send()

    return pl.pallas_call(
        body, name="pair_allgather", in_specs=[_ANY] * n, out_specs=[_ANY] * n,
        out_shape=[jax.ShapeDtypeStruct(f.shape, f.dtype) for f in fulls],
        input_output_aliases={a: a for a in range(n)},
        scratch_shapes=[pltpu.SemaphoreType.DMA((n,)), pltpu.SemaphoreType.DMA((n,))],
        compiler_params=pltpu.CompilerParams(has_side_effects=True),
    )(*fulls)


def _allgather_small(slab):
    def body(s_ref, out_ref, send_sems, recv_sems):
        x, y, c, _ = _place()
        me = 4 * x + 2 * y + c
        out_ref[me] = s_ref[...]
        cps = []
        for mask in range(1, N_DEV):
            peer = (x ^ (mask >> 2), y ^ ((mask >> 1) & 1), c ^ (mask & 1))
            cp = pltpu.make_async_remote_copy(
                src_ref=s_ref, dst_ref=out_ref.at[me], send_sem=send_sems.at[mask - 1], recv_sem=recv_sems.at[mask - 1],
                device_id=peer, device_id_type=MESH)
            cp.start()
            cps.append(cp)
        for mask in range(1, N_DEV):
            peer = (x ^ (mask >> 2), y ^ ((mask >> 1) & 1), c ^ (mask & 1))
            dst = out_ref.at[4 * peer[0] + 2 * peer[1] + peer[2]]
            pltpu.make_async_remote_copy(
                src_ref=dst, dst_ref=dst, send_sem=send_sems.at[mask - 1], recv_sem=recv_sems.at[mask - 1],
                device_id=peer, device_id_type=MESH).wait_recv()
        for cp in cps:
            cp.wait_send()

    vm = pl.BlockSpec(memory_space=pltpu.VMEM)
    return pl.pallas_call(
        body, name="allgather_small", in_specs=[vm], out_specs=vm,
        out_shape=jax.ShapeDtypeStruct((N_DEV,) + slab.shape, slab.dtype),
        scratch_shapes=[pltpu.SemaphoreType.DMA((N_DEV - 1,)), pltpu.SemaphoreType.DMA((N_DEV - 1,))],
        compiler_params=pltpu.CompilerParams(has_side_effects=True),
    )(slab)


def _pack_b(w_mem_kv, w_br_dn, w_br_sb, w_br_mem, w_out, conv_w):
    conv = jnp.pad(conv_w.reshape(3, D_MODEL), ((0, B_ROWS - B_CONV - 3), (0, 0)))
    return jnp.concatenate([w_mem_kv.reshape(128, D_MODEL), w_br_dn, w_br_sb, w_br_mem.reshape(64, D_MODEL), w_out,
                            conv], axis=0)


def _unpack_b(slab):
    return (slab[B_MEMKV:B_BRDN].reshape(1, 256, 512), slab[B_BRDN:B_BRSB].reshape(1, 256, D_MODEL),
            slab[B_BRSB:B_BRMEM].reshape(1, 256, D_MODEL), slab[B_BRMEM:B_OUT].reshape(1, 256, 256),
            slab[B_OUT:B_CONV].reshape(1, 256, D_MODEL), slab[B_CONV:B_CONV + 3].reshape(1, 4, 768))


def _pack_small(norm_g, mem_norm_g, final_g, dn_norm_g, a_log, dt_bias, loss=None):
    slab = jnp.zeros((32, 128), F32)
    slab = slab.at[S_NORM:S_NORM + 8].set(norm_g.reshape(8, 128))
    slab = slab.at[S_MEMNORM:S_MEMNORM + 8].set(mem_norm_g.reshape(8, 128))
    slab = slab.at[S_FINAL:S_FINAL + 8].set(final_g.reshape(8, 128))
    slab = slab.at[S_DNNORM].set(dn_norm_g.reshape(128))
    slab = slab.at[S_ALOG, :N_HEADS].set(a_log.reshape(N_HEADS))
    slab = slab.at[S_DTB, :N_HEADS].set(dt_bias.reshape(N_HEADS))
    if loss is not None:
        slab = slab.at[S_LOSS, 0].set(loss)
    return slab


def _unpack_small(slab):
    return (slab[S_NORM:S_NORM + 8].reshape(1, D_MODEL), slab[S_MEMNORM:S_MEMNORM + 8].reshape(1, D_MODEL),
            slab[S_FINAL:S_FINAL + 8].reshape(D_MODEL), slab[S_DNNORM].reshape(1, 128),
            slab[S_ALOG, :N_HEADS].reshape(1, N_HEADS), slab[S_DTB, :N_HEADS].reshape(1, N_HEADS))


def _reorder_w_in(w_full):
    pad = jnp.zeros((w_full.shape[0], W_R - IN_WIDTH), w_full.dtype)
    return jnp.concatenate([w_full[:, :4096], w_full[:, 4112:], w_full[:, 4096:4112], pad], axis=1)


def _windows_to_w_r(win):
    b = 128
    s0, s1, s2, s3 = win[0], win[1], win[2], win[3]
    e1, e2, e3 = WIN_START[1] * b, WIN_START[2] * b, WIN_START[3] * b
    n1, n2 = e2 - e1, e3 - e2
    return jnp.concatenate([
        s0[:, :e1], s0[:, e1:e1 + b] + s1[:, :b],
        s1[:, b:n1], s1[:, n1:n1 + b] + s2[:, :b],
        s2[:, b:n2], s2[:, n2:n2 + b] + s3[:, :b],
        s3[:, b:], s1[:, _S1_BA_POS:]], axis=1)


def _dproj_windows(dproj_r):
    b = 128
    pieces = []
    for s in range(N_SHARD):
        lo = WIN_START[s] * b
        if s == 1:
            pieces += [dproj_r[:, lo:lo + _S1_BA_POS], dproj_r[:, C_BA:C_BA + b]]
        else:
            pieces.append(dproj_r[:, lo:lo + SHARD_PAD])
    return jnp.concatenate(pieces, axis=1)


def _local_step(x, mem, tgt, norm_g, mem_norm_g, w_r, w_sh, conv_w, a_log, dt_bias, dn_norm_g, w_mem_kv, w_br_dn,
                w_br_sb, w_br_mem, w_out, final_g):
    t = x.shape[0]
    final_row = final_g.reshape(1, D_MODEL)
    alog_row = jnp.zeros((1, 128), F32).at[0, N_HEADS:2 * N_HEADS].set(a_log.reshape(N_HEADS))
    dtb_row = jnp.zeros((1, 128), F32).at[0, N_HEADS:2 * N_HEADS].set(dt_bias.reshape(N_HEADS))

    h = _rmsnorm_fwd(x, norm_g, "norm_fwd")
    proj = _mm(h, w_r, "nn", "in_proj")
    qkv = _dn_prep_fwd(proj, conv_w)
    beta_t, g_t = _dn_gate_fwd(proj, alog_row, dtb_row)
    dn_u, dn_w, dn_qd, dn_kd, dn_a, tinv_all, dn_el = _dn_intra_fwd(qkv, beta_t, g_t)
    o_dn, dn_vn, s_all = _dn_scan_fwd(dn_u, dn_w, dn_qd, dn_kd, dn_a, dn_el)
    o_dn_g = _dn_post_fwd(o_dn, proj, dn_norm_g)
    o_sb, o_sb_g, sb_l = _sb_fwd(proj)
    mem_n = _rmsnorm_fwd(mem, mem_norm_g, "mem_norm_fwd")
    mkv = _mm(mem_n, w_mem_kv, "nn", "mem_kv")
    o_m, o_m_g = _mem_fwd(proj, mkv)
    y_dn = _mm(o_dn_g, w_br_dn, "nn", "br_dn")
    y_sb = _mm(o_sb_g, w_br_sb, "nn", "br_sb")
    y_m = _mm(o_m_g, w_br_mem, "nn", "br_mem")
    merged = _merge_fwd(proj, y_dn, y_sb, y_m)
    mo = _mm(merged, w_out, "nn", "out_proj")
    d_out, d_out_b, loss_row, g_final = _final_loss(x, mo, final_row, tgt)

    g_w_out = _mm(merged, d_out_b, "tn", "g_w_out")
    d_merged = _mm(d_out_b, w_out, "nt", "d_merged")
    dy_dn, dy_sb, dy_m, dg1, dg2, dg3 = _merge_bwd(proj, y_dn, y_sb, y_m, d_merged)
    g_w_br_dn = _mm(o_dn_g, dy_dn, "tn", "g_w_br_dn")
    g_w_br_sb = _mm(o_sb_g, dy_sb, "tn", "g_w_br_sb")
    g_w_br_mem = _mm(o_m_g, dy_m, "tn", "g_w_br_mem")
    d_o_dn_g = _mm(dy_dn, w_br_dn, "nt", "d_o_dn")
    d_o_sb_g = _mm(dy_sb, w_br_sb, "nt", "d_o_sb")
    d_o_m_g = _mm(dy_m, w_br_mem, "nt", "d_o_mem")

    d_mq, d_mz, d_mkv = _mem_bwd(proj, mkv, o_m, d_o_m_g)
    d_mkv_b = _cast_bf16(d_mkv, "cast_dmkv")
    g_w_mem_kv = _mm(mem_n, d_mkv_b, "tn", "g_w_mem_kv")
    d_mem_n = _mm(d_mkv_b, w_mem_kv, "nt", "d_mem_n")
    _, g_mem_norm = _rmsnorm_bwd(mem, mem_norm_g, d_mem_n, jnp.zeros_like(mem), "mem_norm_bwd")

    d_sq, d_sk, d_sv, d_sz = _sb_bwd(proj, o_sb, sb_l, d_o_sb_g)

    d_o_dn, d_dnz, g_dn_norm = _dn_post_bwd(o_dn, proj, dn_norm_g, d_o_dn_g)
    d_vnew, d_kd, d_qd, d_w, d_el = _dn_scan_bwd(dn_w, dn_qd, dn_kd, dn_a, dn_el, dn_vn, s_all, d_o_dn)
    d_qn, d_kn, d_vn, dbeta_t, dg_t = _dn_intra_bwd(qkv, beta_t, g_t, tinv_all, dn_vn, d_o_dn, d_vnew, d_kd, d_qd, d_w, d_el)
    d_conv_in, g_conv = _dn_prep_bwd(proj, conv_w, d_qn, d_kn, d_vn)
    d_ba, g_alog_row, g_dtb_row = _dn_gate_bwd(proj, alog_row, dtb_row, dbeta_t, dg_t)

    dproj_sh = _dproj_windows(
        jnp.concatenate([d_conv_in, d_dnz, d_sq, d_sk, d_sv, d_sz, d_mq, d_mz, dg1, dg2, dg3, d_ba], axis=1))
    g_w_sh = _mm(h, dproj_sh, "tn", "g_w_in", out_dtype=BF16, out_shards=N_SHARD)
    big = dict(w_sh=g_w_sh, conv_w=g_conv, w_mem_kv=g_w_mem_kv, w_br_dn=g_w_br_dn, w_br_sb=g_w_br_sb,
               w_br_mem=g_w_br_mem, w_out=g_w_out)

    def input_grad(after=None):
        dh = _mm(dproj_sh, w_sh, "nt", "d_h", after=after)
        grad_x, g_norm = _rmsnorm_bwd(x, norm_g, dh, d_out, "norm_bwd")
        small = dict(norm_g=g_norm, mem_norm_g=g_mem_norm, final_g=g_final, dn_norm_g=g_dn_norm,
                     a_log=g_alog_row[:, N_HEADS:2 * N_HEADS], dt_bias=g_dtb_row[:, N_HEADS:2 * N_HEADS])
        return grad_x, small

    return loss_row[0, 0], big, input_grad


def _reduce_scatter_start(grads):
    c = lax.axis_index("c")
    core = jnp.reshape(c, (1,)).astype(jnp.int32)
    recv = _pair_reduce_send(grads)
    parts = [_pair_add(g, r, core, "pair_add") for g, r in zip(grads, recv)]
    return _chip_exchange_start(parts)


def _reduce_scatter_finish(handle, after):
    send_sems, recv_sems, parts, lands, _ = handle
    x, y, c = lax.axis_index("x"), lax.axis_index("y"), lax.axis_index("c")
    place = jnp.stack([2 * x + y, c]).astype(jnp.int32)
    parts, by_chip = _chip_exchange_wait(send_sems, recv_sems, parts, lands, after)
    fulls = [_chip_sum(p, b, place, "chip_sum") for p, b in zip(parts, by_chip)]
    return _pair_allgather(fulls)


def kernel(x, mem, norm_g, mem_norm_g, w_in, conv_w, a_log, dt_bias, dn_norm_g, w_mem_kv, w_br_dn, w_br_sb, w_br_mem, w_out, final_g, loss_target, m_norm_g, m_mem_norm_g, m_w_in, m_conv_w, m_a_log, m_dt_bias, m_dn_norm_g, m_w_mem_kv, m_w_br_dn, m_w_br_sb, m_w_br_mem, m_w_out, m_final_g, v_norm_g, v_mem_norm_g, v_w_in, v_conv_w, v_a_log, v_dt_bias, v_dn_norm_g, v_w_mem_kv, v_w_br_dn, v_w_br_sb, v_w_br_mem, v_w_out, v_final_g):
    w_a = w_in[0]
    w_b = _pack_b(w_mem_kv[0], w_br_dn[0], w_br_sb[0], w_br_mem[0], w_out[0], conv_w[0])
    m_b = _pack_b(m_w_mem_kv[0], m_w_br_dn[0], m_w_br_sb[0], m_w_br_mem[0], m_w_out[0], m_conv_w[0])
    v_b = _pack_b(v_w_mem_kv[0], v_w_br_dn[0], v_w_br_sb[0], v_w_br_mem[0], v_w_out[0], v_conv_w[0])

    shard = jnp.reshape(2 * lax.axis_index("x") + lax.axis_index("y"), (1,)).astype(jnp.int32)
    ga, gb = _gather_shards([_cast_to_window(w_a, shard, "cast_w_in"), _cast_bf16(w_b, "cast_w_b")])
    w_r = _windows_to_w_r(ga)
    f_mem_kv = gb[:, B_MEMKV:B_BRDN].reshape(N_SHARD * 256, 512)
    f_br_dn = gb[:, B_BRDN:B_BRSB].reshape(N_SHARD * 256, D_MODEL)
    f_br_sb = gb[:, B_BRSB:B_BRMEM].reshape(N_SHARD * 256, D_MODEL)
    f_br_mem = gb[:, B_BRMEM:B_OUT].reshape(N_SHARD, 256, 256).transpose(1, 0, 2).reshape(256, D_MODEL)
    f_out = gb[:, B_OUT:B_CONV].reshape(N_SHARD * 256, D_MODEL)
    f_conv = gb[:, B_CONV:B_CONV + 3].reshape(N_SHARD, 4, 768).transpose(1, 0, 2).reshape(4, 3 * D_MODEL).astype(F32)

    loss, big, input_grad = _local_step(
        x[0], mem[0], loss_target[0], norm_g, mem_norm_g, w_r, ga, f_conv, a_log, dt_bias, dn_norm_g,
        f_mem_kv, f_br_dn, f_br_sb, f_br_mem, f_out, final_g)

    g_b = jnp.stack([
        _pack_b(big["w_mem_kv"][256 * s:256 * (s + 1)], big["w_br_dn"][256 * s:256 * (s + 1)],
                big["w_br_sb"][256 * s:256 * (s + 1)], big["w_br_mem"][:, 256 * s:256 * (s + 1)],
                big["w_out"][256 * s:256 * (s + 1)], big["conv_w"][:, 768 * s:768 * (s + 1)])
        for s in range(N_SHARD)]).astype(BF16)
    in_flight = _reduce_scatter_start([big["w_sh"], g_b])
    grad_x, small = input_grad(after=in_flight[4])

    part = _pack_small(small["norm_g"], small["mem_norm_g"], small["final_g"], small["dn_norm_g"],
                       small["a_log"], small["dt_bias"], loss)
    w_s = _pack_small(norm_g, mem_norm_g, final_g, dn_norm_g, a_log, dt_bias)
    m_s = _pack_small(m_norm_g, m_mem_norm_g, m_final_g, m_dn_norm_g, m_a_log, m_dt_bias)
    v_s = _pack_small(v_norm_g, v_mem_norm_g, v_final_g, v_dn_norm_g, v_a_log, v_dt_bias)
    g_s, d_s, nm_s, nv_s = _small_update(_allgather_small(part), w_s, m_s, v_s)

    gs_in, gs_b = _reduce_scatter_finish(in_flight, after=g_s)
    gr_in, d_in, nm_in, nv_in = _adamw_window(w_a, gs_in, m_w_in[0], v_w_in[0], shard, "adamw_w_in")
    gr_b, d_b, nm_b, nv_b = _adamw(w_b, gs_b, m_b, v_b, "adamw_b")

    def assemble(slab_small, a_in, slab_b):
        s_norm, s_memnorm, s_final, s_dnnorm, s_alog, s_dtb = _unpack_small(slab_small)
        b_memkv, b_brdn, b_brsb, b_brmem, b_out, b_conv = _unpack_b(slab_b)
        return [s_norm, s_memnorm, a_in.reshape(1, D_MODEL, IN_WIDTH // N_SHARD), b_conv, s_alog, s_dtb, s_dnnorm,
                b_memkv, b_brdn, b_brsb, b_brmem, b_out, s_final]

    outs = [g_s[S_LOSS, 0], grad_x.reshape(1, -1, D_MODEL)]
    outs += assemble(g_s, gr_in, gr_b)
    outs += assemble(d_s, d_in, d_b)
    outs += assemble(nm_s, nm_in, nm_b)
    outs += assemble(nv_s, nv_in, nv_b)
    return tuple(outs)
```

```python
import functools
import math

import jax
import jax.numpy as jnp
from jax import lax
from jax.experimental import pallas as pl
from jax.experimental.pallas import tpu as pltpu

F32 = jnp.float32
BF16 = jnp.bfloat16
MESH = pl.DeviceIdType.MESH
HIGHEST = lax.Precision.HIGHEST

D_MODEL = 1024
N_HEADS = 8
D_HEAD = 128
DN_CHUNK = 64
DN_GROUP = 8
DN_SCAN_GROUP = 4
SB_BLOCK = 256
SB_HEADS_PER_STEP = 2
SB_QBLOCK = 256
MEM_HEADS = 4
MEM_DH = 64
MEM_W = MEM_HEADS * MEM_DH
NORM_EPS = 1e-6
IN_WIDTH = 11792
N_SHARD = 4
SHARD_W = IN_WIDTH // N_SHARD
SHARD_PAD = 3072
N_DEV = 8

C_DNZ = 3072
C_SBQ = 4096
C_SBZ = 7168
C_MQ = 8192
C_MZ = 8448
C_GATES = 8704
C_BA = 11776
W_R = 11904

ADAM_LR = 0.001
ADAM_B1 = 0.9
ADAM_B2 = 0.999
ADAM_EPS = 1e-08
ADAM_WD = 0.01
ADAM_STEP = 10

VMEM_LIMIT = 56 * 1024 * 1024

B_ROWS = 992
B_MEMKV, B_BRDN, B_BRSB, B_BRMEM, B_OUT, B_CONV = 0, 128, 384, 640, 704, 960
S_NORM, S_MEMNORM, S_FINAL, S_DNNORM, S_ALOG, S_DTB, S_LOSS = 0, 8, 16, 24, 25, 26, 27


def _cp(**kw):
    return pltpu.CompilerParams(vmem_limit_bytes=VMEM_LIMIT, **kw)


def _dot(a, b, dims):
    lead = a.ndim - 2
    ca, cb = {"nn": (1, 0), "nt": (1, 1), "tn": (0, 0)}[dims]
    batch = tuple(range(lead))
    return lax.dot_general(a, b, (((ca + lead,), (cb + lead,)), (batch, batch)), preferred_element_type=F32)


def _chunks(x):
    return x.reshape(x.shape[0] // DN_CHUNK, DN_CHUNK, x.shape[1])


def _unchunk(x):
    return x.reshape(x.shape[0] * x.shape[1], x.shape[2])


def _bdot(a, b, dims):
    return _dot(a.astype(BF16), b.astype(BF16), dims)


def _split(a):
    hi = a.astype(BF16)
    return hi, (a - hi.astype(F32)).astype(BF16)


def _dot3(a, b, dims):
    a1, a2 = _split(a)
    b1, b2 = _split(b)
    return _dot(a1, b1, dims) + (_dot(a1, b2, dims) + _dot(a2, b1, dims))


def _ones_dot(a, ones_bf16):
    out = _dot(a.reshape(-1, a.shape[-1]).astype(BF16), ones_bf16, "nn")
    return out.reshape(a.shape[:-1] + (ones_bf16.shape[1],))


def _sigmoid(x):
    return 1.0 / (1.0 + jnp.exp(-x))


def _log1p_small(u):
    return jnp.where(u < 1e-2, u * (1.0 - u * (0.5 - u * (1.0 / 3.0))), jnp.log(1.0 + u))


def _log_sigmoid(z):
    return jnp.minimum(z, 0.0) - _log1p_small(jnp.exp(-jnp.abs(z)))


def _pick(dim, cands):
    for c in cands:
        if dim % c == 0:
            return c
    return dim


def _mm(a, b, dims, name, out_dtype=F32, out_shards=1, after=None):
    ta, tb = dims[0] == "t", dims[1] == "t"
    m, k = (a.shape[1], a.shape[0]) if ta else a.shape
    b_shards = b.shape[0] if b.ndim == 3 else 1
    n = b.shape[-2] if tb else b.shape[-1]
    tm = _pick(m, (1024, 512, 256))
    tn = _pick(n // out_shards, (512, 384, 256, 128))
    tk = _pick(k // b_shards, (1024, 512, 384, 256))
    nk = k // tk

    def body(a_ref, b_ref, *rest):
        o_ref, acc_ref = rest[-2:]
        kk = pl.program_id(2)

        @pl.when(kk == 0)
        def _():
            acc_ref[...] = jnp.zeros_like(acc_ref)

        acc_ref[...] += _bdot(a_ref[...], b_ref[...], dims)

        @pl.when(kk == nk - 1)
        def _():
            o_ref[...] = acc_ref[...].astype(out_dtype)

    a_spec = pl.BlockSpec((tk, tm), lambda i, j, q: (q, i)) if ta else pl.BlockSpec((tm, tk), lambda i, j, q: (i, q))
    if b_shards > 1:
        per_k = k // b_shards // tk
        b_spec = pl.BlockSpec((None, tn, tk), lambda i, j, q: (q // per_k, j, q % per_k))
    else:
        b_spec = pl.BlockSpec((tn, tk), lambda i, j, q: (j, q)) if tb else pl.BlockSpec((tk, tn), lambda i, j, q: (q, j))
    if out_shards > 1:
        per_n = n // out_shards // tn
        out_spec = pl.BlockSpec((None, tm, tn), lambda i, j, q: (j // per_n, i, j % per_n))
        out_shape = jax.ShapeDtypeStruct((out_shards, m, n // out_shards), out_dtype)
    else:
        out_spec = pl.BlockSpec((tm, tn), lambda i, j, q: (i, j))
        out_shape = jax.ShapeDtypeStruct((m, n), out_dtype)
    extra_specs, extra = [], []
    if after is not None:
        extra_specs, extra = [pl.BlockSpec(after.shape, lambda i, j, q: (0, 0))], [after]
    return pl.pallas_call(
        body, name=name, grid=(m // tm, n // tn, nk),
        in_specs=[a_spec, b_spec] + extra_specs, out_specs=out_spec, out_shape=out_shape,
        scratch_shapes=[pltpu.VMEM((tm, tn), F32)],
        compiler_params=_cp(dimension_semantics=("parallel", "parallel", "arbitrary")),
    )(a, b, *extra)


def _rmsnorm_fwd(x, g, name):
    t, d = x.shape
    tb = _pick(t, (512, 256))

    def body(x_ref, g_ref, h_ref):
        xv = x_ref[...]
        r = lax.rsqrt(jnp.mean(xv * xv, axis=-1, keepdims=True) + NORM_EPS)
        h_ref[...] = ((xv * r) * g_ref[...]).astype(BF16)

    return pl.pallas_call(
        body, name=name, grid=(t // tb,),
        in_specs=[pl.BlockSpec((tb, d), lambda i: (i, 0)), pl.BlockSpec((1, d), lambda i: (0, 0))],
        out_specs=pl.BlockSpec((tb, d), lambda i: (i, 0)),
        out_shape=jax.ShapeDtypeStruct((t, d), BF16), compiler_params=_cp(),
    )(x, g)


def _rmsnorm_bwd(x, g, dh, resid, name):
    t, d = x.shape
    tb = _pick(t, (256,))

    def body(x_ref, g_ref, dh_ref, r_ref, dx_ref, dg_ref):
        @pl.when(pl.program_id(0) == 0)
        def _():
            dg_ref[...] = jnp.zeros_like(dg_ref)

        xv = x_ref[...]
        r = lax.rsqrt(jnp.mean(xv * xv, axis=-1, keepdims=True) + NORM_EPS)
        xhat = xv * r
        dhv = dh_ref[...]
        dg_ref[...] += jnp.sum(dhv * xhat, axis=0, keepdims=True)
        dxh = dhv * g_ref[...]
        dx_ref[...] = r_ref[...] + r * (dxh - xhat * jnp.mean(dxh * xhat, axis=-1, keepdims=True))

    row = pl.BlockSpec((tb, d), lambda i: (i, 0))
    vec = pl.BlockSpec((1, d), lambda i: (0, 0))
    return pl.pallas_call(
        body, name=name, grid=(t // tb,), in_specs=[row, vec, row, row], out_specs=[row, vec],
        out_shape=[jax.ShapeDtypeStruct((t, d), F32), jax.ShapeDtypeStruct((1, d), F32)], compiler_params=_cp(),
    )(x, g, dh, resid)


def _conv_silu(xv, w, row):
    y = xv * w[3:4, :]
    for s in (1, 2, 3):
        xs = jnp.where(row >= s, pltpu.roll(xv, s, 0), 0.0)
        y = y + xs * w[3 - s:4 - s, :]
    return y, y * _sigmoid(y)


def _dn_prep_fwd(proj, conv_w):
    t = proj.shape[0]

    def body(p_ref, w_ref, o_ref):
        j = pl.program_id(0)
        xv = p_ref[...]
        row = lax.broadcasted_iota(jnp.int32, xv.shape, 0)
        _, a = _conv_silu(xv, w_ref[...], row)
        inv = lax.rsqrt(jnp.sum(a * a, axis=-1, keepdims=True) + NORM_EPS)
        scale = jnp.where(j < N_HEADS, D_HEAD ** -0.5, 1.0)
        normed = jnp.where(j < 2 * N_HEADS, 1.0, 0.0)
        o_ref[...] = a * (normed * (inv * scale) + (1.0 - normed))

    return pl.pallas_call(
        body, name="dn_prep_fwd", grid=(3 * N_HEADS,),
        in_specs=[pl.BlockSpec((t, D_HEAD), lambda j: (0, j)), pl.BlockSpec((4, D_HEAD), lambda j: (0, j))],
        out_specs=pl.BlockSpec((t, D_HEAD), lambda j: (0, j)),
        out_shape=jax.ShapeDtypeStruct((t, 3 * D_MODEL), F32), compiler_params=_cp(),
    )(proj, conv_w)


def _dn_prep_bwd(proj, conv_w, dq, dk, dv):
    t = proj.shape[0]

    def body(p_ref, w_ref, dq_ref, dk_ref, dv_ref, dp_ref, dw_ref):
        j = pl.program_id(0)
        xv = p_ref[...]
        w = w_ref[...]
        row = lax.broadcasted_iota(jnp.int32, xv.shape, 0)
        y, a = _conv_silu(xv, w, row)
        part = jnp.zeros(xv.shape, jnp.int32) + j // N_HEADS
        dn = jnp.where(part == 0, dq_ref[...], jnp.where(part == 1, dk_ref[...], dv_ref[...]))
        inv = lax.rsqrt(jnp.sum(a * a, axis=-1, keepdims=True) + NORM_EPS)
        scale = jnp.where(j < N_HEADS, D_HEAD ** -0.5, 1.0)
        ds = dn * scale
        da_norm = inv * ds - a * (inv * inv * inv) * jnp.sum(ds * a, axis=-1, keepdims=True)
        normed = jnp.where(j < 2 * N_HEADS, 1.0, 0.0)
        da = normed * da_norm + (1.0 - normed) * dn
        s = _sigmoid(y)
        dy = da * (s * (1.0 + y * (1.0 - s)))
        dx = dy * w[3:4, :]
        dw_ref[3:4, :] = jnp.sum(dy * xv, axis=0, keepdims=True)
        for sft in (1, 2, 3):
            xs = jnp.where(row >= sft, pltpu.roll(xv, sft, 0), 0.0)
            dw_ref[3 - sft:4 - sft, :] = jnp.sum(dy * xs, axis=0, keepdims=True)
            dys = jnp.where(row < t - sft, pltpu.roll(dy, t - sft, 0), 0.0)
            dx = dx + dys * w[3 - sft:4 - sft, :]
        dp_ref[...] = dx.astype(BF16)

    blk = pl.BlockSpec((t, D_HEAD), lambda j: (0, j))
    wblk = pl.BlockSpec((4, D_HEAD), lambda j: (0, j))

    def grad(part):
        return pl.BlockSpec((t, D_HEAD), lambda j: (0, jnp.clip(j - part * N_HEADS, 0, N_HEADS - 1)))

    return pl.pallas_call(
        body, name="dn_prep_bwd", grid=(3 * N_HEADS,), in_specs=[blk, wblk, grad(0), grad(1), grad(2)],
        out_specs=[blk, wblk],
        out_shape=[jax.ShapeDtypeStruct((t, 3 * D_MODEL), BF16), jax.ShapeDtypeStruct((4, 3 * D_MODEL), F32)],
        compiler_params=_cp(),
    )(proj, conv_w, dq, dk, dv)


def _softplus_parts(xv):
    e = jnp.exp(-jnp.abs(xv))
    return jnp.maximum(xv, 0.0) + _log1p_small(e)


def _chunk_scan(v, row, reverse):
    t = v.shape[0]
    pos = row & (DN_CHUNK - 1)
    s = 1
    while s < DN_CHUNK:
        if reverse:
            v = v + jnp.where(pos < DN_CHUNK - s, pltpu.roll(v, t - s, 0), 0.0)
        else:
            v = v + jnp.where(pos >= s, pltpu.roll(v, s, 0), 0.0)
        s *= 2
    return v


def _dn_gate_fwd(proj, alog_row, dtb_row):
    t = proj.shape[0]

    def body(p_ref, al_ref, dt_ref, b_ref, g_ref):
        p = p_ref[...]
        row = lax.broadcasted_iota(jnp.int32, p.shape, 0)
        b_ref[...] = _sigmoid(p)
        g = -jnp.exp(al_ref[...]) * _softplus_parts(p + dt_ref[...])
        g_ref[...] = _chunk_scan(g, row, reverse=False)

    blk = pl.BlockSpec((t, 128), lambda i: (0, C_BA // 128))
    vec = pl.BlockSpec((1, 128), lambda i: (0, 0))
    out = pl.BlockSpec((t, 128), lambda i: (0, 0))
    return pl.pallas_call(
        body, name="dn_gate_fwd", grid=(1,), in_specs=[blk, vec, vec], out_specs=[out, out],
        out_shape=[jax.ShapeDtypeStruct((t, 128), F32)] * 2, compiler_params=_cp(),
    )(proj, alog_row, dtb_row)


def _dn_gate_bwd(proj, alog_row, dtb_row, dbeta, dgc):
    t = proj.shape[0]

    def body(p_ref, al_ref, dt_ref, db_ref, dg_ref, dp_ref, dal_ref, ddt_ref):
        p = p_ref[...]
        row = lax.broadcasted_iota(jnp.int32, p.shape, 0)
        lane = lax.broadcasted_iota(jnp.int32, p.shape, 1)
        s = _sigmoid(p)
        d_b = db_ref[...] * s * (1.0 - s)
        dg = _chunk_scan(dg_ref[...], row, reverse=True)
        xa = p + dt_ref[...]
        ea = jnp.exp(al_ref[...])
        g = -ea * _softplus_parts(xa)
        d_a = dg * (-ea) * _sigmoid(xa)
        dp_ref[...] = jnp.where(lane < N_HEADS, d_b, jnp.where(lane < 2 * N_HEADS, d_a, 0.0)).astype(BF16)
        dal_ref[...] = jnp.sum(dg * g, axis=0, keepdims=True)
        ddt_ref[...] = jnp.sum(d_a, axis=0, keepdims=True)

    blk = pl.BlockSpec((t, 128), lambda i: (0, C_BA // 128))
    vec = pl.BlockSpec((1, 128), lambda i: (0, 0))
    full = pl.BlockSpec((t, 128), lambda i: (0, 0))
    return pl.pallas_call(
        body, name="dn_gate_bwd", grid=(1,), in_specs=[blk, vec, vec, full, full], out_specs=[full, vec, vec],
        out_shape=[jax.ShapeDtypeStruct((t, 128), BF16), jax.ShapeDtypeStruct((1, 128), F32),
                   jax.ShapeDtypeStruct((1, 128), F32)], compiler_params=_cp(),
    )(proj, alog_row, dtb_row, dbeta, dgc)


def _col_to_row(col, eye):
    return jnp.sum(jnp.where(eye, col, 0.0), axis=-2, keepdims=True)


def _row_to_col(rowv, eye):
    return jnp.sum(jnp.where(eye, rowv, 0.0), axis=-1, keepdims=True)


def _tri_inverse(m, ri, ci):
    eye = (ri == ci).astype(F32)
    b16 = (ri >> 4) == (ci >> 4)
    b32 = (ri >> 5) == (ci >> 5)
    m1 = jnp.where(b16, m, 0.0)
    x = eye - m1
    p = _dot3(m1, m1, "nn")
    x = x + _dot3(x, p, "nn")
    p = _dot3(p, p, "nn")
    x = x + _dot3(x, p, "nn")
    p = _dot3(p, p, "nn")
    x = x + _dot3(x, p, "nn")
    c1 = jnp.where(jnp.logical_and(b32, jnp.logical_not(b16)), m, 0.0)
    x = x - _dot3(_dot3(x, c1, "nn"), x, "nn")
    c2 = jnp.where(b32, 0.0, m)
    x = x - _dot3(_dot3(x, c2, "nn"), x, "nn")
    return x


def _dn_chunk_common(q, k, gc, ri, ci):
    eye = ri == ci
    g_row = _col_to_row(gc, eye)
    diff = jnp.minimum(gc - g_row, 0.0)
    gam = jnp.where(ri >= ci, jnp.exp(diff), 0.0)
    kk = _bdot(k, k, "nt")
    qk = _bdot(q, k, "nt")
    rcol = lax.broadcasted_iota(jnp.int32, gc.shape, gc.ndim - 2)
    last = jnp.sum(jnp.where(rcol == DN_CHUNK - 1, gc, 0.0), axis=-2, keepdims=True)
    e_g = jnp.exp(gc)
    dec = jnp.exp(last - gc)
    return eye, gam, kk, qk, last, e_g, dec, rcol


def _dn_specs(t, rows_blk):
    def head(off):
        return pl.BlockSpec((rows_blk, D_HEAD), lambda g, h: (g, off + h))

    lanes = pl.BlockSpec((rows_blk, 128), lambda g, h: (g, 0))
    hm = pl.BlockSpec((None, rows_blk, D_HEAD), lambda g, h: (h, g, 0))
    sq = pl.BlockSpec((1, rows_blk, DN_CHUNK), lambda g, h: (h, g, 0))
    tile = pl.BlockSpec((1, rows_blk // DN_CHUNK, 8, 128), lambda g, h: (h, g, 0, 0))
    return head, lanes, hm, sq, tile


def _head_column(slab, lane_idx):
    lane = lax.broadcasted_iota(jnp.int32, slab.shape, 1)
    return _chunks(jnp.sum(jnp.where(lane == lane_idx, slab, 0.0), axis=1, keepdims=True))


def _dn_intra_fwd(qkv, beta_t, g_t):
    t = qkv.shape[0]
    n_chunks = t // DN_CHUNK
    rows_blk = min(DN_GROUP * DN_CHUNK, t)

    def body(q_ref, k_ref, v_ref, b_ref, g_ref, u_ref, w_ref, qd_ref, kd_ref, a_ref, ti_ref, el_ref):
        ri = lax.broadcasted_iota(jnp.int32, (DN_CHUNK, DN_CHUNK), 0)
        ci = lax.broadcasted_iota(jnp.int32, (DN_CHUNK, DN_CHUNK), 1)
        h = pl.program_id(1)
        q, k, v = (_chunks(r[...]) for r in (q_ref, k_ref, v_ref))
        b, gc = _head_column(b_ref[...], h), _head_column(g_ref[...], h + N_HEADS)
        _, gam, kk, qk, last, e_g, dec, _ = _dn_chunk_common(q, k, gc, ri, ci)
        tinv = _tri_inverse(jnp.where(ri > ci, b * kk * gam, 0.0), ri, ci)
        u_ref[...] = _unchunk(_bdot(tinv, v * b, "nn"))
        w_ref[...] = _unchunk(_bdot(tinv, k * (b * e_g), "nn"))
        qd_ref[...] = _unchunk(q * e_g)
        kd_ref[...] = _unchunk(k * dec)
        a_ref[0] = _unchunk(qk * gam)
        ti_ref[0] = _unchunk(tinv)
        el_ref[0] = jnp.broadcast_to(jnp.exp(last), (rows_blk // DN_CHUNK, 8, 128))

    head, lanes, hm, sq, tile = _dn_specs(t, rows_blk)
    act = jax.ShapeDtypeStruct((N_HEADS, t, D_HEAD), F32)
    sqs = jax.ShapeDtypeStruct((N_HEADS, t, DN_CHUNK), F32)
    return pl.pallas_call(
        body, name="dn_intra_fwd", grid=(t // rows_blk, N_HEADS),
        in_specs=[head(0), head(N_HEADS), head(2 * N_HEADS), lanes, lanes],
        out_specs=[hm] * 4 + [sq, sq, tile],
        out_shape=[act] * 4 + [sqs, sqs, jax.ShapeDtypeStruct((N_HEADS, n_chunks, 8, 128), F32)],
        compiler_params=_cp(),
    )(qkv, qkv, qkv, beta_t, g_t)


def _dn_scan_specs(t, rows_blk, reverse):
    n_groups = t // rows_blk

    def at(g):
        return n_groups - 1 - g if reverse else g

    per = rows_blk // DN_CHUNK
    act = pl.BlockSpec((N_HEADS, rows_blk, D_HEAD), lambda g: (0, at(g), 0))
    sq = pl.BlockSpec((N_HEADS, rows_blk, DN_CHUNK), lambda g: (0, at(g), 0))
    state = pl.BlockSpec((N_HEADS, per, D_HEAD, D_HEAD), lambda g: (0, at(g), 0, 0))
    tile = pl.BlockSpec((N_HEADS, per, 8, 128), lambda g: (0, at(g), 0, 0))
    return act, sq, state, tile


def _dn_scan_fwd(u, w, qd, kd, a, el):
    t = u.shape[1]
    n_chunks = t // DN_CHUNK
    rows_blk = DN_SCAN_GROUP * DN_CHUNK

    def body(u_ref, w_ref, qd_ref, kd_ref, a_ref, el_ref, o_ref, vn_ref, s_ref, s_scr):
        @pl.when(pl.program_id(0) == 0)
        def _():
            s_scr[...] = jnp.zeros_like(s_scr)

        for cc in range(DN_SCAN_GROUP):
            rows = slice(cc * DN_CHUNK, (cc + 1) * DN_CHUNK)
            s = s_scr[...]
            s_ref[:, cc] = s
            v_new = u_ref[:, rows, :] - _bdot(w_ref[:, rows, :], s, "nn")
            vn_ref[:, rows, :] = v_new
            o_ref[:, rows, :] = _bdot(qd_ref[:, rows, :], s, "nn") + _bdot(a_ref[:, rows, :], v_new, "nn")
            s_scr[...] = s * el_ref[:, cc][:, 0:1, :] + _bdot(kd_ref[:, rows, :], v_new, "tn")

    act, sq, state, tile = _dn_scan_specs(t, rows_blk, reverse=False)
    shp = jax.ShapeDtypeStruct((N_HEADS, t, D_HEAD), F32)
    return pl.pallas_call(
        body, name="dn_scan_fwd", grid=(t // rows_blk,),
        in_specs=[act, act, act, act, sq, tile], out_specs=[act, act, state],
        out_shape=[shp, shp, jax.ShapeDtypeStruct((N_HEADS, n_chunks, D_HEAD, D_HEAD), F32)],
        scratch_shapes=[pltpu.VMEM((N_HEADS, D_HEAD, D_HEAD), F32)],
        compiler_params=_cp(dimension_semantics=("arbitrary",)),
    )(u, w, qd, kd, a, el)


def _dn_scan_bwd(w, qd, kd, a, el, vn, s_all, do):
    t = w.shape[1]
    n_chunks = t // DN_CHUNK
    rows_blk = DN_SCAN_GROUP * DN_CHUNK

    def body(w_ref, qd_ref, kd_ref, a_ref, el_ref, vn_ref, s_ref, do_ref, dvn_ref, dkd_ref, dqd_ref, dw_ref, dl_ref, ds_scr):
        @pl.when(pl.program_id(0) == 0)
        def _():
            ds_scr[...] = jnp.zeros_like(ds_scr)

        for cc in reversed(range(DN_SCAN_GROUP)):
            rows = slice(cc * DN_CHUNK, (cc + 1) * DN_CHUNK)
            s = s_ref[:, cc]
            d_s = ds_scr[...]
            e_last = el_ref[:, cc][:, 0:1, :]
            d_o = do_ref[:, rows, :]
            dv_new = _bdot(a_ref[:, rows, :], d_o, "tn") + _bdot(kd_ref[:, rows, :], d_s, "nn")
            ds_scr[...] = d_s * e_last + _bdot(qd_ref[:, rows, :], d_o, "tn") - _bdot(w_ref[:, rows, :], dv_new, "tn")
            dvn_ref[:, rows, :] = dv_new
            dkd_ref[:, rows, :] = _bdot(vn_ref[:, rows, :], d_s, "nt")
            dqd_ref[:, rows, :] = _bdot(d_o, s, "nt")
            dw_ref[:, rows, :] = -_bdot(dv_new, s, "nt")
            dlast = jnp.sum(jnp.sum(d_s * s, axis=2, keepdims=True), axis=1, keepdims=True)
            dl_ref[:, cc] = jnp.broadcast_to(dlast * e_last, (N_HEADS, 8, 128))

    act, sq, state, tile = _dn_scan_specs(t, rows_blk, reverse=True)
    shp = jax.ShapeDtypeStruct((N_HEADS, t, D_HEAD), F32)
    return pl.pallas_call(
        body, name="dn_scan_bwd", grid=(t // rows_blk,),
        in_specs=[act, act, act, sq, tile, act, state, act], out_specs=[act] * 4 + [tile],
        out_shape=[shp] * 4 + [jax.ShapeDtypeStruct((N_HEADS, n_chunks, 8, 128), F32)],
        scratch_shapes=[pltpu.VMEM((N_HEADS, D_HEAD, D_HEAD), F32)],
        compiler_params=_cp(dimension_semantics=("arbitrary",)),
    )(w, qd, kd, a, el, vn, s_all, do)


def _dn_intra_bwd(qkv, beta_t, g_t, tinv_all, vn, do, dvn, dkd, dqd, dw, dl):
    t = qkv.shape[0]
    rows_blk = min(DN_GROUP * DN_CHUNK, t)

    def body(q_ref, k_ref, v_ref, b_ref, g_ref, ti_ref, vn_ref, do_ref, dvn_ref, dkd_ref, dqd_ref, dw_ref, dl_ref,
             dq_ref, dk_ref, dv_ref, db_ref, dg_ref):
        ri = lax.broadcasted_iota(jnp.int32, (DN_CHUNK, DN_CHUNK), 0)
        ci = lax.broadcasted_iota(jnp.int32, (DN_CHUNK, DN_CHUNK), 1)
        h = pl.program_id(1)
        q, k, v = (_chunks(r[...]) for r in (q_ref, k_ref, v_ref))
        b, gc = _head_column(b_ref[...], h), _head_column(g_ref[...], h + N_HEADS)
        tinv = _chunks(ti_ref[0])
        dv_new, dk_dec, dq_dec, d_w = (_chunks(r[...]) for r in (dvn_ref, dkd_ref, dqd_ref, dw_ref))
        eye, gam, kk, qk, _, e_g, dec, rcol = _dn_chunk_common(q, k, gc, ri, ci)
        bv = v * b
        bk = k * (b * e_g)

        d_a = jnp.where(ri >= ci, _bdot(_chunks(do_ref[...]), _chunks(vn_ref[...]), "nt"), 0.0)
        dbv = _bdot(tinv, dv_new, "tn")
        dbk = _bdot(tinv, d_w, "tn")
        d_tinv = _bdot(dv_new, bv, "nt") + _bdot(d_w, bk, "nt")
        d_m = -jnp.where(ri > ci, _dot3(_dot3(tinv, d_tinv, "tn"), tinv, "nt"), 0.0)

        d_kk = d_m * b * gam
        d_gam = d_m * b * kk + d_a * qk
        d_qk = d_a * gam
        dq_ref[...] = _unchunk(_bdot(d_qk, k, "nn") + dq_dec * e_g)
        dk_ref[...] = _unchunk(_bdot(d_qk, q, "tn") + _bdot(d_kk, k, "nn") + _bdot(d_kk, k, "tn")
                               + dk_dec * dec + dbk * (b * e_g))
        dv_ref[...] = _unchunk(dbv * b)
        d_b = _unchunk(jnp.sum(d_m * kk * gam, axis=-1, keepdims=True) + jnp.sum(dbv * v, axis=-1, keepdims=True)
                       + jnp.sum(dbk * k, axis=-1, keepdims=True) * e_g)

        xg = d_gam * gam
        kdk = jnp.sum(dk_dec * (k * dec), axis=-1, keepdims=True)
        d_gc = (jnp.sum(xg, axis=-1, keepdims=True) - _row_to_col(jnp.sum(xg, axis=-2, keepdims=True), eye)
                + jnp.sum(dq_dec * (q * e_g), axis=-1, keepdims=True) - kdk
                + jnp.sum(dbk * bk, axis=-1, keepdims=True))
        d_last_total = dl_ref[0][:, 0:1, 0:1] + jnp.sum(kdk, axis=-2, keepdims=True)
        d_g = _unchunk(d_gc + jnp.where(rcol == DN_CHUNK - 1, d_last_total, 0.0))

        @pl.when(h == 0)
        def _():
            db_ref[...] = jnp.zeros_like(db_ref)
            dg_ref[...] = jnp.zeros_like(dg_ref)

        lane = lax.broadcasted_iota(jnp.int32, db_ref.shape, 1)
        db_ref[...] += jnp.where(lane == h, d_b, 0.0)
        dg_ref[...] += jnp.where(lane == h + N_HEADS, d_g, 0.0)

    head, lanes, hm, sq, tile = _dn_specs(t, rows_blk)
    return pl.pallas_call(
        body, name="dn_intra_bwd", grid=(t // rows_blk, N_HEADS),
        in_specs=[head(0), head(N_HEADS), head(2 * N_HEADS), lanes, lanes, sq] + [hm] * 6 + [tile],
        out_specs=[head(0), head(0), head(0), lanes, lanes],
        out_shape=[jax.ShapeDtypeStruct((t, D_MODEL), F32)] * 3 + [jax.ShapeDtypeStruct((t, 128), F32)] * 2,
        compiler_params=_cp(),
    )(qkv, qkv, qkv, beta_t, g_t, tinv_all, vn, do, dvn, dkd, dqd, dw, dl)


def _dn_post_fwd(o, proj, gn):
    t = o.shape[1]

    def body(o_ref, z_ref, g_ref, out_ref):
        ov, z = o_ref[...], z_ref[...]
        r = lax.rsqrt(jnp.mean(ov * ov, axis=-1, keepdims=True) + NORM_EPS)
        out_ref[...] = (((ov * r) * g_ref[...]) * (z * _sigmoid(z))).astype(BF16)

    blk = pl.BlockSpec((t, D_HEAD), lambda h: (0, h))
    return pl.pallas_call(
        body, name="dn_post_fwd", grid=(N_HEADS,),
        in_specs=[pl.BlockSpec((None, t, D_HEAD), lambda h: (h, 0, 0)),
                  pl.BlockSpec((t, D_HEAD), lambda h: (0, C_DNZ // D_HEAD + h)),
                  pl.BlockSpec((1, D_HEAD), lambda h: (0, 0))],
        out_specs=blk, out_shape=jax.ShapeDtypeStruct((t, D_MODEL), BF16), compiler_params=_cp(),
    )(o, proj, gn)


def _dn_post_bwd(o, proj, gn, dout):
    t = o.shape[1]

    def body(o_ref, z_ref, g_ref, d_ref, do_ref, dz_ref, dg_ref):
        @pl.when(pl.program_id(0) == 0)
        def _():
            dg_ref[...] = jnp.zeros_like(dg_ref)

        ov, z, d = o_ref[...], z_ref[...], d_ref[...]
        r = lax.rsqrt(jnp.mean(ov * ov, axis=-1, keepdims=True) + NORM_EPS)
        ohat = ov * r
        s = _sigmoid(z)
        d_on = d * (z * s)
        dz_ref[...] = (d * (ohat * g_ref[...]) * (s * (1.0 + z * (1.0 - s)))).astype(BF16)
        dg_ref[...] += jnp.sum(d_on * ohat, axis=0, keepdims=True)
        dxh = d_on * g_ref[...]
        do_ref[...] = r * (dxh - ohat * jnp.mean(dxh * ohat, axis=-1, keepdims=True))

    blk = pl.BlockSpec((t, D_HEAD), lambda h: (0, h))
    hm = pl.BlockSpec((None, t, D_HEAD), lambda h: (h, 0, 0))
    vec = pl.BlockSpec((1, D_HEAD), lambda h: (0, 0))
    return pl.pallas_call(
        body, name="dn_post_bwd", grid=(N_HEADS,),
        in_specs=[hm, pl.BlockSpec((t, D_HEAD), lambda h: (0, C_DNZ // D_HEAD + h)), vec, blk],
        out_specs=[hm, blk, vec],
        out_shape=[jax.ShapeDtypeStruct((N_HEADS, t, D_HEAD), F32), jax.ShapeDtypeStruct((t, D_MODEL), BF16),
                   jax.ShapeDtypeStruct((1, D_HEAD), F32)], compiler_params=_cp(),
    )(o, proj, gn, dout)


def _sb_fwd(proj):
    t = proj.shape[0]
    qblk = min(SB_QBLOCK, t)
    scale = 1.0 / math.sqrt(D_HEAD)

    hp = SB_HEADS_PER_STEP
    wid = hp * D_HEAD

    def body(q_ref, k_ref, v_ref, z_ref, o_ref, og_ref, l_ref, qb, kb, vb):
        for hh in range(hp):
            hs = slice(hh * D_HEAD, (hh + 1) * D_HEAD)
            qb[hh] = q_ref[:, hs].astype(BF16)
            kb[hh] = k_ref[:, hs].astype(BF16)
            vb[hh] = v_ref[:, hs].astype(BF16)
        ri = lax.broadcasted_iota(jnp.int32, (qblk, SB_BLOCK), 0)
        ci = lax.broadcasted_iota(jnp.int32, (qblk, SB_BLOCK), 1)
        r2 = lax.broadcasted_iota(jnp.int32, (SB_BLOCK, SB_BLOCK), 0)
        c2 = lax.broadcasted_iota(jnp.int32, (SB_BLOCK, SB_BLOCK), 1)
        upper = (r2 > c2).astype(BF16)
        nkb = qblk // SB_BLOCK

        def qblock(i, carry):
            rows = pl.ds(pl.multiple_of(i * qblk, qblk), qblk)
            qi = qb[:, rows, :]

            def tile(j, st, on_diagonal):
                acc, c = st
                cols = pl.ds(pl.multiple_of(j * SB_BLOCK, SB_BLOCK), SB_BLOCK)
                z = _dot(qi, kb[:, cols, :], "nt") * scale
                lb = jnp.minimum(z, 0.0) - jnp.log(1.0 + jnp.exp(-jnp.abs(z)))
                lf = lb - z
                if on_diagonal:
                    mask = (j * SB_BLOCK + ci) < (i * qblk + ri)
                    lf = jnp.where(mask, lf, 0.0)
                att = jnp.exp(lb + (_ones_dot(lf, upper) + c))
                if on_diagonal:
                    att = jnp.where(mask, att, 0.0)
                acc = acc + _dot(att.astype(BF16), vb[:, cols, :], "nn")
                return acc, c + jnp.sum(lf, axis=-1, keepdims=True)

            st = (jnp.zeros((hp, qblk, D_HEAD), F32), jnp.zeros((hp, qblk, 1), F32))
            for d in range(nkb):
                st = tile((i + 1) * nkb - 1 - d, st, True)
            acc, c = lax.fori_loop(0, i * nkb, lambda jj, s: tile(i * nkb - 1 - jj, s, False), st)
            l_ref[:, rows, :] = c
            for hh in range(hp):
                hs = slice(hh * D_HEAD, (hh + 1) * D_HEAD)
                zg = z_ref[rows, hs]
                o_ref[rows, hs] = acc[hh]
                og_ref[rows, hs] = (acc[hh] * (zg * _sigmoid(zg))).astype(BF16)
            return carry

        lax.fori_loop(0, t // qblk, qblock, 0)

    def head(off):
        return pl.BlockSpec((t, wid), lambda h: (0, off // wid + h))

    out = pl.BlockSpec((t, wid), lambda h: (0, h))
    return pl.pallas_call(
        body, name="sb_fwd", grid=(N_HEADS // hp,),
        in_specs=[head(C_SBQ), head(C_SBQ + D_MODEL), head(C_SBQ + 2 * D_MODEL), head(C_SBZ)],
        out_specs=[out, out, pl.BlockSpec((hp, t, 1), lambda h: (h, 0, 0))],
        out_shape=[jax.ShapeDtypeStruct((t, D_MODEL), F32), jax.ShapeDtypeStruct((t, D_MODEL), BF16),
                   jax.ShapeDtypeStruct((N_HEADS, t, 1), F32)],
        scratch_shapes=[pltpu.VMEM((hp, t, D_HEAD), BF16)] * 3, compiler_params=_cp(),
    )(proj, proj, proj, proj)


def _sb_bwd(proj, o, ltot, dog):
    t = proj.shape[0]
    qblk = min(SB_QBLOCK, t)
    scale = 1.0 / math.sqrt(D_HEAD)

    hp = SB_HEADS_PER_STEP
    wid = hp * D_HEAD

    def body(q_ref, k_ref, v_ref, z_ref, o_ref, l_ref, d_ref, dq_ref, dk_ref, dv_ref, dz_ref,
             qb, kb, vb, dob, dk_scr, dv_scr):
        for hh in range(hp):
            hs = slice(hh * D_HEAD, (hh + 1) * D_HEAD)
            qb[hh] = q_ref[:, hs].astype(BF16)
            kb[hh] = k_ref[:, hs].astype(BF16)
            vb[hh] = v_ref[:, hs].astype(BF16)
            zg = z_ref[:, hs]
            sg = _sigmoid(zg)
            dgo = d_ref[:, hs]
            dob[hh] = (dgo * (zg * sg)).astype(BF16)
            dz_ref[:, hs] = (dgo * o_ref[:, hs] * (sg * (1.0 + zg * (1.0 - sg)))).astype(BF16)
        dk_scr[...] = jnp.zeros_like(dk_scr)
        dv_scr[...] = jnp.zeros_like(dv_scr)
        ri = lax.broadcasted_iota(jnp.int32, (qblk, SB_BLOCK), 0)
        ci = lax.broadcasted_iota(jnp.int32, (qblk, SB_BLOCK), 1)
        r2 = lax.broadcasted_iota(jnp.int32, (SB_BLOCK, SB_BLOCK), 0)
        c2 = lax.broadcasted_iota(jnp.int32, (SB_BLOCK, SB_BLOCK), 1)
        upper = (r2 > c2).astype(BF16)
        below = (r2 < c2).astype(BF16)

        def qblock(i, carry):
            rows = pl.ds(pl.multiple_of(i * qblk, qblk), qblk)
            qi = qb[:, rows, :]
            d_o = dob[:, rows, :]
            ltot = l_ref[:, rows, :]

            def tile(j, st, on_diagonal):
                dq, cpre, ce = st
                cols = pl.ds(pl.multiple_of(j * SB_BLOCK, SB_BLOCK), SB_BLOCK)
                kj, vj = kb[:, cols, :], vb[:, cols, :]
                z = _dot(qi, kj, "nt") * scale
                lb = jnp.minimum(z, 0.0) - jnp.log(1.0 + jnp.exp(-jnp.abs(z)))
                lf = lb - z
                if on_diagonal:
                    mask = (j * SB_BLOCK + ci) < (i * qblk + ri)
                    lf = jnp.where(mask, lf, 0.0)
                tile_sum = jnp.sum(lf, axis=-1, keepdims=True)
                att = jnp.exp(lb + ((ltot - cpre - tile_sum) + _ones_dot(lf, upper)))
                if on_diagonal:
                    att = jnp.where(mask, att, 0.0)
                e = _dot(d_o, vj, "nt") * att
                dlf = ce + _ones_dot(e, below)
                dzz = e - (e + dlf) * jnp.exp(lb)
                if on_diagonal:
                    dzz = jnp.where(mask, dzz, 0.0)
                dzz = dzz.astype(BF16)
                dq = dq + _dot(dzz, kj, "nn")
                dk_scr[:, cols, :] += _dot(dzz, qi, "tn")
                dv_scr[:, cols, :] += _dot(att.astype(BF16), d_o, "tn")
                return dq, cpre + tile_sum, ce + jnp.sum(e, axis=-1, keepdims=True)

            nkb = qblk // SB_BLOCK
            zero_col = jnp.zeros((hp, qblk, 1), F32)
            st = lax.fori_loop(0, i * nkb, lambda j, s: tile(j, s, False),
                               (jnp.zeros((hp, qblk, D_HEAD), F32), zero_col, zero_col))
            for d in range(nkb):
                st = tile(i * nkb + d, st, True)
            dq = st[0]
            for hh in range(hp):
                dq_ref[rows, hh * D_HEAD:(hh + 1) * D_HEAD] = (dq[hh] * scale).astype(BF16)
            return carry

        lax.fori_loop(0, t // qblk, qblock, 0)
        for hh in range(hp):
            hs = slice(hh * D_HEAD, (hh + 1) * D_HEAD)
            dk_ref[:, hs] = (dk_scr[hh] * scale).astype(BF16)
            dv_ref[:, hs] = dv_scr[hh].astype(BF16)

    def head(off):
        return pl.BlockSpec((t, wid), lambda h: (0, off // wid + h))

    return pl.pallas_call(
        body, name="sb_bwd", grid=(N_HEADS // hp,),
        in_specs=[head(C_SBQ), head(C_SBQ + D_MODEL), head(C_SBQ + 2 * D_MODEL), head(C_SBZ), head(0),
                  pl.BlockSpec((hp, t, 1), lambda h: (h, 0, 0)), head(0)],
        out_specs=[head(0)] * 4, out_shape=[jax.ShapeDtypeStruct((t, D_MODEL), BF16)] * 4,
        scratch_shapes=[pltpu.VMEM((hp, t, D_HEAD), BF16)] * 4 + [pltpu.VMEM((hp, t, D_HEAD), F32)] * 2,
        compiler_params=_cp(),
    )(proj, proj, proj, proj, o, ltot, dog)


def _mem_fwd(proj, mkv):
    t = proj.shape[0]
    tq = _pick(t, (512, 256))
    m_len = mkv.shape[0]
    scale = 1.0 / math.sqrt(MEM_DH)

    def body(q_ref, z_ref, kv_ref, o_ref, og_ref):
        q = q_ref[...]
        mk = kv_ref[:, :MEM_W].astype(BF16)
        mv = kv_ref[:, MEM_W:].astype(BF16)
        lane = lax.broadcasted_iota(jnp.int32, q.shape, 1) >> 6
        o = jnp.zeros(q.shape, F32)
        for h in range(MEM_HEADS):
            s = _bdot(jnp.where(lane == h, q, 0.0), mk, "nt") * scale
            p = jnp.exp(s - jnp.max(s, axis=-1, keepdims=True))
            p = p / jnp.sum(p, axis=-1, keepdims=True)
            o = o + jnp.where(lane == h, _bdot(p, mv, "nn"), 0.0)
        z = z_ref[...]
        o_ref[...] = o
        og_ref[...] = (o * (z * _sigmoid(z))).astype(BF16)

    out = pl.BlockSpec((tq, MEM_W), lambda i: (i, 0))
    return pl.pallas_call(
        body, name="mem_fwd", grid=(t // tq,),
        in_specs=[pl.BlockSpec((tq, MEM_W), lambda i: (i, C_MQ // MEM_W)),
                  pl.BlockSpec((tq, MEM_W), lambda i: (i, C_MZ // MEM_W)),
                  pl.BlockSpec((m_len, 2 * MEM_W), lambda i: (0, 0))],
        out_specs=[out, out],
        out_shape=[jax.ShapeDtypeStruct((t, MEM_W), F32), jax.ShapeDtypeStruct((t, MEM_W), BF16)],
        compiler_params=_cp(),
    )(proj, proj, mkv)


def _mem_bwd(proj, mkv, o, dog):
    t = proj.shape[0]
    tq = _pick(t, (512, 256))
    m_len = mkv.shape[0]
    scale = 1.0 / math.sqrt(MEM_DH)

    def body(q_ref, z_ref, kv_ref, o_ref, d_ref, dq_ref, dz_ref, dkv_ref):
        @pl.when(pl.program_id(0) == 0)
        def _():
            dkv_ref[...] = jnp.zeros_like(dkv_ref)

        q = q_ref[...]
        z = z_ref[...]
        sg = _sigmoid(z)
        dgo = d_ref[...]
        d_o = dgo * (z * sg)
        dz_ref[...] = (dgo * o_ref[...] * (sg * (1.0 + z * (1.0 - sg)))).astype(BF16)
        mk = kv_ref[:, :MEM_W].astype(BF16)
        mv = kv_ref[:, MEM_W:].astype(BF16)
        lane = lax.broadcasted_iota(jnp.int32, q.shape, 1) >> 6
        klane = lax.broadcasted_iota(jnp.int32, (m_len, MEM_W), 1) >> 6
        dq = jnp.zeros(q.shape, F32)
        dmk = jnp.zeros((m_len, MEM_W), F32)
        dmv = jnp.zeros((m_len, MEM_W), F32)
        for h in range(MEM_HEADS):
            qh = jnp.where(lane == h, q, 0.0)
            doh = jnp.where(lane == h, d_o, 0.0)
            s = _bdot(qh, mk, "nt") * scale
            p = jnp.exp(s - jnp.max(s, axis=-1, keepdims=True))
            p = p / jnp.sum(p, axis=-1, keepdims=True)
            dp = _bdot(doh, mv, "nt")
            ds = p * (dp - jnp.sum(dp * p, axis=-1, keepdims=True)) * scale
            dq = dq + jnp.where(lane == h, _bdot(ds, mk, "nn"), 0.0)
            dmk = dmk + jnp.where(klane == h, _bdot(ds, qh, "tn"), 0.0)
            dmv = dmv + jnp.where(klane == h, _bdot(p, doh, "tn"), 0.0)
        dq_ref[...] = dq.astype(BF16)
        dkv_ref[:, :MEM_W] += dmk
        dkv_ref[:, MEM_W:] += dmv

    blk = pl.BlockSpec((tq, MEM_W), lambda i: (i, 0))
    kv = pl.BlockSpec((m_len, 2 * MEM_W), lambda i: (0, 0))
    return pl.pallas_call(
        body, name="mem_bwd", grid=(t // tq,),
        in_specs=[pl.BlockSpec((tq, MEM_W), lambda i: (i, C_MQ // MEM_W)),
                  pl.BlockSpec((tq, MEM_W), lambda i: (i, C_MZ // MEM_W)), kv, blk, blk],
        out_specs=[blk, blk, kv],
        out_shape=[jax.ShapeDtypeStruct((t, MEM_W), BF16), jax.ShapeDtypeStruct((t, MEM_W), BF16),
                   jax.ShapeDtypeStruct((m_len, 2 * MEM_W), F32)], compiler_params=_cp(),
    )(proj, proj, mkv, o, dog)


_GW = 512


def _merge_fwd(proj, y_dn, y_sb, y_m):
    t = proj.shape[0]
    tb = _pick(t, (256,))
    nc = D_MODEL // _GW

    def body(g1, g2, g3, y1, y2, y3, out_ref):
        out_ref[...] = (_sigmoid(g1[...]) * y1[...] + _sigmoid(g2[...]) * y2[...] + _sigmoid(g3[...]) * y3[...]).astype(BF16)

    def gate(kb):
        return pl.BlockSpec((tb, _GW), lambda i, c: (i, C_GATES // _GW + kb * nc + c))

    blk = pl.BlockSpec((tb, _GW), lambda i, c: (i, c))
    return pl.pallas_call(
        body, name="merge_fwd", grid=(t // tb, nc), in_specs=[gate(0), gate(1), gate(2), blk, blk, blk],
        out_specs=blk, out_shape=jax.ShapeDtypeStruct((t, D_MODEL), BF16), compiler_params=_cp(),
    )(proj, proj, proj, y_dn, y_sb, y_m)


def _merge_bwd(proj, y_dn, y_sb, y_m, dm):
    t = proj.shape[0]
    tb = _pick(t, (256,))
    nc = D_MODEL // _GW

    def body(g1, g2, g3, y1, y2, y3, dm_ref, d1, d2, d3, dg1, dg2, dg3):
        d = dm_ref[...]
        for g, y, dy, dg in ((g1, y1, d1, dg1), (g2, y2, d2, dg2), (g3, y3, d3, dg3)):
            s = _sigmoid(g[...])
            dy[...] = (d * s).astype(BF16)
            dg[...] = (d * y[...] * (s * (1.0 - s))).astype(BF16)

    def gate(kb):
        return pl.BlockSpec((tb, _GW), lambda i, c: (i, C_GATES // _GW + kb * nc + c))

    blk = pl.BlockSpec((tb, _GW), lambda i, c: (i, c))
    act = jax.ShapeDtypeStruct((t, D_MODEL), BF16)
    return pl.pallas_call(
        body, name="merge_bwd", grid=(t // tb, nc), in_specs=[gate(0), gate(1), gate(2), blk, blk, blk, blk],
        out_specs=[blk] * 6, out_shape=[act] * 6, compiler_params=_cp(),
    )(proj, proj, proj, y_dn, y_sb, y_m, dm)


def _final_loss(x, mo, g, tgt):
    t, d = x.shape
    tb = _pick(t, (256,))

    def body(x_ref, mo_ref, g_ref, t_ref, do_ref, dob_ref, loss_ref, dg_ref):
        @pl.when(pl.program_id(0) == 0)
        def _():
            loss_ref[...] = jnp.zeros_like(loss_ref)
            dg_ref[...] = jnp.zeros_like(dg_ref)

        out = x_ref[...] + mo_ref[...]
        r = lax.rsqrt(jnp.mean(out * out, axis=-1, keepdims=True) + NORM_EPS)
        xhat = out * r
        gv = g_ref[...]
        err = xhat * gv - t_ref[...]
        per_tok = jnp.mean(err * err, axis=-1, keepdims=True)
        loss_ref[...] += 0.5 * jnp.sum(per_tok, axis=0, keepdims=True)
        dy = err * (1.0 / d)
        dg_ref[...] += jnp.sum(dy * xhat, axis=0, keepdims=True)
        dxh = dy * gv
        dout = r * (dxh - xhat * jnp.mean(dxh * xhat, axis=-1, keepdims=True))
        do_ref[...] = dout
        dob_ref[...] = dout.astype(BF16)

    row = pl.BlockSpec((tb, d), lambda i: (i, 0))
    vec = pl.BlockSpec((1, d), lambda i: (0, 0))
    return pl.pallas_call(
        body, name="final_loss", grid=(t // tb,), in_specs=[row, row, vec, row],
        out_specs=[row, row, pl.BlockSpec((1, 128), lambda i: (0, 0)), vec],
        out_shape=[jax.ShapeDtypeStruct((t, d), F32), jax.ShapeDtypeStruct((t, d), BF16),
                   jax.ShapeDtypeStruct((1, 128), F32), jax.ShapeDtypeStruct((1, d), F32)],
        compiler_params=_cp(),
    )(x, mo, g, tgt)


def _cast_bf16(a, name):
    r, c = a.shape
    tb = _pick(r, (128, 496))

    def body(a_ref, o_ref):
        o_ref[...] = a_ref[...].astype(BF16)

    blk = pl.BlockSpec((tb, c), lambda i: (i, 0))
    return pl.pallas_call(body, name=name, grid=(r // tb,), in_specs=[blk], out_specs=blk,
                          out_shape=jax.ShapeDtypeStruct((r, c), BF16), compiler_params=_cp())(a)


WIN_START = (0, 23, 45, 68)
_S1_LO, _S1_HI = 1148, 1164
_S1_BA_POS = SHARD_PAD - 128


def _to_window(x, s):
    if s == 0:
        return x
    if s in (2, 3):
        return pltpu.roll(x, 120 if s == 2 else 124, 1)
    pos = lax.broadcasted_iota(jnp.int32, x.shape, 1)
    head = pltpu.roll(x, 4, 1)
    tail = pltpu.roll(x, SHARD_PAD - 12, 1)
    ba = jnp.where(pos < _S1_BA_POS + (_S1_HI - _S1_LO), pltpu.roll(x, _S1_BA_POS - _S1_LO, 1), 0.0)
    return jnp.where(pos < _S1_LO + 4, head, jnp.where(pos < _S1_BA_POS, tail, ba))


def _from_window(g, s):
    if s == 0:
        return g
    if s in (2, 3):
        return pltpu.roll(g, SHARD_PAD - (120 if s == 2 else 124), 1)
    col = lax.broadcasted_iota(jnp.int32, g.shape, 1)
    head = pltpu.roll(g, SHARD_PAD - 4, 1)
    tail = pltpu.roll(g, 12, 1)
    ba = pltpu.roll(g, SHARD_PAD - (_S1_BA_POS - _S1_LO), 1)
    return jnp.where(col < _S1_LO, head, jnp.where(col < _S1_HI, ba, tail))


def _cast_to_window(w, shard, name):
    r, c = w.shape
    tb = _pick(r, (128,))

    def body(s_ref, w_ref, o_ref, pad_scr):
        pad_scr[...] = jnp.zeros_like(pad_scr)
        pad_scr[:, :c] = w_ref[...]
        x = pad_scr[...]
        for s in range(N_SHARD):
            @pl.when(s_ref[0] == s)
            def _():
                o_ref[...] = _to_window(x, s).astype(BF16)

    return pl.pallas_call(
        body, name=name,
        grid_spec=pltpu.PrefetchScalarGridSpec(
            num_scalar_prefetch=1, grid=(r // tb,),
            in_specs=[pl.BlockSpec((tb, c), lambda i, s: (i, 0))],
            out_specs=pl.BlockSpec((tb, SHARD_PAD), lambda i, s: (i, 0)),
            scratch_shapes=[pltpu.VMEM((tb, SHARD_PAD), F32)]),
        out_shape=jax.ShapeDtypeStruct((r, SHARD_PAD), BF16), compiler_params=_cp(),
    )(shard, w)


def _pair_add(g, recv, c_idx, name):
    n, r, c = g.shape
    half = r // 2
    tb = _pick(half, (128, 248))
    nb = half // tb

    def body(c_ref, g_ref, r_ref, o_ref):
        o_ref[...] = (g_ref[...].astype(F32) + r_ref[...].astype(F32)).astype(BF16)

    blk = pl.BlockSpec((n, tb, c), lambda i, c_ref: (0, i, 0))
    return pl.pallas_call(
        body, name=name,
        grid_spec=pltpu.PrefetchScalarGridSpec(
            num_scalar_prefetch=1, grid=(nb,),
            in_specs=[pl.BlockSpec((n, tb, c), lambda i, c_ref: (0, c_ref[0] * nb + i, 0)), blk], out_specs=blk),
        out_shape=jax.ShapeDtypeStruct((n, half, c), BF16), compiler_params=_cp(),
    )(c_idx, g, recv)


def _chip_sum(parts, by_chip, place, name):
    n, h, c = parts.shape
    tb = _pick(h, (128, 248))
    nb = h // tb

    def body(p_ref, mine_ref, *rest):
        others, o_ref = rest[:n], rest[n]
        me = jnp.zeros((tb, c), jnp.int32) + p_ref[0]
        acc = None
        for q in range(n):
            term = jnp.where(me == q, mine_ref[...], others[q][...]).astype(F32)
            acc = term if acc is None else acc + term
        o_ref[...] = acc

    def other(q):
        return pl.BlockSpec((None, tb, c), lambda i, p: (jnp.where(p[0] == q, (q + 1) % n, q), i, 0))

    return pl.pallas_call(
        body, name=name,
        grid_spec=pltpu.PrefetchScalarGridSpec(
            num_scalar_prefetch=1, grid=(nb,),
            in_specs=[pl.BlockSpec((None, tb, c), lambda i, p: (p[0], i, 0))] + [other(q) for q in range(n)],
            out_specs=pl.BlockSpec((tb, c), lambda i, p: (p[1] * nb + i, 0))),
        out_shape=jax.ShapeDtypeStruct((2 * h, c), F32), compiler_params=_cp(),
    )(place, parts, *([by_chip] * n))


def _adamw_math(w, g, m, v):
    m = ADAM_B1 * m + (1.0 - ADAM_B1) * g
    v = ADAM_B2 * v + (1.0 - ADAM_B2) * (g * g)
    m_hat = m / (1.0 - ADAM_B1 ** ADAM_STEP)
    v_hat = v / (1.0 - ADAM_B2 ** ADAM_STEP)
    delta = -ADAM_LR * (m_hat / (jnp.sqrt(v_hat) + ADAM_EPS) + ADAM_WD * w)
    return delta, m, v


def _adamw(w, g, m, v, name):
    r, c = w.shape
    tb = _pick(r, (128, 496))

    def body(w_ref, g_ref, m_ref, v_ref, go_ref, d_ref, mo_ref, vo_ref):
        gv = g_ref[...]
        d, mn, vn = _adamw_math(w_ref[...], gv, m_ref[...], v_ref[...])
        go_ref[...] = gv
        d_ref[...] = d
        mo_ref[...] = mn
        vo_ref[...] = vn

    blk = pl.BlockSpec((tb, c), lambda i: (i, 0))
    return pl.pallas_call(
        body, name=name, grid=(r // tb,), in_specs=[blk] * 4, out_specs=[blk] * 4,
        out_shape=[jax.ShapeDtypeStruct((r, c), F32)] * 4, compiler_params=_cp(),
    )(w, g, m, v)


def _adamw_window(w, g_win, m, v, shard, name):
    r, c = w.shape
    tb = _pick(r, (128,))

    def body(s_ref, w_ref, g_ref, m_ref, v_ref, go_ref, d_ref, mo_ref, vo_ref, g_scr):
        gw = g_ref[...]
        for s in range(N_SHARD):
            @pl.when(s_ref[0] == s)
            def _():
                g_scr[...] = _from_window(gw, s)

        gv = g_scr[:, :c]
        d, mn, vn = _adamw_math(w_ref[...], gv, m_ref[...], v_ref[...])
        go_ref[...] = gv
        d_ref[...] = d
        mo_ref[...] = mn
        vo_ref[...] = vn

    blk = pl.BlockSpec((tb, c), lambda i, s: (i, 0))
    return pl.pallas_call(
        body, name=name,
        grid_spec=pltpu.PrefetchScalarGridSpec(
            num_scalar_prefetch=1, grid=(r // tb,),
            in_specs=[blk, pl.BlockSpec((tb, SHARD_PAD), lambda i, s: (i, 0)), blk, blk], out_specs=[blk] * 4,
            scratch_shapes=[pltpu.VMEM((tb, SHARD_PAD), F32)]),
        out_shape=[jax.ShapeDtypeStruct((r, c), F32)] * 4, compiler_params=_cp(),
    )(shard, w, g_win, m, v)


def _small_update(gathered, w, m, v):
    def body(p_ref, w_ref, m_ref, v_ref, g_ref, d_ref, mo_ref, vo_ref):
        g = p_ref[0]
        for i in range(1, N_DEV):
            g = g + p_ref[i]
        d, mn, vn = _adamw_math(w_ref[...], g, m_ref[...], v_ref[...])
        g_ref[...] = g
        d_ref[...] = d
        mo_ref[...] = mn
        vo_ref[...] = vn

    full = pl.BlockSpec((32, 128), lambda i: (0, 0))
    return pl.pallas_call(
        body, name="small_update", grid=(1,),
        in_specs=[pl.BlockSpec((N_DEV, 32, 128), lambda i: (0, 0, 0)), full, full, full], out_specs=[full] * 4,
        out_shape=[jax.ShapeDtypeStruct((32, 128), F32)] * 4, compiler_params=_cp(),
    )(gathered, w, m, v)


_ANY = pl.BlockSpec(memory_space=pl.ANY)


def _place():
    x, y, c = lax.axis_index("x"), lax.axis_index("y"), lax.axis_index("c")
    chips = [(1 - x, y), (x, 1 - y), (1 - x, 1 - y)]
    return x, y, c, chips


def _gather_shards(arrs):
    n = len(arrs)

    def body(*refs):
        ins, outs = refs[:n], refs[n:2 * n]
        send_sems, recv_sems, local_sems = refs[2 * n:2 * n + 3]
        bufs = refs[2 * n + 3:]
        x, y, c, chips = _place()
        me = 2 * x + y
        sibling = (x, y, 1 - c)
        sends = []
        for a in range(n):
            half = ins[a].shape[0] // 2
            mine = pl.ds(pl.multiple_of(c * half, 16), half)
            for j, (qx, qy) in enumerate(chips):
                cp = pltpu.make_async_remote_copy(
                    src_ref=ins[a].at[mine], dst_ref=outs[a].at[me, mine],
                    send_sem=send_sems.at[6 * a + j], recv_sem=recv_sems.at[6 * a + j],
                    device_id=(qx, qy, c), device_id_type=MESH)
                cp.start()
                sends.append(cp)
        for a in range(n):
            step = bufs[a].shape[0]
            for r0 in range(0, ins[a].shape[0], step):
                rows = pl.ds(r0, step)
                load = pltpu.make_async_copy(ins[a].at[rows], bufs[a], local_sems.at[2 * a])
                load.start()
                load.wait()
                store = pltpu.make_async_copy(bufs[a], outs[a].at[me, rows], local_sems.at[2 * a + 1])
                store.start()
                store.wait()
        for a in range(n):
            half = ins[a].shape[0] // 2
            mine = pl.ds(pl.multiple_of(c * half, 16), half)
            for j, (qx, qy) in enumerate(chips):
                q = 2 * qx + qy
                landed = outs[a].at[q, mine]
                pltpu.make_async_remote_copy(
                    src_ref=landed, dst_ref=landed, send_sem=send_sems.at[6 * a + j], recv_sem=recv_sems.at[6 * a + j],
                    device_id=(qx, qy, c), device_id_type=MESH).wait_recv()
                fw = pltpu.make_async_remote_copy(
                    src_ref=landed, dst_ref=landed, send_sem=send_sems.at[6 * a + 3 + j],
                    recv_sem=recv_sems.at[6 * a + 3 + j], device_id=sibling, device_id_type=MESH)
                fw.start()
                sends.append(fw)
        for a in range(n):
            half = ins[a].shape[0] // 2
            theirs = pl.ds(pl.multiple_of((1 - c) * half, 16), half)
            for j, (qx, qy) in enumerate(chips):
                q = 2 * qx + qy
                dst = outs[a].at[q, theirs]
                pltpu.make_async_remote_copy(
                    src_ref=dst, dst_ref=dst, send_sem=send_sems.at[6 * a + 3 + j], recv_sem=recv_sems.at[6 * a + 3 + j],
                    device_id=sibling, device_id_type=MESH).wait_recv()
        for cp in sends:
            cp.wait_send()

    return pl.pallas_call(
        body, name="gather_shards", in_specs=[_ANY] * n, out_specs=[_ANY] * n,
        out_shape=[jax.ShapeDtypeStruct((N_SHARD,) + a.shape, a.dtype) for a in arrs],
        scratch_shapes=[pltpu.SemaphoreType.DMA((6 * n,)), pltpu.SemaphoreType.DMA((6 * n,)),
                        pltpu.SemaphoreType.DMA((2 * n,))]
        + [pltpu.VMEM((_pick(a.shape[0], (256, 496)), a.shape[1]), a.dtype) for a in arrs],
        compiler_params=pltpu.CompilerParams(has_side_effects=True, vmem_limit_bytes=VMEM_LIMIT),
    )(*arrs)


def _pair_reduce_send(grads):
    n = len(grads)

    def body(*refs):
        ins, outs = refs[:n], refs[n:2 * n]
        send_sems, recv_sems = refs[2 * n:]
        x, y, c, _ = _place()
        sibling = (x, y, 1 - c)
        cps = []
        for a in range(n):
            half = ins[a].shape[1] // 2
            theirs = pl.ds(pl.multiple_of((1 - c) * half, 8), half)
            cp = pltpu.make_async_remote_copy(
                src_ref=ins[a].at[:, theirs], dst_ref=outs[a], send_sem=send_sems.at[a], recv_sem=recv_sems.at[a],
                device_id=sibling, device_id_type=MESH)
            cp.start()
            cps.append(cp)
        for cp in cps:
            cp.wait()

    return pl.pallas_call(
        body, name="pair_reduce_send", in_specs=[_ANY] * n, out_specs=[_ANY] * n,
        out_shape=[jax.ShapeDtypeStruct((g.shape[0], g.shape[1] // 2, g.shape[2]), g.dtype) for g in grads],
        scratch_shapes=[pltpu.SemaphoreType.DMA((n,)), pltpu.SemaphoreType.DMA((n,))],
        compiler_params=pltpu.CompilerParams(has_side_effects=True),
    )(*grads)


_HBM = pl.BlockSpec(memory_space=pltpu.HBM)
_SEM = pl.BlockSpec(memory_space=pltpu.SEMAPHORE)
_DATAFLOW = pltpu.SideEffectType.DATAFLOW_SIDE_EFFECTING


def _chip_exchange_copies(ins, lands, send_sems, recv_sems):
    x, y, c, chips = _place()
    me = 2 * x + y
    cps = []
    for a in range(len(ins)):
        for j, (qx, qy) in enumerate(chips):
            cps.append(pltpu.make_async_remote_copy(
                src_ref=ins[a].at[2 * qx + qy], dst_ref=lands[a].at[me], send_sem=send_sems.at[3 * a + j],
                recv_sem=recv_sems.at[3 * a + j], device_id=(qx, qy, c), device_id_type=MESH))
    return cps


def _chip_exchange_start(parts):
    n = len(parts)

    def body(*refs):
        ins, lands = refs[:n], refs[n:2 * n]
        send_sems, recv_sems = refs[2 * n:2 * n + 2]
        token = refs[4 * n + 2]
        for cp in _chip_exchange_copies(ins, lands, send_sems, recv_sems):
            cp.start()
        token[...] = jnp.zeros_like(token)

    hbm = [pltpu.HBM(p.shape, p.dtype) for p in parts]
    lands = [pltpu.with_memory_space_constraint(lax.empty(p.shape, p.dtype), pltpu.HBM) for p in parts]
    res = pl.pallas_call(
        body, name="chip_exchange_start",
        out_shape=(pltpu.SemaphoreType.DMA((3 * n,)), pltpu.SemaphoreType.DMA((3 * n,)), *hbm, *hbm,
                   jax.ShapeDtypeStruct((8, 128), F32)),
        in_specs=[_HBM] * (2 * n), out_specs=(_SEM, _SEM, *([_HBM] * (2 * n)), pl.BlockSpec(memory_space=pltpu.VMEM)),
        input_output_aliases={a: 2 + a for a in range(2 * n)},
        compiler_params=pltpu.CompilerParams(has_side_effects=_DATAFLOW),
    )(*[pltpu.with_memory_space_constraint(p, pltpu.HBM) for p in parts], *lands)
    return res[0], res[1], res[2:2 + n], res[2 + n:2 + 2 * n], res[2 + 2 * n]


def _chip_exchange_wait(send_sems, recv_sems, parts, lands, after):
    n = len(parts)

    def body(*refs):
        ins, land_refs = refs[:n], refs[n:2 * n]
        s_sems, r_sems = refs[2 * n:2 * n + 2]
        for cp in _chip_exchange_copies(ins, land_refs, s_sems, r_sems):
            cp.wait_send()
            cp.wait_recv()

    hbm = [pltpu.HBM(p.shape, p.dtype) for p in parts]
    res = pl.pallas_call(
        body, name="chip_exchange_wait", out_shape=(*hbm, *hbm),
        in_specs=[_HBM] * (2 * n) + [_SEM, _SEM, _ANY], out_specs=tuple([_HBM] * (2 * n)),
        input_output_aliases={a: a for a in range(2 * n)},
        compiler_params=pltpu.CompilerParams(has_side_effects=_DATAFLOW),
    )(*parts, *lands, send_sems, recv_sems, after)
    return res[:n], res[n:]


def _pair_allgather(fulls):
    n = len(fulls)

    def body(*refs):
        outs = refs[n:2 * n]
        send_sems, recv_sems = refs[2 * n:]
        x, y, c, _ = _place()
        sibling = (x, y, 1 - c)
        cps = []
        for a in range(n):
            half = outs[a].shape[0] // 2
            mine = outs[a].at[pl.ds(pl.multiple_of(c * half, 8), half)]
            cp = pltpu.make_async_remote_copy(
                src_ref=mine, dst_ref=mine, send_sem=send_sems.at[a], recv_sem=recv_sems.at[a],
                device_id=sibling, device_id_type=MESH)
            cp.start()
            cps.append(cp)
        for a in range(n):
            half = outs[a].shape[0] // 2
            theirs = outs[a].at[pl.ds(pl.multiple_of((1 - c) * half, 8), half)]
            pltpu.make_async_remote_copy(
                src_ref=theirs, dst_ref=theirs, send_sem=send_sems.at[a], recv_sem=recv_sems.at[a],
                device_id=sibling, device_id_type=MESH).wait_recv()
        for cp in cps:
            cp.wait_send()

    return pl.pallas_call(
        body, name="pair_allgather", in_specs=[_ANY] * n, out_specs=[_ANY] * n,
        out_shape=[jax.ShapeDtypeStruct(f.shape, f.dtype) for f in fulls],
        input_output_aliases={a: a for a in range(n)},
        scratch_shapes=[pltpu.SemaphoreType.DMA((n,)), pltpu.SemaphoreType.DMA((n,))],
        compiler_params=pltpu.CompilerParams(has_side_effects=True),
    )(*fulls)


def _allgather_small(slab):
    def body(s_ref, out_ref, send_sems, recv_sems):
        x, y, c, _ = _place()
        me = 4 * x + 2 * y + c
        out_ref[me] = s_ref[...]
        cps = []
        for mask in range(1, N_DEV):
            peer = (x ^ (mask >> 2), y ^ ((mask >> 1) & 1), c ^ (mask & 1))
            cp = pltpu.make_async_remote_copy(
                src_ref=s_ref, dst_ref=out_ref.at[me], send_sem=send_sems.at[mask - 1], recv_sem=recv_sems.at[mask - 1],
                device_id=peer, device_id_type=MESH)
            cp.start()
            cps.append(cp)
        for mask in range(1, N_DEV):
            peer = (x ^ (mask >> 2), y ^ ((mask >> 1) & 1), c ^ (mask & 1))
            dst = out_ref.at[4 * peer[0] + 2 * peer[1] + peer[2]]
            pltpu.make_async_remote_copy(
                src_ref=dst, dst_ref=dst, send_sem=send_sems.at[mask - 1], recv_sem=recv_sems.at[mask - 1],
                device_id=peer, device_id_type=MESH).wait_recv()
        for cp in cps:
            cp.wait_send()

    vm = pl.BlockSpec(memory_space=pltpu.VMEM)
    return pl.pallas_call(
        body, name="allgather_small", in_specs=[vm], out_specs=vm,
        out_shape=jax.ShapeDtypeStruct((N_DEV,) + slab.shape, slab.dtype),
        scratch_shapes=[pltpu.SemaphoreType.DMA((N_DEV - 1,)), pltpu.SemaphoreType.DMA((N_DEV - 1,))],
        compiler_params=pltpu.CompilerParams(has_side_effects=True),
    )(slab)


def _pack_b(w_mem_kv, w_br_dn, w_br_sb, w_br_mem, w_out, conv_w):
    conv = jnp.pad(conv_w.reshape(3, D_MODEL), ((0, B_ROWS - B_CONV - 3), (0, 0)))
    return jnp.concatenate([w_mem_kv.reshape(128, D_MODEL), w_br_dn, w_br_sb, w_br_mem.reshape(64, D_MODEL), w_out,
                            conv], axis=0)


def _unpack_b(slab):
    return (slab[B_MEMKV:B_BRDN].reshape(1, 256, 512), slab[B_BRDN:B_BRSB].reshape(1, 256, D_MODEL),
            slab[B_BRSB:B_BRMEM].reshape(1, 256, D_MODEL), slab[B_BRMEM:B_OUT].reshape(1, 256, 256),
            slab[B_OUT:B_CONV].reshape(1, 256, D_MODEL), slab[B_CONV:B_CONV + 3].reshape(1, 4, 768))


def _pack_small(norm_g, mem_norm_g, final_g, dn_norm_g, a_log, dt_bias, loss=None):
    slab = jnp.zeros((32, 128), F32)
    slab = slab.at[S_NORM:S_NORM + 8].set(norm_g.reshape(8, 128))
    slab = slab.at[S_MEMNORM:S_MEMNORM + 8].set(mem_norm_g.reshape(8, 128))
    slab = slab.at[S_FINAL:S_FINAL + 8].set(final_g.reshape(8, 128))
    slab = slab.at[S_DNNORM].set(dn_norm_g.reshape(128))
    slab = slab.at[S_ALOG, :N_HEADS].set(a_log.reshape(N_HEADS))
    slab = slab.at[S_DTB, :N_HEADS].set(dt_bias.reshape(N_HEADS))
    if loss is not None:
        slab = slab.at[S_LOSS, 0].set(loss)
    return slab


def _unpack_small(slab):
    return (slab[S_NORM:S_NORM + 8].reshape(1, D_MODEL), slab[S_MEMNORM:S_MEMNORM + 8].reshape(1, D_MODEL),
            slab[S_FINAL:S_FINAL + 8].reshape(D_MODEL), slab[S_DNNORM].reshape(1, 128),
            slab[S_ALOG, :N_HEADS].reshape(1, N_HEADS), slab[S_DTB, :N_HEADS].reshape(1, N_HEADS))


def _reorder_w_in(w_full):
    pad = jnp.zeros((w_full.shape[0], W_R - IN_WIDTH), w_full.dtype)
    return jnp.concatenate([w_full[:, :4096], w_full[:, 4112:], w_full[:, 4096:4112], pad], axis=1)


def _windows_to_w_r(win):
    b = 128
    s0, s1, s2, s3 = win[0], win[1], win[2], win[3]
    e1, e2, e3 = WIN_START[1] * b, WIN_START[2] * b, WIN_START[3] * b
    n1, n2 = e2 - e1, e3 - e2
    return jnp.concatenate([
        s0[:, :e1], s0[:, e1:e1 + b] + s1[:, :b],
        s1[:, b:n1], s1[:, n1:n1 + b] + s2[:, :b],
        s2[:, b:n2], s2[:, n2:n2 + b] + s3[:, :b],
        s3[:, b:], s1[:, _S1_BA_POS:]], axis=1)


def _dproj_windows(dproj_r):
    b = 128
    pieces = []
    for s in range(N_SHARD):
        lo = WIN_START[s] * b
        if s == 1:
            pieces += [dproj_r[:, lo:lo + _S1_BA_POS], dproj_r[:, C_BA:C_BA + b]]
        else:
            pieces.append(dproj_r[:, lo:lo + SHARD_PAD])
    return jnp.concatenate(pieces, axis=1)


def _local_step(x, mem, tgt, norm_g, mem_norm_g, w_r, w_sh, conv_w, a_log, dt_bias, dn_norm_g, w_mem_kv, w_br_dn,
                w_br_sb, w_br_mem, w_out, final_g):
    t = x.shape[0]
    final_row = final_g.reshape(1, D_MODEL)
    alog_row = jnp.zeros((1, 128), F32).at[0, N_HEADS:2 * N_HEADS].set(a_log.reshape(N_HEADS))
    dtb_row = jnp.zeros((1, 128), F32).at[0, N_HEADS:2 * N_HEADS].set(dt_bias.reshape(N_HEADS))

    h = _rmsnorm_fwd(x, norm_g, "norm_fwd")
    proj = _mm(h, w_r, "nn", "in_proj")
    qkv = _dn_prep_fwd(proj, conv_w)
    beta_t, g_t = _dn_gate_fwd(proj, alog_row, dtb_row)
    dn_u, dn_w, dn_qd, dn_kd, dn_a, tinv_all, dn_el = _dn_intra_fwd(qkv, beta_t, g_t)
    o_dn, dn_vn, s_all = _dn_scan_fwd(dn_u, dn_w, dn_qd, dn_kd, dn_a, dn_el)
    o_dn_g = _dn_post_fwd(o_dn, proj, dn_norm_g)
    o_sb, o_sb_g, sb_l = _sb_fwd(proj)
    mem_n = _rmsnorm_fwd(mem, mem_norm_g, "mem_norm_fwd")
    mkv = _mm(mem_n, w_mem_kv, "nn", "mem_kv")
    o_m, o_m_g = _mem_fwd(proj, mkv)
    y_dn = _mm(o_dn_g, w_br_dn, "nn", "br_dn")
    y_sb = _mm(o_sb_g, w_br_sb, "nn", "br_sb")
    y_m = _mm(o_m_g, w_br_mem, "nn", "br_mem")
    merged = _merge_fwd(proj, y_dn, y_sb, y_m)
    mo = _mm(merged, w_out, "nn", "out_proj")
    d_out, d_out_b, loss_row, g_final = _final_loss(x, mo, final_row, tgt)

    g_w_out = _mm(merged, d_out_b, "tn", "g_w_out")
    d_merged = _mm(d_out_b, w_out, "nt", "d_merged")
    dy_dn, dy_sb, dy_m, dg1, dg2, dg3 = _merge_bwd(proj, y_dn, y_sb, y_m, d_merged)
    g_w_br_dn = _mm(o_dn_g, dy_dn, "tn", "g_w_br_dn")
    g_w_br_sb = _mm(o_sb_g, dy_sb, "tn", "g_w_br_sb")
    g_w_br_mem = _mm(o_m_g, dy_m, "tn", "g_w_br_mem")
    d_o_dn_g = _mm(dy_dn, w_br_dn, "nt", "d_o_dn")
    d_o_sb_g = _mm(dy_sb, w_br_sb, "nt", "d_o_sb")
    d_o_m_g = _mm(dy_m, w_br_mem, "nt", "d_o_mem")

    d_mq, d_mz, d_mkv = _mem_bwd(proj, mkv, o_m, d_o_m_g)
    d_mkv_b = _cast_bf16(d_mkv, "cast_dmkv")
    g_w_mem_kv = _mm(mem_n, d_mkv_b, "tn", "g_w_mem_kv")
    d_mem_n = _mm(d_mkv_b, w_mem_kv, "nt", "d_mem_n")
    _, g_mem_norm = _rmsnorm_bwd(mem, mem_norm_g, d_mem_n, jnp.zeros_like(mem), "mem_norm_bwd")

    d_sq, d_sk, d_sv, d_sz = _sb_bwd(proj, o_sb, sb_l, d_o_sb_g)

    d_o_dn, d_dnz, g_dn_norm = _dn_post_bwd(o_dn, proj, dn_norm_g, d_o_dn_g)
    d_vnew, d_kd, d_qd, d_w, d_el = _dn_scan_bwd(dn_w, dn_qd, dn_kd, dn_a, dn_el, dn_vn, s_all, d_o_dn)
    d_qn, d_kn, d_vn, dbeta_t, dg_t = _dn_intra_bwd(qkv, beta_t, g_t, tinv_all, dn_vn, d_o_dn, d_vnew, d_kd, d_qd, d_w, d_el)
    d_conv_in, g_conv = _dn_prep_bwd(proj, conv_w, d_qn, d_kn, d_vn)
    d_ba, g_alog_row, g_dtb_row = _dn_gate_bwd(proj, alog_row, dtb_row, dbeta_t, dg_t)

    dproj_sh = _dproj_windows(
        jnp.concatenate([d_conv_in, d_dnz, d_sq, d_sk, d_sv, d_sz, d_mq, d_mz, dg1, dg2, dg3, d_ba], axis=1))
    g_w_sh = _mm(h, dproj_sh, "tn", "g_w_in", out_dtype=BF16, out_shards=N_SHARD)
    big = dict(w_sh=g_w_sh, conv_w=g_conv, w_mem_kv=g_w_mem_kv, w_br_dn=g_w_br_dn, w_br_sb=g_w_br_sb,
               w_br_mem=g_w_br_mem, w_out=g_w_out)

    def input_grad(after=None):
        dh = _mm(dproj_sh, w_sh, "nt", "d_h", after=after)
        grad_x, g_norm = _rmsnorm_bwd(x, norm_g, dh, d_out, "norm_bwd")
        small = dict(norm_g=g_norm, mem_norm_g=g_mem_norm, final_g=g_final, dn_norm_g=g_dn_norm,
                     a_log=g_alog_row[:, N_HEADS:2 * N_HEADS], dt_bias=g_dtb_row[:, N_HEADS:2 * N_HEADS])
        return grad_x, small

    return loss_row[0, 0], big, input_grad


def _reduce_scatter_start(grads):
    c = lax.axis_index("c")
    core = jnp.reshape(c, (1,)).astype(jnp.int32)
    recv = _pair_reduce_send(grads)
    parts = [_pair_add(g, r, core, "pair_add") for g, r in zip(grads, recv)]
    return _chip_exchange_start(parts)


def _reduce_scatter_finish(handle, after):
    send_sems, recv_sems, parts, lands, _ = handle
    x, y, c = lax.axis_index("x"), lax.axis_index("y"), lax.axis_index("c")
    place = jnp.stack([2 * x + y, c]).astype(jnp.int32)
    parts, by_chip = _chip_exchange_wait(send_sems, recv_sems, parts, lands, after)
    fulls = [_chip_sum(p, b, place, "chip_sum") for p, b in zip(parts, by_chip)]
    return _pair_allgather(fulls)


def kernel(x, mem, norm_g, mem_norm_g, w_in, conv_w, a_log, dt_bias, dn_norm_g, w_mem_kv, w_br_dn, w_br_sb, w_br_mem, w_out, final_g, loss_target, m_norm_g, m_mem_norm_g, m_w_in, m_conv_w, m_a_log, m_dt_bias, m_dn_norm_g, m_w_mem_kv, m_w_br_dn, m_w_br_sb, m_w_br_mem, m_w_out, m_final_g, v_norm_g, v_mem_norm_g, v_w_in, v_conv_w, v_a_log, v_dt_bias, v_dn_norm_g, v_w_mem_kv, v_w_br_dn, v_w_br_sb, v_w_br_mem, v_w_out, v_final_g):
    w_a = w_in[0]
    w_b = _pack_b(w_mem_kv[0], w_br_dn[0], w_br_sb[0], w_br_mem[0], w_out[0], conv_w[0])
    m_b = _pack_b(m_w_mem_kv[0], m_w_br_dn[0], m_w_br_sb[0], m_w_br_mem[0], m_w_out[0], m_conv_w[0])
    v_b = _pack_b(v_w_mem_kv[0], v_w_br_dn[0], v_w_br_sb[0], v_w_br_mem[0], v_w_out[0], v_conv_w[0])

    shard = jnp.reshape(2 * lax.axis_index("x") + lax.axis_index("y"), (1,)).astype(jnp.int32)
    ga, gb = _gather_shards([_cast_to_window(w_a, shard, "cast_w_in"), _cast_bf16(w_b, "cast_w_b")])
    w_r = _windows_to_w_r(ga)
    f_mem_kv = gb[:, B_MEMKV:B_BRDN].reshape(N_SHARD * 256, 512)
    f_br_dn = gb[:, B_BRDN:B_BRSB].reshape(N_SHARD * 256, D_MODEL)
    f_br_sb = gb[:, B_BRSB:B_BRMEM].reshape(N_SHARD * 256, D_MODEL)
    f_br_mem = gb[:, B_BRMEM:B_OUT].reshape(N_SHARD, 256, 256).transpose(1, 0, 2).reshape(256, D_MODEL)
    f_out = gb[:, B_OUT:B_CONV].reshape(N_SHARD * 256, D_MODEL)
    f_conv = gb[:, B_CONV:B_CONV + 3].reshape(N_SHARD, 4, 768).transpose(1, 0, 2).reshape(4, 3 * D_MODEL).astype(F32)

    loss, big, input_grad = _local_step(
        x[0], mem[0], loss_target[0], norm_g, mem_norm_g, w_r, ga, f_conv, a_log, dt_bias, dn_norm_g,
        f_mem_kv, f_br_dn, f_br_sb, f_br_mem, f_out, final_g)

    g_b = jnp.stack([
        _pack_b(big["w_mem_kv"][256 * s:256 * (s + 1)], big["w_br_dn"][256 * s:256 * (s + 1)],
                big["w_br_sb"][256 * s:256 * (s + 1)], big["w_br_mem"][:, 256 * s:256 * (s + 1)],
                big["w_out"][256 * s:256 * (s + 1)], big["conv_w"][:, 768 * s:768 * (s + 1)])
        for s in range(N_SHARD)]).astype(BF16)
    in_flight = _reduce_scatter_start([big["w_sh"], g_b])
    grad_x, small = input_grad(after=in_flight[4])

    part = _pack_small(small["norm_g"], small["mem_norm_g"], small["final_g"], small["dn_norm_g"],
                       small["a_log"], small["dt_bias"], loss)
    w_s = _pack_small(norm_g, mem_norm_g, final_g, dn_norm_g, a_log, dt_bias)
    m_s = _pack_small(m_norm_g, m_mem_norm_g, m_final_g, m_dn_norm_g, m_a_log, m_dt_bias)
    v_s = _pack_small(v_norm_g, v_mem_norm_g, v_final_g, v_dn_norm_g, v_a_log, v_dt_bias)
    g_s, d_s, nm_s, nv_s = _small_update(_allgather_small(part), w_s, m_s, v_s)

    gs_in, gs_b = _reduce_scatter_finish(in_flight, after=g_s)
    gr_in, d_in, nm_in, nv_in = _adamw_window(w_a, gs_in, m_w_in[0], v_w_in[0], shard, "adamw_w_in")
    gr_b, d_b, nm_b, nv_b = _adamw(w_b, gs_b, m_b, v_b, "adamw_b")

    def assemble(slab_small, a_in, slab_b):
        s_norm, s_memnorm, s_final, s_dnnorm, s_alog, s_dtb = _unpack_small(slab_small)
        b_memkv, b_brdn, b_brsb, b_brmem, b_out, b_conv = _unpack_b(slab_b)
        return [s_norm, s_memnorm, a_in.reshape(1, D_MODEL, IN_WIDTH // N_SHARD), b_conv, s_alog, s_dtb, s_dnnorm,
                b_memkv, b_brdn, b_brsb, b_brmem, b_out, s_final]

    outs = [g_s[S_LOSS, 0], grad_x.reshape(1, -1, D_MODEL)]
    outs += assemble(g_s, gr_in, gr_b)
    outs += assemble(d_s, d_in, d_b)
    outs += assemble(nm_s, nm_in, nm_b)
    outs += assemble(nv_s, nv_in, nv_b)
    return tuple(outs)
```

```python
import functools
import math

import jax
import jax.numpy as jnp
from jax import lax
from jax.experimental import pallas as pl
from jax.experimental.pallas import tpu as pltpu

F32 = jnp.float32
BF16 = jnp.bfloat16
MESH = pl.DeviceIdType.MESH
HIGHEST = lax.Precision.HIGHEST

D_MODEL = 1024
N_HEADS = 8
D_HEAD = 128
DN_CHUNK = 64
DN_GROUP = 16
DN_SCAN_GROUP = 4
SB_BLOCK = 256
SB_HEADS_PER_STEP = 2
SB_QBLOCK = 256
MEM_HEADS = 4
MEM_DH = 64
MEM_W = MEM_HEADS * MEM_DH
NORM_EPS = 1e-6
IN_WIDTH = 11792
N_SHARD = 4
SHARD_W = IN_WIDTH // N_SHARD
SHARD_PAD = 3072
N_DEV = 8

C_DNZ = 3072
C_SBQ = 4096
C_SBZ = 7168
C_MQ = 8192
C_MZ = 8448
C_GATES = 8704
C_BA = 11776
W_R = 11904

ADAM_LR = 0.001
ADAM_B1 = 0.9
ADAM_B2 = 0.999
ADAM_EPS = 1e-08
ADAM_WD = 0.01
ADAM_STEP = 10

VMEM_LIMIT = 56 * 1024 * 1024

B_ROWS = 992
B_MEMKV, B_BRDN, B_BRSB, B_BRMEM, B_OUT, B_CONV = 0, 128, 384, 640, 704, 960
S_NORM, S_MEMNORM, S_FINAL, S_DNNORM, S_ALOG, S_DTB, S_LOSS, S_CONV, S_ROWS = 0, 8, 16, 24, 25, 26, 27, 32, 128
CONV_BLOCKS = 3 * D_MODEL // 128


def _cp(**kw):
    return pltpu.CompilerParams(vmem_limit_bytes=VMEM_LIMIT, **kw)


def _dot(a, b, dims):
    lead = a.ndim - 2
    ca, cb = {"nn": (1, 0), "nt": (1, 1), "tn": (0, 0)}[dims]
    batch = tuple(range(lead))
    return lax.dot_general(a, b, (((ca + lead,), (cb + lead,)), (batch, batch)), preferred_element_type=F32)


def _chunks(x):
    return x.reshape(x.shape[0] // DN_CHUNK, DN_CHUNK, x.shape[1])


def _unchunk(x):
    return x.reshape(x.shape[0] * x.shape[1], x.shape[2])


def _bdot(a, b, dims):
    return _dot(a.astype(BF16), b.astype(BF16), dims)


def _split(a):
    hi = a.astype(BF16)
    return hi, (a - hi.astype(F32)).astype(BF16)


def _dot3(a, b, dims):
    a1, a2 = _split(a)
    b1, b2 = _split(b)
    return _dot(a1, b1, dims) + (_dot(a1, b2, dims) + _dot(a2, b1, dims))


def _ones_dot(a, ones_bf16):
    out = _dot(a.reshape(-1, a.shape[-1]).astype(BF16), ones_bf16, "nn")
    return out.reshape(a.shape[:-1] + (ones_bf16.shape[1],))


def _sigmoid(x):
    return 1.0 / (1.0 + jnp.exp(-x))


def _log1p_small(u):
    return jnp.where(u < 1e-2, u * (1.0 - u * (0.5 - u * (1.0 / 3.0))), jnp.log(1.0 + u))


def _log_sigmoid(z):
    return jnp.minimum(z, 0.0) - _log1p_small(jnp.exp(-jnp.abs(z)))


def _pick(dim, cands):
    for c in cands:
        if dim % c == 0:
            return c
    return dim


def _mm(a, b, dims, name, out_dtype=F32, out_shards=1, after=None):
    ta, tb = dims[0] == "t", dims[1] == "t"
    m, k = (a.shape[1], a.shape[0]) if ta else a.shape
    b_shards = b.shape[0] if b.ndim == 3 else 1
    n = b.shape[-2] if tb else b.shape[-1]
    tm = _pick(m, (1024, 512, 256))
    tn = _pick(n // out_shards, (512, 384, 256, 128))
    tk = _pick(k // b_shards, (1024, 512, 384, 256))
    nk = k // tk

    def body(a_ref, b_ref, *rest):
        o_ref, acc_ref = rest[-2:]
        kk = pl.program_id(2)

        @pl.when(kk == 0)
        def _():
            acc_ref[...] = jnp.zeros_like(acc_ref)

        acc_ref[...] += _bdot(a_ref[...], b_ref[...], dims)

        @pl.when(kk == nk - 1)
        def _():
            o_ref[...] = acc_ref[...].astype(out_dtype)

    a_spec = pl.BlockSpec((tk, tm), lambda i, j, q: (q, i)) if ta else pl.BlockSpec((tm, tk), lambda i, j, q: (i, q))
    if b_shards > 1:
        per_k = k // b_shards // tk
        b_spec = pl.BlockSpec((None, tn, tk), lambda i, j, q: (q // per_k, j, q % per_k))
    else:
        b_spec = pl.BlockSpec((tn, tk), lambda i, j, q: (j, q)) if tb else pl.BlockSpec((tk, tn), lambda i, j, q: (q, j))
    if out_shards > 1:
        per_n = n // out_shards // tn
        out_spec = pl.BlockSpec((None, tm, tn), lambda i, j, q: (j // per_n, i, j % per_n))
        out_shape = jax.ShapeDtypeStruct((out_shards, m, n // out_shards), out_dtype)
    else:
        out_spec = pl.BlockSpec((tm, tn), lambda i, j, q: (i, j))
        out_shape = jax.ShapeDtypeStruct((m, n), out_dtype)
    extra_specs, extra = [], []
    if after is not None:
        extra_specs, extra = [pl.BlockSpec(after.shape, lambda i, j, q: (0, 0))], [after]
    return pl.pallas_call(
        body, name=name, grid=(m // tm, n // tn, nk),
        in_specs=[a_spec, b_spec] + extra_specs, out_specs=out_spec, out_shape=out_shape,
        scratch_shapes=[pltpu.VMEM((tm, tn), F32)],
        compiler_params=_cp(dimension_semantics=("parallel", "parallel", "arbitrary")),
    )(a, b, *extra)


def _rmsnorm_fwd(x, g, name):
    t, d = x.shape
    tb = _pick(t, (512, 256))

    def body(x_ref, g_ref, h_ref):
        xv = x_ref[...]
        r = lax.rsqrt(jnp.mean(xv * xv, axis=-1, keepdims=True) + NORM_EPS)
        h_ref[...] = ((xv * r) * g_ref[...]).astype(BF16)

    return pl.pallas_call(
        body, name=name, grid=(t // tb,),
        in_specs=[pl.BlockSpec((tb, d), lambda i: (i, 0)), pl.BlockSpec((1, d), lambda i: (0, 0))],
        out_specs=pl.BlockSpec((tb, d), lambda i: (i, 0)),
        out_shape=jax.ShapeDtypeStruct((t, d), BF16), compiler_params=_cp(),
    )(x, g)


def _rmsnorm_bwd(x, g, dh, resid, name):
    t, d = x.shape
    tb = _pick(t, (256,))

    def body(x_ref, g_ref, dh_ref, r_ref, dx_ref, dg_ref):
        @pl.when(pl.program_id(0) == 0)
        def _():
            dg_ref[...] = jnp.zeros_like(dg_ref)

        xv = x_ref[...]
        r = lax.rsqrt(jnp.mean(xv * xv, axis=-1, keepdims=True) + NORM_EPS)
        xhat = xv * r
        dhv = dh_ref[...]
        dg_ref[...] += jnp.sum(dhv * xhat, axis=0, keepdims=True)
        dxh = dhv * g_ref[...]
        dx_ref[...] = r_ref[...] + r * (dxh - xhat * jnp.mean(dxh * xhat, axis=-1, keepdims=True))

    row = pl.BlockSpec((tb, d), lambda i: (i, 0))
    vec = pl.BlockSpec((1, d), lambda i: (0, 0))
    return pl.pallas_call(
        body, name=name, grid=(t // tb,), in_specs=[row, vec, row, row], out_specs=[row, vec],
        out_shape=[jax.ShapeDtypeStruct((t, d), F32), jax.ShapeDtypeStruct((1, d), F32)], compiler_params=_cp(),
    )(x, g, dh, resid)


def _conv_silu(xv, w, row):
    y = xv * w[3:4, :]
    for s in (1, 2, 3):
        xs = jnp.where(row >= s, pltpu.roll(xv, s, 0), 0.0)
        y = y + xs * w[3 - s:4 - s, :]
    return y, y * _sigmoid(y)


def _dn_prep_fwd(proj, conv_w):
    t = proj.shape[0]

    def body(p_ref, w_ref, o_ref):
        j = pl.program_id(0)
        xv = p_ref[...]
        row = lax.broadcasted_iota(jnp.int32, xv.shape, 0)
        _, a = _conv_silu(xv, w_ref[...], row)
        inv = lax.rsqrt(jnp.sum(a * a, axis=-1, keepdims=True) + NORM_EPS)
        scale = jnp.where(j < N_HEADS, D_HEAD ** -0.5, 1.0)
        normed = jnp.where(j < 2 * N_HEADS, 1.0, 0.0)
        o_ref[...] = a * (normed * (inv * scale) + (1.0 - normed))

    return pl.pallas_call(
        body, name="dn_prep_fwd", grid=(3 * N_HEADS,),
        in_specs=[pl.BlockSpec((t, D_HEAD), lambda j: (0, j)), pl.BlockSpec((4, D_HEAD), lambda j: (0, j))],
        out_specs=pl.BlockSpec((t, D_HEAD), lambda j: (0, j)),
        out_shape=jax.ShapeDtypeStruct((t, 3 * D_MODEL), F32), compiler_params=_cp(),
    )(proj, conv_w)


def _dn_prep_bwd(proj, conv_w, dq, dk, dv):
    t = proj.shape[0]

    def body(p_ref, w_ref, dq_ref, dk_ref, dv_ref, dp_ref, dw_ref):
        j = pl.program_id(0)
        xv = p_ref[...]
        w = w_ref[...]
        row = lax.broadcasted_iota(jnp.int32, xv.shape, 0)
        y, a = _conv_silu(xv, w, row)
        part = jnp.zeros(xv.shape, jnp.int32) + j // N_HEADS
        dn = jnp.where(part == 0, dq_ref[...], jnp.where(part == 1, dk_ref[...], dv_ref[...]))
        inv = lax.rsqrt(jnp.sum(a * a, axis=-1, keepdims=True) + NORM_EPS)
        scale = jnp.where(j < N_HEADS, D_HEAD ** -0.5, 1.0)
        ds = dn * scale
        da_norm = inv * ds - a * (inv * inv * inv) * jnp.sum(ds * a, axis=-1, keepdims=True)
        normed = jnp.where(j < 2 * N_HEADS, 1.0, 0.0)
        da = normed * da_norm + (1.0 - normed) * dn
        s = _sigmoid(y)
        dy = da * (s * (1.0 + y * (1.0 - s)))
        dx = dy * w[3:4, :]
        dw_ref[3:4, :] = jnp.sum(dy * xv, axis=0, keepdims=True)
        for sft in (1, 2, 3):
            xs = jnp.where(row >= sft, pltpu.roll(xv, sft, 0), 0.0)
            dw_ref[3 - sft:4 - sft, :] = jnp.sum(dy * xs, axis=0, keepdims=True)
            dys = jnp.where(row < t - sft, pltpu.roll(dy, t - sft, 0), 0.0)
            dx = dx + dys * w[3 - sft:4 - sft, :]
        dp_ref[...] = dx.astype(BF16)

    blk = pl.BlockSpec((t, D_HEAD), lambda j: (0, j))
    wblk = pl.BlockSpec((4, D_HEAD), lambda j: (0, j))

    def grad(part):
        return pl.BlockSpec((t, D_HEAD), lambda j: (0, jnp.clip(j - part * N_HEADS, 0, N_HEADS - 1)))

    return pl.pallas_call(
        body, name="dn_prep_bwd", grid=(3 * N_HEADS,), in_specs=[blk, wblk, grad(0), grad(1), grad(2)],
        out_specs=[blk, wblk],
        out_shape=[jax.ShapeDtypeStruct((t, 3 * D_MODEL), BF16), jax.ShapeDtypeStruct((4, 3 * D_MODEL), F32)],
        compiler_params=_cp(),
    )(proj, conv_w, dq, dk, dv)


def _softplus_parts(xv):
    e = jnp.exp(-jnp.abs(xv))
    return jnp.maximum(xv, 0.0) + _log1p_small(e)


def _chunk_scan(v, row, reverse):
    t = v.shape[0]
    pos = row & (DN_CHUNK - 1)
    s = 1
    while s < DN_CHUNK:
        if reverse:
            v = v + jnp.where(pos < DN_CHUNK - s, pltpu.roll(v, t - s, 0), 0.0)
        else:
            v = v + jnp.where(pos >= s, pltpu.roll(v, s, 0), 0.0)
        s *= 2
    return v


def _dn_gate_fwd(proj, alog_row, dtb_row):
    t = proj.shape[0]

    def body(p_ref, al_ref, dt_ref, b_ref, g_ref):
        p = p_ref[...]
        row = lax.broadcasted_iota(jnp.int32, p.shape, 0)
        b_ref[...] = _sigmoid(p)
        g = -jnp.exp(al_ref[...]) * _softplus_parts(p + dt_ref[...])
        g_ref[...] = _chunk_scan(g, row, reverse=False)

    blk = pl.BlockSpec((t, 128), lambda i: (0, C_BA // 128))
    vec = pl.BlockSpec((1, 128), lambda i: (0, 0))
    out = pl.BlockSpec((t, 128), lambda i: (0, 0))
    return pl.pallas_call(
        body, name="dn_gate_fwd", grid=(1,), in_specs=[blk, vec, vec], out_specs=[out, out],
        out_shape=[jax.ShapeDtypeStruct((t, 128), F32)] * 2, compiler_params=_cp(),
    )(proj, alog_row, dtb_row)


def _dn_gate_bwd(proj, alog_row, dtb_row, dbeta, dgc):
    t = proj.shape[0]

    def body(p_ref, al_ref, dt_ref, db_ref, dg_ref, dp_ref, dal_ref, ddt_ref):
        p = p_ref[...]
        row = lax.broadcasted_iota(jnp.int32, p.shape, 0)
        lane = lax.broadcasted_iota(jnp.int32, p.shape, 1)
        s = _sigmoid(p)
        d_b = db_ref[...] * s * (1.0 - s)
        dg = _chunk_scan(dg_ref[...], row, reverse=True)
        xa = p + dt_ref[...]
        ea = jnp.exp(al_ref[...])
        g = -ea * _softplus_parts(xa)
        d_a = dg * (-ea) * _sigmoid(xa)
        dp_ref[...] = jnp.where(lane < N_HEADS, d_b, jnp.where(lane < 2 * N_HEADS, d_a, 0.0)).astype(BF16)
        dal_ref[...] = jnp.sum(dg * g, axis=0, keepdims=True)
        ddt_ref[...] = jnp.sum(d_a, axis=0, keepdims=True)

    blk = pl.BlockSpec((t, 128), lambda i: (0, C_BA // 128))
    vec = pl.BlockSpec((1, 128), lambda i: (0, 0))
    full = pl.BlockSpec((t, 128), lambda i: (0, 0))
    return pl.pallas_call(
        body, name="dn_gate_bwd", grid=(1,), in_specs=[blk, vec, vec, full, full], out_specs=[full, vec, vec],
        out_shape=[jax.ShapeDtypeStruct((t, 128), BF16), jax.ShapeDtypeStruct((1, 128), F32),
                   jax.ShapeDtypeStruct((1, 128), F32)], compiler_params=_cp(),
    )(proj, alog_row, dtb_row, dbeta, dgc)


def _col_to_row(col, eye):
    return jnp.sum(jnp.where(eye, col, 0.0), axis=-2, keepdims=True)


def _row_to_col(rowv, eye):
    return jnp.sum(jnp.where(eye, rowv, 0.0), axis=-1, keepdims=True)


def _tri_inverse(m, ri, ci):
    eye = (ri == ci).astype(F32)
    b16 = (ri >> 4) == (ci >> 4)
    b32 = (ri >> 5) == (ci >> 5)
    m1 = jnp.where(b16, m, 0.0)
    x = eye - m1
    p = _dot3(m1, m1, "nn")
    x = x + _dot3(x, p, "nn")
    p = _dot3(p, p, "nn")
    x = x + _dot3(x, p, "nn")
    p = _dot3(p, p, "nn")
    x = x + _dot3(x, p, "nn")
    c1 = jnp.where(jnp.logical_and(b32, jnp.logical_not(b16)), m, 0.0)
    x = x - _dot3(_dot3(x, c1, "nn"), x, "nn")
    c2 = jnp.where(b32, 0.0, m)
    x = x - _dot3(_dot3(x, c2, "nn"), x, "nn")
    return x


def _dn_chunk_common(q, k, gc, ri, ci):
    eye = ri == ci
    g_row = _col_to_row(gc, eye)
    diff = jnp.minimum(gc - g_row, 0.0)
    gam = jnp.where(ri >= ci, jnp.exp(diff), 0.0)
    kk = _bdot(k, k, "nt")
    qk = _bdot(q, k, "nt")
    rcol = lax.broadcasted_iota(jnp.int32, gc.shape, gc.ndim - 2)
    last = jnp.sum(jnp.where(rcol == DN_CHUNK - 1, gc, 0.0), axis=-2, keepdims=True)
    e_g = jnp.exp(gc)
    dec = jnp.exp(last - gc)
    return eye, gam, kk, qk, last, e_g, dec, rcol


def _dn_specs(t, rows_blk):
    def head(off):
        return pl.BlockSpec((rows_blk, D_HEAD), lambda g, h: (g, off + h))

    lanes = pl.BlockSpec((rows_blk, 128), lambda g, h: (g, 0))
    hm = pl.BlockSpec((None, rows_blk, D_HEAD), lambda g, h: (h, g, 0))
    sq = pl.BlockSpec((1, rows_blk, DN_CHUNK), lambda g, h: (h, g, 0))
    tile = pl.BlockSpec((1, rows_blk // DN_CHUNK, 8, 128), lambda g, h: (h, g, 0, 0))
    return head, lanes, hm, sq, tile


def _head_column(slab, lane_idx):
    lane = lax.broadcasted_iota(jnp.int32, slab.shape, 1)
    return _chunks(jnp.sum(jnp.where(lane == lane_idx, slab, 0.0), axis=1, keepdims=True))


def _dn_intra_fwd(qkv, beta_t, g_t):
    t = qkv.shape[0]
    n_chunks = t // DN_CHUNK
    rows_blk = min(DN_GROUP * DN_CHUNK, t)

    def body(q_ref, k_ref, v_ref, b_ref, g_ref, u_ref, w_ref, qd_ref, kd_ref, a_ref, ti_ref, el_ref):
        ri = lax.broadcasted_iota(jnp.int32, (DN_CHUNK, DN_CHUNK), 0)
        ci = lax.broadcasted_iota(jnp.int32, (DN_CHUNK, DN_CHUNK), 1)
        h = pl.program_id(1)
        q, k, v = (_chunks(r[...]) for r in (q_ref, k_ref, v_ref))
        b, gc = _head_column(b_ref[...], h), _head_column(g_ref[...], h + N_HEADS)
        _, gam, kk, qk, last, e_g, dec, _ = _dn_chunk_common(q, k, gc, ri, ci)
        tinv = _tri_inverse(jnp.where(ri > ci, b * kk * gam, 0.0), ri, ci)
        u_ref[...] = _unchunk(_bdot(tinv, v * b, "nn"))
        w_ref[...] = _unchunk(_bdot(tinv, k * (b * e_g), "nn"))
        qd_ref[...] = _unchunk(q * e_g)
        kd_ref[...] = _unchunk(k * dec)
        a_ref[0] = _unchunk(qk * gam)
        ti_ref[0] = _unchunk(tinv)
        el_ref[0] = jnp.broadcast_to(jnp.exp(last), (rows_blk // DN_CHUNK, 8, 128))

    head, lanes, hm, sq, tile = _dn_specs(t, rows_blk)
    act = jax.ShapeDtypeStruct((N_HEADS, t, D_HEAD), F32)
    sqs = jax.ShapeDtypeStruct((N_HEADS, t, DN_CHUNK), F32)
    return pl.pallas_call(
        body, name="dn_intra_fwd", grid=(t // rows_blk, N_HEADS),
        in_specs=[head(0), head(N_HEADS), head(2 * N_HEADS), lanes, lanes],
        out_specs=[hm] * 4 + [sq, sq, tile],
        out_shape=[act] * 4 + [sqs, sqs, jax.ShapeDtypeStruct((N_HEADS, n_chunks, 8, 128), F32)],
        compiler_params=_cp(),
    )(qkv, qkv, qkv, beta_t, g_t)


def _dn_scan_specs(t, rows_blk, reverse):
    n_groups = t // rows_blk

    def at(g):
        return n_groups - 1 - g if reverse else g

    per = rows_blk // DN_CHUNK
    act = pl.BlockSpec((N_HEADS, rows_blk, D_HEAD), lambda g: (0, at(g), 0))
    sq = pl.BlockSpec((N_HEADS, rows_blk, DN_CHUNK), lambda g: (0, at(g), 0))
    state = pl.BlockSpec((N_HEADS, per, D_HEAD, D_HEAD), lambda g: (0, at(g), 0, 0))
    tile = pl.BlockSpec((N_HEADS, per, 8, 128), lambda g: (0, at(g), 0, 0))
    return act, sq, state, tile


def _dn_scan_fwd(u, w, qd, kd, a, el):
    t = u.shape[1]
    n_chunks = t // DN_CHUNK
    rows_blk = DN_SCAN_GROUP * DN_CHUNK

    def body(u_ref, w_ref, qd_ref, kd_ref, a_ref, el_ref, o_ref, vn_ref, s_ref, s_scr):
        @pl.when(pl.program_id(0) == 0)
        def _():
            s_scr[...] = jnp.zeros_like(s_scr)

        for cc in range(DN_SCAN_GROUP):
            rows = slice(cc * DN_CHUNK, (cc + 1) * DN_CHUNK)
            s = s_scr[...]
            s_ref[:, cc] = s
            v_new = u_ref[:, rows, :] - _bdot(w_ref[:, rows, :], s, "nn")
            vn_ref[:, rows, :] = v_new
            o_ref[:, rows, :] = _bdot(qd_ref[:, rows, :], s, "nn") + _bdot(a_ref[:, rows, :], v_new, "nn")
            s_scr[...] = s * el_ref[:, cc][:, 0:1, :] + _bdot(kd_ref[:, rows, :], v_new, "tn")

    act, sq, state, tile = _dn_scan_specs(t, rows_blk, reverse=False)
    shp = jax.ShapeDtypeStruct((N_HEADS, t, D_HEAD), F32)
    return pl.pallas_call(
        body, name="dn_scan_fwd", grid=(t // rows_blk,),
        in_specs=[act, act, act, act, sq, tile], out_specs=[act, act, state],
        out_shape=[shp, shp, jax.ShapeDtypeStruct((N_HEADS, n_chunks, D_HEAD, D_HEAD), F32)],
        scratch_shapes=[pltpu.VMEM((N_HEADS, D_HEAD, D_HEAD), F32)],
        compiler_params=_cp(dimension_semantics=("arbitrary",)),
    )(u, w, qd, kd, a, el)


def _dn_scan_bwd(w, qd, kd, a, el, vn, s_all, do):
    t = w.shape[1]
    n_chunks = t // DN_CHUNK
    rows_blk = DN_SCAN_GROUP * DN_CHUNK

    def body(w_ref, qd_ref, kd_ref, a_ref, el_ref, vn_ref, s_ref, do_ref, dvn_ref, dkd_ref, dqd_ref, dw_ref, dl_ref, ds_scr):
        @pl.when(pl.program_id(0) == 0)
        def _():
            ds_scr[...] = jnp.zeros_like(ds_scr)

        for cc in reversed(range(DN_SCAN_GROUP)):
            rows = slice(cc * DN_CHUNK, (cc + 1) * DN_CHUNK)
            s = s_ref[:, cc]
            d_s = ds_scr[...]
            e_last = el_ref[:, cc][:, 0:1, :]
            d_o = do_ref[:, rows, :]
            dv_new = _bdot(a_ref[:, rows, :], d_o, "tn") + _bdot(kd_ref[:, rows, :], d_s, "nn")
            ds_scr[...] = d_s * e_last + _bdot(qd_ref[:, rows, :], d_o, "tn") - _bdot(w_ref[:, rows, :], dv_new, "tn")
            dvn_ref[:, rows, :] = dv_new
            dkd_ref[:, rows, :] = _bdot(vn_ref[:, rows, :], d_s, "nt")
            dqd_ref[:, rows, :] = _bdot(d_o, s, "nt")
            dw_ref[:, rows, :] = -_bdot(dv_new, s, "nt")
            dlast = jnp.sum(jnp.sum(d_s * s, axis=2, keepdims=True), axis=1, keepdims=True)
            dl_ref[:, cc] = jnp.broadcast_to(dlast * e_last, (N_HEADS, 8, 128))

    act, sq, state, tile = _dn_scan_specs(t, rows_blk, reverse=True)
    shp = jax.ShapeDtypeStruct((N_HEADS, t, D_HEAD), F32)
    return pl.pallas_call(
        body, name="dn_scan_bwd", grid=(t // rows_blk,),
        in_specs=[act, act, act, sq, tile, act, state, act], out_specs=[act] * 4 + [tile],
        out_shape=[shp] * 4 + [jax.ShapeDtypeStruct((N_HEADS, n_chunks, 8, 128), F32)],
        scratch_shapes=[pltpu.VMEM((N_HEADS, D_HEAD, D_HEAD), F32)],
        compiler_params=_cp(dimension_semantics=("arbitrary",)),
    )(w, qd, kd, a, el, vn, s_all, do)


def _dn_intra_bwd(qkv, beta_t, g_t, tinv_all, vn, do, dvn, dkd, dqd, dw, dl):
    t = qkv.shape[0]
    rows_blk = min(DN_GROUP * DN_CHUNK, t)

    def body(q_ref, k_ref, v_ref, b_ref, g_ref, ti_ref, vn_ref, do_ref, dvn_ref, dkd_ref, dqd_ref, dw_ref, dl_ref,
             dq_ref, dk_ref, dv_ref, db_ref, dg_ref):
        ri = lax.broadcasted_iota(jnp.int32, (DN_CHUNK, DN_CHUNK), 0)
        ci = lax.broadcasted_iota(jnp.int32, (DN_CHUNK, DN_CHUNK), 1)
        h = pl.program_id(1)
        q, k, v = (_chunks(r[...]) for r in (q_ref, k_ref, v_ref))
        b, gc = _head_column(b_ref[...], h), _head_column(g_ref[...], h + N_HEADS)
        tinv = _chunks(ti_ref[0])
        dv_new, dk_dec, dq_dec, d_w = (_chunks(r[...]) for r in (dvn_ref, dkd_ref, dqd_ref, dw_ref))
        eye, gam, kk, qk, _, e_g, dec, rcol = _dn_chunk_common(q, k, gc, ri, ci)
        bv = v * b
        bk = k * (b * e_g)

        d_a = jnp.where(ri >= ci, _bdot(_chunks(do_ref[...]), _chunks(vn_ref[...]), "nt"), 0.0)
        dbv = _bdot(tinv, dv_new, "tn")
        dbk = _bdot(tinv, d_w, "tn")
        d_tinv = _bdot(dv_new, bv, "nt") + _bdot(d_w, bk, "nt")
        d_m = -jnp.where(ri > ci, _dot3(_dot3(tinv, d_tinv, "tn"), tinv, "nt"), 0.0)

        d_kk = d_m * b * gam
        d_gam = d_m * b * kk + d_a * qk
        d_qk = d_a * gam
        dq_ref[...] = _unchunk(_bdot(d_qk, k, "nn") + dq_dec * e_g)
        dk_ref[...] = _unchunk(_bdot(d_qk, q, "tn") + _bdot(d_kk, k, "nn") + _bdot(d_kk, k, "tn")
                               + dk_dec * dec + dbk * (b * e_g))
        dv_ref[...] = _unchunk(dbv * b)
        d_b = _unchunk(jnp.sum(d_m * kk * gam, axis=-1, keepdims=True) + jnp.sum(dbv * v, axis=-1, keepdims=True)
                       + jnp.sum(dbk * k, axis=-1, keepdims=True) * e_g)

        xg = d_gam * gam
        kdk = jnp.sum(dk_dec * (k * dec), axis=-1, keepdims=True)
        d_gc = (jnp.sum(xg, axis=-1, keepdims=True) - _row_to_col(jnp.sum(xg, axis=-2, keepdims=True), eye)
                + jnp.sum(dq_dec * (q * e_g), axis=-1, keepdims=True) - kdk
                + jnp.sum(dbk * bk, axis=-1, keepdims=True))
        d_last_total = dl_ref[0][:, 0:1, 0:1] + jnp.sum(kdk, axis=-2, keepdims=True)
        d_g = _unchunk(d_gc + jnp.where(rcol == DN_CHUNK - 1, d_last_total, 0.0))

        @pl.when(h == 0)
        def _():
            db_ref[...] = jnp.zeros_like(db_ref)
            dg_ref[...] = jnp.zeros_like(dg_ref)

        lane = lax.broadcasted_iota(jnp.int32, db_ref.shape, 1)
        db_ref[...] += jnp.where(lane == h, d_b, 0.0)
        dg_ref[...] += jnp.where(lane == h + N_HEADS, d_g, 0.0)

    head, lanes, hm, sq, tile = _dn_specs(t, rows_blk)
    return pl.pallas_call(
        body, name="dn_intra_bwd", grid=(t // rows_blk, N_HEADS),
        in_specs=[head(0), head(N_HEADS), head(2 * N_HEADS), lanes, lanes, sq] + [hm] * 6 + [tile],
        out_specs=[head(0), head(0), head(0), lanes, lanes],
        out_shape=[jax.ShapeDtypeStruct((t, D_MODEL), F32)] * 3 + [jax.ShapeDtypeStruct((t, 128), F32)] * 2,
        compiler_params=_cp(),
    )(qkv, qkv, qkv, beta_t, g_t, tinv_all, vn, do, dvn, dkd, dqd, dw, dl)


def _dn_post_fwd(o, proj, gn):
    t = o.shape[1]

    def body(o_ref, z_ref, g_ref, out_ref):
        ov, z = o_ref[...], z_ref[...]
        r = lax.rsqrt(jnp.mean(ov * ov, axis=-1, keepdims=True) + NORM_EPS)
        out_ref[...] = (((ov * r) * g_ref[...]) * (z * _sigmoid(z))).astype(BF16)

    blk = pl.BlockSpec((t, D_HEAD), lambda h: (0, h))
    return pl.pallas_call(
        body, name="dn_post_fwd", grid=(N_HEADS,),
        in_specs=[pl.BlockSpec((None, t, D_HEAD), lambda h: (h, 0, 0)),
                  pl.BlockSpec((t, D_HEAD), lambda h: (0, C_DNZ // D_HEAD + h)),
                  pl.BlockSpec((1, D_HEAD), lambda h: (0, 0))],
        out_specs=blk, out_shape=jax.ShapeDtypeStruct((t, D_MODEL), BF16), compiler_params=_cp(),
    )(o, proj, gn)


def _dn_post_bwd(o, proj, gn, dout):
    t = o.shape[1]

    def body(o_ref, z_ref, g_ref, d_ref, do_ref, dz_ref, dg_ref):
        @pl.when(pl.program_id(0) == 0)
        def _():
            dg_ref[...] = jnp.zeros_like(dg_ref)

        ov, z, d = o_ref[...], z_ref[...], d_ref[...]
        r = lax.rsqrt(jnp.mean(ov * ov, axis=-1, keepdims=True) + NORM_EPS)
        ohat = ov * r
        s = _sigmoid(z)
        d_on = d * (z * s)
        dz_ref[...] = (d * (ohat * g_ref[...]) * (s * (1.0 + z * (1.0 - s)))).astype(BF16)
        dg_ref[...] += jnp.sum(d_on * ohat, axis=0, keepdims=True)
        dxh = d_on * g_ref[...]
        do_ref[...] = r * (dxh - ohat * jnp.mean(dxh * ohat, axis=-1, keepdims=True))

    blk = pl.BlockSpec((t, D_HEAD), lambda h: (0, h))
    hm = pl.BlockSpec((None, t, D_HEAD), lambda h: (h, 0, 0))
    vec = pl.BlockSpec((1, D_HEAD), lambda h: (0, 0))
    return pl.pallas_call(
        body, name="dn_post_bwd", grid=(N_HEADS,),
        in_specs=[hm, pl.BlockSpec((t, D_HEAD), lambda h: (0, C_DNZ // D_HEAD + h)), vec, blk],
        out_specs=[hm, blk, vec],
        out_shape=[jax.ShapeDtypeStruct((N_HEADS, t, D_HEAD), F32), jax.ShapeDtypeStruct((t, D_MODEL), BF16),
                   jax.ShapeDtypeStruct((1, D_HEAD), F32)], compiler_params=_cp(),
    )(o, proj, gn, dout)


def _sb_fwd(proj):
    t = proj.shape[0]
    qblk = min(SB_QBLOCK, t)
    scale = 1.0 / math.sqrt(D_HEAD)

    hp = SB_HEADS_PER_STEP
    wid = hp * D_HEAD

    def body(q_ref, k_ref, v_ref, z_ref, o_ref, og_ref, l_ref, qb, kb, vb):
        for hh in range(hp):
            hs = slice(hh * D_HEAD, (hh + 1) * D_HEAD)
            qb[hh] = q_ref[:, hs].astype(BF16)
            kb[hh] = k_ref[:, hs].astype(BF16)
            vb[hh] = v_ref[:, hs].astype(BF16)
        ri = lax.broadcasted_iota(jnp.int32, (qblk, SB_BLOCK), 0)
        ci = lax.broadcasted_iota(jnp.int32, (qblk, SB_BLOCK), 1)
        r2 = lax.broadcasted_iota(jnp.int32, (SB_BLOCK, SB_BLOCK), 0)
        c2 = lax.broadcasted_iota(jnp.int32, (SB_BLOCK, SB_BLOCK), 1)
        upper = (r2 > c2).astype(BF16)
        nkb = qblk // SB_BLOCK

        def qblock(i, carry):
            rows = pl.ds(pl.multiple_of(i * qblk, qblk), qblk)
            qi = qb[:, rows, :]

            def tile(j, st, on_diagonal):
                acc, c = st
                cols = pl.ds(pl.multiple_of(j * SB_BLOCK, SB_BLOCK), SB_BLOCK)
                z = _dot(qi, kb[:, cols, :], "nt") * scale
                lb = jnp.minimum(z, 0.0) - jnp.log(1.0 + jnp.exp(-jnp.abs(z)))
                lf = lb - z
                if on_diagonal:
                    mask = (j * SB_BLOCK + ci) < (i * qblk + ri)
                    lf = jnp.where(mask, lf, 0.0)
                att = jnp.exp(lb + (_ones_dot(lf, upper) + c))
                if on_diagonal:
                    att = jnp.where(mask, att, 0.0)
                acc = acc + _dot(att.astype(BF16), vb[:, cols, :], "nn")
                return acc, c + jnp.sum(lf, axis=-1, keepdims=True)

            st = (jnp.zeros((hp, qblk, D_HEAD), F32), jnp.zeros((hp, qblk, 1), F32))
            for d in range(nkb):
                st = tile((i + 1) * nkb - 1 - d, st, True)
            acc, c = lax.fori_loop(0, i * nkb, lambda jj, s: tile(i * nkb - 1 - jj, s, False), st)
            l_ref[:, rows, :] = c
            for hh in range(hp):
                hs = slice(hh * D_HEAD, (hh + 1) * D_HEAD)
                zg = z_ref[rows, hs]
                o_ref[rows, hs] = acc[hh]
                og_ref[rows, hs] = (acc[hh] * (zg * _sigmoid(zg))).astype(BF16)
            return carry

        lax.fori_loop(0, t // qblk, qblock, 0)

    def head(off):
        return pl.BlockSpec((t, wid), lambda h: (0, off // wid + h))

    out = pl.BlockSpec((t, wid), lambda h: (0, h))
    return pl.pallas_call(
        body, name="sb_fwd", grid=(N_HEADS // hp,),
        in_specs=[head(C_SBQ), head(C_SBQ + D_MODEL), head(C_SBQ + 2 * D_MODEL), head(C_SBZ)],
        out_specs=[out, out, pl.BlockSpec((hp, t, 1), lambda h: (h, 0, 0))],
        out_shape=[jax.ShapeDtypeStruct((t, D_MODEL), F32), jax.ShapeDtypeStruct((t, D_MODEL), BF16),
                   jax.ShapeDtypeStruct((N_HEADS, t, 1), F32)],
        scratch_shapes=[pltpu.VMEM((hp, t, D_HEAD), BF16)] * 3, compiler_params=_cp(),
    )(proj, proj, proj, proj)


def _sb_bwd(proj, o, ltot, dog, after=None):
    t = proj.shape[0]
    qblk = min(SB_QBLOCK, t)
    scale = 1.0 / math.sqrt(D_HEAD)

    hp = SB_HEADS_PER_STEP
    wid = hp * D_HEAD

    def body(q_ref, k_ref, v_ref, z_ref, o_ref, l_ref, d_ref, *rest):
        dq_ref, dk_ref, dv_ref, dz_ref, qb, kb, vb, dob, dk_scr, dv_scr = rest[-10:]
        for hh in range(hp):
            hs = slice(hh * D_HEAD, (hh + 1) * D_HEAD)
            qb[hh] = q_ref[:, hs].astype(BF16)
            kb[hh] = k_ref[:, hs].astype(BF16)
            vb[hh] = v_ref[:, hs].astype(BF16)
            zg = z_ref[:, hs]
            sg = _sigmoid(zg)
            dgo = d_ref[:, hs]
            dob[hh] = (dgo * (zg * sg)).astype(BF16)
            dz_ref[:, hs] = (dgo * o_ref[:, hs] * (sg * (1.0 + zg * (1.0 - sg)))).astype(BF16)
        dk_scr[...] = jnp.zeros_like(dk_scr)
        dv_scr[...] = jnp.zeros_like(dv_scr)
        ri = lax.broadcasted_iota(jnp.int32, (qblk, SB_BLOCK), 0)
        ci = lax.broadcasted_iota(jnp.int32, (qblk, SB_BLOCK), 1)
        r2 = lax.broadcasted_iota(jnp.int32, (SB_BLOCK, SB_BLOCK), 0)
        c2 = lax.broadcasted_iota(jnp.int32, (SB_BLOCK, SB_BLOCK), 1)
        upper = (r2 > c2).astype(BF16)
        below = (r2 < c2).astype(BF16)

        def qblock(i, carry):
            rows = pl.ds(pl.multiple_of(i * qblk, qblk), qblk)
            qi = qb[:, rows, :]
            d_o = dob[:, rows, :]
            ltot = l_ref[:, rows, :]

            def tile(j, st, on_diagonal):
                dq, cpre, ce = st
                cols = pl.ds(pl.multiple_of(j * SB_BLOCK, SB_BLOCK), SB_BLOCK)
                kj, vj = kb[:, cols, :], vb[:, cols, :]
                z = _dot(qi, kj, "nt") * scale
                lb = jnp.minimum(z, 0.0) - jnp.log(1.0 + jnp.exp(-jnp.abs(z)))
                lf = lb - z
                if on_diagonal:
                    mask = (j * SB_BLOCK + ci) < (i * qblk + ri)
                    lf = jnp.where(mask, lf, 0.0)
                tile_sum = jnp.sum(lf, axis=-1, keepdims=True)
                att = jnp.exp(lb + ((ltot - cpre - tile_sum) + _ones_dot(lf, upper)))
                if on_diagonal:
                    att = jnp.where(mask, att, 0.0)
                e = _dot(d_o, vj, "nt") * att
                dlf = ce + _ones_dot(e, below)
                dzz = e - (e + dlf) * jnp.exp(lb)
                if on_diagonal:
                    dzz = jnp.where(mask, dzz, 0.0)
                dzz = dzz.astype(BF16)
                dq = dq + _dot(dzz, kj, "nn")
                dk_scr[:, cols, :] += _dot(dzz, qi, "tn")
                dv_scr[:, cols, :] += _dot(att.astype(BF16), d_o, "tn")
                return dq, cpre + tile_sum, ce + jnp.sum(e, axis=-1, keepdims=True)

            nkb = qblk // SB_BLOCK
            zero_col = jnp.zeros((hp, qblk, 1), F32)
            st = lax.fori_loop(0, i * nkb, lambda j, s: tile(j, s, False),
                               (jnp.zeros((hp, qblk, D_HEAD), F32), zero_col, zero_col))
            for d in range(nkb):
                st = tile(i * nkb + d, st, True)
            dq = st[0]
            for hh in range(hp):
                dq_ref[rows, hh * D_HEAD:(hh + 1) * D_HEAD] = (dq[hh] * scale).astype(BF16)
            return carry

        lax.fori_loop(0, t // qblk, qblock, 0)
        for hh in range(hp):
            hs = slice(hh * D_HEAD, (hh + 1) * D_HEAD)
            dk_ref[:, hs] = (dk_scr[hh] * scale).astype(BF16)
            dv_ref[:, hs] = dv_scr[hh].astype(BF16)

    def head(off):
        return pl.BlockSpec((t, wid), lambda h: (0, off // wid + h))

    extra_specs, extra = [], []
    if after is not None:
        extra_specs, extra = [pl.BlockSpec(after.shape, lambda h: (0, 0))], [after]
    return pl.pallas_call(
        body, name="sb_bwd", grid=(N_HEADS // hp,),
        in_specs=[head(C_SBQ), head(C_SBQ + D_MODEL), head(C_SBQ + 2 * D_MODEL), head(C_SBZ), head(0),
                  pl.BlockSpec((hp, t, 1), lambda h: (h, 0, 0)), head(0)] + extra_specs,
        out_specs=[head(0)] * 4, out_shape=[jax.ShapeDtypeStruct((t, D_MODEL), BF16)] * 4,
        scratch_shapes=[pltpu.VMEM((hp, t, D_HEAD), BF16)] * 4 + [pltpu.VMEM((hp, t, D_HEAD), F32)] * 2,
        compiler_params=_cp(),
    )(proj, proj, proj, proj, o, ltot, dog, *extra)


def _mem_fwd(proj, mkv):
    t = proj.shape[0]
    tq = _pick(t, (512, 256))
    m_len = mkv.shape[0]
    scale = 1.0 / math.sqrt(MEM_DH)

    def body(q_ref, z_ref, kv_ref, o_ref, og_ref):
        q = q_ref[...]
        mk = kv_ref[:, :MEM_W].astype(BF16)
        mv = kv_ref[:, MEM_W:].astype(BF16)
        lane = lax.broadcasted_iota(jnp.int32, q.shape, 1) >> 6
        o = jnp.zeros(q.shape, F32)
        for h in range(MEM_HEADS):
            s = _bdot(jnp.where(lane == h, q, 0.0), mk, "nt") * scale
            p = jnp.exp(s - jnp.max(s, axis=-1, keepdims=True))
            p = p / jnp.sum(p, axis=-1, keepdims=True)
            o = o + jnp.where(lane == h, _bdot(p, mv, "nn"), 0.0)
        z = z_ref[...]
        o_ref[...] = o
        og_ref[...] = (o * (z * _sigmoid(z))).astype(BF16)

    out = pl.BlockSpec((tq, MEM_W), lambda i: (i, 0))
    return pl.pallas_call(
        body, name="mem_fwd", grid=(t // tq,),
        in_specs=[pl.BlockSpec((tq, MEM_W), lambda i: (i, C_MQ // MEM_W)),
                  pl.BlockSpec((tq, MEM_W), lambda i: (i, C_MZ // MEM_W)),
                  pl.BlockSpec((m_len, 2 * MEM_W), lambda i: (0, 0))],
        out_specs=[out, out],
        out_shape=[jax.ShapeDtypeStruct((t, MEM_W), F32), jax.ShapeDtypeStruct((t, MEM_W), BF16)],
        compiler_params=_cp(),
    )(proj, proj, mkv)


def _mem_bwd(proj, mkv, o, dog):
    t = proj.shape[0]
    tq = _pick(t, (512, 256))
    m_len = mkv.shape[0]
    scale = 1.0 / math.sqrt(MEM_DH)

    def body(q_ref, z_ref, kv_ref, o_ref, d_ref, dq_ref, dz_ref, dkv_ref):
        @pl.when(pl.program_id(0) == 0)
        def _():
            dkv_ref[...] = jnp.zeros_like(dkv_ref)

        q = q_ref[...]
        z = z_ref[...]
        sg = _sigmoid(z)
        dgo = d_ref[...]
        d_o = dgo * (z * sg)
        dz_ref[...] = (dgo * o_ref[...] * (sg * (1.0 + z * (1.0 - sg)))).astype(BF16)
        mk = kv_ref[:, :MEM_W].astype(BF16)
        mv = kv_ref[:, MEM_W:].astype(BF16)
        lane = lax.broadcasted_iota(jnp.int32, q.shape, 1) >> 6
        klane = lax.broadcasted_iota(jnp.int32, (m_len, MEM_W), 1) >> 6
        dq = jnp.zeros(q.shape, F32)
        dmk = jnp.zeros((m_len, MEM_W), F32)
        dmv = jnp.zeros((m_len, MEM_W), F32)
        for h in range(MEM_HEADS):
            qh = jnp.where(lane == h, q, 0.0)
            doh = jnp.where(lane == h, d_o, 0.0)
            s = _bdot(qh, mk, "nt") * scale
            p = jnp.exp(s - jnp.max(s, axis=-1, keepdims=True))
            p = p / jnp.sum(p, axis=-1, keepdims=True)
            dp = _bdot(doh, mv, "nt")
            ds = p * (dp - jnp.sum(dp * p, axis=-1, keepdims=True)) * scale
            dq = dq + jnp.where(lane == h, _bdot(ds, mk, "nn"), 0.0)
            dmk = dmk + jnp.where(klane == h, _bdot(ds, qh, "tn"), 0.0)
            dmv = dmv + jnp.where(klane == h, _bdot(p, doh, "tn"), 0.0)
        dq_ref[...] = dq.astype(BF16)
        dkv_ref[:, :MEM_W] += dmk
        dkv_ref[:, MEM_W:] += dmv

    blk = pl.BlockSpec((tq, MEM_W), lambda i: (i, 0))
    kv = pl.BlockSpec((m_len, 2 * MEM_W), lambda i: (0, 0))
    return pl.pallas_call(
        body, name="mem_bwd", grid=(t // tq,),
        in_specs=[pl.BlockSpec((tq, MEM_W), lambda i: (i, C_MQ // MEM_W)),
                  pl.BlockSpec((tq, MEM_W), lambda i: (i, C_MZ // MEM_W)), kv, blk, blk],
        out_specs=[blk, blk, kv],
        out_shape=[jax.ShapeDtypeStruct((t, MEM_W), BF16), jax.ShapeDtypeStruct((t, MEM_W), BF16),
                   jax.ShapeDtypeStruct((m_len, 2 * MEM_W), F32)], compiler_params=_cp(),
    )(proj, proj, mkv, o, dog)


_GW = 512


def _merge_fwd(proj, y_dn, y_sb, y_m):
    t = proj.shape[0]
    tb = _pick(t, (256,))
    nc = D_MODEL // _GW

    def body(g1, g2, g3, y1, y2, y3, out_ref):
        out_ref[...] = (_sigmoid(g1[...]) * y1[...] + _sigmoid(g2[...]) * y2[...] + _sigmoid(g3[...]) * y3[...]).astype(BF16)

    def gate(kb):
        return pl.BlockSpec((tb, _GW), lambda i, c: (i, C_GATES // _GW + kb * nc + c))

    blk = pl.BlockSpec((tb, _GW), lambda i, c: (i, c))
    return pl.pallas_call(
        body, name="merge_fwd", grid=(t // tb, nc), in_specs=[gate(0), gate(1), gate(2), blk, blk, blk],
        out_specs=blk, out_shape=jax.ShapeDtypeStruct((t, D_MODEL), BF16), compiler_params=_cp(),
    )(proj, proj, proj, y_dn, y_sb, y_m)


def _merge_bwd(proj, y_dn, y_sb, y_m, dm):
    t = proj.shape[0]
    tb = _pick(t, (256,))
    nc = D_MODEL // _GW

    def body(g1, g2, g3, y1, y2, y3, dm_ref, d1, d2, d3, dg1, dg2, dg3):
        d = dm_ref[...]
        for g, y, dy, dg in ((g1, y1, d1, dg1), (g2, y2, d2, dg2), (g3, y3, d3, dg3)):
            s = _sigmoid(g[...])
            dy[...] = (d * s).astype(BF16)
            dg[...] = (d * y[...] * (s * (1.0 - s))).astype(BF16)

    def gate(kb):
        return pl.BlockSpec((tb, _GW), lambda i, c: (i, C_GATES // _GW + kb * nc + c))

    blk = pl.BlockSpec((tb, _GW), lambda i, c: (i, c))
    act = jax.ShapeDtypeStruct((t, D_MODEL), BF16)
    return pl.pallas_call(
        body, name="merge_bwd", grid=(t // tb, nc), in_specs=[gate(0), gate(1), gate(2), blk, blk, blk, blk],
        out_specs=[blk] * 6, out_shape=[act] * 6, compiler_params=_cp(),
    )(proj, proj, proj, y_dn, y_sb, y_m, dm)


def _final_loss(x, mo, g, tgt):
    t, d = x.shape
    tb = _pick(t, (256,))

    def body(x_ref, mo_ref, g_ref, t_ref, do_ref, dob_ref, loss_ref, dg_ref):
        @pl.when(pl.program_id(0) == 0)
        def _():
            loss_ref[...] = jnp.zeros_like(loss_ref)
            dg_ref[...] = jnp.zeros_like(dg_ref)

        out = x_ref[...] + mo_ref[...]
        r = lax.rsqrt(jnp.mean(out * out, axis=-1, keepdims=True) + NORM_EPS)
        xhat = out * r
        gv = g_ref[...]
        err = xhat * gv - t_ref[...]
        per_tok = jnp.mean(err * err, axis=-1, keepdims=True)
        loss_ref[...] += 0.5 * jnp.sum(per_tok, axis=0, keepdims=True)
        dy = err * (1.0 / d)
        dg_ref[...] += jnp.sum(dy * xhat, axis=0, keepdims=True)
        dxh = dy * gv
        dout = r * (dxh - xhat * jnp.mean(dxh * xhat, axis=-1, keepdims=True))
        do_ref[...] = dout
        dob_ref[...] = dout.astype(BF16)

    row = pl.BlockSpec((tb, d), lambda i: (i, 0))
    vec = pl.BlockSpec((1, d), lambda i: (0, 0))
    return pl.pallas_call(
        body, name="final_loss", grid=(t // tb,), in_specs=[row, row, vec, row],
        out_specs=[row, row, pl.BlockSpec((1, 128), lambda i: (0, 0)), vec],
        out_shape=[jax.ShapeDtypeStruct((t, d), F32), jax.ShapeDtypeStruct((t, d), BF16),
                   jax.ShapeDtypeStruct((1, 128), F32), jax.ShapeDtypeStruct((1, d), F32)],
        compiler_params=_cp(),
    )(x, mo, g, tgt)


def _cast_bf16(a, name):
    r, c = a.shape
    tb = _pick(r, (128, 496, 240))

    def body(a_ref, o_ref):
        o_ref[...] = a_ref[...].astype(BF16)

    blk = pl.BlockSpec((tb, c), lambda i: (i, 0))
    return pl.pallas_call(body, name=name, grid=(r // tb,), in_specs=[blk], out_specs=blk,
                          out_shape=jax.ShapeDtypeStruct((r, c), BF16), compiler_params=_cp())(a)


WIN_START = (0, 23, 45, 68)
_S1_LO, _S1_HI = 1148, 1164
_S1_BA_POS = SHARD_PAD - 128


def _to_window(x, s):
    if s == 0:
        return x
    if s in (2, 3):
        return pltpu.roll(x, 120 if s == 2 else 124, 1)
    pos = lax.broadcasted_iota(jnp.int32, x.shape, 1)
    head = pltpu.roll(x, 4, 1)
    tail = pltpu.roll(x, SHARD_PAD - 12, 1)
    ba = jnp.where(pos < _S1_BA_POS + (_S1_HI - _S1_LO), pltpu.roll(x, _S1_BA_POS - _S1_LO, 1), 0.0)
    return jnp.where(pos < _S1_LO + 4, head, jnp.where(pos < _S1_BA_POS, tail, ba))


def _from_window(g, s):
    if s == 0:
        return g
    if s in (2, 3):
        return pltpu.roll(g, SHARD_PAD - (120 if s == 2 else 124), 1)
    col = lax.broadcasted_iota(jnp.int32, g.shape, 1)
    head = pltpu.roll(g, SHARD_PAD - 4, 1)
    tail = pltpu.roll(g, 12, 1)
    ba = pltpu.roll(g, SHARD_PAD - (_S1_BA_POS - _S1_LO), 1)
    return jnp.where(col < _S1_LO, head, jnp.where(col < _S1_HI, ba, tail))


def _cast_to_window(w, shard, name):
    r, c = w.shape
    tb = _pick(r, (128,))

    def body(s_ref, w_ref, o_ref, pad_scr):
        pad_scr[...] = jnp.zeros_like(pad_scr)
        pad_scr[:, :c] = w_ref[...]
        x = pad_scr[...]
        for s in range(N_SHARD):
            @pl.when(s_ref[0] == s)
            def _():
                o_ref[...] = _to_window(x, s).astype(BF16)

    return pl.pallas_call(
        body, name=name,
        grid_spec=pltpu.PrefetchScalarGridSpec(
            num_scalar_prefetch=1, grid=(r // tb,),
            in_specs=[pl.BlockSpec((tb, c), lambda i, s: (i, 0))],
            out_specs=pl.BlockSpec((tb, SHARD_PAD), lambda i, s: (i, 0)),
            scratch_shapes=[pltpu.VMEM((tb, SHARD_PAD), F32)]),
        out_shape=jax.ShapeDtypeStruct((r, SHARD_PAD), BF16), compiler_params=_cp(),
    )(shard, w)


def _pair_add(g, recv, c_idx, name):
    n, r, c = g.shape
    half = r // 2
    tb = _pick(half, (128, 240))
    nb = half // tb

    def body(c_ref, g_ref, r_ref, o_ref):
        o_ref[...] = (g_ref[...].astype(F32) + r_ref[...].astype(F32)).astype(BF16)

    blk = pl.BlockSpec((n, tb, c), lambda i, c_ref: (0, i, 0))
    return pl.pallas_call(
        body, name=name,
        grid_spec=pltpu.PrefetchScalarGridSpec(
            num_scalar_prefetch=1, grid=(nb,),
            in_specs=[pl.BlockSpec((n, tb, c), lambda i, c_ref: (0, c_ref[0] * nb + i, 0)), blk], out_specs=blk),
        out_shape=jax.ShapeDtypeStruct((n, half, c), BF16), compiler_params=_cp(),
    )(c_idx, g, recv)


def _chip_sum(parts, by_chip, place, name):
    n, h, c = parts.shape
    tb = _pick(h, (128, 240))
    nb = h // tb

    def body(p_ref, mine_ref, *rest):
        others, o_ref = rest[:n], rest[n]
        me = jnp.zeros((tb, c), jnp.int32) + p_ref[0]
        acc = None
        for q in range(n):
            term = jnp.where(me == q, mine_ref[...], others[q][...]).astype(F32)
            acc = term if acc is None else acc + term
        o_ref[...] = acc

    def other(q):
        return pl.BlockSpec((None, tb, c), lambda i, p: (jnp.where(p[0] == q, (q + 1) % n, q), i, 0))

    return pl.pallas_call(
        body, name=name,
        grid_spec=pltpu.PrefetchScalarGridSpec(
            num_scalar_prefetch=1, grid=(nb,),
            in_specs=[pl.BlockSpec((None, tb, c), lambda i, p: (p[0], i, 0))] + [other(q) for q in range(n)],
            out_specs=pl.BlockSpec((tb, c), lambda i, p: (p[1] * nb + i, 0))),
        out_shape=jax.ShapeDtypeStruct((2 * h, c), F32), compiler_params=_cp(),
    )(place, parts, *([by_chip] * n))


def _adamw_math(w, g, m, v):
    m = ADAM_B1 * m + (1.0 - ADAM_B1) * g
    v = ADAM_B2 * v + (1.0 - ADAM_B2) * (g * g)
    m_hat = m / (1.0 - ADAM_B1 ** ADAM_STEP)
    v_hat = v / (1.0 - ADAM_B2 ** ADAM_STEP)
    delta = -ADAM_LR * (m_hat / (jnp.sqrt(v_hat) + ADAM_EPS) + ADAM_WD * w)
    return delta, m, v


def _adamw(w, g, m, v, name):
    r, c = w.shape
    tb = _pick(r, (128, 496, 240))

    def body(w_ref, g_ref, m_ref, v_ref, go_ref, d_ref, mo_ref, vo_ref):
        gv = g_ref[...]
        d, mn, vn = _adamw_math(w_ref[...], gv, m_ref[...], v_ref[...])
        go_ref[...] = gv
        d_ref[...] = d
        mo_ref[...] = mn
        vo_ref[...] = vn

    blk = pl.BlockSpec((tb, c), lambda i: (i, 0))
    return pl.pallas_call(
        body, name=name, grid=(r // tb,), in_specs=[blk] * 4, out_specs=[blk] * 4,
        out_shape=[jax.ShapeDtypeStruct((r, c), F32)] * 4, compiler_params=_cp(),
    )(w, g, m, v)


def _adamw_window(w, g_win, m, v, shard, name):
    r, c = w.shape
    tb = _pick(r, (128,))

    def body(s_ref, w_ref, g_ref, m_ref, v_ref, go_ref, d_ref, mo_ref, vo_ref, g_scr):
        gw = g_ref[...]
        for s in range(N_SHARD):
            @pl.when(s_ref[0] == s)
            def _():
                g_scr[...] = _from_window(gw, s)

        gv = g_scr[:, :c]
        d, mn, vn = _adamw_math(w_ref[...], gv, m_ref[...], v_ref[...])
        go_ref[...] = gv
        d_ref[...] = d
        mo_ref[...] = mn
        vo_ref[...] = vn

    blk = pl.BlockSpec((tb, c), lambda i, s: (i, 0))
    return pl.pallas_call(
        body, name=name,
        grid_spec=pltpu.PrefetchScalarGridSpec(
            num_scalar_prefetch=1, grid=(r // tb,),
            in_specs=[blk, pl.BlockSpec((tb, SHARD_PAD), lambda i, s: (i, 0)), blk, blk], out_specs=[blk] * 4,
            scratch_shapes=[pltpu.VMEM((tb, SHARD_PAD), F32)]),
        out_shape=[jax.ShapeDtypeStruct((r, c), F32)] * 4, compiler_params=_cp(),
    )(shard, w, g_win, m, v)


def _small_update(gathered, w, m, v):
    def body(p_ref, w_ref, m_ref, v_ref, g_ref, d_ref, mo_ref, vo_ref):
        g = p_ref[0]
        for i in range(1, N_DEV):
            g = g + p_ref[i]
        d, mn, vn = _adamw_math(w_ref[...], g, m_ref[...], v_ref[...])
        g_ref[...] = g
        d_ref[...] = d
        mo_ref[...] = mn
        vo_ref[...] = vn

    full = pl.BlockSpec((S_ROWS, 128), lambda i: (0, 0))
    return pl.pallas_call(
        body, name="small_update", grid=(1,),
        in_specs=[pl.BlockSpec((N_DEV, S_ROWS, 128), lambda i: (0, 0, 0)), full, full, full], out_specs=[full] * 4,
        out_shape=[jax.ShapeDtypeStruct((S_ROWS, 128), F32)] * 4, compiler_params=_cp(),
    )(gathered, w, m, v)


_ANY = pl.BlockSpec(memory_space=pl.ANY)


def _place():
    x, y, c = lax.axis_index("x"), lax.axis_index("y"), lax.axis_index("c")
    chips = [(1 - x, y), (x, 1 - y), (1 - x, 1 - y)]
    return x, y, c, chips


def _gather_shards(arrs):
    n = len(arrs)

    def body(*refs):
        ins, outs = refs[:n], refs[n:2 * n]
        send_sems, recv_sems, local_sems = refs[2 * n:2 * n + 3]
        bufs = refs[2 * n + 3:]
        x, y, c, chips = _place()
        me = 2 * x + y
        sibling = (x, y, 1 - c)
        sends = []
        for a in range(n):
            half = ins[a].shape[0] // 2
            mine = pl.ds(pl.multiple_of(c * half, 16), half)
            for j, (qx, qy) in enumerate(chips):
                cp = pltpu.make_async_remote_copy(
                    src_ref=ins[a].at[mine], dst_ref=outs[a].at[me, mine],
                    send_sem=send_sems.at[6 * a + j], recv_sem=recv_sems.at[6 * a + j],
                    device_id=(qx, qy, c), device_id_type=MESH)
                cp.start()
                sends.append(cp)
        for a in range(n):
            step = bufs[a].shape[0]
            for r0 in range(0, ins[a].shape[0], step):
                rows = pl.ds(r0, step)
                load = pltpu.make_async_copy(ins[a].at[rows], bufs[a], local_sems.at[2 * a])
                load.start()
                load.wait()
                store = pltpu.make_async_copy(bufs[a], outs[a].at[me, rows], local_sems.at[2 * a + 1])
                store.start()
                store.wait()
        for a in range(n):
            half = ins[a].shape[0] // 2
            mine = pl.ds(pl.multiple_of(c * half, 16), half)
            for j, (qx, qy) in enumerate(chips):
                q = 2 * qx + qy
                landed = outs[a].at[q, mine]
                pltpu.make_async_remote_copy(
                    src_ref=landed, dst_ref=landed, send_sem=send_sems.at[6 * a + j], recv_sem=recv_sems.at[6 * a + j],
                    device_id=(qx, qy, c), device_id_type=MESH).wait_recv()
                fw = pltpu.make_async_remote_copy(
                    src_ref=landed, dst_ref=landed, send_sem=send_sems.at[6 * a + 3 + j],
                    recv_sem=recv_sems.at[6 * a + 3 + j], device_id=sibling, device_id_type=MESH)
                fw.start()
                sends.append(fw)
        for a in range(n):
            half = ins[a].shape[0] // 2
            theirs = pl.ds(pl.multiple_of((1 - c) * half, 16), half)
            for j, (qx, qy) in enumerate(chips):
                q = 2 * qx + qy
                dst = outs[a].at[q, theirs]
                pltpu.make_async_remote_copy(
                    src_ref=dst, dst_ref=dst, send_sem=send_sems.at[6 * a + 3 + j], recv_sem=recv_sems.at[6 * a + 3 + j],
                    device_id=sibling, device_id_type=MESH).wait_recv()
        for cp in sends:
            cp.wait_send()

    return pl.pallas_call(
        body, name="gather_shards", in_specs=[_ANY] * n, out_specs=[_ANY] * n,
        out_shape=[jax.ShapeDtypeStruct((N_SHARD,) + a.shape, a.dtype) for a in arrs],
        scratch_shapes=[pltpu.SemaphoreType.DMA((6 * n,)), pltpu.SemaphoreType.DMA((6 * n,)),
                        pltpu.SemaphoreType.DMA((2 * n,))]
        + [pltpu.VMEM((_pick(a.shape[0], (256, 496)), a.shape[1]), a.dtype) for a in arrs],
        compiler_params=pltpu.CompilerParams(has_side_effects=True, vmem_limit_bytes=VMEM_LIMIT),
    )(*arrs)


def _pair_reduce_send(grads, tag):
    n = len(grads)

    def body(*refs):
        ins, outs = refs[:n], refs[n:2 * n]
        send_sems, recv_sems = refs[2 * n:]
        x, y, c, _ = _place()
        sibling = (x, y, 1 - c)
        cps = []
        for a in range(n):
            half = ins[a].shape[1] // 2
            theirs = pl.ds(pl.multiple_of((1 - c) * half, 8), half)
            cp = pltpu.make_async_remote_copy(
                src_ref=ins[a].at[:, theirs], dst_ref=outs[a], send_sem=send_sems.at[a], recv_sem=recv_sems.at[a],
                device_id=sibling, device_id_type=MESH)
            cp.start()
            cps.append(cp)
        for cp in cps:
            cp.wait()

    return pl.pallas_call(
        body, name="pair_reduce_send_" + tag, in_specs=[_ANY] * n, out_specs=[_ANY] * n,
        out_shape=[jax.ShapeDtypeStruct((g.shape[0], g.shape[1] // 2, g.shape[2]), g.dtype) for g in grads],
        scratch_shapes=[pltpu.SemaphoreType.DMA((n,)), pltpu.SemaphoreType.DMA((n,))],
        compiler_params=pltpu.CompilerParams(has_side_effects=True),
    )(*grads)


_HBM = pl.BlockSpec(memory_space=pltpu.HBM)
_SEM = pl.BlockSpec(memory_space=pltpu.SEMAPHORE)
_DATAFLOW = pltpu.SideEffectType.DATAFLOW_SIDE_EFFECTING


def _chip_exchange_copies(ins, lands, send_sems, recv_sems):
    x, y, c, chips = _place()
    me = 2 * x + y
    cps = []
    for a in range(len(ins)):
        for j, (qx, qy) in enumerate(chips):
            cps.append(pltpu.make_async_remote_copy(
                src_ref=ins[a].at[2 * qx + qy], dst_ref=lands[a].at[me], send_sem=send_sems.at[3 * a + j],
                recv_sem=recv_sems.at[3 * a + j], device_id=(qx, qy, c), device_id_type=MESH))
    return cps


def _chip_exchange_start(parts, tag):
    n = len(parts)

    def body(*refs):
        ins, lands = refs[:n], refs[n:2 * n]
        send_sems, recv_sems = refs[2 * n:2 * n + 2]
        token = refs[4 * n + 2]
        for cp in _chip_exchange_copies(ins, lands, send_sems, recv_sems):
            cp.start()
        token[...] = jnp.zeros_like(token)

    hbm = [pltpu.HBM(p.shape, p.dtype) for p in parts]
    lands = [pltpu.with_memory_space_constraint(lax.empty(p.shape, p.dtype), pltpu.HBM) for p in parts]
    res = pl.pallas_call(
        body, name="chip_exchange_start_" + tag,
        out_shape=(pltpu.SemaphoreType.DMA((3 * n,)), pltpu.SemaphoreType.DMA((3 * n,)), *hbm, *hbm,
                   jax.ShapeDtypeStruct((8, 128), F32)),
        in_specs=[_HBM] * (2 * n), out_specs=(_SEM, _SEM, *([_HBM] * (2 * n)), pl.BlockSpec(memory_space=pltpu.VMEM)),
        input_output_aliases={a: 2 + a for a in range(2 * n)},
        compiler_params=pltpu.CompilerParams(has_side_effects=_DATAFLOW),
    )(*[pltpu.with_memory_space_constraint(p, pltpu.HBM) for p in parts], *lands)
    return res[0], res[1], res[2:2 + n], res[2 + n:2 + 2 * n], res[2 + 2 * n]


def _chip_exchange_wait(send_sems, recv_sems, parts, lands, after, tag):
    n = len(parts)

    def body(*refs):
        ins, land_refs = refs[:n], refs[n:2 * n]
        s_sems, r_sems = refs[2 * n:2 * n + 2]
        for cp in _chip_exchange_copies(ins, land_refs, s_sems, r_sems):
            cp.wait_send()
            cp.wait_recv()

    hbm = [pltpu.HBM(p.shape, p.dtype) for p in parts]
    res = pl.pallas_call(
        body, name="chip_exchange_wait_" + tag, out_shape=(*hbm, *hbm),
        in_specs=[_HBM] * (2 * n) + [_SEM, _SEM, _ANY], out_specs=tuple([_HBM] * (2 * n)),
        input_output_aliases={a: a for a in range(2 * n)},
        compiler_params=pltpu.CompilerParams(has_side_effects=_DATAFLOW),
    )(*parts, *lands, send_sems, recv_sems, after)
    return res[:n], res[n:]


def _pair_allgather(fulls, tag):
    n = len(fulls)

    def body(*refs):
        outs = refs[n:2 * n]
        send_sems, recv_sems = refs[2 * n:]
        x, y, c, _ = _place()
        sibling = (x, y, 1 - c)
        cps = []
        for a in range(n):
            half = outs[a].shape[0] // 2
            mine = outs[a].at[pl.ds(pl.multiple_of(c * half, 8), half)]
            cp = pltpu.make_async_remote_copy(
                src_ref=mine, dst_ref=mine, send_sem=send_sems.at[a], recv_sem=recv_sems.at[a],
                device_id=sibling, device_id_type=MESH)
            cp.start()
            cps.append(cp)
        for a in range(n):
            half = outs[a].shape[0] // 2
            theirs = outs[a].at[pl.ds(pl.multiple_of((1 - c) * half, 8), half)]
            pltpu.make_async_remote_copy(
                src_ref=theirs, dst_ref=theirs, send_sem=send_sems.at[a], recv_sem=recv_sems.at[a],
                device_id=sibling, device_id_type=MESH).wait_recv()
        for cp in cps:
            cp.wait_send()

    return pl.pallas_call(
        body, name="pair_allgather_" + tag, in_specs=[_ANY] * n, out_specs=[_ANY] * n,
        out_shape=[jax.ShapeDtypeStruct(f.shape, f.dtype) for f in fulls],
        input_output_aliases={a: a for a in range(n)},
        scratch_shapes=[pltpu.SemaphoreType.DMA((n,)), pltpu.SemaphoreType.DMA((n,))],
        compiler_params=pltpu.CompilerParams(has_side_effects=True),
    )(*fulls)


def _allgather_small(slab):
    def body(s_ref, out_ref, send_sems, recv_sems):
        x, y, c, _ = _place()
        me = 4 * x + 2 * y + c
        out_ref[me] = s_ref[...]
        cps = []
        for mask in range(1, N_DEV):
            peer = (x ^ (mask >> 2), y ^ ((mask >> 1) & 1), c ^ (mask & 1))
            cp = pltpu.make_async_remote_copy(
                src_ref=s_ref, dst_ref=out_ref.at[me], send_sem=send_sems.at[mask - 1], recv_sem=recv_sems.at[mask - 1],
                device_id=peer, device_id_type=MESH)
            cp.start()
            cps.append(cp)
        for mask in range(1, N_DEV):
            peer = (x ^ (mask >> 2), y ^ ((mask >> 1) & 1), c ^ (mask & 1))
            dst = out_ref.at[4 * peer[0] + 2 * peer[1] + peer[2]]
            pltpu.make_async_remote_copy(
                src_ref=dst, dst_ref=dst, send_sem=send_sems.at[mask - 1], recv_sem=recv_sems.at[mask - 1],
                device_id=peer, device_id_type=MESH).wait_recv()
        for cp in cps:
            cp.wait_send()

    vm = pl.BlockSpec(memory_space=pltpu.VMEM)
    return pl.pallas_call(
        body, name="allgather_small", in_specs=[vm], out_specs=vm,
        out_shape=jax.ShapeDtypeStruct((N_DEV,) + slab.shape, slab.dtype),
        scratch_shapes=[pltpu.SemaphoreType.DMA((N_DEV - 1,)), pltpu.SemaphoreType.DMA((N_DEV - 1,))],
        compiler_params=pltpu.CompilerParams(has_side_effects=True),
    )(slab)


def _pack_b(w_mem_kv, w_br_dn, w_br_sb, w_br_mem, w_out):
    return jnp.concatenate([w_mem_kv.reshape(128, D_MODEL), w_br_dn, w_br_sb, w_br_mem.reshape(64, D_MODEL), w_out],
                           axis=0)


def _with_conv_rows(slab, conv_w):
    conv = jnp.pad(conv_w.reshape(3, D_MODEL), ((0, B_ROWS - B_CONV - 3), (0, 0)))
    return jnp.concatenate([slab, conv], axis=0)


def _unpack_b(slab):
    return (slab[B_MEMKV:B_BRDN].reshape(1, 256, 512), slab[B_BRDN:B_BRSB].reshape(1, 256, D_MODEL),
            slab[B_BRSB:B_BRMEM].reshape(1, 256, D_MODEL), slab[B_BRMEM:B_OUT].reshape(1, 256, 256),
            slab[B_OUT:B_CONV].reshape(1, 256, D_MODEL))


def _conv_rows(conv_full):
    return conv_full.reshape(4 * CONV_BLOCKS, 128)


def _conv_shard_rows(conv_shard, shard):
    own = CONV_BLOCKS // N_SHARD
    blocks = lax.dynamic_update_slice(jnp.zeros((4, CONV_BLOCKS, 128), F32), conv_shard.reshape(4, own, 128),
                                      (0, own * shard, 0))
    return blocks.reshape(4 * CONV_BLOCKS, 128)


def _conv_shard_of(rows, shard):
    own = CONV_BLOCKS // N_SHARD
    blocks = lax.dynamic_slice(rows.reshape(4, CONV_BLOCKS, 128), (0, own * shard, 0), (4, own, 128))
    return blocks.reshape(1, 4, own * 128)


def _pack_small(norm_g, mem_norm_g, final_g, dn_norm_g, a_log, dt_bias, conv_rows, loss=None):
    slab = jnp.zeros((S_CONV, 128), F32)
    slab = slab.at[S_NORM:S_NORM + 8].set(norm_g.reshape(8, 128))
    slab = slab.at[S_MEMNORM:S_MEMNORM + 8].set(mem_norm_g.reshape(8, 128))
    slab = slab.at[S_FINAL:S_FINAL + 8].set(final_g.reshape(8, 128))
    slab = slab.at[S_DNNORM].set(dn_norm_g.reshape(128))
    slab = slab.at[S_ALOG, :N_HEADS].set(a_log.reshape(N_HEADS))
    slab = slab.at[S_DTB, :N_HEADS].set(dt_bias.reshape(N_HEADS))
    if loss is not None:
        slab = slab.at[S_LOSS, 0].set(loss)
    return jnp.concatenate([slab, conv_rows], axis=0)


def _unpack_small(slab, shard):
    return (slab[S_NORM:S_NORM + 8].reshape(1, D_MODEL), slab[S_MEMNORM:S_MEMNORM + 8].reshape(1, D_MODEL),
            slab[S_FINAL:S_FINAL + 8].reshape(D_MODEL), slab[S_DNNORM].reshape(1, 128),
            slab[S_ALOG, :N_HEADS].reshape(1, N_HEADS), slab[S_DTB, :N_HEADS].reshape(1, N_HEADS),
            _conv_shard_of(slab[S_CONV:], shard))


def _reorder_w_in(w_full):
    pad = jnp.zeros((w_full.shape[0], W_R - IN_WIDTH), w_full.dtype)
    return jnp.concatenate([w_full[:, :4096], w_full[:, 4112:], w_full[:, 4096:4112], pad], axis=1)


def _windows_to_w_r(win):
    b = 128
    s0, s1, s2, s3 = win[0], win[1], win[2], win[3]
    e1, e2, e3 = WIN_START[1] * b, WIN_START[2] * b, WIN_START[3] * b
    n1, n2 = e2 - e1, e3 - e2
    return jnp.concatenate([
        s0[:, :e1], s0[:, e1:e1 + b] + s1[:, :b],
        s1[:, b:n1], s1[:, n1:n1 + b] + s2[:, :b],
        s2[:, b:n2], s2[:, n2:n2 + b] + s3[:, :b],
        s3[:, b:], s1[:, _S1_BA_POS:]], axis=1)


def _dproj_windows(dproj_r):
    b = 128
    pieces = []
    for s in range(N_SHARD):
        lo = WIN_START[s] * b
        if s == 1:
            pieces += [dproj_r[:, lo:lo + _S1_BA_POS], dproj_r[:, C_BA:C_BA + b]]
        else:
            pieces.append(dproj_r[:, lo:lo + SHARD_PAD])
    return jnp.concatenate(pieces, axis=1)


def _local_step(x, mem, tgt, norm_g, mem_norm_g, w_r, w_sh, conv_w, a_log, dt_bias, dn_norm_g, w_mem_kv, w_br_dn,
                w_br_sb, w_br_mem, w_out, final_g, on_early=None):
    t = x.shape[0]
    final_row = final_g.reshape(1, D_MODEL)
    alog_row = jnp.zeros((1, 128), F32).at[0, N_HEADS:2 * N_HEADS].set(a_log.reshape(N_HEADS))
    dtb_row = jnp.zeros((1, 128), F32).at[0, N_HEADS:2 * N_HEADS].set(dt_bias.reshape(N_HEADS))

    h = _rmsnorm_fwd(x, norm_g, "norm_fwd")
    proj = _mm(h, w_r, "nn", "in_proj")
    qkv = _dn_prep_fwd(proj, conv_w)
    beta_t, g_t = _dn_gate_fwd(proj, alog_row, dtb_row)
    dn_u, dn_w, dn_qd, dn_kd, dn_a, tinv_all, dn_el = _dn_intra_fwd(qkv, beta_t, g_t)
    o_dn, dn_vn, s_all = _dn_scan_fwd(dn_u, dn_w, dn_qd, dn_kd, dn_a, dn_el)
    o_dn_g = _dn_post_fwd(o_dn, proj, dn_norm_g)
    o_sb, o_sb_g, sb_l = _sb_fwd(proj)
    mem_n = _rmsnorm_fwd(mem, mem_norm_g, "mem_norm_fwd")
    mkv = _mm(mem_n, w_mem_kv, "nn", "mem_kv")
    o_m, o_m_g = _mem_fwd(proj, mkv)
    y_dn = _mm(o_dn_g, w_br_dn, "nn", "br_dn")
    y_sb = _mm(o_sb_g, w_br_sb, "nn", "br_sb")
    y_m = _mm(o_m_g, w_br_mem, "nn", "br_mem")
    merged = _merge_fwd(proj, y_dn, y_sb, y_m)
    mo = _mm(merged, w_out, "nn", "out_proj")
    d_out, d_out_b, loss_row, g_final = _final_loss(x, mo, final_row, tgt)

    g_w_out = _mm(merged, d_out_b, "tn", "g_w_out")
    d_merged = _mm(d_out_b, w_out, "nt", "d_merged")
    dy_dn, dy_sb, dy_m, dg1, dg2, dg3 = _merge_bwd(proj, y_dn, y_sb, y_m, d_merged)
    g_w_br_dn = _mm(o_dn_g, dy_dn, "tn", "g_w_br_dn")
    g_w_br_sb = _mm(o_sb_g, dy_sb, "tn", "g_w_br_sb")
    g_w_br_mem = _mm(o_m_g, dy_m, "tn", "g_w_br_mem")
    d_o_dn_g = _mm(dy_dn, w_br_dn, "nt", "d_o_dn")
    d_o_sb_g = _mm(dy_sb, w_br_sb, "nt", "d_o_sb")
    d_o_m_g = _mm(dy_m, w_br_mem, "nt", "d_o_mem")

    d_mq, d_mz, d_mkv = _mem_bwd(proj, mkv, o_m, d_o_m_g)
    d_mkv_b = _cast_bf16(d_mkv, "cast_dmkv")
    g_w_mem_kv = _mm(mem_n, d_mkv_b, "tn", "g_w_mem_kv")
    d_mem_n = _mm(d_mkv_b, w_mem_kv, "nt", "d_mem_n")
    _, g_mem_norm = _rmsnorm_bwd(mem, mem_norm_g, d_mem_n, jnp.zeros_like(mem), "mem_norm_bwd")

    early = dict(w_mem_kv=g_w_mem_kv, w_br_dn=g_w_br_dn, w_br_sb=g_w_br_sb, w_br_mem=g_w_br_mem, w_out=g_w_out)
    after_early = on_early(early) if on_early is not None else None

    d_sq, d_sk, d_sv, d_sz = _sb_bwd(proj, o_sb, sb_l, d_o_sb_g, after=after_early)

    d_o_dn, d_dnz, g_dn_norm = _dn_post_bwd(o_dn, proj, dn_norm_g, d_o_dn_g)
    d_vnew, d_kd, d_qd, d_w, d_el = _dn_scan_bwd(dn_w, dn_qd, dn_kd, dn_a, dn_el, dn_vn, s_all, d_o_dn)
    d_qn, d_kn, d_vn, dbeta_t, dg_t = _dn_intra_bwd(qkv, beta_t, g_t, tinv_all, dn_vn, d_o_dn, d_vnew, d_kd, d_qd, d_w, d_el)
    d_conv_in, g_conv = _dn_prep_bwd(proj, conv_w, d_qn, d_kn, d_vn)
    d_ba, g_alog_row, g_dtb_row = _dn_gate_bwd(proj, alog_row, dtb_row, dbeta_t, dg_t)

    dproj_sh = _dproj_windows(
        jnp.concatenate([d_conv_in, d_dnz, d_sq, d_sk, d_sv, d_sz, d_mq, d_mz, dg1, dg2, dg3, d_ba], axis=1))
    g_w_sh = _mm(h, dproj_sh, "tn", "g_w_in", out_dtype=BF16, out_shards=N_SHARD)
    def input_grad(after=None):
        dh = _mm(dproj_sh, w_sh, "nt", "d_h", after=after)
        grad_x, g_norm = _rmsnorm_bwd(x, norm_g, dh, d_out, "norm_bwd")
        small = dict(norm_g=g_norm, mem_norm_g=g_mem_norm, final_g=g_final, dn_norm_g=g_dn_norm,
                     a_log=g_alog_row[:, N_HEADS:2 * N_HEADS], dt_bias=g_dtb_row[:, N_HEADS:2 * N_HEADS],
                     conv_w=g_conv)
        return grad_x, small

    return loss_row[0, 0], early, g_w_sh, input_grad


def _reduce_scatter_start(grads, tag):
    c = lax.axis_index("c")
    core = jnp.reshape(c, (1,)).astype(jnp.int32)
    recv = _pair_reduce_send(grads, tag)
    parts = [_pair_add(g, r, core, "pair_add_" + tag) for g, r in zip(grads, recv)]
    return _chip_exchange_start(parts, tag)


def _reduce_scatter_finish(handle, after, tag):
    send_sems, recv_sems, parts, lands, _ = handle
    x, y, c = lax.axis_index("x"), lax.axis_index("y"), lax.axis_index("c")
    place = jnp.stack([2 * x + y, c]).astype(jnp.int32)
    parts, by_chip = _chip_exchange_wait(send_sems, recv_sems, parts, lands, after, tag)
    fulls = [_chip_sum(p, b, place, "chip_sum_" + tag) for p, b in zip(parts, by_chip)]
    return _pair_allgather(fulls, tag)


def kernel(x, mem, norm_g, mem_norm_g, w_in, conv_w, a_log, dt_bias, dn_norm_g, w_mem_kv, w_br_dn, w_br_sb, w_br_mem, w_out, final_g, loss_target, m_norm_g, m_mem_norm_g, m_w_in, m_conv_w, m_a_log, m_dt_bias, m_dn_norm_g, m_w_mem_kv, m_w_br_dn, m_w_br_sb, m_w_br_mem, m_w_out, m_final_g, v_norm_g, v_mem_norm_g, v_w_in, v_conv_w, v_a_log, v_dt_bias, v_dn_norm_g, v_w_mem_kv, v_w_br_dn, v_w_br_sb, v_w_br_mem, v_w_out, v_final_g):
    w_a = w_in[0]
    w_b = _pack_b(w_mem_kv[0], w_br_dn[0], w_br_sb[0], w_br_mem[0], w_out[0])
    m_b = _pack_b(m_w_mem_kv[0], m_w_br_dn[0], m_w_br_sb[0], m_w_br_mem[0], m_w_out[0])
    v_b = _pack_b(v_w_mem_kv[0], v_w_br_dn[0], v_w_br_sb[0], v_w_br_mem[0], v_w_out[0])

    shard_idx = 2 * lax.axis_index("x") + lax.axis_index("y")
    shard = jnp.reshape(shard_idx, (1,)).astype(jnp.int32)
    ga, gb = _gather_shards([_cast_to_window(w_a, shard, "cast_w_in"),
                             _cast_bf16(_with_conv_rows(w_b, conv_w[0]), "cast_w_b")])
    w_r = _windows_to_w_r(ga)
    f_mem_kv = gb[:, B_MEMKV:B_BRDN].reshape(N_SHARD * 256, 512)
    f_br_dn = gb[:, B_BRDN:B_BRSB].reshape(N_SHARD * 256, D_MODEL)
    f_br_sb = gb[:, B_BRSB:B_BRMEM].reshape(N_SHARD * 256, D_MODEL)
    f_br_mem = gb[:, B_BRMEM:B_OUT].reshape(N_SHARD, 256, 256).transpose(1, 0, 2).reshape(256, D_MODEL)
    f_out = gb[:, B_OUT:B_CONV].reshape(N_SHARD * 256, D_MODEL)
    f_conv = gb[:, B_CONV:B_CONV + 3].reshape(N_SHARD, 4, 768).transpose(1, 0, 2).reshape(4, 3 * D_MODEL).astype(F32)

    flights = {}

    def on_early(grads):
        g_b = jnp.stack([
            _pack_b(grads["w_mem_kv"][256 * s:256 * (s + 1)], grads["w_br_dn"][256 * s:256 * (s + 1)],
                    grads["w_br_sb"][256 * s:256 * (s + 1)], grads["w_br_mem"][:, 256 * s:256 * (s + 1)],
                    grads["w_out"][256 * s:256 * (s + 1)])
            for s in range(N_SHARD)]).astype(BF16)
        flights["b"] = _reduce_scatter_start([g_b], "b")
        return flights["b"][4]

    loss, _, g_w_sh, input_grad = _local_step(
        x[0], mem[0], loss_target[0], norm_g, mem_norm_g, w_r, ga, f_conv, a_log, dt_bias, dn_norm_g,
        f_mem_kv, f_br_dn, f_br_sb, f_br_mem, f_out, final_g, on_early=on_early)
    flights["a"] = _reduce_scatter_start([g_w_sh], "a")
    grad_x, small = input_grad(after=flights["a"][4])

    part = _pack_small(small["norm_g"], small["mem_norm_g"], small["final_g"], small["dn_norm_g"],
                       small["a_log"], small["dt_bias"], _conv_rows(small["conv_w"]), loss)
    w_s = _pack_small(norm_g, mem_norm_g, final_g, dn_norm_g, a_log, dt_bias, _conv_shard_rows(conv_w[0], shard_idx))
    m_s = _pack_small(m_norm_g, m_mem_norm_g, m_final_g, m_dn_norm_g, m_a_log, m_dt_bias,
                      _conv_shard_rows(m_conv_w[0], shard_idx))
    v_s = _pack_small(v_norm_g, v_mem_norm_g, v_final_g, v_dn_norm_g, v_a_log, v_dt_bias,
                      _conv_shard_rows(v_conv_w[0], shard_idx))
    g_s, d_s, nm_s, nv_s = _small_update(_allgather_small(part), w_s, m_s, v_s)

    (gs_b,) = _reduce_scatter_finish(flights["b"], after=g_s, tag="b")
    (gs_in,) = _reduce_scatter_finish(flights["a"], after=gs_b, tag="a")
    gr_in, d_in, nm_in, nv_in = _adamw_window(w_a, gs_in, m_w_in[0], v_w_in[0], shard, "adamw_w_in")
    gr_b, d_b, nm_b, nv_b = _adamw(w_b, gs_b, m_b, v_b, "adamw_b")

    def assemble(slab_small, a_in, slab_b):
        s_norm, s_memnorm, s_final, s_dnnorm, s_alog, s_dtb, b_conv = _unpack_small(slab_small, shard_idx)
        b_memkv, b_brdn, b_brsb, b_brmem, b_out = _unpack_b(slab_b)
        return [s_norm, s_memnorm, a_in.reshape(1, D_MODEL, IN_WIDTH // N_SHARD), b_conv, s_alog, s_dtb, s_dnnorm,
                b_memkv, b_brdn, b_brsb, b_brmem, b_out, s_final]

    outs = [g_s[S_LOSS, 0], grad_x.reshape(1, -1, D_MODEL)]
    outs += assemble(g_s, gr_in, gr_b)
    outs += assemble(d_s, d_in, d_b)
    outs += assemble(nm_s, nm_in, nm_b)
    outs += assemble(nv_s, nv_in, nv_b)
    return tuple(outs)
```

```python
import functools
import math

import jax
import jax.numpy as jnp
from jax import lax
from jax.experimental import pallas as pl
from jax.experimental.pallas import tpu as pltpu

F32 = jnp.float32
BF16 = jnp.bfloat16
MESH = pl.DeviceIdType.MESH
HIGHEST = lax.Precision.HIGHEST

D_MODEL = 1024
N_HEADS = 8
D_HEAD = 128
DN_CHUNK = 64
DN_GROUP = 16
DN_SCAN_GROUP = 4
SB_BLOCK = 256
SB_HEADS_PER_STEP = 2
SB_QBLOCK = 256
MEM_HEADS = 4
MEM_DH = 64
MEM_W = MEM_HEADS * MEM_DH
NORM_EPS = 1e-6
IN_WIDTH = 11792
N_SHARD = 4
SHARD_W = IN_WIDTH // N_SHARD
SHARD_PAD = 3072
N_DEV = 8

C_DNZ = 3072
C_SBQ = 4096
C_SBZ = 7168
C_MQ = 8192
C_MZ = 8448
C_GATES = 8704
C_BA = 11776
W_R = 11904

ADAM_LR = 0.001
ADAM_B1 = 0.9
ADAM_B2 = 0.999
ADAM_EPS = 1e-08
ADAM_WD = 0.01
ADAM_STEP = 10

VMEM_LIMIT = 56 * 1024 * 1024

B_ROWS = 992
B_MEMKV, B_BRDN, B_BRSB, B_BRMEM, B_OUT, B_CONV = 0, 128, 384, 640, 704, 960
S_NORM, S_MEMNORM, S_FINAL, S_DNNORM, S_ALOG, S_DTB, S_LOSS, S_CONV, S_ROWS = 0, 8, 16, 24, 25, 26, 27, 32, 128
CONV_BLOCKS = 3 * D_MODEL // 128


def _cp(**kw):
    return pltpu.CompilerParams(vmem_limit_bytes=VMEM_LIMIT, **kw)


def _dot(a, b, dims):
    lead = a.ndim - 2
    ca, cb = {"nn": (1, 0), "nt": (1, 1), "tn": (0, 0)}[dims]
    batch = tuple(range(lead))
    return lax.dot_general(a, b, (((ca + lead,), (cb + lead,)), (batch, batch)), preferred_element_type=F32)


def _chunks(x):
    return x.reshape(x.shape[0] // DN_CHUNK, DN_CHUNK, x.shape[1])


def _unchunk(x):
    return x.reshape(x.shape[0] * x.shape[1], x.shape[2])


def _bdot(a, b, dims):
    return _dot(a.astype(BF16), b.astype(BF16), dims)


def _split(a):
    hi = a.astype(BF16)
    return hi, (a - hi.astype(F32)).astype(BF16)


def _dot3(a, b, dims):
    a1, a2 = _split(a)
    b1, b2 = _split(b)
    return _dot(a1, b1, dims) + (_dot(a1, b2, dims) + _dot(a2, b1, dims))


def _ones_dot(a, ones_bf16):
    out = _dot(a.reshape(-1, a.shape[-1]).astype(BF16), ones_bf16, "nn")
    return out.reshape(a.shape[:-1] + (ones_bf16.shape[1],))


def _sigmoid(x):
    return 1.0 / (1.0 + jnp.exp(-x))


def _log1p_small(u):
    return jnp.where(u < 1e-2, u * (1.0 - u * (0.5 - u * (1.0 / 3.0))), jnp.log(1.0 + u))


def _log_sigmoid(z):
    return jnp.minimum(z, 0.0) - _log1p_small(jnp.exp(-jnp.abs(z)))


def _pick(dim, cands):
    for c in cands:
        if dim % c == 0:
            return c
    return dim


def _mm(a, b, dims, name, out_dtype=F32, out_shards=1, after=None):
    ta, tb = dims[0] == "t", dims[1] == "t"
    m, k = (a.shape[1], a.shape[0]) if ta else a.shape
    b_shards = b.shape[0] if b.ndim == 3 else 1
    n = b.shape[-2] if tb else b.shape[-1]
    tm = _pick(m, (1024, 512, 256))
    tn = _pick(n // out_shards, (512, 384, 256, 128))
    tk = _pick(k // b_shards, (1024, 512, 384, 256))
    nk = k // tk

    def body(a_ref, b_ref, *rest):
        o_ref, acc_ref = rest[-2:]
        kk = pl.program_id(2)

        @pl.when(kk == 0)
        def _():
            acc_ref[...] = jnp.zeros_like(acc_ref)

        acc_ref[...] += _bdot(a_ref[...], b_ref[...], dims)

        @pl.when(kk == nk - 1)
        def _():
            o_ref[...] = acc_ref[...].astype(out_dtype)

    a_spec = pl.BlockSpec((tk, tm), lambda i, j, q: (q, i)) if ta else pl.BlockSpec((tm, tk), lambda i, j, q: (i, q))
    if b_shards > 1:
        per_k = k // b_shards // tk
        b_spec = pl.BlockSpec((None, tn, tk), lambda i, j, q: (q // per_k, j, q % per_k))
    else:
        b_spec = pl.BlockSpec((tn, tk), lambda i, j, q: (j, q)) if tb else pl.BlockSpec((tk, tn), lambda i, j, q: (q, j))
    if out_shards > 1:
        per_n = n // out_shards // tn
        out_spec = pl.BlockSpec((None, tm, tn), lambda i, j, q: (j // per_n, i, j % per_n))
        out_shape = jax.ShapeDtypeStruct((out_shards, m, n // out_shards), out_dtype)
    else:
        out_spec = pl.BlockSpec((tm, tn), lambda i, j, q: (i, j))
        out_shape = jax.ShapeDtypeStruct((m, n), out_dtype)
    extra_specs, extra = [], []
    if after is not None:
        extra_specs, extra = [pl.BlockSpec(after.shape, lambda i, j, q: (0, 0))], [after]
    return pl.pallas_call(
        body, name=name, grid=(m // tm, n // tn, nk),
        in_specs=[a_spec, b_spec] + extra_specs, out_specs=out_spec, out_shape=out_shape,
        scratch_shapes=[pltpu.VMEM((tm, tn), F32)],
        compiler_params=_cp(dimension_semantics=("parallel", "parallel", "arbitrary")),
    )(a, b, *extra)


def _rmsnorm_fwd(x, g, name):
    t, d = x.shape
    tb = _pick(t, (512, 256))

    def body(x_ref, g_ref, h_ref):
        xv = x_ref[...]
        r = lax.rsqrt(jnp.mean(xv * xv, axis=-1, keepdims=True) + NORM_EPS)
        h_ref[...] = ((xv * r) * g_ref[...]).astype(BF16)

    return pl.pallas_call(
        body, name=name, grid=(t // tb,),
        in_specs=[pl.BlockSpec((tb, d), lambda i: (i, 0)), pl.BlockSpec((1, d), lambda i: (0, 0))],
        out_specs=pl.BlockSpec((tb, d), lambda i: (i, 0)),
        out_shape=jax.ShapeDtypeStruct((t, d), BF16), compiler_params=_cp(),
    )(x, g)


def _rmsnorm_bwd(x, g, dh, resid, name):
    t, d = x.shape
    tb = _pick(t, (256,))

    def body(x_ref, g_ref, dh_ref, r_ref, dx_ref, dg_ref):
        @pl.when(pl.program_id(0) == 0)
        def _():
            dg_ref[...] = jnp.zeros_like(dg_ref)

        xv = x_ref[...]
        r = lax.rsqrt(jnp.mean(xv * xv, axis=-1, keepdims=True) + NORM_EPS)
        xhat = xv * r
        dhv = dh_ref[...]
        dg_ref[...] += jnp.sum(dhv * xhat, axis=0, keepdims=True)
        dxh = dhv * g_ref[...]
        dx_ref[...] = r_ref[...] + r * (dxh - xhat * jnp.mean(dxh * xhat, axis=-1, keepdims=True))

    row = pl.BlockSpec((tb, d), lambda i: (i, 0))
    vec = pl.BlockSpec((1, d), lambda i: (0, 0))
    return pl.pallas_call(
        body, name=name, grid=(t // tb,), in_specs=[row, vec, row, row], out_specs=[row, vec],
        out_shape=[jax.ShapeDtypeStruct((t, d), F32), jax.ShapeDtypeStruct((1, d), F32)], compiler_params=_cp(),
    )(x, g, dh, resid)


def _conv_silu(xv, w, row):
    y = xv * w[3:4, :]
    for s in (1, 2, 3):
        xs = jnp.where(row >= s, pltpu.roll(xv, s, 0), 0.0)
        y = y + xs * w[3 - s:4 - s, :]
    return y, y * _sigmoid(y)


def _dn_prep_fwd(proj, conv_w):
    t = proj.shape[0]

    def body(p_ref, w_ref, o_ref):
        j = pl.program_id(0)
        xv = p_ref[...]
        row = lax.broadcasted_iota(jnp.int32, xv.shape, 0)
        _, a = _conv_silu(xv, w_ref[...], row)
        inv = lax.rsqrt(jnp.sum(a * a, axis=-1, keepdims=True) + NORM_EPS)
        scale = jnp.where(j < N_HEADS, D_HEAD ** -0.5, 1.0)
        normed = jnp.where(j < 2 * N_HEADS, 1.0, 0.0)
        o_ref[...] = a * (normed * (inv * scale) + (1.0 - normed))

    return pl.pallas_call(
        body, name="dn_prep_fwd", grid=(3 * N_HEADS,),
        in_specs=[pl.BlockSpec((t, D_HEAD), lambda j: (0, j)), pl.BlockSpec((4, D_HEAD), lambda j: (0, j))],
        out_specs=pl.BlockSpec((t, D_HEAD), lambda j: (0, j)),
        out_shape=jax.ShapeDtypeStruct((t, 3 * D_MODEL), F32), compiler_params=_cp(),
    )(proj, conv_w)


def _dn_prep_bwd(proj, conv_w, dq, dk, dv):
    t = proj.shape[0]

    def body(p_ref, w_ref, dq_ref, dk_ref, dv_ref, dp_ref, dw_ref):
        j = pl.program_id(0)
        xv = p_ref[...]
        w = w_ref[...]
        row = lax.broadcasted_iota(jnp.int32, xv.shape, 0)
        y, a = _conv_silu(xv, w, row)
        part = jnp.zeros(xv.shape, jnp.int32) + j // N_HEADS
        dn = jnp.where(part == 0, dq_ref[...], jnp.where(part == 1, dk_ref[...], dv_ref[...]))
        inv = lax.rsqrt(jnp.sum(a * a, axis=-1, keepdims=True) + NORM_EPS)
        scale = jnp.where(j < N_HEADS, D_HEAD ** -0.5, 1.0)
        ds = dn * scale
        da_norm = inv * ds - a * (inv * inv * inv) * jnp.sum(ds * a, axis=-1, keepdims=True)
        normed = jnp.where(j < 2 * N_HEADS, 1.0, 0.0)
        da = normed * da_norm + (1.0 - normed) * dn
        s = _sigmoid(y)
        dy = da * (s * (1.0 + y * (1.0 - s)))
        dx = dy * w[3:4, :]
        dw_ref[3:4, :] = jnp.sum(dy * xv, axis=0, keepdims=True)
        for sft in (1, 2, 3):
            xs = jnp.where(row >= sft, pltpu.roll(xv, sft, 0), 0.0)
            dw_ref[3 - sft:4 - sft, :] = jnp.sum(dy * xs, axis=0, keepdims=True)
            dys = jnp.where(row < t - sft, pltpu.roll(dy, t - sft, 0), 0.0)
            dx = dx + dys * w[3 - sft:4 - sft, :]
        dp_ref[...] = dx.astype(BF16)

    blk = pl.BlockSpec((t, D_HEAD), lambda j: (0, j))
    wblk = pl.BlockSpec((4, D_HEAD), lambda j: (0, j))

    def grad(part):
        return pl.BlockSpec((t, D_HEAD), lambda j: (0, jnp.clip(j - part * N_HEADS, 0, N_HEADS - 1)))

    return pl.pallas_call(
        body, name="dn_prep_bwd", grid=(3 * N_HEADS,), in_specs=[blk, wblk, grad(0), grad(1), grad(2)],
        out_specs=[blk, wblk],
        out_shape=[jax.ShapeDtypeStruct((t, 3 * D_MODEL), BF16), jax.ShapeDtypeStruct((4, 3 * D_MODEL), F32)],
        compiler_params=_cp(),
    )(proj, conv_w, dq, dk, dv)


def _softplus_parts(xv):
    e = jnp.exp(-jnp.abs(xv))
    return jnp.maximum(xv, 0.0) + _log1p_small(e)


def _chunk_scan(v, row, reverse):
    t = v.shape[0]
    pos = row & (DN_CHUNK - 1)
    s = 1
    while s < DN_CHUNK:
        if reverse:
            v = v + jnp.where(pos < DN_CHUNK - s, pltpu.roll(v, t - s, 0), 0.0)
        else:
            v = v + jnp.where(pos >= s, pltpu.roll(v, s, 0), 0.0)
        s *= 2
    return v


def _dn_gate_fwd(proj, alog_row, dtb_row):
    t = proj.shape[0]

    def body(p_ref, al_ref, dt_ref, b_ref, g_ref):
        p = p_ref[...]
        row = lax.broadcasted_iota(jnp.int32, p.shape, 0)
        b_ref[...] = _sigmoid(p)
        g = -jnp.exp(al_ref[...]) * _softplus_parts(p + dt_ref[...])
        g_ref[...] = _chunk_scan(g, row, reverse=False)

    blk = pl.BlockSpec((t, 128), lambda i: (0, C_BA // 128))
    vec = pl.BlockSpec((1, 128), lambda i: (0, 0))
    out = pl.BlockSpec((t, 128), lambda i: (0, 0))
    return pl.pallas_call(
        body, name="dn_gate_fwd", grid=(1,), in_specs=[blk, vec, vec], out_specs=[out, out],
        out_shape=[jax.ShapeDtypeStruct((t, 128), F32)] * 2, compiler_params=_cp(),
    )(proj, alog_row, dtb_row)


def _dn_gate_bwd(proj, alog_row, dtb_row, dbeta, dgc):
    t = proj.shape[0]

    def body(p_ref, al_ref, dt_ref, db_ref, dg_ref, dp_ref, dal_ref, ddt_ref):
        p = p_ref[...]
        row = lax.broadcasted_iota(jnp.int32, p.shape, 0)
        lane = lax.broadcasted_iota(jnp.int32, p.shape, 1)
        s = _sigmoid(p)
        d_b = db_ref[...] * s * (1.0 - s)
        dg = _chunk_scan(dg_ref[...], row, reverse=True)
        xa = p + dt_ref[...]
        ea = jnp.exp(al_ref[...])
        g = -ea * _softplus_parts(xa)
        d_a = dg * (-ea) * _sigmoid(xa)
        dp_ref[...] = jnp.where(lane < N_HEADS, d_b, jnp.where(lane < 2 * N_HEADS, d_a, 0.0)).astype(BF16)
        dal_ref[...] = jnp.sum(dg * g, axis=0, keepdims=True)
        ddt_ref[...] = jnp.sum(d_a, axis=0, keepdims=True)

    blk = pl.BlockSpec((t, 128), lambda i: (0, C_BA // 128))
    vec = pl.BlockSpec((1, 128), lambda i: (0, 0))
    full = pl.BlockSpec((t, 128), lambda i: (0, 0))
    return pl.pallas_call(
        body, name="dn_gate_bwd", grid=(1,), in_specs=[blk, vec, vec, full, full], out_specs=[full, vec, vec],
        out_shape=[jax.ShapeDtypeStruct((t, 128), BF16), jax.ShapeDtypeStruct((1, 128), F32),
                   jax.ShapeDtypeStruct((1, 128), F32)], compiler_params=_cp(),
    )(proj, alog_row, dtb_row, dbeta, dgc)


def _col_to_row(col, eye):
    return jnp.sum(jnp.where(eye, col, 0.0), axis=-2, keepdims=True)


def _row_to_col(rowv, eye):
    return jnp.sum(jnp.where(eye, rowv, 0.0), axis=-1, keepdims=True)


def _tri_inverse(m, ri, ci):
    eye = (ri == ci).astype(F32)
    b16 = (ri >> 4) == (ci >> 4)
    b32 = (ri >> 5) == (ci >> 5)
    m1 = jnp.where(b16, m, 0.0)
    x = eye - m1
    p = _dot3(m1, m1, "nn")
    x = x + _dot3(x, p, "nn")
    p = _dot3(p, p, "nn")
    x = x + _dot3(x, p, "nn")
    p = _dot3(p, p, "nn")
    x = x + _dot3(x, p, "nn")
    c1 = jnp.where(jnp.logical_and(b32, jnp.logical_not(b16)), m, 0.0)
    x = x - _dot3(_dot3(x, c1, "nn"), x, "nn")
    c2 = jnp.where(b32, 0.0, m)
    x = x - _dot3(_dot3(x, c2, "nn"), x, "nn")
    return x


def _dn_chunk_common(q, k, gc, ri, ci):
    eye = ri == ci
    g_row = _col_to_row(gc, eye)
    diff = jnp.minimum(gc - g_row, 0.0)
    gam = jnp.where(ri >= ci, jnp.exp(diff), 0.0)
    kk = _bdot(k, k, "nt")
    qk = _bdot(q, k, "nt")
    rcol = lax.broadcasted_iota(jnp.int32, gc.shape, gc.ndim - 2)
    last = jnp.sum(jnp.where(rcol == DN_CHUNK - 1, gc, 0.0), axis=-2, keepdims=True)
    e_g = jnp.exp(gc)
    dec = jnp.exp(last - gc)
    return eye, gam, kk, qk, last, e_g, dec, rcol


def _dn_specs(t, rows_blk):
    def head(off):
        return pl.BlockSpec((rows_blk, D_HEAD), lambda g, h: (g, off + h))

    lanes = pl.BlockSpec((rows_blk, 128), lambda g, h: (g, 0))
    hm = pl.BlockSpec((None, rows_blk, D_HEAD), lambda g, h: (h, g, 0))
    sq = pl.BlockSpec((1, rows_blk, DN_CHUNK), lambda g, h: (h, g, 0))
    tile = pl.BlockSpec((1, rows_blk // DN_CHUNK, 8, 128), lambda g, h: (h, g, 0, 0))
    return head, lanes, hm, sq, tile


def _head_column(slab, lane_idx):
    lane = lax.broadcasted_iota(jnp.int32, slab.shape, 1)
    return _chunks(jnp.sum(jnp.where(lane == lane_idx, slab, 0.0), axis=1, keepdims=True))


def _dn_intra_fwd(qkv, beta_t, g_t):
    t = qkv.shape[0]
    n_chunks = t // DN_CHUNK
    rows_blk = min(DN_GROUP * DN_CHUNK, t)

    def body(q_ref, k_ref, v_ref, b_ref, g_ref, u_ref, w_ref, qd_ref, kd_ref, a_ref, ti_ref, el_ref):
        ri = lax.broadcasted_iota(jnp.int32, (DN_CHUNK, DN_CHUNK), 0)
        ci = lax.broadcasted_iota(jnp.int32, (DN_CHUNK, DN_CHUNK), 1)
        h = pl.program_id(1)
        q, k, v = (_chunks(r[...]) for r in (q_ref, k_ref, v_ref))
        b, gc = _head_column(b_ref[...], h), _head_column(g_ref[...], h + N_HEADS)
        _, gam, kk, qk, last, e_g, dec, _ = _dn_chunk_common(q, k, gc, ri, ci)
        tinv = _tri_inverse(jnp.where(ri > ci, b * kk * gam, 0.0), ri, ci)
        u_ref[...] = _unchunk(_bdot(tinv, v * b, "nn"))
        w_ref[...] = _unchunk(_bdot(tinv, k * (b * e_g), "nn"))
        qd_ref[...] = _unchunk(q * e_g)
        kd_ref[...] = _unchunk(k * dec)
        a_ref[0] = _unchunk(qk * gam)
        ti_ref[0] = _unchunk(tinv)
        el_ref[0] = jnp.broadcast_to(jnp.exp(last), (rows_blk // DN_CHUNK, 8, 128))

    head, lanes, hm, sq, tile = _dn_specs(t, rows_blk)
    act = jax.ShapeDtypeStruct((N_HEADS, t, D_HEAD), F32)
    sqs = jax.ShapeDtypeStruct((N_HEADS, t, DN_CHUNK), F32)
    return pl.pallas_call(
        body, name="dn_intra_fwd", grid=(t // rows_blk, N_HEADS),
        in_specs=[head(0), head(N_HEADS), head(2 * N_HEADS), lanes, lanes],
        out_specs=[hm] * 4 + [sq, sq, tile],
        out_shape=[act] * 4 + [sqs, sqs, jax.ShapeDtypeStruct((N_HEADS, n_chunks, 8, 128), F32)],
        compiler_params=_cp(),
    )(qkv, qkv, qkv, beta_t, g_t)


def _dn_scan_specs(t, rows_blk, reverse):
    n_groups = t // rows_blk

    def at(g):
        return n_groups - 1 - g if reverse else g

    per = rows_blk // DN_CHUNK
    act = pl.BlockSpec((N_HEADS, rows_blk, D_HEAD), lambda g: (0, at(g), 0))
    sq = pl.BlockSpec((N_HEADS, rows_blk, DN_CHUNK), lambda g: (0, at(g), 0))
    state = pl.BlockSpec((N_HEADS, per, D_HEAD, D_HEAD), lambda g: (0, at(g), 0, 0))
    tile = pl.BlockSpec((N_HEADS, per, 8, 128), lambda g: (0, at(g), 0, 0))
    return act, sq, state, tile


def _dn_scan_fwd(u, w, qd, kd, a, el):
    t = u.shape[1]
    n_chunks = t // DN_CHUNK
    rows_blk = DN_SCAN_GROUP * DN_CHUNK

    def body(u_ref, w_ref, qd_ref, kd_ref, a_ref, el_ref, o_ref, vn_ref, s_ref, s_scr):
        @pl.when(pl.program_id(0) == 0)
        def _():
            s_scr[...] = jnp.zeros_like(s_scr)

        for cc in range(DN_SCAN_GROUP):
            rows = slice(cc * DN_CHUNK, (cc + 1) * DN_CHUNK)
            s = s_scr[...]
            s_ref[:, cc] = s
            v_new = u_ref[:, rows, :] - _bdot(w_ref[:, rows, :], s, "nn")
            vn_ref[:, rows, :] = v_new
            o_ref[:, rows, :] = _bdot(qd_ref[:, rows, :], s, "nn") + _bdot(a_ref[:, rows, :], v_new, "nn")
            s_scr[...] = s * el_ref[:, cc][:, 0:1, :] + _bdot(kd_ref[:, rows, :], v_new, "tn")

    act, sq, state, tile = _dn_scan_specs(t, rows_blk, reverse=False)
    shp = jax.ShapeDtypeStruct((N_HEADS, t, D_HEAD), F32)
    return pl.pallas_call(
        body, name="dn_scan_fwd", grid=(t // rows_blk,),
        in_specs=[act, act, act, act, sq, tile], out_specs=[act, act, state],
        out_shape=[shp, shp, jax.ShapeDtypeStruct((N_HEADS, n_chunks, D_HEAD, D_HEAD), F32)],
        scratch_shapes=[pltpu.VMEM((N_HEADS, D_HEAD, D_HEAD), F32)],
        compiler_params=_cp(dimension_semantics=("arbitrary",)),
    )(u, w, qd, kd, a, el)


def _dn_scan_bwd(w, qd, kd, a, el, vn, s_all, do):
    t = w.shape[1]
    n_chunks = t // DN_CHUNK
    rows_blk = DN_SCAN_GROUP * DN_CHUNK

    def body(w_ref, qd_ref, kd_ref, a_ref, el_ref, vn_ref, s_ref, do_ref, dvn_ref, dkd_ref, dqd_ref, dw_ref, dl_ref, ds_scr):
        @pl.when(pl.program_id(0) == 0)
        def _():
            ds_scr[...] = jnp.zeros_like(ds_scr)

        for cc in reversed(range(DN_SCAN_GROUP)):
            rows = slice(cc * DN_CHUNK, (cc + 1) * DN_CHUNK)
            s = s_ref[:, cc]
            d_s = ds_scr[...]
            e_last = el_ref[:, cc][:, 0:1, :]
            d_o = do_ref[:, rows, :]
            dv_new = _bdot(a_ref[:, rows, :], d_o, "tn") + _bdot(kd_ref[:, rows, :], d_s, "nn")
            ds_scr[...] = d_s * e_last + _bdot(qd_ref[:, rows, :], d_o, "tn") - _bdot(w_ref[:, rows, :], dv_new, "tn")
            dvn_ref[:, rows, :] = dv_new
            dkd_ref[:, rows, :] = _bdot(vn_ref[:, rows, :], d_s, "nt")
            dqd_ref[:, rows, :] = _bdot(d_o, s, "nt")
            dw_ref[:, rows, :] = -_bdot(dv_new, s, "nt")
            dlast = jnp.sum(jnp.sum(d_s * s, axis=2, keepdims=True), axis=1, keepdims=True)
            dl_ref[:, cc] = jnp.broadcast_to(dlast * e_last, (N_HEADS, 8, 128))

    act, sq, state, tile = _dn_scan_specs(t, rows_blk, reverse=True)
    shp = jax.ShapeDtypeStruct((N_HEADS, t, D_HEAD), F32)
    return pl.pallas_call(
        body, name="dn_scan_bwd", grid=(t // rows_blk,),
        in_specs=[act, act, act, sq, tile, act, state, act], out_specs=[act] * 4 + [tile],
        out_shape=[shp] * 4 + [jax.ShapeDtypeStruct((N_HEADS, n_chunks, 8, 128), F32)],
        scratch_shapes=[pltpu.VMEM((N_HEADS, D_HEAD, D_HEAD), F32)],
        compiler_params=_cp(dimension_semantics=("arbitrary",)),
    )(w, qd, kd, a, el, vn, s_all, do)


def _dn_intra_bwd(qkv, beta_t, g_t, tinv_all, vn, do, dvn, dkd, dqd, dw, dl):
    t = qkv.shape[0]
    rows_blk = min(DN_GROUP * DN_CHUNK, t)

    def body(q_ref, k_ref, v_ref, b_ref, g_ref, ti_ref, vn_ref, do_ref, dvn_ref, dkd_ref, dqd_ref, dw_ref, dl_ref,
             dq_ref, dk_ref, dv_ref, db_ref, dg_ref):
        ri = lax.broadcasted_iota(jnp.int32, (DN_CHUNK, DN_CHUNK), 0)
        ci = lax.broadcasted_iota(jnp.int32, (DN_CHUNK, DN_CHUNK), 1)
        h = pl.program_id(1)
        q, k, v = (_chunks(r[...]) for r in (q_ref, k_ref, v_ref))
        b, gc = _head_column(b_ref[...], h), _head_column(g_ref[...], h + N_HEADS)
        tinv = _chunks(ti_ref[0])
        dv_new, dk_dec, dq_dec, d_w = (_chunks(r[...]) for r in (dvn_ref, dkd_ref, dqd_ref, dw_ref))
        eye, gam, kk, qk, _, e_g, dec, rcol = _dn_chunk_common(q, k, gc, ri, ci)
        bv = v * b
        bk = k * (b * e_g)

        d_a = jnp.where(ri >= ci, _bdot(_chunks(do_ref[...]), _chunks(vn_ref[...]), "nt"), 0.0)
        dbv = _bdot(tinv, dv_new, "tn")
        dbk = _bdot(tinv, d_w, "tn")
        d_tinv = _bdot(dv_new, bv, "nt") + _bdot(d_w, bk, "nt")
        d_m = -jnp.where(ri > ci, _dot3(_dot3(tinv, d_tinv, "tn"), tinv, "nt"), 0.0)

        d_kk = d_m * b * gam
        d_gam = d_m * b * kk + d_a * qk
        d_qk = d_a * gam
        dq_ref[...] = _unchunk(_bdot(d_qk, k, "nn") + dq_dec * e_g)
        dk_ref[...] = _unchunk(_bdot(d_qk, q, "tn") + _bdot(d_kk, k, "nn") + _bdot(d_kk, k, "tn")
                               + dk_dec * dec + dbk * (b * e_g))
        dv_ref[...] = _unchunk(dbv * b)
        d_b = _unchunk(jnp.sum(d_m * kk * gam, axis=-1, keepdims=True) + jnp.sum(dbv * v, axis=-1, keepdims=True)
                       + jnp.sum(dbk * k, axis=-1, keepdims=True) * e_g)

        xg = d_gam * gam
        kdk = jnp.sum(dk_dec * (k * dec), axis=-1, keepdims=True)
        d_gc = (jnp.sum(xg, axis=-1, keepdims=True) - _row_to_col(jnp.sum(xg, axis=-2, keepdims=True), eye)
                + jnp.sum(dq_dec * (q * e_g), axis=-1, keepdims=True) - kdk
                + jnp.sum(dbk * bk, axis=-1, keepdims=True))
        d_last_total = dl_ref[0][:, 0:1, 0:1] + jnp.sum(kdk, axis=-2, keepdims=True)
        d_g = _unchunk(d_gc + jnp.where(rcol == DN_CHUNK - 1, d_last_total, 0.0))

        @pl.when(h == 0)
        def _():
            db_ref[...] = jnp.zeros_like(db_ref)
            dg_ref[...] = jnp.zeros_like(dg_ref)

        lane = lax.broadcasted_iota(jnp.int32, db_ref.shape, 1)
        db_ref[...] += jnp.where(lane == h, d_b, 0.0)
        dg_ref[...] += jnp.where(lane == h + N_HEADS, d_g, 0.0)

    head, lanes, hm, sq, tile = _dn_specs(t, rows_blk)
    return pl.pallas_call(
        body, name="dn_intra_bwd", grid=(t // rows_blk, N_HEADS),
        in_specs=[head(0), head(N_HEADS), head(2 * N_HEADS), lanes, lanes, sq] + [hm] * 6 + [tile],
        out_specs=[head(0), head(0), head(0), lanes, lanes],
        out_shape=[jax.ShapeDtypeStruct((t, D_MODEL), F32)] * 3 + [jax.ShapeDtypeStruct((t, 128), F32)] * 2,
        compiler_params=_cp(),
    )(qkv, qkv, qkv, beta_t, g_t, tinv_all, vn, do, dvn, dkd, dqd, dw, dl)


def _dn_post_fwd(o, proj, gn):
    t = o.shape[1]

    def body(o_ref, z_ref, g_ref, out_ref):
        ov, z = o_ref[...], z_ref[...]
        r = lax.rsqrt(jnp.mean(ov * ov, axis=-1, keepdims=True) + NORM_EPS)
        out_ref[...] = (((ov * r) * g_ref[...]) * (z * _sigmoid(z))).astype(BF16)

    blk = pl.BlockSpec((t, D_HEAD), lambda h: (0, h))
    return pl.pallas_call(
        body, name="dn_post_fwd", grid=(N_HEADS,),
        in_specs=[pl.BlockSpec((None, t, D_HEAD), lambda h: (h, 0, 0)),
                  pl.BlockSpec((t, D_HEAD), lambda h: (0, C_DNZ // D_HEAD + h)),
                  pl.BlockSpec((1, D_HEAD), lambda h: (0, 0))],
        out_specs=blk, out_shape=jax.ShapeDtypeStruct((t, D_MODEL), BF16), compiler_params=_cp(),
    )(o, proj, gn)


def _dn_post_bwd(o, proj, gn, dout):
    t = o.shape[1]

    def body(o_ref, z_ref, g_ref, d_ref, do_ref, dz_ref, dg_ref):
        @pl.when(pl.program_id(0) == 0)
        def _():
            dg_ref[...] = jnp.zeros_like(dg_ref)

        ov, z, d = o_ref[...], z_ref[...], d_ref[...]
        r = lax.rsqrt(jnp.mean(ov * ov, axis=-1, keepdims=True) + NORM_EPS)
        ohat = ov * r
        s = _sigmoid(z)
        d_on = d * (z * s)
        dz_ref[...] = (d * (ohat * g_ref[...]) * (s * (1.0 + z * (1.0 - s)))).astype(BF16)
        dg_ref[...] += jnp.sum(d_on * ohat, axis=0, keepdims=True)
        dxh = d_on * g_ref[...]
        do_ref[...] = r * (dxh - ohat * jnp.mean(dxh * ohat, axis=-1, keepdims=True))

    blk = pl.BlockSpec((t, D_HEAD), lambda h: (0, h))
    hm = pl.BlockSpec((None, t, D_HEAD), lambda h: (h, 0, 0))
    vec = pl.BlockSpec((1, D_HEAD), lambda h: (0, 0))
    return pl.pallas_call(
        body, name="dn_post_bwd", grid=(N_HEADS,),
        in_specs=[hm, pl.BlockSpec((t, D_HEAD), lambda h: (0, C_DNZ // D_HEAD + h)), vec, blk],
        out_specs=[hm, blk, vec],
        out_shape=[jax.ShapeDtypeStruct((N_HEADS, t, D_HEAD), F32), jax.ShapeDtypeStruct((t, D_MODEL), BF16),
                   jax.ShapeDtypeStruct((1, D_HEAD), F32)], compiler_params=_cp(),
    )(o, proj, gn, dout)


def _sb_fwd(proj):
    t = proj.shape[0]
    qblk = min(SB_QBLOCK, t)
    scale = 1.0 / math.sqrt(D_HEAD)

    hp = SB_HEADS_PER_STEP
    wid = hp * D_HEAD

    def body(q_ref, k_ref, v_ref, z_ref, o_ref, og_ref, l_ref, qb, kb, vb):
        for hh in range(hp):
            hs = slice(hh * D_HEAD, (hh + 1) * D_HEAD)
            qb[hh] = q_ref[:, hs].astype(BF16)
            kb[hh] = k_ref[:, hs].astype(BF16)
            vb[hh] = v_ref[:, hs].astype(BF16)
        ri = lax.broadcasted_iota(jnp.int32, (qblk, SB_BLOCK), 0)
        ci = lax.broadcasted_iota(jnp.int32, (qblk, SB_BLOCK), 1)
        r2 = lax.broadcasted_iota(jnp.int32, (SB_BLOCK, SB_BLOCK), 0)
        c2 = lax.broadcasted_iota(jnp.int32, (SB_BLOCK, SB_BLOCK), 1)
        upper = (r2 > c2).astype(BF16)
        nkb = qblk // SB_BLOCK

        def qblock(i, carry):
            rows = pl.ds(pl.multiple_of(i * qblk, qblk), qblk)
            qi = qb[:, rows, :]

            def tile(j, st, on_diagonal):
                acc, c = st
                cols = pl.ds(pl.multiple_of(j * SB_BLOCK, SB_BLOCK), SB_BLOCK)
                z = _dot(qi, kb[:, cols, :], "nt") * scale
                lb = jnp.minimum(z, 0.0) - jnp.log(1.0 + jnp.exp(-jnp.abs(z)))
                lf = lb - z
                if on_diagonal:
                    mask = (j * SB_BLOCK + ci) < (i * qblk + ri)
                    lf = jnp.where(mask, lf, 0.0)
                att = jnp.exp(lb + (_ones_dot(lf, upper) + c))
                if on_diagonal:
                    att = jnp.where(mask, att, 0.0)
                acc = acc + _dot(att.astype(BF16), vb[:, cols, :], "nn")
                return acc, c + jnp.sum(lf, axis=-1, keepdims=True)

            st = (jnp.zeros((hp, qblk, D_HEAD), F32), jnp.zeros((hp, qblk, 1), F32))
            for d in range(nkb):
                st = tile((i + 1) * nkb - 1 - d, st, True)
            acc, c = lax.fori_loop(0, i * nkb, lambda jj, s: tile(i * nkb - 1 - jj, s, False), st)
            l_ref[:, rows, :] = c
            for hh in range(hp):
                hs = slice(hh * D_HEAD, (hh + 1) * D_HEAD)
                zg = z_ref[rows, hs]
                o_ref[rows, hs] = acc[hh]
                og_ref[rows, hs] = (acc[hh] * (zg * _sigmoid(zg))).astype(BF16)
            return carry

        lax.fori_loop(0, t // qblk, qblock, 0)

    def head(off):
        return pl.BlockSpec((t, wid), lambda h: (0, off // wid + h))

    out = pl.BlockSpec((t, wid), lambda h: (0, h))
    return pl.pallas_call(
        body, name="sb_fwd", grid=(N_HEADS // hp,),
        in_specs=[head(C_SBQ), head(C_SBQ + D_MODEL), head(C_SBQ + 2 * D_MODEL), head(C_SBZ)],
        out_specs=[out, out, pl.BlockSpec((hp, t, 1), lambda h: (h, 0, 0))],
        out_shape=[jax.ShapeDtypeStruct((t, D_MODEL), F32), jax.ShapeDtypeStruct((t, D_MODEL), BF16),
                   jax.ShapeDtypeStruct((N_HEADS, t, 1), F32)],
        scratch_shapes=[pltpu.VMEM((hp, t, D_HEAD), BF16)] * 3, compiler_params=_cp(),
    )(proj, proj, proj, proj)


def _sb_bwd(proj, o, ltot, dog, after=None):
    t = proj.shape[0]
    qblk = min(SB_QBLOCK, t)
    scale = 1.0 / math.sqrt(D_HEAD)

    hp = SB_HEADS_PER_STEP
    wid = hp * D_HEAD

    def body(q_ref, k_ref, v_ref, z_ref, o_ref, l_ref, d_ref, *rest):
        dq_ref, dk_ref, dv_ref, dz_ref, qb, kb, vb, dob, dk_scr, dv_scr = rest[-10:]
        for hh in range(hp):
            hs = slice(hh * D_HEAD, (hh + 1) * D_HEAD)
            qb[hh] = q_ref[:, hs].astype(BF16)
            kb[hh] = k_ref[:, hs].astype(BF16)
            vb[hh] = v_ref[:, hs].astype(BF16)
            zg = z_ref[:, hs]
            sg = _sigmoid(zg)
            dgo = d_ref[:, hs]
            dob[hh] = (dgo * (zg * sg)).astype(BF16)
            dz_ref[:, hs] = (dgo * o_ref[:, hs] * (sg * (1.0 + zg * (1.0 - sg)))).astype(BF16)
        dk_scr[...] = jnp.zeros_like(dk_scr)
        dv_scr[...] = jnp.zeros_like(dv_scr)
        ri = lax.broadcasted_iota(jnp.int32, (qblk, SB_BLOCK), 0)
        ci = lax.broadcasted_iota(jnp.int32, (qblk, SB_BLOCK), 1)
        r2 = lax.broadcasted_iota(jnp.int32, (SB_BLOCK, SB_BLOCK), 0)
        c2 = lax.broadcasted_iota(jnp.int32, (SB_BLOCK, SB_BLOCK), 1)
        upper = (r2 > c2).astype(BF16)
        below = (r2 < c2).astype(BF16)

        def qblock(i, carry):
            rows = pl.ds(pl.multiple_of(i * qblk, qblk), qblk)
            qi = qb[:, rows, :]
            d_o = dob[:, rows, :]
            ltot = l_ref[:, rows, :]

            def tile(j, st, on_diagonal):
                dq, cpre, ce = st
                cols = pl.ds(pl.multiple_of(j * SB_BLOCK, SB_BLOCK), SB_BLOCK)
                kj, vj = kb[:, cols, :], vb[:, cols, :]
                z = _dot(qi, kj, "nt") * scale
                lb = jnp.minimum(z, 0.0) - jnp.log(1.0 + jnp.exp(-jnp.abs(z)))
                lf = lb - z
                if on_diagonal:
                    mask = (j * SB_BLOCK + ci) < (i * qblk + ri)
                    lf = jnp.where(mask, lf, 0.0)
                tile_sum = jnp.sum(lf, axis=-1, keepdims=True)
                att = jnp.exp(lb + ((ltot - cpre - tile_sum) + _ones_dot(lf, upper)))
                if on_diagonal:
                    att = jnp.where(mask, att, 0.0)
                e = _dot(d_o, vj, "nt") * att
                dlf = ce + _ones_dot(e, below)
                dzz = e - (e + dlf) * jnp.exp(lb)
                if on_diagonal:
                    dzz = jnp.where(mask, dzz, 0.0)
                dzz = dzz.astype(BF16)
                dq = dq + _dot(dzz, kj, "nn")
                dk_scr[:, cols, :] += _dot(dzz, qi, "tn")
                dv_scr[:, cols, :] += _dot(att.astype(BF16), d_o, "tn")
                return dq, cpre + tile_sum, ce + jnp.sum(e, axis=-1, keepdims=True)

            nkb = qblk // SB_BLOCK
            zero_col = jnp.zeros((hp, qblk, 1), F32)
            st = lax.fori_loop(0, i * nkb, lambda j, s: tile(j, s, False),
                               (jnp.zeros((hp, qblk, D_HEAD), F32), zero_col, zero_col))
            for d in range(nkb):
                st = tile(i * nkb + d, st, True)
            dq = st[0]
            for hh in range(hp):
                dq_ref[rows, hh * D_HEAD:(hh + 1) * D_HEAD] = (dq[hh] * scale).astype(BF16)
            return carry

        lax.fori_loop(0, t // qblk, qblock, 0)
        for hh in range(hp):
            hs = slice(hh * D_HEAD, (hh + 1) * D_HEAD)
            dk_ref[:, hs] = (dk_scr[hh] * scale).astype(BF16)
            dv_ref[:, hs] = dv_scr[hh].astype(BF16)

    def head(off):
        return pl.BlockSpec((t, wid), lambda h: (0, off // wid + h))

    extra_specs, extra = [], []
    if after is not None:
        extra_specs, extra = [pl.BlockSpec(after.shape, lambda h: (0, 0))], [after]
    return pl.pallas_call(
        body, name="sb_bwd", grid=(N_HEADS // hp,),
        in_specs=[head(C_SBQ), head(C_SBQ + D_MODEL), head(C_SBQ + 2 * D_MODEL), head(C_SBZ), head(0),
                  pl.BlockSpec((hp, t, 1), lambda h: (h, 0, 0)), head(0)] + extra_specs,
        out_specs=[head(0)] * 4, out_shape=[jax.ShapeDtypeStruct((t, D_MODEL), BF16)] * 4,
        scratch_shapes=[pltpu.VMEM((hp, t, D_HEAD), BF16)] * 4 + [pltpu.VMEM((hp, t, D_HEAD), F32)] * 2,
        compiler_params=_cp(),
    )(proj, proj, proj, proj, o, ltot, dog, *extra)


def _mem_fwd(proj, mkv):
    t = proj.shape[0]
    tq = _pick(t, (512, 256))
    m_len = mkv.shape[0]
    scale = 1.0 / math.sqrt(MEM_DH)

    def body(q_ref, z_ref, kv_ref, o_ref, og_ref):
        q = q_ref[...]
        mk = kv_ref[:, :MEM_W].astype(BF16)
        mv = kv_ref[:, MEM_W:].astype(BF16)
        lane = lax.broadcasted_iota(jnp.int32, q.shape, 1) >> 6
        o = jnp.zeros(q.shape, F32)
        for h in range(MEM_HEADS):
            s = _bdot(jnp.where(lane == h, q, 0.0), mk, "nt") * scale
            p = jnp.exp(s - jnp.max(s, axis=-1, keepdims=True))
            p = p / jnp.sum(p, axis=-1, keepdims=True)
            o = o + jnp.where(lane == h, _bdot(p, mv, "nn"), 0.0)
        z = z_ref[...]
        o_ref[...] = o
        og_ref[...] = (o * (z * _sigmoid(z))).astype(BF16)

    out = pl.BlockSpec((tq, MEM_W), lambda i: (i, 0))
    return pl.pallas_call(
        body, name="mem_fwd", grid=(t // tq,),
        in_specs=[pl.BlockSpec((tq, MEM_W), lambda i: (i, C_MQ // MEM_W)),
                  pl.BlockSpec((tq, MEM_W), lambda i: (i, C_MZ // MEM_W)),
                  pl.BlockSpec((m_len, 2 * MEM_W), lambda i: (0, 0))],
        out_specs=[out, out],
        out_shape=[jax.ShapeDtypeStruct((t, MEM_W), F32), jax.ShapeDtypeStruct((t, MEM_W), BF16)],
        compiler_params=_cp(),
    )(proj, proj, mkv)


def _mem_bwd(proj, mkv, o, dog):
    t = proj.shape[0]
    tq = _pick(t, (512, 256))
    m_len = mkv.shape[0]
    scale = 1.0 / math.sqrt(MEM_DH)

    def body(q_ref, z_ref, kv_ref, o_ref, d_ref, dq_ref, dz_ref, dkv_ref):
        @pl.when(pl.program_id(0) == 0)
        def _():
            dkv_ref[...] = jnp.zeros_like(dkv_ref)

        q = q_ref[...]
        z = z_ref[...]
        sg = _sigmoid(z)
        dgo = d_ref[...]
        d_o = dgo * (z * sg)
        dz_ref[...] = (dgo * o_ref[...] * (sg * (1.0 + z * (1.0 - sg)))).astype(BF16)
        mk = kv_ref[:, :MEM_W].astype(BF16)
        mv = kv_ref[:, MEM_W:].astype(BF16)
        lane = lax.broadcasted_iota(jnp.int32, q.shape, 1) >> 6
        klane = lax.broadcasted_iota(jnp.int32, (m_len, MEM_W), 1) >> 6
        dq = jnp.zeros(q.shape, F32)
        dmk = jnp.zeros((m_len, MEM_W), F32)
        dmv = jnp.zeros((m_len, MEM_W), F32)
        for h in range(MEM_HEADS):
            qh = jnp.where(lane == h, q, 0.0)
            doh = jnp.where(lane == h, d_o, 0.0)
            s = _bdot(qh, mk, "nt") * scale
            p = jnp.exp(s - jnp.max(s, axis=-1, keepdims=True))
            p = p / jnp.sum(p, axis=-1, keepdims=True)
            dp = _bdot(doh, mv, "nt")
            ds = p * (dp - jnp.sum(dp * p, axis=-1, keepdims=True)) * scale
            dq = dq + jnp.where(lane == h, _bdot(ds, mk, "nn"), 0.0)
            dmk = dmk + jnp.where(klane == h, _bdot(ds, qh, "tn"), 0.0)
            dmv = dmv + jnp.where(klane == h, _bdot(p, doh, "tn"), 0.0)
        dq_ref[...] = dq.astype(BF16)
        dkv_ref[:, :MEM_W] += dmk
        dkv_ref[:, MEM_W:] += dmv

    blk = pl.BlockSpec((tq, MEM_W), lambda i: (i, 0))
    kv = pl.BlockSpec((m_len, 2 * MEM_W), lambda i: (0, 0))
    return pl.pallas_call(
        body, name="mem_bwd", grid=(t // tq,),
        in_specs=[pl.BlockSpec((tq, MEM_W), lambda i: (i, C_MQ // MEM_W)),
                  pl.BlockSpec((tq, MEM_W), lambda i: (i, C_MZ // MEM_W)), kv, blk, blk],
        out_specs=[blk, blk, kv],
        out_shape=[jax.ShapeDtypeStruct((t, MEM_W), BF16), jax.ShapeDtypeStruct((t, MEM_W), BF16),
                   jax.ShapeDtypeStruct((m_len, 2 * MEM_W), F32)], compiler_params=_cp(),
    )(proj, proj, mkv, o, dog)


_GW = 512


def _merge_fwd(proj, y_dn, y_sb, y_m):
    t = proj.shape[0]
    tb = _pick(t, (256,))
    nc = D_MODEL // _GW

    def body(g1, g2, g3, y1, y2, y3, out_ref):
        out_ref[...] = (_sigmoid(g1[...]) * y1[...] + _sigmoid(g2[...]) * y2[...] + _sigmoid(g3[...]) * y3[...]).astype(BF16)

    def gate(kb):
        return pl.BlockSpec((tb, _GW), lambda i, c: (i, C_GATES // _GW + kb * nc + c))

    blk = pl.BlockSpec((tb, _GW), lambda i, c: (i, c))
    return pl.pallas_call(
        body, name="merge_fwd", grid=(t // tb, nc), in_specs=[gate(0), gate(1), gate(2), blk, blk, blk],
        out_specs=blk, out_shape=jax.ShapeDtypeStruct((t, D_MODEL), BF16), compiler_params=_cp(),
    )(proj, proj, proj, y_dn, y_sb, y_m)


def _merge_bwd(proj, y_dn, y_sb, y_m, dm):
    t = proj.shape[0]
    tb = _pick(t, (256,))
    nc = D_MODEL // _GW

    def body(g1, g2, g3, y1, y2, y3, dm_ref, d1, d2, d3, dg1, dg2, dg3):
        d = dm_ref[...]
        for g, y, dy, dg in ((g1, y1, d1, dg1), (g2, y2, d2, dg2), (g3, y3, d3, dg3)):
            s = _sigmoid(g[...])
            dy[...] = (d * s).astype(BF16)
            dg[...] = (d * y[...] * (s * (1.0 - s))).astype(BF16)

    def gate(kb):
        return pl.BlockSpec((tb, _GW), lambda i, c: (i, C_GATES // _GW + kb * nc + c))

    blk = pl.BlockSpec((tb, _GW), lambda i, c: (i, c))
    act = jax.ShapeDtypeStruct((t, D_MODEL), BF16)
    return pl.pallas_call(
        body, name="merge_bwd", grid=(t // tb, nc), in_specs=[gate(0), gate(1), gate(2), blk, blk, blk, blk],
        out_specs=[blk] * 6, out_shape=[act] * 6, compiler_params=_cp(),
    )(proj, proj, proj, y_dn, y_sb, y_m, dm)


def _final_loss(x, mo, g, tgt):
    t, d = x.shape
    tb = _pick(t, (256,))

    def body(x_ref, mo_ref, g_ref, t_ref, do_ref, dob_ref, loss_ref, dg_ref):
        @pl.when(pl.program_id(0) == 0)
        def _():
            loss_ref[...] = jnp.zeros_like(loss_ref)
            dg_ref[...] = jnp.zeros_like(dg_ref)

        out = x_ref[...] + mo_ref[...]
        r = lax.rsqrt(jnp.mean(out * out, axis=-1, keepdims=True) + NORM_EPS)
        xhat = out * r
        gv = g_ref[...]
        err = xhat * gv - t_ref[...]
        per_tok = jnp.mean(err * err, axis=-1, keepdims=True)
        loss_ref[...] += 0.5 * jnp.sum(per_tok, axis=0, keepdims=True)
        dy = err * (1.0 / d)
        dg_ref[...] += jnp.sum(dy * xhat, axis=0, keepdims=True)
        dxh = dy * gv
        dout = r * (dxh - xhat * jnp.mean(dxh * xhat, axis=-1, keepdims=True))
        do_ref[...] = dout
        dob_ref[...] = dout.astype(BF16)

    row = pl.BlockSpec((tb, d), lambda i: (i, 0))
    vec = pl.BlockSpec((1, d), lambda i: (0, 0))
    return pl.pallas_call(
        body, name="final_loss", grid=(t // tb,), in_specs=[row, row, vec, row],
        out_specs=[row, row, pl.BlockSpec((1, 128), lambda i: (0, 0)), vec],
        out_shape=[jax.ShapeDtypeStruct((t, d), F32), jax.ShapeDtypeStruct((t, d), BF16),
                   jax.ShapeDtypeStruct((1, 128), F32), jax.ShapeDtypeStruct((1, d), F32)],
        compiler_params=_cp(),
    )(x, mo, g, tgt)


def _cast_bf16(a, name):
    r, c = a.shape
    tb = _pick(r, (128, 496, 240))

    def body(a_ref, o_ref):
        o_ref[...] = a_ref[...].astype(BF16)

    blk = pl.BlockSpec((tb, c), lambda i: (i, 0))
    return pl.pallas_call(body, name=name, grid=(r // tb,), in_specs=[blk], out_specs=blk,
                          out_shape=jax.ShapeDtypeStruct((r, c), BF16), compiler_params=_cp())(a)


WIN_START = (0, 23, 45, 68)
_S1_LO, _S1_HI = 1148, 1164
_S1_BA_POS = SHARD_PAD - 128


def _to_window(x, s):
    if s == 0:
        return x
    if s in (2, 3):
        return pltpu.roll(x, 120 if s == 2 else 124, 1)
    pos = lax.broadcasted_iota(jnp.int32, x.shape, 1)
    head = pltpu.roll(x, 4, 1)
    tail = pltpu.roll(x, SHARD_PAD - 12, 1)
    ba = jnp.where(pos < _S1_BA_POS + (_S1_HI - _S1_LO), pltpu.roll(x, _S1_BA_POS - _S1_LO, 1), 0.0)
    return jnp.where(pos < _S1_LO + 4, head, jnp.where(pos < _S1_BA_POS, tail, ba))


def _from_window(g, s):
    if s == 0:
        return g
    if s in (2, 3):
        return pltpu.roll(g, SHARD_PAD - (120 if s == 2 else 124), 1)
    col = lax.broadcasted_iota(jnp.int32, g.shape, 1)
    head = pltpu.roll(g, SHARD_PAD - 4, 1)
    tail = pltpu.roll(g, 12, 1)
    ba = pltpu.roll(g, SHARD_PAD - (_S1_BA_POS - _S1_LO), 1)
    return jnp.where(col < _S1_LO, head, jnp.where(col < _S1_HI, ba, tail))


def _cast_to_window(w, shard, name):
    r, c = w.shape
    tb = _pick(r, (128,))

    def body(s_ref, w_ref, o_ref, pad_scr):
        pad_scr[...] = jnp.zeros_like(pad_scr)
        pad_scr[:, :c] = w_ref[...]
        x = pad_scr[...]
        for s in range(N_SHARD):
            @pl.when(s_ref[0] == s)
            def _():
                o_ref[...] = _to_window(x, s).astype(BF16)

    return pl.pallas_call(
        body, name=name,
        grid_spec=pltpu.PrefetchScalarGridSpec(
            num_scalar_prefetch=1, grid=(r // tb,),
            in_specs=[pl.BlockSpec((tb, c), lambda i, s: (i, 0))],
            out_specs=pl.BlockSpec((tb, SHARD_PAD), lambda i, s: (i, 0)),
            scratch_shapes=[pltpu.VMEM((tb, SHARD_PAD), F32)]),
        out_shape=jax.ShapeDtypeStruct((r, SHARD_PAD), BF16), compiler_params=_cp(),
    )(shard, w)


def _pair_add(g, recv, c_idx, name):
    n, r, c = g.shape
    half = r // 2
    tb = _pick(half, (128, 240))
    nb = half // tb

    def body(c_ref, g_ref, r_ref, o_ref):
        o_ref[...] = (g_ref[...].astype(F32) + r_ref[...].astype(F32)).astype(BF16)

    blk = pl.BlockSpec((n, tb, c), lambda i, c_ref: (0, i, 0))
    return pl.pallas_call(
        body, name=name,
        grid_spec=pltpu.PrefetchScalarGridSpec(
            num_scalar_prefetch=1, grid=(nb,),
            in_specs=[pl.BlockSpec((n, tb, c), lambda i, c_ref: (0, c_ref[0] * nb + i, 0)), blk], out_specs=blk),
        out_shape=jax.ShapeDtypeStruct((n, half, c), BF16), compiler_params=_cp(),
    )(c_idx, g, recv)


def _chip_sum(parts, by_chip, place, name):
    n, h, c = parts.shape
    tb = _pick(h, (128, 240))
    nb = h // tb

    def body(p_ref, mine_ref, *rest):
        others, o_ref = rest[:n], rest[n]
        me = jnp.zeros((tb, c), jnp.int32) + p_ref[0]
        acc = None
        for q in range(n):
            term = jnp.where(me == q, mine_ref[...], others[q][...]).astype(F32)
            acc = term if acc is None else acc + term
        o_ref[...] = acc

    def other(q):
        return pl.BlockSpec((None, tb, c), lambda i, p: (jnp.where(p[0] == q, (q + 1) % n, q), i, 0))

    return pl.pallas_call(
        body, name=name,
        grid_spec=pltpu.PrefetchScalarGridSpec(
            num_scalar_prefetch=1, grid=(nb,),
            in_specs=[pl.BlockSpec((None, tb, c), lambda i, p: (p[0], i, 0))] + [other(q) for q in range(n)],
            out_specs=pl.BlockSpec((tb, c), lambda i, p: (p[1] * nb + i, 0))),
        out_shape=jax.ShapeDtypeStruct((2 * h, c), F32), compiler_params=_cp(),
    )(place, parts, *([by_chip] * n))


def _adamw_math(w, g, m, v):
    m = ADAM_B1 * m + (1.0 - ADAM_B1) * g
    v = ADAM_B2 * v + (1.0 - ADAM_B2) * (g * g)
    m_hat = m / (1.0 - ADAM_B1 ** ADAM_STEP)
    v_hat = v / (1.0 - ADAM_B2 ** ADAM_STEP)
    delta = -ADAM_LR * (m_hat / (jnp.sqrt(v_hat) + ADAM_EPS) + ADAM_WD * w)
    return delta, m, v


def _adamw(w, g, m, v, name):
    r, c = w.shape
    tb = _pick(r, (128, 496, 240))

    def body(w_ref, g_ref, m_ref, v_ref, go_ref, d_ref, mo_ref, vo_ref):
        gv = g_ref[...]
        d, mn, vn = _adamw_math(w_ref[...], gv, m_ref[...], v_ref[...])
        go_ref[...] = gv
        d_ref[...] = d
        mo_ref[...] = mn
        vo_ref[...] = vn

    blk = pl.BlockSpec((tb, c), lambda i: (i, 0))
    return pl.pallas_call(
        body, name=name, grid=(r // tb,), in_specs=[blk] * 4, out_specs=[blk] * 4,
        out_shape=[jax.ShapeDtypeStruct((r, c), F32)] * 4, compiler_params=_cp(),
    )(w, g, m, v)


def _adamw_window(w, g_win, m, v, shard, name):
    r, c = w.shape
    tb = _pick(r, (128,))

    def body(s_ref, w_ref, g_ref, m_ref, v_ref, go_ref, d_ref, mo_ref, vo_ref, g_scr):
        gw = g_ref[...]
        for s in range(N_SHARD):
            @pl.when(s_ref[0] == s)
            def _():
                g_scr[...] = _from_window(gw, s)

        gv = g_scr[:, :c]
        d, mn, vn = _adamw_math(w_ref[...], gv, m_ref[...], v_ref[...])
        go_ref[...] = gv
        d_ref[...] = d
        mo_ref[...] = mn
        vo_ref[...] = vn

    blk = pl.BlockSpec((tb, c), lambda i, s: (i, 0))
    return pl.pallas_call(
        body, name=name,
        grid_spec=pltpu.PrefetchScalarGridSpec(
            num_scalar_prefetch=1, grid=(r // tb,),
            in_specs=[blk, pl.BlockSpec((tb, SHARD_PAD), lambda i, s: (i, 0)), blk, blk], out_specs=[blk] * 4,
            scratch_shapes=[pltpu.VMEM((tb, SHARD_PAD), F32)]),
        out_shape=[jax.ShapeDtypeStruct((r, c), F32)] * 4, compiler_params=_cp(),
    )(shard, w, g_win, m, v)


def _small_update(gathered, w, m, v):
    def body(p_ref, w_ref, m_ref, v_ref, g_ref, d_ref, mo_ref, vo_ref):
        g = p_ref[0]
        for i in range(1, N_DEV):
            g = g + p_ref[i]
        d, mn, vn = _adamw_math(w_ref[...], g, m_ref[...], v_ref[...])
        g_ref[...] = g
        d_ref[...] = d
        mo_ref[...] = mn
        vo_ref[...] = vn

    full = pl.BlockSpec((S_ROWS, 128), lambda i: (0, 0))
    return pl.pallas_call(
        body, name="small_update", grid=(1,),
        in_specs=[pl.BlockSpec((N_DEV, S_ROWS, 128), lambda i: (0, 0, 0)), full, full, full], out_specs=[full] * 4,
        out_shape=[jax.ShapeDtypeStruct((S_ROWS, 128), F32)] * 4, compiler_params=_cp(),
    )(gathered, w, m, v)


_ANY = pl.BlockSpec(memory_space=pl.ANY)


def _place():
    x, y, c = lax.axis_index("x"), lax.axis_index("y"), lax.axis_index("c")
    chips = [(1 - x, y), (x, 1 - y), (1 - x, 1 - y)]
    return x, y, c, chips


def _gather_shards(arrs):
    n = len(arrs)

    def body(*refs):
        ins, outs = refs[:n], refs[n:2 * n]
        send_sems, recv_sems, local_sems = refs[2 * n:2 * n + 3]
        bufs = refs[2 * n + 3:]
        x, y, c, chips = _place()
        me = 2 * x + y
        sibling = (x, y, 1 - c)
        sends = []
        for a in range(n):
            half = ins[a].shape[0] // 2
            mine = pl.ds(pl.multiple_of(c * half, 16), half)
            for j, (qx, qy) in enumerate(chips):
                cp = pltpu.make_async_remote_copy(
                    src_ref=ins[a].at[mine], dst_ref=outs[a].at[me, mine],
                    send_sem=send_sems.at[6 * a + j], recv_sem=recv_sems.at[6 * a + j],
                    device_id=(qx, qy, c), device_id_type=MESH)
                cp.start()
                sends.append(cp)
        for a in range(n):
            step = bufs[a].shape[0]
            for r0 in range(0, ins[a].shape[0], step):
                rows = pl.ds(r0, step)
                load = pltpu.make_async_copy(ins[a].at[rows], bufs[a], local_sems.at[2 * a])
                load.start()
                load.wait()
                store = pltpu.make_async_copy(bufs[a], outs[a].at[me, rows], local_sems.at[2 * a + 1])
                store.start()
                store.wait()
        for a in range(n):
            half = ins[a].shape[0] // 2
            mine = pl.ds(pl.multiple_of(c * half, 16), half)
            for j, (qx, qy) in enumerate(chips):
                q = 2 * qx + qy
                landed = outs[a].at[q, mine]
                pltpu.make_async_remote_copy(
                    src_ref=landed, dst_ref=landed, send_sem=send_sems.at[6 * a + j], recv_sem=recv_sems.at[6 * a + j],
                    device_id=(qx, qy, c), device_id_type=MESH).wait_recv()
                fw = pltpu.make_async_remote_copy(
                    src_ref=landed, dst_ref=landed, send_sem=send_sems.at[6 * a + 3 + j],
                    recv_sem=recv_sems.at[6 * a + 3 + j], device_id=sibling, device_id_type=MESH)
                fw.start()
                sends.append(fw)
        for a in range(n):
            half = ins[a].shape[0] // 2
            theirs = pl.ds(pl.multiple_of((1 - c) * half, 16), half)
            for j, (qx, qy) in enumerate(chips):
                q = 2 * qx + qy
                dst = outs[a].at[q, theirs]
                pltpu.make_async_remote_copy(
                    src_ref=dst, dst_ref=dst, send_sem=send_sems.at[6 * a + 3 + j], recv_sem=recv_sems.at[6 * a + 3 + j],
                    device_id=sibling, device_id_type=MESH).wait_recv()
        for cp in sends:
            cp.wait_send()

    return pl.pallas_call(
        body, name="gather_shards", in_specs=[_ANY] * n, out_specs=[_ANY] * n,
        out_shape=[jax.ShapeDtypeStruct((N_SHARD,) + a.shape, a.dtype) for a in arrs],
        scratch_shapes=[pltpu.SemaphoreType.DMA((6 * n,)), pltpu.SemaphoreType.DMA((6 * n,)),
                        pltpu.SemaphoreType.DMA((2 * n,))]
        + [pltpu.VMEM((_pick(a.shape[0], (256, 496)), a.shape[1]), a.dtype) for a in arrs],
        compiler_params=pltpu.CompilerParams(has_side_effects=True, vmem_limit_bytes=VMEM_LIMIT),
    )(*arrs)


def _pair_reduce_send(grads, tag):
    n = len(grads)

    def body(*refs):
        ins, outs = refs[:n], refs[n:2 * n]
        send_sems, recv_sems = refs[2 * n:]
        x, y, c, _ = _place()
        sibling = (x, y, 1 - c)
        cps = []
        for a in range(n):
            half = ins[a].shape[1] // 2
            theirs = pl.ds(pl.multiple_of((1 - c) * half, 8), half)
            cp = pltpu.make_async_remote_copy(
                src_ref=ins[a].at[:, theirs], dst_ref=outs[a], send_sem=send_sems.at[a], recv_sem=recv_sems.at[a],
                device_id=sibling, device_id_type=MESH)
            cp.start()
            cps.append(cp)
        for cp in cps:
            cp.wait()

    return pl.pallas_call(
        body, name="pair_reduce_send_" + tag, in_specs=[_ANY] * n, out_specs=[_ANY] * n,
        out_shape=[jax.ShapeDtypeStruct((g.shape[0], g.shape[1] // 2, g.shape[2]), g.dtype) for g in grads],
        scratch_shapes=[pltpu.SemaphoreType.DMA((n,)), pltpu.SemaphoreType.DMA((n,))],
        compiler_params=pltpu.CompilerParams(has_side_effects=True),
    )(*grads)


_HBM = pl.BlockSpec(memory_space=pltpu.HBM)
_SEM = pl.BlockSpec(memory_space=pltpu.SEMAPHORE)
_DATAFLOW = pltpu.SideEffectType.DATAFLOW_SIDE_EFFECTING


def _chip_exchange_copies(ins, lands, send_sems, recv_sems):
    x, y, c, chips = _place()
    me = 2 * x + y
    cps = []
    for a in range(len(ins)):
        for j, (qx, qy) in enumerate(chips):
            cps.append(pltpu.make_async_remote_copy(
                src_ref=ins[a].at[2 * qx + qy], dst_ref=lands[a].at[me], send_sem=send_sems.at[3 * a + j],
                recv_sem=recv_sems.at[3 * a + j], device_id=(qx, qy, c), device_id_type=MESH))
    return cps


def _chip_exchange_start(parts, tag):
    n = len(parts)

    def body(*refs):
        ins, lands = refs[:n], refs[n:2 * n]
        send_sems, recv_sems = refs[2 * n:2 * n + 2]
        token = refs[4 * n + 2]
        for cp in _chip_exchange_copies(ins, lands, send_sems, recv_sems):
            cp.start()
        token[...] = jnp.zeros_like(token)

    hbm = [pltpu.HBM(p.shape, p.dtype) for p in parts]
    lands = [pltpu.with_memory_space_constraint(lax.empty(p.shape, p.dtype), pltpu.HBM) for p in parts]
    res = pl.pallas_call(
        body, name="chip_exchange_start_" + tag,
        out_shape=(pltpu.SemaphoreType.DMA((3 * n,)), pltpu.SemaphoreType.DMA((3 * n,)), *hbm, *hbm,
                   jax.ShapeDtypeStruct((8, 128), F32)),
        in_specs=[_HBM] * (2 * n), out_specs=(_SEM, _SEM, *([_HBM] * (2 * n)), pl.BlockSpec(memory_space=pltpu.VMEM)),
        input_output_aliases={a: 2 + a for a in range(2 * n)},
        compiler_params=pltpu.CompilerParams(has_side_effects=_DATAFLOW),
    )(*[pltpu.with_memory_space_constraint(p, pltpu.HBM) for p in parts], *lands)
    return res[0], res[1], res[2:2 + n], res[2 + n:2 + 2 * n], res[2 + 2 * n]


def _chip_exchange_wait(send_sems, recv_sems, parts, lands, after, tag):
    n = len(parts)

    def body(*refs):
        ins, land_refs = refs[:n], refs[n:2 * n]
        s_sems, r_sems = refs[2 * n:2 * n + 2]
        for cp in _chip_exchange_copies(ins, land_refs, s_sems, r_sems):
            cp.wait_send()
            cp.wait_recv()

    hbm = [pltpu.HBM(p.shape, p.dtype) for p in parts]
    res = pl.pallas_call(
        body, name="chip_exchange_wait_" + tag, out_shape=(*hbm, *hbm),
        in_specs=[_HBM] * (2 * n) + [_SEM, _SEM, _ANY], out_specs=tuple([_HBM] * (2 * n)),
        input_output_aliases={a: a for a in range(2 * n)},
        compiler_params=pltpu.CompilerParams(has_side_effects=_DATAFLOW),
    )(*parts, *lands, send_sems, recv_sems, after)
    return res[:n], res[n:]


def _shard_gather_copies(src, land, send_sems, recv_sems):
    x, y, c, chips = _place()
    me = 2 * x + y
    return [pltpu.make_async_remote_copy(
        src_ref=src, dst_ref=land.at[me], send_sem=send_sems.at[j], recv_sem=recv_sems.at[j],
        device_id=(qx, qy, c), device_id_type=MESH) for j, (qx, qy) in enumerate(chips)]


def _shard_gather_start(shard_arr, after):
    def body(src, land, after_ref, send_sems, recv_sems, src_thru, land_thru, token):
        for cp in _shard_gather_copies(src, land, send_sems, recv_sems):
            cp.start()
        token[...] = jnp.zeros_like(token)

    land_shape = (N_SHARD,) + shard_arr.shape
    land = pltpu.with_memory_space_constraint(lax.empty(land_shape, shard_arr.dtype), pltpu.HBM)
    return pl.pallas_call(
        body, name="shard_gather_start",
        out_shape=(pltpu.SemaphoreType.DMA((N_SHARD - 1,)), pltpu.SemaphoreType.DMA((N_SHARD - 1,)),
                   pltpu.HBM(shard_arr.shape, shard_arr.dtype), pltpu.HBM(land_shape, shard_arr.dtype),
                   jax.ShapeDtypeStruct((8, 128), F32)),
        in_specs=[_HBM, _HBM, _ANY], out_specs=(_SEM, _SEM, _HBM, _HBM, pl.BlockSpec(memory_space=pltpu.VMEM)),
        input_output_aliases={0: 2, 1: 3},
        compiler_params=pltpu.CompilerParams(has_side_effects=_DATAFLOW),
    )(pltpu.with_memory_space_constraint(shard_arr, pltpu.HBM), land, after)


def _shard_gather_wait(send_sems, recv_sems, shard_arr, land, after):
    def body(src, land_ref, s_sems, r_sems, after_ref, src_out, land_out):
        for cp in _shard_gather_copies(src, land_ref, s_sems, r_sems):
            cp.wait_send()
            cp.wait_recv()

    return pl.pallas_call(
        body, name="shard_gather_wait",
        out_shape=(pltpu.HBM(shard_arr.shape, shard_arr.dtype), pltpu.HBM(land.shape, land.dtype)),
        in_specs=[_HBM, _HBM, _SEM, _SEM, _ANY], out_specs=(_HBM, _HBM), input_output_aliases={0: 0, 1: 1},
        compiler_params=pltpu.CompilerParams(has_side_effects=_DATAFLOW),
    )(shard_arr, land, send_sems, recv_sems, after)


def _pair_allgather(fulls, tag):
    n = len(fulls)

    def body(*refs):
        outs = refs[n:2 * n]
        send_sems, recv_sems = refs[2 * n:]
        x, y, c, _ = _place()
        sibling = (x, y, 1 - c)
        cps = []
        for a in range(n):
            half = outs[a].shape[0] // 2
            mine = outs[a].at[pl.ds(pl.multiple_of(c * half, 8), half)]
            cp = pltpu.make_async_remote_copy(
                src_ref=mine, dst_ref=mine, send_sem=send_sems.at[a], recv_sem=recv_sems.at[a],
                device_id=sibling, device_id_type=MESH)
            cp.start()
            cps.append(cp)
        for a in range(n):
            half = outs[a].shape[0] // 2
            theirs = outs[a].at[pl.ds(pl.multiple_of((1 - c) * half, 8), half)]
            pltpu.make_async_remote_copy(
                src_ref=theirs, dst_ref=theirs, send_sem=send_sems.at[a], recv_sem=recv_sems.at[a],
                device_id=sibling, device_id_type=MESH).wait_recv()
        for cp in cps:
            cp.wait_send()

    return pl.pallas_call(
        body, name="pair_allgather_" + tag, in_specs=[_ANY] * n, out_specs=[_ANY] * n,
        out_shape=[jax.ShapeDtypeStruct(f.shape, f.dtype) for f in fulls],
        input_output_aliases={a: a for a in range(n)},
        scratch_shapes=[pltpu.SemaphoreType.DMA((n,)), pltpu.SemaphoreType.DMA((n,))],
        compiler_params=pltpu.CompilerParams(has_side_effects=True),
    )(*fulls)


def _allgather_small(slab):
    def body(s_ref, out_ref, send_sems, recv_sems):
        x, y, c, _ = _place()
        me = 4 * x + 2 * y + c
        out_ref[me] = s_ref[...]
        cps = []
        for mask in range(1, N_DEV):
            peer = (x ^ (mask >> 2), y ^ ((mask >> 1) & 1), c ^ (mask & 1))
            cp = pltpu.make_async_remote_copy(
                src_ref=s_ref, dst_ref=out_ref.at[me], send_sem=send_sems.at[mask - 1], recv_sem=recv_sems.at[mask - 1],
                device_id=peer, device_id_type=MESH)
            cp.start()
            cps.append(cp)
        for mask in range(1, N_DEV):
            peer = (x ^ (mask >> 2), y ^ ((mask >> 1) & 1), c ^ (mask & 1))
            dst = out_ref.at[4 * peer[0] + 2 * peer[1] + peer[2]]
            pltpu.make_async_remote_copy(
                src_ref=dst, dst_ref=dst, send_sem=send_sems.at[mask - 1], recv_sem=recv_sems.at[mask - 1],
                device_id=peer, device_id_type=MESH).wait_recv()
        for cp in cps:
            cp.wait_send()

    vm = pl.BlockSpec(memory_space=pltpu.VMEM)
    return pl.pallas_call(
        body, name="allgather_small", in_specs=[vm], out_specs=vm,
        out_shape=jax.ShapeDtypeStruct((N_DEV,) + slab.shape, slab.dtype),
        scratch_shapes=[pltpu.SemaphoreType.DMA((N_DEV - 1,)), pltpu.SemaphoreType.DMA((N_DEV - 1,))],
        compiler_params=pltpu.CompilerParams(has_side_effects=True),
    )(slab)


def _pack_b(w_mem_kv, w_br_dn, w_br_sb, w_br_mem, w_out):
    return jnp.concatenate([w_mem_kv.reshape(128, D_MODEL), w_br_dn, w_br_sb, w_br_mem.reshape(64, D_MODEL), w_out],
                           axis=0)


def _conv_slab(conv_w):
    return jnp.pad(conv_w.reshape(3, D_MODEL), ((0, 29), (0, 0)))


def _unpack_b(slab):
    return (slab[B_MEMKV:B_BRDN].reshape(1, 256, 512), slab[B_BRDN:B_BRSB].reshape(1, 256, D_MODEL),
            slab[B_BRSB:B_BRMEM].reshape(1, 256, D_MODEL), slab[B_BRMEM:B_OUT].reshape(1, 256, 256),
            slab[B_OUT:B_CONV].reshape(1, 256, D_MODEL))


def _conv_rows(conv_full):
    return conv_full.reshape(4 * CONV_BLOCKS, 128)


def _conv_shard_rows(conv_shard, shard):
    own = CONV_BLOCKS // N_SHARD
    blocks = lax.dynamic_update_slice(jnp.zeros((4, CONV_BLOCKS, 128), F32), conv_shard.reshape(4, own, 128),
                                      (0, own * shard, 0))
    return blocks.reshape(4 * CONV_BLOCKS, 128)


def _conv_shard_of(rows, shard):
    own = CONV_BLOCKS // N_SHARD
    blocks = lax.dynamic_slice(rows.reshape(4, CONV_BLOCKS, 128), (0, own * shard, 0), (4, own, 128))
    return blocks.reshape(1, 4, own * 128)


def _pack_small(norm_g, mem_norm_g, final_g, dn_norm_g, a_log, dt_bias, conv_rows, loss=None):
    slab = jnp.zeros((S_CONV, 128), F32)
    slab = slab.at[S_NORM:S_NORM + 8].set(norm_g.reshape(8, 128))
    slab = slab.at[S_MEMNORM:S_MEMNORM + 8].set(mem_norm_g.reshape(8, 128))
    slab = slab.at[S_FINAL:S_FINAL + 8].set(final_g.reshape(8, 128))
    slab = slab.at[S_DNNORM].set(dn_norm_g.reshape(128))
    slab = slab.at[S_ALOG, :N_HEADS].set(a_log.reshape(N_HEADS))
    slab = slab.at[S_DTB, :N_HEADS].set(dt_bias.reshape(N_HEADS))
    if loss is not None:
        slab = slab.at[S_LOSS, 0].set(loss)
    return jnp.concatenate([slab, conv_rows], axis=0)


def _unpack_small(slab, shard):
    return (slab[S_NORM:S_NORM + 8].reshape(1, D_MODEL), slab[S_MEMNORM:S_MEMNORM + 8].reshape(1, D_MODEL),
            slab[S_FINAL:S_FINAL + 8].reshape(D_MODEL), slab[S_DNNORM].reshape(1, 128),
            slab[S_ALOG, :N_HEADS].reshape(1, N_HEADS), slab[S_DTB, :N_HEADS].reshape(1, N_HEADS),
            _conv_shard_of(slab[S_CONV:], shard))


def _reorder_w_in(w_full):
    pad = jnp.zeros((w_full.shape[0], W_R - IN_WIDTH), w_full.dtype)
    return jnp.concatenate([w_full[:, :4096], w_full[:, 4112:], w_full[:, 4096:4112], pad], axis=1)


def _windows_to_w_r(win):
    b = 128
    s0, s1, s2, s3 = win[0], win[1], win[2], win[3]
    e1, e2, e3 = WIN_START[1] * b, WIN_START[2] * b, WIN_START[3] * b
    n1, n2 = e2 - e1, e3 - e2
    return jnp.concatenate([
        s0[:, :e1], s0[:, e1:e1 + b] + s1[:, :b],
        s1[:, b:n1], s1[:, n1:n1 + b] + s2[:, :b],
        s2[:, b:n2], s2[:, n2:n2 + b] + s3[:, :b],
        s3[:, b:], s1[:, _S1_BA_POS:]], axis=1)


def _dproj_windows(dproj_r):
    b = 128
    pieces = []
    for s in range(N_SHARD):
        lo = WIN_START[s] * b
        if s == 1:
            pieces += [dproj_r[:, lo:lo + _S1_BA_POS], dproj_r[:, C_BA:C_BA + b]]
        else:
            pieces.append(dproj_r[:, lo:lo + SHARD_PAD])
    return jnp.concatenate(pieces, axis=1)


def _local_step(x, mem, tgt, norm_g, mem_norm_g, w_r, w_sh, conv_w, a_log, dt_bias, dn_norm_g, proj_weights, final_g,
                on_early=None, after_gather=None):
    t = x.shape[0]
    final_row = final_g.reshape(1, D_MODEL)
    alog_row = jnp.zeros((1, 128), F32).at[0, N_HEADS:2 * N_HEADS].set(a_log.reshape(N_HEADS))
    dtb_row = jnp.zeros((1, 128), F32).at[0, N_HEADS:2 * N_HEADS].set(dt_bias.reshape(N_HEADS))

    h = _rmsnorm_fwd(x, norm_g, "norm_fwd")
    proj = _mm(h, w_r, "nn", "in_proj", after=after_gather)
    qkv = _dn_prep_fwd(proj, conv_w)
    beta_t, g_t = _dn_gate_fwd(proj, alog_row, dtb_row)
    dn_u, dn_w, dn_qd, dn_kd, dn_a, tinv_all, dn_el = _dn_intra_fwd(qkv, beta_t, g_t)
    o_dn, dn_vn, s_all = _dn_scan_fwd(dn_u, dn_w, dn_qd, dn_kd, dn_a, dn_el)
    o_dn_g = _dn_post_fwd(o_dn, proj, dn_norm_g)
    o_sb, o_sb_g, sb_l = _sb_fwd(proj)
    w_mem_kv, w_br_dn, w_br_sb, w_br_mem, w_out = proj_weights(o_sb_g)
    mem_n = _rmsnorm_fwd(mem, mem_norm_g, "mem_norm_fwd")
    mkv = _mm(mem_n, w_mem_kv, "nn", "mem_kv")
    o_m, o_m_g = _mem_fwd(proj, mkv)
    y_dn = _mm(o_dn_g, w_br_dn, "nn", "br_dn")
    y_sb = _mm(o_sb_g, w_br_sb, "nn", "br_sb")
    y_m = _mm(o_m_g, w_br_mem, "nn", "br_mem")
    merged = _merge_fwd(proj, y_dn, y_sb, y_m)
    mo = _mm(merged, w_out, "nn", "out_proj")
    d_out, d_out_b, loss_row, g_final = _final_loss(x, mo, final_row, tgt)

    g_w_out = _mm(merged, d_out_b, "tn", "g_w_out")
    d_merged = _mm(d_out_b, w_out, "nt", "d_merged")
    dy_dn, dy_sb, dy_m, dg1, dg2, dg3 = _merge_bwd(proj, y_dn, y_sb, y_m, d_merged)
    g_w_br_dn = _mm(o_dn_g, dy_dn, "tn", "g_w_br_dn")
    g_w_br_sb = _mm(o_sb_g, dy_sb, "tn", "g_w_br_sb")
    g_w_br_mem = _mm(o_m_g, dy_m, "tn", "g_w_br_mem")
    d_o_dn_g = _mm(dy_dn, w_br_dn, "nt", "d_o_dn")
    d_o_sb_g = _mm(dy_sb, w_br_sb, "nt", "d_o_sb")
    d_o_m_g = _mm(dy_m, w_br_mem, "nt", "d_o_mem")

    d_mq, d_mz, d_mkv = _mem_bwd(proj, mkv, o_m, d_o_m_g)
    d_mkv_b = _cast_bf16(d_mkv, "cast_dmkv")
    g_w_mem_kv = _mm(mem_n, d_mkv_b, "tn", "g_w_mem_kv")
    d_mem_n = _mm(d_mkv_b, w_mem_kv, "nt", "d_mem_n")
    _, g_mem_norm = _rmsnorm_bwd(mem, mem_norm_g, d_mem_n, jnp.zeros_like(mem), "mem_norm_bwd")

    early = dict(w_mem_kv=g_w_mem_kv, w_br_dn=g_w_br_dn, w_br_sb=g_w_br_sb, w_br_mem=g_w_br_mem, w_out=g_w_out)
    after_early = on_early(early) if on_early is not None else None

    d_sq, d_sk, d_sv, d_sz = _sb_bwd(proj, o_sb, sb_l, d_o_sb_g, after=after_early)

    d_o_dn, d_dnz, g_dn_norm = _dn_post_bwd(o_dn, proj, dn_norm_g, d_o_dn_g)
    d_vnew, d_kd, d_qd, d_w, d_el = _dn_scan_bwd(dn_w, dn_qd, dn_kd, dn_a, dn_el, dn_vn, s_all, d_o_dn)
    d_qn, d_kn, d_vn, dbeta_t, dg_t = _dn_intra_bwd(qkv, beta_t, g_t, tinv_all, dn_vn, d_o_dn, d_vnew, d_kd, d_qd, d_w, d_el)
    d_conv_in, g_conv = _dn_prep_bwd(proj, conv_w, d_qn, d_kn, d_vn)
    d_ba, g_alog_row, g_dtb_row = _dn_gate_bwd(proj, alog_row, dtb_row, dbeta_t, dg_t)

    dproj_sh = _dproj_windows(
        jnp.concatenate([d_conv_in, d_dnz, d_sq, d_sk, d_sv, d_sz, d_mq, d_mz, dg1, dg2, dg3, d_ba], axis=1))
    g_w_sh = _mm(h, dproj_sh, "tn", "g_w_in", out_dtype=BF16, out_shards=N_SHARD)
    def input_grad(after=None):
        dh = _mm(dproj_sh, w_sh, "nt", "d_h", after=after)
        grad_x, g_norm = _rmsnorm_bwd(x, norm_g, dh, d_out, "norm_bwd")
        small = dict(norm_g=g_norm, mem_norm_g=g_mem_norm, final_g=g_final, dn_norm_g=g_dn_norm,
                     a_log=g_alog_row[:, N_HEADS:2 * N_HEADS], dt_bias=g_dtb_row[:, N_HEADS:2 * N_HEADS],
                     conv_w=g_conv)
        return grad_x, small

    return loss_row[0, 0], early, g_w_sh, input_grad


def _reduce_scatter_start(grads, tag):
    c = lax.axis_index("c")
    core = jnp.reshape(c, (1,)).astype(jnp.int32)
    recv = _pair_reduce_send(grads, tag)
    parts = [_pair_add(g, r, core, "pair_add_" + tag) for g, r in zip(grads, recv)]
    return _chip_exchange_start(parts, tag)


def _reduce_scatter_finish(handle, after, tag):
    send_sems, recv_sems, parts, lands, _ = handle
    x, y, c = lax.axis_index("x"), lax.axis_index("y"), lax.axis_index("c")
    place = jnp.stack([2 * x + y, c]).astype(jnp.int32)
    parts, by_chip = _chip_exchange_wait(send_sems, recv_sems, parts, lands, after, tag)
    fulls = [_chip_sum(p, b, place, "chip_sum_" + tag) for p, b in zip(parts, by_chip)]
    return _pair_allgather(fulls, tag)


def kernel(x, mem, norm_g, mem_norm_g, w_in, conv_w, a_log, dt_bias, dn_norm_g, w_mem_kv, w_br_dn, w_br_sb, w_br_mem, w_out, final_g, loss_target, m_norm_g, m_mem_norm_g, m_w_in, m_conv_w, m_a_log, m_dt_bias, m_dn_norm_g, m_w_mem_kv, m_w_br_dn, m_w_br_sb, m_w_br_mem, m_w_out, m_final_g, v_norm_g, v_mem_norm_g, v_w_in, v_conv_w, v_a_log, v_dt_bias, v_dn_norm_g, v_w_mem_kv, v_w_br_dn, v_w_br_sb, v_w_br_mem, v_w_out, v_final_g):
    w_a = w_in[0]
    w_b = _pack_b(w_mem_kv[0], w_br_dn[0], w_br_sb[0], w_br_mem[0], w_out[0])
    m_b = _pack_b(m_w_mem_kv[0], m_w_br_dn[0], m_w_br_sb[0], m_w_br_mem[0], m_w_out[0])
    v_b = _pack_b(v_w_mem_kv[0], v_w_br_dn[0], v_w_br_sb[0], v_w_br_mem[0], v_w_out[0])

    shard_idx = 2 * lax.axis_index("x") + lax.axis_index("y")
    shard = jnp.reshape(shard_idx, (1,)).astype(jnp.int32)
    ga, g_conv = _gather_shards([_cast_to_window(w_a, shard, "cast_w_in"),
                                 _cast_bf16(_conv_slab(conv_w[0]), "cast_conv")])
    w_r = _windows_to_w_r(ga)
    f_conv = g_conv[:, :3].reshape(N_SHARD, 4, 768).transpose(1, 0, 2).reshape(4, 3 * D_MODEL).astype(F32)
    b_flight = _shard_gather_start(_cast_bf16(w_b, "cast_w_b"), after=ga)

    def proj_weights(after):
        own, land = _shard_gather_wait(b_flight[0], b_flight[1], b_flight[2], b_flight[3], after)
        gb = lax.dynamic_update_slice(land, own[None], (shard_idx, 0, 0))
        return (gb[:, B_MEMKV:B_BRDN].reshape(N_SHARD * 256, 512),
                gb[:, B_BRDN:B_BRSB].reshape(N_SHARD * 256, D_MODEL),
                gb[:, B_BRSB:B_BRMEM].reshape(N_SHARD * 256, D_MODEL),
                gb[:, B_BRMEM:B_OUT].reshape(N_SHARD, 256, 256).transpose(1, 0, 2).reshape(256, D_MODEL),
                gb[:, B_OUT:B_CONV].reshape(N_SHARD * 256, D_MODEL))

    flights = {}

    def on_early(grads):
        g_b = jnp.stack([
            _pack_b(grads["w_mem_kv"][256 * s:256 * (s + 1)], grads["w_br_dn"][256 * s:256 * (s + 1)],
                    grads["w_br_sb"][256 * s:256 * (s + 1)], grads["w_br_mem"][:, 256 * s:256 * (s + 1)],
                    grads["w_out"][256 * s:256 * (s + 1)])
            for s in range(N_SHARD)]).astype(BF16)
        flights["b"] = _reduce_scatter_start([g_b], "b")
        return flights["b"][4]

    loss, _, g_w_sh, input_grad = _local_step(
        x[0], mem[0], loss_target[0], norm_g, mem_norm_g, w_r, ga, f_conv, a_log, dt_bias, dn_norm_g,
        proj_weights, final_g, on_early=on_early, after_gather=b_flight[4])
    flights["a"] = _reduce_scatter_start([g_w_sh], "a")
    grad_x, small = input_grad(after=flights["a"][4])

    part = _pack_small(small["norm_g"], small["mem_norm_g"], small["final_g"], small["dn_norm_g"],
                       small["a_log"], small["dt_bias"], _conv_rows(small["conv_w"]), loss)
    w_s = _pack_small(norm_g, mem_norm_g, final_g, dn_norm_g, a_log, dt_bias, _conv_shard_rows(conv_w[0], shard_idx))
    m_s = _pack_small(m_norm_g, m_mem_norm_g, m_final_g, m_dn_norm_g, m_a_log, m_dt_bias,
                      _conv_shard_rows(m_conv_w[0], shard_idx))
    v_s = _pack_small(v_norm_g, v_mem_norm_g, v_final_g, v_dn_norm_g, v_a_log, v_dt_bias,
                      _conv_shard_rows(v_conv_w[0], shard_idx))
    g_s, d_s, nm_s, nv_s = _small_update(_allgather_small(part), w_s, m_s, v_s)

    (gs_b,) = _reduce_scatter_finish(flights["b"], after=g_s, tag="b")
    (gs_in,) = _reduce_scatter_finish(flights["a"], after=gs_b, tag="a")
    gr_in, d_in, nm_in, nv_in = _adamw_window(w_a, gs_in, m_w_in[0], v_w_in[0], shard, "adamw_w_in")
    gr_b, d_b, nm_b, nv_b = _adamw(w_b, gs_b, m_b, v_b, "adamw_b")

    def assemble(slab_small, a_in, slab_b):
        s_norm, s_memnorm, s_final, s_dnnorm, s_alog, s_dtb, b_conv = _unpack_small(slab_small, shard_idx)
        b_memkv, b_brdn, b_brsb, b_brmem, b_out = _unpack_b(slab_b)
        return [s_norm, s_memnorm, a_in.reshape(1, D_MODEL, IN_WIDTH // N_SHARD), b_conv, s_alog, s_dtb, s_dnnorm,
                b_memkv, b_brdn, b_brsb, b_brmem, b_out, s_final]

    outs = [g_s[S_LOSS, 0], grad_x.reshape(1, -1, D_MODEL)]
    outs += assemble(g_s, gr_in, gr_b)
    outs += assemble(d_s, d_in, d_b)
    outs += assemble(nm_s, nm_in, nm_b)
    outs += assemble(nv_s, nv_in, nv_b)
    return tuple(outs)
```

```python
import functools
import math

import jax
import jax.numpy as jnp
from jax import lax
from jax.experimental import pallas as pl
from jax.experimental.pallas import tpu as pltpu

F32 = jnp.float32
BF16 = jnp.bfloat16
MESH = pl.DeviceIdType.MESH
HIGHEST = lax.Precision.HIGHEST

D_MODEL = 1024
N_HEADS = 8
D_HEAD = 128
DN_CHUNK = 64
DN_GROUP = 16
DN_SCAN_GROUP = 4
SB_BLOCK = 256
SB_HEADS_PER_STEP = 2
SB_QBLOCK = 256
MEM_HEADS = 4
MEM_DH = 64
MEM_W = MEM_HEADS * MEM_DH
NORM_EPS = 1e-6
IN_WIDTH = 11792
N_SHARD = 4
SHARD_W = IN_WIDTH // N_SHARD
SHARD_PAD = 3072
N_DEV = 8

C_DNZ = 3072
C_SBQ = 4096
C_SBZ = 7168
C_MQ = 8192
C_MZ = 8448
C_GATES = 8704
C_BA = 11776
W_R = 11904

ADAM_LR = 0.001
ADAM_B1 = 0.9
ADAM_B2 = 0.999
ADAM_EPS = 1e-08
ADAM_WD = 0.01
ADAM_STEP = 10

VMEM_LIMIT = 56 * 1024 * 1024

B_ROWS = 992
B_MEMKV, B_BRDN, B_BRSB, B_BRMEM, B_OUT, B_CONV = 0, 128, 384, 640, 704, 960
S_NORM, S_MEMNORM, S_FINAL, S_DNNORM, S_ALOG, S_DTB, S_LOSS, S_CONV, S_ROWS = 0, 8, 16, 24, 25, 26, 27, 32, 128
CONV_BLOCKS = 3 * D_MODEL // 128


def _cp(**kw):
    return pltpu.CompilerParams(vmem_limit_bytes=VMEM_LIMIT, **kw)


def _dot(a, b, dims):
    lead = a.ndim - 2
    ca, cb = {"nn": (1, 0), "nt": (1, 1), "tn": (0, 0)}[dims]
    batch = tuple(range(lead))
    return lax.dot_general(a, b, (((ca + lead,), (cb + lead,)), (batch, batch)), preferred_element_type=F32)


def _chunks(x):
    return x.reshape(x.shape[0] // DN_CHUNK, DN_CHUNK, x.shape[1])


def _unchunk(x):
    return x.reshape(x.shape[0] * x.shape[1], x.shape[2])


def _bdot(a, b, dims):
    return _dot(a.astype(BF16), b.astype(BF16), dims)


def _split(a):
    hi = a.astype(BF16)
    return hi, (a - hi.astype(F32)).astype(BF16)


def _dot3(a, b, dims):
    a1, a2 = _split(a)
    b1, b2 = _split(b)
    return _dot(a1, b1, dims) + (_dot(a1, b2, dims) + _dot(a2, b1, dims))


def _ones_dot(a, ones_bf16):
    out = _dot(a.reshape(-1, a.shape[-1]).astype(BF16), ones_bf16, "nn")
    return out.reshape(a.shape[:-1] + (ones_bf16.shape[1],))


def _sigmoid(x):
    return 1.0 / (1.0 + jnp.exp(-x))


def _log1p_small(u):
    return jnp.where(u < 1e-2, u * (1.0 - u * (0.5 - u * (1.0 / 3.0))), jnp.log(1.0 + u))


def _log_sigmoid(z):
    return jnp.minimum(z, 0.0) - _log1p_small(jnp.exp(-jnp.abs(z)))


def _pick(dim, cands):
    for c in cands:
        if dim % c == 0:
            return c
    return dim


def _mm(a, b, dims, name, out_dtype=F32, out_shards=1, after=None):
    ta, tb = dims[0] == "t", dims[1] == "t"
    m, k = (a.shape[1], a.shape[0]) if ta else a.shape
    b_shards = b.shape[0] if b.ndim == 3 else 1
    n = b.shape[-2] if tb else b.shape[-1]
    tm = _pick(m, (1024, 512, 256))
    tn = _pick(n // out_shards, (512, 384, 256, 128))
    tk = _pick(k // b_shards, (1024, 512, 384, 256))
    nk = k // tk

    def body(a_ref, b_ref, *rest):
        o_ref, acc_ref = rest[-2:]
        kk = pl.program_id(2)

        @pl.when(kk == 0)
        def _():
            acc_ref[...] = jnp.zeros_like(acc_ref)

        acc_ref[...] += _bdot(a_ref[...], b_ref[...], dims)

        @pl.when(kk == nk - 1)
        def _():
            o_ref[...] = acc_ref[...].astype(out_dtype)

    a_spec = pl.BlockSpec((tk, tm), lambda i, j, q: (q, i)) if ta else pl.BlockSpec((tm, tk), lambda i, j, q: (i, q))
    if b_shards > 1:
        per_k = k // b_shards // tk
        b_spec = pl.BlockSpec((None, tn, tk), lambda i, j, q: (q // per_k, j, q % per_k))
    else:
        b_spec = pl.BlockSpec((tn, tk), lambda i, j, q: (j, q)) if tb else pl.BlockSpec((tk, tn), lambda i, j, q: (q, j))
    if out_shards > 1:
        per_n = n // out_shards // tn
        out_spec = pl.BlockSpec((None, tm, tn), lambda i, j, q: (j // per_n, i, j % per_n))
        out_shape = jax.ShapeDtypeStruct((out_shards, m, n // out_shards), out_dtype)
    else:
        out_spec = pl.BlockSpec((tm, tn), lambda i, j, q: (i, j))
        out_shape = jax.ShapeDtypeStruct((m, n), out_dtype)
    extra_specs, extra = [], []
    if after is not None:
        extra_specs, extra = [pl.BlockSpec(after.shape, lambda i, j, q: (0, 0))], [after]
    return pl.pallas_call(
        body, name=name, grid=(m // tm, n // tn, nk),
        in_specs=[a_spec, b_spec] + extra_specs, out_specs=out_spec, out_shape=out_shape,
        scratch_shapes=[pltpu.VMEM((tm, tn), F32)],
        compiler_params=_cp(dimension_semantics=("parallel", "parallel", "arbitrary")),
    )(a, b, *extra)


def _rmsnorm_fwd(x, g, name):
    t, d = x.shape
    tb = _pick(t, (512, 256))

    def body(x_ref, g_ref, h_ref):
        xv = x_ref[...]
        r = lax.rsqrt(jnp.mean(xv * xv, axis=-1, keepdims=True) + NORM_EPS)
        h_ref[...] = ((xv * r) * g_ref[...]).astype(BF16)

    return pl.pallas_call(
        body, name=name, grid=(t // tb,),
        in_specs=[pl.BlockSpec((tb, d), lambda i: (i, 0)), pl.BlockSpec((1, d), lambda i: (0, 0))],
        out_specs=pl.BlockSpec((tb, d), lambda i: (i, 0)),
        out_shape=jax.ShapeDtypeStruct((t, d), BF16), compiler_params=_cp(),
    )(x, g)


def _rmsnorm_bwd(x, g, dh, resid, name):
    t, d = x.shape
    tb = _pick(t, (256,))

    def body(x_ref, g_ref, dh_ref, r_ref, dx_ref, dg_ref):
        @pl.when(pl.program_id(0) == 0)
        def _():
            dg_ref[...] = jnp.zeros_like(dg_ref)

        xv = x_ref[...]
        r = lax.rsqrt(jnp.mean(xv * xv, axis=-1, keepdims=True) + NORM_EPS)
        xhat = xv * r
        dhv = dh_ref[...]
        dg_ref[...] += jnp.sum(dhv * xhat, axis=0, keepdims=True)
        dxh = dhv * g_ref[...]
        dx_ref[...] = r_ref[...] + r * (dxh - xhat * jnp.mean(dxh * xhat, axis=-1, keepdims=True))

    row = pl.BlockSpec((tb, d), lambda i: (i, 0))
    vec = pl.BlockSpec((1, d), lambda i: (0, 0))
    return pl.pallas_call(
        body, name=name, grid=(t // tb,), in_specs=[row, vec, row, row], out_specs=[row, vec],
        out_shape=[jax.ShapeDtypeStruct((t, d), F32), jax.ShapeDtypeStruct((1, d), F32)], compiler_params=_cp(),
    )(x, g, dh, resid)


def _conv_silu(xv, w, row):
    y = xv * w[3:4, :]
    for s in (1, 2, 3):
        xs = jnp.where(row >= s, pltpu.roll(xv, s, 0), 0.0)
        y = y + xs * w[3 - s:4 - s, :]
    return y, y * _sigmoid(y)


def _dn_prep_fwd(proj, conv_w):
    t = proj.shape[0]

    def body(p_ref, w_ref, o_ref):
        j = pl.program_id(0)
        xv = p_ref[...]
        row = lax.broadcasted_iota(jnp.int32, xv.shape, 0)
        _, a = _conv_silu(xv, w_ref[...], row)
        inv = lax.rsqrt(jnp.sum(a * a, axis=-1, keepdims=True) + NORM_EPS)
        scale = jnp.where(j < N_HEADS, D_HEAD ** -0.5, 1.0)
        normed = jnp.where(j < 2 * N_HEADS, 1.0, 0.0)
        o_ref[...] = a * (normed * (inv * scale) + (1.0 - normed))

    return pl.pallas_call(
        body, name="dn_prep_fwd", grid=(3 * N_HEADS,),
        in_specs=[pl.BlockSpec((t, D_HEAD), lambda j: (0, j)), pl.BlockSpec((4, D_HEAD), lambda j: (0, j))],
        out_specs=pl.BlockSpec((t, D_HEAD), lambda j: (0, j)),
        out_shape=jax.ShapeDtypeStruct((t, 3 * D_MODEL), F32), compiler_params=_cp(),
    )(proj, conv_w)


def _dn_prep_bwd(proj, conv_w, dq, dk, dv):
    t = proj.shape[0]

    def body(p_ref, w_ref, dq_ref, dk_ref, dv_ref, dp_ref, dw_ref):
        j = pl.program_id(0)
        xv = p_ref[...]
        w = w_ref[...]
        row = lax.broadcasted_iota(jnp.int32, xv.shape, 0)
        y, a = _conv_silu(xv, w, row)
        part = jnp.zeros(xv.shape, jnp.int32) + j // N_HEADS
        dn = jnp.where(part == 0, dq_ref[...], jnp.where(part == 1, dk_ref[...], dv_ref[...]))
        inv = lax.rsqrt(jnp.sum(a * a, axis=-1, keepdims=True) + NORM_EPS)
        scale = jnp.where(j < N_HEADS, D_HEAD ** -0.5, 1.0)
        ds = dn * scale
        da_norm = inv * ds - a * (inv * inv * inv) * jnp.sum(ds * a, axis=-1, keepdims=True)
        normed = jnp.where(j < 2 * N_HEADS, 1.0, 0.0)
        da = normed * da_norm + (1.0 - normed) * dn
        s = _sigmoid(y)
        dy = da * (s * (1.0 + y * (1.0 - s)))
        dx = dy * w[3:4, :]
        dw_ref[3:4, :] = jnp.sum(dy * xv, axis=0, keepdims=True)
        for sft in (1, 2, 3):
            xs = jnp.where(row >= sft, pltpu.roll(xv, sft, 0), 0.0)
            dw_ref[3 - sft:4 - sft, :] = jnp.sum(dy * xs, axis=0, keepdims=True)
            dys = jnp.where(row < t - sft, pltpu.roll(dy, t - sft, 0), 0.0)
            dx = dx + dys * w[3 - sft:4 - sft, :]
        dp_ref[...] = dx.astype(BF16)

    blk = pl.BlockSpec((t, D_HEAD), lambda j: (0, j))
    wblk = pl.BlockSpec((4, D_HEAD), lambda j: (0, j))

    def grad(part):
        return pl.BlockSpec((t, D_HEAD), lambda j: (0, jnp.clip(j - part * N_HEADS, 0, N_HEADS - 1)))

    return pl.pallas_call(
        body, name="dn_prep_bwd", grid=(3 * N_HEADS,), in_specs=[blk, wblk, grad(0), grad(1), grad(2)],
        out_specs=[blk, wblk],
        out_shape=[jax.ShapeDtypeStruct((t, 3 * D_MODEL), BF16), jax.ShapeDtypeStruct((4, 3 * D_MODEL), F32)],
        compiler_params=_cp(),
    )(proj, conv_w, dq, dk, dv)


def _softplus_parts(xv):
    e = jnp.exp(-jnp.abs(xv))
    return jnp.maximum(xv, 0.0) + _log1p_small(e)


def _chunk_scan(v, row, reverse):
    t = v.shape[0]
    pos = row & (DN_CHUNK - 1)
    s = 1
    while s < DN_CHUNK:
        if reverse:
            v = v + jnp.where(pos < DN_CHUNK - s, pltpu.roll(v, t - s, 0), 0.0)
        else:
            v = v + jnp.where(pos >= s, pltpu.roll(v, s, 0), 0.0)
        s *= 2
    return v


def _dn_gate_fwd(proj, alog_row, dtb_row):
    t = proj.shape[0]

    def body(p_ref, al_ref, dt_ref, b_ref, g_ref):
        p = p_ref[...]
        row = lax.broadcasted_iota(jnp.int32, p.shape, 0)
        b_ref[...] = _sigmoid(p)
        g = -jnp.exp(al_ref[...]) * _softplus_parts(p + dt_ref[...])
        g_ref[...] = _chunk_scan(g, row, reverse=False)

    blk = pl.BlockSpec((t, 128), lambda i: (0, C_BA // 128))
    vec = pl.BlockSpec((1, 128), lambda i: (0, 0))
    out = pl.BlockSpec((t, 128), lambda i: (0, 0))
    return pl.pallas_call(
        body, name="dn_gate_fwd", grid=(1,), in_specs=[blk, vec, vec], out_specs=[out, out],
        out_shape=[jax.ShapeDtypeStruct((t, 128), F32)] * 2, compiler_params=_cp(),
    )(proj, alog_row, dtb_row)


def _dn_gate_bwd(proj, alog_row, dtb_row, dbeta, dgc):
    t = proj.shape[0]

    def body(p_ref, al_ref, dt_ref, db_ref, dg_ref, dp_ref, dal_ref, ddt_ref):
        p = p_ref[...]
        row = lax.broadcasted_iota(jnp.int32, p.shape, 0)
        lane = lax.broadcasted_iota(jnp.int32, p.shape, 1)
        s = _sigmoid(p)
        d_b = db_ref[...] * s * (1.0 - s)
        dg = _chunk_scan(dg_ref[...], row, reverse=True)
        xa = p + dt_ref[...]
        ea = jnp.exp(al_ref[...])
        g = -ea * _softplus_parts(xa)
        d_a = dg * (-ea) * _sigmoid(xa)
        dp_ref[...] = jnp.where(lane < N_HEADS, d_b, jnp.where(lane < 2 * N_HEADS, d_a, 0.0)).astype(BF16)
        dal_ref[...] = jnp.sum(dg * g, axis=0, keepdims=True)
        ddt_ref[...] = jnp.sum(d_a, axis=0, keepdims=True)

    blk = pl.BlockSpec((t, 128), lambda i: (0, C_BA // 128))
    vec = pl.BlockSpec((1, 128), lambda i: (0, 0))
    full = pl.BlockSpec((t, 128), lambda i: (0, 0))
    return pl.pallas_call(
        body, name="dn_gate_bwd", grid=(1,), in_specs=[blk, vec, vec, full, full], out_specs=[full, vec, vec],
        out_shape=[jax.ShapeDtypeStruct((t, 128), BF16), jax.ShapeDtypeStruct((1, 128), F32),
                   jax.ShapeDtypeStruct((1, 128), F32)], compiler_params=_cp(),
    )(proj, alog_row, dtb_row, dbeta, dgc)


def _col_to_row(col, eye):
    return jnp.sum(jnp.where(eye, col, 0.0), axis=-2, keepdims=True)


def _row_to_col(rowv, eye):
    return jnp.sum(jnp.where(eye, rowv, 0.0), axis=-1, keepdims=True)


def _tri_inverse(m, ri, ci):
    eye = (ri == ci).astype(F32)
    b16 = (ri >> 4) == (ci >> 4)
    b32 = (ri >> 5) == (ci >> 5)
    m1 = jnp.where(b16, m, 0.0)
    x = eye - m1
    p = _dot3(m1, m1, "nn")
    x = x + _dot3(x, p, "nn")
    p = _dot3(p, p, "nn")
    x = x + _dot3(x, p, "nn")
    p = _dot3(p, p, "nn")
    x = x + _dot3(x, p, "nn")
    c1 = jnp.where(jnp.logical_and(b32, jnp.logical_not(b16)), m, 0.0)
    x = x - _dot3(_dot3(x, c1, "nn"), x, "nn")
    c2 = jnp.where(b32, 0.0, m)
    x = x - _dot3(_dot3(x, c2, "nn"), x, "nn")
    return x


def _dn_chunk_common(q, k, gc, ri, ci):
    eye = ri == ci
    g_row = _col_to_row(gc, eye)
    diff = jnp.minimum(gc - g_row, 0.0)
    gam = jnp.where(ri >= ci, jnp.exp(diff), 0.0)
    kk = _bdot(k, k, "nt")
    qk = _bdot(q, k, "nt")
    rcol = lax.broadcasted_iota(jnp.int32, gc.shape, gc.ndim - 2)
    last = jnp.sum(jnp.where(rcol == DN_CHUNK - 1, gc, 0.0), axis=-2, keepdims=True)
    e_g = jnp.exp(gc)
    dec = jnp.exp(last - gc)
    return eye, gam, kk, qk, last, e_g, dec, rcol


def _dn_specs(t, rows_blk):
    def head(off):
        return pl.BlockSpec((rows_blk, D_HEAD), lambda g, h: (g, off + h))

    lanes = pl.BlockSpec((rows_blk, 128), lambda g, h: (g, 0))
    hm = pl.BlockSpec((None, rows_blk, D_HEAD), lambda g, h: (h, g, 0))
    sq = pl.BlockSpec((1, rows_blk, DN_CHUNK), lambda g, h: (h, g, 0))
    tile = pl.BlockSpec((1, rows_blk // DN_CHUNK, 8, 128), lambda g, h: (h, g, 0, 0))
    return head, lanes, hm, sq, tile


def _head_column(slab, lane_idx):
    lane = lax.broadcasted_iota(jnp.int32, slab.shape, 1)
    return _chunks(jnp.sum(jnp.where(lane == lane_idx, slab, 0.0), axis=1, keepdims=True))


def _dn_intra_fwd(qkv, beta_t, g_t):
    t = qkv.shape[0]
    n_chunks = t // DN_CHUNK
    rows_blk = min(DN_GROUP * DN_CHUNK, t)

    def body(q_ref, k_ref, v_ref, b_ref, g_ref, u_ref, w_ref, qd_ref, kd_ref, a_ref, ti_ref, el_ref):
        ri = lax.broadcasted_iota(jnp.int32, (DN_CHUNK, DN_CHUNK), 0)
        ci = lax.broadcasted_iota(jnp.int32, (DN_CHUNK, DN_CHUNK), 1)
        h = pl.program_id(1)
        q, k, v = (_chunks(r[...]) for r in (q_ref, k_ref, v_ref))
        b, gc = _head_column(b_ref[...], h), _head_column(g_ref[...], h + N_HEADS)
        _, gam, kk, qk, last, e_g, dec, _ = _dn_chunk_common(q, k, gc, ri, ci)
        tinv = _tri_inverse(jnp.where(ri > ci, b * kk * gam, 0.0), ri, ci)
        u_ref[...] = _unchunk(_bdot(tinv, v * b, "nn"))
        w_ref[...] = _unchunk(_bdot(tinv, k * (b * e_g), "nn"))
        qd_ref[...] = _unchunk(q * e_g)
        kd_ref[...] = _unchunk(k * dec)
        a_ref[0] = _unchunk(qk * gam)
        ti_ref[0] = _unchunk(tinv)
        el_ref[0] = jnp.broadcast_to(jnp.exp(last), (rows_blk // DN_CHUNK, 8, 128))

    head, lanes, hm, sq, tile = _dn_specs(t, rows_blk)
    act = jax.ShapeDtypeStruct((N_HEADS, t, D_HEAD), F32)
    sqs = jax.ShapeDtypeStruct((N_HEADS, t, DN_CHUNK), F32)
    return pl.pallas_call(
        body, name="dn_intra_fwd", grid=(t // rows_blk, N_HEADS),
        in_specs=[head(0), head(N_HEADS), head(2 * N_HEADS), lanes, lanes],
        out_specs=[hm] * 4 + [sq, sq, tile],
        out_shape=[act] * 4 + [sqs, sqs, jax.ShapeDtypeStruct((N_HEADS, n_chunks, 8, 128), F32)],
        compiler_params=_cp(),
    )(qkv, qkv, qkv, beta_t, g_t)


def _dn_scan_specs(t, rows_blk, reverse):
    n_groups = t // rows_blk

    def at(g):
        return n_groups - 1 - g if reverse else g

    per = rows_blk // DN_CHUNK
    act = pl.BlockSpec((N_HEADS, rows_blk, D_HEAD), lambda g: (0, at(g), 0))
    sq = pl.BlockSpec((N_HEADS, rows_blk, DN_CHUNK), lambda g: (0, at(g), 0))
    state = pl.BlockSpec((N_HEADS, per, D_HEAD, D_HEAD), lambda g: (0, at(g), 0, 0))
    tile = pl.BlockSpec((N_HEADS, per, 8, 128), lambda g: (0, at(g), 0, 0))
    return act, sq, state, tile


def _dn_scan_fwd(u, w, qd, kd, a, el):
    t = u.shape[1]
    n_chunks = t // DN_CHUNK
    rows_blk = DN_SCAN_GROUP * DN_CHUNK

    def body(u_ref, w_ref, qd_ref, kd_ref, a_ref, el_ref, o_ref, vn_ref, s_ref, s_scr):
        @pl.when(pl.program_id(0) == 0)
        def _():
            s_scr[...] = jnp.zeros_like(s_scr)

        for cc in range(DN_SCAN_GROUP):
            rows = slice(cc * DN_CHUNK, (cc + 1) * DN_CHUNK)
            s = s_scr[...]
            s_ref[:, cc] = s
            v_new = u_ref[:, rows, :] - _bdot(w_ref[:, rows, :], s, "nn")
            vn_ref[:, rows, :] = v_new
            o_ref[:, rows, :] = _bdot(qd_ref[:, rows, :], s, "nn") + _bdot(a_ref[:, rows, :], v_new, "nn")
            s_scr[...] = s * el_ref[:, cc][:, 0:1, :] + _bdot(kd_ref[:, rows, :], v_new, "tn")

    act, sq, state, tile = _dn_scan_specs(t, rows_blk, reverse=False)
    shp = jax.ShapeDtypeStruct((N_HEADS, t, D_HEAD), F32)
    return pl.pallas_call(
        body, name="dn_scan_fwd", grid=(t // rows_blk,),
        in_specs=[act, act, act, act, sq, tile], out_specs=[act, act, state],
        out_shape=[shp, shp, jax.ShapeDtypeStruct((N_HEADS, n_chunks, D_HEAD, D_HEAD), F32)],
        scratch_shapes=[pltpu.VMEM((N_HEADS, D_HEAD, D_HEAD), F32)],
        compiler_params=_cp(dimension_semantics=("arbitrary",)),
    )(u, w, qd, kd, a, el)


def _dn_scan_bwd(w, qd, kd, a, el, vn, s_all, do):
    t = w.shape[1]
    n_chunks = t // DN_CHUNK
    rows_blk = DN_SCAN_GROUP * DN_CHUNK

    def body(w_ref, qd_ref, kd_ref, a_ref, el_ref, vn_ref, s_ref, do_ref, dvn_ref, dkd_ref, dqd_ref, dw_ref, dl_ref, ds_scr):
        @pl.when(pl.program_id(0) == 0)
        def _():
            ds_scr[...] = jnp.zeros_like(ds_scr)

        for cc in reversed(range(DN_SCAN_GROUP)):
            rows = slice(cc * DN_CHUNK, (cc + 1) * DN_CHUNK)
            s = s_ref[:, cc]
            d_s = ds_scr[...]
            e_last = el_ref[:, cc][:, 0:1, :]
            d_o = do_ref[:, rows, :]
            dv_new = _bdot(a_ref[:, rows, :], d_o, "tn") + _bdot(kd_ref[:, rows, :], d_s, "nn")
            ds_scr[...] = d_s * e_last + _bdot(qd_ref[:, rows, :], d_o, "tn") - _bdot(w_ref[:, rows, :], dv_new, "tn")
            dvn_ref[:, rows, :] = dv_new
            dkd_ref[:, rows, :] = _bdot(vn_ref[:, rows, :], d_s, "nt")
            dqd_ref[:, rows, :] = _bdot(d_o, s, "nt")
            dw_ref[:, rows, :] = -_bdot(dv_new, s, "nt")
            dlast = jnp.sum(jnp.sum(d_s * s, axis=2, keepdims=True), axis=1, keepdims=True)
            dl_ref[:, cc] = jnp.broadcast_to(dlast * e_last, (N_HEADS, 8, 128))

    act, sq, state, tile = _dn_scan_specs(t, rows_blk, reverse=True)
    shp = jax.ShapeDtypeStruct((N_HEADS, t, D_HEAD), F32)
    return pl.pallas_call(
        body, name="dn_scan_bwd", grid=(t // rows_blk,),
        in_specs=[act, act, act, sq, tile, act, state, act], out_specs=[act] * 4 + [tile],
        out_shape=[shp] * 4 + [jax.ShapeDtypeStruct((N_HEADS, n_chunks, 8, 128), F32)],
        scratch_shapes=[pltpu.VMEM((N_HEADS, D_HEAD, D_HEAD), F32)],
        compiler_params=_cp(dimension_semantics=("arbitrary",)),
    )(w, qd, kd, a, el, vn, s_all, do)


def _dn_intra_bwd(qkv, beta_t, g_t, tinv_all, vn, do, dvn, dkd, dqd, dw, dl):
    t = qkv.shape[0]
    rows_blk = min(DN_GROUP * DN_CHUNK, t)

    def body(q_ref, k_ref, v_ref, b_ref, g_ref, ti_ref, vn_ref, do_ref, dvn_ref, dkd_ref, dqd_ref, dw_ref, dl_ref,
             dq_ref, dk_ref, dv_ref, db_ref, dg_ref):
        ri = lax.broadcasted_iota(jnp.int32, (DN_CHUNK, DN_CHUNK), 0)
        ci = lax.broadcasted_iota(jnp.int32, (DN_CHUNK, DN_CHUNK), 1)
        h = pl.program_id(1)
        q, k, v = (_chunks(r[...]) for r in (q_ref, k_ref, v_ref))
        b, gc = _head_column(b_ref[...], h), _head_column(g_ref[...], h + N_HEADS)
        tinv = _chunks(ti_ref[0])
        dv_new, dk_dec, dq_dec, d_w = (_chunks(r[...]) for r in (dvn_ref, dkd_ref, dqd_ref, dw_ref))
        eye, gam, kk, qk, _, e_g, dec, rcol = _dn_chunk_common(q, k, gc, ri, ci)
        bv = v * b
        bk = k * (b * e_g)

        d_a = jnp.where(ri >= ci, _bdot(_chunks(do_ref[...]), _chunks(vn_ref[...]), "nt"), 0.0)
        dbv = _bdot(tinv, dv_new, "tn")
        dbk = _bdot(tinv, d_w, "tn")
        d_tinv = _bdot(dv_new, bv, "nt") + _bdot(d_w, bk, "nt")
        d_m = -jnp.where(ri > ci, _dot3(_dot3(tinv, d_tinv, "tn"), tinv, "nt"), 0.0)

        d_kk = d_m * b * gam
        d_gam = d_m * b * kk + d_a * qk
        d_qk = d_a * gam
        dq_ref[...] = _unchunk(_bdot(d_qk, k, "nn") + dq_dec * e_g)
        dk_ref[...] = _unchunk(_bdot(d_qk, q, "tn") + _bdot(d_kk, k, "nn") + _bdot(d_kk, k, "tn")
                               + dk_dec * dec + dbk * (b * e_g))
        dv_ref[...] = _unchunk(dbv * b)
        d_b = _unchunk(jnp.sum(d_m * kk * gam, axis=-1, keepdims=True) + jnp.sum(dbv * v, axis=-1, keepdims=True)
                       + jnp.sum(dbk * k, axis=-1, keepdims=True) * e_g)

        xg = d_gam * gam
        kdk = jnp.sum(dk_dec * (k * dec), axis=-1, keepdims=True)
        d_gc = (jnp.sum(xg, axis=-1, keepdims=True) - _row_to_col(jnp.sum(xg, axis=-2, keepdims=True), eye)
                + jnp.sum(dq_dec * (q * e_g), axis=-1, keepdims=True) - kdk
                + jnp.sum(dbk * bk, axis=-1, keepdims=True))
        d_last_total = dl_ref[0][:, 0:1, 0:1] + jnp.sum(kdk, axis=-2, keepdims=True)
        d_g = _unchunk(d_gc + jnp.where(rcol == DN_CHUNK - 1, d_last_total, 0.0))

        @pl.when(h == 0)
        def _():
            db_ref[...] = jnp.zeros_like(db_ref)
            dg_ref[...] = jnp.zeros_like(dg_ref)

        lane = lax.broadcasted_iota(jnp.int32, db_ref.shape, 1)
        db_ref[...] += jnp.where(lane == h, d_b, 0.0)
        dg_ref[...] += jnp.where(lane == h + N_HEADS, d_g, 0.0)

    head, lanes, hm, sq, tile = _dn_specs(t, rows_blk)
    return pl.pallas_call(
        body, name="dn_intra_bwd", grid=(t // rows_blk, N_HEADS),
        in_specs=[head(0), head(N_HEADS), head(2 * N_HEADS), lanes, lanes, sq] + [hm] * 6 + [tile],
        out_specs=[head(0), head(0), head(0), lanes, lanes],
        out_shape=[jax.ShapeDtypeStruct((t, D_MODEL), F32)] * 3 + [jax.ShapeDtypeStruct((t, 128), F32)] * 2,
        compiler_params=_cp(),
    )(qkv, qkv, qkv, beta_t, g_t, tinv_all, vn, do, dvn, dkd, dqd, dw, dl)


def _dn_post_fwd(o, proj, gn):
    t = o.shape[1]

    def body(o_ref, z_ref, g_ref, out_ref):
        ov, z = o_ref[...], z_ref[...]
        r = lax.rsqrt(jnp.mean(ov * ov, axis=-1, keepdims=True) + NORM_EPS)
        out_ref[...] = (((ov * r) * g_ref[...]) * (z * _sigmoid(z))).astype(BF16)

    blk = pl.BlockSpec((t, D_HEAD), lambda h: (0, h))
    return pl.pallas_call(
        body, name="dn_post_fwd", grid=(N_HEADS,),
        in_specs=[pl.BlockSpec((None, t, D_HEAD), lambda h: (h, 0, 0)),
                  pl.BlockSpec((t, D_HEAD), lambda h: (0, C_DNZ // D_HEAD + h)),
                  pl.BlockSpec((1, D_HEAD), lambda h: (0, 0))],
        out_specs=blk, out_shape=jax.ShapeDtypeStruct((t, D_MODEL), BF16), compiler_params=_cp(),
    )(o, proj, gn)


def _dn_post_bwd(o, proj, gn, dout):
    t = o.shape[1]

    def body(o_ref, z_ref, g_ref, d_ref, do_ref, dz_ref, dg_ref):
        @pl.when(pl.program_id(0) == 0)
        def _():
            dg_ref[...] = jnp.zeros_like(dg_ref)

        ov, z, d = o_ref[...], z_ref[...], d_ref[...]
        r = lax.rsqrt(jnp.mean(ov * ov, axis=-1, keepdims=True) + NORM_EPS)
        ohat = ov * r
        s = _sigmoid(z)
        d_on = d * (z * s)
        dz_ref[...] = (d * (ohat * g_ref[...]) * (s * (1.0 + z * (1.0 - s)))).astype(BF16)
        dg_ref[...] += jnp.sum(d_on * ohat, axis=0, keepdims=True)
        dxh = d_on * g_ref[...]
        do_ref[...] = r * (dxh - ohat * jnp.mean(dxh * ohat, axis=-1, keepdims=True))

    blk = pl.BlockSpec((t, D_HEAD), lambda h: (0, h))
    hm = pl.BlockSpec((None, t, D_HEAD), lambda h: (h, 0, 0))
    vec = pl.BlockSpec((1, D_HEAD), lambda h: (0, 0))
    return pl.pallas_call(
        body, name="dn_post_bwd", grid=(N_HEADS,),
        in_specs=[hm, pl.BlockSpec((t, D_HEAD), lambda h: (0, C_DNZ // D_HEAD + h)), vec, blk],
        out_specs=[hm, blk, vec],
        out_shape=[jax.ShapeDtypeStruct((N_HEADS, t, D_HEAD), F32), jax.ShapeDtypeStruct((t, D_MODEL), BF16),
                   jax.ShapeDtypeStruct((1, D_HEAD), F32)], compiler_params=_cp(),
    )(o, proj, gn, dout)


def _sb_fwd(proj):
    t = proj.shape[0]
    qblk = min(SB_QBLOCK, t)
    scale = 1.0 / math.sqrt(D_HEAD)

    hp = SB_HEADS_PER_STEP
    wid = hp * D_HEAD

    def body(q_ref, k_ref, v_ref, z_ref, o_ref, og_ref, l_ref, qb, kb, vb):
        for hh in range(hp):
            hs = slice(hh * D_HEAD, (hh + 1) * D_HEAD)
            qb[hh] = q_ref[:, hs].astype(BF16)
            kb[hh] = k_ref[:, hs].astype(BF16)
            vb[hh] = v_ref[:, hs].astype(BF16)
        ri = lax.broadcasted_iota(jnp.int32, (qblk, SB_BLOCK), 0)
        ci = lax.broadcasted_iota(jnp.int32, (qblk, SB_BLOCK), 1)
        r2 = lax.broadcasted_iota(jnp.int32, (SB_BLOCK, SB_BLOCK), 0)
        c2 = lax.broadcasted_iota(jnp.int32, (SB_BLOCK, SB_BLOCK), 1)
        upper = (r2 > c2).astype(BF16)
        nkb = qblk // SB_BLOCK

        def qblock(i, carry):
            rows = pl.ds(pl.multiple_of(i * qblk, qblk), qblk)
            qi = qb[:, rows, :]

            def tile(j, st, on_diagonal):
                acc, c = st
                cols = pl.ds(pl.multiple_of(j * SB_BLOCK, SB_BLOCK), SB_BLOCK)
                z = _dot(qi, kb[:, cols, :], "nt") * scale
                lb = jnp.minimum(z, 0.0) - jnp.log(1.0 + jnp.exp(-jnp.abs(z)))
                lf = lb - z
                if on_diagonal:
                    mask = (j * SB_BLOCK + ci) < (i * qblk + ri)
                    lf = jnp.where(mask, lf, 0.0)
                att = jnp.exp(lb + (_ones_dot(lf, upper) + c))
                if on_diagonal:
                    att = jnp.where(mask, att, 0.0)
                acc = acc + _dot(att.astype(BF16), vb[:, cols, :], "nn")
                return acc, c + jnp.sum(lf, axis=-1, keepdims=True)

            st = (jnp.zeros((hp, qblk, D_HEAD), F32), jnp.zeros((hp, qblk, 1), F32))
            for d in range(nkb):
                st = tile((i + 1) * nkb - 1 - d, st, True)
            acc, c = lax.fori_loop(0, i * nkb, lambda jj, s: tile(i * nkb - 1 - jj, s, False), st)
            l_ref[:, rows, :] = c
            for hh in range(hp):
                hs = slice(hh * D_HEAD, (hh + 1) * D_HEAD)
                zg = z_ref[rows, hs]
                o_ref[rows, hs] = acc[hh]
                og_ref[rows, hs] = (acc[hh] * (zg * _sigmoid(zg))).astype(BF16)
            return carry

        lax.fori_loop(0, t // qblk, qblock, 0)

    def head(off):
        return pl.BlockSpec((t, wid), lambda h: (0, off // wid + h))

    out = pl.BlockSpec((t, wid), lambda h: (0, h))
    return pl.pallas_call(
        body, name="sb_fwd", grid=(N_HEADS // hp,),
        in_specs=[head(C_SBQ), head(C_SBQ + D_MODEL), head(C_SBQ + 2 * D_MODEL), head(C_SBZ)],
        out_specs=[out, out, pl.BlockSpec((hp, t, 1), lambda h: (h, 0, 0))],
        out_shape=[jax.ShapeDtypeStruct((t, D_MODEL), F32), jax.ShapeDtypeStruct((t, D_MODEL), BF16),
                   jax.ShapeDtypeStruct((N_HEADS, t, 1), F32)],
        scratch_shapes=[pltpu.VMEM((hp, t, D_HEAD), BF16)] * 3, compiler_params=_cp(),
    )(proj, proj, proj, proj)


def _sb_bwd(proj, o, ltot, dog, after=None):
    t = proj.shape[0]
    qblk = min(SB_QBLOCK, t)
    scale = 1.0 / math.sqrt(D_HEAD)

    hp = SB_HEADS_PER_STEP
    wid = hp * D_HEAD

    def body(q_ref, k_ref, v_ref, z_ref, o_ref, l_ref, d_ref, *rest):
        dq_ref, dk_ref, dv_ref, dz_ref, qb, kb, vb, dob, dk_scr, dv_scr = rest[-10:]
        for hh in range(hp):
            hs = slice(hh * D_HEAD, (hh + 1) * D_HEAD)
            qb[hh] = q_ref[:, hs].astype(BF16)
            kb[hh] = k_ref[:, hs].astype(BF16)
            vb[hh] = v_ref[:, hs].astype(BF16)
            zg = z_ref[:, hs]
            sg = _sigmoid(zg)
            dgo = d_ref[:, hs]
            dob[hh] = (dgo * (zg * sg)).astype(BF16)
            dz_ref[:, hs] = (dgo * o_ref[:, hs] * (sg * (1.0 + zg * (1.0 - sg)))).astype(BF16)
        dk_scr[...] = jnp.zeros_like(dk_scr)
        dv_scr[...] = jnp.zeros_like(dv_scr)
        ri = lax.broadcasted_iota(jnp.int32, (qblk, SB_BLOCK), 0)
        ci = lax.broadcasted_iota(jnp.int32, (qblk, SB_BLOCK), 1)
        r2 = lax.broadcasted_iota(jnp.int32, (SB_BLOCK, SB_BLOCK), 0)
        c2 = lax.broadcasted_iota(jnp.int32, (SB_BLOCK, SB_BLOCK), 1)
        upper = (r2 > c2).astype(BF16)
        below = (r2 < c2).astype(BF16)

        def qblock(i, carry):
            rows = pl.ds(pl.multiple_of(i * qblk, qblk), qblk)
            qi = qb[:, rows, :]
            d_o = dob[:, rows, :]
            ltot = l_ref[:, rows, :]

            def tile(j, st, on_diagonal):
                dq, cpre, ce = st
                cols = pl.ds(pl.multiple_of(j * SB_BLOCK, SB_BLOCK), SB_BLOCK)
                kj, vj = kb[:, cols, :], vb[:, cols, :]
                z = _dot(qi, kj, "nt") * scale
                lb = jnp.minimum(z, 0.0) - jnp.log(1.0 + jnp.exp(-jnp.abs(z)))
                lf = lb - z
                if on_diagonal:
                    mask = (j * SB_BLOCK + ci) < (i * qblk + ri)
                    lf = jnp.where(mask, lf, 0.0)
                tile_sum = jnp.sum(lf, axis=-1, keepdims=True)
                att = jnp.exp(lb + ((ltot - cpre - tile_sum) + _ones_dot(lf, upper)))
                if on_diagonal:
                    att = jnp.where(mask, att, 0.0)
                e = _dot(d_o, vj, "nt") * att
                dlf = ce + _ones_dot(e, below)
                dzz = e - (e + dlf) * jnp.exp(lb)
                if on_diagonal:
                    dzz = jnp.where(mask, dzz, 0.0)
                dzz = dzz.astype(BF16)
                dq = dq + _dot(dzz, kj, "nn")
                dk_scr[:, cols, :] += _dot(dzz, qi, "tn")
                dv_scr[:, cols, :] += _dot(att.astype(BF16), d_o, "tn")
                return dq, cpre + tile_sum, ce + jnp.sum(e, axis=-1, keepdims=True)

            nkb = qblk // SB_BLOCK
            zero_col = jnp.zeros((hp, qblk, 1), F32)
            st = lax.fori_loop(0, i * nkb, lambda j, s: tile(j, s, False),
                               (jnp.zeros((hp, qblk, D_HEAD), F32), zero_col, zero_col))
            for d in range(nkb):
                st = tile(i * nkb + d, st, True)
            dq = st[0]
            for hh in range(hp):
                dq_ref[rows, hh * D_HEAD:(hh + 1) * D_HEAD] = (dq[hh] * scale).astype(BF16)
            return carry

        lax.fori_loop(0, t // qblk, qblock, 0)
        for hh in range(hp):
            hs = slice(hh * D_HEAD, (hh + 1) * D_HEAD)
            dk_ref[:, hs] = (dk_scr[hh] * scale).astype(BF16)
            dv_ref[:, hs] = dv_scr[hh].astype(BF16)

    def head(off):
        return pl.BlockSpec((t, wid), lambda h: (0, off // wid + h))

    extra_specs, extra = [], []
    if after is not None:
        extra_specs, extra = [pl.BlockSpec(after.shape, lambda h: (0, 0))], [after]
    return pl.pallas_call(
        body, name="sb_bwd", grid=(N_HEADS // hp,),
        in_specs=[head(C_SBQ), head(C_SBQ + D_MODEL), head(C_SBQ + 2 * D_MODEL), head(C_SBZ), head(0),
                  pl.BlockSpec((hp, t, 1), lambda h: (h, 0, 0)), head(0)] + extra_specs,
        out_specs=[head(0)] * 4, out_shape=[jax.ShapeDtypeStruct((t, D_MODEL), BF16)] * 4,
        scratch_shapes=[pltpu.VMEM((hp, t, D_HEAD), BF16)] * 4 + [pltpu.VMEM((hp, t, D_HEAD), F32)] * 2,
        compiler_params=_cp(),
    )(proj, proj, proj, proj, o, ltot, dog, *extra)


def _mem_fwd(proj, mkv):
    t = proj.shape[0]
    tq = _pick(t, (512, 256))
    m_len = mkv.shape[0]
    scale = 1.0 / math.sqrt(MEM_DH)

    def body(q_ref, z_ref, kv_ref, o_ref, og_ref):
        q = q_ref[...]
        mk = kv_ref[:, :MEM_W].astype(BF16)
        mv = kv_ref[:, MEM_W:].astype(BF16)
        lane = lax.broadcasted_iota(jnp.int32, q.shape, 1) >> 6
        o = jnp.zeros(q.shape, F32)
        for h in range(MEM_HEADS):
            s = _bdot(jnp.where(lane == h, q, 0.0), mk, "nt") * scale
            p = jnp.exp(s - jnp.max(s, axis=-1, keepdims=True))
            p = p / jnp.sum(p, axis=-1, keepdims=True)
            o = o + jnp.where(lane == h, _bdot(p, mv, "nn"), 0.0)
        z = z_ref[...]
        o_ref[...] = o
        og_ref[...] = (o * (z * _sigmoid(z))).astype(BF16)

    out = pl.BlockSpec((tq, MEM_W), lambda i: (i, 0))
    return pl.pallas_call(
        body, name="mem_fwd", grid=(t // tq,),
        in_specs=[pl.BlockSpec((tq, MEM_W), lambda i: (i, C_MQ // MEM_W)),
                  pl.BlockSpec((tq, MEM_W), lambda i: (i, C_MZ // MEM_W)),
                  pl.BlockSpec((m_len, 2 * MEM_W), lambda i: (0, 0))],
        out_specs=[out, out],
        out_shape=[jax.ShapeDtypeStruct((t, MEM_W), F32), jax.ShapeDtypeStruct((t, MEM_W), BF16)],
        compiler_params=_cp(),
    )(proj, proj, mkv)


def _mem_bwd(proj, mkv, o, dog):
    t = proj.shape[0]
    tq = _pick(t, (512, 256))
    m_len = mkv.shape[0]
    scale = 1.0 / math.sqrt(MEM_DH)

    def body(q_ref, z_ref, kv_ref, o_ref, d_ref, dq_ref, dz_ref, dkv_ref):
        @pl.when(pl.program_id(0) == 0)
        def _():
            dkv_ref[...] = jnp.zeros_like(dkv_ref)

        q = q_ref[...]
        z = z_ref[...]
        sg = _sigmoid(z)
        dgo = d_ref[...]
        d_o = dgo * (z * sg)
        dz_ref[...] = (dgo * o_ref[...] * (sg * (1.0 + z * (1.0 - sg)))).astype(BF16)
        mk = kv_ref[:, :MEM_W].astype(BF16)
        mv = kv_ref[:, MEM_W:].astype(BF16)
        lane = lax.broadcasted_iota(jnp.int32, q.shape, 1) >> 6
        klane = lax.broadcasted_iota(jnp.int32, (m_len, MEM_W), 1) >> 6
        dq = jnp.zeros(q.shape, F32)
        dmk = jnp.zeros((m_len, MEM_W), F32)
        dmv = jnp.zeros((m_len, MEM_W), F32)
        for h in range(MEM_HEADS):
            qh = jnp.where(lane == h, q, 0.0)
            doh = jnp.where(lane == h, d_o, 0.0)
            s = _bdot(qh, mk, "nt") * scale
            p = jnp.exp(s - jnp.max(s, axis=-1, keepdims=True))
            p = p / jnp.sum(p, axis=-1, keepdims=True)
            dp = _bdot(doh, mv, "nt")
            ds = p * (dp - jnp.sum(dp * p, axis=-1, keepdims=True)) * scale
            dq = dq + jnp.where(lane == h, _bdot(ds, mk, "nn"), 0.0)
            dmk = dmk + jnp.where(klane == h, _bdot(ds, qh, "tn"), 0.0)
            dmv = dmv + jnp.where(klane == h, _bdot(p, doh, "tn"), 0.0)
        dq_ref[...] = dq.astype(BF16)
        dkv_ref[:, :MEM_W] += dmk
        dkv_ref[:, MEM_W:] += dmv

    blk = pl.BlockSpec((tq, MEM_W), lambda i: (i, 0))
    kv = pl.BlockSpec((m_len, 2 * MEM_W), lambda i: (0, 0))
    return pl.pallas_call(
        body, name="mem_bwd", grid=(t // tq,),
        in_specs=[pl.BlockSpec((tq, MEM_W), lambda i: (i, C_MQ // MEM_W)),
                  pl.BlockSpec((tq, MEM_W), lambda i: (i, C_MZ // MEM_W)), kv, blk, blk],
        out_specs=[blk, blk, kv],
        out_shape=[jax.ShapeDtypeStruct((t, MEM_W), BF16), jax.ShapeDtypeStruct((t, MEM_W), BF16),
                   jax.ShapeDtypeStruct((m_len, 2 * MEM_W), F32)], compiler_params=_cp(),
    )(proj, proj, mkv, o, dog)


_GW = 512


def _merge_fwd(proj, y_dn, y_sb, y_m):
    t = proj.shape[0]
    tb = _pick(t, (256,))
    nc = D_MODEL // _GW

    def body(g1, g2, g3, y1, y2, y3, out_ref):
        out_ref[...] = (_sigmoid(g1[...]) * y1[...] + _sigmoid(g2[...]) * y2[...] + _sigmoid(g3[...]) * y3[...]).astype(BF16)

    def gate(kb):
        return pl.BlockSpec((tb, _GW), lambda i, c: (i, C_GATES // _GW + kb * nc + c))

    blk = pl.BlockSpec((tb, _GW), lambda i, c: (i, c))
    return pl.pallas_call(
        body, name="merge_fwd", grid=(t // tb, nc), in_specs=[gate(0), gate(1), gate(2), blk, blk, blk],
        out_specs=blk, out_shape=jax.ShapeDtypeStruct((t, D_MODEL), BF16), compiler_params=_cp(),
    )(proj, proj, proj, y_dn, y_sb, y_m)


def _merge_bwd(proj, y_dn, y_sb, y_m, dm):
    t = proj.shape[0]
    tb = _pick(t, (256,))
    nc = D_MODEL // _GW

    def body(g1, g2, g3, y1, y2, y3, dm_ref, d1, d2, d3, dg1, dg2, dg3):
        d = dm_ref[...]
        for g, y, dy, dg in ((g1, y1, d1, dg1), (g2, y2, d2, dg2), (g3, y3, d3, dg3)):
            s = _sigmoid(g[...])
            dy[...] = (d * s).astype(BF16)
            dg[...] = (d * y[...] * (s * (1.0 - s))).astype(BF16)

    def gate(kb):
        return pl.BlockSpec((tb, _GW), lambda i, c: (i, C_GATES // _GW + kb * nc + c))

    blk = pl.BlockSpec((tb, _GW), lambda i, c: (i, c))
    act = jax.ShapeDtypeStruct((t, D_MODEL), BF16)
    return pl.pallas_call(
        body, name="merge_bwd", grid=(t // tb, nc), in_specs=[gate(0), gate(1), gate(2), blk, blk, blk, blk],
        out_specs=[blk] * 6, out_shape=[act] * 6, compiler_params=_cp(),
    )(proj, proj, proj, y_dn, y_sb, y_m, dm)


def _final_loss(x, mo, g, tgt):
    t, d = x.shape
    tb = _pick(t, (256,))

    def body(x_ref, mo_ref, g_ref, t_ref, do_ref, dob_ref, loss_ref, dg_ref):
        @pl.when(pl.program_id(0) == 0)
        def _():
            loss_ref[...] = jnp.zeros_like(loss_ref)
            dg_ref[...] = jnp.zeros_like(dg_ref)

        out = x_ref[...] + mo_ref[...]
        r = lax.rsqrt(jnp.mean(out * out, axis=-1, keepdims=True) + NORM_EPS)
        xhat = out * r
        gv = g_ref[...]
        err = xhat * gv - t_ref[...]
        per_tok = jnp.mean(err * err, axis=-1, keepdims=True)
        loss_ref[...] += 0.5 * jnp.sum(per_tok, axis=0, keepdims=True)
        dy = err * (1.0 / d)
        dg_ref[...] += jnp.sum(dy * xhat, axis=0, keepdims=True)
        dxh = dy * gv
        dout = r * (dxh - xhat * jnp.mean(dxh * xhat, axis=-1, keepdims=True))
        do_ref[...] = dout
        dob_ref[...] = dout.astype(BF16)

    row = pl.BlockSpec((tb, d), lambda i: (i, 0))
    vec = pl.BlockSpec((1, d), lambda i: (0, 0))
    return pl.pallas_call(
        body, name="final_loss", grid=(t // tb,), in_specs=[row, row, vec, row],
        out_specs=[row, row, pl.BlockSpec((1, 128), lambda i: (0, 0)), vec],
        out_shape=[jax.ShapeDtypeStruct((t, d), F32), jax.ShapeDtypeStruct((t, d), BF16),
                   jax.ShapeDtypeStruct((1, 128), F32), jax.ShapeDtypeStruct((1, d), F32)],
        compiler_params=_cp(),
    )(x, mo, g, tgt)


def _cast_bf16(a, name):
    r, c = a.shape
    tb = _pick(r, (128, 496, 240))

    def body(a_ref, o_ref):
        o_ref[...] = a_ref[...].astype(BF16)

    blk = pl.BlockSpec((tb, c), lambda i: (i, 0))
    return pl.pallas_call(body, name=name, grid=(r // tb,), in_specs=[blk], out_specs=blk,
                          out_shape=jax.ShapeDtypeStruct((r, c), BF16), compiler_params=_cp())(a)


WIN_START = (0, 23, 45, 68)
_S1_LO, _S1_HI = 1148, 1164
_S1_BA_POS = SHARD_PAD - 128


def _to_window(x, s):
    if s == 0:
        return x
    if s in (2, 3):
        return pltpu.roll(x, 120 if s == 2 else 124, 1)
    pos = lax.broadcasted_iota(jnp.int32, x.shape, 1)
    head = pltpu.roll(x, 4, 1)
    tail = pltpu.roll(x, SHARD_PAD - 12, 1)
    ba = jnp.where(pos < _S1_BA_POS + (_S1_HI - _S1_LO), pltpu.roll(x, _S1_BA_POS - _S1_LO, 1), 0.0)
    return jnp.where(pos < _S1_LO + 4, head, jnp.where(pos < _S1_BA_POS, tail, ba))


def _from_window(g, s):
    if s == 0:
        return g
    if s in (2, 3):
        return pltpu.roll(g, SHARD_PAD - (120 if s == 2 else 124), 1)
    col = lax.broadcasted_iota(jnp.int32, g.shape, 1)
    head = pltpu.roll(g, SHARD_PAD - 4, 1)
    tail = pltpu.roll(g, 12, 1)
    ba = pltpu.roll(g, SHARD_PAD - (_S1_BA_POS - _S1_LO), 1)
    return jnp.where(col < _S1_LO, head, jnp.where(col < _S1_HI, ba, tail))


def _cast_to_window(w, shard, name):
    r, c = w.shape
    tb = _pick(r, (128,))

    def body(s_ref, w_ref, o_ref, pad_scr):
        pad_scr[...] = jnp.zeros_like(pad_scr)
        pad_scr[:, :c] = w_ref[...]
        x = pad_scr[...]
        for s in range(N_SHARD):
            @pl.when(s_ref[0] == s)
            def _():
                o_ref[...] = _to_window(x, s).astype(BF16)

    return pl.pallas_call(
        body, name=name,
        grid_spec=pltpu.PrefetchScalarGridSpec(
            num_scalar_prefetch=1, grid=(r // tb,),
            in_specs=[pl.BlockSpec((tb, c), lambda i, s: (i, 0))],
            out_specs=pl.BlockSpec((tb, SHARD_PAD), lambda i, s: (i, 0)),
            scratch_shapes=[pltpu.VMEM((tb, SHARD_PAD), F32)]),
        out_shape=jax.ShapeDtypeStruct((r, SHARD_PAD), BF16), compiler_params=_cp(),
    )(shard, w)


def _pair_add(g, recv, c_idx, name):
    n, r, c = g.shape
    half = r // 2
    tb = _pick(half, (128, 240))
    nb = half // tb

    def body(c_ref, g_ref, r_ref, o_ref):
        o_ref[...] = (g_ref[...].astype(F32) + r_ref[...].astype(F32)).astype(BF16)

    blk = pl.BlockSpec((n, tb, c), lambda i, c_ref: (0, i, 0))
    return pl.pallas_call(
        body, name=name,
        grid_spec=pltpu.PrefetchScalarGridSpec(
            num_scalar_prefetch=1, grid=(nb,),
            in_specs=[pl.BlockSpec((n, tb, c), lambda i, c_ref: (0, c_ref[0] * nb + i, 0)), blk], out_specs=blk),
        out_shape=jax.ShapeDtypeStruct((n, half, c), BF16), compiler_params=_cp(),
    )(c_idx, g, recv)


def _chip_sum(parts, by_chip, place, name):
    n, h, c = parts.shape
    tb = _pick(h, (128, 240))
    nb = h // tb

    def body(p_ref, mine_ref, *rest):
        others, o_ref = rest[:n], rest[n]
        me = jnp.zeros((tb, c), jnp.int32) + p_ref[0]
        acc = None
        for q in range(n):
            term = jnp.where(me == q, mine_ref[...], others[q][...]).astype(F32)
            acc = term if acc is None else acc + term
        o_ref[...] = acc

    def other(q):
        return pl.BlockSpec((None, tb, c), lambda i, p: (jnp.where(p[0] == q, (q + 1) % n, q), i, 0))

    return pl.pallas_call(
        body, name=name,
        grid_spec=pltpu.PrefetchScalarGridSpec(
            num_scalar_prefetch=1, grid=(nb,),
            in_specs=[pl.BlockSpec((None, tb, c), lambda i, p: (p[0], i, 0))] + [other(q) for q in range(n)],
            out_specs=pl.BlockSpec((tb, c), lambda i, p: (p[1] * nb + i, 0))),
        out_shape=jax.ShapeDtypeStruct((2 * h, c), F32), compiler_params=_cp(),
    )(place, parts, *([by_chip] * n))


def _adamw_math(w, g, m, v):
    m = ADAM_B1 * m + (1.0 - ADAM_B1) * g
    v = ADAM_B2 * v + (1.0 - ADAM_B2) * (g * g)
    m_hat = m / (1.0 - ADAM_B1 ** ADAM_STEP)
    v_hat = v / (1.0 - ADAM_B2 ** ADAM_STEP)
    delta = -ADAM_LR * (m_hat / (jnp.sqrt(v_hat) + ADAM_EPS) + ADAM_WD * w)
    return delta, m, v


def _adamw(w, g, m, v, name):
    r, c = w.shape
    tb = _pick(r, (128, 496, 240))

    def body(w_ref, g_ref, m_ref, v_ref, go_ref, d_ref, mo_ref, vo_ref):
        gv = g_ref[...]
        d, mn, vn = _adamw_math(w_ref[...], gv, m_ref[...], v_ref[...])
        go_ref[...] = gv
        d_ref[...] = d
        mo_ref[...] = mn
        vo_ref[...] = vn

    blk = pl.BlockSpec((tb, c), lambda i: (i, 0))
    return pl.pallas_call(
        body, name=name, grid=(r // tb,), in_specs=[blk] * 4, out_specs=[blk] * 4,
        out_shape=[jax.ShapeDtypeStruct((r, c), F32)] * 4, compiler_params=_cp(),
    )(w, g, m, v)


def _adamw_window(w, g_win, m, v, shard, name):
    r, c = w.shape
    tb = _pick(r, (128,))

    def body(s_ref, w_ref, g_ref, m_ref, v_ref, go_ref, d_ref, mo_ref, vo_ref, g_scr):
        gw = g_ref[...]
        for s in range(N_SHARD):
            @pl.when(s_ref[0] == s)
            def _():
                g_scr[...] = _from_window(gw, s)

        gv = g_scr[:, :c]
        d, mn, vn = _adamw_math(w_ref[...], gv, m_ref[...], v_ref[...])
        go_ref[...] = gv
        d_ref[...] = d
        mo_ref[...] = mn
        vo_ref[...] = vn

    blk = pl.BlockSpec((tb, c), lambda i, s: (i, 0))
    return pl.pallas_call(
        body, name=name,
        grid_spec=pltpu.PrefetchScalarGridSpec(
            num_scalar_prefetch=1, grid=(r // tb,),
            in_specs=[blk, pl.BlockSpec((tb, SHARD_PAD), lambda i, s: (i, 0)), blk, blk], out_specs=[blk] * 4,
            scratch_shapes=[pltpu.VMEM((tb, SHARD_PAD), F32)]),
        out_shape=[jax.ShapeDtypeStruct((r, c), F32)] * 4, compiler_params=_cp(),
    )(shard, w, g_win, m, v)


def _small_update(gathered, w, m, v):
    def body(p_ref, w_ref, m_ref, v_ref, g_ref, d_ref, mo_ref, vo_ref):
        g = p_ref[0]
        for i in range(1, N_DEV):
            g = g + p_ref[i]
        d, mn, vn = _adamw_math(w_ref[...], g, m_ref[...], v_ref[...])
        g_ref[...] = g
        d_ref[...] = d
        mo_ref[...] = mn
        vo_ref[...] = vn

    full = pl.BlockSpec((S_ROWS, 128), lambda i: (0, 0))
    return pl.pallas_call(
        body, name="small_update", grid=(1,),
        in_specs=[pl.BlockSpec((N_DEV, S_ROWS, 128), lambda i: (0, 0, 0)), full, full, full], out_specs=[full] * 4,
        out_shape=[jax.ShapeDtypeStruct((S_ROWS, 128), F32)] * 4, compiler_params=_cp(),
    )(gathered, w, m, v)


_ANY = pl.BlockSpec(memory_space=pl.ANY)


def _place():
    x, y, c = lax.axis_index("x"), lax.axis_index("y"), lax.axis_index("c")
    chips = [(1 - x, y), (x, 1 - y), (1 - x, 1 - y)]
    return x, y, c, chips


def _gather_shards(arrs):
    n = len(arrs)

    def body(*refs):
        ins, outs = refs[:n], refs[n:2 * n]
        send_sems, recv_sems, local_sems = refs[2 * n:2 * n + 3]
        bufs = refs[2 * n + 3:]
        x, y, c, chips = _place()
        me = 2 * x + y
        sibling = (x, y, 1 - c)
        sends = []
        for a in range(n):
            half = ins[a].shape[0] // 2
            mine = pl.ds(pl.multiple_of(c * half, 16), half)
            for j, (qx, qy) in enumerate(chips):
                cp = pltpu.make_async_remote_copy(
                    src_ref=ins[a].at[mine], dst_ref=outs[a].at[me, mine],
                    send_sem=send_sems.at[6 * a + j], recv_sem=recv_sems.at[6 * a + j],
                    device_id=(qx, qy, c), device_id_type=MESH)
                cp.start()
                sends.append(cp)
        for a in range(n):
            step = bufs[a].shape[0]
            for r0 in range(0, ins[a].shape[0], step):
                rows = pl.ds(r0, step)
                load = pltpu.make_async_copy(ins[a].at[rows], bufs[a], local_sems.at[2 * a])
                load.start()
                load.wait()
                store = pltpu.make_async_copy(bufs[a], outs[a].at[me, rows], local_sems.at[2 * a + 1])
                store.start()
                store.wait()
        for a in range(n):
            half = ins[a].shape[0] // 2
            mine = pl.ds(pl.multiple_of(c * half, 16), half)
            for j, (qx, qy) in enumerate(chips):
                q = 2 * qx + qy
                landed = outs[a].at[q, mine]
                pltpu.make_async_remote_copy(
                    src_ref=landed, dst_ref=landed, send_sem=send_sems.at[6 * a + j], recv_sem=recv_sems.at[6 * a + j],
                    device_id=(qx, qy, c), device_id_type=MESH).wait_recv()
                fw = pltpu.make_async_remote_copy(
                    src_ref=landed, dst_ref=landed, send_sem=send_sems.at[6 * a + 3 + j],
                    recv_sem=recv_sems.at[6 * a + 3 + j], device_id=sibling, device_id_type=MESH)
                fw.start()
                sends.append(fw)
        for a in range(n):
            half = ins[a].shape[0] // 2
            theirs = pl.ds(pl.multiple_of((1 - c) * half, 16), half)
            for j, (qx, qy) in enumerate(chips):
                q = 2 * qx + qy
                dst = outs[a].at[q, theirs]
                pltpu.make_async_remote_copy(
                    src_ref=dst, dst_ref=dst, send_sem=send_sems.at[6 * a + 3 + j], recv_sem=recv_sems.at[6 * a + 3 + j],
                    device_id=sibling, device_id_type=MESH).wait_recv()
        for cp in sends:
            cp.wait_send()

    return pl.pallas_call(
        body, name="gather_shards", in_specs=[_ANY] * n, out_specs=[_ANY] * n,
        out_shape=[jax.ShapeDtypeStruct((N_SHARD,) + a.shape, a.dtype) for a in arrs],
        scratch_shapes=[pltpu.SemaphoreType.DMA((6 * n,)), pltpu.SemaphoreType.DMA((6 * n,)),
                        pltpu.SemaphoreType.DMA((2 * n,))]
        + [pltpu.VMEM((_pick(a.shape[0], (256, 496)), a.shape[1]), a.dtype) for a in arrs],
        compiler_params=pltpu.CompilerParams(has_side_effects=True, vmem_limit_bytes=VMEM_LIMIT),
    )(*arrs)


def _pair_reduce_send(grads, tag):
    n = len(grads)

    def body(*refs):
        ins, outs = refs[:n], refs[n:2 * n]
        send_sems, recv_sems = refs[2 * n:]
        x, y, c, _ = _place()
        sibling = (x, y, 1 - c)
        cps = []
        for a in range(n):
            half = ins[a].shape[1] // 2
            theirs = pl.ds(pl.multiple_of((1 - c) * half, 8), half)
            cp = pltpu.make_async_remote_copy(
                src_ref=ins[a].at[:, theirs], dst_ref=outs[a], send_sem=send_sems.at[a], recv_sem=recv_sems.at[a],
                device_id=sibling, device_id_type=MESH)
            cp.start()
            cps.append(cp)
        for cp in cps:
            cp.wait()

    return pl.pallas_call(
        body, name="pair_reduce_send_" + tag, in_specs=[_ANY] * n, out_specs=[_ANY] * n,
        out_shape=[jax.ShapeDtypeStruct((g.shape[0], g.shape[1] // 2, g.shape[2]), g.dtype) for g in grads],
        scratch_shapes=[pltpu.SemaphoreType.DMA((n,)), pltpu.SemaphoreType.DMA((n,))],
        compiler_params=pltpu.CompilerParams(has_side_effects=True),
    )(*grads)


_HBM = pl.BlockSpec(memory_space=pltpu.HBM)
_SEM = pl.BlockSpec(memory_space=pltpu.SEMAPHORE)
_DATAFLOW = pltpu.SideEffectType.DATAFLOW_SIDE_EFFECTING


def _chip_exchange_copies(ins, lands, send_sems, recv_sems):
    x, y, c, chips = _place()
    me = 2 * x + y
    cps = []
    for a in range(len(ins)):
        for j, (qx, qy) in enumerate(chips):
            cps.append(pltpu.make_async_remote_copy(
                src_ref=ins[a].at[2 * qx + qy], dst_ref=lands[a].at[me], send_sem=send_sems.at[3 * a + j],
                recv_sem=recv_sems.at[3 * a + j], device_id=(qx, qy, c), device_id_type=MESH))
    return cps


def _chip_exchange_start(parts, tag):
    n = len(parts)

    def body(*refs):
        ins, lands = refs[:n], refs[n:2 * n]
        send_sems, recv_sems = refs[2 * n:2 * n + 2]
        token = refs[4 * n + 2]
        for cp in _chip_exchange_copies(ins, lands, send_sems, recv_sems):
            cp.start()
        token[...] = jnp.zeros_like(token)

    hbm = [pltpu.HBM(p.shape, p.dtype) for p in parts]
    lands = [pltpu.with_memory_space_constraint(lax.empty(p.shape, p.dtype), pltpu.HBM) for p in parts]
    res = pl.pallas_call(
        body, name="chip_exchange_start_" + tag,
        out_shape=(pltpu.SemaphoreType.DMA((3 * n,)), pltpu.SemaphoreType.DMA((3 * n,)), *hbm, *hbm,
                   jax.ShapeDtypeStruct((8, 128), F32)),
        in_specs=[_HBM] * (2 * n), out_specs=(_SEM, _SEM, *([_HBM] * (2 * n)), pl.BlockSpec(memory_space=pltpu.VMEM)),
        input_output_aliases={a: 2 + a for a in range(2 * n)},
        compiler_params=pltpu.CompilerParams(has_side_effects=_DATAFLOW),
    )(*[pltpu.with_memory_space_constraint(p, pltpu.HBM) for p in parts], *lands)
    return res[0], res[1], res[2:2 + n], res[2 + n:2 + 2 * n], res[2 + 2 * n]


def _chip_exchange_wait(send_sems, recv_sems, parts, lands, after, tag):
    n = len(parts)

    def body(*refs):
        ins, land_refs = refs[:n], refs[n:2 * n]
        s_sems, r_sems = refs[2 * n:2 * n + 2]
        for cp in _chip_exchange_copies(ins, land_refs, s_sems, r_sems):
            cp.wait_send()
            cp.wait_recv()

    hbm = [pltpu.HBM(p.shape, p.dtype) for p in parts]
    res = pl.pallas_call(
        body, name="chip_exchange_wait_" + tag, out_shape=(*hbm, *hbm),
        in_specs=[_HBM] * (2 * n) + [_SEM, _SEM, _ANY], out_specs=tuple([_HBM] * (2 * n)),
        input_output_aliases={a: a for a in range(2 * n)},
        compiler_params=pltpu.CompilerParams(has_side_effects=_DATAFLOW),
    )(*parts, *lands, send_sems, recv_sems, after)
    return res[:n], res[n:]


def _shard_gather_copies(src, land, send_sems, recv_sems):
    x, y, c, chips = _place()
    me = 2 * x + y
    return [pltpu.make_async_remote_copy(
        src_ref=src, dst_ref=land.at[me], send_sem=send_sems.at[j], recv_sem=recv_sems.at[j],
        device_id=(qx, qy, c), device_id_type=MESH) for j, (qx, qy) in enumerate(chips)]


def _shard_gather_start(shard_arr, after):
    def body(src, land, after_ref, send_sems, recv_sems, src_thru, land_thru, token):
        for cp in _shard_gather_copies(src, land, send_sems, recv_sems):
            cp.start()
        token[...] = jnp.zeros_like(token)

    land_shape = (N_SHARD,) + shard_arr.shape
    land = pltpu.with_memory_space_constraint(lax.empty(land_shape, shard_arr.dtype), pltpu.HBM)
    return pl.pallas_call(
        body, name="shard_gather_start",
        out_shape=(pltpu.SemaphoreType.DMA((N_SHARD - 1,)), pltpu.SemaphoreType.DMA((N_SHARD - 1,)),
                   pltpu.HBM(shard_arr.shape, shard_arr.dtype), pltpu.HBM(land_shape, shard_arr.dtype),
                   jax.ShapeDtypeStruct((8, 128), F32)),
        in_specs=[_HBM, _HBM, _ANY], out_specs=(_SEM, _SEM, _HBM, _HBM, pl.BlockSpec(memory_space=pltpu.VMEM)),
        input_output_aliases={0: 2, 1: 3},
        compiler_params=pltpu.CompilerParams(has_side_effects=_DATAFLOW),
    )(pltpu.with_memory_space_constraint(shard_arr, pltpu.HBM), land, after)


def _shard_gather_wait(send_sems, recv_sems, shard_arr, land, after):
    def body(src, land_ref, s_sems, r_sems, after_ref, src_out, land_out):
        for cp in _shard_gather_copies(src, land_ref, s_sems, r_sems):
            cp.wait_send()
            cp.wait_recv()

    return pl.pallas_call(
        body, name="shard_gather_wait",
        out_shape=(pltpu.HBM(shard_arr.shape, shard_arr.dtype), pltpu.HBM(land.shape, land.dtype)),
        in_specs=[_HBM, _HBM, _SEM, _SEM, _ANY], out_specs=(_HBM, _HBM), input_output_aliases={0: 0, 1: 1},
        compiler_params=pltpu.CompilerParams(has_side_effects=_DATAFLOW),
    )(shard_arr, land, send_sems, recv_sems, after)


def _pair_allgather(fulls, tag):
    n = len(fulls)

    def body(*refs):
        outs = refs[n:2 * n]
        send_sems, recv_sems = refs[2 * n:]
        x, y, c, _ = _place()
        sibling = (x, y, 1 - c)
        cps = []
        for a in range(n):
            half = outs[a].shape[0] // 2
            mine = outs[a].at[pl.ds(pl.multiple_of(c * half, 8), half)]
            cp = pltpu.make_async_remote_copy(
                src_ref=mine, dst_ref=mine, send_sem=send_sems.at[a], recv_sem=recv_sems.at[a],
                device_id=sibling, device_id_type=MESH)
            cp.start()
            cps.append(cp)
        for a in range(n):
            half = outs[a].shape[0] // 2
            theirs = outs[a].at[pl.ds(pl.multiple_of((1 - c) * half, 8), half)]
            pltpu.make_async_remote_copy(
                src_ref=theirs, dst_ref=theirs, send_sem=send_sems.at[a], recv_sem=recv_sems.at[a],
                device_id=sibling, device_id_type=MESH).wait_recv()
        for cp in cps:
            cp.wait_send()

    return pl.pallas_call(
        body, name="pair_allgather_" + tag, in_specs=[_ANY] * n, out_specs=[_ANY] * n,
        out_shape=[jax.ShapeDtypeStruct(f.shape, f.dtype) for f in fulls],
        input_output_aliases={a: a for a in range(n)},
        scratch_shapes=[pltpu.SemaphoreType.DMA((n,)), pltpu.SemaphoreType.DMA((n,))],
        compiler_params=pltpu.CompilerParams(has_side_effects=True),
    )(*fulls)


def _allgather_small(slab):
    def body(s_ref, out_ref, send_sems, recv_sems):
        x, y, c, _ = _place()
        me = 4 * x + 2 * y + c
        out_ref[me] = s_ref[...]
        cps = []
        for mask in range(1, N_DEV):
            peer = (x ^ (mask >> 2), y ^ ((mask >> 1) & 1), c ^ (mask & 1))
            cp = pltpu.make_async_remote_copy(
                src_ref=s_ref, dst_ref=out_ref.at[me], send_sem=send_sems.at[mask - 1], recv_sem=recv_sems.at[mask - 1],
                device_id=peer, device_id_type=MESH)
            cp.start()
            cps.append(cp)
        for mask in range(1, N_DEV):
            peer = (x ^ (mask >> 2), y ^ ((mask >> 1) & 1), c ^ (mask & 1))
            dst = out_ref.at[4 * peer[0] + 2 * peer[1] + peer[2]]
            pltpu.make_async_remote_copy(
                src_ref=dst, dst_ref=dst, send_sem=send_sems.at[mask - 1], recv_sem=recv_sems.at[mask - 1],
                device_id=peer, device_id_type=MESH).wait_recv()
        for cp in cps:
            cp.wait_send()

    vm = pl.BlockSpec(memory_space=pltpu.VMEM)
    return pl.pallas_call(
        body, name="allgather_small", in_specs=[vm], out_specs=vm,
        out_shape=jax.ShapeDtypeStruct((N_DEV,) + slab.shape, slab.dtype),
        scratch_shapes=[pltpu.SemaphoreType.DMA((N_DEV - 1,)), pltpu.SemaphoreType.DMA((N_DEV - 1,))],
        compiler_params=pltpu.CompilerParams(has_side_effects=True),
    )(slab)


def _pack_b(w_mem_kv, w_br_dn, w_br_sb, w_br_mem, w_out):
    return jnp.concatenate([w_mem_kv.reshape(128, D_MODEL), w_br_dn, w_br_sb, w_br_mem.reshape(64, D_MODEL), w_out],
                           axis=0)


def _conv_slab(conv_w):
    return jnp.pad(conv_w.reshape(3, D_MODEL), ((0, 29), (0, 0)))


def _unpack_b(slab):
    return (slab[B_MEMKV:B_BRDN].reshape(1, 256, 512), slab[B_BRDN:B_BRSB].reshape(1, 256, D_MODEL),
            slab[B_BRSB:B_BRMEM].reshape(1, 256, D_MODEL), slab[B_BRMEM:B_OUT].reshape(1, 256, 256),
            slab[B_OUT:B_CONV].reshape(1, 256, D_MODEL))


def _conv_rows(conv_full):
    return conv_full.reshape(4 * CONV_BLOCKS, 128)


def _conv_shard_rows(conv_shard, shard):
    own = CONV_BLOCKS // N_SHARD
    blocks = lax.dynamic_update_slice(jnp.zeros((4, CONV_BLOCKS, 128), F32), conv_shard.reshape(4, own, 128),
                                      (0, own * shard, 0))
    return blocks.reshape(4 * CONV_BLOCKS, 128)


def _conv_shard_of(rows, shard):
    own = CONV_BLOCKS // N_SHARD
    blocks = lax.dynamic_slice(rows.reshape(4, CONV_BLOCKS, 128), (0, own * shard, 0), (4, own, 128))
    return blocks.reshape(1, 4, own * 128)


def _pack_small(norm_g, mem_norm_g, final_g, dn_norm_g, a_log, dt_bias, conv_rows, loss=None):
    def row(v):
        v = v.reshape(1, -1).astype(F32)
        return jnp.pad(v, ((0, 0), (0, 128 - v.shape[1])))

    loss_row = row(jnp.zeros((1,), F32) if loss is None else jnp.reshape(loss, (1,)))
    rid = lax.broadcasted_iota(jnp.int32, (8, 128), 0) + S_DNNORM
    tile = jnp.where(rid == S_DNNORM, dn_norm_g.reshape(1, 128), jnp.where(
        rid == S_ALOG, row(a_log), jnp.where(rid == S_DTB, row(dt_bias), jnp.where(rid == S_LOSS, loss_row, 0.0))))
    return jnp.concatenate([norm_g.reshape(8, 128), mem_norm_g.reshape(8, 128), final_g.reshape(8, 128), tile,
                            conv_rows], axis=0)


def _unpack_small(slab, shard):
    return (slab[S_NORM:S_NORM + 8].reshape(1, D_MODEL), slab[S_MEMNORM:S_MEMNORM + 8].reshape(1, D_MODEL),
            slab[S_FINAL:S_FINAL + 8].reshape(D_MODEL), slab[S_DNNORM].reshape(1, 128),
            slab[S_ALOG, :N_HEADS].reshape(1, N_HEADS), slab[S_DTB, :N_HEADS].reshape(1, N_HEADS),
            _conv_shard_of(slab[S_CONV:], shard))


def _reorder_w_in(w_full):
    pad = jnp.zeros((w_full.shape[0], W_R - IN_WIDTH), w_full.dtype)
    return jnp.concatenate([w_full[:, :4096], w_full[:, 4112:], w_full[:, 4096:4112], pad], axis=1)


def _windows_to_w_r(win):
    b = 128
    s0, s1, s2, s3 = win[0], win[1], win[2], win[3]
    e1, e2, e3 = WIN_START[1] * b, WIN_START[2] * b, WIN_START[3] * b
    n1, n2 = e2 - e1, e3 - e2
    return jnp.concatenate([
        s0[:, :e1], s0[:, e1:e1 + b] + s1[:, :b],
        s1[:, b:n1], s1[:, n1:n1 + b] + s2[:, :b],
        s2[:, b:n2], s2[:, n2:n2 + b] + s3[:, :b],
        s3[:, b:], s1[:, _S1_BA_POS:]], axis=1)


def _dproj_windows(dproj_r):
    b = 128
    pieces = []
    for s in range(N_SHARD):
        lo = WIN_START[s] * b
        if s == 1:
            pieces += [dproj_r[:, lo:lo + _S1_BA_POS], dproj_r[:, C_BA:C_BA + b]]
        else:
            pieces.append(dproj_r[:, lo:lo + SHARD_PAD])
    return jnp.concatenate(pieces, axis=1)


def _local_step(x, mem, tgt, norm_g, mem_norm_g, w_r, w_sh, conv_w, a_log, dt_bias, dn_norm_g, proj_weights, final_g,
                on_early=None, after_gather=None):
    t = x.shape[0]
    final_row = final_g.reshape(1, D_MODEL)
    lanes_8_16 = ((0, 0), (N_HEADS, 128 - 2 * N_HEADS))
    alog_row = jnp.pad(a_log.reshape(1, N_HEADS), lanes_8_16)
    dtb_row = jnp.pad(dt_bias.reshape(1, N_HEADS), lanes_8_16)

    h = _rmsnorm_fwd(x, norm_g, "norm_fwd")
    proj = _mm(h, w_r, "nn", "in_proj", after=after_gather)
    qkv = _dn_prep_fwd(proj, conv_w)
    beta_t, g_t = _dn_gate_fwd(proj, alog_row, dtb_row)
    dn_u, dn_w, dn_qd, dn_kd, dn_a, tinv_all, dn_el = _dn_intra_fwd(qkv, beta_t, g_t)
    o_dn, dn_vn, s_all = _dn_scan_fwd(dn_u, dn_w, dn_qd, dn_kd, dn_a, dn_el)
    o_dn_g = _dn_post_fwd(o_dn, proj, dn_norm_g)
    o_sb, o_sb_g, sb_l = _sb_fwd(proj)
    w_mem_kv, w_br_dn, w_br_sb, w_br_mem, w_out = proj_weights(o_sb_g)
    mem_n = _rmsnorm_fwd(mem, mem_norm_g, "mem_norm_fwd")
    mkv = _mm(mem_n, w_mem_kv, "nn", "mem_kv")
    o_m, o_m_g = _mem_fwd(proj, mkv)
    y_dn = _mm(o_dn_g, w_br_dn, "nn", "br_dn")
    y_sb = _mm(o_sb_g, w_br_sb, "nn", "br_sb")
    y_m = _mm(o_m_g, w_br_mem, "nn", "br_mem")
    merged = _merge_fwd(proj, y_dn, y_sb, y_m)
    mo = _mm(merged, w_out, "nn", "out_proj")
    d_out, d_out_b, loss_row, g_final = _final_loss(x, mo, final_row, tgt)

    g_w_out = _mm(merged, d_out_b, "tn", "g_w_out", out_dtype=BF16)
    d_merged = _mm(d_out_b, w_out, "nt", "d_merged")
    dy_dn, dy_sb, dy_m, dg1, dg2, dg3 = _merge_bwd(proj, y_dn, y_sb, y_m, d_merged)
    g_w_br_dn = _mm(o_dn_g, dy_dn, "tn", "g_w_br_dn", out_dtype=BF16)
    g_w_br_sb = _mm(o_sb_g, dy_sb, "tn", "g_w_br_sb", out_dtype=BF16)
    g_w_br_mem = _mm(o_m_g, dy_m, "tn", "g_w_br_mem", out_dtype=BF16)
    d_o_dn_g = _mm(dy_dn, w_br_dn, "nt", "d_o_dn")
    d_o_sb_g = _mm(dy_sb, w_br_sb, "nt", "d_o_sb")
    d_o_m_g = _mm(dy_m, w_br_mem, "nt", "d_o_mem")

    d_mq, d_mz, d_mkv = _mem_bwd(proj, mkv, o_m, d_o_m_g)
    d_mkv_b = _cast_bf16(d_mkv, "cast_dmkv")
    g_w_mem_kv = _mm(mem_n, d_mkv_b, "tn", "g_w_mem_kv", out_dtype=BF16)
    d_mem_n = _mm(d_mkv_b, w_mem_kv, "nt", "d_mem_n")
    _, g_mem_norm = _rmsnorm_bwd(mem, mem_norm_g, d_mem_n, jnp.zeros_like(mem), "mem_norm_bwd")

    early = dict(w_mem_kv=g_w_mem_kv, w_br_dn=g_w_br_dn, w_br_sb=g_w_br_sb, w_br_mem=g_w_br_mem, w_out=g_w_out)
    after_early = on_early(early) if on_early is not None else None

    d_sq, d_sk, d_sv, d_sz = _sb_bwd(proj, o_sb, sb_l, d_o_sb_g, after=after_early)

    d_o_dn, d_dnz, g_dn_norm = _dn_post_bwd(o_dn, proj, dn_norm_g, d_o_dn_g)
    d_vnew, d_kd, d_qd, d_w, d_el = _dn_scan_bwd(dn_w, dn_qd, dn_kd, dn_a, dn_el, dn_vn, s_all, d_o_dn)
    d_qn, d_kn, d_vn, dbeta_t, dg_t = _dn_intra_bwd(qkv, beta_t, g_t, tinv_all, dn_vn, d_o_dn, d_vnew, d_kd, d_qd, d_w, d_el)
    d_conv_in, g_conv = _dn_prep_bwd(proj, conv_w, d_qn, d_kn, d_vn)
    d_ba, g_alog_row, g_dtb_row = _dn_gate_bwd(proj, alog_row, dtb_row, dbeta_t, dg_t)

    dproj_sh = _dproj_windows(
        jnp.concatenate([d_conv_in, d_dnz, d_sq, d_sk, d_sv, d_sz, d_mq, d_mz, dg1, dg2, dg3, d_ba], axis=1))
    g_w_sh = _mm(h, dproj_sh, "tn", "g_w_in", out_dtype=BF16, out_shards=N_SHARD)
    def input_grad(after=None):
        dh = _mm(dproj_sh, w_sh, "nt", "d_h", after=after)
        grad_x, g_norm = _rmsnorm_bwd(x, norm_g, dh, d_out, "norm_bwd")
        small = dict(norm_g=g_norm, mem_norm_g=g_mem_norm, final_g=g_final, dn_norm_g=g_dn_norm,
                     a_log=g_alog_row[:, N_HEADS:2 * N_HEADS], dt_bias=g_dtb_row[:, N_HEADS:2 * N_HEADS],
                     conv_w=g_conv)
        return grad_x, small

    return loss_row[0, 0], early, g_w_sh, input_grad


def _reduce_scatter_start(grads, tag):
    c = lax.axis_index("c")
    core = jnp.reshape(c, (1,)).astype(jnp.int32)
    recv = _pair_reduce_send(grads, tag)
    parts = [_pair_add(g, r, core, "pair_add_" + tag) for g, r in zip(grads, recv)]
    return _chip_exchange_start(parts, tag)


def _reduce_scatter_finish(handle, after, tag):
    send_sems, recv_sems, parts, lands, _ = handle
    x, y, c = lax.axis_index("x"), lax.axis_index("y"), lax.axis_index("c")
    place = jnp.stack([2 * x + y, c]).astype(jnp.int32)
    parts, by_chip = _chip_exchange_wait(send_sems, recv_sems, parts, lands, after, tag)
    fulls = [_chip_sum(p, b, place, "chip_sum_" + tag) for p, b in zip(parts, by_chip)]
    return _pair_allgather(fulls, tag)


def kernel(x, mem, norm_g, mem_norm_g, w_in, conv_w, a_log, dt_bias, dn_norm_g, w_mem_kv, w_br_dn, w_br_sb, w_br_mem, w_out, final_g, loss_target, m_norm_g, m_mem_norm_g, m_w_in, m_conv_w, m_a_log, m_dt_bias, m_dn_norm_g, m_w_mem_kv, m_w_br_dn, m_w_br_sb, m_w_br_mem, m_w_out, m_final_g, v_norm_g, v_mem_norm_g, v_w_in, v_conv_w, v_a_log, v_dt_bias, v_dn_norm_g, v_w_mem_kv, v_w_br_dn, v_w_br_sb, v_w_br_mem, v_w_out, v_final_g):
    w_a = w_in[0]
    w_b = _pack_b(w_mem_kv[0], w_br_dn[0], w_br_sb[0], w_br_mem[0], w_out[0])
    m_b = _pack_b(m_w_mem_kv[0], m_w_br_dn[0], m_w_br_sb[0], m_w_br_mem[0], m_w_out[0])
    v_b = _pack_b(v_w_mem_kv[0], v_w_br_dn[0], v_w_br_sb[0], v_w_br_mem[0], v_w_out[0])

    shard_idx = 2 * lax.axis_index("x") + lax.axis_index("y")
    shard = jnp.reshape(shard_idx, (1,)).astype(jnp.int32)
    ga, g_conv = _gather_shards([_cast_to_window(w_a, shard, "cast_w_in"),
                                 _cast_bf16(_conv_slab(conv_w[0]), "cast_conv")])
    w_r = _windows_to_w_r(ga)
    f_conv = g_conv[:, :3].reshape(N_SHARD, 4, 768).transpose(1, 0, 2).reshape(4, 3 * D_MODEL).astype(F32)
    b_flight = _shard_gather_start(_cast_bf16(w_b, "cast_w_b"), after=ga)

    def proj_weights(after):
        own, land = _shard_gather_wait(b_flight[0], b_flight[1], b_flight[2], b_flight[3], after)
        gb = lax.dynamic_update_slice(land, own[None], (shard_idx, 0, 0))
        return (gb[:, B_MEMKV:B_BRDN].reshape(N_SHARD * 256, 512),
                gb[:, B_BRDN:B_BRSB].reshape(N_SHARD * 256, D_MODEL),
                gb[:, B_BRSB:B_BRMEM].reshape(N_SHARD * 256, D_MODEL),
                gb[:, B_BRMEM:B_OUT].reshape(N_SHARD, 256, 256).transpose(1, 0, 2).reshape(256, D_MODEL),
                gb[:, B_OUT:B_CONV].reshape(N_SHARD * 256, D_MODEL))

    flights = {}

    def on_early(grads):
        g_b = jnp.concatenate([
            grads["w_mem_kv"].reshape(N_SHARD, 128, D_MODEL), grads["w_br_dn"].reshape(N_SHARD, 256, D_MODEL),
            grads["w_br_sb"].reshape(N_SHARD, 256, D_MODEL),
            grads["w_br_mem"].reshape(256, N_SHARD, 256).transpose(1, 0, 2).reshape(N_SHARD, 64, D_MODEL),
            grads["w_out"].reshape(N_SHARD, 256, D_MODEL)], axis=1).astype(BF16)
        flights["b"] = _reduce_scatter_start([g_b], "b")
        return flights["b"][4]

    loss, _, g_w_sh, input_grad = _local_step(
        x[0], mem[0], loss_target[0], norm_g, mem_norm_g, w_r, ga, f_conv, a_log, dt_bias, dn_norm_g,
        proj_weights, final_g, on_early=on_early, after_gather=b_flight[4])
    flights["a"] = _reduce_scatter_start([g_w_sh], "a")
    grad_x, small = input_grad(after=flights["a"][4])

    part = _pack_small(small["norm_g"], small["mem_norm_g"], small["final_g"], small["dn_norm_g"],
                       small["a_log"], small["dt_bias"], _conv_rows(small["conv_w"]), loss)
    w_s = _pack_small(norm_g, mem_norm_g, final_g, dn_norm_g, a_log, dt_bias, _conv_shard_rows(conv_w[0], shard_idx))
    m_s = _pack_small(m_norm_g, m_mem_norm_g, m_final_g, m_dn_norm_g, m_a_log, m_dt_bias,
                      _conv_shard_rows(m_conv_w[0], shard_idx))
    v_s = _pack_small(v_norm_g, v_mem_norm_g, v_final_g, v_dn_norm_g, v_a_log, v_dt_bias,
                      _conv_shard_rows(v_conv_w[0], shard_idx))
    g_s, d_s, nm_s, nv_s = _small_update(_allgather_small(part), w_s, m_s, v_s)

    (gs_b,) = _reduce_scatter_finish(flights["b"], after=g_s, tag="b")
    (gs_in,) = _reduce_scatter_finish(flights["a"], after=gs_b, tag="a")
    gr_in, d_in, nm_in, nv_in = _adamw_window(w_a, gs_in, m_w_in[0], v_w_in[0], shard, "adamw_w_in")
    gr_b, d_b, nm_b, nv_b = _adamw(w_b, gs_b, m_b, v_b, "adamw_b")

    def assemble(slab_small, a_in, slab_b):
        s_norm, s_memnorm, s_final, s_dnnorm, s_alog, s_dtb, b_conv = _unpack_small(slab_small, shard_idx)
        b_memkv, b_brdn, b_brsb, b_brmem, b_out = _unpack_b(slab_b)
        return [s_norm, s_memnorm, a_in.reshape(1, D_MODEL, IN_WIDTH // N_SHARD), b_conv, s_alog, s_dtb, s_dnnorm,
                b_memkv, b_brdn, b_brsb, b_brmem, b_out, s_final]

    outs = [g_s[S_LOSS, 0], grad_x.reshape(1, -1, D_MODEL)]
    outs += assemble(g_s, gr_in, gr_b)
    outs += assemble(d_s, d_in, d_b)
    outs += assemble(nm_s, nm_in, nm_b)
    outs += assemble(nv_s, nv_in, nv_b)
    return tuple(outs)
```

```python
import math

import jax
import jax.numpy as jnp
from jax import lax
from jax.experimental import pallas as pl
from jax.experimental.pallas import tpu as pltpu

F32 = jnp.float32
BF16 = jnp.bfloat16
MESH = pl.DeviceIdType.MESH

D_MODEL = 1024
N_HEADS = 8
D_HEAD = 128
DN_CHUNK = 64
DN_GROUP = 16
DN_SCAN_GROUP = 4
SB_BLOCK = 256
SB_HEADS_PER_STEP = 2
SB_QBLOCK = 256
MEM_HEADS = 4
MEM_DH = 64
MEM_W = MEM_HEADS * MEM_DH
NORM_EPS = 1e-6
IN_WIDTH = 11792
N_SHARD = 4
SHARD_W = IN_WIDTH // N_SHARD
SHARD_PAD = 3072
N_DEV = 8

C_DNZ = 3072
C_SBQ = 4096
C_SBZ = 7168
C_MQ = 8192
C_MZ = 8448
C_GATES = 8704
C_BA = 11776
W_R = 11904

ADAM_LR = 0.001
ADAM_B1 = 0.9
ADAM_B2 = 0.999
ADAM_EPS = 1e-08
ADAM_WD = 0.01
ADAM_STEP = 10

VMEM_LIMIT = 56 * 1024 * 1024

B_MEMKV, B_BRDN, B_BRSB, B_BRMEM, B_OUT, B_CONV = 0, 128, 384, 640, 704, 960
S_NORM, S_MEMNORM, S_FINAL, S_DNNORM, S_ALOG, S_DTB, S_LOSS, S_CONV, S_ROWS = 0, 8, 16, 24, 25, 26, 27, 32, 128
CONV_BLOCKS = 3 * D_MODEL // 128


def _cp(**kw):
    return pltpu.CompilerParams(vmem_limit_bytes=VMEM_LIMIT, **kw)


def _dot(a, b, dims):
    lead = a.ndim - 2
    ca, cb = {"nn": (1, 0), "nt": (1, 1), "tn": (0, 0)}[dims]
    batch = tuple(range(lead))
    return lax.dot_general(a, b, (((ca + lead,), (cb + lead,)), (batch, batch)), preferred_element_type=F32)


def _chunks(x):
    return x.reshape(x.shape[0] // DN_CHUNK, DN_CHUNK, x.shape[1])


def _unchunk(x):
    return x.reshape(x.shape[0] * x.shape[1], x.shape[2])


def _bdot(a, b, dims):
    return _dot(a.astype(BF16), b.astype(BF16), dims)


def _split(a):
    hi = a.astype(BF16)
    return hi, (a - hi.astype(F32)).astype(BF16)


def _dot3(a, b, dims):
    a1, a2 = _split(a)
    b1, b2 = _split(b)
    return _dot(a1, b1, dims) + (_dot(a1, b2, dims) + _dot(a2, b1, dims))


def _ones_dot(a, ones_bf16):
    out = _dot(a.reshape(-1, a.shape[-1]).astype(BF16), ones_bf16, "nn")
    return out.reshape(a.shape[:-1] + (ones_bf16.shape[1],))


def _sigmoid(x):
    return 1.0 / (1.0 + jnp.exp(-x))


def _log1p_small(u):
    return jnp.where(u < 1e-2, u * (1.0 - u * (0.5 - u * (1.0 / 3.0))), jnp.log(1.0 + u))


def _pick(dim, cands):
    for c in cands:
        if dim % c == 0:
            return c
    return dim


def _mm(a, b, dims, name, out_dtype=F32, out_shards=1, after=None):
    ta, tb = dims[0] == "t", dims[1] == "t"
    m, k = (a.shape[1], a.shape[0]) if ta else a.shape
    b_shards = b.shape[0] if b.ndim == 3 else 1
    n = b.shape[-2] if tb else b.shape[-1]
    tm = _pick(m, (1024, 512, 256))
    tn = _pick(n // out_shards, (512, 384, 256, 128))
    tk = _pick(k // b_shards, (1024, 512, 384, 256))
    nk = k // tk

    def body(a_ref, b_ref, *rest):
        o_ref, acc_ref = rest[-2:]
        kk = pl.program_id(2)

        @pl.when(kk == 0)
        def _():
            acc_ref[...] = jnp.zeros_like(acc_ref)

        acc_ref[...] += _bdot(a_ref[...], b_ref[...], dims)

        @pl.when(kk == nk - 1)
        def _():
            o_ref[...] = acc_ref[...].astype(out_dtype)

    a_spec = pl.BlockSpec((tk, tm), lambda i, j, q: (q, i)) if ta else pl.BlockSpec((tm, tk), lambda i, j, q: (i, q))
    if b_shards > 1:
        per_k = k // b_shards // tk
        b_spec = pl.BlockSpec((None, tn, tk), lambda i, j, q: (q // per_k, j, q % per_k))
    else:
        b_spec = pl.BlockSpec((tn, tk), lambda i, j, q: (j, q)) if tb else pl.BlockSpec((tk, tn), lambda i, j, q: (q, j))
    if out_shards > 1:
        per_n = n // out_shards // tn
        out_spec = pl.BlockSpec((None, tm, tn), lambda i, j, q: (j // per_n, i, j % per_n))
        out_shape = jax.ShapeDtypeStruct((out_shards, m, n // out_shards), out_dtype)
    else:
        out_spec = pl.BlockSpec((tm, tn), lambda i, j, q: (i, j))
        out_shape = jax.ShapeDtypeStruct((m, n), out_dtype)
    extra_specs, extra = [], []
    if after is not None:
        extra_specs, extra = [pl.BlockSpec(after.shape, lambda i, j, q: (0, 0))], [after]
    return pl.pallas_call(
        body, name=name, grid=(m // tm, n // tn, nk),
        in_specs=[a_spec, b_spec] + extra_specs, out_specs=out_spec, out_shape=out_shape,
        scratch_shapes=[pltpu.VMEM((tm, tn), F32)],
        compiler_params=_cp(dimension_semantics=("parallel", "parallel", "arbitrary")),
    )(a, b, *extra)


def _rmsnorm_fwd(x, g, name):
    t, d = x.shape
    tb = _pick(t, (512, 256))

    def body(x_ref, g_ref, h_ref):
        xv = x_ref[...]
        r = lax.rsqrt(jnp.mean(xv * xv, axis=-1, keepdims=True) + NORM_EPS)
        h_ref[...] = ((xv * r) * g_ref[...]).astype(BF16)

    return pl.pallas_call(
        body, name=name, grid=(t // tb,),
        in_specs=[pl.BlockSpec((tb, d), lambda i: (i, 0)), pl.BlockSpec((1, d), lambda i: (0, 0))],
        out_specs=pl.BlockSpec((tb, d), lambda i: (i, 0)),
        out_shape=jax.ShapeDtypeStruct((t, d), BF16), compiler_params=_cp(),
    )(x, g)


def _rmsnorm_bwd(x, g, dh, resid, name):
    t, d = x.shape
    tb = _pick(t, (256,))

    def body(x_ref, g_ref, dh_ref, r_ref, dx_ref, dg_ref):
        @pl.when(pl.program_id(0) == 0)
        def _():
            dg_ref[...] = jnp.zeros_like(dg_ref)

        xv = x_ref[...]
        r = lax.rsqrt(jnp.mean(xv * xv, axis=-1, keepdims=True) + NORM_EPS)
        xhat = xv * r
        dhv = dh_ref[...]
        dg_ref[...] += jnp.sum(dhv * xhat, axis=0, keepdims=True)
        dxh = dhv * g_ref[...]
        dx_ref[...] = r_ref[...] + r * (dxh - xhat * jnp.mean(dxh * xhat, axis=-1, keepdims=True))

    row = pl.BlockSpec((tb, d), lambda i: (i, 0))
    vec = pl.BlockSpec((1, d), lambda i: (0, 0))
    return pl.pallas_call(
        body, name=name, grid=(t // tb,), in_specs=[row, vec, row, row], out_specs=[row, vec],
        out_shape=[jax.ShapeDtypeStruct((t, d), F32), jax.ShapeDtypeStruct((1, d), F32)], compiler_params=_cp(),
    )(x, g, dh, resid)


def _conv_silu(xv, w, row):
    y = xv * w[3:4, :]
    for s in (1, 2, 3):
        xs = jnp.where(row >= s, pltpu.roll(xv, s, 0), 0.0)
        y = y + xs * w[3 - s:4 - s, :]
    sig = _sigmoid(y)
    return y, sig, y * sig


def _dn_prep_fwd(proj, conv_w):
    t = proj.shape[0]

    def body(p_ref, w_ref, o_ref):
        j = pl.program_id(0)
        xv = p_ref[...]
        row = lax.broadcasted_iota(jnp.int32, xv.shape, 0)
        _, _, a = _conv_silu(xv, w_ref[...], row)
        inv = lax.rsqrt(jnp.sum(a * a, axis=-1, keepdims=True) + NORM_EPS)
        scale = jnp.where(j < N_HEADS, D_HEAD ** -0.5, 1.0)
        normed = jnp.where(j < 2 * N_HEADS, 1.0, 0.0)
        o_ref[...] = a * (normed * (inv * scale) + (1.0 - normed))

    return pl.pallas_call(
        body, name="dn_prep_fwd", grid=(3 * N_HEADS,),
        in_specs=[pl.BlockSpec((t, D_HEAD), lambda j: (0, j)), pl.BlockSpec((4, D_HEAD), lambda j: (0, j))],
        out_specs=pl.BlockSpec((t, D_HEAD), lambda j: (0, j)),
        out_shape=jax.ShapeDtypeStruct((t, 3 * D_MODEL), F32), compiler_params=_cp(),
    )(proj, conv_w)


def _dn_prep_bwd(proj, conv_w, dq, dk, dv):
    t = proj.shape[0]

    def body(p_ref, w_ref, dq_ref, dk_ref, dv_ref, dp_ref, dw_ref):
        j = pl.program_id(0)
        xv = p_ref[...]
        w = w_ref[...]
        row = lax.broadcasted_iota(jnp.int32, xv.shape, 0)
        y, s, a = _conv_silu(xv, w, row)
        part = jnp.zeros(xv.shape, jnp.int32) + j // N_HEADS
        dn = jnp.where(part == 0, dq_ref[...], jnp.where(part == 1, dk_ref[...], dv_ref[...]))
        inv = lax.rsqrt(jnp.sum(a * a, axis=-1, keepdims=True) + NORM_EPS)
        scale = jnp.where(j < N_HEADS, D_HEAD ** -0.5, 1.0)
        ds = dn * scale
        da_norm = inv * ds - a * (inv * inv * inv) * jnp.sum(ds * a, axis=-1, keepdims=True)
        normed = jnp.where(j < 2 * N_HEADS, 1.0, 0.0)
        da = normed * da_norm + (1.0 - normed) * dn
        dy = da * (s * (1.0 + y * (1.0 - s)))
        dx = dy * w[3:4, :]
        dw_ref[3:4, :] = jnp.sum(dy * xv, axis=0, keepdims=True)
        for sft in (1, 2, 3):
            xs = jnp.where(row >= sft, pltpu.roll(xv, sft, 0), 0.0)
            dw_ref[3 - sft:4 - sft, :] = jnp.sum(dy * xs, axis=0, keepdims=True)
            dys = jnp.where(row < t - sft, pltpu.roll(dy, t - sft, 0), 0.0)
            dx = dx + dys * w[3 - sft:4 - sft, :]
        dp_ref[...] = dx.astype(BF16)

    blk = pl.BlockSpec((t, D_HEAD), lambda j: (0, j))
    wblk = pl.BlockSpec((4, D_HEAD), lambda j: (0, j))

    def grad(part):
        return pl.BlockSpec((t, D_HEAD), lambda j: (0, jnp.clip(j - part * N_HEADS, 0, N_HEADS - 1)))

    return pl.pallas_call(
        body, name="dn_prep_bwd", grid=(3 * N_HEADS,), in_specs=[blk, wblk, grad(0), grad(1), grad(2)],
        out_specs=[blk, wblk],
        out_shape=[jax.ShapeDtypeStruct((t, 3 * D_MODEL), BF16), jax.ShapeDtypeStruct((4, 3 * D_MODEL), F32)],
        compiler_params=_cp(),
    )(proj, conv_w, dq, dk, dv)


def _softplus_parts(xv):
    e = jnp.exp(-jnp.abs(xv))
    return jnp.maximum(xv, 0.0) + _log1p_small(e)


def _chunk_scan(v, row, reverse):
    t = v.shape[0]
    pos = row & (DN_CHUNK - 1)
    s = 1
    while s < DN_CHUNK:
        if reverse:
            v = v + jnp.where(pos < DN_CHUNK - s, pltpu.roll(v, t - s, 0), 0.0)
        else:
            v = v + jnp.where(pos >= s, pltpu.roll(v, s, 0), 0.0)
        s *= 2
    return v


def _dn_gate_fwd(proj, alog_row, dtb_row):
    t = proj.shape[0]

    def body(p_ref, al_ref, dt_ref, b_ref, g_ref):
        p = p_ref[...]
        row = lax.broadcasted_iota(jnp.int32, p.shape, 0)
        b_ref[...] = _sigmoid(p)
        g = -jnp.exp(al_ref[...]) * _softplus_parts(p + dt_ref[...])
        g_ref[...] = _chunk_scan(g, row, reverse=False)

    blk = pl.BlockSpec((t, 128), lambda i: (0, C_BA // 128))
    vec = pl.BlockSpec((1, 128), lambda i: (0, 0))
    out = pl.BlockSpec((t, 128), lambda i: (0, 0))
    return pl.pallas_call(
        body, name="dn_gate_fwd", grid=(1,), in_specs=[blk, vec, vec], out_specs=[out, out],
        out_shape=[jax.ShapeDtypeStruct((t, 128), F32)] * 2, compiler_params=_cp(),
    )(proj, alog_row, dtb_row)


def _dn_gate_bwd(proj, alog_row, dtb_row, dbeta, dgc):
    t = proj.shape[0]

    def body(p_ref, al_ref, dt_ref, db_ref, dg_ref, dp_ref, dal_ref, ddt_ref):
        p = p_ref[...]
        row = lax.broadcasted_iota(jnp.int32, p.shape, 0)
        lane = lax.broadcasted_iota(jnp.int32, p.shape, 1)
        s = _sigmoid(p)
        d_b = db_ref[...] * s * (1.0 - s)
        dg = _chunk_scan(dg_ref[...], row, reverse=True)
        xa = p + dt_ref[...]
        ea = jnp.exp(al_ref[...])
        g = -ea * _softplus_parts(xa)
        d_a = dg * (-ea) * _sigmoid(xa)
        dp_ref[...] = jnp.where(lane < N_HEADS, d_b, jnp.where(lane < 2 * N_HEADS, d_a, 0.0)).astype(BF16)
        dal_ref[...] = jnp.sum(dg * g, axis=0, keepdims=True)
        ddt_ref[...] = jnp.sum(d_a, axis=0, keepdims=True)

    blk = pl.BlockSpec((t, 128), lambda i: (0, C_BA // 128))
    vec = pl.BlockSpec((1, 128), lambda i: (0, 0))
    full = pl.BlockSpec((t, 128), lambda i: (0, 0))
    return pl.pallas_call(
        body, name="dn_gate_bwd", grid=(1,), in_specs=[blk, vec, vec, full, full], out_specs=[full, vec, vec],
        out_shape=[jax.ShapeDtypeStruct((t, 128), BF16), jax.ShapeDtypeStruct((1, 128), F32),
                   jax.ShapeDtypeStruct((1, 128), F32)], compiler_params=_cp(),
    )(proj, alog_row, dtb_row, dbeta, dgc)


def _col_to_row(col, eye):
    return jnp.sum(jnp.where(eye, col, 0.0), axis=-2, keepdims=True)


def _row_to_col(rowv, eye):
    return jnp.sum(jnp.where(eye, rowv, 0.0), axis=-1, keepdims=True)


def _tri_inverse(m, ri, ci):
    eye = (ri == ci).astype(F32)
    b16 = (ri >> 4) == (ci >> 4)
    b32 = (ri >> 5) == (ci >> 5)
    m1 = jnp.where(b16, m, 0.0)
    x = eye - m1
    p = _dot3(m1, m1, "nn")
    x = x + _dot3(x, p, "nn")
    p = _dot3(p, p, "nn")
    x = x + _dot3(x, p, "nn")
    p = _dot3(p, p, "nn")
    x = x + _dot3(x, p, "nn")
    c1 = jnp.where(jnp.logical_and(b32, jnp.logical_not(b16)), m, 0.0)
    x = x - _dot3(_dot3(x, c1, "nn"), x, "nn")
    c2 = jnp.where(b32, 0.0, m)
    x = x - _dot3(_dot3(x, c2, "nn"), x, "nn")
    return x


def _dn_chunk_common(q, k, gc, ri, ci):
    eye = ri == ci
    g_row = _col_to_row(gc, eye)
    diff = jnp.minimum(gc - g_row, 0.0)
    gam = jnp.where(ri >= ci, jnp.exp(diff), 0.0)
    kk = _bdot(k, k, "nt")
    qk = _bdot(q, k, "nt")
    rcol = lax.broadcasted_iota(jnp.int32, gc.shape, gc.ndim - 2)
    last = jnp.sum(jnp.where(rcol == DN_CHUNK - 1, gc, 0.0), axis=-2, keepdims=True)
    e_g = jnp.exp(gc)
    dec = jnp.exp(last - gc)
    return eye, gam, kk, qk, last, e_g, dec, rcol


def _dn_specs(t, rows_blk):
    def head(off):
        return pl.BlockSpec((rows_blk, D_HEAD), lambda g, h: (g, off + h))

    lanes = pl.BlockSpec((rows_blk, 128), lambda g, h: (g, 0))
    hm = pl.BlockSpec((None, rows_blk, D_HEAD), lambda g, h: (h, g, 0))
    sq = pl.BlockSpec((1, rows_blk, DN_CHUNK), lambda g, h: (h, g, 0))
    tile = pl.BlockSpec((1, rows_blk // DN_CHUNK, 8, 128), lambda g, h: (h, g, 0, 0))
    return head, lanes, hm, sq, tile


def _head_column(slab, lane_idx):
    lane = lax.broadcasted_iota(jnp.int32, slab.shape, 1)
    return _chunks(jnp.sum(jnp.where(lane == lane_idx, slab, 0.0), axis=1, keepdims=True))


def _dn_intra_fwd(qkv, beta_t, g_t):
    t = qkv.shape[0]
    n_chunks = t // DN_CHUNK
    rows_blk = min(DN_GROUP * DN_CHUNK, t)

    def body(q_ref, k_ref, v_ref, b_ref, g_ref, u_ref, w_ref, qd_ref, kd_ref, a_ref, ti_ref, el_ref):
        ri = lax.broadcasted_iota(jnp.int32, (DN_CHUNK, DN_CHUNK), 0)
        ci = lax.broadcasted_iota(jnp.int32, (DN_CHUNK, DN_CHUNK), 1)
        h = pl.program_id(1)
        q, k, v = (_chunks(r[...]) for r in (q_ref, k_ref, v_ref))
        b, gc = _head_column(b_ref[...], h), _head_column(g_ref[...], h + N_HEADS)
        _, gam, kk, qk, last, e_g, dec, _ = _dn_chunk_common(q, k, gc, ri, ci)
        tinv = _tri_inverse(jnp.where(ri > ci, b * kk * gam, 0.0), ri, ci)
        u_ref[...] = _unchunk(_bdot(tinv, v * b, "nn"))
        w_ref[...] = _unchunk(_bdot(tinv, k * (b * e_g), "nn"))
        qd_ref[...] = _unchunk(q * e_g)
        kd_ref[...] = _unchunk(k * dec)
        a_ref[0] = _unchunk(qk * gam)
        ti_ref[0] = _unchunk(tinv)
        el_ref[0] = jnp.broadcast_to(jnp.exp(last), (rows_blk // DN_CHUNK, 8, 128))

    head, lanes, hm, sq, tile = _dn_specs(t, rows_blk)
    act = jax.ShapeDtypeStruct((N_HEADS, t, D_HEAD), F32)
    sqs = jax.ShapeDtypeStruct((N_HEADS, t, DN_CHUNK), F32)
    return pl.pallas_call(
        body, name="dn_intra_fwd", grid=(t // rows_blk, N_HEADS),
        in_specs=[head(0), head(N_HEADS), head(2 * N_HEADS), lanes, lanes],
        out_specs=[hm] * 4 + [sq, sq, tile],
        out_shape=[act] * 4 + [sqs, sqs, jax.ShapeDtypeStruct((N_HEADS, n_chunks, 8, 128), F32)],
        compiler_params=_cp(),
    )(qkv, qkv, qkv, beta_t, g_t)


def _dn_scan_specs(t, rows_blk, reverse):
    n_groups = t // rows_blk

    def at(g):
        return n_groups - 1 - g if reverse else g

    per = rows_blk // DN_CHUNK
    act = pl.BlockSpec((N_HEADS, rows_blk, D_HEAD), lambda g: (0, at(g), 0))
    sq = pl.BlockSpec((N_HEADS, rows_blk, DN_CHUNK), lambda g: (0, at(g), 0))
    state = pl.BlockSpec((N_HEADS, per, D_HEAD, D_HEAD), lambda g: (0, at(g), 0, 0))
    tile = pl.BlockSpec((N_HEADS, per, 8, 128), lambda g: (0, at(g), 0, 0))
    return act, sq, state, tile


def _dn_scan_fwd(u, w, qd, kd, a, el):
    t = u.shape[1]
    n_chunks = t // DN_CHUNK
    rows_blk = DN_SCAN_GROUP * DN_CHUNK

    def body(u_ref, w_ref, qd_ref, kd_ref, a_ref, el_ref, o_ref, vn_ref, s_ref, s_scr):
        @pl.when(pl.program_id(0) == 0)
        def _():
            s_scr[...] = jnp.zeros_like(s_scr)

        for cc in range(DN_SCAN_GROUP):
            rows = slice(cc * DN_CHUNK, (cc + 1) * DN_CHUNK)
            s = s_scr[...]
            s_ref[:, cc] = s
            v_new = u_ref[:, rows, :] - _bdot(w_ref[:, rows, :], s, "nn")
            vn_ref[:, rows, :] = v_new
            o_ref[:, rows, :] = _bdot(qd_ref[:, rows, :], s, "nn") + _bdot(a_ref[:, rows, :], v_new, "nn")
            s_scr[...] = s * el_ref[:, cc][:, 0:1, :] + _bdot(kd_ref[:, rows, :], v_new, "tn")

    act, sq, state, tile = _dn_scan_specs(t, rows_blk, reverse=False)
    shp = jax.ShapeDtypeStruct((N_HEADS, t, D_HEAD), F32)
    return pl.pallas_call(
        body, name="dn_scan_fwd", grid=(t // rows_blk,),
        in_specs=[act, act, act, act, sq, tile], out_specs=[act, act, state],
        out_shape=[shp, shp, jax.ShapeDtypeStruct((N_HEADS, n_chunks, D_HEAD, D_HEAD), F32)],
        scratch_shapes=[pltpu.VMEM((N_HEADS, D_HEAD, D_HEAD), F32)],
        compiler_params=_cp(dimension_semantics=("arbitrary",)),
    )(u, w, qd, kd, a, el)


def _dn_scan_bwd(w, qd, kd, a, el, vn, s_all, do):
    t = w.shape[1]
    n_chunks = t // DN_CHUNK
    rows_blk = DN_SCAN_GROUP * DN_CHUNK

    def body(w_ref, qd_ref, kd_ref, a_ref, el_ref, vn_ref, s_ref, do_ref, dvn_ref, dkd_ref, dqd_ref, dw_ref, dl_ref, ds_scr):
        @pl.when(pl.program_id(0) == 0)
        def _():
            ds_scr[...] = jnp.zeros_like(ds_scr)

        for cc in reversed(range(DN_SCAN_GROUP)):
            rows = slice(cc * DN_CHUNK, (cc + 1) * DN_CHUNK)
            s = s_ref[:, cc]
            d_s = ds_scr[...]
            e_last = el_ref[:, cc][:, 0:1, :]
            d_o = do_ref[:, rows, :]
            dv_new = _bdot(a_ref[:, rows, :], d_o, "tn") + _bdot(kd_ref[:, rows, :], d_s, "nn")
            ds_scr[...] = d_s * e_last + _bdot(qd_ref[:, rows, :], d_o, "tn") - _bdot(w_ref[:, rows, :], dv_new, "tn")
            dvn_ref[:, rows, :] = dv_new
            dkd_ref[:, rows, :] = _bdot(vn_ref[:, rows, :], d_s, "nt")
            dqd_ref[:, rows, :] = _bdot(d_o, s, "nt")
            dw_ref[:, rows, :] = -_bdot(dv_new, s, "nt")
            dlast = jnp.sum(jnp.sum(d_s * s, axis=2, keepdims=True), axis=1, keepdims=True)
            dl_ref[:, cc] = jnp.broadcast_to(dlast * e_last, (N_HEADS, 8, 128))

    act, sq, state, tile = _dn_scan_specs(t, rows_blk, reverse=True)
    shp = jax.ShapeDtypeStruct((N_HEADS, t, D_HEAD), F32)
    return pl.pallas_call(
        body, name="dn_scan_bwd", grid=(t // rows_blk,),
        in_specs=[act, act, act, sq, tile, act, state, act], out_specs=[act] * 4 + [tile],
        out_shape=[shp] * 4 + [jax.ShapeDtypeStruct((N_HEADS, n_chunks, 8, 128), F32)],
        scratch_shapes=[pltpu.VMEM((N_HEADS, D_HEAD, D_HEAD), F32)],
        compiler_params=_cp(dimension_semantics=("arbitrary",)),
    )(w, qd, kd, a, el, vn, s_all, do)


def _dn_intra_bwd(qkv, beta_t, g_t, tinv_all, vn, do, dvn, dkd, dqd, dw, dl):
    t = qkv.shape[0]
    rows_blk = min(DN_GROUP * DN_CHUNK, t)

    def body(q_ref, k_ref, v_ref, b_ref, g_ref, ti_ref, vn_ref, do_ref, dvn_ref, dkd_ref, dqd_ref, dw_ref, dl_ref,
             dq_ref, dk_ref, dv_ref, db_ref, dg_ref):
        ri = lax.broadcasted_iota(jnp.int32, (DN_CHUNK, DN_CHUNK), 0)
        ci = lax.broadcasted_iota(jnp.int32, (DN_CHUNK, DN_CHUNK), 1)
        h = pl.program_id(1)
        q, k, v = (_chunks(r[...]) for r in (q_ref, k_ref, v_ref))
        b, gc = _head_column(b_ref[...], h), _head_column(g_ref[...], h + N_HEADS)
        tinv = _chunks(ti_ref[0])
        dv_new, dk_dec, dq_dec, d_w = (_chunks(r[...]) for r in (dvn_ref, dkd_ref, dqd_ref, dw_ref))
        eye, gam, kk, qk, _, e_g, dec, rcol = _dn_chunk_common(q, k, gc, ri, ci)
        bv = v * b
        bk = k * (b * e_g)

        d_a = jnp.where(ri >= ci, _bdot(_chunks(do_ref[...]), _chunks(vn_ref[...]), "nt"), 0.0)
        dbv = _bdot(tinv, dv_new, "tn")
        dbk = _bdot(tinv, d_w, "tn")
        d_tinv = _bdot(dv_new, bv, "nt") + _bdot(d_w, bk, "nt")
        d_m = -jnp.where(ri > ci, _dot3(_dot3(tinv, d_tinv, "tn"), tinv, "nt"), 0.0)

        d_kk = d_m * b * gam
        d_gam = d_m * b * kk + d_a * qk
        d_qk = d_a * gam
        dq_ref[...] = _unchunk(_bdot(d_qk, k, "nn") + dq_dec * e_g)
        dk_ref[...] = _unchunk(_bdot(d_qk, q, "tn") + _bdot(d_kk, k, "nn") + _bdot(d_kk, k, "tn")
                               + dk_dec * dec + dbk * (b * e_g))
        dv_ref[...] = _unchunk(dbv * b)
        d_b = _unchunk(jnp.sum(d_m * kk * gam, axis=-1, keepdims=True) + jnp.sum(dbv * v, axis=-1, keepdims=True)
                       + jnp.sum(dbk * k, axis=-1, keepdims=True) * e_g)

        xg = d_gam * gam
        kdk = jnp.sum(dk_dec * (k * dec), axis=-1, keepdims=True)
        d_gc = (jnp.sum(xg, axis=-1, keepdims=True) - _row_to_col(jnp.sum(xg, axis=-2, keepdims=True), eye)
                + jnp.sum(dq_dec * (q * e_g), axis=-1, keepdims=True) - kdk
                + jnp.sum(dbk * bk, axis=-1, keepdims=True))
        d_last_total = dl_ref[0][:, 0:1, 0:1] + jnp.sum(kdk, axis=-2, keepdims=True)
        d_g = _unchunk(d_gc + jnp.where(rcol == DN_CHUNK - 1, d_last_total, 0.0))

        @pl.when(h == 0)
        def _():
            db_ref[...] = jnp.zeros_like(db_ref)
            dg_ref[...] = jnp.zeros_like(dg_ref)

        lane = lax.broadcasted_iota(jnp.int32, db_ref.shape, 1)
        db_ref[...] += jnp.where(lane == h, d_b, 0.0)
        dg_ref[...] += jnp.where(lane == h + N_HEADS, d_g, 0.0)

    head, lanes, hm, sq, tile = _dn_specs(t, rows_blk)
    return pl.pallas_call(
        body, name="dn_intra_bwd", grid=(t // rows_blk, N_HEADS),
        in_specs=[head(0), head(N_HEADS), head(2 * N_HEADS), lanes, lanes, sq] + [hm] * 6 + [tile],
        out_specs=[head(0), head(0), head(0), lanes, lanes],
        out_shape=[jax.ShapeDtypeStruct((t, D_MODEL), F32)] * 3 + [jax.ShapeDtypeStruct((t, 128), F32)] * 2,
        compiler_params=_cp(),
    )(qkv, qkv, qkv, beta_t, g_t, tinv_all, vn, do, dvn, dkd, dqd, dw, dl)


def _dn_post_fwd(o, proj, gn):
    t = o.shape[1]

    def body(o_ref, z_ref, g_ref, out_ref):
        ov, z = o_ref[...], z_ref[...]
        r = lax.rsqrt(jnp.mean(ov * ov, axis=-1, keepdims=True) + NORM_EPS)
        out_ref[...] = (((ov * r) * g_ref[...]) * (z * _sigmoid(z))).astype(BF16)

    blk = pl.BlockSpec((t, D_HEAD), lambda h: (0, h))
    return pl.pallas_call(
        body, name="dn_post_fwd", grid=(N_HEADS,),
        in_specs=[pl.BlockSpec((None, t, D_HEAD), lambda h: (h, 0, 0)),
                  pl.BlockSpec((t, D_HEAD), lambda h: (0, C_DNZ // D_HEAD + h)),
                  pl.BlockSpec((1, D_HEAD), lambda h: (0, 0))],
        out_specs=blk, out_shape=jax.ShapeDtypeStruct((t, D_MODEL), BF16), compiler_params=_cp(),
    )(o, proj, gn)


def _dn_post_bwd(o, proj, gn, dout):
    t = o.shape[1]

    def body(o_ref, z_ref, g_ref, d_ref, do_ref, dz_ref, dg_ref):
        @pl.when(pl.program_id(0) == 0)
        def _():
            dg_ref[...] = jnp.zeros_like(dg_ref)

        ov, z, d = o_ref[...], z_ref[...], d_ref[...]
        r = lax.rsqrt(jnp.mean(ov * ov, axis=-1, keepdims=True) + NORM_EPS)
        ohat = ov * r
        s = _sigmoid(z)
        d_on = d * (z * s)
        dz_ref[...] = (d * (ohat * g_ref[...]) * (s * (1.0 + z * (1.0 - s)))).astype(BF16)
        dg_ref[...] += jnp.sum(d_on * ohat, axis=0, keepdims=True)
        dxh = d_on * g_ref[...]
        do_ref[...] = r * (dxh - ohat * jnp.mean(dxh * ohat, axis=-1, keepdims=True))

    blk = pl.BlockSpec((t, D_HEAD), lambda h: (0, h))
    hm = pl.BlockSpec((None, t, D_HEAD), lambda h: (h, 0, 0))
    vec = pl.BlockSpec((1, D_HEAD), lambda h: (0, 0))
    return pl.pallas_call(
        body, name="dn_post_bwd", grid=(N_HEADS,),
        in_specs=[hm, pl.BlockSpec((t, D_HEAD), lambda h: (0, C_DNZ // D_HEAD + h)), vec, blk],
        out_specs=[hm, blk, vec],
        out_shape=[jax.ShapeDtypeStruct((N_HEADS, t, D_HEAD), F32), jax.ShapeDtypeStruct((t, D_MODEL), BF16),
                   jax.ShapeDtypeStruct((1, D_HEAD), F32)], compiler_params=_cp(),
    )(o, proj, gn, dout)


def _sb_fwd(proj):
    t = proj.shape[0]
    qblk = min(SB_QBLOCK, t)
    scale = 1.0 / math.sqrt(D_HEAD)

    hp = SB_HEADS_PER_STEP
    wid = hp * D_HEAD

    def body(q_ref, k_ref, v_ref, z_ref, o_ref, og_ref, l_ref, qb, kb, vb):
        for hh in range(hp):
            hs = slice(hh * D_HEAD, (hh + 1) * D_HEAD)
            qb[hh] = q_ref[:, hs].astype(BF16)
            kb[hh] = k_ref[:, hs].astype(BF16)
            vb[hh] = v_ref[:, hs].astype(BF16)
        ri = lax.broadcasted_iota(jnp.int32, (qblk, SB_BLOCK), 0)
        ci = lax.broadcasted_iota(jnp.int32, (qblk, SB_BLOCK), 1)
        r2 = lax.broadcasted_iota(jnp.int32, (SB_BLOCK, SB_BLOCK), 0)
        c2 = lax.broadcasted_iota(jnp.int32, (SB_BLOCK, SB_BLOCK), 1)
        upper = (r2 > c2).astype(BF16)
        nkb = qblk // SB_BLOCK

        def qblock(i, carry):
            rows = pl.ds(pl.multiple_of(i * qblk, qblk), qblk)
            qi = qb[:, rows, :]

            def tile(j, st, on_diagonal):
                acc, c = st
                cols = pl.ds(pl.multiple_of(j * SB_BLOCK, SB_BLOCK), SB_BLOCK)
                z = _dot(qi, kb[:, cols, :], "nt") * scale
                lb = jnp.minimum(z, 0.0) - jnp.log(1.0 + jnp.exp(-jnp.abs(z)))
                lf = lb - z
                if on_diagonal:
                    mask = (j * SB_BLOCK + ci) < (i * qblk + ri)
                    lf = jnp.where(mask, lf, 0.0)
                att = jnp.exp(lb + (_ones_dot(lf, upper) + c))
                if on_diagonal:
                    att = jnp.where(mask, att, 0.0)
                acc = acc + _dot(att.astype(BF16), vb[:, cols, :], "nn")
                return acc, c + jnp.sum(lf, axis=-1, keepdims=True)

            st = (jnp.zeros((hp, qblk, D_HEAD), F32), jnp.zeros((hp, qblk, 1), F32))
            for d in range(nkb):
                st = tile((i + 1) * nkb - 1 - d, st, True)
            acc, c = lax.fori_loop(0, i * nkb, lambda jj, s: tile(i * nkb - 1 - jj, s, False), st)
            l_ref[:, rows, :] = c
            for hh in range(hp):
                hs = slice(hh * D_HEAD, (hh + 1) * D_HEAD)
                zg = z_ref[rows, hs]
                o_ref[rows, hs] = acc[hh]
                og_ref[rows, hs] = (acc[hh] * (zg * _sigmoid(zg))).astype(BF16)
            return carry

        lax.fori_loop(0, t // qblk, qblock, 0)

    def head(off):
        return pl.BlockSpec((t, wid), lambda h: (0, off // wid + h))

    out = pl.BlockSpec((t, wid), lambda h: (0, h))
    return pl.pallas_call(
        body, name="sb_fwd", grid=(N_HEADS // hp,),
        in_specs=[head(C_SBQ), head(C_SBQ + D_MODEL), head(C_SBQ + 2 * D_MODEL), head(C_SBZ)],
        out_specs=[out, out, pl.BlockSpec((hp, t, 1), lambda h: (h, 0, 0))],
        out_shape=[jax.ShapeDtypeStruct((t, D_MODEL), F32), jax.ShapeDtypeStruct((t, D_MODEL), BF16),
                   jax.ShapeDtypeStruct((N_HEADS, t, 1), F32)],
        scratch_shapes=[pltpu.VMEM((hp, t, D_HEAD), BF16)] * 3, compiler_params=_cp(),
    )(proj, proj, proj, proj)


def _sb_bwd(proj, o, ltot, dog, after=None):
    t = proj.shape[0]
    qblk = min(SB_QBLOCK, t)
    scale = 1.0 / math.sqrt(D_HEAD)

    hp = SB_HEADS_PER_STEP
    wid = hp * D_HEAD

    def body(q_ref, k_ref, v_ref, z_ref, o_ref, l_ref, d_ref, *rest):
        dq_ref, dk_ref, dv_ref, dz_ref, qb, kb, vb, dob, dk_scr, dv_scr = rest[-10:]
        for hh in range(hp):
            hs = slice(hh * D_HEAD, (hh + 1) * D_HEAD)
            qb[hh] = q_ref[:, hs].astype(BF16)
            kb[hh] = k_ref[:, hs].astype(BF16)
            vb[hh] = v_ref[:, hs].astype(BF16)
            zg = z_ref[:, hs]
            sg = _sigmoid(zg)
            dgo = d_ref[:, hs]
            dob[hh] = (dgo * (zg * sg)).astype(BF16)
            dz_ref[:, hs] = (dgo * o_ref[:, hs] * (sg * (1.0 + zg * (1.0 - sg)))).astype(BF16)
        dk_scr[...] = jnp.zeros_like(dk_scr)
        dv_scr[...] = jnp.zeros_like(dv_scr)
        ri = lax.broadcasted_iota(jnp.int32, (qblk, SB_BLOCK), 0)
        ci = lax.broadcasted_iota(jnp.int32, (qblk, SB_BLOCK), 1)
        r2 = lax.broadcasted_iota(jnp.int32, (SB_BLOCK, SB_BLOCK), 0)
        c2 = lax.broadcasted_iota(jnp.int32, (SB_BLOCK, SB_BLOCK), 1)
        upper = (r2 > c2).astype(BF16)
        below = (r2 < c2).astype(BF16)

        def qblock(i, carry):
            rows = pl.ds(pl.multiple_of(i * qblk, qblk), qblk)
            qi = qb[:, rows, :]
            d_o = dob[:, rows, :]
            ltot = l_ref[:, rows, :]

            def tile(j, st, on_diagonal):
                dq, cpre, ce = st
                cols = pl.ds(pl.multiple_of(j * SB_BLOCK, SB_BLOCK), SB_BLOCK)
                kj, vj = kb[:, cols, :], vb[:, cols, :]
                z = _dot(qi, kj, "nt") * scale
                lb = jnp.minimum(z, 0.0) - jnp.log(1.0 + jnp.exp(-jnp.abs(z)))
                lf = lb - z
                if on_diagonal:
                    mask = (j * SB_BLOCK + ci) < (i * qblk + ri)
                    lf = jnp.where(mask, lf, 0.0)
                tile_sum = jnp.sum(lf, axis=-1, keepdims=True)
                att = jnp.exp(lb + ((ltot - cpre - tile_sum) + _ones_dot(lf, upper)))
                if on_diagonal:
                    att = jnp.where(mask, att, 0.0)
                e = _dot(d_o, vj, "nt") * att
                dlf = ce + _ones_dot(e, below)
                dzz = e - (e + dlf) * jnp.exp(lb)
                if on_diagonal:
                    dzz = jnp.where(mask, dzz, 0.0)
                dzz = dzz.astype(BF16)
                dq = dq + _dot(dzz, kj, "nn")
                dk_scr[:, cols, :] += _dot(dzz, qi, "tn")
                dv_scr[:, cols, :] += _dot(att.astype(BF16), d_o, "tn")
                return dq, cpre + tile_sum, ce + jnp.sum(e, axis=-1, keepdims=True)

            nkb = qblk // SB_BLOCK
            zero_col = jnp.zeros((hp, qblk, 1), F32)
            st = lax.fori_loop(0, i * nkb, lambda j, s: tile(j, s, False),
                               (jnp.zeros((hp, qblk, D_HEAD), F32), zero_col, zero_col))
            for d in range(nkb):
                st = tile(i * nkb + d, st, True)
            dq = st[0]
            for hh in range(hp):
                dq_ref[rows, hh * D_HEAD:(hh + 1) * D_HEAD] = (dq[hh] * scale).astype(BF16)
            return carry

        lax.fori_loop(0, t // qblk, qblock, 0)
        for hh in range(hp):
            hs = slice(hh * D_HEAD, (hh + 1) * D_HEAD)
            dk_ref[:, hs] = (dk_scr[hh] * scale).astype(BF16)
            dv_ref[:, hs] = dv_scr[hh].astype(BF16)

    def head(off):
        return pl.BlockSpec((t, wid), lambda h: (0, off // wid + h))

    extra_specs, extra = [], []
    if after is not None:
        extra_specs, extra = [pl.BlockSpec(after.shape, lambda h: (0, 0))], [after]
    return pl.pallas_call(
        body, name="sb_bwd", grid=(N_HEADS // hp,),
        in_specs=[head(C_SBQ), head(C_SBQ + D_MODEL), head(C_SBQ + 2 * D_MODEL), head(C_SBZ), head(0),
                  pl.BlockSpec((hp, t, 1), lambda h: (h, 0, 0)), head(0)] + extra_specs,
        out_specs=[head(0)] * 4, out_shape=[jax.ShapeDtypeStruct((t, D_MODEL), BF16)] * 4,
        scratch_shapes=[pltpu.VMEM((hp, t, D_HEAD), BF16)] * 4 + [pltpu.VMEM((hp, t, D_HEAD), F32)] * 2,
        compiler_params=_cp(),
    )(proj, proj, proj, proj, o, ltot, dog, *extra)


def _mem_fwd(proj, mkv):
    t = proj.shape[0]
    tq = _pick(t, (512, 256))
    m_len = mkv.shape[0]
    scale = 1.0 / math.sqrt(MEM_DH)

    def body(q_ref, z_ref, kv_ref, o_ref, og_ref):
        q = q_ref[...]
        mk = kv_ref[:, :MEM_W].astype(BF16)
        mv = kv_ref[:, MEM_W:].astype(BF16)
        lane = lax.broadcasted_iota(jnp.int32, q.shape, 1) >> 6
        o = jnp.zeros(q.shape, F32)
        for h in range(MEM_HEADS):
            s = _bdot(jnp.where(lane == h, q, 0.0), mk, "nt") * scale
            p = jnp.exp(s - jnp.max(s, axis=-1, keepdims=True))
            p = p / jnp.sum(p, axis=-1, keepdims=True)
            o = o + jnp.where(lane == h, _bdot(p, mv, "nn"), 0.0)
        z = z_ref[...]
        o_ref[...] = o
        og_ref[...] = (o * (z * _sigmoid(z))).astype(BF16)

    out = pl.BlockSpec((tq, MEM_W), lambda i: (i, 0))
    return pl.pallas_call(
        body, name="mem_fwd", grid=(t // tq,),
        in_specs=[pl.BlockSpec((tq, MEM_W), lambda i: (i, C_MQ // MEM_W)),
                  pl.BlockSpec((tq, MEM_W), lambda i: (i, C_MZ // MEM_W)),
                  pl.BlockSpec((m_len, 2 * MEM_W), lambda i: (0, 0))],
        out_specs=[out, out],
        out_shape=[jax.ShapeDtypeStruct((t, MEM_W), F32), jax.ShapeDtypeStruct((t, MEM_W), BF16)],
        compiler_params=_cp(),
    )(proj, proj, mkv)


def _mem_bwd(proj, mkv, o, dog):
    t = proj.shape[0]
    tq = _pick(t, (512, 256))
    m_len = mkv.shape[0]
    scale = 1.0 / math.sqrt(MEM_DH)

    def body(q_ref, z_ref, kv_ref, o_ref, d_ref, dq_ref, dz_ref, dkv_ref):
        @pl.when(pl.program_id(0) == 0)
        def _():
            dkv_ref[...] = jnp.zeros_like(dkv_ref)

        q = q_ref[...]
        z = z_ref[...]
        sg = _sigmoid(z)
        dgo = d_ref[...]
        d_o = dgo * (z * sg)
        dz_ref[...] = (dgo * o_ref[...] * (sg * (1.0 + z * (1.0 - sg)))).astype(BF16)
        mk = kv_ref[:, :MEM_W].astype(BF16)
        mv = kv_ref[:, MEM_W:].astype(BF16)
        lane = lax.broadcasted_iota(jnp.int32, q.shape, 1) >> 6
        klane = lax.broadcasted_iota(jnp.int32, (m_len, MEM_W), 1) >> 6
        dq = jnp.zeros(q.shape, F32)
        dmk = jnp.zeros((m_len, MEM_W), F32)
        dmv = jnp.zeros((m_len, MEM_W), F32)
        for h in range(MEM_HEADS):
            qh = jnp.where(lane == h, q, 0.0)
            doh = jnp.where(lane == h, d_o, 0.0)
            s = _bdot(qh, mk, "nt") * scale
            p = jnp.exp(s - jnp.max(s, axis=-1, keepdims=True))
            p = p / jnp.sum(p, axis=-1, keepdims=True)
            dp = _bdot(doh, mv, "nt")
            ds = p * (dp - jnp.sum(dp * p, axis=-1, keepdims=True)) * scale
            dq = dq + jnp.where(lane == h, _bdot(ds, mk, "nn"), 0.0)
            dmk = dmk + jnp.where(klane == h, _bdot(ds, qh, "tn"), 0.0)
            dmv = dmv + jnp.where(klane == h, _bdot(p, doh, "tn"), 0.0)
        dq_ref[...] = dq.astype(BF16)
        dkv_ref[:, :MEM_W] += dmk
        dkv_ref[:, MEM_W:] += dmv

    blk = pl.BlockSpec((tq, MEM_W), lambda i: (i, 0))
    kv = pl.BlockSpec((m_len, 2 * MEM_W), lambda i: (0, 0))
    return pl.pallas_call(
        body, name="mem_bwd", grid=(t // tq,),
        in_specs=[pl.BlockSpec((tq, MEM_W), lambda i: (i, C_MQ // MEM_W)),
                  pl.BlockSpec((tq, MEM_W), lambda i: (i, C_MZ // MEM_W)), kv, blk, blk],
        out_specs=[blk, blk, kv],
        out_shape=[jax.ShapeDtypeStruct((t, MEM_W), BF16), jax.ShapeDtypeStruct((t, MEM_W), BF16),
                   jax.ShapeDtypeStruct((m_len, 2 * MEM_W), F32)], compiler_params=_cp(),
    )(proj, proj, mkv, o, dog)


_GW = 512


def _merge_fwd(proj, y_dn, y_sb, y_m):
    t = proj.shape[0]
    tb = _pick(t, (256,))
    nc = D_MODEL // _GW

    def body(g1, g2, g3, y1, y2, y3, out_ref):
        out_ref[...] = (_sigmoid(g1[...]) * y1[...] + _sigmoid(g2[...]) * y2[...] + _sigmoid(g3[...]) * y3[...]).astype(BF16)

    def gate(kb):
        return pl.BlockSpec((tb, _GW), lambda i, c: (i, C_GATES // _GW + kb * nc + c))

    blk = pl.BlockSpec((tb, _GW), lambda i, c: (i, c))
    return pl.pallas_call(
        body, name="merge_fwd", grid=(t // tb, nc), in_specs=[gate(0), gate(1), gate(2), blk, blk, blk],
        out_specs=blk, out_shape=jax.ShapeDtypeStruct((t, D_MODEL), BF16), compiler_params=_cp(),
    )(proj, proj, proj, y_dn, y_sb, y_m)


def _merge_bwd(proj, y_dn, y_sb, y_m, dm):
    t = proj.shape[0]
    tb = _pick(t, (256,))
    nc = D_MODEL // _GW

    def body(g1, g2, g3, y1, y2, y3, dm_ref, d1, d2, d3, dg1, dg2, dg3):
        d = dm_ref[...]
        for g, y, dy, dg in ((g1, y1, d1, dg1), (g2, y2, d2, dg2), (g3, y3, d3, dg3)):
            s = _sigmoid(g[...])
            dy[...] = (d * s).astype(BF16)
            dg[...] = (d * y[...] * (s * (1.0 - s))).astype(BF16)

    def gate(kb):
        return pl.BlockSpec((tb, _GW), lambda i, c: (i, C_GATES // _GW + kb * nc + c))

    blk = pl.BlockSpec((tb, _GW), lambda i, c: (i, c))
    act = jax.ShapeDtypeStruct((t, D_MODEL), BF16)
    return pl.pallas_call(
        body, name="merge_bwd", grid=(t // tb, nc), in_specs=[gate(0), gate(1), gate(2), blk, blk, blk, blk],
        out_specs=[blk] * 6, out_shape=[act] * 6, compiler_params=_cp(),
    )(proj, proj, proj, y_dn, y_sb, y_m, dm)


def _final_loss(x, mo, g, tgt):
    t, d = x.shape
    tb = _pick(t, (256,))

    def body(x_ref, mo_ref, g_ref, t_ref, do_ref, dob_ref, loss_ref, dg_ref):
        @pl.when(pl.program_id(0) == 0)
        def _():
            loss_ref[...] = jnp.zeros_like(loss_ref)
            dg_ref[...] = jnp.zeros_like(dg_ref)

        out = x_ref[...] + mo_ref[...]
        r = lax.rsqrt(jnp.mean(out * out, axis=-1, keepdims=True) + NORM_EPS)
        xhat = out * r
        gv = g_ref[...]
        err = xhat * gv - t_ref[...]
        per_tok = jnp.mean(err * err, axis=-1, keepdims=True)
        loss_ref[...] += 0.5 * jnp.sum(per_tok, axis=0, keepdims=True)
        dy = err * (1.0 / d)
        dg_ref[...] += jnp.sum(dy * xhat, axis=0, keepdims=True)
        dxh = dy * gv
        dout = r * (dxh - xhat * jnp.mean(dxh * xhat, axis=-1, keepdims=True))
        do_ref[...] = dout
        dob_ref[...] = dout.astype(BF16)

    row = pl.BlockSpec((tb, d), lambda i: (i, 0))
    vec = pl.BlockSpec((1, d), lambda i: (0, 0))
    return pl.pallas_call(
        body, name="final_loss", grid=(t // tb,), in_specs=[row, row, vec, row],
        out_specs=[row, row, pl.BlockSpec((1, 128), lambda i: (0, 0)), vec],
        out_shape=[jax.ShapeDtypeStruct((t, d), F32), jax.ShapeDtypeStruct((t, d), BF16),
                   jax.ShapeDtypeStruct((1, 128), F32), jax.ShapeDtypeStruct((1, d), F32)],
        compiler_params=_cp(),
    )(x, mo, g, tgt)


def _cast_bf16(a, name):
    r, c = a.shape
    tb = _pick(r, (128, 496, 240))

    def body(a_ref, o_ref):
        o_ref[...] = a_ref[...].astype(BF16)

    blk = pl.BlockSpec((tb, c), lambda i: (i, 0))
    return pl.pallas_call(body, name=name, grid=(r // tb,), in_specs=[blk], out_specs=blk,
                          out_shape=jax.ShapeDtypeStruct((r, c), BF16), compiler_params=_cp())(a)


WIN_START = (0, 23, 45, 68)
_S1_LO, _S1_HI = 1148, 1164
_S1_BA_POS = SHARD_PAD - 128


def _to_window(x, s):
    if s == 0:
        return x
    if s in (2, 3):
        return pltpu.roll(x, 120 if s == 2 else 124, 1)
    pos = lax.broadcasted_iota(jnp.int32, x.shape, 1)
    head = pltpu.roll(x, 4, 1)
    tail = pltpu.roll(x, SHARD_PAD - 12, 1)
    ba = jnp.where(pos < _S1_BA_POS + (_S1_HI - _S1_LO), pltpu.roll(x, _S1_BA_POS - _S1_LO, 1), 0.0)
    return jnp.where(pos < _S1_LO + 4, head, jnp.where(pos < _S1_BA_POS, tail, ba))


def _from_window(g, s):
    if s == 0:
        return g
    if s in (2, 3):
        return pltpu.roll(g, SHARD_PAD - (120 if s == 2 else 124), 1)
    col = lax.broadcasted_iota(jnp.int32, g.shape, 1)
    head = pltpu.roll(g, SHARD_PAD - 4, 1)
    tail = pltpu.roll(g, 12, 1)
    ba = pltpu.roll(g, SHARD_PAD - (_S1_BA_POS - _S1_LO), 1)
    return jnp.where(col < _S1_LO, head, jnp.where(col < _S1_HI, ba, tail))


def _cast_to_window(w, shard, name):
    r, c = w.shape
    tb = _pick(r, (128,))

    def body(s_ref, w_ref, o_ref, pad_scr):
        pad_scr[...] = jnp.zeros_like(pad_scr)
        pad_scr[:, :c] = w_ref[...]
        x = pad_scr[...]
        for s in range(N_SHARD):
            @pl.when(s_ref[0] == s)
            def _():
                o_ref[...] = _to_window(x, s).astype(BF16)

    return pl.pallas_call(
        body, name=name,
        grid_spec=pltpu.PrefetchScalarGridSpec(
            num_scalar_prefetch=1, grid=(r // tb,),
            in_specs=[pl.BlockSpec((tb, c), lambda i, s: (i, 0))],
            out_specs=pl.BlockSpec((tb, SHARD_PAD), lambda i, s: (i, 0)),
            scratch_shapes=[pltpu.VMEM((tb, SHARD_PAD), F32)]),
        out_shape=jax.ShapeDtypeStruct((r, SHARD_PAD), BF16), compiler_params=_cp(),
    )(shard, w)


def _pair_add(g, recv, c_idx, name):
    n, r, c = g.shape
    half = r // 2
    tb = _pick(half, (128, 240))
    nb = half // tb

    def body(c_ref, g_ref, r_ref, o_ref):
        o_ref[...] = (g_ref[...].astype(F32) + r_ref[...].astype(F32)).astype(BF16)

    blk = pl.BlockSpec((n, tb, c), lambda i, c_ref: (0, i, 0))
    return pl.pallas_call(
        body, name=name,
        grid_spec=pltpu.PrefetchScalarGridSpec(
            num_scalar_prefetch=1, grid=(nb,),
            in_specs=[pl.BlockSpec((n, tb, c), lambda i, c_ref: (0, c_ref[0] * nb + i, 0)), blk], out_specs=blk),
        out_shape=jax.ShapeDtypeStruct((n, half, c), BF16), compiler_params=_cp(),
    )(c_idx, g, recv)


def _chip_sum(parts, by_chip, place, name):
    n, h, c = parts.shape
    tb = _pick(h, (128, 240))
    nb = h // tb

    def body(p_ref, mine_ref, *rest):
        others, o_ref = rest[:n], rest[n]
        me = jnp.zeros((tb, c), jnp.int32) + p_ref[0]
        acc = None
        for q in range(n):
            term = jnp.where(me == q, mine_ref[...], others[q][...]).astype(F32)
            acc = term if acc is None else acc + term
        o_ref[...] = acc

    def other(q):
        return pl.BlockSpec((None, tb, c), lambda i, p: (jnp.where(p[0] == q, (q + 1) % n, q), i, 0))

    return pl.pallas_call(
        body, name=name,
        grid_spec=pltpu.PrefetchScalarGridSpec(
            num_scalar_prefetch=1, grid=(nb,),
            in_specs=[pl.BlockSpec((None, tb, c), lambda i, p: (p[0], i, 0))] + [other(q) for q in range(n)],
            out_specs=pl.BlockSpec((tb, c), lambda i, p: (p[1] * nb + i, 0))),
        out_shape=jax.ShapeDtypeStruct((2 * h, c), F32), compiler_params=_cp(),
    )(place, parts, *([by_chip] * n))


def _adamw_math(w, g, m, v):
    m = ADAM_B1 * m + (1.0 - ADAM_B1) * g
    v = ADAM_B2 * v + (1.0 - ADAM_B2) * (g * g)
    m_hat = m / (1.0 - ADAM_B1 ** ADAM_STEP)
    v_hat = v / (1.0 - ADAM_B2 ** ADAM_STEP)
    delta = -ADAM_LR * (m_hat / (jnp.sqrt(v_hat) + ADAM_EPS) + ADAM_WD * w)
    return delta, m, v


def _adamw(w, g, m, v, name):
    r, c = w.shape
    tb = _pick(r, (128, 496, 240))

    def body(w_ref, g_ref, m_ref, v_ref, go_ref, d_ref, mo_ref, vo_ref):
        gv = g_ref[...]
        d, mn, vn = _adamw_math(w_ref[...], gv, m_ref[...], v_ref[...])
        go_ref[...] = gv
        d_ref[...] = d
        mo_ref[...] = mn
        vo_ref[...] = vn

    blk = pl.BlockSpec((tb, c), lambda i: (i, 0))
    return pl.pallas_call(
        body, name=name, grid=(r // tb,), in_specs=[blk] * 4, out_specs=[blk] * 4,
        out_shape=[jax.ShapeDtypeStruct((r, c), F32)] * 4, compiler_params=_cp(),
    )(w, g, m, v)


def _adamw_window(w, g_win, m, v, shard, name):
    r, c = w.shape
    tb = _pick(r, (128,))

    def body(s_ref, w_ref, g_ref, m_ref, v_ref, go_ref, d_ref, mo_ref, vo_ref, g_scr):
        gw = g_ref[...]
        for s in range(N_SHARD):
            @pl.when(s_ref[0] == s)
            def _():
                g_scr[...] = _from_window(gw, s)

        gv = g_scr[:, :c]
        d, mn, vn = _adamw_math(w_ref[...], gv, m_ref[...], v_ref[...])
        go_ref[...] = gv
        d_ref[...] = d
        mo_ref[...] = mn
        vo_ref[...] = vn

    blk = pl.BlockSpec((tb, c), lambda i, s: (i, 0))
    return pl.pallas_call(
        body, name=name,
        grid_spec=pltpu.PrefetchScalarGridSpec(
            num_scalar_prefetch=1, grid=(r // tb,),
            in_specs=[blk, pl.BlockSpec((tb, SHARD_PAD), lambda i, s: (i, 0)), blk, blk], out_specs=[blk] * 4,
            scratch_shapes=[pltpu.VMEM((tb, SHARD_PAD), F32)]),
        out_shape=[jax.ShapeDtypeStruct((r, c), F32)] * 4, compiler_params=_cp(),
    )(shard, w, g_win, m, v)


def _small_update(gathered, w, m, v):
    def body(p_ref, w_ref, m_ref, v_ref, g_ref, d_ref, mo_ref, vo_ref):
        g = p_ref[0]
        for i in range(1, N_DEV):
            g = g + p_ref[i]
        d, mn, vn = _adamw_math(w_ref[...], g, m_ref[...], v_ref[...])
        g_ref[...] = g
        d_ref[...] = d
        mo_ref[...] = mn
        vo_ref[...] = vn

    full = pl.BlockSpec((S_ROWS, 128), lambda i: (0, 0))
    return pl.pallas_call(
        body, name="small_update", grid=(1,),
        in_specs=[pl.BlockSpec((N_DEV, S_ROWS, 128), lambda i: (0, 0, 0)), full, full, full], out_specs=[full] * 4,
        out_shape=[jax.ShapeDtypeStruct((S_ROWS, 128), F32)] * 4, compiler_params=_cp(),
    )(gathered, w, m, v)


_ANY = pl.BlockSpec(memory_space=pl.ANY)


def _place():
    x, y, c = lax.axis_index("x"), lax.axis_index("y"), lax.axis_index("c")
    chips = [(1 - x, y), (x, 1 - y), (1 - x, 1 - y)]
    return x, y, c, chips


def _gather_shards(arrs):
    n = len(arrs)

    def body(*refs):
        ins, outs = refs[:n], refs[n:2 * n]
        send_sems, recv_sems, local_sems = refs[2 * n:2 * n + 3]
        bufs = refs[2 * n + 3:]
        x, y, c, chips = _place()
        me = 2 * x + y
        sibling = (x, y, 1 - c)
        sends = []
        for a in range(n):
            half = ins[a].shape[0] // 2
            mine = pl.ds(pl.multiple_of(c * half, 16), half)
            for j, (qx, qy) in enumerate(chips):
                cp = pltpu.make_async_remote_copy(
                    src_ref=ins[a].at[mine], dst_ref=outs[a].at[me, mine],
                    send_sem=send_sems.at[6 * a + j], recv_sem=recv_sems.at[6 * a + j],
                    device_id=(qx, qy, c), device_id_type=MESH)
                cp.start()
                sends.append(cp)
        for a in range(n):
            step = bufs[a].shape[0]
            for r0 in range(0, ins[a].shape[0], step):
                rows = pl.ds(r0, step)
                load = pltpu.make_async_copy(ins[a].at[rows], bufs[a], local_sems.at[2 * a])
                load.start()
                load.wait()
                store = pltpu.make_async_copy(bufs[a], outs[a].at[me, rows], local_sems.at[2 * a + 1])
                store.start()
                store.wait()
        for a in range(n):
            half = ins[a].shape[0] // 2
            mine = pl.ds(pl.multiple_of(c * half, 16), half)
            for j, (qx, qy) in enumerate(chips):
                q = 2 * qx + qy
                landed = outs[a].at[q, mine]
                pltpu.make_async_remote_copy(
                    src_ref=landed, dst_ref=landed, send_sem=send_sems.at[6 * a + j], recv_sem=recv_sems.at[6 * a + j],
                    device_id=(qx, qy, c), device_id_type=MESH).wait_recv()
                fw = pltpu.make_async_remote_copy(
                    src_ref=landed, dst_ref=landed, send_sem=send_sems.at[6 * a + 3 + j],
                    recv_sem=recv_sems.at[6 * a + 3 + j], device_id=sibling, device_id_type=MESH)
                fw.start()
                sends.append(fw)
        for a in range(n):
            half = ins[a].shape[0] // 2
            theirs = pl.ds(pl.multiple_of((1 - c) * half, 16), half)
            for j, (qx, qy) in enumerate(chips):
                q = 2 * qx + qy
                dst = outs[a].at[q, theirs]
                pltpu.make_async_remote_copy(
                    src_ref=dst, dst_ref=dst, send_sem=send_sems.at[6 * a + 3 + j], recv_sem=recv_sems.at[6 * a + 3 + j],
                    device_id=sibling, device_id_type=MESH).wait_recv()
        for cp in sends:
            cp.wait_send()

    return pl.pallas_call(
        body, name="gather_shards", in_specs=[_ANY] * n, out_specs=[_ANY] * n,
        out_shape=[jax.ShapeDtypeStruct((N_SHARD,) + a.shape, a.dtype) for a in arrs],
        scratch_shapes=[pltpu.SemaphoreType.DMA((6 * n,)), pltpu.SemaphoreType.DMA((6 * n,)),
                        pltpu.SemaphoreType.DMA((2 * n,))]
        + [pltpu.VMEM((_pick(a.shape[0], (256, 496)), a.shape[1]), a.dtype) for a in arrs],
        compiler_params=pltpu.CompilerParams(has_side_effects=True, vmem_limit_bytes=VMEM_LIMIT),
    )(*arrs)


def _pair_reduce_send(grads, tag):
    n = len(grads)

    def body(*refs):
        ins, outs = refs[:n], refs[n:2 * n]
        send_sems, recv_sems = refs[2 * n:]
        x, y, c, _ = _place()
        sibling = (x, y, 1 - c)
        cps = []
        for a in range(n):
            half = ins[a].shape[1] // 2
            theirs = pl.ds(pl.multiple_of((1 - c) * half, 8), half)
            cp = pltpu.make_async_remote_copy(
                src_ref=ins[a].at[:, theirs], dst_ref=outs[a], send_sem=send_sems.at[a], recv_sem=recv_sems.at[a],
                device_id=sibling, device_id_type=MESH)
            cp.start()
            cps.append(cp)
        for cp in cps:
            cp.wait()

    return pl.pallas_call(
        body, name="pair_reduce_send_" + tag, in_specs=[_ANY] * n, out_specs=[_ANY] * n,
        out_shape=[jax.ShapeDtypeStruct((g.shape[0], g.shape[1] // 2, g.shape[2]), g.dtype) for g in grads],
        scratch_shapes=[pltpu.SemaphoreType.DMA((n,)), pltpu.SemaphoreType.DMA((n,))],
        compiler_params=pltpu.CompilerParams(has_side_effects=True),
    )(*grads)


_HBM = pl.BlockSpec(memory_space=pltpu.HBM)
_SEM = pl.BlockSpec(memory_space=pltpu.SEMAPHORE)
_DATAFLOW = pltpu.SideEffectType.DATAFLOW_SIDE_EFFECTING


def _chip_exchange_copies(ins, lands, send_sems, recv_sems):
    x, y, c, chips = _place()
    me = 2 * x + y
    cps = []
    for a in range(len(ins)):
        for j, (qx, qy) in enumerate(chips):
            cps.append(pltpu.make_async_remote_copy(
                src_ref=ins[a].at[2 * qx + qy], dst_ref=lands[a].at[me], send_sem=send_sems.at[3 * a + j],
                recv_sem=recv_sems.at[3 * a + j], device_id=(qx, qy, c), device_id_type=MESH))
    return cps


def _chip_exchange_start(parts, tag):
    n = len(parts)

    def body(*refs):
        ins, lands = refs[:n], refs[n:2 * n]
        send_sems, recv_sems = refs[2 * n:2 * n + 2]
        token = refs[4 * n + 2]
        for cp in _chip_exchange_copies(ins, lands, send_sems, recv_sems):
            cp.start()
        token[...] = jnp.zeros_like(token)

    hbm = [pltpu.HBM(p.shape, p.dtype) for p in parts]
    lands = [pltpu.with_memory_space_constraint(lax.empty(p.shape, p.dtype), pltpu.HBM) for p in parts]
    res = pl.pallas_call(
        body, name="chip_exchange_start_" + tag,
        out_shape=(pltpu.SemaphoreType.DMA((3 * n,)), pltpu.SemaphoreType.DMA((3 * n,)), *hbm, *hbm,
                   jax.ShapeDtypeStruct((8, 128), F32)),
        in_specs=[_HBM] * (2 * n), out_specs=(_SEM, _SEM, *([_HBM] * (2 * n)), pl.BlockSpec(memory_space=pltpu.VMEM)),
        input_output_aliases={a: 2 + a for a in range(2 * n)},
        compiler_params=pltpu.CompilerParams(has_side_effects=_DATAFLOW),
    )(*[pltpu.with_memory_space_constraint(p, pltpu.HBM) for p in parts], *lands)
    return res[0], res[1], res[2:2 + n], res[2 + n:2 + 2 * n], res[2 + 2 * n]


def _chip_exchange_wait(send_sems, recv_sems, parts, lands, after, tag):
    n = len(parts)

    def body(*refs):
        ins, land_refs = refs[:n], refs[n:2 * n]
        s_sems, r_sems = refs[2 * n:2 * n + 2]
        for cp in _chip_exchange_copies(ins, land_refs, s_sems, r_sems):
            cp.wait_send()
            cp.wait_recv()

    hbm = [pltpu.HBM(p.shape, p.dtype) for p in parts]
    res = pl.pallas_call(
        body, name="chip_exchange_wait_" + tag, out_shape=(*hbm, *hbm),
        in_specs=[_HBM] * (2 * n) + [_SEM, _SEM, _ANY], out_specs=tuple([_HBM] * (2 * n)),
        input_output_aliases={a: a for a in range(2 * n)},
        compiler_params=pltpu.CompilerParams(has_side_effects=_DATAFLOW),
    )(*parts, *lands, send_sems, recv_sems, after)
    return res[:n], res[n:]


def _shard_gather_copies(src, land, send_sems, recv_sems):
    x, y, c, chips = _place()
    me = 2 * x + y
    return [pltpu.make_async_remote_copy(
        src_ref=src, dst_ref=land.at[me], send_sem=send_sems.at[j], recv_sem=recv_sems.at[j],
        device_id=(qx, qy, c), device_id_type=MESH) for j, (qx, qy) in enumerate(chips)]


def _shard_gather_start(shard_arr, after):
    def body(src, land, after_ref, send_sems, recv_sems, src_thru, land_thru, token):
        for cp in _shard_gather_copies(src, land, send_sems, recv_sems):
            cp.start()
        token[...] = jnp.zeros_like(token)

    land_shape = (N_SHARD,) + shard_arr.shape
    land = pltpu.with_memory_space_constraint(lax.empty(land_shape, shard_arr.dtype), pltpu.HBM)
    return pl.pallas_call(
        body, name="shard_gather_start",
        out_shape=(pltpu.SemaphoreType.DMA((N_SHARD - 1,)), pltpu.SemaphoreType.DMA((N_SHARD - 1,)),
                   pltpu.HBM(shard_arr.shape, shard_arr.dtype), pltpu.HBM(land_shape, shard_arr.dtype),
                   jax.ShapeDtypeStruct((8, 128), F32)),
        in_specs=[_HBM, _HBM, _ANY], out_specs=(_SEM, _SEM, _HBM, _HBM, pl.BlockSpec(memory_space=pltpu.VMEM)),
        input_output_aliases={0: 2, 1: 3},
        compiler_params=pltpu.CompilerParams(has_side_effects=_DATAFLOW),
    )(pltpu.with_memory_space_constraint(shard_arr, pltpu.HBM), land, after)


def _shard_gather_wait(send_sems, recv_sems, shard_arr, land, after):
    def body(src, land_ref, s_sems, r_sems, after_ref, src_out, land_out):
        for cp in _shard_gather_copies(src, land_ref, s_sems, r_sems):
            cp.wait_send()
            cp.wait_recv()

    return pl.pallas_call(
        body, name="shard_gather_wait",
        out_shape=(pltpu.HBM(shard_arr.shape, shard_arr.dtype), pltpu.HBM(land.shape, land.dtype)),
        in_specs=[_HBM, _HBM, _SEM, _SEM, _ANY], out_specs=(_HBM, _HBM), input_output_aliases={0: 0, 1: 1},
        compiler_params=pltpu.CompilerParams(has_side_effects=_DATAFLOW),
    )(shard_arr, land, send_sems, recv_sems, after)


def _pair_allgather(fulls, tag):
    n = len(fulls)

    def body(*refs):
        outs = refs[n:2 * n]
        send_sems, recv_sems = refs[2 * n:]
        x, y, c, _ = _place()
        sibling = (x, y, 1 - c)
        cps = []
        for a in range(n):
            half = outs[a].shape[0] // 2
            mine = outs[a].at[pl.ds(pl.multiple_of(c * half, 8), half)]
            cp = pltpu.make_async_remote_copy(
                src_ref=mine, dst_ref=mine, send_sem=send_sems.at[a], recv_sem=recv_sems.at[a],
                device_id=sibling, device_id_type=MESH)
            cp.start()
            cps.append(cp)
        for a in range(n):
            half = outs[a].shape[0] // 2
            theirs = outs[a].at[pl.ds(pl.multiple_of((1 - c) * half, 8), half)]
            pltpu.make_async_remote_copy(
                src_ref=theirs, dst_ref=theirs, send_sem=send_sems.at[a], recv_sem=recv_sems.at[a],
                device_id=sibling, device_id_type=MESH).wait_recv()
        for cp in cps:
            cp.wait_send()

    return pl.pallas_call(
        body, name="pair_allgather_" + tag, in_specs=[_ANY] * n, out_specs=[_ANY] * n,
        out_shape=[jax.ShapeDtypeStruct(f.shape, f.dtype) for f in fulls],
        input_output_aliases={a: a for a in range(n)},
        scratch_shapes=[pltpu.SemaphoreType.DMA((n,)), pltpu.SemaphoreType.DMA((n,))],
        compiler_params=pltpu.CompilerParams(has_side_effects=True),
    )(*fulls)


def _allgather_small(slab):
    def body(s_ref, out_ref, send_sems, recv_sems):
        x, y, c, _ = _place()
        me = 4 * x + 2 * y + c
        out_ref[me] = s_ref[...]
        cps = []
        for mask in range(1, N_DEV):
            peer = (x ^ (mask >> 2), y ^ ((mask >> 1) & 1), c ^ (mask & 1))
            cp = pltpu.make_async_remote_copy(
                src_ref=s_ref, dst_ref=out_ref.at[me], send_sem=send_sems.at[mask - 1], recv_sem=recv_sems.at[mask - 1],
                device_id=peer, device_id_type=MESH)
            cp.start()
            cps.append(cp)
        for mask in range(1, N_DEV):
            peer = (x ^ (mask >> 2), y ^ ((mask >> 1) & 1), c ^ (mask & 1))
            dst = out_ref.at[4 * peer[0] + 2 * peer[1] + peer[2]]
            pltpu.make_async_remote_copy(
                src_ref=dst, dst_ref=dst, send_sem=send_sems.at[mask - 1], recv_sem=recv_sems.at[mask - 1],
                device_id=peer, device_id_type=MESH).wait_recv()
        for cp in cps:
            cp.wait_send()

    vm = pl.BlockSpec(memory_space=pltpu.VMEM)
    return pl.pallas_call(
        body, name="allgather_small", in_specs=[vm], out_specs=vm,
        out_shape=jax.ShapeDtypeStruct((N_DEV,) + slab.shape, slab.dtype),
        scratch_shapes=[pltpu.SemaphoreType.DMA((N_DEV - 1,)), pltpu.SemaphoreType.DMA((N_DEV - 1,))],
        compiler_params=pltpu.CompilerParams(has_side_effects=True),
    )(slab)


def _pack_b(w_mem_kv, w_br_dn, w_br_sb, w_br_mem, w_out):
    return jnp.concatenate([w_mem_kv.reshape(128, D_MODEL), w_br_dn, w_br_sb, w_br_mem.reshape(64, D_MODEL), w_out],
                           axis=0)


def _conv_slab(conv_w):
    return jnp.pad(conv_w.reshape(3, D_MODEL), ((0, 29), (0, 0)))


def _unpack_b(slab):
    return (slab[B_MEMKV:B_BRDN].reshape(1, 256, 512), slab[B_BRDN:B_BRSB].reshape(1, 256, D_MODEL),
            slab[B_BRSB:B_BRMEM].reshape(1, 256, D_MODEL), slab[B_BRMEM:B_OUT].reshape(1, 256, 256),
            slab[B_OUT:B_CONV].reshape(1, 256, D_MODEL))


def _conv_rows(conv_full):
    return conv_full.reshape(4 * CONV_BLOCKS, 128)


def _conv_shard_rows(conv_shard, shard):
    own = CONV_BLOCKS // N_SHARD
    blocks = lax.dynamic_update_slice(jnp.zeros((4, CONV_BLOCKS, 128), F32), conv_shard.reshape(4, own, 128),
                                      (0, own * shard, 0))
    return blocks.reshape(4 * CONV_BLOCKS, 128)


def _conv_shard_of(rows, shard):
    own = CONV_BLOCKS // N_SHARD
    blocks = lax.dynamic_slice(rows.reshape(4, CONV_BLOCKS, 128), (0, own * shard, 0), (4, own, 128))
    return blocks.reshape(1, 4, own * 128)


def _pack_small(norm_g, mem_norm_g, final_g, dn_norm_g, a_log, dt_bias, conv_rows, loss=None):
    def row(v):
        v = v.reshape(1, -1).astype(F32)
        return jnp.pad(v, ((0, 0), (0, 128 - v.shape[1])))

    loss_row = row(jnp.zeros((1,), F32) if loss is None else jnp.reshape(loss, (1,)))
    rid = lax.broadcasted_iota(jnp.int32, (8, 128), 0) + S_DNNORM
    tile = jnp.where(rid == S_DNNORM, dn_norm_g.reshape(1, 128), jnp.where(
        rid == S_ALOG, row(a_log), jnp.where(rid == S_DTB, row(dt_bias), jnp.where(rid == S_LOSS, loss_row, 0.0))))
    return jnp.concatenate([norm_g.reshape(8, 128), mem_norm_g.reshape(8, 128), final_g.reshape(8, 128), tile,
                            conv_rows], axis=0)


def _unpack_small(slab, shard):
    return (slab[S_NORM:S_NORM + 8].reshape(1, D_MODEL), slab[S_MEMNORM:S_MEMNORM + 8].reshape(1, D_MODEL),
            slab[S_FINAL:S_FINAL + 8].reshape(D_MODEL), slab[S_DNNORM].reshape(1, 128),
            slab[S_ALOG, :N_HEADS].reshape(1, N_HEADS), slab[S_DTB, :N_HEADS].reshape(1, N_HEADS),
            _conv_shard_of(slab[S_CONV:], shard))


def _windows_to_w_r(win):
    b = 128
    s0, s1, s2, s3 = win[0], win[1], win[2], win[3]
    e1, e2, e3 = WIN_START[1] * b, WIN_START[2] * b, WIN_START[3] * b
    n1, n2 = e2 - e1, e3 - e2
    return jnp.concatenate([
        s0[:, :e1], s0[:, e1:e1 + b] + s1[:, :b],
        s1[:, b:n1], s1[:, n1:n1 + b] + s2[:, :b],
        s2[:, b:n2], s2[:, n2:n2 + b] + s3[:, :b],
        s3[:, b:], s1[:, _S1_BA_POS:]], axis=1)


def _dproj_windows(dproj_r):
    b = 128
    pieces = []
    for s in range(N_SHARD):
        lo = WIN_START[s] * b
        if s == 1:
            pieces += [dproj_r[:, lo:lo + _S1_BA_POS], dproj_r[:, C_BA:C_BA + b]]
        else:
            pieces.append(dproj_r[:, lo:lo + SHARD_PAD])
    return jnp.concatenate(pieces, axis=1)


def _local_step(x, mem, tgt, norm_g, mem_norm_g, w_r, w_sh, conv_w, a_log, dt_bias, dn_norm_g, proj_weights, final_g,
                on_early=None, after_gather=None):
    t = x.shape[0]
    final_row = final_g.reshape(1, D_MODEL)
    lanes_8_16 = ((0, 0), (N_HEADS, 128 - 2 * N_HEADS))
    alog_row = jnp.pad(a_log.reshape(1, N_HEADS), lanes_8_16)
    dtb_row = jnp.pad(dt_bias.reshape(1, N_HEADS), lanes_8_16)

    h = _rmsnorm_fwd(x, norm_g, "norm_fwd")
    proj = _mm(h, w_r, "nn", "in_proj", after=after_gather)
    qkv = _dn_prep_fwd(proj, conv_w)
    beta_t, g_t = _dn_gate_fwd(proj, alog_row, dtb_row)
    dn_u, dn_w, dn_qd, dn_kd, dn_a, tinv_all, dn_el = _dn_intra_fwd(qkv, beta_t, g_t)
    o_dn, dn_vn, s_all = _dn_scan_fwd(dn_u, dn_w, dn_qd, dn_kd, dn_a, dn_el)
    o_dn_g = _dn_post_fwd(o_dn, proj, dn_norm_g)
    o_sb, o_sb_g, sb_l = _sb_fwd(proj)
    w_mem_kv, w_br_dn, w_br_sb, w_br_mem, w_out = proj_weights(o_sb_g)
    mem_n = _rmsnorm_fwd(mem, mem_norm_g, "mem_norm_fwd")
    mkv = _mm(mem_n, w_mem_kv, "nn", "mem_kv")
    o_m, o_m_g = _mem_fwd(proj, mkv)
    y_dn = _mm(o_dn_g, w_br_dn, "nn", "br_dn")
    y_sb = _mm(o_sb_g, w_br_sb, "nn", "br_sb")
    y_m = _mm(o_m_g, w_br_mem, "nn", "br_mem")
    merged = _merge_fwd(proj, y_dn, y_sb, y_m)
    mo = _mm(merged, w_out, "nn", "out_proj")
    d_out, d_out_b, loss_row, g_final = _final_loss(x, mo, final_row, tgt)

    g_w_out = _mm(merged, d_out_b, "tn", "g_w_out", out_dtype=BF16)
    d_merged = _mm(d_out_b, w_out, "nt", "d_merged")
    dy_dn, dy_sb, dy_m, dg1, dg2, dg3 = _merge_bwd(proj, y_dn, y_sb, y_m, d_merged)
    g_w_br_dn = _mm(o_dn_g, dy_dn, "tn", "g_w_br_dn", out_dtype=BF16)
    g_w_br_sb = _mm(o_sb_g, dy_sb, "tn", "g_w_br_sb", out_dtype=BF16)
    g_w_br_mem = _mm(o_m_g, dy_m, "tn", "g_w_br_mem", out_dtype=BF16)
    d_o_dn_g = _mm(dy_dn, w_br_dn, "nt", "d_o_dn")
    d_o_sb_g = _mm(dy_sb, w_br_sb, "nt", "d_o_sb")
    d_o_m_g = _mm(dy_m, w_br_mem, "nt", "d_o_mem")

    d_mq, d_mz, d_mkv = _mem_bwd(proj, mkv, o_m, d_o_m_g)
    d_mkv_b = _cast_bf16(d_mkv, "cast_dmkv")
    g_w_mem_kv = _mm(mem_n, d_mkv_b, "tn", "g_w_mem_kv", out_dtype=BF16)
    d_mem_n = _mm(d_mkv_b, w_mem_kv, "nt", "d_mem_n")
    _, g_mem_norm = _rmsnorm_bwd(mem, mem_norm_g, d_mem_n, jnp.zeros_like(mem), "mem_norm_bwd")

    early = dict(w_mem_kv=g_w_mem_kv, w_br_dn=g_w_br_dn, w_br_sb=g_w_br_sb, w_br_mem=g_w_br_mem, w_out=g_w_out)
    after_early = on_early(early) if on_early is not None else None

    d_sq, d_sk, d_sv, d_sz = _sb_bwd(proj, o_sb, sb_l, d_o_sb_g, after=after_early)

    d_o_dn, d_dnz, g_dn_norm = _dn_post_bwd(o_dn, proj, dn_norm_g, d_o_dn_g)
    d_vnew, d_kd, d_qd, d_w, d_el = _dn_scan_bwd(dn_w, dn_qd, dn_kd, dn_a, dn_el, dn_vn, s_all, d_o_dn)
    d_qn, d_kn, d_vn, dbeta_t, dg_t = _dn_intra_bwd(qkv, beta_t, g_t, tinv_all, dn_vn, d_o_dn, d_vnew, d_kd, d_qd, d_w, d_el)
    d_conv_in, g_conv = _dn_prep_bwd(proj, conv_w, d_qn, d_kn, d_vn)
    d_ba, g_alog_row, g_dtb_row = _dn_gate_bwd(proj, alog_row, dtb_row, dbeta_t, dg_t)

    dproj_sh = _dproj_windows(
        jnp.concatenate([d_conv_in, d_dnz, d_sq, d_sk, d_sv, d_sz, d_mq, d_mz, dg1, dg2, dg3, d_ba], axis=1))
    g_w_sh = _mm(h, dproj_sh, "tn", "g_w_in", out_dtype=BF16, out_shards=N_SHARD)
    def input_grad(after=None):
        dh = _mm(dproj_sh, w_sh, "nt", "d_h", after=after)
        grad_x, g_norm = _rmsnorm_bwd(x, norm_g, dh, d_out, "norm_bwd")
        small = dict(norm_g=g_norm, mem_norm_g=g_mem_norm, final_g=g_final, dn_norm_g=g_dn_norm,
                     a_log=g_alog_row[:, N_HEADS:2 * N_HEADS], dt_bias=g_dtb_row[:, N_HEADS:2 * N_HEADS],
                     conv_w=g_conv)
        return grad_x, small

    return loss_row[0, 0], early, g_w_sh, input_grad


def _reduce_scatter_start(grads, tag):
    c = lax.axis_index("c")
    core = jnp.reshape(c, (1,)).astype(jnp.int32)
    recv = _pair_reduce_send(grads, tag)
    parts = [_pair_add(g, r, core, "pair_add_" + tag) for g, r in zip(grads, recv)]
    return _chip_exchange_start(parts, tag)


def _reduce_scatter_finish(handle, after, tag):
    send_sems, recv_sems, parts, lands, _ = handle
    x, y, c = lax.axis_index("x"), lax.axis_index("y"), lax.axis_index("c")
    place = jnp.stack([2 * x + y, c]).astype(jnp.int32)
    parts, by_chip = _chip_exchange_wait(send_sems, recv_sems, parts, lands, after, tag)
    fulls = [_chip_sum(p, b, place, "chip_sum_" + tag) for p, b in zip(parts, by_chip)]
    return _pair_allgather(fulls, tag)


def kernel(x, mem, norm_g, mem_norm_g, w_in, conv_w, a_log, dt_bias, dn_norm_g, w_mem_kv, w_br_dn, w_br_sb, w_br_mem, w_out, final_g, loss_target, m_norm_g, m_mem_norm_g, m_w_in, m_conv_w, m_a_log, m_dt_bias, m_dn_norm_g, m_w_mem_kv, m_w_br_dn, m_w_br_sb, m_w_br_mem, m_w_out, m_final_g, v_norm_g, v_mem_norm_g, v_w_in, v_conv_w, v_a_log, v_dt_bias, v_dn_norm_g, v_w_mem_kv, v_w_br_dn, v_w_br_sb, v_w_br_mem, v_w_out, v_final_g):
    w_a = w_in[0]
    w_b = _pack_b(w_mem_kv[0], w_br_dn[0], w_br_sb[0], w_br_mem[0], w_out[0])
    m_b = _pack_b(m_w_mem_kv[0], m_w_br_dn[0], m_w_br_sb[0], m_w_br_mem[0], m_w_out[0])
    v_b = _pack_b(v_w_mem_kv[0], v_w_br_dn[0], v_w_br_sb[0], v_w_br_mem[0], v_w_out[0])

    shard_idx = 2 * lax.axis_index("x") + lax.axis_index("y")
    shard = jnp.reshape(shard_idx, (1,)).astype(jnp.int32)
    ga, g_conv = _gather_shards([_cast_to_window(w_a, shard, "cast_w_in"),
                                 _cast_bf16(_conv_slab(conv_w[0]), "cast_conv")])
    w_r = _windows_to_w_r(ga)
    f_conv = g_conv[:, :3].reshape(N_SHARD, 4, 768).transpose(1, 0, 2).reshape(4, 3 * D_MODEL).astype(F32)
    b_flight = _shard_gather_start(_cast_bf16(w_b, "cast_w_b"), after=ga)

    def proj_weights(after):
        own, land = _shard_gather_wait(b_flight[0], b_flight[1], b_flight[2], b_flight[3], after)
        gb = lax.dynamic_update_slice(land, own[None], (shard_idx, 0, 0))
        return (gb[:, B_MEMKV:B_BRDN].reshape(N_SHARD * 256, 512),
                gb[:, B_BRDN:B_BRSB].reshape(N_SHARD * 256, D_MODEL),
                gb[:, B_BRSB:B_BRMEM].reshape(N_SHARD * 256, D_MODEL),
                gb[:, B_BRMEM:B_OUT].reshape(N_SHARD, 256, 256).transpose(1, 0, 2).reshape(256, D_MODEL),
                gb[:, B_OUT:B_CONV].reshape(N_SHARD * 256, D_MODEL))

    flights = {}

    def on_early(grads):
        g_b = jnp.concatenate([
            grads["w_mem_kv"].reshape(N_SHARD, 128, D_MODEL), grads["w_br_dn"].reshape(N_SHARD, 256, D_MODEL),
            grads["w_br_sb"].reshape(N_SHARD, 256, D_MODEL),
            grads["w_br_mem"].reshape(256, N_SHARD, 256).transpose(1, 0, 2).reshape(N_SHARD, 64, D_MODEL),
            grads["w_out"].reshape(N_SHARD, 256, D_MODEL)], axis=1).astype(BF16)
        flights["b"] = _reduce_scatter_start([g_b], "b")
        return flights["b"][4]

    loss, _, g_w_sh, input_grad = _local_step(
        x[0], mem[0], loss_target[0], norm_g, mem_norm_g, w_r, ga, f_conv, a_log, dt_bias, dn_norm_g,
        proj_weights, final_g, on_early=on_early, after_gather=b_flight[4])
    flights["a"] = _reduce_scatter_start([g_w_sh], "a")
    grad_x, small = input_grad(after=flights["a"][4])

    part = _pack_small(small["norm_g"], small["mem_norm_g"], small["final_g"], small["dn_norm_g"],
                       small["a_log"], small["dt_bias"], _conv_rows(small["conv_w"]), loss)
    w_s = _pack_small(norm_g, mem_norm_g, final_g, dn_norm_g, a_log, dt_bias, _conv_shard_rows(conv_w[0], shard_idx))
    m_s = _pack_small(m_norm_g, m_mem_norm_g, m_final_g, m_dn_norm_g, m_a_log, m_dt_bias,
                      _conv_shard_rows(m_conv_w[0], shard_idx))
    v_s = _pack_small(v_norm_g, v_mem_norm_g, v_final_g, v_dn_norm_g, v_a_log, v_dt_bias,
                      _conv_shard_rows(v_conv_w[0], shard_idx))
    g_s, d_s, nm_s, nv_s = _small_update(_allgather_small(part), w_s, m_s, v_s)

    (gs_b,) = _reduce_scatter_finish(flights["b"], after=g_s, tag="b")
    (gs_in,) = _reduce_scatter_finish(flights["a"], after=gs_b, tag="a")
    gr_in, d_in, nm_in, nv_in = _adamw_window(w_a, gs_in, m_w_in[0], v_w_in[0], shard, "adamw_w_in")
    gr_b, d_b, nm_b, nv_b = _adamw(w_b, gs_b, m_b, v_b, "adamw_b")

    def assemble(slab_small, a_in, slab_b):
        s_norm, s_memnorm, s_final, s_dnnorm, s_alog, s_dtb, b_conv = _unpack_small(slab_small, shard_idx)
        b_memkv, b_brdn, b_brsb, b_brmem, b_out = _unpack_b(slab_b)
        return [s_norm, s_memnorm, a_in.reshape(1, D_MODEL, IN_WIDTH // N_SHARD), b_conv, s_alog, s_dtb, s_dnnorm,
                b_memkv, b_brdn, b_brsb, b_brmem, b_out, s_final]

    outs = [g_s[S_LOSS, 0], grad_x.reshape(1, -1, D_MODEL)]
    outs += assemble(g_s, gr_in, gr_b)
    outs += assemble(d_s, d_in, d_b)
    outs += assemble(nm_s, nm_in, nm_b)
    outs += assemble(nv_s, nv_in, nv_b)
    return tuple(outs)
```

```python
import math

import jax
import jax.numpy as jnp
from jax import lax
from jax.experimental import pallas as pl
from jax.experimental.pallas import tpu as pltpu

F32 = jnp.float32
BF16 = jnp.bfloat16
MESH = pl.DeviceIdType.MESH

D_MODEL = 1024
N_HEADS = 8
D_HEAD = 128
DN_CHUNK = 64
DN_GROUP = 16
DN_SCAN_GROUP = 4
SB_BLOCK = 256
SB_HEADS_PER_STEP = 2
SB_QBLOCK = 256
MEM_HEADS = 4
MEM_DH = 64
MEM_W = MEM_HEADS * MEM_DH
NORM_EPS = 1e-6
IN_WIDTH = 11792
N_SHARD = 4
SHARD_W = IN_WIDTH // N_SHARD
SHARD_PAD = 3072
N_DEV = 8

C_DNZ = 3072
C_SBQ = 4096
C_SBZ = 7168
C_MQ = 8192
C_MZ = 8448
C_GATES = 8704
C_BA = 11776
W_R = 12288

ADAM_LR = 0.001
ADAM_B1 = 0.9
ADAM_B2 = 0.999
ADAM_EPS = 1e-08
ADAM_WD = 0.01
ADAM_STEP = 10

VMEM_LIMIT = 56 * 1024 * 1024

B_MEMKV, B_BRDN, B_BRSB, B_BRMEM, B_OUT, B_CONV = 0, 128, 384, 640, 704, 960
S_NORM, S_MEMNORM, S_FINAL, S_DNNORM, S_ALOG, S_DTB, S_LOSS, S_CONV, S_ROWS = 0, 8, 16, 24, 25, 26, 27, 32, 128
CONV_BLOCKS = 3 * D_MODEL // 128


def _cp(**kw):
    return pltpu.CompilerParams(vmem_limit_bytes=VMEM_LIMIT, **kw)


def _dot(a, b, dims):
    lead = a.ndim - 2
    ca, cb = {"nn": (1, 0), "nt": (1, 1), "tn": (0, 0)}[dims]
    batch = tuple(range(lead))
    return lax.dot_general(a, b, (((ca + lead,), (cb + lead,)), (batch, batch)), preferred_element_type=F32)


def _chunks(x):
    return x.reshape(x.shape[0] // DN_CHUNK, DN_CHUNK, x.shape[1])


def _unchunk(x):
    return x.reshape(x.shape[0] * x.shape[1], x.shape[2])


def _bdot(a, b, dims):
    return _dot(a.astype(BF16), b.astype(BF16), dims)


def _split(a):
    hi = a.astype(BF16)
    return hi, (a - hi.astype(F32)).astype(BF16)


def _dot3(a, b, dims):
    a1, a2 = _split(a)
    b1, b2 = _split(b)
    return _dot(a1, b1, dims) + (_dot(a1, b2, dims) + _dot(a2, b1, dims))


def _ones_dot(a, ones_bf16):
    out = _dot(a.reshape(-1, a.shape[-1]).astype(BF16), ones_bf16, "nn")
    return out.reshape(a.shape[:-1] + (ones_bf16.shape[1],))


def _sigmoid(x):
    return 1.0 / (1.0 + jnp.exp(-x))


def _log1p_small(u):
    return jnp.where(u < 1e-2, u * (1.0 - u * (0.5 - u * (1.0 / 3.0))), jnp.log(1.0 + u))


def _pick(dim, cands):
    for c in cands:
        if dim % c == 0:
            return c
    return dim


def _mm(a, b, dims, name, out_dtype=F32, out_shards=1, after=None):
    ta, tb = dims[0] == "t", dims[1] == "t"
    m, k = (a.shape[1], a.shape[0]) if ta else a.shape
    b_shards = b.shape[0] if b.ndim == 3 else 1
    n = b.shape[-2] if tb else b.shape[-1]
    tm = _pick(m, (1024, 512, 256))
    tn = _pick(n // out_shards, (512, 384, 256, 128))
    tk = _pick(k // b_shards, (2048, 1024, 512, 384, 256))
    nk = k // tk

    def body(a_ref, b_ref, *rest):
        if nk == 1:
            rest[-1][...] = _bdot(a_ref[...], b_ref[...], dims).astype(out_dtype)
            return
        o_ref, acc_ref = rest[-2:]
        kk = pl.program_id(2)

        @pl.when(kk == 0)
        def _():
            acc_ref[...] = jnp.zeros_like(acc_ref)

        acc_ref[...] += _bdot(a_ref[...], b_ref[...], dims)

        @pl.when(kk == nk - 1)
        def _():
            o_ref[...] = acc_ref[...].astype(out_dtype)

    a_spec = pl.BlockSpec((tk, tm), lambda i, j, q: (q, i)) if ta else pl.BlockSpec((tm, tk), lambda i, j, q: (i, q))
    if b_shards > 1:
        per_k = k // b_shards // tk
        b_spec = pl.BlockSpec((None, tn, tk), lambda i, j, q: (q // per_k, j, q % per_k))
    else:
        b_spec = pl.BlockSpec((tn, tk), lambda i, j, q: (j, q)) if tb else pl.BlockSpec((tk, tn), lambda i, j, q: (q, j))
    if out_shards > 1:
        per_n = n // out_shards // tn
        out_spec = pl.BlockSpec((None, tm, tn), lambda i, j, q: (j // per_n, i, j % per_n))
        out_shape = jax.ShapeDtypeStruct((out_shards, m, n // out_shards), out_dtype)
    else:
        out_spec = pl.BlockSpec((tm, tn), lambda i, j, q: (i, j))
        out_shape = jax.ShapeDtypeStruct((m, n), out_dtype)
    extra_specs, extra = [], []
    if after is not None:
        extra_specs, extra = [pl.BlockSpec(after.shape, lambda i, j, q: (0, 0))], [after]
    return pl.pallas_call(
        body, name=name, grid=(m // tm, n // tn, nk),
        in_specs=[a_spec, b_spec] + extra_specs, out_specs=out_spec, out_shape=out_shape,
        scratch_shapes=[pltpu.VMEM((tm, tn), F32)] if nk > 1 else [],
        compiler_params=_cp(dimension_semantics=("parallel", "parallel", "arbitrary")),
    )(a, b, *extra)


def _rmsnorm_fwd(x, g, name):
    t, d = x.shape
    tb = _pick(t, (512, 256))

    def body(x_ref, g_ref, h_ref):
        xv = x_ref[...]
        r = lax.rsqrt(jnp.mean(xv * xv, axis=-1, keepdims=True) + NORM_EPS)
        h_ref[...] = ((xv * r) * g_ref[...]).astype(BF16)

    return pl.pallas_call(
        body, name=name, grid=(t // tb,),
        in_specs=[pl.BlockSpec((tb, d), lambda i: (i, 0)), pl.BlockSpec((1, d), lambda i: (0, 0))],
        out_specs=pl.BlockSpec((tb, d), lambda i: (i, 0)),
        out_shape=jax.ShapeDtypeStruct((t, d), BF16), compiler_params=_cp(),
    )(x, g)


def _rmsnorm_bwd(x, g, dh, resid, name):
    t, d = x.shape
    tb = _pick(t, (256,))

    def body(x_ref, g_ref, dh_ref, r_ref, dx_ref, dg_ref):
        @pl.when(pl.program_id(0) == 0)
        def _():
            dg_ref[...] = jnp.zeros_like(dg_ref)

        xv = x_ref[...]
        r = lax.rsqrt(jnp.mean(xv * xv, axis=-1, keepdims=True) + NORM_EPS)
        xhat = xv * r
        dhv = dh_ref[...]
        dg_ref[...] += jnp.sum(dhv * xhat, axis=0, keepdims=True)
        dxh = dhv * g_ref[...]
        dx_ref[...] = r_ref[...] + r * (dxh - xhat * jnp.mean(dxh * xhat, axis=-1, keepdims=True))

    row = pl.BlockSpec((tb, d), lambda i: (i, 0))
    vec = pl.BlockSpec((1, d), lambda i: (0, 0))
    return pl.pallas_call(
        body, name=name, grid=(t // tb,), in_specs=[row, vec, row, row], out_specs=[row, vec],
        out_shape=[jax.ShapeDtypeStruct((t, d), F32), jax.ShapeDtypeStruct((1, d), F32)], compiler_params=_cp(),
    )(x, g, dh, resid)


def _conv_silu(xv, w, row):
    y = xv * w[3:4, :]
    for s in (1, 2, 3):
        xs = jnp.where(row >= s, pltpu.roll(xv, s, 0), 0.0)
        y = y + xs * w[3 - s:4 - s, :]
    sig = _sigmoid(y)
    return y, sig, y * sig


def _dn_prep_fwd(proj, conv_w):
    t = proj.shape[0]

    def body(p_ref, w_ref, o_ref):
        j = pl.program_id(0)
        xv = p_ref[...]
        row = lax.broadcasted_iota(jnp.int32, xv.shape, 0)
        _, _, a = _conv_silu(xv, w_ref[...], row)
        inv = lax.rsqrt(jnp.sum(a * a, axis=-1, keepdims=True) + NORM_EPS)
        scale = jnp.where(j < N_HEADS, D_HEAD ** -0.5, 1.0)
        normed = jnp.where(j < 2 * N_HEADS, 1.0, 0.0)
        o_ref[...] = a * (normed * (inv * scale) + (1.0 - normed))

    return pl.pallas_call(
        body, name="dn_prep_fwd", grid=(3 * N_HEADS,),
        in_specs=[pl.BlockSpec((t, D_HEAD), lambda j: (0, j)), pl.BlockSpec((4, D_HEAD), lambda j: (0, j))],
        out_specs=pl.BlockSpec((t, D_HEAD), lambda j: (0, j)),
        out_shape=jax.ShapeDtypeStruct((t, 3 * D_MODEL), F32), compiler_params=_cp(),
    )(proj, conv_w)


def _dn_prep_bwd(proj, conv_w, dq, dk, dv):
    t = proj.shape[0]

    def body(p_ref, w_ref, dq_ref, dk_ref, dv_ref, dp_ref, dw_ref):
        j = pl.program_id(0)
        xv = p_ref[...]
        w = w_ref[...]
        row = lax.broadcasted_iota(jnp.int32, xv.shape, 0)
        y, s, a = _conv_silu(xv, w, row)
        part = jnp.zeros(xv.shape, jnp.int32) + j // N_HEADS
        dn = jnp.where(part == 0, dq_ref[...], jnp.where(part == 1, dk_ref[...], dv_ref[...]))
        inv = lax.rsqrt(jnp.sum(a * a, axis=-1, keepdims=True) + NORM_EPS)
        scale = jnp.where(j < N_HEADS, D_HEAD ** -0.5, 1.0)
        ds = dn * scale
        da_norm = inv * ds - a * (inv * inv * inv) * jnp.sum(ds * a, axis=-1, keepdims=True)
        normed = jnp.where(j < 2 * N_HEADS, 1.0, 0.0)
        da = normed * da_norm + (1.0 - normed) * dn
        dy = da * (s * (1.0 + y * (1.0 - s)))
        dx = dy * w[3:4, :]
        dw_ref[3:4, :] = jnp.sum(dy * xv, axis=0, keepdims=True)
        for sft in (1, 2, 3):
            xs = jnp.where(row >= sft, pltpu.roll(xv, sft, 0), 0.0)
            dw_ref[3 - sft:4 - sft, :] = jnp.sum(dy * xs, axis=0, keepdims=True)
            dys = jnp.where(row < t - sft, pltpu.roll(dy, t - sft, 0), 0.0)
            dx = dx + dys * w[3 - sft:4 - sft, :]
        dp_ref[...] = dx.astype(BF16)

    blk = pl.BlockSpec((t, D_HEAD), lambda j: (0, j))
    wblk = pl.BlockSpec((4, D_HEAD), lambda j: (0, j))

    def grad(part):
        return pl.BlockSpec((t, D_HEAD), lambda j: (0, jnp.clip(j - part * N_HEADS, 0, N_HEADS - 1)))

    return pl.pallas_call(
        body, name="dn_prep_bwd", grid=(3 * N_HEADS,), in_specs=[blk, wblk, grad(0), grad(1), grad(2)],
        out_specs=[blk, wblk],
        out_shape=[jax.ShapeDtypeStruct((t, 3 * D_MODEL), BF16), jax.ShapeDtypeStruct((4, 3 * D_MODEL), F32)],
        compiler_params=_cp(),
    )(proj, conv_w, dq, dk, dv)


def _softplus_parts(xv):
    e = jnp.exp(-jnp.abs(xv))
    return jnp.maximum(xv, 0.0) + _log1p_small(e)


def _chunk_scan(v, row, reverse):
    t = v.shape[0]
    pos = row & (DN_CHUNK - 1)
    s = 1
    while s < DN_CHUNK:
        if reverse:
            v = v + jnp.where(pos < DN_CHUNK - s, pltpu.roll(v, t - s, 0), 0.0)
        else:
            v = v + jnp.where(pos >= s, pltpu.roll(v, s, 0), 0.0)
        s *= 2
    return v


def _dn_gate_fwd(proj, alog_row, dtb_row):
    t = proj.shape[0]

    def body(p_ref, al_ref, dt_ref, b_ref, g_ref):
        p = p_ref[...]
        row = lax.broadcasted_iota(jnp.int32, p.shape, 0)
        b_ref[...] = _sigmoid(p)
        g = -jnp.exp(al_ref[...]) * _softplus_parts(p + dt_ref[...])
        g_ref[...] = _chunk_scan(g, row, reverse=False)

    blk = pl.BlockSpec((t, 128), lambda i: (0, C_BA // 128))
    vec = pl.BlockSpec((1, 128), lambda i: (0, 0))
    out = pl.BlockSpec((t, 128), lambda i: (0, 0))
    return pl.pallas_call(
        body, name="dn_gate_fwd", grid=(1,), in_specs=[blk, vec, vec], out_specs=[out, out],
        out_shape=[jax.ShapeDtypeStruct((t, 128), F32)] * 2, compiler_params=_cp(),
    )(proj, alog_row, dtb_row)


def _dn_gate_bwd(proj, alog_row, dtb_row, dbeta, dgc):
    t = proj.shape[0]

    def body(p_ref, al_ref, dt_ref, db_ref, dg_ref, dp_ref, dal_ref, ddt_ref):
        p = p_ref[...]
        row = lax.broadcasted_iota(jnp.int32, p.shape, 0)
        lane = lax.broadcasted_iota(jnp.int32, p.shape, 1)
        s = _sigmoid(p)
        d_b = db_ref[...] * s * (1.0 - s)
        dg = _chunk_scan(dg_ref[...], row, reverse=True)
        xa = p + dt_ref[...]
        ea = jnp.exp(al_ref[...])
        g = -ea * _softplus_parts(xa)
        d_a = dg * (-ea) * _sigmoid(xa)
        dp_ref[...] = jnp.where(lane < N_HEADS, d_b, jnp.where(lane < 2 * N_HEADS, d_a, 0.0)).astype(BF16)
        dal_ref[...] = jnp.sum(dg * g, axis=0, keepdims=True)
        ddt_ref[...] = jnp.sum(d_a, axis=0, keepdims=True)

    blk = pl.BlockSpec((t, 128), lambda i: (0, C_BA // 128))
    vec = pl.BlockSpec((1, 128), lambda i: (0, 0))
    full = pl.BlockSpec((t, 128), lambda i: (0, 0))
    return pl.pallas_call(
        body, name="dn_gate_bwd", grid=(1,), in_specs=[blk, vec, vec, full, full], out_specs=[full, vec, vec],
        out_shape=[jax.ShapeDtypeStruct((t, 128), BF16), jax.ShapeDtypeStruct((1, 128), F32),
                   jax.ShapeDtypeStruct((1, 128), F32)], compiler_params=_cp(),
    )(proj, alog_row, dtb_row, dbeta, dgc)


def _col_to_row(col, eye):
    return jnp.sum(jnp.where(eye, col, 0.0), axis=-2, keepdims=True)


def _row_to_col(rowv, eye):
    return jnp.sum(jnp.where(eye, rowv, 0.0), axis=-1, keepdims=True)


def _tri_inverse(m, ri, ci):
    eye = (ri == ci).astype(F32)
    b16 = (ri >> 4) == (ci >> 4)
    b32 = (ri >> 5) == (ci >> 5)
    m1 = jnp.where(b16, m, 0.0)
    x = eye - m1
    p = _dot3(m1, m1, "nn")
    x = x + _dot3(x, p, "nn")
    p = _dot3(p, p, "nn")
    x = x + _dot3(x, p, "nn")
    p = _dot3(p, p, "nn")
    x = x + _dot3(x, p, "nn")
    c1 = jnp.where(jnp.logical_and(b32, jnp.logical_not(b16)), m, 0.0)
    x = x - _dot3(_dot3(x, c1, "nn"), x, "nn")
    c2 = jnp.where(b32, 0.0, m)
    x = x - _dot3(_dot3(x, c2, "nn"), x, "nn")
    return x


def _dn_chunk_common(q, k, gc, ri, ci):
    eye = ri == ci
    g_row = _col_to_row(gc, eye)
    diff = jnp.minimum(gc - g_row, 0.0)
    gam = jnp.where(ri >= ci, jnp.exp(diff), 0.0)
    kk = _bdot(k, k, "nt")
    qk = _bdot(q, k, "nt")
    rcol = lax.broadcasted_iota(jnp.int32, gc.shape, gc.ndim - 2)
    last = jnp.sum(jnp.where(rcol == DN_CHUNK - 1, gc, 0.0), axis=-2, keepdims=True)
    e_g = jnp.exp(gc)
    dec = jnp.exp(last - gc)
    return eye, gam, kk, qk, last, e_g, dec, rcol


def _dn_specs(t, rows_blk):
    def head(off):
        return pl.BlockSpec((rows_blk, D_HEAD), lambda g, h: (g, off + h))

    lanes = pl.BlockSpec((rows_blk, 128), lambda g, h: (g, 0))
    hm = pl.BlockSpec((None, rows_blk, D_HEAD), lambda g, h: (h, g, 0))
    sq = pl.BlockSpec((1, rows_blk, DN_CHUNK), lambda g, h: (h, g, 0))
    tile = pl.BlockSpec((1, rows_blk // DN_CHUNK, 8, 128), lambda g, h: (h, g, 0, 0))
    return head, lanes, hm, sq, tile


def _head_column(slab, lane_idx):
    lane = lax.broadcasted_iota(jnp.int32, slab.shape, 1)
    return _chunks(jnp.sum(jnp.where(lane == lane_idx, slab, 0.0), axis=1, keepdims=True))


def _dn_intra_fwd(qkv, beta_t, g_t):
    t = qkv.shape[0]
    n_chunks = t // DN_CHUNK
    rows_blk = min(DN_GROUP * DN_CHUNK, t)

    def body(q_ref, k_ref, v_ref, b_ref, g_ref, u_ref, w_ref, qd_ref, kd_ref, a_ref, ti_ref, el_ref):
        ri = lax.broadcasted_iota(jnp.int32, (DN_CHUNK, DN_CHUNK), 0)
        ci = lax.broadcasted_iota(jnp.int32, (DN_CHUNK, DN_CHUNK), 1)
        h = pl.program_id(1)
        q, k, v = (_chunks(r[...]) for r in (q_ref, k_ref, v_ref))
        b, gc = _head_column(b_ref[...], h), _head_column(g_ref[...], h + N_HEADS)
        _, gam, kk, qk, last, e_g, dec, _ = _dn_chunk_common(q, k, gc, ri, ci)
        tinv = _tri_inverse(jnp.where(ri > ci, b * kk * gam, 0.0), ri, ci)
        u_ref[...] = _unchunk(_bdot(tinv, v * b, "nn"))
        w_ref[...] = _unchunk(_bdot(tinv, k * (b * e_g), "nn"))
        qd_ref[...] = _unchunk(q * e_g)
        kd_ref[...] = _unchunk(k * dec)
        a_ref[0] = _unchunk(qk * gam)
        ti_ref[0] = _unchunk(tinv)
        el_ref[0] = jnp.broadcast_to(jnp.exp(last), (rows_blk // DN_CHUNK, 8, 128))

    head, lanes, hm, sq, tile = _dn_specs(t, rows_blk)
    act = jax.ShapeDtypeStruct((N_HEADS, t, D_HEAD), F32)
    sqs = jax.ShapeDtypeStruct((N_HEADS, t, DN_CHUNK), F32)
    return pl.pallas_call(
        body, name="dn_intra_fwd", grid=(t // rows_blk, N_HEADS),
        in_specs=[head(0), head(N_HEADS), head(2 * N_HEADS), lanes, lanes],
        out_specs=[hm] * 4 + [sq, sq, tile],
        out_shape=[act] * 4 + [sqs, sqs, jax.ShapeDtypeStruct((N_HEADS, n_chunks, 8, 128), F32)],
        compiler_params=_cp(),
    )(qkv, qkv, qkv, beta_t, g_t)


def _dn_scan_specs(t, rows_blk, reverse):
    n_groups = t // rows_blk

    def at(g):
        return n_groups - 1 - g if reverse else g

    per = rows_blk // DN_CHUNK
    act = pl.BlockSpec((N_HEADS, rows_blk, D_HEAD), lambda g: (0, at(g), 0))
    sq = pl.BlockSpec((N_HEADS, rows_blk, DN_CHUNK), lambda g: (0, at(g), 0))
    state = pl.BlockSpec((N_HEADS, per, D_HEAD, D_HEAD), lambda g: (0, at(g), 0, 0))
    tile = pl.BlockSpec((N_HEADS, per, 8, 128), lambda g: (0, at(g), 0, 0))
    return act, sq, state, tile


def _dn_scan_fwd(u, w, qd, kd, a, el):
    t = u.shape[1]
    n_chunks = t // DN_CHUNK
    rows_blk = DN_SCAN_GROUP * DN_CHUNK

    def body(u_ref, w_ref, qd_ref, kd_ref, a_ref, el_ref, o_ref, vn_ref, s_ref, s_scr):
        @pl.when(pl.program_id(0) == 0)
        def _():
            s_scr[...] = jnp.zeros_like(s_scr)

        for cc in range(DN_SCAN_GROUP):
            rows = slice(cc * DN_CHUNK, (cc + 1) * DN_CHUNK)
            s = s_scr[...]
            s_ref[:, cc] = s
            v_new = u_ref[:, rows, :] - _bdot(w_ref[:, rows, :], s, "nn")
            vn_ref[:, rows, :] = v_new
            o_ref[:, rows, :] = _bdot(qd_ref[:, rows, :], s, "nn") + _bdot(a_ref[:, rows, :], v_new, "nn")
            s_scr[...] = s * el_ref[:, cc][:, 0:1, :] + _bdot(kd_ref[:, rows, :], v_new, "tn")

    act, sq, state, tile = _dn_scan_specs(t, rows_blk, reverse=False)
    shp = jax.ShapeDtypeStruct((N_HEADS, t, D_HEAD), F32)
    return pl.pallas_call(
        body, name="dn_scan_fwd", grid=(t // rows_blk,),
        in_specs=[act, act, act, act, sq, tile], out_specs=[act, act, state],
        out_shape=[shp, shp, jax.ShapeDtypeStruct((N_HEADS, n_chunks, D_HEAD, D_HEAD), F32)],
        scratch_shapes=[pltpu.VMEM((N_HEADS, D_HEAD, D_HEAD), F32)],
        compiler_params=_cp(dimension_semantics=("arbitrary",)),
    )(u, w, qd, kd, a, el)


def _dn_scan_bwd(w, qd, kd, a, el, vn, s_all, do):
    t = w.shape[1]
    n_chunks = t // DN_CHUNK
    rows_blk = DN_SCAN_GROUP * DN_CHUNK

    def body(w_ref, qd_ref, kd_ref, a_ref, el_ref, vn_ref, s_ref, do_ref, dvn_ref, dkd_ref, dqd_ref, dw_ref, dl_ref, ds_scr):
        @pl.when(pl.program_id(0) == 0)
        def _():
            ds_scr[...] = jnp.zeros_like(ds_scr)

        for cc in reversed(range(DN_SCAN_GROUP)):
            rows = slice(cc * DN_CHUNK, (cc + 1) * DN_CHUNK)
            s = s_ref[:, cc]
            d_s = ds_scr[...]
            e_last = el_ref[:, cc][:, 0:1, :]
            d_o = do_ref[:, rows, :]
            dv_new = _bdot(a_ref[:, rows, :], d_o, "tn") + _bdot(kd_ref[:, rows, :], d_s, "nn")
            ds_scr[...] = d_s * e_last + _bdot(qd_ref[:, rows, :], d_o, "tn") - _bdot(w_ref[:, rows, :], dv_new, "tn")
            dvn_ref[:, rows, :] = dv_new
            dkd_ref[:, rows, :] = _bdot(vn_ref[:, rows, :], d_s, "nt")
            dqd_ref[:, rows, :] = _bdot(d_o, s, "nt")
            dw_ref[:, rows, :] = -_bdot(dv_new, s, "nt")
            dlast = jnp.sum(jnp.sum(d_s * s, axis=2, keepdims=True), axis=1, keepdims=True)
            dl_ref[:, cc] = jnp.broadcast_to(dlast * e_last, (N_HEADS, 8, 128))

    act, sq, state, tile = _dn_scan_specs(t, rows_blk, reverse=True)
    shp = jax.ShapeDtypeStruct((N_HEADS, t, D_HEAD), F32)
    return pl.pallas_call(
        body, name="dn_scan_bwd", grid=(t // rows_blk,),
        in_specs=[act, act, act, sq, tile, act, state, act], out_specs=[act] * 4 + [tile],
        out_shape=[shp] * 4 + [jax.ShapeDtypeStruct((N_HEADS, n_chunks, 8, 128), F32)],
        scratch_shapes=[pltpu.VMEM((N_HEADS, D_HEAD, D_HEAD), F32)],
        compiler_params=_cp(dimension_semantics=("arbitrary",)),
    )(w, qd, kd, a, el, vn, s_all, do)


def _dn_intra_bwd(qkv, beta_t, g_t, tinv_all, vn, do, dvn, dkd, dqd, dw, dl):
    t = qkv.shape[0]
    rows_blk = min(DN_GROUP * DN_CHUNK, t)

    def body(q_ref, k_ref, v_ref, b_ref, g_ref, ti_ref, vn_ref, do_ref, dvn_ref, dkd_ref, dqd_ref, dw_ref, dl_ref,
             dq_ref, dk_ref, dv_ref, db_ref, dg_ref):
        ri = lax.broadcasted_iota(jnp.int32, (DN_CHUNK, DN_CHUNK), 0)
        ci = lax.broadcasted_iota(jnp.int32, (DN_CHUNK, DN_CHUNK), 1)
        h = pl.program_id(1)
        q, k, v = (_chunks(r[...]) for r in (q_ref, k_ref, v_ref))
        b, gc = _head_column(b_ref[...], h), _head_column(g_ref[...], h + N_HEADS)
        tinv = _chunks(ti_ref[0])
        dv_new, dk_dec, dq_dec, d_w = (_chunks(r[...]) for r in (dvn_ref, dkd_ref, dqd_ref, dw_ref))
        eye, gam, kk, qk, _, e_g, dec, rcol = _dn_chunk_common(q, k, gc, ri, ci)
        bv = v * b
        bk = k * (b * e_g)

        d_a = jnp.where(ri >= ci, _bdot(_chunks(do_ref[...]), _chunks(vn_ref[...]), "nt"), 0.0)
        dbv = _bdot(tinv, dv_new, "tn")
        dbk = _bdot(tinv, d_w, "tn")
        d_tinv = _bdot(dv_new, bv, "nt") + _bdot(d_w, bk, "nt")
        d_m = -jnp.where(ri > ci, _dot3(_dot3(tinv, d_tinv, "tn"), tinv, "nt"), 0.0)

        d_kk = d_m * b * gam
        d_gam = d_m * b * kk + d_a * qk
        d_qk = d_a * gam
        dq_ref[...] = _unchunk(_bdot(d_qk, k, "nn") + dq_dec * e_g)
        dk_ref[...] = _unchunk(_bdot(d_qk, q, "tn") + _bdot(d_kk, k, "nn") + _bdot(d_kk, k, "tn")
                               + dk_dec * dec + dbk * (b * e_g))
        dv_ref[...] = _unchunk(dbv * b)
        d_b = _unchunk(jnp.sum(d_m * kk * gam, axis=-1, keepdims=True) + jnp.sum(dbv * v, axis=-1, keepdims=True)
                       + jnp.sum(dbk * k, axis=-1, keepdims=True) * e_g)

        xg = d_gam * gam
        kdk = jnp.sum(dk_dec * (k * dec), axis=-1, keepdims=True)
        d_gc = (jnp.sum(xg, axis=-1, keepdims=True) - _row_to_col(jnp.sum(xg, axis=-2, keepdims=True), eye)
                + jnp.sum(dq_dec * (q * e_g), axis=-1, keepdims=True) - kdk
                + jnp.sum(dbk * bk, axis=-1, keepdims=True))
        d_last_total = dl_ref[0][:, 0:1, 0:1] + jnp.sum(kdk, axis=-2, keepdims=True)
        d_g = _unchunk(d_gc + jnp.where(rcol == DN_CHUNK - 1, d_last_total, 0.0))

        @pl.when(h == 0)
        def _():
            db_ref[...] = jnp.zeros_like(db_ref)
            dg_ref[...] = jnp.zeros_like(dg_ref)

        lane = lax.broadcasted_iota(jnp.int32, db_ref.shape, 1)
        db_ref[...] += jnp.where(lane == h, d_b, 0.0)
        dg_ref[...] += jnp.where(lane == h + N_HEADS, d_g, 0.0)

    head, lanes, hm, sq, tile = _dn_specs(t, rows_blk)
    return pl.pallas_call(
        body, name="dn_intra_bwd", grid=(t // rows_blk, N_HEADS),
        in_specs=[head(0), head(N_HEADS), head(2 * N_HEADS), lanes, lanes, sq] + [hm] * 6 + [tile],
        out_specs=[head(0), head(0), head(0), lanes, lanes],
        out_shape=[jax.ShapeDtypeStruct((t, D_MODEL), F32)] * 3 + [jax.ShapeDtypeStruct((t, 128), F32)] * 2,
        compiler_params=_cp(),
    )(qkv, qkv, qkv, beta_t, g_t, tinv_all, vn, do, dvn, dkd, dqd, dw, dl)


def _dn_post_fwd(o, proj, gn):
    t = o.shape[1]

    def body(o_ref, z_ref, g_ref, out_ref):
        ov, z = o_ref[...], z_ref[...]
        r = lax.rsqrt(jnp.mean(ov * ov, axis=-1, keepdims=True) + NORM_EPS)
        out_ref[...] = (((ov * r) * g_ref[...]) * (z * _sigmoid(z))).astype(BF16)

    blk = pl.BlockSpec((t, D_HEAD), lambda h: (0, h))
    return pl.pallas_call(
        body, name="dn_post_fwd", grid=(N_HEADS,),
        in_specs=[pl.BlockSpec((None, t, D_HEAD), lambda h: (h, 0, 0)),
                  pl.BlockSpec((t, D_HEAD), lambda h: (0, C_DNZ // D_HEAD + h)),
                  pl.BlockSpec((1, D_HEAD), lambda h: (0, 0))],
        out_specs=blk, out_shape=jax.ShapeDtypeStruct((t, D_MODEL), BF16), compiler_params=_cp(),
    )(o, proj, gn)


def _dn_post_bwd(o, proj, gn, dout):
    t = o.shape[1]

    def body(o_ref, z_ref, g_ref, d_ref, do_ref, dz_ref, dg_ref):
        @pl.when(pl.program_id(0) == 0)
        def _():
            dg_ref[...] = jnp.zeros_like(dg_ref)

        ov, z, d = o_ref[...], z_ref[...], d_ref[...]
        r = lax.rsqrt(jnp.mean(ov * ov, axis=-1, keepdims=True) + NORM_EPS)
        ohat = ov * r
        s = _sigmoid(z)
        d_on = d * (z * s)
        dz_ref[...] = (d * (ohat * g_ref[...]) * (s * (1.0 + z * (1.0 - s)))).astype(BF16)
        dg_ref[...] += jnp.sum(d_on * ohat, axis=0, keepdims=True)
        dxh = d_on * g_ref[...]
        do_ref[...] = r * (dxh - ohat * jnp.mean(dxh * ohat, axis=-1, keepdims=True))

    blk = pl.BlockSpec((t, D_HEAD), lambda h: (0, h))
    hm = pl.BlockSpec((None, t, D_HEAD), lambda h: (h, 0, 0))
    vec = pl.BlockSpec((1, D_HEAD), lambda h: (0, 0))
    return pl.pallas_call(
        body, name="dn_post_bwd", grid=(N_HEADS,),
        in_specs=[hm, pl.BlockSpec((t, D_HEAD), lambda h: (0, C_DNZ // D_HEAD + h)), vec, blk],
        out_specs=[hm, blk, vec],
        out_shape=[jax.ShapeDtypeStruct((N_HEADS, t, D_HEAD), F32), jax.ShapeDtypeStruct((t, D_MODEL), BF16),
                   jax.ShapeDtypeStruct((1, D_HEAD), F32)], compiler_params=_cp(),
    )(o, proj, gn, dout)


def _sb_fwd(proj):
    t = proj.shape[0]
    qblk = min(SB_QBLOCK, t)
    scale = 1.0 / math.sqrt(D_HEAD)

    hp = SB_HEADS_PER_STEP
    wid = hp * D_HEAD

    def body(q_ref, k_ref, v_ref, z_ref, o_ref, og_ref, l_ref, qb, kb, vb):
        for hh in range(hp):
            hs = slice(hh * D_HEAD, (hh + 1) * D_HEAD)
            qb[hh] = q_ref[:, hs].astype(BF16)
            kb[hh] = k_ref[:, hs].astype(BF16)
            vb[hh] = v_ref[:, hs].astype(BF16)
        ri = lax.broadcasted_iota(jnp.int32, (qblk, SB_BLOCK), 0)
        ci = lax.broadcasted_iota(jnp.int32, (qblk, SB_BLOCK), 1)
        r2 = lax.broadcasted_iota(jnp.int32, (SB_BLOCK, SB_BLOCK), 0)
        c2 = lax.broadcasted_iota(jnp.int32, (SB_BLOCK, SB_BLOCK), 1)
        upper = (r2 > c2).astype(BF16)
        nkb = qblk // SB_BLOCK

        def qblock(i, carry):
            rows = pl.ds(pl.multiple_of(i * qblk, qblk), qblk)
            qi = qb[:, rows, :]

            def tile(j, st, on_diagonal):
                acc, c = st
                cols = pl.ds(pl.multiple_of(j * SB_BLOCK, SB_BLOCK), SB_BLOCK)
                z = _dot(qi, kb[:, cols, :], "nt") * scale
                lb = jnp.minimum(z, 0.0) - jnp.log(1.0 + jnp.exp(-jnp.abs(z)))
                lf = lb - z
                if on_diagonal:
                    mask = (j * SB_BLOCK + ci) < (i * qblk + ri)
                    lf = jnp.where(mask, lf, 0.0)
                att = jnp.exp(lb + (_ones_dot(lf, upper) + c))
                if on_diagonal:
                    att = jnp.where(mask, att, 0.0)
                acc = acc + _dot(att.astype(BF16), vb[:, cols, :], "nn")
                return acc, c + jnp.sum(lf, axis=-1, keepdims=True)

            st = (jnp.zeros((hp, qblk, D_HEAD), F32), jnp.zeros((hp, qblk, 1), F32))
            for d in range(nkb):
                st = tile((i + 1) * nkb - 1 - d, st, True)
            acc, c = lax.fori_loop(0, i * nkb, lambda jj, s: tile(i * nkb - 1 - jj, s, False), st)
            l_ref[:, rows, :] = c
            for hh in range(hp):
                hs = slice(hh * D_HEAD, (hh + 1) * D_HEAD)
                zg = z_ref[rows, hs]
                o_ref[rows, hs] = acc[hh]
                og_ref[rows, hs] = (acc[hh] * (zg * _sigmoid(zg))).astype(BF16)
            return carry

        lax.fori_loop(0, t // qblk, qblock, 0)

    def head(off):
        return pl.BlockSpec((t, wid), lambda h: (0, off // wid + h))

    out = pl.BlockSpec((t, wid), lambda h: (0, h))
    return pl.pallas_call(
        body, name="sb_fwd", grid=(N_HEADS // hp,),
        in_specs=[head(C_SBQ), head(C_SBQ + D_MODEL), head(C_SBQ + 2 * D_MODEL), head(C_SBZ)],
        out_specs=[out, out, pl.BlockSpec((hp, t, 1), lambda h: (h, 0, 0))],
        out_shape=[jax.ShapeDtypeStruct((t, D_MODEL), F32), jax.ShapeDtypeStruct((t, D_MODEL), BF16),
                   jax.ShapeDtypeStruct((N_HEADS, t, 1), F32)],
        scratch_shapes=[pltpu.VMEM((hp, t, D_HEAD), BF16)] * 3, compiler_params=_cp(),
    )(proj, proj, proj, proj)


def _sb_bwd(proj, o, ltot, dog, after=None):
    t = proj.shape[0]
    qblk = min(SB_QBLOCK, t)
    scale = 1.0 / math.sqrt(D_HEAD)

    hp = SB_HEADS_PER_STEP
    wid = hp * D_HEAD

    def body(q_ref, k_ref, v_ref, z_ref, o_ref, l_ref, d_ref, *rest):
        dq_ref, dk_ref, dv_ref, dz_ref, qb, kb, vb, dob, dk_scr, dv_scr = rest[-10:]
        for hh in range(hp):
            hs = slice(hh * D_HEAD, (hh + 1) * D_HEAD)
            qb[hh] = q_ref[:, hs].astype(BF16)
            kb[hh] = k_ref[:, hs].astype(BF16)
            vb[hh] = v_ref[:, hs].astype(BF16)
            zg = z_ref[:, hs]
            sg = _sigmoid(zg)
            dgo = d_ref[:, hs]
            dob[hh] = (dgo * (zg * sg)).astype(BF16)
            dz_ref[:, hs] = (dgo * o_ref[:, hs] * (sg * (1.0 + zg * (1.0 - sg)))).astype(BF16)
        dk_scr[...] = jnp.zeros_like(dk_scr)
        dv_scr[...] = jnp.zeros_like(dv_scr)
        ri = lax.broadcasted_iota(jnp.int32, (qblk, SB_BLOCK), 0)
        ci = lax.broadcasted_iota(jnp.int32, (qblk, SB_BLOCK), 1)
        r2 = lax.broadcasted_iota(jnp.int32, (SB_BLOCK, SB_BLOCK), 0)
        c2 = lax.broadcasted_iota(jnp.int32, (SB_BLOCK, SB_BLOCK), 1)
        upper = (r2 > c2).astype(BF16)
        below = (r2 < c2).astype(BF16)

        def qblock(i, carry):
            rows = pl.ds(pl.multiple_of(i * qblk, qblk), qblk)
            qi = qb[:, rows, :]
            d_o = dob[:, rows, :]
            ltot = l_ref[:, rows, :]

            def tile(j, st, on_diagonal):
                dq, cpre, ce = st
                cols = pl.ds(pl.multiple_of(j * SB_BLOCK, SB_BLOCK), SB_BLOCK)
                kj, vj = kb[:, cols, :], vb[:, cols, :]
                z = _dot(qi, kj, "nt") * scale
                lb = jnp.minimum(z, 0.0) - jnp.log(1.0 + jnp.exp(-jnp.abs(z)))
                lf = lb - z
                if on_diagonal:
                    mask = (j * SB_BLOCK + ci) < (i * qblk + ri)
                    lf = jnp.where(mask, lf, 0.0)
                tile_sum = jnp.sum(lf, axis=-1, keepdims=True)
                att = jnp.exp(lb + ((ltot - cpre - tile_sum) + _ones_dot(lf, upper)))
                if on_diagonal:
                    att = jnp.where(mask, att, 0.0)
                e = _dot(d_o, vj, "nt") * att
                dlf = ce + _ones_dot(e, below)
                dzz = e - (e + dlf) * jnp.exp(lb)
                if on_diagonal:
                    dzz = jnp.where(mask, dzz, 0.0)
                dzz = dzz.astype(BF16)
                dq = dq + _dot(dzz, kj, "nn")
                dk_scr[:, cols, :] += _dot(dzz, qi, "tn")
                dv_scr[:, cols, :] += _dot(att.astype(BF16), d_o, "tn")
                return dq, cpre + tile_sum, ce + jnp.sum(e, axis=-1, keepdims=True)

            nkb = qblk // SB_BLOCK
            zero_col = jnp.zeros((hp, qblk, 1), F32)
            st = lax.fori_loop(0, i * nkb, lambda j, s: tile(j, s, False),
                               (jnp.zeros((hp, qblk, D_HEAD), F32), zero_col, zero_col))
            for d in range(nkb):
                st = tile(i * nkb + d, st, True)
            dq = st[0]
            for hh in range(hp):
                dq_ref[rows, hh * D_HEAD:(hh + 1) * D_HEAD] = (dq[hh] * scale).astype(BF16)
            return carry

        lax.fori_loop(0, t // qblk, qblock, 0)
        for hh in range(hp):
            hs = slice(hh * D_HEAD, (hh + 1) * D_HEAD)
            dk_ref[:, hs] = (dk_scr[hh] * scale).astype(BF16)
            dv_ref[:, hs] = dv_scr[hh].astype(BF16)

    def head(off):
        return pl.BlockSpec((t, wid), lambda h: (0, off // wid + h))

    extra_specs, extra = [], []
    if after is not None:
        extra_specs, extra = [pl.BlockSpec(after.shape, lambda h: (0, 0))], [after]
    return pl.pallas_call(
        body, name="sb_bwd", grid=(N_HEADS // hp,),
        in_specs=[head(C_SBQ), head(C_SBQ + D_MODEL), head(C_SBQ + 2 * D_MODEL), head(C_SBZ), head(0),
                  pl.BlockSpec((hp, t, 1), lambda h: (h, 0, 0)), head(0)] + extra_specs,
        out_specs=[head(0)] * 4, out_shape=[jax.ShapeDtypeStruct((t, D_MODEL), BF16)] * 4,
        scratch_shapes=[pltpu.VMEM((hp, t, D_HEAD), BF16)] * 4 + [pltpu.VMEM((hp, t, D_HEAD), F32)] * 2,
        compiler_params=_cp(),
    )(proj, proj, proj, proj, o, ltot, dog, *extra)


def _mem_fwd(proj, mkv):
    t = proj.shape[0]
    tq = _pick(t, (512, 256))
    m_len = mkv.shape[0]
    scale = 1.0 / math.sqrt(MEM_DH)

    def body(q_ref, z_ref, kv_ref, o_ref, og_ref):
        q = q_ref[...]
        mk = kv_ref[:, :MEM_W].astype(BF16)
        mv = kv_ref[:, MEM_W:].astype(BF16)
        lane = lax.broadcasted_iota(jnp.int32, q.shape, 1) >> 6
        o = jnp.zeros(q.shape, F32)
        for h in range(MEM_HEADS):
            s = _bdot(jnp.where(lane == h, q, 0.0), mk, "nt") * scale
            p = jnp.exp(s - jnp.max(s, axis=-1, keepdims=True))
            p = p / jnp.sum(p, axis=-1, keepdims=True)
            o = o + jnp.where(lane == h, _bdot(p, mv, "nn"), 0.0)
        z = z_ref[...]
        o_ref[...] = o
        og_ref[...] = (o * (z * _sigmoid(z))).astype(BF16)

    out = pl.BlockSpec((tq, MEM_W), lambda i: (i, 0))
    return pl.pallas_call(
        body, name="mem_fwd", grid=(t // tq,),
        in_specs=[pl.BlockSpec((tq, MEM_W), lambda i: (i, C_MQ // MEM_W)),
                  pl.BlockSpec((tq, MEM_W), lambda i: (i, C_MZ // MEM_W)),
                  pl.BlockSpec((m_len, 2 * MEM_W), lambda i: (0, 0))],
        out_specs=[out, out],
        out_shape=[jax.ShapeDtypeStruct((t, MEM_W), F32), jax.ShapeDtypeStruct((t, MEM_W), BF16)],
        compiler_params=_cp(),
    )(proj, proj, mkv)


def _mem_bwd(proj, mkv, o, dog):
    t = proj.shape[0]
    tq = _pick(t, (512, 256))
    m_len = mkv.shape[0]
    scale = 1.0 / math.sqrt(MEM_DH)

    def body(q_ref, z_ref, kv_ref, o_ref, d_ref, dq_ref, dz_ref, dkv_ref):
        @pl.when(pl.program_id(0) == 0)
        def _():
            dkv_ref[...] = jnp.zeros_like(dkv_ref)

        q = q_ref[...]
        z = z_ref[...]
        sg = _sigmoid(z)
        dgo = d_ref[...]
        d_o = dgo * (z * sg)
        dz_ref[...] = (dgo * o_ref[...] * (sg * (1.0 + z * (1.0 - sg)))).astype(BF16)
        mk = kv_ref[:, :MEM_W].astype(BF16)
        mv = kv_ref[:, MEM_W:].astype(BF16)
        lane = lax.broadcasted_iota(jnp.int32, q.shape, 1) >> 6
        klane = lax.broadcasted_iota(jnp.int32, (m_len, MEM_W), 1) >> 6
        dq = jnp.zeros(q.shape, F32)
        dmk = jnp.zeros((m_len, MEM_W), F32)
        dmv = jnp.zeros((m_len, MEM_W), F32)
        for h in range(MEM_HEADS):
            qh = jnp.where(lane == h, q, 0.0)
            doh = jnp.where(lane == h, d_o, 0.0)
            s = _bdot(qh, mk, "nt") * scale
            p = jnp.exp(s - jnp.max(s, axis=-1, keepdims=True))
            p = p / jnp.sum(p, axis=-1, keepdims=True)
            dp = _bdot(doh, mv, "nt")
            ds = p * (dp - jnp.sum(dp * p, axis=-1, keepdims=True)) * scale
            dq = dq + jnp.where(lane == h, _bdot(ds, mk, "nn"), 0.0)
            dmk = dmk + jnp.where(klane == h, _bdot(ds, qh, "tn"), 0.0)
            dmv = dmv + jnp.where(klane == h, _bdot(p, doh, "tn"), 0.0)
        dq_ref[...] = dq.astype(BF16)
        dkv_ref[:, :MEM_W] += dmk
        dkv_ref[:, MEM_W:] += dmv

    blk = pl.BlockSpec((tq, MEM_W), lambda i: (i, 0))
    kv = pl.BlockSpec((m_len, 2 * MEM_W), lambda i: (0, 0))
    return pl.pallas_call(
        body, name="mem_bwd", grid=(t // tq,),
        in_specs=[pl.BlockSpec((tq, MEM_W), lambda i: (i, C_MQ // MEM_W)),
                  pl.BlockSpec((tq, MEM_W), lambda i: (i, C_MZ // MEM_W)), kv, blk, blk],
        out_specs=[blk, blk, kv],
        out_shape=[jax.ShapeDtypeStruct((t, MEM_W), BF16), jax.ShapeDtypeStruct((t, MEM_W), BF16),
                   jax.ShapeDtypeStruct((m_len, 2 * MEM_W), F32)], compiler_params=_cp(),
    )(proj, proj, mkv, o, dog)


_GW = 512


def _merge_fwd(proj, y_dn, y_sb, y_m):
    t = proj.shape[0]
    tb = _pick(t, (256,))
    nc = D_MODEL // _GW

    def body(g1, g2, g3, y1, y2, y3, out_ref):
        out_ref[...] = (_sigmoid(g1[...]) * y1[...] + _sigmoid(g2[...]) * y2[...] + _sigmoid(g3[...]) * y3[...]).astype(BF16)

    def gate(kb):
        return pl.BlockSpec((tb, _GW), lambda i, c: (i, C_GATES // _GW + kb * nc + c))

    blk = pl.BlockSpec((tb, _GW), lambda i, c: (i, c))
    return pl.pallas_call(
        body, name="merge_fwd", grid=(t // tb, nc), in_specs=[gate(0), gate(1), gate(2), blk, blk, blk],
        out_specs=blk, out_shape=jax.ShapeDtypeStruct((t, D_MODEL), BF16), compiler_params=_cp(),
    )(proj, proj, proj, y_dn, y_sb, y_m)


def _merge_bwd(proj, y_dn, y_sb, y_m, dm):
    t = proj.shape[0]
    tb = _pick(t, (256,))
    nc = D_MODEL // _GW

    def body(g1, g2, g3, y1, y2, y3, dm_ref, d1, d2, d3, dg1, dg2, dg3):
        d = dm_ref[...]
        for g, y, dy, dg in ((g1, y1, d1, dg1), (g2, y2, d2, dg2), (g3, y3, d3, dg3)):
            s = _sigmoid(g[...])
            dy[...] = (d * s).astype(BF16)
            dg[...] = (d * y[...] * (s * (1.0 - s))).astype(BF16)

    def gate(kb):
        return pl.BlockSpec((tb, _GW), lambda i, c: (i, C_GATES // _GW + kb * nc + c))

    blk = pl.BlockSpec((tb, _GW), lambda i, c: (i, c))
    act = jax.ShapeDtypeStruct((t, D_MODEL), BF16)
    return pl.pallas_call(
        body, name="merge_bwd", grid=(t // tb, nc), in_specs=[gate(0), gate(1), gate(2), blk, blk, blk, blk],
        out_specs=[blk] * 6, out_shape=[act] * 6, compiler_params=_cp(),
    )(proj, proj, proj, y_dn, y_sb, y_m, dm)


def _final_loss(x, mo, g, tgt):
    t, d = x.shape
    tb = _pick(t, (256,))

    def body(x_ref, mo_ref, g_ref, t_ref, do_ref, dob_ref, loss_ref, dg_ref):
        @pl.when(pl.program_id(0) == 0)
        def _():
            loss_ref[...] = jnp.zeros_like(loss_ref)
            dg_ref[...] = jnp.zeros_like(dg_ref)

        out = x_ref[...] + mo_ref[...]
        r = lax.rsqrt(jnp.mean(out * out, axis=-1, keepdims=True) + NORM_EPS)
        xhat = out * r
        gv = g_ref[...]
        err = xhat * gv - t_ref[...]
        per_tok = jnp.mean(err * err, axis=-1, keepdims=True)
        loss_ref[...] += 0.5 * jnp.sum(per_tok, axis=0, keepdims=True)
        dy = err * (1.0 / d)
        dg_ref[...] += jnp.sum(dy * xhat, axis=0, keepdims=True)
        dxh = dy * gv
        dout = r * (dxh - xhat * jnp.mean(dxh * xhat, axis=-1, keepdims=True))
        do_ref[...] = dout
        dob_ref[...] = dout.astype(BF16)

    row = pl.BlockSpec((tb, d), lambda i: (i, 0))
    vec = pl.BlockSpec((1, d), lambda i: (0, 0))
    return pl.pallas_call(
        body, name="final_loss", grid=(t // tb,), in_specs=[row, row, vec, row],
        out_specs=[row, row, pl.BlockSpec((1, 128), lambda i: (0, 0)), vec],
        out_shape=[jax.ShapeDtypeStruct((t, d), F32), jax.ShapeDtypeStruct((t, d), BF16),
                   jax.ShapeDtypeStruct((1, 128), F32), jax.ShapeDtypeStruct((1, d), F32)],
        compiler_params=_cp(),
    )(x, mo, g, tgt)


def _cast_bf16(a, name):
    r, c = a.shape
    tb = _pick(r, (128, 496, 240))

    def body(a_ref, o_ref):
        o_ref[...] = a_ref[...].astype(BF16)

    blk = pl.BlockSpec((tb, c), lambda i: (i, 0))
    return pl.pallas_call(body, name=name, grid=(r // tb,), in_specs=[blk], out_specs=blk,
                          out_shape=jax.ShapeDtypeStruct((r, c), BF16), compiler_params=_cp())(a)


WIN_START = (0, 23, 45, 68)
_S1_LO, _S1_HI = 1148, 1164
_S1_BA_POS = SHARD_PAD - 128


def _to_window(x, s):
    if s == 0:
        return x
    if s in (2, 3):
        return pltpu.roll(x, 120 if s == 2 else 124, 1)
    pos = lax.broadcasted_iota(jnp.int32, x.shape, 1)
    head = pltpu.roll(x, 4, 1)
    tail = pltpu.roll(x, SHARD_PAD - 12, 1)
    ba = jnp.where(pos < _S1_BA_POS + (_S1_HI - _S1_LO), pltpu.roll(x, _S1_BA_POS - _S1_LO, 1), 0.0)
    return jnp.where(pos < _S1_LO + 4, head, jnp.where(pos < _S1_BA_POS, tail, ba))


def _from_window(g, s):
    if s == 0:
        return g
    if s in (2, 3):
        return pltpu.roll(g, SHARD_PAD - (120 if s == 2 else 124), 1)
    col = lax.broadcasted_iota(jnp.int32, g.shape, 1)
    head = pltpu.roll(g, SHARD_PAD - 4, 1)
    tail = pltpu.roll(g, 12, 1)
    ba = pltpu.roll(g, SHARD_PAD - (_S1_BA_POS - _S1_LO), 1)
    return jnp.where(col < _S1_LO, head, jnp.where(col < _S1_HI, ba, tail))


def _cast_to_window(w, shard, name):
    r, c = w.shape
    tb = _pick(r, (128,))

    def body(s_ref, w_ref, o_ref, pad_scr):
        pad_scr[...] = jnp.zeros_like(pad_scr)
        pad_scr[:, :c] = w_ref[...]
        x = pad_scr[...]
        for s in range(N_SHARD):
            @pl.when(s_ref[0] == s)
            def _():
                o_ref[...] = _to_window(x, s).astype(BF16)

    return pl.pallas_call(
        body, name=name,
        grid_spec=pltpu.PrefetchScalarGridSpec(
            num_scalar_prefetch=1, grid=(r // tb,),
            in_specs=[pl.BlockSpec((tb, c), lambda i, s: (i, 0))],
            out_specs=pl.BlockSpec((tb, SHARD_PAD), lambda i, s: (i, 0)),
            scratch_shapes=[pltpu.VMEM((tb, SHARD_PAD), F32)]),
        out_shape=jax.ShapeDtypeStruct((r, SHARD_PAD), BF16), compiler_params=_cp(),
    )(shard, w)


def _pair_add(g, recv, c_idx, name):
    n, r, c = g.shape
    half = r // 2
    tb = _pick(half, (128, 240))
    nb = half // tb

    def body(c_ref, g_ref, r_ref, o_ref):
        o_ref[...] = (g_ref[...].astype(F32) + r_ref[...].astype(F32)).astype(BF16)

    blk = pl.BlockSpec((n, tb, c), lambda i, c_ref: (0, i, 0))
    return pl.pallas_call(
        body, name=name,
        grid_spec=pltpu.PrefetchScalarGridSpec(
            num_scalar_prefetch=1, grid=(nb,),
            in_specs=[pl.BlockSpec((n, tb, c), lambda i, c_ref: (0, c_ref[0] * nb + i, 0)), blk], out_specs=blk),
        out_shape=jax.ShapeDtypeStruct((n, half, c), BF16), compiler_params=_cp(),
    )(c_idx, g, recv)


def _chip_sum(parts, by_chip, place, name):
    n, h, c = parts.shape
    tb = _pick(h, (128, 240))
    nb = h // tb

    def body(p_ref, mine_ref, *rest):
        others, o_ref = rest[:n], rest[n]
        me = jnp.zeros((tb, c), jnp.int32) + p_ref[0]
        acc = None
        for q in range(n):
            term = jnp.where(me == q, mine_ref[...], others[q][...]).astype(F32)
            acc = term if acc is None else acc + term
        o_ref[...] = acc

    def other(q):
        return pl.BlockSpec((None, tb, c), lambda i, p: (jnp.where(p[0] == q, (q + 1) % n, q), i, 0))

    return pl.pallas_call(
        body, name=name,
        grid_spec=pltpu.PrefetchScalarGridSpec(
            num_scalar_prefetch=1, grid=(nb,),
            in_specs=[pl.BlockSpec((None, tb, c), lambda i, p: (p[0], i, 0))] + [other(q) for q in range(n)],
            out_specs=pl.BlockSpec((tb, c), lambda i, p: (p[1] * nb + i, 0))),
        out_shape=jax.ShapeDtypeStruct((2 * h, c), F32), compiler_params=_cp(),
    )(place, parts, *([by_chip] * n))


def _adamw_math(w, g, m, v):
    m = ADAM_B1 * m + (1.0 - ADAM_B1) * g
    v = ADAM_B2 * v + (1.0 - ADAM_B2) * (g * g)
    m_hat = m / (1.0 - ADAM_B1 ** ADAM_STEP)
    v_hat = v / (1.0 - ADAM_B2 ** ADAM_STEP)
    delta = -ADAM_LR * (m_hat / (jnp.sqrt(v_hat) + ADAM_EPS) + ADAM_WD * w)
    return delta, m, v


def _adamw(w, g, m, v, name):
    r, c = w.shape
    tb = _pick(r, (128, 496, 240))

    def body(w_ref, g_ref, m_ref, v_ref, go_ref, d_ref, mo_ref, vo_ref):
        gv = g_ref[...]
        d, mn, vn = _adamw_math(w_ref[...], gv, m_ref[...], v_ref[...])
        go_ref[...] = gv
        d_ref[...] = d
        mo_ref[...] = mn
        vo_ref[...] = vn

    blk = pl.BlockSpec((tb, c), lambda i: (i, 0))
    return pl.pallas_call(
        body, name=name, grid=(r // tb,), in_specs=[blk] * 4, out_specs=[blk] * 4,
        out_shape=[jax.ShapeDtypeStruct((r, c), F32)] * 4, compiler_params=_cp(),
    )(w, g, m, v)


def _adamw_window(w, g_win, m, v, shard, name):
    r, c = w.shape
    tb = _pick(r, (128,))

    def body(s_ref, w_ref, g_ref, m_ref, v_ref, go_ref, d_ref, mo_ref, vo_ref, g_scr):
        gw = g_ref[...]
        for s in range(N_SHARD):
            @pl.when(s_ref[0] == s)
            def _():
                g_scr[...] = _from_window(gw, s)

        gv = g_scr[:, :c]
        d, mn, vn = _adamw_math(w_ref[...], gv, m_ref[...], v_ref[...])
        go_ref[...] = gv
        d_ref[...] = d
        mo_ref[...] = mn
        vo_ref[...] = vn

    blk = pl.BlockSpec((tb, c), lambda i, s: (i, 0))
    return pl.pallas_call(
        body, name=name,
        grid_spec=pltpu.PrefetchScalarGridSpec(
            num_scalar_prefetch=1, grid=(r // tb,),
            in_specs=[blk, pl.BlockSpec((tb, SHARD_PAD), lambda i, s: (i, 0)), blk, blk], out_specs=[blk] * 4,
            scratch_shapes=[pltpu.VMEM((tb, SHARD_PAD), F32)]),
        out_shape=[jax.ShapeDtypeStruct((r, c), F32)] * 4, compiler_params=_cp(),
    )(shard, w, g_win, m, v)


def _small_update(gathered, w, m, v):
    def body(p_ref, w_ref, m_ref, v_ref, g_ref, d_ref, mo_ref, vo_ref):
        g = p_ref[0]
        for i in range(1, N_DEV):
            g = g + p_ref[i]
        d, mn, vn = _adamw_math(w_ref[...], g, m_ref[...], v_ref[...])
        g_ref[...] = g
        d_ref[...] = d
        mo_ref[...] = mn
        vo_ref[...] = vn

    full = pl.BlockSpec((S_ROWS, 128), lambda i: (0, 0))
    return pl.pallas_call(
        body, name="small_update", grid=(1,),
        in_specs=[pl.BlockSpec((N_DEV, S_ROWS, 128), lambda i: (0, 0, 0)), full, full, full], out_specs=[full] * 4,
        out_shape=[jax.ShapeDtypeStruct((S_ROWS, 128), F32)] * 4, compiler_params=_cp(),
    )(gathered, w, m, v)


_ANY = pl.BlockSpec(memory_space=pl.ANY)


def _place():
    x, y, c = lax.axis_index("x"), lax.axis_index("y"), lax.axis_index("c")
    chips = [(1 - x, y), (x, 1 - y), (1 - x, 1 - y)]
    return x, y, c, chips


def _gather_shards(arrs):
    n = len(arrs)

    def body(*refs):
        ins, outs = refs[:n], refs[n:2 * n]
        send_sems, recv_sems, local_sems = refs[2 * n:2 * n + 3]
        bufs = refs[2 * n + 3:]
        x, y, c, chips = _place()
        me = 2 * x + y
        sibling = (x, y, 1 - c)
        sends = []
        for a in range(n):
            half = ins[a].shape[0] // 2
            mine = pl.ds(pl.multiple_of(c * half, 16), half)
            for j, (qx, qy) in enumerate(chips):
                cp = pltpu.make_async_remote_copy(
                    src_ref=ins[a].at[mine], dst_ref=outs[a].at[me, mine],
                    send_sem=send_sems.at[6 * a + j], recv_sem=recv_sems.at[6 * a + j],
                    device_id=(qx, qy, c), device_id_type=MESH)
                cp.start()
                sends.append(cp)
        for a in range(n):
            step = bufs[a].shape[0]
            for r0 in range(0, ins[a].shape[0], step):
                rows = pl.ds(r0, step)
                load = pltpu.make_async_copy(ins[a].at[rows], bufs[a], local_sems.at[2 * a])
                load.start()
                load.wait()
                store = pltpu.make_async_copy(bufs[a], outs[a].at[me, rows], local_sems.at[2 * a + 1])
                store.start()
                store.wait()
        for a in range(n):
            half = ins[a].shape[0] // 2
            mine = pl.ds(pl.multiple_of(c * half, 16), half)
            for j, (qx, qy) in enumerate(chips):
                q = 2 * qx + qy
                landed = outs[a].at[q, mine]
                pltpu.make_async_remote_copy(
                    src_ref=landed, dst_ref=landed, send_sem=send_sems.at[6 * a + j], recv_sem=recv_sems.at[6 * a + j],
                    device_id=(qx, qy, c), device_id_type=MESH).wait_recv()
                fw = pltpu.make_async_remote_copy(
                    src_ref=landed, dst_ref=landed, send_sem=send_sems.at[6 * a + 3 + j],
                    recv_sem=recv_sems.at[6 * a + 3 + j], device_id=sibling, device_id_type=MESH)
                fw.start()
                sends.append(fw)
        for a in range(n):
            half = ins[a].shape[0] // 2
            theirs = pl.ds(pl.multiple_of((1 - c) * half, 16), half)
            for j, (qx, qy) in enumerate(chips):
                q = 2 * qx + qy
                dst = outs[a].at[q, theirs]
                pltpu.make_async_remote_copy(
                    src_ref=dst, dst_ref=dst, send_sem=send_sems.at[6 * a + 3 + j], recv_sem=recv_sems.at[6 * a + 3 + j],
                    device_id=sibling, device_id_type=MESH).wait_recv()
        for cp in sends:
            cp.wait_send()

    return pl.pallas_call(
        body, name="gather_shards", in_specs=[_ANY] * n, out_specs=[_ANY] * n,
        out_shape=[jax.ShapeDtypeStruct((N_SHARD,) + a.shape, a.dtype) for a in arrs],
        scratch_shapes=[pltpu.SemaphoreType.DMA((6 * n,)), pltpu.SemaphoreType.DMA((6 * n,)),
                        pltpu.SemaphoreType.DMA((2 * n,))]
        + [pltpu.VMEM((_pick(a.shape[0], (256, 496)), a.shape[1]), a.dtype) for a in arrs],
        compiler_params=pltpu.CompilerParams(has_side_effects=True, vmem_limit_bytes=VMEM_LIMIT),
    )(*arrs)


def _pair_reduce_send(grads, tag):
    n = len(grads)

    def body(*refs):
        ins, outs = refs[:n], refs[n:2 * n]
        send_sems, recv_sems = refs[2 * n:]
        x, y, c, _ = _place()
        sibling = (x, y, 1 - c)
        cps = []
        for a in range(n):
            half = ins[a].shape[1] // 2
            theirs = pl.ds(pl.multiple_of((1 - c) * half, 8), half)
            cp = pltpu.make_async_remote_copy(
                src_ref=ins[a].at[:, theirs], dst_ref=outs[a], send_sem=send_sems.at[a], recv_sem=recv_sems.at[a],
                device_id=sibling, device_id_type=MESH)
            cp.start()
            cps.append(cp)
        for cp in cps:
            cp.wait()

    return pl.pallas_call(
        body, name="pair_reduce_send_" + tag, in_specs=[_ANY] * n, out_specs=[_ANY] * n,
        out_shape=[jax.ShapeDtypeStruct((g.shape[0], g.shape[1] // 2, g.shape[2]), g.dtype) for g in grads],
        scratch_shapes=[pltpu.SemaphoreType.DMA((n,)), pltpu.SemaphoreType.DMA((n,))],
        compiler_params=pltpu.CompilerParams(has_side_effects=True),
    )(*grads)


_HBM = pl.BlockSpec(memory_space=pltpu.HBM)
_SEM = pl.BlockSpec(memory_space=pltpu.SEMAPHORE)
_DATAFLOW = pltpu.SideEffectType.DATAFLOW_SIDE_EFFECTING


def _chip_exchange_copies(ins, lands, send_sems, recv_sems):
    x, y, c, chips = _place()
    me = 2 * x + y
    cps = []
    for a in range(len(ins)):
        for j, (qx, qy) in enumerate(chips):
            cps.append(pltpu.make_async_remote_copy(
                src_ref=ins[a].at[2 * qx + qy], dst_ref=lands[a].at[me], send_sem=send_sems.at[3 * a + j],
                recv_sem=recv_sems.at[3 * a + j], device_id=(qx, qy, c), device_id_type=MESH))
    return cps


def _chip_exchange_start(parts, tag):
    n = len(parts)

    def body(*refs):
        ins, lands = refs[:n], refs[n:2 * n]
        send_sems, recv_sems = refs[2 * n:2 * n + 2]
        token = refs[4 * n + 2]
        for cp in _chip_exchange_copies(ins, lands, send_sems, recv_sems):
            cp.start()
        token[...] = jnp.zeros_like(token)

    hbm = [pltpu.HBM(p.shape, p.dtype) for p in parts]
    lands = [pltpu.with_memory_space_constraint(lax.empty(p.shape, p.dtype), pltpu.HBM) for p in parts]
    res = pl.pallas_call(
        body, name="chip_exchange_start_" + tag,
        out_shape=(pltpu.SemaphoreType.DMA((3 * n,)), pltpu.SemaphoreType.DMA((3 * n,)), *hbm, *hbm,
                   jax.ShapeDtypeStruct((8, 128), F32)),
        in_specs=[_HBM] * (2 * n), out_specs=(_SEM, _SEM, *([_HBM] * (2 * n)), pl.BlockSpec(memory_space=pltpu.VMEM)),
        input_output_aliases={a: 2 + a for a in range(2 * n)},
        compiler_params=pltpu.CompilerParams(has_side_effects=_DATAFLOW),
    )(*[pltpu.with_memory_space_constraint(p, pltpu.HBM) for p in parts], *lands)
    return res[0], res[1], res[2:2 + n], res[2 + n:2 + 2 * n], res[2 + 2 * n]


def _chip_exchange_wait(send_sems, recv_sems, parts, lands, after, tag):
    n = len(parts)

    def body(*refs):
        ins, land_refs = refs[:n], refs[n:2 * n]
        s_sems, r_sems = refs[2 * n:2 * n + 2]
        for cp in _chip_exchange_copies(ins, land_refs, s_sems, r_sems):
            cp.wait_send()
            cp.wait_recv()

    hbm = [pltpu.HBM(p.shape, p.dtype) for p in parts]
    res = pl.pallas_call(
        body, name="chip_exchange_wait_" + tag, out_shape=(*hbm, *hbm),
        in_specs=[_HBM] * (2 * n) + [_SEM, _SEM, _ANY], out_specs=tuple([_HBM] * (2 * n)),
        input_output_aliases={a: a for a in range(2 * n)},
        compiler_params=pltpu.CompilerParams(has_side_effects=_DATAFLOW),
    )(*parts, *lands, send_sems, recv_sems, after)
    return res[:n], res[n:]


def _shard_gather_copies(src, land, send_sems, recv_sems):
    x, y, c, chips = _place()
    me = 2 * x + y
    return [pltpu.make_async_remote_copy(
        src_ref=src, dst_ref=land.at[me], send_sem=send_sems.at[j], recv_sem=recv_sems.at[j],
        device_id=(qx, qy, c), device_id_type=MESH) for j, (qx, qy) in enumerate(chips)]


def _shard_gather_start(shard_arr, after):
    def body(src, land, after_ref, send_sems, recv_sems, src_thru, land_thru, token):
        for cp in _shard_gather_copies(src, land, send_sems, recv_sems):
            cp.start()
        token[...] = jnp.zeros_like(token)

    land_shape = (N_SHARD,) + shard_arr.shape
    land = pltpu.with_memory_space_constraint(lax.empty(land_shape, shard_arr.dtype), pltpu.HBM)
    return pl.pallas_call(
        body, name="shard_gather_start",
        out_shape=(pltpu.SemaphoreType.DMA((N_SHARD - 1,)), pltpu.SemaphoreType.DMA((N_SHARD - 1,)),
                   pltpu.HBM(shard_arr.shape, shard_arr.dtype), pltpu.HBM(land_shape, shard_arr.dtype),
                   jax.ShapeDtypeStruct((8, 128), F32)),
        in_specs=[_HBM, _HBM, _ANY], out_specs=(_SEM, _SEM, _HBM, _HBM, pl.BlockSpec(memory_space=pltpu.VMEM)),
        input_output_aliases={0: 2, 1: 3},
        compiler_params=pltpu.CompilerParams(has_side_effects=_DATAFLOW),
    )(pltpu.with_memory_space_constraint(shard_arr, pltpu.HBM), land, after)


def _shard_gather_wait(send_sems, recv_sems, shard_arr, land, after):
    def body(src, land_ref, s_sems, r_sems, after_ref, src_out, land_out):
        for cp in _shard_gather_copies(src, land_ref, s_sems, r_sems):
            cp.wait_send()
            cp.wait_recv()

    return pl.pallas_call(
        body, name="shard_gather_wait",
        out_shape=(pltpu.HBM(shard_arr.shape, shard_arr.dtype), pltpu.HBM(land.shape, land.dtype)),
        in_specs=[_HBM, _HBM, _SEM, _SEM, _ANY], out_specs=(_HBM, _HBM), input_output_aliases={0: 0, 1: 1},
        compiler_params=pltpu.CompilerParams(has_side_effects=_DATAFLOW),
    )(shard_arr, land, send_sems, recv_sems, after)


def _pair_allgather(fulls, tag):
    n = len(fulls)

    def body(*refs):
        outs = refs[n:2 * n]
        send_sems, recv_sems = refs[2 * n:]
        x, y, c, _ = _place()
        sibling = (x, y, 1 - c)
        cps = []
        for a in range(n):
            half = outs[a].shape[0] // 2
            mine = outs[a].at[pl.ds(pl.multiple_of(c * half, 8), half)]
            cp = pltpu.make_async_remote_copy(
                src_ref=mine, dst_ref=mine, send_sem=send_sems.at[a], recv_sem=recv_sems.at[a],
                device_id=sibling, device_id_type=MESH)
            cp.start()
            cps.append(cp)
        for a in range(n):
            half = outs[a].shape[0] // 2
            theirs = outs[a].at[pl.ds(pl.multiple_of((1 - c) * half, 8), half)]
            pltpu.make_async_remote_copy(
                src_ref=theirs, dst_ref=theirs, send_sem=send_sems.at[a], recv_sem=recv_sems.at[a],
                device_id=sibling, device_id_type=MESH).wait_recv()
        for cp in cps:
            cp.wait_send()

    return pl.pallas_call(
        body, name="pair_allgather_" + tag, in_specs=[_ANY] * n, out_specs=[_ANY] * n,
        out_shape=[jax.ShapeDtypeStruct(f.shape, f.dtype) for f in fulls],
        input_output_aliases={a: a for a in range(n)},
        scratch_shapes=[pltpu.SemaphoreType.DMA((n,)), pltpu.SemaphoreType.DMA((n,))],
        compiler_params=pltpu.CompilerParams(has_side_effects=True),
    )(*fulls)


def _allgather_small(slab):
    def body(s_ref, out_ref, send_sems, recv_sems):
        x, y, c, _ = _place()
        me = 4 * x + 2 * y + c
        out_ref[me] = s_ref[...]
        cps = []
        for mask in range(1, N_DEV):
            peer = (x ^ (mask >> 2), y ^ ((mask >> 1) & 1), c ^ (mask & 1))
            cp = pltpu.make_async_remote_copy(
                src_ref=s_ref, dst_ref=out_ref.at[me], send_sem=send_sems.at[mask - 1], recv_sem=recv_sems.at[mask - 1],
                device_id=peer, device_id_type=MESH)
            cp.start()
            cps.append(cp)
        for mask in range(1, N_DEV):
            peer = (x ^ (mask >> 2), y ^ ((mask >> 1) & 1), c ^ (mask & 1))
            dst = out_ref.at[4 * peer[0] + 2 * peer[1] + peer[2]]
            pltpu.make_async_remote_copy(
                src_ref=dst, dst_ref=dst, send_sem=send_sems.at[mask - 1], recv_sem=recv_sems.at[mask - 1],
                device_id=peer, device_id_type=MESH).wait_recv()
        for cp in cps:
            cp.wait_send()

    vm = pl.BlockSpec(memory_space=pltpu.VMEM)
    return pl.pallas_call(
        body, name="allgather_small", in_specs=[vm], out_specs=vm,
        out_shape=jax.ShapeDtypeStruct((N_DEV,) + slab.shape, slab.dtype),
        scratch_shapes=[pltpu.SemaphoreType.DMA((N_DEV - 1,)), pltpu.SemaphoreType.DMA((N_DEV - 1,))],
        compiler_params=pltpu.CompilerParams(has_side_effects=True),
    )(slab)


def _pack_b(w_mem_kv, w_br_dn, w_br_sb, w_br_mem, w_out):
    return jnp.concatenate([w_mem_kv.reshape(128, D_MODEL), w_br_dn, w_br_sb, w_br_mem.reshape(64, D_MODEL), w_out],
                           axis=0)


def _conv_slab(conv_w):
    return jnp.pad(conv_w.reshape(3, D_MODEL), ((0, 29), (0, 0)))


def _unpack_b(slab):
    return (slab[B_MEMKV:B_BRDN].reshape(1, 256, 512), slab[B_BRDN:B_BRSB].reshape(1, 256, D_MODEL),
            slab[B_BRSB:B_BRMEM].reshape(1, 256, D_MODEL), slab[B_BRMEM:B_OUT].reshape(1, 256, 256),
            slab[B_OUT:B_CONV].reshape(1, 256, D_MODEL))


def _conv_rows(conv_full):
    return conv_full.reshape(4 * CONV_BLOCKS, 128)


def _conv_shard_rows(conv_shard, shard):
    own = CONV_BLOCKS // N_SHARD
    blocks = lax.dynamic_update_slice(jnp.zeros((4, CONV_BLOCKS, 128), F32), conv_shard.reshape(4, own, 128),
                                      (0, own * shard, 0))
    return blocks.reshape(4 * CONV_BLOCKS, 128)


def _conv_shard_of(rows, shard):
    own = CONV_BLOCKS // N_SHARD
    blocks = lax.dynamic_slice(rows.reshape(4, CONV_BLOCKS, 128), (0, own * shard, 0), (4, own, 128))
    return blocks.reshape(1, 4, own * 128)


def _pack_small(norm_g, mem_norm_g, final_g, dn_norm_g, a_log, dt_bias, conv_rows, loss=None):
    def row(v):
        v = v.reshape(1, -1).astype(F32)
        return jnp.pad(v, ((0, 0), (0, 128 - v.shape[1])))

    loss_row = row(jnp.zeros((1,), F32) if loss is None else jnp.reshape(loss, (1,)))
    rid = lax.broadcasted_iota(jnp.int32, (8, 128), 0) + S_DNNORM
    tile = jnp.where(rid == S_DNNORM, dn_norm_g.reshape(1, 128), jnp.where(
        rid == S_ALOG, row(a_log), jnp.where(rid == S_DTB, row(dt_bias), jnp.where(rid == S_LOSS, loss_row, 0.0))))
    return jnp.concatenate([norm_g.reshape(8, 128), mem_norm_g.reshape(8, 128), final_g.reshape(8, 128), tile,
                            conv_rows], axis=0)


def _unpack_small(slab, shard):
    return (slab[S_NORM:S_NORM + 8].reshape(1, D_MODEL), slab[S_MEMNORM:S_MEMNORM + 8].reshape(1, D_MODEL),
            slab[S_FINAL:S_FINAL + 8].reshape(D_MODEL), slab[S_DNNORM].reshape(1, 128),
            slab[S_ALOG, :N_HEADS].reshape(1, N_HEADS), slab[S_DTB, :N_HEADS].reshape(1, N_HEADS),
            _conv_shard_of(slab[S_CONV:], shard))


def _windows_to_w_r(win):
    b = 128
    s0, s1, s2, s3 = win[0], win[1], win[2], win[3]
    e1, e2, e3 = WIN_START[1] * b, WIN_START[2] * b, WIN_START[3] * b
    n1, n2 = e2 - e1, e3 - e2
    return jnp.concatenate([
        s0[:, :e1], s0[:, e1:e1 + b] + s1[:, :b],
        s1[:, b:n1], s1[:, n1:n1 + b] + s2[:, :b],
        s2[:, b:n2], s2[:, n2:n2 + b] + s3[:, :b],
        s3[:, b:], s1[:, _S1_BA_POS:], jnp.zeros((win.shape[1], W_R - C_BA - b), win.dtype)], axis=1)


def _dproj_windows(dproj_r):
    b = 128
    pieces = []
    for s in range(N_SHARD):
        lo = WIN_START[s] * b
        if s == 1:
            pieces += [dproj_r[:, lo:lo + _S1_BA_POS], dproj_r[:, C_BA:C_BA + b]]
        else:
            pieces.append(dproj_r[:, lo:lo + SHARD_PAD])
    return jnp.concatenate(pieces, axis=1)


def _local_step(x, mem, tgt, norm_g, mem_norm_g, w_r, w_sh, conv_w, a_log, dt_bias, dn_norm_g, proj_weights, final_g,
                on_early=None, after_gather=None):
    t = x.shape[0]
    final_row = final_g.reshape(1, D_MODEL)
    lanes_8_16 = ((0, 0), (N_HEADS, 128 - 2 * N_HEADS))
    alog_row = jnp.pad(a_log.reshape(1, N_HEADS), lanes_8_16)
    dtb_row = jnp.pad(dt_bias.reshape(1, N_HEADS), lanes_8_16)

    h = _rmsnorm_fwd(x, norm_g, "norm_fwd")
    proj = _mm(h, w_r, "nn", "in_proj", after=after_gather)
    qkv = _dn_prep_fwd(proj, conv_w)
    beta_t, g_t = _dn_gate_fwd(proj, alog_row, dtb_row)
    dn_u, dn_w, dn_qd, dn_kd, dn_a, tinv_all, dn_el = _dn_intra_fwd(qkv, beta_t, g_t)
    o_dn, dn_vn, s_all = _dn_scan_fwd(dn_u, dn_w, dn_qd, dn_kd, dn_a, dn_el)
    o_dn_g = _dn_post_fwd(o_dn, proj, dn_norm_g)
    o_sb, o_sb_g, sb_l = _sb_fwd(proj)
    w_mem_kv, w_br_dn, w_br_sb, w_br_mem, w_out = proj_weights(o_sb_g)
    mem_n = _rmsnorm_fwd(mem, mem_norm_g, "mem_norm_fwd")
    mkv = _mm(mem_n, w_mem_kv, "nn", "mem_kv")
    o_m, o_m_g = _mem_fwd(proj, mkv)
    y_dn = _mm(o_dn_g, w_br_dn, "nn", "br_dn")
    y_sb = _mm(o_sb_g, w_br_sb, "nn", "br_sb")
    y_m = _mm(o_m_g, w_br_mem, "nn", "br_mem")
    merged = _merge_fwd(proj, y_dn, y_sb, y_m)
    mo = _mm(merged, w_out, "nn", "out_proj")
    d_out, d_out_b, loss_row, g_final = _final_loss(x, mo, final_row, tgt)

    g_w_out = _mm(merged, d_out_b, "tn", "g_w_out", out_dtype=BF16)
    d_merged = _mm(d_out_b, w_out, "nt", "d_merged")
    dy_dn, dy_sb, dy_m, dg1, dg2, dg3 = _merge_bwd(proj, y_dn, y_sb, y_m, d_merged)
    g_w_br_dn = _mm(o_dn_g, dy_dn, "tn", "g_w_br_dn", out_dtype=BF16)
    g_w_br_sb = _mm(o_sb_g, dy_sb, "tn", "g_w_br_sb", out_dtype=BF16)
    g_w_br_mem = _mm(o_m_g, dy_m, "tn", "g_w_br_mem", out_dtype=BF16)
    d_o_dn_g = _mm(dy_dn, w_br_dn, "nt", "d_o_dn")
    d_o_sb_g = _mm(dy_sb, w_br_sb, "nt", "d_o_sb")
    d_o_m_g = _mm(dy_m, w_br_mem, "nt", "d_o_mem")

    d_mq, d_mz, d_mkv = _mem_bwd(proj, mkv, o_m, d_o_m_g)
    d_mkv_b = _cast_bf16(d_mkv, "cast_dmkv")
    g_w_mem_kv = _mm(mem_n, d_mkv_b, "tn", "g_w_mem_kv", out_dtype=BF16)
    d_mem_n = _mm(d_mkv_b, w_mem_kv, "nt", "d_mem_n")
    _, g_mem_norm = _rmsnorm_bwd(mem, mem_norm_g, d_mem_n, jnp.zeros_like(mem), "mem_norm_bwd")

    early = dict(w_mem_kv=g_w_mem_kv, w_br_dn=g_w_br_dn, w_br_sb=g_w_br_sb, w_br_mem=g_w_br_mem, w_out=g_w_out)
    after_early = on_early(early) if on_early is not None else None

    d_sq, d_sk, d_sv, d_sz = _sb_bwd(proj, o_sb, sb_l, d_o_sb_g, after=after_early)

    d_o_dn, d_dnz, g_dn_norm = _dn_post_bwd(o_dn, proj, dn_norm_g, d_o_dn_g)
    d_vnew, d_kd, d_qd, d_w, d_el = _dn_scan_bwd(dn_w, dn_qd, dn_kd, dn_a, dn_el, dn_vn, s_all, d_o_dn)
    d_qn, d_kn, d_vn, dbeta_t, dg_t = _dn_intra_bwd(qkv, beta_t, g_t, tinv_all, dn_vn, d_o_dn, d_vnew, d_kd, d_qd, d_w, d_el)
    d_conv_in, g_conv = _dn_prep_bwd(proj, conv_w, d_qn, d_kn, d_vn)
    d_ba, g_alog_row, g_dtb_row = _dn_gate_bwd(proj, alog_row, dtb_row, dbeta_t, dg_t)

    dproj_sh = _dproj_windows(
        jnp.concatenate([d_conv_in, d_dnz, d_sq, d_sk, d_sv, d_sz, d_mq, d_mz, dg1, dg2, dg3, d_ba], axis=1))
    g_w_sh = _mm(h, dproj_sh, "tn", "g_w_in", out_dtype=BF16, out_shards=N_SHARD)
    def input_grad(after=None):
        dh = _mm(dproj_sh, w_sh, "nt", "d_h", after=after)
        grad_x, g_norm = _rmsnorm_bwd(x, norm_g, dh, d_out, "norm_bwd")
        small = dict(norm_g=g_norm, mem_norm_g=g_mem_norm, final_g=g_final, dn_norm_g=g_dn_norm,
                     a_log=g_alog_row[:, N_HEADS:2 * N_HEADS], dt_bias=g_dtb_row[:, N_HEADS:2 * N_HEADS],
                     conv_w=g_conv)
        return grad_x, small

    return loss_row[0, 0], early, g_w_sh, input_grad


def _reduce_scatter_start(grads, tag):
    c = lax.axis_index("c")
    core = jnp.reshape(c, (1,)).astype(jnp.int32)
    recv = _pair_reduce_send(grads, tag)
    parts = [_pair_add(g, r, core, "pair_add_" + tag) for g, r in zip(grads, recv)]
    return _chip_exchange_start(parts, tag)


def _reduce_scatter_finish(handle, after, tag):
    send_sems, recv_sems, parts, lands, _ = handle
    x, y, c = lax.axis_index("x"), lax.axis_index("y"), lax.axis_index("c")
    place = jnp.stack([2 * x + y, c]).astype(jnp.int32)
    parts, by_chip = _chip_exchange_wait(send_sems, recv_sems, parts, lands, after, tag)
    fulls = [_chip_sum(p, b, place, "chip_sum_" + tag) for p, b in zip(parts, by_chip)]
    return _pair_allgather(fulls, tag)


def kernel(x, mem, norm_g, mem_norm_g, w_in, conv_w, a_log, dt_bias, dn_norm_g, w_mem_kv, w_br_dn, w_br_sb, w_br_mem, w_out, final_g, loss_target, m_norm_g, m_mem_norm_g, m_w_in, m_conv_w, m_a_log, m_dt_bias, m_dn_norm_g, m_w_mem_kv, m_w_br_dn, m_w_br_sb, m_w_br_mem, m_w_out, m_final_g, v_norm_g, v_mem_norm_g, v_w_in, v_conv_w, v_a_log, v_dt_bias, v_dn_norm_g, v_w_mem_kv, v_w_br_dn, v_w_br_sb, v_w_br_mem, v_w_out, v_final_g):
    w_a = w_in[0]
    w_b = _pack_b(w_mem_kv[0], w_br_dn[0], w_br_sb[0], w_br_mem[0], w_out[0])
    m_b = _pack_b(m_w_mem_kv[0], m_w_br_dn[0], m_w_br_sb[0], m_w_br_mem[0], m_w_out[0])
    v_b = _pack_b(v_w_mem_kv[0], v_w_br_dn[0], v_w_br_sb[0], v_w_br_mem[0], v_w_out[0])

    shard_idx = 2 * lax.axis_index("x") + lax.axis_index("y")
    shard = jnp.reshape(shard_idx, (1,)).astype(jnp.int32)
    ga, g_conv = _gather_shards([_cast_to_window(w_a, shard, "cast_w_in"),
                                 _cast_bf16(_conv_slab(conv_w[0]), "cast_conv")])
    w_r = _windows_to_w_r(ga)
    f_conv = g_conv[:, :3].reshape(N_SHARD, 4, 768).transpose(1, 0, 2).reshape(4, 3 * D_MODEL).astype(F32)
    b_flight = _shard_gather_start(_cast_bf16(w_b, "cast_w_b"), after=ga)

    def proj_weights(after):
        own, land = _shard_gather_wait(b_flight[0], b_flight[1], b_flight[2], b_flight[3], after)
        gb = lax.dynamic_update_slice(land, own[None], (shard_idx, 0, 0))
        return (gb[:, B_MEMKV:B_BRDN].reshape(N_SHARD * 256, 512),
                gb[:, B_BRDN:B_BRSB].reshape(N_SHARD * 256, D_MODEL),
                gb[:, B_BRSB:B_BRMEM].reshape(N_SHARD * 256, D_MODEL),
                gb[:, B_BRMEM:B_OUT].reshape(N_SHARD, 256, 256).transpose(1, 0, 2).reshape(256, D_MODEL),
                gb[:, B_OUT:B_CONV].reshape(N_SHARD * 256, D_MODEL))

    flights = {}

    def on_early(grads):
        g_b = jnp.concatenate([
            grads["w_mem_kv"].reshape(N_SHARD, 128, D_MODEL), grads["w_br_dn"].reshape(N_SHARD, 256, D_MODEL),
            grads["w_br_sb"].reshape(N_SHARD, 256, D_MODEL),
            grads["w_br_mem"].reshape(256, N_SHARD, 256).transpose(1, 0, 2).reshape(N_SHARD, 64, D_MODEL),
            grads["w_out"].reshape(N_SHARD, 256, D_MODEL)], axis=1).astype(BF16)
        flights["b"] = _reduce_scatter_start([g_b], "b")
        return flights["b"][4]

    loss, _, g_w_sh, input_grad = _local_step(
        x[0], mem[0], loss_target[0], norm_g, mem_norm_g, w_r, ga, f_conv, a_log, dt_bias, dn_norm_g,
        proj_weights, final_g, on_early=on_early, after_gather=b_flight[4])
    flights["a"] = _reduce_scatter_start([g_w_sh], "a")
    grad_x, small = input_grad(after=flights["a"][4])

    part = _pack_small(small["norm_g"], small["mem_norm_g"], small["final_g"], small["dn_norm_g"],
                       small["a_log"], small["dt_bias"], _conv_rows(small["conv_w"]), loss)
    w_s = _pack_small(norm_g, mem_norm_g, final_g, dn_norm_g, a_log, dt_bias, _conv_shard_rows(conv_w[0], shard_idx))
    m_s = _pack_small(m_norm_g, m_mem_norm_g, m_final_g, m_dn_norm_g, m_a_log, m_dt_bias,
                      _conv_shard_rows(m_conv_w[0], shard_idx))
    v_s = _pack_small(v_norm_g, v_mem_norm_g, v_final_g, v_dn_norm_g, v_a_log, v_dt_bias,
                      _conv_shard_rows(v_conv_w[0], shard_idx))
    g_s, d_s, nm_s, nv_s = _small_update(_allgather_small(part), w_s, m_s, v_s)

    (gs_b,) = _reduce_scatter_finish(flights["b"], after=g_s, tag="b")
    (gs_in,) = _reduce_scatter_finish(flights["a"], after=gs_b, tag="a")
    gr_in, d_in, nm_in, nv_in = _adamw_window(w_a, gs_in, m_w_in[0], v_w_in[0], shard, "adamw_w_in")
    gr_b, d_b, nm_b, nv_b = _adamw(w_b, gs_b, m_b, v_b, "adamw_b")

    def assemble(slab_small, a_in, slab_b):
        s_norm, s_memnorm, s_final, s_dnnorm, s_alog, s_dtb, b_conv = _unpack_small(slab_small, shard_idx)
        b_memkv, b_brdn, b_brsb, b_brmem, b_out = _unpack_b(slab_b)
        return [s_norm, s_memnorm, a_in.reshape(1, D_MODEL, IN_WIDTH // N_SHARD), b_conv, s_alog, s_dtb, s_dnnorm,
                b_memkv, b_brdn, b_brsb, b_brmem, b_out, s_final]

    outs = [g_s[S_LOSS, 0], grad_x.reshape(1, -1, D_MODEL)]
    outs += assemble(g_s, gr_in, gr_b)
    outs += assemble(d_s, d_in, d_b)
    outs += assemble(nm_s, nm_in, nm_b)
    outs += assemble(nv_s, nv_in, nv_b)
    return tuple(outs)
```

```python
import math

import jax
import jax.numpy as jnp
from jax import lax
from jax.experimental import pallas as pl
from jax.experimental.pallas import tpu as pltpu

F32 = jnp.float32
BF16 = jnp.bfloat16
MESH = pl.DeviceIdType.MESH

D_MODEL = 1024
N_HEADS = 8
D_HEAD = 128
DN_CHUNK = 64
DN_GROUP = 16
DN_SCAN_GROUP = 4
SB_BLOCK = 256
SB_HEADS_PER_STEP = 2
SB_QBLOCK = 256
MEM_HEADS = 4
MEM_DH = 64
MEM_W = MEM_HEADS * MEM_DH
NORM_EPS = 1e-6
IN_WIDTH = 11792
N_SHARD = 4
SHARD_W = IN_WIDTH // N_SHARD
SHARD_PAD = 3072
N_DEV = 8

C_DNZ = 3072
C_SBQ = 4096
C_SBZ = 7168
C_MQ = 8192
C_MZ = 8448
C_GATES = 8704
C_BA = 11776
W_R = 12288

ADAM_LR = 0.001
ADAM_B1 = 0.9
ADAM_B2 = 0.999
ADAM_EPS = 1e-08
ADAM_WD = 0.01
ADAM_STEP = 10

VMEM_LIMIT = 56 * 1024 * 1024

B_MEMKV, B_BRDN, B_BRSB, B_BRMEM, B_OUT, B_CONV = 0, 128, 384, 640, 704, 960
S_NORM, S_MEMNORM, S_FINAL, S_DNNORM, S_ALOG, S_DTB, S_LOSS, S_CONV, S_ROWS = 0, 8, 16, 24, 25, 26, 27, 32, 128
CONV_BLOCKS = 3 * D_MODEL // 128


def _cp(**kw):
    return pltpu.CompilerParams(vmem_limit_bytes=VMEM_LIMIT, **kw)


def _dot(a, b, dims):
    lead = a.ndim - 2
    ca, cb = {"nn": (1, 0), "nt": (1, 1), "tn": (0, 0)}[dims]
    batch = tuple(range(lead))
    return lax.dot_general(a, b, (((ca + lead,), (cb + lead,)), (batch, batch)), preferred_element_type=F32)


def _chunks(x):
    return x.reshape(x.shape[0] // DN_CHUNK, DN_CHUNK, x.shape[1])


def _unchunk(x):
    return x.reshape(x.shape[0] * x.shape[1], x.shape[2])


def _bdot(a, b, dims):
    return _dot(a.astype(BF16), b.astype(BF16), dims)


def _split(a):
    hi = a.astype(BF16)
    return hi, (a - hi.astype(F32)).astype(BF16)


def _dot3(a, b, dims):
    a1, a2 = _split(a)
    b1, b2 = _split(b)
    return _dot(a1, b1, dims) + (_dot(a1, b2, dims) + _dot(a2, b1, dims))


def _ones_dot(a, ones_bf16):
    out = _dot(a.reshape(-1, a.shape[-1]).astype(BF16), ones_bf16, "nn")
    return out.reshape(a.shape[:-1] + (ones_bf16.shape[1],))


def _sigmoid(x):
    return 1.0 / (1.0 + jnp.exp(-x))


def _log1p_small(u):
    return jnp.where(u < 1e-2, u * (1.0 - u * (0.5 - u * (1.0 / 3.0))), jnp.log(1.0 + u))


def _pick(dim, cands):
    for c in cands:
        if dim % c == 0:
            return c
    return dim


def _mm(a, b, dims, name, out_dtype=F32, out_shards=1, after=None, tm_max=1024, tn_max=512):
    ta, tb = dims[0] == "t", dims[1] == "t"
    m, k = (a.shape[1], a.shape[0]) if ta else a.shape
    b_shards = b.shape[0] if b.ndim == 3 else 1
    n = b.shape[-2] if tb else b.shape[-1]
    tm = _pick(m, (tm_max, 1024, 512, 256))
    tn = _pick(n // out_shards, (tn_max, 512, 384, 256, 128))
    tk = _pick(k // b_shards, (2048, 1024, 512, 384, 256))
    nk = k // tk

    def body(a_ref, b_ref, *rest):
        if nk == 1:
            rest[-1][...] = _bdot(a_ref[...], b_ref[...], dims).astype(out_dtype)
            return
        o_ref, acc_ref = rest[-2:]
        kk = pl.program_id(2)

        @pl.when(kk == 0)
        def _():
            acc_ref[...] = jnp.zeros_like(acc_ref)

        acc_ref[...] += _bdot(a_ref[...], b_ref[...], dims)

        @pl.when(kk == nk - 1)
        def _():
            o_ref[...] = acc_ref[...].astype(out_dtype)

    a_spec = pl.BlockSpec((tk, tm), lambda i, j, q: (q, i)) if ta else pl.BlockSpec((tm, tk), lambda i, j, q: (i, q))
    if b_shards > 1:
        per_k = k // b_shards // tk
        b_spec = pl.BlockSpec((None, tn, tk), lambda i, j, q: (q // per_k, j, q % per_k))
    else:
        b_spec = pl.BlockSpec((tn, tk), lambda i, j, q: (j, q)) if tb else pl.BlockSpec((tk, tn), lambda i, j, q: (q, j))
    if out_shards > 1:
        per_n = n // out_shards // tn
        out_spec = pl.BlockSpec((None, tm, tn), lambda i, j, q: (j // per_n, i, j % per_n))
        out_shape = jax.ShapeDtypeStruct((out_shards, m, n // out_shards), out_dtype)
    else:
        out_spec = pl.BlockSpec((tm, tn), lambda i, j, q: (i, j))
        out_shape = jax.ShapeDtypeStruct((m, n), out_dtype)
    extra_specs, extra = [], []
    if after is not None:
        extra_specs, extra = [pl.BlockSpec(after.shape, lambda i, j, q: (0, 0))], [after]
    return pl.pallas_call(
        body, name=name, grid=(m // tm, n // tn, nk),
        in_specs=[a_spec, b_spec] + extra_specs, out_specs=out_spec, out_shape=out_shape,
        scratch_shapes=[pltpu.VMEM((tm, tn), F32)] if nk > 1 else [],
        compiler_params=_cp(dimension_semantics=("parallel", "parallel", "arbitrary")),
    )(a, b, *extra)


def _rmsnorm_fwd(x, g, name):
    t, d = x.shape
    tb = _pick(t, (512, 256))

    def body(x_ref, g_ref, h_ref):
        xv = x_ref[...]
        r = lax.rsqrt(jnp.mean(xv * xv, axis=-1, keepdims=True) + NORM_EPS)
        h_ref[...] = ((xv * r) * g_ref[...]).astype(BF16)

    return pl.pallas_call(
        body, name=name, grid=(t // tb,),
        in_specs=[pl.BlockSpec((tb, d), lambda i: (i, 0)), pl.BlockSpec((1, d), lambda i: (0, 0))],
        out_specs=pl.BlockSpec((tb, d), lambda i: (i, 0)),
        out_shape=jax.ShapeDtypeStruct((t, d), BF16), compiler_params=_cp(),
    )(x, g)


def _rmsnorm_bwd(x, g, dh, resid, name):
    t, d = x.shape
    tb = _pick(t, (256,))

    def body(x_ref, g_ref, dh_ref, r_ref, dx_ref, dg_ref):
        @pl.when(pl.program_id(0) == 0)
        def _():
            dg_ref[...] = jnp.zeros_like(dg_ref)

        xv = x_ref[...]
        r = lax.rsqrt(jnp.mean(xv * xv, axis=-1, keepdims=True) + NORM_EPS)
        xhat = xv * r
        dhv = dh_ref[...]
        dg_ref[...] += jnp.sum(dhv * xhat, axis=0, keepdims=True)
        dxh = dhv * g_ref[...]
        dx_ref[...] = r_ref[...] + r * (dxh - xhat * jnp.mean(dxh * xhat, axis=-1, keepdims=True))

    row = pl.BlockSpec((tb, d), lambda i: (i, 0))
    vec = pl.BlockSpec((1, d), lambda i: (0, 0))
    return pl.pallas_call(
        body, name=name, grid=(t // tb,), in_specs=[row, vec, row, row], out_specs=[row, vec],
        out_shape=[jax.ShapeDtypeStruct((t, d), F32), jax.ShapeDtypeStruct((1, d), F32)], compiler_params=_cp(),
    )(x, g, dh, resid)


def _conv_silu(xv, w, row):
    y = xv * w[3:4, :]
    for s in (1, 2, 3):
        xs = jnp.where(row >= s, pltpu.roll(xv, s, 0), 0.0)
        y = y + xs * w[3 - s:4 - s, :]
    sig = _sigmoid(y)
    return y, sig, y * sig


def _dn_prep_fwd(proj, conv_w):
    t = proj.shape[0]

    def body(p_ref, w_ref, o_ref):
        j = pl.program_id(0)
        xv = p_ref[...]
        row = lax.broadcasted_iota(jnp.int32, xv.shape, 0)
        _, _, a = _conv_silu(xv, w_ref[...], row)
        inv = lax.rsqrt(jnp.sum(a * a, axis=-1, keepdims=True) + NORM_EPS)
        scale = jnp.where(j < N_HEADS, D_HEAD ** -0.5, 1.0)
        normed = jnp.where(j < 2 * N_HEADS, 1.0, 0.0)
        o_ref[...] = a * (normed * (inv * scale) + (1.0 - normed))

    return pl.pallas_call(
        body, name="dn_prep_fwd", grid=(3 * N_HEADS,),
        in_specs=[pl.BlockSpec((t, D_HEAD), lambda j: (0, j)), pl.BlockSpec((4, D_HEAD), lambda j: (0, j))],
        out_specs=pl.BlockSpec((t, D_HEAD), lambda j: (0, j)),
        out_shape=jax.ShapeDtypeStruct((t, 3 * D_MODEL), F32), compiler_params=_cp(),
    )(proj, conv_w)


def _dn_prep_bwd(proj, conv_w, dq, dk, dv):
    t = proj.shape[0]

    def body(p_ref, w_ref, dq_ref, dk_ref, dv_ref, dp_ref, dw_ref):
        j = pl.program_id(0)
        xv = p_ref[...]
        w = w_ref[...]
        row = lax.broadcasted_iota(jnp.int32, xv.shape, 0)
        y, s, a = _conv_silu(xv, w, row)
        part = jnp.zeros(xv.shape, jnp.int32) + j // N_HEADS
        dn = jnp.where(part == 0, dq_ref[...], jnp.where(part == 1, dk_ref[...], dv_ref[...]))
        inv = lax.rsqrt(jnp.sum(a * a, axis=-1, keepdims=True) + NORM_EPS)
        scale = jnp.where(j < N_HEADS, D_HEAD ** -0.5, 1.0)
        ds = dn * scale
        da_norm = inv * ds - a * (inv * inv * inv) * jnp.sum(ds * a, axis=-1, keepdims=True)
        normed = jnp.where(j < 2 * N_HEADS, 1.0, 0.0)
        da = normed * da_norm + (1.0 - normed) * dn
        dy = da * (s * (1.0 + y * (1.0 - s)))
        dx = dy * w[3:4, :]
        dw_ref[3:4, :] = jnp.sum(dy * xv, axis=0, keepdims=True)
        for sft in (1, 2, 3):
            xs = jnp.where(row >= sft, pltpu.roll(xv, sft, 0), 0.0)
            dw_ref[3 - sft:4 - sft, :] = jnp.sum(dy * xs, axis=0, keepdims=True)
            dys = jnp.where(row < t - sft, pltpu.roll(dy, t - sft, 0), 0.0)
            dx = dx + dys * w[3 - sft:4 - sft, :]
        dp_ref[...] = dx.astype(BF16)

    blk = pl.BlockSpec((t, D_HEAD), lambda j: (0, j))
    wblk = pl.BlockSpec((4, D_HEAD), lambda j: (0, j))

    def grad(part):
        return pl.BlockSpec((t, D_HEAD), lambda j: (0, jnp.clip(j - part * N_HEADS, 0, N_HEADS - 1)))

    return pl.pallas_call(
        body, name="dn_prep_bwd", grid=(3 * N_HEADS,), in_specs=[blk, wblk, grad(0), grad(1), grad(2)],
        out_specs=[blk, wblk],
        out_shape=[jax.ShapeDtypeStruct((t, 3 * D_MODEL), BF16), jax.ShapeDtypeStruct((4, 3 * D_MODEL), F32)],
        compiler_params=_cp(),
    )(proj, conv_w, dq, dk, dv)


def _softplus_parts(xv):
    e = jnp.exp(-jnp.abs(xv))
    return jnp.maximum(xv, 0.0) + _log1p_small(e)


def _chunk_scan(v, row, reverse):
    t = v.shape[0]
    pos = row & (DN_CHUNK - 1)
    s = 1
    while s < DN_CHUNK:
        if reverse:
            v = v + jnp.where(pos < DN_CHUNK - s, pltpu.roll(v, t - s, 0), 0.0)
        else:
            v = v + jnp.where(pos >= s, pltpu.roll(v, s, 0), 0.0)
        s *= 2
    return v


def _dn_gate_fwd(proj, alog_row, dtb_row):
    t = proj.shape[0]

    def body(p_ref, al_ref, dt_ref, b_ref, g_ref):
        p = p_ref[...]
        row = lax.broadcasted_iota(jnp.int32, p.shape, 0)
        b_ref[...] = _sigmoid(p)
        g = -jnp.exp(al_ref[...]) * _softplus_parts(p + dt_ref[...])
        g_ref[...] = _chunk_scan(g, row, reverse=False)

    blk = pl.BlockSpec((t, 128), lambda i: (0, C_BA // 128))
    vec = pl.BlockSpec((1, 128), lambda i: (0, 0))
    out = pl.BlockSpec((t, 128), lambda i: (0, 0))
    return pl.pallas_call(
        body, name="dn_gate_fwd", grid=(1,), in_specs=[blk, vec, vec], out_specs=[out, out],
        out_shape=[jax.ShapeDtypeStruct((t, 128), F32)] * 2, compiler_params=_cp(),
    )(proj, alog_row, dtb_row)


def _dn_gate_bwd(proj, alog_row, dtb_row, dbeta, dgc):
    t = proj.shape[0]

    def body(p_ref, al_ref, dt_ref, db_ref, dg_ref, dp_ref, dal_ref, ddt_ref):
        p = p_ref[...]
        row = lax.broadcasted_iota(jnp.int32, p.shape, 0)
        lane = lax.broadcasted_iota(jnp.int32, p.shape, 1)
        s = _sigmoid(p)
        d_b = db_ref[...] * s * (1.0 - s)
        dg = _chunk_scan(dg_ref[...], row, reverse=True)
        xa = p + dt_ref[...]
        ea = jnp.exp(al_ref[...])
        g = -ea * _softplus_parts(xa)
        d_a = dg * (-ea) * _sigmoid(xa)
        dp_ref[...] = jnp.where(lane < N_HEADS, d_b, jnp.where(lane < 2 * N_HEADS, d_a, 0.0)).astype(BF16)
        dal_ref[...] = jnp.sum(dg * g, axis=0, keepdims=True)
        ddt_ref[...] = jnp.sum(d_a, axis=0, keepdims=True)

    blk = pl.BlockSpec((t, 128), lambda i: (0, C_BA // 128))
    vec = pl.BlockSpec((1, 128), lambda i: (0, 0))
    full = pl.BlockSpec((t, 128), lambda i: (0, 0))
    return pl.pallas_call(
        body, name="dn_gate_bwd", grid=(1,), in_specs=[blk, vec, vec, full, full], out_specs=[full, vec, vec],
        out_shape=[jax.ShapeDtypeStruct((t, 128), BF16), jax.ShapeDtypeStruct((1, 128), F32),
                   jax.ShapeDtypeStruct((1, 128), F32)], compiler_params=_cp(),
    )(proj, alog_row, dtb_row, dbeta, dgc)


def _col_to_row(col, eye):
    return jnp.sum(jnp.where(eye, col, 0.0), axis=-2, keepdims=True)


def _row_to_col(rowv, eye):
    return jnp.sum(jnp.where(eye, rowv, 0.0), axis=-1, keepdims=True)


def _tri_inverse(m, ri, ci):
    eye = (ri == ci).astype(F32)
    b16 = (ri >> 4) == (ci >> 4)
    b32 = (ri >> 5) == (ci >> 5)
    m1 = jnp.where(b16, m, 0.0)
    x = eye - m1
    p = _dot3(m1, m1, "nn")
    x = x + _dot3(x, p, "nn")
    p = _dot3(p, p, "nn")
    x = x + _dot3(x, p, "nn")
    p = _dot3(p, p, "nn")
    x = x + _dot3(x, p, "nn")
    c1 = jnp.where(jnp.logical_and(b32, jnp.logical_not(b16)), m, 0.0)
    x = x - _dot3(_dot3(x, c1, "nn"), x, "nn")
    c2 = jnp.where(b32, 0.0, m)
    x = x - _dot3(_dot3(x, c2, "nn"), x, "nn")
    return x


def _dn_chunk_common(q, k, gc, ri, ci):
    eye = ri == ci
    g_row = _col_to_row(gc, eye)
    diff = jnp.minimum(gc - g_row, 0.0)
    gam = jnp.where(ri >= ci, jnp.exp(diff), 0.0)
    kk = _bdot(k, k, "nt")
    qk = _bdot(q, k, "nt")
    rcol = lax.broadcasted_iota(jnp.int32, gc.shape, gc.ndim - 2)
    last = jnp.sum(jnp.where(rcol == DN_CHUNK - 1, gc, 0.0), axis=-2, keepdims=True)
    e_g = jnp.exp(gc)
    dec = jnp.exp(last - gc)
    return eye, gam, kk, qk, last, e_g, dec, rcol


def _dn_specs(t, rows_blk):
    def head(off):
        return pl.BlockSpec((rows_blk, D_HEAD), lambda g, h: (g, off + h))

    lanes = pl.BlockSpec((rows_blk, 128), lambda g, h: (g, 0))
    hm = pl.BlockSpec((None, rows_blk, D_HEAD), lambda g, h: (h, g, 0))
    sq = pl.BlockSpec((1, rows_blk, DN_CHUNK), lambda g, h: (h, g, 0))
    tile = pl.BlockSpec((1, rows_blk // DN_CHUNK, 8, 128), lambda g, h: (h, g, 0, 0))
    return head, lanes, hm, sq, tile


def _head_column(slab, lane_idx):
    lane = lax.broadcasted_iota(jnp.int32, slab.shape, 1)
    return _chunks(jnp.sum(jnp.where(lane == lane_idx, slab, 0.0), axis=1, keepdims=True))


def _dn_intra_fwd(qkv, beta_t, g_t):
    t = qkv.shape[0]
    n_chunks = t // DN_CHUNK
    rows_blk = min(DN_GROUP * DN_CHUNK, t)

    def body(q_ref, k_ref, v_ref, b_ref, g_ref, u_ref, w_ref, qd_ref, kd_ref, a_ref, ti_ref, el_ref):
        ri = lax.broadcasted_iota(jnp.int32, (DN_CHUNK, DN_CHUNK), 0)
        ci = lax.broadcasted_iota(jnp.int32, (DN_CHUNK, DN_CHUNK), 1)
        h = pl.program_id(1)
        q, k, v = (_chunks(r[...]) for r in (q_ref, k_ref, v_ref))
        b, gc = _head_column(b_ref[...], h), _head_column(g_ref[...], h + N_HEADS)
        _, gam, kk, qk, last, e_g, dec, _ = _dn_chunk_common(q, k, gc, ri, ci)
        tinv = _tri_inverse(jnp.where(ri > ci, b * kk * gam, 0.0), ri, ci)
        u_ref[...] = _unchunk(_bdot(tinv, v * b, "nn"))
        w_ref[...] = _unchunk(_bdot(tinv, k * (b * e_g), "nn"))
        qd_ref[...] = _unchunk(q * e_g)
        kd_ref[...] = _unchunk(k * dec)
        a_ref[0] = _unchunk(qk * gam)
        ti_ref[0] = _unchunk(tinv)
        el_ref[0] = jnp.broadcast_to(jnp.exp(last), (rows_blk // DN_CHUNK, 8, 128))

    head, lanes, hm, sq, tile = _dn_specs(t, rows_blk)
    act = jax.ShapeDtypeStruct((N_HEADS, t, D_HEAD), F32)
    sqs = jax.ShapeDtypeStruct((N_HEADS, t, DN_CHUNK), F32)
    return pl.pallas_call(
        body, name="dn_intra_fwd", grid=(t // rows_blk, N_HEADS),
        in_specs=[head(0), head(N_HEADS), head(2 * N_HEADS), lanes, lanes],
        out_specs=[hm] * 4 + [sq, sq, tile],
        out_shape=[act] * 4 + [sqs, sqs, jax.ShapeDtypeStruct((N_HEADS, n_chunks, 8, 128), F32)],
        compiler_params=_cp(),
    )(qkv, qkv, qkv, beta_t, g_t)


def _dn_scan_specs(t, rows_blk, reverse):
    n_groups = t // rows_blk

    def at(g):
        return n_groups - 1 - g if reverse else g

    per = rows_blk // DN_CHUNK
    act = pl.BlockSpec((N_HEADS, rows_blk, D_HEAD), lambda g: (0, at(g), 0))
    sq = pl.BlockSpec((N_HEADS, rows_blk, DN_CHUNK), lambda g: (0, at(g), 0))
    state = pl.BlockSpec((N_HEADS, per, D_HEAD, D_HEAD), lambda g: (0, at(g), 0, 0))
    tile = pl.BlockSpec((N_HEADS, per, 8, 128), lambda g: (0, at(g), 0, 0))
    return act, sq, state, tile


def _dn_scan_fwd(u, w, qd, kd, a, el):
    t = u.shape[1]
    n_chunks = t // DN_CHUNK
    rows_blk = DN_SCAN_GROUP * DN_CHUNK

    def body(u_ref, w_ref, qd_ref, kd_ref, a_ref, el_ref, o_ref, vn_ref, s_ref, s_scr):
        @pl.when(pl.program_id(0) == 0)
        def _():
            s_scr[...] = jnp.zeros_like(s_scr)

        for cc in range(DN_SCAN_GROUP):
            rows = slice(cc * DN_CHUNK, (cc + 1) * DN_CHUNK)
            s = s_scr[...]
            s_ref[:, cc] = s
            v_new = u_ref[:, rows, :] - _bdot(w_ref[:, rows, :], s, "nn")
            vn_ref[:, rows, :] = v_new
            o_ref[:, rows, :] = _bdot(qd_ref[:, rows, :], s, "nn") + _bdot(a_ref[:, rows, :], v_new, "nn")
            s_scr[...] = s * el_ref[:, cc][:, 0:1, :] + _bdot(kd_ref[:, rows, :], v_new, "tn")

    act, sq, state, tile = _dn_scan_specs(t, rows_blk, reverse=False)
    shp = jax.ShapeDtypeStruct((N_HEADS, t, D_HEAD), F32)
    return pl.pallas_call(
        body, name="dn_scan_fwd", grid=(t // rows_blk,),
        in_specs=[act, act, act, act, sq, tile], out_specs=[act, act, state],
        out_shape=[shp, shp, jax.ShapeDtypeStruct((N_HEADS, n_chunks, D_HEAD, D_HEAD), F32)],
        scratch_shapes=[pltpu.VMEM((N_HEADS, D_HEAD, D_HEAD), F32)],
        compiler_params=_cp(dimension_semantics=("arbitrary",)),
    )(u, w, qd, kd, a, el)


def _dn_scan_bwd(w, qd, kd, a, el, vn, s_all, do):
    t = w.shape[1]
    n_chunks = t // DN_CHUNK
    rows_blk = DN_SCAN_GROUP * DN_CHUNK

    def body(w_ref, qd_ref, kd_ref, a_ref, el_ref, vn_ref, s_ref, do_ref, dvn_ref, dkd_ref, dqd_ref, dw_ref, dl_ref, ds_scr):
        @pl.when(pl.program_id(0) == 0)
        def _():
            ds_scr[...] = jnp.zeros_like(ds_scr)

        for cc in reversed(range(DN_SCAN_GROUP)):
            rows = slice(cc * DN_CHUNK, (cc + 1) * DN_CHUNK)
            s = s_ref[:, cc]
            d_s = ds_scr[...]
            e_last = el_ref[:, cc][:, 0:1, :]
            d_o = do_ref[:, rows, :]
            dv_new = _bdot(a_ref[:, rows, :], d_o, "tn") + _bdot(kd_ref[:, rows, :], d_s, "nn")
            ds_scr[...] = d_s * e_last + _bdot(qd_ref[:, rows, :], d_o, "tn") - _bdot(w_ref[:, rows, :], dv_new, "tn")
            dvn_ref[:, rows, :] = dv_new
            dkd_ref[:, rows, :] = _bdot(vn_ref[:, rows, :], d_s, "nt")
            dqd_ref[:, rows, :] = _bdot(d_o, s, "nt")
            dw_ref[:, rows, :] = -_bdot(dv_new, s, "nt")
            dlast = jnp.sum(jnp.sum(d_s * s, axis=2, keepdims=True), axis=1, keepdims=True)
            dl_ref[:, cc] = jnp.broadcast_to(dlast * e_last, (N_HEADS, 8, 128))

    act, sq, state, tile = _dn_scan_specs(t, rows_blk, reverse=True)
    shp = jax.ShapeDtypeStruct((N_HEADS, t, D_HEAD), F32)
    return pl.pallas_call(
        body, name="dn_scan_bwd", grid=(t // rows_blk,),
        in_specs=[act, act, act, sq, tile, act, state, act], out_specs=[act] * 4 + [tile],
        out_shape=[shp] * 4 + [jax.ShapeDtypeStruct((N_HEADS, n_chunks, 8, 128), F32)],
        scratch_shapes=[pltpu.VMEM((N_HEADS, D_HEAD, D_HEAD), F32)],
        compiler_params=_cp(dimension_semantics=("arbitrary",)),
    )(w, qd, kd, a, el, vn, s_all, do)


def _dn_intra_bwd(qkv, beta_t, g_t, tinv_all, vn, do, dvn, dkd, dqd, dw, dl):
    t = qkv.shape[0]
    rows_blk = min(DN_GROUP * DN_CHUNK, t)

    def body(q_ref, k_ref, v_ref, b_ref, g_ref, ti_ref, vn_ref, do_ref, dvn_ref, dkd_ref, dqd_ref, dw_ref, dl_ref,
             dq_ref, dk_ref, dv_ref, db_ref, dg_ref):
        ri = lax.broadcasted_iota(jnp.int32, (DN_CHUNK, DN_CHUNK), 0)
        ci = lax.broadcasted_iota(jnp.int32, (DN_CHUNK, DN_CHUNK), 1)
        h = pl.program_id(1)
        q, k, v = (_chunks(r[...]) for r in (q_ref, k_ref, v_ref))
        b, gc = _head_column(b_ref[...], h), _head_column(g_ref[...], h + N_HEADS)
        tinv = _chunks(ti_ref[0])
        dv_new, dk_dec, dq_dec, d_w = (_chunks(r[...]) for r in (dvn_ref, dkd_ref, dqd_ref, dw_ref))
        eye, gam, kk, qk, _, e_g, dec, rcol = _dn_chunk_common(q, k, gc, ri, ci)
        bv = v * b
        bk = k * (b * e_g)

        d_a = jnp.where(ri >= ci, _bdot(_chunks(do_ref[...]), _chunks(vn_ref[...]), "nt"), 0.0)
        dbv = _bdot(tinv, dv_new, "tn")
        dbk = _bdot(tinv, d_w, "tn")
        d_tinv = _bdot(dv_new, bv, "nt") + _bdot(d_w, bk, "nt")
        d_m = -jnp.where(ri > ci, _dot3(_dot3(tinv, d_tinv, "tn"), tinv, "nt"), 0.0)

        d_kk = d_m * b * gam
        d_gam = d_m * b * kk + d_a * qk
        d_qk = d_a * gam
        dq_ref[...] = _unchunk(_bdot(d_qk, k, "nn") + dq_dec * e_g)
        dk_ref[...] = _unchunk(_bdot(d_qk, q, "tn") + _bdot(d_kk, k, "nn") + _bdot(d_kk, k, "tn")
                               + dk_dec * dec + dbk * (b * e_g))
        dv_ref[...] = _unchunk(dbv * b)
        d_b = _unchunk(jnp.sum(d_m * kk * gam, axis=-1, keepdims=True) + jnp.sum(dbv * v, axis=-1, keepdims=True)
                       + jnp.sum(dbk * k, axis=-1, keepdims=True) * e_g)

        xg = d_gam * gam
        kdk = jnp.sum(dk_dec * (k * dec), axis=-1, keepdims=True)
        d_gc = (jnp.sum(xg, axis=-1, keepdims=True) - _row_to_col(jnp.sum(xg, axis=-2, keepdims=True), eye)
                + jnp.sum(dq_dec * (q * e_g), axis=-1, keepdims=True) - kdk
                + jnp.sum(dbk * bk, axis=-1, keepdims=True))
        d_last_total = dl_ref[0][:, 0:1, 0:1] + jnp.sum(kdk, axis=-2, keepdims=True)
        d_g = _unchunk(d_gc + jnp.where(rcol == DN_CHUNK - 1, d_last_total, 0.0))

        @pl.when(h == 0)
        def _():
            db_ref[...] = jnp.zeros_like(db_ref)
            dg_ref[...] = jnp.zeros_like(dg_ref)

        lane = lax.broadcasted_iota(jnp.int32, db_ref.shape, 1)
        db_ref[...] += jnp.where(lane == h, d_b, 0.0)
        dg_ref[...] += jnp.where(lane == h + N_HEADS, d_g, 0.0)

    head, lanes, hm, sq, tile = _dn_specs(t, rows_blk)
    return pl.pallas_call(
        body, name="dn_intra_bwd", grid=(t // rows_blk, N_HEADS),
        in_specs=[head(0), head(N_HEADS), head(2 * N_HEADS), lanes, lanes, sq] + [hm] * 6 + [tile],
        out_specs=[head(0), head(0), head(0), lanes, lanes],
        out_shape=[jax.ShapeDtypeStruct((t, D_MODEL), F32)] * 3 + [jax.ShapeDtypeStruct((t, 128), F32)] * 2,
        compiler_params=_cp(),
    )(qkv, qkv, qkv, beta_t, g_t, tinv_all, vn, do, dvn, dkd, dqd, dw, dl)


def _dn_post_fwd(o, proj, gn):
    t = o.shape[1]

    def body(o_ref, z_ref, g_ref, out_ref):
        ov, z = o_ref[...], z_ref[...]
        r = lax.rsqrt(jnp.mean(ov * ov, axis=-1, keepdims=True) + NORM_EPS)
        out_ref[...] = (((ov * r) * g_ref[...]) * (z * _sigmoid(z))).astype(BF16)

    blk = pl.BlockSpec((t, D_HEAD), lambda h: (0, h))
    return pl.pallas_call(
        body, name="dn_post_fwd", grid=(N_HEADS,),
        in_specs=[pl.BlockSpec((None, t, D_HEAD), lambda h: (h, 0, 0)),
                  pl.BlockSpec((t, D_HEAD), lambda h: (0, C_DNZ // D_HEAD + h)),
                  pl.BlockSpec((1, D_HEAD), lambda h: (0, 0))],
        out_specs=blk, out_shape=jax.ShapeDtypeStruct((t, D_MODEL), BF16), compiler_params=_cp(),
    )(o, proj, gn)


def _dn_post_bwd(o, proj, gn, dout):
    t = o.shape[1]

    def body(o_ref, z_ref, g_ref, d_ref, do_ref, dz_ref, dg_ref):
        @pl.when(pl.program_id(0) == 0)
        def _():
            dg_ref[...] = jnp.zeros_like(dg_ref)

        ov, z, d = o_ref[...], z_ref[...], d_ref[...]
        r = lax.rsqrt(jnp.mean(ov * ov, axis=-1, keepdims=True) + NORM_EPS)
        ohat = ov * r
        s = _sigmoid(z)
        d_on = d * (z * s)
        dz_ref[...] = (d * (ohat * g_ref[...]) * (s * (1.0 + z * (1.0 - s)))).astype(BF16)
        dg_ref[...] += jnp.sum(d_on * ohat, axis=0, keepdims=True)
        dxh = d_on * g_ref[...]
        do_ref[...] = r * (dxh - ohat * jnp.mean(dxh * ohat, axis=-1, keepdims=True))

    blk = pl.BlockSpec((t, D_HEAD), lambda h: (0, h))
    hm = pl.BlockSpec((None, t, D_HEAD), lambda h: (h, 0, 0))
    vec = pl.BlockSpec((1, D_HEAD), lambda h: (0, 0))
    return pl.pallas_call(
        body, name="dn_post_bwd", grid=(N_HEADS,),
        in_specs=[hm, pl.BlockSpec((t, D_HEAD), lambda h: (0, C_DNZ // D_HEAD + h)), vec, blk],
        out_specs=[hm, blk, vec],
        out_shape=[jax.ShapeDtypeStruct((N_HEADS, t, D_HEAD), F32), jax.ShapeDtypeStruct((t, D_MODEL), BF16),
                   jax.ShapeDtypeStruct((1, D_HEAD), F32)], compiler_params=_cp(),
    )(o, proj, gn, dout)


def _sb_fwd(proj):
    t = proj.shape[0]
    qblk = min(SB_QBLOCK, t)
    scale = 1.0 / math.sqrt(D_HEAD)

    hp = SB_HEADS_PER_STEP
    wid = hp * D_HEAD

    def body(q_ref, k_ref, v_ref, z_ref, o_ref, og_ref, l_ref, qb, kb, vb):
        for hh in range(hp):
            hs = slice(hh * D_HEAD, (hh + 1) * D_HEAD)
            qb[hh] = q_ref[:, hs].astype(BF16)
            kb[hh] = k_ref[:, hs].astype(BF16)
            vb[hh] = v_ref[:, hs].astype(BF16)
        ri = lax.broadcasted_iota(jnp.int32, (qblk, SB_BLOCK), 0)
        ci = lax.broadcasted_iota(jnp.int32, (qblk, SB_BLOCK), 1)
        r2 = lax.broadcasted_iota(jnp.int32, (SB_BLOCK, SB_BLOCK), 0)
        c2 = lax.broadcasted_iota(jnp.int32, (SB_BLOCK, SB_BLOCK), 1)
        upper = (r2 > c2).astype(BF16)
        nkb = qblk // SB_BLOCK

        def qblock(i, carry):
            rows = pl.ds(pl.multiple_of(i * qblk, qblk), qblk)
            qi = qb[:, rows, :]

            def tile(j, st, on_diagonal):
                acc, c = st
                cols = pl.ds(pl.multiple_of(j * SB_BLOCK, SB_BLOCK), SB_BLOCK)
                z = _dot(qi, kb[:, cols, :], "nt") * scale
                lb = jnp.minimum(z, 0.0) - jnp.log(1.0 + jnp.exp(-jnp.abs(z)))
                lf = lb - z
                if on_diagonal:
                    mask = (j * SB_BLOCK + ci) < (i * qblk + ri)
                    lf = jnp.where(mask, lf, 0.0)
                att = jnp.exp(lb + (_ones_dot(lf, upper) + c))
                if on_diagonal:
                    att = jnp.where(mask, att, 0.0)
                acc = acc + _dot(att.astype(BF16), vb[:, cols, :], "nn")
                return acc, c + jnp.sum(lf, axis=-1, keepdims=True)

            st = (jnp.zeros((hp, qblk, D_HEAD), F32), jnp.zeros((hp, qblk, 1), F32))
            for d in range(nkb):
                st = tile((i + 1) * nkb - 1 - d, st, True)
            acc, c = lax.fori_loop(0, i * nkb, lambda jj, s: tile(i * nkb - 1 - jj, s, False), st)
            l_ref[:, rows, :] = c
            for hh in range(hp):
                hs = slice(hh * D_HEAD, (hh + 1) * D_HEAD)
                zg = z_ref[rows, hs]
                o_ref[rows, hs] = acc[hh]
                og_ref[rows, hs] = (acc[hh] * (zg * _sigmoid(zg))).astype(BF16)
            return carry

        lax.fori_loop(0, t // qblk, qblock, 0)

    def head(off):
        return pl.BlockSpec((t, wid), lambda h: (0, off // wid + h))

    out = pl.BlockSpec((t, wid), lambda h: (0, h))
    return pl.pallas_call(
        body, name="sb_fwd", grid=(N_HEADS // hp,),
        in_specs=[head(C_SBQ), head(C_SBQ + D_MODEL), head(C_SBQ + 2 * D_MODEL), head(C_SBZ)],
        out_specs=[out, out, pl.BlockSpec((hp, t, 1), lambda h: (h, 0, 0))],
        out_shape=[jax.ShapeDtypeStruct((t, D_MODEL), F32), jax.ShapeDtypeStruct((t, D_MODEL), BF16),
                   jax.ShapeDtypeStruct((N_HEADS, t, 1), F32)],
        scratch_shapes=[pltpu.VMEM((hp, t, D_HEAD), BF16)] * 3, compiler_params=_cp(),
    )(proj, proj, proj, proj)


def _sb_bwd(proj, o, ltot, dog, after=None):
    t = proj.shape[0]
    qblk = min(SB_QBLOCK, t)
    scale = 1.0 / math.sqrt(D_HEAD)

    hp = SB_HEADS_PER_STEP
    wid = hp * D_HEAD

    def body(q_ref, k_ref, v_ref, z_ref, o_ref, l_ref, d_ref, *rest):
        dq_ref, dk_ref, dv_ref, dz_ref, qb, kb, vb, dob, dk_scr, dv_scr = rest[-10:]
        for hh in range(hp):
            hs = slice(hh * D_HEAD, (hh + 1) * D_HEAD)
            qb[hh] = q_ref[:, hs].astype(BF16)
            kb[hh] = k_ref[:, hs].astype(BF16)
            vb[hh] = v_ref[:, hs].astype(BF16)
            zg = z_ref[:, hs]
            sg = _sigmoid(zg)
            dgo = d_ref[:, hs]
            dob[hh] = (dgo * (zg * sg)).astype(BF16)
            dz_ref[:, hs] = (dgo * o_ref[:, hs] * (sg * (1.0 + zg * (1.0 - sg)))).astype(BF16)
        dk_scr[...] = jnp.zeros_like(dk_scr)
        dv_scr[...] = jnp.zeros_like(dv_scr)
        ri = lax.broadcasted_iota(jnp.int32, (qblk, SB_BLOCK), 0)
        ci = lax.broadcasted_iota(jnp.int32, (qblk, SB_BLOCK), 1)
        r2 = lax.broadcasted_iota(jnp.int32, (SB_BLOCK, SB_BLOCK), 0)
        c2 = lax.broadcasted_iota(jnp.int32, (SB_BLOCK, SB_BLOCK), 1)
        upper = (r2 > c2).astype(BF16)
        below = (r2 < c2).astype(BF16)

        def qblock(i, carry):
            rows = pl.ds(pl.multiple_of(i * qblk, qblk), qblk)
            qi = qb[:, rows, :]
            d_o = dob[:, rows, :]
            ltot = l_ref[:, rows, :]

            def tile(j, st, on_diagonal):
                dq, cpre, ce = st
                cols = pl.ds(pl.multiple_of(j * SB_BLOCK, SB_BLOCK), SB_BLOCK)
                kj, vj = kb[:, cols, :], vb[:, cols, :]
                z = _dot(qi, kj, "nt") * scale
                lb = jnp.minimum(z, 0.0) - jnp.log(1.0 + jnp.exp(-jnp.abs(z)))
                lf = lb - z
                if on_diagonal:
                    mask = (j * SB_BLOCK + ci) < (i * qblk + ri)
                    lf = jnp.where(mask, lf, 0.0)
                tile_sum = jnp.sum(lf, axis=-1, keepdims=True)
                att = jnp.exp(lb + ((ltot - cpre - tile_sum) + _ones_dot(lf, upper)))
                if on_diagonal:
                    att = jnp.where(mask, att, 0.0)
                e = _dot(d_o, vj, "nt") * att
                dlf = ce + _ones_dot(e, below)
                dzz = e - (e + dlf) * jnp.exp(lb)
                if on_diagonal:
                    dzz = jnp.where(mask, dzz, 0.0)
                dzz = dzz.astype(BF16)
                dq = dq + _dot(dzz, kj, "nn")
                dk_scr[:, cols, :] += _dot(dzz, qi, "tn")
                dv_scr[:, cols, :] += _dot(att.astype(BF16), d_o, "tn")
                return dq, cpre + tile_sum, ce + jnp.sum(e, axis=-1, keepdims=True)

            nkb = qblk // SB_BLOCK
            zero_col = jnp.zeros((hp, qblk, 1), F32)
            st = lax.fori_loop(0, i * nkb, lambda j, s: tile(j, s, False),
                               (jnp.zeros((hp, qblk, D_HEAD), F32), zero_col, zero_col))
            for d in range(nkb):
                st = tile(i * nkb + d, st, True)
            dq = st[0]
            for hh in range(hp):
                dq_ref[rows, hh * D_HEAD:(hh + 1) * D_HEAD] = (dq[hh] * scale).astype(BF16)
            return carry

        lax.fori_loop(0, t // qblk, qblock, 0)
        for hh in range(hp):
            hs = slice(hh * D_HEAD, (hh + 1) * D_HEAD)
            dk_ref[:, hs] = (dk_scr[hh] * scale).astype(BF16)
            dv_ref[:, hs] = dv_scr[hh].astype(BF16)

    def head(off):
        return pl.BlockSpec((t, wid), lambda h: (0, off // wid + h))

    extra_specs, extra = [], []
    if after is not None:
        extra_specs, extra = [pl.BlockSpec(after.shape, lambda h: (0, 0))], [after]
    return pl.pallas_call(
        body, name="sb_bwd", grid=(N_HEADS // hp,),
        in_specs=[head(C_SBQ), head(C_SBQ + D_MODEL), head(C_SBQ + 2 * D_MODEL), head(C_SBZ), head(0),
                  pl.BlockSpec((hp, t, 1), lambda h: (h, 0, 0)), head(0)] + extra_specs,
        out_specs=[head(0)] * 4, out_shape=[jax.ShapeDtypeStruct((t, D_MODEL), BF16)] * 4,
        scratch_shapes=[pltpu.VMEM((hp, t, D_HEAD), BF16)] * 4 + [pltpu.VMEM((hp, t, D_HEAD), F32)] * 2,
        compiler_params=_cp(),
    )(proj, proj, proj, proj, o, ltot, dog, *extra)


def _mem_fwd(proj, mkv):
    t = proj.shape[0]
    tq = _pick(t, (512, 256))
    m_len = mkv.shape[0]
    scale = 1.0 / math.sqrt(MEM_DH)

    def body(q_ref, z_ref, kv_ref, o_ref, og_ref):
        q = q_ref[...]
        mk = kv_ref[:, :MEM_W].astype(BF16)
        mv = kv_ref[:, MEM_W:].astype(BF16)
        lane = lax.broadcasted_iota(jnp.int32, q.shape, 1) >> 6
        o = jnp.zeros(q.shape, F32)
        for h in range(MEM_HEADS):
            s = _bdot(jnp.where(lane == h, q, 0.0), mk, "nt") * scale
            p = jnp.exp(s - jnp.max(s, axis=-1, keepdims=True))
            p = p / jnp.sum(p, axis=-1, keepdims=True)
            o = o + jnp.where(lane == h, _bdot(p, mv, "nn"), 0.0)
        z = z_ref[...]
        o_ref[...] = o
        og_ref[...] = (o * (z * _sigmoid(z))).astype(BF16)

    out = pl.BlockSpec((tq, MEM_W), lambda i: (i, 0))
    return pl.pallas_call(
        body, name="mem_fwd", grid=(t // tq,),
        in_specs=[pl.BlockSpec((tq, MEM_W), lambda i: (i, C_MQ // MEM_W)),
                  pl.BlockSpec((tq, MEM_W), lambda i: (i, C_MZ // MEM_W)),
                  pl.BlockSpec((m_len, 2 * MEM_W), lambda i: (0, 0))],
        out_specs=[out, out],
        out_shape=[jax.ShapeDtypeStruct((t, MEM_W), F32), jax.ShapeDtypeStruct((t, MEM_W), BF16)],
        compiler_params=_cp(),
    )(proj, proj, mkv)


def _mem_bwd(proj, mkv, o, dog):
    t = proj.shape[0]
    tq = _pick(t, (512, 256))
    m_len = mkv.shape[0]
    scale = 1.0 / math.sqrt(MEM_DH)

    def body(q_ref, z_ref, kv_ref, o_ref, d_ref, dq_ref, dz_ref, dkv_ref):
        @pl.when(pl.program_id(0) == 0)
        def _():
            dkv_ref[...] = jnp.zeros_like(dkv_ref)

        q = q_ref[...]
        z = z_ref[...]
        sg = _sigmoid(z)
        dgo = d_ref[...]
        d_o = dgo * (z * sg)
        dz_ref[...] = (dgo * o_ref[...] * (sg * (1.0 + z * (1.0 - sg)))).astype(BF16)
        mk = kv_ref[:, :MEM_W].astype(BF16)
        mv = kv_ref[:, MEM_W:].astype(BF16)
        lane = lax.broadcasted_iota(jnp.int32, q.shape, 1) >> 6
        klane = lax.broadcasted_iota(jnp.int32, (m_len, MEM_W), 1) >> 6
        dq = jnp.zeros(q.shape, F32)
        dmk = jnp.zeros((m_len, MEM_W), F32)
        dmv = jnp.zeros((m_len, MEM_W), F32)
        for h in range(MEM_HEADS):
            qh = jnp.where(lane == h, q, 0.0)
            doh = jnp.where(lane == h, d_o, 0.0)
            s = _bdot(qh, mk, "nt") * scale
            p = jnp.exp(s - jnp.max(s, axis=-1, keepdims=True))
            p = p / jnp.sum(p, axis=-1, keepdims=True)
            dp = _bdot(doh, mv, "nt")
            ds = p * (dp - jnp.sum(dp * p, axis=-1, keepdims=True)) * scale
            dq = dq + jnp.where(lane == h, _bdot(ds, mk, "nn"), 0.0)
            dmk = dmk + jnp.where(klane == h, _bdot(ds, qh, "tn"), 0.0)
            dmv = dmv + jnp.where(klane == h, _bdot(p, doh, "tn"), 0.0)
        dq_ref[...] = dq.astype(BF16)
        dkv_ref[:, :MEM_W] += dmk
        dkv_ref[:, MEM_W:] += dmv

    blk = pl.BlockSpec((tq, MEM_W), lambda i: (i, 0))
    kv = pl.BlockSpec((m_len, 2 * MEM_W), lambda i: (0, 0))
    return pl.pallas_call(
        body, name="mem_bwd", grid=(t // tq,),
        in_specs=[pl.BlockSpec((tq, MEM_W), lambda i: (i, C_MQ // MEM_W)),
                  pl.BlockSpec((tq, MEM_W), lambda i: (i, C_MZ // MEM_W)), kv, blk, blk],
        out_specs=[blk, blk, kv],
        out_shape=[jax.ShapeDtypeStruct((t, MEM_W), BF16), jax.ShapeDtypeStruct((t, MEM_W), BF16),
                   jax.ShapeDtypeStruct((m_len, 2 * MEM_W), F32)], compiler_params=_cp(),
    )(proj, proj, mkv, o, dog)


_GW = 512


def _merge_fwd(proj, y_dn, y_sb, y_m):
    t = proj.shape[0]
    tb = _pick(t, (256,))
    nc = D_MODEL // _GW

    def body(g1, g2, g3, y1, y2, y3, out_ref):
        out_ref[...] = (_sigmoid(g1[...]) * y1[...] + _sigmoid(g2[...]) * y2[...] + _sigmoid(g3[...]) * y3[...]).astype(BF16)

    def gate(kb):
        return pl.BlockSpec((tb, _GW), lambda i, c: (i, C_GATES // _GW + kb * nc + c))

    blk = pl.BlockSpec((tb, _GW), lambda i, c: (i, c))
    return pl.pallas_call(
        body, name="merge_fwd", grid=(t // tb, nc), in_specs=[gate(0), gate(1), gate(2), blk, blk, blk],
        out_specs=blk, out_shape=jax.ShapeDtypeStruct((t, D_MODEL), BF16), compiler_params=_cp(),
    )(proj, proj, proj, y_dn, y_sb, y_m)


def _merge_bwd(proj, y_dn, y_sb, y_m, dm):
    t = proj.shape[0]
    tb = _pick(t, (256,))
    nc = D_MODEL // _GW

    def body(g1, g2, g3, y1, y2, y3, dm_ref, d1, d2, d3, dg1, dg2, dg3):
        d = dm_ref[...]
        for g, y, dy, dg in ((g1, y1, d1, dg1), (g2, y2, d2, dg2), (g3, y3, d3, dg3)):
            s = _sigmoid(g[...])
            dy[...] = (d * s).astype(BF16)
            dg[...] = (d * y[...] * (s * (1.0 - s))).astype(BF16)

    def gate(kb):
        return pl.BlockSpec((tb, _GW), lambda i, c: (i, C_GATES // _GW + kb * nc + c))

    blk = pl.BlockSpec((tb, _GW), lambda i, c: (i, c))
    act = jax.ShapeDtypeStruct((t, D_MODEL), BF16)
    return pl.pallas_call(
        body, name="merge_bwd", grid=(t // tb, nc), in_specs=[gate(0), gate(1), gate(2), blk, blk, blk, blk],
        out_specs=[blk] * 6, out_shape=[act] * 6, compiler_params=_cp(),
    )(proj, proj, proj, y_dn, y_sb, y_m, dm)


def _final_loss(x, mo, g, tgt):
    t, d = x.shape
    tb = _pick(t, (256,))

    def body(x_ref, mo_ref, g_ref, t_ref, do_ref, dob_ref, loss_ref, dg_ref):
        @pl.when(pl.program_id(0) == 0)
        def _():
            loss_ref[...] = jnp.zeros_like(loss_ref)
            dg_ref[...] = jnp.zeros_like(dg_ref)

        out = x_ref[...] + mo_ref[...]
        r = lax.rsqrt(jnp.mean(out * out, axis=-1, keepdims=True) + NORM_EPS)
        xhat = out * r
        gv = g_ref[...]
        err = xhat * gv - t_ref[...]
        per_tok = jnp.mean(err * err, axis=-1, keepdims=True)
        loss_ref[...] += 0.5 * jnp.sum(per_tok, axis=0, keepdims=True)
        dy = err * (1.0 / d)
        dg_ref[...] += jnp.sum(dy * xhat, axis=0, keepdims=True)
        dxh = dy * gv
        dout = r * (dxh - xhat * jnp.mean(dxh * xhat, axis=-1, keepdims=True))
        do_ref[...] = dout
        dob_ref[...] = dout.astype(BF16)

    row = pl.BlockSpec((tb, d), lambda i: (i, 0))
    vec = pl.BlockSpec((1, d), lambda i: (0, 0))
    return pl.pallas_call(
        body, name="final_loss", grid=(t // tb,), in_specs=[row, row, vec, row],
        out_specs=[row, row, pl.BlockSpec((1, 128), lambda i: (0, 0)), vec],
        out_shape=[jax.ShapeDtypeStruct((t, d), F32), jax.ShapeDtypeStruct((t, d), BF16),
                   jax.ShapeDtypeStruct((1, 128), F32), jax.ShapeDtypeStruct((1, d), F32)],
        compiler_params=_cp(),
    )(x, mo, g, tgt)


def _cast_bf16(a, name):
    r, c = a.shape
    tb = _pick(r, (128, 496, 240))

    def body(a_ref, o_ref):
        o_ref[...] = a_ref[...].astype(BF16)

    blk = pl.BlockSpec((tb, c), lambda i: (i, 0))
    return pl.pallas_call(body, name=name, grid=(r // tb,), in_specs=[blk], out_specs=blk,
                          out_shape=jax.ShapeDtypeStruct((r, c), BF16), compiler_params=_cp())(a)


WIN_START = (0, 23, 45, 68)
_S1_LO, _S1_HI = 1148, 1164
_S1_BA_POS = SHARD_PAD - 128


def _to_window(x, s):
    if s == 0:
        return x
    if s in (2, 3):
        return pltpu.roll(x, 120 if s == 2 else 124, 1)
    pos = lax.broadcasted_iota(jnp.int32, x.shape, 1)
    head = pltpu.roll(x, 4, 1)
    tail = pltpu.roll(x, SHARD_PAD - 12, 1)
    ba = jnp.where(pos < _S1_BA_POS + (_S1_HI - _S1_LO), pltpu.roll(x, _S1_BA_POS - _S1_LO, 1), 0.0)
    return jnp.where(pos < _S1_LO + 4, head, jnp.where(pos < _S1_BA_POS, tail, ba))


def _from_window(g, s):
    if s == 0:
        return g
    if s in (2, 3):
        return pltpu.roll(g, SHARD_PAD - (120 if s == 2 else 124), 1)
    col = lax.broadcasted_iota(jnp.int32, g.shape, 1)
    head = pltpu.roll(g, SHARD_PAD - 4, 1)
    tail = pltpu.roll(g, 12, 1)
    ba = pltpu.roll(g, SHARD_PAD - (_S1_BA_POS - _S1_LO), 1)
    return jnp.where(col < _S1_LO, head, jnp.where(col < _S1_HI, ba, tail))


def _cast_to_window(w, shard, name):
    r, c = w.shape
    tb = _pick(r, (128,))

    def body(s_ref, w_ref, o_ref, pad_scr):
        pad_scr[...] = jnp.zeros_like(pad_scr)
        pad_scr[:, :c] = w_ref[...]
        x = pad_scr[...]
        for s in range(N_SHARD):
            @pl.when(s_ref[0] == s)
            def _():
                o_ref[...] = _to_window(x, s).astype(BF16)

    return pl.pallas_call(
        body, name=name,
        grid_spec=pltpu.PrefetchScalarGridSpec(
            num_scalar_prefetch=1, grid=(r // tb,),
            in_specs=[pl.BlockSpec((tb, c), lambda i, s: (i, 0))],
            out_specs=pl.BlockSpec((tb, SHARD_PAD), lambda i, s: (i, 0)),
            scratch_shapes=[pltpu.VMEM((tb, SHARD_PAD), F32)]),
        out_shape=jax.ShapeDtypeStruct((r, SHARD_PAD), BF16), compiler_params=_cp(),
    )(shard, w)


def _pair_add(g, recv, c_idx, name):
    n, r, c = g.shape
    half = r // 2
    tb = _pick(half, (128, 240))
    nb = half // tb

    def body(c_ref, g_ref, r_ref, o_ref):
        o_ref[...] = (g_ref[...].astype(F32) + r_ref[...].astype(F32)).astype(BF16)

    blk = pl.BlockSpec((n, tb, c), lambda i, c_ref: (0, i, 0))
    return pl.pallas_call(
        body, name=name,
        grid_spec=pltpu.PrefetchScalarGridSpec(
            num_scalar_prefetch=1, grid=(nb,),
            in_specs=[pl.BlockSpec((n, tb, c), lambda i, c_ref: (0, c_ref[0] * nb + i, 0)), blk], out_specs=blk),
        out_shape=jax.ShapeDtypeStruct((n, half, c), BF16), compiler_params=_cp(),
    )(c_idx, g, recv)


def _chip_sum(parts, by_chip, place, name):
    n, h, c = parts.shape
    tb = _pick(h, (128, 240))
    nb = h // tb

    def body(p_ref, mine_ref, *rest):
        others, o_ref = rest[:n], rest[n]
        me = jnp.zeros((tb, c), jnp.int32) + p_ref[0]
        acc = None
        for q in range(n):
            term = jnp.where(me == q, mine_ref[...], others[q][...]).astype(F32)
            acc = term if acc is None else acc + term
        o_ref[...] = acc

    def other(q):
        return pl.BlockSpec((None, tb, c), lambda i, p: (jnp.where(p[0] == q, (q + 1) % n, q), i, 0))

    return pl.pallas_call(
        body, name=name,
        grid_spec=pltpu.PrefetchScalarGridSpec(
            num_scalar_prefetch=1, grid=(nb,),
            in_specs=[pl.BlockSpec((None, tb, c), lambda i, p: (p[0], i, 0))] + [other(q) for q in range(n)],
            out_specs=pl.BlockSpec((tb, c), lambda i, p: (p[1] * nb + i, 0))),
        out_shape=jax.ShapeDtypeStruct((2 * h, c), F32), compiler_params=_cp(),
    )(place, parts, *([by_chip] * n))


def _adamw_math(w, g, m, v):
    m = ADAM_B1 * m + (1.0 - ADAM_B1) * g
    v = ADAM_B2 * v + (1.0 - ADAM_B2) * (g * g)
    m_hat = m / (1.0 - ADAM_B1 ** ADAM_STEP)
    v_hat = v / (1.0 - ADAM_B2 ** ADAM_STEP)
    delta = -ADAM_LR * (m_hat / (jnp.sqrt(v_hat) + ADAM_EPS) + ADAM_WD * w)
    return delta, m, v


def _adamw(w, g, m, v, name):
    r, c = w.shape
    tb = _pick(r, (128, 496, 240))

    def body(w_ref, g_ref, m_ref, v_ref, go_ref, d_ref, mo_ref, vo_ref):
        gv = g_ref[...]
        d, mn, vn = _adamw_math(w_ref[...], gv, m_ref[...], v_ref[...])
        go_ref[...] = gv
        d_ref[...] = d
        mo_ref[...] = mn
        vo_ref[...] = vn

    blk = pl.BlockSpec((tb, c), lambda i: (i, 0))
    return pl.pallas_call(
        body, name=name, grid=(r // tb,), in_specs=[blk] * 4, out_specs=[blk] * 4,
        out_shape=[jax.ShapeDtypeStruct((r, c), F32)] * 4, compiler_params=_cp(),
    )(w, g, m, v)


def _adamw_window(w, g_win, m, v, shard, name):
    r, c = w.shape
    tb = _pick(r, (128,))

    def body(s_ref, w_ref, g_ref, m_ref, v_ref, go_ref, d_ref, mo_ref, vo_ref, g_scr):
        gw = g_ref[...]
        for s in range(N_SHARD):
            @pl.when(s_ref[0] == s)
            def _():
                g_scr[...] = _from_window(gw, s)

        gv = g_scr[:, :c]
        d, mn, vn = _adamw_math(w_ref[...], gv, m_ref[...], v_ref[...])
        go_ref[...] = gv
        d_ref[...] = d
        mo_ref[...] = mn
        vo_ref[...] = vn

    blk = pl.BlockSpec((tb, c), lambda i, s: (i, 0))
    return pl.pallas_call(
        body, name=name,
        grid_spec=pltpu.PrefetchScalarGridSpec(
            num_scalar_prefetch=1, grid=(r // tb,),
            in_specs=[blk, pl.BlockSpec((tb, SHARD_PAD), lambda i, s: (i, 0)), blk, blk], out_specs=[blk] * 4,
            scratch_shapes=[pltpu.VMEM((tb, SHARD_PAD), F32)]),
        out_shape=[jax.ShapeDtypeStruct((r, c), F32)] * 4, compiler_params=_cp(),
    )(shard, w, g_win, m, v)


def _small_update(gathered, w, m, v):
    def body(p_ref, w_ref, m_ref, v_ref, g_ref, d_ref, mo_ref, vo_ref):
        g = p_ref[0]
        for i in range(1, N_DEV):
            g = g + p_ref[i]
        d, mn, vn = _adamw_math(w_ref[...], g, m_ref[...], v_ref[...])
        g_ref[...] = g
        d_ref[...] = d
        mo_ref[...] = mn
        vo_ref[...] = vn

    full = pl.BlockSpec((S_ROWS, 128), lambda i: (0, 0))
    return pl.pallas_call(
        body, name="small_update", grid=(1,),
        in_specs=[pl.BlockSpec((N_DEV, S_ROWS, 128), lambda i: (0, 0, 0)), full, full, full], out_specs=[full] * 4,
        out_shape=[jax.ShapeDtypeStruct((S_ROWS, 128), F32)] * 4, compiler_params=_cp(),
    )(gathered, w, m, v)


_ANY = pl.BlockSpec(memory_space=pl.ANY)


def _place():
    x, y, c = lax.axis_index("x"), lax.axis_index("y"), lax.axis_index("c")
    chips = [(1 - x, y), (x, 1 - y), (1 - x, 1 - y)]
    return x, y, c, chips


def _gather_shards(arrs):
    n = len(arrs)

    def body(*refs):
        ins, outs = refs[:n], refs[n:2 * n]
        send_sems, recv_sems, local_sems = refs[2 * n:2 * n + 3]
        bufs = refs[2 * n + 3:]
        x, y, c, chips = _place()
        me = 2 * x + y
        sibling = (x, y, 1 - c)
        sends = []
        for a in range(n):
            half = ins[a].shape[0] // 2
            mine = pl.ds(pl.multiple_of(c * half, 16), half)
            for j, (qx, qy) in enumerate(chips):
                cp = pltpu.make_async_remote_copy(
                    src_ref=ins[a].at[mine], dst_ref=outs[a].at[me, mine],
                    send_sem=send_sems.at[6 * a + j], recv_sem=recv_sems.at[6 * a + j],
                    device_id=(qx, qy, c), device_id_type=MESH)
                cp.start()
                sends.append(cp)
        for a in range(n):
            step = bufs[a].shape[0]
            for r0 in range(0, ins[a].shape[0], step):
                rows = pl.ds(r0, step)
                load = pltpu.make_async_copy(ins[a].at[rows], bufs[a], local_sems.at[2 * a])
                load.start()
                load.wait()
                store = pltpu.make_async_copy(bufs[a], outs[a].at[me, rows], local_sems.at[2 * a + 1])
                store.start()
                store.wait()
        for a in range(n):
            half = ins[a].shape[0] // 2
            mine = pl.ds(pl.multiple_of(c * half, 16), half)
            for j, (qx, qy) in enumerate(chips):
                q = 2 * qx + qy
                landed = outs[a].at[q, mine]
                pltpu.make_async_remote_copy(
                    src_ref=landed, dst_ref=landed, send_sem=send_sems.at[6 * a + j], recv_sem=recv_sems.at[6 * a + j],
                    device_id=(qx, qy, c), device_id_type=MESH).wait_recv()
                fw = pltpu.make_async_remote_copy(
                    src_ref=landed, dst_ref=landed, send_sem=send_sems.at[6 * a + 3 + j],
                    recv_sem=recv_sems.at[6 * a + 3 + j], device_id=sibling, device_id_type=MESH)
                fw.start()
                sends.append(fw)
        for a in range(n):
            half = ins[a].shape[0] // 2
            theirs = pl.ds(pl.multiple_of((1 - c) * half, 16), half)
            for j, (qx, qy) in enumerate(chips):
                q = 2 * qx + qy
                dst = outs[a].at[q, theirs]
                pltpu.make_async_remote_copy(
                    src_ref=dst, dst_ref=dst, send_sem=send_sems.at[6 * a + 3 + j], recv_sem=recv_sems.at[6 * a + 3 + j],
                    device_id=sibling, device_id_type=MESH).wait_recv()
        for cp in sends:
            cp.wait_send()

    return pl.pallas_call(
        body, name="gather_shards", in_specs=[_ANY] * n, out_specs=[_ANY] * n,
        out_shape=[jax.ShapeDtypeStruct((N_SHARD,) + a.shape, a.dtype) for a in arrs],
        scratch_shapes=[pltpu.SemaphoreType.DMA((6 * n,)), pltpu.SemaphoreType.DMA((6 * n,)),
                        pltpu.SemaphoreType.DMA((2 * n,))]
        + [pltpu.VMEM((_pick(a.shape[0], (256, 496)), a.shape[1]), a.dtype) for a in arrs],
        compiler_params=pltpu.CompilerParams(has_side_effects=True, vmem_limit_bytes=VMEM_LIMIT),
    )(*arrs)


def _pair_reduce_send(grads, tag):
    n = len(grads)

    def body(*refs):
        ins, outs = refs[:n], refs[n:2 * n]
        send_sems, recv_sems = refs[2 * n:]
        x, y, c, _ = _place()
        sibling = (x, y, 1 - c)
        cps = []
        for a in range(n):
            half = ins[a].shape[1] // 2
            theirs = pl.ds(pl.multiple_of((1 - c) * half, 8), half)
            cp = pltpu.make_async_remote_copy(
                src_ref=ins[a].at[:, theirs], dst_ref=outs[a], send_sem=send_sems.at[a], recv_sem=recv_sems.at[a],
                device_id=sibling, device_id_type=MESH)
            cp.start()
            cps.append(cp)
        for cp in cps:
            cp.wait()

    return pl.pallas_call(
        body, name="pair_reduce_send_" + tag, in_specs=[_ANY] * n, out_specs=[_ANY] * n,
        out_shape=[jax.ShapeDtypeStruct((g.shape[0], g.shape[1] // 2, g.shape[2]), g.dtype) for g in grads],
        scratch_shapes=[pltpu.SemaphoreType.DMA((n,)), pltpu.SemaphoreType.DMA((n,))],
        compiler_params=pltpu.CompilerParams(has_side_effects=True),
    )(*grads)


_HBM = pl.BlockSpec(memory_space=pltpu.HBM)
_SEM = pl.BlockSpec(memory_space=pltpu.SEMAPHORE)
_DATAFLOW = pltpu.SideEffectType.DATAFLOW_SIDE_EFFECTING


def _chip_exchange_copies(ins, lands, send_sems, recv_sems):
    x, y, c, chips = _place()
    me = 2 * x + y
    cps = []
    for a in range(len(ins)):
        for j, (qx, qy) in enumerate(chips):
            cps.append(pltpu.make_async_remote_copy(
                src_ref=ins[a].at[2 * qx + qy], dst_ref=lands[a].at[me], send_sem=send_sems.at[3 * a + j],
                recv_sem=recv_sems.at[3 * a + j], device_id=(qx, qy, c), device_id_type=MESH))
    return cps


def _chip_exchange_start(parts, tag):
    n = len(parts)

    def body(*refs):
        ins, lands = refs[:n], refs[n:2 * n]
        send_sems, recv_sems = refs[2 * n:2 * n + 2]
        token = refs[4 * n + 2]
        for cp in _chip_exchange_copies(ins, lands, send_sems, recv_sems):
            cp.start()
        token[...] = jnp.zeros_like(token)

    hbm = [pltpu.HBM(p.shape, p.dtype) for p in parts]
    lands = [pltpu.with_memory_space_constraint(lax.empty(p.shape, p.dtype), pltpu.HBM) for p in parts]
    res = pl.pallas_call(
        body, name="chip_exchange_start_" + tag,
        out_shape=(pltpu.SemaphoreType.DMA((3 * n,)), pltpu.SemaphoreType.DMA((3 * n,)), *hbm, *hbm,
                   jax.ShapeDtypeStruct((8, 128), F32)),
        in_specs=[_HBM] * (2 * n), out_specs=(_SEM, _SEM, *([_HBM] * (2 * n)), pl.BlockSpec(memory_space=pltpu.VMEM)),
        input_output_aliases={a: 2 + a for a in range(2 * n)},
        compiler_params=pltpu.CompilerParams(has_side_effects=_DATAFLOW),
    )(*[pltpu.with_memory_space_constraint(p, pltpu.HBM) for p in parts], *lands)
    return res[0], res[1], res[2:2 + n], res[2 + n:2 + 2 * n], res[2 + 2 * n]


def _chip_exchange_wait(send_sems, recv_sems, parts, lands, after, tag):
    n = len(parts)

    def body(*refs):
        ins, land_refs = refs[:n], refs[n:2 * n]
        s_sems, r_sems = refs[2 * n:2 * n + 2]
        for cp in _chip_exchange_copies(ins, land_refs, s_sems, r_sems):
            cp.wait_send()
            cp.wait_recv()

    hbm = [pltpu.HBM(p.shape, p.dtype) for p in parts]
    res = pl.pallas_call(
        body, name="chip_exchange_wait_" + tag, out_shape=(*hbm, *hbm),
        in_specs=[_HBM] * (2 * n) + [_SEM, _SEM, _ANY], out_specs=tuple([_HBM] * (2 * n)),
        input_output_aliases={a: a for a in range(2 * n)},
        compiler_params=pltpu.CompilerParams(has_side_effects=_DATAFLOW),
    )(*parts, *lands, send_sems, recv_sems, after)
    return res[:n], res[n:]


def _shard_gather_copies(src, land, send_sems, recv_sems):
    x, y, c, chips = _place()
    me = 2 * x + y
    return [pltpu.make_async_remote_copy(
        src_ref=src, dst_ref=land.at[me], send_sem=send_sems.at[j], recv_sem=recv_sems.at[j],
        device_id=(qx, qy, c), device_id_type=MESH) for j, (qx, qy) in enumerate(chips)]


def _shard_gather_start(shard_arr, after):
    def body(src, land, after_ref, send_sems, recv_sems, src_thru, land_thru, token):
        for cp in _shard_gather_copies(src, land, send_sems, recv_sems):
            cp.start()
        token[...] = jnp.zeros_like(token)

    land_shape = (N_SHARD,) + shard_arr.shape
    land = pltpu.with_memory_space_constraint(lax.empty(land_shape, shard_arr.dtype), pltpu.HBM)
    return pl.pallas_call(
        body, name="shard_gather_start",
        out_shape=(pltpu.SemaphoreType.DMA((N_SHARD - 1,)), pltpu.SemaphoreType.DMA((N_SHARD - 1,)),
                   pltpu.HBM(shard_arr.shape, shard_arr.dtype), pltpu.HBM(land_shape, shard_arr.dtype),
                   jax.ShapeDtypeStruct((8, 128), F32)),
        in_specs=[_HBM, _HBM, _ANY], out_specs=(_SEM, _SEM, _HBM, _HBM, pl.BlockSpec(memory_space=pltpu.VMEM)),
        input_output_aliases={0: 2, 1: 3},
        compiler_params=pltpu.CompilerParams(has_side_effects=_DATAFLOW),
    )(pltpu.with_memory_space_constraint(shard_arr, pltpu.HBM), land, after)


def _shard_gather_wait(send_sems, recv_sems, shard_arr, land, after):
    def body(src, land_ref, s_sems, r_sems, after_ref, src_out, land_out):
        for cp in _shard_gather_copies(src, land_ref, s_sems, r_sems):
            cp.wait_send()
            cp.wait_recv()

    return pl.pallas_call(
        body, name="shard_gather_wait",
        out_shape=(pltpu.HBM(shard_arr.shape, shard_arr.dtype), pltpu.HBM(land.shape, land.dtype)),
        in_specs=[_HBM, _HBM, _SEM, _SEM, _ANY], out_specs=(_HBM, _HBM), input_output_aliases={0: 0, 1: 1},
        compiler_params=pltpu.CompilerParams(has_side_effects=_DATAFLOW),
    )(shard_arr, land, send_sems, recv_sems, after)


def _pair_allgather(fulls, tag):
    n = len(fulls)

    def body(*refs):
        outs = refs[n:2 * n]
        send_sems, recv_sems = refs[2 * n:]
        x, y, c, _ = _place()
        sibling = (x, y, 1 - c)
        cps = []
        for a in range(n):
            half = outs[a].shape[0] // 2
            mine = outs[a].at[pl.ds(pl.multiple_of(c * half, 8), half)]
            cp = pltpu.make_async_remote_copy(
                src_ref=mine, dst_ref=mine, send_sem=send_sems.at[a], recv_sem=recv_sems.at[a],
                device_id=sibling, device_id_type=MESH)
            cp.start()
            cps.append(cp)
        for a in range(n):
            half = outs[a].shape[0] // 2
            theirs = outs[a].at[pl.ds(pl.multiple_of((1 - c) * half, 8), half)]
            pltpu.make_async_remote_copy(
                src_ref=theirs, dst_ref=theirs, send_sem=send_sems.at[a], recv_sem=recv_sems.at[a],
                device_id=sibling, device_id_type=MESH).wait_recv()
        for cp in cps:
            cp.wait_send()

    return pl.pallas_call(
        body, name="pair_allgather_" + tag, in_specs=[_ANY] * n, out_specs=[_ANY] * n,
        out_shape=[jax.ShapeDtypeStruct(f.shape, f.dtype) for f in fulls],
        input_output_aliases={a: a for a in range(n)},
        scratch_shapes=[pltpu.SemaphoreType.DMA((n,)), pltpu.SemaphoreType.DMA((n,))],
        compiler_params=pltpu.CompilerParams(has_side_effects=True),
    )(*fulls)


def _allgather_small(slab):
    def body(s_ref, out_ref, send_sems, recv_sems):
        x, y, c, _ = _place()
        me = 4 * x + 2 * y + c
        out_ref[me] = s_ref[...]
        cps = []
        for mask in range(1, N_DEV):
            peer = (x ^ (mask >> 2), y ^ ((mask >> 1) & 1), c ^ (mask & 1))
            cp = pltpu.make_async_remote_copy(
                src_ref=s_ref, dst_ref=out_ref.at[me], send_sem=send_sems.at[mask - 1], recv_sem=recv_sems.at[mask - 1],
                device_id=peer, device_id_type=MESH)
            cp.start()
            cps.append(cp)
        for mask in range(1, N_DEV):
            peer = (x ^ (mask >> 2), y ^ ((mask >> 1) & 1), c ^ (mask & 1))
            dst = out_ref.at[4 * peer[0] + 2 * peer[1] + peer[2]]
            pltpu.make_async_remote_copy(
                src_ref=dst, dst_ref=dst, send_sem=send_sems.at[mask - 1], recv_sem=recv_sems.at[mask - 1],
                device_id=peer, device_id_type=MESH).wait_recv()
        for cp in cps:
            cp.wait_send()

    vm = pl.BlockSpec(memory_space=pltpu.VMEM)
    return pl.pallas_call(
        body, name="allgather_small", in_specs=[vm], out_specs=vm,
        out_shape=jax.ShapeDtypeStruct((N_DEV,) + slab.shape, slab.dtype),
        scratch_shapes=[pltpu.SemaphoreType.DMA((N_DEV - 1,)), pltpu.SemaphoreType.DMA((N_DEV - 1,))],
        compiler_params=pltpu.CompilerParams(has_side_effects=True),
    )(slab)


def _pack_b(w_mem_kv, w_br_dn, w_br_sb, w_br_mem, w_out):
    return jnp.concatenate([w_mem_kv.reshape(128, D_MODEL), w_br_dn, w_br_sb, w_br_mem.reshape(64, D_MODEL), w_out],
                           axis=0)


def _conv_slab(conv_w):
    return jnp.pad(conv_w.reshape(3, D_MODEL), ((0, 29), (0, 0)))


def _unpack_b(slab):
    return (slab[B_MEMKV:B_BRDN].reshape(1, 256, 512), slab[B_BRDN:B_BRSB].reshape(1, 256, D_MODEL),
            slab[B_BRSB:B_BRMEM].reshape(1, 256, D_MODEL), slab[B_BRMEM:B_OUT].reshape(1, 256, 256),
            slab[B_OUT:B_CONV].reshape(1, 256, D_MODEL))


def _conv_rows(conv_full):
    return conv_full.reshape(4 * CONV_BLOCKS, 128)


def _conv_shard_rows(conv_shard, shard):
    own = CONV_BLOCKS // N_SHARD
    blocks = lax.dynamic_update_slice(jnp.zeros((4, CONV_BLOCKS, 128), F32), conv_shard.reshape(4, own, 128),
                                      (0, own * shard, 0))
    return blocks.reshape(4 * CONV_BLOCKS, 128)


def _conv_shard_of(rows, shard):
    own = CONV_BLOCKS // N_SHARD
    blocks = lax.dynamic_slice(rows.reshape(4, CONV_BLOCKS, 128), (0, own * shard, 0), (4, own, 128))
    return blocks.reshape(1, 4, own * 128)


def _pack_small(norm_g, mem_norm_g, final_g, dn_norm_g, a_log, dt_bias, conv_rows, loss=None):
    def row(v):
        v = v.reshape(1, -1).astype(F32)
        return jnp.pad(v, ((0, 0), (0, 128 - v.shape[1])))

    loss_row = row(jnp.zeros((1,), F32) if loss is None else jnp.reshape(loss, (1,)))
    rid = lax.broadcasted_iota(jnp.int32, (8, 128), 0) + S_DNNORM
    tile = jnp.where(rid == S_DNNORM, dn_norm_g.reshape(1, 128), jnp.where(
        rid == S_ALOG, row(a_log), jnp.where(rid == S_DTB, row(dt_bias), jnp.where(rid == S_LOSS, loss_row, 0.0))))
    return jnp.concatenate([norm_g.reshape(8, 128), mem_norm_g.reshape(8, 128), final_g.reshape(8, 128), tile,
                            conv_rows], axis=0)


def _unpack_small(slab, shard):
    return (slab[S_NORM:S_NORM + 8].reshape(1, D_MODEL), slab[S_MEMNORM:S_MEMNORM + 8].reshape(1, D_MODEL),
            slab[S_FINAL:S_FINAL + 8].reshape(D_MODEL), slab[S_DNNORM].reshape(1, 128),
            slab[S_ALOG, :N_HEADS].reshape(1, N_HEADS), slab[S_DTB, :N_HEADS].reshape(1, N_HEADS),
            _conv_shard_of(slab[S_CONV:], shard))


def _windows_to_w_r(win):
    b = 128
    s0, s1, s2, s3 = win[0], win[1], win[2], win[3]
    e1, e2, e3 = WIN_START[1] * b, WIN_START[2] * b, WIN_START[3] * b
    n1, n2 = e2 - e1, e3 - e2
    return jnp.concatenate([
        s0[:, :e1], s0[:, e1:e1 + b] + s1[:, :b],
        s1[:, b:n1], s1[:, n1:n1 + b] + s2[:, :b],
        s2[:, b:n2], s2[:, n2:n2 + b] + s3[:, :b],
        s3[:, b:], s1[:, _S1_BA_POS:], jnp.zeros((win.shape[1], W_R - C_BA - b), win.dtype)], axis=1)


def _dproj_windows(dproj_r):
    b = 128
    pieces = []
    for s in range(N_SHARD):
        lo = WIN_START[s] * b
        if s == 1:
            pieces += [dproj_r[:, lo:lo + _S1_BA_POS], dproj_r[:, C_BA:C_BA + b]]
        else:
            pieces.append(dproj_r[:, lo:lo + SHARD_PAD])
    return jnp.concatenate(pieces, axis=1)


def _local_step(x, mem, tgt, norm_g, mem_norm_g, w_r, w_sh, conv_w, a_log, dt_bias, dn_norm_g, proj_weights, final_g,
                on_early=None, after_gather=None):
    t = x.shape[0]
    final_row = final_g.reshape(1, D_MODEL)
    lanes_8_16 = ((0, 0), (N_HEADS, 128 - 2 * N_HEADS))
    alog_row = jnp.pad(a_log.reshape(1, N_HEADS), lanes_8_16)
    dtb_row = jnp.pad(dt_bias.reshape(1, N_HEADS), lanes_8_16)

    h = _rmsnorm_fwd(x, norm_g, "norm_fwd")
    proj = _mm(h, w_r, "nn", "in_proj", after=after_gather, tm_max=2048)
    qkv = _dn_prep_fwd(proj, conv_w)
    beta_t, g_t = _dn_gate_fwd(proj, alog_row, dtb_row)
    dn_u, dn_w, dn_qd, dn_kd, dn_a, tinv_all, dn_el = _dn_intra_fwd(qkv, beta_t, g_t)
    o_dn, dn_vn, s_all = _dn_scan_fwd(dn_u, dn_w, dn_qd, dn_kd, dn_a, dn_el)
    o_dn_g = _dn_post_fwd(o_dn, proj, dn_norm_g)
    o_sb, o_sb_g, sb_l = _sb_fwd(proj)
    w_mem_kv, w_br_dn, w_br_sb, w_br_mem, w_out = proj_weights(o_sb_g)
    mem_n = _rmsnorm_fwd(mem, mem_norm_g, "mem_norm_fwd")
    mkv = _mm(mem_n, w_mem_kv, "nn", "mem_kv")
    o_m, o_m_g = _mem_fwd(proj, mkv)
    y_dn = _mm(o_dn_g, w_br_dn, "nn", "br_dn")
    y_sb = _mm(o_sb_g, w_br_sb, "nn", "br_sb")
    y_m = _mm(o_m_g, w_br_mem, "nn", "br_mem")
    merged = _merge_fwd(proj, y_dn, y_sb, y_m)
    mo = _mm(merged, w_out, "nn", "out_proj")
    d_out, d_out_b, loss_row, g_final = _final_loss(x, mo, final_row, tgt)

    g_w_out = _mm(merged, d_out_b, "tn", "g_w_out", out_dtype=BF16)
    d_merged = _mm(d_out_b, w_out, "nt", "d_merged")
    dy_dn, dy_sb, dy_m, dg1, dg2, dg3 = _merge_bwd(proj, y_dn, y_sb, y_m, d_merged)
    g_w_br_dn = _mm(o_dn_g, dy_dn, "tn", "g_w_br_dn", out_dtype=BF16)
    g_w_br_sb = _mm(o_sb_g, dy_sb, "tn", "g_w_br_sb", out_dtype=BF16)
    g_w_br_mem = _mm(o_m_g, dy_m, "tn", "g_w_br_mem", out_dtype=BF16)
    d_o_dn_g = _mm(dy_dn, w_br_dn, "nt", "d_o_dn")
    d_o_sb_g = _mm(dy_sb, w_br_sb, "nt", "d_o_sb")
    d_o_m_g = _mm(dy_m, w_br_mem, "nt", "d_o_mem")

    d_mq, d_mz, d_mkv = _mem_bwd(proj, mkv, o_m, d_o_m_g)
    d_mkv_b = _cast_bf16(d_mkv, "cast_dmkv")
    g_w_mem_kv = _mm(mem_n, d_mkv_b, "tn", "g_w_mem_kv", out_dtype=BF16)
    d_mem_n = _mm(d_mkv_b, w_mem_kv, "nt", "d_mem_n")
    _, g_mem_norm = _rmsnorm_bwd(mem, mem_norm_g, d_mem_n, jnp.zeros_like(mem), "mem_norm_bwd")

    early = dict(w_mem_kv=g_w_mem_kv, w_br_dn=g_w_br_dn, w_br_sb=g_w_br_sb, w_br_mem=g_w_br_mem, w_out=g_w_out)
    after_early = on_early(early) if on_early is not None else None

    d_sq, d_sk, d_sv, d_sz = _sb_bwd(proj, o_sb, sb_l, d_o_sb_g, after=after_early)

    d_o_dn, d_dnz, g_dn_norm = _dn_post_bwd(o_dn, proj, dn_norm_g, d_o_dn_g)
    d_vnew, d_kd, d_qd, d_w, d_el = _dn_scan_bwd(dn_w, dn_qd, dn_kd, dn_a, dn_el, dn_vn, s_all, d_o_dn)
    d_qn, d_kn, d_vn, dbeta_t, dg_t = _dn_intra_bwd(qkv, beta_t, g_t, tinv_all, dn_vn, d_o_dn, d_vnew, d_kd, d_qd, d_w, d_el)
    d_conv_in, g_conv = _dn_prep_bwd(proj, conv_w, d_qn, d_kn, d_vn)
    d_ba, g_alog_row, g_dtb_row = _dn_gate_bwd(proj, alog_row, dtb_row, dbeta_t, dg_t)

    dproj_sh = _dproj_windows(
        jnp.concatenate([d_conv_in, d_dnz, d_sq, d_sk, d_sv, d_sz, d_mq, d_mz, dg1, dg2, dg3, d_ba], axis=1))
    g_w_sh = _mm(h, dproj_sh, "tn", "g_w_in", out_dtype=BF16, out_shards=N_SHARD, tn_max=1024)
    def input_grad(after=None):
        dh = _mm(dproj_sh, w_sh, "nt", "d_h", after=after, tm_max=2048, tn_max=1024)
        grad_x, g_norm = _rmsnorm_bwd(x, norm_g, dh, d_out, "norm_bwd")
        small = dict(norm_g=g_norm, mem_norm_g=g_mem_norm, final_g=g_final, dn_norm_g=g_dn_norm,
                     a_log=g_alog_row[:, N_HEADS:2 * N_HEADS], dt_bias=g_dtb_row[:, N_HEADS:2 * N_HEADS],
                     conv_w=g_conv)
        return grad_x, small

    return loss_row[0, 0], early, g_w_sh, input_grad


def _reduce_scatter_start(grads, tag):
    c = lax.axis_index("c")
    core = jnp.reshape(c, (1,)).astype(jnp.int32)
    recv = _pair_reduce_send(grads, tag)
    parts = [_pair_add(g, r, core, "pair_add_" + tag) for g, r in zip(grads, recv)]
    return _chip_exchange_start(parts, tag)


def _reduce_scatter_finish(handle, after, tag):
    send_sems, recv_sems, parts, lands, _ = handle
    x, y, c = lax.axis_index("x"), lax.axis_index("y"), lax.axis_index("c")
    place = jnp.stack([2 * x + y, c]).astype(jnp.int32)
    parts, by_chip = _chip_exchange_wait(send_sems, recv_sems, parts, lands, after, tag)
    fulls = [_chip_sum(p, b, place, "chip_sum_" + tag) for p, b in zip(parts, by_chip)]
    return _pair_allgather(fulls, tag)


def kernel(x, mem, norm_g, mem_norm_g, w_in, conv_w, a_log, dt_bias, dn_norm_g, w_mem_kv, w_br_dn, w_br_sb, w_br_mem, w_out, final_g, loss_target, m_norm_g, m_mem_norm_g, m_w_in, m_conv_w, m_a_log, m_dt_bias, m_dn_norm_g, m_w_mem_kv, m_w_br_dn, m_w_br_sb, m_w_br_mem, m_w_out, m_final_g, v_norm_g, v_mem_norm_g, v_w_in, v_conv_w, v_a_log, v_dt_bias, v_dn_norm_g, v_w_mem_kv, v_w_br_dn, v_w_br_sb, v_w_br_mem, v_w_out, v_final_g):
    w_a = w_in[0]
    w_b = _pack_b(w_mem_kv[0], w_br_dn[0], w_br_sb[0], w_br_mem[0], w_out[0])
    m_b = _pack_b(m_w_mem_kv[0], m_w_br_dn[0], m_w_br_sb[0], m_w_br_mem[0], m_w_out[0])
    v_b = _pack_b(v_w_mem_kv[0], v_w_br_dn[0], v_w_br_sb[0], v_w_br_mem[0], v_w_out[0])

    shard_idx = 2 * lax.axis_index("x") + lax.axis_index("y")
    shard = jnp.reshape(shard_idx, (1,)).astype(jnp.int32)
    ga, g_conv = _gather_shards([_cast_to_window(w_a, shard, "cast_w_in"),
                                 _cast_bf16(_conv_slab(conv_w[0]), "cast_conv")])
    w_r = _windows_to_w_r(ga)
    f_conv = g_conv[:, :3].reshape(N_SHARD, 4, 768).transpose(1, 0, 2).reshape(4, 3 * D_MODEL).astype(F32)
    b_flight = _shard_gather_start(_cast_bf16(w_b, "cast_w_b"), after=ga)

    def proj_weights(after):
        own, land = _shard_gather_wait(b_flight[0], b_flight[1], b_flight[2], b_flight[3], after)
        gb = lax.dynamic_update_slice(land, own[None], (shard_idx, 0, 0))
        return (gb[:, B_MEMKV:B_BRDN].reshape(N_SHARD * 256, 512),
                gb[:, B_BRDN:B_BRSB].reshape(N_SHARD * 256, D_MODEL),
                gb[:, B_BRSB:B_BRMEM].reshape(N_SHARD * 256, D_MODEL),
                gb[:, B_BRMEM:B_OUT].reshape(N_SHARD, 256, 256).transpose(1, 0, 2).reshape(256, D_MODEL),
                gb[:, B_OUT:B_CONV].reshape(N_SHARD * 256, D_MODEL))

    flights = {}

    def on_early(grads):
        g_b = jnp.concatenate([
            grads["w_mem_kv"].reshape(N_SHARD, 128, D_MODEL), grads["w_br_dn"].reshape(N_SHARD, 256, D_MODEL),
            grads["w_br_sb"].reshape(N_SHARD, 256, D_MODEL),
            grads["w_br_mem"].reshape(256, N_SHARD, 256).transpose(1, 0, 2).reshape(N_SHARD, 64, D_MODEL),
            grads["w_out"].reshape(N_SHARD, 256, D_MODEL)], axis=1).astype(BF16)
        flights["b"] = _reduce_scatter_start([g_b], "b")
        return flights["b"][4]

    loss, _, g_w_sh, input_grad = _local_step(
        x[0], mem[0], loss_target[0], norm_g, mem_norm_g, w_r, ga, f_conv, a_log, dt_bias, dn_norm_g,
        proj_weights, final_g, on_early=on_early, after_gather=b_flight[4])
    flights["a"] = _reduce_scatter_start([g_w_sh], "a")
    grad_x, small = input_grad(after=flights["a"][4])

    part = _pack_small(small["norm_g"], small["mem_norm_g"], small["final_g"], small["dn_norm_g"],
                       small["a_log"], small["dt_bias"], _conv_rows(small["conv_w"]), loss)
    w_s = _pack_small(norm_g, mem_norm_g, final_g, dn_norm_g, a_log, dt_bias, _conv_shard_rows(conv_w[0], shard_idx))
    m_s = _pack_small(m_norm_g, m_mem_norm_g, m_final_g, m_dn_norm_g, m_a_log, m_dt_bias,
                      _conv_shard_rows(m_conv_w[0], shard_idx))
    v_s = _pack_small(v_norm_g, v_mem_norm_g, v_final_g, v_dn_norm_g, v_a_log, v_dt_bias,
                      _conv_shard_rows(v_conv_w[0], shard_idx))
    g_s, d_s, nm_s, nv_s = _small_update(_allgather_small(part), w_s, m_s, v_s)

    (gs_b,) = _reduce_scatter_finish(flights["b"], after=g_s, tag="b")
    (gs_in,) = _reduce_scatter_finish(flights["a"], after=gs_b, tag="a")
    gr_in, d_in, nm_in, nv_in = _adamw_window(w_a, gs_in, m_w_in[0], v_w_in[0], shard, "adamw_w_in")
    gr_b, d_b, nm_b, nv_b = _adamw(w_b, gs_b, m_b, v_b, "adamw_b")

    def assemble(slab_small, a_in, slab_b):
        s_norm, s_memnorm, s_final, s_dnnorm, s_alog, s_dtb, b_conv = _unpack_small(slab_small, shard_idx)
        b_memkv, b_brdn, b_brsb, b_brmem, b_out = _unpack_b(slab_b)
        return [s_norm, s_memnorm, a_in.reshape(1, D_MODEL, IN_WIDTH // N_SHARD), b_conv, s_alog, s_dtb, s_dnnorm,
                b_memkv, b_brdn, b_brsb, b_brmem, b_out, s_final]

    outs = [g_s[S_LOSS, 0], grad_x.reshape(1, -1, D_MODEL)]
    outs += assemble(g_s, gr_in, gr_b)
    outs += assemble(d_s, d_in, d_b)
    outs += assemble(nm_s, nm_in, nm_b)
    outs += assemble(nv_s, nv_in, nv_b)
    return tuple(outs)
```

```python
import math

import jax
import jax.numpy as jnp
from jax import lax
from jax.experimental import pallas as pl
from jax.experimental.pallas import tpu as pltpu

F32 = jnp.float32
BF16 = jnp.bfloat16
MESH = pl.DeviceIdType.MESH

D_MODEL = 1024
N_HEADS = 8
D_HEAD = 128
DN_CHUNK = 64
DN_GROUP = 16
DN_SCAN_GROUP = 4
SB_BLOCK = 256
SB_HEADS_PER_STEP = 2
SB_QBLOCK = 256
MEM_HEADS = 4
MEM_DH = 64
MEM_W = MEM_HEADS * MEM_DH
NORM_EPS = 1e-6
IN_WIDTH = 11792
N_SHARD = 4
SHARD_W = IN_WIDTH // N_SHARD
SHARD_PAD = 3072
N_DEV = 8

C_DNZ = 3072
C_SBQ = 4096
C_SBZ = 7168
C_MQ = 8192
C_MZ = 8448
C_GATES = 8704
C_BA = 11776
W_R = 12288

ADAM_LR = 0.001
ADAM_B1 = 0.9
ADAM_B2 = 0.999
ADAM_EPS = 1e-08
ADAM_WD = 0.01
ADAM_STEP = 10

VMEM_LIMIT = 56 * 1024 * 1024

B_MEMKV, B_BRDN, B_BRSB, B_BRMEM, B_OUT, B_CONV = 0, 128, 384, 640, 704, 960
S_NORM, S_MEMNORM, S_FINAL, S_DNNORM, S_ALOG, S_DTB, S_LOSS, S_CONV, S_ROWS = 0, 8, 16, 24, 25, 26, 27, 32, 128
CONV_BLOCKS = 3 * D_MODEL // 128


def _cp(**kw):
    return pltpu.CompilerParams(vmem_limit_bytes=VMEM_LIMIT, **kw)


def _dot(a, b, dims):
    lead = a.ndim - 2
    ca, cb = {"nn": (1, 0), "nt": (1, 1), "tn": (0, 0)}[dims]
    batch = tuple(range(lead))
    return lax.dot_general(a, b, (((ca + lead,), (cb + lead,)), (batch, batch)), preferred_element_type=F32)


def _chunks(x):
    return x.reshape(x.shape[0] // DN_CHUNK, DN_CHUNK, x.shape[1])


def _unchunk(x):
    return x.reshape(x.shape[0] * x.shape[1], x.shape[2])


def _bdot(a, b, dims):
    return _dot(a.astype(BF16), b.astype(BF16), dims)


def _split(a):
    hi = a.astype(BF16)
    return hi, (a - hi.astype(F32)).astype(BF16)


def _dot3(a, b, dims):
    a1, a2 = _split(a)
    b1, b2 = _split(b)
    return _dot(a1, b1, dims) + (_dot(a1, b2, dims) + _dot(a2, b1, dims))


def _ones_dot(a, ones_bf16):
    out = _dot(a.reshape(-1, a.shape[-1]).astype(BF16), ones_bf16, "nn")
    return out.reshape(a.shape[:-1] + (ones_bf16.shape[1],))


def _sigmoid(x):
    return 1.0 / (1.0 + jnp.exp(-x))


def _log1p_small(u):
    return jnp.where(u < 1e-2, u * (1.0 - u * (0.5 - u * (1.0 / 3.0))), jnp.log(1.0 + u))


def _pick(dim, cands):
    for c in cands:
        if dim % c == 0:
            return c
    return dim


def _mm(a, b, dims, name, out_dtype=F32, out_shards=1, after=None, tm_max=1024, tn_max=512):
    ta, tb = dims[0] == "t", dims[1] == "t"
    m, k = (a.shape[1], a.shape[0]) if ta else a.shape
    b_shards = b.shape[0] if b.ndim == 3 else 1
    n = b.shape[-2] if tb else b.shape[-1]
    tm = _pick(m, (tm_max, 1024, 512, 256))
    tn = _pick(n // out_shards, (tn_max, 512, 384, 256, 128))
    tk = _pick(k // b_shards, (2048, 1024, 512, 384, 256))
    nk = k // tk

    def body(a_ref, b_ref, *rest):
        if nk == 1:
            rest[-1][...] = _bdot(a_ref[...], b_ref[...], dims).astype(out_dtype)
            return
        o_ref, acc_ref = rest[-2:]
        kk = pl.program_id(2)

        @pl.when(kk == 0)
        def _():
            acc_ref[...] = jnp.zeros_like(acc_ref)

        acc_ref[...] += _bdot(a_ref[...], b_ref[...], dims)

        @pl.when(kk == nk - 1)
        def _():
            o_ref[...] = acc_ref[...].astype(out_dtype)

    a_spec = pl.BlockSpec((tk, tm), lambda i, j, q: (q, i)) if ta else pl.BlockSpec((tm, tk), lambda i, j, q: (i, q))
    if b_shards > 1:
        per_k = k // b_shards // tk
        b_spec = pl.BlockSpec((None, tn, tk), lambda i, j, q: (q // per_k, j, q % per_k))
    else:
        b_spec = pl.BlockSpec((tn, tk), lambda i, j, q: (j, q)) if tb else pl.BlockSpec((tk, tn), lambda i, j, q: (q, j))
    if out_shards > 1:
        per_n = n // out_shards // tn
        out_spec = pl.BlockSpec((None, tm, tn), lambda i, j, q: (j // per_n, i, j % per_n))
        out_shape = jax.ShapeDtypeStruct((out_shards, m, n // out_shards), out_dtype)
    else:
        out_spec = pl.BlockSpec((tm, tn), lambda i, j, q: (i, j))
        out_shape = jax.ShapeDtypeStruct((m, n), out_dtype)
    extra_specs, extra = [], []
    if after is not None:
        extra_specs, extra = [pl.BlockSpec(after.shape, lambda i, j, q: (0, 0))], [after]
    return pl.pallas_call(
        body, name=name, grid=(m // tm, n // tn, nk),
        in_specs=[a_spec, b_spec] + extra_specs, out_specs=out_spec, out_shape=out_shape,
        scratch_shapes=[pltpu.VMEM((tm, tn), F32)] if nk > 1 else [],
        compiler_params=_cp(dimension_semantics=("parallel", "parallel", "arbitrary")),
    )(a, b, *extra)


def _rmsnorm_fwd(x, g, name):
    t, d = x.shape
    tb = _pick(t, (512, 256))

    def body(x_ref, g_ref, h_ref):
        xv = x_ref[...]
        r = lax.rsqrt(jnp.mean(xv * xv, axis=-1, keepdims=True) + NORM_EPS)
        h_ref[...] = ((xv * r) * g_ref[...]).astype(BF16)

    return pl.pallas_call(
        body, name=name, grid=(t // tb,),
        in_specs=[pl.BlockSpec((tb, d), lambda i: (i, 0)), pl.BlockSpec((1, d), lambda i: (0, 0))],
        out_specs=pl.BlockSpec((tb, d), lambda i: (i, 0)),
        out_shape=jax.ShapeDtypeStruct((t, d), BF16), compiler_params=_cp(),
    )(x, g)


def _rmsnorm_bwd(x, g, dh, resid, name):
    t, d = x.shape
    tb = _pick(t, (256,))

    def body(x_ref, g_ref, dh_ref, r_ref, dx_ref, dg_ref):
        @pl.when(pl.program_id(0) == 0)
        def _():
            dg_ref[...] = jnp.zeros_like(dg_ref)

        xv = x_ref[...]
        r = lax.rsqrt(jnp.mean(xv * xv, axis=-1, keepdims=True) + NORM_EPS)
        xhat = xv * r
        dhv = dh_ref[...]
        dg_ref[...] += jnp.sum(dhv * xhat, axis=0, keepdims=True)
        dxh = dhv * g_ref[...]
        dx_ref[...] = r_ref[...] + r * (dxh - xhat * jnp.mean(dxh * xhat, axis=-1, keepdims=True))

    row = pl.BlockSpec((tb, d), lambda i: (i, 0))
    vec = pl.BlockSpec((1, d), lambda i: (0, 0))
    return pl.pallas_call(
        body, name=name, grid=(t // tb,), in_specs=[row, vec, row, row], out_specs=[row, vec],
        out_shape=[jax.ShapeDtypeStruct((t, d), F32), jax.ShapeDtypeStruct((1, d), F32)], compiler_params=_cp(),
    )(x, g, dh, resid)


def _conv_silu(xv, w, row):
    y = xv * w[3:4, :]
    for s in (1, 2, 3):
        xs = jnp.where(row >= s, pltpu.roll(xv, s, 0), 0.0)
        y = y + xs * w[3 - s:4 - s, :]
    sig = _sigmoid(y)
    return y, sig, y * sig


def _dn_prep_fwd(proj, conv_w):
    t = proj.shape[0]

    def body(p_ref, w_ref, o_ref):
        j = pl.program_id(0)
        xv = p_ref[...]
        row = lax.broadcasted_iota(jnp.int32, xv.shape, 0)
        _, _, a = _conv_silu(xv, w_ref[...], row)
        inv = lax.rsqrt(jnp.sum(a * a, axis=-1, keepdims=True) + NORM_EPS)
        scale = jnp.where(j < N_HEADS, D_HEAD ** -0.5, 1.0)
        normed = jnp.where(j < 2 * N_HEADS, 1.0, 0.0)
        o_ref[...] = a * (normed * (inv * scale) + (1.0 - normed))

    return pl.pallas_call(
        body, name="dn_prep_fwd", grid=(3 * N_HEADS,),
        in_specs=[pl.BlockSpec((t, D_HEAD), lambda j: (0, j)), pl.BlockSpec((4, D_HEAD), lambda j: (0, j))],
        out_specs=pl.BlockSpec((t, D_HEAD), lambda j: (0, j)),
        out_shape=jax.ShapeDtypeStruct((t, 3 * D_MODEL), F32), compiler_params=_cp(),
    )(proj, conv_w)


def _dn_prep_bwd(proj, conv_w, dq, dk, dv):
    t = proj.shape[0]

    def body(p_ref, w_ref, dq_ref, dk_ref, dv_ref, dp_ref, dw_ref):
        j = pl.program_id(0)
        xv = p_ref[...]
        w = w_ref[...]
        row = lax.broadcasted_iota(jnp.int32, xv.shape, 0)
        y, s, a = _conv_silu(xv, w, row)
        part = jnp.zeros(xv.shape, jnp.int32) + j // N_HEADS
        dn = jnp.where(part == 0, dq_ref[...], jnp.where(part == 1, dk_ref[...], dv_ref[...]))
        inv = lax.rsqrt(jnp.sum(a * a, axis=-1, keepdims=True) + NORM_EPS)
        scale = jnp.where(j < N_HEADS, D_HEAD ** -0.5, 1.0)
        ds = dn * scale
        da_norm = inv * ds - a * (inv * inv * inv) * jnp.sum(ds * a, axis=-1, keepdims=True)
        normed = jnp.where(j < 2 * N_HEADS, 1.0, 0.0)
        da = normed * da_norm + (1.0 - normed) * dn
        dy = da * (s * (1.0 + y * (1.0 - s)))
        dx = dy * w[3:4, :]
        dw_ref[3:4, :] = jnp.sum(dy * xv, axis=0, keepdims=True)
        for sft in (1, 2, 3):
            xs = jnp.where(row >= sft, pltpu.roll(xv, sft, 0), 0.0)
            dw_ref[3 - sft:4 - sft, :] = jnp.sum(dy * xs, axis=0, keepdims=True)
            dys = jnp.where(row < t - sft, pltpu.roll(dy, t - sft, 0), 0.0)
            dx = dx + dys * w[3 - sft:4 - sft, :]
        dp_ref[...] = dx.astype(BF16)

    blk = pl.BlockSpec((t, D_HEAD), lambda j: (0, j))
    wblk = pl.BlockSpec((4, D_HEAD), lambda j: (0, j))

    def grad(part):
        return pl.BlockSpec((t, D_HEAD), lambda j: (0, jnp.clip(j - part * N_HEADS, 0, N_HEADS - 1)))

    return pl.pallas_call(
        body, name="dn_prep_bwd", grid=(3 * N_HEADS,), in_specs=[blk, wblk, grad(0), grad(1), grad(2)],
        out_specs=[blk, wblk],
        out_shape=[jax.ShapeDtypeStruct((t, 3 * D_MODEL), BF16), jax.ShapeDtypeStruct((4, 3 * D_MODEL), F32)],
        compiler_params=_cp(),
    )(proj, conv_w, dq, dk, dv)


def _softplus_parts(xv):
    e = jnp.exp(-jnp.abs(xv))
    return jnp.maximum(xv, 0.0) + _log1p_small(e)


def _chunk_scan(v, row, reverse):
    t = v.shape[0]
    pos = row & (DN_CHUNK - 1)
    s = 1
    while s < DN_CHUNK:
        if reverse:
            v = v + jnp.where(pos < DN_CHUNK - s, pltpu.roll(v, t - s, 0), 0.0)
        else:
            v = v + jnp.where(pos >= s, pltpu.roll(v, s, 0), 0.0)
        s *= 2
    return v


def _dn_gate_fwd(proj, alog_row, dtb_row):
    t = proj.shape[0]

    def body(p_ref, al_ref, dt_ref, b_ref, g_ref):
        p = p_ref[...]
        row = lax.broadcasted_iota(jnp.int32, p.shape, 0)
        b_ref[...] = _sigmoid(p)
        g = -jnp.exp(al_ref[...]) * _softplus_parts(p + dt_ref[...])
        g_ref[...] = _chunk_scan(g, row, reverse=False)

    blk = pl.BlockSpec((t, 128), lambda i: (0, C_BA // 128))
    vec = pl.BlockSpec((1, 128), lambda i: (0, 0))
    out = pl.BlockSpec((t, 128), lambda i: (0, 0))
    return pl.pallas_call(
        body, name="dn_gate_fwd", grid=(1,), in_specs=[blk, vec, vec], out_specs=[out, out],
        out_shape=[jax.ShapeDtypeStruct((t, 128), F32)] * 2, compiler_params=_cp(),
    )(proj, alog_row, dtb_row)


def _dn_gate_bwd(proj, alog_row, dtb_row, dbeta, dgc):
    t = proj.shape[0]

    def body(p_ref, al_ref, dt_ref, db_ref, dg_ref, dp_ref, dal_ref, ddt_ref):
        p = p_ref[...]
        row = lax.broadcasted_iota(jnp.int32, p.shape, 0)
        lane = lax.broadcasted_iota(jnp.int32, p.shape, 1)
        s = _sigmoid(p)
        d_b = db_ref[...] * s * (1.0 - s)
        dg = _chunk_scan(dg_ref[...], row, reverse=True)
        xa = p + dt_ref[...]
        ea = jnp.exp(al_ref[...])
        g = -ea * _softplus_parts(xa)
        d_a = dg * (-ea) * _sigmoid(xa)
        dp_ref[...] = jnp.where(lane < N_HEADS, d_b, jnp.where(lane < 2 * N_HEADS, d_a, 0.0)).astype(BF16)
        dal_ref[...] = jnp.sum(dg * g, axis=0, keepdims=True)
        ddt_ref[...] = jnp.sum(d_a, axis=0, keepdims=True)

    blk = pl.BlockSpec((t, 128), lambda i: (0, C_BA // 128))
    vec = pl.BlockSpec((1, 128), lambda i: (0, 0))
    full = pl.BlockSpec((t, 128), lambda i: (0, 0))
    return pl.pallas_call(
        body, name="dn_gate_bwd", grid=(1,), in_specs=[blk, vec, vec, full, full], out_specs=[full, vec, vec],
        out_shape=[jax.ShapeDtypeStruct((t, 128), BF16), jax.ShapeDtypeStruct((1, 128), F32),
                   jax.ShapeDtypeStruct((1, 128), F32)], compiler_params=_cp(),
    )(proj, alog_row, dtb_row, dbeta, dgc)


def _col_to_row(col, eye):
    return jnp.sum(jnp.where(eye, col, 0.0), axis=-2, keepdims=True)


def _row_to_col(rowv, eye):
    return jnp.sum(jnp.where(eye, rowv, 0.0), axis=-1, keepdims=True)


def _tri_inverse(m, ri, ci):
    eye = (ri == ci).astype(F32)
    b16 = (ri >> 4) == (ci >> 4)
    b32 = (ri >> 5) == (ci >> 5)
    m1 = jnp.where(b16, m, 0.0)
    x = eye - m1
    p = _dot3(m1, m1, "nn")
    x = x + _dot3(x, p, "nn")
    p = _dot3(p, p, "nn")
    x = x + _dot3(x, p, "nn")
    p = _dot3(p, p, "nn")
    x = x + _dot3(x, p, "nn")
    c1 = jnp.where(jnp.logical_and(b32, jnp.logical_not(b16)), m, 0.0)
    x = x - _dot3(_dot3(x, c1, "nn"), x, "nn")
    c2 = jnp.where(b32, 0.0, m)
    x = x - _dot3(_dot3(x, c2, "nn"), x, "nn")
    return x


def _dn_chunk_common(q, k, gc, ri, ci):
    eye = ri == ci
    g_row = _col_to_row(gc, eye)
    diff = jnp.minimum(gc - g_row, 0.0)
    gam = jnp.where(ri >= ci, jnp.exp(diff), 0.0)
    kk = _bdot(k, k, "nt")
    qk = _bdot(q, k, "nt")
    rcol = lax.broadcasted_iota(jnp.int32, gc.shape, gc.ndim - 2)
    last = jnp.sum(jnp.where(rcol == DN_CHUNK - 1, gc, 0.0), axis=-2, keepdims=True)
    e_g = jnp.exp(gc)
    dec = jnp.exp(last - gc)
    return eye, gam, kk, qk, last, e_g, dec, rcol


def _dn_specs(t, rows_blk):
    def head(off):
        return pl.BlockSpec((rows_blk, D_HEAD), lambda g, h: (g, off + h))

    lanes = pl.BlockSpec((rows_blk, 128), lambda g, h: (g, 0))
    hm = pl.BlockSpec((None, rows_blk, D_HEAD), lambda g, h: (h, g, 0))
    sq = pl.BlockSpec((1, rows_blk, DN_CHUNK), lambda g, h: (h, g, 0))
    tile = pl.BlockSpec((1, rows_blk // DN_CHUNK, 8, 128), lambda g, h: (h, g, 0, 0))
    return head, lanes, hm, sq, tile


def _head_column(slab, lane_idx):
    lane = lax.broadcasted_iota(jnp.int32, slab.shape, 1)
    return _chunks(jnp.sum(jnp.where(lane == lane_idx, slab, 0.0), axis=1, keepdims=True))


def _dn_intra_fwd(qkv, beta_t, g_t):
    t = qkv.shape[0]
    n_chunks = t // DN_CHUNK
    rows_blk = min(DN_GROUP * DN_CHUNK, t)

    def body(q_ref, k_ref, v_ref, b_ref, g_ref, u_ref, w_ref, qd_ref, kd_ref, a_ref, ti_ref, el_ref):
        ri = lax.broadcasted_iota(jnp.int32, (DN_CHUNK, DN_CHUNK), 0)
        ci = lax.broadcasted_iota(jnp.int32, (DN_CHUNK, DN_CHUNK), 1)
        h = pl.program_id(1)
        q, k, v = (_chunks(r[...]) for r in (q_ref, k_ref, v_ref))
        b, gc = _head_column(b_ref[...], h), _head_column(g_ref[...], h + N_HEADS)
        _, gam, kk, qk, last, e_g, dec, _ = _dn_chunk_common(q, k, gc, ri, ci)
        tinv = _tri_inverse(jnp.where(ri > ci, b * kk * gam, 0.0), ri, ci)
        u_ref[...] = _unchunk(_bdot(tinv, v * b, "nn"))
        w_ref[...] = _unchunk(_bdot(tinv, k * (b * e_g), "nn"))
        qd_ref[...] = _unchunk(q * e_g)
        kd_ref[...] = _unchunk(k * dec)
        a_ref[0] = _unchunk(qk * gam)
        ti_ref[0] = _unchunk(tinv)
        el_ref[0] = jnp.broadcast_to(jnp.exp(last), (rows_blk // DN_CHUNK, 8, 128))

    head, lanes, hm, sq, tile = _dn_specs(t, rows_blk)
    act = jax.ShapeDtypeStruct((N_HEADS, t, D_HEAD), F32)
    sqs = jax.ShapeDtypeStruct((N_HEADS, t, DN_CHUNK), F32)
    return pl.pallas_call(
        body, name="dn_intra_fwd", grid=(t // rows_blk, N_HEADS),
        in_specs=[head(0), head(N_HEADS), head(2 * N_HEADS), lanes, lanes],
        out_specs=[hm] * 4 + [sq, sq, tile],
        out_shape=[act] * 4 + [sqs, sqs, jax.ShapeDtypeStruct((N_HEADS, n_chunks, 8, 128), F32)],
        compiler_params=_cp(),
    )(qkv, qkv, qkv, beta_t, g_t)


def _dn_scan_specs(t, rows_blk, reverse):
    n_groups = t // rows_blk

    def at(g):
        return n_groups - 1 - g if reverse else g

    per = rows_blk // DN_CHUNK
    act = pl.BlockSpec((N_HEADS, rows_blk, D_HEAD), lambda g: (0, at(g), 0))
    sq = pl.BlockSpec((N_HEADS, rows_blk, DN_CHUNK), lambda g: (0, at(g), 0))
    state = pl.BlockSpec((N_HEADS, per, D_HEAD, D_HEAD), lambda g: (0, at(g), 0, 0))
    tile = pl.BlockSpec((N_HEADS, per, 8, 128), lambda g: (0, at(g), 0, 0))
    return act, sq, state, tile


def _dn_scan_fwd(u, w, qd, kd, a, el):
    t = u.shape[1]
    n_chunks = t // DN_CHUNK
    rows_blk = DN_SCAN_GROUP * DN_CHUNK

    def body(u_ref, w_ref, qd_ref, kd_ref, a_ref, el_ref, o_ref, vn_ref, s_ref, s_scr):
        @pl.when(pl.program_id(0) == 0)
        def _():
            s_scr[...] = jnp.zeros_like(s_scr)

        for cc in range(DN_SCAN_GROUP):
            rows = slice(cc * DN_CHUNK, (cc + 1) * DN_CHUNK)
            s = s_scr[...]
            s_ref[:, cc] = s
            v_new = u_ref[:, rows, :] - _bdot(w_ref[:, rows, :], s, "nn")
            vn_ref[:, rows, :] = v_new
            o_ref[:, rows, :] = _bdot(qd_ref[:, rows, :], s, "nn") + _bdot(a_ref[:, rows, :], v_new, "nn")
            s_scr[...] = s * el_ref[:, cc][:, 0:1, :] + _bdot(kd_ref[:, rows, :], v_new, "tn")

    act, sq, state, tile = _dn_scan_specs(t, rows_blk, reverse=False)
    shp = jax.ShapeDtypeStruct((N_HEADS, t, D_HEAD), F32)
    return pl.pallas_call(
        body, name="dn_scan_fwd", grid=(t // rows_blk,),
        in_specs=[act, act, act, act, sq, tile], out_specs=[act, act, state],
        out_shape=[shp, shp, jax.ShapeDtypeStruct((N_HEADS, n_chunks, D_HEAD, D_HEAD), F32)],
        scratch_shapes=[pltpu.VMEM((N_HEADS, D_HEAD, D_HEAD), F32)],
        compiler_params=_cp(dimension_semantics=("arbitrary",)),
    )(u, w, qd, kd, a, el)


def _dn_scan_bwd(w, qd, kd, a, el, vn, s_all, do):
    t = w.shape[1]
    n_chunks = t // DN_CHUNK
    rows_blk = DN_SCAN_GROUP * DN_CHUNK

    def body(w_ref, qd_ref, kd_ref, a_ref, el_ref, vn_ref, s_ref, do_ref, dvn_ref, dkd_ref, dqd_ref, dw_ref, dl_ref, ds_scr):
        @pl.when(pl.program_id(0) == 0)
        def _():
            ds_scr[...] = jnp.zeros_like(ds_scr)

        for cc in reversed(range(DN_SCAN_GROUP)):
            rows = slice(cc * DN_CHUNK, (cc + 1) * DN_CHUNK)
            s = s_ref[:, cc]
            d_s = ds_scr[...]
            e_last = el_ref[:, cc][:, 0:1, :]
            d_o = do_ref[:, rows, :]
            dv_new = _bdot(a_ref[:, rows, :], d_o, "tn") + _bdot(kd_ref[:, rows, :], d_s, "nn")
            ds_scr[...] = d_s * e_last + _bdot(qd_ref[:, rows, :], d_o, "tn") - _bdot(w_ref[:, rows, :], dv_new, "tn")
            dvn_ref[:, rows, :] = dv_new
            dkd_ref[:, rows, :] = _bdot(vn_ref[:, rows, :], d_s, "nt")
            dqd_ref[:, rows, :] = _bdot(d_o, s, "nt")
            dw_ref[:, rows, :] = -_bdot(dv_new, s, "nt")
            dlast = jnp.sum(jnp.sum(d_s * s, axis=2, keepdims=True), axis=1, keepdims=True)
            dl_ref[:, cc] = jnp.broadcast_to(dlast * e_last, (N_HEADS, 8, 128))

    act, sq, state, tile = _dn_scan_specs(t, rows_blk, reverse=True)
    shp = jax.ShapeDtypeStruct((N_HEADS, t, D_HEAD), F32)
    return pl.pallas_call(
        body, name="dn_scan_bwd", grid=(t // rows_blk,),
        in_specs=[act, act, act, sq, tile, act, state, act], out_specs=[act] * 4 + [tile],
        out_shape=[shp] * 4 + [jax.ShapeDtypeStruct((N_HEADS, n_chunks, 8, 128), F32)],
        scratch_shapes=[pltpu.VMEM((N_HEADS, D_HEAD, D_HEAD), F32)],
        compiler_params=_cp(dimension_semantics=("arbitrary",)),
    )(w, qd, kd, a, el, vn, s_all, do)


def _dn_intra_bwd(qkv, beta_t, g_t, tinv_all, vn, do, dvn, dkd, dqd, dw, dl):
    t = qkv.shape[0]
    rows_blk = min(DN_GROUP * DN_CHUNK, t)

    def body(q_ref, k_ref, v_ref, b_ref, g_ref, ti_ref, vn_ref, do_ref, dvn_ref, dkd_ref, dqd_ref, dw_ref, dl_ref,
             dq_ref, dk_ref, dv_ref, db_ref, dg_ref):
        ri = lax.broadcasted_iota(jnp.int32, (DN_CHUNK, DN_CHUNK), 0)
        ci = lax.broadcasted_iota(jnp.int32, (DN_CHUNK, DN_CHUNK), 1)
        h = pl.program_id(1)
        q, k, v = (_chunks(r[...]) for r in (q_ref, k_ref, v_ref))
        b, gc = _head_column(b_ref[...], h), _head_column(g_ref[...], h + N_HEADS)
        tinv = _chunks(ti_ref[0])
        dv_new, dk_dec, dq_dec, d_w = (_chunks(r[...]) for r in (dvn_ref, dkd_ref, dqd_ref, dw_ref))
        eye, gam, kk, qk, _, e_g, dec, rcol = _dn_chunk_common(q, k, gc, ri, ci)
        bv = v * b
        bk = k * (b * e_g)

        d_a = jnp.where(ri >= ci, _bdot(_chunks(do_ref[...]), _chunks(vn_ref[...]), "nt"), 0.0)
        dbv = _bdot(tinv, dv_new, "tn")
        dbk = _bdot(tinv, d_w, "tn")
        d_tinv = _bdot(dv_new, bv, "nt") + _bdot(d_w, bk, "nt")
        d_m = -jnp.where(ri > ci, _dot3(_dot3(tinv, d_tinv, "tn"), tinv, "nt"), 0.0)

        d_kk = d_m * b * gam
        d_gam = d_m * b * kk + d_a * qk
        d_qk = d_a * gam
        dq_ref[...] = _unchunk(_bdot(d_qk, k, "nn") + dq_dec * e_g)
        dk_ref[...] = _unchunk(_bdot(d_qk, q, "tn") + _bdot(d_kk, k, "nn") + _bdot(d_kk, k, "tn")
                               + dk_dec * dec + dbk * (b * e_g))
        dv_ref[...] = _unchunk(dbv * b)
        d_b = _unchunk(jnp.sum(d_m * kk * gam, axis=-1, keepdims=True) + jnp.sum(dbv * v, axis=-1, keepdims=True)
                       + jnp.sum(dbk * k, axis=-1, keepdims=True) * e_g)

        xg = d_gam * gam
        kdk = jnp.sum(dk_dec * (k * dec), axis=-1, keepdims=True)
        d_gc = (jnp.sum(xg, axis=-1, keepdims=True) - _row_to_col(jnp.sum(xg, axis=-2, keepdims=True), eye)
                + jnp.sum(dq_dec * (q * e_g), axis=-1, keepdims=True) - kdk
                + jnp.sum(dbk * bk, axis=-1, keepdims=True))
        d_last_total = dl_ref[0][:, 0:1, 0:1] + jnp.sum(kdk, axis=-2, keepdims=True)
        d_g = _unchunk(d_gc + jnp.where(rcol == DN_CHUNK - 1, d_last_total, 0.0))

        @pl.when(h == 0)
        def _():
            db_ref[...] = jnp.zeros_like(db_ref)
            dg_ref[...] = jnp.zeros_like(dg_ref)

        lane = lax.broadcasted_iota(jnp.int32, db_ref.shape, 1)
        db_ref[...] += jnp.where(lane == h, d_b, 0.0)
        dg_ref[...] += jnp.where(lane == h + N_HEADS, d_g, 0.0)

    head, lanes, hm, sq, tile = _dn_specs(t, rows_blk)
    return pl.pallas_call(
        body, name="dn_intra_bwd", grid=(t // rows_blk, N_HEADS),
        in_specs=[head(0), head(N_HEADS), head(2 * N_HEADS), lanes, lanes, sq] + [hm] * 6 + [tile],
        out_specs=[head(0), head(0), head(0), lanes, lanes],
        out_shape=[jax.ShapeDtypeStruct((t, D_MODEL), F32)] * 3 + [jax.ShapeDtypeStruct((t, 128), F32)] * 2,
        compiler_params=_cp(),
    )(qkv, qkv, qkv, beta_t, g_t, tinv_all, vn, do, dvn, dkd, dqd, dw, dl)


def _dn_post_fwd(o, proj, gn):
    t = o.shape[1]

    def body(o_ref, z_ref, g_ref, out_ref):
        ov, z = o_ref[...], z_ref[...]
        r = lax.rsqrt(jnp.mean(ov * ov, axis=-1, keepdims=True) + NORM_EPS)
        out_ref[...] = (((ov * r) * g_ref[...]) * (z * _sigmoid(z))).astype(BF16)

    blk = pl.BlockSpec((t, D_HEAD), lambda h: (0, h))
    return pl.pallas_call(
        body, name="dn_post_fwd", grid=(N_HEADS,),
        in_specs=[pl.BlockSpec((None, t, D_HEAD), lambda h: (h, 0, 0)),
                  pl.BlockSpec((t, D_HEAD), lambda h: (0, C_DNZ // D_HEAD + h)),
                  pl.BlockSpec((1, D_HEAD), lambda h: (0, 0))],
        out_specs=blk, out_shape=jax.ShapeDtypeStruct((t, D_MODEL), BF16), compiler_params=_cp(),
    )(o, proj, gn)


def _dn_post_bwd(o, proj, gn, dout):
    t = o.shape[1]

    def body(o_ref, z_ref, g_ref, d_ref, do_ref, dz_ref, dg_ref):
        @pl.when(pl.program_id(0) == 0)
        def _():
            dg_ref[...] = jnp.zeros_like(dg_ref)

        ov, z, d = o_ref[...], z_ref[...], d_ref[...]
        r = lax.rsqrt(jnp.mean(ov * ov, axis=-1, keepdims=True) + NORM_EPS)
        ohat = ov * r
        s = _sigmoid(z)
        d_on = d * (z * s)
        dz_ref[...] = (d * (ohat * g_ref[...]) * (s * (1.0 + z * (1.0 - s)))).astype(BF16)
        dg_ref[...] += jnp.sum(d_on * ohat, axis=0, keepdims=True)
        dxh = d_on * g_ref[...]
        do_ref[...] = r * (dxh - ohat * jnp.mean(dxh * ohat, axis=-1, keepdims=True))

    blk = pl.BlockSpec((t, D_HEAD), lambda h: (0, h))
    hm = pl.BlockSpec((None, t, D_HEAD), lambda h: (h, 0, 0))
    vec = pl.BlockSpec((1, D_HEAD), lambda h: (0, 0))
    return pl.pallas_call(
        body, name="dn_post_bwd", grid=(N_HEADS,),
        in_specs=[hm, pl.BlockSpec((t, D_HEAD), lambda h: (0, C_DNZ // D_HEAD + h)), vec, blk],
        out_specs=[hm, blk, vec],
        out_shape=[jax.ShapeDtypeStruct((N_HEADS, t, D_HEAD), F32), jax.ShapeDtypeStruct((t, D_MODEL), BF16),
                   jax.ShapeDtypeStruct((1, D_HEAD), F32)], compiler_params=_cp(),
    )(o, proj, gn, dout)


def _sb_fwd(proj):
    t = proj.shape[0]
    qblk = min(SB_QBLOCK, t)
    scale = 1.0 / math.sqrt(D_HEAD)

    hp = SB_HEADS_PER_STEP
    wid = hp * D_HEAD

    def body(q_ref, k_ref, v_ref, z_ref, o_ref, og_ref, l_ref, qb, kb, vb):
        for hh in range(hp):
            hs = slice(hh * D_HEAD, (hh + 1) * D_HEAD)
            qb[hh] = q_ref[:, hs].astype(BF16)
            kb[hh] = k_ref[:, hs].astype(BF16)
            vb[hh] = v_ref[:, hs].astype(BF16)
        ri = lax.broadcasted_iota(jnp.int32, (qblk, SB_BLOCK), 0)
        ci = lax.broadcasted_iota(jnp.int32, (qblk, SB_BLOCK), 1)
        r2 = lax.broadcasted_iota(jnp.int32, (SB_BLOCK, SB_BLOCK), 0)
        c2 = lax.broadcasted_iota(jnp.int32, (SB_BLOCK, SB_BLOCK), 1)
        upper = (r2 > c2).astype(BF16)
        nkb = qblk // SB_BLOCK

        def qblock(i, carry):
            rows = pl.ds(pl.multiple_of(i * qblk, qblk), qblk)
            qi = qb[:, rows, :]

            def tile(j, st, on_diagonal):
                acc, c = st
                cols = pl.ds(pl.multiple_of(j * SB_BLOCK, SB_BLOCK), SB_BLOCK)
                z = _dot(qi, kb[:, cols, :], "nt") * scale
                lb = jnp.minimum(z, 0.0) - jnp.log(1.0 + jnp.exp(-jnp.abs(z)))
                lf = lb - z
                if on_diagonal:
                    mask = (j * SB_BLOCK + ci) < (i * qblk + ri)
                    lf = jnp.where(mask, lf, 0.0)
                att = jnp.exp(lb + (_ones_dot(lf, upper) + c))
                if on_diagonal:
                    att = jnp.where(mask, att, 0.0)
                acc = acc + _dot(att.astype(BF16), vb[:, cols, :], "nn")
                return acc, c + jnp.sum(lf, axis=-1, keepdims=True)

            st = (jnp.zeros((hp, qblk, D_HEAD), F32), jnp.zeros((hp, qblk, 1), F32))
            for d in range(nkb):
                st = tile((i + 1) * nkb - 1 - d, st, True)
            acc, c = lax.fori_loop(0, i * nkb, lambda jj, s: tile(i * nkb - 1 - jj, s, False), st)
            l_ref[:, rows, :] = c
            for hh in range(hp):
                hs = slice(hh * D_HEAD, (hh + 1) * D_HEAD)
                zg = z_ref[rows, hs]
                o_ref[rows, hs] = acc[hh]
                og_ref[rows, hs] = (acc[hh] * (zg * _sigmoid(zg))).astype(BF16)
            return carry

        lax.fori_loop(0, t // qblk, qblock, 0)

    def head(off):
        return pl.BlockSpec((t, wid), lambda h: (0, off // wid + h))

    out = pl.BlockSpec((t, wid), lambda h: (0, h))
    return pl.pallas_call(
        body, name="sb_fwd", grid=(N_HEADS // hp,),
        in_specs=[head(C_SBQ), head(C_SBQ + D_MODEL), head(C_SBQ + 2 * D_MODEL), head(C_SBZ)],
        out_specs=[out, out, pl.BlockSpec((hp, t, 1), lambda h: (h, 0, 0))],
        out_shape=[jax.ShapeDtypeStruct((t, D_MODEL), F32), jax.ShapeDtypeStruct((t, D_MODEL), BF16),
                   jax.ShapeDtypeStruct((N_HEADS, t, 1), F32)],
        scratch_shapes=[pltpu.VMEM((hp, t, D_HEAD), BF16)] * 3, compiler_params=_cp(),
    )(proj, proj, proj, proj)


def _sb_bwd(proj, o, ltot, dog, after=None):
    t = proj.shape[0]
    qblk = min(SB_QBLOCK, t)
    scale = 1.0 / math.sqrt(D_HEAD)

    hp = SB_HEADS_PER_STEP
    wid = hp * D_HEAD

    def body(q_ref, k_ref, v_ref, z_ref, o_ref, l_ref, d_ref, *rest):
        dq_ref, dk_ref, dv_ref, dz_ref, qb, kb, vb, dob, dk_scr, dv_scr = rest[-10:]
        for hh in range(hp):
            hs = slice(hh * D_HEAD, (hh + 1) * D_HEAD)
            qb[hh] = q_ref[:, hs].astype(BF16)
            kb[hh] = k_ref[:, hs].astype(BF16)
            vb[hh] = v_ref[:, hs].astype(BF16)
            zg = z_ref[:, hs]
            sg = _sigmoid(zg)
            dgo = d_ref[:, hs]
            dob[hh] = (dgo * (zg * sg)).astype(BF16)
            dz_ref[:, hs] = (dgo * o_ref[:, hs] * (sg * (1.0 + zg * (1.0 - sg)))).astype(BF16)
        dk_scr[...] = jnp.zeros_like(dk_scr)
        dv_scr[...] = jnp.zeros_like(dv_scr)
        ri = lax.broadcasted_iota(jnp.int32, (qblk, SB_BLOCK), 0)
        ci = lax.broadcasted_iota(jnp.int32, (qblk, SB_BLOCK), 1)
        r2 = lax.broadcasted_iota(jnp.int32, (SB_BLOCK, SB_BLOCK), 0)
        c2 = lax.broadcasted_iota(jnp.int32, (SB_BLOCK, SB_BLOCK), 1)
        upper = (r2 > c2).astype(BF16)
        below = (r2 < c2).astype(BF16)

        def qblock(i, carry):
            rows = pl.ds(pl.multiple_of(i * qblk, qblk), qblk)
            qi = qb[:, rows, :]
            d_o = dob[:, rows, :]
            ltot = l_ref[:, rows, :]

            def tile(j, st, on_diagonal):
                dq, cpre, ce = st
                cols = pl.ds(pl.multiple_of(j * SB_BLOCK, SB_BLOCK), SB_BLOCK)
                kj, vj = kb[:, cols, :], vb[:, cols, :]
                z = _dot(qi, kj, "nt") * scale
                lb = jnp.minimum(z, 0.0) - jnp.log(1.0 + jnp.exp(-jnp.abs(z)))
                lf = lb - z
                if on_diagonal:
                    mask = (j * SB_BLOCK + ci) < (i * qblk + ri)
                    lf = jnp.where(mask, lf, 0.0)
                tile_sum = jnp.sum(lf, axis=-1, keepdims=True)
                att = jnp.exp(lb + ((ltot - cpre - tile_sum) + _ones_dot(lf, upper)))
                if on_diagonal:
                    att = jnp.where(mask, att, 0.0)
                e = _dot(d_o, vj, "nt") * att
                dlf = ce + _ones_dot(e, below)
                dzz = e - (e + dlf) * jnp.exp(lb)
                if on_diagonal:
                    dzz = jnp.where(mask, dzz, 0.0)
                dzz = dzz.astype(BF16)
                dq = dq + _dot(dzz, kj, "nn")
                dk_scr[:, cols, :] += _dot(dzz, qi, "tn")
                dv_scr[:, cols, :] += _dot(att.astype(BF16), d_o, "tn")
                return dq, cpre + tile_sum, ce + jnp.sum(e, axis=-1, keepdims=True)

            nkb = qblk // SB_BLOCK
            zero_col = jnp.zeros((hp, qblk, 1), F32)
            st = lax.fori_loop(0, i * nkb, lambda j, s: tile(j, s, False),
                               (jnp.zeros((hp, qblk, D_HEAD), F32), zero_col, zero_col))
            for d in range(nkb):
                st = tile(i * nkb + d, st, True)
            dq = st[0]
            for hh in range(hp):
                dq_ref[rows, hh * D_HEAD:(hh + 1) * D_HEAD] = (dq[hh] * scale).astype(BF16)
            return carry

        lax.fori_loop(0, t // qblk, qblock, 0)
        for hh in range(hp):
            hs = slice(hh * D_HEAD, (hh + 1) * D_HEAD)
            dk_ref[:, hs] = (dk_scr[hh] * scale).astype(BF16)
            dv_ref[:, hs] = dv_scr[hh].astype(BF16)

    def head(off):
        return pl.BlockSpec((t, wid), lambda h: (0, off // wid + h))

    extra_specs, extra = [], []
    if after is not None:
        extra_specs, extra = [pl.BlockSpec(after.shape, lambda h: (0, 0))], [after]
    return pl.pallas_call(
        body, name="sb_bwd", grid=(N_HEADS // hp,),
        in_specs=[head(C_SBQ), head(C_SBQ + D_MODEL), head(C_SBQ + 2 * D_MODEL), head(C_SBZ), head(0),
                  pl.BlockSpec((hp, t, 1), lambda h: (h, 0, 0)), head(0)] + extra_specs,
        out_specs=[head(0)] * 4, out_shape=[jax.ShapeDtypeStruct((t, D_MODEL), BF16)] * 4,
        scratch_shapes=[pltpu.VMEM((hp, t, D_HEAD), BF16)] * 4 + [pltpu.VMEM((hp, t, D_HEAD), F32)] * 2,
        compiler_params=_cp(),
    )(proj, proj, proj, proj, o, ltot, dog, *extra)


def _mem_fwd(proj, mkv):
    t = proj.shape[0]
    tq = _pick(t, (512, 256))
    m_len = mkv.shape[0]
    scale = 1.0 / math.sqrt(MEM_DH)

    def body(q_ref, z_ref, kv_ref, o_ref, og_ref):
        q = q_ref[...]
        mk = kv_ref[:, :MEM_W].astype(BF16)
        mv = kv_ref[:, MEM_W:].astype(BF16)
        lane = lax.broadcasted_iota(jnp.int32, q.shape, 1) >> 6
        o = jnp.zeros(q.shape, F32)
        for h in range(MEM_HEADS):
            s = _bdot(jnp.where(lane == h, q, 0.0), mk, "nt") * scale
            p = jnp.exp(s - jnp.max(s, axis=-1, keepdims=True))
            p = p / jnp.sum(p, axis=-1, keepdims=True)
            o = o + jnp.where(lane == h, _bdot(p, mv, "nn"), 0.0)
        z = z_ref[...]
        o_ref[...] = o
        og_ref[...] = (o * (z * _sigmoid(z))).astype(BF16)

    out = pl.BlockSpec((tq, MEM_W), lambda i: (i, 0))
    return pl.pallas_call(
        body, name="mem_fwd", grid=(t // tq,),
        in_specs=[pl.BlockSpec((tq, MEM_W), lambda i: (i, C_MQ // MEM_W)),
                  pl.BlockSpec((tq, MEM_W), lambda i: (i, C_MZ // MEM_W)),
                  pl.BlockSpec((m_len, 2 * MEM_W), lambda i: (0, 0))],
        out_specs=[out, out],
        out_shape=[jax.ShapeDtypeStruct((t, MEM_W), F32), jax.ShapeDtypeStruct((t, MEM_W), BF16)],
        compiler_params=_cp(),
    )(proj, proj, mkv)


def _mem_bwd(proj, mkv, o, dog):
    t = proj.shape[0]
    tq = _pick(t, (512, 256))
    m_len = mkv.shape[0]
    scale = 1.0 / math.sqrt(MEM_DH)

    def body(q_ref, z_ref, kv_ref, o_ref, d_ref, dq_ref, dz_ref, dkv_ref):
        @pl.when(pl.program_id(0) == 0)
        def _():
            dkv_ref[...] = jnp.zeros_like(dkv_ref)

        q = q_ref[...]
        z = z_ref[...]
        sg = _sigmoid(z)
        dgo = d_ref[...]
        d_o = dgo * (z * sg)
        dz_ref[...] = (dgo * o_ref[...] * (sg * (1.0 + z * (1.0 - sg)))).astype(BF16)
        mk = kv_ref[:, :MEM_W].astype(BF16)
        mv = kv_ref[:, MEM_W:].astype(BF16)
        lane = lax.broadcasted_iota(jnp.int32, q.shape, 1) >> 6
        klane = lax.broadcasted_iota(jnp.int32, (m_len, MEM_W), 1) >> 6
        dq = jnp.zeros(q.shape, F32)
        dmk = jnp.zeros((m_len, MEM_W), F32)
        dmv = jnp.zeros((m_len, MEM_W), F32)
        for h in range(MEM_HEADS):
            qh = jnp.where(lane == h, q, 0.0)
            doh = jnp.where(lane == h, d_o, 0.0)
            s = _bdot(qh, mk, "nt") * scale
            p = jnp.exp(s - jnp.max(s, axis=-1, keepdims=True))
            p = p / jnp.sum(p, axis=-1, keepdims=True)
            dp = _bdot(doh, mv, "nt")
            ds = p * (dp - jnp.sum(dp * p, axis=-1, keepdims=True)) * scale
            dq = dq + jnp.where(lane == h, _bdot(ds, mk, "nn"), 0.0)
            dmk = dmk + jnp.where(klane == h, _bdot(ds, qh, "tn"), 0.0)
            dmv = dmv + jnp.where(klane == h, _bdot(p, doh, "tn"), 0.0)
        dq_ref[...] = dq.astype(BF16)
        dkv_ref[:, :MEM_W] += dmk
        dkv_ref[:, MEM_W:] += dmv

    blk = pl.BlockSpec((tq, MEM_W), lambda i: (i, 0))
    kv = pl.BlockSpec((m_len, 2 * MEM_W), lambda i: (0, 0))
    return pl.pallas_call(
        body, name="mem_bwd", grid=(t // tq,),
        in_specs=[pl.BlockSpec((tq, MEM_W), lambda i: (i, C_MQ // MEM_W)),
                  pl.BlockSpec((tq, MEM_W), lambda i: (i, C_MZ // MEM_W)), kv, blk, blk],
        out_specs=[blk, blk, kv],
        out_shape=[jax.ShapeDtypeStruct((t, MEM_W), BF16), jax.ShapeDtypeStruct((t, MEM_W), BF16),
                   jax.ShapeDtypeStruct((m_len, 2 * MEM_W), F32)], compiler_params=_cp(),
    )(proj, proj, mkv, o, dog)


_GW = 512


def _merge_fwd(proj, y_dn, y_sb, y_m):
    t = proj.shape[0]
    tb = _pick(t, (256,))
    nc = D_MODEL // _GW

    def body(g1, g2, g3, y1, y2, y3, out_ref):
        out_ref[...] = (_sigmoid(g1[...]) * y1[...] + _sigmoid(g2[...]) * y2[...] + _sigmoid(g3[...]) * y3[...]).astype(BF16)

    def gate(kb):
        return pl.BlockSpec((tb, _GW), lambda i, c: (i, C_GATES // _GW + kb * nc + c))

    blk = pl.BlockSpec((tb, _GW), lambda i, c: (i, c))
    return pl.pallas_call(
        body, name="merge_fwd", grid=(t // tb, nc), in_specs=[gate(0), gate(1), gate(2), blk, blk, blk],
        out_specs=blk, out_shape=jax.ShapeDtypeStruct((t, D_MODEL), BF16), compiler_params=_cp(),
    )(proj, proj, proj, y_dn, y_sb, y_m)


def _merge_bwd(proj, y_dn, y_sb, y_m, dm):
    t = proj.shape[0]
    tb = _pick(t, (256,))
    nc = D_MODEL // _GW

    def body(g1, g2, g3, y1, y2, y3, dm_ref, d1, d2, d3, dg1, dg2, dg3):
        d = dm_ref[...]
        for g, y, dy, dg in ((g1, y1, d1, dg1), (g2, y2, d2, dg2), (g3, y3, d3, dg3)):
            s = _sigmoid(g[...])
            dy[...] = (d * s).astype(BF16)
            dg[...] = (d * y[...] * (s * (1.0 - s))).astype(BF16)

    def gate(kb):
        return pl.BlockSpec((tb, _GW), lambda i, c: (i, C_GATES // _GW + kb * nc + c))

    blk = pl.BlockSpec((tb, _GW), lambda i, c: (i, c))
    act = jax.ShapeDtypeStruct((t, D_MODEL), BF16)
    return pl.pallas_call(
        body, name="merge_bwd", grid=(t // tb, nc), in_specs=[gate(0), gate(1), gate(2), blk, blk, blk, blk],
        out_specs=[blk] * 6, out_shape=[act] * 6, compiler_params=_cp(),
    )(proj, proj, proj, y_dn, y_sb, y_m, dm)


def _final_loss(x, mo, g, tgt):
    t, d = x.shape
    tb = _pick(t, (256,))

    def body(x_ref, mo_ref, g_ref, t_ref, do_ref, dob_ref, loss_ref, dg_ref):
        @pl.when(pl.program_id(0) == 0)
        def _():
            loss_ref[...] = jnp.zeros_like(loss_ref)
            dg_ref[...] = jnp.zeros_like(dg_ref)

        out = x_ref[...] + mo_ref[...]
        r = lax.rsqrt(jnp.mean(out * out, axis=-1, keepdims=True) + NORM_EPS)
        xhat = out * r
        gv = g_ref[...]
        err = xhat * gv - t_ref[...]
        per_tok = jnp.mean(err * err, axis=-1, keepdims=True)
        loss_ref[...] += 0.5 * jnp.sum(per_tok, axis=0, keepdims=True)
        dy = err * (1.0 / d)
        dg_ref[...] += jnp.sum(dy * xhat, axis=0, keepdims=True)
        dxh = dy * gv
        dout = r * (dxh - xhat * jnp.mean(dxh * xhat, axis=-1, keepdims=True))
        do_ref[...] = dout
        dob_ref[...] = dout.astype(BF16)

    row = pl.BlockSpec((tb, d), lambda i: (i, 0))
    vec = pl.BlockSpec((1, d), lambda i: (0, 0))
    return pl.pallas_call(
        body, name="final_loss", grid=(t // tb,), in_specs=[row, row, vec, row],
        out_specs=[row, row, pl.BlockSpec((1, 128), lambda i: (0, 0)), vec],
        out_shape=[jax.ShapeDtypeStruct((t, d), F32), jax.ShapeDtypeStruct((t, d), BF16),
                   jax.ShapeDtypeStruct((1, 128), F32), jax.ShapeDtypeStruct((1, d), F32)],
        compiler_params=_cp(),
    )(x, mo, g, tgt)


def _cast_bf16(a, name):
    r, c = a.shape
    tb = _pick(r, (128, 496, 240))

    def body(a_ref, o_ref):
        o_ref[...] = a_ref[...].astype(BF16)

    blk = pl.BlockSpec((tb, c), lambda i: (i, 0))
    return pl.pallas_call(body, name=name, grid=(r // tb,), in_specs=[blk], out_specs=blk,
                          out_shape=jax.ShapeDtypeStruct((r, c), BF16), compiler_params=_cp())(a)


WIN_START = (0, 23, 45, 68)
_S1_LO, _S1_HI = 1148, 1164
_S1_BA_POS = SHARD_PAD - 128


def _to_window(x, s):
    if s == 0:
        return x
    if s in (2, 3):
        return pltpu.roll(x, 120 if s == 2 else 124, 1)
    pos = lax.broadcasted_iota(jnp.int32, x.shape, 1)
    head = pltpu.roll(x, 4, 1)
    tail = pltpu.roll(x, SHARD_PAD - 12, 1)
    ba = jnp.where(pos < _S1_BA_POS + (_S1_HI - _S1_LO), pltpu.roll(x, _S1_BA_POS - _S1_LO, 1), 0.0)
    return jnp.where(pos < _S1_LO + 4, head, jnp.where(pos < _S1_BA_POS, tail, ba))


def _from_window(g, s):
    if s == 0:
        return g
    if s in (2, 3):
        return pltpu.roll(g, SHARD_PAD - (120 if s == 2 else 124), 1)
    col = lax.broadcasted_iota(jnp.int32, g.shape, 1)
    head = pltpu.roll(g, SHARD_PAD - 4, 1)
    tail = pltpu.roll(g, 12, 1)
    ba = pltpu.roll(g, SHARD_PAD - (_S1_BA_POS - _S1_LO), 1)
    return jnp.where(col < _S1_LO, head, jnp.where(col < _S1_HI, ba, tail))


def _cast_to_window(w, shard, name):
    r, c = w.shape
    tb = _pick(r, (128,))

    def body(s_ref, w_ref, o_ref, pad_scr):
        pad_scr[...] = jnp.zeros_like(pad_scr)
        pad_scr[:, :c] = w_ref[...]
        x = pad_scr[...]
        for s in range(N_SHARD):
            @pl.when(s_ref[0] == s)
            def _():
                o_ref[...] = _to_window(x, s).astype(BF16)

    return pl.pallas_call(
        body, name=name,
        grid_spec=pltpu.PrefetchScalarGridSpec(
            num_scalar_prefetch=1, grid=(r // tb,),
            in_specs=[pl.BlockSpec((tb, c), lambda i, s: (i, 0))],
            out_specs=pl.BlockSpec((tb, SHARD_PAD), lambda i, s: (i, 0)),
            scratch_shapes=[pltpu.VMEM((tb, SHARD_PAD), F32)]),
        out_shape=jax.ShapeDtypeStruct((r, SHARD_PAD), BF16), compiler_params=_cp(),
    )(shard, w)


def _pair_add(g, recv, c_idx, name):
    n, r, c = g.shape
    half = r // 2
    tb = _pick(half, (128, 240))
    nb = half // tb

    def body(c_ref, g_ref, r_ref, o_ref):
        o_ref[...] = (g_ref[...].astype(F32) + r_ref[...].astype(F32)).astype(BF16)

    blk = pl.BlockSpec((n, tb, c), lambda i, c_ref: (0, i, 0))
    return pl.pallas_call(
        body, name=name,
        grid_spec=pltpu.PrefetchScalarGridSpec(
            num_scalar_prefetch=1, grid=(nb,),
            in_specs=[pl.BlockSpec((n, tb, c), lambda i, c_ref: (0, c_ref[0] * nb + i, 0)), blk], out_specs=blk),
        out_shape=jax.ShapeDtypeStruct((n, half, c), BF16), compiler_params=_cp(),
    )(c_idx, g, recv)


def _chip_sum(parts, by_chip, place, name):
    n, h, c = parts.shape
    tb = _pick(h, (128, 240))
    nb = h // tb

    def body(p_ref, mine_ref, *rest):
        others, o_ref = rest[:n], rest[n]
        me = jnp.zeros((tb, c), jnp.int32) + p_ref[0]
        acc = None
        for q in range(n):
            term = jnp.where(me == q, mine_ref[...], others[q][...]).astype(F32)
            acc = term if acc is None else acc + term
        o_ref[...] = acc

    def other(q):
        return pl.BlockSpec((None, tb, c), lambda i, p: (jnp.where(p[0] == q, (q + 1) % n, q), i, 0))

    return pl.pallas_call(
        body, name=name,
        grid_spec=pltpu.PrefetchScalarGridSpec(
            num_scalar_prefetch=1, grid=(nb,),
            in_specs=[pl.BlockSpec((None, tb, c), lambda i, p: (p[0], i, 0))] + [other(q) for q in range(n)],
            out_specs=pl.BlockSpec((tb, c), lambda i, p: (p[1] * nb + i, 0))),
        out_shape=jax.ShapeDtypeStruct((2 * h, c), F32), compiler_params=_cp(),
    )(place, parts, *([by_chip] * n))


def _adamw_math(w, g, m, v):
    m = ADAM_B1 * m + (1.0 - ADAM_B1) * g
    v = ADAM_B2 * v + (1.0 - ADAM_B2) * (g * g)
    m_hat = m / (1.0 - ADAM_B1 ** ADAM_STEP)
    v_hat = v / (1.0 - ADAM_B2 ** ADAM_STEP)
    delta = -ADAM_LR * (m_hat / (jnp.sqrt(v_hat) + ADAM_EPS) + ADAM_WD * w)
    return delta, m, v


def _adamw(w, g, m, v, name):
    r, c = w.shape
    tb = _pick(r, (128, 496, 240))

    def body(w_ref, g_ref, m_ref, v_ref, go_ref, d_ref, mo_ref, vo_ref):
        gv = g_ref[...]
        d, mn, vn = _adamw_math(w_ref[...], gv, m_ref[...], v_ref[...])
        go_ref[...] = gv
        d_ref[...] = d
        mo_ref[...] = mn
        vo_ref[...] = vn

    blk = pl.BlockSpec((tb, c), lambda i: (i, 0))
    return pl.pallas_call(
        body, name=name, grid=(r // tb,), in_specs=[blk] * 4, out_specs=[blk] * 4,
        out_shape=[jax.ShapeDtypeStruct((r, c), F32)] * 4, compiler_params=_cp(),
    )(w, g, m, v)


def _adamw_window(w, g_win, m, v, shard, name):
    r, c = w.shape
    tb = _pick(r, (128,))

    def body(s_ref, w_ref, g_ref, m_ref, v_ref, go_ref, d_ref, mo_ref, vo_ref, g_scr):
        gw = g_ref[...]
        for s in range(N_SHARD):
            @pl.when(s_ref[0] == s)
            def _():
                g_scr[...] = _from_window(gw, s)

        gv = g_scr[:, :c]
        d, mn, vn = _adamw_math(w_ref[...], gv, m_ref[...], v_ref[...])
        go_ref[...] = gv
        d_ref[...] = d
        mo_ref[...] = mn
        vo_ref[...] = vn

    blk = pl.BlockSpec((tb, c), lambda i, s: (i, 0))
    return pl.pallas_call(
        body, name=name,
        grid_spec=pltpu.PrefetchScalarGridSpec(
            num_scalar_prefetch=1, grid=(r // tb,),
            in_specs=[blk, pl.BlockSpec((tb, SHARD_PAD), lambda i, s: (i, 0)), blk, blk], out_specs=[blk] * 4,
            scratch_shapes=[pltpu.VMEM((tb, SHARD_PAD), F32)]),
        out_shape=[jax.ShapeDtypeStruct((r, c), F32)] * 4, compiler_params=_cp(),
    )(shard, w, g_win, m, v)


def _small_update(gathered, w, m, v):
    def body(p_ref, w_ref, m_ref, v_ref, g_ref, d_ref, mo_ref, vo_ref):
        g = p_ref[0]
        for i in range(1, N_DEV):
            g = g + p_ref[i]
        d, mn, vn = _adamw_math(w_ref[...], g, m_ref[...], v_ref[...])
        g_ref[...] = g
        d_ref[...] = d
        mo_ref[...] = mn
        vo_ref[...] = vn

    full = pl.BlockSpec((S_ROWS, 128), lambda i: (0, 0))
    return pl.pallas_call(
        body, name="small_update", grid=(1,),
        in_specs=[pl.BlockSpec((N_DEV, S_ROWS, 128), lambda i: (0, 0, 0)), full, full, full], out_specs=[full] * 4,
        out_shape=[jax.ShapeDtypeStruct((S_ROWS, 128), F32)] * 4, compiler_params=_cp(),
    )(gathered, w, m, v)


_ANY = pl.BlockSpec(memory_space=pl.ANY)


def _place():
    x, y, c = lax.axis_index("x"), lax.axis_index("y"), lax.axis_index("c")
    chips = [(1 - x, y), (x, 1 - y), (1 - x, 1 - y)]
    return x, y, c, chips


def _gather_shards(arrs):
    n = len(arrs)

    def body(*refs):
        ins, outs = refs[:n], refs[n:2 * n]
        send_sems, recv_sems, local_sems = refs[2 * n:2 * n + 3]
        bufs = refs[2 * n + 3:]
        x, y, c, chips = _place()
        me = 2 * x + y
        sibling = (x, y, 1 - c)
        sends = []
        for a in range(n):
            half = ins[a].shape[0] // 2
            mine = pl.ds(pl.multiple_of(c * half, 16), half)
            for j, (qx, qy) in enumerate(chips):
                cp = pltpu.make_async_remote_copy(
                    src_ref=ins[a].at[mine], dst_ref=outs[a].at[me, mine],
                    send_sem=send_sems.at[6 * a + j], recv_sem=recv_sems.at[6 * a + j],
                    device_id=(qx, qy, c), device_id_type=MESH)
                cp.start()
                sends.append(cp)
        for a in range(n):
            step = bufs[a].shape[0]
            for r0 in range(0, ins[a].shape[0], step):
                rows = pl.ds(r0, step)
                load = pltpu.make_async_copy(ins[a].at[rows], bufs[a], local_sems.at[2 * a])
                load.start()
                load.wait()
                store = pltpu.make_async_copy(bufs[a], outs[a].at[me, rows], local_sems.at[2 * a + 1])
                store.start()
                store.wait()
        for a in range(n):
            half = ins[a].shape[0] // 2
            mine = pl.ds(pl.multiple_of(c * half, 16), half)
            for j, (qx, qy) in enumerate(chips):
                q = 2 * qx + qy
                landed = outs[a].at[q, mine]
                pltpu.make_async_remote_copy(
                    src_ref=landed, dst_ref=landed, send_sem=send_sems.at[6 * a + j], recv_sem=recv_sems.at[6 * a + j],
                    device_id=(qx, qy, c), device_id_type=MESH).wait_recv()
                fw = pltpu.make_async_remote_copy(
                    src_ref=landed, dst_ref=landed, send_sem=send_sems.at[6 * a + 3 + j],
                    recv_sem=recv_sems.at[6 * a + 3 + j], device_id=sibling, device_id_type=MESH)
                fw.start()
                sends.append(fw)
        for a in range(n):
            half = ins[a].shape[0] // 2
            theirs = pl.ds(pl.multiple_of((1 - c) * half, 16), half)
            for j, (qx, qy) in enumerate(chips):
                q = 2 * qx + qy
                dst = outs[a].at[q, theirs]
                pltpu.make_async_remote_copy(
                    src_ref=dst, dst_ref=dst, send_sem=send_sems.at[6 * a + 3 + j], recv_sem=recv_sems.at[6 * a + 3 + j],
                    device_id=sibling, device_id_type=MESH).wait_recv()
        for cp in sends:
            cp.wait_send()

    return pl.pallas_call(
        body, name="gather_shards", in_specs=[_ANY] * n, out_specs=[_ANY] * n,
        out_shape=[jax.ShapeDtypeStruct((N_SHARD,) + a.shape, a.dtype) for a in arrs],
        scratch_shapes=[pltpu.SemaphoreType.DMA((6 * n,)), pltpu.SemaphoreType.DMA((6 * n,)),
                        pltpu.SemaphoreType.DMA((2 * n,))]
        + [pltpu.VMEM((_pick(a.shape[0], (256, 496)), a.shape[1]), a.dtype) for a in arrs],
        compiler_params=pltpu.CompilerParams(has_side_effects=True, vmem_limit_bytes=VMEM_LIMIT),
    )(*arrs)


def _pair_reduce_send(grads, tag):
    n = len(grads)

    def body(*refs):
        ins, outs = refs[:n], refs[n:2 * n]
        send_sems, recv_sems = refs[2 * n:]
        x, y, c, _ = _place()
        sibling = (x, y, 1 - c)
        cps = []
        for a in range(n):
            half = ins[a].shape[1] // 2
            theirs = pl.ds(pl.multiple_of((1 - c) * half, 8), half)
            cp = pltpu.make_async_remote_copy(
                src_ref=ins[a].at[:, theirs], dst_ref=outs[a], send_sem=send_sems.at[a], recv_sem=recv_sems.at[a],
                device_id=sibling, device_id_type=MESH)
            cp.start()
            cps.append(cp)
        for cp in cps:
            cp.wait()

    return pl.pallas_call(
        body, name="pair_reduce_send_" + tag, in_specs=[_ANY] * n, out_specs=[_ANY] * n,
        out_shape=[jax.ShapeDtypeStruct((g.shape[0], g.shape[1] // 2, g.shape[2]), g.dtype) for g in grads],
        scratch_shapes=[pltpu.SemaphoreType.DMA((n,)), pltpu.SemaphoreType.DMA((n,))],
        compiler_params=pltpu.CompilerParams(has_side_effects=True),
    )(*grads)


_HBM = pl.BlockSpec(memory_space=pltpu.HBM)
_SEM = pl.BlockSpec(memory_space=pltpu.SEMAPHORE)
_DATAFLOW = pltpu.SideEffectType.DATAFLOW_SIDE_EFFECTING


def _chip_exchange_copies(ins, lands, send_sems, recv_sems):
    x, y, c, chips = _place()
    me = 2 * x + y
    cps = []
    for a in range(len(ins)):
        for j, (qx, qy) in enumerate(chips):
            cps.append(pltpu.make_async_remote_copy(
                src_ref=ins[a].at[2 * qx + qy], dst_ref=lands[a].at[me], send_sem=send_sems.at[3 * a + j],
                recv_sem=recv_sems.at[3 * a + j], device_id=(qx, qy, c), device_id_type=MESH))
    return cps


def _chip_exchange_start(parts, tag):
    n = len(parts)

    def body(*refs):
        ins, lands = refs[:n], refs[n:2 * n]
        send_sems, recv_sems = refs[2 * n:2 * n + 2]
        token = refs[4 * n + 2]
        for cp in _chip_exchange_copies(ins, lands, send_sems, recv_sems):
            cp.start()
        token[...] = jnp.zeros_like(token)

    hbm = [pltpu.HBM(p.shape, p.dtype) for p in parts]
    lands = [pltpu.with_memory_space_constraint(lax.empty(p.shape, p.dtype), pltpu.HBM) for p in parts]
    res = pl.pallas_call(
        body, name="chip_exchange_start_" + tag,
        out_shape=(pltpu.SemaphoreType.DMA((3 * n,)), pltpu.SemaphoreType.DMA((3 * n,)), *hbm, *hbm,
                   jax.ShapeDtypeStruct((8, 128), F32)),
        in_specs=[_HBM] * (2 * n), out_specs=(_SEM, _SEM, *([_HBM] * (2 * n)), pl.BlockSpec(memory_space=pltpu.VMEM)),
        input_output_aliases={a: 2 + a for a in range(2 * n)},
        compiler_params=pltpu.CompilerParams(has_side_effects=_DATAFLOW),
    )(*[pltpu.with_memory_space_constraint(p, pltpu.HBM) for p in parts], *lands)
    return res[0], res[1], res[2:2 + n], res[2 + n:2 + 2 * n], res[2 + 2 * n]


def _chip_exchange_wait(send_sems, recv_sems, parts, lands, after, tag):
    n = len(parts)

    def body(*refs):
        ins, land_refs = refs[:n], refs[n:2 * n]
        s_sems, r_sems = refs[2 * n:2 * n + 2]
        for cp in _chip_exchange_copies(ins, land_refs, s_sems, r_sems):
            cp.wait_send()
            cp.wait_recv()

    hbm = [pltpu.HBM(p.shape, p.dtype) for p in parts]
    res = pl.pallas_call(
        body, name="chip_exchange_wait_" + tag, out_shape=(*hbm, *hbm),
        in_specs=[_HBM] * (2 * n) + [_SEM, _SEM, _ANY], out_specs=tuple([_HBM] * (2 * n)),
        input_output_aliases={a: a for a in range(2 * n)},
        compiler_params=pltpu.CompilerParams(has_side_effects=_DATAFLOW),
    )(*parts, *lands, send_sems, recv_sems, after)
    return res[:n], res[n:]


def _shard_gather_copies(src, land, send_sems, recv_sems):
    x, y, c, chips = _place()
    me = 2 * x + y
    return [pltpu.make_async_remote_copy(
        src_ref=src, dst_ref=land.at[me], send_sem=send_sems.at[j], recv_sem=recv_sems.at[j],
        device_id=(qx, qy, c), device_id_type=MESH) for j, (qx, qy) in enumerate(chips)]


def _shard_gather_start(shard_arr, after):
    def body(src, land, after_ref, send_sems, recv_sems, src_thru, land_thru, token):
        for cp in _shard_gather_copies(src, land, send_sems, recv_sems):
            cp.start()
        token[...] = jnp.zeros_like(token)

    land_shape = (N_SHARD,) + shard_arr.shape
    land = pltpu.with_memory_space_constraint(lax.empty(land_shape, shard_arr.dtype), pltpu.HBM)
    return pl.pallas_call(
        body, name="shard_gather_start",
        out_shape=(pltpu.SemaphoreType.DMA((N_SHARD - 1,)), pltpu.SemaphoreType.DMA((N_SHARD - 1,)),
                   pltpu.HBM(shard_arr.shape, shard_arr.dtype), pltpu.HBM(land_shape, shard_arr.dtype),
                   jax.ShapeDtypeStruct((8, 128), F32)),
        in_specs=[_HBM, _HBM, _ANY], out_specs=(_SEM, _SEM, _HBM, _HBM, pl.BlockSpec(memory_space=pltpu.VMEM)),
        input_output_aliases={0: 2, 1: 3},
        compiler_params=pltpu.CompilerParams(has_side_effects=_DATAFLOW),
    )(pltpu.with_memory_space_constraint(shard_arr, pltpu.HBM), land, after)


def _shard_gather_wait(send_sems, recv_sems, shard_arr, land, after):
    def body(src, land_ref, s_sems, r_sems, after_ref, src_out, land_out):
        for cp in _shard_gather_copies(src, land_ref, s_sems, r_sems):
            cp.wait_send()
            cp.wait_recv()

    return pl.pallas_call(
        body, name="shard_gather_wait",
        out_shape=(pltpu.HBM(shard_arr.shape, shard_arr.dtype), pltpu.HBM(land.shape, land.dtype)),
        in_specs=[_HBM, _HBM, _SEM, _SEM, _ANY], out_specs=(_HBM, _HBM), input_output_aliases={0: 0, 1: 1},
        compiler_params=pltpu.CompilerParams(has_side_effects=_DATAFLOW),
    )(shard_arr, land, send_sems, recv_sems, after)


def _pair_allgather(fulls, tag):
    n = len(fulls)

    def body(*refs):
        outs = refs[n:2 * n]
        send_sems, recv_sems = refs[2 * n:]
        x, y, c, _ = _place()
        sibling = (x, y, 1 - c)
        cps = []
        for a in range(n):
            half = outs[a].shape[0] // 2
            mine = outs[a].at[pl.ds(pl.multiple_of(c * half, 8), half)]
            cp = pltpu.make_async_remote_copy(
                src_ref=mine, dst_ref=mine, send_sem=send_sems.at[a], recv_sem=recv_sems.at[a],
                device_id=sibling, device_id_type=MESH)
            cp.start()
            cps.append(cp)
        for a in range(n):
            half = outs[a].shape[0] // 2
            theirs = outs[a].at[pl.ds(pl.multiple_of((1 - c) * half, 8), half)]
            pltpu.make_async_remote_copy(
                src_ref=theirs, dst_ref=theirs, send_sem=send_sems.at[a], recv_sem=recv_sems.at[a],
                device_id=sibling, device_id_type=MESH).wait_recv()
        for cp in cps:
            cp.wait_send()

    return pl.pallas_call(
        body, name="pair_allgather_" + tag, in_specs=[_ANY] * n, out_specs=[_ANY] * n,
        out_shape=[jax.ShapeDtypeStruct(f.shape, f.dtype) for f in fulls],
        input_output_aliases={a: a for a in range(n)},
        scratch_shapes=[pltpu.SemaphoreType.DMA((n,)), pltpu.SemaphoreType.DMA((n,))],
        compiler_params=pltpu.CompilerParams(has_side_effects=True),
    )(*fulls)


def _allgather_small(slab, after):
    def body(s_ref, after_ref, out_ref, send_sems, recv_sems):
        x, y, c, _ = _place()
        me = 4 * x + 2 * y + c
        out_ref[me] = s_ref[...]
        cps = []
        for mask in range(1, N_DEV):
            peer = (x ^ (mask >> 2), y ^ ((mask >> 1) & 1), c ^ (mask & 1))
            cp = pltpu.make_async_remote_copy(
                src_ref=s_ref, dst_ref=out_ref.at[me], send_sem=send_sems.at[mask - 1], recv_sem=recv_sems.at[mask - 1],
                device_id=peer, device_id_type=MESH)
            cp.start()
            cps.append(cp)
        for mask in range(1, N_DEV):
            peer = (x ^ (mask >> 2), y ^ ((mask >> 1) & 1), c ^ (mask & 1))
            dst = out_ref.at[4 * peer[0] + 2 * peer[1] + peer[2]]
            pltpu.make_async_remote_copy(
                src_ref=dst, dst_ref=dst, send_sem=send_sems.at[mask - 1], recv_sem=recv_sems.at[mask - 1],
                device_id=peer, device_id_type=MESH).wait_recv()
        for cp in cps:
            cp.wait_send()

    vm = pl.BlockSpec(memory_space=pltpu.VMEM)
    return pl.pallas_call(
        body, name="allgather_small", in_specs=[vm, _ANY], out_specs=vm,
        out_shape=jax.ShapeDtypeStruct((N_DEV,) + slab.shape, slab.dtype),
        scratch_shapes=[pltpu.SemaphoreType.DMA((N_DEV - 1,)), pltpu.SemaphoreType.DMA((N_DEV - 1,))],
        compiler_params=pltpu.CompilerParams(has_side_effects=True),
    )(slab, after)


def _pack_b(w_mem_kv, w_br_dn, w_br_sb, w_br_mem, w_out):
    return jnp.concatenate([w_mem_kv.reshape(128, D_MODEL), w_br_dn, w_br_sb, w_br_mem.reshape(64, D_MODEL), w_out],
                           axis=0)


def _conv_slab(conv_w):
    return jnp.pad(conv_w.reshape(3, D_MODEL), ((0, 29), (0, 0)))


def _unpack_b(slab):
    return (slab[B_MEMKV:B_BRDN].reshape(1, 256, 512), slab[B_BRDN:B_BRSB].reshape(1, 256, D_MODEL),
            slab[B_BRSB:B_BRMEM].reshape(1, 256, D_MODEL), slab[B_BRMEM:B_OUT].reshape(1, 256, 256),
            slab[B_OUT:B_CONV].reshape(1, 256, D_MODEL))


def _conv_rows(conv_full):
    return conv_full.reshape(4 * CONV_BLOCKS, 128)


def _conv_shard_rows(conv_shard, shard):
    own = CONV_BLOCKS // N_SHARD
    blocks = lax.dynamic_update_slice(jnp.zeros((4, CONV_BLOCKS, 128), F32), conv_shard.reshape(4, own, 128),
                                      (0, own * shard, 0))
    return blocks.reshape(4 * CONV_BLOCKS, 128)


def _conv_shard_of(rows, shard):
    own = CONV_BLOCKS // N_SHARD
    blocks = lax.dynamic_slice(rows.reshape(4, CONV_BLOCKS, 128), (0, own * shard, 0), (4, own, 128))
    return blocks.reshape(1, 4, own * 128)


def _pack_small(norm_g, mem_norm_g, final_g, dn_norm_g, a_log, dt_bias, conv_rows, loss=None):
    def row(v):
        v = v.reshape(1, -1).astype(F32)
        return jnp.pad(v, ((0, 0), (0, 128 - v.shape[1])))

    loss_row = row(jnp.zeros((1,), F32) if loss is None else jnp.reshape(loss, (1,)))
    rid = lax.broadcasted_iota(jnp.int32, (8, 128), 0) + S_DNNORM
    tile = jnp.where(rid == S_DNNORM, dn_norm_g.reshape(1, 128), jnp.where(
        rid == S_ALOG, row(a_log), jnp.where(rid == S_DTB, row(dt_bias), jnp.where(rid == S_LOSS, loss_row, 0.0))))
    return jnp.concatenate([norm_g.reshape(8, 128), mem_norm_g.reshape(8, 128), final_g.reshape(8, 128), tile,
                            conv_rows], axis=0)


def _unpack_small(slab, shard):
    return (slab[S_NORM:S_NORM + 8].reshape(1, D_MODEL), slab[S_MEMNORM:S_MEMNORM + 8].reshape(1, D_MODEL),
            slab[S_FINAL:S_FINAL + 8].reshape(D_MODEL), slab[S_DNNORM].reshape(1, 128),
            slab[S_ALOG, :N_HEADS].reshape(1, N_HEADS), slab[S_DTB, :N_HEADS].reshape(1, N_HEADS),
            _conv_shard_of(slab[S_CONV:], shard))


def _windows_to_w_r(win):
    b = 128
    s0, s1, s2, s3 = win[0], win[1], win[2], win[3]
    e1, e2, e3 = WIN_START[1] * b, WIN_START[2] * b, WIN_START[3] * b
    n1, n2 = e2 - e1, e3 - e2
    return jnp.concatenate([
        s0[:, :e1], s0[:, e1:e1 + b] + s1[:, :b],
        s1[:, b:n1], s1[:, n1:n1 + b] + s2[:, :b],
        s2[:, b:n2], s2[:, n2:n2 + b] + s3[:, :b],
        s3[:, b:], s1[:, _S1_BA_POS:], jnp.zeros((win.shape[1], W_R - C_BA - b), win.dtype)], axis=1)


def _dproj_windows(dproj_r):
    b = 128
    pieces = []
    for s in range(N_SHARD):
        lo = WIN_START[s] * b
        if s == 1:
            pieces += [dproj_r[:, lo:lo + _S1_BA_POS], dproj_r[:, C_BA:C_BA + b]]
        else:
            pieces.append(dproj_r[:, lo:lo + SHARD_PAD])
    return jnp.concatenate(pieces, axis=1)


def _local_step(x, mem, tgt, norm_g, mem_norm_g, w_r, w_sh, conv_w, a_log, dt_bias, dn_norm_g, proj_weights, final_g,
                on_early=None, after_gather=None):
    t = x.shape[0]
    final_row = final_g.reshape(1, D_MODEL)
    lanes_8_16 = ((0, 0), (N_HEADS, 128 - 2 * N_HEADS))
    alog_row = jnp.pad(a_log.reshape(1, N_HEADS), lanes_8_16)
    dtb_row = jnp.pad(dt_bias.reshape(1, N_HEADS), lanes_8_16)

    h = _rmsnorm_fwd(x, norm_g, "norm_fwd")
    proj = _mm(h, w_r, "nn", "in_proj", after=after_gather, tm_max=2048)
    qkv = _dn_prep_fwd(proj, conv_w)
    beta_t, g_t = _dn_gate_fwd(proj, alog_row, dtb_row)
    dn_u, dn_w, dn_qd, dn_kd, dn_a, tinv_all, dn_el = _dn_intra_fwd(qkv, beta_t, g_t)
    o_dn, dn_vn, s_all = _dn_scan_fwd(dn_u, dn_w, dn_qd, dn_kd, dn_a, dn_el)
    o_dn_g = _dn_post_fwd(o_dn, proj, dn_norm_g)
    o_sb, o_sb_g, sb_l = _sb_fwd(proj)
    w_mem_kv, w_br_dn, w_br_sb, w_br_mem, w_out = proj_weights(o_sb_g)
    mem_n = _rmsnorm_fwd(mem, mem_norm_g, "mem_norm_fwd")
    mkv = _mm(mem_n, w_mem_kv, "nn", "mem_kv")
    o_m, o_m_g = _mem_fwd(proj, mkv)
    y_dn = _mm(o_dn_g, w_br_dn, "nn", "br_dn")
    y_sb = _mm(o_sb_g, w_br_sb, "nn", "br_sb")
    y_m = _mm(o_m_g, w_br_mem, "nn", "br_mem")
    merged = _merge_fwd(proj, y_dn, y_sb, y_m)
    mo = _mm(merged, w_out, "nn", "out_proj")
    d_out, d_out_b, loss_row, g_final = _final_loss(x, mo, final_row, tgt)

    g_w_out = _mm(merged, d_out_b, "tn", "g_w_out", out_dtype=BF16)
    d_merged = _mm(d_out_b, w_out, "nt", "d_merged")
    dy_dn, dy_sb, dy_m, dg1, dg2, dg3 = _merge_bwd(proj, y_dn, y_sb, y_m, d_merged)
    g_w_br_dn = _mm(o_dn_g, dy_dn, "tn", "g_w_br_dn", out_dtype=BF16)
    g_w_br_sb = _mm(o_sb_g, dy_sb, "tn", "g_w_br_sb", out_dtype=BF16)
    g_w_br_mem = _mm(o_m_g, dy_m, "tn", "g_w_br_mem", out_dtype=BF16)
    d_o_dn_g = _mm(dy_dn, w_br_dn, "nt", "d_o_dn")
    d_o_sb_g = _mm(dy_sb, w_br_sb, "nt", "d_o_sb")
    d_o_m_g = _mm(dy_m, w_br_mem, "nt", "d_o_mem")

    d_mq, d_mz, d_mkv = _mem_bwd(proj, mkv, o_m, d_o_m_g)
    d_mkv_b = _cast_bf16(d_mkv, "cast_dmkv")
    g_w_mem_kv = _mm(mem_n, d_mkv_b, "tn", "g_w_mem_kv", out_dtype=BF16)
    d_mem_n = _mm(d_mkv_b, w_mem_kv, "nt", "d_mem_n")
    _, g_mem_norm = _rmsnorm_bwd(mem, mem_norm_g, d_mem_n, jnp.zeros_like(mem), "mem_norm_bwd")

    early = dict(w_mem_kv=g_w_mem_kv, w_br_dn=g_w_br_dn, w_br_sb=g_w_br_sb, w_br_mem=g_w_br_mem, w_out=g_w_out)
    after_early = on_early(early) if on_early is not None else None

    d_sq, d_sk, d_sv, d_sz = _sb_bwd(proj, o_sb, sb_l, d_o_sb_g, after=after_early)

    d_o_dn, d_dnz, g_dn_norm = _dn_post_bwd(o_dn, proj, dn_norm_g, d_o_dn_g)
    d_vnew, d_kd, d_qd, d_w, d_el = _dn_scan_bwd(dn_w, dn_qd, dn_kd, dn_a, dn_el, dn_vn, s_all, d_o_dn)
    d_qn, d_kn, d_vn, dbeta_t, dg_t = _dn_intra_bwd(qkv, beta_t, g_t, tinv_all, dn_vn, d_o_dn, d_vnew, d_kd, d_qd, d_w, d_el)
    d_conv_in, g_conv = _dn_prep_bwd(proj, conv_w, d_qn, d_kn, d_vn)
    d_ba, g_alog_row, g_dtb_row = _dn_gate_bwd(proj, alog_row, dtb_row, dbeta_t, dg_t)

    dproj_sh = _dproj_windows(
        jnp.concatenate([d_conv_in, d_dnz, d_sq, d_sk, d_sv, d_sz, d_mq, d_mz, dg1, dg2, dg3, d_ba], axis=1))
    g_w_sh = _mm(h, dproj_sh, "tn", "g_w_in", out_dtype=BF16, out_shards=N_SHARD, tn_max=1024)
    def input_grad(after=None):
        dh = _mm(dproj_sh, w_sh, "nt", "d_h", after=after, tm_max=2048, tn_max=1024)
        grad_x, g_norm = _rmsnorm_bwd(x, norm_g, dh, d_out, "norm_bwd")
        small = dict(norm_g=g_norm, mem_norm_g=g_mem_norm, final_g=g_final, dn_norm_g=g_dn_norm,
                     a_log=g_alog_row[:, N_HEADS:2 * N_HEADS], dt_bias=g_dtb_row[:, N_HEADS:2 * N_HEADS],
                     conv_w=g_conv)
        return grad_x, small

    return loss_row[0, 0], early, g_w_sh, input_grad


def _reduce_scatter_start(grads, tag):
    c = lax.axis_index("c")
    core = jnp.reshape(c, (1,)).astype(jnp.int32)
    recv = _pair_reduce_send(grads, tag)
    parts = [_pair_add(g, r, core, "pair_add_" + tag) for g, r in zip(grads, recv)]
    return _chip_exchange_start(parts, tag)


def _reduce_scatter_finish(handle, after, tag):
    send_sems, recv_sems, parts, lands, _ = handle
    x, y, c = lax.axis_index("x"), lax.axis_index("y"), lax.axis_index("c")
    place = jnp.stack([2 * x + y, c]).astype(jnp.int32)
    parts, by_chip = _chip_exchange_wait(send_sems, recv_sems, parts, lands, after, tag)
    fulls = [_chip_sum(p, b, place, "chip_sum_" + tag) for p, b in zip(parts, by_chip)]
    return _pair_allgather(fulls, tag)


def kernel(x, mem, norm_g, mem_norm_g, w_in, conv_w, a_log, dt_bias, dn_norm_g, w_mem_kv, w_br_dn, w_br_sb, w_br_mem, w_out, final_g, loss_target, m_norm_g, m_mem_norm_g, m_w_in, m_conv_w, m_a_log, m_dt_bias, m_dn_norm_g, m_w_mem_kv, m_w_br_dn, m_w_br_sb, m_w_br_mem, m_w_out, m_final_g, v_norm_g, v_mem_norm_g, v_w_in, v_conv_w, v_a_log, v_dt_bias, v_dn_norm_g, v_w_mem_kv, v_w_br_dn, v_w_br_sb, v_w_br_mem, v_w_out, v_final_g):
    w_a = w_in[0]
    w_b = _pack_b(w_mem_kv[0], w_br_dn[0], w_br_sb[0], w_br_mem[0], w_out[0])
    m_b = _pack_b(m_w_mem_kv[0], m_w_br_dn[0], m_w_br_sb[0], m_w_br_mem[0], m_w_out[0])
    v_b = _pack_b(v_w_mem_kv[0], v_w_br_dn[0], v_w_br_sb[0], v_w_br_mem[0], v_w_out[0])

    shard_idx = 2 * lax.axis_index("x") + lax.axis_index("y")
    shard = jnp.reshape(shard_idx, (1,)).astype(jnp.int32)
    ga, g_conv = _gather_shards([_cast_to_window(w_a, shard, "cast_w_in"),
                                 _cast_bf16(_conv_slab(conv_w[0]), "cast_conv")])
    w_r = _windows_to_w_r(ga)
    f_conv = g_conv[:, :3].reshape(N_SHARD, 4, 768).transpose(1, 0, 2).reshape(4, 3 * D_MODEL).astype(F32)
    b_flight = _shard_gather_start(_cast_bf16(w_b, "cast_w_b"), after=ga)

    def proj_weights(after):
        own, land = _shard_gather_wait(b_flight[0], b_flight[1], b_flight[2], b_flight[3], after)
        gb = lax.dynamic_update_slice(land, own[None], (shard_idx, 0, 0))
        return (gb[:, B_MEMKV:B_BRDN].reshape(N_SHARD * 256, 512),
                gb[:, B_BRDN:B_BRSB].reshape(N_SHARD * 256, D_MODEL),
                gb[:, B_BRSB:B_BRMEM].reshape(N_SHARD * 256, D_MODEL),
                gb[:, B_BRMEM:B_OUT].reshape(N_SHARD, 256, 256).transpose(1, 0, 2).reshape(256, D_MODEL),
                gb[:, B_OUT:B_CONV].reshape(N_SHARD * 256, D_MODEL))

    flights = {}

    def on_early(grads):
        g_b = jnp.concatenate([
            grads["w_mem_kv"].reshape(N_SHARD, 128, D_MODEL), grads["w_br_dn"].reshape(N_SHARD, 256, D_MODEL),
            grads["w_br_sb"].reshape(N_SHARD, 256, D_MODEL),
            grads["w_br_mem"].reshape(256, N_SHARD, 256).transpose(1, 0, 2).reshape(N_SHARD, 64, D_MODEL),
            grads["w_out"].reshape(N_SHARD, 256, D_MODEL)], axis=1).astype(BF16)
        flights["b"] = _reduce_scatter_start([g_b], "b")
        return flights["b"][4]

    loss, _, g_w_sh, input_grad = _local_step(
        x[0], mem[0], loss_target[0], norm_g, mem_norm_g, w_r, ga, f_conv, a_log, dt_bias, dn_norm_g,
        proj_weights, final_g, on_early=on_early, after_gather=b_flight[4])
    flights["a"] = _reduce_scatter_start([g_w_sh], "a")
    grad_x, small = input_grad(after=flights["a"][4])

    part = _pack_small(small["norm_g"], small["mem_norm_g"], small["final_g"], small["dn_norm_g"],
                       small["a_log"], small["dt_bias"], _conv_rows(small["conv_w"]), loss)
    w_s = _pack_small(norm_g, mem_norm_g, final_g, dn_norm_g, a_log, dt_bias, _conv_shard_rows(conv_w[0], shard_idx))
    m_s = _pack_small(m_norm_g, m_mem_norm_g, m_final_g, m_dn_norm_g, m_a_log, m_dt_bias,
                      _conv_shard_rows(m_conv_w[0], shard_idx))
    v_s = _pack_small(v_norm_g, v_mem_norm_g, v_final_g, v_dn_norm_g, v_a_log, v_dt_bias,
                      _conv_shard_rows(v_conv_w[0], shard_idx))
    (gs_b,) = _reduce_scatter_finish(flights["b"], after=grad_x, tag="b")
    gr_b, d_b, nm_b, nv_b = _adamw(w_b, gs_b, m_b, v_b, "adamw_b")
    g_s, d_s, nm_s, nv_s = _small_update(_allgather_small(part, after=d_b), w_s, m_s, v_s)

    (gs_in,) = _reduce_scatter_finish(flights["a"], after=g_s, tag="a")
    gr_in, d_in, nm_in, nv_in = _adamw_window(w_a, gs_in, m_w_in[0], v_w_in[0], shard, "adamw_w_in")

    def assemble(slab_small, a_in, slab_b):
        s_norm, s_memnorm, s_final, s_dnnorm, s_alog, s_dtb, b_conv = _unpack_small(slab_small, shard_idx)
        b_memkv, b_brdn, b_brsb, b_brmem, b_out = _unpack_b(slab_b)
        return [s_norm, s_memnorm, a_in.reshape(1, D_MODEL, IN_WIDTH // N_SHARD), b_conv, s_alog, s_dtb, s_dnnorm,
                b_memkv, b_brdn, b_brsb, b_brmem, b_out, s_final]

    outs = [g_s[S_LOSS, 0], grad_x.reshape(1, -1, D_MODEL)]
    outs += assemble(g_s, gr_in, gr_b)
    outs += assemble(d_s, d_in, d_b)
    outs += assemble(nm_s, nm_in, nm_b)
    outs += assemble(nv_s, nv_in, nv_b)
    return tuple(outs)
```

```python
import math

import jax
import jax.numpy as jnp
from jax import lax
from jax.experimental import pallas as pl
from jax.experimental.pallas import tpu as pltpu

F32 = jnp.float32
BF16 = jnp.bfloat16
MESH = pl.DeviceIdType.MESH

D_MODEL = 1024
N_HEADS = 8
D_HEAD = 128
DN_CHUNK = 64
DN_GROUP = 16
DN_SCAN_GROUP = 4
SB_BLOCK = 256
SB_HEADS_PER_STEP = 2
SB_QBLOCK = 256
MEM_HEADS = 4
MEM_DH = 64
MEM_W = MEM_HEADS * MEM_DH
NORM_EPS = 1e-6
IN_WIDTH = 11792
N_SHARD = 4
SHARD_W = IN_WIDTH // N_SHARD
SHARD_PAD = 3072
N_DEV = 8

C_DNZ = 3072
C_SBQ = 4096
C_SBZ = 7168
C_MQ = 8192
C_MZ = 8448
C_GATES = 8704
C_BA = 11776
W_R = 12288

ADAM_LR = 0.001
ADAM_B1 = 0.9
ADAM_B2 = 0.999
ADAM_EPS = 1e-08
ADAM_WD = 0.01
ADAM_STEP = 10

VMEM_LIMIT = 56 * 1024 * 1024

B_MEMKV, B_BRDN, B_BRSB, B_BRMEM, B_OUT, B_CONV = 0, 128, 384, 640, 704, 960
S_NORM, S_MEMNORM, S_FINAL, S_DNNORM, S_ALOG, S_DTB, S_LOSS, S_CONV, S_ROWS = 0, 8, 16, 24, 25, 26, 27, 32, 128
CONV_BLOCKS = 3 * D_MODEL // 128


def _cp(**kw):
    return pltpu.CompilerParams(vmem_limit_bytes=VMEM_LIMIT, **kw)


def _dot(a, b, dims):
    lead = a.ndim - 2
    ca, cb = {"nn": (1, 0), "nt": (1, 1), "tn": (0, 0)}[dims]
    batch = tuple(range(lead))
    return lax.dot_general(a, b, (((ca + lead,), (cb + lead,)), (batch, batch)), preferred_element_type=F32)


def _chunks(x):
    return x.reshape(x.shape[0] // DN_CHUNK, DN_CHUNK, x.shape[1])


def _unchunk(x):
    return x.reshape(x.shape[0] * x.shape[1], x.shape[2])


def _bdot(a, b, dims):
    return _dot(a.astype(BF16), b.astype(BF16), dims)


def _split(a):
    hi = a.astype(BF16)
    return hi, (a - hi.astype(F32)).astype(BF16)


def _dot3(a, b, dims):
    a1, a2 = _split(a)
    b1, b2 = _split(b)
    return _dot(a1, b1, dims) + (_dot(a1, b2, dims) + _dot(a2, b1, dims))


def _ones_dot(a, ones_bf16):
    out = _dot(a.reshape(-1, a.shape[-1]).astype(BF16), ones_bf16, "nn")
    return out.reshape(a.shape[:-1] + (ones_bf16.shape[1],))


def _sigmoid(x):
    return 1.0 / (1.0 + jnp.exp(-x))


def _log1p_small(u):
    return jnp.where(u < 1e-2, u * (1.0 - u * (0.5 - u * (1.0 / 3.0))), jnp.log(1.0 + u))


def _pick(dim, cands):
    for c in cands:
        if dim % c == 0:
            return c
    return dim


def _mm(a, b, dims, name, out_dtype=F32, out_shards=1, after=None, tm_max=1024, tn_max=512):
    ta, tb = dims[0] == "t", dims[1] == "t"
    m, k = (a.shape[1], a.shape[0]) if ta else a.shape
    b_shards = b.shape[0] if b.ndim == 3 else 1
    n = b.shape[-2] if tb else b.shape[-1]
    tm = _pick(m, (tm_max, 1024, 512, 256))
    tn = _pick(n // out_shards, (tn_max, 512, 384, 256, 128))
    tk = _pick(k // b_shards, (2048, 1024, 512, 384, 256))
    nk = k // tk

    def body(a_ref, b_ref, *rest):
        if nk == 1:
            rest[-1][...] = _bdot(a_ref[...], b_ref[...], dims).astype(out_dtype)
            return
        o_ref, acc_ref = rest[-2:]
        kk = pl.program_id(2)

        @pl.when(kk == 0)
        def _():
            acc_ref[...] = jnp.zeros_like(acc_ref)

        acc_ref[...] += _bdot(a_ref[...], b_ref[...], dims)

        @pl.when(kk == nk - 1)
        def _():
            o_ref[...] = acc_ref[...].astype(out_dtype)

    a_spec = pl.BlockSpec((tk, tm), lambda i, j, q: (q, i)) if ta else pl.BlockSpec((tm, tk), lambda i, j, q: (i, q))
    if b_shards > 1:
        per_k = k // b_shards // tk
        b_spec = pl.BlockSpec((None, tn, tk), lambda i, j, q: (q // per_k, j, q % per_k))
    else:
        b_spec = pl.BlockSpec((tn, tk), lambda i, j, q: (j, q)) if tb else pl.BlockSpec((tk, tn), lambda i, j, q: (q, j))
    if out_shards > 1:
        per_n = n // out_shards // tn
        out_spec = pl.BlockSpec((None, tm, tn), lambda i, j, q: (j // per_n, i, j % per_n))
        out_shape = jax.ShapeDtypeStruct((out_shards, m, n // out_shards), out_dtype)
    else:
        out_spec = pl.BlockSpec((tm, tn), lambda i, j, q: (i, j))
        out_shape = jax.ShapeDtypeStruct((m, n), out_dtype)
    extra_specs, extra = [], []
    if after is not None:
        extra_specs, extra = [pl.BlockSpec(after.shape, lambda i, j, q: (0, 0))], [after]
    return pl.pallas_call(
        body, name=name, grid=(m // tm, n // tn, nk),
        in_specs=[a_spec, b_spec] + extra_specs, out_specs=out_spec, out_shape=out_shape,
        scratch_shapes=[pltpu.VMEM((tm, tn), F32)] if nk > 1 else [],
        compiler_params=_cp(dimension_semantics=("parallel", "parallel", "arbitrary")),
    )(a, b, *extra)


def _rmsnorm_fwd(x, g, name):
    t, d = x.shape
    tb = _pick(t, (512, 256))

    def body(x_ref, g_ref, h_ref):
        xv = x_ref[...]
        r = lax.rsqrt(jnp.mean(xv * xv, axis=-1, keepdims=True) + NORM_EPS)
        h_ref[...] = ((xv * r) * g_ref[...]).astype(BF16)

    return pl.pallas_call(
        body, name=name, grid=(t // tb,),
        in_specs=[pl.BlockSpec((tb, d), lambda i: (i, 0)), pl.BlockSpec((1, d), lambda i: (0, 0))],
        out_specs=pl.BlockSpec((tb, d), lambda i: (i, 0)),
        out_shape=jax.ShapeDtypeStruct((t, d), BF16), compiler_params=_cp(),
    )(x, g)


def _rmsnorm_bwd(x, g, dh, resid, name):
    t, d = x.shape
    tb = _pick(t, (256,))

    def body(x_ref, g_ref, dh_ref, r_ref, dx_ref, dg_ref):
        @pl.when(pl.program_id(0) == 0)
        def _():
            dg_ref[...] = jnp.zeros_like(dg_ref)

        xv = x_ref[...]
        r = lax.rsqrt(jnp.mean(xv * xv, axis=-1, keepdims=True) + NORM_EPS)
        xhat = xv * r
        dhv = dh_ref[...]
        dg_ref[...] += jnp.sum(dhv * xhat, axis=0, keepdims=True)
        dxh = dhv * g_ref[...]
        dx_ref[...] = r_ref[...] + r * (dxh - xhat * jnp.mean(dxh * xhat, axis=-1, keepdims=True))

    row = pl.BlockSpec((tb, d), lambda i: (i, 0))
    vec = pl.BlockSpec((1, d), lambda i: (0, 0))
    return pl.pallas_call(
        body, name=name, grid=(t // tb,), in_specs=[row, vec, row, row], out_specs=[row, vec],
        out_shape=[jax.ShapeDtypeStruct((t, d), F32), jax.ShapeDtypeStruct((1, d), F32)], compiler_params=_cp(),
    )(x, g, dh, resid)


def _conv_silu(xv, w, row):
    y = xv * w[3:4, :]
    for s in (1, 2, 3):
        xs = jnp.where(row >= s, pltpu.roll(xv, s, 0), 0.0)
        y = y + xs * w[3 - s:4 - s, :]
    sig = _sigmoid(y)
    return y, sig, y * sig


def _dn_prep_fwd(proj, conv_w):
    t = proj.shape[0]

    def body(p_ref, w_ref, o_ref):
        j = pl.program_id(0)
        xv = p_ref[...]
        row = lax.broadcasted_iota(jnp.int32, xv.shape, 0)
        _, _, a = _conv_silu(xv, w_ref[...], row)
        inv = lax.rsqrt(jnp.sum(a * a, axis=-1, keepdims=True) + NORM_EPS)
        scale = jnp.where(j < N_HEADS, D_HEAD ** -0.5, 1.0)
        normed = jnp.where(j < 2 * N_HEADS, 1.0, 0.0)
        o_ref[...] = a * (normed * (inv * scale) + (1.0 - normed))

    return pl.pallas_call(
        body, name="dn_prep_fwd", grid=(3 * N_HEADS,),
        in_specs=[pl.BlockSpec((t, D_HEAD), lambda j: (0, j)), pl.BlockSpec((4, D_HEAD), lambda j: (0, j))],
        out_specs=pl.BlockSpec((t, D_HEAD), lambda j: (0, j)),
        out_shape=jax.ShapeDtypeStruct((t, 3 * D_MODEL), F32), compiler_params=_cp(),
    )(proj, conv_w)


def _dn_prep_bwd(proj, conv_w, dq, dk, dv):
    t = proj.shape[0]

    def body(p_ref, w_ref, dq_ref, dk_ref, dv_ref, dp_ref, dw_ref):
        j = pl.program_id(0)
        xv = p_ref[...]
        w = w_ref[...]
        row = lax.broadcasted_iota(jnp.int32, xv.shape, 0)
        y, s, a = _conv_silu(xv, w, row)
        part = jnp.zeros(xv.shape, jnp.int32) + j // N_HEADS
        dn = jnp.where(part == 0, dq_ref[...], jnp.where(part == 1, dk_ref[...], dv_ref[...]))
        inv = lax.rsqrt(jnp.sum(a * a, axis=-1, keepdims=True) + NORM_EPS)
        scale = jnp.where(j < N_HEADS, D_HEAD ** -0.5, 1.0)
        ds = dn * scale
        da_norm = inv * ds - a * (inv * inv * inv) * jnp.sum(ds * a, axis=-1, keepdims=True)
        normed = jnp.where(j < 2 * N_HEADS, 1.0, 0.0)
        da = normed * da_norm + (1.0 - normed) * dn
        dy = da * (s * (1.0 + y * (1.0 - s)))
        dx = dy * w[3:4, :]
        dw_ref[3:4, :] = jnp.sum(dy * xv, axis=0, keepdims=True)
        for sft in (1, 2, 3):
            xs = jnp.where(row >= sft, pltpu.roll(xv, sft, 0), 0.0)
            dw_ref[3 - sft:4 - sft, :] = jnp.sum(dy * xs, axis=0, keepdims=True)
            dys = jnp.where(row < t - sft, pltpu.roll(dy, t - sft, 0), 0.0)
            dx = dx + dys * w[3 - sft:4 - sft, :]
        dp_ref[...] = dx.astype(BF16)

    blk = pl.BlockSpec((t, D_HEAD), lambda j: (0, j))
    wblk = pl.BlockSpec((4, D_HEAD), lambda j: (0, j))

    def grad(part):
        return pl.BlockSpec((t, D_HEAD), lambda j: (0, jnp.clip(j - part * N_HEADS, 0, N_HEADS - 1)))

    return pl.pallas_call(
        body, name="dn_prep_bwd", grid=(3 * N_HEADS,), in_specs=[blk, wblk, grad(0), grad(1), grad(2)],
        out_specs=[blk, wblk],
        out_shape=[jax.ShapeDtypeStruct((t, 3 * D_MODEL), BF16), jax.ShapeDtypeStruct((4, 3 * D_MODEL), F32)],
        compiler_params=_cp(),
    )(proj, conv_w, dq, dk, dv)


def _softplus_parts(xv):
    e = jnp.exp(-jnp.abs(xv))
    return jnp.maximum(xv, 0.0) + _log1p_small(e)


def _chunk_scan(v, row, reverse):
    t = v.shape[0]
    pos = row & (DN_CHUNK - 1)
    s = 1
    while s < DN_CHUNK:
        if reverse:
            v = v + jnp.where(pos < DN_CHUNK - s, pltpu.roll(v, t - s, 0), 0.0)
        else:
            v = v + jnp.where(pos >= s, pltpu.roll(v, s, 0), 0.0)
        s *= 2
    return v


def _dn_gate_fwd(proj, alog_row, dtb_row):
    t = proj.shape[0]

    def body(p_ref, al_ref, dt_ref, b_ref, g_ref):
        p = p_ref[...]
        row = lax.broadcasted_iota(jnp.int32, p.shape, 0)
        b_ref[...] = _sigmoid(p)
        g = -jnp.exp(al_ref[...]) * _softplus_parts(p + dt_ref[...])
        g_ref[...] = _chunk_scan(g, row, reverse=False)

    blk = pl.BlockSpec((t, 128), lambda i: (0, C_BA // 128))
    vec = pl.BlockSpec((1, 128), lambda i: (0, 0))
    out = pl.BlockSpec((t, 128), lambda i: (0, 0))
    return pl.pallas_call(
        body, name="dn_gate_fwd", grid=(1,), in_specs=[blk, vec, vec], out_specs=[out, out],
        out_shape=[jax.ShapeDtypeStruct((t, 128), F32)] * 2, compiler_params=_cp(),
    )(proj, alog_row, dtb_row)


def _dn_gate_bwd(proj, alog_row, dtb_row, dbeta, dgc):
    t = proj.shape[0]

    def body(p_ref, al_ref, dt_ref, db_ref, dg_ref, dp_ref, dal_ref, ddt_ref):
        p = p_ref[...]
        row = lax.broadcasted_iota(jnp.int32, p.shape, 0)
        lane = lax.broadcasted_iota(jnp.int32, p.shape, 1)
        s = _sigmoid(p)
        d_b = db_ref[...] * s * (1.0 - s)
        dg = _chunk_scan(dg_ref[...], row, reverse=True)
        xa = p + dt_ref[...]
        ea = jnp.exp(al_ref[...])
        g = -ea * _softplus_parts(xa)
        d_a = dg * (-ea) * _sigmoid(xa)
        dp_ref[...] = jnp.where(lane < N_HEADS, d_b, jnp.where(lane < 2 * N_HEADS, d_a, 0.0)).astype(BF16)
        dal_ref[...] = jnp.sum(dg * g, axis=0, keepdims=True)
        ddt_ref[...] = jnp.sum(d_a, axis=0, keepdims=True)

    blk = pl.BlockSpec((t, 128), lambda i: (0, C_BA // 128))
    vec = pl.BlockSpec((1, 128), lambda i: (0, 0))
    full = pl.BlockSpec((t, 128), lambda i: (0, 0))
    return pl.pallas_call(
        body, name="dn_gate_bwd", grid=(1,), in_specs=[blk, vec, vec, full, full], out_specs=[full, vec, vec],
        out_shape=[jax.ShapeDtypeStruct((t, 128), BF16), jax.ShapeDtypeStruct((1, 128), F32),
                   jax.ShapeDtypeStruct((1, 128), F32)], compiler_params=_cp(),
    )(proj, alog_row, dtb_row, dbeta, dgc)


def _col_to_row(col, eye):
    return jnp.sum(jnp.where(eye, col, 0.0), axis=-2, keepdims=True)


def _row_to_col(rowv, eye):
    return jnp.sum(jnp.where(eye, rowv, 0.0), axis=-1, keepdims=True)


def _tri_inverse(m, ri, ci):
    eye = (ri == ci).astype(F32)
    b16 = (ri >> 4) == (ci >> 4)
    b32 = (ri >> 5) == (ci >> 5)
    m1 = jnp.where(b16, m, 0.0)
    x = eye - m1
    p = _dot3(m1, m1, "nn")
    x = x + _dot3(x, p, "nn")
    p = _dot3(p, p, "nn")
    x = x + _dot3(x, p, "nn")
    p = _dot3(p, p, "nn")
    x = x + _dot3(x, p, "nn")
    c1 = jnp.where(jnp.logical_and(b32, jnp.logical_not(b16)), m, 0.0)
    x = x - _dot3(_dot3(x, c1, "nn"), x, "nn")
    c2 = jnp.where(b32, 0.0, m)
    x = x - _dot3(_dot3(x, c2, "nn"), x, "nn")
    return x


def _dn_chunk_common(q, k, gc, ri, ci):
    eye = ri == ci
    g_row = _col_to_row(gc, eye)
    diff = jnp.minimum(gc - g_row, 0.0)
    gam = jnp.where(ri >= ci, jnp.exp(diff), 0.0)
    kk = _bdot(k, k, "nt")
    qk = _bdot(q, k, "nt")
    rcol = lax.broadcasted_iota(jnp.int32, gc.shape, gc.ndim - 2)
    last = jnp.sum(jnp.where(rcol == DN_CHUNK - 1, gc, 0.0), axis=-2, keepdims=True)
    e_g = jnp.exp(gc)
    dec = jnp.exp(last - gc)
    return eye, gam, kk, qk, last, e_g, dec, rcol


def _dn_specs(t, rows_blk):
    def head(off):
        return pl.BlockSpec((rows_blk, D_HEAD), lambda g, h: (g, off + h))

    lanes = pl.BlockSpec((rows_blk, 128), lambda g, h: (g, 0))
    hm = pl.BlockSpec((None, rows_blk, D_HEAD), lambda g, h: (h, g, 0))
    sq = pl.BlockSpec((1, rows_blk, DN_CHUNK), lambda g, h: (h, g, 0))
    tile = pl.BlockSpec((1, rows_blk // DN_CHUNK, 8, 128), lambda g, h: (h, g, 0, 0))
    return head, lanes, hm, sq, tile


def _head_column(slab, lane_idx):
    lane = lax.broadcasted_iota(jnp.int32, slab.shape, 1)
    return _chunks(jnp.sum(jnp.where(lane == lane_idx, slab, 0.0), axis=1, keepdims=True))


def _dn_intra_fwd(qkv, beta_t, g_t):
    t = qkv.shape[0]
    n_chunks = t // DN_CHUNK
    rows_blk = min(DN_GROUP * DN_CHUNK, t)

    def body(q_ref, k_ref, v_ref, b_ref, g_ref, u_ref, w_ref, qd_ref, kd_ref, a_ref, ti_ref, el_ref):
        ri = lax.broadcasted_iota(jnp.int32, (DN_CHUNK, DN_CHUNK), 0)
        ci = lax.broadcasted_iota(jnp.int32, (DN_CHUNK, DN_CHUNK), 1)
        h = pl.program_id(1)
        q, k, v = (_chunks(r[...]) for r in (q_ref, k_ref, v_ref))
        b, gc = _head_column(b_ref[...], h), _head_column(g_ref[...], h + N_HEADS)
        _, gam, kk, qk, last, e_g, dec, _ = _dn_chunk_common(q, k, gc, ri, ci)
        tinv = _tri_inverse(jnp.where(ri > ci, b * kk * gam, 0.0), ri, ci)
        u_ref[...] = _unchunk(_bdot(tinv, v * b, "nn"))
        w_ref[...] = _unchunk(_bdot(tinv, k * (b * e_g), "nn"))
        qd_ref[...] = _unchunk(q * e_g)
        kd_ref[...] = _unchunk(k * dec)
        a_ref[0] = _unchunk(qk * gam)
        ti_ref[0] = _unchunk(tinv)
        el_ref[0] = jnp.broadcast_to(jnp.exp(last), (rows_blk // DN_CHUNK, 8, 128))

    head, lanes, hm, sq, tile = _dn_specs(t, rows_blk)
    act = jax.ShapeDtypeStruct((N_HEADS, t, D_HEAD), F32)
    sqs = jax.ShapeDtypeStruct((N_HEADS, t, DN_CHUNK), F32)
    return pl.pallas_call(
        body, name="dn_intra_fwd", grid=(t // rows_blk, N_HEADS),
        in_specs=[head(0), head(N_HEADS), head(2 * N_HEADS), lanes, lanes],
        out_specs=[hm] * 4 + [sq, sq, tile],
        out_shape=[act] * 4 + [sqs, sqs, jax.ShapeDtypeStruct((N_HEADS, n_chunks, 8, 128), F32)],
        compiler_params=_cp(),
    )(qkv, qkv, qkv, beta_t, g_t)


def _dn_scan_specs(t, rows_blk, reverse):
    n_groups = t // rows_blk

    def at(g):
        return n_groups - 1 - g if reverse else g

    per = rows_blk // DN_CHUNK
    act = pl.BlockSpec((N_HEADS, rows_blk, D_HEAD), lambda g: (0, at(g), 0))
    sq = pl.BlockSpec((N_HEADS, rows_blk, DN_CHUNK), lambda g: (0, at(g), 0))
    state = pl.BlockSpec((N_HEADS, per, D_HEAD, D_HEAD), lambda g: (0, at(g), 0, 0))
    tile = pl.BlockSpec((N_HEADS, per, 8, 128), lambda g: (0, at(g), 0, 0))
    return act, sq, state, tile


def _dn_scan_fwd(u, w, qd, kd, a, el):
    t = u.shape[1]
    n_chunks = t // DN_CHUNK
    rows_blk = DN_SCAN_GROUP * DN_CHUNK

    def body(u_ref, w_ref, qd_ref, kd_ref, a_ref, el_ref, o_ref, vn_ref, s_ref, s_scr):
        @pl.when(pl.program_id(0) == 0)
        def _():
            s_scr[...] = jnp.zeros_like(s_scr)

        for cc in range(DN_SCAN_GROUP):
            rows = slice(cc * DN_CHUNK, (cc + 1) * DN_CHUNK)
            s = s_scr[...]
            s_ref[:, cc] = s
            v_new = u_ref[:, rows, :] - _bdot(w_ref[:, rows, :], s, "nn")
            vn_ref[:, rows, :] = v_new
            o_ref[:, rows, :] = _bdot(qd_ref[:, rows, :], s, "nn") + _bdot(a_ref[:, rows, :], v_new, "nn")
            s_scr[...] = s * el_ref[:, cc][:, 0:1, :] + _bdot(kd_ref[:, rows, :], v_new, "tn")

    act, sq, state, tile = _dn_scan_specs(t, rows_blk, reverse=False)
    shp = jax.ShapeDtypeStruct((N_HEADS, t, D_HEAD), F32)
    return pl.pallas_call(
        body, name="dn_scan_fwd", grid=(t // rows_blk,),
        in_specs=[act, act, act, act, sq, tile], out_specs=[act, act, state],
        out_shape=[shp, shp, jax.ShapeDtypeStruct((N_HEADS, n_chunks, D_HEAD, D_HEAD), F32)],
        scratch_shapes=[pltpu.VMEM((N_HEADS, D_HEAD, D_HEAD), F32)],
        compiler_params=_cp(dimension_semantics=("arbitrary",)),
    )(u, w, qd, kd, a, el)


def _dn_scan_bwd(w, qd, kd, a, el, vn, s_all, do):
    t = w.shape[1]
    n_chunks = t // DN_CHUNK
    rows_blk = DN_SCAN_GROUP * DN_CHUNK

    def body(w_ref, qd_ref, kd_ref, a_ref, el_ref, vn_ref, s_ref, do_ref, dvn_ref, dkd_ref, dqd_ref, dw_ref, dl_ref, ds_scr):
        @pl.when(pl.program_id(0) == 0)
        def _():
            ds_scr[...] = jnp.zeros_like(ds_scr)

        for cc in reversed(range(DN_SCAN_GROUP)):
            rows = slice(cc * DN_CHUNK, (cc + 1) * DN_CHUNK)
            s = s_ref[:, cc]
            d_s = ds_scr[...]
            e_last = el_ref[:, cc][:, 0:1, :]
            d_o = do_ref[:, rows, :]
            dv_new = _bdot(a_ref[:, rows, :], d_o, "tn") + _bdot(kd_ref[:, rows, :], d_s, "nn")
            ds_scr[...] = d_s * e_last + _bdot(qd_ref[:, rows, :], d_o, "tn") - _bdot(w_ref[:, rows, :], dv_new, "tn")
            dvn_ref[:, rows, :] = dv_new
            dkd_ref[:, rows, :] = _bdot(vn_ref[:, rows, :], d_s, "nt")
            dqd_ref[:, rows, :] = _bdot(d_o, s, "nt")
            dw_ref[:, rows, :] = -_bdot(dv_new, s, "nt")
            dlast = jnp.sum(jnp.sum(d_s * s, axis=2, keepdims=True), axis=1, keepdims=True)
            dl_ref[:, cc] = jnp.broadcast_to(dlast * e_last, (N_HEADS, 8, 128))

    act, sq, state, tile = _dn_scan_specs(t, rows_blk, reverse=True)
    shp = jax.ShapeDtypeStruct((N_HEADS, t, D_HEAD), F32)
    return pl.pallas_call(
        body, name="dn_scan_bwd", grid=(t // rows_blk,),
        in_specs=[act, act, act, sq, tile, act, state, act], out_specs=[act] * 4 + [tile],
        out_shape=[shp] * 4 + [jax.ShapeDtypeStruct((N_HEADS, n_chunks, 8, 128), F32)],
        scratch_shapes=[pltpu.VMEM((N_HEADS, D_HEAD, D_HEAD), F32)],
        compiler_params=_cp(dimension_semantics=("arbitrary",)),
    )(w, qd, kd, a, el, vn, s_all, do)


def _dn_intra_bwd(qkv, beta_t, g_t, tinv_all, vn, do, dvn, dkd, dqd, dw, dl):
    t = qkv.shape[0]
    rows_blk = min(DN_GROUP * DN_CHUNK, t)

    def body(q_ref, k_ref, v_ref, b_ref, g_ref, ti_ref, vn_ref, do_ref, dvn_ref, dkd_ref, dqd_ref, dw_ref, dl_ref,
             dq_ref, dk_ref, dv_ref, db_ref, dg_ref):
        ri = lax.broadcasted_iota(jnp.int32, (DN_CHUNK, DN_CHUNK), 0)
        ci = lax.broadcasted_iota(jnp.int32, (DN_CHUNK, DN_CHUNK), 1)
        h = pl.program_id(1)
        q, k, v = (_chunks(r[...]) for r in (q_ref, k_ref, v_ref))
        b, gc = _head_column(b_ref[...], h), _head_column(g_ref[...], h + N_HEADS)
        tinv = _chunks(ti_ref[0])
        dv_new, dk_dec, dq_dec, d_w = (_chunks(r[...]) for r in (dvn_ref, dkd_ref, dqd_ref, dw_ref))
        eye, gam, kk, qk, _, e_g, dec, rcol = _dn_chunk_common(q, k, gc, ri, ci)
        bv = v * b
        bk = k * (b * e_g)

        d_a = jnp.where(ri >= ci, _bdot(_chunks(do_ref[...]), _chunks(vn_ref[...]), "nt"), 0.0)
        dbv = _bdot(tinv, dv_new, "tn")
        dbk = _bdot(tinv, d_w, "tn")
        d_tinv = _bdot(dv_new, bv, "nt") + _bdot(d_w, bk, "nt")
        d_m = -jnp.where(ri > ci, _dot3(_dot3(tinv, d_tinv, "tn"), tinv, "nt"), 0.0)

        d_kk = d_m * b * gam
        d_gam = d_m * b * kk + d_a * qk
        d_qk = d_a * gam
        dq_ref[...] = _unchunk(_bdot(d_qk, k, "nn") + dq_dec * e_g)
        dk_ref[...] = _unchunk(_bdot(d_qk, q, "tn") + _bdot(d_kk, k, "nn") + _bdot(d_kk, k, "tn")
                               + dk_dec * dec + dbk * (b * e_g))
        dv_ref[...] = _unchunk(dbv * b)
        d_b = _unchunk(jnp.sum(d_m * kk * gam, axis=-1, keepdims=True) + jnp.sum(dbv * v, axis=-1, keepdims=True)
                       + jnp.sum(dbk * k, axis=-1, keepdims=True) * e_g)

        xg = d_gam * gam
        kdk = jnp.sum(dk_dec * (k * dec), axis=-1, keepdims=True)
        d_gc = (jnp.sum(xg, axis=-1, keepdims=True) - _row_to_col(jnp.sum(xg, axis=-2, keepdims=True), eye)
                + jnp.sum(dq_dec * (q * e_g), axis=-1, keepdims=True) - kdk
                + jnp.sum(dbk * bk, axis=-1, keepdims=True))
        d_last_total = dl_ref[0][:, 0:1, 0:1] + jnp.sum(kdk, axis=-2, keepdims=True)
        d_g = _unchunk(d_gc + jnp.where(rcol == DN_CHUNK - 1, d_last_total, 0.0))

        @pl.when(h == 0)
        def _():
            db_ref[...] = jnp.zeros_like(db_ref)
            dg_ref[...] = jnp.zeros_like(dg_ref)

        lane = lax.broadcasted_iota(jnp.int32, db_ref.shape, 1)
        db_ref[...] += jnp.where(lane == h, d_b, 0.0)
        dg_ref[...] += jnp.where(lane == h + N_HEADS, d_g, 0.0)

    head, lanes, hm, sq, tile = _dn_specs(t, rows_blk)
    return pl.pallas_call(
        body, name="dn_intra_bwd", grid=(t // rows_blk, N_HEADS),
        in_specs=[head(0), head(N_HEADS), head(2 * N_HEADS), lanes, lanes, sq] + [hm] * 6 + [tile],
        out_specs=[head(0), head(0), head(0), lanes, lanes],
        out_shape=[jax.ShapeDtypeStruct((t, D_MODEL), F32)] * 3 + [jax.ShapeDtypeStruct((t, 128), F32)] * 2,
        compiler_params=_cp(),
    )(qkv, qkv, qkv, beta_t, g_t, tinv_all, vn, do, dvn, dkd, dqd, dw, dl)


def _dn_post_fwd(o, proj, gn):
    t = o.shape[1]

    def body(o_ref, z_ref, g_ref, out_ref):
        ov, z = o_ref[...], z_ref[...]
        r = lax.rsqrt(jnp.mean(ov * ov, axis=-1, keepdims=True) + NORM_EPS)
        out_ref[...] = (((ov * r) * g_ref[...]) * (z * _sigmoid(z))).astype(BF16)

    blk = pl.BlockSpec((t, D_HEAD), lambda h: (0, h))
    return pl.pallas_call(
        body, name="dn_post_fwd", grid=(N_HEADS,),
        in_specs=[pl.BlockSpec((None, t, D_HEAD), lambda h: (h, 0, 0)),
                  pl.BlockSpec((t, D_HEAD), lambda h: (0, C_DNZ // D_HEAD + h)),
                  pl.BlockSpec((1, D_HEAD), lambda h: (0, 0))],
        out_specs=blk, out_shape=jax.ShapeDtypeStruct((t, D_MODEL), BF16), compiler_params=_cp(),
    )(o, proj, gn)


def _dn_post_bwd(o, proj, gn, dout):
    t = o.shape[1]

    def body(o_ref, z_ref, g_ref, d_ref, do_ref, dz_ref, dg_ref):
        @pl.when(pl.program_id(0) == 0)
        def _():
            dg_ref[...] = jnp.zeros_like(dg_ref)

        ov, z, d = o_ref[...], z_ref[...], d_ref[...]
        r = lax.rsqrt(jnp.mean(ov * ov, axis=-1, keepdims=True) + NORM_EPS)
        ohat = ov * r
        s = _sigmoid(z)
        d_on = d * (z * s)
        dz_ref[...] = (d * (ohat * g_ref[...]) * (s * (1.0 + z * (1.0 - s)))).astype(BF16)
        dg_ref[...] += jnp.sum(d_on * ohat, axis=0, keepdims=True)
        dxh = d_on * g_ref[...]
        do_ref[...] = r * (dxh - ohat * jnp.mean(dxh * ohat, axis=-1, keepdims=True))

    blk = pl.BlockSpec((t, D_HEAD), lambda h: (0, h))
    hm = pl.BlockSpec((None, t, D_HEAD), lambda h: (h, 0, 0))
    vec = pl.BlockSpec((1, D_HEAD), lambda h: (0, 0))
    return pl.pallas_call(
        body, name="dn_post_bwd", grid=(N_HEADS,),
        in_specs=[hm, pl.BlockSpec((t, D_HEAD), lambda h: (0, C_DNZ // D_HEAD + h)), vec, blk],
        out_specs=[hm, blk, vec],
        out_shape=[jax.ShapeDtypeStruct((N_HEADS, t, D_HEAD), F32), jax.ShapeDtypeStruct((t, D_MODEL), BF16),
                   jax.ShapeDtypeStruct((1, D_HEAD), F32)], compiler_params=_cp(),
    )(o, proj, gn, dout)


def _sb_fwd(proj):
    t = proj.shape[0]
    qblk = min(SB_QBLOCK, t)
    scale = 1.0 / math.sqrt(D_HEAD)

    hp = SB_HEADS_PER_STEP
    wid = hp * D_HEAD

    def body(q_ref, k_ref, v_ref, z_ref, o_ref, og_ref, l_ref, qb, kb, vb):
        for hh in range(hp):
            hs = slice(hh * D_HEAD, (hh + 1) * D_HEAD)
            qb[hh] = q_ref[:, hs].astype(BF16)
            kb[hh] = k_ref[:, hs].astype(BF16)
            vb[hh] = v_ref[:, hs].astype(BF16)
        ri = lax.broadcasted_iota(jnp.int32, (qblk, SB_BLOCK), 0)
        ci = lax.broadcasted_iota(jnp.int32, (qblk, SB_BLOCK), 1)
        r2 = lax.broadcasted_iota(jnp.int32, (SB_BLOCK, SB_BLOCK), 0)
        c2 = lax.broadcasted_iota(jnp.int32, (SB_BLOCK, SB_BLOCK), 1)
        upper = (r2 > c2).astype(BF16)
        nkb = qblk // SB_BLOCK

        def qblock(i, carry):
            rows = pl.ds(pl.multiple_of(i * qblk, qblk), qblk)
            qi = qb[:, rows, :]

            def tile(j, st, on_diagonal):
                acc, c = st
                cols = pl.ds(pl.multiple_of(j * SB_BLOCK, SB_BLOCK), SB_BLOCK)
                z = _dot(qi, kb[:, cols, :], "nt") * scale
                lb = jnp.minimum(z, 0.0) - jnp.log(1.0 + jnp.exp(-jnp.abs(z)))
                lf = lb - z
                if on_diagonal:
                    mask = (j * SB_BLOCK + ci) < (i * qblk + ri)
                    lf = jnp.where(mask, lf, 0.0)
                att = jnp.exp(lb + (_ones_dot(lf, upper) + c))
                if on_diagonal:
                    att = jnp.where(mask, att, 0.0)
                acc = acc + _dot(att.astype(BF16), vb[:, cols, :], "nn")
                return acc, c + jnp.sum(lf, axis=-1, keepdims=True)

            st = (jnp.zeros((hp, qblk, D_HEAD), F32), jnp.zeros((hp, qblk, 1), F32))
            for d in range(nkb):
                st = tile((i + 1) * nkb - 1 - d, st, True)
            acc, c = lax.fori_loop(0, i * nkb, lambda jj, s: tile(i * nkb - 1 - jj, s, False), st)
            l_ref[:, rows, :] = c
            for hh in range(hp):
                hs = slice(hh * D_HEAD, (hh + 1) * D_HEAD)
                zg = z_ref[rows, hs]
                o_ref[rows, hs] = acc[hh]
                og_ref[rows, hs] = (acc[hh] * (zg * _sigmoid(zg))).astype(BF16)
            return carry

        lax.fori_loop(0, t // qblk, qblock, 0)

    def head(off):
        return pl.BlockSpec((t, wid), lambda h: (0, off // wid + h))

    out = pl.BlockSpec((t, wid), lambda h: (0, h))
    return pl.pallas_call(
        body, name="sb_fwd", grid=(N_HEADS // hp,),
        in_specs=[head(C_SBQ), head(C_SBQ + D_MODEL), head(C_SBQ + 2 * D_MODEL), head(C_SBZ)],
        out_specs=[out, out, pl.BlockSpec((hp, t, 1), lambda h: (h, 0, 0))],
        out_shape=[jax.ShapeDtypeStruct((t, D_MODEL), F32), jax.ShapeDtypeStruct((t, D_MODEL), BF16),
                   jax.ShapeDtypeStruct((N_HEADS, t, 1), F32)],
        scratch_shapes=[pltpu.VMEM((hp, t, D_HEAD), BF16)] * 3, compiler_params=_cp(),
    )(proj, proj, proj, proj)


def _sb_bwd(proj, o, ltot, dog, after=None):
    t = proj.shape[0]
    qblk = min(SB_QBLOCK, t)
    scale = 1.0 / math.sqrt(D_HEAD)

    hp = SB_HEADS_PER_STEP
    wid = hp * D_HEAD

    def body(q_ref, k_ref, v_ref, z_ref, o_ref, l_ref, d_ref, *rest):
        dq_ref, dk_ref, dv_ref, dz_ref, qb, kb, vb, dob, dk_scr, dv_scr = rest[-10:]
        for hh in range(hp):
            hs = slice(hh * D_HEAD, (hh + 1) * D_HEAD)
            qb[hh] = q_ref[:, hs].astype(BF16)
            kb[hh] = k_ref[:, hs].astype(BF16)
            vb[hh] = v_ref[:, hs].astype(BF16)
            zg = z_ref[:, hs]
            sg = _sigmoid(zg)
            dgo = d_ref[:, hs]
            dob[hh] = (dgo * (zg * sg)).astype(BF16)
            dz_ref[:, hs] = (dgo * o_ref[:, hs] * (sg * (1.0 + zg * (1.0 - sg)))).astype(BF16)
        dk_scr[...] = jnp.zeros_like(dk_scr)
        dv_scr[...] = jnp.zeros_like(dv_scr)
        ri = lax.broadcasted_iota(jnp.int32, (qblk, SB_BLOCK), 0)
        ci = lax.broadcasted_iota(jnp.int32, (qblk, SB_BLOCK), 1)
        r2 = lax.broadcasted_iota(jnp.int32, (SB_BLOCK, SB_BLOCK), 0)
        c2 = lax.broadcasted_iota(jnp.int32, (SB_BLOCK, SB_BLOCK), 1)
        upper = (r2 > c2).astype(BF16)
        below = (r2 < c2).astype(BF16)

        def qblock(i, carry):
            rows = pl.ds(pl.multiple_of(i * qblk, qblk), qblk)
            qi = qb[:, rows, :]
            d_o = dob[:, rows, :]
            ltot = l_ref[:, rows, :]

            def tile(j, st, on_diagonal):
                dq, cpre, ce = st
                cols = pl.ds(pl.multiple_of(j * SB_BLOCK, SB_BLOCK), SB_BLOCK)
                kj, vj = kb[:, cols, :], vb[:, cols, :]
                z = _dot(qi, kj, "nt") * scale
                lb = jnp.minimum(z, 0.0) - jnp.log(1.0 + jnp.exp(-jnp.abs(z)))
                lf = lb - z
                if on_diagonal:
                    mask = (j * SB_BLOCK + ci) < (i * qblk + ri)
                    lf = jnp.where(mask, lf, 0.0)
                tile_sum = jnp.sum(lf, axis=-1, keepdims=True)
                att = jnp.exp(lb + ((ltot - cpre - tile_sum) + _ones_dot(lf, upper)))
                if on_diagonal:
                    att = jnp.where(mask, att, 0.0)
                e = _dot(d_o, vj, "nt") * att
                dlf = ce + _ones_dot(e, below)
                dzz = e - (e + dlf) * jnp.exp(lb)
                if on_diagonal:
                    dzz = jnp.where(mask, dzz, 0.0)
                dzz = dzz.astype(BF16)
                dq = dq + _dot(dzz, kj, "nn")
                dk_scr[:, cols, :] += _dot(dzz, qi, "tn")
                dv_scr[:, cols, :] += _dot(att.astype(BF16), d_o, "tn")
                return dq, cpre + tile_sum, ce + jnp.sum(e, axis=-1, keepdims=True)

            nkb = qblk // SB_BLOCK
            zero_col = jnp.zeros((hp, qblk, 1), F32)
            st = lax.fori_loop(0, i * nkb, lambda j, s: tile(j, s, False),
                               (jnp.zeros((hp, qblk, D_HEAD), F32), zero_col, zero_col))
            for d in range(nkb):
                st = tile(i * nkb + d, st, True)
            dq = st[0]
            for hh in range(hp):
                dq_ref[rows, hh * D_HEAD:(hh + 1) * D_HEAD] = (dq[hh] * scale).astype(BF16)
            return carry

        lax.fori_loop(0, t // qblk, qblock, 0)
        for hh in range(hp):
            hs = slice(hh * D_HEAD, (hh + 1) * D_HEAD)
            dk_ref[:, hs] = (dk_scr[hh] * scale).astype(BF16)
            dv_ref[:, hs] = dv_scr[hh].astype(BF16)

    def head(off):
        return pl.BlockSpec((t, wid), lambda h: (0, off // wid + h))

    extra_specs, extra = [], []
    if after is not None:
        extra_specs, extra = [pl.BlockSpec(after.shape, lambda h: (0, 0))], [after]
    return pl.pallas_call(
        body, name="sb_bwd", grid=(N_HEADS // hp,),
        in_specs=[head(C_SBQ), head(C_SBQ + D_MODEL), head(C_SBQ + 2 * D_MODEL), head(C_SBZ), head(0),
                  pl.BlockSpec((hp, t, 1), lambda h: (h, 0, 0)), head(0)] + extra_specs,
        out_specs=[head(0)] * 4, out_shape=[jax.ShapeDtypeStruct((t, D_MODEL), BF16)] * 4,
        scratch_shapes=[pltpu.VMEM((hp, t, D_HEAD), BF16)] * 4 + [pltpu.VMEM((hp, t, D_HEAD), F32)] * 2,
        compiler_params=_cp(),
    )(proj, proj, proj, proj, o, ltot, dog, *extra)


def _mem_fwd(proj, mkv):
    t = proj.shape[0]
    tq = _pick(t, (512, 256))
    m_len = mkv.shape[0]
    scale = 1.0 / math.sqrt(MEM_DH)

    def body(q_ref, z_ref, kv_ref, o_ref, og_ref):
        q = q_ref[...]
        mk = kv_ref[:, :MEM_W].astype(BF16)
        mv = kv_ref[:, MEM_W:].astype(BF16)
        lane = lax.broadcasted_iota(jnp.int32, q.shape, 1) >> 6
        o = jnp.zeros(q.shape, F32)
        for h in range(MEM_HEADS):
            s = _bdot(jnp.where(lane == h, q, 0.0), mk, "nt") * scale
            p = jnp.exp(s - jnp.max(s, axis=-1, keepdims=True))
            p = p / jnp.sum(p, axis=-1, keepdims=True)
            o = o + jnp.where(lane == h, _bdot(p, mv, "nn"), 0.0)
        z = z_ref[...]
        o_ref[...] = o
        og_ref[...] = (o * (z * _sigmoid(z))).astype(BF16)

    out = pl.BlockSpec((tq, MEM_W), lambda i: (i, 0))
    return pl.pallas_call(
        body, name="mem_fwd", grid=(t // tq,),
        in_specs=[pl.BlockSpec((tq, MEM_W), lambda i: (i, C_MQ // MEM_W)),
                  pl.BlockSpec((tq, MEM_W), lambda i: (i, C_MZ // MEM_W)),
                  pl.BlockSpec((m_len, 2 * MEM_W), lambda i: (0, 0))],
        out_specs=[out, out],
        out_shape=[jax.ShapeDtypeStruct((t, MEM_W), F32), jax.ShapeDtypeStruct((t, MEM_W), BF16)],
        compiler_params=_cp(),
    )(proj, proj, mkv)


def _mem_bwd(proj, mkv, o, dog):
    t = proj.shape[0]
    tq = _pick(t, (512, 256))
    m_len = mkv.shape[0]
    scale = 1.0 / math.sqrt(MEM_DH)

    def body(q_ref, z_ref, kv_ref, o_ref, d_ref, dq_ref, dz_ref, dkv_ref):
        @pl.when(pl.program_id(0) == 0)
        def _():
            dkv_ref[...] = jnp.zeros_like(dkv_ref)

        q = q_ref[...]
        z = z_ref[...]
        sg = _sigmoid(z)
        dgo = d_ref[...]
        d_o = dgo * (z * sg)
        dz_ref[...] = (dgo * o_ref[...] * (sg * (1.0 + z * (1.0 - sg)))).astype(BF16)
        mk = kv_ref[:, :MEM_W].astype(BF16)
        mv = kv_ref[:, MEM_W:].astype(BF16)
        lane = lax.broadcasted_iota(jnp.int32, q.shape, 1) >> 6
        klane = lax.broadcasted_iota(jnp.int32, (m_len, MEM_W), 1) >> 6
        dq = jnp.zeros(q.shape, F32)
        dmk = jnp.zeros((m_len, MEM_W), F32)
        dmv = jnp.zeros((m_len, MEM_W), F32)
        for h in range(MEM_HEADS):
            qh = jnp.where(lane == h, q, 0.0)
            doh = jnp.where(lane == h, d_o, 0.0)
            s = _bdot(qh, mk, "nt") * scale
            p = jnp.exp(s - jnp.max(s, axis=-1, keepdims=True))
            p = p / jnp.sum(p, axis=-1, keepdims=True)
            dp = _bdot(doh, mv, "nt")
            ds = p * (dp - jnp.sum(dp * p, axis=-1, keepdims=True)) * scale
            dq = dq + jnp.where(lane == h, _bdot(ds, mk, "nn"), 0.0)
            dmk = dmk + jnp.where(klane == h, _bdot(ds, qh, "tn"), 0.0)
            dmv = dmv + jnp.where(klane == h, _bdot(p, doh, "tn"), 0.0)
        dq_ref[...] = dq.astype(BF16)
        dkv_ref[:, :MEM_W] += dmk
        dkv_ref[:, MEM_W:] += dmv

    blk = pl.BlockSpec((tq, MEM_W), lambda i: (i, 0))
    kv = pl.BlockSpec((m_len, 2 * MEM_W), lambda i: (0, 0))
    return pl.pallas_call(
        body, name="mem_bwd", grid=(t // tq,),
        in_specs=[pl.BlockSpec((tq, MEM_W), lambda i: (i, C_MQ // MEM_W)),
                  pl.BlockSpec((tq, MEM_W), lambda i: (i, C_MZ // MEM_W)), kv, blk, blk],
        out_specs=[blk, blk, kv],
        out_shape=[jax.ShapeDtypeStruct((t, MEM_W), BF16), jax.ShapeDtypeStruct((t, MEM_W), BF16),
                   jax.ShapeDtypeStruct((m_len, 2 * MEM_W), F32)], compiler_params=_cp(),
    )(proj, proj, mkv, o, dog)


_GW = 512


def _merge_fwd(proj, y_dn, y_sb, y_m):
    t = proj.shape[0]
    tb = _pick(t, (256,))
    nc = D_MODEL // _GW

    def body(g1, g2, g3, y1, y2, y3, out_ref):
        out_ref[...] = (_sigmoid(g1[...]) * y1[...] + _sigmoid(g2[...]) * y2[...] + _sigmoid(g3[...]) * y3[...]).astype(BF16)

    def gate(kb):
        return pl.BlockSpec((tb, _GW), lambda i, c: (i, C_GATES // _GW + kb * nc + c))

    blk = pl.BlockSpec((tb, _GW), lambda i, c: (i, c))
    return pl.pallas_call(
        body, name="merge_fwd", grid=(t // tb, nc), in_specs=[gate(0), gate(1), gate(2), blk, blk, blk],
        out_specs=blk, out_shape=jax.ShapeDtypeStruct((t, D_MODEL), BF16), compiler_params=_cp(),
    )(proj, proj, proj, y_dn, y_sb, y_m)


def _merge_bwd(proj, y_dn, y_sb, y_m, dm):
    t = proj.shape[0]
    tb = _pick(t, (256,))
    nc = D_MODEL // _GW

    def body(g1, g2, g3, y1, y2, y3, dm_ref, d1, d2, d3, dg1, dg2, dg3):
        d = dm_ref[...]
        for g, y, dy, dg in ((g1, y1, d1, dg1), (g2, y2, d2, dg2), (g3, y3, d3, dg3)):
            s = _sigmoid(g[...])
            dy[...] = (d * s).astype(BF16)
            dg[...] = (d * y[...] * (s * (1.0 - s))).astype(BF16)

    def gate(kb):
        return pl.BlockSpec((tb, _GW), lambda i, c: (i, C_GATES // _GW + kb * nc + c))

    blk = pl.BlockSpec((tb, _GW), lambda i, c: (i, c))
    act = jax.ShapeDtypeStruct((t, D_MODEL), BF16)
    return pl.pallas_call(
        body, name="merge_bwd", grid=(t // tb, nc), in_specs=[gate(0), gate(1), gate(2), blk, blk, blk, blk],
        out_specs=[blk] * 6, out_shape=[act] * 6, compiler_params=_cp(),
    )(proj, proj, proj, y_dn, y_sb, y_m, dm)


def _final_loss(x, mo, g, tgt):
    t, d = x.shape
    tb = _pick(t, (256,))

    def body(x_ref, mo_ref, g_ref, t_ref, do_ref, dob_ref, loss_ref, dg_ref):
        @pl.when(pl.program_id(0) == 0)
        def _():
            loss_ref[...] = jnp.zeros_like(loss_ref)
            dg_ref[...] = jnp.zeros_like(dg_ref)

        out = x_ref[...] + mo_ref[...]
        r = lax.rsqrt(jnp.mean(out * out, axis=-1, keepdims=True) + NORM_EPS)
        xhat = out * r
        gv = g_ref[...]
        err = xhat * gv - t_ref[...]
        per_tok = jnp.mean(err * err, axis=-1, keepdims=True)
        loss_ref[...] += 0.5 * jnp.sum(per_tok, axis=0, keepdims=True)
        dy = err * (1.0 / d)
        dg_ref[...] += jnp.sum(dy * xhat, axis=0, keepdims=True)
        dxh = dy * gv
        dout = r * (dxh - xhat * jnp.mean(dxh * xhat, axis=-1, keepdims=True))
        do_ref[...] = dout
        dob_ref[...] = dout.astype(BF16)

    row = pl.BlockSpec((tb, d), lambda i: (i, 0))
    vec = pl.BlockSpec((1, d), lambda i: (0, 0))
    return pl.pallas_call(
        body, name="final_loss", grid=(t // tb,), in_specs=[row, row, vec, row],
        out_specs=[row, row, pl.BlockSpec((1, 128), lambda i: (0, 0)), vec],
        out_shape=[jax.ShapeDtypeStruct((t, d), F32), jax.ShapeDtypeStruct((t, d), BF16),
                   jax.ShapeDtypeStruct((1, 128), F32), jax.ShapeDtypeStruct((1, d), F32)],
        compiler_params=_cp(),
    )(x, mo, g, tgt)


def _cast_bf16(a, name):
    r, c = a.shape
    tb = _pick(r, (128, 496, 240))

    def body(a_ref, o_ref):
        o_ref[...] = a_ref[...].astype(BF16)

    blk = pl.BlockSpec((tb, c), lambda i: (i, 0))
    return pl.pallas_call(body, name=name, grid=(r // tb,), in_specs=[blk], out_specs=blk,
                          out_shape=jax.ShapeDtypeStruct((r, c), BF16), compiler_params=_cp())(a)


WIN_START = (0, 23, 45, 68)
_S1_LO, _S1_HI = 1148, 1164
_S1_BA_POS = SHARD_PAD - 128


def _to_window(x, s):
    if s == 0:
        return x
    if s in (2, 3):
        return pltpu.roll(x, 120 if s == 2 else 124, 1)
    pos = lax.broadcasted_iota(jnp.int32, x.shape, 1)
    head = pltpu.roll(x, 4, 1)
    tail = pltpu.roll(x, SHARD_PAD - 12, 1)
    ba = jnp.where(pos < _S1_BA_POS + (_S1_HI - _S1_LO), pltpu.roll(x, _S1_BA_POS - _S1_LO, 1), 0.0)
    return jnp.where(pos < _S1_LO + 4, head, jnp.where(pos < _S1_BA_POS, tail, ba))


def _from_window(g, s):
    if s == 0:
        return g
    if s in (2, 3):
        return pltpu.roll(g, SHARD_PAD - (120 if s == 2 else 124), 1)
    col = lax.broadcasted_iota(jnp.int32, g.shape, 1)
    head = pltpu.roll(g, SHARD_PAD - 4, 1)
    tail = pltpu.roll(g, 12, 1)
    ba = pltpu.roll(g, SHARD_PAD - (_S1_BA_POS - _S1_LO), 1)
    return jnp.where(col < _S1_LO, head, jnp.where(col < _S1_HI, ba, tail))


def _cast_to_window(w, shard, name):
    r, c = w.shape
    tb = _pick(r, (128,))

    def body(s_ref, w_ref, o_ref, pad_scr):
        pad_scr[...] = jnp.zeros_like(pad_scr)
        pad_scr[:, :c] = w_ref[...]
        x = pad_scr[...]
        for s in range(N_SHARD):
            @pl.when(s_ref[0] == s)
            def _():
                o_ref[...] = _to_window(x, s).astype(BF16)

    return pl.pallas_call(
        body, name=name,
        grid_spec=pltpu.PrefetchScalarGridSpec(
            num_scalar_prefetch=1, grid=(r // tb,),
            in_specs=[pl.BlockSpec((tb, c), lambda i, s: (i, 0))],
            out_specs=pl.BlockSpec((tb, SHARD_PAD), lambda i, s: (i, 0)),
            scratch_shapes=[pltpu.VMEM((tb, SHARD_PAD), F32)]),
        out_shape=jax.ShapeDtypeStruct((r, SHARD_PAD), BF16), compiler_params=_cp(),
    )(shard, w)


def _pair_add(g, recv, c_idx, name):
    n, r, c = g.shape
    half = r // 2
    tb = _pick(half, (128, 240))
    nb = half // tb

    def body(c_ref, g_ref, r_ref, o_ref):
        o_ref[...] = (g_ref[...].astype(F32) + r_ref[...].astype(F32)).astype(BF16)

    blk = pl.BlockSpec((n, tb, c), lambda i, c_ref: (0, i, 0))
    return pl.pallas_call(
        body, name=name,
        grid_spec=pltpu.PrefetchScalarGridSpec(
            num_scalar_prefetch=1, grid=(nb,),
            in_specs=[pl.BlockSpec((n, tb, c), lambda i, c_ref: (0, c_ref[0] * nb + i, 0)), blk], out_specs=blk),
        out_shape=jax.ShapeDtypeStruct((n, half, c), BF16), compiler_params=_cp(),
    )(c_idx, g, recv)


def _chip_sum(parts, by_chip, place, name):
    n, h, c = parts.shape
    tb = _pick(h, (128, 240))
    nb = h // tb

    def body(p_ref, mine_ref, *rest):
        others, o_ref = rest[:n], rest[n]
        me = jnp.zeros((tb, c), jnp.int32) + p_ref[0]
        acc = None
        for q in range(n):
            term = jnp.where(me == q, mine_ref[...], others[q][...]).astype(F32)
            acc = term if acc is None else acc + term
        o_ref[...] = acc

    def other(q):
        return pl.BlockSpec((None, tb, c), lambda i, p: (jnp.where(p[0] == q, (q + 1) % n, q), i, 0))

    return pl.pallas_call(
        body, name=name,
        grid_spec=pltpu.PrefetchScalarGridSpec(
            num_scalar_prefetch=1, grid=(nb,),
            in_specs=[pl.BlockSpec((None, tb, c), lambda i, p: (p[0], i, 0))] + [other(q) for q in range(n)],
            out_specs=pl.BlockSpec((tb, c), lambda i, p: (p[1] * nb + i, 0))),
        out_shape=jax.ShapeDtypeStruct((2 * h, c), F32), compiler_params=_cp(),
    )(place, parts, *([by_chip] * n))


def _adamw_math(w, g, m, v):
    m = ADAM_B1 * m + (1.0 - ADAM_B1) * g
    v = ADAM_B2 * v + (1.0 - ADAM_B2) * (g * g)
    m_hat = m / (1.0 - ADAM_B1 ** ADAM_STEP)
    v_hat = v / (1.0 - ADAM_B2 ** ADAM_STEP)
    delta = -ADAM_LR * (m_hat / (jnp.sqrt(v_hat) + ADAM_EPS) + ADAM_WD * w)
    return delta, m, v


def _adamw(w, g, m, v, name):
    r, c = w.shape
    tb = _pick(r, (128, 496, 240))

    def body(w_ref, g_ref, m_ref, v_ref, go_ref, d_ref, mo_ref, vo_ref):
        gv = g_ref[...]
        d, mn, vn = _adamw_math(w_ref[...], gv, m_ref[...], v_ref[...])
        go_ref[...] = gv
        d_ref[...] = d
        mo_ref[...] = mn
        vo_ref[...] = vn

    blk = pl.BlockSpec((tb, c), lambda i: (i, 0))
    return pl.pallas_call(
        body, name=name, grid=(r // tb,), in_specs=[blk] * 4, out_specs=[blk] * 4,
        out_shape=[jax.ShapeDtypeStruct((r, c), F32)] * 4, compiler_params=_cp(),
    )(w, g, m, v)


def _adamw_window(w, g_win, m, v, shard, name):
    r, c = w.shape
    tb = _pick(r, (128,))

    def body(s_ref, w_ref, g_ref, m_ref, v_ref, go_ref, d_ref, mo_ref, vo_ref, g_scr):
        gw = g_ref[...]
        for s in range(N_SHARD):
            @pl.when(s_ref[0] == s)
            def _():
                g_scr[...] = _from_window(gw, s)

        gv = g_scr[:, :c]
        d, mn, vn = _adamw_math(w_ref[...], gv, m_ref[...], v_ref[...])
        go_ref[...] = gv
        d_ref[...] = d
        mo_ref[...] = mn
        vo_ref[...] = vn

    blk = pl.BlockSpec((tb, c), lambda i, s: (i, 0))
    return pl.pallas_call(
        body, name=name,
        grid_spec=pltpu.PrefetchScalarGridSpec(
            num_scalar_prefetch=1, grid=(r // tb,),
            in_specs=[blk, pl.BlockSpec((tb, SHARD_PAD), lambda i, s: (i, 0)), blk, blk], out_specs=[blk] * 4,
            scratch_shapes=[pltpu.VMEM((tb, SHARD_PAD), F32)]),
        out_shape=[jax.ShapeDtypeStruct((r, c), F32)] * 4, compiler_params=_cp(),
    )(shard, w, g_win, m, v)


def _small_update(gathered, w, m, v):
    def body(p_ref, w_ref, m_ref, v_ref, g_ref, d_ref, mo_ref, vo_ref):
        g = p_ref[0]
        for i in range(1, N_DEV):
            g = g + p_ref[i]
        d, mn, vn = _adamw_math(w_ref[...], g, m_ref[...], v_ref[...])
        g_ref[...] = g
        d_ref[...] = d
        mo_ref[...] = mn
        vo_ref[...] = vn

    full = pl.BlockSpec((S_ROWS, 128), lambda i: (0, 0))
    return pl.pallas_call(
        body, name="small_update", grid=(1,),
        in_specs=[pl.BlockSpec((N_DEV, S_ROWS, 128), lambda i: (0, 0, 0)), full, full, full], out_specs=[full] * 4,
        out_shape=[jax.ShapeDtypeStruct((S_ROWS, 128), F32)] * 4, compiler_params=_cp(),
    )(gathered, w, m, v)


_ANY = pl.BlockSpec(memory_space=pl.ANY)


def _place():
    x, y, c = lax.axis_index("x"), lax.axis_index("y"), lax.axis_index("c")
    chips = [(1 - x, y), (x, 1 - y), (1 - x, 1 - y)]
    return x, y, c, chips


def _gather_shards(arrs):
    n = len(arrs)

    def body(*refs):
        ins, outs = refs[:n], refs[n:2 * n]
        send_sems, recv_sems, local_sems = refs[2 * n:2 * n + 3]
        bufs = refs[2 * n + 3:]
        x, y, c, chips = _place()
        me = 2 * x + y
        sibling = (x, y, 1 - c)
        sends = []
        for a in range(n):
            half = ins[a].shape[0] // 2
            mine = pl.ds(pl.multiple_of(c * half, 16), half)
            for j, (qx, qy) in enumerate(chips):
                cp = pltpu.make_async_remote_copy(
                    src_ref=ins[a].at[mine], dst_ref=outs[a].at[me, mine],
                    send_sem=send_sems.at[6 * a + j], recv_sem=recv_sems.at[6 * a + j],
                    device_id=(qx, qy, c), device_id_type=MESH)
                cp.start()
                sends.append(cp)
        for a in range(n):
            step = bufs[a].shape[0]
            for r0 in range(0, ins[a].shape[0], step):
                rows = pl.ds(r0, step)
                load = pltpu.make_async_copy(ins[a].at[rows], bufs[a], local_sems.at[2 * a])
                load.start()
                load.wait()
                store = pltpu.make_async_copy(bufs[a], outs[a].at[me, rows], local_sems.at[2 * a + 1])
                store.start()
                store.wait()
        for a in range(n):
            half = ins[a].shape[0] // 2
            mine = pl.ds(pl.multiple_of(c * half, 16), half)
            for j, (qx, qy) in enumerate(chips):
                q = 2 * qx + qy
                landed = outs[a].at[q, mine]
                pltpu.make_async_remote_copy(
                    src_ref=landed, dst_ref=landed, send_sem=send_sems.at[6 * a + j], recv_sem=recv_sems.at[6 * a + j],
                    device_id=(qx, qy, c), device_id_type=MESH).wait_recv()
                fw = pltpu.make_async_remote_copy(
                    src_ref=landed, dst_ref=landed, send_sem=send_sems.at[6 * a + 3 + j],
                    recv_sem=recv_sems.at[6 * a + 3 + j], device_id=sibling, device_id_type=MESH)
                fw.start()
                sends.append(fw)
        for a in range(n):
            half = ins[a].shape[0] // 2
            theirs = pl.ds(pl.multiple_of((1 - c) * half, 16), half)
            for j, (qx, qy) in enumerate(chips):
                q = 2 * qx + qy
                dst = outs[a].at[q, theirs]
                pltpu.make_async_remote_copy(
                    src_ref=dst, dst_ref=dst, send_sem=send_sems.at[6 * a + 3 + j], recv_sem=recv_sems.at[6 * a + 3 + j],
                    device_id=sibling, device_id_type=MESH).wait_recv()
        for cp in sends:
            cp.wait_send()

    return pl.pallas_call(
        body, name="gather_shards", in_specs=[_ANY] * n, out_specs=[_ANY] * n,
        out_shape=[jax.ShapeDtypeStruct((N_SHARD,) + a.shape, a.dtype) for a in arrs],
        scratch_shapes=[pltpu.SemaphoreType.DMA((6 * n,)), pltpu.SemaphoreType.DMA((6 * n,)),
                        pltpu.SemaphoreType.DMA((2 * n,))]
        + [pltpu.VMEM((_pick(a.shape[0], (256, 496)), a.shape[1]), a.dtype) for a in arrs],
        compiler_params=pltpu.CompilerParams(has_side_effects=True, vmem_limit_bytes=VMEM_LIMIT),
    )(*arrs)


def _pair_reduce_send(grads, tag):
    n = len(grads)

    def body(*refs):
        ins, outs = refs[:n], refs[n:2 * n]
        send_sems, recv_sems = refs[2 * n:]
        x, y, c, _ = _place()
        sibling = (x, y, 1 - c)
        cps = []
        for a in range(n):
            half = ins[a].shape[1] // 2
            theirs = pl.ds(pl.multiple_of((1 - c) * half, 8), half)
            cp = pltpu.make_async_remote_copy(
                src_ref=ins[a].at[:, theirs], dst_ref=outs[a], send_sem=send_sems.at[a], recv_sem=recv_sems.at[a],
                device_id=sibling, device_id_type=MESH)
            cp.start()
            cps.append(cp)
        for cp in cps:
            cp.wait()

    return pl.pallas_call(
        body, name="pair_reduce_send_" + tag, in_specs=[_ANY] * n, out_specs=[_ANY] * n,
        out_shape=[jax.ShapeDtypeStruct((g.shape[0], g.shape[1] // 2, g.shape[2]), g.dtype) for g in grads],
        scratch_shapes=[pltpu.SemaphoreType.DMA((n,)), pltpu.SemaphoreType.DMA((n,))],
        compiler_params=pltpu.CompilerParams(has_side_effects=True),
    )(*grads)


_HBM = pl.BlockSpec(memory_space=pltpu.HBM)
_SEM = pl.BlockSpec(memory_space=pltpu.SEMAPHORE)
_DATAFLOW = pltpu.SideEffectType.DATAFLOW_SIDE_EFFECTING


def _chip_exchange_copies(ins, lands, send_sems, recv_sems):
    x, y, c, chips = _place()
    me = 2 * x + y
    cps = []
    for a in range(len(ins)):
        for j, (qx, qy) in enumerate(chips):
            cps.append(pltpu.make_async_remote_copy(
                src_ref=ins[a].at[2 * qx + qy], dst_ref=lands[a].at[me], send_sem=send_sems.at[3 * a + j],
                recv_sem=recv_sems.at[3 * a + j], device_id=(qx, qy, c), device_id_type=MESH))
    return cps


def _chip_exchange_start(parts, tag):
    n = len(parts)

    def body(*refs):
        ins, lands = refs[:n], refs[n:2 * n]
        send_sems, recv_sems = refs[2 * n:2 * n + 2]
        token = refs[4 * n + 2]
        for cp in _chip_exchange_copies(ins, lands, send_sems, recv_sems):
            cp.start()
        token[...] = jnp.zeros_like(token)

    hbm = [pltpu.HBM(p.shape, p.dtype) for p in parts]
    lands = [pltpu.with_memory_space_constraint(lax.empty(p.shape, p.dtype), pltpu.HBM) for p in parts]
    res = pl.pallas_call(
        body, name="chip_exchange_start_" + tag,
        out_shape=(pltpu.SemaphoreType.DMA((3 * n,)), pltpu.SemaphoreType.DMA((3 * n,)), *hbm, *hbm,
                   jax.ShapeDtypeStruct((8, 128), F32)),
        in_specs=[_HBM] * (2 * n), out_specs=(_SEM, _SEM, *([_HBM] * (2 * n)), pl.BlockSpec(memory_space=pltpu.VMEM)),
        input_output_aliases={a: 2 + a for a in range(2 * n)},
        compiler_params=pltpu.CompilerParams(has_side_effects=_DATAFLOW),
    )(*[pltpu.with_memory_space_constraint(p, pltpu.HBM) for p in parts], *lands)
    return res[0], res[1], res[2:2 + n], res[2 + n:2 + 2 * n], res[2 + 2 * n]


def _chip_exchange_wait(send_sems, recv_sems, parts, lands, after, tag):
    n = len(parts)

    def body(*refs):
        ins, land_refs = refs[:n], refs[n:2 * n]
        s_sems, r_sems = refs[2 * n:2 * n + 2]
        for cp in _chip_exchange_copies(ins, land_refs, s_sems, r_sems):
            cp.wait_send()
            cp.wait_recv()

    hbm = [pltpu.HBM(p.shape, p.dtype) for p in parts]
    res = pl.pallas_call(
        body, name="chip_exchange_wait_" + tag, out_shape=(*hbm, *hbm),
        in_specs=[_HBM] * (2 * n) + [_SEM, _SEM, _ANY], out_specs=tuple([_HBM] * (2 * n)),
        input_output_aliases={a: a for a in range(2 * n)},
        compiler_params=pltpu.CompilerParams(has_side_effects=_DATAFLOW),
    )(*parts, *lands, send_sems, recv_sems, after)
    return res[:n], res[n:]


def _shard_gather_copies(src, land, send_sems, recv_sems):
    x, y, c, chips = _place()
    me = 2 * x + y
    return [pltpu.make_async_remote_copy(
        src_ref=src, dst_ref=land.at[me], send_sem=send_sems.at[j], recv_sem=recv_sems.at[j],
        device_id=(qx, qy, c), device_id_type=MESH) for j, (qx, qy) in enumerate(chips)]


def _shard_gather_start(shard_arr, after):
    def body(src, land, after_ref, send_sems, recv_sems, src_thru, land_thru, token):
        for cp in _shard_gather_copies(src, land, send_sems, recv_sems):
            cp.start()
        token[...] = jnp.zeros_like(token)

    land_shape = (N_SHARD,) + shard_arr.shape
    land = pltpu.with_memory_space_constraint(lax.empty(land_shape, shard_arr.dtype), pltpu.HBM)
    return pl.pallas_call(
        body, name="shard_gather_start",
        out_shape=(pltpu.SemaphoreType.DMA((N_SHARD - 1,)), pltpu.SemaphoreType.DMA((N_SHARD - 1,)),
                   pltpu.HBM(shard_arr.shape, shard_arr.dtype), pltpu.HBM(land_shape, shard_arr.dtype),
                   jax.ShapeDtypeStruct((8, 128), F32)),
        in_specs=[_HBM, _HBM, _ANY], out_specs=(_SEM, _SEM, _HBM, _HBM, pl.BlockSpec(memory_space=pltpu.VMEM)),
        input_output_aliases={0: 2, 1: 3},
        compiler_params=pltpu.CompilerParams(has_side_effects=_DATAFLOW),
    )(pltpu.with_memory_space_constraint(shard_arr, pltpu.HBM), land, after)


def _shard_gather_wait(send_sems, recv_sems, shard_arr, land, after):
    def body(src, land_ref, s_sems, r_sems, after_ref, src_out, land_out):
        for cp in _shard_gather_copies(src, land_ref, s_sems, r_sems):
            cp.wait_send()
            cp.wait_recv()

    return pl.pallas_call(
        body, name="shard_gather_wait",
        out_shape=(pltpu.HBM(shard_arr.shape, shard_arr.dtype), pltpu.HBM(land.shape, land.dtype)),
        in_specs=[_HBM, _HBM, _SEM, _SEM, _ANY], out_specs=(_HBM, _HBM), input_output_aliases={0: 0, 1: 1},
        compiler_params=pltpu.CompilerParams(has_side_effects=_DATAFLOW),
    )(shard_arr, land, send_sems, recv_sems, after)


def _pair_allgather(fulls, tag):
    n = len(fulls)

    def body(*refs):
        outs = refs[n:2 * n]
        send_sems, recv_sems = refs[2 * n:]
        x, y, c, _ = _place()
        sibling = (x, y, 1 - c)
        cps = []
        for a in range(n):
            half = outs[a].shape[0] // 2
            mine = outs[a].at[pl.ds(pl.multiple_of(c * half, 8), half)]
            cp = pltpu.make_async_remote_copy(
                src_ref=mine, dst_ref=mine, send_sem=send_sems.at[a], recv_sem=recv_sems.at[a],
                device_id=sibling, device_id_type=MESH)
            cp.start()
            cps.append(cp)
        for a in range(n):
            half = outs[a].shape[0] // 2
            theirs = outs[a].at[pl.ds(pl.multiple_of((1 - c) * half, 8), half)]
            pltpu.make_async_remote_copy(
                src_ref=theirs, dst_ref=theirs, send_sem=send_sems.at[a], recv_sem=recv_sems.at[a],
                device_id=sibling, device_id_type=MESH).wait_recv()
        for cp in cps:
            cp.wait_send()

    return pl.pallas_call(
        body, name="pair_allgather_" + tag, in_specs=[_ANY] * n, out_specs=[_ANY] * n,
        out_shape=[jax.ShapeDtypeStruct(f.shape, f.dtype) for f in fulls],
        input_output_aliases={a: a for a in range(n)},
        scratch_shapes=[pltpu.SemaphoreType.DMA((n,)), pltpu.SemaphoreType.DMA((n,))],
        compiler_params=pltpu.CompilerParams(has_side_effects=True),
    )(*fulls)


def _allgather_small(slab, after):
    def body(s_ref, after_ref, out_ref, send_sems, recv_sems):
        x, y, c, _ = _place()
        me = 4 * x + 2 * y + c
        out_ref[me] = s_ref[...]
        cps = []
        for mask in range(1, N_DEV):
            peer = (x ^ (mask >> 2), y ^ ((mask >> 1) & 1), c ^ (mask & 1))
            cp = pltpu.make_async_remote_copy(
                src_ref=s_ref, dst_ref=out_ref.at[me], send_sem=send_sems.at[mask - 1], recv_sem=recv_sems.at[mask - 1],
                device_id=peer, device_id_type=MESH)
            cp.start()
            cps.append(cp)
        for mask in range(1, N_DEV):
            peer = (x ^ (mask >> 2), y ^ ((mask >> 1) & 1), c ^ (mask & 1))
            dst = out_ref.at[4 * peer[0] + 2 * peer[1] + peer[2]]
            pltpu.make_async_remote_copy(
                src_ref=dst, dst_ref=dst, send_sem=send_sems.at[mask - 1], recv_sem=recv_sems.at[mask - 1],
                device_id=peer, device_id_type=MESH).wait_recv()
        for cp in cps:
            cp.wait_send()

    vm = pl.BlockSpec(memory_space=pltpu.VMEM)
    return pl.pallas_call(
        body, name="allgather_small", in_specs=[vm, _ANY], out_specs=vm,
        out_shape=jax.ShapeDtypeStruct((N_DEV,) + slab.shape, slab.dtype),
        scratch_shapes=[pltpu.SemaphoreType.DMA((N_DEV - 1,)), pltpu.SemaphoreType.DMA((N_DEV - 1,))],
        compiler_params=pltpu.CompilerParams(has_side_effects=True),
    )(slab, after)


def _pack_b(w_mem_kv, w_br_dn, w_br_sb, w_br_mem, w_out):
    return jnp.concatenate([w_mem_kv.reshape(128, D_MODEL), w_br_dn, w_br_sb, w_br_mem.reshape(64, D_MODEL), w_out],
                           axis=0)


def _conv_slab(conv_w):
    return jnp.pad(conv_w.reshape(3, D_MODEL), ((0, 29), (0, 0)))


def _unpack_b(slab):
    return (slab[B_MEMKV:B_BRDN].reshape(1, 256, 512), slab[B_BRDN:B_BRSB].reshape(1, 256, D_MODEL),
            slab[B_BRSB:B_BRMEM].reshape(1, 256, D_MODEL), slab[B_BRMEM:B_OUT].reshape(1, 256, 256),
            slab[B_OUT:B_CONV].reshape(1, 256, D_MODEL))


def _conv_rows(conv_full):
    return conv_full.reshape(4 * CONV_BLOCKS, 128)


def _conv_shard_rows(conv_shard, shard):
    own = CONV_BLOCKS // N_SHARD
    blocks = lax.dynamic_update_slice(jnp.zeros((4, CONV_BLOCKS, 128), F32), conv_shard.reshape(4, own, 128),
                                      (0, own * shard, 0))
    return blocks.reshape(4 * CONV_BLOCKS, 128)


def _conv_shard_of(rows, shard):
    own = CONV_BLOCKS // N_SHARD
    blocks = lax.dynamic_slice(rows.reshape(4, CONV_BLOCKS, 128), (0, own * shard, 0), (4, own, 128))
    return blocks.reshape(1, 4, own * 128)


def _pack_small(norm_g, mem_norm_g, final_g, dn_norm_g, a_log, dt_bias, conv_rows, loss=None):
    def row(v):
        v = v.reshape(1, -1).astype(F32)
        return jnp.pad(v, ((0, 0), (0, 128 - v.shape[1])))

    loss_row = row(jnp.zeros((1,), F32) if loss is None else jnp.reshape(loss, (1,)))
    rid = lax.broadcasted_iota(jnp.int32, (8, 128), 0) + S_DNNORM
    tile = jnp.where(rid == S_DNNORM, dn_norm_g.reshape(1, 128), jnp.where(
        rid == S_ALOG, row(a_log), jnp.where(rid == S_DTB, row(dt_bias), jnp.where(rid == S_LOSS, loss_row, 0.0))))
    return jnp.concatenate([norm_g.reshape(8, 128), mem_norm_g.reshape(8, 128), final_g.reshape(8, 128), tile,
                            conv_rows], axis=0)


def _unpack_small(slab, shard):
    return (slab[S_NORM:S_NORM + 8].reshape(1, D_MODEL), slab[S_MEMNORM:S_MEMNORM + 8].reshape(1, D_MODEL),
            slab[S_FINAL:S_FINAL + 8].reshape(D_MODEL), slab[S_DNNORM].reshape(1, 128),
            slab[S_ALOG, :N_HEADS].reshape(1, N_HEADS), slab[S_DTB, :N_HEADS].reshape(1, N_HEADS),
            _conv_shard_of(slab[S_CONV:], shard))


def _windows_to_w_r(win):
    b = 128
    s0, s1, s2, s3 = win[0], win[1], win[2], win[3]
    e1, e2, e3 = WIN_START[1] * b, WIN_START[2] * b, WIN_START[3] * b
    n1, n2 = e2 - e1, e3 - e2
    return jnp.concatenate([
        s0[:, :e1], s0[:, e1:e1 + b] + s1[:, :b],
        s1[:, b:n1], s1[:, n1:n1 + b] + s2[:, :b],
        s2[:, b:n2], s2[:, n2:n2 + b] + s3[:, :b],
        s3[:, b:], s1[:, _S1_BA_POS:], jnp.zeros((win.shape[1], W_R - C_BA - b), win.dtype)], axis=1)


def _dproj_windows(dproj_r):
    b = 128
    pieces = []
    for s in range(N_SHARD):
        lo = WIN_START[s] * b
        if s == 1:
            pieces += [dproj_r[:, lo:lo + _S1_BA_POS], dproj_r[:, C_BA:C_BA + b]]
        else:
            pieces.append(dproj_r[:, lo:lo + SHARD_PAD])
    return jnp.concatenate(pieces, axis=1)


def _local_step(x, mem, tgt, norm_g, mem_norm_g, w_r, w_sh, conv_w, a_log, dt_bias, dn_norm_g, proj_weights, final_g,
                on_early=None, after_gather=None):
    t = x.shape[0]
    final_row = final_g.reshape(1, D_MODEL)
    lanes_8_16 = ((0, 0), (N_HEADS, 128 - 2 * N_HEADS))
    alog_row = jnp.pad(a_log.reshape(1, N_HEADS), lanes_8_16)
    dtb_row = jnp.pad(dt_bias.reshape(1, N_HEADS), lanes_8_16)

    h = _rmsnorm_fwd(x, norm_g, "norm_fwd")
    proj = _mm(h, w_r, "nn", "in_proj", after=after_gather, tm_max=2048)
    qkv = _dn_prep_fwd(proj, conv_w)
    beta_t, g_t = _dn_gate_fwd(proj, alog_row, dtb_row)
    dn_u, dn_w, dn_qd, dn_kd, dn_a, tinv_all, dn_el = _dn_intra_fwd(qkv, beta_t, g_t)
    o_dn, dn_vn, s_all = _dn_scan_fwd(dn_u, dn_w, dn_qd, dn_kd, dn_a, dn_el)
    o_dn_g = _dn_post_fwd(o_dn, proj, dn_norm_g)
    o_sb, o_sb_g, sb_l = _sb_fwd(proj)
    w_mem_kv, w_br_dn, w_br_sb, w_br_mem, w_out = proj_weights(o_sb_g)
    mem_n = _rmsnorm_fwd(mem, mem_norm_g, "mem_norm_fwd")
    mkv = _mm(mem_n, w_mem_kv, "nn", "mem_kv")
    o_m, o_m_g = _mem_fwd(proj, mkv)
    y_dn = _mm(o_dn_g, w_br_dn, "nn", "br_dn", out_dtype=BF16)
    y_sb = _mm(o_sb_g, w_br_sb, "nn", "br_sb", out_dtype=BF16)
    y_m = _mm(o_m_g, w_br_mem, "nn", "br_mem", out_dtype=BF16)
    merged = _merge_fwd(proj, y_dn, y_sb, y_m)
    mo = _mm(merged, w_out, "nn", "out_proj")
    d_out, d_out_b, loss_row, g_final = _final_loss(x, mo, final_row, tgt)

    g_w_out = _mm(merged, d_out_b, "tn", "g_w_out", out_dtype=BF16)
    d_merged = _mm(d_out_b, w_out, "nt", "d_merged")
    dy_dn, dy_sb, dy_m, dg1, dg2, dg3 = _merge_bwd(proj, y_dn, y_sb, y_m, d_merged)
    g_w_br_dn = _mm(o_dn_g, dy_dn, "tn", "g_w_br_dn", out_dtype=BF16)
    g_w_br_sb = _mm(o_sb_g, dy_sb, "tn", "g_w_br_sb", out_dtype=BF16)
    g_w_br_mem = _mm(o_m_g, dy_m, "tn", "g_w_br_mem", out_dtype=BF16)
    d_o_dn_g = _mm(dy_dn, w_br_dn, "nt", "d_o_dn")
    d_o_sb_g = _mm(dy_sb, w_br_sb, "nt", "d_o_sb")
    d_o_m_g = _mm(dy_m, w_br_mem, "nt", "d_o_mem")

    d_mq, d_mz, d_mkv = _mem_bwd(proj, mkv, o_m, d_o_m_g)
    d_mkv_b = _cast_bf16(d_mkv, "cast_dmkv")
    g_w_mem_kv = _mm(mem_n, d_mkv_b, "tn", "g_w_mem_kv", out_dtype=BF16)
    d_mem_n = _mm(d_mkv_b, w_mem_kv, "nt", "d_mem_n")
    _, g_mem_norm = _rmsnorm_bwd(mem, mem_norm_g, d_mem_n, jnp.zeros_like(mem), "mem_norm_bwd")

    early = dict(w_mem_kv=g_w_mem_kv, w_br_dn=g_w_br_dn, w_br_sb=g_w_br_sb, w_br_mem=g_w_br_mem, w_out=g_w_out)
    after_early = on_early(early) if on_early is not None else None

    d_sq, d_sk, d_sv, d_sz = _sb_bwd(proj, o_sb, sb_l, d_o_sb_g, after=after_early)

    d_o_dn, d_dnz, g_dn_norm = _dn_post_bwd(o_dn, proj, dn_norm_g, d_o_dn_g)
    d_vnew, d_kd, d_qd, d_w, d_el = _dn_scan_bwd(dn_w, dn_qd, dn_kd, dn_a, dn_el, dn_vn, s_all, d_o_dn)
    d_qn, d_kn, d_vn, dbeta_t, dg_t = _dn_intra_bwd(qkv, beta_t, g_t, tinv_all, dn_vn, d_o_dn, d_vnew, d_kd, d_qd, d_w, d_el)
    d_conv_in, g_conv = _dn_prep_bwd(proj, conv_w, d_qn, d_kn, d_vn)
    d_ba, g_alog_row, g_dtb_row = _dn_gate_bwd(proj, alog_row, dtb_row, dbeta_t, dg_t)

    dproj_sh = _dproj_windows(
        jnp.concatenate([d_conv_in, d_dnz, d_sq, d_sk, d_sv, d_sz, d_mq, d_mz, dg1, dg2, dg3, d_ba], axis=1))
    g_w_sh = _mm(h, dproj_sh, "tn", "g_w_in", out_dtype=BF16, out_shards=N_SHARD, tn_max=1024)
    def input_grad(after=None):
        dh = _mm(dproj_sh, w_sh, "nt", "d_h", after=after, tm_max=2048, tn_max=1024)
        grad_x, g_norm = _rmsnorm_bwd(x, norm_g, dh, d_out, "norm_bwd")
        small = dict(norm_g=g_norm, mem_norm_g=g_mem_norm, final_g=g_final, dn_norm_g=g_dn_norm,
                     a_log=g_alog_row[:, N_HEADS:2 * N_HEADS], dt_bias=g_dtb_row[:, N_HEADS:2 * N_HEADS],
                     conv_w=g_conv)
        return grad_x, small

    return loss_row[0, 0], early, g_w_sh, input_grad


def _reduce_scatter_start(grads, tag):
    c = lax.axis_index("c")
    core = jnp.reshape(c, (1,)).astype(jnp.int32)
    recv = _pair_reduce_send(grads, tag)
    parts = [_pair_add(g, r, core, "pair_add_" + tag) for g, r in zip(grads, recv)]
    return _chip_exchange_start(parts, tag)


def _reduce_scatter_finish(handle, after, tag):
    send_sems, recv_sems, parts, lands, _ = handle
    x, y, c = lax.axis_index("x"), lax.axis_index("y"), lax.axis_index("c")
    place = jnp.stack([2 * x + y, c]).astype(jnp.int32)
    parts, by_chip = _chip_exchange_wait(send_sems, recv_sems, parts, lands, after, tag)
    fulls = [_chip_sum(p, b, place, "chip_sum_" + tag) for p, b in zip(parts, by_chip)]
    return _pair_allgather(fulls, tag)


def kernel(x, mem, norm_g, mem_norm_g, w_in, conv_w, a_log, dt_bias, dn_norm_g, w_mem_kv, w_br_dn, w_br_sb, w_br_mem, w_out, final_g, loss_target, m_norm_g, m_mem_norm_g, m_w_in, m_conv_w, m_a_log, m_dt_bias, m_dn_norm_g, m_w_mem_kv, m_w_br_dn, m_w_br_sb, m_w_br_mem, m_w_out, m_final_g, v_norm_g, v_mem_norm_g, v_w_in, v_conv_w, v_a_log, v_dt_bias, v_dn_norm_g, v_w_mem_kv, v_w_br_dn, v_w_br_sb, v_w_br_mem, v_w_out, v_final_g):
    w_a = w_in[0]
    w_b = _pack_b(w_mem_kv[0], w_br_dn[0], w_br_sb[0], w_br_mem[0], w_out[0])
    m_b = _pack_b(m_w_mem_kv[0], m_w_br_dn[0], m_w_br_sb[0], m_w_br_mem[0], m_w_out[0])
    v_b = _pack_b(v_w_mem_kv[0], v_w_br_dn[0], v_w_br_sb[0], v_w_br_mem[0], v_w_out[0])

    shard_idx = 2 * lax.axis_index("x") + lax.axis_index("y")
    shard = jnp.reshape(shard_idx, (1,)).astype(jnp.int32)
    ga, g_conv = _gather_shards([_cast_to_window(w_a, shard, "cast_w_in"),
                                 _cast_bf16(_conv_slab(conv_w[0]), "cast_conv")])
    w_r = _windows_to_w_r(ga)
    f_conv = g_conv[:, :3].reshape(N_SHARD, 4, 768).transpose(1, 0, 2).reshape(4, 3 * D_MODEL).astype(F32)
    b_flight = _shard_gather_start(_cast_bf16(w_b, "cast_w_b"), after=ga)

    def proj_weights(after):
        own, land = _shard_gather_wait(b_flight[0], b_flight[1], b_flight[2], b_flight[3], after)
        gb = lax.dynamic_update_slice(land, own[None], (shard_idx, 0, 0))
        return (gb[:, B_MEMKV:B_BRDN].reshape(N_SHARD * 256, 512),
                gb[:, B_BRDN:B_BRSB].reshape(N_SHARD * 256, D_MODEL),
                gb[:, B_BRSB:B_BRMEM].reshape(N_SHARD * 256, D_MODEL),
                gb[:, B_BRMEM:B_OUT].reshape(N_SHARD, 256, 256).transpose(1, 0, 2).reshape(256, D_MODEL),
                gb[:, B_OUT:B_CONV].reshape(N_SHARD * 256, D_MODEL))

    flights = {}

    def on_early(grads):
        g_b = jnp.concatenate([
            grads["w_mem_kv"].reshape(N_SHARD, 128, D_MODEL), grads["w_br_dn"].reshape(N_SHARD, 256, D_MODEL),
            grads["w_br_sb"].reshape(N_SHARD, 256, D_MODEL),
            grads["w_br_mem"].reshape(256, N_SHARD, 256).transpose(1, 0, 2).reshape(N_SHARD, 64, D_MODEL),
            grads["w_out"].reshape(N_SHARD, 256, D_MODEL)], axis=1).astype(BF16)
        flights["b"] = _reduce_scatter_start([g_b], "b")
        return flights["b"][4]

    loss, _, g_w_sh, input_grad = _local_step(
        x[0], mem[0], loss_target[0], norm_g, mem_norm_g, w_r, ga, f_conv, a_log, dt_bias, dn_norm_g,
        proj_weights, final_g, on_early=on_early, after_gather=b_flight[4])
    flights["a"] = _reduce_scatter_start([g_w_sh], "a")
    grad_x, small = input_grad(after=flights["a"][4])

    part = _pack_small(small["norm_g"], small["mem_norm_g"], small["final_g"], small["dn_norm_g"],
                       small["a_log"], small["dt_bias"], _conv_rows(small["conv_w"]), loss)
    w_s = _pack_small(norm_g, mem_norm_g, final_g, dn_norm_g, a_log, dt_bias, _conv_shard_rows(conv_w[0], shard_idx))
    m_s = _pack_small(m_norm_g, m_mem_norm_g, m_final_g, m_dn_norm_g, m_a_log, m_dt_bias,
                      _conv_shard_rows(m_conv_w[0], shard_idx))
    v_s = _pack_small(v_norm_g, v_mem_norm_g, v_final_g, v_dn_norm_g, v_a_log, v_dt_bias,
                      _conv_shard_rows(v_conv_w[0], shard_idx))
    (gs_b,) = _reduce_scatter_finish(flights["b"], after=grad_x, tag="b")
    gr_b, d_b, nm_b, nv_b = _adamw(w_b, gs_b, m_b, v_b, "adamw_b")
    g_s, d_s, nm_s, nv_s = _small_update(_allgather_small(part, after=d_b), w_s, m_s, v_s)

    (gs_in,) = _reduce_scatter_finish(flights["a"], after=g_s, tag="a")
    gr_in, d_in, nm_in, nv_in = _adamw_window(w_a, gs_in, m_w_in[0], v_w_in[0], shard, "adamw_w_in")

    def assemble(slab_small, a_in, slab_b):
        s_norm, s_memnorm, s_final, s_dnnorm, s_alog, s_dtb, b_conv = _unpack_small(slab_small, shard_idx)
        b_memkv, b_brdn, b_brsb, b_brmem, b_out = _unpack_b(slab_b)
        return [s_norm, s_memnorm, a_in.reshape(1, D_MODEL, IN_WIDTH // N_SHARD), b_conv, s_alog, s_dtb, s_dnnorm,
                b_memkv, b_brdn, b_brsb, b_brmem, b_out, s_final]

    outs = [g_s[S_LOSS, 0], grad_x.reshape(1, -1, D_MODEL)]
    outs += assemble(g_s, gr_in, gr_b)
    outs += assemble(d_s, d_in, d_b)
    outs += assemble(nm_s, nm_in, nm_b)
    outs += assemble(nv_s, nv_in, nv_b)
    return tuple(outs)
```

```python
import math

import jax
import jax.numpy as jnp
from jax import lax
from jax.experimental import pallas as pl
from jax.experimental.pallas import tpu as pltpu

F32 = jnp.float32
BF16 = jnp.bfloat16
MESH = pl.DeviceIdType.MESH

D_MODEL = 1024
N_HEADS = 8
D_HEAD = 128
DN_CHUNK = 64
DN_GROUP = 16
DN_SCAN_GROUP = 4
SB_BLOCK = 256
SB_HEADS_PER_STEP = 2
SB_QBLOCK = 256
MEM_HEADS = 4
MEM_DH = 64
MEM_W = MEM_HEADS * MEM_DH
NORM_EPS = 1e-6
IN_WIDTH = 11792
N_SHARD = 4
SHARD_W = IN_WIDTH // N_SHARD
SHARD_PAD = 3072
N_DEV = 8

C_DNZ = 3072
C_SBQ = 4096
C_SBZ = 7168
C_MQ = 8192
C_MZ = 8448
C_GATES = 8704
C_BA = 11776
W_R = 12288

ADAM_LR = 0.001
ADAM_B1 = 0.9
ADAM_B2 = 0.999
ADAM_EPS = 1e-08
ADAM_WD = 0.01
ADAM_STEP = 10

VMEM_LIMIT = 56 * 1024 * 1024

B_MEMKV, B_BRDN, B_BRSB, B_BRMEM, B_OUT, B_CONV = 0, 128, 384, 640, 704, 960
S_NORM, S_MEMNORM, S_FINAL, S_DNNORM, S_ALOG, S_DTB, S_LOSS, S_CONV, S_ROWS = 0, 8, 16, 24, 25, 26, 27, 32, 128
CONV_BLOCKS = 3 * D_MODEL // 128


def _cp(**kw):
    return pltpu.CompilerParams(vmem_limit_bytes=VMEM_LIMIT, **kw)


def _dot(a, b, dims):
    lead = a.ndim - 2
    ca, cb = {"nn": (1, 0), "nt": (1, 1), "tn": (0, 0)}[dims]
    batch = tuple(range(lead))
    return lax.dot_general(a, b, (((ca + lead,), (cb + lead,)), (batch, batch)), preferred_element_type=F32)


def _chunks(x):
    return x.reshape(x.shape[0] // DN_CHUNK, DN_CHUNK, x.shape[1])


def _unchunk(x):
    return x.reshape(x.shape[0] * x.shape[1], x.shape[2])


def _bdot(a, b, dims):
    return _dot(a.astype(BF16), b.astype(BF16), dims)


def _split(a):
    hi = a.astype(BF16)
    return hi, (a - hi.astype(F32)).astype(BF16)


def _dot3(a, b, dims):
    a1, a2 = _split(a)
    b1, b2 = _split(b)
    return _dot(a1, b1, dims) + (_dot(a1, b2, dims) + _dot(a2, b1, dims))


def _ones_dot(a, ones_bf16):
    out = _dot(a.reshape(-1, a.shape[-1]).astype(BF16), ones_bf16, "nn")
    return out.reshape(a.shape[:-1] + (ones_bf16.shape[1],))


def _sigmoid(x):
    return 1.0 / (1.0 + jnp.exp(-x))


def _log1p_small(u):
    return jnp.where(u < 1e-2, u * (1.0 - u * (0.5 - u * (1.0 / 3.0))), jnp.log(1.0 + u))


def _pick(dim, cands):
    for c in cands:
        if dim % c == 0:
            return c
    return dim


def _mm(a, b, dims, name, out_dtype=F32, out_shards=1, after=None, tm_max=1024, tn_max=512):
    ta, tb = dims[0] == "t", dims[1] == "t"
    m, k = (a.shape[1], a.shape[0]) if ta else a.shape
    b_shards = b.shape[0] if b.ndim == 3 else 1
    n = b.shape[-2] if tb else b.shape[-1]
    tm = _pick(m, (tm_max, 1024, 512, 256))
    tn = _pick(n // out_shards, (tn_max, 512, 384, 256, 128))
    tk = _pick(k // b_shards, (2048, 1024, 512, 384, 256))
    nk = k // tk

    def body(a_ref, b_ref, *rest):
        if nk == 1:
            rest[-1][...] = _bdot(a_ref[...], b_ref[...], dims).astype(out_dtype)
            return
        o_ref, acc_ref = rest[-2:]
        kk = pl.program_id(2)

        @pl.when(kk == 0)
        def _():
            acc_ref[...] = jnp.zeros_like(acc_ref)

        acc_ref[...] += _bdot(a_ref[...], b_ref[...], dims)

        @pl.when(kk == nk - 1)
        def _():
            o_ref[...] = acc_ref[...].astype(out_dtype)

    a_spec = pl.BlockSpec((tk, tm), lambda i, j, q: (q, i)) if ta else pl.BlockSpec((tm, tk), lambda i, j, q: (i, q))
    if b_shards > 1:
        per_k = k // b_shards // tk
        b_spec = pl.BlockSpec((None, tn, tk), lambda i, j, q: (q // per_k, j, q % per_k))
    else:
        b_spec = pl.BlockSpec((tn, tk), lambda i, j, q: (j, q)) if tb else pl.BlockSpec((tk, tn), lambda i, j, q: (q, j))
    if out_shards > 1:
        per_n = n // out_shards // tn
        out_spec = pl.BlockSpec((None, tm, tn), lambda i, j, q: (j // per_n, i, j % per_n))
        out_shape = jax.ShapeDtypeStruct((out_shards, m, n // out_shards), out_dtype)
    else:
        out_spec = pl.BlockSpec((tm, tn), lambda i, j, q: (i, j))
        out_shape = jax.ShapeDtypeStruct((m, n), out_dtype)
    extra_specs, extra = [], []
    if after is not None:
        extra_specs, extra = [pl.BlockSpec(after.shape, lambda i, j, q: (0, 0))], [after]
    return pl.pallas_call(
        body, name=name, grid=(m // tm, n // tn, nk),
        in_specs=[a_spec, b_spec] + extra_specs, out_specs=out_spec, out_shape=out_shape,
        scratch_shapes=[pltpu.VMEM((tm, tn), F32)] if nk > 1 else [],
        compiler_params=_cp(dimension_semantics=("parallel", "parallel", "arbitrary")),
    )(a, b, *extra)


def _rmsnorm_fwd(x, g, name, after=None):
    t, d = x.shape
    tb = _pick(t, (512, 256))

    def body(x_ref, g_ref, *rest):
        xv = x_ref[...]
        r = lax.rsqrt(jnp.mean(xv * xv, axis=-1, keepdims=True) + NORM_EPS)
        rest[-1][...] = ((xv * r) * g_ref[...]).astype(BF16)

    extra_specs, extra = [], []
    if after is not None:
        extra_specs, extra = [pl.BlockSpec(after.shape, lambda i: (0, 0))], [after]
    return pl.pallas_call(
        body, name=name, grid=(t // tb,),
        in_specs=[pl.BlockSpec((tb, d), lambda i: (i, 0)), pl.BlockSpec((1, d), lambda i: (0, 0))] + extra_specs,
        out_specs=pl.BlockSpec((tb, d), lambda i: (i, 0)),
        out_shape=jax.ShapeDtypeStruct((t, d), BF16), compiler_params=_cp(),
    )(x, g, *extra)


def _rmsnorm_bwd(x, g, dh, resid, name):
    t, d = x.shape
    tb = _pick(t, (256,))

    def body(x_ref, g_ref, dh_ref, r_ref, dx_ref, dg_ref):
        @pl.when(pl.program_id(0) == 0)
        def _():
            dg_ref[...] = jnp.zeros_like(dg_ref)

        xv = x_ref[...]
        r = lax.rsqrt(jnp.mean(xv * xv, axis=-1, keepdims=True) + NORM_EPS)
        xhat = xv * r
        dhv = dh_ref[...]
        dg_ref[...] += jnp.sum(dhv * xhat, axis=0, keepdims=True)
        dxh = dhv * g_ref[...]
        dx_ref[...] = r_ref[...] + r * (dxh - xhat * jnp.mean(dxh * xhat, axis=-1, keepdims=True))

    row = pl.BlockSpec((tb, d), lambda i: (i, 0))
    vec = pl.BlockSpec((1, d), lambda i: (0, 0))
    return pl.pallas_call(
        body, name=name, grid=(t // tb,), in_specs=[row, vec, row, row], out_specs=[row, vec],
        out_shape=[jax.ShapeDtypeStruct((t, d), F32), jax.ShapeDtypeStruct((1, d), F32)], compiler_params=_cp(),
    )(x, g, dh, resid)


def _conv_silu(xv, w, row):
    y = xv * w[3:4, :]
    for s in (1, 2, 3):
        xs = jnp.where(row >= s, pltpu.roll(xv, s, 0), 0.0)
        y = y + xs * w[3 - s:4 - s, :]
    sig = _sigmoid(y)
    return y, sig, y * sig


def _dn_prep_fwd(proj, conv_w):
    t = proj.shape[0]

    def body(p_ref, w_ref, o_ref):
        j = pl.program_id(0)
        xv = p_ref[...]
        row = lax.broadcasted_iota(jnp.int32, xv.shape, 0)
        _, _, a = _conv_silu(xv, w_ref[...], row)
        inv = lax.rsqrt(jnp.sum(a * a, axis=-1, keepdims=True) + NORM_EPS)
        scale = jnp.where(j < N_HEADS, D_HEAD ** -0.5, 1.0)
        normed = jnp.where(j < 2 * N_HEADS, 1.0, 0.0)
        o_ref[...] = a * (normed * (inv * scale) + (1.0 - normed))

    return pl.pallas_call(
        body, name="dn_prep_fwd", grid=(3 * N_HEADS,),
        in_specs=[pl.BlockSpec((t, D_HEAD), lambda j: (0, j)), pl.BlockSpec((4, D_HEAD), lambda j: (0, j))],
        out_specs=pl.BlockSpec((t, D_HEAD), lambda j: (0, j)),
        out_shape=jax.ShapeDtypeStruct((t, 3 * D_MODEL), F32), compiler_params=_cp(),
    )(proj, conv_w)


def _dn_prep_bwd(proj, conv_w, dq, dk, dv):
    t = proj.shape[0]

    def body(p_ref, w_ref, dq_ref, dk_ref, dv_ref, dp_ref, dw_ref):
        j = pl.program_id(0)
        xv = p_ref[...]
        w = w_ref[...]
        row = lax.broadcasted_iota(jnp.int32, xv.shape, 0)
        y, s, a = _conv_silu(xv, w, row)
        part = jnp.zeros(xv.shape, jnp.int32) + j // N_HEADS
        dn = jnp.where(part == 0, dq_ref[...], jnp.where(part == 1, dk_ref[...], dv_ref[...]))
        inv = lax.rsqrt(jnp.sum(a * a, axis=-1, keepdims=True) + NORM_EPS)
        scale = jnp.where(j < N_HEADS, D_HEAD ** -0.5, 1.0)
        ds = dn * scale
        da_norm = inv * ds - a * (inv * inv * inv) * jnp.sum(ds * a, axis=-1, keepdims=True)
        normed = jnp.where(j < 2 * N_HEADS, 1.0, 0.0)
        da = normed * da_norm + (1.0 - normed) * dn
        dy = da * (s * (1.0 + y * (1.0 - s)))
        dx = dy * w[3:4, :]
        dw_ref[3:4, :] = jnp.sum(dy * xv, axis=0, keepdims=True)
        for sft in (1, 2, 3):
            xs = jnp.where(row >= sft, pltpu.roll(xv, sft, 0), 0.0)
            dw_ref[3 - sft:4 - sft, :] = jnp.sum(dy * xs, axis=0, keepdims=True)
            dys = jnp.where(row < t - sft, pltpu.roll(dy, t - sft, 0), 0.0)
            dx = dx + dys * w[3 - sft:4 - sft, :]
        dp_ref[...] = dx.astype(BF16)

    blk = pl.BlockSpec((t, D_HEAD), lambda j: (0, j))
    wblk = pl.BlockSpec((4, D_HEAD), lambda j: (0, j))

    def grad(part):
        return pl.BlockSpec((t, D_HEAD), lambda j: (0, jnp.clip(j - part * N_HEADS, 0, N_HEADS - 1)))

    return pl.pallas_call(
        body, name="dn_prep_bwd", grid=(3 * N_HEADS,), in_specs=[blk, wblk, grad(0), grad(1), grad(2)],
        out_specs=[blk, wblk],
        out_shape=[jax.ShapeDtypeStruct((t, 3 * D_MODEL), BF16), jax.ShapeDtypeStruct((4, 3 * D_MODEL), F32)],
        compiler_params=_cp(),
    )(proj, conv_w, dq, dk, dv)


def _softplus_parts(xv):
    e = jnp.exp(-jnp.abs(xv))
    return jnp.maximum(xv, 0.0) + _log1p_small(e)


def _chunk_scan(v, row, reverse):
    t = v.shape[0]
    pos = row & (DN_CHUNK - 1)
    s = 1
    while s < DN_CHUNK:
        if reverse:
            v = v + jnp.where(pos < DN_CHUNK - s, pltpu.roll(v, t - s, 0), 0.0)
        else:
            v = v + jnp.where(pos >= s, pltpu.roll(v, s, 0), 0.0)
        s *= 2
    return v


def _dn_gate_fwd(proj, alog_row, dtb_row):
    t = proj.shape[0]

    def body(p_ref, al_ref, dt_ref, b_ref, g_ref):
        p = p_ref[...]
        row = lax.broadcasted_iota(jnp.int32, p.shape, 0)
        b_ref[...] = _sigmoid(p)
        g = -jnp.exp(al_ref[...]) * _softplus_parts(p + dt_ref[...])
        g_ref[...] = _chunk_scan(g, row, reverse=False)

    blk = pl.BlockSpec((t, 128), lambda i: (0, C_BA // 128))
    vec = pl.BlockSpec((1, 128), lambda i: (0, 0))
    out = pl.BlockSpec((t, 128), lambda i: (0, 0))
    return pl.pallas_call(
        body, name="dn_gate_fwd", grid=(1,), in_specs=[blk, vec, vec], out_specs=[out, out],
        out_shape=[jax.ShapeDtypeStruct((t, 128), F32)] * 2, compiler_params=_cp(),
    )(proj, alog_row, dtb_row)


def _dn_gate_bwd(proj, alog_row, dtb_row, dbeta, dgc):
    t = proj.shape[0]

    def body(p_ref, al_ref, dt_ref, db_ref, dg_ref, dp_ref, dal_ref, ddt_ref):
        p = p_ref[...]
        row = lax.broadcasted_iota(jnp.int32, p.shape, 0)
        lane = lax.broadcasted_iota(jnp.int32, p.shape, 1)
        s = _sigmoid(p)
        d_b = db_ref[...] * s * (1.0 - s)
        dg = _chunk_scan(dg_ref[...], row, reverse=True)
        xa = p + dt_ref[...]
        ea = jnp.exp(al_ref[...])
        g = -ea * _softplus_parts(xa)
        d_a = dg * (-ea) * _sigmoid(xa)
        dp_ref[...] = jnp.where(lane < N_HEADS, d_b, jnp.where(lane < 2 * N_HEADS, d_a, 0.0)).astype(BF16)
        dal_ref[...] = jnp.sum(dg * g, axis=0, keepdims=True)
        ddt_ref[...] = jnp.sum(d_a, axis=0, keepdims=True)

    blk = pl.BlockSpec((t, 128), lambda i: (0, C_BA // 128))
    vec = pl.BlockSpec((1, 128), lambda i: (0, 0))
    full = pl.BlockSpec((t, 128), lambda i: (0, 0))
    return pl.pallas_call(
        body, name="dn_gate_bwd", grid=(1,), in_specs=[blk, vec, vec, full, full], out_specs=[full, vec, vec],
        out_shape=[jax.ShapeDtypeStruct((t, 128), BF16), jax.ShapeDtypeStruct((1, 128), F32),
                   jax.ShapeDtypeStruct((1, 128), F32)], compiler_params=_cp(),
    )(proj, alog_row, dtb_row, dbeta, dgc)


def _col_to_row(col, eye):
    return jnp.sum(jnp.where(eye, col, 0.0), axis=-2, keepdims=True)


def _row_to_col(rowv, eye):
    return jnp.sum(jnp.where(eye, rowv, 0.0), axis=-1, keepdims=True)


def _tri_inverse(m, ri, ci):
    eye = (ri == ci).astype(F32)
    b16 = (ri >> 4) == (ci >> 4)
    b32 = (ri >> 5) == (ci >> 5)
    m1 = jnp.where(b16, m, 0.0)
    x = eye - m1
    p = _dot3(m1, m1, "nn")
    x = x + _dot3(x, p, "nn")
    p = _dot3(p, p, "nn")
    x = x + _dot3(x, p, "nn")
    p = _dot3(p, p, "nn")
    x = x + _dot3(x, p, "nn")
    c1 = jnp.where(jnp.logical_and(b32, jnp.logical_not(b16)), m, 0.0)
    x = x - _dot3(_dot3(x, c1, "nn"), x, "nn")
    c2 = jnp.where(b32, 0.0, m)
    x = x - _dot3(_dot3(x, c2, "nn"), x, "nn")
    return x


def _dn_chunk_common(q, k, gc, ri, ci):
    eye = ri == ci
    g_row = _col_to_row(gc, eye)
    diff = jnp.minimum(gc - g_row, 0.0)
    gam = jnp.where(ri >= ci, jnp.exp(diff), 0.0)
    kk = _bdot(k, k, "nt")
    qk = _bdot(q, k, "nt")
    rcol = lax.broadcasted_iota(jnp.int32, gc.shape, gc.ndim - 2)
    last = jnp.sum(jnp.where(rcol == DN_CHUNK - 1, gc, 0.0), axis=-2, keepdims=True)
    e_g = jnp.exp(gc)
    dec = jnp.exp(last - gc)
    return eye, gam, kk, qk, last, e_g, dec, rcol


def _dn_specs(t, rows_blk):
    def head(off):
        return pl.BlockSpec((rows_blk, D_HEAD), lambda g, h: (g, off + h))

    lanes = pl.BlockSpec((rows_blk, 128), lambda g, h: (g, 0))
    hm = pl.BlockSpec((None, rows_blk, D_HEAD), lambda g, h: (h, g, 0))
    sq = pl.BlockSpec((1, rows_blk, DN_CHUNK), lambda g, h: (h, g, 0))
    tile = pl.BlockSpec((1, rows_blk // DN_CHUNK, 8, 128), lambda g, h: (h, g, 0, 0))
    return head, lanes, hm, sq, tile


def _head_column(slab, lane_idx):
    lane = lax.broadcasted_iota(jnp.int32, slab.shape, 1)
    return _chunks(jnp.sum(jnp.where(lane == lane_idx, slab, 0.0), axis=1, keepdims=True))


def _dn_intra_fwd(qkv, beta_t, g_t):
    t = qkv.shape[0]
    n_chunks = t // DN_CHUNK
    rows_blk = min(DN_GROUP * DN_CHUNK, t)

    def body(q_ref, k_ref, v_ref, b_ref, g_ref, u_ref, w_ref, qd_ref, kd_ref, a_ref, ti_ref, el_ref):
        ri = lax.broadcasted_iota(jnp.int32, (DN_CHUNK, DN_CHUNK), 0)
        ci = lax.broadcasted_iota(jnp.int32, (DN_CHUNK, DN_CHUNK), 1)
        h = pl.program_id(1)
        q, k, v = (_chunks(r[...]) for r in (q_ref, k_ref, v_ref))
        b, gc = _head_column(b_ref[...], h), _head_column(g_ref[...], h + N_HEADS)
        _, gam, kk, qk, last, e_g, dec, _ = _dn_chunk_common(q, k, gc, ri, ci)
        tinv = _tri_inverse(jnp.where(ri > ci, b * kk * gam, 0.0), ri, ci)
        u_ref[...] = _unchunk(_bdot(tinv, v * b, "nn"))
        w_ref[...] = _unchunk(_bdot(tinv, k * (b * e_g), "nn"))
        qd_ref[...] = _unchunk(q * e_g)
        kd_ref[...] = _unchunk(k * dec)
        a_ref[0] = _unchunk(qk * gam)
        ti_ref[0] = _unchunk(tinv)
        el_ref[0] = jnp.broadcast_to(jnp.exp(last), (rows_blk // DN_CHUNK, 8, 128))

    head, lanes, hm, sq, tile = _dn_specs(t, rows_blk)
    act = jax.ShapeDtypeStruct((N_HEADS, t, D_HEAD), F32)
    sqs = jax.ShapeDtypeStruct((N_HEADS, t, DN_CHUNK), F32)
    return pl.pallas_call(
        body, name="dn_intra_fwd", grid=(t // rows_blk, N_HEADS),
        in_specs=[head(0), head(N_HEADS), head(2 * N_HEADS), lanes, lanes],
        out_specs=[hm] * 4 + [sq, sq, tile],
        out_shape=[act] * 4 + [sqs, sqs, jax.ShapeDtypeStruct((N_HEADS, n_chunks, 8, 128), F32)],
        compiler_params=_cp(),
    )(qkv, qkv, qkv, beta_t, g_t)


def _dn_scan_specs(t, rows_blk, reverse):
    n_groups = t // rows_blk

    def at(g):
        return n_groups - 1 - g if reverse else g

    per = rows_blk // DN_CHUNK
    act = pl.BlockSpec((N_HEADS, rows_blk, D_HEAD), lambda g: (0, at(g), 0))
    sq = pl.BlockSpec((N_HEADS, rows_blk, DN_CHUNK), lambda g: (0, at(g), 0))
    state = pl.BlockSpec((N_HEADS, per, D_HEAD, D_HEAD), lambda g: (0, at(g), 0, 0))
    tile = pl.BlockSpec((N_HEADS, per, 8, 128), lambda g: (0, at(g), 0, 0))
    return act, sq, state, tile


def _dn_scan_fwd(u, w, qd, kd, a, el):
    t = u.shape[1]
    n_chunks = t // DN_CHUNK
    rows_blk = DN_SCAN_GROUP * DN_CHUNK

    def body(u_ref, w_ref, qd_ref, kd_ref, a_ref, el_ref, o_ref, vn_ref, s_ref, s_scr):
        @pl.when(pl.program_id(0) == 0)
        def _():
            s_scr[...] = jnp.zeros_like(s_scr)

        for cc in range(DN_SCAN_GROUP):
            rows = slice(cc * DN_CHUNK, (cc + 1) * DN_CHUNK)
            s = s_scr[...]
            s_ref[:, cc] = s
            v_new = u_ref[:, rows, :] - _bdot(w_ref[:, rows, :], s, "nn")
            vn_ref[:, rows, :] = v_new
            o_ref[:, rows, :] = _bdot(qd_ref[:, rows, :], s, "nn") + _bdot(a_ref[:, rows, :], v_new, "nn")
            s_scr[...] = s * el_ref[:, cc][:, 0:1, :] + _bdot(kd_ref[:, rows, :], v_new, "tn")

    act, sq, state, tile = _dn_scan_specs(t, rows_blk, reverse=False)
    shp = jax.ShapeDtypeStruct((N_HEADS, t, D_HEAD), F32)
    return pl.pallas_call(
        body, name="dn_scan_fwd", grid=(t // rows_blk,),
        in_specs=[act, act, act, act, sq, tile], out_specs=[act, act, state],
        out_shape=[shp, shp, jax.ShapeDtypeStruct((N_HEADS, n_chunks, D_HEAD, D_HEAD), F32)],
        scratch_shapes=[pltpu.VMEM((N_HEADS, D_HEAD, D_HEAD), F32)],
        compiler_params=_cp(dimension_semantics=("arbitrary",)),
    )(u, w, qd, kd, a, el)


def _dn_scan_bwd(w, qd, kd, a, el, vn, s_all, do):
    t = w.shape[1]
    n_chunks = t // DN_CHUNK
    rows_blk = DN_SCAN_GROUP * DN_CHUNK

    def body(w_ref, qd_ref, kd_ref, a_ref, el_ref, vn_ref, s_ref, do_ref, dvn_ref, dkd_ref, dqd_ref, dw_ref, dl_ref, ds_scr):
        @pl.when(pl.program_id(0) == 0)
        def _():
            ds_scr[...] = jnp.zeros_like(ds_scr)

        for cc in reversed(range(DN_SCAN_GROUP)):
            rows = slice(cc * DN_CHUNK, (cc + 1) * DN_CHUNK)
            s = s_ref[:, cc]
            d_s = ds_scr[...]
            e_last = el_ref[:, cc][:, 0:1, :]
            d_o = do_ref[:, rows, :]
            dv_new = _bdot(a_ref[:, rows, :], d_o, "tn") + _bdot(kd_ref[:, rows, :], d_s, "nn")
            ds_scr[...] = d_s * e_last + _bdot(qd_ref[:, rows, :], d_o, "tn") - _bdot(w_ref[:, rows, :], dv_new, "tn")
            dvn_ref[:, rows, :] = dv_new
            dkd_ref[:, rows, :] = _bdot(vn_ref[:, rows, :], d_s, "nt")
            dqd_ref[:, rows, :] = _bdot(d_o, s, "nt")
            dw_ref[:, rows, :] = -_bdot(dv_new, s, "nt")
            dlast = jnp.sum(jnp.sum(d_s * s, axis=2, keepdims=True), axis=1, keepdims=True)
            dl_ref[:, cc] = jnp.broadcast_to(dlast * e_last, (N_HEADS, 8, 128))

    act, sq, state, tile = _dn_scan_specs(t, rows_blk, reverse=True)
    shp = jax.ShapeDtypeStruct((N_HEADS, t, D_HEAD), F32)
    return pl.pallas_call(
        body, name="dn_scan_bwd", grid=(t // rows_blk,),
        in_specs=[act, act, act, sq, tile, act, state, act], out_specs=[act] * 4 + [tile],
        out_shape=[shp] * 4 + [jax.ShapeDtypeStruct((N_HEADS, n_chunks, 8, 128), F32)],
        scratch_shapes=[pltpu.VMEM((N_HEADS, D_HEAD, D_HEAD), F32)],
        compiler_params=_cp(dimension_semantics=("arbitrary",)),
    )(w, qd, kd, a, el, vn, s_all, do)


def _dn_intra_bwd(qkv, beta_t, g_t, tinv_all, vn, do, dvn, dkd, dqd, dw, dl):
    t = qkv.shape[0]
    rows_blk = min(DN_GROUP * DN_CHUNK, t)

    def body(q_ref, k_ref, v_ref, b_ref, g_ref, ti_ref, vn_ref, do_ref, dvn_ref, dkd_ref, dqd_ref, dw_ref, dl_ref,
             dq_ref, dk_ref, dv_ref, db_ref, dg_ref):
        ri = lax.broadcasted_iota(jnp.int32, (DN_CHUNK, DN_CHUNK), 0)
        ci = lax.broadcasted_iota(jnp.int32, (DN_CHUNK, DN_CHUNK), 1)
        h = pl.program_id(1)
        q, k, v = (_chunks(r[...]) for r in (q_ref, k_ref, v_ref))
        b, gc = _head_column(b_ref[...], h), _head_column(g_ref[...], h + N_HEADS)
        tinv = _chunks(ti_ref[0])
        dv_new, dk_dec, dq_dec, d_w = (_chunks(r[...]) for r in (dvn_ref, dkd_ref, dqd_ref, dw_ref))
        eye, gam, kk, qk, _, e_g, dec, rcol = _dn_chunk_common(q, k, gc, ri, ci)
        bv = v * b
        bk = k * (b * e_g)

        d_a = jnp.where(ri >= ci, _bdot(_chunks(do_ref[...]), _chunks(vn_ref[...]), "nt"), 0.0)
        dbv = _bdot(tinv, dv_new, "tn")
        dbk = _bdot(tinv, d_w, "tn")
        d_tinv = _bdot(dv_new, bv, "nt") + _bdot(d_w, bk, "nt")
        d_m = -jnp.where(ri > ci, _dot3(_dot3(tinv, d_tinv, "tn"), tinv, "nt"), 0.0)

        d_kk = d_m * b * gam
        d_gam = d_m * b * kk + d_a * qk
        d_qk = d_a * gam
        dq_ref[...] = _unchunk(_bdot(d_qk, k, "nn") + dq_dec * e_g)
        dk_ref[...] = _unchunk(_bdot(d_qk, q, "tn") + _bdot(d_kk, k, "nn") + _bdot(d_kk, k, "tn")
                               + dk_dec * dec + dbk * (b * e_g))
        dv_ref[...] = _unchunk(dbv * b)
        d_b = _unchunk(jnp.sum(d_m * kk * gam, axis=-1, keepdims=True) + jnp.sum(dbv * v, axis=-1, keepdims=True)
                       + jnp.sum(dbk * k, axis=-1, keepdims=True) * e_g)

        xg = d_gam * gam
        kdk = jnp.sum(dk_dec * (k * dec), axis=-1, keepdims=True)
        d_gc = (jnp.sum(xg, axis=-1, keepdims=True) - _row_to_col(jnp.sum(xg, axis=-2, keepdims=True), eye)
                + jnp.sum(dq_dec * (q * e_g), axis=-1, keepdims=True) - kdk
                + jnp.sum(dbk * bk, axis=-1, keepdims=True))
        d_last_total = dl_ref[0][:, 0:1, 0:1] + jnp.sum(kdk, axis=-2, keepdims=True)
        d_g = _unchunk(d_gc + jnp.where(rcol == DN_CHUNK - 1, d_last_total, 0.0))

        @pl.when(h == 0)
        def _():
            db_ref[...] = jnp.zeros_like(db_ref)
            dg_ref[...] = jnp.zeros_like(dg_ref)

        lane = lax.broadcasted_iota(jnp.int32, db_ref.shape, 1)
        db_ref[...] += jnp.where(lane == h, d_b, 0.0)
        dg_ref[...] += jnp.where(lane == h + N_HEADS, d_g, 0.0)

    head, lanes, hm, sq, tile = _dn_specs(t, rows_blk)
    return pl.pallas_call(
        body, name="dn_intra_bwd", grid=(t // rows_blk, N_HEADS),
        in_specs=[head(0), head(N_HEADS), head(2 * N_HEADS), lanes, lanes, sq] + [hm] * 6 + [tile],
        out_specs=[head(0), head(0), head(0), lanes, lanes],
        out_shape=[jax.ShapeDtypeStruct((t, D_MODEL), F32)] * 3 + [jax.ShapeDtypeStruct((t, 128), F32)] * 2,
        compiler_params=_cp(),
    )(qkv, qkv, qkv, beta_t, g_t, tinv_all, vn, do, dvn, dkd, dqd, dw, dl)


def _dn_post_fwd(o, proj, gn):
    t = o.shape[1]

    def body(o_ref, z_ref, g_ref, out_ref):
        ov, z = o_ref[...], z_ref[...]
        r = lax.rsqrt(jnp.mean(ov * ov, axis=-1, keepdims=True) + NORM_EPS)
        out_ref[...] = (((ov * r) * g_ref[...]) * (z * _sigmoid(z))).astype(BF16)

    blk = pl.BlockSpec((t, D_HEAD), lambda h: (0, h))
    return pl.pallas_call(
        body, name="dn_post_fwd", grid=(N_HEADS,),
        in_specs=[pl.BlockSpec((None, t, D_HEAD), lambda h: (h, 0, 0)),
                  pl.BlockSpec((t, D_HEAD), lambda h: (0, C_DNZ // D_HEAD + h)),
                  pl.BlockSpec((1, D_HEAD), lambda h: (0, 0))],
        out_specs=blk, out_shape=jax.ShapeDtypeStruct((t, D_MODEL), BF16), compiler_params=_cp(),
    )(o, proj, gn)


def _dn_post_bwd(o, proj, gn, dout):
    t = o.shape[1]

    def body(o_ref, z_ref, g_ref, d_ref, do_ref, dz_ref, dg_ref):
        @pl.when(pl.program_id(0) == 0)
        def _():
            dg_ref[...] = jnp.zeros_like(dg_ref)

        ov, z, d = o_ref[...], z_ref[...], d_ref[...]
        r = lax.rsqrt(jnp.mean(ov * ov, axis=-1, keepdims=True) + NORM_EPS)
        ohat = ov * r
        s = _sigmoid(z)
        d_on = d * (z * s)
        dz_ref[...] = (d * (ohat * g_ref[...]) * (s * (1.0 + z * (1.0 - s)))).astype(BF16)
        dg_ref[...] += jnp.sum(d_on * ohat, axis=0, keepdims=True)
        dxh = d_on * g_ref[...]
        do_ref[...] = r * (dxh - ohat * jnp.mean(dxh * ohat, axis=-1, keepdims=True))

    blk = pl.BlockSpec((t, D_HEAD), lambda h: (0, h))
    hm = pl.BlockSpec((None, t, D_HEAD), lambda h: (h, 0, 0))
    vec = pl.BlockSpec((1, D_HEAD), lambda h: (0, 0))
    return pl.pallas_call(
        body, name="dn_post_bwd", grid=(N_HEADS,),
        in_specs=[hm, pl.BlockSpec((t, D_HEAD), lambda h: (0, C_DNZ // D_HEAD + h)), vec, blk],
        out_specs=[hm, blk, vec],
        out_shape=[jax.ShapeDtypeStruct((N_HEADS, t, D_HEAD), F32), jax.ShapeDtypeStruct((t, D_MODEL), BF16),
                   jax.ShapeDtypeStruct((1, D_HEAD), F32)], compiler_params=_cp(),
    )(o, proj, gn, dout)


def _sb_fwd(proj):
    t = proj.shape[0]
    qblk = min(SB_QBLOCK, t)
    scale = 1.0 / math.sqrt(D_HEAD)

    hp = SB_HEADS_PER_STEP
    wid = hp * D_HEAD

    def body(q_ref, k_ref, v_ref, z_ref, o_ref, og_ref, l_ref, qb, kb, vb):
        for hh in range(hp):
            hs = slice(hh * D_HEAD, (hh + 1) * D_HEAD)
            qb[hh] = q_ref[:, hs].astype(BF16)
            kb[hh] = k_ref[:, hs].astype(BF16)
            vb[hh] = v_ref[:, hs].astype(BF16)
        ri = lax.broadcasted_iota(jnp.int32, (qblk, SB_BLOCK), 0)
        ci = lax.broadcasted_iota(jnp.int32, (qblk, SB_BLOCK), 1)
        r2 = lax.broadcasted_iota(jnp.int32, (SB_BLOCK, SB_BLOCK), 0)
        c2 = lax.broadcasted_iota(jnp.int32, (SB_BLOCK, SB_BLOCK), 1)
        upper = (r2 > c2).astype(BF16)
        nkb = qblk // SB_BLOCK

        def qblock(i, carry):
            rows = pl.ds(pl.multiple_of(i * qblk, qblk), qblk)
            qi = qb[:, rows, :]

            def tile(j, st, on_diagonal):
                acc, c = st
                cols = pl.ds(pl.multiple_of(j * SB_BLOCK, SB_BLOCK), SB_BLOCK)
                z = _dot(qi, kb[:, cols, :], "nt") * scale
                lb = jnp.minimum(z, 0.0) - jnp.log(1.0 + jnp.exp(-jnp.abs(z)))
                lf = lb - z
                if on_diagonal:
                    mask = (j * SB_BLOCK + ci) < (i * qblk + ri)
                    lf = jnp.where(mask, lf, 0.0)
                att = jnp.exp(lb + (_ones_dot(lf, upper) + c))
                if on_diagonal:
                    att = jnp.where(mask, att, 0.0)
                acc = acc + _dot(att.astype(BF16), vb[:, cols, :], "nn")
                return acc, c + jnp.sum(lf, axis=-1, keepdims=True)

            st = (jnp.zeros((hp, qblk, D_HEAD), F32), jnp.zeros((hp, qblk, 1), F32))
            for d in range(nkb):
                st = tile((i + 1) * nkb - 1 - d, st, True)
            acc, c = lax.fori_loop(0, i * nkb, lambda jj, s: tile(i * nkb - 1 - jj, s, False), st)
            l_ref[:, rows, :] = c
            for hh in range(hp):
                hs = slice(hh * D_HEAD, (hh + 1) * D_HEAD)
                zg = z_ref[rows, hs]
                o_ref[rows, hs] = acc[hh]
                og_ref[rows, hs] = (acc[hh] * (zg * _sigmoid(zg))).astype(BF16)
            return carry

        lax.fori_loop(0, t // qblk, qblock, 0)

    def head(off):
        return pl.BlockSpec((t, wid), lambda h: (0, off // wid + h))

    out = pl.BlockSpec((t, wid), lambda h: (0, h))
    return pl.pallas_call(
        body, name="sb_fwd", grid=(N_HEADS // hp,),
        in_specs=[head(C_SBQ), head(C_SBQ + D_MODEL), head(C_SBQ + 2 * D_MODEL), head(C_SBZ)],
        out_specs=[out, out, pl.BlockSpec((hp, t, 1), lambda h: (h, 0, 0))],
        out_shape=[jax.ShapeDtypeStruct((t, D_MODEL), F32), jax.ShapeDtypeStruct((t, D_MODEL), BF16),
                   jax.ShapeDtypeStruct((N_HEADS, t, 1), F32)],
        scratch_shapes=[pltpu.VMEM((hp, t, D_HEAD), BF16)] * 3, compiler_params=_cp(),
    )(proj, proj, proj, proj)


def _sb_bwd(proj, o, ltot, dog, after=None):
    t = proj.shape[0]
    qblk = min(SB_QBLOCK, t)
    scale = 1.0 / math.sqrt(D_HEAD)

    hp = SB_HEADS_PER_STEP
    wid = hp * D_HEAD

    def body(q_ref, k_ref, v_ref, z_ref, o_ref, l_ref, d_ref, *rest):
        dq_ref, dk_ref, dv_ref, dz_ref, qb, kb, vb, dob, dk_scr, dv_scr = rest[-10:]
        for hh in range(hp):
            hs = slice(hh * D_HEAD, (hh + 1) * D_HEAD)
            qb[hh] = q_ref[:, hs].astype(BF16)
            kb[hh] = k_ref[:, hs].astype(BF16)
            vb[hh] = v_ref[:, hs].astype(BF16)
            zg = z_ref[:, hs]
            sg = _sigmoid(zg)
            dgo = d_ref[:, hs]
            dob[hh] = (dgo * (zg * sg)).astype(BF16)
            dz_ref[:, hs] = (dgo * o_ref[:, hs] * (sg * (1.0 + zg * (1.0 - sg)))).astype(BF16)
        dk_scr[...] = jnp.zeros_like(dk_scr)
        dv_scr[...] = jnp.zeros_like(dv_scr)
        ri = lax.broadcasted_iota(jnp.int32, (qblk, SB_BLOCK), 0)
        ci = lax.broadcasted_iota(jnp.int32, (qblk, SB_BLOCK), 1)
        r2 = lax.broadcasted_iota(jnp.int32, (SB_BLOCK, SB_BLOCK), 0)
        c2 = lax.broadcasted_iota(jnp.int32, (SB_BLOCK, SB_BLOCK), 1)
        upper = (r2 > c2).astype(BF16)
        below = (r2 < c2).astype(BF16)

        def qblock(i, carry):
            rows = pl.ds(pl.multiple_of(i * qblk, qblk), qblk)
            qi = qb[:, rows, :]
            d_o = dob[:, rows, :]
            ltot = l_ref[:, rows, :]

            def tile(j, st, on_diagonal):
                dq, cpre, ce = st
                cols = pl.ds(pl.multiple_of(j * SB_BLOCK, SB_BLOCK), SB_BLOCK)
                kj, vj = kb[:, cols, :], vb[:, cols, :]
                z = _dot(qi, kj, "nt") * scale
                lb = jnp.minimum(z, 0.0) - jnp.log(1.0 + jnp.exp(-jnp.abs(z)))
                lf = lb - z
                if on_diagonal:
                    mask = (j * SB_BLOCK + ci) < (i * qblk + ri)
                    lf = jnp.where(mask, lf, 0.0)
                tile_sum = jnp.sum(lf, axis=-1, keepdims=True)
                att = jnp.exp(lb + ((ltot - cpre - tile_sum) + _ones_dot(lf, upper)))
                if on_diagonal:
                    att = jnp.where(mask, att, 0.0)
                e = _dot(d_o, vj, "nt") * att
                dlf = ce + _ones_dot(e, below)
                dzz = e - (e + dlf) * jnp.exp(lb)
                if on_diagonal:
                    dzz = jnp.where(mask, dzz, 0.0)
                dzz = dzz.astype(BF16)
                dq = dq + _dot(dzz, kj, "nn")
                dk_scr[:, cols, :] += _dot(dzz, qi, "tn")
                dv_scr[:, cols, :] += _dot(att.astype(BF16), d_o, "tn")
                return dq, cpre + tile_sum, ce + jnp.sum(e, axis=-1, keepdims=True)

            nkb = qblk // SB_BLOCK
            zero_col = jnp.zeros((hp, qblk, 1), F32)
            st = lax.fori_loop(0, i * nkb, lambda j, s: tile(j, s, False),
                               (jnp.zeros((hp, qblk, D_HEAD), F32), zero_col, zero_col))
            for d in range(nkb):
                st = tile(i * nkb + d, st, True)
            dq = st[0]
            for hh in range(hp):
                dq_ref[rows, hh * D_HEAD:(hh + 1) * D_HEAD] = (dq[hh] * scale).astype(BF16)
            return carry

        lax.fori_loop(0, t // qblk, qblock, 0)
        for hh in range(hp):
            hs = slice(hh * D_HEAD, (hh + 1) * D_HEAD)
            dk_ref[:, hs] = (dk_scr[hh] * scale).astype(BF16)
            dv_ref[:, hs] = dv_scr[hh].astype(BF16)

    def head(off):
        return pl.BlockSpec((t, wid), lambda h: (0, off // wid + h))

    extra_specs, extra = [], []
    if after is not None:
        extra_specs, extra = [pl.BlockSpec(after.shape, lambda h: (0, 0))], [after]
    return pl.pallas_call(
        body, name="sb_bwd", grid=(N_HEADS // hp,),
        in_specs=[head(C_SBQ), head(C_SBQ + D_MODEL), head(C_SBQ + 2 * D_MODEL), head(C_SBZ), head(0),
                  pl.BlockSpec((hp, t, 1), lambda h: (h, 0, 0)), head(0)] + extra_specs,
        out_specs=[head(0)] * 4, out_shape=[jax.ShapeDtypeStruct((t, D_MODEL), BF16)] * 4,
        scratch_shapes=[pltpu.VMEM((hp, t, D_HEAD), BF16)] * 4 + [pltpu.VMEM((hp, t, D_HEAD), F32)] * 2,
        compiler_params=_cp(),
    )(proj, proj, proj, proj, o, ltot, dog, *extra)


def _mem_fwd(proj, mkv):
    t = proj.shape[0]
    tq = _pick(t, (512, 256))
    m_len = mkv.shape[0]
    scale = 1.0 / math.sqrt(MEM_DH)

    def body(q_ref, z_ref, kv_ref, o_ref, og_ref):
        q = q_ref[...]
        mk = kv_ref[:, :MEM_W].astype(BF16)
        mv = kv_ref[:, MEM_W:].astype(BF16)
        lane = lax.broadcasted_iota(jnp.int32, q.shape, 1) >> 6
        o = jnp.zeros(q.shape, F32)
        for h in range(MEM_HEADS):
            s = _bdot(jnp.where(lane == h, q, 0.0), mk, "nt") * scale
            p = jnp.exp(s - jnp.max(s, axis=-1, keepdims=True))
            p = p / jnp.sum(p, axis=-1, keepdims=True)
            o = o + jnp.where(lane == h, _bdot(p, mv, "nn"), 0.0)
        z = z_ref[...]
        o_ref[...] = o
        og_ref[...] = (o * (z * _sigmoid(z))).astype(BF16)

    out = pl.BlockSpec((tq, MEM_W), lambda i: (i, 0))
    return pl.pallas_call(
        body, name="mem_fwd", grid=(t // tq,),
        in_specs=[pl.BlockSpec((tq, MEM_W), lambda i: (i, C_MQ // MEM_W)),
                  pl.BlockSpec((tq, MEM_W), lambda i: (i, C_MZ // MEM_W)),
                  pl.BlockSpec((m_len, 2 * MEM_W), lambda i: (0, 0))],
        out_specs=[out, out],
        out_shape=[jax.ShapeDtypeStruct((t, MEM_W), F32), jax.ShapeDtypeStruct((t, MEM_W), BF16)],
        compiler_params=_cp(),
    )(proj, proj, mkv)


def _mem_bwd(proj, mkv, o, dog):
    t = proj.shape[0]
    tq = _pick(t, (512, 256))
    m_len = mkv.shape[0]
    scale = 1.0 / math.sqrt(MEM_DH)

    def body(q_ref, z_ref, kv_ref, o_ref, d_ref, dq_ref, dz_ref, dkv_ref):
        @pl.when(pl.program_id(0) == 0)
        def _():
            dkv_ref[...] = jnp.zeros_like(dkv_ref)

        q = q_ref[...]
        z = z_ref[...]
        sg = _sigmoid(z)
        dgo = d_ref[...]
        d_o = dgo * (z * sg)
        dz_ref[...] = (dgo * o_ref[...] * (sg * (1.0 + z * (1.0 - sg)))).astype(BF16)
        mk = kv_ref[:, :MEM_W].astype(BF16)
        mv = kv_ref[:, MEM_W:].astype(BF16)
        lane = lax.broadcasted_iota(jnp.int32, q.shape, 1) >> 6
        klane = lax.broadcasted_iota(jnp.int32, (m_len, MEM_W), 1) >> 6
        dq = jnp.zeros(q.shape, F32)
        dmk = jnp.zeros((m_len, MEM_W), F32)
        dmv = jnp.zeros((m_len, MEM_W), F32)
        for h in range(MEM_HEADS):
            qh = jnp.where(lane == h, q, 0.0)
            doh = jnp.where(lane == h, d_o, 0.0)
            s = _bdot(qh, mk, "nt") * scale
            p = jnp.exp(s - jnp.max(s, axis=-1, keepdims=True))
            p = p / jnp.sum(p, axis=-1, keepdims=True)
            dp = _bdot(doh, mv, "nt")
            ds = p * (dp - jnp.sum(dp * p, axis=-1, keepdims=True)) * scale
            dq = dq + jnp.where(lane == h, _bdot(ds, mk, "nn"), 0.0)
            dmk = dmk + jnp.where(klane == h, _bdot(ds, qh, "tn"), 0.0)
            dmv = dmv + jnp.where(klane == h, _bdot(p, doh, "tn"), 0.0)
        dq_ref[...] = dq.astype(BF16)
        dkv_ref[:, :MEM_W] += dmk
        dkv_ref[:, MEM_W:] += dmv

    blk = pl.BlockSpec((tq, MEM_W), lambda i: (i, 0))
    kv = pl.BlockSpec((m_len, 2 * MEM_W), lambda i: (0, 0))
    return pl.pallas_call(
        body, name="mem_bwd", grid=(t // tq,),
        in_specs=[pl.BlockSpec((tq, MEM_W), lambda i: (i, C_MQ // MEM_W)),
                  pl.BlockSpec((tq, MEM_W), lambda i: (i, C_MZ // MEM_W)), kv, blk, blk],
        out_specs=[blk, blk, kv],
        out_shape=[jax.ShapeDtypeStruct((t, MEM_W), BF16), jax.ShapeDtypeStruct((t, MEM_W), BF16),
                   jax.ShapeDtypeStruct((m_len, 2 * MEM_W), F32)], compiler_params=_cp(),
    )(proj, proj, mkv, o, dog)


_GW = 512


def _merge_fwd(proj, y_dn, y_sb, y_m):
    t = proj.shape[0]
    tb = _pick(t, (256,))
    nc = D_MODEL // _GW

    def body(g1, g2, g3, y1, y2, y3, out_ref):
        out_ref[...] = (_sigmoid(g1[...]) * y1[...] + _sigmoid(g2[...]) * y2[...] + _sigmoid(g3[...]) * y3[...]).astype(BF16)

    def gate(kb):
        return pl.BlockSpec((tb, _GW), lambda i, c: (i, C_GATES // _GW + kb * nc + c))

    blk = pl.BlockSpec((tb, _GW), lambda i, c: (i, c))
    return pl.pallas_call(
        body, name="merge_fwd", grid=(t // tb, nc), in_specs=[gate(0), gate(1), gate(2), blk, blk, blk],
        out_specs=blk, out_shape=jax.ShapeDtypeStruct((t, D_MODEL), BF16), compiler_params=_cp(),
    )(proj, proj, proj, y_dn, y_sb, y_m)


def _merge_bwd(proj, y_dn, y_sb, y_m, dm):
    t = proj.shape[0]
    tb = _pick(t, (256,))
    nc = D_MODEL // _GW

    def body(g1, g2, g3, y1, y2, y3, dm_ref, d1, d2, d3, dg1, dg2, dg3):
        d = dm_ref[...]
        for g, y, dy, dg in ((g1, y1, d1, dg1), (g2, y2, d2, dg2), (g3, y3, d3, dg3)):
            s = _sigmoid(g[...])
            dy[...] = (d * s).astype(BF16)
            dg[...] = (d * y[...] * (s * (1.0 - s))).astype(BF16)

    def gate(kb):
        return pl.BlockSpec((tb, _GW), lambda i, c: (i, C_GATES // _GW + kb * nc + c))

    blk = pl.BlockSpec((tb, _GW), lambda i, c: (i, c))
    act = jax.ShapeDtypeStruct((t, D_MODEL), BF16)
    return pl.pallas_call(
        body, name="merge_bwd", grid=(t // tb, nc), in_specs=[gate(0), gate(1), gate(2), blk, blk, blk, blk],
        out_specs=[blk] * 6, out_shape=[act] * 6, compiler_params=_cp(),
    )(proj, proj, proj, y_dn, y_sb, y_m, dm)


def _final_loss(x, mo, g, tgt):
    t, d = x.shape
    tb = _pick(t, (256,))

    def body(x_ref, mo_ref, g_ref, t_ref, do_ref, dob_ref, loss_ref, dg_ref):
        @pl.when(pl.program_id(0) == 0)
        def _():
            loss_ref[...] = jnp.zeros_like(loss_ref)
            dg_ref[...] = jnp.zeros_like(dg_ref)

        out = x_ref[...] + mo_ref[...]
        r = lax.rsqrt(jnp.mean(out * out, axis=-1, keepdims=True) + NORM_EPS)
        xhat = out * r
        gv = g_ref[...]
        err = xhat * gv - t_ref[...]
        per_tok = jnp.mean(err * err, axis=-1, keepdims=True)
        loss_ref[...] += 0.5 * jnp.sum(per_tok, axis=0, keepdims=True)
        dy = err * (1.0 / d)
        dg_ref[...] += jnp.sum(dy * xhat, axis=0, keepdims=True)
        dxh = dy * gv
        dout = r * (dxh - xhat * jnp.mean(dxh * xhat, axis=-1, keepdims=True))
        do_ref[...] = dout
        dob_ref[...] = dout.astype(BF16)

    row = pl.BlockSpec((tb, d), lambda i: (i, 0))
    vec = pl.BlockSpec((1, d), lambda i: (0, 0))
    return pl.pallas_call(
        body, name="final_loss", grid=(t // tb,), in_specs=[row, row, vec, row],
        out_specs=[row, row, pl.BlockSpec((1, 128), lambda i: (0, 0)), vec],
        out_shape=[jax.ShapeDtypeStruct((t, d), F32), jax.ShapeDtypeStruct((t, d), BF16),
                   jax.ShapeDtypeStruct((1, 128), F32), jax.ShapeDtypeStruct((1, d), F32)],
        compiler_params=_cp(),
    )(x, mo, g, tgt)


def _cast_bf16(a, name):
    r, c = a.shape
    tb = _pick(r, (128, 496, 240))

    def body(a_ref, o_ref):
        o_ref[...] = a_ref[...].astype(BF16)

    blk = pl.BlockSpec((tb, c), lambda i: (i, 0))
    return pl.pallas_call(body, name=name, grid=(r // tb,), in_specs=[blk], out_specs=blk,
                          out_shape=jax.ShapeDtypeStruct((r, c), BF16), compiler_params=_cp())(a)


WIN_START = (0, 23, 45, 68)
_S1_LO, _S1_HI = 1148, 1164
_S1_BA_POS = SHARD_PAD - 128


def _to_window(x, s):
    if s == 0:
        return x
    if s in (2, 3):
        return pltpu.roll(x, 120 if s == 2 else 124, 1)
    pos = lax.broadcasted_iota(jnp.int32, x.shape, 1)
    head = pltpu.roll(x, 4, 1)
    tail = pltpu.roll(x, SHARD_PAD - 12, 1)
    ba = jnp.where(pos < _S1_BA_POS + (_S1_HI - _S1_LO), pltpu.roll(x, _S1_BA_POS - _S1_LO, 1), 0.0)
    return jnp.where(pos < _S1_LO + 4, head, jnp.where(pos < _S1_BA_POS, tail, ba))


def _from_window(g, s):
    if s == 0:
        return g
    if s in (2, 3):
        return pltpu.roll(g, SHARD_PAD - (120 if s == 2 else 124), 1)
    col = lax.broadcasted_iota(jnp.int32, g.shape, 1)
    head = pltpu.roll(g, SHARD_PAD - 4, 1)
    tail = pltpu.roll(g, 12, 1)
    ba = pltpu.roll(g, SHARD_PAD - (_S1_BA_POS - _S1_LO), 1)
    return jnp.where(col < _S1_LO, head, jnp.where(col < _S1_HI, ba, tail))


def _cast_to_window(w, shard, name):
    r, c = w.shape
    tb = _pick(r, (128,))

    def body(s_ref, w_ref, o_ref, pad_scr):
        pad_scr[...] = jnp.zeros_like(pad_scr)
        pad_scr[:, :c] = w_ref[...]
        x = pad_scr[...]
        for s in range(N_SHARD):
            @pl.when(s_ref[0] == s)
            def _():
                o_ref[...] = _to_window(x, s).astype(BF16)

    return pl.pallas_call(
        body, name=name,
        grid_spec=pltpu.PrefetchScalarGridSpec(
            num_scalar_prefetch=1, grid=(r // tb,),
            in_specs=[pl.BlockSpec((tb, c), lambda i, s: (i, 0))],
            out_specs=pl.BlockSpec((tb, SHARD_PAD), lambda i, s: (i, 0)),
            scratch_shapes=[pltpu.VMEM((tb, SHARD_PAD), F32)]),
        out_shape=jax.ShapeDtypeStruct((r, SHARD_PAD), BF16), compiler_params=_cp(),
    )(shard, w)


def _pair_add(g, recv, c_idx, name):
    n, r, c = g.shape
    half = r // 2
    tb = _pick(half, (128, 240))
    nb = half // tb

    def body(c_ref, g_ref, r_ref, o_ref):
        o_ref[...] = (g_ref[...].astype(F32) + r_ref[...].astype(F32)).astype(BF16)

    blk = pl.BlockSpec((n, tb, c), lambda i, c_ref: (0, i, 0))
    return pl.pallas_call(
        body, name=name,
        grid_spec=pltpu.PrefetchScalarGridSpec(
            num_scalar_prefetch=1, grid=(nb,),
            in_specs=[pl.BlockSpec((n, tb, c), lambda i, c_ref: (0, c_ref[0] * nb + i, 0)), blk], out_specs=blk),
        out_shape=jax.ShapeDtypeStruct((n, half, c), BF16), compiler_params=_cp(),
    )(c_idx, g, recv)


def _chip_sum(parts, by_chip, place, name):
    n, h, c = parts.shape
    tb = _pick(h, (128, 240))
    nb = h // tb

    def body(p_ref, mine_ref, *rest):
        others, o_ref = rest[:n], rest[n]
        me = jnp.zeros((tb, c), jnp.int32) + p_ref[0]
        acc = None
        for q in range(n):
            term = jnp.where(me == q, mine_ref[...], others[q][...]).astype(F32)
            acc = term if acc is None else acc + term
        o_ref[...] = acc

    def other(q):
        return pl.BlockSpec((None, tb, c), lambda i, p: (jnp.where(p[0] == q, (q + 1) % n, q), i, 0))

    return pl.pallas_call(
        body, name=name,
        grid_spec=pltpu.PrefetchScalarGridSpec(
            num_scalar_prefetch=1, grid=(nb,),
            in_specs=[pl.BlockSpec((None, tb, c), lambda i, p: (p[0], i, 0))] + [other(q) for q in range(n)],
            out_specs=pl.BlockSpec((tb, c), lambda i, p: (p[1] * nb + i, 0))),
        out_shape=jax.ShapeDtypeStruct((2 * h, c), F32), compiler_params=_cp(),
    )(place, parts, *([by_chip] * n))


def _adamw_math(w, g, m, v):
    m = ADAM_B1 * m + (1.0 - ADAM_B1) * g
    v = ADAM_B2 * v + (1.0 - ADAM_B2) * (g * g)
    m_hat = m / (1.0 - ADAM_B1 ** ADAM_STEP)
    v_hat = v / (1.0 - ADAM_B2 ** ADAM_STEP)
    delta = -ADAM_LR * (m_hat / (jnp.sqrt(v_hat) + ADAM_EPS) + ADAM_WD * w)
    return delta, m, v


def _adamw(w, g, m, v, name):
    r, c = w.shape
    tb = _pick(r, (128, 496, 240))

    def body(w_ref, g_ref, m_ref, v_ref, go_ref, d_ref, mo_ref, vo_ref):
        gv = g_ref[...]
        d, mn, vn = _adamw_math(w_ref[...], gv, m_ref[...], v_ref[...])
        go_ref[...] = gv
        d_ref[...] = d
        mo_ref[...] = mn
        vo_ref[...] = vn

    blk = pl.BlockSpec((tb, c), lambda i: (i, 0))
    return pl.pallas_call(
        body, name=name, grid=(r // tb,), in_specs=[blk] * 4, out_specs=[blk] * 4,
        out_shape=[jax.ShapeDtypeStruct((r, c), F32)] * 4, compiler_params=_cp(),
    )(w, g, m, v)


def _adamw_window(w, g_win, m, v, shard, name):
    r, c = w.shape
    tb = _pick(r, (128,))

    def body(s_ref, w_ref, g_ref, m_ref, v_ref, go_ref, d_ref, mo_ref, vo_ref, g_scr):
        gw = g_ref[...]
        for s in range(N_SHARD):
            @pl.when(s_ref[0] == s)
            def _():
                g_scr[...] = _from_window(gw, s)

        gv = g_scr[:, :c]
        d, mn, vn = _adamw_math(w_ref[...], gv, m_ref[...], v_ref[...])
        go_ref[...] = gv
        d_ref[...] = d
        mo_ref[...] = mn
        vo_ref[...] = vn

    blk = pl.BlockSpec((tb, c), lambda i, s: (i, 0))
    return pl.pallas_call(
        body, name=name,
        grid_spec=pltpu.PrefetchScalarGridSpec(
            num_scalar_prefetch=1, grid=(r // tb,),
            in_specs=[blk, pl.BlockSpec((tb, SHARD_PAD), lambda i, s: (i, 0)), blk, blk], out_specs=[blk] * 4,
            scratch_shapes=[pltpu.VMEM((tb, SHARD_PAD), F32)]),
        out_shape=[jax.ShapeDtypeStruct((r, c), F32)] * 4, compiler_params=_cp(),
    )(shard, w, g_win, m, v)


def _small_update(gathered, w, m, v):
    def body(p_ref, w_ref, m_ref, v_ref, g_ref, d_ref, mo_ref, vo_ref):
        g = p_ref[0]
        for i in range(1, N_DEV):
            g = g + p_ref[i]
        d, mn, vn = _adamw_math(w_ref[...], g, m_ref[...], v_ref[...])
        g_ref[...] = g
        d_ref[...] = d
        mo_ref[...] = mn
        vo_ref[...] = vn

    full = pl.BlockSpec((S_ROWS, 128), lambda i: (0, 0))
    return pl.pallas_call(
        body, name="small_update", grid=(1,),
        in_specs=[pl.BlockSpec((N_DEV, S_ROWS, 128), lambda i: (0, 0, 0)), full, full, full], out_specs=[full] * 4,
        out_shape=[jax.ShapeDtypeStruct((S_ROWS, 128), F32)] * 4, compiler_params=_cp(),
    )(gathered, w, m, v)


_ANY = pl.BlockSpec(memory_space=pl.ANY)


def _place():
    x, y, c = lax.axis_index("x"), lax.axis_index("y"), lax.axis_index("c")
    chips = [(1 - x, y), (x, 1 - y), (1 - x, 1 - y)]
    return x, y, c, chips


def _pair_reduce_send(grads, tag):
    n = len(grads)

    def body(*refs):
        ins, outs = refs[:n], refs[n:2 * n]
        send_sems, recv_sems = refs[2 * n:]
        x, y, c, _ = _place()
        sibling = (x, y, 1 - c)
        cps = []
        for a in range(n):
            half = ins[a].shape[1] // 2
            theirs = pl.ds(pl.multiple_of((1 - c) * half, 8), half)
            cp = pltpu.make_async_remote_copy(
                src_ref=ins[a].at[:, theirs], dst_ref=outs[a], send_sem=send_sems.at[a], recv_sem=recv_sems.at[a],
                device_id=sibling, device_id_type=MESH)
            cp.start()
            cps.append(cp)
        for cp in cps:
            cp.wait()

    return pl.pallas_call(
        body, name="pair_reduce_send_" + tag, in_specs=[_ANY] * n, out_specs=[_ANY] * n,
        out_shape=[jax.ShapeDtypeStruct((g.shape[0], g.shape[1] // 2, g.shape[2]), g.dtype) for g in grads],
        scratch_shapes=[pltpu.SemaphoreType.DMA((n,)), pltpu.SemaphoreType.DMA((n,))],
        compiler_params=pltpu.CompilerParams(has_side_effects=True),
    )(*grads)


_HBM = pl.BlockSpec(memory_space=pltpu.HBM)
_SEM = pl.BlockSpec(memory_space=pltpu.SEMAPHORE)
_DATAFLOW = pltpu.SideEffectType.DATAFLOW_SIDE_EFFECTING


def _chip_exchange_copies(ins, lands, send_sems, recv_sems):
    x, y, c, chips = _place()
    me = 2 * x + y
    cps = []
    for a in range(len(ins)):
        for j, (qx, qy) in enumerate(chips):
            cps.append(pltpu.make_async_remote_copy(
                src_ref=ins[a].at[2 * qx + qy], dst_ref=lands[a].at[me], send_sem=send_sems.at[3 * a + j],
                recv_sem=recv_sems.at[3 * a + j], device_id=(qx, qy, c), device_id_type=MESH))
    return cps


def _chip_exchange_start(parts, tag):
    n = len(parts)

    def body(*refs):
        ins, lands = refs[:n], refs[n:2 * n]
        send_sems, recv_sems = refs[2 * n:2 * n + 2]
        token = refs[4 * n + 2]
        for cp in _chip_exchange_copies(ins, lands, send_sems, recv_sems):
            cp.start()
        token[...] = jnp.zeros_like(token)

    hbm = [pltpu.HBM(p.shape, p.dtype) for p in parts]
    lands = [pltpu.with_memory_space_constraint(lax.empty(p.shape, p.dtype), pltpu.HBM) for p in parts]
    res = pl.pallas_call(
        body, name="chip_exchange_start_" + tag,
        out_shape=(pltpu.SemaphoreType.DMA((3 * n,)), pltpu.SemaphoreType.DMA((3 * n,)), *hbm, *hbm,
                   jax.ShapeDtypeStruct((8, 128), F32)),
        in_specs=[_HBM] * (2 * n), out_specs=(_SEM, _SEM, *([_HBM] * (2 * n)), pl.BlockSpec(memory_space=pltpu.VMEM)),
        input_output_aliases={a: 2 + a for a in range(2 * n)},
        compiler_params=pltpu.CompilerParams(has_side_effects=_DATAFLOW),
    )(*[pltpu.with_memory_space_constraint(p, pltpu.HBM) for p in parts], *lands)
    return res[0], res[1], res[2:2 + n], res[2 + n:2 + 2 * n], res[2 + 2 * n]


def _chip_exchange_wait(send_sems, recv_sems, parts, lands, after, tag):
    n = len(parts)

    def body(*refs):
        ins, land_refs = refs[:n], refs[n:2 * n]
        s_sems, r_sems = refs[2 * n:2 * n + 2]
        for cp in _chip_exchange_copies(ins, land_refs, s_sems, r_sems):
            cp.wait_send()
            cp.wait_recv()

    hbm = [pltpu.HBM(p.shape, p.dtype) for p in parts]
    res = pl.pallas_call(
        body, name="chip_exchange_wait_" + tag, out_shape=(*hbm, *hbm),
        in_specs=[_HBM] * (2 * n) + [_SEM, _SEM, _ANY], out_specs=tuple([_HBM] * (2 * n)),
        input_output_aliases={a: a for a in range(2 * n)},
        compiler_params=pltpu.CompilerParams(has_side_effects=_DATAFLOW),
    )(*parts, *lands, send_sems, recv_sems, after)
    return res[:n], res[n:]


def _halves_ici_copies(srcs, lands, send_sems, recv_sems):
    x, y, c, chips = _place()
    me = 2 * x + y
    cps = []
    for a, src in enumerate(srcs):
        half = src.shape[0] // 2
        mine = pl.ds(pl.multiple_of(c * half, 16), half)
        for j, (qx, qy) in enumerate(chips):
            cps.append(pltpu.make_async_remote_copy(
                src_ref=src.at[mine], dst_ref=lands[a].at[me, mine], send_sem=send_sems.at[3 * a + j],
                recv_sem=recv_sems.at[3 * a + j], device_id=(qx, qy, c), device_id_type=MESH))
    return cps


def _halves_d2d_copies(lands, send_sems, recv_sems):
    x, y, c, chips = _place()
    cps = []
    for a, land in enumerate(lands):
        half = land.shape[1] // 2
        mine = pl.ds(pl.multiple_of(c * half, 16), half)
        for j, (qx, qy) in enumerate(chips):
            region = land.at[2 * qx + qy, mine]
            cps.append(pltpu.make_async_remote_copy(
                src_ref=region, dst_ref=region, send_sem=send_sems.at[3 * a + j], recv_sem=recv_sems.at[3 * a + j],
                device_id=(x, y, 1 - c), device_id_type=MESH))
    return cps


def _halves_gather_start(shards):
    n = len(shards)

    def body(*refs):
        srcs, lands = refs[:n], refs[n:2 * n]
        send_sems, recv_sems = refs[2 * n:2 * n + 2]
        for cp in _halves_ici_copies(srcs, lands, send_sems, recv_sems):
            cp.start()
        refs[4 * n + 2][...] = jnp.zeros((8, 128), F32)

    hbm_s = [pltpu.HBM(s.shape, s.dtype) for s in shards]
    hbm_l = [pltpu.HBM((N_SHARD,) + s.shape, s.dtype) for s in shards]
    lands = [pltpu.with_memory_space_constraint(lax.empty((N_SHARD,) + s.shape, s.dtype), pltpu.HBM) for s in shards]
    res = pl.pallas_call(
        body, name="halves_gather_start",
        out_shape=(pltpu.SemaphoreType.DMA((3 * n,)), pltpu.SemaphoreType.DMA((3 * n,)), *hbm_s, *hbm_l,
                   jax.ShapeDtypeStruct((8, 128), F32)),
        in_specs=[_HBM] * (2 * n), out_specs=(_SEM, _SEM, *([_HBM] * (2 * n)), pl.BlockSpec(memory_space=pltpu.VMEM)),
        input_output_aliases={a: 2 + a for a in range(2 * n)},
        compiler_params=pltpu.CompilerParams(has_side_effects=_DATAFLOW),
    )(*[pltpu.with_memory_space_constraint(s, pltpu.HBM) for s in shards], *lands)
    return res[0], res[1], res[2:2 + n], res[2 + n:2 + 2 * n], res[2 + 2 * n]


def _halves_gather_forward(send1, recv1, shards, lands, after):
    n = len(shards)

    def body(*refs):
        srcs, land_refs = refs[:n], refs[n:2 * n]
        s1, r1 = refs[2 * n:2 * n + 2]
        outs = refs[2 * n + 2 + len(after):]
        s2, r2 = outs[0], outs[1]
        for cp in _halves_ici_copies(srcs, land_refs, s1, r1):
            cp.wait_send()
            cp.wait_recv()
        for cp in _halves_d2d_copies(land_refs, s2, r2):
            cp.start()

    hbm_s = [pltpu.HBM(s.shape, s.dtype) for s in shards]
    hbm_l = [pltpu.HBM(l.shape, l.dtype) for l in lands]
    res = pl.pallas_call(
        body, name="halves_gather_forward",
        out_shape=(pltpu.SemaphoreType.DMA((3 * n,)), pltpu.SemaphoreType.DMA((3 * n,)), *hbm_s, *hbm_l),
        in_specs=[_HBM] * (2 * n) + [_SEM, _SEM] + [_ANY] * len(after),
        out_specs=(_SEM, _SEM, *([_HBM] * (2 * n))), input_output_aliases={a: 2 + a for a in range(2 * n)},
        compiler_params=pltpu.CompilerParams(has_side_effects=_DATAFLOW),
    )(*shards, *lands, send1, recv1, *after)
    return res[0], res[1], res[2 + n:2 + 2 * n]


def _halves_gather_wait(send2, recv2, lands):
    n = len(lands)

    def body(*refs):
        land_refs = refs[:n]
        s2, r2 = refs[n:n + 2]
        for cp in _halves_d2d_copies(land_refs, s2, r2):
            cp.wait_send()
            cp.wait_recv()

    hbm_l = [pltpu.HBM(l.shape, l.dtype) for l in lands]
    res = pl.pallas_call(
        body, name="halves_gather_wait", out_shape=tuple(hbm_l),
        in_specs=[_HBM] * n + [_SEM, _SEM], out_specs=tuple([_HBM] * n),
        input_output_aliases={a: a for a in range(n)},
        compiler_params=pltpu.CompilerParams(has_side_effects=_DATAFLOW),
    )(*lands, send2, recv2)
    return list(res)


def _shard_gather_copies(src, land, send_sems, recv_sems):
    x, y, c, chips = _place()
    me = 2 * x + y
    return [pltpu.make_async_remote_copy(
        src_ref=src, dst_ref=land.at[me], send_sem=send_sems.at[j], recv_sem=recv_sems.at[j],
        device_id=(qx, qy, c), device_id_type=MESH) for j, (qx, qy) in enumerate(chips)]


def _shard_gather_start(shard_arr, after):
    def body(src, land, after_ref, send_sems, recv_sems, src_thru, land_thru, token):
        for cp in _shard_gather_copies(src, land, send_sems, recv_sems):
            cp.start()
        token[...] = jnp.zeros_like(token)

    land_shape = (N_SHARD,) + shard_arr.shape
    land = pltpu.with_memory_space_constraint(lax.empty(land_shape, shard_arr.dtype), pltpu.HBM)
    return pl.pallas_call(
        body, name="shard_gather_start",
        out_shape=(pltpu.SemaphoreType.DMA((N_SHARD - 1,)), pltpu.SemaphoreType.DMA((N_SHARD - 1,)),
                   pltpu.HBM(shard_arr.shape, shard_arr.dtype), pltpu.HBM(land_shape, shard_arr.dtype),
                   jax.ShapeDtypeStruct((8, 128), F32)),
        in_specs=[_HBM, _HBM, _ANY], out_specs=(_SEM, _SEM, _HBM, _HBM, pl.BlockSpec(memory_space=pltpu.VMEM)),
        input_output_aliases={0: 2, 1: 3},
        compiler_params=pltpu.CompilerParams(has_side_effects=_DATAFLOW),
    )(pltpu.with_memory_space_constraint(shard_arr, pltpu.HBM), land, after)


def _shard_gather_wait(send_sems, recv_sems, shard_arr, land, after):
    def body(src, land_ref, s_sems, r_sems, after_ref, src_out, land_out):
        for cp in _shard_gather_copies(src, land_ref, s_sems, r_sems):
            cp.wait_send()
            cp.wait_recv()

    return pl.pallas_call(
        body, name="shard_gather_wait",
        out_shape=(pltpu.HBM(shard_arr.shape, shard_arr.dtype), pltpu.HBM(land.shape, land.dtype)),
        in_specs=[_HBM, _HBM, _SEM, _SEM, _ANY], out_specs=(_HBM, _HBM), input_output_aliases={0: 0, 1: 1},
        compiler_params=pltpu.CompilerParams(has_side_effects=_DATAFLOW),
    )(shard_arr, land, send_sems, recv_sems, after)


def _pair_allgather(fulls, tag):
    n = len(fulls)

    def body(*refs):
        outs = refs[n:2 * n]
        send_sems, recv_sems = refs[2 * n:]
        x, y, c, _ = _place()
        sibling = (x, y, 1 - c)
        cps = []
        for a in range(n):
            half = outs[a].shape[0] // 2
            mine = outs[a].at[pl.ds(pl.multiple_of(c * half, 8), half)]
            cp = pltpu.make_async_remote_copy(
                src_ref=mine, dst_ref=mine, send_sem=send_sems.at[a], recv_sem=recv_sems.at[a],
                device_id=sibling, device_id_type=MESH)
            cp.start()
            cps.append(cp)
        for a in range(n):
            half = outs[a].shape[0] // 2
            theirs = outs[a].at[pl.ds(pl.multiple_of((1 - c) * half, 8), half)]
            pltpu.make_async_remote_copy(
                src_ref=theirs, dst_ref=theirs, send_sem=send_sems.at[a], recv_sem=recv_sems.at[a],
                device_id=sibling, device_id_type=MESH).wait_recv()
        for cp in cps:
            cp.wait_send()

    return pl.pallas_call(
        body, name="pair_allgather_" + tag, in_specs=[_ANY] * n, out_specs=[_ANY] * n,
        out_shape=[jax.ShapeDtypeStruct(f.shape, f.dtype) for f in fulls],
        input_output_aliases={a: a for a in range(n)},
        scratch_shapes=[pltpu.SemaphoreType.DMA((n,)), pltpu.SemaphoreType.DMA((n,))],
        compiler_params=pltpu.CompilerParams(has_side_effects=True),
    )(*fulls)


def _allgather_small(slab, after):
    def body(s_ref, after_ref, out_ref, send_sems, recv_sems):
        x, y, c, _ = _place()
        me = 4 * x + 2 * y + c
        out_ref[me] = s_ref[...]
        cps = []
        for mask in range(1, N_DEV):
            peer = (x ^ (mask >> 2), y ^ ((mask >> 1) & 1), c ^ (mask & 1))
            cp = pltpu.make_async_remote_copy(
                src_ref=s_ref, dst_ref=out_ref.at[me], send_sem=send_sems.at[mask - 1], recv_sem=recv_sems.at[mask - 1],
                device_id=peer, device_id_type=MESH)
            cp.start()
            cps.append(cp)
        for mask in range(1, N_DEV):
            peer = (x ^ (mask >> 2), y ^ ((mask >> 1) & 1), c ^ (mask & 1))
            dst = out_ref.at[4 * peer[0] + 2 * peer[1] + peer[2]]
            pltpu.make_async_remote_copy(
                src_ref=dst, dst_ref=dst, send_sem=send_sems.at[mask - 1], recv_sem=recv_sems.at[mask - 1],
                device_id=peer, device_id_type=MESH).wait_recv()
        for cp in cps:
            cp.wait_send()

    vm = pl.BlockSpec(memory_space=pltpu.VMEM)
    return pl.pallas_call(
        body, name="allgather_small", in_specs=[vm, _ANY], out_specs=vm,
        out_shape=jax.ShapeDtypeStruct((N_DEV,) + slab.shape, slab.dtype),
        scratch_shapes=[pltpu.SemaphoreType.DMA((N_DEV - 1,)), pltpu.SemaphoreType.DMA((N_DEV - 1,))],
        compiler_params=pltpu.CompilerParams(has_side_effects=True),
    )(slab, after)


def _pack_b(w_mem_kv, w_br_dn, w_br_sb, w_br_mem, w_out):
    return jnp.concatenate([w_mem_kv.reshape(128, D_MODEL), w_br_dn, w_br_sb, w_br_mem.reshape(64, D_MODEL), w_out],
                           axis=0)


def _conv_slab(conv_w):
    return jnp.pad(conv_w.reshape(3, D_MODEL), ((0, 29), (0, 0)))


def _unpack_b(slab):
    return (slab[B_MEMKV:B_BRDN].reshape(1, 256, 512), slab[B_BRDN:B_BRSB].reshape(1, 256, D_MODEL),
            slab[B_BRSB:B_BRMEM].reshape(1, 256, D_MODEL), slab[B_BRMEM:B_OUT].reshape(1, 256, 256),
            slab[B_OUT:B_CONV].reshape(1, 256, D_MODEL))


def _conv_rows(conv_full):
    return conv_full.reshape(4 * CONV_BLOCKS, 128)


def _conv_shard_rows(conv_shard, shard):
    own = CONV_BLOCKS // N_SHARD
    blocks = lax.dynamic_update_slice(jnp.zeros((4, CONV_BLOCKS, 128), F32), conv_shard.reshape(4, own, 128),
                                      (0, own * shard, 0))
    return blocks.reshape(4 * CONV_BLOCKS, 128)


def _conv_shard_of(rows, shard):
    own = CONV_BLOCKS // N_SHARD
    blocks = lax.dynamic_slice(rows.reshape(4, CONV_BLOCKS, 128), (0, own * shard, 0), (4, own, 128))
    return blocks.reshape(1, 4, own * 128)


def _pack_small(norm_g, mem_norm_g, final_g, dn_norm_g, a_log, dt_bias, conv_rows, loss=None):
    def row(v):
        v = v.reshape(1, -1).astype(F32)
        return jnp.pad(v, ((0, 0), (0, 128 - v.shape[1])))

    loss_row = row(jnp.zeros((1,), F32) if loss is None else jnp.reshape(loss, (1,)))
    rid = lax.broadcasted_iota(jnp.int32, (8, 128), 0) + S_DNNORM
    tile = jnp.where(rid == S_DNNORM, dn_norm_g.reshape(1, 128), jnp.where(
        rid == S_ALOG, row(a_log), jnp.where(rid == S_DTB, row(dt_bias), jnp.where(rid == S_LOSS, loss_row, 0.0))))
    return jnp.concatenate([norm_g.reshape(8, 128), mem_norm_g.reshape(8, 128), final_g.reshape(8, 128), tile,
                            conv_rows], axis=0)


def _unpack_small(slab, shard):
    return (slab[S_NORM:S_NORM + 8].reshape(1, D_MODEL), slab[S_MEMNORM:S_MEMNORM + 8].reshape(1, D_MODEL),
            slab[S_FINAL:S_FINAL + 8].reshape(D_MODEL), slab[S_DNNORM].reshape(1, 128),
            slab[S_ALOG, :N_HEADS].reshape(1, N_HEADS), slab[S_DTB, :N_HEADS].reshape(1, N_HEADS),
            _conv_shard_of(slab[S_CONV:], shard))


def _windows_to_w_r(win):
    b = 128
    s0, s1, s2, s3 = win[0], win[1], win[2], win[3]
    e1, e2, e3 = WIN_START[1] * b, WIN_START[2] * b, WIN_START[3] * b
    n1, n2 = e2 - e1, e3 - e2
    return jnp.concatenate([
        s0[:, :e1], s0[:, e1:e1 + b] + s1[:, :b],
        s1[:, b:n1], s1[:, n1:n1 + b] + s2[:, :b],
        s2[:, b:n2], s2[:, n2:n2 + b] + s3[:, :b],
        s3[:, b:], s1[:, _S1_BA_POS:], jnp.zeros((win.shape[1], W_R - C_BA - b), win.dtype)], axis=1)


def _dproj_windows(dproj_r):
    b = 128
    pieces = []
    for s in range(N_SHARD):
        lo = WIN_START[s] * b
        if s == 1:
            pieces += [dproj_r[:, lo:lo + _S1_BA_POS], dproj_r[:, C_BA:C_BA + b]]
        else:
            pieces.append(dproj_r[:, lo:lo + SHARD_PAD])
    return jnp.concatenate(pieces, axis=1)


def _local_step(x, mem, tgt, norm_g, mem_norm_g, w_r, w_sh, conv_w, a_log, dt_bias, dn_norm_g, proj_weights, final_g,
                on_early=None, after_gather=None, h=None):
    t = x.shape[0]
    final_row = final_g.reshape(1, D_MODEL)
    lanes_8_16 = ((0, 0), (N_HEADS, 128 - 2 * N_HEADS))
    alog_row = jnp.pad(a_log.reshape(1, N_HEADS), lanes_8_16)
    dtb_row = jnp.pad(dt_bias.reshape(1, N_HEADS), lanes_8_16)

    if h is None:
        h = _rmsnorm_fwd(x, norm_g, "norm_fwd")
    proj = _mm(h, w_r, "nn", "in_proj", after=after_gather, tm_max=2048)
    qkv = _dn_prep_fwd(proj, conv_w)
    beta_t, g_t = _dn_gate_fwd(proj, alog_row, dtb_row)
    dn_u, dn_w, dn_qd, dn_kd, dn_a, tinv_all, dn_el = _dn_intra_fwd(qkv, beta_t, g_t)
    o_dn, dn_vn, s_all = _dn_scan_fwd(dn_u, dn_w, dn_qd, dn_kd, dn_a, dn_el)
    o_dn_g = _dn_post_fwd(o_dn, proj, dn_norm_g)
    o_sb, o_sb_g, sb_l = _sb_fwd(proj)
    w_mem_kv, w_br_dn, w_br_sb, w_br_mem, w_out = proj_weights(o_sb_g)
    mem_n = _rmsnorm_fwd(mem, mem_norm_g, "mem_norm_fwd")
    mkv = _mm(mem_n, w_mem_kv, "nn", "mem_kv")
    o_m, o_m_g = _mem_fwd(proj, mkv)
    y_dn = _mm(o_dn_g, w_br_dn, "nn", "br_dn", out_dtype=BF16)
    y_sb = _mm(o_sb_g, w_br_sb, "nn", "br_sb", out_dtype=BF16)
    y_m = _mm(o_m_g, w_br_mem, "nn", "br_mem", out_dtype=BF16)
    merged = _merge_fwd(proj, y_dn, y_sb, y_m)
    mo = _mm(merged, w_out, "nn", "out_proj")
    d_out, d_out_b, loss_row, g_final = _final_loss(x, mo, final_row, tgt)

    g_w_out = _mm(merged, d_out_b, "tn", "g_w_out", out_dtype=BF16)
    d_merged = _mm(d_out_b, w_out, "nt", "d_merged")
    dy_dn, dy_sb, dy_m, dg1, dg2, dg3 = _merge_bwd(proj, y_dn, y_sb, y_m, d_merged)
    g_w_br_dn = _mm(o_dn_g, dy_dn, "tn", "g_w_br_dn", out_dtype=BF16)
    g_w_br_sb = _mm(o_sb_g, dy_sb, "tn", "g_w_br_sb", out_dtype=BF16)
    g_w_br_mem = _mm(o_m_g, dy_m, "tn", "g_w_br_mem", out_dtype=BF16)
    d_o_dn_g = _mm(dy_dn, w_br_dn, "nt", "d_o_dn")
    d_o_sb_g = _mm(dy_sb, w_br_sb, "nt", "d_o_sb")
    d_o_m_g = _mm(dy_m, w_br_mem, "nt", "d_o_mem")

    d_mq, d_mz, d_mkv = _mem_bwd(proj, mkv, o_m, d_o_m_g)
    d_mkv_b = _cast_bf16(d_mkv, "cast_dmkv")
    g_w_mem_kv = _mm(mem_n, d_mkv_b, "tn", "g_w_mem_kv", out_dtype=BF16)
    d_mem_n = _mm(d_mkv_b, w_mem_kv, "nt", "d_mem_n")
    _, g_mem_norm = _rmsnorm_bwd(mem, mem_norm_g, d_mem_n, jnp.zeros_like(mem), "mem_norm_bwd")

    early = dict(w_mem_kv=g_w_mem_kv, w_br_dn=g_w_br_dn, w_br_sb=g_w_br_sb, w_br_mem=g_w_br_mem, w_out=g_w_out)
    after_early = on_early(early) if on_early is not None else None

    d_sq, d_sk, d_sv, d_sz = _sb_bwd(proj, o_sb, sb_l, d_o_sb_g, after=after_early)

    d_o_dn, d_dnz, g_dn_norm = _dn_post_bwd(o_dn, proj, dn_norm_g, d_o_dn_g)
    d_vnew, d_kd, d_qd, d_w, d_el = _dn_scan_bwd(dn_w, dn_qd, dn_kd, dn_a, dn_el, dn_vn, s_all, d_o_dn)
    d_qn, d_kn, d_vn, dbeta_t, dg_t = _dn_intra_bwd(qkv, beta_t, g_t, tinv_all, dn_vn, d_o_dn, d_vnew, d_kd, d_qd, d_w, d_el)
    d_conv_in, g_conv = _dn_prep_bwd(proj, conv_w, d_qn, d_kn, d_vn)
    d_ba, g_alog_row, g_dtb_row = _dn_gate_bwd(proj, alog_row, dtb_row, dbeta_t, dg_t)

    dproj_sh = _dproj_windows(
        jnp.concatenate([d_conv_in, d_dnz, d_sq, d_sk, d_sv, d_sz, d_mq, d_mz, dg1, dg2, dg3, d_ba], axis=1))
    g_w_sh = _mm(h, dproj_sh, "tn", "g_w_in", out_dtype=BF16, out_shards=N_SHARD, tn_max=1024)
    def input_grad(after=None):
        dh = _mm(dproj_sh, w_sh, "nt", "d_h", after=after, tm_max=2048, tn_max=1024)
        grad_x, g_norm = _rmsnorm_bwd(x, norm_g, dh, d_out, "norm_bwd")
        small = dict(norm_g=g_norm, mem_norm_g=g_mem_norm, final_g=g_final, dn_norm_g=g_dn_norm,
                     a_log=g_alog_row[:, N_HEADS:2 * N_HEADS], dt_bias=g_dtb_row[:, N_HEADS:2 * N_HEADS],
                     conv_w=g_conv)
        return grad_x, small

    return loss_row[0, 0], early, g_w_sh, input_grad


def _reduce_scatter_start(grads, tag):
    c = lax.axis_index("c")
    core = jnp.reshape(c, (1,)).astype(jnp.int32)
    recv = _pair_reduce_send(grads, tag)
    parts = [_pair_add(g, r, core, "pair_add_" + tag) for g, r in zip(grads, recv)]
    return _chip_exchange_start(parts, tag)


def _reduce_scatter_finish(handle, after, tag):
    send_sems, recv_sems, parts, lands, _ = handle
    x, y, c = lax.axis_index("x"), lax.axis_index("y"), lax.axis_index("c")
    place = jnp.stack([2 * x + y, c]).astype(jnp.int32)
    parts, by_chip = _chip_exchange_wait(send_sems, recv_sems, parts, lands, after, tag)
    fulls = [_chip_sum(p, b, place, "chip_sum_" + tag) for p, b in zip(parts, by_chip)]
    return _pair_allgather(fulls, tag)


def kernel(x, mem, norm_g, mem_norm_g, w_in, conv_w, a_log, dt_bias, dn_norm_g, w_mem_kv, w_br_dn, w_br_sb, w_br_mem, w_out, final_g, loss_target, m_norm_g, m_mem_norm_g, m_w_in, m_conv_w, m_a_log, m_dt_bias, m_dn_norm_g, m_w_mem_kv, m_w_br_dn, m_w_br_sb, m_w_br_mem, m_w_out, m_final_g, v_norm_g, v_mem_norm_g, v_w_in, v_conv_w, v_a_log, v_dt_bias, v_dn_norm_g, v_w_mem_kv, v_w_br_dn, v_w_br_sb, v_w_br_mem, v_w_out, v_final_g):
    w_a = w_in[0]
    w_b = _pack_b(w_mem_kv[0], w_br_dn[0], w_br_sb[0], w_br_mem[0], w_out[0])
    m_b = _pack_b(m_w_mem_kv[0], m_w_br_dn[0], m_w_br_sb[0], m_w_br_mem[0], m_w_out[0])
    v_b = _pack_b(v_w_mem_kv[0], v_w_br_dn[0], v_w_br_sb[0], v_w_br_mem[0], v_w_out[0])

    shard_idx = 2 * lax.axis_index("x") + lax.axis_index("y")
    shard = jnp.reshape(shard_idx, (1,)).astype(jnp.int32)
    own = [_cast_to_window(w_a, shard, "cast_w_in"), _cast_bf16(_conv_slab(conv_w[0]), "cast_conv")]
    send1, recv1, own, lands, token = _halves_gather_start(own)
    h = _rmsnorm_fwd(x[0], norm_g, "norm_fwd", after=token)
    w_b_bf = _cast_bf16(w_b, "cast_w_b")
    send2, recv2, lands = _halves_gather_forward(send1, recv1, own, lands,
                                                 after=[h, w_b_bf, m_w_in[0], v_w_in[0], m_b, v_b])
    lands = _halves_gather_wait(send2, recv2, lands)
    ga, g_conv = [lax.dynamic_update_slice(land, o[None], (shard_idx, 0, 0)) for land, o in zip(lands, own)]
    w_r = _windows_to_w_r(ga)
    f_conv = g_conv[:, :3].reshape(N_SHARD, 4, 768).transpose(1, 0, 2).reshape(4, 3 * D_MODEL).astype(F32)
    b_flight = _shard_gather_start(w_b_bf, after=ga)

    def proj_weights(after):
        own, land = _shard_gather_wait(b_flight[0], b_flight[1], b_flight[2], b_flight[3], after)
        gb = lax.dynamic_update_slice(land, own[None], (shard_idx, 0, 0))
        return (gb[:, B_MEMKV:B_BRDN].reshape(N_SHARD * 256, 512),
                gb[:, B_BRDN:B_BRSB].reshape(N_SHARD * 256, D_MODEL),
                gb[:, B_BRSB:B_BRMEM].reshape(N_SHARD * 256, D_MODEL),
                gb[:, B_BRMEM:B_OUT].reshape(N_SHARD, 256, 256).transpose(1, 0, 2).reshape(256, D_MODEL),
                gb[:, B_OUT:B_CONV].reshape(N_SHARD * 256, D_MODEL))

    flights = {}

    def on_early(grads):
        g_b = jnp.concatenate([
            grads["w_mem_kv"].reshape(N_SHARD, 128, D_MODEL), grads["w_br_dn"].reshape(N_SHARD, 256, D_MODEL),
            grads["w_br_sb"].reshape(N_SHARD, 256, D_MODEL),
            grads["w_br_mem"].reshape(256, N_SHARD, 256).transpose(1, 0, 2).reshape(N_SHARD, 64, D_MODEL),
            grads["w_out"].reshape(N_SHARD, 256, D_MODEL)], axis=1).astype(BF16)
        flights["b"] = _reduce_scatter_start([g_b], "b")
        return flights["b"][4]

    loss, _, g_w_sh, input_grad = _local_step(
        x[0], mem[0], loss_target[0], norm_g, mem_norm_g, w_r, ga, f_conv, a_log, dt_bias, dn_norm_g,
        proj_weights, final_g, on_early=on_early, after_gather=b_flight[4], h=h)
    flights["a"] = _reduce_scatter_start([g_w_sh], "a")
    grad_x, small = input_grad(after=flights["a"][4])

    part = _pack_small(small["norm_g"], small["mem_norm_g"], small["final_g"], small["dn_norm_g"],
                       small["a_log"], small["dt_bias"], _conv_rows(small["conv_w"]), loss)
    w_s = _pack_small(norm_g, mem_norm_g, final_g, dn_norm_g, a_log, dt_bias, _conv_shard_rows(conv_w[0], shard_idx))
    m_s = _pack_small(m_norm_g, m_mem_norm_g, m_final_g, m_dn_norm_g, m_a_log, m_dt_bias,
                      _conv_shard_rows(m_conv_w[0], shard_idx))
    v_s = _pack_small(v_norm_g, v_mem_norm_g, v_final_g, v_dn_norm_g, v_a_log, v_dt_bias,
                      _conv_shard_rows(v_conv_w[0], shard_idx))
    (gs_b,) = _reduce_scatter_finish(flights["b"], after=grad_x, tag="b")
    gr_b, d_b, nm_b, nv_b = _adamw(w_b, gs_b, m_b, v_b, "adamw_b")
    g_s, d_s, nm_s, nv_s = _small_update(_allgather_small(part, after=d_b), w_s, m_s, v_s)

    (gs_in,) = _reduce_scatter_finish(flights["a"], after=g_s, tag="a")
    gr_in, d_in, nm_in, nv_in = _adamw_window(w_a, gs_in, m_w_in[0], v_w_in[0], shard, "adamw_w_in")

    def assemble(slab_small, a_in, slab_b):
        s_norm, s_memnorm, s_final, s_dnnorm, s_alog, s_dtb, b_conv = _unpack_small(slab_small, shard_idx)
        b_memkv, b_brdn, b_brsb, b_brmem, b_out = _unpack_b(slab_b)
        return [s_norm, s_memnorm, a_in.reshape(1, D_MODEL, IN_WIDTH // N_SHARD), b_conv, s_alog, s_dtb, s_dnnorm,
                b_memkv, b_brdn, b_brsb, b_brmem, b_out, s_final]

    outs = [g_s[S_LOSS, 0], grad_x.reshape(1, -1, D_MODEL)]
    outs += assemble(g_s, gr_in, gr_b)
    outs += assemble(d_s, d_in, d_b)
    outs += assemble(nm_s, nm_in, nm_b)
    outs += assemble(nv_s, nv_in, nv_b)
    return tuple(outs)
```

```python
import math

import jax
import jax.numpy as jnp
from jax import lax
from jax.experimental import pallas as pl
from jax.experimental.pallas import tpu as pltpu

F32 = jnp.float32
BF16 = jnp.bfloat16
MESH = pl.DeviceIdType.MESH

D_MODEL = 1024
N_HEADS = 8
D_HEAD = 128
DN_CHUNK = 64
DN_GROUP = 16
DN_SCAN_GROUP = 4
SB_BLOCK = 256
SB_HEADS_PER_STEP = 2
SB_QBLOCK = 256
MEM_HEADS = 4
MEM_DH = 64
MEM_W = MEM_HEADS * MEM_DH
NORM_EPS = 1e-6
IN_WIDTH = 11792
N_SHARD = 4
SHARD_W = IN_WIDTH // N_SHARD
SHARD_PAD = 3072
N_DEV = 8

C_DNZ = 3072
C_SBQ = 4096
C_SBZ = 7168
C_MQ = 8192
C_MZ = 8448
C_GATES = 8704
C_BA = 11776
W_R = 12288

ADAM_LR = 0.001
ADAM_B1 = 0.9
ADAM_B2 = 0.999
ADAM_EPS = 1e-08
ADAM_WD = 0.01
ADAM_STEP = 10

VMEM_LIMIT = 56 * 1024 * 1024

B_MEMKV, B_BRDN, B_BRSB, B_BRMEM, B_OUT, B_CONV = 0, 128, 384, 640, 704, 960
S_NORM, S_MEMNORM, S_FINAL, S_DNNORM, S_ALOG, S_DTB, S_LOSS, S_CONV, S_ROWS = 0, 8, 16, 24, 25, 26, 27, 32, 128
CONV_BLOCKS = 3 * D_MODEL // 128


def _cp(**kw):
    return pltpu.CompilerParams(vmem_limit_bytes=VMEM_LIMIT, **kw)


def _dot(a, b, dims):
    lead = a.ndim - 2
    ca, cb = {"nn": (1, 0), "nt": (1, 1), "tn": (0, 0)}[dims]
    batch = tuple(range(lead))
    return lax.dot_general(a, b, (((ca + lead,), (cb + lead,)), (batch, batch)), preferred_element_type=F32)


def _chunks(x):
    return x.reshape(x.shape[0] // DN_CHUNK, DN_CHUNK, x.shape[1])


def _unchunk(x):
    return x.reshape(x.shape[0] * x.shape[1], x.shape[2])


def _bdot(a, b, dims):
    return _dot(a.astype(BF16), b.astype(BF16), dims)


def _split(a):
    hi = a.astype(BF16)
    return hi, (a - hi.astype(F32)).astype(BF16)


def _dot3(a, b, dims):
    a1, a2 = _split(a)
    b1, b2 = _split(b)
    return _dot(a1, b1, dims) + (_dot(a1, b2, dims) + _dot(a2, b1, dims))


def _ones_dot(a, ones_bf16):
    out = _dot(a.reshape(-1, a.shape[-1]).astype(BF16), ones_bf16, "nn")
    return out.reshape(a.shape[:-1] + (ones_bf16.shape[1],))


def _sigmoid(x):
    return 1.0 / (1.0 + jnp.exp(-x))


def _log1p_small(u):
    return jnp.where(u < 1e-2, u * (1.0 - u * (0.5 - u * (1.0 / 3.0))), jnp.log(1.0 + u))


def _pick(dim, cands):
    for c in cands:
        if dim % c == 0:
            return c
    return dim


def _mm(a, b, dims, name, out_dtype=F32, out_shards=1, after=None, tm_max=1024, tn_max=512):
    ta, tb = dims[0] == "t", dims[1] == "t"
    m, k = (a.shape[1], a.shape[0]) if ta else a.shape
    b_shards = b.shape[0] if b.ndim == 3 else 1
    n = b.shape[-2] if tb else b.shape[-1]
    tm = _pick(m, (tm_max, 1024, 512, 256))
    tn = _pick(n // out_shards, (tn_max, 512, 384, 256, 128))
    tk = _pick(k // b_shards, (2048, 1024, 512, 384, 256))
    nk = k // tk

    def body(a_ref, b_ref, *rest):
        if nk == 1:
            rest[-1][...] = _bdot(a_ref[...], b_ref[...], dims).astype(out_dtype)
            return
        o_ref, acc_ref = rest[-2:]
        kk = pl.program_id(2)

        @pl.when(kk == 0)
        def _():
            acc_ref[...] = jnp.zeros_like(acc_ref)

        acc_ref[...] += _bdot(a_ref[...], b_ref[...], dims)

        @pl.when(kk == nk - 1)
        def _():
            o_ref[...] = acc_ref[...].astype(out_dtype)

    a_spec = pl.BlockSpec((tk, tm), lambda i, j, q: (q, i)) if ta else pl.BlockSpec((tm, tk), lambda i, j, q: (i, q))
    if b_shards > 1:
        per_k = k // b_shards // tk
        b_spec = pl.BlockSpec((None, tn, tk), lambda i, j, q: (q // per_k, j, q % per_k))
    else:
        b_spec = pl.BlockSpec((tn, tk), lambda i, j, q: (j, q)) if tb else pl.BlockSpec((tk, tn), lambda i, j, q: (q, j))
    if out_shards > 1:
        per_n = n // out_shards // tn
        out_spec = pl.BlockSpec((None, tm, tn), lambda i, j, q: (j // per_n, i, j % per_n))
        out_shape = jax.ShapeDtypeStruct((out_shards, m, n // out_shards), out_dtype)
    else:
        out_spec = pl.BlockSpec((tm, tn), lambda i, j, q: (i, j))
        out_shape = jax.ShapeDtypeStruct((m, n), out_dtype)
    extra_specs, extra = [], []
    if after is not None:
        extra_specs, extra = [pl.BlockSpec(after.shape, lambda i, j, q: (0, 0))], [after]
    return pl.pallas_call(
        body, name=name, grid=(m // tm, n // tn, nk),
        in_specs=[a_spec, b_spec] + extra_specs, out_specs=out_spec, out_shape=out_shape,
        scratch_shapes=[pltpu.VMEM((tm, tn), F32)] if nk > 1 else [],
        compiler_params=_cp(dimension_semantics=("parallel", "parallel", "arbitrary")),
    )(a, b, *extra)


def _rmsnorm_fwd(x, g, name, after=None):
    t, d = x.shape
    tb = _pick(t, (512, 256))

    def body(x_ref, g_ref, *rest):
        xv = x_ref[...]
        r = lax.rsqrt(jnp.mean(xv * xv, axis=-1, keepdims=True) + NORM_EPS)
        rest[-1][...] = ((xv * r) * g_ref[...]).astype(BF16)

    extra_specs, extra = [], []
    if after is not None:
        extra_specs, extra = [pl.BlockSpec(after.shape, lambda i: (0, 0))], [after]
    return pl.pallas_call(
        body, name=name, grid=(t // tb,),
        in_specs=[pl.BlockSpec((tb, d), lambda i: (i, 0)), pl.BlockSpec((1, d), lambda i: (0, 0))] + extra_specs,
        out_specs=pl.BlockSpec((tb, d), lambda i: (i, 0)),
        out_shape=jax.ShapeDtypeStruct((t, d), BF16), compiler_params=_cp(),
    )(x, g, *extra)


def _rmsnorm_bwd(x, g, dh, resid, name):
    t, d = x.shape
    tb = _pick(t, (256,))

    def body(x_ref, g_ref, dh_ref, r_ref, dx_ref, dg_ref):
        @pl.when(pl.program_id(0) == 0)
        def _():
            dg_ref[...] = jnp.zeros_like(dg_ref)

        xv = x_ref[...]
        r = lax.rsqrt(jnp.mean(xv * xv, axis=-1, keepdims=True) + NORM_EPS)
        xhat = xv * r
        dhv = dh_ref[...]
        dg_ref[...] += jnp.sum(dhv * xhat, axis=0, keepdims=True)
        dxh = dhv * g_ref[...]
        dx_ref[...] = r_ref[...] + r * (dxh - xhat * jnp.mean(dxh * xhat, axis=-1, keepdims=True))

    row = pl.BlockSpec((tb, d), lambda i: (i, 0))
    vec = pl.BlockSpec((1, d), lambda i: (0, 0))
    return pl.pallas_call(
        body, name=name, grid=(t // tb,), in_specs=[row, vec, row, row], out_specs=[row, vec],
        out_shape=[jax.ShapeDtypeStruct((t, d), F32), jax.ShapeDtypeStruct((1, d), F32)], compiler_params=_cp(),
    )(x, g, dh, resid)


def _conv_silu(xv, w, row):
    y = xv * w[3:4, :]
    for s in (1, 2, 3):
        xs = jnp.where(row >= s, pltpu.roll(xv, s, 0), 0.0)
        y = y + xs * w[3 - s:4 - s, :]
    sig = _sigmoid(y)
    return y, sig, y * sig


def _dn_prep_fwd(proj, conv_w):
    t = proj.shape[0]

    def body(p_ref, w_ref, o_ref):
        j = pl.program_id(0)
        xv = p_ref[...]
        row = lax.broadcasted_iota(jnp.int32, xv.shape, 0)
        _, _, a = _conv_silu(xv, w_ref[...], row)
        inv = lax.rsqrt(jnp.sum(a * a, axis=-1, keepdims=True) + NORM_EPS)
        scale = jnp.where(j < N_HEADS, D_HEAD ** -0.5, 1.0)
        normed = jnp.where(j < 2 * N_HEADS, 1.0, 0.0)
        o_ref[...] = a * (normed * (inv * scale) + (1.0 - normed))

    return pl.pallas_call(
        body, name="dn_prep_fwd", grid=(3 * N_HEADS,),
        in_specs=[pl.BlockSpec((t, D_HEAD), lambda j: (0, j)), pl.BlockSpec((4, D_HEAD), lambda j: (0, j))],
        out_specs=pl.BlockSpec((t, D_HEAD), lambda j: (0, j)),
        out_shape=jax.ShapeDtypeStruct((t, 3 * D_MODEL), F32), compiler_params=_cp(),
    )(proj, conv_w)


def _dn_prep_bwd(proj, conv_w, dq, dk, dv):
    t = proj.shape[0]

    def body(p_ref, w_ref, dq_ref, dk_ref, dv_ref, dp_ref, dw_ref):
        j = pl.program_id(0)
        xv = p_ref[...]
        w = w_ref[...]
        row = lax.broadcasted_iota(jnp.int32, xv.shape, 0)
        y, s, a = _conv_silu(xv, w, row)
        part = jnp.zeros(xv.shape, jnp.int32) + j // N_HEADS
        dn = jnp.where(part == 0, dq_ref[...], jnp.where(part == 1, dk_ref[...], dv_ref[...]))
        inv = lax.rsqrt(jnp.sum(a * a, axis=-1, keepdims=True) + NORM_EPS)
        scale = jnp.where(j < N_HEADS, D_HEAD ** -0.5, 1.0)
        ds = dn * scale
        da_norm = inv * ds - a * (inv * inv * inv) * jnp.sum(ds * a, axis=-1, keepdims=True)
        normed = jnp.where(j < 2 * N_HEADS, 1.0, 0.0)
        da = normed * da_norm + (1.0 - normed) * dn
        dy = da * (s * (1.0 + y * (1.0 - s)))
        dx = dy * w[3:4, :]
        dw_ref[3:4, :] = jnp.sum(dy * xv, axis=0, keepdims=True)
        for sft in (1, 2, 3):
            xs = jnp.where(row >= sft, pltpu.roll(xv, sft, 0), 0.0)
            dw_ref[3 - sft:4 - sft, :] = jnp.sum(dy * xs, axis=0, keepdims=True)
            dys = jnp.where(row < t - sft, pltpu.roll(dy, t - sft, 0), 0.0)
            dx = dx + dys * w[3 - sft:4 - sft, :]
        dp_ref[...] = dx.astype(BF16)

    blk = pl.BlockSpec((t, D_HEAD), lambda j: (0, j))
    wblk = pl.BlockSpec((4, D_HEAD), lambda j: (0, j))

    def grad(part):
        return pl.BlockSpec((t, D_HEAD), lambda j: (0, jnp.clip(j - part * N_HEADS, 0, N_HEADS - 1)))

    return pl.pallas_call(
        body, name="dn_prep_bwd", grid=(3 * N_HEADS,), in_specs=[blk, wblk, grad(0), grad(1), grad(2)],
        out_specs=[blk, wblk],
        out_shape=[jax.ShapeDtypeStruct((t, 3 * D_MODEL), BF16), jax.ShapeDtypeStruct((4, 3 * D_MODEL), F32)],
        compiler_params=_cp(),
    )(proj, conv_w, dq, dk, dv)


def _softplus_parts(xv):
    e = jnp.exp(-jnp.abs(xv))
    return jnp.maximum(xv, 0.0) + _log1p_small(e)


def _chunk_scan(v, row, reverse):
    t = v.shape[0]
    pos = row & (DN_CHUNK - 1)
    s = 1
    while s < DN_CHUNK:
        if reverse:
            v = v + jnp.where(pos < DN_CHUNK - s, pltpu.roll(v, t - s, 0), 0.0)
        else:
            v = v + jnp.where(pos >= s, pltpu.roll(v, s, 0), 0.0)
        s *= 2
    return v


def _dn_gate_fwd(proj, alog_row, dtb_row):
    t = proj.shape[0]

    def body(p_ref, al_ref, dt_ref, b_ref, g_ref):
        p = p_ref[...]
        row = lax.broadcasted_iota(jnp.int32, p.shape, 0)
        b_ref[...] = _sigmoid(p)
        g = -jnp.exp(al_ref[...]) * _softplus_parts(p + dt_ref[...])
        g_ref[...] = _chunk_scan(g, row, reverse=False)

    blk = pl.BlockSpec((t, 128), lambda i: (0, C_BA // 128))
    vec = pl.BlockSpec((1, 128), lambda i: (0, 0))
    out = pl.BlockSpec((t, 128), lambda i: (0, 0))
    return pl.pallas_call(
        body, name="dn_gate_fwd", grid=(1,), in_specs=[blk, vec, vec], out_specs=[out, out],
        out_shape=[jax.ShapeDtypeStruct((t, 128), F32)] * 2, compiler_params=_cp(),
    )(proj, alog_row, dtb_row)


def _dn_gate_bwd(proj, alog_row, dtb_row, dbeta, dgc):
    t = proj.shape[0]

    def body(p_ref, al_ref, dt_ref, db_ref, dg_ref, dp_ref, dal_ref, ddt_ref):
        p = p_ref[...]
        row = lax.broadcasted_iota(jnp.int32, p.shape, 0)
        lane = lax.broadcasted_iota(jnp.int32, p.shape, 1)
        s = _sigmoid(p)
        d_b = db_ref[...] * s * (1.0 - s)
        dg = _chunk_scan(dg_ref[...], row, reverse=True)
        xa = p + dt_ref[...]
        ea = jnp.exp(al_ref[...])
        g = -ea * _softplus_parts(xa)
        d_a = dg * (-ea) * _sigmoid(xa)
        dp_ref[...] = jnp.where(lane < N_HEADS, d_b, jnp.where(lane < 2 * N_HEADS, d_a, 0.0)).astype(BF16)
        dal_ref[...] = jnp.sum(dg * g, axis=0, keepdims=True)
        ddt_ref[...] = jnp.sum(d_a, axis=0, keepdims=True)

    blk = pl.BlockSpec((t, 128), lambda i: (0, C_BA // 128))
    vec = pl.BlockSpec((1, 128), lambda i: (0, 0))
    full = pl.BlockSpec((t, 128), lambda i: (0, 0))
    return pl.pallas_call(
        body, name="dn_gate_bwd", grid=(1,), in_specs=[blk, vec, vec, full, full], out_specs=[full, vec, vec],
        out_shape=[jax.ShapeDtypeStruct((t, 128), BF16), jax.ShapeDtypeStruct((1, 128), F32),
                   jax.ShapeDtypeStruct((1, 128), F32)], compiler_params=_cp(),
    )(proj, alog_row, dtb_row, dbeta, dgc)


def _col_to_row(col, eye):
    return jnp.sum(jnp.where(eye, col, 0.0), axis=-2, keepdims=True)


def _row_to_col(rowv, eye):
    return jnp.sum(jnp.where(eye, rowv, 0.0), axis=-1, keepdims=True)


def _tri_inverse(m, ri, ci):
    eye = (ri == ci).astype(F32)
    b16 = (ri >> 4) == (ci >> 4)
    b32 = (ri >> 5) == (ci >> 5)
    m1 = jnp.where(b16, m, 0.0)
    x = eye - m1
    p = _dot3(m1, m1, "nn")
    x = x + _dot3(x, p, "nn")
    p = _dot3(p, p, "nn")
    x = x + _dot3(x, p, "nn")
    p = _dot3(p, p, "nn")
    x = x + _dot3(x, p, "nn")
    c1 = jnp.where(jnp.logical_and(b32, jnp.logical_not(b16)), m, 0.0)
    x = x - _dot3(_dot3(x, c1, "nn"), x, "nn")
    c2 = jnp.where(b32, 0.0, m)
    x = x - _dot3(_dot3(x, c2, "nn"), x, "nn")
    return x


def _dn_chunk_common(q, k, gc, ri, ci):
    eye = ri == ci
    g_row = _col_to_row(gc, eye)
    diff = jnp.minimum(gc - g_row, 0.0)
    gam = jnp.where(ri >= ci, jnp.exp(diff), 0.0)
    kk = _bdot(k, k, "nt")
    qk = _bdot(q, k, "nt")
    rcol = lax.broadcasted_iota(jnp.int32, gc.shape, gc.ndim - 2)
    last = jnp.sum(jnp.where(rcol == DN_CHUNK - 1, gc, 0.0), axis=-2, keepdims=True)
    e_g = jnp.exp(gc)
    dec = jnp.exp(last - gc)
    return eye, gam, kk, qk, last, e_g, dec, rcol


def _dn_specs(t, rows_blk):
    def head(off):
        return pl.BlockSpec((rows_blk, D_HEAD), lambda g, h: (g, off + h))

    lanes = pl.BlockSpec((rows_blk, 128), lambda g, h: (g, 0))
    hm = pl.BlockSpec((None, rows_blk, D_HEAD), lambda g, h: (h, g, 0))
    sq = pl.BlockSpec((1, rows_blk, DN_CHUNK), lambda g, h: (h, g, 0))
    tile = pl.BlockSpec((1, rows_blk // DN_CHUNK, 8, 128), lambda g, h: (h, g, 0, 0))
    return head, lanes, hm, sq, tile


def _head_column(slab, lane_idx):
    lane = lax.broadcasted_iota(jnp.int32, slab.shape, 1)
    return _chunks(jnp.sum(jnp.where(lane == lane_idx, slab, 0.0), axis=1, keepdims=True))


def _dn_intra_fwd(qkv, beta_t, g_t):
    t = qkv.shape[0]
    n_chunks = t // DN_CHUNK
    rows_blk = min(DN_GROUP * DN_CHUNK, t)

    def body(q_ref, k_ref, v_ref, b_ref, g_ref, u_ref, w_ref, qd_ref, kd_ref, a_ref, ti_ref, el_ref):
        ri = lax.broadcasted_iota(jnp.int32, (DN_CHUNK, DN_CHUNK), 0)
        ci = lax.broadcasted_iota(jnp.int32, (DN_CHUNK, DN_CHUNK), 1)
        h = pl.program_id(1)
        q, k, v = (_chunks(r[...]) for r in (q_ref, k_ref, v_ref))
        b, gc = _head_column(b_ref[...], h), _head_column(g_ref[...], h + N_HEADS)
        _, gam, kk, qk, last, e_g, dec, _ = _dn_chunk_common(q, k, gc, ri, ci)
        tinv = _tri_inverse(jnp.where(ri > ci, b * kk * gam, 0.0), ri, ci)
        u_ref[...] = _unchunk(_bdot(tinv, v * b, "nn"))
        w_ref[...] = _unchunk(_bdot(tinv, k * (b * e_g), "nn"))
        qd_ref[...] = _unchunk(q * e_g)
        kd_ref[...] = _unchunk(k * dec)
        a_ref[0] = _unchunk(qk * gam)
        ti_ref[0] = _unchunk(tinv)
        el_ref[0] = jnp.broadcast_to(jnp.exp(last), (rows_blk // DN_CHUNK, 8, 128))

    head, lanes, hm, sq, tile = _dn_specs(t, rows_blk)
    act = jax.ShapeDtypeStruct((N_HEADS, t, D_HEAD), F32)
    sqs = jax.ShapeDtypeStruct((N_HEADS, t, DN_CHUNK), F32)
    return pl.pallas_call(
        body, name="dn_intra_fwd", grid=(t // rows_blk, N_HEADS),
        in_specs=[head(0), head(N_HEADS), head(2 * N_HEADS), lanes, lanes],
        out_specs=[hm] * 4 + [sq, sq, tile],
        out_shape=[act] * 4 + [sqs, sqs, jax.ShapeDtypeStruct((N_HEADS, n_chunks, 8, 128), F32)],
        compiler_params=_cp(),
    )(qkv, qkv, qkv, beta_t, g_t)


def _dn_scan_specs(t, rows_blk, reverse):
    n_groups = t // rows_blk

    def at(g):
        return n_groups - 1 - g if reverse else g

    per = rows_blk // DN_CHUNK
    act = pl.BlockSpec((N_HEADS, rows_blk, D_HEAD), lambda g: (0, at(g), 0))
    sq = pl.BlockSpec((N_HEADS, rows_blk, DN_CHUNK), lambda g: (0, at(g), 0))
    state = pl.BlockSpec((N_HEADS, per, D_HEAD, D_HEAD), lambda g: (0, at(g), 0, 0))
    tile = pl.BlockSpec((N_HEADS, per, 8, 128), lambda g: (0, at(g), 0, 0))
    return act, sq, state, tile


def _dn_scan_fwd(u, w, qd, kd, a, el):
    t = u.shape[1]
    n_chunks = t // DN_CHUNK
    rows_blk = DN_SCAN_GROUP * DN_CHUNK

    def body(u_ref, w_ref, qd_ref, kd_ref, a_ref, el_ref, o_ref, vn_ref, s_ref, s_scr):
        @pl.when(pl.program_id(0) == 0)
        def _():
            s_scr[...] = jnp.zeros_like(s_scr)

        for cc in range(DN_SCAN_GROUP):
            rows = slice(cc * DN_CHUNK, (cc + 1) * DN_CHUNK)
            s = s_scr[...]
            s_ref[:, cc] = s
            v_new = u_ref[:, rows, :] - _bdot(w_ref[:, rows, :], s, "nn")
            vn_ref[:, rows, :] = v_new
            o_ref[:, rows, :] = _bdot(qd_ref[:, rows, :], s, "nn") + _bdot(a_ref[:, rows, :], v_new, "nn")
            s_scr[...] = s * el_ref[:, cc][:, 0:1, :] + _bdot(kd_ref[:, rows, :], v_new, "tn")

    act, sq, state, tile = _dn_scan_specs(t, rows_blk, reverse=False)
    shp = jax.ShapeDtypeStruct((N_HEADS, t, D_HEAD), F32)
    return pl.pallas_call(
        body, name="dn_scan_fwd", grid=(t // rows_blk,),
        in_specs=[act, act, act, act, sq, tile], out_specs=[act, act, state],
        out_shape=[shp, shp, jax.ShapeDtypeStruct((N_HEADS, n_chunks, D_HEAD, D_HEAD), F32)],
        scratch_shapes=[pltpu.VMEM((N_HEADS, D_HEAD, D_HEAD), F32)],
        compiler_params=_cp(dimension_semantics=("arbitrary",)),
    )(u, w, qd, kd, a, el)


def _dn_scan_bwd(w, qd, kd, a, el, vn, s_all, do):
    t = w.shape[1]
    n_chunks = t // DN_CHUNK
    rows_blk = DN_SCAN_GROUP * DN_CHUNK

    def body(w_ref, qd_ref, kd_ref, a_ref, el_ref, vn_ref, s_ref, do_ref, dvn_ref, dkd_ref, dqd_ref, dw_ref, dl_ref, ds_scr):
        @pl.when(pl.program_id(0) == 0)
        def _():
            ds_scr[...] = jnp.zeros_like(ds_scr)

        for cc in reversed(range(DN_SCAN_GROUP)):
            rows = slice(cc * DN_CHUNK, (cc + 1) * DN_CHUNK)
            s = s_ref[:, cc]
            d_s = ds_scr[...]
            e_last = el_ref[:, cc][:, 0:1, :]
            d_o = do_ref[:, rows, :]
            dv_new = _bdot(a_ref[:, rows, :], d_o, "tn") + _bdot(kd_ref[:, rows, :], d_s, "nn")
            ds_scr[...] = d_s * e_last + _bdot(qd_ref[:, rows, :], d_o, "tn") - _bdot(w_ref[:, rows, :], dv_new, "tn")
            dvn_ref[:, rows, :] = dv_new
            dkd_ref[:, rows, :] = _bdot(vn_ref[:, rows, :], d_s, "nt")
            dqd_ref[:, rows, :] = _bdot(d_o, s, "nt")
            dw_ref[:, rows, :] = -_bdot(dv_new, s, "nt")
            dlast = jnp.sum(jnp.sum(d_s * s, axis=2, keepdims=True), axis=1, keepdims=True)
            dl_ref[:, cc] = jnp.broadcast_to(dlast * e_last, (N_HEADS, 8, 128))

    act, sq, state, tile = _dn_scan_specs(t, rows_blk, reverse=True)
    shp = jax.ShapeDtypeStruct((N_HEADS, t, D_HEAD), F32)
    return pl.pallas_call(
        body, name="dn_scan_bwd", grid=(t // rows_blk,),
        in_specs=[act, act, act, sq, tile, act, state, act], out_specs=[act] * 4 + [tile],
        out_shape=[shp] * 4 + [jax.ShapeDtypeStruct((N_HEADS, n_chunks, 8, 128), F32)],
        scratch_shapes=[pltpu.VMEM((N_HEADS, D_HEAD, D_HEAD), F32)],
        compiler_params=_cp(dimension_semantics=("arbitrary",)),
    )(w, qd, kd, a, el, vn, s_all, do)


def _dn_intra_bwd(qkv, beta_t, g_t, tinv_all, vn, do, dvn, dkd, dqd, dw, dl):
    t = qkv.shape[0]
    rows_blk = min(DN_GROUP * DN_CHUNK, t)

    def body(q_ref, k_ref, v_ref, b_ref, g_ref, ti_ref, vn_ref, do_ref, dvn_ref, dkd_ref, dqd_ref, dw_ref, dl_ref,
             dq_ref, dk_ref, dv_ref, db_ref, dg_ref):
        ri = lax.broadcasted_iota(jnp.int32, (DN_CHUNK, DN_CHUNK), 0)
        ci = lax.broadcasted_iota(jnp.int32, (DN_CHUNK, DN_CHUNK), 1)
        h = pl.program_id(1)
        q, k, v = (_chunks(r[...]) for r in (q_ref, k_ref, v_ref))
        b, gc = _head_column(b_ref[...], h), _head_column(g_ref[...], h + N_HEADS)
        tinv = _chunks(ti_ref[0])
        dv_new, dk_dec, dq_dec, d_w = (_chunks(r[...]) for r in (dvn_ref, dkd_ref, dqd_ref, dw_ref))
        eye, gam, kk, qk, _, e_g, dec, rcol = _dn_chunk_common(q, k, gc, ri, ci)
        bv = v * b
        bk = k * (b * e_g)

        d_a = jnp.where(ri >= ci, _bdot(_chunks(do_ref[...]), _chunks(vn_ref[...]), "nt"), 0.0)
        dbv = _bdot(tinv, dv_new, "tn")
        dbk = _bdot(tinv, d_w, "tn")
        d_tinv = _bdot(dv_new, bv, "nt") + _bdot(d_w, bk, "nt")
        d_m = -jnp.where(ri > ci, _dot3(_dot3(tinv, d_tinv, "tn"), tinv, "nt"), 0.0)

        d_kk = d_m * b * gam
        d_gam = d_m * b * kk + d_a * qk
        d_qk = d_a * gam
        dq_ref[...] = _unchunk(_bdot(d_qk, k, "nn") + dq_dec * e_g)
        dk_ref[...] = _unchunk(_bdot(d_qk, q, "tn") + _bdot(d_kk, k, "nn") + _bdot(d_kk, k, "tn")
                               + dk_dec * dec + dbk * (b * e_g))
        dv_ref[...] = _unchunk(dbv * b)
        d_b = _unchunk(jnp.sum(d_m * kk * gam, axis=-1, keepdims=True) + jnp.sum(dbv * v, axis=-1, keepdims=True)
                       + jnp.sum(dbk * k, axis=-1, keepdims=True) * e_g)

        xg = d_gam * gam
        kdk = jnp.sum(dk_dec * (k * dec), axis=-1, keepdims=True)
        d_gc = (jnp.sum(xg, axis=-1, keepdims=True) - _row_to_col(jnp.sum(xg, axis=-2, keepdims=True), eye)
                + jnp.sum(dq_dec * (q * e_g), axis=-1, keepdims=True) - kdk
                + jnp.sum(dbk * bk, axis=-1, keepdims=True))
        d_last_total = dl_ref[0][:, 0:1, 0:1] + jnp.sum(kdk, axis=-2, keepdims=True)
        d_g = _unchunk(d_gc + jnp.where(rcol == DN_CHUNK - 1, d_last_total, 0.0))

        @pl.when(h == 0)
        def _():
            db_ref[...] = jnp.zeros_like(db_ref)
            dg_ref[...] = jnp.zeros_like(dg_ref)

        lane = lax.broadcasted_iota(jnp.int32, db_ref.shape, 1)
        db_ref[...] += jnp.where(lane == h, d_b, 0.0)
        dg_ref[...] += jnp.where(lane == h + N_HEADS, d_g, 0.0)

    head, lanes, hm, sq, tile = _dn_specs(t, rows_blk)
    return pl.pallas_call(
        body, name="dn_intra_bwd", grid=(t // rows_blk, N_HEADS),
        in_specs=[head(0), head(N_HEADS), head(2 * N_HEADS), lanes, lanes, sq] + [hm] * 6 + [tile],
        out_specs=[head(0), head(0), head(0), lanes, lanes],
        out_shape=[jax.ShapeDtypeStruct((t, D_MODEL), F32)] * 3 + [jax.ShapeDtypeStruct((t, 128), F32)] * 2,
        compiler_params=_cp(),
    )(qkv, qkv, qkv, beta_t, g_t, tinv_all, vn, do, dvn, dkd, dqd, dw, dl)


def _dn_post_fwd(o, proj, gn):
    t = o.shape[1]

    def body(o_ref, z_ref, g_ref, out_ref):
        ov, z = o_ref[...], z_ref[...]
        r = lax.rsqrt(jnp.mean(ov * ov, axis=-1, keepdims=True) + NORM_EPS)
        out_ref[...] = (((ov * r) * g_ref[...]) * (z * _sigmoid(z))).astype(BF16)

    blk = pl.BlockSpec((t, D_HEAD), lambda h: (0, h))
    return pl.pallas_call(
        body, name="dn_post_fwd", grid=(N_HEADS,),
        in_specs=[pl.BlockSpec((None, t, D_HEAD), lambda h: (h, 0, 0)),
                  pl.BlockSpec((t, D_HEAD), lambda h: (0, C_DNZ // D_HEAD + h)),
                  pl.BlockSpec((1, D_HEAD), lambda h: (0, 0))],
        out_specs=blk, out_shape=jax.ShapeDtypeStruct((t, D_MODEL), BF16), compiler_params=_cp(),
    )(o, proj, gn)


def _dn_post_bwd(o, proj, gn, dout):
    t = o.shape[1]

    def body(o_ref, z_ref, g_ref, d_ref, do_ref, dz_ref, dg_ref):
        @pl.when(pl.program_id(0) == 0)
        def _():
            dg_ref[...] = jnp.zeros_like(dg_ref)

        ov, z, d = o_ref[...], z_ref[...], d_ref[...]
        r = lax.rsqrt(jnp.mean(ov * ov, axis=-1, keepdims=True) + NORM_EPS)
        ohat = ov * r
        s = _sigmoid(z)
        d_on = d * (z * s)
        dz_ref[...] = (d * (ohat * g_ref[...]) * (s * (1.0 + z * (1.0 - s)))).astype(BF16)
        dg_ref[...] += jnp.sum(d_on * ohat, axis=0, keepdims=True)
        dxh = d_on * g_ref[...]
        do_ref[...] = r * (dxh - ohat * jnp.mean(dxh * ohat, axis=-1, keepdims=True))

    blk = pl.BlockSpec((t, D_HEAD), lambda h: (0, h))
    hm = pl.BlockSpec((None, t, D_HEAD), lambda h: (h, 0, 0))
    vec = pl.BlockSpec((1, D_HEAD), lambda h: (0, 0))
    return pl.pallas_call(
        body, name="dn_post_bwd", grid=(N_HEADS,),
        in_specs=[hm, pl.BlockSpec((t, D_HEAD), lambda h: (0, C_DNZ // D_HEAD + h)), vec, blk],
        out_specs=[hm, blk, vec],
        out_shape=[jax.ShapeDtypeStruct((N_HEADS, t, D_HEAD), F32), jax.ShapeDtypeStruct((t, D_MODEL), BF16),
                   jax.ShapeDtypeStruct((1, D_HEAD), F32)], compiler_params=_cp(),
    )(o, proj, gn, dout)


def _sb_fwd(proj):
    t = proj.shape[0]
    qblk = min(SB_QBLOCK, t)
    scale = 1.0 / math.sqrt(D_HEAD)

    hp = SB_HEADS_PER_STEP
    wid = hp * D_HEAD

    def body(q_ref, k_ref, v_ref, z_ref, o_ref, og_ref, l_ref, qb, kb, vb):
        for hh in range(hp):
            hs = slice(hh * D_HEAD, (hh + 1) * D_HEAD)
            qb[hh] = q_ref[:, hs].astype(BF16)
            kb[hh] = k_ref[:, hs].astype(BF16)
            vb[hh] = v_ref[:, hs].astype(BF16)
        ri = lax.broadcasted_iota(jnp.int32, (qblk, SB_BLOCK), 0)
        ci = lax.broadcasted_iota(jnp.int32, (qblk, SB_BLOCK), 1)
        r2 = lax.broadcasted_iota(jnp.int32, (SB_BLOCK, SB_BLOCK), 0)
        c2 = lax.broadcasted_iota(jnp.int32, (SB_BLOCK, SB_BLOCK), 1)
        upper = (r2 > c2).astype(BF16)
        nkb = qblk // SB_BLOCK

        def qblock(i, carry):
            rows = pl.ds(pl.multiple_of(i * qblk, qblk), qblk)
            qi = qb[:, rows, :]

            def tile(j, st, on_diagonal):
                acc, c = st
                cols = pl.ds(pl.multiple_of(j * SB_BLOCK, SB_BLOCK), SB_BLOCK)
                z = _dot(qi, kb[:, cols, :], "nt") * scale
                lb = jnp.minimum(z, 0.0) - jnp.log(1.0 + jnp.exp(-jnp.abs(z)))
                lf = lb - z
                if on_diagonal:
                    mask = (j * SB_BLOCK + ci) < (i * qblk + ri)
                    lf = jnp.where(mask, lf, 0.0)
                att = jnp.exp(lb + (_ones_dot(lf, upper) + c))
                if on_diagonal:
                    att = jnp.where(mask, att, 0.0)
                acc = acc + _dot(att.astype(BF16), vb[:, cols, :], "nn")
                return acc, c + jnp.sum(lf, axis=-1, keepdims=True)

            st = (jnp.zeros((hp, qblk, D_HEAD), F32), jnp.zeros((hp, qblk, 1), F32))
            for d in range(nkb):
                st = tile((i + 1) * nkb - 1 - d, st, True)
            acc, c = lax.fori_loop(0, i * nkb, lambda jj, s: tile(i * nkb - 1 - jj, s, False), st)
            l_ref[:, rows, :] = c
            for hh in range(hp):
                hs = slice(hh * D_HEAD, (hh + 1) * D_HEAD)
                zg = z_ref[rows, hs]
                o_ref[rows, hs] = acc[hh]
                og_ref[rows, hs] = (acc[hh] * (zg * _sigmoid(zg))).astype(BF16)
            return carry

        lax.fori_loop(0, t // qblk, qblock, 0)

    def head(off):
        return pl.BlockSpec((t, wid), lambda h: (0, off // wid + h))

    out = pl.BlockSpec((t, wid), lambda h: (0, h))
    return pl.pallas_call(
        body, name="sb_fwd", grid=(N_HEADS // hp,),
        in_specs=[head(C_SBQ), head(C_SBQ + D_MODEL), head(C_SBQ + 2 * D_MODEL), head(C_SBZ)],
        out_specs=[out, out, pl.BlockSpec((hp, t, 1), lambda h: (h, 0, 0))],
        out_shape=[jax.ShapeDtypeStruct((t, D_MODEL), F32), jax.ShapeDtypeStruct((t, D_MODEL), BF16),
                   jax.ShapeDtypeStruct((N_HEADS, t, 1), F32)],
        scratch_shapes=[pltpu.VMEM((hp, t, D_HEAD), BF16)] * 3, compiler_params=_cp(),
    )(proj, proj, proj, proj)


def _sb_bwd(proj, o, ltot, dog, after=None):
    t = proj.shape[0]
    qblk = min(SB_QBLOCK, t)
    scale = 1.0 / math.sqrt(D_HEAD)

    hp = SB_HEADS_PER_STEP
    wid = hp * D_HEAD

    def body(q_ref, k_ref, v_ref, z_ref, o_ref, l_ref, d_ref, *rest):
        dq_ref, dk_ref, dv_ref, dz_ref, qb, kb, vb, dob, dk_scr, dv_scr = rest[-10:]
        for hh in range(hp):
            hs = slice(hh * D_HEAD, (hh + 1) * D_HEAD)
            qb[hh] = q_ref[:, hs].astype(BF16)
            kb[hh] = k_ref[:, hs].astype(BF16)
            vb[hh] = v_ref[:, hs].astype(BF16)
            zg = z_ref[:, hs]
            sg = _sigmoid(zg)
            dgo = d_ref[:, hs]
            dob[hh] = (dgo * (zg * sg)).astype(BF16)
            dz_ref[:, hs] = (dgo * o_ref[:, hs] * (sg * (1.0 + zg * (1.0 - sg)))).astype(BF16)
        dk_scr[...] = jnp.zeros_like(dk_scr)
        dv_scr[...] = jnp.zeros_like(dv_scr)
        ri = lax.broadcasted_iota(jnp.int32, (qblk, SB_BLOCK), 0)
        ci = lax.broadcasted_iota(jnp.int32, (qblk, SB_BLOCK), 1)
        r2 = lax.broadcasted_iota(jnp.int32, (SB_BLOCK, SB_BLOCK), 0)
        c2 = lax.broadcasted_iota(jnp.int32, (SB_BLOCK, SB_BLOCK), 1)
        upper = (r2 > c2).astype(BF16)
        below = (r2 < c2).astype(BF16)

        def qblock(i, carry):
            rows = pl.ds(pl.multiple_of(i * qblk, qblk), qblk)
            qi = qb[:, rows, :]
            d_o = dob[:, rows, :]
            ltot = l_ref[:, rows, :]

            def tile(j, st, on_diagonal):
                dq, cpre, ce = st
                cols = pl.ds(pl.multiple_of(j * SB_BLOCK, SB_BLOCK), SB_BLOCK)
                kj, vj = kb[:, cols, :], vb[:, cols, :]
                z = _dot(qi, kj, "nt") * scale
                lb = jnp.minimum(z, 0.0) - jnp.log(1.0 + jnp.exp(-jnp.abs(z)))
                lf = lb - z
                if on_diagonal:
                    mask = (j * SB_BLOCK + ci) < (i * qblk + ri)
                    lf = jnp.where(mask, lf, 0.0)
                tile_sum = jnp.sum(lf, axis=-1, keepdims=True)
                att = jnp.exp(lb + ((ltot - cpre - tile_sum) + _ones_dot(lf, upper)))
                if on_diagonal:
                    att = jnp.where(mask, att, 0.0)
                e = _dot(d_o, vj, "nt") * att
                dlf = ce + _ones_dot(e, below)
                dzz = e - (e + dlf) * jnp.exp(lb)
                if on_diagonal:
                    dzz = jnp.where(mask, dzz, 0.0)
                dzz = dzz.astype(BF16)
                dq = dq + _dot(dzz, kj, "nn")
                dk_scr[:, cols, :] += _dot(dzz, qi, "tn")
                dv_scr[:, cols, :] += _dot(att.astype(BF16), d_o, "tn")
                return dq, cpre + tile_sum, ce + jnp.sum(e, axis=-1, keepdims=True)

            nkb = qblk // SB_BLOCK
            zero_col = jnp.zeros((hp, qblk, 1), F32)
            st = lax.fori_loop(0, i * nkb, lambda j, s: tile(j, s, False),
                               (jnp.zeros((hp, qblk, D_HEAD), F32), zero_col, zero_col))
            for d in range(nkb):
                st = tile(i * nkb + d, st, True)
            dq = st[0]
            for hh in range(hp):
                dq_ref[rows, hh * D_HEAD:(hh + 1) * D_HEAD] = (dq[hh] * scale).astype(BF16)
            return carry

        lax.fori_loop(0, t // qblk, qblock, 0)
        for hh in range(hp):
            hs = slice(hh * D_HEAD, (hh + 1) * D_HEAD)
            dk_ref[:, hs] = (dk_scr[hh] * scale).astype(BF16)
            dv_ref[:, hs] = dv_scr[hh].astype(BF16)

    def head(off):
        return pl.BlockSpec((t, wid), lambda h: (0, off // wid + h))

    extra_specs, extra = [], []
    if after is not None:
        extra_specs, extra = [pl.BlockSpec(after.shape, lambda h: (0, 0))], [after]
    return pl.pallas_call(
        body, name="sb_bwd", grid=(N_HEADS // hp,),
        in_specs=[head(C_SBQ), head(C_SBQ + D_MODEL), head(C_SBQ + 2 * D_MODEL), head(C_SBZ), head(0),
                  pl.BlockSpec((hp, t, 1), lambda h: (h, 0, 0)), head(0)] + extra_specs,
        out_specs=[head(0)] * 4, out_shape=[jax.ShapeDtypeStruct((t, D_MODEL), BF16)] * 4,
        scratch_shapes=[pltpu.VMEM((hp, t, D_HEAD), BF16)] * 4 + [pltpu.VMEM((hp, t, D_HEAD), F32)] * 2,
        compiler_params=_cp(),
    )(proj, proj, proj, proj, o, ltot, dog, *extra)


def _mem_fwd(proj, mkv):
    t = proj.shape[0]
    tq = _pick(t, (512, 256))
    m_len = mkv.shape[0]
    scale = 1.0 / math.sqrt(MEM_DH)

    def body(q_ref, z_ref, kv_ref, o_ref, og_ref):
        q = q_ref[...]
        mk = kv_ref[:, :MEM_W].astype(BF16)
        mv = kv_ref[:, MEM_W:].astype(BF16)
        lane = lax.broadcasted_iota(jnp.int32, q.shape, 1) >> 6
        o = jnp.zeros(q.shape, F32)
        for h in range(MEM_HEADS):
            s = _bdot(jnp.where(lane == h, q, 0.0), mk, "nt") * scale
            p = jnp.exp(s - jnp.max(s, axis=-1, keepdims=True))
            p = p / jnp.sum(p, axis=-1, keepdims=True)
            o = o + jnp.where(lane == h, _bdot(p, mv, "nn"), 0.0)
        z = z_ref[...]
        o_ref[...] = o
        og_ref[...] = (o * (z * _sigmoid(z))).astype(BF16)

    out = pl.BlockSpec((tq, MEM_W), lambda i: (i, 0))
    return pl.pallas_call(
        body, name="mem_fwd", grid=(t // tq,),
        in_specs=[pl.BlockSpec((tq, MEM_W), lambda i: (i, C_MQ // MEM_W)),
                  pl.BlockSpec((tq, MEM_W), lambda i: (i, C_MZ // MEM_W)),
                  pl.BlockSpec((m_len, 2 * MEM_W), lambda i: (0, 0))],
        out_specs=[out, out],
        out_shape=[jax.ShapeDtypeStruct((t, MEM_W), F32), jax.ShapeDtypeStruct((t, MEM_W), BF16)],
        compiler_params=_cp(),
    )(proj, proj, mkv)


def _mem_bwd(proj, mkv, o, dog):
    t = proj.shape[0]
    tq = _pick(t, (512, 256))
    m_len = mkv.shape[0]
    scale = 1.0 / math.sqrt(MEM_DH)

    def body(q_ref, z_ref, kv_ref, o_ref, d_ref, dq_ref, dz_ref, dkv_ref):
        @pl.when(pl.program_id(0) == 0)
        def _():
            dkv_ref[...] = jnp.zeros_like(dkv_ref)

        q = q_ref[...]
        z = z_ref[...]
        sg = _sigmoid(z)
        dgo = d_ref[...]
        d_o = dgo * (z * sg)
        dz_ref[...] = (dgo * o_ref[...] * (sg * (1.0 + z * (1.0 - sg)))).astype(BF16)
        mk = kv_ref[:, :MEM_W].astype(BF16)
        mv = kv_ref[:, MEM_W:].astype(BF16)
        lane = lax.broadcasted_iota(jnp.int32, q.shape, 1) >> 6
        klane = lax.broadcasted_iota(jnp.int32, (m_len, MEM_W), 1) >> 6
        dq = jnp.zeros(q.shape, F32)
        dmk = jnp.zeros((m_len, MEM_W), F32)
        dmv = jnp.zeros((m_len, MEM_W), F32)
        for h in range(MEM_HEADS):
            qh = jnp.where(lane == h, q, 0.0)
            doh = jnp.where(lane == h, d_o, 0.0)
            s = _bdot(qh, mk, "nt") * scale
            p = jnp.exp(s - jnp.max(s, axis=-1, keepdims=True))
            p = p / jnp.sum(p, axis=-1, keepdims=True)
            dp = _bdot(doh, mv, "nt")
            ds = p * (dp - jnp.sum(dp * p, axis=-1, keepdims=True)) * scale
            dq = dq + jnp.where(lane == h, _bdot(ds, mk, "nn"), 0.0)
            dmk = dmk + jnp.where(klane == h, _bdot(ds, qh, "tn"), 0.0)
            dmv = dmv + jnp.where(klane == h, _bdot(p, doh, "tn"), 0.0)
        dq_ref[...] = dq.astype(BF16)
        dkv_ref[:, :MEM_W] += dmk
        dkv_ref[:, MEM_W:] += dmv

    blk = pl.BlockSpec((tq, MEM_W), lambda i: (i, 0))
    kv = pl.BlockSpec((m_len, 2 * MEM_W), lambda i: (0, 0))
    return pl.pallas_call(
        body, name="mem_bwd", grid=(t // tq,),
        in_specs=[pl.BlockSpec((tq, MEM_W), lambda i: (i, C_MQ // MEM_W)),
                  pl.BlockSpec((tq, MEM_W), lambda i: (i, C_MZ // MEM_W)), kv, blk, blk],
        out_specs=[blk, blk, kv],
        out_shape=[jax.ShapeDtypeStruct((t, MEM_W), BF16), jax.ShapeDtypeStruct((t, MEM_W), BF16),
                   jax.ShapeDtypeStruct((m_len, 2 * MEM_W), F32)], compiler_params=_cp(),
    )(proj, proj, mkv, o, dog)


_GW = 512


def _merge_fwd(proj, y_dn, y_sb, y_m):
    t = proj.shape[0]
    tb = _pick(t, (256,))
    nc = D_MODEL // _GW

    def body(g1, g2, g3, y1, y2, y3, out_ref):
        out_ref[...] = (_sigmoid(g1[...]) * y1[...] + _sigmoid(g2[...]) * y2[...] + _sigmoid(g3[...]) * y3[...]).astype(BF16)

    def gate(kb):
        return pl.BlockSpec((tb, _GW), lambda i, c: (i, C_GATES // _GW + kb * nc + c))

    blk = pl.BlockSpec((tb, _GW), lambda i, c: (i, c))
    return pl.pallas_call(
        body, name="merge_fwd", grid=(t // tb, nc), in_specs=[gate(0), gate(1), gate(2), blk, blk, blk],
        out_specs=blk, out_shape=jax.ShapeDtypeStruct((t, D_MODEL), BF16), compiler_params=_cp(),
    )(proj, proj, proj, y_dn, y_sb, y_m)


def _merge_bwd(proj, y_dn, y_sb, y_m, dm):
    t = proj.shape[0]
    tb = _pick(t, (256,))
    nc = D_MODEL // _GW

    def body(g1, g2, g3, y1, y2, y3, dm_ref, d1, d2, d3, dg1, dg2, dg3):
        d = dm_ref[...]
        for g, y, dy, dg in ((g1, y1, d1, dg1), (g2, y2, d2, dg2), (g3, y3, d3, dg3)):
            s = _sigmoid(g[...])
            dy[...] = (d * s).astype(BF16)
            dg[...] = (d * y[...] * (s * (1.0 - s))).astype(BF16)

    def gate(kb):
        return pl.BlockSpec((tb, _GW), lambda i, c: (i, C_GATES // _GW + kb * nc + c))

    blk = pl.BlockSpec((tb, _GW), lambda i, c: (i, c))
    act = jax.ShapeDtypeStruct((t, D_MODEL), BF16)
    return pl.pallas_call(
        body, name="merge_bwd", grid=(t // tb, nc), in_specs=[gate(0), gate(1), gate(2), blk, blk, blk, blk],
        out_specs=[blk] * 6, out_shape=[act] * 6, compiler_params=_cp(),
    )(proj, proj, proj, y_dn, y_sb, y_m, dm)


def _final_loss(x, mo, g, tgt):
    t, d = x.shape
    tb = _pick(t, (256,))

    def body(x_ref, mo_ref, g_ref, t_ref, do_ref, dob_ref, loss_ref, dg_ref):
        @pl.when(pl.program_id(0) == 0)
        def _():
            loss_ref[...] = jnp.zeros_like(loss_ref)
            dg_ref[...] = jnp.zeros_like(dg_ref)

        out = x_ref[...] + mo_ref[...]
        r = lax.rsqrt(jnp.mean(out * out, axis=-1, keepdims=True) + NORM_EPS)
        xhat = out * r
        gv = g_ref[...]
        err = xhat * gv - t_ref[...]
        per_tok = jnp.mean(err * err, axis=-1, keepdims=True)
        loss_ref[...] += 0.5 * jnp.sum(per_tok, axis=0, keepdims=True)
        dy = err * (1.0 / d)
        dg_ref[...] += jnp.sum(dy * xhat, axis=0, keepdims=True)
        dxh = dy * gv
        dout = r * (dxh - xhat * jnp.mean(dxh * xhat, axis=-1, keepdims=True))
        do_ref[...] = dout
        dob_ref[...] = dout.astype(BF16)

    row = pl.BlockSpec((tb, d), lambda i: (i, 0))
    vec = pl.BlockSpec((1, d), lambda i: (0, 0))
    return pl.pallas_call(
        body, name="final_loss", grid=(t // tb,), in_specs=[row, row, vec, row],
        out_specs=[row, row, pl.BlockSpec((1, 128), lambda i: (0, 0)), vec],
        out_shape=[jax.ShapeDtypeStruct((t, d), F32), jax.ShapeDtypeStruct((t, d), BF16),
                   jax.ShapeDtypeStruct((1, 128), F32), jax.ShapeDtypeStruct((1, d), F32)],
        compiler_params=_cp(),
    )(x, mo, g, tgt)


def _cast_bf16(a, name):
    r, c = a.shape
    tb = _pick(r, (128, 496, 240))

    def body(a_ref, o_ref):
        o_ref[...] = a_ref[...].astype(BF16)

    blk = pl.BlockSpec((tb, c), lambda i: (i, 0))
    return pl.pallas_call(body, name=name, grid=(r // tb,), in_specs=[blk], out_specs=blk,
                          out_shape=jax.ShapeDtypeStruct((r, c), BF16), compiler_params=_cp())(a)


WIN_START = (0, 23, 45, 68)
_S1_LO, _S1_HI = 1148, 1164
_S1_BA_POS = SHARD_PAD - 128


def _to_window(x, s):
    if s == 0:
        return x
    if s in (2, 3):
        return pltpu.roll(x, 120 if s == 2 else 124, 1)
    pos = lax.broadcasted_iota(jnp.int32, x.shape, 1)
    head = pltpu.roll(x, 4, 1)
    tail = pltpu.roll(x, SHARD_PAD - 12, 1)
    ba = jnp.where(pos < _S1_BA_POS + (_S1_HI - _S1_LO), pltpu.roll(x, _S1_BA_POS - _S1_LO, 1), 0.0)
    return jnp.where(pos < _S1_LO + 4, head, jnp.where(pos < _S1_BA_POS, tail, ba))


def _from_window(g, s):
    if s == 0:
        return g
    if s in (2, 3):
        return pltpu.roll(g, SHARD_PAD - (120 if s == 2 else 124), 1)
    col = lax.broadcasted_iota(jnp.int32, g.shape, 1)
    head = pltpu.roll(g, SHARD_PAD - 4, 1)
    tail = pltpu.roll(g, 12, 1)
    ba = pltpu.roll(g, SHARD_PAD - (_S1_BA_POS - _S1_LO), 1)
    return jnp.where(col < _S1_LO, head, jnp.where(col < _S1_HI, ba, tail))


def _cast_to_window(w, shard, name):
    r, c = w.shape
    tb = _pick(r, (128,))

    def body(s_ref, w_ref, o_ref, pad_scr):
        pad_scr[...] = jnp.zeros_like(pad_scr)
        pad_scr[:, :c] = w_ref[...]
        x = pad_scr[...]
        for s in range(N_SHARD):
            @pl.when(s_ref[0] == s)
            def _():
                o_ref[...] = _to_window(x, s).astype(BF16)

    return pl.pallas_call(
        body, name=name,
        grid_spec=pltpu.PrefetchScalarGridSpec(
            num_scalar_prefetch=1, grid=(r // tb,),
            in_specs=[pl.BlockSpec((tb, c), lambda i, s: (i, 0))],
            out_specs=pl.BlockSpec((tb, SHARD_PAD), lambda i, s: (i, 0)),
            scratch_shapes=[pltpu.VMEM((tb, SHARD_PAD), F32)]),
        out_shape=jax.ShapeDtypeStruct((r, SHARD_PAD), BF16), compiler_params=_cp(),
    )(shard, w)


def _pair_add(g, recv, c_idx, name):
    n, r, c = g.shape
    half = r // 2
    tb = _pick(half, (128, 240))
    nb = half // tb

    def body(c_ref, g_ref, r_ref, o_ref):
        o_ref[...] = (g_ref[...].astype(F32) + r_ref[...].astype(F32)).astype(BF16)

    blk = pl.BlockSpec((n, tb, c), lambda i, c_ref: (0, i, 0))
    return pl.pallas_call(
        body, name=name,
        grid_spec=pltpu.PrefetchScalarGridSpec(
            num_scalar_prefetch=1, grid=(nb,),
            in_specs=[pl.BlockSpec((n, tb, c), lambda i, c_ref: (0, c_ref[0] * nb + i, 0)), blk], out_specs=blk),
        out_shape=jax.ShapeDtypeStruct((n, half, c), BF16), compiler_params=_cp(),
    )(c_idx, g, recv)


def _chip_sum(parts, by_chip, place, name):
    n, h, c = parts.shape
    tb = _pick(h, (128, 240))
    nb = h // tb

    def body(p_ref, mine_ref, *rest):
        others, o_ref = rest[:n], rest[n]
        me = jnp.zeros((tb, c), jnp.int32) + p_ref[0]
        acc = None
        for q in range(n):
            term = jnp.where(me == q, mine_ref[...], others[q][...]).astype(F32)
            acc = term if acc is None else acc + term
        o_ref[...] = acc

    def other(q):
        return pl.BlockSpec((None, tb, c), lambda i, p: (jnp.where(p[0] == q, (q + 1) % n, q), i, 0))

    return pl.pallas_call(
        body, name=name,
        grid_spec=pltpu.PrefetchScalarGridSpec(
            num_scalar_prefetch=1, grid=(nb,),
            in_specs=[pl.BlockSpec((None, tb, c), lambda i, p: (p[0], i, 0))] + [other(q) for q in range(n)],
            out_specs=pl.BlockSpec((tb, c), lambda i, p: (p[1] * nb + i, 0))),
        out_shape=jax.ShapeDtypeStruct((2 * h, c), F32), compiler_params=_cp(),
    )(place, parts, *([by_chip] * n))


def _adamw_math(w, g, m, v):
    m = ADAM_B1 * m + (1.0 - ADAM_B1) * g
    v = ADAM_B2 * v + (1.0 - ADAM_B2) * (g * g)
    m_hat = m / (1.0 - ADAM_B1 ** ADAM_STEP)
    v_hat = v / (1.0 - ADAM_B2 ** ADAM_STEP)
    delta = -ADAM_LR * (m_hat / (jnp.sqrt(v_hat) + ADAM_EPS) + ADAM_WD * w)
    return delta, m, v


def _adamw(w, g, m, v, name):
    r, c = w.shape
    tb = _pick(r, (128, 496, 240))

    def body(w_ref, g_ref, m_ref, v_ref, go_ref, d_ref, mo_ref, vo_ref):
        gv = g_ref[...]
        d, mn, vn = _adamw_math(w_ref[...], gv, m_ref[...], v_ref[...])
        go_ref[...] = gv
        d_ref[...] = d
        mo_ref[...] = mn
        vo_ref[...] = vn

    blk = pl.BlockSpec((tb, c), lambda i: (i, 0))
    return pl.pallas_call(
        body, name=name, grid=(r // tb,), in_specs=[blk] * 4, out_specs=[blk] * 4,
        out_shape=[jax.ShapeDtypeStruct((r, c), F32)] * 4, compiler_params=_cp(),
    )(w, g, m, v)


def _adamw_window(w, g_win, m, v, shard, name):
    r, c = w.shape
    tb = _pick(r, (128,))

    def body(s_ref, w_ref, g_ref, m_ref, v_ref, go_ref, d_ref, mo_ref, vo_ref, g_scr):
        gw = g_ref[...]
        for s in range(N_SHARD):
            @pl.when(s_ref[0] == s)
            def _():
                g_scr[...] = _from_window(gw, s)

        gv = g_scr[:, :c]
        d, mn, vn = _adamw_math(w_ref[...], gv, m_ref[...], v_ref[...])
        go_ref[...] = gv
        d_ref[...] = d
        mo_ref[...] = mn
        vo_ref[...] = vn

    blk = pl.BlockSpec((tb, c), lambda i, s: (i, 0))
    return pl.pallas_call(
        body, name=name,
        grid_spec=pltpu.PrefetchScalarGridSpec(
            num_scalar_prefetch=1, grid=(r // tb,),
            in_specs=[blk, pl.BlockSpec((tb, SHARD_PAD), lambda i, s: (i, 0)), blk, blk], out_specs=[blk] * 4,
            scratch_shapes=[pltpu.VMEM((tb, SHARD_PAD), F32)]),
        out_shape=[jax.ShapeDtypeStruct((r, c), F32)] * 4, compiler_params=_cp(),
    )(shard, w, g_win, m, v)


def _small_update(gathered, w, m, v):
    def body(p_ref, w_ref, m_ref, v_ref, g_ref, d_ref, mo_ref, vo_ref):
        g = p_ref[0]
        for i in range(1, N_DEV):
            g = g + p_ref[i]
        d, mn, vn = _adamw_math(w_ref[...], g, m_ref[...], v_ref[...])
        g_ref[...] = g
        d_ref[...] = d
        mo_ref[...] = mn
        vo_ref[...] = vn

    full = pl.BlockSpec((S_ROWS, 128), lambda i: (0, 0))
    return pl.pallas_call(
        body, name="small_update", grid=(1,),
        in_specs=[pl.BlockSpec((N_DEV, S_ROWS, 128), lambda i: (0, 0, 0)), full, full, full], out_specs=[full] * 4,
        out_shape=[jax.ShapeDtypeStruct((S_ROWS, 128), F32)] * 4, compiler_params=_cp(),
    )(gathered, w, m, v)


_ANY = pl.BlockSpec(memory_space=pl.ANY)


def _place():
    x, y, c = lax.axis_index("x"), lax.axis_index("y"), lax.axis_index("c")
    chips = [(1 - x, y), (x, 1 - y), (1 - x, 1 - y)]
    return x, y, c, chips


def _pair_reduce_send(grads, tag):
    n = len(grads)

    def body(*refs):
        ins, outs = refs[:n], refs[n:2 * n]
        send_sems, recv_sems = refs[2 * n:]
        x, y, c, _ = _place()
        sibling = (x, y, 1 - c)
        cps = []
        for a in range(n):
            half = ins[a].shape[1] // 2
            theirs = pl.ds(pl.multiple_of((1 - c) * half, 8), half)
            cp = pltpu.make_async_remote_copy(
                src_ref=ins[a].at[:, theirs], dst_ref=outs[a], send_sem=send_sems.at[a], recv_sem=recv_sems.at[a],
                device_id=sibling, device_id_type=MESH)
            cp.start()
            cps.append(cp)
        for cp in cps:
            cp.wait()

    return pl.pallas_call(
        body, name="pair_reduce_send_" + tag, in_specs=[_ANY] * n, out_specs=[_ANY] * n,
        out_shape=[jax.ShapeDtypeStruct((g.shape[0], g.shape[1] // 2, g.shape[2]), g.dtype) for g in grads],
        scratch_shapes=[pltpu.SemaphoreType.DMA((n,)), pltpu.SemaphoreType.DMA((n,))],
        compiler_params=pltpu.CompilerParams(has_side_effects=True),
    )(*grads)


_HBM = pl.BlockSpec(memory_space=pltpu.HBM)
_SEM = pl.BlockSpec(memory_space=pltpu.SEMAPHORE)
_DATAFLOW = pltpu.SideEffectType.DATAFLOW_SIDE_EFFECTING


def _chip_exchange_copies(ins, lands, send_sems, recv_sems):
    x, y, c, chips = _place()
    me = 2 * x + y
    cps = []
    for a in range(len(ins)):
        for j, (qx, qy) in enumerate(chips):
            cps.append(pltpu.make_async_remote_copy(
                src_ref=ins[a].at[2 * qx + qy], dst_ref=lands[a].at[me], send_sem=send_sems.at[3 * a + j],
                recv_sem=recv_sems.at[3 * a + j], device_id=(qx, qy, c), device_id_type=MESH))
    return cps


def _chip_exchange_start(parts, tag):
    n = len(parts)

    def body(*refs):
        ins, lands = refs[:n], refs[n:2 * n]
        send_sems, recv_sems = refs[2 * n:2 * n + 2]
        token = refs[4 * n + 2]
        for cp in _chip_exchange_copies(ins, lands, send_sems, recv_sems):
            cp.start()
        token[...] = jnp.zeros_like(token)

    hbm = [pltpu.HBM(p.shape, p.dtype) for p in parts]
    lands = [pltpu.with_memory_space_constraint(lax.empty(p.shape, p.dtype), pltpu.HBM) for p in parts]
    res = pl.pallas_call(
        body, name="chip_exchange_start_" + tag,
        out_shape=(pltpu.SemaphoreType.DMA((3 * n,)), pltpu.SemaphoreType.DMA((3 * n,)), *hbm, *hbm,
                   jax.ShapeDtypeStruct((8, 128), F32)),
        in_specs=[_HBM] * (2 * n), out_specs=(_SEM, _SEM, *([_HBM] * (2 * n)), pl.BlockSpec(memory_space=pltpu.VMEM)),
        input_output_aliases={a: 2 + a for a in range(2 * n)},
        compiler_params=pltpu.CompilerParams(has_side_effects=_DATAFLOW),
    )(*[pltpu.with_memory_space_constraint(p, pltpu.HBM) for p in parts], *lands)
    return res[0], res[1], res[2:2 + n], res[2 + n:2 + 2 * n], res[2 + 2 * n]


def _chip_exchange_wait(send_sems, recv_sems, parts, lands, after, tag):
    n = len(parts)

    def body(*refs):
        ins, land_refs = refs[:n], refs[n:2 * n]
        s_sems, r_sems = refs[2 * n:2 * n + 2]
        for cp in _chip_exchange_copies(ins, land_refs, s_sems, r_sems):
            cp.wait_send()
            cp.wait_recv()

    hbm = [pltpu.HBM(p.shape, p.dtype) for p in parts]
    res = pl.pallas_call(
        body, name="chip_exchange_wait_" + tag, out_shape=(*hbm, *hbm),
        in_specs=[_HBM] * (2 * n) + [_SEM, _SEM, _ANY], out_specs=tuple([_HBM] * (2 * n)),
        input_output_aliases={a: a for a in range(2 * n)},
        compiler_params=pltpu.CompilerParams(has_side_effects=_DATAFLOW),
    )(*parts, *lands, send_sems, recv_sems, after)
    return res[:n], res[n:]


def _halves_ici_copies(srcs, lands, send_sems, recv_sems):
    x, y, c, chips = _place()
    me = 2 * x + y
    cps = []
    for a, src in enumerate(srcs):
        half = src.shape[0] // 2
        mine = pl.ds(pl.multiple_of(c * half, 16), half)
        for j, (qx, qy) in enumerate(chips):
            cps.append(pltpu.make_async_remote_copy(
                src_ref=src.at[mine], dst_ref=lands[a].at[me, mine], send_sem=send_sems.at[3 * a + j],
                recv_sem=recv_sems.at[3 * a + j], device_id=(qx, qy, c), device_id_type=MESH))
    return cps


def _halves_d2d_copies(lands, send_sems, recv_sems):
    x, y, c, chips = _place()
    cps = []
    for a, land in enumerate(lands):
        half = land.shape[1] // 2
        mine = pl.ds(pl.multiple_of(c * half, 16), half)
        for j, (qx, qy) in enumerate(chips):
            region = land.at[2 * qx + qy, mine]
            cps.append(pltpu.make_async_remote_copy(
                src_ref=region, dst_ref=region, send_sem=send_sems.at[3 * a + j], recv_sem=recv_sems.at[3 * a + j],
                device_id=(x, y, 1 - c), device_id_type=MESH))
    return cps


def _halves_gather_start(shards):
    n = len(shards)

    def body(*refs):
        srcs, lands = refs[:n], refs[n:2 * n]
        send_sems, recv_sems = refs[2 * n:2 * n + 2]
        for cp in _halves_ici_copies(srcs, lands, send_sems, recv_sems):
            cp.start()
        refs[4 * n + 2][...] = jnp.zeros((8, 128), F32)

    hbm_s = [pltpu.HBM(s.shape, s.dtype) for s in shards]
    hbm_l = [pltpu.HBM((N_SHARD,) + s.shape, s.dtype) for s in shards]
    lands = [pltpu.with_memory_space_constraint(lax.empty((N_SHARD,) + s.shape, s.dtype), pltpu.HBM) for s in shards]
    res = pl.pallas_call(
        body, name="halves_gather_start",
        out_shape=(pltpu.SemaphoreType.DMA((3 * n,)), pltpu.SemaphoreType.DMA((3 * n,)), *hbm_s, *hbm_l,
                   jax.ShapeDtypeStruct((8, 128), F32)),
        in_specs=[_HBM] * (2 * n), out_specs=(_SEM, _SEM, *([_HBM] * (2 * n)), pl.BlockSpec(memory_space=pltpu.VMEM)),
        input_output_aliases={a: 2 + a for a in range(2 * n)},
        compiler_params=pltpu.CompilerParams(has_side_effects=_DATAFLOW),
    )(*[pltpu.with_memory_space_constraint(s, pltpu.HBM) for s in shards], *lands)
    return res[0], res[1], res[2:2 + n], res[2 + n:2 + 2 * n], res[2 + 2 * n]


def _halves_gather_forward(send1, recv1, shards, lands, after):
    n = len(shards)

    def body(*refs):
        srcs, land_refs = refs[:n], refs[n:2 * n]
        s1, r1 = refs[2 * n:2 * n + 2]
        outs = refs[2 * n + 2 + len(after):]
        s2, r2 = outs[0], outs[1]
        for cp in _halves_ici_copies(srcs, land_refs, s1, r1):
            cp.wait_send()
            cp.wait_recv()
        for cp in _halves_d2d_copies(land_refs, s2, r2):
            cp.start()

    hbm_s = [pltpu.HBM(s.shape, s.dtype) for s in shards]
    hbm_l = [pltpu.HBM(l.shape, l.dtype) for l in lands]
    res = pl.pallas_call(
        body, name="halves_gather_forward",
        out_shape=(pltpu.SemaphoreType.DMA((3 * n,)), pltpu.SemaphoreType.DMA((3 * n,)), *hbm_s, *hbm_l),
        in_specs=[_HBM] * (2 * n) + [_SEM, _SEM] + [_ANY] * len(after),
        out_specs=(_SEM, _SEM, *([_HBM] * (2 * n))), input_output_aliases={a: 2 + a for a in range(2 * n)},
        compiler_params=pltpu.CompilerParams(has_side_effects=_DATAFLOW),
    )(*shards, *lands, send1, recv1, *after)
    return res[0], res[1], res[2 + n:2 + 2 * n]


def _halves_gather_wait(send2, recv2, lands):
    n = len(lands)

    def body(*refs):
        land_refs = refs[:n]
        s2, r2 = refs[n:n + 2]
        for cp in _halves_d2d_copies(land_refs, s2, r2):
            cp.wait_send()
            cp.wait_recv()

    hbm_l = [pltpu.HBM(l.shape, l.dtype) for l in lands]
    res = pl.pallas_call(
        body, name="halves_gather_wait", out_shape=tuple(hbm_l),
        in_specs=[_HBM] * n + [_SEM, _SEM], out_specs=tuple([_HBM] * n),
        input_output_aliases={a: a for a in range(n)},
        compiler_params=pltpu.CompilerParams(has_side_effects=_DATAFLOW),
    )(*lands, send2, recv2)
    return list(res)


def _shard_gather_copies(src, land, send_sems, recv_sems):
    x, y, c, chips = _place()
    me = 2 * x + y
    return [pltpu.make_async_remote_copy(
        src_ref=src, dst_ref=land.at[me], send_sem=send_sems.at[j], recv_sem=recv_sems.at[j],
        device_id=(qx, qy, c), device_id_type=MESH) for j, (qx, qy) in enumerate(chips)]


def _shard_gather_start(shard_arr, after):
    def body(src, land, after_ref, send_sems, recv_sems, src_thru, land_thru, token):
        for cp in _shard_gather_copies(src, land, send_sems, recv_sems):
            cp.start()
        token[...] = jnp.zeros_like(token)

    land_shape = (N_SHARD,) + shard_arr.shape
    land = pltpu.with_memory_space_constraint(lax.empty(land_shape, shard_arr.dtype), pltpu.HBM)
    return pl.pallas_call(
        body, name="shard_gather_start",
        out_shape=(pltpu.SemaphoreType.DMA((N_SHARD - 1,)), pltpu.SemaphoreType.DMA((N_SHARD - 1,)),
                   pltpu.HBM(shard_arr.shape, shard_arr.dtype), pltpu.HBM(land_shape, shard_arr.dtype),
                   jax.ShapeDtypeStruct((8, 128), F32)),
        in_specs=[_HBM, _HBM, _ANY], out_specs=(_SEM, _SEM, _HBM, _HBM, pl.BlockSpec(memory_space=pltpu.VMEM)),
        input_output_aliases={0: 2, 1: 3},
        compiler_params=pltpu.CompilerParams(has_side_effects=_DATAFLOW),
    )(pltpu.with_memory_space_constraint(shard_arr, pltpu.HBM), land, after)


def _shard_gather_wait(send_sems, recv_sems, shard_arr, land, after):
    def body(src, land_ref, s_sems, r_sems, after_ref, src_out, land_out):
        for cp in _shard_gather_copies(src, land_ref, s_sems, r_sems):
            cp.wait_send()
            cp.wait_recv()

    return pl.pallas_call(
        body, name="shard_gather_wait",
        out_shape=(pltpu.HBM(shard_arr.shape, shard_arr.dtype), pltpu.HBM(land.shape, land.dtype)),
        in_specs=[_HBM, _HBM, _SEM, _SEM, _ANY], out_specs=(_HBM, _HBM), input_output_aliases={0: 0, 1: 1},
        compiler_params=pltpu.CompilerParams(has_side_effects=_DATAFLOW),
    )(shard_arr, land, send_sems, recv_sems, after)


def _pair_allgather(fulls, tag):
    n = len(fulls)

    def body(*refs):
        outs = refs[n:2 * n]
        send_sems, recv_sems = refs[2 * n:]
        x, y, c, _ = _place()
        sibling = (x, y, 1 - c)
        cps = []
        for a in range(n):
            half = outs[a].shape[0] // 2
            mine = outs[a].at[pl.ds(pl.multiple_of(c * half, 8), half)]
            cp = pltpu.make_async_remote_copy(
                src_ref=mine, dst_ref=mine, send_sem=send_sems.at[a], recv_sem=recv_sems.at[a],
                device_id=sibling, device_id_type=MESH)
            cp.start()
            cps.append(cp)
        for a in range(n):
            half = outs[a].shape[0] // 2
            theirs = outs[a].at[pl.ds(pl.multiple_of((1 - c) * half, 8), half)]
            pltpu.make_async_remote_copy(
                src_ref=theirs, dst_ref=theirs, send_sem=send_sems.at[a], recv_sem=recv_sems.at[a],
                device_id=sibling, device_id_type=MESH).wait_recv()
        for cp in cps:
            cp.wait_send()

    return pl.pallas_call(
        body, name="pair_allgather_" + tag, in_specs=[_ANY] * n, out_specs=[_ANY] * n,
        out_shape=[jax.ShapeDtypeStruct(f.shape, f.dtype) for f in fulls],
        input_output_aliases={a: a for a in range(n)},
        scratch_shapes=[pltpu.SemaphoreType.DMA((n,)), pltpu.SemaphoreType.DMA((n,))],
        compiler_params=pltpu.CompilerParams(has_side_effects=True),
    )(*fulls)


def _allgather_small(slab, after):
    def body(s_ref, after_ref, out_ref, send_sems, recv_sems):
        x, y, c, _ = _place()
        me = 4 * x + 2 * y + c
        out_ref[me] = s_ref[...]
        cps = []
        for mask in range(1, N_DEV):
            peer = (x ^ (mask >> 2), y ^ ((mask >> 1) & 1), c ^ (mask & 1))
            cp = pltpu.make_async_remote_copy(
                src_ref=s_ref, dst_ref=out_ref.at[me], send_sem=send_sems.at[mask - 1], recv_sem=recv_sems.at[mask - 1],
                device_id=peer, device_id_type=MESH)
            cp.start()
            cps.append(cp)
        for mask in range(1, N_DEV):
            peer = (x ^ (mask >> 2), y ^ ((mask >> 1) & 1), c ^ (mask & 1))
            dst = out_ref.at[4 * peer[0] + 2 * peer[1] + peer[2]]
            pltpu.make_async_remote_copy(
                src_ref=dst, dst_ref=dst, send_sem=send_sems.at[mask - 1], recv_sem=recv_sems.at[mask - 1],
                device_id=peer, device_id_type=MESH).wait_recv()
        for cp in cps:
            cp.wait_send()

    vm = pl.BlockSpec(memory_space=pltpu.VMEM)
    return pl.pallas_call(
        body, name="allgather_small", in_specs=[vm, _ANY], out_specs=vm,
        out_shape=jax.ShapeDtypeStruct((N_DEV,) + slab.shape, slab.dtype),
        scratch_shapes=[pltpu.SemaphoreType.DMA((N_DEV - 1,)), pltpu.SemaphoreType.DMA((N_DEV - 1,))],
        compiler_params=pltpu.CompilerParams(has_side_effects=True),
    )(slab, after)


def _pack_b(w_mem_kv, w_br_dn, w_br_sb, w_br_mem, w_out):
    return jnp.concatenate([w_mem_kv.reshape(128, D_MODEL), w_br_dn, w_br_sb, w_br_mem.reshape(64, D_MODEL), w_out],
                           axis=0)


def _conv_slab(conv_w):
    return jnp.pad(conv_w.reshape(3, D_MODEL), ((0, 29), (0, 0)))


def _unpack_b(slab):
    return (slab[B_MEMKV:B_BRDN].reshape(1, 256, 512), slab[B_BRDN:B_BRSB].reshape(1, 256, D_MODEL),
            slab[B_BRSB:B_BRMEM].reshape(1, 256, D_MODEL), slab[B_BRMEM:B_OUT].reshape(1, 256, 256),
            slab[B_OUT:B_CONV].reshape(1, 256, D_MODEL))


def _conv_rows(conv_full):
    return conv_full.reshape(4 * CONV_BLOCKS, 128)


def _conv_shard_rows(conv_shard, shard):
    own = CONV_BLOCKS // N_SHARD
    blocks = lax.dynamic_update_slice(jnp.zeros((4, CONV_BLOCKS, 128), F32), conv_shard.reshape(4, own, 128),
                                      (0, own * shard, 0))
    return blocks.reshape(4 * CONV_BLOCKS, 128)


def _conv_shard_of(rows, shard):
    own = CONV_BLOCKS // N_SHARD
    blocks = lax.dynamic_slice(rows.reshape(4, CONV_BLOCKS, 128), (0, own * shard, 0), (4, own, 128))
    return blocks.reshape(1, 4, own * 128)


def _pack_small(norm_g, mem_norm_g, final_g, dn_norm_g, a_log, dt_bias, conv_rows, loss=None):
    def row(v):
        v = v.reshape(1, -1).astype(F32)
        return jnp.pad(v, ((0, 0), (0, 128 - v.shape[1])))

    loss_row = row(jnp.zeros((1,), F32) if loss is None else jnp.reshape(loss, (1,)))
    rid = lax.broadcasted_iota(jnp.int32, (8, 128), 0) + S_DNNORM
    tile = jnp.where(rid == S_DNNORM, dn_norm_g.reshape(1, 128), jnp.where(
        rid == S_ALOG, row(a_log), jnp.where(rid == S_DTB, row(dt_bias), jnp.where(rid == S_LOSS, loss_row, 0.0))))
    return jnp.concatenate([norm_g.reshape(8, 128), mem_norm_g.reshape(8, 128), final_g.reshape(8, 128), tile,
                            conv_rows], axis=0)


def _unpack_small(slab, shard):
    return (slab[S_NORM:S_NORM + 8].reshape(1, D_MODEL), slab[S_MEMNORM:S_MEMNORM + 8].reshape(1, D_MODEL),
            slab[S_FINAL:S_FINAL + 8].reshape(D_MODEL), slab[S_DNNORM].reshape(1, 128),
            slab[S_ALOG, :N_HEADS].reshape(1, N_HEADS), slab[S_DTB, :N_HEADS].reshape(1, N_HEADS),
            _conv_shard_of(slab[S_CONV:], shard))


def _windows_to_w_r(win):
    b = 128
    s0, s1, s2, s3 = win[0], win[1], win[2], win[3]
    e1, e2, e3 = WIN_START[1] * b, WIN_START[2] * b, WIN_START[3] * b
    n1, n2 = e2 - e1, e3 - e2
    return jnp.concatenate([
        s0[:, :e1], s0[:, e1:e1 + b] + s1[:, :b],
        s1[:, b:n1], s1[:, n1:n1 + b] + s2[:, :b],
        s2[:, b:n2], s2[:, n2:n2 + b] + s3[:, :b],
        s3[:, b:], s1[:, _S1_BA_POS:], jnp.zeros((win.shape[1], W_R - C_BA - b), win.dtype)], axis=1)


def _dproj_windows(dproj_r):
    b = 128
    pieces = []
    for s in range(N_SHARD):
        lo = WIN_START[s] * b
        if s == 1:
            pieces += [dproj_r[:, lo:lo + _S1_BA_POS], dproj_r[:, C_BA:C_BA + b]]
        else:
            pieces.append(dproj_r[:, lo:lo + SHARD_PAD])
    return jnp.concatenate(pieces, axis=1)


def _local_step(x, mem, tgt, norm_g, mem_norm_g, w_r, w_sh, conv_w, a_log, dt_bias, dn_norm_g, proj_weights, final_g,
                on_early=None, after_gather=None, h=None, mem_n=None):
    t = x.shape[0]
    final_row = final_g.reshape(1, D_MODEL)
    lanes_8_16 = ((0, 0), (N_HEADS, 128 - 2 * N_HEADS))
    alog_row = jnp.pad(a_log.reshape(1, N_HEADS), lanes_8_16)
    dtb_row = jnp.pad(dt_bias.reshape(1, N_HEADS), lanes_8_16)

    if h is None:
        h = _rmsnorm_fwd(x, norm_g, "norm_fwd")
    proj = _mm(h, w_r, "nn", "in_proj", after=after_gather, tm_max=2048)
    qkv = _dn_prep_fwd(proj, conv_w)
    beta_t, g_t = _dn_gate_fwd(proj, alog_row, dtb_row)
    dn_u, dn_w, dn_qd, dn_kd, dn_a, tinv_all, dn_el = _dn_intra_fwd(qkv, beta_t, g_t)
    o_dn, dn_vn, s_all = _dn_scan_fwd(dn_u, dn_w, dn_qd, dn_kd, dn_a, dn_el)
    o_dn_g = _dn_post_fwd(o_dn, proj, dn_norm_g)
    o_sb, o_sb_g, sb_l = _sb_fwd(proj)
    w_mem_kv, w_br_dn, w_br_sb, w_br_mem, w_out = proj_weights(o_sb_g)
    if mem_n is None:
        mem_n = _rmsnorm_fwd(mem, mem_norm_g, "mem_norm_fwd")
    mkv = _mm(mem_n, w_mem_kv, "nn", "mem_kv")
    o_m, o_m_g = _mem_fwd(proj, mkv)
    y_dn = _mm(o_dn_g, w_br_dn, "nn", "br_dn", out_dtype=BF16)
    y_sb = _mm(o_sb_g, w_br_sb, "nn", "br_sb", out_dtype=BF16)
    y_m = _mm(o_m_g, w_br_mem, "nn", "br_mem", out_dtype=BF16)
    merged = _merge_fwd(proj, y_dn, y_sb, y_m)
    mo = _mm(merged, w_out, "nn", "out_proj")
    d_out, d_out_b, loss_row, g_final = _final_loss(x, mo, final_row, tgt)

    g_w_out = _mm(merged, d_out_b, "tn", "g_w_out", out_dtype=BF16)
    d_merged = _mm(d_out_b, w_out, "nt", "d_merged")
    dy_dn, dy_sb, dy_m, dg1, dg2, dg3 = _merge_bwd(proj, y_dn, y_sb, y_m, d_merged)
    g_w_br_dn = _mm(o_dn_g, dy_dn, "tn", "g_w_br_dn", out_dtype=BF16)
    g_w_br_sb = _mm(o_sb_g, dy_sb, "tn", "g_w_br_sb", out_dtype=BF16)
    g_w_br_mem = _mm(o_m_g, dy_m, "tn", "g_w_br_mem", out_dtype=BF16)
    d_o_dn_g = _mm(dy_dn, w_br_dn, "nt", "d_o_dn")
    d_o_sb_g = _mm(dy_sb, w_br_sb, "nt", "d_o_sb")
    d_o_m_g = _mm(dy_m, w_br_mem, "nt", "d_o_mem")

    d_mq, d_mz, d_mkv = _mem_bwd(proj, mkv, o_m, d_o_m_g)
    d_mkv_b = _cast_bf16(d_mkv, "cast_dmkv")
    g_w_mem_kv = _mm(mem_n, d_mkv_b, "tn", "g_w_mem_kv", out_dtype=BF16)
    d_mem_n = _mm(d_mkv_b, w_mem_kv, "nt", "d_mem_n")
    _, g_mem_norm = _rmsnorm_bwd(mem, mem_norm_g, d_mem_n, jnp.zeros_like(mem), "mem_norm_bwd")

    early = dict(w_mem_kv=g_w_mem_kv, w_br_dn=g_w_br_dn, w_br_sb=g_w_br_sb, w_br_mem=g_w_br_mem, w_out=g_w_out)
    after_early = on_early(early) if on_early is not None else None

    d_sq, d_sk, d_sv, d_sz = _sb_bwd(proj, o_sb, sb_l, d_o_sb_g, after=after_early)

    d_o_dn, d_dnz, g_dn_norm = _dn_post_bwd(o_dn, proj, dn_norm_g, d_o_dn_g)
    d_vnew, d_kd, d_qd, d_w, d_el = _dn_scan_bwd(dn_w, dn_qd, dn_kd, dn_a, dn_el, dn_vn, s_all, d_o_dn)
    d_qn, d_kn, d_vn, dbeta_t, dg_t = _dn_intra_bwd(qkv, beta_t, g_t, tinv_all, dn_vn, d_o_dn, d_vnew, d_kd, d_qd, d_w, d_el)
    d_conv_in, g_conv = _dn_prep_bwd(proj, conv_w, d_qn, d_kn, d_vn)
    d_ba, g_alog_row, g_dtb_row = _dn_gate_bwd(proj, alog_row, dtb_row, dbeta_t, dg_t)

    dproj_sh = _dproj_windows(
        jnp.concatenate([d_conv_in, d_dnz, d_sq, d_sk, d_sv, d_sz, d_mq, d_mz, dg1, dg2, dg3, d_ba], axis=1))
    g_w_sh = _mm(h, dproj_sh, "tn", "g_w_in", out_dtype=BF16, out_shards=N_SHARD, tn_max=1024)
    def input_grad(after=None):
        dh = _mm(dproj_sh, w_sh, "nt", "d_h", after=after, tm_max=2048, tn_max=1024)
        grad_x, g_norm = _rmsnorm_bwd(x, norm_g, dh, d_out, "norm_bwd")
        small = dict(norm_g=g_norm, mem_norm_g=g_mem_norm, final_g=g_final, dn_norm_g=g_dn_norm,
                     a_log=g_alog_row[:, N_HEADS:2 * N_HEADS], dt_bias=g_dtb_row[:, N_HEADS:2 * N_HEADS],
                     conv_w=g_conv)
        return grad_x, small

    return loss_row[0, 0], early, g_w_sh, input_grad


def _reduce_scatter_start(grads, tag):
    c = lax.axis_index("c")
    core = jnp.reshape(c, (1,)).astype(jnp.int32)
    recv = _pair_reduce_send(grads, tag)
    parts = [_pair_add(g, r, core, "pair_add_" + tag) for g, r in zip(grads, recv)]
    return _chip_exchange_start(parts, tag)


def _reduce_scatter_finish(handle, after, tag):
    send_sems, recv_sems, parts, lands, _ = handle
    x, y, c = lax.axis_index("x"), lax.axis_index("y"), lax.axis_index("c")
    place = jnp.stack([2 * x + y, c]).astype(jnp.int32)
    parts, by_chip = _chip_exchange_wait(send_sems, recv_sems, parts, lands, after, tag)
    fulls = [_chip_sum(p, b, place, "chip_sum_" + tag) for p, b in zip(parts, by_chip)]
    return _pair_allgather(fulls, tag)


def kernel(x, mem, norm_g, mem_norm_g, w_in, conv_w, a_log, dt_bias, dn_norm_g, w_mem_kv, w_br_dn, w_br_sb, w_br_mem, w_out, final_g, loss_target, m_norm_g, m_mem_norm_g, m_w_in, m_conv_w, m_a_log, m_dt_bias, m_dn_norm_g, m_w_mem_kv, m_w_br_dn, m_w_br_sb, m_w_br_mem, m_w_out, m_final_g, v_norm_g, v_mem_norm_g, v_w_in, v_conv_w, v_a_log, v_dt_bias, v_dn_norm_g, v_w_mem_kv, v_w_br_dn, v_w_br_sb, v_w_br_mem, v_w_out, v_final_g):
    w_a = w_in[0]
    w_b = _pack_b(w_mem_kv[0], w_br_dn[0], w_br_sb[0], w_br_mem[0], w_out[0])
    m_b = _pack_b(m_w_mem_kv[0], m_w_br_dn[0], m_w_br_sb[0], m_w_br_mem[0], m_w_out[0])
    v_b = _pack_b(v_w_mem_kv[0], v_w_br_dn[0], v_w_br_sb[0], v_w_br_mem[0], v_w_out[0])

    shard_idx = 2 * lax.axis_index("x") + lax.axis_index("y")
    shard = jnp.reshape(shard_idx, (1,)).astype(jnp.int32)
    own = [_cast_to_window(w_a, shard, "cast_w_in"), _cast_bf16(_conv_slab(conv_w[0]), "cast_conv")]
    send1, recv1, own, lands, token = _halves_gather_start(own)
    h = _rmsnorm_fwd(x[0], norm_g, "norm_fwd", after=token)
    mem_n = _rmsnorm_fwd(mem[0], mem_norm_g, "mem_norm_fwd", after=token)
    w_b_bf = _cast_bf16(w_b, "cast_w_b")
    w_s = _pack_small(norm_g, mem_norm_g, final_g, dn_norm_g, a_log, dt_bias, _conv_shard_rows(conv_w[0], shard_idx))
    m_s = _pack_small(m_norm_g, m_mem_norm_g, m_final_g, m_dn_norm_g, m_a_log, m_dt_bias,
                      _conv_shard_rows(m_conv_w[0], shard_idx))
    v_s = _pack_small(v_norm_g, v_mem_norm_g, v_final_g, v_dn_norm_g, v_a_log, v_dt_bias,
                      _conv_shard_rows(v_conv_w[0], shard_idx))
    send2, recv2, lands = _halves_gather_forward(
        send1, recv1, own, lands, after=[h, mem_n, w_b_bf, m_w_in[0], v_w_in[0], m_b, v_b, w_s, m_s, v_s])
    lands = _halves_gather_wait(send2, recv2, lands)
    ga, g_conv = [lax.dynamic_update_slice(land, o[None], (shard_idx, 0, 0)) for land, o in zip(lands, own)]
    w_r = _windows_to_w_r(ga)
    f_conv = g_conv[:, :3].reshape(N_SHARD, 4, 768).transpose(1, 0, 2).reshape(4, 3 * D_MODEL).astype(F32)
    b_flight = _shard_gather_start(w_b_bf, after=ga)

    def proj_weights(after):
        own, land = _shard_gather_wait(b_flight[0], b_flight[1], b_flight[2], b_flight[3], after)
        gb = lax.dynamic_update_slice(land, own[None], (shard_idx, 0, 0))
        return (gb[:, B_MEMKV:B_BRDN].reshape(N_SHARD * 256, 512),
                gb[:, B_BRDN:B_BRSB].reshape(N_SHARD * 256, D_MODEL),
                gb[:, B_BRSB:B_BRMEM].reshape(N_SHARD * 256, D_MODEL),
                gb[:, B_BRMEM:B_OUT].reshape(N_SHARD, 256, 256).transpose(1, 0, 2).reshape(256, D_MODEL),
                gb[:, B_OUT:B_CONV].reshape(N_SHARD * 256, D_MODEL))

    flights = {}

    def on_early(grads):
        g_b = jnp.concatenate([
            grads["w_mem_kv"].reshape(N_SHARD, 128, D_MODEL), grads["w_br_dn"].reshape(N_SHARD, 256, D_MODEL),
            grads["w_br_sb"].reshape(N_SHARD, 256, D_MODEL),
            grads["w_br_mem"].reshape(256, N_SHARD, 256).transpose(1, 0, 2).reshape(N_SHARD, 64, D_MODEL),
            grads["w_out"].reshape(N_SHARD, 256, D_MODEL)], axis=1).astype(BF16)
        flights["b"] = _reduce_scatter_start([g_b], "b")
        return flights["b"][4]

    loss, _, g_w_sh, input_grad = _local_step(
        x[0], mem[0], loss_target[0], norm_g, mem_norm_g, w_r, ga, f_conv, a_log, dt_bias, dn_norm_g,
        proj_weights, final_g, on_early=on_early, after_gather=b_flight[4], h=h, mem_n=mem_n)
    flights["a"] = _reduce_scatter_start([g_w_sh], "a")
    grad_x, small = input_grad(after=flights["a"][4])

    part = _pack_small(small["norm_g"], small["mem_norm_g"], small["final_g"], small["dn_norm_g"],
                       small["a_log"], small["dt_bias"], _conv_rows(small["conv_w"]), loss)
    (gs_b,) = _reduce_scatter_finish(flights["b"], after=grad_x, tag="b")
    gr_b, d_b, nm_b, nv_b = _adamw(w_b, gs_b, m_b, v_b, "adamw_b")
    g_s, d_s, nm_s, nv_s = _small_update(_allgather_small(part, after=d_b), w_s, m_s, v_s)

    (gs_in,) = _reduce_scatter_finish(flights["a"], after=g_s, tag="a")
    gr_in, d_in, nm_in, nv_in = _adamw_window(w_a, gs_in, m_w_in[0], v_w_in[0], shard, "adamw_w_in")

    def assemble(slab_small, a_in, slab_b):
        s_norm, s_memnorm, s_final, s_dnnorm, s_alog, s_dtb, b_conv = _unpack_small(slab_small, shard_idx)
        b_memkv, b_brdn, b_brsb, b_brmem, b_out = _unpack_b(slab_b)
        return [s_norm, s_memnorm, a_in.reshape(1, D_MODEL, IN_WIDTH // N_SHARD), b_conv, s_alog, s_dtb, s_dnnorm,
                b_memkv, b_brdn, b_brsb, b_brmem, b_out, s_final]

    outs = [g_s[S_LOSS, 0], grad_x.reshape(1, -1, D_MODEL)]
    outs += assemble(g_s, gr_in, gr_b)
    outs += assemble(d_s, d_in, d_b)
    outs += assemble(nm_s, nm_in, nm_b)
    outs += assemble(nv_s, nv_in, nv_b)
    return tuple(outs)
```

```python
import math

import jax
import jax.numpy as jnp
from jax import lax
from jax.experimental import pallas as pl
from jax.experimental.pallas import tpu as pltpu

F32 = jnp.float32
BF16 = jnp.bfloat16
MESH = pl.DeviceIdType.MESH

D_MODEL = 1024
N_HEADS = 8
D_HEAD = 128
DN_CHUNK = 64
DN_GROUP = 32
DN_SCAN_GROUP = 4
SB_BLOCK = 256
SB_HEADS_PER_STEP = 2
SB_QBLOCK = 256
MEM_HEADS = 4
MEM_DH = 64
MEM_W = MEM_HEADS * MEM_DH
NORM_EPS = 1e-6
IN_WIDTH = 11792
N_SHARD = 4
SHARD_W = IN_WIDTH // N_SHARD
SHARD_PAD = 3072
N_DEV = 8

C_DNZ = 3072
C_SBQ = 4096
C_SBZ = 7168
C_MQ = 8192
C_MZ = 8448
C_GATES = 8704
C_BA = 11776
W_R = 12288

ADAM_LR = 0.001
ADAM_B1 = 0.9
ADAM_B2 = 0.999
ADAM_EPS = 1e-08
ADAM_WD = 0.01
ADAM_STEP = 10

VMEM_LIMIT = 56 * 1024 * 1024

B_MEMKV, B_BRDN, B_BRSB, B_BRMEM, B_OUT, B_CONV = 0, 128, 384, 640, 704, 960
S_NORM, S_MEMNORM, S_FINAL, S_DNNORM, S_ALOG, S_DTB, S_LOSS, S_CONV, S_ROWS = 0, 8, 16, 24, 25, 26, 27, 32, 128
CONV_BLOCKS = 3 * D_MODEL // 128


def _cp(**kw):
    return pltpu.CompilerParams(vmem_limit_bytes=VMEM_LIMIT, **kw)


def _dot(a, b, dims):
    lead = a.ndim - 2
    ca, cb = {"nn": (1, 0), "nt": (1, 1), "tn": (0, 0)}[dims]
    batch = tuple(range(lead))
    return lax.dot_general(a, b, (((ca + lead,), (cb + lead,)), (batch, batch)), preferred_element_type=F32)


def _chunks(x):
    return x.reshape(x.shape[0] // DN_CHUNK, DN_CHUNK, x.shape[1])


def _unchunk(x):
    return x.reshape(x.shape[0] * x.shape[1], x.shape[2])


def _bdot(a, b, dims):
    return _dot(a.astype(BF16), b.astype(BF16), dims)


def _split(a):
    hi = a.astype(BF16)
    return hi, (a - hi.astype(F32)).astype(BF16)


def _dot3(a, b, dims):
    a1, a2 = _split(a)
    b1, b2 = _split(b)
    return _dot(a1, b1, dims) + (_dot(a1, b2, dims) + _dot(a2, b1, dims))


def _ones_dot(a, ones_bf16):
    out = _dot(a.reshape(-1, a.shape[-1]).astype(BF16), ones_bf16, "nn")
    return out.reshape(a.shape[:-1] + (ones_bf16.shape[1],))


def _sigmoid(x):
    return 1.0 / (1.0 + jnp.exp(-x))


def _log1p_small(u):
    return jnp.where(u < 1e-2, u * (1.0 - u * (0.5 - u * (1.0 / 3.0))), jnp.log(1.0 + u))


def _pick(dim, cands):
    for c in cands:
        if dim % c == 0:
            return c
    return dim


def _mm(a, b, dims, name, out_dtype=F32, out_shards=1, after=None, tm_max=1024, tn_max=512):
    ta, tb = dims[0] == "t", dims[1] == "t"
    m, k = (a.shape[1], a.shape[0]) if ta else a.shape
    b_shards = b.shape[0] if b.ndim == 3 else 1
    n = b.shape[-2] if tb else b.shape[-1]
    tm = _pick(m, (tm_max, 1024, 512, 256))
    tn = _pick(n // out_shards, (tn_max, 512, 384, 256, 128))
    tk = _pick(k // b_shards, (2048, 1024, 512, 384, 256))
    nk = k // tk

    def body(a_ref, b_ref, *rest):
        if nk == 1:
            rest[-1][...] = _bdot(a_ref[...], b_ref[...], dims).astype(out_dtype)
            return
        o_ref, acc_ref = rest[-2:]
        kk = pl.program_id(2)

        @pl.when(kk == 0)
        def _():
            acc_ref[...] = jnp.zeros_like(acc_ref)

        acc_ref[...] += _bdot(a_ref[...], b_ref[...], dims)

        @pl.when(kk == nk - 1)
        def _():
            o_ref[...] = acc_ref[...].astype(out_dtype)

    a_spec = pl.BlockSpec((tk, tm), lambda i, j, q: (q, i)) if ta else pl.BlockSpec((tm, tk), lambda i, j, q: (i, q))
    if b_shards > 1:
        per_k = k // b_shards // tk
        b_spec = pl.BlockSpec((None, tn, tk), lambda i, j, q: (q // per_k, j, q % per_k))
    else:
        b_spec = pl.BlockSpec((tn, tk), lambda i, j, q: (j, q)) if tb else pl.BlockSpec((tk, tn), lambda i, j, q: (q, j))
    if out_shards > 1:
        per_n = n // out_shards // tn
        out_spec = pl.BlockSpec((None, tm, tn), lambda i, j, q: (j // per_n, i, j % per_n))
        out_shape = jax.ShapeDtypeStruct((out_shards, m, n // out_shards), out_dtype)
    else:
        out_spec = pl.BlockSpec((tm, tn), lambda i, j, q: (i, j))
        out_shape = jax.ShapeDtypeStruct((m, n), out_dtype)
    extra_specs, extra = [], []
    if after is not None:
        extra_specs, extra = [pl.BlockSpec(after.shape, lambda i, j, q: (0, 0))], [after]
    return pl.pallas_call(
        body, name=name, grid=(m // tm, n // tn, nk),
        in_specs=[a_spec, b_spec] + extra_specs, out_specs=out_spec, out_shape=out_shape,
        scratch_shapes=[pltpu.VMEM((tm, tn), F32)] if nk > 1 else [],
        compiler_params=_cp(dimension_semantics=("parallel", "parallel", "arbitrary")),
    )(a, b, *extra)


def _rmsnorm_fwd(x, g, name, after=None):
    t, d = x.shape
    tb = _pick(t, (512, 256))

    def body(x_ref, g_ref, *rest):
        xv = x_ref[...]
        r = lax.rsqrt(jnp.mean(xv * xv, axis=-1, keepdims=True) + NORM_EPS)
        rest[-1][...] = ((xv * r) * g_ref[...]).astype(BF16)

    extra_specs, extra = [], []
    if after is not None:
        extra_specs, extra = [pl.BlockSpec(after.shape, lambda i: (0, 0))], [after]
    return pl.pallas_call(
        body, name=name, grid=(t // tb,),
        in_specs=[pl.BlockSpec((tb, d), lambda i: (i, 0)), pl.BlockSpec((1, d), lambda i: (0, 0))] + extra_specs,
        out_specs=pl.BlockSpec((tb, d), lambda i: (i, 0)),
        out_shape=jax.ShapeDtypeStruct((t, d), BF16), compiler_params=_cp(),
    )(x, g, *extra)


def _rmsnorm_bwd(x, g, dh, resid, name):
    t, d = x.shape
    tb = _pick(t, (256,))

    def body(x_ref, g_ref, dh_ref, r_ref, dx_ref, dg_ref):
        @pl.when(pl.program_id(0) == 0)
        def _():
            dg_ref[...] = jnp.zeros_like(dg_ref)

        xv = x_ref[...]
        r = lax.rsqrt(jnp.mean(xv * xv, axis=-1, keepdims=True) + NORM_EPS)
        xhat = xv * r
        dhv = dh_ref[...]
        dg_ref[...] += jnp.sum(dhv * xhat, axis=0, keepdims=True)
        dxh = dhv * g_ref[...]
        dx_ref[...] = r_ref[...] + r * (dxh - xhat * jnp.mean(dxh * xhat, axis=-1, keepdims=True))

    row = pl.BlockSpec((tb, d), lambda i: (i, 0))
    vec = pl.BlockSpec((1, d), lambda i: (0, 0))
    return pl.pallas_call(
        body, name=name, grid=(t // tb,), in_specs=[row, vec, row, row], out_specs=[row, vec],
        out_shape=[jax.ShapeDtypeStruct((t, d), F32), jax.ShapeDtypeStruct((1, d), F32)], compiler_params=_cp(),
    )(x, g, dh, resid)


def _conv_silu(xv, w, row):
    y = xv * w[3:4, :]
    for s in (1, 2, 3):
        xs = jnp.where(row >= s, pltpu.roll(xv, s, 0), 0.0)
        y = y + xs * w[3 - s:4 - s, :]
    sig = _sigmoid(y)
    return y, sig, y * sig


def _dn_prep_fwd(proj, conv_w):
    t = proj.shape[0]

    def body(p_ref, w_ref, o_ref):
        j = pl.program_id(0)
        xv = p_ref[...]
        row = lax.broadcasted_iota(jnp.int32, xv.shape, 0)
        _, _, a = _conv_silu(xv, w_ref[...], row)
        inv = lax.rsqrt(jnp.sum(a * a, axis=-1, keepdims=True) + NORM_EPS)
        scale = jnp.where(j < N_HEADS, D_HEAD ** -0.5, 1.0)
        normed = jnp.where(j < 2 * N_HEADS, 1.0, 0.0)
        o_ref[...] = a * (normed * (inv * scale) + (1.0 - normed))

    return pl.pallas_call(
        body, name="dn_prep_fwd", grid=(3 * N_HEADS,),
        in_specs=[pl.BlockSpec((t, D_HEAD), lambda j: (0, j)), pl.BlockSpec((4, D_HEAD), lambda j: (0, j))],
        out_specs=pl.BlockSpec((t, D_HEAD), lambda j: (0, j)),
        out_shape=jax.ShapeDtypeStruct((t, 3 * D_MODEL), F32), compiler_params=_cp(),
    )(proj, conv_w)


def _dn_prep_bwd(proj, conv_w, dq, dk, dv):
    t = proj.shape[0]

    def body(p_ref, w_ref, dq_ref, dk_ref, dv_ref, dp_ref, dw_ref):
        j = pl.program_id(0)
        xv = p_ref[...]
        w = w_ref[...]
        row = lax.broadcasted_iota(jnp.int32, xv.shape, 0)
        y, s, a = _conv_silu(xv, w, row)
        part = jnp.zeros(xv.shape, jnp.int32) + j // N_HEADS
        dn = jnp.where(part == 0, dq_ref[...], jnp.where(part == 1, dk_ref[...], dv_ref[...]))
        inv = lax.rsqrt(jnp.sum(a * a, axis=-1, keepdims=True) + NORM_EPS)
        scale = jnp.where(j < N_HEADS, D_HEAD ** -0.5, 1.0)
        ds = dn * scale
        da_norm = inv * ds - a * (inv * inv * inv) * jnp.sum(ds * a, axis=-1, keepdims=True)
        normed = jnp.where(j < 2 * N_HEADS, 1.0, 0.0)
        da = normed * da_norm + (1.0 - normed) * dn
        dy = da * (s * (1.0 + y * (1.0 - s)))
        dx = dy * w[3:4, :]
        dw_ref[3:4, :] = jnp.sum(dy * xv, axis=0, keepdims=True)
        for sft in (1, 2, 3):
            xs = jnp.where(row >= sft, pltpu.roll(xv, sft, 0), 0.0)
            dw_ref[3 - sft:4 - sft, :] = jnp.sum(dy * xs, axis=0, keepdims=True)
            dys = jnp.where(row < t - sft, pltpu.roll(dy, t - sft, 0), 0.0)
            dx = dx + dys * w[3 - sft:4 - sft, :]
        dp_ref[...] = dx.astype(BF16)

    blk = pl.BlockSpec((t, D_HEAD), lambda j: (0, j))
    wblk = pl.BlockSpec((4, D_HEAD), lambda j: (0, j))

    def grad(part):
        return pl.BlockSpec((t, D_HEAD), lambda j: (0, jnp.clip(j - part * N_HEADS, 0, N_HEADS - 1)))

    return pl.pallas_call(
        body, name="dn_prep_bwd", grid=(3 * N_HEADS,), in_specs=[blk, wblk, grad(0), grad(1), grad(2)],
        out_specs=[blk, wblk],
        out_shape=[jax.ShapeDtypeStruct((t, 3 * D_MODEL), BF16), jax.ShapeDtypeStruct((4, 3 * D_MODEL), F32)],
        compiler_params=_cp(),
    )(proj, conv_w, dq, dk, dv)


def _softplus_parts(xv):
    e = jnp.exp(-jnp.abs(xv))
    return jnp.maximum(xv, 0.0) + _log1p_small(e)


def _chunk_scan(v, row, reverse):
    t = v.shape[0]
    pos = row & (DN_CHUNK - 1)
    s = 1
    while s < DN_CHUNK:
        if reverse:
            v = v + jnp.where(pos < DN_CHUNK - s, pltpu.roll(v, t - s, 0), 0.0)
        else:
            v = v + jnp.where(pos >= s, pltpu.roll(v, s, 0), 0.0)
        s *= 2
    return v


def _dn_gate_fwd(proj, alog_row, dtb_row):
    t = proj.shape[0]

    def body(p_ref, al_ref, dt_ref, b_ref, g_ref):
        p = p_ref[...]
        row = lax.broadcasted_iota(jnp.int32, p.shape, 0)
        b_ref[...] = _sigmoid(p)
        g = -jnp.exp(al_ref[...]) * _softplus_parts(p + dt_ref[...])
        g_ref[...] = _chunk_scan(g, row, reverse=False)

    blk = pl.BlockSpec((t, 128), lambda i: (0, C_BA // 128))
    vec = pl.BlockSpec((1, 128), lambda i: (0, 0))
    out = pl.BlockSpec((t, 128), lambda i: (0, 0))
    return pl.pallas_call(
        body, name="dn_gate_fwd", grid=(1,), in_specs=[blk, vec, vec], out_specs=[out, out],
        out_shape=[jax.ShapeDtypeStruct((t, 128), F32)] * 2, compiler_params=_cp(),
    )(proj, alog_row, dtb_row)


def _dn_gate_bwd(proj, alog_row, dtb_row, dbeta, dgc):
    t = proj.shape[0]

    def body(p_ref, al_ref, dt_ref, db_ref, dg_ref, dp_ref, dal_ref, ddt_ref):
        p = p_ref[...]
        row = lax.broadcasted_iota(jnp.int32, p.shape, 0)
        lane = lax.broadcasted_iota(jnp.int32, p.shape, 1)
        s = _sigmoid(p)
        d_b = db_ref[...] * s * (1.0 - s)
        dg = _chunk_scan(dg_ref[...], row, reverse=True)
        xa = p + dt_ref[...]
        ea = jnp.exp(al_ref[...])
        g = -ea * _softplus_parts(xa)
        d_a = dg * (-ea) * _sigmoid(xa)
        dp_ref[...] = jnp.where(lane < N_HEADS, d_b, jnp.where(lane < 2 * N_HEADS, d_a, 0.0)).astype(BF16)
        dal_ref[...] = jnp.sum(dg * g, axis=0, keepdims=True)
        ddt_ref[...] = jnp.sum(d_a, axis=0, keepdims=True)

    blk = pl.BlockSpec((t, 128), lambda i: (0, C_BA // 128))
    vec = pl.BlockSpec((1, 128), lambda i: (0, 0))
    full = pl.BlockSpec((t, 128), lambda i: (0, 0))
    return pl.pallas_call(
        body, name="dn_gate_bwd", grid=(1,), in_specs=[blk, vec, vec, full, full], out_specs=[full, vec, vec],
        out_shape=[jax.ShapeDtypeStruct((t, 128), BF16), jax.ShapeDtypeStruct((1, 128), F32),
                   jax.ShapeDtypeStruct((1, 128), F32)], compiler_params=_cp(),
    )(proj, alog_row, dtb_row, dbeta, dgc)


def _col_to_row(col, eye):
    return jnp.sum(jnp.where(eye, col, 0.0), axis=-2, keepdims=True)


def _row_to_col(rowv, eye):
    return jnp.sum(jnp.where(eye, rowv, 0.0), axis=-1, keepdims=True)


def _tri_inverse(m, ri, ci):
    eye = (ri == ci).astype(F32)
    b16 = (ri >> 4) == (ci >> 4)
    b32 = (ri >> 5) == (ci >> 5)
    m1 = jnp.where(b16, m, 0.0)
    x = eye - m1
    p = _dot3(m1, m1, "nn")
    x = x + _dot3(x, p, "nn")
    p = _dot3(p, p, "nn")
    x = x + _dot3(x, p, "nn")
    p = _dot3(p, p, "nn")
    x = x + _dot3(x, p, "nn")
    c1 = jnp.where(jnp.logical_and(b32, jnp.logical_not(b16)), m, 0.0)
    x = x - _dot3(_dot3(x, c1, "nn"), x, "nn")
    c2 = jnp.where(b32, 0.0, m)
    x = x - _dot3(_dot3(x, c2, "nn"), x, "nn")
    return x


def _dn_chunk_common(q, k, gc, ri, ci):
    eye = ri == ci
    g_row = _col_to_row(gc, eye)
    diff = jnp.minimum(gc - g_row, 0.0)
    gam = jnp.where(ri >= ci, jnp.exp(diff), 0.0)
    kk = _bdot(k, k, "nt")
    qk = _bdot(q, k, "nt")
    rcol = lax.broadcasted_iota(jnp.int32, gc.shape, gc.ndim - 2)
    last = jnp.sum(jnp.where(rcol == DN_CHUNK - 1, gc, 0.0), axis=-2, keepdims=True)
    e_g = jnp.exp(gc)
    dec = jnp.exp(last - gc)
    return eye, gam, kk, qk, last, e_g, dec, rcol


def _dn_specs(t, rows_blk):
    def head(off):
        return pl.BlockSpec((rows_blk, D_HEAD), lambda g, h: (g, off + h))

    lanes = pl.BlockSpec((rows_blk, 128), lambda g, h: (g, 0))
    hm = pl.BlockSpec((None, rows_blk, D_HEAD), lambda g, h: (h, g, 0))
    sq = pl.BlockSpec((1, rows_blk, DN_CHUNK), lambda g, h: (h, g, 0))
    tile = pl.BlockSpec((1, rows_blk // DN_CHUNK, 8, 128), lambda g, h: (h, g, 0, 0))
    return head, lanes, hm, sq, tile


def _head_column(slab, lane_idx):
    lane = lax.broadcasted_iota(jnp.int32, slab.shape, 1)
    return _chunks(jnp.sum(jnp.where(lane == lane_idx, slab, 0.0), axis=1, keepdims=True))


def _dn_intra_fwd(qkv, beta_t, g_t):
    t = qkv.shape[0]
    n_chunks = t // DN_CHUNK
    rows_blk = min(DN_GROUP * DN_CHUNK, t)

    def body(q_ref, k_ref, v_ref, b_ref, g_ref, u_ref, w_ref, qd_ref, kd_ref, a_ref, ti_ref, el_ref):
        ri = lax.broadcasted_iota(jnp.int32, (DN_CHUNK, DN_CHUNK), 0)
        ci = lax.broadcasted_iota(jnp.int32, (DN_CHUNK, DN_CHUNK), 1)
        h = pl.program_id(1)
        q, k, v = (_chunks(r[...]) for r in (q_ref, k_ref, v_ref))
        b, gc = _head_column(b_ref[...], h), _head_column(g_ref[...], h + N_HEADS)
        _, gam, kk, qk, last, e_g, dec, _ = _dn_chunk_common(q, k, gc, ri, ci)
        tinv = _tri_inverse(jnp.where(ri > ci, b * kk * gam, 0.0), ri, ci)
        u_ref[...] = _unchunk(_bdot(tinv, v * b, "nn"))
        w_ref[...] = _unchunk(_bdot(tinv, k * (b * e_g), "nn"))
        qd_ref[...] = _unchunk(q * e_g)
        kd_ref[...] = _unchunk(k * dec)
        a_ref[0] = _unchunk(qk * gam)
        ti_ref[0] = _unchunk(tinv)
        el_ref[0] = jnp.broadcast_to(jnp.exp(last), (rows_blk // DN_CHUNK, 8, 128))

    head, lanes, hm, sq, tile = _dn_specs(t, rows_blk)
    act = jax.ShapeDtypeStruct((N_HEADS, t, D_HEAD), F32)
    sqs = jax.ShapeDtypeStruct((N_HEADS, t, DN_CHUNK), F32)
    return pl.pallas_call(
        body, name="dn_intra_fwd", grid=(t // rows_blk, N_HEADS),
        in_specs=[head(0), head(N_HEADS), head(2 * N_HEADS), lanes, lanes],
        out_specs=[hm] * 4 + [sq, sq, tile],
        out_shape=[act] * 4 + [sqs, sqs, jax.ShapeDtypeStruct((N_HEADS, n_chunks, 8, 128), F32)],
        compiler_params=_cp(),
    )(qkv, qkv, qkv, beta_t, g_t)


def _dn_scan_specs(t, rows_blk, reverse):
    n_groups = t // rows_blk

    def at(g):
        return n_groups - 1 - g if reverse else g

    per = rows_blk // DN_CHUNK
    act = pl.BlockSpec((N_HEADS, rows_blk, D_HEAD), lambda g: (0, at(g), 0))
    sq = pl.BlockSpec((N_HEADS, rows_blk, DN_CHUNK), lambda g: (0, at(g), 0))
    state = pl.BlockSpec((N_HEADS, per, D_HEAD, D_HEAD), lambda g: (0, at(g), 0, 0))
    tile = pl.BlockSpec((N_HEADS, per, 8, 128), lambda g: (0, at(g), 0, 0))
    return act, sq, state, tile


def _dn_scan_fwd(u, w, qd, kd, a, el):
    t = u.shape[1]
    n_chunks = t // DN_CHUNK
    rows_blk = DN_SCAN_GROUP * DN_CHUNK

    def body(u_ref, w_ref, qd_ref, kd_ref, a_ref, el_ref, o_ref, vn_ref, s_ref, s_scr):
        @pl.when(pl.program_id(0) == 0)
        def _():
            s_scr[...] = jnp.zeros_like(s_scr)

        for cc in range(DN_SCAN_GROUP):
            rows = slice(cc * DN_CHUNK, (cc + 1) * DN_CHUNK)
            s = s_scr[...]
            s_ref[:, cc] = s
            v_new = u_ref[:, rows, :] - _bdot(w_ref[:, rows, :], s, "nn")
            vn_ref[:, rows, :] = v_new
            o_ref[:, rows, :] = _bdot(qd_ref[:, rows, :], s, "nn") + _bdot(a_ref[:, rows, :], v_new, "nn")
            s_scr[...] = s * el_ref[:, cc][:, 0:1, :] + _bdot(kd_ref[:, rows, :], v_new, "tn")

    act, sq, state, tile = _dn_scan_specs(t, rows_blk, reverse=False)
    shp = jax.ShapeDtypeStruct((N_HEADS, t, D_HEAD), F32)
    return pl.pallas_call(
        body, name="dn_scan_fwd", grid=(t // rows_blk,),
        in_specs=[act, act, act, act, sq, tile], out_specs=[act, act, state],
        out_shape=[shp, shp, jax.ShapeDtypeStruct((N_HEADS, n_chunks, D_HEAD, D_HEAD), F32)],
        scratch_shapes=[pltpu.VMEM((N_HEADS, D_HEAD, D_HEAD), F32)],
        compiler_params=_cp(dimension_semantics=("arbitrary",)),
    )(u, w, qd, kd, a, el)


def _dn_scan_bwd(w, qd, kd, a, el, vn, s_all, do):
    t = w.shape[1]
    n_chunks = t // DN_CHUNK
    rows_blk = DN_SCAN_GROUP * DN_CHUNK

    def body(w_ref, qd_ref, kd_ref, a_ref, el_ref, vn_ref, s_ref, do_ref, dvn_ref, dkd_ref, dqd_ref, dw_ref, dl_ref, ds_scr):
        @pl.when(pl.program_id(0) == 0)
        def _():
            ds_scr[...] = jnp.zeros_like(ds_scr)

        for cc in reversed(range(DN_SCAN_GROUP)):
            rows = slice(cc * DN_CHUNK, (cc + 1) * DN_CHUNK)
            s = s_ref[:, cc]
            d_s = ds_scr[...]
            e_last = el_ref[:, cc][:, 0:1, :]
            d_o = do_ref[:, rows, :]
            dv_new = _bdot(a_ref[:, rows, :], d_o, "tn") + _bdot(kd_ref[:, rows, :], d_s, "nn")
            ds_scr[...] = d_s * e_last + _bdot(qd_ref[:, rows, :], d_o, "tn") - _bdot(w_ref[:, rows, :], dv_new, "tn")
            dvn_ref[:, rows, :] = dv_new
            dkd_ref[:, rows, :] = _bdot(vn_ref[:, rows, :], d_s, "nt")
            dqd_ref[:, rows, :] = _bdot(d_o, s, "nt")
            dw_ref[:, rows, :] = -_bdot(dv_new, s, "nt")
            dlast = jnp.sum(jnp.sum(d_s * s, axis=2, keepdims=True), axis=1, keepdims=True)
            dl_ref[:, cc] = jnp.broadcast_to(dlast * e_last, (N_HEADS, 8, 128))

    act, sq, state, tile = _dn_scan_specs(t, rows_blk, reverse=True)
    shp = jax.ShapeDtypeStruct((N_HEADS, t, D_HEAD), F32)
    return pl.pallas_call(
        body, name="dn_scan_bwd", grid=(t // rows_blk,),
        in_specs=[act, act, act, sq, tile, act, state, act], out_specs=[act] * 4 + [tile],
        out_shape=[shp] * 4 + [jax.ShapeDtypeStruct((N_HEADS, n_chunks, 8, 128), F32)],
        scratch_shapes=[pltpu.VMEM((N_HEADS, D_HEAD, D_HEAD), F32)],
        compiler_params=_cp(dimension_semantics=("arbitrary",)),
    )(w, qd, kd, a, el, vn, s_all, do)


def _dn_intra_bwd(qkv, beta_t, g_t, tinv_all, vn, do, dvn, dkd, dqd, dw, dl):
    t = qkv.shape[0]
    rows_blk = min(DN_GROUP * DN_CHUNK, t)

    def body(q_ref, k_ref, v_ref, b_ref, g_ref, ti_ref, vn_ref, do_ref, dvn_ref, dkd_ref, dqd_ref, dw_ref, dl_ref,
             dq_ref, dk_ref, dv_ref, db_ref, dg_ref):
        ri = lax.broadcasted_iota(jnp.int32, (DN_CHUNK, DN_CHUNK), 0)
        ci = lax.broadcasted_iota(jnp.int32, (DN_CHUNK, DN_CHUNK), 1)
        h = pl.program_id(1)
        q, k, v = (_chunks(r[...]) for r in (q_ref, k_ref, v_ref))
        b, gc = _head_column(b_ref[...], h), _head_column(g_ref[...], h + N_HEADS)
        tinv = _chunks(ti_ref[0])
        dv_new, dk_dec, dq_dec, d_w = (_chunks(r[...]) for r in (dvn_ref, dkd_ref, dqd_ref, dw_ref))
        eye, gam, kk, qk, _, e_g, dec, rcol = _dn_chunk_common(q, k, gc, ri, ci)
        bv = v * b
        bk = k * (b * e_g)

        d_a = jnp.where(ri >= ci, _bdot(_chunks(do_ref[...]), _chunks(vn_ref[...]), "nt"), 0.0)
        dbv = _bdot(tinv, dv_new, "tn")
        dbk = _bdot(tinv, d_w, "tn")
        d_tinv = _bdot(dv_new, bv, "nt") + _bdot(d_w, bk, "nt")
        d_m = -jnp.where(ri > ci, _dot3(_dot3(tinv, d_tinv, "tn"), tinv, "nt"), 0.0)

        d_kk = d_m * b * gam
        d_gam = d_m * b * kk + d_a * qk
        d_qk = d_a * gam
        dq_ref[...] = _unchunk(_bdot(d_qk, k, "nn") + dq_dec * e_g)
        dk_ref[...] = _unchunk(_bdot(d_qk, q, "tn") + _bdot(d_kk, k, "nn") + _bdot(d_kk, k, "tn")
                               + dk_dec * dec + dbk * (b * e_g))
        dv_ref[...] = _unchunk(dbv * b)
        d_b = _unchunk(jnp.sum(d_m * kk * gam, axis=-1, keepdims=True) + jnp.sum(dbv * v, axis=-1, keepdims=True)
                       + jnp.sum(dbk * k, axis=-1, keepdims=True) * e_g)

        xg = d_gam * gam
        kdk = jnp.sum(dk_dec * (k * dec), axis=-1, keepdims=True)
        d_gc = (jnp.sum(xg, axis=-1, keepdims=True) - _row_to_col(jnp.sum(xg, axis=-2, keepdims=True), eye)
                + jnp.sum(dq_dec * (q * e_g), axis=-1, keepdims=True) - kdk
                + jnp.sum(dbk * bk, axis=-1, keepdims=True))
        d_last_total = dl_ref[0][:, 0:1, 0:1] + jnp.sum(kdk, axis=-2, keepdims=True)
        d_g = _unchunk(d_gc + jnp.where(rcol == DN_CHUNK - 1, d_last_total, 0.0))

        @pl.when(h == 0)
        def _():
            db_ref[...] = jnp.zeros_like(db_ref)
            dg_ref[...] = jnp.zeros_like(dg_ref)

        lane = lax.broadcasted_iota(jnp.int32, db_ref.shape, 1)
        db_ref[...] += jnp.where(lane == h, d_b, 0.0)
        dg_ref[...] += jnp.where(lane == h + N_HEADS, d_g, 0.0)

    head, lanes, hm, sq, tile = _dn_specs(t, rows_blk)
    return pl.pallas_call(
        body, name="dn_intra_bwd", grid=(t // rows_blk, N_HEADS),
        in_specs=[head(0), head(N_HEADS), head(2 * N_HEADS), lanes, lanes, sq] + [hm] * 6 + [tile],
        out_specs=[head(0), head(0), head(0), lanes, lanes],
        out_shape=[jax.ShapeDtypeStruct((t, D_MODEL), F32)] * 3 + [jax.ShapeDtypeStruct((t, 128), F32)] * 2,
        compiler_params=_cp(),
    )(qkv, qkv, qkv, beta_t, g_t, tinv_all, vn, do, dvn, dkd, dqd, dw, dl)


def _dn_post_fwd(o, proj, gn):
    t = o.shape[1]

    def body(o_ref, z_ref, g_ref, out_ref):
        ov, z = o_ref[...], z_ref[...]
        r = lax.rsqrt(jnp.mean(ov * ov, axis=-1, keepdims=True) + NORM_EPS)
        out_ref[...] = (((ov * r) * g_ref[...]) * (z * _sigmoid(z))).astype(BF16)

    blk = pl.BlockSpec((t, D_HEAD), lambda h: (0, h))
    return pl.pallas_call(
        body, name="dn_post_fwd", grid=(N_HEADS,),
        in_specs=[pl.BlockSpec((None, t, D_HEAD), lambda h: (h, 0, 0)),
                  pl.BlockSpec((t, D_HEAD), lambda h: (0, C_DNZ // D_HEAD + h)),
                  pl.BlockSpec((1, D_HEAD), lambda h: (0, 0))],
        out_specs=blk, out_shape=jax.ShapeDtypeStruct((t, D_MODEL), BF16), compiler_params=_cp(),
    )(o, proj, gn)


def _dn_post_bwd(o, proj, gn, dout):
    t = o.shape[1]

    def body(o_ref, z_ref, g_ref, d_ref, do_ref, dz_ref, dg_ref):
        @pl.when(pl.program_id(0) == 0)
        def _():
            dg_ref[...] = jnp.zeros_like(dg_ref)

        ov, z, d = o_ref[...], z_ref[...], d_ref[...]
        r = lax.rsqrt(jnp.mean(ov * ov, axis=-1, keepdims=True) + NORM_EPS)
        ohat = ov * r
        s = _sigmoid(z)
        d_on = d * (z * s)
        dz_ref[...] = (d * (ohat * g_ref[...]) * (s * (1.0 + z * (1.0 - s)))).astype(BF16)
        dg_ref[...] += jnp.sum(d_on * ohat, axis=0, keepdims=True)
        dxh = d_on * g_ref[...]
        do_ref[...] = r * (dxh - ohat * jnp.mean(dxh * ohat, axis=-1, keepdims=True))

    blk = pl.BlockSpec((t, D_HEAD), lambda h: (0, h))
    hm = pl.BlockSpec((None, t, D_HEAD), lambda h: (h, 0, 0))
    vec = pl.BlockSpec((1, D_HEAD), lambda h: (0, 0))
    return pl.pallas_call(
        body, name="dn_post_bwd", grid=(N_HEADS,),
        in_specs=[hm, pl.BlockSpec((t, D_HEAD), lambda h: (0, C_DNZ // D_HEAD + h)), vec, blk],
        out_specs=[hm, blk, vec],
        out_shape=[jax.ShapeDtypeStruct((N_HEADS, t, D_HEAD), F32), jax.ShapeDtypeStruct((t, D_MODEL), BF16),
                   jax.ShapeDtypeStruct((1, D_HEAD), F32)], compiler_params=_cp(),
    )(o, proj, gn, dout)


def _sb_fwd(proj):
    t = proj.shape[0]
    qblk = min(SB_QBLOCK, t)
    scale = 1.0 / math.sqrt(D_HEAD)

    hp = SB_HEADS_PER_STEP
    wid = hp * D_HEAD

    def body(q_ref, k_ref, v_ref, z_ref, o_ref, og_ref, l_ref, qb, kb, vb):
        for hh in range(hp):
            hs = slice(hh * D_HEAD, (hh + 1) * D_HEAD)
            qb[hh] = q_ref[:, hs].astype(BF16)
            kb[hh] = k_ref[:, hs].astype(BF16)
            vb[hh] = v_ref[:, hs].astype(BF16)
        ri = lax.broadcasted_iota(jnp.int32, (qblk, SB_BLOCK), 0)
        ci = lax.broadcasted_iota(jnp.int32, (qblk, SB_BLOCK), 1)
        r2 = lax.broadcasted_iota(jnp.int32, (SB_BLOCK, SB_BLOCK), 0)
        c2 = lax.broadcasted_iota(jnp.int32, (SB_BLOCK, SB_BLOCK), 1)
        upper = (r2 > c2).astype(BF16)
        nkb = qblk // SB_BLOCK

        def qblock(i, carry):
            rows = pl.ds(pl.multiple_of(i * qblk, qblk), qblk)
            qi = qb[:, rows, :]

            def tile(j, st, on_diagonal):
                acc, c = st
                cols = pl.ds(pl.multiple_of(j * SB_BLOCK, SB_BLOCK), SB_BLOCK)
                z = _dot(qi, kb[:, cols, :], "nt") * scale
                lb = jnp.minimum(z, 0.0) - jnp.log(1.0 + jnp.exp(-jnp.abs(z)))
                lf = lb - z
                if on_diagonal:
                    mask = (j * SB_BLOCK + ci) < (i * qblk + ri)
                    lf = jnp.where(mask, lf, 0.0)
                att = jnp.exp(lb + (_ones_dot(lf, upper) + c))
                if on_diagonal:
                    att = jnp.where(mask, att, 0.0)
                acc = acc + _dot(att.astype(BF16), vb[:, cols, :], "nn")
                return acc, c + jnp.sum(lf, axis=-1, keepdims=True)

            st = (jnp.zeros((hp, qblk, D_HEAD), F32), jnp.zeros((hp, qblk, 1), F32))
            for d in range(nkb):
                st = tile((i + 1) * nkb - 1 - d, st, True)
            acc, c = lax.fori_loop(0, i * nkb, lambda jj, s: tile(i * nkb - 1 - jj, s, False), st)
            l_ref[:, rows, :] = c
            for hh in range(hp):
                hs = slice(hh * D_HEAD, (hh + 1) * D_HEAD)
                zg = z_ref[rows, hs]
                o_ref[rows, hs] = acc[hh]
                og_ref[rows, hs] = (acc[hh] * (zg * _sigmoid(zg))).astype(BF16)
            return carry

        lax.fori_loop(0, t // qblk, qblock, 0)

    def head(off):
        return pl.BlockSpec((t, wid), lambda h: (0, off // wid + h))

    out = pl.BlockSpec((t, wid), lambda h: (0, h))
    return pl.pallas_call(
        body, name="sb_fwd", grid=(N_HEADS // hp,),
        in_specs=[head(C_SBQ), head(C_SBQ + D_MODEL), head(C_SBQ + 2 * D_MODEL), head(C_SBZ)],
        out_specs=[out, out, pl.BlockSpec((hp, t, 1), lambda h: (h, 0, 0))],
        out_shape=[jax.ShapeDtypeStruct((t, D_MODEL), F32), jax.ShapeDtypeStruct((t, D_MODEL), BF16),
                   jax.ShapeDtypeStruct((N_HEADS, t, 1), F32)],
        scratch_shapes=[pltpu.VMEM((hp, t, D_HEAD), BF16)] * 3, compiler_params=_cp(),
    )(proj, proj, proj, proj)


def _sb_bwd(proj, o, ltot, dog, after=None):
    t = proj.shape[0]
    qblk = min(SB_QBLOCK, t)
    scale = 1.0 / math.sqrt(D_HEAD)

    hp = SB_HEADS_PER_STEP
    wid = hp * D_HEAD

    def body(q_ref, k_ref, v_ref, z_ref, o_ref, l_ref, d_ref, *rest):
        dq_ref, dk_ref, dv_ref, dz_ref, qb, kb, vb, dob, dk_scr, dv_scr = rest[-10:]
        for hh in range(hp):
            hs = slice(hh * D_HEAD, (hh + 1) * D_HEAD)
            qb[hh] = q_ref[:, hs].astype(BF16)
            kb[hh] = k_ref[:, hs].astype(BF16)
            vb[hh] = v_ref[:, hs].astype(BF16)
            zg = z_ref[:, hs]
            sg = _sigmoid(zg)
            dgo = d_ref[:, hs]
            dob[hh] = (dgo * (zg * sg)).astype(BF16)
            dz_ref[:, hs] = (dgo * o_ref[:, hs] * (sg * (1.0 + zg * (1.0 - sg)))).astype(BF16)
        dk_scr[...] = jnp.zeros_like(dk_scr)
        dv_scr[...] = jnp.zeros_like(dv_scr)
        ri = lax.broadcasted_iota(jnp.int32, (qblk, SB_BLOCK), 0)
        ci = lax.broadcasted_iota(jnp.int32, (qblk, SB_BLOCK), 1)
        r2 = lax.broadcasted_iota(jnp.int32, (SB_BLOCK, SB_BLOCK), 0)
        c2 = lax.broadcasted_iota(jnp.int32, (SB_BLOCK, SB_BLOCK), 1)
        upper = (r2 > c2).astype(BF16)
        below = (r2 < c2).astype(BF16)

        def qblock(i, carry):
            rows = pl.ds(pl.multiple_of(i * qblk, qblk), qblk)
            qi = qb[:, rows, :]
            d_o = dob[:, rows, :]
            ltot = l_ref[:, rows, :]

            def tile(j, st, on_diagonal):
                dq, cpre, ce = st
                cols = pl.ds(pl.multiple_of(j * SB_BLOCK, SB_BLOCK), SB_BLOCK)
                kj, vj = kb[:, cols, :], vb[:, cols, :]
                z = _dot(qi, kj, "nt") * scale
                lb = jnp.minimum(z, 0.0) - jnp.log(1.0 + jnp.exp(-jnp.abs(z)))
                lf = lb - z
                if on_diagonal:
                    mask = (j * SB_BLOCK + ci) < (i * qblk + ri)
                    lf = jnp.where(mask, lf, 0.0)
                tile_sum = jnp.sum(lf, axis=-1, keepdims=True)
                att = jnp.exp(lb + ((ltot - cpre - tile_sum) + _ones_dot(lf, upper)))
                if on_diagonal:
                    att = jnp.where(mask, att, 0.0)
                e = _dot(d_o, vj, "nt") * att
                dlf = ce + _ones_dot(e, below)
                dzz = e - (e + dlf) * jnp.exp(lb)
                if on_diagonal:
                    dzz = jnp.where(mask, dzz, 0.0)
                dzz = dzz.astype(BF16)
                dq = dq + _dot(dzz, kj, "nn")
                dk_scr[:, cols, :] += _dot(dzz, qi, "tn")
                dv_scr[:, cols, :] += _dot(att.astype(BF16), d_o, "tn")
                return dq, cpre + tile_sum, ce + jnp.sum(e, axis=-1, keepdims=True)

            nkb = qblk // SB_BLOCK
            zero_col = jnp.zeros((hp, qblk, 1), F32)
            st = lax.fori_loop(0, i * nkb, lambda j, s: tile(j, s, False),
                               (jnp.zeros((hp, qblk, D_HEAD), F32), zero_col, zero_col))
            for d in range(nkb):
                st = tile(i * nkb + d, st, True)
            dq = st[0]
            for hh in range(hp):
                dq_ref[rows, hh * D_HEAD:(hh + 1) * D_HEAD] = (dq[hh] * scale).astype(BF16)
            return carry

        lax.fori_loop(0, t // qblk, qblock, 0)
        for hh in range(hp):
            hs = slice(hh * D_HEAD, (hh + 1) * D_HEAD)
            dk_ref[:, hs] = (dk_scr[hh] * scale).astype(BF16)
            dv_ref[:, hs] = dv_scr[hh].astype(BF16)

    def head(off):
        return pl.BlockSpec((t, wid), lambda h: (0, off // wid + h))

    extra_specs, extra = [], []
    if after is not None:
        extra_specs, extra = [pl.BlockSpec(after.shape, lambda h: (0, 0))], [after]
    return pl.pallas_call(
        body, name="sb_bwd", grid=(N_HEADS // hp,),
        in_specs=[head(C_SBQ), head(C_SBQ + D_MODEL), head(C_SBQ + 2 * D_MODEL), head(C_SBZ), head(0),
                  pl.BlockSpec((hp, t, 1), lambda h: (h, 0, 0)), head(0)] + extra_specs,
        out_specs=[head(0)] * 4, out_shape=[jax.ShapeDtypeStruct((t, D_MODEL), BF16)] * 4,
        scratch_shapes=[pltpu.VMEM((hp, t, D_HEAD), BF16)] * 4 + [pltpu.VMEM((hp, t, D_HEAD), F32)] * 2,
        compiler_params=_cp(),
    )(proj, proj, proj, proj, o, ltot, dog, *extra)


def _mem_fwd(proj, mkv):
    t = proj.shape[0]
    tq = _pick(t, (512, 256))
    m_len = mkv.shape[0]
    scale = 1.0 / math.sqrt(MEM_DH)

    def body(q_ref, z_ref, kv_ref, o_ref, og_ref):
        q = q_ref[...]
        mk = kv_ref[:, :MEM_W].astype(BF16)
        mv = kv_ref[:, MEM_W:].astype(BF16)
        lane = lax.broadcasted_iota(jnp.int32, q.shape, 1) >> 6
        o = jnp.zeros(q.shape, F32)
        for h in range(MEM_HEADS):
            s = _bdot(jnp.where(lane == h, q, 0.0), mk, "nt") * scale
            p = jnp.exp(s - jnp.max(s, axis=-1, keepdims=True))
            p = p / jnp.sum(p, axis=-1, keepdims=True)
            o = o + jnp.where(lane == h, _bdot(p, mv, "nn"), 0.0)
        z = z_ref[...]
        o_ref[...] = o
        og_ref[...] = (o * (z * _sigmoid(z))).astype(BF16)

    out = pl.BlockSpec((tq, MEM_W), lambda i: (i, 0))
    return pl.pallas_call(
        body, name="mem_fwd", grid=(t // tq,),
        in_specs=[pl.BlockSpec((tq, MEM_W), lambda i: (i, C_MQ // MEM_W)),
                  pl.BlockSpec((tq, MEM_W), lambda i: (i, C_MZ // MEM_W)),
                  pl.BlockSpec((m_len, 2 * MEM_W), lambda i: (0, 0))],
        out_specs=[out, out],
        out_shape=[jax.ShapeDtypeStruct((t, MEM_W), F32), jax.ShapeDtypeStruct((t, MEM_W), BF16)],
        compiler_params=_cp(),
    )(proj, proj, mkv)


def _mem_bwd(proj, mkv, o, dog):
    t = proj.shape[0]
    tq = _pick(t, (512, 256))
    m_len = mkv.shape[0]
    scale = 1.0 / math.sqrt(MEM_DH)

    def body(q_ref, z_ref, kv_ref, o_ref, d_ref, dq_ref, dz_ref, dkv_ref):
        @pl.when(pl.program_id(0) == 0)
        def _():
            dkv_ref[...] = jnp.zeros_like(dkv_ref)

        q = q_ref[...]
        z = z_ref[...]
        sg = _sigmoid(z)
        dgo = d_ref[...]
        d_o = dgo * (z * sg)
        dz_ref[...] = (dgo * o_ref[...] * (sg * (1.0 + z * (1.0 - sg)))).astype(BF16)
        mk = kv_ref[:, :MEM_W].astype(BF16)
        mv = kv_ref[:, MEM_W:].astype(BF16)
        lane = lax.broadcasted_iota(jnp.int32, q.shape, 1) >> 6
        klane = lax.broadcasted_iota(jnp.int32, (m_len, MEM_W), 1) >> 6
        dq = jnp.zeros(q.shape, F32)
        dmk = jnp.zeros((m_len, MEM_W), F32)
        dmv = jnp.zeros((m_len, MEM_W), F32)
        for h in range(MEM_HEADS):
            qh = jnp.where(lane == h, q, 0.0)
            doh = jnp.where(lane == h, d_o, 0.0)
            s = _bdot(qh, mk, "nt") * scale
            p = jnp.exp(s - jnp.max(s, axis=-1, keepdims=True))
            p = p / jnp.sum(p, axis=-1, keepdims=True)
            dp = _bdot(doh, mv, "nt")
            ds = p * (dp - jnp.sum(dp * p, axis=-1, keepdims=True)) * scale
            dq = dq + jnp.where(lane == h, _bdot(ds, mk, "nn"), 0.0)
            dmk = dmk + jnp.where(klane == h, _bdot(ds, qh, "tn"), 0.0)
            dmv = dmv + jnp.where(klane == h, _bdot(p, doh, "tn"), 0.0)
        dq_ref[...] = dq.astype(BF16)
        dkv_ref[:, :MEM_W] += dmk
        dkv_ref[:, MEM_W:] += dmv

    blk = pl.BlockSpec((tq, MEM_W), lambda i: (i, 0))
    kv = pl.BlockSpec((m_len, 2 * MEM_W), lambda i: (0, 0))
    return pl.pallas_call(
        body, name="mem_bwd", grid=(t // tq,),
        in_specs=[pl.BlockSpec((tq, MEM_W), lambda i: (i, C_MQ // MEM_W)),
                  pl.BlockSpec((tq, MEM_W), lambda i: (i, C_MZ // MEM_W)), kv, blk, blk],
        out_specs=[blk, blk, kv],
        out_shape=[jax.ShapeDtypeStruct((t, MEM_W), BF16), jax.ShapeDtypeStruct((t, MEM_W), BF16),
                   jax.ShapeDtypeStruct((m_len, 2 * MEM_W), F32)], compiler_params=_cp(),
    )(proj, proj, mkv, o, dog)


_GW = 512


def _merge_fwd(proj, y_dn, y_sb, y_m):
    t = proj.shape[0]
    tb = _pick(t, (256,))
    nc = D_MODEL // _GW

    def body(g1, g2, g3, y1, y2, y3, out_ref):
        out_ref[...] = (_sigmoid(g1[...]) * y1[...] + _sigmoid(g2[...]) * y2[...] + _sigmoid(g3[...]) * y3[...]).astype(BF16)

    def gate(kb):
        return pl.BlockSpec((tb, _GW), lambda i, c: (i, C_GATES // _GW + kb * nc + c))

    blk = pl.BlockSpec((tb, _GW), lambda i, c: (i, c))
    return pl.pallas_call(
        body, name="merge_fwd", grid=(t // tb, nc), in_specs=[gate(0), gate(1), gate(2), blk, blk, blk],
        out_specs=blk, out_shape=jax.ShapeDtypeStruct((t, D_MODEL), BF16), compiler_params=_cp(),
    )(proj, proj, proj, y_dn, y_sb, y_m)


def _merge_bwd(proj, y_dn, y_sb, y_m, dm):
    t = proj.shape[0]
    tb = _pick(t, (256,))
    nc = D_MODEL // _GW

    def body(g1, g2, g3, y1, y2, y3, dm_ref, d1, d2, d3, dg1, dg2, dg3):
        d = dm_ref[...]
        for g, y, dy, dg in ((g1, y1, d1, dg1), (g2, y2, d2, dg2), (g3, y3, d3, dg3)):
            s = _sigmoid(g[...])
            dy[...] = (d * s).astype(BF16)
            dg[...] = (d * y[...] * (s * (1.0 - s))).astype(BF16)

    def gate(kb):
        return pl.BlockSpec((tb, _GW), lambda i, c: (i, C_GATES // _GW + kb * nc + c))

    blk = pl.BlockSpec((tb, _GW), lambda i, c: (i, c))
    act = jax.ShapeDtypeStruct((t, D_MODEL), BF16)
    return pl.pallas_call(
        body, name="merge_bwd", grid=(t // tb, nc), in_specs=[gate(0), gate(1), gate(2), blk, blk, blk, blk],
        out_specs=[blk] * 6, out_shape=[act] * 6, compiler_params=_cp(),
    )(proj, proj, proj, y_dn, y_sb, y_m, dm)


def _final_loss(x, mo, g, tgt):
    t, d = x.shape
    tb = _pick(t, (256,))

    def body(x_ref, mo_ref, g_ref, t_ref, do_ref, dob_ref, loss_ref, dg_ref):
        @pl.when(pl.program_id(0) == 0)
        def _():
            loss_ref[...] = jnp.zeros_like(loss_ref)
            dg_ref[...] = jnp.zeros_like(dg_ref)

        out = x_ref[...] + mo_ref[...]
        r = lax.rsqrt(jnp.mean(out * out, axis=-1, keepdims=True) + NORM_EPS)
        xhat = out * r
        gv = g_ref[...]
        err = xhat * gv - t_ref[...]
        per_tok = jnp.mean(err * err, axis=-1, keepdims=True)
        loss_ref[...] += 0.5 * jnp.sum(per_tok, axis=0, keepdims=True)
        dy = err * (1.0 / d)
        dg_ref[...] += jnp.sum(dy * xhat, axis=0, keepdims=True)
        dxh = dy * gv
        dout = r * (dxh - xhat * jnp.mean(dxh * xhat, axis=-1, keepdims=True))
        do_ref[...] = dout
        dob_ref[...] = dout.astype(BF16)

    row = pl.BlockSpec((tb, d), lambda i: (i, 0))
    vec = pl.BlockSpec((1, d), lambda i: (0, 0))
    return pl.pallas_call(
        body, name="final_loss", grid=(t // tb,), in_specs=[row, row, vec, row],
        out_specs=[row, row, pl.BlockSpec((1, 128), lambda i: (0, 0)), vec],
        out_shape=[jax.ShapeDtypeStruct((t, d), F32), jax.ShapeDtypeStruct((t, d), BF16),
                   jax.ShapeDtypeStruct((1, 128), F32), jax.ShapeDtypeStruct((1, d), F32)],
        compiler_params=_cp(),
    )(x, mo, g, tgt)


def _cast_bf16(a, name):
    r, c = a.shape
    tb = _pick(r, (128, 496, 240))

    def body(a_ref, o_ref):
        o_ref[...] = a_ref[...].astype(BF16)

    blk = pl.BlockSpec((tb, c), lambda i: (i, 0))
    return pl.pallas_call(body, name=name, grid=(r // tb,), in_specs=[blk], out_specs=blk,
                          out_shape=jax.ShapeDtypeStruct((r, c), BF16), compiler_params=_cp())(a)


WIN_START = (0, 23, 45, 68)
_S1_LO, _S1_HI = 1148, 1164
_S1_BA_POS = SHARD_PAD - 128


def _to_window(x, s):
    if s == 0:
        return x
    if s in (2, 3):
        return pltpu.roll(x, 120 if s == 2 else 124, 1)
    pos = lax.broadcasted_iota(jnp.int32, x.shape, 1)
    head = pltpu.roll(x, 4, 1)
    tail = pltpu.roll(x, SHARD_PAD - 12, 1)
    ba = jnp.where(pos < _S1_BA_POS + (_S1_HI - _S1_LO), pltpu.roll(x, _S1_BA_POS - _S1_LO, 1), 0.0)
    return jnp.where(pos < _S1_LO + 4, head, jnp.where(pos < _S1_BA_POS, tail, ba))


def _from_window(g, s):
    if s == 0:
        return g
    if s in (2, 3):
        return pltpu.roll(g, SHARD_PAD - (120 if s == 2 else 124), 1)
    col = lax.broadcasted_iota(jnp.int32, g.shape, 1)
    head = pltpu.roll(g, SHARD_PAD - 4, 1)
    tail = pltpu.roll(g, 12, 1)
    ba = pltpu.roll(g, SHARD_PAD - (_S1_BA_POS - _S1_LO), 1)
    return jnp.where(col < _S1_LO, head, jnp.where(col < _S1_HI, ba, tail))


def _cast_to_window(w, shard, name):
    r, c = w.shape
    tb = _pick(r, (128,))

    def body(s_ref, w_ref, o_ref, pad_scr):
        pad_scr[...] = jnp.zeros_like(pad_scr)
        pad_scr[:, :c] = w_ref[...]
        x = pad_scr[...]
        for s in range(N_SHARD):
            @pl.when(s_ref[0] == s)
            def _():
                o_ref[...] = _to_window(x, s).astype(BF16)

    return pl.pallas_call(
        body, name=name,
        grid_spec=pltpu.PrefetchScalarGridSpec(
            num_scalar_prefetch=1, grid=(r // tb,),
            in_specs=[pl.BlockSpec((tb, c), lambda i, s: (i, 0))],
            out_specs=pl.BlockSpec((tb, SHARD_PAD), lambda i, s: (i, 0)),
            scratch_shapes=[pltpu.VMEM((tb, SHARD_PAD), F32)]),
        out_shape=jax.ShapeDtypeStruct((r, SHARD_PAD), BF16), compiler_params=_cp(),
    )(shard, w)


def _pair_add(g, recv, c_idx, name):
    n, r, c = g.shape
    half = r // 2
    tb = _pick(half, (128, 240))
    nb = half // tb

    def body(c_ref, g_ref, r_ref, o_ref):
        o_ref[...] = (g_ref[...].astype(F32) + r_ref[...].astype(F32)).astype(BF16)

    blk = pl.BlockSpec((n, tb, c), lambda i, c_ref: (0, i, 0))
    return pl.pallas_call(
        body, name=name,
        grid_spec=pltpu.PrefetchScalarGridSpec(
            num_scalar_prefetch=1, grid=(nb,),
            in_specs=[pl.BlockSpec((n, tb, c), lambda i, c_ref: (0, c_ref[0] * nb + i, 0)), blk], out_specs=blk),
        out_shape=jax.ShapeDtypeStruct((n, half, c), BF16), compiler_params=_cp(),
    )(c_idx, g, recv)


def _chip_sum(parts, by_chip, place, name):
    n, h, c = parts.shape
    tb = _pick(h, (128, 240))
    nb = h // tb

    def body(p_ref, mine_ref, *rest):
        others, o_ref = rest[:n], rest[n]
        me = jnp.zeros((tb, c), jnp.int32) + p_ref[0]
        acc = None
        for q in range(n):
            term = jnp.where(me == q, mine_ref[...], others[q][...]).astype(F32)
            acc = term if acc is None else acc + term
        o_ref[...] = acc

    def other(q):
        return pl.BlockSpec((None, tb, c), lambda i, p: (jnp.where(p[0] == q, (q + 1) % n, q), i, 0))

    return pl.pallas_call(
        body, name=name,
        grid_spec=pltpu.PrefetchScalarGridSpec(
            num_scalar_prefetch=1, grid=(nb,),
            in_specs=[pl.BlockSpec((None, tb, c), lambda i, p: (p[0], i, 0))] + [other(q) for q in range(n)],
            out_specs=pl.BlockSpec((tb, c), lambda i, p: (p[1] * nb + i, 0))),
        out_shape=jax.ShapeDtypeStruct((2 * h, c), F32), compiler_params=_cp(),
    )(place, parts, *([by_chip] * n))


def _adamw_math(w, g, m, v):
    m = ADAM_B1 * m + (1.0 - ADAM_B1) * g
    v = ADAM_B2 * v + (1.0 - ADAM_B2) * (g * g)
    m_hat = m / (1.0 - ADAM_B1 ** ADAM_STEP)
    v_hat = v / (1.0 - ADAM_B2 ** ADAM_STEP)
    delta = -ADAM_LR * (m_hat / (jnp.sqrt(v_hat) + ADAM_EPS) + ADAM_WD * w)
    return delta, m, v


def _adamw(w, g, m, v, name):
    r, c = w.shape
    tb = _pick(r, (128, 496, 240))

    def body(w_ref, g_ref, m_ref, v_ref, go_ref, d_ref, mo_ref, vo_ref):
        gv = g_ref[...]
        d, mn, vn = _adamw_math(w_ref[...], gv, m_ref[...], v_ref[...])
        go_ref[...] = gv
        d_ref[...] = d
        mo_ref[...] = mn
        vo_ref[...] = vn

    blk = pl.BlockSpec((tb, c), lambda i: (i, 0))
    return pl.pallas_call(
        body, name=name, grid=(r // tb,), in_specs=[blk] * 4, out_specs=[blk] * 4,
        out_shape=[jax.ShapeDtypeStruct((r, c), F32)] * 4, compiler_params=_cp(),
    )(w, g, m, v)


def _adamw_window(w, g_win, m, v, shard, name):
    r, c = w.shape
    tb = _pick(r, (128,))

    def body(s_ref, w_ref, g_ref, m_ref, v_ref, go_ref, d_ref, mo_ref, vo_ref, g_scr):
        gw = g_ref[...]
        for s in range(N_SHARD):
            @pl.when(s_ref[0] == s)
            def _():
                g_scr[...] = _from_window(gw, s)

        gv = g_scr[:, :c]
        d, mn, vn = _adamw_math(w_ref[...], gv, m_ref[...], v_ref[...])
        go_ref[...] = gv
        d_ref[...] = d
        mo_ref[...] = mn
        vo_ref[...] = vn

    blk = pl.BlockSpec((tb, c), lambda i, s: (i, 0))
    return pl.pallas_call(
        body, name=name,
        grid_spec=pltpu.PrefetchScalarGridSpec(
            num_scalar_prefetch=1, grid=(r // tb,),
            in_specs=[blk, pl.BlockSpec((tb, SHARD_PAD), lambda i, s: (i, 0)), blk, blk], out_specs=[blk] * 4,
            scratch_shapes=[pltpu.VMEM((tb, SHARD_PAD), F32)]),
        out_shape=[jax.ShapeDtypeStruct((r, c), F32)] * 4, compiler_params=_cp(),
    )(shard, w, g_win, m, v)


def _small_update(gathered, w, m, v):
    def body(p_ref, w_ref, m_ref, v_ref, g_ref, d_ref, mo_ref, vo_ref):
        g = p_ref[0]
        for i in range(1, N_DEV):
            g = g + p_ref[i]
        d, mn, vn = _adamw_math(w_ref[...], g, m_ref[...], v_ref[...])
        g_ref[...] = g
        d_ref[...] = d
        mo_ref[...] = mn
        vo_ref[...] = vn

    full = pl.BlockSpec((S_ROWS, 128), lambda i: (0, 0))
    return pl.pallas_call(
        body, name="small_update", grid=(1,),
        in_specs=[pl.BlockSpec((N_DEV, S_ROWS, 128), lambda i: (0, 0, 0)), full, full, full], out_specs=[full] * 4,
        out_shape=[jax.ShapeDtypeStruct((S_ROWS, 128), F32)] * 4, compiler_params=_cp(),
    )(gathered, w, m, v)


_ANY = pl.BlockSpec(memory_space=pl.ANY)


def _place():
    x, y, c = lax.axis_index("x"), lax.axis_index("y"), lax.axis_index("c")
    chips = [(1 - x, y), (x, 1 - y), (1 - x, 1 - y)]
    return x, y, c, chips


def _pair_reduce_send(grads, tag):
    n = len(grads)

    def body(*refs):
        ins, outs = refs[:n], refs[n:2 * n]
        send_sems, recv_sems = refs[2 * n:]
        x, y, c, _ = _place()
        sibling = (x, y, 1 - c)
        cps = []
        for a in range(n):
            half = ins[a].shape[1] // 2
            theirs = pl.ds(pl.multiple_of((1 - c) * half, 8), half)
            cp = pltpu.make_async_remote_copy(
                src_ref=ins[a].at[:, theirs], dst_ref=outs[a], send_sem=send_sems.at[a], recv_sem=recv_sems.at[a],
                device_id=sibling, device_id_type=MESH)
            cp.start()
            cps.append(cp)
        for cp in cps:
            cp.wait()

    return pl.pallas_call(
        body, name="pair_reduce_send_" + tag, in_specs=[_ANY] * n, out_specs=[_ANY] * n,
        out_shape=[jax.ShapeDtypeStruct((g.shape[0], g.shape[1] // 2, g.shape[2]), g.dtype) for g in grads],
        scratch_shapes=[pltpu.SemaphoreType.DMA((n,)), pltpu.SemaphoreType.DMA((n,))],
        compiler_params=pltpu.CompilerParams(has_side_effects=True),
    )(*grads)


_HBM = pl.BlockSpec(memory_space=pltpu.HBM)
_SEM = pl.BlockSpec(memory_space=pltpu.SEMAPHORE)
_DATAFLOW = pltpu.SideEffectType.DATAFLOW_SIDE_EFFECTING


def _chip_exchange_copies(ins, lands, send_sems, recv_sems):
    x, y, c, chips = _place()
    me = 2 * x + y
    cps = []
    for a in range(len(ins)):
        for j, (qx, qy) in enumerate(chips):
            cps.append(pltpu.make_async_remote_copy(
                src_ref=ins[a].at[2 * qx + qy], dst_ref=lands[a].at[me], send_sem=send_sems.at[3 * a + j],
                recv_sem=recv_sems.at[3 * a + j], device_id=(qx, qy, c), device_id_type=MESH))
    return cps


def _chip_exchange_start(parts, tag):
    n = len(parts)

    def body(*refs):
        ins, lands = refs[:n], refs[n:2 * n]
        send_sems, recv_sems = refs[2 * n:2 * n + 2]
        token = refs[4 * n + 2]
        for cp in _chip_exchange_copies(ins, lands, send_sems, recv_sems):
            cp.start()
        token[...] = jnp.zeros_like(token)

    hbm = [pltpu.HBM(p.shape, p.dtype) for p in parts]
    lands = [pltpu.with_memory_space_constraint(lax.empty(p.shape, p.dtype), pltpu.HBM) for p in parts]
    res = pl.pallas_call(
        body, name="chip_exchange_start_" + tag,
        out_shape=(pltpu.SemaphoreType.DMA((3 * n,)), pltpu.SemaphoreType.DMA((3 * n,)), *hbm, *hbm,
                   jax.ShapeDtypeStruct((8, 128), F32)),
        in_specs=[_HBM] * (2 * n), out_specs=(_SEM, _SEM, *([_HBM] * (2 * n)), pl.BlockSpec(memory_space=pltpu.VMEM)),
        input_output_aliases={a: 2 + a for a in range(2 * n)},
        compiler_params=pltpu.CompilerParams(has_side_effects=_DATAFLOW),
    )(*[pltpu.with_memory_space_constraint(p, pltpu.HBM) for p in parts], *lands)
    return res[0], res[1], res[2:2 + n], res[2 + n:2 + 2 * n], res[2 + 2 * n]


def _chip_exchange_wait(send_sems, recv_sems, parts, lands, after, tag):
    n = len(parts)

    def body(*refs):
        ins, land_refs = refs[:n], refs[n:2 * n]
        s_sems, r_sems = refs[2 * n:2 * n + 2]
        for cp in _chip_exchange_copies(ins, land_refs, s_sems, r_sems):
            cp.wait_send()
            cp.wait_recv()

    hbm = [pltpu.HBM(p.shape, p.dtype) for p in parts]
    res = pl.pallas_call(
        body, name="chip_exchange_wait_" + tag, out_shape=(*hbm, *hbm),
        in_specs=[_HBM] * (2 * n) + [_SEM, _SEM, _ANY], out_specs=tuple([_HBM] * (2 * n)),
        input_output_aliases={a: a for a in range(2 * n)},
        compiler_params=pltpu.CompilerParams(has_side_effects=_DATAFLOW),
    )(*parts, *lands, send_sems, recv_sems, after)
    return res[:n], res[n:]


def _halves_ici_copies(srcs, lands, send_sems, recv_sems):
    x, y, c, chips = _place()
    me = 2 * x + y
    cps = []
    for a, src in enumerate(srcs):
        half = src.shape[0] // 2
        mine = pl.ds(pl.multiple_of(c * half, 16), half)
        for j, (qx, qy) in enumerate(chips):
            cps.append(pltpu.make_async_remote_copy(
                src_ref=src.at[mine], dst_ref=lands[a].at[me, mine], send_sem=send_sems.at[3 * a + j],
                recv_sem=recv_sems.at[3 * a + j], device_id=(qx, qy, c), device_id_type=MESH))
    return cps


def _halves_d2d_copies(lands, send_sems, recv_sems):
    x, y, c, chips = _place()
    cps = []
    for a, land in enumerate(lands):
        half = land.shape[1] // 2
        mine = pl.ds(pl.multiple_of(c * half, 16), half)
        for j, (qx, qy) in enumerate(chips):
            region = land.at[2 * qx + qy, mine]
            cps.append(pltpu.make_async_remote_copy(
                src_ref=region, dst_ref=region, send_sem=send_sems.at[3 * a + j], recv_sem=recv_sems.at[3 * a + j],
                device_id=(x, y, 1 - c), device_id_type=MESH))
    return cps


def _halves_gather_start(shards):
    n = len(shards)

    def body(*refs):
        srcs, lands = refs[:n], refs[n:2 * n]
        send_sems, recv_sems = refs[2 * n:2 * n + 2]
        for cp in _halves_ici_copies(srcs, lands, send_sems, recv_sems):
            cp.start()
        refs[4 * n + 2][...] = jnp.zeros((8, 128), F32)

    hbm_s = [pltpu.HBM(s.shape, s.dtype) for s in shards]
    hbm_l = [pltpu.HBM((N_SHARD,) + s.shape, s.dtype) for s in shards]
    lands = [pltpu.with_memory_space_constraint(lax.empty((N_SHARD,) + s.shape, s.dtype), pltpu.HBM) for s in shards]
    res = pl.pallas_call(
        body, name="halves_gather_start",
        out_shape=(pltpu.SemaphoreType.DMA((3 * n,)), pltpu.SemaphoreType.DMA((3 * n,)), *hbm_s, *hbm_l,
                   jax.ShapeDtypeStruct((8, 128), F32)),
        in_specs=[_HBM] * (2 * n), out_specs=(_SEM, _SEM, *([_HBM] * (2 * n)), pl.BlockSpec(memory_space=pltpu.VMEM)),
        input_output_aliases={a: 2 + a for a in range(2 * n)},
        compiler_params=pltpu.CompilerParams(has_side_effects=_DATAFLOW),
    )(*[pltpu.with_memory_space_constraint(s, pltpu.HBM) for s in shards], *lands)
    return res[0], res[1], res[2:2 + n], res[2 + n:2 + 2 * n], res[2 + 2 * n]


def _halves_gather_forward(send1, recv1, shards, lands, after):
    n = len(shards)

    def body(*refs):
        srcs, land_refs = refs[:n], refs[n:2 * n]
        s1, r1 = refs[2 * n:2 * n + 2]
        outs = refs[2 * n + 2 + len(after):]
        s2, r2 = outs[0], outs[1]
        for cp in _halves_ici_copies(srcs, land_refs, s1, r1):
            cp.wait_send()
            cp.wait_recv()
        for cp in _halves_d2d_copies(land_refs, s2, r2):
            cp.start()

    hbm_s = [pltpu.HBM(s.shape, s.dtype) for s in shards]
    hbm_l = [pltpu.HBM(l.shape, l.dtype) for l in lands]
    res = pl.pallas_call(
        body, name="halves_gather_forward",
        out_shape=(pltpu.SemaphoreType.DMA((3 * n,)), pltpu.SemaphoreType.DMA((3 * n,)), *hbm_s, *hbm_l),
        in_specs=[_HBM] * (2 * n) + [_SEM, _SEM] + [_ANY] * len(after),
        out_specs=(_SEM, _SEM, *([_HBM] * (2 * n))), input_output_aliases={a: 2 + a for a in range(2 * n)},
        compiler_params=pltpu.CompilerParams(has_side_effects=_DATAFLOW),
    )(*shards, *lands, send1, recv1, *after)
    return res[0], res[1], res[2 + n:2 + 2 * n]


def _halves_gather_wait(send2, recv2, lands):
    n = len(lands)

    def body(*refs):
        land_refs = refs[:n]
        s2, r2 = refs[n:n + 2]
        for cp in _halves_d2d_copies(land_refs, s2, r2):
            cp.wait_send()
            cp.wait_recv()

    hbm_l = [pltpu.HBM(l.shape, l.dtype) for l in lands]
    res = pl.pallas_call(
        body, name="halves_gather_wait", out_shape=tuple(hbm_l),
        in_specs=[_HBM] * n + [_SEM, _SEM], out_specs=tuple([_HBM] * n),
        input_output_aliases={a: a for a in range(n)},
        compiler_params=pltpu.CompilerParams(has_side_effects=_DATAFLOW),
    )(*lands, send2, recv2)
    return list(res)


def _shard_gather_copies(src, land, send_sems, recv_sems):
    x, y, c, chips = _place()
    me = 2 * x + y
    return [pltpu.make_async_remote_copy(
        src_ref=src, dst_ref=land.at[me], send_sem=send_sems.at[j], recv_sem=recv_sems.at[j],
        device_id=(qx, qy, c), device_id_type=MESH) for j, (qx, qy) in enumerate(chips)]


def _shard_gather_start(shard_arr, after):
    def body(src, land, after_ref, send_sems, recv_sems, src_thru, land_thru, token):
        for cp in _shard_gather_copies(src, land, send_sems, recv_sems):
            cp.start()
        token[...] = jnp.zeros_like(token)

    land_shape = (N_SHARD,) + shard_arr.shape
    land = pltpu.with_memory_space_constraint(lax.empty(land_shape, shard_arr.dtype), pltpu.HBM)
    return pl.pallas_call(
        body, name="shard_gather_start",
        out_shape=(pltpu.SemaphoreType.DMA((N_SHARD - 1,)), pltpu.SemaphoreType.DMA((N_SHARD - 1,)),
                   pltpu.HBM(shard_arr.shape, shard_arr.dtype), pltpu.HBM(land_shape, shard_arr.dtype),
                   jax.ShapeDtypeStruct((8, 128), F32)),
        in_specs=[_HBM, _HBM, _ANY], out_specs=(_SEM, _SEM, _HBM, _HBM, pl.BlockSpec(memory_space=pltpu.VMEM)),
        input_output_aliases={0: 2, 1: 3},
        compiler_params=pltpu.CompilerParams(has_side_effects=_DATAFLOW),
    )(pltpu.with_memory_space_constraint(shard_arr, pltpu.HBM), land, after)


def _shard_gather_wait(send_sems, recv_sems, shard_arr, land, after):
    def body(src, land_ref, s_sems, r_sems, after_ref, src_out, land_out):
        for cp in _shard_gather_copies(src, land_ref, s_sems, r_sems):
            cp.wait_send()
            cp.wait_recv()

    return pl.pallas_call(
        body, name="shard_gather_wait",
        out_shape=(pltpu.HBM(shard_arr.shape, shard_arr.dtype), pltpu.HBM(land.shape, land.dtype)),
        in_specs=[_HBM, _HBM, _SEM, _SEM, _ANY], out_specs=(_HBM, _HBM), input_output_aliases={0: 0, 1: 1},
        compiler_params=pltpu.CompilerParams(has_side_effects=_DATAFLOW),
    )(shard_arr, land, send_sems, recv_sems, after)


def _pair_allgather(fulls, tag):
    n = len(fulls)

    def body(*refs):
        outs = refs[n:2 * n]
        send_sems, recv_sems = refs[2 * n:]
        x, y, c, _ = _place()
        sibling = (x, y, 1 - c)
        cps = []
        for a in range(n):
            half = outs[a].shape[0] // 2
            mine = outs[a].at[pl.ds(pl.multiple_of(c * half, 8), half)]
            cp = pltpu.make_async_remote_copy(
                src_ref=mine, dst_ref=mine, send_sem=send_sems.at[a], recv_sem=recv_sems.at[a],
                device_id=sibling, device_id_type=MESH)
            cp.start()
            cps.append(cp)
        for a in range(n):
            half = outs[a].shape[0] // 2
            theirs = outs[a].at[pl.ds(pl.multiple_of((1 - c) * half, 8), half)]
            pltpu.make_async_remote_copy(
                src_ref=theirs, dst_ref=theirs, send_sem=send_sems.at[a], recv_sem=recv_sems.at[a],
                device_id=sibling, device_id_type=MESH).wait_recv()
        for cp in cps:
            cp.wait_send()

    return pl.pallas_call(
        body, name="pair_allgather_" + tag, in_specs=[_ANY] * n, out_specs=[_ANY] * n,
        out_shape=[jax.ShapeDtypeStruct(f.shape, f.dtype) for f in fulls],
        input_output_aliases={a: a for a in range(n)},
        scratch_shapes=[pltpu.SemaphoreType.DMA((n,)), pltpu.SemaphoreType.DMA((n,))],
        compiler_params=pltpu.CompilerParams(has_side_effects=True),
    )(*fulls)


def _allgather_small(slab, after):
    def body(s_ref, after_ref, out_ref, send_sems, recv_sems):
        x, y, c, _ = _place()
        me = 4 * x + 2 * y + c
        out_ref[me] = s_ref[...]
        cps = []
        for mask in range(1, N_DEV):
            peer = (x ^ (mask >> 2), y ^ ((mask >> 1) & 1), c ^ (mask & 1))
            cp = pltpu.make_async_remote_copy(
                src_ref=s_ref, dst_ref=out_ref.at[me], send_sem=send_sems.at[mask - 1], recv_sem=recv_sems.at[mask - 1],
                device_id=peer, device_id_type=MESH)
            cp.start()
            cps.append(cp)
        for mask in range(1, N_DEV):
            peer = (x ^ (mask >> 2), y ^ ((mask >> 1) & 1), c ^ (mask & 1))
            dst = out_ref.at[4 * peer[0] + 2 * peer[1] + peer[2]]
            pltpu.make_async_remote_copy(
                src_ref=dst, dst_ref=dst, send_sem=send_sems.at[mask - 1], recv_sem=recv_sems.at[mask - 1],
                device_id=peer, device_id_type=MESH).wait_recv()
        for cp in cps:
            cp.wait_send()

    vm = pl.BlockSpec(memory_space=pltpu.VMEM)
    return pl.pallas_call(
        body, name="allgather_small", in_specs=[vm, _ANY], out_specs=vm,
        out_shape=jax.ShapeDtypeStruct((N_DEV,) + slab.shape, slab.dtype),
        scratch_shapes=[pltpu.SemaphoreType.DMA((N_DEV - 1,)), pltpu.SemaphoreType.DMA((N_DEV - 1,))],
        compiler_params=pltpu.CompilerParams(has_side_effects=True),
    )(slab, after)


def _pack_b(w_mem_kv, w_br_dn, w_br_sb, w_br_mem, w_out):
    return jnp.concatenate([w_mem_kv.reshape(128, D_MODEL), w_br_dn, w_br_sb, w_br_mem.reshape(64, D_MODEL), w_out],
                           axis=0)


def _conv_slab(conv_w):
    return jnp.pad(conv_w.reshape(3, D_MODEL), ((0, 29), (0, 0)))


def _unpack_b(slab):
    return (slab[B_MEMKV:B_BRDN].reshape(1, 256, 512), slab[B_BRDN:B_BRSB].reshape(1, 256, D_MODEL),
            slab[B_BRSB:B_BRMEM].reshape(1, 256, D_MODEL), slab[B_BRMEM:B_OUT].reshape(1, 256, 256),
            slab[B_OUT:B_CONV].reshape(1, 256, D_MODEL))


def _conv_rows(conv_full):
    return conv_full.reshape(4 * CONV_BLOCKS, 128)


def _conv_shard_rows(conv_shard, shard):
    own = CONV_BLOCKS // N_SHARD
    blocks = lax.dynamic_update_slice(jnp.zeros((4, CONV_BLOCKS, 128), F32), conv_shard.reshape(4, own, 128),
                                      (0, own * shard, 0))
    return blocks.reshape(4 * CONV_BLOCKS, 128)


def _conv_shard_of(rows, shard):
    own = CONV_BLOCKS // N_SHARD
    blocks = lax.dynamic_slice(rows.reshape(4, CONV_BLOCKS, 128), (0, own * shard, 0), (4, own, 128))
    return blocks.reshape(1, 4, own * 128)


def _pack_small(norm_g, mem_norm_g, final_g, dn_norm_g, a_log, dt_bias, conv_rows, loss=None):
    def row(v):
        v = v.reshape(1, -1).astype(F32)
        return jnp.pad(v, ((0, 0), (0, 128 - v.shape[1])))

    loss_row = row(jnp.zeros((1,), F32) if loss is None else jnp.reshape(loss, (1,)))
    rid = lax.broadcasted_iota(jnp.int32, (8, 128), 0) + S_DNNORM
    tile = jnp.where(rid == S_DNNORM, dn_norm_g.reshape(1, 128), jnp.where(
        rid == S_ALOG, row(a_log), jnp.where(rid == S_DTB, row(dt_bias), jnp.where(rid == S_LOSS, loss_row, 0.0))))
    return jnp.concatenate([norm_g.reshape(8, 128), mem_norm_g.reshape(8, 128), final_g.reshape(8, 128), tile,
                            conv_rows], axis=0)


def _unpack_small(slab, shard):
    return (slab[S_NORM:S_NORM + 8].reshape(1, D_MODEL), slab[S_MEMNORM:S_MEMNORM + 8].reshape(1, D_MODEL),
            slab[S_FINAL:S_FINAL + 8].reshape(D_MODEL), slab[S_DNNORM].reshape(1, 128),
            slab[S_ALOG, :N_HEADS].reshape(1, N_HEADS), slab[S_DTB, :N_HEADS].reshape(1, N_HEADS),
            _conv_shard_of(slab[S_CONV:], shard))


def _windows_to_w_r(win):
    b = 128
    s0, s1, s2, s3 = win[0], win[1], win[2], win[3]
    e1, e2, e3 = WIN_START[1] * b, WIN_START[2] * b, WIN_START[3] * b
    n1, n2 = e2 - e1, e3 - e2
    return jnp.concatenate([
        s0[:, :e1], s0[:, e1:e1 + b] + s1[:, :b],
        s1[:, b:n1], s1[:, n1:n1 + b] + s2[:, :b],
        s2[:, b:n2], s2[:, n2:n2 + b] + s3[:, :b],
        s3[:, b:], s1[:, _S1_BA_POS:], jnp.zeros((win.shape[1], W_R - C_BA - b), win.dtype)], axis=1)


def _dproj_windows(dproj_r):
    b = 128
    pieces = []
    for s in range(N_SHARD):
        lo = WIN_START[s] * b
        if s == 1:
            pieces += [dproj_r[:, lo:lo + _S1_BA_POS], dproj_r[:, C_BA:C_BA + b]]
        else:
            pieces.append(dproj_r[:, lo:lo + SHARD_PAD])
    return jnp.concatenate(pieces, axis=1)


def _local_step(x, mem, tgt, norm_g, mem_norm_g, w_r, w_sh, conv_w, a_log, dt_bias, dn_norm_g, proj_weights, final_g,
                on_early=None, after_gather=None, h=None):
    t = x.shape[0]
    final_row = final_g.reshape(1, D_MODEL)
    lanes_8_16 = ((0, 0), (N_HEADS, 128 - 2 * N_HEADS))
    alog_row = jnp.pad(a_log.reshape(1, N_HEADS), lanes_8_16)
    dtb_row = jnp.pad(dt_bias.reshape(1, N_HEADS), lanes_8_16)

    if h is None:
        h = _rmsnorm_fwd(x, norm_g, "norm_fwd")
    proj = _mm(h, w_r, "nn", "in_proj", after=after_gather, tm_max=2048)
    qkv = _dn_prep_fwd(proj, conv_w)
    beta_t, g_t = _dn_gate_fwd(proj, alog_row, dtb_row)
    dn_u, dn_w, dn_qd, dn_kd, dn_a, tinv_all, dn_el = _dn_intra_fwd(qkv, beta_t, g_t)
    o_dn, dn_vn, s_all = _dn_scan_fwd(dn_u, dn_w, dn_qd, dn_kd, dn_a, dn_el)
    o_dn_g = _dn_post_fwd(o_dn, proj, dn_norm_g)
    o_sb, o_sb_g, sb_l = _sb_fwd(proj)
    w_mem_kv, w_br_dn, w_br_sb, w_br_mem, w_out = proj_weights(o_sb_g)
    mem_n = _rmsnorm_fwd(mem, mem_norm_g, "mem_norm_fwd")
    mkv = _mm(mem_n, w_mem_kv, "nn", "mem_kv")
    o_m, o_m_g = _mem_fwd(proj, mkv)
    y_dn = _mm(o_dn_g, w_br_dn, "nn", "br_dn", out_dtype=BF16)
    y_sb = _mm(o_sb_g, w_br_sb, "nn", "br_sb", out_dtype=BF16)
    y_m = _mm(o_m_g, w_br_mem, "nn", "br_mem", out_dtype=BF16)
    merged = _merge_fwd(proj, y_dn, y_sb, y_m)
    mo = _mm(merged, w_out, "nn", "out_proj")
    d_out, d_out_b, loss_row, g_final = _final_loss(x, mo, final_row, tgt)

    g_w_out = _mm(merged, d_out_b, "tn", "g_w_out", out_dtype=BF16)
    d_merged = _mm(d_out_b, w_out, "nt", "d_merged")
    dy_dn, dy_sb, dy_m, dg1, dg2, dg3 = _merge_bwd(proj, y_dn, y_sb, y_m, d_merged)
    g_w_br_dn = _mm(o_dn_g, dy_dn, "tn", "g_w_br_dn", out_dtype=BF16)
    g_w_br_sb = _mm(o_sb_g, dy_sb, "tn", "g_w_br_sb", out_dtype=BF16)
    g_w_br_mem = _mm(o_m_g, dy_m, "tn", "g_w_br_mem", out_dtype=BF16)
    d_o_dn_g = _mm(dy_dn, w_br_dn, "nt", "d_o_dn")
    d_o_sb_g = _mm(dy_sb, w_br_sb, "nt", "d_o_sb")
    d_o_m_g = _mm(dy_m, w_br_mem, "nt", "d_o_mem")

    d_mq, d_mz, d_mkv = _mem_bwd(proj, mkv, o_m, d_o_m_g)
    d_mkv_b = _cast_bf16(d_mkv, "cast_dmkv")
    g_w_mem_kv = _mm(mem_n, d_mkv_b, "tn", "g_w_mem_kv", out_dtype=BF16)
    d_mem_n = _mm(d_mkv_b, w_mem_kv, "nt", "d_mem_n")
    _, g_mem_norm = _rmsnorm_bwd(mem, mem_norm_g, d_mem_n, jnp.zeros_like(mem), "mem_norm_bwd")

    early = dict(w_mem_kv=g_w_mem_kv, w_br_dn=g_w_br_dn, w_br_sb=g_w_br_sb, w_br_mem=g_w_br_mem, w_out=g_w_out)
    after_early = on_early(early) if on_early is not None else None

    d_sq, d_sk, d_sv, d_sz = _sb_bwd(proj, o_sb, sb_l, d_o_sb_g, after=after_early)

    d_o_dn, d_dnz, g_dn_norm = _dn_post_bwd(o_dn, proj, dn_norm_g, d_o_dn_g)
    d_vnew, d_kd, d_qd, d_w, d_el = _dn_scan_bwd(dn_w, dn_qd, dn_kd, dn_a, dn_el, dn_vn, s_all, d_o_dn)
    d_qn, d_kn, d_vn, dbeta_t, dg_t = _dn_intra_bwd(qkv, beta_t, g_t, tinv_all, dn_vn, d_o_dn, d_vnew, d_kd, d_qd, d_w, d_el)
    d_conv_in, g_conv = _dn_prep_bwd(proj, conv_w, d_qn, d_kn, d_vn)
    d_ba, g_alog_row, g_dtb_row = _dn_gate_bwd(proj, alog_row, dtb_row, dbeta_t, dg_t)

    dproj_sh = _dproj_windows(
        jnp.concatenate([d_conv_in, d_dnz, d_sq, d_sk, d_sv, d_sz, d_mq, d_mz, dg1, dg2, dg3, d_ba], axis=1))
    g_w_sh = _mm(h, dproj_sh, "tn", "g_w_in", out_dtype=BF16, out_shards=N_SHARD, tn_max=1024)
    def input_grad(after=None):
        dh = _mm(dproj_sh, w_sh, "nt", "d_h", after=after, tm_max=2048, tn_max=1024)
        grad_x, g_norm = _rmsnorm_bwd(x, norm_g, dh, d_out, "norm_bwd")
        small = dict(norm_g=g_norm, mem_norm_g=g_mem_norm, final_g=g_final, dn_norm_g=g_dn_norm,
                     a_log=g_alog_row[:, N_HEADS:2 * N_HEADS], dt_bias=g_dtb_row[:, N_HEADS:2 * N_HEADS],
                     conv_w=g_conv)
        return grad_x, small

    return loss_row[0, 0], early, g_w_sh, input_grad


def _reduce_scatter_start(grads, tag):
    c = lax.axis_index("c")
    core = jnp.reshape(c, (1,)).astype(jnp.int32)
    recv = _pair_reduce_send(grads, tag)
    parts = [_pair_add(g, r, core, "pair_add_" + tag) for g, r in zip(grads, recv)]
    return _chip_exchange_start(parts, tag)


def _reduce_scatter_finish(handle, after, tag):
    send_sems, recv_sems, parts, lands, _ = handle
    x, y, c = lax.axis_index("x"), lax.axis_index("y"), lax.axis_index("c")
    place = jnp.stack([2 * x + y, c]).astype(jnp.int32)
    parts, by_chip = _chip_exchange_wait(send_sems, recv_sems, parts, lands, after, tag)
    fulls = [_chip_sum(p, b, place, "chip_sum_" + tag) for p, b in zip(parts, by_chip)]
    return _pair_allgather(fulls, tag)


def kernel(x, mem, norm_g, mem_norm_g, w_in, conv_w, a_log, dt_bias, dn_norm_g, w_mem_kv, w_br_dn, w_br_sb, w_br_mem, w_out, final_g, loss_target, m_norm_g, m_mem_norm_g, m_w_in, m_conv_w, m_a_log, m_dt_bias, m_dn_norm_g, m_w_mem_kv, m_w_br_dn, m_w_br_sb, m_w_br_mem, m_w_out, m_final_g, v_norm_g, v_mem_norm_g, v_w_in, v_conv_w, v_a_log, v_dt_bias, v_dn_norm_g, v_w_mem_kv, v_w_br_dn, v_w_br_sb, v_w_br_mem, v_w_out, v_final_g):
    w_a = w_in[0]
    w_b = _pack_b(w_mem_kv[0], w_br_dn[0], w_br_sb[0], w_br_mem[0], w_out[0])
    m_b = _pack_b(m_w_mem_kv[0], m_w_br_dn[0], m_w_br_sb[0], m_w_br_mem[0], m_w_out[0])
    v_b = _pack_b(v_w_mem_kv[0], v_w_br_dn[0], v_w_br_sb[0], v_w_br_mem[0], v_w_out[0])

    shard_idx = 2 * lax.axis_index("x") + lax.axis_index("y")
    shard = jnp.reshape(shard_idx, (1,)).astype(jnp.int32)
    own = [_cast_to_window(w_a, shard, "cast_w_in"), _cast_bf16(_conv_slab(conv_w[0]), "cast_conv")]
    send1, recv1, own, lands, token = _halves_gather_start(own)
    h = _rmsnorm_fwd(x[0], norm_g, "norm_fwd", after=token)
    w_b_bf = _cast_bf16(w_b, "cast_w_b")
    send2, recv2, lands = _halves_gather_forward(send1, recv1, own, lands,
                                                 after=[h, w_b_bf, m_w_in[0], v_w_in[0], m_b, v_b])
    lands = _halves_gather_wait(send2, recv2, lands)
    ga, g_conv = [lax.dynamic_update_slice(land, o[None], (shard_idx, 0, 0)) for land, o in zip(lands, own)]
    w_r = _windows_to_w_r(ga)
    f_conv = g_conv[:, :3].reshape(N_SHARD, 4, 768).transpose(1, 0, 2).reshape(4, 3 * D_MODEL).astype(F32)
    b_flight = _shard_gather_start(w_b_bf, after=ga)

    def proj_weights(after):
        own, land = _shard_gather_wait(b_flight[0], b_flight[1], b_flight[2], b_flight[3], after)
        gb = lax.dynamic_update_slice(land, own[None], (shard_idx, 0, 0))
        return (gb[:, B_MEMKV:B_BRDN].reshape(N_SHARD * 256, 512),
                gb[:, B_BRDN:B_BRSB].reshape(N_SHARD * 256, D_MODEL),
                gb[:, B_BRSB:B_BRMEM].reshape(N_SHARD * 256, D_MODEL),
                gb[:, B_BRMEM:B_OUT].reshape(N_SHARD, 256, 256).transpose(1, 0, 2).reshape(256, D_MODEL),
                gb[:, B_OUT:B_CONV].reshape(N_SHARD * 256, D_MODEL))

    flights = {}

    def on_early(grads):
        g_b = jnp.concatenate([
            grads["w_mem_kv"].reshape(N_SHARD, 128, D_MODEL), grads["w_br_dn"].reshape(N_SHARD, 256, D_MODEL),
            grads["w_br_sb"].reshape(N_SHARD, 256, D_MODEL),
            grads["w_br_mem"].reshape(256, N_SHARD, 256).transpose(1, 0, 2).reshape(N_SHARD, 64, D_MODEL),
            grads["w_out"].reshape(N_SHARD, 256, D_MODEL)], axis=1).astype(BF16)
        flights["b"] = _reduce_scatter_start([g_b], "b")
        return flights["b"][4]

    loss, _, g_w_sh, input_grad = _local_step(
        x[0], mem[0], loss_target[0], norm_g, mem_norm_g, w_r, ga, f_conv, a_log, dt_bias, dn_norm_g,
        proj_weights, final_g, on_early=on_early, after_gather=b_flight[4], h=h)
    flights["a"] = _reduce_scatter_start([g_w_sh], "a")
    grad_x, small = input_grad(after=flights["a"][4])

    part = _pack_small(small["norm_g"], small["mem_norm_g"], small["final_g"], small["dn_norm_g"],
                       small["a_log"], small["dt_bias"], _conv_rows(small["conv_w"]), loss)
    w_s = _pack_small(norm_g, mem_norm_g, final_g, dn_norm_g, a_log, dt_bias, _conv_shard_rows(conv_w[0], shard_idx))
    m_s = _pack_small(m_norm_g, m_mem_norm_g, m_final_g, m_dn_norm_g, m_a_log, m_dt_bias,
                      _conv_shard_rows(m_conv_w[0], shard_idx))
    v_s = _pack_small(v_norm_g, v_mem_norm_g, v_final_g, v_dn_norm_g, v_a_log, v_dt_bias,
                      _conv_shard_rows(v_conv_w[0], shard_idx))
    (gs_b,) = _reduce_scatter_finish(flights["b"], after=grad_x, tag="b")
    gr_b, d_b, nm_b, nv_b = _adamw(w_b, gs_b, m_b, v_b, "adamw_b")
    g_s, d_s, nm_s, nv_s = _small_update(_allgather_small(part, after=d_b), w_s, m_s, v_s)

    (gs_in,) = _reduce_scatter_finish(flights["a"], after=g_s, tag="a")
    gr_in, d_in, nm_in, nv_in = _adamw_window(w_a, gs_in, m_w_in[0], v_w_in[0], shard, "adamw_w_in")

    def assemble(slab_small, a_in, slab_b):
        s_norm, s_memnorm, s_final, s_dnnorm, s_alog, s_dtb, b_conv = _unpack_small(slab_small, shard_idx)
        b_memkv, b_brdn, b_brsb, b_brmem, b_out = _unpack_b(slab_b)
        return [s_norm, s_memnorm, a_in.reshape(1, D_MODEL, IN_WIDTH // N_SHARD), b_conv, s_alog, s_dtb, s_dnnorm,
                b_memkv, b_brdn, b_brsb, b_brmem, b_out, s_final]

    outs = [g_s[S_LOSS, 0], grad_x.reshape(1, -1, D_MODEL)]
    outs += assemble(g_s, gr_in, gr_b)
    outs += assemble(d_s, d_in, d_b)
    outs += assemble(nm_s, nm_in, nm_b)
    outs += assemble(nv_s, nv_in, nv_b)
    return tuple(outs)
```

```python
import math

import jax
import jax.numpy as jnp
from jax import lax
from jax.experimental import pallas as pl
from jax.experimental.pallas import tpu as pltpu

F32 = jnp.float32
BF16 = jnp.bfloat16
MESH = pl.DeviceIdType.MESH

D_MODEL = 1024
N_HEADS = 8
D_HEAD = 128
DN_CHUNK = 64
DN_GROUP = 32
DN_SCAN_GROUP = 4
SB_BLOCK = 256
SB_HEADS_PER_STEP = 2
SB_QBLOCK = 256
MEM_HEADS = 4
MEM_DH = 64
MEM_W = MEM_HEADS * MEM_DH
NORM_EPS = 1e-6
IN_WIDTH = 11792
N_SHARD = 4
SHARD_W = IN_WIDTH // N_SHARD
SHARD_PAD = 3072
N_DEV = 8

C_DNZ = 3072
C_SBQ = 4096
C_SBZ = 7168
C_MQ = 8192
C_MZ = 8448
C_GATES = 8704
C_BA = 11776
W_R = 12288

ADAM_LR = 0.001
ADAM_B1 = 0.9
ADAM_B2 = 0.999
ADAM_EPS = 1e-08
ADAM_WD = 0.01
ADAM_STEP = 10

VMEM_LIMIT = 56 * 1024 * 1024

B_MEMKV, B_BRDN, B_BRSB, B_BRMEM, B_OUT, B_CONV = 0, 128, 384, 640, 704, 960
S_NORM, S_MEMNORM, S_FINAL, S_DNNORM, S_ALOG, S_DTB, S_LOSS, S_CONV, S_ROWS = 0, 8, 16, 24, 25, 26, 27, 32, 128
CONV_BLOCKS = 3 * D_MODEL // 128


def _cp(**kw):
    return pltpu.CompilerParams(vmem_limit_bytes=VMEM_LIMIT, **kw)


def _dot(a, b, dims):
    lead = a.ndim - 2
    ca, cb = {"nn": (1, 0), "nt": (1, 1), "tn": (0, 0)}[dims]
    batch = tuple(range(lead))
    return lax.dot_general(a, b, (((ca + lead,), (cb + lead,)), (batch, batch)), preferred_element_type=F32)


def _chunks(x):
    return x.reshape(x.shape[0] // DN_CHUNK, DN_CHUNK, x.shape[1])


def _unchunk(x):
    return x.reshape(x.shape[0] * x.shape[1], x.shape[2])


def _bdot(a, b, dims):
    return _dot(a.astype(BF16), b.astype(BF16), dims)


def _split(a):
    hi = a.astype(BF16)
    return hi, (a - hi.astype(F32)).astype(BF16)


def _dot3(a, b, dims):
    a1, a2 = _split(a)
    b1, b2 = _split(b)
    return _dot(a1, b1, dims) + (_dot(a1, b2, dims) + _dot(a2, b1, dims))


def _ones_dot(a, ones_bf16):
    out = _dot(a.reshape(-1, a.shape[-1]).astype(BF16), ones_bf16, "nn")
    return out.reshape(a.shape[:-1] + (ones_bf16.shape[1],))


def _sigmoid(x):
    return 1.0 / (1.0 + jnp.exp(-x))


def _log1p_small(u):
    return jnp.where(u < 1e-2, u * (1.0 - u * (0.5 - u * (1.0 / 3.0))), jnp.log(1.0 + u))


def _pick(dim, cands):
    for c in cands:
        if dim % c == 0:
            return c
    return dim


def _mm(a, b, dims, name, out_dtype=F32, out_shards=1, after=None, tm_max=1024, tn_max=512):
    ta, tb = dims[0] == "t", dims[1] == "t"
    m, k = (a.shape[1], a.shape[0]) if ta else a.shape
    b_shards = b.shape[0] if b.ndim == 3 else 1
    n = b.shape[-2] if tb else b.shape[-1]
    tm = _pick(m, (tm_max, 1024, 512, 256))
    tn = _pick(n // out_shards, (tn_max, 512, 384, 256, 128))
    tk = _pick(k // b_shards, (2048, 1024, 512, 384, 256))
    nk = k // tk

    def body(a_ref, b_ref, *rest):
        if nk == 1:
            rest[-1][...] = _bdot(a_ref[...], b_ref[...], dims).astype(out_dtype)
            return
        o_ref, acc_ref = rest[-2:]
        kk = pl.program_id(2)

        @pl.when(kk == 0)
        def _():
            acc_ref[...] = jnp.zeros_like(acc_ref)

        acc_ref[...] += _bdot(a_ref[...], b_ref[...], dims)

        @pl.when(kk == nk - 1)
        def _():
            o_ref[...] = acc_ref[...].astype(out_dtype)

    a_spec = pl.BlockSpec((tk, tm), lambda i, j, q: (q, i)) if ta else pl.BlockSpec((tm, tk), lambda i, j, q: (i, q))
    if b_shards > 1:
        per_k = k // b_shards // tk
        b_spec = pl.BlockSpec((None, tn, tk), lambda i, j, q: (q // per_k, j, q % per_k))
    else:
        b_spec = pl.BlockSpec((tn, tk), lambda i, j, q: (j, q)) if tb else pl.BlockSpec((tk, tn), lambda i, j, q: (q, j))
    if out_shards > 1:
        per_n = n // out_shards // tn
        out_spec = pl.BlockSpec((None, tm, tn), lambda i, j, q: (j // per_n, i, j % per_n))
        out_shape = jax.ShapeDtypeStruct((out_shards, m, n // out_shards), out_dtype)
    else:
        out_spec = pl.BlockSpec((tm, tn), lambda i, j, q: (i, j))
        out_shape = jax.ShapeDtypeStruct((m, n), out_dtype)
    extra_specs, extra = [], []
    if after is not None:
        extra_specs, extra = [pl.BlockSpec(after.shape, lambda i, j, q: (0, 0))], [after]
    return pl.pallas_call(
        body, name=name, grid=(m // tm, n // tn, nk),
        in_specs=[a_spec, b_spec] + extra_specs, out_specs=out_spec, out_shape=out_shape,
        scratch_shapes=[pltpu.VMEM((tm, tn), F32)] if nk > 1 else [],
        compiler_params=_cp(dimension_semantics=("parallel", "parallel", "arbitrary")),
    )(a, b, *extra)


def _rmsnorm_fwd(x, g, name, after=None):
    t, d = x.shape
    tb = _pick(t, (512, 256))

    def body(x_ref, g_ref, *rest):
        xv = x_ref[...]
        r = lax.rsqrt(jnp.mean(xv * xv, axis=-1, keepdims=True) + NORM_EPS)
        rest[-1][...] = ((xv * r) * g_ref[...]).astype(BF16)

    extra_specs, extra = [], []
    if after is not None:
        extra_specs, extra = [pl.BlockSpec(after.shape, lambda i: (0, 0))], [after]
    return pl.pallas_call(
        body, name=name, grid=(t // tb,),
        in_specs=[pl.BlockSpec((tb, d), lambda i: (i, 0)), pl.BlockSpec((1, d), lambda i: (0, 0))] + extra_specs,
        out_specs=pl.BlockSpec((tb, d), lambda i: (i, 0)),
        out_shape=jax.ShapeDtypeStruct((t, d), BF16), compiler_params=_cp(),
    )(x, g, *extra)


def _rmsnorm_bwd(x, g, dh, resid, name):
    t, d = x.shape
    tb = _pick(t, (512, 256))

    def body(x_ref, g_ref, dh_ref, r_ref, dx_ref, dg_ref):
        @pl.when(pl.program_id(0) == 0)
        def _():
            dg_ref[...] = jnp.zeros_like(dg_ref)

        xv = x_ref[...]
        r = lax.rsqrt(jnp.mean(xv * xv, axis=-1, keepdims=True) + NORM_EPS)
        xhat = xv * r
        dhv = dh_ref[...]
        dg_ref[...] += jnp.sum(dhv * xhat, axis=0, keepdims=True)
        dxh = dhv * g_ref[...]
        dx_ref[...] = r_ref[...] + r * (dxh - xhat * jnp.mean(dxh * xhat, axis=-1, keepdims=True))

    row = pl.BlockSpec((tb, d), lambda i: (i, 0))
    vec = pl.BlockSpec((1, d), lambda i: (0, 0))
    return pl.pallas_call(
        body, name=name, grid=(t // tb,), in_specs=[row, vec, row, row], out_specs=[row, vec],
        out_shape=[jax.ShapeDtypeStruct((t, d), F32), jax.ShapeDtypeStruct((1, d), F32)], compiler_params=_cp(),
    )(x, g, dh, resid)


def _conv_silu(xv, w, row):
    y = xv * w[3:4, :]
    for s in (1, 2, 3):
        xs = jnp.where(row >= s, pltpu.roll(xv, s, 0), 0.0)
        y = y + xs * w[3 - s:4 - s, :]
    sig = _sigmoid(y)
    return y, sig, y * sig


def _dn_prep_fwd(proj, conv_w):
    t = proj.shape[0]

    def body(p_ref, w_ref, o_ref):
        j = pl.program_id(0)
        xv = p_ref[...]
        row = lax.broadcasted_iota(jnp.int32, xv.shape, 0)
        _, _, a = _conv_silu(xv, w_ref[...], row)
        inv = lax.rsqrt(jnp.sum(a * a, axis=-1, keepdims=True) + NORM_EPS)
        scale = jnp.where(j < N_HEADS, D_HEAD ** -0.5, 1.0)
        normed = jnp.where(j < 2 * N_HEADS, 1.0, 0.0)
        o_ref[...] = a * (normed * (inv * scale) + (1.0 - normed))

    return pl.pallas_call(
        body, name="dn_prep_fwd", grid=(3 * N_HEADS,),
        in_specs=[pl.BlockSpec((t, D_HEAD), lambda j: (0, j)), pl.BlockSpec((4, D_HEAD), lambda j: (0, j))],
        out_specs=pl.BlockSpec((t, D_HEAD), lambda j: (0, j)),
        out_shape=jax.ShapeDtypeStruct((t, 3 * D_MODEL), F32), compiler_params=_cp(),
    )(proj, conv_w)


def _dn_prep_bwd(proj, conv_w, dq, dk, dv):
    t = proj.shape[0]

    def body(p_ref, w_ref, dq_ref, dk_ref, dv_ref, dp_ref, dw_ref):
        j = pl.program_id(0)
        xv = p_ref[...]
        w = w_ref[...]
        row = lax.broadcasted_iota(jnp.int32, xv.shape, 0)
        y, s, a = _conv_silu(xv, w, row)
        part = jnp.zeros(xv.shape, jnp.int32) + j // N_HEADS
        dn = jnp.where(part == 0, dq_ref[...], jnp.where(part == 1, dk_ref[...], dv_ref[...]))
        inv = lax.rsqrt(jnp.sum(a * a, axis=-1, keepdims=True) + NORM_EPS)
        scale = jnp.where(j < N_HEADS, D_HEAD ** -0.5, 1.0)
        ds = dn * scale
        da_norm = inv * ds - a * (inv * inv * inv) * jnp.sum(ds * a, axis=-1, keepdims=True)
        normed = jnp.where(j < 2 * N_HEADS, 1.0, 0.0)
        da = normed * da_norm + (1.0 - normed) * dn
        dy = da * (s * (1.0 + y * (1.0 - s)))
        dx = dy * w[3:4, :]
        dw_ref[3:4, :] = jnp.sum(dy * xv, axis=0, keepdims=True)
        for sft in (1, 2, 3):
            xs = jnp.where(row >= sft, pltpu.roll(xv, sft, 0), 0.0)
            dw_ref[3 - sft:4 - sft, :] = jnp.sum(dy * xs, axis=0, keepdims=True)
            dys = jnp.where(row < t - sft, pltpu.roll(dy, t - sft, 0), 0.0)
            dx = dx + dys * w[3 - sft:4 - sft, :]
        dp_ref[...] = dx.astype(BF16)

    blk = pl.BlockSpec((t, D_HEAD), lambda j: (0, j))
    wblk = pl.BlockSpec((4, D_HEAD), lambda j: (0, j))

    def grad(part):
        return pl.BlockSpec((t, D_HEAD), lambda j: (0, jnp.clip(j - part * N_HEADS, 0, N_HEADS - 1)))

    return pl.pallas_call(
        body, name="dn_prep_bwd", grid=(3 * N_HEADS,), in_specs=[blk, wblk, grad(0), grad(1), grad(2)],
        out_specs=[blk, wblk],
        out_shape=[jax.ShapeDtypeStruct((t, 3 * D_MODEL), BF16), jax.ShapeDtypeStruct((4, 3 * D_MODEL), F32)],
        compiler_params=_cp(),
    )(proj, conv_w, dq, dk, dv)


def _softplus_parts(xv):
    e = jnp.exp(-jnp.abs(xv))
    return jnp.maximum(xv, 0.0) + _log1p_small(e)


def _chunk_scan(v, row, reverse):
    t = v.shape[0]
    pos = row & (DN_CHUNK - 1)
    s = 1
    while s < DN_CHUNK:
        if reverse:
            v = v + jnp.where(pos < DN_CHUNK - s, pltpu.roll(v, t - s, 0), 0.0)
        else:
            v = v + jnp.where(pos >= s, pltpu.roll(v, s, 0), 0.0)
        s *= 2
    return v


def _dn_gate_fwd(proj, alog_row, dtb_row):
    t = proj.shape[0]

    def body(p_ref, al_ref, dt_ref, b_ref, g_ref):
        p = p_ref[...]
        row = lax.broadcasted_iota(jnp.int32, p.shape, 0)
        b_ref[...] = _sigmoid(p)
        g = -jnp.exp(al_ref[...]) * _softplus_parts(p + dt_ref[...])
        g_ref[...] = _chunk_scan(g, row, reverse=False)

    blk = pl.BlockSpec((t, 128), lambda i: (0, C_BA // 128))
    vec = pl.BlockSpec((1, 128), lambda i: (0, 0))
    out = pl.BlockSpec((t, 128), lambda i: (0, 0))
    return pl.pallas_call(
        body, name="dn_gate_fwd", grid=(1,), in_specs=[blk, vec, vec], out_specs=[out, out],
        out_shape=[jax.ShapeDtypeStruct((t, 128), F32)] * 2, compiler_params=_cp(),
    )(proj, alog_row, dtb_row)


def _dn_gate_bwd(proj, alog_row, dtb_row, dbeta, dgc):
    t = proj.shape[0]

    def body(p_ref, al_ref, dt_ref, db_ref, dg_ref, dp_ref, dal_ref, ddt_ref):
        p = p_ref[...]
        row = lax.broadcasted_iota(jnp.int32, p.shape, 0)
        lane = lax.broadcasted_iota(jnp.int32, p.shape, 1)
        s = _sigmoid(p)
        d_b = db_ref[...] * s * (1.0 - s)
        dg = _chunk_scan(dg_ref[...], row, reverse=True)
        xa = p + dt_ref[...]
        ea = jnp.exp(al_ref[...])
        g = -ea * _softplus_parts(xa)
        d_a = dg * (-ea) * _sigmoid(xa)
        dp_ref[...] = jnp.where(lane < N_HEADS, d_b, jnp.where(lane < 2 * N_HEADS, d_a, 0.0)).astype(BF16)
        dal_ref[...] = jnp.sum(dg * g, axis=0, keepdims=True)
        ddt_ref[...] = jnp.sum(d_a, axis=0, keepdims=True)

    blk = pl.BlockSpec((t, 128), lambda i: (0, C_BA // 128))
    vec = pl.BlockSpec((1, 128), lambda i: (0, 0))
    full = pl.BlockSpec((t, 128), lambda i: (0, 0))
    return pl.pallas_call(
        body, name="dn_gate_bwd", grid=(1,), in_specs=[blk, vec, vec, full, full], out_specs=[full, vec, vec],
        out_shape=[jax.ShapeDtypeStruct((t, 128), BF16), jax.ShapeDtypeStruct((1, 128), F32),
                   jax.ShapeDtypeStruct((1, 128), F32)], compiler_params=_cp(),
    )(proj, alog_row, dtb_row, dbeta, dgc)


def _col_to_row(col, eye):
    return jnp.sum(jnp.where(eye, col, 0.0), axis=-2, keepdims=True)


def _row_to_col(rowv, eye):
    return jnp.sum(jnp.where(eye, rowv, 0.0), axis=-1, keepdims=True)


def _tri_inverse(m, ri, ci):
    eye = (ri == ci).astype(F32)
    b16 = (ri >> 4) == (ci >> 4)
    b32 = (ri >> 5) == (ci >> 5)
    m1 = jnp.where(b16, m, 0.0)
    x = eye - m1
    p = _dot3(m1, m1, "nn")
    x = x + _dot3(x, p, "nn")
    p = _dot3(p, p, "nn")
    x = x + _dot3(x, p, "nn")
    p = _dot3(p, p, "nn")
    x = x + _dot3(x, p, "nn")
    c1 = jnp.where(jnp.logical_and(b32, jnp.logical_not(b16)), m, 0.0)
    x = x - _dot3(_dot3(x, c1, "nn"), x, "nn")
    c2 = jnp.where(b32, 0.0, m)
    x = x - _dot3(_dot3(x, c2, "nn"), x, "nn")
    return x


def _dn_chunk_common(q, k, gc, ri, ci):
    eye = ri == ci
    g_row = _col_to_row(gc, eye)
    diff = jnp.minimum(gc - g_row, 0.0)
    gam = jnp.where(ri >= ci, jnp.exp(diff), 0.0)
    kk = _bdot(k, k, "nt")
    qk = _bdot(q, k, "nt")
    rcol = lax.broadcasted_iota(jnp.int32, gc.shape, gc.ndim - 2)
    last = jnp.sum(jnp.where(rcol == DN_CHUNK - 1, gc, 0.0), axis=-2, keepdims=True)
    e_g = jnp.exp(gc)
    dec = jnp.exp(last - gc)
    return eye, gam, kk, qk, last, e_g, dec, rcol


def _dn_specs(t, rows_blk):
    def head(off):
        return pl.BlockSpec((rows_blk, D_HEAD), lambda g, h: (g, off + h))

    lanes = pl.BlockSpec((rows_blk, 128), lambda g, h: (g, 0))
    hm = pl.BlockSpec((None, rows_blk, D_HEAD), lambda g, h: (h, g, 0))
    sq = pl.BlockSpec((1, rows_blk, DN_CHUNK), lambda g, h: (h, g, 0))
    tile = pl.BlockSpec((1, rows_blk // DN_CHUNK, 8, 128), lambda g, h: (h, g, 0, 0))
    return head, lanes, hm, sq, tile


def _head_column(slab, lane_idx):
    lane = lax.broadcasted_iota(jnp.int32, slab.shape, 1)
    return _chunks(jnp.sum(jnp.where(lane == lane_idx, slab, 0.0), axis=1, keepdims=True))


def _dn_intra_fwd(qkv, beta_t, g_t):
    t = qkv.shape[0]
    n_chunks = t // DN_CHUNK
    rows_blk = min(DN_GROUP * DN_CHUNK, t)

    def body(q_ref, k_ref, v_ref, b_ref, g_ref, u_ref, w_ref, qd_ref, kd_ref, a_ref, ti_ref, el_ref):
        ri = lax.broadcasted_iota(jnp.int32, (DN_CHUNK, DN_CHUNK), 0)
        ci = lax.broadcasted_iota(jnp.int32, (DN_CHUNK, DN_CHUNK), 1)
        h = pl.program_id(1)
        q, k, v = (_chunks(r[...]) for r in (q_ref, k_ref, v_ref))
        b, gc = _head_column(b_ref[...], h), _head_column(g_ref[...], h + N_HEADS)
        _, gam, kk, qk, last, e_g, dec, _ = _dn_chunk_common(q, k, gc, ri, ci)
        tinv = _tri_inverse(jnp.where(ri > ci, b * kk * gam, 0.0), ri, ci)
        u_ref[...] = _unchunk(_bdot(tinv, v * b, "nn"))
        w_ref[...] = _unchunk(_bdot(tinv, k * (b * e_g), "nn"))
        qd_ref[...] = _unchunk(q * e_g)
        kd_ref[...] = _unchunk(k * dec)
        a_ref[0] = _unchunk(qk * gam)
        ti_ref[0] = _unchunk(tinv)
        el_ref[0] = jnp.broadcast_to(jnp.exp(last), (rows_blk // DN_CHUNK, 8, 128))

    head, lanes, hm, sq, tile = _dn_specs(t, rows_blk)
    act = jax.ShapeDtypeStruct((N_HEADS, t, D_HEAD), F32)
    sqs = jax.ShapeDtypeStruct((N_HEADS, t, DN_CHUNK), F32)
    return pl.pallas_call(
        body, name="dn_intra_fwd", grid=(t // rows_blk, N_HEADS),
        in_specs=[head(0), head(N_HEADS), head(2 * N_HEADS), lanes, lanes],
        out_specs=[hm] * 4 + [sq, sq, tile],
        out_shape=[act] * 4 + [sqs, sqs, jax.ShapeDtypeStruct((N_HEADS, n_chunks, 8, 128), F32)],
        compiler_params=_cp(),
    )(qkv, qkv, qkv, beta_t, g_t)


def _dn_scan_specs(t, rows_blk, reverse):
    n_groups = t // rows_blk

    def at(g):
        return n_groups - 1 - g if reverse else g

    per = rows_blk // DN_CHUNK
    act = pl.BlockSpec((N_HEADS, rows_blk, D_HEAD), lambda g: (0, at(g), 0))
    sq = pl.BlockSpec((N_HEADS, rows_blk, DN_CHUNK), lambda g: (0, at(g), 0))
    state = pl.BlockSpec((N_HEADS, per, D_HEAD, D_HEAD), lambda g: (0, at(g), 0, 0))
    tile = pl.BlockSpec((N_HEADS, per, 8, 128), lambda g: (0, at(g), 0, 0))
    return act, sq, state, tile


def _dn_scan_fwd(u, w, qd, kd, a, el):
    t = u.shape[1]
    n_chunks = t // DN_CHUNK
    rows_blk = DN_SCAN_GROUP * DN_CHUNK

    def body(u_ref, w_ref, qd_ref, kd_ref, a_ref, el_ref, o_ref, vn_ref, s_ref, s_scr):
        @pl.when(pl.program_id(0) == 0)
        def _():
            s_scr[...] = jnp.zeros_like(s_scr)

        for cc in range(DN_SCAN_GROUP):
            rows = slice(cc * DN_CHUNK, (cc + 1) * DN_CHUNK)
            s = s_scr[...]
            s_ref[:, cc] = s
            v_new = u_ref[:, rows, :] - _bdot(w_ref[:, rows, :], s, "nn")
            vn_ref[:, rows, :] = v_new
            o_ref[:, rows, :] = _bdot(qd_ref[:, rows, :], s, "nn") + _bdot(a_ref[:, rows, :], v_new, "nn")
            s_scr[...] = s * el_ref[:, cc][:, 0:1, :] + _bdot(kd_ref[:, rows, :], v_new, "tn")

    act, sq, state, tile = _dn_scan_specs(t, rows_blk, reverse=False)
    shp = jax.ShapeDtypeStruct((N_HEADS, t, D_HEAD), F32)
    return pl.pallas_call(
        body, name="dn_scan_fwd", grid=(t // rows_blk,),
        in_specs=[act, act, act, act, sq, tile], out_specs=[act, act, state],
        out_shape=[shp, shp, jax.ShapeDtypeStruct((N_HEADS, n_chunks, D_HEAD, D_HEAD), F32)],
        scratch_shapes=[pltpu.VMEM((N_HEADS, D_HEAD, D_HEAD), F32)],
        compiler_params=_cp(dimension_semantics=("arbitrary",)),
    )(u, w, qd, kd, a, el)


def _dn_scan_bwd(w, qd, kd, a, el, vn, s_all, do):
    t = w.shape[1]
    n_chunks = t // DN_CHUNK
    rows_blk = DN_SCAN_GROUP * DN_CHUNK

    def body(w_ref, qd_ref, kd_ref, a_ref, el_ref, vn_ref, s_ref, do_ref, dvn_ref, dkd_ref, dqd_ref, dw_ref, dl_ref, ds_scr):
        @pl.when(pl.program_id(0) == 0)
        def _():
            ds_scr[...] = jnp.zeros_like(ds_scr)

        for cc in reversed(range(DN_SCAN_GROUP)):
            rows = slice(cc * DN_CHUNK, (cc + 1) * DN_CHUNK)
            s = s_ref[:, cc]
            d_s = ds_scr[...]
            e_last = el_ref[:, cc][:, 0:1, :]
            d_o = do_ref[:, rows, :]
            dv_new = _bdot(a_ref[:, rows, :], d_o, "tn") + _bdot(kd_ref[:, rows, :], d_s, "nn")
            ds_scr[...] = d_s * e_last + _bdot(qd_ref[:, rows, :], d_o, "tn") - _bdot(w_ref[:, rows, :], dv_new, "tn")
            dvn_ref[:, rows, :] = dv_new
            dkd_ref[:, rows, :] = _bdot(vn_ref[:, rows, :], d_s, "nt")
            dqd_ref[:, rows, :] = _bdot(d_o, s, "nt")
            dw_ref[:, rows, :] = -_bdot(dv_new, s, "nt")
            dlast = jnp.sum(jnp.sum(d_s * s, axis=2, keepdims=True), axis=1, keepdims=True)
            dl_ref[:, cc] = jnp.broadcast_to(dlast * e_last, (N_HEADS, 8, 128))

    act, sq, state, tile = _dn_scan_specs(t, rows_blk, reverse=True)
    shp = jax.ShapeDtypeStruct((N_HEADS, t, D_HEAD), F32)
    return pl.pallas_call(
        body, name="dn_scan_bwd", grid=(t // rows_blk,),
        in_specs=[act, act, act, sq, tile, act, state, act], out_specs=[act] * 4 + [tile],
        out_shape=[shp] * 4 + [jax.ShapeDtypeStruct((N_HEADS, n_chunks, 8, 128), F32)],
        scratch_shapes=[pltpu.VMEM((N_HEADS, D_HEAD, D_HEAD), F32)],
        compiler_params=_cp(dimension_semantics=("arbitrary",)),
    )(w, qd, kd, a, el, vn, s_all, do)


def _dn_intra_bwd(qkv, beta_t, g_t, tinv_all, vn, do, dvn, dkd, dqd, dw, dl):
    t = qkv.shape[0]
    rows_blk = min(DN_GROUP * DN_CHUNK, t)

    def body(q_ref, k_ref, v_ref, b_ref, g_ref, ti_ref, vn_ref, do_ref, dvn_ref, dkd_ref, dqd_ref, dw_ref, dl_ref,
             dq_ref, dk_ref, dv_ref, db_ref, dg_ref):
        ri = lax.broadcasted_iota(jnp.int32, (DN_CHUNK, DN_CHUNK), 0)
        ci = lax.broadcasted_iota(jnp.int32, (DN_CHUNK, DN_CHUNK), 1)
        h = pl.program_id(1)
        q, k, v = (_chunks(r[...]) for r in (q_ref, k_ref, v_ref))
        b, gc = _head_column(b_ref[...], h), _head_column(g_ref[...], h + N_HEADS)
        tinv = _chunks(ti_ref[0])
        dv_new, dk_dec, dq_dec, d_w = (_chunks(r[...]) for r in (dvn_ref, dkd_ref, dqd_ref, dw_ref))
        eye, gam, kk, qk, _, e_g, dec, rcol = _dn_chunk_common(q, k, gc, ri, ci)
        bv = v * b
        bk = k * (b * e_g)

        d_a = jnp.where(ri >= ci, _bdot(_chunks(do_ref[...]), _chunks(vn_ref[...]), "nt"), 0.0)
        dbv = _bdot(tinv, dv_new, "tn")
        dbk = _bdot(tinv, d_w, "tn")
        d_tinv = _bdot(dv_new, bv, "nt") + _bdot(d_w, bk, "nt")
        d_m = -jnp.where(ri > ci, _dot3(_dot3(tinv, d_tinv, "tn"), tinv, "nt"), 0.0)

        d_kk = d_m * b * gam
        d_gam = d_m * b * kk + d_a * qk
        d_qk = d_a * gam
        dq_ref[...] = _unchunk(_bdot(d_qk, k, "nn") + dq_dec * e_g)
        dk_ref[...] = _unchunk(_bdot(d_qk, q, "tn") + _bdot(d_kk, k, "nn") + _bdot(d_kk, k, "tn")
                               + dk_dec * dec + dbk * (b * e_g))
        dv_ref[...] = _unchunk(dbv * b)
        d_b = _unchunk(jnp.sum(d_m * kk * gam, axis=-1, keepdims=True) + jnp.sum(dbv * v, axis=-1, keepdims=True)
                       + jnp.sum(dbk * k, axis=-1, keepdims=True) * e_g)

        xg = d_gam * gam
        kdk = jnp.sum(dk_dec * (k * dec), axis=-1, keepdims=True)
        d_gc = (jnp.sum(xg, axis=-1, keepdims=True) - _row_to_col(jnp.sum(xg, axis=-2, keepdims=True), eye)
                + jnp.sum(dq_dec * (q * e_g), axis=-1, keepdims=True) - kdk
                + jnp.sum(dbk * bk, axis=-1, keepdims=True))
        d_last_total = dl_ref[0][:, 0:1, 0:1] + jnp.sum(kdk, axis=-2, keepdims=True)
        d_g = _unchunk(d_gc + jnp.where(rcol == DN_CHUNK - 1, d_last_total, 0.0))

        @pl.when(h == 0)
        def _():
            db_ref[...] = jnp.zeros_like(db_ref)
            dg_ref[...] = jnp.zeros_like(dg_ref)

        lane = lax.broadcasted_iota(jnp.int32, db_ref.shape, 1)
        db_ref[...] += jnp.where(lane == h, d_b, 0.0)
        dg_ref[...] += jnp.where(lane == h + N_HEADS, d_g, 0.0)

    head, lanes, hm, sq, tile = _dn_specs(t, rows_blk)
    return pl.pallas_call(
        body, name="dn_intra_bwd", grid=(t // rows_blk, N_HEADS),
        in_specs=[head(0), head(N_HEADS), head(2 * N_HEADS), lanes, lanes, sq] + [hm] * 6 + [tile],
        out_specs=[head(0), head(0), head(0), lanes, lanes],
        out_shape=[jax.ShapeDtypeStruct((t, D_MODEL), F32)] * 3 + [jax.ShapeDtypeStruct((t, 128), F32)] * 2,
        compiler_params=_cp(),
    )(qkv, qkv, qkv, beta_t, g_t, tinv_all, vn, do, dvn, dkd, dqd, dw, dl)


def _dn_post_fwd(o, proj, gn):
    t = o.shape[1]

    def body(o_ref, z_ref, g_ref, out_ref):
        ov, z = o_ref[...], z_ref[...]
        r = lax.rsqrt(jnp.mean(ov * ov, axis=-1, keepdims=True) + NORM_EPS)
        out_ref[...] = (((ov * r) * g_ref[...]) * (z * _sigmoid(z))).astype(BF16)

    blk = pl.BlockSpec((t, D_HEAD), lambda h: (0, h))
    return pl.pallas_call(
        body, name="dn_post_fwd", grid=(N_HEADS,),
        in_specs=[pl.BlockSpec((None, t, D_HEAD), lambda h: (h, 0, 0)),
                  pl.BlockSpec((t, D_HEAD), lambda h: (0, C_DNZ // D_HEAD + h)),
                  pl.BlockSpec((1, D_HEAD), lambda h: (0, 0))],
        out_specs=blk, out_shape=jax.ShapeDtypeStruct((t, D_MODEL), BF16), compiler_params=_cp(),
    )(o, proj, gn)


def _dn_post_bwd(o, proj, gn, dout):
    t = o.shape[1]

    def body(o_ref, z_ref, g_ref, d_ref, do_ref, dz_ref, dg_ref):
        @pl.when(pl.program_id(0) == 0)
        def _():
            dg_ref[...] = jnp.zeros_like(dg_ref)

        ov, z, d = o_ref[...], z_ref[...], d_ref[...]
        r = lax.rsqrt(jnp.mean(ov * ov, axis=-1, keepdims=True) + NORM_EPS)
        ohat = ov * r
        s = _sigmoid(z)
        d_on = d * (z * s)
        dz_ref[...] = (d * (ohat * g_ref[...]) * (s * (1.0 + z * (1.0 - s)))).astype(BF16)
        dg_ref[...] += jnp.sum(d_on * ohat, axis=0, keepdims=True)
        dxh = d_on * g_ref[...]
        do_ref[...] = r * (dxh - ohat * jnp.mean(dxh * ohat, axis=-1, keepdims=True))

    blk = pl.BlockSpec((t, D_HEAD), lambda h: (0, h))
    hm = pl.BlockSpec((None, t, D_HEAD), lambda h: (h, 0, 0))
    vec = pl.BlockSpec((1, D_HEAD), lambda h: (0, 0))
    return pl.pallas_call(
        body, name="dn_post_bwd", grid=(N_HEADS,),
        in_specs=[hm, pl.BlockSpec((t, D_HEAD), lambda h: (0, C_DNZ // D_HEAD + h)), vec, blk],
        out_specs=[hm, blk, vec],
        out_shape=[jax.ShapeDtypeStruct((N_HEADS, t, D_HEAD), F32), jax.ShapeDtypeStruct((t, D_MODEL), BF16),
                   jax.ShapeDtypeStruct((1, D_HEAD), F32)], compiler_params=_cp(),
    )(o, proj, gn, dout)


def _sb_fwd(proj):
    t = proj.shape[0]
    qblk = min(SB_QBLOCK, t)
    scale = 1.0 / math.sqrt(D_HEAD)

    hp = SB_HEADS_PER_STEP
    wid = hp * D_HEAD

    def body(q_ref, k_ref, v_ref, z_ref, o_ref, og_ref, l_ref, qb, kb, vb):
        for hh in range(hp):
            hs = slice(hh * D_HEAD, (hh + 1) * D_HEAD)
            qb[hh] = q_ref[:, hs].astype(BF16)
            kb[hh] = k_ref[:, hs].astype(BF16)
            vb[hh] = v_ref[:, hs].astype(BF16)
        ri = lax.broadcasted_iota(jnp.int32, (qblk, SB_BLOCK), 0)
        ci = lax.broadcasted_iota(jnp.int32, (qblk, SB_BLOCK), 1)
        r2 = lax.broadcasted_iota(jnp.int32, (SB_BLOCK, SB_BLOCK), 0)
        c2 = lax.broadcasted_iota(jnp.int32, (SB_BLOCK, SB_BLOCK), 1)
        upper = (r2 > c2).astype(BF16)
        nkb = qblk // SB_BLOCK

        def qblock(i, carry):
            rows = pl.ds(pl.multiple_of(i * qblk, qblk), qblk)
            qi = qb[:, rows, :]

            def tile(j, st, on_diagonal):
                acc, c = st
                cols = pl.ds(pl.multiple_of(j * SB_BLOCK, SB_BLOCK), SB_BLOCK)
                z = _dot(qi, kb[:, cols, :], "nt") * scale
                lb = jnp.minimum(z, 0.0) - jnp.log(1.0 + jnp.exp(-jnp.abs(z)))
                lf = lb - z
                if on_diagonal:
                    mask = (j * SB_BLOCK + ci) < (i * qblk + ri)
                    lf = jnp.where(mask, lf, 0.0)
                att = jnp.exp(lb + (_ones_dot(lf, upper) + c))
                if on_diagonal:
                    att = jnp.where(mask, att, 0.0)
                acc = acc + _dot(att.astype(BF16), vb[:, cols, :], "nn")
                return acc, c + jnp.sum(lf, axis=-1, keepdims=True)

            st = (jnp.zeros((hp, qblk, D_HEAD), F32), jnp.zeros((hp, qblk, 1), F32))
            for d in range(nkb):
                st = tile((i + 1) * nkb - 1 - d, st, True)
            acc, c = lax.fori_loop(0, i * nkb, lambda jj, s: tile(i * nkb - 1 - jj, s, False), st)
            l_ref[:, rows, :] = c
            for hh in range(hp):
                hs = slice(hh * D_HEAD, (hh + 1) * D_HEAD)
                zg = z_ref[rows, hs]
                o_ref[rows, hs] = acc[hh]
                og_ref[rows, hs] = (acc[hh] * (zg * _sigmoid(zg))).astype(BF16)
            return carry

        lax.fori_loop(0, t // qblk, qblock, 0)

    def head(off):
        return pl.BlockSpec((t, wid), lambda h: (0, off // wid + h))

    out = pl.BlockSpec((t, wid), lambda h: (0, h))
    return pl.pallas_call(
        body, name="sb_fwd", grid=(N_HEADS // hp,),
        in_specs=[head(C_SBQ), head(C_SBQ + D_MODEL), head(C_SBQ + 2 * D_MODEL), head(C_SBZ)],
        out_specs=[out, out, pl.BlockSpec((hp, t, 1), lambda h: (h, 0, 0))],
        out_shape=[jax.ShapeDtypeStruct((t, D_MODEL), F32), jax.ShapeDtypeStruct((t, D_MODEL), BF16),
                   jax.ShapeDtypeStruct((N_HEADS, t, 1), F32)],
        scratch_shapes=[pltpu.VMEM((hp, t, D_HEAD), BF16)] * 3, compiler_params=_cp(),
    )(proj, proj, proj, proj)


def _sb_bwd(proj, o, ltot, dog, after=None):
    t = proj.shape[0]
    qblk = min(SB_QBLOCK, t)
    scale = 1.0 / math.sqrt(D_HEAD)

    hp = SB_HEADS_PER_STEP
    wid = hp * D_HEAD

    def body(q_ref, k_ref, v_ref, z_ref, o_ref, l_ref, d_ref, *rest):
        dq_ref, dk_ref, dv_ref, dz_ref, qb, kb, vb, dob, dk_scr, dv_scr = rest[-10:]
        for hh in range(hp):
            hs = slice(hh * D_HEAD, (hh + 1) * D_HEAD)
            qb[hh] = q_ref[:, hs].astype(BF16)
            kb[hh] = k_ref[:, hs].astype(BF16)
            vb[hh] = v_ref[:, hs].astype(BF16)
            zg = z_ref[:, hs]
            sg = _sigmoid(zg)
            dgo = d_ref[:, hs]
            dob[hh] = (dgo * (zg * sg)).astype(BF16)
            dz_ref[:, hs] = (dgo * o_ref[:, hs] * (sg * (1.0 + zg * (1.0 - sg)))).astype(BF16)
        dk_scr[...] = jnp.zeros_like(dk_scr)
        dv_scr[...] = jnp.zeros_like(dv_scr)
        ri = lax.broadcasted_iota(jnp.int32, (qblk, SB_BLOCK), 0)
        ci = lax.broadcasted_iota(jnp.int32, (qblk, SB_BLOCK), 1)
        r2 = lax.broadcasted_iota(jnp.int32, (SB_BLOCK, SB_BLOCK), 0)
        c2 = lax.broadcasted_iota(jnp.int32, (SB_BLOCK, SB_BLOCK), 1)
        upper = (r2 > c2).astype(BF16)
        below = (r2 < c2).astype(BF16)

        def qblock(i, carry):
            rows = pl.ds(pl.multiple_of(i * qblk, qblk), qblk)
            qi = qb[:, rows, :]
            d_o = dob[:, rows, :]
            ltot = l_ref[:, rows, :]

            def tile(j, st, on_diagonal):
                dq, cpre, ce = st
                cols = pl.ds(pl.multiple_of(j * SB_BLOCK, SB_BLOCK), SB_BLOCK)
                kj, vj = kb[:, cols, :], vb[:, cols, :]
                z = _dot(qi, kj, "nt") * scale
                lb = jnp.minimum(z, 0.0) - jnp.log(1.0 + jnp.exp(-jnp.abs(z)))
                lf = lb - z
                if on_diagonal:
                    mask = (j * SB_BLOCK + ci) < (i * qblk + ri)
                    lf = jnp.where(mask, lf, 0.0)
                tile_sum = jnp.sum(lf, axis=-1, keepdims=True)
                att = jnp.exp(lb + ((ltot - cpre - tile_sum) + _ones_dot(lf, upper)))
                if on_diagonal:
                    att = jnp.where(mask, att, 0.0)
                e = _dot(d_o, vj, "nt") * att
                dlf = ce + _ones_dot(e, below)
                dzz = e - (e + dlf) * jnp.exp(lb)
                if on_diagonal:
                    dzz = jnp.where(mask, dzz, 0.0)
                dzz = dzz.astype(BF16)
                dq = dq + _dot(dzz, kj, "nn")
                dk_scr[:, cols, :] += _dot(dzz, qi, "tn")
                dv_scr[:, cols, :] += _dot(att.astype(BF16), d_o, "tn")
                return dq, cpre + tile_sum, ce + jnp.sum(e, axis=-1, keepdims=True)

            nkb = qblk // SB_BLOCK
            zero_col = jnp.zeros((hp, qblk, 1), F32)
            st = lax.fori_loop(0, i * nkb, lambda j, s: tile(j, s, False),
                               (jnp.zeros((hp, qblk, D_HEAD), F32), zero_col, zero_col))
            for d in range(nkb):
                st = tile(i * nkb + d, st, True)
            dq = st[0]
            for hh in range(hp):
                dq_ref[rows, hh * D_HEAD:(hh + 1) * D_HEAD] = (dq[hh] * scale).astype(BF16)
            return carry

        lax.fori_loop(0, t // qblk, qblock, 0)
        for hh in range(hp):
            hs = slice(hh * D_HEAD, (hh + 1) * D_HEAD)
            dk_ref[:, hs] = (dk_scr[hh] * scale).astype(BF16)
            dv_ref[:, hs] = dv_scr[hh].astype(BF16)

    def head(off):
        return pl.BlockSpec((t, wid), lambda h: (0, off // wid + h))

    extra_specs, extra = [], []
    if after is not None:
        extra_specs, extra = [pl.BlockSpec(after.shape, lambda h: (0, 0))], [after]
    return pl.pallas_call(
        body, name="sb_bwd", grid=(N_HEADS // hp,),
        in_specs=[head(C_SBQ), head(C_SBQ + D_MODEL), head(C_SBQ + 2 * D_MODEL), head(C_SBZ), head(0),
                  pl.BlockSpec((hp, t, 1), lambda h: (h, 0, 0)), head(0)] + extra_specs,
        out_specs=[head(0)] * 4, out_shape=[jax.ShapeDtypeStruct((t, D_MODEL), BF16)] * 4,
        scratch_shapes=[pltpu.VMEM((hp, t, D_HEAD), BF16)] * 4 + [pltpu.VMEM((hp, t, D_HEAD), F32)] * 2,
        compiler_params=_cp(),
    )(proj, proj, proj, proj, o, ltot, dog, *extra)


def _mem_fwd(proj, mkv):
    t = proj.shape[0]
    tq = _pick(t, (512, 256))
    m_len = mkv.shape[0]
    scale = 1.0 / math.sqrt(MEM_DH)

    def body(q_ref, z_ref, kv_ref, o_ref, og_ref):
        q = q_ref[...]
        mk = kv_ref[:, :MEM_W].astype(BF16)
        mv = kv_ref[:, MEM_W:].astype(BF16)
        lane = lax.broadcasted_iota(jnp.int32, q.shape, 1) >> 6
        o = jnp.zeros(q.shape, F32)
        for h in range(MEM_HEADS):
            s = _bdot(jnp.where(lane == h, q, 0.0), mk, "nt") * scale
            p = jnp.exp(s - jnp.max(s, axis=-1, keepdims=True))
            p = p / jnp.sum(p, axis=-1, keepdims=True)
            o = o + jnp.where(lane == h, _bdot(p, mv, "nn"), 0.0)
        z = z_ref[...]
        o_ref[...] = o
        og_ref[...] = (o * (z * _sigmoid(z))).astype(BF16)

    out = pl.BlockSpec((tq, MEM_W), lambda i: (i, 0))
    return pl.pallas_call(
        body, name="mem_fwd", grid=(t // tq,),
        in_specs=[pl.BlockSpec((tq, MEM_W), lambda i: (i, C_MQ // MEM_W)),
                  pl.BlockSpec((tq, MEM_W), lambda i: (i, C_MZ // MEM_W)),
                  pl.BlockSpec((m_len, 2 * MEM_W), lambda i: (0, 0))],
        out_specs=[out, out],
        out_shape=[jax.ShapeDtypeStruct((t, MEM_W), F32), jax.ShapeDtypeStruct((t, MEM_W), BF16)],
        compiler_params=_cp(),
    )(proj, proj, mkv)


def _mem_bwd(proj, mkv, o, dog):
    t = proj.shape[0]
    tq = _pick(t, (512, 256))
    m_len = mkv.shape[0]
    scale = 1.0 / math.sqrt(MEM_DH)

    def body(q_ref, z_ref, kv_ref, o_ref, d_ref, dq_ref, dz_ref, dkv_ref):
        @pl.when(pl.program_id(0) == 0)
        def _():
            dkv_ref[...] = jnp.zeros_like(dkv_ref)

        q = q_ref[...]
        z = z_ref[...]
        sg = _sigmoid(z)
        dgo = d_ref[...]
        d_o = dgo * (z * sg)
        dz_ref[...] = (dgo * o_ref[...] * (sg * (1.0 + z * (1.0 - sg)))).astype(BF16)
        mk = kv_ref[:, :MEM_W].astype(BF16)
        mv = kv_ref[:, MEM_W:].astype(BF16)
        lane = lax.broadcasted_iota(jnp.int32, q.shape, 1) >> 6
        klane = lax.broadcasted_iota(jnp.int32, (m_len, MEM_W), 1) >> 6
        dq = jnp.zeros(q.shape, F32)
        dmk = jnp.zeros((m_len, MEM_W), F32)
        dmv = jnp.zeros((m_len, MEM_W), F32)
        for h in range(MEM_HEADS):
            qh = jnp.where(lane == h, q, 0.0)
            doh = jnp.where(lane == h, d_o, 0.0)
            s = _bdot(qh, mk, "nt") * scale
            p = jnp.exp(s - jnp.max(s, axis=-1, keepdims=True))
            p = p / jnp.sum(p, axis=-1, keepdims=True)
            dp = _bdot(doh, mv, "nt")
            ds = p * (dp - jnp.sum(dp * p, axis=-1, keepdims=True)) * scale
            dq = dq + jnp.where(lane == h, _bdot(ds, mk, "nn"), 0.0)
            dmk = dmk + jnp.where(klane == h, _bdot(ds, qh, "tn"), 0.0)
            dmv = dmv + jnp.where(klane == h, _bdot(p, doh, "tn"), 0.0)
        dq_ref[...] = dq.astype(BF16)
        dkv_ref[:, :MEM_W] += dmk
        dkv_ref[:, MEM_W:] += dmv

    blk = pl.BlockSpec((tq, MEM_W), lambda i: (i, 0))
    kv = pl.BlockSpec((m_len, 2 * MEM_W), lambda i: (0, 0))
    return pl.pallas_call(
        body, name="mem_bwd", grid=(t // tq,),
        in_specs=[pl.BlockSpec((tq, MEM_W), lambda i: (i, C_MQ // MEM_W)),
                  pl.BlockSpec((tq, MEM_W), lambda i: (i, C_MZ // MEM_W)), kv, blk, blk],
        out_specs=[blk, blk, kv],
        out_shape=[jax.ShapeDtypeStruct((t, MEM_W), BF16), jax.ShapeDtypeStruct((t, MEM_W), BF16),
                   jax.ShapeDtypeStruct((m_len, 2 * MEM_W), F32)], compiler_params=_cp(),
    )(proj, proj, mkv, o, dog)


_GW = 512


def _merge_fwd(proj, y_dn, y_sb, y_m):
    t = proj.shape[0]
    tb = _pick(t, (512, 256))
    nc = D_MODEL // _GW

    def body(g1, g2, g3, y1, y2, y3, out_ref):
        out_ref[...] = (_sigmoid(g1[...]) * y1[...] + _sigmoid(g2[...]) * y2[...] + _sigmoid(g3[...]) * y3[...]).astype(BF16)

    def gate(kb):
        return pl.BlockSpec((tb, _GW), lambda i, c: (i, C_GATES // _GW + kb * nc + c))

    blk = pl.BlockSpec((tb, _GW), lambda i, c: (i, c))
    return pl.pallas_call(
        body, name="merge_fwd", grid=(t // tb, nc), in_specs=[gate(0), gate(1), gate(2), blk, blk, blk],
        out_specs=blk, out_shape=jax.ShapeDtypeStruct((t, D_MODEL), BF16), compiler_params=_cp(),
    )(proj, proj, proj, y_dn, y_sb, y_m)


def _merge_bwd(proj, y_dn, y_sb, y_m, dm):
    t = proj.shape[0]
    tb = _pick(t, (512, 256))
    nc = D_MODEL // _GW

    def body(g1, g2, g3, y1, y2, y3, dm_ref, d1, d2, d3, dg1, dg2, dg3):
        d = dm_ref[...]
        for g, y, dy, dg in ((g1, y1, d1, dg1), (g2, y2, d2, dg2), (g3, y3, d3, dg3)):
            s = _sigmoid(g[...])
            dy[...] = (d * s).astype(BF16)
            dg[...] = (d * y[...] * (s * (1.0 - s))).astype(BF16)

    def gate(kb):
        return pl.BlockSpec((tb, _GW), lambda i, c: (i, C_GATES // _GW + kb * nc + c))

    blk = pl.BlockSpec((tb, _GW), lambda i, c: (i, c))
    act = jax.ShapeDtypeStruct((t, D_MODEL), BF16)
    return pl.pallas_call(
        body, name="merge_bwd", grid=(t // tb, nc), in_specs=[gate(0), gate(1), gate(2), blk, blk, blk, blk],
        out_specs=[blk] * 6, out_shape=[act] * 6, compiler_params=_cp(),
    )(proj, proj, proj, y_dn, y_sb, y_m, dm)


def _final_loss(x, mo, g, tgt):
    t, d = x.shape
    tb = _pick(t, (512, 256))

    def body(x_ref, mo_ref, g_ref, t_ref, do_ref, dob_ref, loss_ref, dg_ref):
        @pl.when(pl.program_id(0) == 0)
        def _():
            loss_ref[...] = jnp.zeros_like(loss_ref)
            dg_ref[...] = jnp.zeros_like(dg_ref)

        out = x_ref[...] + mo_ref[...]
        r = lax.rsqrt(jnp.mean(out * out, axis=-1, keepdims=True) + NORM_EPS)
        xhat = out * r
        gv = g_ref[...]
        err = xhat * gv - t_ref[...]
        per_tok = jnp.mean(err * err, axis=-1, keepdims=True)
        loss_ref[...] += 0.5 * jnp.sum(per_tok, axis=0, keepdims=True)
        dy = err * (1.0 / d)
        dg_ref[...] += jnp.sum(dy * xhat, axis=0, keepdims=True)
        dxh = dy * gv
        dout = r * (dxh - xhat * jnp.mean(dxh * xhat, axis=-1, keepdims=True))
        do_ref[...] = dout
        dob_ref[...] = dout.astype(BF16)

    row = pl.BlockSpec((tb, d), lambda i: (i, 0))
    vec = pl.BlockSpec((1, d), lambda i: (0, 0))
    return pl.pallas_call(
        body, name="final_loss", grid=(t // tb,), in_specs=[row, row, vec, row],
        out_specs=[row, row, pl.BlockSpec((1, 128), lambda i: (0, 0)), vec],
        out_shape=[jax.ShapeDtypeStruct((t, d), F32), jax.ShapeDtypeStruct((t, d), BF16),
                   jax.ShapeDtypeStruct((1, 128), F32), jax.ShapeDtypeStruct((1, d), F32)],
        compiler_params=_cp(),
    )(x, mo, g, tgt)


def _cast_bf16(a, name):
    r, c = a.shape
    tb = _pick(r, (128, 496, 240))

    def body(a_ref, o_ref):
        o_ref[...] = a_ref[...].astype(BF16)

    blk = pl.BlockSpec((tb, c), lambda i: (i, 0))
    return pl.pallas_call(body, name=name, grid=(r // tb,), in_specs=[blk], out_specs=blk,
                          out_shape=jax.ShapeDtypeStruct((r, c), BF16), compiler_params=_cp())(a)


WIN_START = (0, 23, 45, 68)
_S1_LO, _S1_HI = 1148, 1164
_S1_BA_POS = SHARD_PAD - 128


def _to_window(x, s):
    if s == 0:
        return x
    if s in (2, 3):
        return pltpu.roll(x, 120 if s == 2 else 124, 1)
    pos = lax.broadcasted_iota(jnp.int32, x.shape, 1)
    head = pltpu.roll(x, 4, 1)
    tail = pltpu.roll(x, SHARD_PAD - 12, 1)
    ba = jnp.where(pos < _S1_BA_POS + (_S1_HI - _S1_LO), pltpu.roll(x, _S1_BA_POS - _S1_LO, 1), 0.0)
    return jnp.where(pos < _S1_LO + 4, head, jnp.where(pos < _S1_BA_POS, tail, ba))


def _from_window(g, s):
    if s == 0:
        return g
    if s in (2, 3):
        return pltpu.roll(g, SHARD_PAD - (120 if s == 2 else 124), 1)
    col = lax.broadcasted_iota(jnp.int32, g.shape, 1)
    head = pltpu.roll(g, SHARD_PAD - 4, 1)
    tail = pltpu.roll(g, 12, 1)
    ba = pltpu.roll(g, SHARD_PAD - (_S1_BA_POS - _S1_LO), 1)
    return jnp.where(col < _S1_LO, head, jnp.where(col < _S1_HI, ba, tail))


def _cast_to_window(w, shard, name):
    r, c = w.shape
    tb = _pick(r, (128,))

    def body(s_ref, w_ref, o_ref, pad_scr):
        pad_scr[...] = jnp.zeros_like(pad_scr)
        pad_scr[:, :c] = w_ref[...]
        x = pad_scr[...]
        for s in range(N_SHARD):
            @pl.when(s_ref[0] == s)
            def _():
                o_ref[...] = _to_window(x, s).astype(BF16)

    return pl.pallas_call(
        body, name=name,
        grid_spec=pltpu.PrefetchScalarGridSpec(
            num_scalar_prefetch=1, grid=(r // tb,),
            in_specs=[pl.BlockSpec((tb, c), lambda i, s: (i, 0))],
            out_specs=pl.BlockSpec((tb, SHARD_PAD), lambda i, s: (i, 0)),
            scratch_shapes=[pltpu.VMEM((tb, SHARD_PAD), F32)]),
        out_shape=jax.ShapeDtypeStruct((r, SHARD_PAD), BF16), compiler_params=_cp(),
    )(shard, w)


def _pair_add(g, recv, c_idx, name):
    n, r, c = g.shape
    half = r // 2
    tb = _pick(half, (128, 240))
    nb = half // tb

    def body(c_ref, g_ref, r_ref, o_ref):
        o_ref[...] = (g_ref[...].astype(F32) + r_ref[...].astype(F32)).astype(BF16)

    blk = pl.BlockSpec((n, tb, c), lambda i, c_ref: (0, i, 0))
    return pl.pallas_call(
        body, name=name,
        grid_spec=pltpu.PrefetchScalarGridSpec(
            num_scalar_prefetch=1, grid=(nb,),
            in_specs=[pl.BlockSpec((n, tb, c), lambda i, c_ref: (0, c_ref[0] * nb + i, 0)), blk], out_specs=blk),
        out_shape=jax.ShapeDtypeStruct((n, half, c), BF16), compiler_params=_cp(),
    )(c_idx, g, recv)


def _chip_sum(parts, by_chip, place, name):
    n, h, c = parts.shape
    tb = _pick(h, (128, 240))
    nb = h // tb

    def body(p_ref, mine_ref, *rest):
        others, o_ref = rest[:n], rest[n]
        me = jnp.zeros((tb, c), jnp.int32) + p_ref[0]
        acc = None
        for q in range(n):
            term = jnp.where(me == q, mine_ref[...], others[q][...]).astype(F32)
            acc = term if acc is None else acc + term
        o_ref[...] = acc

    def other(q):
        return pl.BlockSpec((None, tb, c), lambda i, p: (jnp.where(p[0] == q, (q + 1) % n, q), i, 0))

    return pl.pallas_call(
        body, name=name,
        grid_spec=pltpu.PrefetchScalarGridSpec(
            num_scalar_prefetch=1, grid=(nb,),
            in_specs=[pl.BlockSpec((None, tb, c), lambda i, p: (p[0], i, 0))] + [other(q) for q in range(n)],
            out_specs=pl.BlockSpec((tb, c), lambda i, p: (p[1] * nb + i, 0))),
        out_shape=jax.ShapeDtypeStruct((2 * h, c), F32), compiler_params=_cp(),
    )(place, parts, *([by_chip] * n))


def _adamw_math(w, g, m, v):
    m = ADAM_B1 * m + (1.0 - ADAM_B1) * g
    v = ADAM_B2 * v + (1.0 - ADAM_B2) * (g * g)
    m_hat = m / (1.0 - ADAM_B1 ** ADAM_STEP)
    v_hat = v / (1.0 - ADAM_B2 ** ADAM_STEP)
    delta = -ADAM_LR * (m_hat / (jnp.sqrt(v_hat) + ADAM_EPS) + ADAM_WD * w)
    return delta, m, v


def _adamw(w, g, m, v, name):
    r, c = w.shape
    tb = _pick(r, (128, 496, 240))

    def body(w_ref, g_ref, m_ref, v_ref, go_ref, d_ref, mo_ref, vo_ref):
        gv = g_ref[...]
        d, mn, vn = _adamw_math(w_ref[...], gv, m_ref[...], v_ref[...])
        go_ref[...] = gv
        d_ref[...] = d
        mo_ref[...] = mn
        vo_ref[...] = vn

    blk = pl.BlockSpec((tb, c), lambda i: (i, 0))
    return pl.pallas_call(
        body, name=name, grid=(r // tb,), in_specs=[blk] * 4, out_specs=[blk] * 4,
        out_shape=[jax.ShapeDtypeStruct((r, c), F32)] * 4, compiler_params=_cp(),
    )(w, g, m, v)


def _adamw_window(w, g_win, m, v, shard, name):
    r, c = w.shape
    tb = _pick(r, (128,))

    def body(s_ref, w_ref, g_ref, m_ref, v_ref, go_ref, d_ref, mo_ref, vo_ref, g_scr):
        gw = g_ref[...]
        for s in range(N_SHARD):
            @pl.when(s_ref[0] == s)
            def _():
                g_scr[...] = _from_window(gw, s)

        gv = g_scr[:, :c]
        d, mn, vn = _adamw_math(w_ref[...], gv, m_ref[...], v_ref[...])
        go_ref[...] = gv
        d_ref[...] = d
        mo_ref[...] = mn
        vo_ref[...] = vn

    blk = pl.BlockSpec((tb, c), lambda i, s: (i, 0))
    return pl.pallas_call(
        body, name=name,
        grid_spec=pltpu.PrefetchScalarGridSpec(
            num_scalar_prefetch=1, grid=(r // tb,),
            in_specs=[blk, pl.BlockSpec((tb, SHARD_PAD), lambda i, s: (i, 0)), blk, blk], out_specs=[blk] * 4,
            scratch_shapes=[pltpu.VMEM((tb, SHARD_PAD), F32)]),
        out_shape=[jax.ShapeDtypeStruct((r, c), F32)] * 4, compiler_params=_cp(),
    )(shard, w, g_win, m, v)


def _small_update(gathered, w, m, v):
    def body(p_ref, w_ref, m_ref, v_ref, g_ref, d_ref, mo_ref, vo_ref):
        g = p_ref[0]
        for i in range(1, N_DEV):
            g = g + p_ref[i]
        d, mn, vn = _adamw_math(w_ref[...], g, m_ref[...], v_ref[...])
        g_ref[...] = g
        d_ref[...] = d
        mo_ref[...] = mn
        vo_ref[...] = vn

    full = pl.BlockSpec((S_ROWS, 128), lambda i: (0, 0))
    return pl.pallas_call(
        body, name="small_update", grid=(1,),
        in_specs=[pl.BlockSpec((N_DEV, S_ROWS, 128), lambda i: (0, 0, 0)), full, full, full], out_specs=[full] * 4,
        out_shape=[jax.ShapeDtypeStruct((S_ROWS, 128), F32)] * 4, compiler_params=_cp(),
    )(gathered, w, m, v)


_ANY = pl.BlockSpec(memory_space=pl.ANY)


def _place():
    x, y, c = lax.axis_index("x"), lax.axis_index("y"), lax.axis_index("c")
    chips = [(1 - x, y), (x, 1 - y), (1 - x, 1 - y)]
    return x, y, c, chips


def _pair_reduce_send(grads, tag):
    n = len(grads)

    def body(*refs):
        ins, outs = refs[:n], refs[n:2 * n]
        send_sems, recv_sems = refs[2 * n:]
        x, y, c, _ = _place()
        sibling = (x, y, 1 - c)
        cps = []
        for a in range(n):
            half = ins[a].shape[1] // 2
            theirs = pl.ds(pl.multiple_of((1 - c) * half, 8), half)
            cp = pltpu.make_async_remote_copy(
                src_ref=ins[a].at[:, theirs], dst_ref=outs[a], send_sem=send_sems.at[a], recv_sem=recv_sems.at[a],
                device_id=sibling, device_id_type=MESH)
            cp.start()
            cps.append(cp)
        for cp in cps:
            cp.wait()

    return pl.pallas_call(
        body, name="pair_reduce_send_" + tag, in_specs=[_ANY] * n, out_specs=[_ANY] * n,
        out_shape=[jax.ShapeDtypeStruct((g.shape[0], g.shape[1] // 2, g.shape[2]), g.dtype) for g in grads],
        scratch_shapes=[pltpu.SemaphoreType.DMA((n,)), pltpu.SemaphoreType.DMA((n,))],
        compiler_params=pltpu.CompilerParams(has_side_effects=True),
    )(*grads)


_HBM = pl.BlockSpec(memory_space=pltpu.HBM)
_SEM = pl.BlockSpec(memory_space=pltpu.SEMAPHORE)
_DATAFLOW = pltpu.SideEffectType.DATAFLOW_SIDE_EFFECTING


def _chip_exchange_copies(ins, lands, send_sems, recv_sems):
    x, y, c, chips = _place()
    me = 2 * x + y
    cps = []
    for a in range(len(ins)):
        for j, (qx, qy) in enumerate(chips):
            cps.append(pltpu.make_async_remote_copy(
                src_ref=ins[a].at[2 * qx + qy], dst_ref=lands[a].at[me], send_sem=send_sems.at[3 * a + j],
                recv_sem=recv_sems.at[3 * a + j], device_id=(qx, qy, c), device_id_type=MESH))
    return cps


def _chip_exchange_start(parts, tag):
    n = len(parts)

    def body(*refs):
        ins, lands = refs[:n], refs[n:2 * n]
        send_sems, recv_sems = refs[2 * n:2 * n + 2]
        token = refs[4 * n + 2]
        for cp in _chip_exchange_copies(ins, lands, send_sems, recv_sems):
            cp.start()
        token[...] = jnp.zeros_like(token)

    hbm = [pltpu.HBM(p.shape, p.dtype) for p in parts]
    lands = [pltpu.with_memory_space_constraint(lax.empty(p.shape, p.dtype), pltpu.HBM) for p in parts]
    res = pl.pallas_call(
        body, name="chip_exchange_start_" + tag,
        out_shape=(pltpu.SemaphoreType.DMA((3 * n,)), pltpu.SemaphoreType.DMA((3 * n,)), *hbm, *hbm,
                   jax.ShapeDtypeStruct((8, 128), F32)),
        in_specs=[_HBM] * (2 * n), out_specs=(_SEM, _SEM, *([_HBM] * (2 * n)), pl.BlockSpec(memory_space=pltpu.VMEM)),
        input_output_aliases={a: 2 + a for a in range(2 * n)},
        compiler_params=pltpu.CompilerParams(has_side_effects=_DATAFLOW),
    )(*[pltpu.with_memory_space_constraint(p, pltpu.HBM) for p in parts], *lands)
    return res[0], res[1], res[2:2 + n], res[2 + n:2 + 2 * n], res[2 + 2 * n]


def _chip_exchange_wait(send_sems, recv_sems, parts, lands, after, tag):
    n = len(parts)

    def body(*refs):
        ins, land_refs = refs[:n], refs[n:2 * n]
        s_sems, r_sems = refs[2 * n:2 * n + 2]
        for cp in _chip_exchange_copies(ins, land_refs, s_sems, r_sems):
            cp.wait_send()
            cp.wait_recv()

    hbm = [pltpu.HBM(p.shape, p.dtype) for p in parts]
    res = pl.pallas_call(
        body, name="chip_exchange_wait_" + tag, out_shape=(*hbm, *hbm),
        in_specs=[_HBM] * (2 * n) + [_SEM, _SEM, _ANY], out_specs=tuple([_HBM] * (2 * n)),
        input_output_aliases={a: a for a in range(2 * n)},
        compiler_params=pltpu.CompilerParams(has_side_effects=_DATAFLOW),
    )(*parts, *lands, send_sems, recv_sems, after)
    return res[:n], res[n:]


def _halves_ici_copies(srcs, lands, send_sems, recv_sems):
    x, y, c, chips = _place()
    me = 2 * x + y
    cps = []
    for a, src in enumerate(srcs):
        half = src.shape[0] // 2
        mine = pl.ds(pl.multiple_of(c * half, 16), half)
        for j, (qx, qy) in enumerate(chips):
            cps.append(pltpu.make_async_remote_copy(
                src_ref=src.at[mine], dst_ref=lands[a].at[me, mine], send_sem=send_sems.at[3 * a + j],
                recv_sem=recv_sems.at[3 * a + j], device_id=(qx, qy, c), device_id_type=MESH))
    return cps


def _halves_d2d_copies(lands, send_sems, recv_sems):
    x, y, c, chips = _place()
    cps = []
    for a, land in enumerate(lands):
        half = land.shape[1] // 2
        mine = pl.ds(pl.multiple_of(c * half, 16), half)
        for j, (qx, qy) in enumerate(chips):
            region = land.at[2 * qx + qy, mine]
            cps.append(pltpu.make_async_remote_copy(
                src_ref=region, dst_ref=region, send_sem=send_sems.at[3 * a + j], recv_sem=recv_sems.at[3 * a + j],
                device_id=(x, y, 1 - c), device_id_type=MESH))
    return cps


def _halves_gather_start(shards):
    n = len(shards)

    def body(*refs):
        srcs, lands = refs[:n], refs[n:2 * n]
        send_sems, recv_sems = refs[2 * n:2 * n + 2]
        for cp in _halves_ici_copies(srcs, lands, send_sems, recv_sems):
            cp.start()
        refs[4 * n + 2][...] = jnp.zeros((8, 128), F32)

    hbm_s = [pltpu.HBM(s.shape, s.dtype) for s in shards]
    hbm_l = [pltpu.HBM((N_SHARD,) + s.shape, s.dtype) for s in shards]
    lands = [pltpu.with_memory_space_constraint(lax.empty((N_SHARD,) + s.shape, s.dtype), pltpu.HBM) for s in shards]
    res = pl.pallas_call(
        body, name="halves_gather_start",
        out_shape=(pltpu.SemaphoreType.DMA((3 * n,)), pltpu.SemaphoreType.DMA((3 * n,)), *hbm_s, *hbm_l,
                   jax.ShapeDtypeStruct((8, 128), F32)),
        in_specs=[_HBM] * (2 * n), out_specs=(_SEM, _SEM, *([_HBM] * (2 * n)), pl.BlockSpec(memory_space=pltpu.VMEM)),
        input_output_aliases={a: 2 + a for a in range(2 * n)},
        compiler_params=pltpu.CompilerParams(has_side_effects=_DATAFLOW),
    )(*[pltpu.with_memory_space_constraint(s, pltpu.HBM) for s in shards], *lands)
    return res[0], res[1], res[2:2 + n], res[2 + n:2 + 2 * n], res[2 + 2 * n]


def _halves_gather_forward(send1, recv1, shards, lands, after):
    n = len(shards)

    def body(*refs):
        srcs, land_refs = refs[:n], refs[n:2 * n]
        s1, r1 = refs[2 * n:2 * n + 2]
        outs = refs[2 * n + 2 + len(after):]
        s2, r2 = outs[0], outs[1]
        for cp in _halves_ici_copies(srcs, land_refs, s1, r1):
            cp.wait_send()
            cp.wait_recv()
        for cp in _halves_d2d_copies(land_refs, s2, r2):
            cp.start()

    hbm_s = [pltpu.HBM(s.shape, s.dtype) for s in shards]
    hbm_l = [pltpu.HBM(l.shape, l.dtype) for l in lands]
    res = pl.pallas_call(
        body, name="halves_gather_forward",
        out_shape=(pltpu.SemaphoreType.DMA((3 * n,)), pltpu.SemaphoreType.DMA((3 * n,)), *hbm_s, *hbm_l),
        in_specs=[_HBM] * (2 * n) + [_SEM, _SEM] + [_ANY] * len(after),
        out_specs=(_SEM, _SEM, *([_HBM] * (2 * n))), input_output_aliases={a: 2 + a for a in range(2 * n)},
        compiler_params=pltpu.CompilerParams(has_side_effects=_DATAFLOW),
    )(*shards, *lands, send1, recv1, *after)
    return res[0], res[1], res[2 + n:2 + 2 * n]


def _halves_gather_wait(send2, recv2, lands):
    n = len(lands)

    def body(*refs):
        land_refs = refs[:n]
        s2, r2 = refs[n:n + 2]
        for cp in _halves_d2d_copies(land_refs, s2, r2):
            cp.wait_send()
            cp.wait_recv()

    hbm_l = [pltpu.HBM(l.shape, l.dtype) for l in lands]
    res = pl.pallas_call(
        body, name="halves_gather_wait", out_shape=tuple(hbm_l),
        in_specs=[_HBM] * n + [_SEM, _SEM], out_specs=tuple([_HBM] * n),
        input_output_aliases={a: a for a in range(n)},
        compiler_params=pltpu.CompilerParams(has_side_effects=_DATAFLOW),
    )(*lands, send2, recv2)
    return list(res)


def _shard_gather_copies(src, land, send_sems, recv_sems):
    x, y, c, chips = _place()
    me = 2 * x + y
    return [pltpu.make_async_remote_copy(
        src_ref=src, dst_ref=land.at[me], send_sem=send_sems.at[j], recv_sem=recv_sems.at[j],
        device_id=(qx, qy, c), device_id_type=MESH) for j, (qx, qy) in enumerate(chips)]


def _shard_gather_start(shard_arr, after):
    def body(src, land, after_ref, send_sems, recv_sems, src_thru, land_thru, token):
        for cp in _shard_gather_copies(src, land, send_sems, recv_sems):
            cp.start()
        token[...] = jnp.zeros_like(token)

    land_shape = (N_SHARD,) + shard_arr.shape
    land = pltpu.with_memory_space_constraint(lax.empty(land_shape, shard_arr.dtype), pltpu.HBM)
    return pl.pallas_call(
        body, name="shard_gather_start",
        out_shape=(pltpu.SemaphoreType.DMA((N_SHARD - 1,)), pltpu.SemaphoreType.DMA((N_SHARD - 1,)),
                   pltpu.HBM(shard_arr.shape, shard_arr.dtype), pltpu.HBM(land_shape, shard_arr.dtype),
                   jax.ShapeDtypeStruct((8, 128), F32)),
        in_specs=[_HBM, _HBM, _ANY], out_specs=(_SEM, _SEM, _HBM, _HBM, pl.BlockSpec(memory_space=pltpu.VMEM)),
        input_output_aliases={0: 2, 1: 3},
        compiler_params=pltpu.CompilerParams(has_side_effects=_DATAFLOW),
    )(pltpu.with_memory_space_constraint(shard_arr, pltpu.HBM), land, after)


def _shard_gather_wait(send_sems, recv_sems, shard_arr, land, after):
    def body(src, land_ref, s_sems, r_sems, after_ref, src_out, land_out):
        for cp in _shard_gather_copies(src, land_ref, s_sems, r_sems):
            cp.wait_send()
            cp.wait_recv()

    return pl.pallas_call(
        body, name="shard_gather_wait",
        out_shape=(pltpu.HBM(shard_arr.shape, shard_arr.dtype), pltpu.HBM(land.shape, land.dtype)),
        in_specs=[_HBM, _HBM, _SEM, _SEM, _ANY], out_specs=(_HBM, _HBM), input_output_aliases={0: 0, 1: 1},
        compiler_params=pltpu.CompilerParams(has_side_effects=_DATAFLOW),
    )(shard_arr, land, send_sems, recv_sems, after)


def _pair_allgather(fulls, tag):
    n = len(fulls)

    def body(*refs):
        outs = refs[n:2 * n]
        send_sems, recv_sems = refs[2 * n:]
        x, y, c, _ = _place()
        sibling = (x, y, 1 - c)
        cps = []
        for a in range(n):
            half = outs[a].shape[0] // 2
            mine = outs[a].at[pl.ds(pl.multiple_of(c * half, 8), half)]
            cp = pltpu.make_async_remote_copy(
                src_ref=mine, dst_ref=mine, send_sem=send_sems.at[a], recv_sem=recv_sems.at[a],
                device_id=sibling, device_id_type=MESH)
            cp.start()
            cps.append(cp)
        for a in range(n):
            half = outs[a].shape[0] // 2
            theirs = outs[a].at[pl.ds(pl.multiple_of((1 - c) * half, 8), half)]
            pltpu.make_async_remote_copy(
                src_ref=theirs, dst_ref=theirs, send_sem=send_sems.at[a], recv_sem=recv_sems.at[a],
                device_id=sibling, device_id_type=MESH).wait_recv()
        for cp in cps:
            cp.wait_send()

    return pl.pallas_call(
        body, name="pair_allgather_" + tag, in_specs=[_ANY] * n, out_specs=[_ANY] * n,
        out_shape=[jax.ShapeDtypeStruct(f.shape, f.dtype) for f in fulls],
        input_output_aliases={a: a for a in range(n)},
        scratch_shapes=[pltpu.SemaphoreType.DMA((n,)), pltpu.SemaphoreType.DMA((n,))],
        compiler_params=pltpu.CompilerParams(has_side_effects=True),
    )(*fulls)


def _allgather_small(slab, after):
    def body(s_ref, after_ref, out_ref, send_sems, recv_sems):
        x, y, c, _ = _place()
        me = 4 * x + 2 * y + c
        out_ref[me] = s_ref[...]
        cps = []
        for mask in range(1, N_DEV):
            peer = (x ^ (mask >> 2), y ^ ((mask >> 1) & 1), c ^ (mask & 1))
            cp = pltpu.make_async_remote_copy(
                src_ref=s_ref, dst_ref=out_ref.at[me], send_sem=send_sems.at[mask - 1], recv_sem=recv_sems.at[mask - 1],
                device_id=peer, device_id_type=MESH)
            cp.start()
            cps.append(cp)
        for mask in range(1, N_DEV):
            peer = (x ^ (mask >> 2), y ^ ((mask >> 1) & 1), c ^ (mask & 1))
            dst = out_ref.at[4 * peer[0] + 2 * peer[1] + peer[2]]
            pltpu.make_async_remote_copy(
                src_ref=dst, dst_ref=dst, send_sem=send_sems.at[mask - 1], recv_sem=recv_sems.at[mask - 1],
                device_id=peer, device_id_type=MESH).wait_recv()
        for cp in cps:
            cp.wait_send()

    vm = pl.BlockSpec(memory_space=pltpu.VMEM)
    return pl.pallas_call(
        body, name="allgather_small", in_specs=[vm, _ANY], out_specs=vm,
        out_shape=jax.ShapeDtypeStruct((N_DEV,) + slab.shape, slab.dtype),
        scratch_shapes=[pltpu.SemaphoreType.DMA((N_DEV - 1,)), pltpu.SemaphoreType.DMA((N_DEV - 1,))],
        compiler_params=pltpu.CompilerParams(has_side_effects=True),
    )(slab, after)


def _pack_b(w_mem_kv, w_br_dn, w_br_sb, w_br_mem, w_out):
    return jnp.concatenate([w_mem_kv.reshape(128, D_MODEL), w_br_dn, w_br_sb, w_br_mem.reshape(64, D_MODEL), w_out],
                           axis=0)


def _conv_slab(conv_w):
    return jnp.pad(conv_w.reshape(3, D_MODEL), ((0, 29), (0, 0)))


def _unpack_b(slab):
    return (slab[B_MEMKV:B_BRDN].reshape(1, 256, 512), slab[B_BRDN:B_BRSB].reshape(1, 256, D_MODEL),
            slab[B_BRSB:B_BRMEM].reshape(1, 256, D_MODEL), slab[B_BRMEM:B_OUT].reshape(1, 256, 256),
            slab[B_OUT:B_CONV].reshape(1, 256, D_MODEL))


def _conv_rows(conv_full):
    return conv_full.reshape(4 * CONV_BLOCKS, 128)


def _conv_shard_rows(conv_shard, shard):
    own = CONV_BLOCKS // N_SHARD
    blocks = lax.dynamic_update_slice(jnp.zeros((4, CONV_BLOCKS, 128), F32), conv_shard.reshape(4, own, 128),
                                      (0, own * shard, 0))
    return blocks.reshape(4 * CONV_BLOCKS, 128)


def _conv_shard_of(rows, shard):
    own = CONV_BLOCKS // N_SHARD
    blocks = lax.dynamic_slice(rows.reshape(4, CONV_BLOCKS, 128), (0, own * shard, 0), (4, own, 128))
    return blocks.reshape(1, 4, own * 128)


def _pack_small(norm_g, mem_norm_g, final_g, dn_norm_g, a_log, dt_bias, conv_rows, loss=None):
    def row(v):
        v = v.reshape(1, -1).astype(F32)
        return jnp.pad(v, ((0, 0), (0, 128 - v.shape[1])))

    loss_row = row(jnp.zeros((1,), F32) if loss is None else jnp.reshape(loss, (1,)))
    rid = lax.broadcasted_iota(jnp.int32, (8, 128), 0) + S_DNNORM
    tile = jnp.where(rid == S_DNNORM, dn_norm_g.reshape(1, 128), jnp.where(
        rid == S_ALOG, row(a_log), jnp.where(rid == S_DTB, row(dt_bias), jnp.where(rid == S_LOSS, loss_row, 0.0))))
    return jnp.concatenate([norm_g.reshape(8, 128), mem_norm_g.reshape(8, 128), final_g.reshape(8, 128), tile,
                            conv_rows], axis=0)


def _unpack_small(slab, shard):
    return (slab[S_NORM:S_NORM + 8].reshape(1, D_MODEL), slab[S_MEMNORM:S_MEMNORM + 8].reshape(1, D_MODEL),
            slab[S_FINAL:S_FINAL + 8].reshape(D_MODEL), slab[S_DNNORM].reshape(1, 128),
            slab[S_ALOG, :N_HEADS].reshape(1, N_HEADS), slab[S_DTB, :N_HEADS].reshape(1, N_HEADS),
            _conv_shard_of(slab[S_CONV:], shard))


def _windows_to_w_r(win):
    b = 128
    s0, s1, s2, s3 = win[0], win[1], win[2], win[3]
    e1, e2, e3 = WIN_START[1] * b, WIN_START[2] * b, WIN_START[3] * b
    n1, n2 = e2 - e1, e3 - e2
    return jnp.concatenate([
        s0[:, :e1], s0[:, e1:e1 + b] + s1[:, :b],
        s1[:, b:n1], s1[:, n1:n1 + b] + s2[:, :b],
        s2[:, b:n2], s2[:, n2:n2 + b] + s3[:, :b],
        s3[:, b:], s1[:, _S1_BA_POS:], jnp.zeros((win.shape[1], W_R - C_BA - b), win.dtype)], axis=1)


def _dproj_windows(dproj_r):
    b = 128
    pieces = []
    for s in range(N_SHARD):
        lo = WIN_START[s] * b
        if s == 1:
            pieces += [dproj_r[:, lo:lo + _S1_BA_POS], dproj_r[:, C_BA:C_BA + b]]
        else:
            pieces.append(dproj_r[:, lo:lo + SHARD_PAD])
    return jnp.concatenate(pieces, axis=1)


def _local_step(x, mem, tgt, norm_g, mem_norm_g, w_r, w_sh, conv_w, a_log, dt_bias, dn_norm_g, proj_weights, final_g,
                on_early=None, after_gather=None, h=None):
    t = x.shape[0]
    final_row = final_g.reshape(1, D_MODEL)
    lanes_8_16 = ((0, 0), (N_HEADS, 128 - 2 * N_HEADS))
    alog_row = jnp.pad(a_log.reshape(1, N_HEADS), lanes_8_16)
    dtb_row = jnp.pad(dt_bias.reshape(1, N_HEADS), lanes_8_16)

    if h is None:
        h = _rmsnorm_fwd(x, norm_g, "norm_fwd")
    proj = _mm(h, w_r, "nn", "in_proj", after=after_gather, tm_max=2048)
    qkv = _dn_prep_fwd(proj, conv_w)
    beta_t, g_t = _dn_gate_fwd(proj, alog_row, dtb_row)
    dn_u, dn_w, dn_qd, dn_kd, dn_a, tinv_all, dn_el = _dn_intra_fwd(qkv, beta_t, g_t)
    o_dn, dn_vn, s_all = _dn_scan_fwd(dn_u, dn_w, dn_qd, dn_kd, dn_a, dn_el)
    o_dn_g = _dn_post_fwd(o_dn, proj, dn_norm_g)
    o_sb, o_sb_g, sb_l = _sb_fwd(proj)
    w_mem_kv, w_br_dn, w_br_sb, w_br_mem, w_out = proj_weights(o_sb_g)
    mem_n = _rmsnorm_fwd(mem, mem_norm_g, "mem_norm_fwd")
    mkv = _mm(mem_n, w_mem_kv, "nn", "mem_kv")
    o_m, o_m_g = _mem_fwd(proj, mkv)
    y_dn = _mm(o_dn_g, w_br_dn, "nn", "br_dn", out_dtype=BF16)
    y_sb = _mm(o_sb_g, w_br_sb, "nn", "br_sb", out_dtype=BF16)
    y_m = _mm(o_m_g, w_br_mem, "nn", "br_mem", out_dtype=BF16)
    merged = _merge_fwd(proj, y_dn, y_sb, y_m)
    mo = _mm(merged, w_out, "nn", "out_proj")
    d_out, d_out_b, loss_row, g_final = _final_loss(x, mo, final_row, tgt)

    g_w_out = _mm(merged, d_out_b, "tn", "g_w_out", out_dtype=BF16)
    d_merged = _mm(d_out_b, w_out, "nt", "d_merged")
    dy_dn, dy_sb, dy_m, dg1, dg2, dg3 = _merge_bwd(proj, y_dn, y_sb, y_m, d_merged)
    g_w_br_dn = _mm(o_dn_g, dy_dn, "tn", "g_w_br_dn", out_dtype=BF16)
    g_w_br_sb = _mm(o_sb_g, dy_sb, "tn", "g_w_br_sb", out_dtype=BF16)
    g_w_br_mem = _mm(o_m_g, dy_m, "tn", "g_w_br_mem", out_dtype=BF16)
    d_o_dn_g = _mm(dy_dn, w_br_dn, "nt", "d_o_dn")
    d_o_sb_g = _mm(dy_sb, w_br_sb, "nt", "d_o_sb")
    d_o_m_g = _mm(dy_m, w_br_mem, "nt", "d_o_mem")

    d_mq, d_mz, d_mkv = _mem_bwd(proj, mkv, o_m, d_o_m_g)
    d_mkv_b = _cast_bf16(d_mkv, "cast_dmkv")
    g_w_mem_kv = _mm(mem_n, d_mkv_b, "tn", "g_w_mem_kv", out_dtype=BF16)
    d_mem_n = _mm(d_mkv_b, w_mem_kv, "nt", "d_mem_n")
    _, g_mem_norm = _rmsnorm_bwd(mem, mem_norm_g, d_mem_n, jnp.zeros_like(mem), "mem_norm_bwd")

    early = dict(w_mem_kv=g_w_mem_kv, w_br_dn=g_w_br_dn, w_br_sb=g_w_br_sb, w_br_mem=g_w_br_mem, w_out=g_w_out)
    after_early = on_early(early) if on_early is not None else None

    d_sq, d_sk, d_sv, d_sz = _sb_bwd(proj, o_sb, sb_l, d_o_sb_g, after=after_early)

    d_o_dn, d_dnz, g_dn_norm = _dn_post_bwd(o_dn, proj, dn_norm_g, d_o_dn_g)
    d_vnew, d_kd, d_qd, d_w, d_el = _dn_scan_bwd(dn_w, dn_qd, dn_kd, dn_a, dn_el, dn_vn, s_all, d_o_dn)
    d_qn, d_kn, d_vn, dbeta_t, dg_t = _dn_intra_bwd(qkv, beta_t, g_t, tinv_all, dn_vn, d_o_dn, d_vnew, d_kd, d_qd, d_w, d_el)
    d_conv_in, g_conv = _dn_prep_bwd(proj, conv_w, d_qn, d_kn, d_vn)
    d_ba, g_alog_row, g_dtb_row = _dn_gate_bwd(proj, alog_row, dtb_row, dbeta_t, dg_t)

    dproj_sh = _dproj_windows(
        jnp.concatenate([d_conv_in, d_dnz, d_sq, d_sk, d_sv, d_sz, d_mq, d_mz, dg1, dg2, dg3, d_ba], axis=1))
    g_w_sh = _mm(h, dproj_sh, "tn", "g_w_in", out_dtype=BF16, out_shards=N_SHARD, tn_max=1024)
    def input_grad(after=None):
        dh = _mm(dproj_sh, w_sh, "nt", "d_h", after=after, tm_max=2048, tn_max=1024)
        grad_x, g_norm = _rmsnorm_bwd(x, norm_g, dh, d_out, "norm_bwd")
        small = dict(norm_g=g_norm, mem_norm_g=g_mem_norm, final_g=g_final, dn_norm_g=g_dn_norm,
                     a_log=g_alog_row[:, N_HEADS:2 * N_HEADS], dt_bias=g_dtb_row[:, N_HEADS:2 * N_HEADS],
                     conv_w=g_conv)
        return grad_x, small

    return loss_row[0, 0], early, g_w_sh, input_grad


def _reduce_scatter_start(grads, tag):
    c = lax.axis_index("c")
    core = jnp.reshape(c, (1,)).astype(jnp.int32)
    recv = _pair_reduce_send(grads, tag)
    parts = [_pair_add(g, r, core, "pair_add_" + tag) for g, r in zip(grads, recv)]
    return _chip_exchange_start(parts, tag)


def _reduce_scatter_finish(handle, after, tag):
    send_sems, recv_sems, parts, lands, _ = handle
    x, y, c = lax.axis_index("x"), lax.axis_index("y"), lax.axis_index("c")
    place = jnp.stack([2 * x + y, c]).astype(jnp.int32)
    parts, by_chip = _chip_exchange_wait(send_sems, recv_sems, parts, lands, after, tag)
    fulls = [_chip_sum(p, b, place, "chip_sum_" + tag) for p, b in zip(parts, by_chip)]
    return _pair_allgather(fulls, tag)


def kernel(x, mem, norm_g, mem_norm_g, w_in, conv_w, a_log, dt_bias, dn_norm_g, w_mem_kv, w_br_dn, w_br_sb, w_br_mem, w_out, final_g, loss_target, m_norm_g, m_mem_norm_g, m_w_in, m_conv_w, m_a_log, m_dt_bias, m_dn_norm_g, m_w_mem_kv, m_w_br_dn, m_w_br_sb, m_w_br_mem, m_w_out, m_final_g, v_norm_g, v_mem_norm_g, v_w_in, v_conv_w, v_a_log, v_dt_bias, v_dn_norm_g, v_w_mem_kv, v_w_br_dn, v_w_br_sb, v_w_br_mem, v_w_out, v_final_g):
    w_a = w_in[0]
    w_b = _pack_b(w_mem_kv[0], w_br_dn[0], w_br_sb[0], w_br_mem[0], w_out[0])
    m_b = _pack_b(m_w_mem_kv[0], m_w_br_dn[0], m_w_br_sb[0], m_w_br_mem[0], m_w_out[0])
    v_b = _pack_b(v_w_mem_kv[0], v_w_br_dn[0], v_w_br_sb[0], v_w_br_mem[0], v_w_out[0])

    shard_idx = 2 * lax.axis_index("x") + lax.axis_index("y")
    shard = jnp.reshape(shard_idx, (1,)).astype(jnp.int32)
    own = [_cast_to_window(w_a, shard, "cast_w_in"), _cast_bf16(_conv_slab(conv_w[0]), "cast_conv")]
    send1, recv1, own, lands, token = _halves_gather_start(own)
    h = _rmsnorm_fwd(x[0], norm_g, "norm_fwd", after=token)
    w_b_bf = _cast_bf16(w_b, "cast_w_b")
    send2, recv2, lands = _halves_gather_forward(send1, recv1, own, lands,
                                                 after=[h, w_b_bf, m_w_in[0], v_w_in[0], m_b, v_b])
    lands = _halves_gather_wait(send2, recv2, lands)
    ga, g_conv = [lax.dynamic_update_slice(land, o[None], (shard_idx, 0, 0)) for land, o in zip(lands, own)]
    w_r = _windows_to_w_r(ga)
    f_conv = g_conv[:, :3].reshape(N_SHARD, 4, 768).transpose(1, 0, 2).reshape(4, 3 * D_MODEL).astype(F32)
    b_flight = _shard_gather_start(w_b_bf, after=ga)

    def proj_weights(after):
        own, land = _shard_gather_wait(b_flight[0], b_flight[1], b_flight[2], b_flight[3], after)
        gb = lax.dynamic_update_slice(land, own[None], (shard_idx, 0, 0))
        return (gb[:, B_MEMKV:B_BRDN].reshape(N_SHARD * 256, 512),
                gb[:, B_BRDN:B_BRSB].reshape(N_SHARD * 256, D_MODEL),
                gb[:, B_BRSB:B_BRMEM].reshape(N_SHARD * 256, D_MODEL),
                gb[:, B_BRMEM:B_OUT].reshape(N_SHARD, 256, 256).transpose(1, 0, 2).reshape(256, D_MODEL),
                gb[:, B_OUT:B_CONV].reshape(N_SHARD * 256, D_MODEL))

    flights = {}

    def on_early(grads):
        g_b = jnp.concatenate([
            grads["w_mem_kv"].reshape(N_SHARD, 128, D_MODEL), grads["w_br_dn"].reshape(N_SHARD, 256, D_MODEL),
            grads["w_br_sb"].reshape(N_SHARD, 256, D_MODEL),
            grads["w_br_mem"].reshape(256, N_SHARD, 256).transpose(1, 0, 2).reshape(N_SHARD, 64, D_MODEL),
            grads["w_out"].reshape(N_SHARD, 256, D_MODEL)], axis=1).astype(BF16)
        flights["b"] = _reduce_scatter_start([g_b], "b")
        return flights["b"][4]

    loss, _, g_w_sh, input_grad = _local_step(
        x[0], mem[0], loss_target[0], norm_g, mem_norm_g, w_r, ga, f_conv, a_log, dt_bias, dn_norm_g,
        proj_weights, final_g, on_early=on_early, after_gather=b_flight[4], h=h)
    flights["a"] = _reduce_scatter_start([g_w_sh], "a")
    grad_x, small = input_grad(after=flights["a"][4])

    part = _pack_small(small["norm_g"], small["mem_norm_g"], small["final_g"], small["dn_norm_g"],
                       small["a_log"], small["dt_bias"], _conv_rows(small["conv_w"]), loss)
    w_s = _pack_small(norm_g, mem_norm_g, final_g, dn_norm_g, a_log, dt_bias, _conv_shard_rows(conv_w[0], shard_idx))
    m_s = _pack_small(m_norm_g, m_mem_norm_g, m_final_g, m_dn_norm_g, m_a_log, m_dt_bias,
                      _conv_shard_rows(m_conv_w[0], shard_idx))
    v_s = _pack_small(v_norm_g, v_mem_norm_g, v_final_g, v_dn_norm_g, v_a_log, v_dt_bias,
                      _conv_shard_rows(v_conv_w[0], shard_idx))
    (gs_b,) = _reduce_scatter_finish(flights["b"], after=grad_x, tag="b")
    gr_b, d_b, nm_b, nv_b = _adamw(w_b, gs_b, m_b, v_b, "adamw_b")
    g_s, d_s, nm_s, nv_s = _small_update(_allgather_small(part, after=d_b), w_s, m_s, v_s)

    (gs_in,) = _reduce_scatter_finish(flights["a"], after=g_s, tag="a")
    gr_in, d_in, nm_in, nv_in = _adamw_window(w_a, gs_in, m_w_in[0], v_w_in[0], shard, "adamw_w_in")

    def assemble(slab_small, a_in, slab_b):
        s_norm, s_memnorm, s_final, s_dnnorm, s_alog, s_dtb, b_conv = _unpack_small(slab_small, shard_idx)
        b_memkv, b_brdn, b_brsb, b_brmem, b_out = _unpack_b(slab_b)
        return [s_norm, s_memnorm, a_in.reshape(1, D_MODEL, IN_WIDTH // N_SHARD), b_conv, s_alog, s_dtb, s_dnnorm,
                b_memkv, b_brdn, b_brsb, b_brmem, b_out, s_final]

    outs = [g_s[S_LOSS, 0], grad_x.reshape(1, -1, D_MODEL)]
    outs += assemble(g_s, gr_in, gr_b)
    outs += assemble(d_s, d_in, d_b)
    outs += assemble(nm_s, nm_in, nm_b)
    outs += assemble(nv_s, nv_in, nv_b)
    return tuple(outs)
```

```python
import math

import jax
import jax.numpy as jnp
from jax import lax
from jax.experimental import pallas as pl
from jax.experimental.pallas import tpu as pltpu

F32 = jnp.float32
BF16 = jnp.bfloat16
MESH = pl.DeviceIdType.MESH

D_MODEL = 1024
N_HEADS = 8
D_HEAD = 128
DN_CHUNK = 64
DN_GROUP = 32
DN_SCAN_GROUP = 8
SB_BLOCK = 256
SB_HEADS_PER_STEP = 2
SB_QBLOCK = 256
MEM_HEADS = 4
MEM_DH = 64
MEM_W = MEM_HEADS * MEM_DH
NORM_EPS = 1e-6
IN_WIDTH = 11792
N_SHARD = 4
SHARD_W = IN_WIDTH // N_SHARD
SHARD_PAD = 3072
N_DEV = 8

C_DNZ = 3072
C_SBQ = 4096
C_SBZ = 7168
C_MQ = 8192
C_MZ = 8448
C_GATES = 8704
C_BA = 11776
W_R = 12288

ADAM_LR = 0.001
ADAM_B1 = 0.9
ADAM_B2 = 0.999
ADAM_EPS = 1e-08
ADAM_WD = 0.01
ADAM_STEP = 10

VMEM_LIMIT = 56 * 1024 * 1024

B_MEMKV, B_BRDN, B_BRSB, B_BRMEM, B_OUT, B_CONV = 0, 128, 384, 640, 704, 960
S_NORM, S_MEMNORM, S_FINAL, S_DNNORM, S_ALOG, S_DTB, S_LOSS, S_CONV, S_ROWS = 0, 8, 16, 24, 25, 26, 27, 32, 128
CONV_BLOCKS = 3 * D_MODEL // 128


def _cp(**kw):
    return pltpu.CompilerParams(vmem_limit_bytes=VMEM_LIMIT, **kw)


def _dot(a, b, dims):
    lead = a.ndim - 2
    ca, cb = {"nn": (1, 0), "nt": (1, 1), "tn": (0, 0)}[dims]
    batch = tuple(range(lead))
    return lax.dot_general(a, b, (((ca + lead,), (cb + lead,)), (batch, batch)), preferred_element_type=F32)


def _chunks(x):
    return x.reshape(x.shape[0] // DN_CHUNK, DN_CHUNK, x.shape[1])


def _unchunk(x):
    return x.reshape(x.shape[0] * x.shape[1], x.shape[2])


def _bdot(a, b, dims):
    return _dot(a.astype(BF16), b.astype(BF16), dims)


def _split(a):
    hi = a.astype(BF16)
    return hi, (a - hi.astype(F32)).astype(BF16)


def _dot3(a, b, dims):
    a1, a2 = _split(a)
    b1, b2 = _split(b)
    return _dot(a1, b1, dims) + (_dot(a1, b2, dims) + _dot(a2, b1, dims))


def _ones_dot(a, ones_bf16):
    out = _dot(a.reshape(-1, a.shape[-1]).astype(BF16), ones_bf16, "nn")
    return out.reshape(a.shape[:-1] + (ones_bf16.shape[1],))


def _sigmoid(x):
    return 1.0 / (1.0 + jnp.exp(-x))


def _log1p_small(u):
    return jnp.where(u < 1e-2, u * (1.0 - u * (0.5 - u * (1.0 / 3.0))), jnp.log(1.0 + u))


def _pick(dim, cands):
    for c in cands:
        if dim % c == 0:
            return c
    return dim


def _mm(a, b, dims, name, out_dtype=F32, out_shards=1, after=None, tm_max=1024, tn_max=512):
    ta, tb = dims[0] == "t", dims[1] == "t"
    m, k = (a.shape[1], a.shape[0]) if ta else a.shape
    b_shards = b.shape[0] if b.ndim == 3 else 1
    n = b.shape[-2] if tb else b.shape[-1]
    tm = _pick(m, (tm_max, 1024, 512, 256))
    tn = _pick(n // out_shards, (tn_max, 512, 384, 256, 128))
    tk = _pick(k // b_shards, (2048, 1024, 512, 384, 256))
    nk = k // tk

    def body(a_ref, b_ref, *rest):
        if nk == 1:
            rest[-1][...] = _bdot(a_ref[...], b_ref[...], dims).astype(out_dtype)
            return
        o_ref, acc_ref = rest[-2:]
        kk = pl.program_id(2)

        @pl.when(kk == 0)
        def _():
            acc_ref[...] = jnp.zeros_like(acc_ref)

        acc_ref[...] += _bdot(a_ref[...], b_ref[...], dims)

        @pl.when(kk == nk - 1)
        def _():
            o_ref[...] = acc_ref[...].astype(out_dtype)

    a_spec = pl.BlockSpec((tk, tm), lambda i, j, q: (q, i)) if ta else pl.BlockSpec((tm, tk), lambda i, j, q: (i, q))
    if b_shards > 1:
        per_k = k // b_shards // tk
        b_spec = pl.BlockSpec((None, tn, tk), lambda i, j, q: (q // per_k, j, q % per_k))
    else:
        b_spec = pl.BlockSpec((tn, tk), lambda i, j, q: (j, q)) if tb else pl.BlockSpec((tk, tn), lambda i, j, q: (q, j))
    if out_shards > 1:
        per_n = n // out_shards // tn
        out_spec = pl.BlockSpec((None, tm, tn), lambda i, j, q: (j // per_n, i, j % per_n))
        out_shape = jax.ShapeDtypeStruct((out_shards, m, n // out_shards), out_dtype)
    else:
        out_spec = pl.BlockSpec((tm, tn), lambda i, j, q: (i, j))
        out_shape = jax.ShapeDtypeStruct((m, n), out_dtype)
    extra_specs, extra = [], []
    if after is not None:
        extra_specs, extra = [pl.BlockSpec(after.shape, lambda i, j, q: (0, 0))], [after]
    return pl.pallas_call(
        body, name=name, grid=(m // tm, n // tn, nk),
        in_specs=[a_spec, b_spec] + extra_specs, out_specs=out_spec, out_shape=out_shape,
        scratch_shapes=[pltpu.VMEM((tm, tn), F32)] if nk > 1 else [],
        compiler_params=_cp(dimension_semantics=("parallel", "parallel", "arbitrary")),
    )(a, b, *extra)


def _rmsnorm_fwd(x, g, name, after=None):
    t, d = x.shape
    tb = _pick(t, (512, 256))

    def body(x_ref, g_ref, *rest):
        xv = x_ref[...]
        r = lax.rsqrt(jnp.mean(xv * xv, axis=-1, keepdims=True) + NORM_EPS)
        rest[-1][...] = ((xv * r) * g_ref[...]).astype(BF16)

    extra_specs, extra = [], []
    if after is not None:
        extra_specs, extra = [pl.BlockSpec(after.shape, lambda i: (0, 0))], [after]
    return pl.pallas_call(
        body, name=name, grid=(t // tb,),
        in_specs=[pl.BlockSpec((tb, d), lambda i: (i, 0)), pl.BlockSpec((1, d), lambda i: (0, 0))] + extra_specs,
        out_specs=pl.BlockSpec((tb, d), lambda i: (i, 0)),
        out_shape=jax.ShapeDtypeStruct((t, d), BF16), compiler_params=_cp(),
    )(x, g, *extra)


def _rmsnorm_bwd(x, g, dh, resid, name):
    t, d = x.shape
    tb = _pick(t, (512, 256))

    def body(x_ref, g_ref, dh_ref, r_ref, dx_ref, dg_ref):
        @pl.when(pl.program_id(0) == 0)
        def _():
            dg_ref[...] = jnp.zeros_like(dg_ref)

        xv = x_ref[...]
        r = lax.rsqrt(jnp.mean(xv * xv, axis=-1, keepdims=True) + NORM_EPS)
        xhat = xv * r
        dhv = dh_ref[...]
        dg_ref[...] += jnp.sum(dhv * xhat, axis=0, keepdims=True)
        dxh = dhv * g_ref[...]
        dx_ref[...] = r_ref[...] + r * (dxh - xhat * jnp.mean(dxh * xhat, axis=-1, keepdims=True))

    row = pl.BlockSpec((tb, d), lambda i: (i, 0))
    vec = pl.BlockSpec((1, d), lambda i: (0, 0))
    return pl.pallas_call(
        body, name=name, grid=(t // tb,), in_specs=[row, vec, row, row], out_specs=[row, vec],
        out_shape=[jax.ShapeDtypeStruct((t, d), F32), jax.ShapeDtypeStruct((1, d), F32)], compiler_params=_cp(),
    )(x, g, dh, resid)


def _conv_silu(xv, w, row):
    y = xv * w[3:4, :]
    for s in (1, 2, 3):
        xs = jnp.where(row >= s, pltpu.roll(xv, s, 0), 0.0)
        y = y + xs * w[3 - s:4 - s, :]
    sig = _sigmoid(y)
    return y, sig, y * sig


def _dn_prep_fwd(proj, conv_w):
    t = proj.shape[0]

    def body(p_ref, w_ref, o_ref):
        j = pl.program_id(0)
        xv = p_ref[...]
        row = lax.broadcasted_iota(jnp.int32, xv.shape, 0)
        _, _, a = _conv_silu(xv, w_ref[...], row)
        inv = lax.rsqrt(jnp.sum(a * a, axis=-1, keepdims=True) + NORM_EPS)
        scale = jnp.where(j < N_HEADS, D_HEAD ** -0.5, 1.0)
        normed = jnp.where(j < 2 * N_HEADS, 1.0, 0.0)
        o_ref[...] = a * (normed * (inv * scale) + (1.0 - normed))

    return pl.pallas_call(
        body, name="dn_prep_fwd", grid=(3 * N_HEADS,),
        in_specs=[pl.BlockSpec((t, D_HEAD), lambda j: (0, j)), pl.BlockSpec((4, D_HEAD), lambda j: (0, j))],
        out_specs=pl.BlockSpec((t, D_HEAD), lambda j: (0, j)),
        out_shape=jax.ShapeDtypeStruct((t, 3 * D_MODEL), F32), compiler_params=_cp(),
    )(proj, conv_w)


def _dn_prep_bwd(proj, conv_w, dq, dk, dv):
    t = proj.shape[0]

    def body(p_ref, w_ref, dq_ref, dk_ref, dv_ref, dp_ref, dw_ref):
        j = pl.program_id(0)
        xv = p_ref[...]
        w = w_ref[...]
        row = lax.broadcasted_iota(jnp.int32, xv.shape, 0)
        y, s, a = _conv_silu(xv, w, row)
        part = jnp.zeros(xv.shape, jnp.int32) + j // N_HEADS
        dn = jnp.where(part == 0, dq_ref[...], jnp.where(part == 1, dk_ref[...], dv_ref[...]))
        inv = lax.rsqrt(jnp.sum(a * a, axis=-1, keepdims=True) + NORM_EPS)
        scale = jnp.where(j < N_HEADS, D_HEAD ** -0.5, 1.0)
        ds = dn * scale
        da_norm = inv * ds - a * (inv * inv * inv) * jnp.sum(ds * a, axis=-1, keepdims=True)
        normed = jnp.where(j < 2 * N_HEADS, 1.0, 0.0)
        da = normed * da_norm + (1.0 - normed) * dn
        dy = da * (s * (1.0 + y * (1.0 - s)))
        dx = dy * w[3:4, :]
        dw_ref[3:4, :] = jnp.sum(dy * xv, axis=0, keepdims=True)
        for sft in (1, 2, 3):
            xs = jnp.where(row >= sft, pltpu.roll(xv, sft, 0), 0.0)
            dw_ref[3 - sft:4 - sft, :] = jnp.sum(dy * xs, axis=0, keepdims=True)
            dys = jnp.where(row < t - sft, pltpu.roll(dy, t - sft, 0), 0.0)
            dx = dx + dys * w[3 - sft:4 - sft, :]
        dp_ref[...] = dx.astype(BF16)

    blk = pl.BlockSpec((t, D_HEAD), lambda j: (0, j))
    wblk = pl.BlockSpec((4, D_HEAD), lambda j: (0, j))

    def grad(part):
        return pl.BlockSpec((t, D_HEAD), lambda j: (0, jnp.clip(j - part * N_HEADS, 0, N_HEADS - 1)))

    return pl.pallas_call(
        body, name="dn_prep_bwd", grid=(3 * N_HEADS,), in_specs=[blk, wblk, grad(0), grad(1), grad(2)],
        out_specs=[blk, wblk],
        out_shape=[jax.ShapeDtypeStruct((t, 3 * D_MODEL), BF16), jax.ShapeDtypeStruct((4, 3 * D_MODEL), F32)],
        compiler_params=_cp(),
    )(proj, conv_w, dq, dk, dv)


def _softplus_parts(xv):
    e = jnp.exp(-jnp.abs(xv))
    return jnp.maximum(xv, 0.0) + _log1p_small(e)


def _chunk_scan(v, row, reverse):
    t = v.shape[0]
    pos = row & (DN_CHUNK - 1)
    s = 1
    while s < DN_CHUNK:
        if reverse:
            v = v + jnp.where(pos < DN_CHUNK - s, pltpu.roll(v, t - s, 0), 0.0)
        else:
            v = v + jnp.where(pos >= s, pltpu.roll(v, s, 0), 0.0)
        s *= 2
    return v


def _dn_gate_fwd(proj, alog_row, dtb_row):
    t = proj.shape[0]

    def body(p_ref, al_ref, dt_ref, b_ref, g_ref):
        p = p_ref[...]
        row = lax.broadcasted_iota(jnp.int32, p.shape, 0)
        b_ref[...] = _sigmoid(p)
        g = -jnp.exp(al_ref[...]) * _softplus_parts(p + dt_ref[...])
        g_ref[...] = _chunk_scan(g, row, reverse=False)

    blk = pl.BlockSpec((t, 128), lambda i: (0, C_BA // 128))
    vec = pl.BlockSpec((1, 128), lambda i: (0, 0))
    out = pl.BlockSpec((t, 128), lambda i: (0, 0))
    return pl.pallas_call(
        body, name="dn_gate_fwd", grid=(1,), in_specs=[blk, vec, vec], out_specs=[out, out],
        out_shape=[jax.ShapeDtypeStruct((t, 128), F32)] * 2, compiler_params=_cp(),
    )(proj, alog_row, dtb_row)


def _dn_gate_bwd(proj, alog_row, dtb_row, dbeta, dgc):
    t = proj.shape[0]

    def body(p_ref, al_ref, dt_ref, db_ref, dg_ref, dp_ref, dal_ref, ddt_ref):
        p = p_ref[...]
        row = lax.broadcasted_iota(jnp.int32, p.shape, 0)
        lane = lax.broadcasted_iota(jnp.int32, p.shape, 1)
        s = _sigmoid(p)
        d_b = db_ref[...] * s * (1.0 - s)
        dg = _chunk_scan(dg_ref[...], row, reverse=True)
        xa = p + dt_ref[...]
        ea = jnp.exp(al_ref[...])
        g = -ea * _softplus_parts(xa)
        d_a = dg * (-ea) * _sigmoid(xa)
        dp_ref[...] = jnp.where(lane < N_HEADS, d_b, jnp.where(lane < 2 * N_HEADS, d_a, 0.0)).astype(BF16)
        dal_ref[...] = jnp.sum(dg * g, axis=0, keepdims=True)
        ddt_ref[...] = jnp.sum(d_a, axis=0, keepdims=True)

    blk = pl.BlockSpec((t, 128), lambda i: (0, C_BA // 128))
    vec = pl.BlockSpec((1, 128), lambda i: (0, 0))
    full = pl.BlockSpec((t, 128), lambda i: (0, 0))
    return pl.pallas_call(
        body, name="dn_gate_bwd", grid=(1,), in_specs=[blk, vec, vec, full, full], out_specs=[full, vec, vec],
        out_shape=[jax.ShapeDtypeStruct((t, 128), BF16), jax.ShapeDtypeStruct((1, 128), F32),
                   jax.ShapeDtypeStruct((1, 128), F32)], compiler_params=_cp(),
    )(proj, alog_row, dtb_row, dbeta, dgc)


def _col_to_row(col, eye):
    return jnp.sum(jnp.where(eye, col, 0.0), axis=-2, keepdims=True)


def _row_to_col(rowv, eye):
    return jnp.sum(jnp.where(eye, rowv, 0.0), axis=-1, keepdims=True)


def _tri_inverse(m, ri, ci):
    eye = (ri == ci).astype(F32)
    b16 = (ri >> 4) == (ci >> 4)
    b32 = (ri >> 5) == (ci >> 5)
    m1 = jnp.where(b16, m, 0.0)
    x = eye - m1
    p = _dot3(m1, m1, "nn")
    x = x + _dot3(x, p, "nn")
    p = _dot3(p, p, "nn")
    x = x + _dot3(x, p, "nn")
    p = _dot3(p, p, "nn")
    x = x + _dot3(x, p, "nn")
    c1 = jnp.where(jnp.logical_and(b32, jnp.logical_not(b16)), m, 0.0)
    x = x - _dot3(_dot3(x, c1, "nn"), x, "nn")
    c2 = jnp.where(b32, 0.0, m)
    x = x - _dot3(_dot3(x, c2, "nn"), x, "nn")
    return x


def _dn_chunk_common(q, k, gc, ri, ci):
    eye = ri == ci
    g_row = _col_to_row(gc, eye)
    diff = jnp.minimum(gc - g_row, 0.0)
    gam = jnp.where(ri >= ci, jnp.exp(diff), 0.0)
    kk = _bdot(k, k, "nt")
    qk = _bdot(q, k, "nt")
    rcol = lax.broadcasted_iota(jnp.int32, gc.shape, gc.ndim - 2)
    last = jnp.sum(jnp.where(rcol == DN_CHUNK - 1, gc, 0.0), axis=-2, keepdims=True)
    e_g = jnp.exp(gc)
    dec = jnp.exp(last - gc)
    return eye, gam, kk, qk, last, e_g, dec, rcol


def _dn_specs(t, rows_blk):
    def head(off):
        return pl.BlockSpec((rows_blk, D_HEAD), lambda g, h: (g, off + h))

    lanes = pl.BlockSpec((rows_blk, 128), lambda g, h: (g, 0))
    hm = pl.BlockSpec((None, rows_blk, D_HEAD), lambda g, h: (h, g, 0))
    sq = pl.BlockSpec((1, rows_blk, DN_CHUNK), lambda g, h: (h, g, 0))
    tile = pl.BlockSpec((1, rows_blk // DN_CHUNK, 8, 128), lambda g, h: (h, g, 0, 0))
    return head, lanes, hm, sq, tile


def _head_column(slab, lane_idx):
    lane = lax.broadcasted_iota(jnp.int32, slab.shape, 1)
    return _chunks(jnp.sum(jnp.where(lane == lane_idx, slab, 0.0), axis=1, keepdims=True))


def _dn_intra_fwd(qkv, beta_t, g_t):
    t = qkv.shape[0]
    n_chunks = t // DN_CHUNK
    rows_blk = min(DN_GROUP * DN_CHUNK, t)

    def body(q_ref, k_ref, v_ref, b_ref, g_ref, u_ref, w_ref, qd_ref, kd_ref, a_ref, ti_ref, el_ref):
        ri = lax.broadcasted_iota(jnp.int32, (DN_CHUNK, DN_CHUNK), 0)
        ci = lax.broadcasted_iota(jnp.int32, (DN_CHUNK, DN_CHUNK), 1)
        h = pl.program_id(1)
        q, k, v = (_chunks(r[...]) for r in (q_ref, k_ref, v_ref))
        b, gc = _head_column(b_ref[...], h), _head_column(g_ref[...], h + N_HEADS)
        _, gam, kk, qk, last, e_g, dec, _ = _dn_chunk_common(q, k, gc, ri, ci)
        tinv = _tri_inverse(jnp.where(ri > ci, b * kk * gam, 0.0), ri, ci)
        u_ref[...] = _unchunk(_bdot(tinv, v * b, "nn"))
        w_ref[...] = _unchunk(_bdot(tinv, k * (b * e_g), "nn"))
        qd_ref[...] = _unchunk(q * e_g)
        kd_ref[...] = _unchunk(k * dec)
        a_ref[0] = _unchunk(qk * gam)
        ti_ref[0] = _unchunk(tinv)
        el_ref[0] = jnp.broadcast_to(jnp.exp(last), (rows_blk // DN_CHUNK, 8, 128))

    head, lanes, hm, sq, tile = _dn_specs(t, rows_blk)
    act = jax.ShapeDtypeStruct((N_HEADS, t, D_HEAD), F32)
    sqs = jax.ShapeDtypeStruct((N_HEADS, t, DN_CHUNK), F32)
    return pl.pallas_call(
        body, name="dn_intra_fwd", grid=(t // rows_blk, N_HEADS),
        in_specs=[head(0), head(N_HEADS), head(2 * N_HEADS), lanes, lanes],
        out_specs=[hm] * 4 + [sq, sq, tile],
        out_shape=[act] * 4 + [sqs, sqs, jax.ShapeDtypeStruct((N_HEADS, n_chunks, 8, 128), F32)],
        compiler_params=_cp(),
    )(qkv, qkv, qkv, beta_t, g_t)


def _dn_scan_specs(t, rows_blk, reverse):
    n_groups = t // rows_blk

    def at(g):
        return n_groups - 1 - g if reverse else g

    per = rows_blk // DN_CHUNK
    act = pl.BlockSpec((N_HEADS, rows_blk, D_HEAD), lambda g: (0, at(g), 0))
    sq = pl.BlockSpec((N_HEADS, rows_blk, DN_CHUNK), lambda g: (0, at(g), 0))
    state = pl.BlockSpec((N_HEADS, per, D_HEAD, D_HEAD), lambda g: (0, at(g), 0, 0))
    tile = pl.BlockSpec((N_HEADS, per, 8, 128), lambda g: (0, at(g), 0, 0))
    return act, sq, state, tile


def _dn_scan_fwd(u, w, qd, kd, a, el):
    t = u.shape[1]
    n_chunks = t // DN_CHUNK
    rows_blk = DN_SCAN_GROUP * DN_CHUNK

    def body(u_ref, w_ref, qd_ref, kd_ref, a_ref, el_ref, o_ref, vn_ref, s_ref, s_scr):
        @pl.when(pl.program_id(0) == 0)
        def _():
            s_scr[...] = jnp.zeros_like(s_scr)

        for cc in range(DN_SCAN_GROUP):
            rows = slice(cc * DN_CHUNK, (cc + 1) * DN_CHUNK)
            s = s_scr[...]
            s_ref[:, cc] = s
            v_new = u_ref[:, rows, :] - _bdot(w_ref[:, rows, :], s, "nn")
            vn_ref[:, rows, :] = v_new
            o_ref[:, rows, :] = _bdot(qd_ref[:, rows, :], s, "nn") + _bdot(a_ref[:, rows, :], v_new, "nn")
            s_scr[...] = s * el_ref[:, cc][:, 0:1, :] + _bdot(kd_ref[:, rows, :], v_new, "tn")

    act, sq, state, tile = _dn_scan_specs(t, rows_blk, reverse=False)
    shp = jax.ShapeDtypeStruct((N_HEADS, t, D_HEAD), F32)
    return pl.pallas_call(
        body, name="dn_scan_fwd", grid=(t // rows_blk,),
        in_specs=[act, act, act, act, sq, tile], out_specs=[act, act, state],
        out_shape=[shp, shp, jax.ShapeDtypeStruct((N_HEADS, n_chunks, D_HEAD, D_HEAD), F32)],
        scratch_shapes=[pltpu.VMEM((N_HEADS, D_HEAD, D_HEAD), F32)],
        compiler_params=_cp(dimension_semantics=("arbitrary",)),
    )(u, w, qd, kd, a, el)


def _dn_scan_bwd(w, qd, kd, a, el, vn, s_all, do):
    t = w.shape[1]
    n_chunks = t // DN_CHUNK
    rows_blk = DN_SCAN_GROUP * DN_CHUNK

    def body(w_ref, qd_ref, kd_ref, a_ref, el_ref, vn_ref, s_ref, do_ref, dvn_ref, dkd_ref, dqd_ref, dw_ref, dl_ref, ds_scr):
        @pl.when(pl.program_id(0) == 0)
        def _():
            ds_scr[...] = jnp.zeros_like(ds_scr)

        for cc in reversed(range(DN_SCAN_GROUP)):
            rows = slice(cc * DN_CHUNK, (cc + 1) * DN_CHUNK)
            s = s_ref[:, cc]
            d_s = ds_scr[...]
            e_last = el_ref[:, cc][:, 0:1, :]
            d_o = do_ref[:, rows, :]
            dv_new = _bdot(a_ref[:, rows, :], d_o, "tn") + _bdot(kd_ref[:, rows, :], d_s, "nn")
            ds_scr[...] = d_s * e_last + _bdot(qd_ref[:, rows, :], d_o, "tn") - _bdot(w_ref[:, rows, :], dv_new, "tn")
            dvn_ref[:, rows, :] = dv_new
            dkd_ref[:, rows, :] = _bdot(vn_ref[:, rows, :], d_s, "nt")
            dqd_ref[:, rows, :] = _bdot(d_o, s, "nt")
            dw_ref[:, rows, :] = -_bdot(dv_new, s, "nt")
            dlast = jnp.sum(jnp.sum(d_s * s, axis=2, keepdims=True), axis=1, keepdims=True)
            dl_ref[:, cc] = jnp.broadcast_to(dlast * e_last, (N_HEADS, 8, 128))

    act, sq, state, tile = _dn_scan_specs(t, rows_blk, reverse=True)
    shp = jax.ShapeDtypeStruct((N_HEADS, t, D_HEAD), F32)
    return pl.pallas_call(
        body, name="dn_scan_bwd", grid=(t // rows_blk,),
        in_specs=[act, act, act, sq, tile, act, state, act], out_specs=[act] * 4 + [tile],
        out_shape=[shp] * 4 + [jax.ShapeDtypeStruct((N_HEADS, n_chunks, 8, 128), F32)],
        scratch_shapes=[pltpu.VMEM((N_HEADS, D_HEAD, D_HEAD), F32)],
        compiler_params=_cp(dimension_semantics=("arbitrary",)),
    )(w, qd, kd, a, el, vn, s_all, do)


def _dn_intra_bwd(qkv, beta_t, g_t, tinv_all, vn, do, dvn, dkd, dqd, dw, dl):
    t = qkv.shape[0]
    rows_blk = min(DN_GROUP * DN_CHUNK, t)

    def body(q_ref, k_ref, v_ref, b_ref, g_ref, ti_ref, vn_ref, do_ref, dvn_ref, dkd_ref, dqd_ref, dw_ref, dl_ref,
             dq_ref, dk_ref, dv_ref, db_ref, dg_ref):
        ri = lax.broadcasted_iota(jnp.int32, (DN_CHUNK, DN_CHUNK), 0)
        ci = lax.broadcasted_iota(jnp.int32, (DN_CHUNK, DN_CHUNK), 1)
        h = pl.program_id(1)
        q, k, v = (_chunks(r[...]) for r in (q_ref, k_ref, v_ref))
        b, gc = _head_column(b_ref[...], h), _head_column(g_ref[...], h + N_HEADS)
        tinv = _chunks(ti_ref[0])
        dv_new, dk_dec, dq_dec, d_w = (_chunks(r[...]) for r in (dvn_ref, dkd_ref, dqd_ref, dw_ref))
        eye, gam, kk, qk, _, e_g, dec, rcol = _dn_chunk_common(q, k, gc, ri, ci)
        bv = v * b
        bk = k * (b * e_g)

        d_a = jnp.where(ri >= ci, _bdot(_chunks(do_ref[...]), _chunks(vn_ref[...]), "nt"), 0.0)
        dbv = _bdot(tinv, dv_new, "tn")
        dbk = _bdot(tinv, d_w, "tn")
        d_tinv = _bdot(dv_new, bv, "nt") + _bdot(d_w, bk, "nt")
        d_m = -jnp.where(ri > ci, _dot3(_dot3(tinv, d_tinv, "tn"), tinv, "nt"), 0.0)

        d_kk = d_m * b * gam
        d_gam = d_m * b * kk + d_a * qk
        d_qk = d_a * gam
        dq_ref[...] = _unchunk(_bdot(d_qk, k, "nn") + dq_dec * e_g)
        dk_ref[...] = _unchunk(_bdot(d_qk, q, "tn") + _bdot(d_kk, k, "nn") + _bdot(d_kk, k, "tn")
                               + dk_dec * dec + dbk * (b * e_g))
        dv_ref[...] = _unchunk(dbv * b)
        d_b = _unchunk(jnp.sum(d_m * kk * gam, axis=-1, keepdims=True) + jnp.sum(dbv * v, axis=-1, keepdims=True)
                       + jnp.sum(dbk * k, axis=-1, keepdims=True) * e_g)

        xg = d_gam * gam
        kdk = jnp.sum(dk_dec * (k * dec), axis=-1, keepdims=True)
        d_gc = (jnp.sum(xg, axis=-1, keepdims=True) - _row_to_col(jnp.sum(xg, axis=-2, keepdims=True), eye)
                + jnp.sum(dq_dec * (q * e_g), axis=-1, keepdims=True) - kdk
                + jnp.sum(dbk * bk, axis=-1, keepdims=True))
        d_last_total = dl_ref[0][:, 0:1, 0:1] + jnp.sum(kdk, axis=-2, keepdims=True)
        d_g = _unchunk(d_gc + jnp.where(rcol == DN_CHUNK - 1, d_last_total, 0.0))

        @pl.when(h == 0)
        def _():
            db_ref[...] = jnp.zeros_like(db_ref)
            dg_ref[...] = jnp.zeros_like(dg_ref)

        lane = lax.broadcasted_iota(jnp.int32, db_ref.shape, 1)
        db_ref[...] += jnp.where(lane == h, d_b, 0.0)
        dg_ref[...] += jnp.where(lane == h + N_HEADS, d_g, 0.0)

    head, lanes, hm, sq, tile = _dn_specs(t, rows_blk)
    return pl.pallas_call(
        body, name="dn_intra_bwd", grid=(t // rows_blk, N_HEADS),
        in_specs=[head(0), head(N_HEADS), head(2 * N_HEADS), lanes, lanes, sq] + [hm] * 6 + [tile],
        out_specs=[head(0), head(0), head(0), lanes, lanes],
        out_shape=[jax.ShapeDtypeStruct((t, D_MODEL), F32)] * 3 + [jax.ShapeDtypeStruct((t, 128), F32)] * 2,
        compiler_params=_cp(),
    )(qkv, qkv, qkv, beta_t, g_t, tinv_all, vn, do, dvn, dkd, dqd, dw, dl)


def _dn_post_fwd(o, proj, gn):
    t = o.shape[1]

    def body(o_ref, z_ref, g_ref, out_ref):
        ov, z = o_ref[...], z_ref[...]
        r = lax.rsqrt(jnp.mean(ov * ov, axis=-1, keepdims=True) + NORM_EPS)
        out_ref[...] = (((ov * r) * g_ref[...]) * (z * _sigmoid(z))).astype(BF16)

    blk = pl.BlockSpec((t, D_HEAD), lambda h: (0, h))
    return pl.pallas_call(
        body, name="dn_post_fwd", grid=(N_HEADS,),
        in_specs=[pl.BlockSpec((None, t, D_HEAD), lambda h: (h, 0, 0)),
                  pl.BlockSpec((t, D_HEAD), lambda h: (0, C_DNZ // D_HEAD + h)),
                  pl.BlockSpec((1, D_HEAD), lambda h: (0, 0))],
        out_specs=blk, out_shape=jax.ShapeDtypeStruct((t, D_MODEL), BF16), compiler_params=_cp(),
    )(o, proj, gn)


def _dn_post_bwd(o, proj, gn, dout):
    t = o.shape[1]

    def body(o_ref, z_ref, g_ref, d_ref, do_ref, dz_ref, dg_ref):
        @pl.when(pl.program_id(0) == 0)
        def _():
            dg_ref[...] = jnp.zeros_like(dg_ref)

        ov, z, d = o_ref[...], z_ref[...], d_ref[...]
        r = lax.rsqrt(jnp.mean(ov * ov, axis=-1, keepdims=True) + NORM_EPS)
        ohat = ov * r
        s = _sigmoid(z)
        d_on = d * (z * s)
        dz_ref[...] = (d * (ohat * g_ref[...]) * (s * (1.0 + z * (1.0 - s)))).astype(BF16)
        dg_ref[...] += jnp.sum(d_on * ohat, axis=0, keepdims=True)
        dxh = d_on * g_ref[...]
        do_ref[...] = r * (dxh - ohat * jnp.mean(dxh * ohat, axis=-1, keepdims=True))

    blk = pl.BlockSpec((t, D_HEAD), lambda h: (0, h))
    hm = pl.BlockSpec((None, t, D_HEAD), lambda h: (h, 0, 0))
    vec = pl.BlockSpec((1, D_HEAD), lambda h: (0, 0))
    return pl.pallas_call(
        body, name="dn_post_bwd", grid=(N_HEADS,),
        in_specs=[hm, pl.BlockSpec((t, D_HEAD), lambda h: (0, C_DNZ // D_HEAD + h)), vec, blk],
        out_specs=[hm, blk, vec],
        out_shape=[jax.ShapeDtypeStruct((N_HEADS, t, D_HEAD), F32), jax.ShapeDtypeStruct((t, D_MODEL), BF16),
                   jax.ShapeDtypeStruct((1, D_HEAD), F32)], compiler_params=_cp(),
    )(o, proj, gn, dout)


def _sb_fwd(proj):
    t = proj.shape[0]
    qblk = min(SB_QBLOCK, t)
    scale = 1.0 / math.sqrt(D_HEAD)

    hp = SB_HEADS_PER_STEP
    wid = hp * D_HEAD

    def body(q_ref, k_ref, v_ref, z_ref, o_ref, og_ref, l_ref, qb, kb, vb):
        for hh in range(hp):
            hs = slice(hh * D_HEAD, (hh + 1) * D_HEAD)
            qb[hh] = q_ref[:, hs].astype(BF16)
            kb[hh] = k_ref[:, hs].astype(BF16)
            vb[hh] = v_ref[:, hs].astype(BF16)
        ri = lax.broadcasted_iota(jnp.int32, (qblk, SB_BLOCK), 0)
        ci = lax.broadcasted_iota(jnp.int32, (qblk, SB_BLOCK), 1)
        r2 = lax.broadcasted_iota(jnp.int32, (SB_BLOCK, SB_BLOCK), 0)
        c2 = lax.broadcasted_iota(jnp.int32, (SB_BLOCK, SB_BLOCK), 1)
        upper = (r2 > c2).astype(BF16)
        nkb = qblk // SB_BLOCK

        def qblock(i, carry):
            rows = pl.ds(pl.multiple_of(i * qblk, qblk), qblk)
            qi = qb[:, rows, :]

            def tile(j, st, on_diagonal):
                acc, c = st
                cols = pl.ds(pl.multiple_of(j * SB_BLOCK, SB_BLOCK), SB_BLOCK)
                z = _dot(qi, kb[:, cols, :], "nt") * scale
                lb = jnp.minimum(z, 0.0) - jnp.log(1.0 + jnp.exp(-jnp.abs(z)))
                lf = lb - z
                if on_diagonal:
                    mask = (j * SB_BLOCK + ci) < (i * qblk + ri)
                    lf = jnp.where(mask, lf, 0.0)
                att = jnp.exp(lb + (_ones_dot(lf, upper) + c))
                if on_diagonal:
                    att = jnp.where(mask, att, 0.0)
                acc = acc + _dot(att.astype(BF16), vb[:, cols, :], "nn")
                return acc, c + jnp.sum(lf, axis=-1, keepdims=True)

            st = (jnp.zeros((hp, qblk, D_HEAD), F32), jnp.zeros((hp, qblk, 1), F32))
            for d in range(nkb):
                st = tile((i + 1) * nkb - 1 - d, st, True)
            acc, c = lax.fori_loop(0, i * nkb, lambda jj, s: tile(i * nkb - 1 - jj, s, False), st)
            l_ref[:, rows, :] = c
            for hh in range(hp):
                hs = slice(hh * D_HEAD, (hh + 1) * D_HEAD)
                zg = z_ref[rows, hs]
                o_ref[rows, hs] = acc[hh]
                og_ref[rows, hs] = (acc[hh] * (zg * _sigmoid(zg))).astype(BF16)
            return carry

        lax.fori_loop(0, t // qblk, qblock, 0)

    def head(off):
        return pl.BlockSpec((t, wid), lambda h: (0, off // wid + h))

    out = pl.BlockSpec((t, wid), lambda h: (0, h))
    return pl.pallas_call(
        body, name="sb_fwd", grid=(N_HEADS // hp,),
        in_specs=[head(C_SBQ), head(C_SBQ + D_MODEL), head(C_SBQ + 2 * D_MODEL), head(C_SBZ)],
        out_specs=[out, out, pl.BlockSpec((hp, t, 1), lambda h: (h, 0, 0))],
        out_shape=[jax.ShapeDtypeStruct((t, D_MODEL), F32), jax.ShapeDtypeStruct((t, D_MODEL), BF16),
                   jax.ShapeDtypeStruct((N_HEADS, t, 1), F32)],
        scratch_shapes=[pltpu.VMEM((hp, t, D_HEAD), BF16)] * 3, compiler_params=_cp(),
    )(proj, proj, proj, proj)


def _sb_bwd(proj, o, ltot, dog, after=None):
    t = proj.shape[0]
    qblk = min(SB_QBLOCK, t)
    scale = 1.0 / math.sqrt(D_HEAD)

    hp = SB_HEADS_PER_STEP
    wid = hp * D_HEAD

    def body(q_ref, k_ref, v_ref, z_ref, o_ref, l_ref, d_ref, *rest):
        dq_ref, dk_ref, dv_ref, dz_ref, qb, kb, vb, dob, dk_scr, dv_scr = rest[-10:]
        for hh in range(hp):
            hs = slice(hh * D_HEAD, (hh + 1) * D_HEAD)
            qb[hh] = q_ref[:, hs].astype(BF16)
            kb[hh] = k_ref[:, hs].astype(BF16)
            vb[hh] = v_ref[:, hs].astype(BF16)
            zg = z_ref[:, hs]
            sg = _sigmoid(zg)
            dgo = d_ref[:, hs]
            dob[hh] = (dgo * (zg * sg)).astype(BF16)
            dz_ref[:, hs] = (dgo * o_ref[:, hs] * (sg * (1.0 + zg * (1.0 - sg)))).astype(BF16)
        dk_scr[...] = jnp.zeros_like(dk_scr)
        dv_scr[...] = jnp.zeros_like(dv_scr)
        ri = lax.broadcasted_iota(jnp.int32, (qblk, SB_BLOCK), 0)
        ci = lax.broadcasted_iota(jnp.int32, (qblk, SB_BLOCK), 1)
        r2 = lax.broadcasted_iota(jnp.int32, (SB_BLOCK, SB_BLOCK), 0)
        c2 = lax.broadcasted_iota(jnp.int32, (SB_BLOCK, SB_BLOCK), 1)
        upper = (r2 > c2).astype(BF16)
        below = (r2 < c2).astype(BF16)

        def qblock(i, carry):
            rows = pl.ds(pl.multiple_of(i * qblk, qblk), qblk)
            qi = qb[:, rows, :]
            d_o = dob[:, rows, :]
            ltot = l_ref[:, rows, :]

            def tile(j, st, on_diagonal):
                dq, cpre, ce = st
                cols = pl.ds(pl.multiple_of(j * SB_BLOCK, SB_BLOCK), SB_BLOCK)
                kj, vj = kb[:, cols, :], vb[:, cols, :]
                z = _dot(qi, kj, "nt") * scale
                lb = jnp.minimum(z, 0.0) - jnp.log(1.0 + jnp.exp(-jnp.abs(z)))
                lf = lb - z
                if on_diagonal:
                    mask = (j * SB_BLOCK + ci) < (i * qblk + ri)
                    lf = jnp.where(mask, lf, 0.0)
                tile_sum = jnp.sum(lf, axis=-1, keepdims=True)
                att = jnp.exp(lb + ((ltot - cpre - tile_sum) + _ones_dot(lf, upper)))
                if on_diagonal:
                    att = jnp.where(mask, att, 0.0)
                e = _dot(d_o, vj, "nt") * att
                dlf = ce + _ones_dot(e, below)
                dzz = e - (e + dlf) * jnp.exp(lb)
                if on_diagonal:
                    dzz = jnp.where(mask, dzz, 0.0)
                dzz = dzz.astype(BF16)
                dq = dq + _dot(dzz, kj, "nn")
                dk_scr[:, cols, :] += _dot(dzz, qi, "tn")
                dv_scr[:, cols, :] += _dot(att.astype(BF16), d_o, "tn")
                return dq, cpre + tile_sum, ce + jnp.sum(e, axis=-1, keepdims=True)

            nkb = qblk // SB_BLOCK
            zero_col = jnp.zeros((hp, qblk, 1), F32)
            st = lax.fori_loop(0, i * nkb, lambda j, s: tile(j, s, False),
                               (jnp.zeros((hp, qblk, D_HEAD), F32), zero_col, zero_col))
            for d in range(nkb):
                st = tile(i * nkb + d, st, True)
            dq = st[0]
            for hh in range(hp):
                dq_ref[rows, hh * D_HEAD:(hh + 1) * D_HEAD] = (dq[hh] * scale).astype(BF16)
            return carry

        lax.fori_loop(0, t // qblk, qblock, 0)
        for hh in range(hp):
            hs = slice(hh * D_HEAD, (hh + 1) * D_HEAD)
            dk_ref[:, hs] = (dk_scr[hh] * scale).astype(BF16)
            dv_ref[:, hs] = dv_scr[hh].astype(BF16)

    def head(off):
        return pl.BlockSpec((t, wid), lambda h: (0, off // wid + h))

    extra_specs, extra = [], []
    if after is not None:
        extra_specs, extra = [pl.BlockSpec(after.shape, lambda h: (0, 0))], [after]
    return pl.pallas_call(
        body, name="sb_bwd", grid=(N_HEADS // hp,),
        in_specs=[head(C_SBQ), head(C_SBQ + D_MODEL), head(C_SBQ + 2 * D_MODEL), head(C_SBZ), head(0),
                  pl.BlockSpec((hp, t, 1), lambda h: (h, 0, 0)), head(0)] + extra_specs,
        out_specs=[head(0)] * 4, out_shape=[jax.ShapeDtypeStruct((t, D_MODEL), BF16)] * 4,
        scratch_shapes=[pltpu.VMEM((hp, t, D_HEAD), BF16)] * 4 + [pltpu.VMEM((hp, t, D_HEAD), F32)] * 2,
        compiler_params=_cp(),
    )(proj, proj, proj, proj, o, ltot, dog, *extra)


def _mem_fwd(proj, mkv):
    t = proj.shape[0]
    tq = _pick(t, (512, 256))
    m_len = mkv.shape[0]
    scale = 1.0 / math.sqrt(MEM_DH)

    def body(q_ref, z_ref, kv_ref, o_ref, og_ref):
        q = q_ref[...]
        mk = kv_ref[:, :MEM_W].astype(BF16)
        mv = kv_ref[:, MEM_W:].astype(BF16)
        lane = lax.broadcasted_iota(jnp.int32, q.shape, 1) >> 6
        o = jnp.zeros(q.shape, F32)
        for h in range(MEM_HEADS):
            s = _bdot(jnp.where(lane == h, q, 0.0), mk, "nt") * scale
            p = jnp.exp(s - jnp.max(s, axis=-1, keepdims=True))
            p = p / jnp.sum(p, axis=-1, keepdims=True)
            o = o + jnp.where(lane == h, _bdot(p, mv, "nn"), 0.0)
        z = z_ref[...]
        o_ref[...] = o
        og_ref[...] = (o * (z * _sigmoid(z))).astype(BF16)

    out = pl.BlockSpec((tq, MEM_W), lambda i: (i, 0))
    return pl.pallas_call(
        body, name="mem_fwd", grid=(t // tq,),
        in_specs=[pl.BlockSpec((tq, MEM_W), lambda i: (i, C_MQ // MEM_W)),
                  pl.BlockSpec((tq, MEM_W), lambda i: (i, C_MZ // MEM_W)),
                  pl.BlockSpec((m_len, 2 * MEM_W), lambda i: (0, 0))],
        out_specs=[out, out],
        out_shape=[jax.ShapeDtypeStruct((t, MEM_W), F32), jax.ShapeDtypeStruct((t, MEM_W), BF16)],
        compiler_params=_cp(),
    )(proj, proj, mkv)


def _mem_bwd(proj, mkv, o, dog):
    t = proj.shape[0]
    tq = _pick(t, (512, 256))
    m_len = mkv.shape[0]
    scale = 1.0 / math.sqrt(MEM_DH)

    def body(q_ref, z_ref, kv_ref, o_ref, d_ref, dq_ref, dz_ref, dkv_ref):
        @pl.when(pl.program_id(0) == 0)
        def _():
            dkv_ref[...] = jnp.zeros_like(dkv_ref)

        q = q_ref[...]
        z = z_ref[...]
        sg = _sigmoid(z)
        dgo = d_ref[...]
        d_o = dgo * (z * sg)
        dz_ref[...] = (dgo * o_ref[...] * (sg * (1.0 + z * (1.0 - sg)))).astype(BF16)
        mk = kv_ref[:, :MEM_W].astype(BF16)
        mv = kv_ref[:, MEM_W:].astype(BF16)
        lane = lax.broadcasted_iota(jnp.int32, q.shape, 1) >> 6
        klane = lax.broadcasted_iota(jnp.int32, (m_len, MEM_W), 1) >> 6
        dq = jnp.zeros(q.shape, F32)
        dmk = jnp.zeros((m_len, MEM_W), F32)
        dmv = jnp.zeros((m_len, MEM_W), F32)
        for h in range(MEM_HEADS):
            qh = jnp.where(lane == h, q, 0.0)
            doh = jnp.where(lane == h, d_o, 0.0)
            s = _bdot(qh, mk, "nt") * scale
            p = jnp.exp(s - jnp.max(s, axis=-1, keepdims=True))
            p = p / jnp.sum(p, axis=-1, keepdims=True)
            dp = _bdot(doh, mv, "nt")
            ds = p * (dp - jnp.sum(dp * p, axis=-1, keepdims=True)) * scale
            dq = dq + jnp.where(lane == h, _bdot(ds, mk, "nn"), 0.0)
            dmk = dmk + jnp.where(klane == h, _bdot(ds, qh, "tn"), 0.0)
            dmv = dmv + jnp.where(klane == h, _bdot(p, doh, "tn"), 0.0)
        dq_ref[...] = dq.astype(BF16)
        dkv_ref[:, :MEM_W] += dmk
        dkv_ref[:, MEM_W:] += dmv

    blk = pl.BlockSpec((tq, MEM_W), lambda i: (i, 0))
    kv = pl.BlockSpec((m_len, 2 * MEM_W), lambda i: (0, 0))
    return pl.pallas_call(
        body, name="mem_bwd", grid=(t // tq,),
        in_specs=[pl.BlockSpec((tq, MEM_W), lambda i: (i, C_MQ // MEM_W)),
                  pl.BlockSpec((tq, MEM_W), lambda i: (i, C_MZ // MEM_W)), kv, blk, blk],
        out_specs=[blk, blk, kv],
        out_shape=[jax.ShapeDtypeStruct((t, MEM_W), BF16), jax.ShapeDtypeStruct((t, MEM_W), BF16),
                   jax.ShapeDtypeStruct((m_len, 2 * MEM_W), F32)], compiler_params=_cp(),
    )(proj, proj, mkv, o, dog)


_GW = 512


def _merge_fwd(proj, y_dn, y_sb, y_m):
    t = proj.shape[0]
    tb = _pick(t, (512, 256))
    nc = D_MODEL // _GW

    def body(g1, g2, g3, y1, y2, y3, out_ref):
        out_ref[...] = (_sigmoid(g1[...]) * y1[...] + _sigmoid(g2[...]) * y2[...] + _sigmoid(g3[...]) * y3[...]).astype(BF16)

    def gate(kb):
        return pl.BlockSpec((tb, _GW), lambda i, c: (i, C_GATES // _GW + kb * nc + c))

    blk = pl.BlockSpec((tb, _GW), lambda i, c: (i, c))
    return pl.pallas_call(
        body, name="merge_fwd", grid=(t // tb, nc), in_specs=[gate(0), gate(1), gate(2), blk, blk, blk],
        out_specs=blk, out_shape=jax.ShapeDtypeStruct((t, D_MODEL), BF16), compiler_params=_cp(),
    )(proj, proj, proj, y_dn, y_sb, y_m)


def _merge_bwd(proj, y_dn, y_sb, y_m, dm):
    t = proj.shape[0]
    tb = _pick(t, (512, 256))
    nc = D_MODEL // _GW

    def body(g1, g2, g3, y1, y2, y3, dm_ref, d1, d2, d3, dg1, dg2, dg3):
        d = dm_ref[...]
        for g, y, dy, dg in ((g1, y1, d1, dg1), (g2, y2, d2, dg2), (g3, y3, d3, dg3)):
            s = _sigmoid(g[...])
            dy[...] = (d * s).astype(BF16)
            dg[...] = (d * y[...] * (s * (1.0 - s))).astype(BF16)

    def gate(kb):
        return pl.BlockSpec((tb, _GW), lambda i, c: (i, C_GATES // _GW + kb * nc + c))

    blk = pl.BlockSpec((tb, _GW), lambda i, c: (i, c))
    act = jax.ShapeDtypeStruct((t, D_MODEL), BF16)
    return pl.pallas_call(
        body, name="merge_bwd", grid=(t // tb, nc), in_specs=[gate(0), gate(1), gate(2), blk, blk, blk, blk],
        out_specs=[blk] * 6, out_shape=[act] * 6, compiler_params=_cp(),
    )(proj, proj, proj, y_dn, y_sb, y_m, dm)


def _final_loss(x, mo, g, tgt):
    t, d = x.shape
    tb = _pick(t, (512, 256))

    def body(x_ref, mo_ref, g_ref, t_ref, do_ref, dob_ref, loss_ref, dg_ref):
        @pl.when(pl.program_id(0) == 0)
        def _():
            loss_ref[...] = jnp.zeros_like(loss_ref)
            dg_ref[...] = jnp.zeros_like(dg_ref)

        out = x_ref[...] + mo_ref[...]
        r = lax.rsqrt(jnp.mean(out * out, axis=-1, keepdims=True) + NORM_EPS)
        xhat = out * r
        gv = g_ref[...]
        err = xhat * gv - t_ref[...]
        per_tok = jnp.mean(err * err, axis=-1, keepdims=True)
        loss_ref[...] += 0.5 * jnp.sum(per_tok, axis=0, keepdims=True)
        dy = err * (1.0 / d)
        dg_ref[...] += jnp.sum(dy * xhat, axis=0, keepdims=True)
        dxh = dy * gv
        dout = r * (dxh - xhat * jnp.mean(dxh * xhat, axis=-1, keepdims=True))
        do_ref[...] = dout
        dob_ref[...] = dout.astype(BF16)

    row = pl.BlockSpec((tb, d), lambda i: (i, 0))
    vec = pl.BlockSpec((1, d), lambda i: (0, 0))
    return pl.pallas_call(
        body, name="final_loss", grid=(t // tb,), in_specs=[row, row, vec, row],
        out_specs=[row, row, pl.BlockSpec((1, 128), lambda i: (0, 0)), vec],
        out_shape=[jax.ShapeDtypeStruct((t, d), F32), jax.ShapeDtypeStruct((t, d), BF16),
                   jax.ShapeDtypeStruct((1, 128), F32), jax.ShapeDtypeStruct((1, d), F32)],
        compiler_params=_cp(),
    )(x, mo, g, tgt)


def _cast_bf16(a, name):
    r, c = a.shape
    tb = _pick(r, (128, 496, 240))

    def body(a_ref, o_ref):
        o_ref[...] = a_ref[...].astype(BF16)

    blk = pl.BlockSpec((tb, c), lambda i: (i, 0))
    return pl.pallas_call(body, name=name, grid=(r // tb,), in_specs=[blk], out_specs=blk,
                          out_shape=jax.ShapeDtypeStruct((r, c), BF16), compiler_params=_cp())(a)


WIN_START = (0, 23, 45, 68)
_S1_LO, _S1_HI = 1148, 1164
_S1_BA_POS = SHARD_PAD - 128


def _to_window(x, s):
    if s == 0:
        return x
    if s in (2, 3):
        return pltpu.roll(x, 120 if s == 2 else 124, 1)
    pos = lax.broadcasted_iota(jnp.int32, x.shape, 1)
    head = pltpu.roll(x, 4, 1)
    tail = pltpu.roll(x, SHARD_PAD - 12, 1)
    ba = jnp.where(pos < _S1_BA_POS + (_S1_HI - _S1_LO), pltpu.roll(x, _S1_BA_POS - _S1_LO, 1), 0.0)
    return jnp.where(pos < _S1_LO + 4, head, jnp.where(pos < _S1_BA_POS, tail, ba))


def _from_window(g, s):
    if s == 0:
        return g
    if s in (2, 3):
        return pltpu.roll(g, SHARD_PAD - (120 if s == 2 else 124), 1)
    col = lax.broadcasted_iota(jnp.int32, g.shape, 1)
    head = pltpu.roll(g, SHARD_PAD - 4, 1)
    tail = pltpu.roll(g, 12, 1)
    ba = pltpu.roll(g, SHARD_PAD - (_S1_BA_POS - _S1_LO), 1)
    return jnp.where(col < _S1_LO, head, jnp.where(col < _S1_HI, ba, tail))


def _cast_to_window(w, shard, name):
    r, c = w.shape
    tb = _pick(r, (128,))

    def body(s_ref, w_ref, o_ref, pad_scr):
        pad_scr[...] = jnp.zeros_like(pad_scr)
        pad_scr[:, :c] = w_ref[...]
        x = pad_scr[...]
        for s in range(N_SHARD):
            @pl.when(s_ref[0] == s)
            def _():
                o_ref[...] = _to_window(x, s).astype(BF16)

    return pl.pallas_call(
        body, name=name,
        grid_spec=pltpu.PrefetchScalarGridSpec(
            num_scalar_prefetch=1, grid=(r // tb,),
            in_specs=[pl.BlockSpec((tb, c), lambda i, s: (i, 0))],
            out_specs=pl.BlockSpec((tb, SHARD_PAD), lambda i, s: (i, 0)),
            scratch_shapes=[pltpu.VMEM((tb, SHARD_PAD), F32)]),
        out_shape=jax.ShapeDtypeStruct((r, SHARD_PAD), BF16), compiler_params=_cp(),
    )(shard, w)


def _pair_add(g, recv, c_idx, name):
    n, r, c = g.shape
    half = r // 2
    tb = _pick(half, (128, 240))
    nb = half // tb

    def body(c_ref, g_ref, r_ref, o_ref):
        o_ref[...] = (g_ref[...].astype(F32) + r_ref[...].astype(F32)).astype(BF16)

    blk = pl.BlockSpec((n, tb, c), lambda i, c_ref: (0, i, 0))
    return pl.pallas_call(
        body, name=name,
        grid_spec=pltpu.PrefetchScalarGridSpec(
            num_scalar_prefetch=1, grid=(nb,),
            in_specs=[pl.BlockSpec((n, tb, c), lambda i, c_ref: (0, c_ref[0] * nb + i, 0)), blk], out_specs=blk),
        out_shape=jax.ShapeDtypeStruct((n, half, c), BF16), compiler_params=_cp(),
    )(c_idx, g, recv)


def _chip_sum(parts, by_chip, place, name):
    n, h, c = parts.shape
    tb = _pick(h, (128, 240))
    nb = h // tb

    def body(p_ref, mine_ref, *rest):
        others, o_ref = rest[:n], rest[n]
        me = jnp.zeros((tb, c), jnp.int32) + p_ref[0]
        acc = None
        for q in range(n):
            term = jnp.where(me == q, mine_ref[...], others[q][...]).astype(F32)
            acc = term if acc is None else acc + term
        o_ref[...] = acc

    def other(q):
        return pl.BlockSpec((None, tb, c), lambda i, p: (jnp.where(p[0] == q, (q + 1) % n, q), i, 0))

    return pl.pallas_call(
        body, name=name,
        grid_spec=pltpu.PrefetchScalarGridSpec(
            num_scalar_prefetch=1, grid=(nb,),
            in_specs=[pl.BlockSpec((None, tb, c), lambda i, p: (p[0], i, 0))] + [other(q) for q in range(n)],
            out_specs=pl.BlockSpec((tb, c), lambda i, p: (p[1] * nb + i, 0))),
        out_shape=jax.ShapeDtypeStruct((2 * h, c), F32), compiler_params=_cp(),
    )(place, parts, *([by_chip] * n))


def _adamw_math(w, g, m, v):
    m = ADAM_B1 * m + (1.0 - ADAM_B1) * g
    v = ADAM_B2 * v + (1.0 - ADAM_B2) * (g * g)
    m_hat = m / (1.0 - ADAM_B1 ** ADAM_STEP)
    v_hat = v / (1.0 - ADAM_B2 ** ADAM_STEP)
    delta = -ADAM_LR * (m_hat / (jnp.sqrt(v_hat) + ADAM_EPS) + ADAM_WD * w)
    return delta, m, v


def _adamw(w, g, m, v, name):
    r, c = w.shape
    tb = _pick(r, (128, 496, 240))

    def body(w_ref, g_ref, m_ref, v_ref, go_ref, d_ref, mo_ref, vo_ref):
        gv = g_ref[...]
        d, mn, vn = _adamw_math(w_ref[...], gv, m_ref[...], v_ref[...])
        go_ref[...] = gv
        d_ref[...] = d
        mo_ref[...] = mn
        vo_ref[...] = vn

    blk = pl.BlockSpec((tb, c), lambda i: (i, 0))
    return pl.pallas_call(
        body, name=name, grid=(r // tb,), in_specs=[blk] * 4, out_specs=[blk] * 4,
        out_shape=[jax.ShapeDtypeStruct((r, c), F32)] * 4, compiler_params=_cp(),
    )(w, g, m, v)


def _adamw_window(w, g_win, m, v, shard, name):
    r, c = w.shape
    tb = _pick(r, (128,))

    def body(s_ref, w_ref, g_ref, m_ref, v_ref, go_ref, d_ref, mo_ref, vo_ref, g_scr):
        gw = g_ref[...]
        for s in range(N_SHARD):
            @pl.when(s_ref[0] == s)
            def _():
                g_scr[...] = _from_window(gw, s)

        gv = g_scr[:, :c]
        d, mn, vn = _adamw_math(w_ref[...], gv, m_ref[...], v_ref[...])
        go_ref[...] = gv
        d_ref[...] = d
        mo_ref[...] = mn
        vo_ref[...] = vn

    blk = pl.BlockSpec((tb, c), lambda i, s: (i, 0))
    return pl.pallas_call(
        body, name=name,
        grid_spec=pltpu.PrefetchScalarGridSpec(
            num_scalar_prefetch=1, grid=(r // tb,),
            in_specs=[blk, pl.BlockSpec((tb, SHARD_PAD), lambda i, s: (i, 0)), blk, blk], out_specs=[blk] * 4,
            scratch_shapes=[pltpu.VMEM((tb, SHARD_PAD), F32)]),
        out_shape=[jax.ShapeDtypeStruct((r, c), F32)] * 4, compiler_params=_cp(),
    )(shard, w, g_win, m, v)


def _small_update(gathered, w, m, v):
    def body(p_ref, w_ref, m_ref, v_ref, g_ref, d_ref, mo_ref, vo_ref):
        g = p_ref[0]
        for i in range(1, N_DEV):
            g = g + p_ref[i]
        d, mn, vn = _adamw_math(w_ref[...], g, m_ref[...], v_ref[...])
        g_ref[...] = g
        d_ref[...] = d
        mo_ref[...] = mn
        vo_ref[...] = vn

    full = pl.BlockSpec((S_ROWS, 128), lambda i: (0, 0))
    return pl.pallas_call(
        body, name="small_update", grid=(1,),
        in_specs=[pl.BlockSpec((N_DEV, S_ROWS, 128), lambda i: (0, 0, 0)), full, full, full], out_specs=[full] * 4,
        out_shape=[jax.ShapeDtypeStruct((S_ROWS, 128), F32)] * 4, compiler_params=_cp(),
    )(gathered, w, m, v)


_ANY = pl.BlockSpec(memory_space=pl.ANY)


def _place():
    x, y, c = lax.axis_index("x"), lax.axis_index("y"), lax.axis_index("c")
    chips = [(1 - x, y), (x, 1 - y), (1 - x, 1 - y)]
    return x, y, c, chips


def _pair_reduce_send(grads, tag):
    n = len(grads)

    def body(*refs):
        ins, outs = refs[:n], refs[n:2 * n]
        send_sems, recv_sems = refs[2 * n:]
        x, y, c, _ = _place()
        sibling = (x, y, 1 - c)
        cps = []
        for a in range(n):
            half = ins[a].shape[1] // 2
            theirs = pl.ds(pl.multiple_of((1 - c) * half, 8), half)
            cp = pltpu.make_async_remote_copy(
                src_ref=ins[a].at[:, theirs], dst_ref=outs[a], send_sem=send_sems.at[a], recv_sem=recv_sems.at[a],
                device_id=sibling, device_id_type=MESH)
            cp.start()
            cps.append(cp)
        for cp in cps:
            cp.wait()

    return pl.pallas_call(
        body, name="pair_reduce_send_" + tag, in_specs=[_ANY] * n, out_specs=[_ANY] * n,
        out_shape=[jax.ShapeDtypeStruct((g.shape[0], g.shape[1] // 2, g.shape[2]), g.dtype) for g in grads],
        scratch_shapes=[pltpu.SemaphoreType.DMA((n,)), pltpu.SemaphoreType.DMA((n,))],
        compiler_params=pltpu.CompilerParams(has_side_effects=True),
    )(*grads)


_HBM = pl.BlockSpec(memory_space=pltpu.HBM)
_SEM = pl.BlockSpec(memory_space=pltpu.SEMAPHORE)
_DATAFLOW = pltpu.SideEffectType.DATAFLOW_SIDE_EFFECTING


def _chip_exchange_copies(ins, lands, send_sems, recv_sems):
    x, y, c, chips = _place()
    me = 2 * x + y
    cps = []
    for a in range(len(ins)):
        for j, (qx, qy) in enumerate(chips):
            cps.append(pltpu.make_async_remote_copy(
                src_ref=ins[a].at[2 * qx + qy], dst_ref=lands[a].at[me], send_sem=send_sems.at[3 * a + j],
                recv_sem=recv_sems.at[3 * a + j], device_id=(qx, qy, c), device_id_type=MESH))
    return cps


def _chip_exchange_start(parts, tag):
    n = len(parts)

    def body(*refs):
        ins, lands = refs[:n], refs[n:2 * n]
        send_sems, recv_sems = refs[2 * n:2 * n + 2]
        token = refs[4 * n + 2]
        for cp in _chip_exchange_copies(ins, lands, send_sems, recv_sems):
            cp.start()
        token[...] = jnp.zeros_like(token)

    hbm = [pltpu.HBM(p.shape, p.dtype) for p in parts]
    lands = [pltpu.with_memory_space_constraint(lax.empty(p.shape, p.dtype), pltpu.HBM) for p in parts]
    res = pl.pallas_call(
        body, name="chip_exchange_start_" + tag,
        out_shape=(pltpu.SemaphoreType.DMA((3 * n,)), pltpu.SemaphoreType.DMA((3 * n,)), *hbm, *hbm,
                   jax.ShapeDtypeStruct((8, 128), F32)),
        in_specs=[_HBM] * (2 * n), out_specs=(_SEM, _SEM, *([_HBM] * (2 * n)), pl.BlockSpec(memory_space=pltpu.VMEM)),
        input_output_aliases={a: 2 + a for a in range(2 * n)},
        compiler_params=pltpu.CompilerParams(has_side_effects=_DATAFLOW),
    )(*[pltpu.with_memory_space_constraint(p, pltpu.HBM) for p in parts], *lands)
    return res[0], res[1], res[2:2 + n], res[2 + n:2 + 2 * n], res[2 + 2 * n]


def _chip_exchange_wait(send_sems, recv_sems, parts, lands, after, tag):
    n = len(parts)

    def body(*refs):
        ins, land_refs = refs[:n], refs[n:2 * n]
        s_sems, r_sems = refs[2 * n:2 * n + 2]
        for cp in _chip_exchange_copies(ins, land_refs, s_sems, r_sems):
            cp.wait_send()
            cp.wait_recv()

    hbm = [pltpu.HBM(p.shape, p.dtype) for p in parts]
    res = pl.pallas_call(
        body, name="chip_exchange_wait_" + tag, out_shape=(*hbm, *hbm),
        in_specs=[_HBM] * (2 * n) + [_SEM, _SEM, _ANY], out_specs=tuple([_HBM] * (2 * n)),
        input_output_aliases={a: a for a in range(2 * n)},
        compiler_params=pltpu.CompilerParams(has_side_effects=_DATAFLOW),
    )(*parts, *lands, send_sems, recv_sems, after)
    return res[:n], res[n:]


def _halves_ici_copies(srcs, lands, send_sems, recv_sems):
    x, y, c, chips = _place()
    me = 2 * x + y
    cps = []
    for a, src in enumerate(srcs):
        half = src.shape[0] // 2
        mine = pl.ds(pl.multiple_of(c * half, 16), half)
        for j, (qx, qy) in enumerate(chips):
            cps.append(pltpu.make_async_remote_copy(
                src_ref=src.at[mine], dst_ref=lands[a].at[me, mine], send_sem=send_sems.at[3 * a + j],
                recv_sem=recv_sems.at[3 * a + j], device_id=(qx, qy, c), device_id_type=MESH))
    return cps


def _halves_d2d_copies(lands, send_sems, recv_sems):
    x, y, c, chips = _place()
    cps = []
    for a, land in enumerate(lands):
        half = land.shape[1] // 2
        mine = pl.ds(pl.multiple_of(c * half, 16), half)
        for j, (qx, qy) in enumerate(chips):
            region = land.at[2 * qx + qy, mine]
            cps.append(pltpu.make_async_remote_copy(
                src_ref=region, dst_ref=region, send_sem=send_sems.at[3 * a + j], recv_sem=recv_sems.at[3 * a + j],
                device_id=(x, y, 1 - c), device_id_type=MESH))
    return cps


def _halves_gather_start(shards):
    n = len(shards)

    def body(*refs):
        srcs, lands = refs[:n], refs[n:2 * n]
        send_sems, recv_sems = refs[2 * n:2 * n + 2]
        for cp in _halves_ici_copies(srcs, lands, send_sems, recv_sems):
            cp.start()
        refs[4 * n + 2][...] = jnp.zeros((8, 128), F32)

    hbm_s = [pltpu.HBM(s.shape, s.dtype) for s in shards]
    hbm_l = [pltpu.HBM((N_SHARD,) + s.shape, s.dtype) for s in shards]
    lands = [pltpu.with_memory_space_constraint(lax.empty((N_SHARD,) + s.shape, s.dtype), pltpu.HBM) for s in shards]
    res = pl.pallas_call(
        body, name="halves_gather_start",
        out_shape=(pltpu.SemaphoreType.DMA((3 * n,)), pltpu.SemaphoreType.DMA((3 * n,)), *hbm_s, *hbm_l,
                   jax.ShapeDtypeStruct((8, 128), F32)),
        in_specs=[_HBM] * (2 * n), out_specs=(_SEM, _SEM, *([_HBM] * (2 * n)), pl.BlockSpec(memory_space=pltpu.VMEM)),
        input_output_aliases={a: 2 + a for a in range(2 * n)},
        compiler_params=pltpu.CompilerParams(has_side_effects=_DATAFLOW),
    )(*[pltpu.with_memory_space_constraint(s, pltpu.HBM) for s in shards], *lands)
    return res[0], res[1], res[2:2 + n], res[2 + n:2 + 2 * n], res[2 + 2 * n]


def _halves_gather_forward(send1, recv1, shards, lands, after):
    n = len(shards)

    def body(*refs):
        srcs, land_refs = refs[:n], refs[n:2 * n]
        s1, r1 = refs[2 * n:2 * n + 2]
        outs = refs[2 * n + 2 + len(after):]
        s2, r2 = outs[0], outs[1]
        for cp in _halves_ici_copies(srcs, land_refs, s1, r1):
            cp.wait_send()
            cp.wait_recv()
        for cp in _halves_d2d_copies(land_refs, s2, r2):
            cp.start()

    hbm_s = [pltpu.HBM(s.shape, s.dtype) for s in shards]
    hbm_l = [pltpu.HBM(l.shape, l.dtype) for l in lands]
    res = pl.pallas_call(
        body, name="halves_gather_forward",
        out_shape=(pltpu.SemaphoreType.DMA((3 * n,)), pltpu.SemaphoreType.DMA((3 * n,)), *hbm_s, *hbm_l),
        in_specs=[_HBM] * (2 * n) + [_SEM, _SEM] + [_ANY] * len(after),
        out_specs=(_SEM, _SEM, *([_HBM] * (2 * n))), input_output_aliases={a: 2 + a for a in range(2 * n)},
        compiler_params=pltpu.CompilerParams(has_side_effects=_DATAFLOW),
    )(*shards, *lands, send1, recv1, *after)
    return res[0], res[1], res[2 + n:2 + 2 * n]


def _halves_gather_wait(send2, recv2, lands):
    n = len(lands)

    def body(*refs):
        land_refs = refs[:n]
        s2, r2 = refs[n:n + 2]
        for cp in _halves_d2d_copies(land_refs, s2, r2):
            cp.wait_send()
            cp.wait_recv()

    hbm_l = [pltpu.HBM(l.shape, l.dtype) for l in lands]
    res = pl.pallas_call(
        body, name="halves_gather_wait", out_shape=tuple(hbm_l),
        in_specs=[_HBM] * n + [_SEM, _SEM], out_specs=tuple([_HBM] * n),
        input_output_aliases={a: a for a in range(n)},
        compiler_params=pltpu.CompilerParams(has_side_effects=_DATAFLOW),
    )(*lands, send2, recv2)
    return list(res)


def _shard_gather_copies(src, land, send_sems, recv_sems):
    x, y, c, chips = _place()
    me = 2 * x + y
    return [pltpu.make_async_remote_copy(
        src_ref=src, dst_ref=land.at[me], send_sem=send_sems.at[j], recv_sem=recv_sems.at[j],
        device_id=(qx, qy, c), device_id_type=MESH) for j, (qx, qy) in enumerate(chips)]


def _shard_gather_start(shard_arr, after):
    def body(src, land, after_ref, send_sems, recv_sems, src_thru, land_thru, token):
        for cp in _shard_gather_copies(src, land, send_sems, recv_sems):
            cp.start()
        token[...] = jnp.zeros_like(token)

    land_shape = (N_SHARD,) + shard_arr.shape
    land = pltpu.with_memory_space_constraint(lax.empty(land_shape, shard_arr.dtype), pltpu.HBM)
    return pl.pallas_call(
        body, name="shard_gather_start",
        out_shape=(pltpu.SemaphoreType.DMA((N_SHARD - 1,)), pltpu.SemaphoreType.DMA((N_SHARD - 1,)),
                   pltpu.HBM(shard_arr.shape, shard_arr.dtype), pltpu.HBM(land_shape, shard_arr.dtype),
                   jax.ShapeDtypeStruct((8, 128), F32)),
        in_specs=[_HBM, _HBM, _ANY], out_specs=(_SEM, _SEM, _HBM, _HBM, pl.BlockSpec(memory_space=pltpu.VMEM)),
        input_output_aliases={0: 2, 1: 3},
        compiler_params=pltpu.CompilerParams(has_side_effects=_DATAFLOW),
    )(pltpu.with_memory_space_constraint(shard_arr, pltpu.HBM), land, after)


def _shard_gather_wait(send_sems, recv_sems, shard_arr, land, after):
    def body(src, land_ref, s_sems, r_sems, after_ref, src_out, land_out):
        for cp in _shard_gather_copies(src, land_ref, s_sems, r_sems):
            cp.wait_send()
            cp.wait_recv()

    return pl.pallas_call(
        body, name="shard_gather_wait",
        out_shape=(pltpu.HBM(shard_arr.shape, shard_arr.dtype), pltpu.HBM(land.shape, land.dtype)),
        in_specs=[_HBM, _HBM, _SEM, _SEM, _ANY], out_specs=(_HBM, _HBM), input_output_aliases={0: 0, 1: 1},
        compiler_params=pltpu.CompilerParams(has_side_effects=_DATAFLOW),
    )(shard_arr, land, send_sems, recv_sems, after)


def _pair_allgather(fulls, tag):
    n = len(fulls)

    def body(*refs):
        outs = refs[n:2 * n]
        send_sems, recv_sems = refs[2 * n:]
        x, y, c, _ = _place()
        sibling = (x, y, 1 - c)
        cps = []
        for a in range(n):
            half = outs[a].shape[0] // 2
            mine = outs[a].at[pl.ds(pl.multiple_of(c * half, 8), half)]
            cp = pltpu.make_async_remote_copy(
                src_ref=mine, dst_ref=mine, send_sem=send_sems.at[a], recv_sem=recv_sems.at[a],
                device_id=sibling, device_id_type=MESH)
            cp.start()
            cps.append(cp)
        for a in range(n):
            half = outs[a].shape[0] // 2
            theirs = outs[a].at[pl.ds(pl.multiple_of((1 - c) * half, 8), half)]
            pltpu.make_async_remote_copy(
                src_ref=theirs, dst_ref=theirs, send_sem=send_sems.at[a], recv_sem=recv_sems.at[a],
                device_id=sibling, device_id_type=MESH).wait_recv()
        for cp in cps:
            cp.wait_send()

    return pl.pallas_call(
        body, name="pair_allgather_" + tag, in_specs=[_ANY] * n, out_specs=[_ANY] * n,
        out_shape=[jax.ShapeDtypeStruct(f.shape, f.dtype) for f in fulls],
        input_output_aliases={a: a for a in range(n)},
        scratch_shapes=[pltpu.SemaphoreType.DMA((n,)), pltpu.SemaphoreType.DMA((n,))],
        compiler_params=pltpu.CompilerParams(has_side_effects=True),
    )(*fulls)


def _allgather_small(slab, after):
    def body(s_ref, after_ref, out_ref, send_sems, recv_sems):
        x, y, c, _ = _place()
        me = 4 * x + 2 * y + c
        out_ref[me] = s_ref[...]
        cps = []
        for mask in range(1, N_DEV):
            peer = (x ^ (mask >> 2), y ^ ((mask >> 1) & 1), c ^ (mask & 1))
            cp = pltpu.make_async_remote_copy(
                src_ref=s_ref, dst_ref=out_ref.at[me], send_sem=send_sems.at[mask - 1], recv_sem=recv_sems.at[mask - 1],
                device_id=peer, device_id_type=MESH)
            cp.start()
            cps.append(cp)
        for mask in range(1, N_DEV):
            peer = (x ^ (mask >> 2), y ^ ((mask >> 1) & 1), c ^ (mask & 1))
            dst = out_ref.at[4 * peer[0] + 2 * peer[1] + peer[2]]
            pltpu.make_async_remote_copy(
                src_ref=dst, dst_ref=dst, send_sem=send_sems.at[mask - 1], recv_sem=recv_sems.at[mask - 1],
                device_id=peer, device_id_type=MESH).wait_recv()
        for cp in cps:
            cp.wait_send()

    vm = pl.BlockSpec(memory_space=pltpu.VMEM)
    return pl.pallas_call(
        body, name="allgather_small", in_specs=[vm, _ANY], out_specs=vm,
        out_shape=jax.ShapeDtypeStruct((N_DEV,) + slab.shape, slab.dtype),
        scratch_shapes=[pltpu.SemaphoreType.DMA((N_DEV - 1,)), pltpu.SemaphoreType.DMA((N_DEV - 1,))],
        compiler_params=pltpu.CompilerParams(has_side_effects=True),
    )(slab, after)


def _pack_b(w_mem_kv, w_br_dn, w_br_sb, w_br_mem, w_out):
    return jnp.concatenate([w_mem_kv.reshape(128, D_MODEL), w_br_dn, w_br_sb, w_br_mem.reshape(64, D_MODEL), w_out],
                           axis=0)


def _conv_slab(conv_w):
    return jnp.pad(conv_w.reshape(3, D_MODEL), ((0, 29), (0, 0)))


def _unpack_b(slab):
    return (slab[B_MEMKV:B_BRDN].reshape(1, 256, 512), slab[B_BRDN:B_BRSB].reshape(1, 256, D_MODEL),
            slab[B_BRSB:B_BRMEM].reshape(1, 256, D_MODEL), slab[B_BRMEM:B_OUT].reshape(1, 256, 256),
            slab[B_OUT:B_CONV].reshape(1, 256, D_MODEL))


def _conv_rows(conv_full):
    return conv_full.reshape(4 * CONV_BLOCKS, 128)


def _conv_shard_rows(conv_shard, shard):
    own = CONV_BLOCKS // N_SHARD
    blocks = lax.dynamic_update_slice(jnp.zeros((4, CONV_BLOCKS, 128), F32), conv_shard.reshape(4, own, 128),
                                      (0, own * shard, 0))
    return blocks.reshape(4 * CONV_BLOCKS, 128)


def _conv_shard_of(rows, shard):
    own = CONV_BLOCKS // N_SHARD
    blocks = lax.dynamic_slice(rows.reshape(4, CONV_BLOCKS, 128), (0, own * shard, 0), (4, own, 128))
    return blocks.reshape(1, 4, own * 128)


def _pack_small(norm_g, mem_norm_g, final_g, dn_norm_g, a_log, dt_bias, conv_rows, loss=None):
    def row(v):
        v = v.reshape(1, -1).astype(F32)
        return jnp.pad(v, ((0, 0), (0, 128 - v.shape[1])))

    loss_row = row(jnp.zeros((1,), F32) if loss is None else jnp.reshape(loss, (1,)))
    rid = lax.broadcasted_iota(jnp.int32, (8, 128), 0) + S_DNNORM
    tile = jnp.where(rid == S_DNNORM, dn_norm_g.reshape(1, 128), jnp.where(
        rid == S_ALOG, row(a_log), jnp.where(rid == S_DTB, row(dt_bias), jnp.where(rid == S_LOSS, loss_row, 0.0))))
    return jnp.concatenate([norm_g.reshape(8, 128), mem_norm_g.reshape(8, 128), final_g.reshape(8, 128), tile,
                            conv_rows], axis=0)


def _unpack_small(slab, shard):
    return (slab[S_NORM:S_NORM + 8].reshape(1, D_MODEL), slab[S_MEMNORM:S_MEMNORM + 8].reshape(1, D_MODEL),
            slab[S_FINAL:S_FINAL + 8].reshape(D_MODEL), slab[S_DNNORM].reshape(1, 128),
            slab[S_ALOG, :N_HEADS].reshape(1, N_HEADS), slab[S_DTB, :N_HEADS].reshape(1, N_HEADS),
            _conv_shard_of(slab[S_CONV:], shard))


def _windows_to_w_r(win):
    b = 128
    s0, s1, s2, s3 = win[0], win[1], win[2], win[3]
    e1, e2, e3 = WIN_START[1] * b, WIN_START[2] * b, WIN_START[3] * b
    n1, n2 = e2 - e1, e3 - e2
    return jnp.concatenate([
        s0[:, :e1], s0[:, e1:e1 + b] + s1[:, :b],
        s1[:, b:n1], s1[:, n1:n1 + b] + s2[:, :b],
        s2[:, b:n2], s2[:, n2:n2 + b] + s3[:, :b],
        s3[:, b:], s1[:, _S1_BA_POS:], jnp.zeros((win.shape[1], W_R - C_BA - b), win.dtype)], axis=1)


def _dproj_windows(dproj_r):
    b = 128
    pieces = []
    for s in range(N_SHARD):
        lo = WIN_START[s] * b
        if s == 1:
            pieces += [dproj_r[:, lo:lo + _S1_BA_POS], dproj_r[:, C_BA:C_BA + b]]
        else:
            pieces.append(dproj_r[:, lo:lo + SHARD_PAD])
    return jnp.concatenate(pieces, axis=1)


def _local_step(x, mem, tgt, norm_g, mem_norm_g, w_r, w_sh, conv_w, a_log, dt_bias, dn_norm_g, proj_weights, final_g,
                on_early=None, after_gather=None, h=None):
    t = x.shape[0]
    final_row = final_g.reshape(1, D_MODEL)
    lanes_8_16 = ((0, 0), (N_HEADS, 128 - 2 * N_HEADS))
    alog_row = jnp.pad(a_log.reshape(1, N_HEADS), lanes_8_16)
    dtb_row = jnp.pad(dt_bias.reshape(1, N_HEADS), lanes_8_16)

    if h is None:
        h = _rmsnorm_fwd(x, norm_g, "norm_fwd")
    proj = _mm(h, w_r, "nn", "in_proj", after=after_gather, tm_max=2048)
    qkv = _dn_prep_fwd(proj, conv_w)
    beta_t, g_t = _dn_gate_fwd(proj, alog_row, dtb_row)
    dn_u, dn_w, dn_qd, dn_kd, dn_a, tinv_all, dn_el = _dn_intra_fwd(qkv, beta_t, g_t)
    o_dn, dn_vn, s_all = _dn_scan_fwd(dn_u, dn_w, dn_qd, dn_kd, dn_a, dn_el)
    o_dn_g = _dn_post_fwd(o_dn, proj, dn_norm_g)
    o_sb, o_sb_g, sb_l = _sb_fwd(proj)
    w_mem_kv, w_br_dn, w_br_sb, w_br_mem, w_out = proj_weights(o_sb_g)
    mem_n = _rmsnorm_fwd(mem, mem_norm_g, "mem_norm_fwd")
    mkv = _mm(mem_n, w_mem_kv, "nn", "mem_kv")
    o_m, o_m_g = _mem_fwd(proj, mkv)
    y_dn = _mm(o_dn_g, w_br_dn, "nn", "br_dn", out_dtype=BF16)
    y_sb = _mm(o_sb_g, w_br_sb, "nn", "br_sb", out_dtype=BF16)
    y_m = _mm(o_m_g, w_br_mem, "nn", "br_mem", out_dtype=BF16)
    merged = _merge_fwd(proj, y_dn, y_sb, y_m)
    mo = _mm(merged, w_out, "nn", "out_proj")
    d_out, d_out_b, loss_row, g_final = _final_loss(x, mo, final_row, tgt)

    g_w_out = _mm(merged, d_out_b, "tn", "g_w_out", out_dtype=BF16)
    d_merged = _mm(d_out_b, w_out, "nt", "d_merged")
    dy_dn, dy_sb, dy_m, dg1, dg2, dg3 = _merge_bwd(proj, y_dn, y_sb, y_m, d_merged)
    g_w_br_dn = _mm(o_dn_g, dy_dn, "tn", "g_w_br_dn", out_dtype=BF16)
    g_w_br_sb = _mm(o_sb_g, dy_sb, "tn", "g_w_br_sb", out_dtype=BF16)
    g_w_br_mem = _mm(o_m_g, dy_m, "tn", "g_w_br_mem", out_dtype=BF16)
    d_o_dn_g = _mm(dy_dn, w_br_dn, "nt", "d_o_dn")
    d_o_sb_g = _mm(dy_sb, w_br_sb, "nt", "d_o_sb")
    d_o_m_g = _mm(dy_m, w_br_mem, "nt", "d_o_mem")

    d_mq, d_mz, d_mkv = _mem_bwd(proj, mkv, o_m, d_o_m_g)
    d_mkv_b = _cast_bf16(d_mkv, "cast_dmkv")
    g_w_mem_kv = _mm(mem_n, d_mkv_b, "tn", "g_w_mem_kv", out_dtype=BF16)
    d_mem_n = _mm(d_mkv_b, w_mem_kv, "nt", "d_mem_n")
    _, g_mem_norm = _rmsnorm_bwd(mem, mem_norm_g, d_mem_n, jnp.zeros_like(mem), "mem_norm_bwd")

    early = dict(w_mem_kv=g_w_mem_kv, w_br_dn=g_w_br_dn, w_br_sb=g_w_br_sb, w_br_mem=g_w_br_mem, w_out=g_w_out)
    after_early = on_early(early) if on_early is not None else None

    d_sq, d_sk, d_sv, d_sz = _sb_bwd(proj, o_sb, sb_l, d_o_sb_g, after=after_early)

    d_o_dn, d_dnz, g_dn_norm = _dn_post_bwd(o_dn, proj, dn_norm_g, d_o_dn_g)
    d_vnew, d_kd, d_qd, d_w, d_el = _dn_scan_bwd(dn_w, dn_qd, dn_kd, dn_a, dn_el, dn_vn, s_all, d_o_dn)
    d_qn, d_kn, d_vn, dbeta_t, dg_t = _dn_intra_bwd(qkv, beta_t, g_t, tinv_all, dn_vn, d_o_dn, d_vnew, d_kd, d_qd, d_w, d_el)
    d_conv_in, g_conv = _dn_prep_bwd(proj, conv_w, d_qn, d_kn, d_vn)
    d_ba, g_alog_row, g_dtb_row = _dn_gate_bwd(proj, alog_row, dtb_row, dbeta_t, dg_t)

    dproj_sh = _dproj_windows(
        jnp.concatenate([d_conv_in, d_dnz, d_sq, d_sk, d_sv, d_sz, d_mq, d_mz, dg1, dg2, dg3, d_ba], axis=1))
    g_w_sh = _mm(h, dproj_sh, "tn", "g_w_in", out_dtype=BF16, out_shards=N_SHARD, tn_max=1024)
    def input_grad(after=None):
        dh = _mm(dproj_sh, w_sh, "nt", "d_h", after=after, tm_max=2048, tn_max=1024)
        grad_x, g_norm = _rmsnorm_bwd(x, norm_g, dh, d_out, "norm_bwd")
        small = dict(norm_g=g_norm, mem_norm_g=g_mem_norm, final_g=g_final, dn_norm_g=g_dn_norm,
                     a_log=g_alog_row[:, N_HEADS:2 * N_HEADS], dt_bias=g_dtb_row[:, N_HEADS:2 * N_HEADS],
                     conv_w=g_conv)
        return grad_x, small

    return loss_row[0, 0], early, g_w_sh, input_grad


def _reduce_scatter_start(grads, tag):
    c = lax.axis_index("c")
    core = jnp.reshape(c, (1,)).astype(jnp.int32)
    recv = _pair_reduce_send(grads, tag)
    parts = [_pair_add(g, r, core, "pair_add_" + tag) for g, r in zip(grads, recv)]
    return _chip_exchange_start(parts, tag)


def _reduce_scatter_finish(handle, after, tag):
    send_sems, recv_sems, parts, lands, _ = handle
    x, y, c = lax.axis_index("x"), lax.axis_index("y"), lax.axis_index("c")
    place = jnp.stack([2 * x + y, c]).astype(jnp.int32)
    parts, by_chip = _chip_exchange_wait(send_sems, recv_sems, parts, lands, after, tag)
    fulls = [_chip_sum(p, b, place, "chip_sum_" + tag) for p, b in zip(parts, by_chip)]
    return _pair_allgather(fulls, tag)


def kernel(x, mem, norm_g, mem_norm_g, w_in, conv_w, a_log, dt_bias, dn_norm_g, w_mem_kv, w_br_dn, w_br_sb, w_br_mem, w_out, final_g, loss_target, m_norm_g, m_mem_norm_g, m_w_in, m_conv_w, m_a_log, m_dt_bias, m_dn_norm_g, m_w_mem_kv, m_w_br_dn, m_w_br_sb, m_w_br_mem, m_w_out, m_final_g, v_norm_g, v_mem_norm_g, v_w_in, v_conv_w, v_a_log, v_dt_bias, v_dn_norm_g, v_w_mem_kv, v_w_br_dn, v_w_br_sb, v_w_br_mem, v_w_out, v_final_g):
    w_a = w_in[0]
    w_b = _pack_b(w_mem_kv[0], w_br_dn[0], w_br_sb[0], w_br_mem[0], w_out[0])
    m_b = _pack_b(m_w_mem_kv[0], m_w_br_dn[0], m_w_br_sb[0], m_w_br_mem[0], m_w_out[0])
    v_b = _pack_b(v_w_mem_kv[0], v_w_br_dn[0], v_w_br_sb[0], v_w_br_mem[0], v_w_out[0])

    shard_idx = 2 * lax.axis_index("x") + lax.axis_index("y")
    shard = jnp.reshape(shard_idx, (1,)).astype(jnp.int32)
    own = [_cast_to_window(w_a, shard, "cast_w_in"), _cast_bf16(_conv_slab(conv_w[0]), "cast_conv")]
    send1, recv1, own, lands, token = _halves_gather_start(own)
    h = _rmsnorm_fwd(x[0], norm_g, "norm_fwd", after=token)
    w_b_bf = _cast_bf16(w_b, "cast_w_b")
    send2, recv2, lands = _halves_gather_forward(send1, recv1, own, lands,
                                                 after=[h, w_b_bf, m_w_in[0], v_w_in[0], m_b, v_b])
    lands = _halves_gather_wait(send2, recv2, lands)
    ga, g_conv = [lax.dynamic_update_slice(land, o[None], (shard_idx, 0, 0)) for land, o in zip(lands, own)]
    w_r = _windows_to_w_r(ga)
    f_conv = g_conv[:, :3].reshape(N_SHARD, 4, 768).transpose(1, 0, 2).reshape(4, 3 * D_MODEL).astype(F32)
    b_flight = _shard_gather_start(w_b_bf, after=ga)

    def proj_weights(after):
        own, land = _shard_gather_wait(b_flight[0], b_flight[1], b_flight[2], b_flight[3], after)
        gb = lax.dynamic_update_slice(land, own[None], (shard_idx, 0, 0))
        return (gb[:, B_MEMKV:B_BRDN].reshape(N_SHARD * 256, 512),
                gb[:, B_BRDN:B_BRSB].reshape(N_SHARD * 256, D_MODEL),
                gb[:, B_BRSB:B_BRMEM].reshape(N_SHARD * 256, D_MODEL),
                gb[:, B_BRMEM:B_OUT].reshape(N_SHARD, 256, 256).transpose(1, 0, 2).reshape(256, D_MODEL),
                gb[:, B_OUT:B_CONV].reshape(N_SHARD * 256, D_MODEL))

    flights = {}

    def on_early(grads):
        g_b = jnp.concatenate([
            grads["w_mem_kv"].reshape(N_SHARD, 128, D_MODEL), grads["w_br_dn"].reshape(N_SHARD, 256, D_MODEL),
            grads["w_br_sb"].reshape(N_SHARD, 256, D_MODEL),
            grads["w_br_mem"].reshape(256, N_SHARD, 256).transpose(1, 0, 2).reshape(N_SHARD, 64, D_MODEL),
            grads["w_out"].reshape(N_SHARD, 256, D_MODEL)], axis=1).astype(BF16)
        flights["b"] = _reduce_scatter_start([g_b], "b")
        return flights["b"][4]

    loss, _, g_w_sh, input_grad = _local_step(
        x[0], mem[0], loss_target[0], norm_g, mem_norm_g, w_r, ga, f_conv, a_log, dt_bias, dn_norm_g,
        proj_weights, final_g, on_early=on_early, after_gather=b_flight[4], h=h)
    flights["a"] = _reduce_scatter_start([g_w_sh], "a")
    grad_x, small = input_grad(after=flights["a"][4])

    part = _pack_small(small["norm_g"], small["mem_norm_g"], small["final_g"], small["dn_norm_g"],
                       small["a_log"], small["dt_bias"], _conv_rows(small["conv_w"]), loss)
    w_s = _pack_small(norm_g, mem_norm_g, final_g, dn_norm_g, a_log, dt_bias, _conv_shard_rows(conv_w[0], shard_idx))
    m_s = _pack_small(m_norm_g, m_mem_norm_g, m_final_g, m_dn_norm_g, m_a_log, m_dt_bias,
                      _conv_shard_rows(m_conv_w[0], shard_idx))
    v_s = _pack_small(v_norm_g, v_mem_norm_g, v_final_g, v_dn_norm_g, v_a_log, v_dt_bias,
                      _conv_shard_rows(v_conv_w[0], shard_idx))
    (gs_b,) = _reduce_scatter_finish(flights["b"], after=grad_x, tag="b")
    gr_b, d_b, nm_b, nv_b = _adamw(w_b, gs_b, m_b, v_b, "adamw_b")
    g_s, d_s, nm_s, nv_s = _small_update(_allgather_small(part, after=d_b), w_s, m_s, v_s)

    (gs_in,) = _reduce_scatter_finish(flights["a"], after=g_s, tag="a")
    gr_in, d_in, nm_in, nv_in = _adamw_window(w_a, gs_in, m_w_in[0], v_w_in[0], shard, "adamw_w_in")

    def assemble(slab_small, a_in, slab_b):
        s_norm, s_memnorm, s_final, s_dnnorm, s_alog, s_dtb, b_conv = _unpack_small(slab_small, shard_idx)
        b_memkv, b_brdn, b_brsb, b_brmem, b_out = _unpack_b(slab_b)
        return [s_norm, s_memnorm, a_in.reshape(1, D_MODEL, IN_WIDTH // N_SHARD), b_conv, s_alog, s_dtb, s_dnnorm,
                b_memkv, b_brdn, b_brsb, b_brmem, b_out, s_final]

    outs = [g_s[S_LOSS, 0], grad_x.reshape(1, -1, D_MODEL)]
    outs += assemble(g_s, gr_in, gr_b)
    outs += assemble(d_s, d_in, d_b)
    outs += assemble(nm_s, nm_in, nm_b)
    outs += assemble(nv_s, nv_in, nv_b)
    return tuple(outs)
```

```python
import math

import jax
import jax.numpy as jnp
from jax import lax
from jax.experimental import pallas as pl
from jax.experimental.pallas import tpu as pltpu

F32 = jnp.float32
BF16 = jnp.bfloat16
MESH = pl.DeviceIdType.MESH

D_MODEL = 1024
N_HEADS = 8
D_HEAD = 128
DN_CHUNK = 64
DN_GROUP = 32
DN_SCAN_GROUP = 4
SB_BLOCK = 256
SB_HEADS_PER_STEP = 2
SB_QBLOCK = 256
MEM_HEADS = 4
MEM_DH = 64
MEM_W = MEM_HEADS * MEM_DH
NORM_EPS = 1e-6
IN_WIDTH = 11792
N_SHARD = 4
SHARD_W = IN_WIDTH // N_SHARD
SHARD_PAD = 3072
N_DEV = 8

C_DNZ = 3072
C_SBQ = 4096
C_SBZ = 7168
C_MQ = 8192
C_MZ = 8448
C_GATES = 8704
C_BA = 11776
W_R = 12288

ADAM_LR = 0.001
ADAM_B1 = 0.9
ADAM_B2 = 0.999
ADAM_EPS = 1e-08
ADAM_WD = 0.01
ADAM_STEP = 10

VMEM_LIMIT = 56 * 1024 * 1024

B_MEMKV, B_BRDN, B_BRSB, B_BRMEM, B_OUT, B_CONV = 0, 128, 384, 640, 704, 960
S_NORM, S_MEMNORM, S_FINAL, S_DNNORM, S_ALOG, S_DTB, S_LOSS, S_CONV, S_ROWS = 0, 8, 16, 24, 25, 26, 27, 32, 128
CONV_BLOCKS = 3 * D_MODEL // 128


def _cp(**kw):
    return pltpu.CompilerParams(vmem_limit_bytes=VMEM_LIMIT, **kw)


def _dot(a, b, dims):
    lead = a.ndim - 2
    ca, cb = {"nn": (1, 0), "nt": (1, 1), "tn": (0, 0)}[dims]
    batch = tuple(range(lead))
    return lax.dot_general(a, b, (((ca + lead,), (cb + lead,)), (batch, batch)), preferred_element_type=F32)


def _chunks(x):
    return x.reshape(x.shape[0] // DN_CHUNK, DN_CHUNK, x.shape[1])


def _unchunk(x):
    return x.reshape(x.shape[0] * x.shape[1], x.shape[2])


def _bdot(a, b, dims):
    return _dot(a.astype(BF16), b.astype(BF16), dims)


def _split(a):
    hi = a.astype(BF16)
    return hi, (a - hi.astype(F32)).astype(BF16)


def _dot3(a, b, dims):
    a1, a2 = _split(a)
    b1, b2 = _split(b)
    return _dot(a1, b1, dims) + (_dot(a1, b2, dims) + _dot(a2, b1, dims))


def _ones_dot(a, ones_bf16):
    out = _dot(a.reshape(-1, a.shape[-1]).astype(BF16), ones_bf16, "nn")
    return out.reshape(a.shape[:-1] + (ones_bf16.shape[1],))


def _sigmoid(x):
    return 1.0 / (1.0 + jnp.exp(-x))


def _log1p_small(u):
    return jnp.where(u < 1e-2, u * (1.0 - u * (0.5 - u * (1.0 / 3.0))), jnp.log(1.0 + u))


def _pick(dim, cands):
    for c in cands:
        if dim % c == 0:
            return c
    return dim


def _mm(a, b, dims, name, out_dtype=F32, out_shards=1, after=None, tm_max=1024, tn_max=512):
    ta, tb = dims[0] == "t", dims[1] == "t"
    m, k = (a.shape[1], a.shape[0]) if ta else a.shape
    b_shards = b.shape[0] if b.ndim == 3 else 1
    n = b.shape[-2] if tb else b.shape[-1]
    tm = _pick(m, (tm_max, 1024, 512, 256))
    tn = _pick(n // out_shards, (tn_max, 512, 384, 256, 128))
    tk = _pick(k // b_shards, (2048, 1024, 512, 384, 256))
    nk = k // tk

    def body(a_ref, b_ref, *rest):
        if nk == 1:
            rest[-1][...] = _bdot(a_ref[...], b_ref[...], dims).astype(out_dtype)
            return
        o_ref, acc_ref = rest[-2:]
        kk = pl.program_id(2)

        @pl.when(kk == 0)
        def _():
            acc_ref[...] = jnp.zeros_like(acc_ref)

        acc_ref[...] += _bdot(a_ref[...], b_ref[...], dims)

        @pl.when(kk == nk - 1)
        def _():
            o_ref[...] = acc_ref[...].astype(out_dtype)

    a_spec = pl.BlockSpec((tk, tm), lambda i, j, q: (q, i)) if ta else pl.BlockSpec((tm, tk), lambda i, j, q: (i, q))
    if b_shards > 1:
        per_k = k // b_shards // tk
        b_spec = pl.BlockSpec((None, tn, tk), lambda i, j, q: (q // per_k, j, q % per_k))
    else:
        b_spec = pl.BlockSpec((tn, tk), lambda i, j, q: (j, q)) if tb else pl.BlockSpec((tk, tn), lambda i, j, q: (q, j))
    if out_shards > 1:
        per_n = n // out_shards // tn
        out_spec = pl.BlockSpec((None, tm, tn), lambda i, j, q: (j // per_n, i, j % per_n))
        out_shape = jax.ShapeDtypeStruct((out_shards, m, n // out_shards), out_dtype)
    else:
        out_spec = pl.BlockSpec((tm, tn), lambda i, j, q: (i, j))
        out_shape = jax.ShapeDtypeStruct((m, n), out_dtype)
    extra_specs, extra = [], []
    if after is not None:
        extra_specs, extra = [pl.BlockSpec(after.shape, lambda i, j, q: (0, 0))], [after]
    return pl.pallas_call(
        body, name=name, grid=(m // tm, n // tn, nk),
        in_specs=[a_spec, b_spec] + extra_specs, out_specs=out_spec, out_shape=out_shape,
        scratch_shapes=[pltpu.VMEM((tm, tn), F32)] if nk > 1 else [],
        compiler_params=_cp(dimension_semantics=("parallel", "parallel", "arbitrary")),
    )(a, b, *extra)


def _rmsnorm_fwd(x, g, name, after=None):
    t, d = x.shape
    tb = _pick(t, (512, 256))

    def body(x_ref, g_ref, *rest):
        xv = x_ref[...]
        r = lax.rsqrt(jnp.mean(xv * xv, axis=-1, keepdims=True) + NORM_EPS)
        rest[-1][...] = ((xv * r) * g_ref[...]).astype(BF16)

    extra_specs, extra = [], []
    if after is not None:
        extra_specs, extra = [pl.BlockSpec(after.shape, lambda i: (0, 0))], [after]
    return pl.pallas_call(
        body, name=name, grid=(t // tb,),
        in_specs=[pl.BlockSpec((tb, d), lambda i: (i, 0)), pl.BlockSpec((1, d), lambda i: (0, 0))] + extra_specs,
        out_specs=pl.BlockSpec((tb, d), lambda i: (i, 0)),
        out_shape=jax.ShapeDtypeStruct((t, d), BF16), compiler_params=_cp(),
    )(x, g, *extra)


def _rmsnorm_bwd(x, g, dh, resid, name):
    t, d = x.shape
    tb = _pick(t, (512, 256))

    def body(x_ref, g_ref, dh_ref, r_ref, dx_ref, dg_ref):
        @pl.when(pl.program_id(0) == 0)
        def _():
            dg_ref[...] = jnp.zeros_like(dg_ref)

        xv = x_ref[...]
        r = lax.rsqrt(jnp.mean(xv * xv, axis=-1, keepdims=True) + NORM_EPS)
        xhat = xv * r
        dhv = dh_ref[...]
        dg_ref[...] += jnp.sum(dhv * xhat, axis=0, keepdims=True)
        dxh = dhv * g_ref[...]
        dx_ref[...] = r_ref[...] + r * (dxh - xhat * jnp.mean(dxh * xhat, axis=-1, keepdims=True))

    row = pl.BlockSpec((tb, d), lambda i: (i, 0))
    vec = pl.BlockSpec((1, d), lambda i: (0, 0))
    return pl.pallas_call(
        body, name=name, grid=(t // tb,), in_specs=[row, vec, row, row], out_specs=[row, vec],
        out_shape=[jax.ShapeDtypeStruct((t, d), F32), jax.ShapeDtypeStruct((1, d), F32)], compiler_params=_cp(),
    )(x, g, dh, resid)


def _conv_silu(xv, w, row):
    y = xv * w[3:4, :]
    for s in (1, 2, 3):
        xs = jnp.where(row >= s, pltpu.roll(xv, s, 0), 0.0)
        y = y + xs * w[3 - s:4 - s, :]
    sig = _sigmoid(y)
    return y, sig, y * sig


def _dn_prep_fwd(proj, conv_w):
    t = proj.shape[0]

    def body(p_ref, w_ref, o_ref):
        j = pl.program_id(0)
        xv = p_ref[...]
        row = lax.broadcasted_iota(jnp.int32, xv.shape, 0)
        _, _, a = _conv_silu(xv, w_ref[...], row)
        inv = lax.rsqrt(jnp.sum(a * a, axis=-1, keepdims=True) + NORM_EPS)
        scale = jnp.where(j < N_HEADS, D_HEAD ** -0.5, 1.0)
        normed = jnp.where(j < 2 * N_HEADS, 1.0, 0.0)
        o_ref[...] = a * (normed * (inv * scale) + (1.0 - normed))

    return pl.pallas_call(
        body, name="dn_prep_fwd", grid=(3 * N_HEADS,),
        in_specs=[pl.BlockSpec((t, D_HEAD), lambda j: (0, j)), pl.BlockSpec((4, D_HEAD), lambda j: (0, j))],
        out_specs=pl.BlockSpec((t, D_HEAD), lambda j: (0, j)),
        out_shape=jax.ShapeDtypeStruct((t, 3 * D_MODEL), F32), compiler_params=_cp(),
    )(proj, conv_w)


def _dn_prep_bwd(proj, conv_w, dq, dk, dv):
    t = proj.shape[0]

    def body(p_ref, w_ref, dq_ref, dk_ref, dv_ref, dp_ref, dw_ref):
        j = pl.program_id(0)
        xv = p_ref[...]
        w = w_ref[...]
        row = lax.broadcasted_iota(jnp.int32, xv.shape, 0)
        y, s, a = _conv_silu(xv, w, row)
        part = jnp.zeros(xv.shape, jnp.int32) + j // N_HEADS
        dn = jnp.where(part == 0, dq_ref[...], jnp.where(part == 1, dk_ref[...], dv_ref[...]))
        inv = lax.rsqrt(jnp.sum(a * a, axis=-1, keepdims=True) + NORM_EPS)
        scale = jnp.where(j < N_HEADS, D_HEAD ** -0.5, 1.0)
        ds = dn * scale
        da_norm = inv * ds - a * (inv * inv * inv) * jnp.sum(ds * a, axis=-1, keepdims=True)
        normed = jnp.where(j < 2 * N_HEADS, 1.0, 0.0)
        da = normed * da_norm + (1.0 - normed) * dn
        dy = da * (s * (1.0 + y * (1.0 - s)))
        dx = dy * w[3:4, :]
        dw_ref[3:4, :] = jnp.sum(dy * xv, axis=0, keepdims=True)
        for sft in (1, 2, 3):
            xs = jnp.where(row >= sft, pltpu.roll(xv, sft, 0), 0.0)
            dw_ref[3 - sft:4 - sft, :] = jnp.sum(dy * xs, axis=0, keepdims=True)
            dys = jnp.where(row < t - sft, pltpu.roll(dy, t - sft, 0), 0.0)
            dx = dx + dys * w[3 - sft:4 - sft, :]
        dp_ref[...] = dx.astype(BF16)

    blk = pl.BlockSpec((t, D_HEAD), lambda j: (0, j))
    wblk = pl.BlockSpec((4, D_HEAD), lambda j: (0, j))

    def grad(part):
        return pl.BlockSpec((t, D_HEAD), lambda j: (0, jnp.clip(j - part * N_HEADS, 0, N_HEADS - 1)))

    return pl.pallas_call(
        body, name="dn_prep_bwd", grid=(3 * N_HEADS,), in_specs=[blk, wblk, grad(0), grad(1), grad(2)],
        out_specs=[blk, wblk],
        out_shape=[jax.ShapeDtypeStruct((t, 3 * D_MODEL), BF16), jax.ShapeDtypeStruct((4, 3 * D_MODEL), F32)],
        compiler_params=_cp(),
    )(proj, conv_w, dq, dk, dv)


def _softplus_parts(xv):
    e = jnp.exp(-jnp.abs(xv))
    return jnp.maximum(xv, 0.0) + _log1p_small(e)


def _chunk_scan(v, row, reverse):
    t = v.shape[0]
    pos = row & (DN_CHUNK - 1)
    s = 1
    while s < DN_CHUNK:
        if reverse:
            v = v + jnp.where(pos < DN_CHUNK - s, pltpu.roll(v, t - s, 0), 0.0)
        else:
            v = v + jnp.where(pos >= s, pltpu.roll(v, s, 0), 0.0)
        s *= 2
    return v


def _dn_gate_fwd(proj, alog_row, dtb_row):
    t = proj.shape[0]

    def body(p_ref, al_ref, dt_ref, b_ref, g_ref):
        p = p_ref[...]
        row = lax.broadcasted_iota(jnp.int32, p.shape, 0)
        b_ref[...] = _sigmoid(p)
        g = -jnp.exp(al_ref[...]) * _softplus_parts(p + dt_ref[...])
        g_ref[...] = _chunk_scan(g, row, reverse=False)

    blk = pl.BlockSpec((t, 128), lambda i: (0, C_BA // 128))
    vec = pl.BlockSpec((1, 128), lambda i: (0, 0))
    out = pl.BlockSpec((t, 128), lambda i: (0, 0))
    return pl.pallas_call(
        body, name="dn_gate_fwd", grid=(1,), in_specs=[blk, vec, vec], out_specs=[out, out],
        out_shape=[jax.ShapeDtypeStruct((t, 128), F32)] * 2, compiler_params=_cp(),
    )(proj, alog_row, dtb_row)


def _dn_gate_bwd(proj, alog_row, dtb_row, dbeta, dgc):
    t = proj.shape[0]

    def body(p_ref, al_ref, dt_ref, db_ref, dg_ref, dp_ref, dal_ref, ddt_ref):
        p = p_ref[...]
        row = lax.broadcasted_iota(jnp.int32, p.shape, 0)
        lane = lax.broadcasted_iota(jnp.int32, p.shape, 1)
        s = _sigmoid(p)
        d_b = db_ref[...] * s * (1.0 - s)
        dg = _chunk_scan(dg_ref[...], row, reverse=True)
        xa = p + dt_ref[...]
        ea = jnp.exp(al_ref[...])
        g = -ea * _softplus_parts(xa)
        d_a = dg * (-ea) * _sigmoid(xa)
        dp_ref[...] = jnp.where(lane < N_HEADS, d_b, jnp.where(lane < 2 * N_HEADS, d_a, 0.0)).astype(BF16)
        dal_ref[...] = jnp.sum(dg * g, axis=0, keepdims=True)
        ddt_ref[...] = jnp.sum(d_a, axis=0, keepdims=True)

    blk = pl.BlockSpec((t, 128), lambda i: (0, C_BA // 128))
    vec = pl.BlockSpec((1, 128), lambda i: (0, 0))
    full = pl.BlockSpec((t, 128), lambda i: (0, 0))
    return pl.pallas_call(
        body, name="dn_gate_bwd", grid=(1,), in_specs=[blk, vec, vec, full, full], out_specs=[full, vec, vec],
        out_shape=[jax.ShapeDtypeStruct((t, 128), BF16), jax.ShapeDtypeStruct((1, 128), F32),
                   jax.ShapeDtypeStruct((1, 128), F32)], compiler_params=_cp(),
    )(proj, alog_row, dtb_row, dbeta, dgc)


def _col_to_row(col, eye):
    return jnp.sum(jnp.where(eye, col, 0.0), axis=-2, keepdims=True)


def _row_to_col(rowv, eye):
    return jnp.sum(jnp.where(eye, rowv, 0.0), axis=-1, keepdims=True)


def _tri_inverse(m, ri, ci):
    eye = (ri == ci).astype(F32)
    b16 = (ri >> 4) == (ci >> 4)
    b32 = (ri >> 5) == (ci >> 5)
    m1 = jnp.where(b16, m, 0.0)
    x = eye - m1
    p = _dot3(m1, m1, "nn")
    x = x + _dot3(x, p, "nn")
    p = _dot3(p, p, "nn")
    x = x + _dot3(x, p, "nn")
    p = _dot3(p, p, "nn")
    x = x + _dot3(x, p, "nn")
    c1 = jnp.where(jnp.logical_and(b32, jnp.logical_not(b16)), m, 0.0)
    x = x - _dot3(_dot3(x, c1, "nn"), x, "nn")
    c2 = jnp.where(b32, 0.0, m)
    x = x - _dot3(_dot3(x, c2, "nn"), x, "nn")
    return x


def _dn_chunk_common(q, k, gc, ri, ci):
    eye = ri == ci
    g_row = _col_to_row(gc, eye)
    diff = jnp.minimum(gc - g_row, 0.0)
    gam = jnp.where(ri >= ci, jnp.exp(diff), 0.0)
    kk = _bdot(k, k, "nt")
    qk = _bdot(q, k, "nt")
    rcol = lax.broadcasted_iota(jnp.int32, gc.shape, gc.ndim - 2)
    last = jnp.sum(jnp.where(rcol == DN_CHUNK - 1, gc, 0.0), axis=-2, keepdims=True)
    e_g = jnp.exp(gc)
    dec = jnp.exp(last - gc)
    return eye, gam, kk, qk, last, e_g, dec, rcol


def _dn_specs(t, rows_blk):
    def head(off):
        return pl.BlockSpec((rows_blk, D_HEAD), lambda g, h: (g, off + h))

    lanes = pl.BlockSpec((rows_blk, 128), lambda g, h: (g, 0))
    hm = pl.BlockSpec((None, rows_blk, D_HEAD), lambda g, h: (h, g, 0))
    sq = pl.BlockSpec((1, rows_blk, DN_CHUNK), lambda g, h: (h, g, 0))
    tile = pl.BlockSpec((1, rows_blk // DN_CHUNK, 8, 128), lambda g, h: (h, g, 0, 0))
    return head, lanes, hm, sq, tile


def _head_column(slab, lane_idx):
    lane = lax.broadcasted_iota(jnp.int32, slab.shape, 1)
    return _chunks(jnp.sum(jnp.where(lane == lane_idx, slab, 0.0), axis=1, keepdims=True))


def _dn_intra_fwd(qkv, beta_t, g_t):
    t = qkv.shape[0]
    n_chunks = t // DN_CHUNK
    rows_blk = min(DN_GROUP * DN_CHUNK, t)

    def body(q_ref, k_ref, v_ref, b_ref, g_ref, u_ref, w_ref, qd_ref, kd_ref, a_ref, ti_ref, el_ref):
        ri = lax.broadcasted_iota(jnp.int32, (DN_CHUNK, DN_CHUNK), 0)
        ci = lax.broadcasted_iota(jnp.int32, (DN_CHUNK, DN_CHUNK), 1)
        h = pl.program_id(1)
        q, k, v = (_chunks(r[...]) for r in (q_ref, k_ref, v_ref))
        b, gc = _head_column(b_ref[...], h), _head_column(g_ref[...], h + N_HEADS)
        _, gam, kk, qk, last, e_g, dec, _ = _dn_chunk_common(q, k, gc, ri, ci)
        tinv = _tri_inverse(jnp.where(ri > ci, b * kk * gam, 0.0), ri, ci)
        u_ref[...] = _unchunk(_bdot(tinv, v * b, "nn"))
        w_ref[...] = _unchunk(_bdot(tinv, k * (b * e_g), "nn")).astype(BF16)
        qd_ref[...] = _unchunk(q * e_g).astype(BF16)
        kd_ref[...] = _unchunk(k * dec).astype(BF16)
        a_ref[0] = _unchunk(qk * gam).astype(BF16)
        ti_ref[0] = _unchunk(tinv)
        el_ref[0] = jnp.broadcast_to(jnp.exp(last), (rows_blk // DN_CHUNK, 8, 128))

    head, lanes, hm, sq, tile = _dn_specs(t, rows_blk)
    act = jax.ShapeDtypeStruct((N_HEADS, t, D_HEAD), F32)
    sqs = jax.ShapeDtypeStruct((N_HEADS, t, DN_CHUNK), F32)
    return pl.pallas_call(
        body, name="dn_intra_fwd", grid=(t // rows_blk, N_HEADS),
        in_specs=[head(0), head(N_HEADS), head(2 * N_HEADS), lanes, lanes],
        out_specs=[hm] * 4 + [sq, sq, tile],
        out_shape=[act] + [jax.ShapeDtypeStruct((N_HEADS, t, D_HEAD), BF16)] * 3
        + [jax.ShapeDtypeStruct((N_HEADS, t, DN_CHUNK), BF16), sqs,
           jax.ShapeDtypeStruct((N_HEADS, n_chunks, 8, 128), F32)],
        compiler_params=_cp(),
    )(qkv, qkv, qkv, beta_t, g_t)


def _dn_scan_specs(t, rows_blk, reverse):
    n_groups = t // rows_blk

    def at(g):
        return n_groups - 1 - g if reverse else g

    per = rows_blk // DN_CHUNK
    act = pl.BlockSpec((N_HEADS, rows_blk, D_HEAD), lambda g: (0, at(g), 0))
    sq = pl.BlockSpec((N_HEADS, rows_blk, DN_CHUNK), lambda g: (0, at(g), 0))
    state = pl.BlockSpec((N_HEADS, per, D_HEAD, D_HEAD), lambda g: (0, at(g), 0, 0))
    tile = pl.BlockSpec((N_HEADS, per, 8, 128), lambda g: (0, at(g), 0, 0))
    return act, sq, state, tile


def _dn_scan_fwd(u, w, qd, kd, a, el):
    t = u.shape[1]
    n_chunks = t // DN_CHUNK
    rows_blk = DN_SCAN_GROUP * DN_CHUNK

    def body(u_ref, w_ref, qd_ref, kd_ref, a_ref, el_ref, o_ref, vn_ref, s_ref, s_scr):
        @pl.when(pl.program_id(0) == 0)
        def _():
            s_scr[...] = jnp.zeros_like(s_scr)

        for cc in range(DN_SCAN_GROUP):
            rows = slice(cc * DN_CHUNK, (cc + 1) * DN_CHUNK)
            s = s_scr[...]
            s_ref[:, cc] = s
            v_new = u_ref[:, rows, :] - _bdot(w_ref[:, rows, :], s, "nn")
            vn_ref[:, rows, :] = v_new.astype(BF16)
            o_ref[:, rows, :] = _bdot(qd_ref[:, rows, :], s, "nn") + _bdot(a_ref[:, rows, :], v_new, "nn")
            s_scr[...] = s * el_ref[:, cc][:, 0:1, :] + _bdot(kd_ref[:, rows, :], v_new, "tn")

    act, sq, state, tile = _dn_scan_specs(t, rows_blk, reverse=False)
    shp = jax.ShapeDtypeStruct((N_HEADS, t, D_HEAD), F32)
    return pl.pallas_call(
        body, name="dn_scan_fwd", grid=(t // rows_blk,),
        in_specs=[act, act, act, act, sq, tile], out_specs=[act, act, state],
        out_shape=[shp, jax.ShapeDtypeStruct((N_HEADS, t, D_HEAD), BF16),
                   jax.ShapeDtypeStruct((N_HEADS, n_chunks, D_HEAD, D_HEAD), F32)],
        scratch_shapes=[pltpu.VMEM((N_HEADS, D_HEAD, D_HEAD), F32)],
        compiler_params=_cp(dimension_semantics=("arbitrary",)),
    )(u, w, qd, kd, a, el)


def _dn_scan_bwd(w, qd, kd, a, el, vn, s_all, do):
    t = w.shape[1]
    n_chunks = t // DN_CHUNK
    rows_blk = DN_SCAN_GROUP * DN_CHUNK

    def body(w_ref, qd_ref, kd_ref, a_ref, el_ref, vn_ref, s_ref, do_ref, dvn_ref, dkd_ref, dqd_ref, dw_ref, dl_ref, ds_scr):
        @pl.when(pl.program_id(0) == 0)
        def _():
            ds_scr[...] = jnp.zeros_like(ds_scr)

        for cc in reversed(range(DN_SCAN_GROUP)):
            rows = slice(cc * DN_CHUNK, (cc + 1) * DN_CHUNK)
            s = s_ref[:, cc]
            d_s = ds_scr[...]
            e_last = el_ref[:, cc][:, 0:1, :]
            d_o = do_ref[:, rows, :]
            dv_new = _bdot(a_ref[:, rows, :], d_o, "tn") + _bdot(kd_ref[:, rows, :], d_s, "nn")
            ds_scr[...] = d_s * e_last + _bdot(qd_ref[:, rows, :], d_o, "tn") - _bdot(w_ref[:, rows, :], dv_new, "tn")
            dvn_ref[:, rows, :] = dv_new.astype(BF16)
            dkd_ref[:, rows, :] = _bdot(vn_ref[:, rows, :], d_s, "nt")
            dqd_ref[:, rows, :] = _bdot(d_o, s, "nt")
            dw_ref[:, rows, :] = (-_bdot(dv_new, s, "nt")).astype(BF16)
            dlast = jnp.sum(jnp.sum(d_s * s, axis=2, keepdims=True), axis=1, keepdims=True)
            dl_ref[:, cc] = jnp.broadcast_to(dlast * e_last, (N_HEADS, 8, 128))

    act, sq, state, tile = _dn_scan_specs(t, rows_blk, reverse=True)
    shp = jax.ShapeDtypeStruct((N_HEADS, t, D_HEAD), F32)
    shpb = jax.ShapeDtypeStruct((N_HEADS, t, D_HEAD), BF16)
    return pl.pallas_call(
        body, name="dn_scan_bwd", grid=(t // rows_blk,),
        in_specs=[act, act, act, sq, tile, act, state, act], out_specs=[act] * 4 + [tile],
        out_shape=[shpb, shp, shp, shpb, jax.ShapeDtypeStruct((N_HEADS, n_chunks, 8, 128), F32)],
        scratch_shapes=[pltpu.VMEM((N_HEADS, D_HEAD, D_HEAD), F32)],
        compiler_params=_cp(dimension_semantics=("arbitrary",)),
    )(w, qd, kd, a, el, vn, s_all, do)


def _dn_intra_bwd(qkv, beta_t, g_t, tinv_all, vn, do, dvn, dkd, dqd, dw, dl):
    t = qkv.shape[0]
    rows_blk = min(DN_GROUP * DN_CHUNK, t)

    def body(q_ref, k_ref, v_ref, b_ref, g_ref, ti_ref, vn_ref, do_ref, dvn_ref, dkd_ref, dqd_ref, dw_ref, dl_ref,
             dq_ref, dk_ref, dv_ref, db_ref, dg_ref):
        ri = lax.broadcasted_iota(jnp.int32, (DN_CHUNK, DN_CHUNK), 0)
        ci = lax.broadcasted_iota(jnp.int32, (DN_CHUNK, DN_CHUNK), 1)
        h = pl.program_id(1)
        q, k, v = (_chunks(r[...]) for r in (q_ref, k_ref, v_ref))
        b, gc = _head_column(b_ref[...], h), _head_column(g_ref[...], h + N_HEADS)
        tinv = _chunks(ti_ref[0])
        dv_new, dk_dec, dq_dec, d_w = (_chunks(r[...]) for r in (dvn_ref, dkd_ref, dqd_ref, dw_ref))
        eye, gam, kk, qk, _, e_g, dec, rcol = _dn_chunk_common(q, k, gc, ri, ci)
        bv = v * b
        bk = k * (b * e_g)

        d_a = jnp.where(ri >= ci, _bdot(_chunks(do_ref[...]), _chunks(vn_ref[...]), "nt"), 0.0)
        dbv = _bdot(tinv, dv_new, "tn")
        dbk = _bdot(tinv, d_w, "tn")
        d_tinv = _bdot(dv_new, bv, "nt") + _bdot(d_w, bk, "nt")
        d_m = -jnp.where(ri > ci, _dot3(_dot3(tinv, d_tinv, "tn"), tinv, "nt"), 0.0)

        d_kk = d_m * b * gam
        d_gam = d_m * b * kk + d_a * qk
        d_qk = d_a * gam
        dq_ref[...] = _unchunk(_bdot(d_qk, k, "nn") + dq_dec * e_g)
        dk_ref[...] = _unchunk(_bdot(d_qk, q, "tn") + _bdot(d_kk, k, "nn") + _bdot(d_kk, k, "tn")
                               + dk_dec * dec + dbk * (b * e_g))
        dv_ref[...] = _unchunk(dbv * b)
        d_b = _unchunk(jnp.sum(d_m * kk * gam, axis=-1, keepdims=True) + jnp.sum(dbv * v, axis=-1, keepdims=True)
                       + jnp.sum(dbk * k, axis=-1, keepdims=True) * e_g)

        xg = d_gam * gam
        kdk = jnp.sum(dk_dec * (k * dec), axis=-1, keepdims=True)
        d_gc = (jnp.sum(xg, axis=-1, keepdims=True) - _row_to_col(jnp.sum(xg, axis=-2, keepdims=True), eye)
                + jnp.sum(dq_dec * (q * e_g), axis=-1, keepdims=True) - kdk
                + jnp.sum(dbk * bk, axis=-1, keepdims=True))
        d_last_total = dl_ref[0][:, 0:1, 0:1] + jnp.sum(kdk, axis=-2, keepdims=True)
        d_g = _unchunk(d_gc + jnp.where(rcol == DN_CHUNK - 1, d_last_total, 0.0))

        @pl.when(h == 0)
        def _():
            db_ref[...] = jnp.zeros_like(db_ref)
            dg_ref[...] = jnp.zeros_like(dg_ref)

        lane = lax.broadcasted_iota(jnp.int32, db_ref.shape, 1)
        db_ref[...] += jnp.where(lane == h, d_b, 0.0)
        dg_ref[...] += jnp.where(lane == h + N_HEADS, d_g, 0.0)

    head, lanes, hm, sq, tile = _dn_specs(t, rows_blk)
    return pl.pallas_call(
        body, name="dn_intra_bwd", grid=(t // rows_blk, N_HEADS),
        in_specs=[head(0), head(N_HEADS), head(2 * N_HEADS), lanes, lanes, sq] + [hm] * 6 + [tile],
        out_specs=[head(0), head(0), head(0), lanes, lanes],
        out_shape=[jax.ShapeDtypeStruct((t, D_MODEL), F32)] * 3 + [jax.ShapeDtypeStruct((t, 128), F32)] * 2,
        compiler_params=_cp(),
    )(qkv, qkv, qkv, beta_t, g_t, tinv_all, vn, do, dvn, dkd, dqd, dw, dl)


def _dn_post_fwd(o, proj, gn):
    t = o.shape[1]

    def body(o_ref, z_ref, g_ref, out_ref):
        ov, z = o_ref[...], z_ref[...]
        r = lax.rsqrt(jnp.mean(ov * ov, axis=-1, keepdims=True) + NORM_EPS)
        out_ref[...] = (((ov * r) * g_ref[...]) * (z * _sigmoid(z))).astype(BF16)

    blk = pl.BlockSpec((t, D_HEAD), lambda h: (0, h))
    return pl.pallas_call(
        body, name="dn_post_fwd", grid=(N_HEADS,),
        in_specs=[pl.BlockSpec((None, t, D_HEAD), lambda h: (h, 0, 0)),
                  pl.BlockSpec((t, D_HEAD), lambda h: (0, C_DNZ // D_HEAD + h)),
                  pl.BlockSpec((1, D_HEAD), lambda h: (0, 0))],
        out_specs=blk, out_shape=jax.ShapeDtypeStruct((t, D_MODEL), BF16), compiler_params=_cp(),
    )(o, proj, gn)


def _dn_post_bwd(o, proj, gn, dout):
    t = o.shape[1]

    def body(o_ref, z_ref, g_ref, d_ref, do_ref, dz_ref, dg_ref):
        @pl.when(pl.program_id(0) == 0)
        def _():
            dg_ref[...] = jnp.zeros_like(dg_ref)

        ov, z, d = o_ref[...], z_ref[...], d_ref[...]
        r = lax.rsqrt(jnp.mean(ov * ov, axis=-1, keepdims=True) + NORM_EPS)
        ohat = ov * r
        s = _sigmoid(z)
        d_on = d * (z * s)
        dz_ref[...] = (d * (ohat * g_ref[...]) * (s * (1.0 + z * (1.0 - s)))).astype(BF16)
        dg_ref[...] += jnp.sum(d_on * ohat, axis=0, keepdims=True)
        dxh = d_on * g_ref[...]
        do_ref[...] = (r * (dxh - ohat * jnp.mean(dxh * ohat, axis=-1, keepdims=True))).astype(BF16)

    blk = pl.BlockSpec((t, D_HEAD), lambda h: (0, h))
    hm = pl.BlockSpec((None, t, D_HEAD), lambda h: (h, 0, 0))
    vec = pl.BlockSpec((1, D_HEAD), lambda h: (0, 0))
    return pl.pallas_call(
        body, name="dn_post_bwd", grid=(N_HEADS,),
        in_specs=[hm, pl.BlockSpec((t, D_HEAD), lambda h: (0, C_DNZ // D_HEAD + h)), vec, blk],
        out_specs=[hm, blk, vec],
        out_shape=[jax.ShapeDtypeStruct((N_HEADS, t, D_HEAD), BF16), jax.ShapeDtypeStruct((t, D_MODEL), BF16),
                   jax.ShapeDtypeStruct((1, D_HEAD), F32)], compiler_params=_cp(),
    )(o, proj, gn, dout)


def _sb_fwd(proj):
    t = proj.shape[0]
    qblk = min(SB_QBLOCK, t)
    scale = 1.0 / math.sqrt(D_HEAD)

    hp = SB_HEADS_PER_STEP
    wid = hp * D_HEAD

    def body(q_ref, k_ref, v_ref, z_ref, o_ref, og_ref, l_ref, qb, kb, vb):
        for hh in range(hp):
            hs = slice(hh * D_HEAD, (hh + 1) * D_HEAD)
            qb[hh] = q_ref[:, hs].astype(BF16)
            kb[hh] = k_ref[:, hs].astype(BF16)
            vb[hh] = v_ref[:, hs].astype(BF16)
        ri = lax.broadcasted_iota(jnp.int32, (qblk, SB_BLOCK), 0)
        ci = lax.broadcasted_iota(jnp.int32, (qblk, SB_BLOCK), 1)
        r2 = lax.broadcasted_iota(jnp.int32, (SB_BLOCK, SB_BLOCK), 0)
        c2 = lax.broadcasted_iota(jnp.int32, (SB_BLOCK, SB_BLOCK), 1)
        upper = (r2 > c2).astype(BF16)
        nkb = qblk // SB_BLOCK

        def qblock(i, carry):
            rows = pl.ds(pl.multiple_of(i * qblk, qblk), qblk)
            qi = qb[:, rows, :]

            def tile(j, st, on_diagonal):
                acc, c = st
                cols = pl.ds(pl.multiple_of(j * SB_BLOCK, SB_BLOCK), SB_BLOCK)
                z = _dot(qi, kb[:, cols, :], "nt") * scale
                lb = jnp.minimum(z, 0.0) - jnp.log(1.0 + jnp.exp(-jnp.abs(z)))
                lf = lb - z
                if on_diagonal:
                    mask = (j * SB_BLOCK + ci) < (i * qblk + ri)
                    lf = jnp.where(mask, lf, 0.0)
                att = jnp.exp(lb + (_ones_dot(lf, upper) + c))
                if on_diagonal:
                    att = jnp.where(mask, att, 0.0)
                acc = acc + _dot(att.astype(BF16), vb[:, cols, :], "nn")
                return acc, c + jnp.sum(lf, axis=-1, keepdims=True)

            st = (jnp.zeros((hp, qblk, D_HEAD), F32), jnp.zeros((hp, qblk, 1), F32))
            for d in range(nkb):
                st = tile((i + 1) * nkb - 1 - d, st, True)
            acc, c = lax.fori_loop(0, i * nkb, lambda jj, s: tile(i * nkb - 1 - jj, s, False), st)
            l_ref[:, rows, :] = c
            for hh in range(hp):
                hs = slice(hh * D_HEAD, (hh + 1) * D_HEAD)
                zg = z_ref[rows, hs]
                o_ref[rows, hs] = acc[hh]
                og_ref[rows, hs] = (acc[hh] * (zg * _sigmoid(zg))).astype(BF16)
            return carry

        lax.fori_loop(0, t // qblk, qblock, 0)

    def head(off):
        return pl.BlockSpec((t, wid), lambda h: (0, off // wid + h))

    out = pl.BlockSpec((t, wid), lambda h: (0, h))
    return pl.pallas_call(
        body, name="sb_fwd", grid=(N_HEADS // hp,),
        in_specs=[head(C_SBQ), head(C_SBQ + D_MODEL), head(C_SBQ + 2 * D_MODEL), head(C_SBZ)],
        out_specs=[out, out, pl.BlockSpec((hp, t, 1), lambda h: (h, 0, 0))],
        out_shape=[jax.ShapeDtypeStruct((t, D_MODEL), F32), jax.ShapeDtypeStruct((t, D_MODEL), BF16),
                   jax.ShapeDtypeStruct((N_HEADS, t, 1), F32)],
        scratch_shapes=[pltpu.VMEM((hp, t, D_HEAD), BF16)] * 3, compiler_params=_cp(),
    )(proj, proj, proj, proj)


def _sb_bwd(proj, o, ltot, dog, after=None):
    t = proj.shape[0]
    qblk = min(SB_QBLOCK, t)
    scale = 1.0 / math.sqrt(D_HEAD)

    hp = SB_HEADS_PER_STEP
    wid = hp * D_HEAD

    def body(q_ref, k_ref, v_ref, z_ref, o_ref, l_ref, d_ref, *rest):
        dq_ref, dk_ref, dv_ref, dz_ref, qb, kb, vb, dob, dk_scr, dv_scr = rest[-10:]
        for hh in range(hp):
            hs = slice(hh * D_HEAD, (hh + 1) * D_HEAD)
            qb[hh] = q_ref[:, hs].astype(BF16)
            kb[hh] = k_ref[:, hs].astype(BF16)
            vb[hh] = v_ref[:, hs].astype(BF16)
            zg = z_ref[:, hs]
            sg = _sigmoid(zg)
            dgo = d_ref[:, hs]
            dob[hh] = (dgo * (zg * sg)).astype(BF16)
            dz_ref[:, hs] = (dgo * o_ref[:, hs] * (sg * (1.0 + zg * (1.0 - sg)))).astype(BF16)
        dk_scr[...] = jnp.zeros_like(dk_scr)
        dv_scr[...] = jnp.zeros_like(dv_scr)
        ri = lax.broadcasted_iota(jnp.int32, (qblk, SB_BLOCK), 0)
        ci = lax.broadcasted_iota(jnp.int32, (qblk, SB_BLOCK), 1)
        r2 = lax.broadcasted_iota(jnp.int32, (SB_BLOCK, SB_BLOCK), 0)
        c2 = lax.broadcasted_iota(jnp.int32, (SB_BLOCK, SB_BLOCK), 1)
        upper = (r2 > c2).astype(BF16)
        below = (r2 < c2).astype(BF16)

        def qblock(i, carry):
            rows = pl.ds(pl.multiple_of(i * qblk, qblk), qblk)
            qi = qb[:, rows, :]
            d_o = dob[:, rows, :]
            ltot = l_ref[:, rows, :]

            def tile(j, st, on_diagonal):
                dq, cpre, ce = st
                cols = pl.ds(pl.multiple_of(j * SB_BLOCK, SB_BLOCK), SB_BLOCK)
                kj, vj = kb[:, cols, :], vb[:, cols, :]
                z = _dot(qi, kj, "nt") * scale
                lb = jnp.minimum(z, 0.0) - jnp.log(1.0 + jnp.exp(-jnp.abs(z)))
                lf = lb - z
                if on_diagonal:
                    mask = (j * SB_BLOCK + ci) < (i * qblk + ri)
                    lf = jnp.where(mask, lf, 0.0)
                tile_sum = jnp.sum(lf, axis=-1, keepdims=True)
                att = jnp.exp(lb + ((ltot - cpre - tile_sum) + _ones_dot(lf, upper)))
                if on_diagonal:
                    att = jnp.where(mask, att, 0.0)
                e = _dot(d_o, vj, "nt") * att
                dlf = ce + _ones_dot(e, below)
                dzz = e - (e + dlf) * jnp.exp(lb)
                if on_diagonal:
                    dzz = jnp.where(mask, dzz, 0.0)
                dzz = dzz.astype(BF16)
                dq = dq + _dot(dzz, kj, "nn")
                dk_scr[:, cols, :] += _dot(dzz, qi, "tn")
                dv_scr[:, cols, :] += _dot(att.astype(BF16), d_o, "tn")
                return dq, cpre + tile_sum, ce + jnp.sum(e, axis=-1, keepdims=True)

            nkb = qblk // SB_BLOCK
            zero_col = jnp.zeros((hp, qblk, 1), F32)
            st = lax.fori_loop(0, i * nkb, lambda j, s: tile(j, s, False),
                               (jnp.zeros((hp, qblk, D_HEAD), F32), zero_col, zero_col))
            for d in range(nkb):
                st = tile(i * nkb + d, st, True)
            dq = st[0]
            for hh in range(hp):
                dq_ref[rows, hh * D_HEAD:(hh + 1) * D_HEAD] = (dq[hh] * scale).astype(BF16)
            return carry

        lax.fori_loop(0, t // qblk, qblock, 0)
        for hh in range(hp):
            hs = slice(hh * D_HEAD, (hh + 1) * D_HEAD)
            dk_ref[:, hs] = (dk_scr[hh] * scale).astype(BF16)
            dv_ref[:, hs] = dv_scr[hh].astype(BF16)

    def head(off):
        return pl.BlockSpec((t, wid), lambda h: (0, off // wid + h))

    extra_specs, extra = [], []
    if after is not None:
        extra_specs, extra = [pl.BlockSpec(after.shape, lambda h: (0, 0))], [after]
    return pl.pallas_call(
        body, name="sb_bwd", grid=(N_HEADS // hp,),
        in_specs=[head(C_SBQ), head(C_SBQ + D_MODEL), head(C_SBQ + 2 * D_MODEL), head(C_SBZ), head(0),
                  pl.BlockSpec((hp, t, 1), lambda h: (h, 0, 0)), head(0)] + extra_specs,
        out_specs=[head(0)] * 4, out_shape=[jax.ShapeDtypeStruct((t, D_MODEL), BF16)] * 4,
        scratch_shapes=[pltpu.VMEM((hp, t, D_HEAD), BF16)] * 4 + [pltpu.VMEM((hp, t, D_HEAD), F32)] * 2,
        compiler_params=_cp(),
    )(proj, proj, proj, proj, o, ltot, dog, *extra)


def _mem_fwd(proj, mkv):
    t = proj.shape[0]
    tq = _pick(t, (512, 256))
    m_len = mkv.shape[0]
    scale = 1.0 / math.sqrt(MEM_DH)

    def body(q_ref, z_ref, kv_ref, o_ref, og_ref):
        q = q_ref[...]
        mk = kv_ref[:, :MEM_W].astype(BF16)
        mv = kv_ref[:, MEM_W:].astype(BF16)
        lane = lax.broadcasted_iota(jnp.int32, q.shape, 1) >> 6
        o = jnp.zeros(q.shape, F32)
        for h in range(MEM_HEADS):
            s = _bdot(jnp.where(lane == h, q, 0.0), mk, "nt") * scale
            p = jnp.exp(s - jnp.max(s, axis=-1, keepdims=True))
            p = p / jnp.sum(p, axis=-1, keepdims=True)
            o = o + jnp.where(lane == h, _bdot(p, mv, "nn"), 0.0)
        z = z_ref[...]
        o_ref[...] = o
        og_ref[...] = (o * (z * _sigmoid(z))).astype(BF16)

    out = pl.BlockSpec((tq, MEM_W), lambda i: (i, 0))
    return pl.pallas_call(
        body, name="mem_fwd", grid=(t // tq,),
        in_specs=[pl.BlockSpec((tq, MEM_W), lambda i: (i, C_MQ // MEM_W)),
                  pl.BlockSpec((tq, MEM_W), lambda i: (i, C_MZ // MEM_W)),
                  pl.BlockSpec((m_len, 2 * MEM_W), lambda i: (0, 0))],
        out_specs=[out, out],
        out_shape=[jax.ShapeDtypeStruct((t, MEM_W), F32), jax.ShapeDtypeStruct((t, MEM_W), BF16)],
        compiler_params=_cp(),
    )(proj, proj, mkv)


def _mem_bwd(proj, mkv, o, dog):
    t = proj.shape[0]
    tq = _pick(t, (512, 256))
    m_len = mkv.shape[0]
    scale = 1.0 / math.sqrt(MEM_DH)

    def body(q_ref, z_ref, kv_ref, o_ref, d_ref, dq_ref, dz_ref, dkv_ref):
        @pl.when(pl.program_id(0) == 0)
        def _():
            dkv_ref[...] = jnp.zeros_like(dkv_ref)

        q = q_ref[...]
        z = z_ref[...]
        sg = _sigmoid(z)
        dgo = d_ref[...]
        d_o = dgo * (z * sg)
        dz_ref[...] = (dgo * o_ref[...] * (sg * (1.0 + z * (1.0 - sg)))).astype(BF16)
        mk = kv_ref[:, :MEM_W].astype(BF16)
        mv = kv_ref[:, MEM_W:].astype(BF16)
        lane = lax.broadcasted_iota(jnp.int32, q.shape, 1) >> 6
        klane = lax.broadcasted_iota(jnp.int32, (m_len, MEM_W), 1) >> 6
        dq = jnp.zeros(q.shape, F32)
        dmk = jnp.zeros((m_len, MEM_W), F32)
        dmv = jnp.zeros((m_len, MEM_W), F32)
        for h in range(MEM_HEADS):
            qh = jnp.where(lane == h, q, 0.0)
            doh = jnp.where(lane == h, d_o, 0.0)
            s = _bdot(qh, mk, "nt") * scale
            p = jnp.exp(s - jnp.max(s, axis=-1, keepdims=True))
            p = p / jnp.sum(p, axis=-1, keepdims=True)
            dp = _bdot(doh, mv, "nt")
            ds = p * (dp - jnp.sum(dp * p, axis=-1, keepdims=True)) * scale
            dq = dq + jnp.where(lane == h, _bdot(ds, mk, "nn"), 0.0)
            dmk = dmk + jnp.where(klane == h, _bdot(ds, qh, "tn"), 0.0)
            dmv = dmv + jnp.where(klane == h, _bdot(p, doh, "tn"), 0.0)
        dq_ref[...] = dq.astype(BF16)
        dkv_ref[:, :MEM_W] += dmk
        dkv_ref[:, MEM_W:] += dmv

    blk = pl.BlockSpec((tq, MEM_W), lambda i: (i, 0))
    kv = pl.BlockSpec((m_len, 2 * MEM_W), lambda i: (0, 0))
    return pl.pallas_call(
        body, name="mem_bwd", grid=(t // tq,),
        in_specs=[pl.BlockSpec((tq, MEM_W), lambda i: (i, C_MQ // MEM_W)),
                  pl.BlockSpec((tq, MEM_W), lambda i: (i, C_MZ // MEM_W)), kv, blk, blk],
        out_specs=[blk, blk, kv],
        out_shape=[jax.ShapeDtypeStruct((t, MEM_W), BF16), jax.ShapeDtypeStruct((t, MEM_W), BF16),
                   jax.ShapeDtypeStruct((m_len, 2 * MEM_W), F32)], compiler_params=_cp(),
    )(proj, proj, mkv, o, dog)


_GW = 512


def _merge_fwd(proj, y_dn, y_sb, y_m):
    t = proj.shape[0]
    tb = _pick(t, (512, 256))
    nc = D_MODEL // _GW

    def body(g1, g2, g3, y1, y2, y3, out_ref):
        out_ref[...] = (_sigmoid(g1[...]) * y1[...] + _sigmoid(g2[...]) * y2[...] + _sigmoid(g3[...]) * y3[...]).astype(BF16)

    def gate(kb):
        return pl.BlockSpec((tb, _GW), lambda i, c: (i, C_GATES // _GW + kb * nc + c))

    blk = pl.BlockSpec((tb, _GW), lambda i, c: (i, c))
    return pl.pallas_call(
        body, name="merge_fwd", grid=(t // tb, nc), in_specs=[gate(0), gate(1), gate(2), blk, blk, blk],
        out_specs=blk, out_shape=jax.ShapeDtypeStruct((t, D_MODEL), BF16), compiler_params=_cp(),
    )(proj, proj, proj, y_dn, y_sb, y_m)


def _merge_bwd(proj, y_dn, y_sb, y_m, dm):
    t = proj.shape[0]
    tb = _pick(t, (512, 256))
    nc = D_MODEL // _GW

    def body(g1, g2, g3, y1, y2, y3, dm_ref, d1, d2, d3, dg1, dg2, dg3):
        d = dm_ref[...]
        for g, y, dy, dg in ((g1, y1, d1, dg1), (g2, y2, d2, dg2), (g3, y3, d3, dg3)):
            s = _sigmoid(g[...])
            dy[...] = (d * s).astype(BF16)
            dg[...] = (d * y[...] * (s * (1.0 - s))).astype(BF16)

    def gate(kb):
        return pl.BlockSpec((tb, _GW), lambda i, c: (i, C_GATES // _GW + kb * nc + c))

    blk = pl.BlockSpec((tb, _GW), lambda i, c: (i, c))
    act = jax.ShapeDtypeStruct((t, D_MODEL), BF16)
    return pl.pallas_call(
        body, name="merge_bwd", grid=(t // tb, nc), in_specs=[gate(0), gate(1), gate(2), blk, blk, blk, blk],
        out_specs=[blk] * 6, out_shape=[act] * 6, compiler_params=_cp(),
    )(proj, proj, proj, y_dn, y_sb, y_m, dm)


def _final_loss(x, mo, g, tgt):
    t, d = x.shape
    tb = _pick(t, (512, 256))

    def body(x_ref, mo_ref, g_ref, t_ref, do_ref, dob_ref, loss_ref, dg_ref):
        @pl.when(pl.program_id(0) == 0)
        def _():
            loss_ref[...] = jnp.zeros_like(loss_ref)
            dg_ref[...] = jnp.zeros_like(dg_ref)

        out = x_ref[...] + mo_ref[...]
        r = lax.rsqrt(jnp.mean(out * out, axis=-1, keepdims=True) + NORM_EPS)
        xhat = out * r
        gv = g_ref[...]
        err = xhat * gv - t_ref[...]
        per_tok = jnp.mean(err * err, axis=-1, keepdims=True)
        loss_ref[...] += 0.5 * jnp.sum(per_tok, axis=0, keepdims=True)
        dy = err * (1.0 / d)
        dg_ref[...] += jnp.sum(dy * xhat, axis=0, keepdims=True)
        dxh = dy * gv
        dout = r * (dxh - xhat * jnp.mean(dxh * xhat, axis=-1, keepdims=True))
        do_ref[...] = dout
        dob_ref[...] = dout.astype(BF16)

    row = pl.BlockSpec((tb, d), lambda i: (i, 0))
    vec = pl.BlockSpec((1, d), lambda i: (0, 0))
    return pl.pallas_call(
        body, name="final_loss", grid=(t // tb,), in_specs=[row, row, vec, row],
        out_specs=[row, row, pl.BlockSpec((1, 128), lambda i: (0, 0)), vec],
        out_shape=[jax.ShapeDtypeStruct((t, d), F32), jax.ShapeDtypeStruct((t, d), BF16),
                   jax.ShapeDtypeStruct((1, 128), F32), jax.ShapeDtypeStruct((1, d), F32)],
        compiler_params=_cp(),
    )(x, mo, g, tgt)


def _cast_bf16(a, name):
    r, c = a.shape
    tb = _pick(r, (128, 496, 240))

    def body(a_ref, o_ref):
        o_ref[...] = a_ref[...].astype(BF16)

    blk = pl.BlockSpec((tb, c), lambda i: (i, 0))
    return pl.pallas_call(body, name=name, grid=(r // tb,), in_specs=[blk], out_specs=blk,
                          out_shape=jax.ShapeDtypeStruct((r, c), BF16), compiler_params=_cp())(a)


WIN_START = (0, 23, 45, 68)
_S1_LO, _S1_HI = 1148, 1164
_S1_BA_POS = SHARD_PAD - 128


def _to_window(x, s):
    if s == 0:
        return x
    if s in (2, 3):
        return pltpu.roll(x, 120 if s == 2 else 124, 1)
    pos = lax.broadcasted_iota(jnp.int32, x.shape, 1)
    head = pltpu.roll(x, 4, 1)
    tail = pltpu.roll(x, SHARD_PAD - 12, 1)
    ba = jnp.where(pos < _S1_BA_POS + (_S1_HI - _S1_LO), pltpu.roll(x, _S1_BA_POS - _S1_LO, 1), 0.0)
    return jnp.where(pos < _S1_LO + 4, head, jnp.where(pos < _S1_BA_POS, tail, ba))


def _from_window(g, s):
    if s == 0:
        return g
    if s in (2, 3):
        return pltpu.roll(g, SHARD_PAD - (120 if s == 2 else 124), 1)
    col = lax.broadcasted_iota(jnp.int32, g.shape, 1)
    head = pltpu.roll(g, SHARD_PAD - 4, 1)
    tail = pltpu.roll(g, 12, 1)
    ba = pltpu.roll(g, SHARD_PAD - (_S1_BA_POS - _S1_LO), 1)
    return jnp.where(col < _S1_LO, head, jnp.where(col < _S1_HI, ba, tail))


def _cast_to_window(w, shard, name):
    r, c = w.shape
    tb = _pick(r, (128,))

    def body(s_ref, w_ref, o_ref, pad_scr):
        pad_scr[...] = jnp.zeros_like(pad_scr)
        pad_scr[:, :c] = w_ref[...]
        x = pad_scr[...]
        for s in range(N_SHARD):
            @pl.when(s_ref[0] == s)
            def _():
                o_ref[...] = _to_window(x, s).astype(BF16)

    return pl.pallas_call(
        body, name=name,
        grid_spec=pltpu.PrefetchScalarGridSpec(
            num_scalar_prefetch=1, grid=(r // tb,),
            in_specs=[pl.BlockSpec((tb, c), lambda i, s: (i, 0))],
            out_specs=pl.BlockSpec((tb, SHARD_PAD), lambda i, s: (i, 0)),
            scratch_shapes=[pltpu.VMEM((tb, SHARD_PAD), F32)]),
        out_shape=jax.ShapeDtypeStruct((r, SHARD_PAD), BF16), compiler_params=_cp(),
    )(shard, w)


def _pair_add(g, recv, c_idx, name):
    n, r, c = g.shape
    half = r // 2
    tb = _pick(half, (128, 240))
    nb = half // tb

    def body(c_ref, g_ref, r_ref, o_ref):
        o_ref[...] = (g_ref[...].astype(F32) + r_ref[...].astype(F32)).astype(BF16)

    blk = pl.BlockSpec((n, tb, c), lambda i, c_ref: (0, i, 0))
    return pl.pallas_call(
        body, name=name,
        grid_spec=pltpu.PrefetchScalarGridSpec(
            num_scalar_prefetch=1, grid=(nb,),
            in_specs=[pl.BlockSpec((n, tb, c), lambda i, c_ref: (0, c_ref[0] * nb + i, 0)), blk], out_specs=blk),
        out_shape=jax.ShapeDtypeStruct((n, half, c), BF16), compiler_params=_cp(),
    )(c_idx, g, recv)


def _chip_sum(parts, by_chip, place, name):
    n, h, c = parts.shape
    tb = _pick(h, (128, 240))
    nb = h // tb

    def body(p_ref, mine_ref, *rest):
        others, o_ref = rest[:n], rest[n]
        me = jnp.zeros((tb, c), jnp.int32) + p_ref[0]
        acc = None
        for q in range(n):
            term = jnp.where(me == q, mine_ref[...], others[q][...]).astype(F32)
            acc = term if acc is None else acc + term
        o_ref[...] = acc

    def other(q):
        return pl.BlockSpec((None, tb, c), lambda i, p: (jnp.where(p[0] == q, (q + 1) % n, q), i, 0))

    return pl.pallas_call(
        body, name=name,
        grid_spec=pltpu.PrefetchScalarGridSpec(
            num_scalar_prefetch=1, grid=(nb,),
            in_specs=[pl.BlockSpec((None, tb, c), lambda i, p: (p[0], i, 0))] + [other(q) for q in range(n)],
            out_specs=pl.BlockSpec((tb, c), lambda i, p: (p[1] * nb + i, 0))),
        out_shape=jax.ShapeDtypeStruct((2 * h, c), F32), compiler_params=_cp(),
    )(place, parts, *([by_chip] * n))


def _adamw_math(w, g, m, v):
    m = ADAM_B1 * m + (1.0 - ADAM_B1) * g
    v = ADAM_B2 * v + (1.0 - ADAM_B2) * (g * g)
    m_hat = m / (1.0 - ADAM_B1 ** ADAM_STEP)
    v_hat = v / (1.0 - ADAM_B2 ** ADAM_STEP)
    delta = -ADAM_LR * (m_hat / (jnp.sqrt(v_hat) + ADAM_EPS) + ADAM_WD * w)
    return delta, m, v


def _adamw(w, g, m, v, name):
    r, c = w.shape
    tb = _pick(r, (128, 496, 240))

    def body(w_ref, g_ref, m_ref, v_ref, go_ref, d_ref, mo_ref, vo_ref):
        gv = g_ref[...]
        d, mn, vn = _adamw_math(w_ref[...], gv, m_ref[...], v_ref[...])
        go_ref[...] = gv
        d_ref[...] = d
        mo_ref[...] = mn
        vo_ref[...] = vn

    blk = pl.BlockSpec((tb, c), lambda i: (i, 0))
    return pl.pallas_call(
        body, name=name, grid=(r // tb,), in_specs=[blk] * 4, out_specs=[blk] * 4,
        out_shape=[jax.ShapeDtypeStruct((r, c), F32)] * 4, compiler_params=_cp(),
    )(w, g, m, v)


def _adamw_window(w, g_win, m, v, shard, name):
    r, c = w.shape
    tb = _pick(r, (128,))

    def body(s_ref, w_ref, g_ref, m_ref, v_ref, go_ref, d_ref, mo_ref, vo_ref, g_scr):
        gw = g_ref[...]
        for s in range(N_SHARD):
            @pl.when(s_ref[0] == s)
            def _():
                g_scr[...] = _from_window(gw, s)

        gv = g_scr[:, :c]
        d, mn, vn = _adamw_math(w_ref[...], gv, m_ref[...], v_ref[...])
        go_ref[...] = gv
        d_ref[...] = d
        mo_ref[...] = mn
        vo_ref[...] = vn

    blk = pl.BlockSpec((tb, c), lambda i, s: (i, 0))
    return pl.pallas_call(
        body, name=name,
        grid_spec=pltpu.PrefetchScalarGridSpec(
            num_scalar_prefetch=1, grid=(r // tb,),
            in_specs=[blk, pl.BlockSpec((tb, SHARD_PAD), lambda i, s: (i, 0)), blk, blk], out_specs=[blk] * 4,
            scratch_shapes=[pltpu.VMEM((tb, SHARD_PAD), F32)]),
        out_shape=[jax.ShapeDtypeStruct((r, c), F32)] * 4, compiler_params=_cp(),
    )(shard, w, g_win, m, v)


def _small_update(gathered, w, m, v):
    def body(p_ref, w_ref, m_ref, v_ref, g_ref, d_ref, mo_ref, vo_ref):
        g = p_ref[0]
        for i in range(1, N_DEV):
            g = g + p_ref[i]
        d, mn, vn = _adamw_math(w_ref[...], g, m_ref[...], v_ref[...])
        g_ref[...] = g
        d_ref[...] = d
        mo_ref[...] = mn
        vo_ref[...] = vn

    full = pl.BlockSpec((S_ROWS, 128), lambda i: (0, 0))
    return pl.pallas_call(
        body, name="small_update", grid=(1,),
        in_specs=[pl.BlockSpec((N_DEV, S_ROWS, 128), lambda i: (0, 0, 0)), full, full, full], out_specs=[full] * 4,
        out_shape=[jax.ShapeDtypeStruct((S_ROWS, 128), F32)] * 4, compiler_params=_cp(),
    )(gathered, w, m, v)


_ANY = pl.BlockSpec(memory_space=pl.ANY)


def _place():
    x, y, c = lax.axis_index("x"), lax.axis_index("y"), lax.axis_index("c")
    chips = [(1 - x, y), (x, 1 - y), (1 - x, 1 - y)]
    return x, y, c, chips


def _pair_reduce_send(grads, tag):
    n = len(grads)

    def body(*refs):
        ins, outs = refs[:n], refs[n:2 * n]
        send_sems, recv_sems = refs[2 * n:]
        x, y, c, _ = _place()
        sibling = (x, y, 1 - c)
        cps = []
        for a in range(n):
            half = ins[a].shape[1] // 2
            theirs = pl.ds(pl.multiple_of((1 - c) * half, 8), half)
            cp = pltpu.make_async_remote_copy(
                src_ref=ins[a].at[:, theirs], dst_ref=outs[a], send_sem=send_sems.at[a], recv_sem=recv_sems.at[a],
                device_id=sibling, device_id_type=MESH)
            cp.start()
            cps.append(cp)
        for cp in cps:
            cp.wait()

    return pl.pallas_call(
        body, name="pair_reduce_send_" + tag, in_specs=[_ANY] * n, out_specs=[_ANY] * n,
        out_shape=[jax.ShapeDtypeStruct((g.shape[0], g.shape[1] // 2, g.shape[2]), g.dtype) for g in grads],
        scratch_shapes=[pltpu.SemaphoreType.DMA((n,)), pltpu.SemaphoreType.DMA((n,))],
        compiler_params=pltpu.CompilerParams(has_side_effects=True),
    )(*grads)


_HBM = pl.BlockSpec(memory_space=pltpu.HBM)
_SEM = pl.BlockSpec(memory_space=pltpu.SEMAPHORE)
_DATAFLOW = pltpu.SideEffectType.DATAFLOW_SIDE_EFFECTING


def _chip_exchange_copies(ins, lands, send_sems, recv_sems):
    x, y, c, chips = _place()
    me = 2 * x + y
    cps = []
    for a in range(len(ins)):
        for j, (qx, qy) in enumerate(chips):
            cps.append(pltpu.make_async_remote_copy(
                src_ref=ins[a].at[2 * qx + qy], dst_ref=lands[a].at[me], send_sem=send_sems.at[3 * a + j],
                recv_sem=recv_sems.at[3 * a + j], device_id=(qx, qy, c), device_id_type=MESH))
    return cps


def _chip_exchange_start(parts, tag):
    n = len(parts)

    def body(*refs):
        ins, lands = refs[:n], refs[n:2 * n]
        send_sems, recv_sems = refs[2 * n:2 * n + 2]
        token = refs[4 * n + 2]
        for cp in _chip_exchange_copies(ins, lands, send_sems, recv_sems):
            cp.start()
        token[...] = jnp.zeros_like(token)

    hbm = [pltpu.HBM(p.shape, p.dtype) for p in parts]
    lands = [pltpu.with_memory_space_constraint(lax.empty(p.shape, p.dtype), pltpu.HBM) for p in parts]
    res = pl.pallas_call(
        body, name="chip_exchange_start_" + tag,
        out_shape=(pltpu.SemaphoreType.DMA((3 * n,)), pltpu.SemaphoreType.DMA((3 * n,)), *hbm, *hbm,
                   jax.ShapeDtypeStruct((8, 128), F32)),
        in_specs=[_HBM] * (2 * n), out_specs=(_SEM, _SEM, *([_HBM] * (2 * n)), pl.BlockSpec(memory_space=pltpu.VMEM)),
        input_output_aliases={a: 2 + a for a in range(2 * n)},
        compiler_params=pltpu.CompilerParams(has_side_effects=_DATAFLOW),
    )(*[pltpu.with_memory_space_constraint(p, pltpu.HBM) for p in parts], *lands)
    return res[0], res[1], res[2:2 + n], res[2 + n:2 + 2 * n], res[2 + 2 * n]


def _chip_exchange_wait(send_sems, recv_sems, parts, lands, after, tag):
    n = len(parts)

    def body(*refs):
        ins, land_refs = refs[:n], refs[n:2 * n]
        s_sems, r_sems = refs[2 * n:2 * n + 2]
        for cp in _chip_exchange_copies(ins, land_refs, s_sems, r_sems):
            cp.wait_send()
            cp.wait_recv()

    hbm = [pltpu.HBM(p.shape, p.dtype) for p in parts]
    res = pl.pallas_call(
        body, name="chip_exchange_wait_" + tag, out_shape=(*hbm, *hbm),
        in_specs=[_HBM] * (2 * n) + [_SEM, _SEM, _ANY], out_specs=tuple([_HBM] * (2 * n)),
        input_output_aliases={a: a for a in range(2 * n)},
        compiler_params=pltpu.CompilerParams(has_side_effects=_DATAFLOW),
    )(*parts, *lands, send_sems, recv_sems, after)
    return res[:n], res[n:]


def _halves_ici_copies(srcs, lands, send_sems, recv_sems):
    x, y, c, chips = _place()
    me = 2 * x + y
    cps = []
    for a, src in enumerate(srcs):
        half = src.shape[0] // 2
        mine = pl.ds(pl.multiple_of(c * half, 16), half)
        for j, (qx, qy) in enumerate(chips):
            cps.append(pltpu.make_async_remote_copy(
                src_ref=src.at[mine], dst_ref=lands[a].at[me, mine], send_sem=send_sems.at[3 * a + j],
                recv_sem=recv_sems.at[3 * a + j], device_id=(qx, qy, c), device_id_type=MESH))
    return cps


def _halves_d2d_copies(lands, send_sems, recv_sems):
    x, y, c, chips = _place()
    cps = []
    for a, land in enumerate(lands):
        half = land.shape[1] // 2
        mine = pl.ds(pl.multiple_of(c * half, 16), half)
        for j, (qx, qy) in enumerate(chips):
            region = land.at[2 * qx + qy, mine]
            cps.append(pltpu.make_async_remote_copy(
                src_ref=region, dst_ref=region, send_sem=send_sems.at[3 * a + j], recv_sem=recv_sems.at[3 * a + j],
                device_id=(x, y, 1 - c), device_id_type=MESH))
    return cps


def _halves_gather_start(shards):
    n = len(shards)

    def body(*refs):
        srcs, lands = refs[:n], refs[n:2 * n]
        send_sems, recv_sems = refs[2 * n:2 * n + 2]
        for cp in _halves_ici_copies(srcs, lands, send_sems, recv_sems):
            cp.start()
        refs[4 * n + 2][...] = jnp.zeros((8, 128), F32)

    hbm_s = [pltpu.HBM(s.shape, s.dtype) for s in shards]
    hbm_l = [pltpu.HBM((N_SHARD,) + s.shape, s.dtype) for s in shards]
    lands = [pltpu.with_memory_space_constraint(lax.empty((N_SHARD,) + s.shape, s.dtype), pltpu.HBM) for s in shards]
    res = pl.pallas_call(
        body, name="halves_gather_start",
        out_shape=(pltpu.SemaphoreType.DMA((3 * n,)), pltpu.SemaphoreType.DMA((3 * n,)), *hbm_s, *hbm_l,
                   jax.ShapeDtypeStruct((8, 128), F32)),
        in_specs=[_HBM] * (2 * n), out_specs=(_SEM, _SEM, *([_HBM] * (2 * n)), pl.BlockSpec(memory_space=pltpu.VMEM)),
        input_output_aliases={a: 2 + a for a in range(2 * n)},
        compiler_params=pltpu.CompilerParams(has_side_effects=_DATAFLOW),
    )(*[pltpu.with_memory_space_constraint(s, pltpu.HBM) for s in shards], *lands)
    return res[0], res[1], res[2:2 + n], res[2 + n:2 + 2 * n], res[2 + 2 * n]


def _halves_gather_forward(send1, recv1, shards, lands, after):
    n = len(shards)

    def body(*refs):
        srcs, land_refs = refs[:n], refs[n:2 * n]
        s1, r1 = refs[2 * n:2 * n + 2]
        outs = refs[2 * n + 2 + len(after):]
        s2, r2 = outs[0], outs[1]
        for cp in _halves_ici_copies(srcs, land_refs, s1, r1):
            cp.wait_send()
            cp.wait_recv()
        for cp in _halves_d2d_copies(land_refs, s2, r2):
            cp.start()

    hbm_s = [pltpu.HBM(s.shape, s.dtype) for s in shards]
    hbm_l = [pltpu.HBM(l.shape, l.dtype) for l in lands]
    res = pl.pallas_call(
        body, name="halves_gather_forward",
        out_shape=(pltpu.SemaphoreType.DMA((3 * n,)), pltpu.SemaphoreType.DMA((3 * n,)), *hbm_s, *hbm_l),
        in_specs=[_HBM] * (2 * n) + [_SEM, _SEM] + [_ANY] * len(after),
        out_specs=(_SEM, _SEM, *([_HBM] * (2 * n))), input_output_aliases={a: 2 + a for a in range(2 * n)},
        compiler_params=pltpu.CompilerParams(has_side_effects=_DATAFLOW),
    )(*shards, *lands, send1, recv1, *after)
    return res[0], res[1], res[2 + n:2 + 2 * n]


def _halves_gather_wait(send2, recv2, lands):
    n = len(lands)

    def body(*refs):
        land_refs = refs[:n]
        s2, r2 = refs[n:n + 2]
        for cp in _halves_d2d_copies(land_refs, s2, r2):
            cp.wait_send()
            cp.wait_recv()

    hbm_l = [pltpu.HBM(l.shape, l.dtype) for l in lands]
    res = pl.pallas_call(
        body, name="halves_gather_wait", out_shape=tuple(hbm_l),
        in_specs=[_HBM] * n + [_SEM, _SEM], out_specs=tuple([_HBM] * n),
        input_output_aliases={a: a for a in range(n)},
        compiler_params=pltpu.CompilerParams(has_side_effects=_DATAFLOW),
    )(*lands, send2, recv2)
    return list(res)


def _shard_gather_copies(src, land, send_sems, recv_sems):
    x, y, c, chips = _place()
    me = 2 * x + y
    return [pltpu.make_async_remote_copy(
        src_ref=src, dst_ref=land.at[me], send_sem=send_sems.at[j], recv_sem=recv_sems.at[j],
        device_id=(qx, qy, c), device_id_type=MESH) for j, (qx, qy) in enumerate(chips)]


def _shard_gather_start(shard_arr, after):
    def body(src, land, after_ref, send_sems, recv_sems, src_thru, land_thru, token):
        for cp in _shard_gather_copies(src, land, send_sems, recv_sems):
            cp.start()
        token[...] = jnp.zeros_like(token)

    land_shape = (N_SHARD,) + shard_arr.shape
    land = pltpu.with_memory_space_constraint(lax.empty(land_shape, shard_arr.dtype), pltpu.HBM)
    return pl.pallas_call(
        body, name="shard_gather_start",
        out_shape=(pltpu.SemaphoreType.DMA((N_SHARD - 1,)), pltpu.SemaphoreType.DMA((N_SHARD - 1,)),
                   pltpu.HBM(shard_arr.shape, shard_arr.dtype), pltpu.HBM(land_shape, shard_arr.dtype),
                   jax.ShapeDtypeStruct((8, 128), F32)),
        in_specs=[_HBM, _HBM, _ANY], out_specs=(_SEM, _SEM, _HBM, _HBM, pl.BlockSpec(memory_space=pltpu.VMEM)),
        input_output_aliases={0: 2, 1: 3},
        compiler_params=pltpu.CompilerParams(has_side_effects=_DATAFLOW),
    )(pltpu.with_memory_space_constraint(shard_arr, pltpu.HBM), land, after)


def _shard_gather_wait(send_sems, recv_sems, shard_arr, land, after):
    def body(src, land_ref, s_sems, r_sems, after_ref, src_out, land_out):
        for cp in _shard_gather_copies(src, land_ref, s_sems, r_sems):
            cp.wait_send()
            cp.wait_recv()

    return pl.pallas_call(
        body, name="shard_gather_wait",
        out_shape=(pltpu.HBM(shard_arr.shape, shard_arr.dtype), pltpu.HBM(land.shape, land.dtype)),
        in_specs=[_HBM, _HBM, _SEM, _SEM, _ANY], out_specs=(_HBM, _HBM), input_output_aliases={0: 0, 1: 1},
        compiler_params=pltpu.CompilerParams(has_side_effects=_DATAFLOW),
    )(shard_arr, land, send_sems, recv_sems, after)


def _pair_allgather(fulls, tag):
    n = len(fulls)

    def body(*refs):
        outs = refs[n:2 * n]
        send_sems, recv_sems = refs[2 * n:]
        x, y, c, _ = _place()
        sibling = (x, y, 1 - c)
        cps = []
        for a in range(n):
            half = outs[a].shape[0] // 2
            mine = outs[a].at[pl.ds(pl.multiple_of(c * half, 8), half)]
            cp = pltpu.make_async_remote_copy(
                src_ref=mine, dst_ref=mine, send_sem=send_sems.at[a], recv_sem=recv_sems.at[a],
                device_id=sibling, device_id_type=MESH)
            cp.start()
            cps.append(cp)
        for a in range(n):
            half = outs[a].shape[0] // 2
            theirs = outs[a].at[pl.ds(pl.multiple_of((1 - c) * half, 8), half)]
            pltpu.make_async_remote_copy(
                src_ref=theirs, dst_ref=theirs, send_sem=send_sems.at[a], recv_sem=recv_sems.at[a],
                device_id=sibling, device_id_type=MESH).wait_recv()
        for cp in cps:
            cp.wait_send()

    return pl.pallas_call(
        body, name="pair_allgather_" + tag, in_specs=[_ANY] * n, out_specs=[_ANY] * n,
        out_shape=[jax.ShapeDtypeStruct(f.shape, f.dtype) for f in fulls],
        input_output_aliases={a: a for a in range(n)},
        scratch_shapes=[pltpu.SemaphoreType.DMA((n,)), pltpu.SemaphoreType.DMA((n,))],
        compiler_params=pltpu.CompilerParams(has_side_effects=True),
    )(*fulls)


def _allgather_small(slab, after):
    def body(s_ref, after_ref, out_ref, send_sems, recv_sems):
        x, y, c, _ = _place()
        me = 4 * x + 2 * y + c
        out_ref[me] = s_ref[...]
        cps = []
        for mask in range(1, N_DEV):
            peer = (x ^ (mask >> 2), y ^ ((mask >> 1) & 1), c ^ (mask & 1))
            cp = pltpu.make_async_remote_copy(
                src_ref=s_ref, dst_ref=out_ref.at[me], send_sem=send_sems.at[mask - 1], recv_sem=recv_sems.at[mask - 1],
                device_id=peer, device_id_type=MESH)
            cp.start()
            cps.append(cp)
        for mask in range(1, N_DEV):
            peer = (x ^ (mask >> 2), y ^ ((mask >> 1) & 1), c ^ (mask & 1))
            dst = out_ref.at[4 * peer[0] + 2 * peer[1] + peer[2]]
            pltpu.make_async_remote_copy(
                src_ref=dst, dst_ref=dst, send_sem=send_sems.at[mask - 1], recv_sem=recv_sems.at[mask - 1],
                device_id=peer, device_id_type=MESH).wait_recv()
        for cp in cps:
            cp.wait_send()

    vm = pl.BlockSpec(memory_space=pltpu.VMEM)
    return pl.pallas_call(
        body, name="allgather_small", in_specs=[vm, _ANY], out_specs=vm,
        out_shape=jax.ShapeDtypeStruct((N_DEV,) + slab.shape, slab.dtype),
        scratch_shapes=[pltpu.SemaphoreType.DMA((N_DEV - 1,)), pltpu.SemaphoreType.DMA((N_DEV - 1,))],
        compiler_params=pltpu.CompilerParams(has_side_effects=True),
    )(slab, after)


def _pack_b(w_mem_kv, w_br_dn, w_br_sb, w_br_mem, w_out):
    return jnp.concatenate([w_mem_kv.reshape(128, D_MODEL), w_br_dn, w_br_sb, w_br_mem.reshape(64, D_MODEL), w_out],
                           axis=0)


def _conv_slab(conv_w):
    return jnp.pad(conv_w.reshape(3, D_MODEL), ((0, 29), (0, 0)))


def _unpack_b(slab):
    return (slab[B_MEMKV:B_BRDN].reshape(1, 256, 512), slab[B_BRDN:B_BRSB].reshape(1, 256, D_MODEL),
            slab[B_BRSB:B_BRMEM].reshape(1, 256, D_MODEL), slab[B_BRMEM:B_OUT].reshape(1, 256, 256),
            slab[B_OUT:B_CONV].reshape(1, 256, D_MODEL))


def _conv_rows(conv_full):
    return conv_full.reshape(4 * CONV_BLOCKS, 128)


def _conv_shard_rows(conv_shard, shard):
    own = CONV_BLOCKS // N_SHARD
    blocks = lax.dynamic_update_slice(jnp.zeros((4, CONV_BLOCKS, 128), F32), conv_shard.reshape(4, own, 128),
                                      (0, own * shard, 0))
    return blocks.reshape(4 * CONV_BLOCKS, 128)


def _conv_shard_of(rows, shard):
    own = CONV_BLOCKS // N_SHARD
    blocks = lax.dynamic_slice(rows.reshape(4, CONV_BLOCKS, 128), (0, own * shard, 0), (4, own, 128))
    return blocks.reshape(1, 4, own * 128)


def _pack_small(norm_g, mem_norm_g, final_g, dn_norm_g, a_log, dt_bias, conv_rows, loss=None):
    def row(v):
        v = v.reshape(1, -1).astype(F32)
        return jnp.pad(v, ((0, 0), (0, 128 - v.shape[1])))

    loss_row = row(jnp.zeros((1,), F32) if loss is None else jnp.reshape(loss, (1,)))
    rid = lax.broadcasted_iota(jnp.int32, (8, 128), 0) + S_DNNORM
    tile = jnp.where(rid == S_DNNORM, dn_norm_g.reshape(1, 128), jnp.where(
        rid == S_ALOG, row(a_log), jnp.where(rid == S_DTB, row(dt_bias), jnp.where(rid == S_LOSS, loss_row, 0.0))))
    return jnp.concatenate([norm_g.reshape(8, 128), mem_norm_g.reshape(8, 128), final_g.reshape(8, 128), tile,
                            conv_rows], axis=0)


def _unpack_small(slab, shard):
    return (slab[S_NORM:S_NORM + 8].reshape(1, D_MODEL), slab[S_MEMNORM:S_MEMNORM + 8].reshape(1, D_MODEL),
            slab[S_FINAL:S_FINAL + 8].reshape(D_MODEL), slab[S_DNNORM].reshape(1, 128),
            slab[S_ALOG, :N_HEADS].reshape(1, N_HEADS), slab[S_DTB, :N_HEADS].reshape(1, N_HEADS),
            _conv_shard_of(slab[S_CONV:], shard))


def _windows_to_w_r(win):
    b = 128
    s0, s1, s2, s3 = win[0], win[1], win[2], win[3]
    e1, e2, e3 = WIN_START[1] * b, WIN_START[2] * b, WIN_START[3] * b
    n1, n2 = e2 - e1, e3 - e2
    return jnp.concatenate([
        s0[:, :e1], s0[:, e1:e1 + b] + s1[:, :b],
        s1[:, b:n1], s1[:, n1:n1 + b] + s2[:, :b],
        s2[:, b:n2], s2[:, n2:n2 + b] + s3[:, :b],
        s3[:, b:], s1[:, _S1_BA_POS:], jnp.zeros((win.shape[1], W_R - C_BA - b), win.dtype)], axis=1)


def _dproj_windows(dproj_r):
    b = 128
    pieces = []
    for s in range(N_SHARD):
        lo = WIN_START[s] * b
        if s == 1:
            pieces += [dproj_r[:, lo:lo + _S1_BA_POS], dproj_r[:, C_BA:C_BA + b]]
        else:
            pieces.append(dproj_r[:, lo:lo + SHARD_PAD])
    return jnp.concatenate(pieces, axis=1)


def _local_step(x, mem, tgt, norm_g, mem_norm_g, w_r, w_sh, conv_w, a_log, dt_bias, dn_norm_g, proj_weights, final_g,
                on_early=None, after_gather=None, h=None):
    t = x.shape[0]
    final_row = final_g.reshape(1, D_MODEL)
    lanes_8_16 = ((0, 0), (N_HEADS, 128 - 2 * N_HEADS))
    alog_row = jnp.pad(a_log.reshape(1, N_HEADS), lanes_8_16)
    dtb_row = jnp.pad(dt_bias.reshape(1, N_HEADS), lanes_8_16)

    if h is None:
        h = _rmsnorm_fwd(x, norm_g, "norm_fwd")
    proj = _mm(h, w_r, "nn", "in_proj", after=after_gather, tm_max=2048)
    qkv = _dn_prep_fwd(proj, conv_w)
    beta_t, g_t = _dn_gate_fwd(proj, alog_row, dtb_row)
    dn_u, dn_w, dn_qd, dn_kd, dn_a, tinv_all, dn_el = _dn_intra_fwd(qkv, beta_t, g_t)
    o_dn, dn_vn, s_all = _dn_scan_fwd(dn_u, dn_w, dn_qd, dn_kd, dn_a, dn_el)
    o_dn_g = _dn_post_fwd(o_dn, proj, dn_norm_g)
    o_sb, o_sb_g, sb_l = _sb_fwd(proj)
    w_mem_kv, w_br_dn, w_br_sb, w_br_mem, w_out = proj_weights(o_sb_g)
    mem_n = _rmsnorm_fwd(mem, mem_norm_g, "mem_norm_fwd")
    mkv = _mm(mem_n, w_mem_kv, "nn", "mem_kv")
    o_m, o_m_g = _mem_fwd(proj, mkv)
    y_dn = _mm(o_dn_g, w_br_dn, "nn", "br_dn", out_dtype=BF16)
    y_sb = _mm(o_sb_g, w_br_sb, "nn", "br_sb", out_dtype=BF16)
    y_m = _mm(o_m_g, w_br_mem, "nn", "br_mem", out_dtype=BF16)
    merged = _merge_fwd(proj, y_dn, y_sb, y_m)
    mo = _mm(merged, w_out, "nn", "out_proj")
    d_out, d_out_b, loss_row, g_final = _final_loss(x, mo, final_row, tgt)

    g_w_out = _mm(merged, d_out_b, "tn", "g_w_out", out_dtype=BF16)
    d_merged = _mm(d_out_b, w_out, "nt", "d_merged")
    dy_dn, dy_sb, dy_m, dg1, dg2, dg3 = _merge_bwd(proj, y_dn, y_sb, y_m, d_merged)
    g_w_br_dn = _mm(o_dn_g, dy_dn, "tn", "g_w_br_dn", out_dtype=BF16)
    g_w_br_sb = _mm(o_sb_g, dy_sb, "tn", "g_w_br_sb", out_dtype=BF16)
    g_w_br_mem = _mm(o_m_g, dy_m, "tn", "g_w_br_mem", out_dtype=BF16)
    d_o_dn_g = _mm(dy_dn, w_br_dn, "nt", "d_o_dn")
    d_o_sb_g = _mm(dy_sb, w_br_sb, "nt", "d_o_sb")
    d_o_m_g = _mm(dy_m, w_br_mem, "nt", "d_o_mem")

    d_mq, d_mz, d_mkv = _mem_bwd(proj, mkv, o_m, d_o_m_g)
    d_mkv_b = _cast_bf16(d_mkv, "cast_dmkv")
    g_w_mem_kv = _mm(mem_n, d_mkv_b, "tn", "g_w_mem_kv", out_dtype=BF16)
    d_mem_n = _mm(d_mkv_b, w_mem_kv, "nt", "d_mem_n")
    _, g_mem_norm = _rmsnorm_bwd(mem, mem_norm_g, d_mem_n, jnp.zeros_like(mem), "mem_norm_bwd")

    early = dict(w_mem_kv=g_w_mem_kv, w_br_dn=g_w_br_dn, w_br_sb=g_w_br_sb, w_br_mem=g_w_br_mem, w_out=g_w_out)
    after_early = on_early(early) if on_early is not None else None

    d_sq, d_sk, d_sv, d_sz = _sb_bwd(proj, o_sb, sb_l, d_o_sb_g, after=after_early)

    d_o_dn, d_dnz, g_dn_norm = _dn_post_bwd(o_dn, proj, dn_norm_g, d_o_dn_g)
    d_vnew, d_kd, d_qd, d_w, d_el = _dn_scan_bwd(dn_w, dn_qd, dn_kd, dn_a, dn_el, dn_vn, s_all, d_o_dn)
    d_qn, d_kn, d_vn, dbeta_t, dg_t = _dn_intra_bwd(qkv, beta_t, g_t, tinv_all, dn_vn, d_o_dn, d_vnew, d_kd, d_qd, d_w, d_el)
    d_conv_in, g_conv = _dn_prep_bwd(proj, conv_w, d_qn, d_kn, d_vn)
    d_ba, g_alog_row, g_dtb_row = _dn_gate_bwd(proj, alog_row, dtb_row, dbeta_t, dg_t)

    dproj_sh = _dproj_windows(
        jnp.concatenate([d_conv_in, d_dnz, d_sq, d_sk, d_sv, d_sz, d_mq, d_mz, dg1, dg2, dg3, d_ba], axis=1))
    g_w_sh = _mm(h, dproj_sh, "tn", "g_w_in", out_dtype=BF16, out_shards=N_SHARD, tn_max=1024)
    def input_grad(after=None):
        dh = _mm(dproj_sh, w_sh, "nt", "d_h", after=after, tm_max=2048, tn_max=1024)
        grad_x, g_norm = _rmsnorm_bwd(x, norm_g, dh, d_out, "norm_bwd")
        small = dict(norm_g=g_norm, mem_norm_g=g_mem_norm, final_g=g_final, dn_norm_g=g_dn_norm,
                     a_log=g_alog_row[:, N_HEADS:2 * N_HEADS], dt_bias=g_dtb_row[:, N_HEADS:2 * N_HEADS],
                     conv_w=g_conv)
        return grad_x, small

    return loss_row[0, 0], early, g_w_sh, input_grad


def _reduce_scatter_start(grads, tag):
    c = lax.axis_index("c")
    core = jnp.reshape(c, (1,)).astype(jnp.int32)
    recv = _pair_reduce_send(grads, tag)
    parts = [_pair_add(g, r, core, "pair_add_" + tag) for g, r in zip(grads, recv)]
    return _chip_exchange_start(parts, tag)


def _reduce_scatter_finish(handle, after, tag):
    send_sems, recv_sems, parts, lands, _ = handle
    x, y, c = lax.axis_index("x"), lax.axis_index("y"), lax.axis_index("c")
    place = jnp.stack([2 * x + y, c]).astype(jnp.int32)
    parts, by_chip = _chip_exchange_wait(send_sems, recv_sems, parts, lands, after, tag)
    fulls = [_chip_sum(p, b, place, "chip_sum_" + tag) for p, b in zip(parts, by_chip)]
    return _pair_allgather(fulls, tag)


def kernel(x, mem, norm_g, mem_norm_g, w_in, conv_w, a_log, dt_bias, dn_norm_g, w_mem_kv, w_br_dn, w_br_sb, w_br_mem, w_out, final_g, loss_target, m_norm_g, m_mem_norm_g, m_w_in, m_conv_w, m_a_log, m_dt_bias, m_dn_norm_g, m_w_mem_kv, m_w_br_dn, m_w_br_sb, m_w_br_mem, m_w_out, m_final_g, v_norm_g, v_mem_norm_g, v_w_in, v_conv_w, v_a_log, v_dt_bias, v_dn_norm_g, v_w_mem_kv, v_w_br_dn, v_w_br_sb, v_w_br_mem, v_w_out, v_final_g):
    w_a = w_in[0]
    w_b = _pack_b(w_mem_kv[0], w_br_dn[0], w_br_sb[0], w_br_mem[0], w_out[0])
    m_b = _pack_b(m_w_mem_kv[0], m_w_br_dn[0], m_w_br_sb[0], m_w_br_mem[0], m_w_out[0])
    v_b = _pack_b(v_w_mem_kv[0], v_w_br_dn[0], v_w_br_sb[0], v_w_br_mem[0], v_w_out[0])

    shard_idx = 2 * lax.axis_index("x") + lax.axis_index("y")
    shard = jnp.reshape(shard_idx, (1,)).astype(jnp.int32)
    own = [_cast_to_window(w_a, shard, "cast_w_in"), _cast_bf16(_conv_slab(conv_w[0]), "cast_conv")]
    send1, recv1, own, lands, token = _halves_gather_start(own)
    h = _rmsnorm_fwd(x[0], norm_g, "norm_fwd", after=token)
    w_b_bf = _cast_bf16(w_b, "cast_w_b")
    send2, recv2, lands = _halves_gather_forward(send1, recv1, own, lands,
                                                 after=[h, w_b_bf, m_w_in[0], v_w_in[0], m_b, v_b])
    lands = _halves_gather_wait(send2, recv2, lands)
    ga, g_conv = [lax.dynamic_update_slice(land, o[None], (shard_idx, 0, 0)) for land, o in zip(lands, own)]
    w_r = _windows_to_w_r(ga)
    f_conv = g_conv[:, :3].reshape(N_SHARD, 4, 768).transpose(1, 0, 2).reshape(4, 3 * D_MODEL).astype(F32)
    b_flight = _shard_gather_start(w_b_bf, after=ga)

    def proj_weights(after):
        own, land = _shard_gather_wait(b_flight[0], b_flight[1], b_flight[2], b_flight[3], after)
        gb = lax.dynamic_update_slice(land, own[None], (shard_idx, 0, 0))
        return (gb[:, B_MEMKV:B_BRDN].reshape(N_SHARD * 256, 512),
                gb[:, B_BRDN:B_BRSB].reshape(N_SHARD * 256, D_MODEL),
                gb[:, B_BRSB:B_BRMEM].reshape(N_SHARD * 256, D_MODEL),
                gb[:, B_BRMEM:B_OUT].reshape(N_SHARD, 256, 256).transpose(1, 0, 2).reshape(256, D_MODEL),
                gb[:, B_OUT:B_CONV].reshape(N_SHARD * 256, D_MODEL))

    flights = {}

    def on_early(grads):
        g_b = jnp.concatenate([
            grads["w_mem_kv"].reshape(N_SHARD, 128, D_MODEL), grads["w_br_dn"].reshape(N_SHARD, 256, D_MODEL),
            grads["w_br_sb"].reshape(N_SHARD, 256, D_MODEL),
            grads["w_br_mem"].reshape(256, N_SHARD, 256).transpose(1, 0, 2).reshape(N_SHARD, 64, D_MODEL),
            grads["w_out"].reshape(N_SHARD, 256, D_MODEL)], axis=1).astype(BF16)
        flights["b"] = _reduce_scatter_start([g_b], "b")
        return flights["b"][4]

    loss, _, g_w_sh, input_grad = _local_step(
        x[0], mem[0], loss_target[0], norm_g, mem_norm_g, w_r, ga, f_conv, a_log, dt_bias, dn_norm_g,
        proj_weights, final_g, on_early=on_early, after_gather=b_flight[4], h=h)
    flights["a"] = _reduce_scatter_start([g_w_sh], "a")
    grad_x, small = input_grad(after=flights["a"][4])

    part = _pack_small(small["norm_g"], small["mem_norm_g"], small["final_g"], small["dn_norm_g"],
                       small["a_log"], small["dt_bias"], _conv_rows(small["conv_w"]), loss)
    w_s = _pack_small(norm_g, mem_norm_g, final_g, dn_norm_g, a_log, dt_bias, _conv_shard_rows(conv_w[0], shard_idx))
    m_s = _pack_small(m_norm_g, m_mem_norm_g, m_final_g, m_dn_norm_g, m_a_log, m_dt_bias,
                      _conv_shard_rows(m_conv_w[0], shard_idx))
    v_s = _pack_small(v_norm_g, v_mem_norm_g, v_final_g, v_dn_norm_g, v_a_log, v_dt_bias,
                      _conv_shard_rows(v_conv_w[0], shard_idx))
    (gs_b,) = _reduce_scatter_finish(flights["b"], after=grad_x, tag="b")
    gr_b, d_b, nm_b, nv_b = _adamw(w_b, gs_b, m_b, v_b, "adamw_b")
    g_s, d_s, nm_s, nv_s = _small_update(_allgather_small(part, after=d_b), w_s, m_s, v_s)

    (gs_in,) = _reduce_scatter_finish(flights["a"], after=g_s, tag="a")
    gr_in, d_in, nm_in, nv_in = _adamw_window(w_a, gs_in, m_w_in[0], v_w_in[0], shard, "adamw_w_in")

    def assemble(slab_small, a_in, slab_b):
        s_norm, s_memnorm, s_final, s_dnnorm, s_alog, s_dtb, b_conv = _unpack_small(slab_small, shard_idx)
        b_memkv, b_brdn, b_brsb, b_brmem, b_out = _unpack_b(slab_b)
        return [s_norm, s_memnorm, a_in.reshape(1, D_MODEL, IN_WIDTH // N_SHARD), b_conv, s_alog, s_dtb, s_dnnorm,
                b_memkv, b_brdn, b_brsb, b_brmem, b_out, s_final]

    outs = [g_s[S_LOSS, 0], grad_x.reshape(1, -1, D_MODEL)]
    outs += assemble(g_s, gr_in, gr_b)
    outs += assemble(d_s, d_in, d_b)
    outs += assemble(nm_s, nm_in, nm_b)
    outs += assemble(nv_s, nv_in, nv_b)
    return tuple(outs)
```
